```python
import math
import jax, jax.numpy as jnp
from jax import lax
import numpy as np

D_MODEL = 1024
BATCH = 8
SEQ = 4096
DEPTH = 1

SSM_GROUPS = 32
SSM_GROUP_CH = 16
SSM_WIDTH = SSM_GROUPS * SSM_GROUP_CH
SSM_STATE = 64
DT_MIN = 1e-3
DT_MAX = 1e-1
N_HEADS = 8
QK_NOPE = 128
QK_ROPE = 64
QK_HEAD = QK_NOPE + QK_ROPE
V_HEAD = 128
Q_LORA = 384
KV_LORA = 256
ROPE_THETA = 10000.0
Q_BLOCK = 128
MAX_POS_OFFSET = 1024
D_FF = 4 * D_MODEL
EPS = 1e-6
IN_SIZES = (SSM_WIDTH, Q_LORA, KV_LORA + QK_ROPE, D_MODEL, D_MODEL)
IN_OFFSETS = tuple(int(v) for v in np.cumsum(IN_SIZES)[:-1])
D_IN = sum(IN_SIZES)

kernel_name = "hybrid_s5_mla_gated_block"


def rms_norm(x, gain):
    xf = x.astype(jnp.float32)
    inv = lax.rsqrt(jnp.mean(xf * xf, axis=-1, keepdims=True) + EPS)
    return (xf * inv * gain.astype(jnp.float32)).astype(x.dtype)


def rope_tables(positions):
    half = QK_ROPE // 2
    inv_freq = ROPE_THETA ** (-jnp.arange(half, dtype=jnp.float32) / half)
    ang = positions.astype(jnp.float32)[..., None] * inv_freq
    return jnp.cos(ang)[:, :, None, :], jnp.sin(ang)[:, :, None, :]


def apply_rope(x, cos, sin):
    xf = x.astype(jnp.float32)
    x1, x2 = jnp.split(xf, 2, axis=-1)
    out = jnp.concatenate([x1 * cos - x2 * sin, x2 * cos + x1 * sin], axis=-1)
    return out.astype(x.dtype)


def causal_block_attention(q, k, v):
    b, l, h, dq = q.shape
    nblk = l // Q_BLOCK
    qb = q.reshape(b, nblk, Q_BLOCK, h, dq).transpose(1, 0, 2, 3, 4)
    key_idx = jnp.arange(l)
    scale = QK_HEAD ** -0.5

    def one_block(args):
        q_blk, blk = args
        s = jnp.einsum('bqhd,bkhd->bhqk', q_blk, k, preferred_element_type=jnp.float32) * scale
        q_idx = blk * Q_BLOCK + jnp.arange(Q_BLOCK)
        s = jnp.where(key_idx[None, :] <= q_idx[:, None], s, -jnp.inf)
        p = jax.nn.softmax(s, axis=-1).astype(v.dtype)
        return jnp.einsum('bhqk,bkhd->bqhd', p, v)

    out = lax.map(one_block, (qb, jnp.arange(nblk)))
    return out.transpose(1, 0, 2, 3, 4).reshape(b, l, h, -1)


def s5_ssm(u, a_re, a_im, log_dt, b_re, b_im, c_re, c_im, d_skip):
    f32 = jnp.float32
    dt = jnp.exp(log_dt.astype(f32))[:, None]
    lam = lax.complex(a_re.astype(f32), a_im.astype(f32))
    lam_bar = jnp.exp(lam * dt)
    b_mat = lax.complex(b_re.astype(f32), b_im.astype(f32))
    b_bar = ((lam_bar - 1.0) / lam)[..., None] * b_mat
    bu = jnp.einsum('gpc,blgc->blgp', b_bar, u.astype(f32).astype(jnp.complex64))
    a = jnp.broadcast_to(lam_bar, (1, u.shape[1]) + lam_bar.shape)

    def combine(left, right):
        a_l, b_l = left
        a_r, b_r = right
        return a_r * a_l, a_r * b_l + b_r

    _, states = lax.associative_scan(combine, (a, bu), axis=1)
    c_mat = lax.complex(c_re.astype(f32), c_im.astype(f32))
    y = jnp.real(jnp.einsum('gcp,blgp->blgc', c_mat, states)) + d_skip.astype(f32) * u.astype(f32)
    return y.astype(u.dtype)


def hybrid_layer(x, positions, norm_mix, w_in, q_a_norm, kv_a_norm, w_q_b, w_kv_b, q_norm, k_norm, w_o_mla,
                 ssm_a_re, ssm_a_im, ssm_log_dt, ssm_b_re, ssm_b_im, ssm_c_re, ssm_c_im, ssm_d,
                 w_glu, b_glu, w_o_ssm, w_out, norm_mlp, w_up, w_down):
    b, l, _ = x.shape
    xn = rms_norm(x, norm_mix)
    proj = xn @ w_in
    u, q_lat, kv_lat, gate_ssm, gate_mla = jnp.split(proj, IN_OFFSETS, axis=-1)

    y = s5_ssm(u.reshape(b, l, SSM_GROUPS, SSM_GROUP_CH), ssm_a_re, ssm_a_im, ssm_log_dt,
               ssm_b_re, ssm_b_im, ssm_c_re, ssm_c_im, ssm_d).reshape(b, l, SSM_WIDTH)
    z = jax.nn.gelu(y)
    z = z * jax.nn.sigmoid(z @ w_glu + b_glu)
    y_ssm = z @ w_o_ssm

    q = (rms_norm(q_lat, q_a_norm) @ w_q_b).reshape(b, l, N_HEADS, QK_HEAD)
    c_kv, k_pe = kv_lat[..., :KV_LORA], kv_lat[..., KV_LORA:]
    kv = (rms_norm(c_kv, kv_a_norm) @ w_kv_b).reshape(b, l, N_HEADS, QK_NOPE + V_HEAD)
    k_nope, v = kv[..., :QK_NOPE], kv[..., QK_NOPE:]
    k = jnp.concatenate([k_nope, jnp.broadcast_to(k_pe[:, :, None, :], (b, l, N_HEADS, QK_ROPE))], axis=-1)
    q = rms_norm(q, q_norm)
    k = rms_norm(k, k_norm)
    cos, sin = rope_tables(positions)
    q = jnp.concatenate([q[..., :QK_NOPE], apply_rope(q[..., QK_NOPE:], cos, sin)], axis=-1)
    k = jnp.concatenate([k[..., :QK_NOPE], apply_rope(k[..., QK_NOPE:], cos, sin)], axis=-1)
    attn = causal_block_attention(q, k, v).reshape(b, l, N_HEADS * V_HEAD)
    y_mla = attn @ w_o_mla

    mixed = jax.nn.sigmoid(gate_ssm) * y_ssm + jax.nn.sigmoid(gate_mla) * y_mla
    h = x + mixed @ w_out

    hidden = jnp.square(jax.nn.relu(rms_norm(h, norm_mlp) @ w_up))
    return h + hidden @ w_down


def _fwd_setup_inputs(seed: int = 0) -> dict:
    key = jax.random.key(seed)
    ks = jax.random.split(key, 32)
    f32 = jnp.float32

    def dense(k, fan_in, shape):
        return jax.random.normal(k, (DEPTH,) + shape, f32) * (fan_in ** -0.5)

    def gain(k, n):
        return 1.0 + 0.02 * jax.random.normal(k, (DEPTH, n), f32)

    x = jax.random.normal(ks[0], (BATCH, SEQ, D_MODEL), f32)
    offset = jax.random.randint(ks[1], (BATCH, 1), 0, MAX_POS_OFFSET, dtype=jnp.int32)
    positions = (offset + jnp.arange(SEQ, dtype=jnp.int32)[None, :]).astype(jnp.int32)
    n_idx = jnp.arange(SSM_STATE, dtype=f32)
    ssm_a_re = -0.5 + 0.01 * jax.random.normal(ks[2], (DEPTH, SSM_GROUPS, SSM_STATE), f32)
    ssm_a_im = math.pi * n_idx[None, None, :] + 0.01 * jax.random.normal(ks[3], (DEPTH, SSM_GROUPS, SSM_STATE), f32)
    ssm_log_dt = jax.random.uniform(ks[4], (DEPTH, SSM_GROUPS), f32, math.log(DT_MIN), math.log(DT_MAX))
    return {
        "x": x,
        "positions": positions,
        "norm_mix": gain(ks[5], D_MODEL),
        "w_in": dense(ks[6], D_MODEL, (D_MODEL, D_IN)),
        "q_a_norm": gain(ks[7], Q_LORA),
        "kv_a_norm": gain(ks[8], KV_LORA),
        "w_q_b": dense(ks[9], Q_LORA, (Q_LORA, N_HEADS * QK_HEAD)),
        "w_kv_b": dense(ks[10], KV_LORA, (KV_LORA, N_HEADS * (QK_NOPE + V_HEAD))),
        "q_norm": gain(ks[11], QK_HEAD),
        "k_norm": gain(ks[12], QK_HEAD),
        "w_o_mla": dense(ks[13], N_HEADS * V_HEAD, (N_HEADS * V_HEAD, D_MODEL)),
        "ssm_a_re": ssm_a_re,
        "ssm_a_im": ssm_a_im,
        "ssm_log_dt": ssm_log_dt,
        "ssm_b_re": dense(ks[14], 2 * SSM_GROUP_CH, (SSM_GROUPS, SSM_STATE, SSM_GROUP_CH)),
        "ssm_b_im": dense(ks[15], 2 * SSM_GROUP_CH, (SSM_GROUPS, SSM_STATE, SSM_GROUP_CH)),
        "ssm_c_re": dense(ks[16], 2 * SSM_STATE, (SSM_GROUPS, SSM_GROUP_CH, SSM_STATE)),
        "ssm_c_im": dense(ks[17], 2 * SSM_STATE, (SSM_GROUPS, SSM_GROUP_CH, SSM_STATE)),
        "ssm_d": jax.random.normal(ks[18], (DEPTH, SSM_GROUPS, SSM_GROUP_CH), f32),
        "w_glu": dense(ks[19], SSM_WIDTH, (SSM_WIDTH, SSM_WIDTH)),
        "b_glu": 0.01 * jax.random.normal(ks[20], (DEPTH, SSM_WIDTH), f32),
        "w_o_ssm": dense(ks[21], SSM_WIDTH, (SSM_WIDTH, D_MODEL)),
        "w_out": dense(ks[22], D_MODEL, (D_MODEL, D_MODEL)),
        "norm_mlp": gain(ks[23], D_MODEL),
        "w_up": dense(ks[24], D_MODEL, (D_MODEL, D_FF)),
        "w_down": dense(ks[25], D_FF, (D_FF, D_MODEL)),
    }


def _fwd_reference(x, positions, norm_mix, w_in, q_a_norm, kv_a_norm, w_q_b, w_kv_b, q_norm, k_norm, w_o_mla,
              ssm_a_re, ssm_a_im, ssm_log_dt, ssm_b_re, ssm_b_im, ssm_c_re, ssm_c_im, ssm_d,
              w_glu, b_glu, w_o_ssm, w_out, norm_mlp, w_up, w_down):
    h = x
    for layer in range(DEPTH):
        h = hybrid_layer(h, positions, norm_mix[layer], w_in[layer], q_a_norm[layer], kv_a_norm[layer],
                         w_q_b[layer], w_kv_b[layer], q_norm[layer], k_norm[layer], w_o_mla[layer],
                         ssm_a_re[layer], ssm_a_im[layer], ssm_log_dt[layer], ssm_b_re[layer], ssm_b_im[layer],
                         ssm_c_re[layer], ssm_c_im[layer], ssm_d[layer], w_glu[layer], b_glu[layer],
                         w_o_ssm[layer], w_out[layer], norm_mlp[layer], w_up[layer], w_down[layer])
    return h


import jax as _jax
import jax.numpy as _jnp

TWIN_FORMAT = 'train_step'
FWD_PARAMS = ['x', 'positions', 'norm_mix', 'w_in', 'q_a_norm', 'kv_a_norm', 'w_q_b', 'w_kv_b', 'q_norm', 'k_norm', 'w_o_mla', 'ssm_a_re', 'ssm_a_im', 'ssm_log_dt', 'ssm_b_re', 'ssm_b_im', 'ssm_c_re', 'ssm_c_im', 'ssm_d', 'w_glu', 'b_glu', 'w_o_ssm', 'w_out', 'norm_mlp', 'w_up', 'w_down']
TWIN_WEIGHTS = ['norm_mix', 'w_in', 'q_a_norm', 'kv_a_norm', 'w_q_b', 'w_kv_b', 'q_norm', 'k_norm', 'w_o_mla', 'ssm_a_re', 'ssm_a_im', 'ssm_log_dt', 'ssm_b_re', 'ssm_b_im', 'ssm_c_re', 'ssm_c_im', 'ssm_d', 'w_glu', 'b_glu', 'w_o_ssm', 'w_out', 'norm_mlp', 'w_up', 'w_down']
TWIN_DIFF_INPUT = 'x'
TWIN_INPUTS = ['x', 'positions', 'norm_mix', 'w_in', 'q_a_norm', 'kv_a_norm', 'w_q_b', 'w_kv_b', 'q_norm', 'k_norm', 'w_o_mla', 'ssm_a_re', 'ssm_a_im', 'ssm_log_dt', 'ssm_b_re', 'ssm_b_im', 'ssm_c_re', 'ssm_c_im', 'ssm_d', 'w_glu', 'b_glu', 'w_o_ssm', 'w_out', 'norm_mlp', 'w_up', 'w_down', 'loss_target', 'm_norm_mix', 'm_w_in', 'm_q_a_norm', 'm_kv_a_norm', 'm_w_q_b', 'm_w_kv_b', 'm_q_norm', 'm_k_norm', 'm_w_o_mla', 'm_ssm_a_re', 'm_ssm_a_im', 'm_ssm_log_dt', 'm_ssm_b_re', 'm_ssm_b_im', 'm_ssm_c_re', 'm_ssm_c_im', 'm_ssm_d', 'm_w_glu', 'm_b_glu', 'm_w_o_ssm', 'm_w_out', 'm_norm_mlp', 'm_w_up', 'm_w_down', 'v_norm_mix', 'v_w_in', 'v_q_a_norm', 'v_kv_a_norm', 'v_w_q_b', 'v_w_kv_b', 'v_q_norm', 'v_k_norm', 'v_w_o_mla', 'v_ssm_a_re', 'v_ssm_a_im', 'v_ssm_log_dt', 'v_ssm_b_re', 'v_ssm_b_im', 'v_ssm_c_re', 'v_ssm_c_im', 'v_ssm_d', 'v_w_glu', 'v_b_glu', 'v_w_o_ssm', 'v_w_out', 'v_norm_mlp', 'v_w_up', 'v_w_down']
TWIN_OUTPUTS = ['loss', 'grad_x', 'grad_norm_mix', 'grad_w_in', 'grad_q_a_norm', 'grad_kv_a_norm', 'grad_w_q_b', 'grad_w_kv_b', 'grad_q_norm', 'grad_k_norm', 'grad_w_o_mla', 'grad_ssm_a_re', 'grad_ssm_a_im', 'grad_ssm_log_dt', 'grad_ssm_b_re', 'grad_ssm_b_im', 'grad_ssm_c_re', 'grad_ssm_c_im', 'grad_ssm_d', 'grad_w_glu', 'grad_b_glu', 'grad_w_o_ssm', 'grad_w_out', 'grad_norm_mlp', 'grad_w_up', 'grad_w_down', 'delta_norm_mix', 'delta_w_in', 'delta_q_a_norm', 'delta_kv_a_norm', 'delta_w_q_b', 'delta_w_kv_b', 'delta_q_norm', 'delta_k_norm', 'delta_w_o_mla', 'delta_ssm_a_re', 'delta_ssm_a_im', 'delta_ssm_log_dt', 'delta_ssm_b_re', 'delta_ssm_b_im', 'delta_ssm_c_re', 'delta_ssm_c_im', 'delta_ssm_d', 'delta_w_glu', 'delta_b_glu', 'delta_w_o_ssm', 'delta_w_out', 'delta_norm_mlp', 'delta_w_up', 'delta_w_down', 'new_m_norm_mix', 'new_m_w_in', 'new_m_q_a_norm', 'new_m_kv_a_norm', 'new_m_w_q_b', 'new_m_w_kv_b', 'new_m_q_norm', 'new_m_k_norm', 'new_m_w_o_mla', 'new_m_ssm_a_re', 'new_m_ssm_a_im', 'new_m_ssm_log_dt', 'new_m_ssm_b_re', 'new_m_ssm_b_im', 'new_m_ssm_c_re', 'new_m_ssm_c_im', 'new_m_ssm_d', 'new_m_w_glu', 'new_m_b_glu', 'new_m_w_o_ssm', 'new_m_w_out', 'new_m_norm_mlp', 'new_m_w_up', 'new_m_w_down', 'new_v_norm_mix', 'new_v_w_in', 'new_v_q_a_norm', 'new_v_kv_a_norm', 'new_v_w_q_b', 'new_v_w_kv_b', 'new_v_q_norm', 'new_v_k_norm', 'new_v_w_o_mla', 'new_v_ssm_a_re', 'new_v_ssm_a_im', 'new_v_ssm_log_dt', 'new_v_ssm_b_re', 'new_v_ssm_b_im', 'new_v_ssm_c_re', 'new_v_ssm_c_im', 'new_v_ssm_d', 'new_v_w_glu', 'new_v_b_glu', 'new_v_w_o_ssm', 'new_v_w_out', 'new_v_norm_mlp', 'new_v_w_up', 'new_v_w_down']
TWIN_LEAF_KINDS = {'loss': 'loss', 'grad_x': 'grad_x', 'grad_norm_mix': 'grad_w', 'grad_w_in': 'grad_w', 'grad_q_a_norm': 'grad_w', 'grad_kv_a_norm': 'grad_w', 'grad_w_q_b': 'grad_w', 'grad_w_kv_b': 'grad_w', 'grad_q_norm': 'grad_w', 'grad_k_norm': 'grad_w', 'grad_w_o_mla': 'grad_w', 'grad_ssm_a_re': 'grad_w', 'grad_ssm_a_im': 'grad_w', 'grad_ssm_log_dt': 'grad_w', 'grad_ssm_b_re': 'grad_w', 'grad_ssm_b_im': 'grad_w', 'grad_ssm_c_re': 'grad_w', 'grad_ssm_c_im': 'grad_w', 'grad_ssm_d': 'grad_w', 'grad_w_glu': 'grad_w', 'grad_b_glu': 'grad_w', 'grad_w_o_ssm': 'grad_w', 'grad_w_out': 'grad_w', 'grad_norm_mlp': 'grad_w', 'grad_w_up': 'grad_w', 'grad_w_down': 'grad_w', 'delta_norm_mix': 'delta_w', 'delta_w_in': 'delta_w', 'delta_q_a_norm': 'delta_w', 'delta_kv_a_norm': 'delta_w', 'delta_w_q_b': 'delta_w', 'delta_w_kv_b': 'delta_w', 'delta_q_norm': 'delta_w', 'delta_k_norm': 'delta_w', 'delta_w_o_mla': 'delta_w', 'delta_ssm_a_re': 'delta_w', 'delta_ssm_a_im': 'delta_w', 'delta_ssm_log_dt': 'delta_w', 'delta_ssm_b_re': 'delta_w', 'delta_ssm_b_im': 'delta_w', 'delta_ssm_c_re': 'delta_w', 'delta_ssm_c_im': 'delta_w', 'delta_ssm_d': 'delta_w', 'delta_w_glu': 'delta_w', 'delta_b_glu': 'delta_w', 'delta_w_o_ssm': 'delta_w', 'delta_w_out': 'delta_w', 'delta_norm_mlp': 'delta_w', 'delta_w_up': 'delta_w', 'delta_w_down': 'delta_w', 'new_m_norm_mix': 'new_m', 'new_m_w_in': 'new_m', 'new_m_q_a_norm': 'new_m', 'new_m_kv_a_norm': 'new_m', 'new_m_w_q_b': 'new_m', 'new_m_w_kv_b': 'new_m', 'new_m_q_norm': 'new_m', 'new_m_k_norm': 'new_m', 'new_m_w_o_mla': 'new_m', 'new_m_ssm_a_re': 'new_m', 'new_m_ssm_a_im': 'new_m', 'new_m_ssm_log_dt': 'new_m', 'new_m_ssm_b_re': 'new_m', 'new_m_ssm_b_im': 'new_m', 'new_m_ssm_c_re': 'new_m', 'new_m_ssm_c_im': 'new_m', 'new_m_ssm_d': 'new_m', 'new_m_w_glu': 'new_m', 'new_m_b_glu': 'new_m', 'new_m_w_o_ssm': 'new_m', 'new_m_w_out': 'new_m', 'new_m_norm_mlp': 'new_m', 'new_m_w_up': 'new_m', 'new_m_w_down': 'new_m', 'new_v_norm_mix': 'new_v', 'new_v_w_in': 'new_v', 'new_v_q_a_norm': 'new_v', 'new_v_kv_a_norm': 'new_v', 'new_v_w_q_b': 'new_v', 'new_v_w_kv_b': 'new_v', 'new_v_q_norm': 'new_v', 'new_v_k_norm': 'new_v', 'new_v_w_o_mla': 'new_v', 'new_v_ssm_a_re': 'new_v', 'new_v_ssm_a_im': 'new_v', 'new_v_ssm_log_dt': 'new_v', 'new_v_ssm_b_re': 'new_v', 'new_v_ssm_b_im': 'new_v', 'new_v_ssm_c_re': 'new_v', 'new_v_ssm_c_im': 'new_v', 'new_v_ssm_d': 'new_v', 'new_v_w_glu': 'new_v', 'new_v_b_glu': 'new_v', 'new_v_w_o_ssm': 'new_v', 'new_v_w_out': 'new_v', 'new_v_norm_mlp': 'new_v', 'new_v_w_up': 'new_v', 'new_v_w_down': 'new_v'}


def _forward(args):
    return _fwd_reference(*[args[k] for k in FWD_PARAMS])


def _output_shape():
    out = _jax.eval_shape(lambda: _forward(_fwd_setup_inputs(0)))
    return out.shape, out.dtype

N_MICROBATCH = 1
ADAM_LR = 0.001
ADAM_B1 = 0.9
ADAM_B2 = 0.999
ADAM_EPS = 1e-08
ADAM_WD = 0.01
ADAM_STEP = 10
PER_EXAMPLE_BATCH_AXIS = {'x': 0, 'positions': 0, 'loss_target': 0}
SHARED_INPUTS = []
_WEIGHT_DTYPES = {'norm_mix': _jnp.float32, 'w_in': _jnp.float32, 'q_a_norm': _jnp.float32, 'kv_a_norm': _jnp.float32, 'w_q_b': _jnp.float32, 'w_kv_b': _jnp.float32, 'q_norm': _jnp.float32, 'k_norm': _jnp.float32, 'w_o_mla': _jnp.float32, 'ssm_a_re': _jnp.float32, 'ssm_a_im': _jnp.float32, 'ssm_log_dt': _jnp.float32, 'ssm_b_re': _jnp.float32, 'ssm_b_im': _jnp.float32, 'ssm_c_re': _jnp.float32, 'ssm_c_im': _jnp.float32, 'ssm_d': _jnp.float32, 'w_glu': _jnp.float32, 'b_glu': _jnp.float32, 'w_o_ssm': _jnp.float32, 'w_out': _jnp.float32, 'norm_mlp': _jnp.float32, 'w_up': _jnp.float32, 'w_down': _jnp.float32}
MOMENT_SCALE = {'norm_mix': 9.555336e-01, 'w_in': 1.450265e-01, 'q_a_norm': 1.297076e-01, 'kv_a_norm': 4.861782e-01, 'w_q_b': 6.159875e-02, 'w_kv_b': 1.270290e-01, 'q_norm': 3.057013e-01, 'k_norm': 3.051076e-01, 'w_o_mla': 1.650411e-01, 'ssm_a_re': 2.345834e-02, 'ssm_a_im': 1.496608e-02, 'ssm_log_dt': 1.152906e+01, 'ssm_b_re': 8.794716e-03, 'ssm_b_im': 8.824089e-03, 'ssm_c_re': 1.880479e-02, 'ssm_c_im': 2.039960e-02, 'ssm_d': 4.668950e+00, 'w_glu': 8.209927e-01, 'b_glu': 2.536240e+00, 'w_o_ssm': 2.595584e+00, 'w_out': 2.489505e+00, 'norm_mlp': 9.516213e+01, 'w_up': 1.102769e+00, 'w_down': 8.076429e+00}


def _to_microbatches(a, axis):
    t = _jnp.moveaxis(a, axis, 0)
    t = t.reshape((N_MICROBATCH, t.shape[0] // N_MICROBATCH) + t.shape[1:])
    return _jnp.moveaxis(t, 1, axis + 1)


def setup_inputs(seed: int = 0) -> dict:
    inp = _fwd_setup_inputs(seed)
    key = _jax.random.fold_in(_jax.random.key(seed), 7919)
    shape, _ = _output_shape()
    out = dict(inp)
    out["loss_target"] = _jax.random.normal(_jax.random.fold_in(key, 0), shape, _jnp.float32)
    for i, name in enumerate(TWIN_WEIGHTS):
        w = inp[name].astype(_jnp.float32)
        if MOMENT_SCALE is None:
            s = _jnp.sqrt(_jnp.mean(_jnp.square(w)) + 1e-30)
        else:
            s = MOMENT_SCALE[name]
        km, kv = _jax.random.split(_jax.random.fold_in(key, i + 1))
        out[name] = w
        out["m_" + name] = s * _jax.random.normal(km, w.shape, _jnp.float32)
        out["v_" + name] = (s * s) * _jax.random.uniform(kv, w.shape, _jnp.float32, 0.5, 1.5)
    if N_MICROBATCH > 1:
        for name, axis in PER_EXAMPLE_BATCH_AXIS.items():
            out[name] = _to_microbatches(out[name], axis)
    return {'x': out['x'], 'positions': out['positions'], 'norm_mix': out['norm_mix'], 'w_in': out['w_in'], 'q_a_norm': out['q_a_norm'], 'kv_a_norm': out['kv_a_norm'], 'w_q_b': out['w_q_b'], 'w_kv_b': out['w_kv_b'], 'q_norm': out['q_norm'], 'k_norm': out['k_norm'], 'w_o_mla': out['w_o_mla'], 'ssm_a_re': out['ssm_a_re'], 'ssm_a_im': out['ssm_a_im'], 'ssm_log_dt': out['ssm_log_dt'], 'ssm_b_re': out['ssm_b_re'], 'ssm_b_im': out['ssm_b_im'], 'ssm_c_re': out['ssm_c_re'], 'ssm_c_im': out['ssm_c_im'], 'ssm_d': out['ssm_d'], 'w_glu': out['w_glu'], 'b_glu': out['b_glu'], 'w_o_ssm': out['w_o_ssm'], 'w_out': out['w_out'], 'norm_mlp': out['norm_mlp'], 'w_up': out['w_up'], 'w_down': out['w_down'], 'loss_target': out['loss_target'], 'm_norm_mix': out['m_norm_mix'], 'm_w_in': out['m_w_in'], 'm_q_a_norm': out['m_q_a_norm'], 'm_kv_a_norm': out['m_kv_a_norm'], 'm_w_q_b': out['m_w_q_b'], 'm_w_kv_b': out['m_w_kv_b'], 'm_q_norm': out['m_q_norm'], 'm_k_norm': out['m_k_norm'], 'm_w_o_mla': out['m_w_o_mla'], 'm_ssm_a_re': out['m_ssm_a_re'], 'm_ssm_a_im': out['m_ssm_a_im'], 'm_ssm_log_dt': out['m_ssm_log_dt'], 'm_ssm_b_re': out['m_ssm_b_re'], 'm_ssm_b_im': out['m_ssm_b_im'], 'm_ssm_c_re': out['m_ssm_c_re'], 'm_ssm_c_im': out['m_ssm_c_im'], 'm_ssm_d': out['m_ssm_d'], 'm_w_glu': out['m_w_glu'], 'm_b_glu': out['m_b_glu'], 'm_w_o_ssm': out['m_w_o_ssm'], 'm_w_out': out['m_w_out'], 'm_norm_mlp': out['m_norm_mlp'], 'm_w_up': out['m_w_up'], 'm_w_down': out['m_w_down'], 'v_norm_mix': out['v_norm_mix'], 'v_w_in': out['v_w_in'], 'v_q_a_norm': out['v_q_a_norm'], 'v_kv_a_norm': out['v_kv_a_norm'], 'v_w_q_b': out['v_w_q_b'], 'v_w_kv_b': out['v_w_kv_b'], 'v_q_norm': out['v_q_norm'], 'v_k_norm': out['v_k_norm'], 'v_w_o_mla': out['v_w_o_mla'], 'v_ssm_a_re': out['v_ssm_a_re'], 'v_ssm_a_im': out['v_ssm_a_im'], 'v_ssm_log_dt': out['v_ssm_log_dt'], 'v_ssm_b_re': out['v_ssm_b_re'], 'v_ssm_b_im': out['v_ssm_b_im'], 'v_ssm_c_re': out['v_ssm_c_re'], 'v_ssm_c_im': out['v_ssm_c_im'], 'v_ssm_d': out['v_ssm_d'], 'v_w_glu': out['v_w_glu'], 'v_b_glu': out['v_b_glu'], 'v_w_o_ssm': out['v_w_o_ssm'], 'v_w_out': out['v_w_out'], 'v_norm_mlp': out['v_norm_mlp'], 'v_w_up': out['v_w_up'], 'v_w_down': out['v_w_down']}


def _loss(weights, diff, rest, loss_target):
    with _jax.named_scope("forward"):
        args = {**rest, TWIN_DIFF_INPUT: diff, **{k: w.astype(_WEIGHT_DTYPES[k]) for k, w in weights.items()}}
        y = _forward(args)
    with _jax.named_scope("loss_head"):
        err = _jnp.square(y.astype(_jnp.float32) - loss_target)
        return 0.5 * _jnp.sum(_jnp.mean(err, axis=-1)) if err.ndim else 0.5 * err


def _adamw(w, g, m, v):
    m = ADAM_B1 * m + (1.0 - ADAM_B1) * g
    v = ADAM_B2 * v + (1.0 - ADAM_B2) * _jnp.square(g)
    m_hat = m / (1.0 - ADAM_B1 ** ADAM_STEP)
    v_hat = v / (1.0 - ADAM_B2 ** ADAM_STEP)
    delta = -ADAM_LR * (m_hat / (_jnp.sqrt(v_hat) + ADAM_EPS) + ADAM_WD * w)
    return delta, m, v


def reference(x, positions, norm_mix, w_in, q_a_norm, kv_a_norm, w_q_b, w_kv_b, q_norm, k_norm, w_o_mla, ssm_a_re, ssm_a_im, ssm_log_dt, ssm_b_re, ssm_b_im, ssm_c_re, ssm_c_im, ssm_d, w_glu, b_glu, w_o_ssm, w_out, norm_mlp, w_up, w_down, loss_target, m_norm_mix, m_w_in, m_q_a_norm, m_kv_a_norm, m_w_q_b, m_w_kv_b, m_q_norm, m_k_norm, m_w_o_mla, m_ssm_a_re, m_ssm_a_im, m_ssm_log_dt, m_ssm_b_re, m_ssm_b_im, m_ssm_c_re, m_ssm_c_im, m_ssm_d, m_w_glu, m_b_glu, m_w_o_ssm, m_w_out, m_norm_mlp, m_w_up, m_w_down, v_norm_mix, v_w_in, v_q_a_norm, v_kv_a_norm, v_w_q_b, v_w_kv_b, v_q_norm, v_k_norm, v_w_o_mla, v_ssm_a_re, v_ssm_a_im, v_ssm_log_dt, v_ssm_b_re, v_ssm_b_im, v_ssm_c_re, v_ssm_c_im, v_ssm_d, v_w_glu, v_b_glu, v_w_o_ssm, v_w_out, v_norm_mlp, v_w_up, v_w_down):
    given = dict(x=x, positions=positions, norm_mix=norm_mix, w_in=w_in, q_a_norm=q_a_norm, kv_a_norm=kv_a_norm, w_q_b=w_q_b, w_kv_b=w_kv_b, q_norm=q_norm, k_norm=k_norm, w_o_mla=w_o_mla, ssm_a_re=ssm_a_re, ssm_a_im=ssm_a_im, ssm_log_dt=ssm_log_dt, ssm_b_re=ssm_b_re, ssm_b_im=ssm_b_im, ssm_c_re=ssm_c_re, ssm_c_im=ssm_c_im, ssm_d=ssm_d, w_glu=w_glu, b_glu=b_glu, w_o_ssm=w_o_ssm, w_out=w_out, norm_mlp=norm_mlp, w_up=w_up, w_down=w_down, loss_target=loss_target, m_norm_mix=m_norm_mix, m_w_in=m_w_in, m_q_a_norm=m_q_a_norm, m_kv_a_norm=m_kv_a_norm, m_w_q_b=m_w_q_b, m_w_kv_b=m_w_kv_b, m_q_norm=m_q_norm, m_k_norm=m_k_norm, m_w_o_mla=m_w_o_mla, m_ssm_a_re=m_ssm_a_re, m_ssm_a_im=m_ssm_a_im, m_ssm_log_dt=m_ssm_log_dt, m_ssm_b_re=m_ssm_b_re, m_ssm_b_im=m_ssm_b_im, m_ssm_c_re=m_ssm_c_re, m_ssm_c_im=m_ssm_c_im, m_ssm_d=m_ssm_d, m_w_glu=m_w_glu, m_b_glu=m_b_glu, m_w_o_ssm=m_w_o_ssm, m_w_out=m_w_out, m_norm_mlp=m_norm_mlp, m_w_up=m_w_up, m_w_down=m_w_down, v_norm_mix=v_norm_mix, v_w_in=v_w_in, v_q_a_norm=v_q_a_norm, v_kv_a_norm=v_kv_a_norm, v_w_q_b=v_w_q_b, v_w_kv_b=v_w_kv_b, v_q_norm=v_q_norm, v_k_norm=v_k_norm, v_w_o_mla=v_w_o_mla, v_ssm_a_re=v_ssm_a_re, v_ssm_a_im=v_ssm_a_im, v_ssm_log_dt=v_ssm_log_dt, v_ssm_b_re=v_ssm_b_re, v_ssm_b_im=v_ssm_b_im, v_ssm_c_re=v_ssm_c_re, v_ssm_c_im=v_ssm_c_im, v_ssm_d=v_ssm_d, v_w_glu=v_w_glu, v_b_glu=v_b_glu, v_w_o_ssm=v_w_o_ssm, v_w_out=v_w_out, v_norm_mlp=v_norm_mlp, v_w_up=v_w_up, v_w_down=v_w_down)
    weights = {n: given[n] for n in TWIN_WEIGHTS}
    shared = {n: given[n] for n in SHARED_INPUTS}
    per_example = {n: given[n] for n in ['x', 'positions']}
    grad_fn = _jax.value_and_grad(_loss, argnums=(0, 1))

    def one_microbatch(ex, loss_target):
        ex = dict(ex)
        diff = ex.pop(TWIN_DIFF_INPUT)
        return grad_fn(weights, diff, {**shared, **ex}, loss_target)

    if N_MICROBATCH == 1:
        loss, (grad_w, grad_x) = one_microbatch(per_example, given["loss_target"])
    else:
        def body(carry, xs):
            loss_sum, grad_sum = carry
            l_k, (gw_k, gx_k) = one_microbatch(xs[0], xs[1])
            with _jax.named_scope("update"):
                return (loss_sum + l_k, _jax.tree.map(_jnp.add, grad_sum, gw_k)), gx_k

        init = (_jnp.zeros((), _jnp.float32), _jax.tree.map(_jnp.zeros_like, weights))
        (loss, grad_w), grad_x = _jax.lax.scan(body, init, (per_example, given["loss_target"]))
    with _jax.named_scope("update"):
        delta_w, new_m, new_v = {}, {}, {}
        for n in TWIN_WEIGHTS:
            delta_w[n], new_m[n], new_v[n] = _adamw(weights[n], grad_w[n], given["m_" + n], given["v_" + n])
    return (loss, grad_x, *[grad_w[n] for n in TWIN_WEIGHTS], *[delta_w[n] for n in TWIN_WEIGHTS],
            *[new_m[n] for n in TWIN_WEIGHTS], *[new_v[n] for n in TWIN_WEIGHTS])
```

```python
import functools
import math

import numpy as np
import jax
import jax.numpy as jnp
from jax import lax
from jax.experimental import pallas as pl
from jax.experimental.pallas import tpu as pltpu

F32 = jnp.float32
BF16 = jnp.bfloat16

D_MODEL = 1024
SSM_GROUPS = 32
SSM_GROUP_CH = 16
SSM_WIDTH = 512
SSM_STATE = 64
N_STATE = SSM_GROUPS * SSM_STATE
N_HEADS = 8
QK_NOPE = 128
QK_ROPE = 64
QK_HEAD = 192
QK_PAD = 256
V_HEAD = 128
Q_LORA = 384
KV_LORA = 256
KV_LAT_PAD = 384
ROPE_THETA = 10000.0
D_FF = 4096
EPS = 1e-6
ATT_SCALE = QK_HEAD ** -0.5
N_DEV = 8

IN_SEGS = ((0, 512), (512, 896), (896, 1280), (1280, 2304), (2304, 3328))
D_IN = 3264
D_IN_PAD = 3328
KV_END = 1216

ADAM_LR = 0.001
ADAM_B1 = 0.9
ADAM_B2 = 0.999
ADAM_EPS = 1e-08
ADAM_WD = 0.01
ADAM_STEP = 10

VMEM_LIMIT = 56 * 1024 * 1024
MESH = pl.DeviceIdType.MESH

SCAN_T = 256
SCAN_CB = 256
ATT_T = 512
ROW_T = 256


def _params(sem=None):
    return pltpu.CompilerParams(dimension_semantics=sem, vmem_limit_bytes=VMEM_LIMIT)


def _rows(arr, tm):
    if arr.ndim == 2:
        return pl.BlockSpec((tm, arr.shape[1]), lambda i: (i, 0))
    return pl.BlockSpec((arr.shape[0], tm, arr.shape[2]), lambda i: (0, i, 0))


def _const(arr):
    nd = arr.ndim
    return pl.BlockSpec(arr.shape, lambda i: (0,) * nd)


def _sds(shape, dtype):
    return jax.ShapeDtypeStruct(shape, dtype)


def _row_call(body, name, n_rows, tm, row_ins, const_ins, row_outs, acc_outs=()):
    outs = [_sds(s, d) for s, d in row_outs] + [_sds(s, d) for s, d in acc_outs]
    out_specs = [_rows(o, tm) for o in outs[: len(row_outs)]] + [_const(o) for o in outs[len(row_outs):]]
    return pl.pallas_call(
        body,
        name=name,
        grid=(n_rows // tm,),
        in_specs=[_rows(a, tm) for a in row_ins] + [_const(a) for a in const_ins],
        out_specs=out_specs,
        out_shape=outs,
        compiler_params=_params(("arbitrary",)),
    )(*row_ins, *const_ins)


def _dot(a, b):
    return jnp.dot(a, b, preferred_element_type=F32)


def _dot_nt(a, b):
    return lax.dot_general(a, b, (((1,), (1,)), ((), ())), preferred_element_type=F32)


def _dot_tn(a, b):
    return lax.dot_general(a, b, (((0,), (0,)), ((), ())), preferred_element_type=F32)


def _rms(x, g, n):
    inv = lax.rsqrt(jnp.sum(x * x, -1, keepdims=True) * (1.0 / n) + EPS)
    return x * inv * g, inv


def _rms_bwd(dy, x, g, inv, n):
    xh = x * inv
    dxh = dy * g
    dx = inv * (dxh - xh * (jnp.sum(dxh * xh, -1, keepdims=True) * (1.0 / n)))
    return dx, dy * xh


def _sigmoid(x):
    return 1.0 / (1.0 + jnp.exp(-x))


_GELU_C = math.sqrt(2.0 / math.pi)


def _gelu(y):
    th = jnp.tanh(_GELU_C * (y + 0.044715 * (y * y * y)))
    return 0.5 * y * (1.0 + th), th


def _gelu_grad(y, th):
    return 0.5 * (1.0 + th) + 0.5 * y * (1.0 - th * th) * (_GELU_C * (1.0 + 3.0 * 0.044715 * (y * y)))


def _acc(ref, val):
    @pl.when(pl.program_id(0) == 0)
    def _():
        ref[...] = jnp.zeros_like(ref)

    ref[...] += val


def _tile(n, limit):
    if n <= limit:
        return n
    return max(t for t in range(128, limit + 1, 128) if n % t == 0)


def _matmul_tn(a, b, name, tm=512, tn=1024, tk=512):
    k_dim, m = a.shape
    n = b.shape[1]
    tm, tn, tk = _tile(m, tm), _tile(n, tn), _tile(k_dim, tk)

    def body(a_ref, b_ref, o_ref):
        @pl.when(pl.program_id(2) == 0)
        def _():
            o_ref[...] = jnp.zeros_like(o_ref)

        o_ref[...] += _dot_tn(a_ref[...].astype(BF16), b_ref[...].astype(BF16))

    return pl.pallas_call(
        body,
        name=name,
        grid=(m // tm, n // tn, k_dim // tk),
        in_specs=[pl.BlockSpec((tk, tm), lambda i, j, k: (k, i)), pl.BlockSpec((tk, tn), lambda i, j, k: (k, j))],
        out_specs=pl.BlockSpec((tm, tn), lambda i, j, k: (i, j)),
        out_shape=_sds((m, n), F32),
        compiler_params=_params(("parallel", "parallel", "arbitrary")),
    )(a, b)


def _rope_tables(pos_col):
    n = pos_col.shape[0]
    half = QK_ROPE // 2
    inv_freq = (ROPE_THETA ** (-np.arange(half, dtype=np.float32) / half)).astype(np.float32)
    freq_row = jnp.asarray(np.concatenate([inv_freq, inv_freq, np.zeros(64, np.float32)])[None, :])

    def body(p_ref, f_ref, c_ref, s_ref):
        ang = p_ref[...].astype(F32) * f_ref[...]
        c_ref[...] = jnp.cos(ang)
        s_ref[...] = jnp.sin(ang)

    return _row_call(body, "rope_tables", n, min(n, 1024), [pos_col], [freq_row], [((n, 128), F32)] * 2)


def _rope_rot(v):
    lane = lax.broadcasted_iota(jnp.int32, v.shape, 1)
    return jnp.where(lane < 32, -pltpu.roll(v, 96, 1), jnp.where(lane < 64, pltpu.roll(v, 32, 1), 0.0))


def _rope_rot_t(v):
    lane = lax.broadcasted_iota(jnp.int32, v.shape, 1)
    return jnp.where(lane < 32, pltpu.roll(v, 96, 1), jnp.where(lane < 64, -pltpu.roll(v, 32, 1), 0.0))


def _in_proj(x, norm_mix, w_in_pad):
    n = x.shape[0]

    def body(x_ref, g_ref, w_ref, xn_ref, u_ref, ql_ref, kvl_ref, gs_ref, gm_ref):
        xn, _ = _rms(x_ref[...], g_ref[...], D_MODEL)
        xb = xn.astype(BF16)
        xn_ref[...] = xb
        for ref, (a, b) in zip((u_ref, ql_ref, kvl_ref, gs_ref, gm_ref), IN_SEGS):
            ref[...] = _dot(xb, w_ref[:, a:b])

    outs = [((n, D_MODEL), BF16)] + [((n, b - a), F32) for a, b in IN_SEGS]
    return _row_call(body, "in_proj", n, ROW_T, [x], [norm_mix, w_in_pad], outs)


def _ssm_prep_fn(a_re, a_im, log_dt, b_re_x, b_im_x):
    dt = jnp.exp(log_dt)
    mag = jnp.exp(a_re * dt)
    lr = mag * jnp.cos(a_im * dt)
    li = mag * jnp.sin(a_im * dt)
    den = a_re * a_re + a_im * a_im
    fr = ((lr - 1.0) * a_re + li * a_im) / den
    fi = (li * a_re - (lr - 1.0) * a_im) / den
    return lr, li, fr * b_re_x - fi * b_im_x, fr * b_im_x + fi * b_re_x


def _ssm_prep(a_re, a_im, log_dt, b_re_x, b_im_x):
    def body(ar, ai, ld, br, bi, lam_ref, bblk_ref):
        lr, li, bbr, bbi = _ssm_prep_fn(ar[...], ai[...], ld[...], br[...], bi[...])
        lam_ref[0:1, :] = lr
        lam_ref[1:2, :] = li
        bblk_ref[:, 0:N_STATE] = bbr.astype(BF16)
        bblk_ref[:, N_STATE:] = bbi.astype(BF16)

    return pl.pallas_call(
        body,
        name="ssm_prep",
        out_shape=[_sds((2, N_STATE), F32), _sds((SSM_WIDTH, 2 * N_STATE), BF16)],
        compiler_params=_params(),
    )(a_re, a_im, log_dt, b_re_x, b_im_x)


def _ssm_prep_bwd(a_re, a_im, log_dt, b_re_x, b_im_x, dlam, dbblk):
    def body(ar, ai, ld, br, bi, dl, db, dar, dai, dld, dbr, dbi):
        _, vjp = jax.vjp(_ssm_prep_fn, ar[...], ai[...], ld[...], br[...], bi[...])
        g = vjp((dl[0:1, :], dl[1:2, :], db[:, 0:N_STATE], db[:, N_STATE:]))
        dar[...] = g[0]
        dai[...] = g[1]
        grp = lax.broadcasted_iota(jnp.int32, (SSM_GROUPS, N_STATE), 0)
        lane = lax.broadcasted_iota(jnp.int32, (SSM_GROUPS, N_STATE), 1)
        sel = (lane // SSM_STATE) == grp
        dld[...] = jnp.sum(jnp.where(sel, jnp.broadcast_to(g[2], (SSM_GROUPS, N_STATE)), 0.0), axis=1, keepdims=True)
        dbr[...] = g[3]
        dbi[...] = g[4]

    return pl.pallas_call(
        body,
        name="ssm_prep_bwd",
        out_shape=[_sds((1, N_STATE), F32), _sds((1, N_STATE), F32), _sds((SSM_GROUPS, 1), F32),
                   _sds((SSM_WIDTH, N_STATE), F32), _sds((SSM_WIDTH, N_STATE), F32)],
        compiler_params=_params(),
    )(a_re, a_im, log_dt, b_re_x, b_im_x, dlam, dbblk)


def _scan_powers(lam_ref, pw_ref, n_lev, conj):
    pr = lam_ref[0:1, :]
    pi = lam_ref[1:2, :]
    if conj:
        pi = -pi
    for lev in range(n_lev):
        pw_ref[2 * lev:2 * lev + 1, :] = pr
        pw_ref[2 * lev + 1:2 * lev + 2, :] = pi
        pr, pi = pr * pr - pi * pi, 2.0 * pr * pi


def _scan(buf_a, buf_b, pw_ref, t, reverse):
    src, dst = buf_a, buf_b
    n_lev = int(math.log2(t))
    for lev in range(n_lev):
        d = 1 << lev
        keep = pl.ds(t - d, d) if reverse else pl.ds(0, d)
        upd = pl.ds(0, t - d) if reverse else pl.ds(d, t - d)
        frm = pl.ds(d, t - d) if reverse else pl.ds(0, t - d)

        def col_block(cb, carry, src=src, dst=dst, lev=lev, keep=keep, upd=upd, frm=frm):
            c0 = pl.multiple_of(cb * SCAN_CB, SCAN_CB)
            re = pl.ds(c0, SCAN_CB)
            im = pl.ds(pl.multiple_of(N_STATE + cb * SCAN_CB, SCAN_CB), SCAN_CB)
            pr = pw_ref[pl.ds(2 * lev, 1), re]
            pi = pw_ref[pl.ds(2 * lev + 1, 1), re]
            sr = src[frm, re]
            si = src[frm, im]
            dst[upd, re] = src[upd, re] + pr * sr - pi * si
            dst[upd, im] = src[upd, im] + pr * si + pi * sr
            dst[keep, re] = src[keep, re]
            dst[keep, im] = src[keep, im]
            return carry

        lax.fori_loop(0, N_STATE // SCAN_CB, col_block, 0)
        src, dst = dst, src
    return src


def _ssm_fwd(u, bblk, cblk, lam, d_row, w_glu, b_glu, w_o_ssm):
    n = u.shape[0]
    t = min(SCAN_T, n)
    n_lev = int(math.log2(t))
    kb = 512

    def body(u_ref, bblk_ref, cblk_ref, lam_ref, d_ref, wg_ref, bg_ref, wo_ref, y_ref, ys_ref, st_ref,
             buf_a, buf_b, pw_ref, carry_ref):
        @pl.when(pl.program_id(0) == 0)
        def _():
            carry_ref[...] = jnp.zeros_like(carry_ref)

        _scan_powers(lam_ref, pw_ref, n_lev, False)
        st_ref[0] = carry_ref[...]
        u_t = u_ref[...]
        ub = u_t.astype(BF16)
        for c in range(0, 2 * N_STATE, kb):
            buf_a[:, c:c + kb] = _dot(ub, bblk_ref[:, c:c + kb])
        lr, li = lam_ref[0:1, :], lam_ref[1:2, :]
        cr, ci = carry_ref[0:1, 0:N_STATE], carry_ref[0:1, N_STATE:]
        buf_a[0:1, 0:N_STATE] += lr * cr - li * ci
        buf_a[0:1, N_STATE:] += lr * ci + li * cr
        res = _scan(buf_a, buf_b, pw_ref, t, False)
        carry_ref[...] = jnp.broadcast_to(res[t - 1:t, :], carry_ref.shape)
        y = d_ref[...] * u_t
        for c in range(0, 2 * N_STATE, kb):
            y += _dot(res[:, c:c + kb].astype(BF16), cblk_ref[c:c + kb, :])
        y_ref[...] = y
        z, _ = _gelu(y)
        s = _sigmoid(_dot(z.astype(BF16), wg_ref[...]) + bg_ref[...])
        ys_ref[...] = _dot((z * s).astype(BF16), wo_ref[...])

    consts = [bblk, cblk, lam, d_row, w_glu, b_glu, w_o_ssm]
    return pl.pallas_call(
        body,
        name="ssm_fwd",
        grid=(n // t,),
        in_specs=[_rows(u, t)] + [_const(a) for a in consts],
        out_specs=[pl.BlockSpec((t, SSM_WIDTH), lambda i: (i, 0)), pl.BlockSpec((t, D_MODEL), lambda i: (i, 0)),
                   pl.BlockSpec((1, 8, 2 * N_STATE), lambda i: (i, 0, 0))],
        out_shape=[_sds((n, SSM_WIDTH), F32), _sds((n, D_MODEL), F32), _sds((n // t, 8, 2 * N_STATE), F32)],
        scratch_shapes=[pltpu.VMEM((t, 2 * N_STATE), F32), pltpu.VMEM((t, 2 * N_STATE), F32),
                        pltpu.VMEM((2 * n_lev, N_STATE), F32), pltpu.VMEM((8, 2 * N_STATE), F32)],
        compiler_params=_params(("arbitrary",)),
    )(u, *consts)


def _head_norm_rope(slab, gain, cos_t, sin_t):
    xn, inv = _rms(slab, gain, QK_HEAD)
    lo, hi = xn[:, 0:128], xn[:, 128:256]
    return jnp.concatenate([lo, hi * cos_t + _rope_rot(hi) * sin_t], axis=-1), inv


def _head_norm_rope_bwd(g, slab, gain, inv, cos_t, sin_t):
    g_lo, g_hi = g[:, 0:128], g[:, 128:256]
    g_n = jnp.concatenate([g_lo, g_hi * cos_t + _rope_rot_t(g_hi * sin_t)], axis=-1)
    return _rms_bwd(g_n, slab, gain, inv, QK_HEAD)


def _qkv_prep(ql, kvl, q_a_norm, kv_a_norm, wq, wkv, gq, gk, cos_t, sin_t):
    n = ql.shape[0]

    def body(ql_ref, kvl_ref, cos_ref, sin_ref, qa_ref, ka_ref, wq_ref, wkv_ref, gq_ref, gk_ref, q_ref, k_ref, v_ref):
        cos_t, sin_t = cos_ref[...], sin_ref[...]
        qa, _ = _rms(ql_ref[...], qa_ref[...], Q_LORA)
        q_pre = _dot(qa.astype(BF16), wq_ref[...])
        kvl_t = kvl_ref[...]
        ca, _ = _rms(kvl_t[:, 0:KV_LORA], ka_ref[...], KV_LORA)
        kv_pre = _dot(ca.astype(BF16), wkv_ref[...])
        kpe = kvl_t[:, KV_LORA:KV_LAT_PAD]
        for h in range(N_HEADS):
            qh, _ = _head_norm_rope(q_pre[:, h * QK_PAD:(h + 1) * QK_PAD], gq_ref[...], cos_t, sin_t)
            q_ref[h] = (qh * ATT_SCALE).astype(BF16)
            k_slab = jnp.concatenate([kv_pre[:, h * QK_NOPE:(h + 1) * QK_NOPE], kpe], axis=-1)
            kh, _ = _head_norm_rope(k_slab, gk_ref[...], cos_t, sin_t)
            k_ref[h] = kh.astype(BF16)
            v_ref[h] = kv_pre[:, N_HEADS * QK_NOPE + h * V_HEAD:N_HEADS * QK_NOPE + (h + 1) * V_HEAD].astype(BF16)

    outs = [((N_HEADS, n, QK_PAD), BF16), ((N_HEADS, n, QK_PAD), BF16), ((N_HEADS, n, V_HEAD), BF16)]
    return _row_call(body, "qkv_prep", n, ROW_T, [ql, kvl, cos_t, sin_t], [q_a_norm, kv_a_norm, wq, wkv, gq, gk], outs)


def _causal_mask(s, t):
    row = lax.broadcasted_iota(jnp.int32, (t, t), 0)
    col = lax.broadcasted_iota(jnp.int32, (t, t), 1)
    return jnp.where(col <= row, s, -jnp.inf)


def _attn_fwd(q, k, v):
    n = q.shape[1]
    t = min(ATT_T, n)

    def body(q_ref, k_ref, v_ref, o_ref, lse_ref):
        i = pl.program_id(1)
        qt = q_ref[0]

        def kv_tile(j, carry):
            m, l, acc = carry
            r0 = pl.multiple_of(j * t, t)
            s = _dot_nt(qt, k_ref[0, pl.ds(r0, t), :])
            m_new = jnp.maximum(m, jnp.max(s, -1, keepdims=True))
            alpha = jnp.exp(m - m_new)
            p = jnp.exp(s - m_new)
            l = alpha * l + jnp.sum(p, -1, keepdims=True)
            acc = alpha * acc + _dot(p.astype(BF16), v_ref[0, pl.ds(r0, t), :])
            return m_new, l, acc

        init = (jnp.full((t, 1), -jnp.inf, F32), jnp.zeros((t, 1), F32), jnp.zeros((t, V_HEAD), F32))
        m, l, acc = lax.fori_loop(0, i, kv_tile, init)
        r0 = pl.multiple_of(i * t, t)
        s = _causal_mask(_dot_nt(qt, k_ref[0, pl.ds(r0, t), :]), t)
        m_new = jnp.maximum(m, jnp.max(s, -1, keepdims=True))
        alpha = jnp.exp(m - m_new)
        p = jnp.exp(s - m_new)
        l = alpha * l + jnp.sum(p, -1, keepdims=True)
        acc = alpha * acc + _dot(p.astype(BF16), v_ref[0, pl.ds(r0, t), :])
        o_ref[...] = acc / l
        lse_ref[0] = m_new + jnp.log(l)

    return pl.pallas_call(
        body,
        name="attn_fwd",
        grid=(N_HEADS, n // t),
        in_specs=[pl.BlockSpec((1, t, QK_PAD), lambda h, i: (h, i, 0)),
                  pl.BlockSpec((1, n, QK_PAD), lambda h, i: (h, 0, 0)),
                  pl.BlockSpec((1, n, V_HEAD), lambda h, i: (h, 0, 0))],
        out_specs=[pl.BlockSpec((t, V_HEAD), lambda h, i: (i, h)), pl.BlockSpec((1, t, 1), lambda h, i: (h, i, 0))],
        out_shape=[_sds((n, N_HEADS * V_HEAD), F32), _sds((N_HEADS, n, 1), F32)],
        compiler_params=_params(("parallel", "arbitrary")),
    )(q, k, v)


def _merge(attn, gs, gm, y_ssm, x, w_o_mla, w_out):
    n = x.shape[0]

    def body(at_ref, gs_ref, gm_ref, ys_ref, x_ref, wo_ref, wout_ref, h_ref, mx_ref, ym_ref):
        y_mla = _dot(at_ref[...].astype(BF16), wo_ref[...])
        ym_ref[...] = y_mla
        mixed = (_sigmoid(gs_ref[...]) * ys_ref[...] + _sigmoid(gm_ref[...]) * y_mla).astype(BF16)
        mx_ref[...] = mixed
        h_ref[...] = x_ref[...] + _dot(mixed, wout_ref[...])

    outs = [((n, D_MODEL), F32), ((n, D_MODEL), BF16), ((n, D_MODEL), F32)]
    return _row_call(body, "merge", n, ROW_T, [attn, gs, gm, y_ssm, x], [w_o_mla, w_out], outs)


def _mlp_fwd_loss(h, target, norm_mlp, w_up, w_down):
    n = h.shape[0]

    def body(h_ref, t_ref, g_ref, wu_ref, wd_ref, hn_ref, do_ref, loss_ref):
        h_t = h_ref[...]
        hn, _ = _rms(h_t, g_ref[...], D_MODEL)
        hb = hn.astype(BF16)
        hn_ref[...] = hb
        a = jnp.maximum(_dot(hb, wu_ref[...]), 0.0)
        out = h_t + _dot((a * a).astype(BF16), wd_ref[...])
        err = out - t_ref[...]
        do_ref[...] = err * (1.0 / D_MODEL)
        _acc(loss_ref, jnp.broadcast_to(jnp.sum(err * err) * (0.5 / D_MODEL), loss_ref.shape))

    outs = [((n, D_MODEL), BF16), ((n, D_MODEL), F32)]
    return _row_call(body, "mlp_fwd_loss", n, ROW_T, [h, target], [norm_mlp, w_up, w_down], outs, [((8, 128), F32)])


def _mlp_bwd(dout, hn, h, norm_mlp, w_up, w_down):
    n = h.shape[0]

    def body(do_ref, hn_ref, h_ref, g_ref, wu_ref, wd_ref, hid_ref, da_ref, dh_ref, dg_ref):
        dout_t = do_ref[...]
        a = jnp.maximum(_dot(hn_ref[...], wu_ref[...]), 0.0)
        hid_ref[...] = (a * a).astype(BF16)
        da = (_dot_nt(dout_t.astype(BF16), wd_ref[...]) * (2.0 * a)).astype(BF16)
        da_ref[...] = da
        dhn = _dot_nt(da, wu_ref[...])
        h_t = h_ref[...]
        inv = lax.rsqrt(jnp.sum(h_t * h_t, -1, keepdims=True) * (1.0 / D_MODEL) + EPS)
        dx, dg = _rms_bwd(dhn, h_t, g_ref[...], inv, D_MODEL)
        dh_ref[...] = dout_t + dx
        _acc(dg_ref, jnp.sum(dg, 0, keepdims=True))

    outs = [((n, D_FF), BF16), ((n, D_FF), BF16), ((n, D_MODEL), F32)]
    return _row_call(body, "mlp_bwd", n, ROW_T, [dout, hn, h], [norm_mlp, w_up, w_down], outs, [((1, D_MODEL), F32)])


def _merge_bwd(dh, gs, gm, y_ssm, y_mla, w_out, w_o_mla):
    n = dh.shape[0]

    def body(dh_ref, gs_ref, gm_ref, ys_ref, ym_ref, wout_ref, wo_ref, dgs_ref, dgm_ref, dys_ref, dym_ref, dat_ref):
        dmix = _dot_nt(dh_ref[...].astype(BF16), wout_ref[...])
        sgs, sgm = _sigmoid(gs_ref[...]), _sigmoid(gm_ref[...])
        dgs_ref[...] = (dmix * ys_ref[...] * sgs * (1.0 - sgs)).astype(BF16)
        dgm_ref[...] = (dmix * ym_ref[...] * sgm * (1.0 - sgm)).astype(BF16)
        dys_ref[...] = (dmix * sgs).astype(BF16)
        dym = (dmix * sgm).astype(BF16)
        dym_ref[...] = dym
        dat_ref[...] = _dot_nt(dym, wo_ref[...])

    outs = [((n, D_MODEL), BF16)] * 4 + [((n, D_MODEL), F32)]
    return _row_call(body, "merge_bwd", n, ROW_T, [dh, gs, gm, y_ssm, y_mla], [w_out, w_o_mla], outs)


def _attn_bwd(q, k, v, out, lse, dout):
    n = q.shape[1]
    t = min(ATT_T, n)
    nt = n // t

    def body(q_ref, k_ref, v_ref, o_ref, lse_ref, do_ref, dq_ref, dk_ref, dv_ref, delta_ref):
        j = pl.program_id(1)

        @pl.when(j == 0)
        def _():
            dq_ref[...] = jnp.zeros_like(dq_ref)
            delta_ref[...] = jnp.sum(do_ref[...] * o_ref[...], -1, keepdims=True)

        kt = k_ref[0]
        vt = v_ref[0]

        def q_tile(i, carry, diag):
            dk, dv = carry
            r0 = pl.multiple_of(i * t, t)
            rows = pl.ds(r0, t)
            qt = q_ref[0, rows, :]
            s = _dot_nt(qt, kt)
            if diag:
                s = _causal_mask(s, t)
            p = jnp.exp(s - lse_ref[0, rows, :])
            dot = do_ref[rows, :].astype(BF16)
            dv = dv + _dot_tn(p.astype(BF16), dot)
            ds = (p * (_dot_nt(dot, vt) - delta_ref[rows, :])).astype(BF16)
            dk = dk + _dot_tn(ds, qt)
            dq_ref[0, rows, :] += _dot(ds, kt)
            return dk, dv

        carry = q_tile(j, (jnp.zeros((t, QK_PAD), F32), jnp.zeros((t, V_HEAD), F32)), True)
        dk, dv = lax.fori_loop(j + 1, nt, functools.partial(q_tile, diag=False), carry)
        dk_ref[0] = dk
        dv_ref[0] = dv

    return pl.pallas_call(
        body,
        name="attn_bwd",
        grid=(N_HEADS, nt),
        in_specs=[pl.BlockSpec((1, n, QK_PAD), lambda h, j: (h, 0, 0)),
                  pl.BlockSpec((1, t, QK_PAD), lambda h, j: (h, j, 0)),
                  pl.BlockSpec((1, t, V_HEAD), lambda h, j: (h, j, 0)),
                  pl.BlockSpec((n, V_HEAD), lambda h, j: (0, h)),
                  pl.BlockSpec((1, n, 1), lambda h, j: (h, 0, 0)),
                  pl.BlockSpec((n, V_HEAD), lambda h, j: (0, h))],
        out_specs=[pl.BlockSpec((1, n, QK_PAD), lambda h, j: (h, 0, 0)),
                   pl.BlockSpec((1, t, QK_PAD), lambda h, j: (h, j, 0)),
                   pl.BlockSpec((1, t, V_HEAD), lambda h, j: (h, j, 0))],
        out_shape=[_sds((N_HEADS, n, QK_PAD), F32), _sds((N_HEADS, n, QK_PAD), F32), _sds((N_HEADS, n, V_HEAD), F32)],
        scratch_shapes=[pltpu.VMEM((n, 1), F32)],
        compiler_params=_params(("parallel", "arbitrary")),
    )(q, k, v, out, lse, dout)


def _qkv_prep_bwd(ql, kvl, dq, dk, dv, q_a_norm, kv_a_norm, wq, wkv, gq, gk, cos_t, sin_t):
    n = ql.shape[0]

    def body(ql_ref, kvl_ref, cos_ref, sin_ref, dq_ref, dk_ref, dv_ref, qa_ref, ka_ref, wq_ref, wkv_ref, gq_ref, gk_ref,
             dql_ref, dkvl_ref, qab_ref, dqp_ref, cab_ref, dkvp_ref, dqa_ref, dka_ref, dgq_ref, dgk_ref):
        cos_t, sin_t = cos_ref[...], sin_ref[...]
        ql_t = ql_ref[...]
        qa, inv_qa = _rms(ql_t, qa_ref[...], Q_LORA)
        qab = qa.astype(BF16)
        qab_ref[...] = qab
        q_pre = _dot(qab, wq_ref[...])
        kvl_t = kvl_ref[...]
        ckv = kvl_t[:, 0:KV_LORA]
        ca, inv_ca = _rms(ckv, ka_ref[...], KV_LORA)
        cab = ca.astype(BF16)
        cab_ref[...] = cab
        kv_pre = _dot(cab, wkv_ref[...])
        kpe = kvl_t[:, KV_LORA:KV_LAT_PAD]
        dgq = jnp.zeros((1, QK_PAD), F32)
        dgk = jnp.zeros((1, QK_PAD), F32)
        dkpe = jnp.zeros_like(kpe)
        for h in range(N_HEADS):
            q_slab = q_pre[:, h * QK_PAD:(h + 1) * QK_PAD]
            inv = lax.rsqrt(jnp.sum(q_slab * q_slab, -1, keepdims=True) * (1.0 / QK_HEAD) + EPS)
            d_slab, dg = _head_norm_rope_bwd(dq_ref[h] * ATT_SCALE, q_slab, gq_ref[...], inv, cos_t, sin_t)
            dqp_ref[:, h * QK_PAD:(h + 1) * QK_PAD] = d_slab.astype(BF16)
            dgq += jnp.sum(dg, 0, keepdims=True)
            k_slab = jnp.concatenate([kv_pre[:, h * QK_NOPE:(h + 1) * QK_NOPE], kpe], axis=-1)
            inv = lax.rsqrt(jnp.sum(k_slab * k_slab, -1, keepdims=True) * (1.0 / QK_HEAD) + EPS)
            d_slab, dg = _head_norm_rope_bwd(dk_ref[h], k_slab, gk_ref[...], inv, cos_t, sin_t)
            dkvp_ref[:, h * QK_NOPE:(h + 1) * QK_NOPE] = d_slab[:, 0:QK_NOPE].astype(BF16)
            dkpe += d_slab[:, QK_NOPE:QK_PAD]
            dgk += jnp.sum(dg, 0, keepdims=True)
            dkvp_ref[:, N_HEADS * QK_NOPE + h * V_HEAD:N_HEADS * QK_NOPE + (h + 1) * V_HEAD] = dv_ref[h].astype(BF16)
        dqa = _dot_nt(dqp_ref[...], wq_ref[...])
        dx, dg = _rms_bwd(dqa, ql_t, qa_ref[...], inv_qa, Q_LORA)
        dql_ref[...] = dx.astype(BF16)
        _acc(dqa_ref, jnp.sum(dg, 0, keepdims=True))
        dca = _dot_nt(dkvp_ref[...], wkv_ref[...])
        dx, dg = _rms_bwd(dca, ckv, ka_ref[...], inv_ca, KV_LORA)
        dkvl_ref[:, 0:KV_LORA] = dx.astype(BF16)
        dkvl_ref[:, KV_LORA:KV_LAT_PAD] = dkpe.astype(BF16)
        _acc(dka_ref, jnp.sum(dg, 0, keepdims=True))
        _acc(dgq_ref, dgq)
        _acc(dgk_ref, dgk)

    row_outs = [((n, Q_LORA), BF16), ((n, KV_LAT_PAD), BF16), ((n, Q_LORA), BF16), ((n, N_HEADS * QK_PAD), BF16),
                ((n, KV_LORA), BF16), ((n, N_HEADS * (QK_NOPE + V_HEAD)), BF16)]
    acc_outs = [((1, Q_LORA), F32), ((1, KV_LORA), F32), ((1, QK_PAD), F32), ((1, QK_PAD), F32)]
    return _row_call(body, "qkv_prep_bwd", n, ROW_T, [ql, kvl, cos_t, sin_t, dq, dk, dv],
                     [q_a_norm, kv_a_norm, wq, wkv, gq, gk], row_outs, acc_outs)


def _glu_bwd(dy_ssm, y, w_glu, b_glu, w_o_ssm):
    n = y.shape[0]

    def body(dys_ref, y_ref, wg_ref, bg_ref, wo_ref, dy_ref, zg_ref, z_ref, dt_ref, db_ref):
        y_t = y_ref[...]
        z, th = _gelu(y_t)
        zb = z.astype(BF16)
        z_ref[...] = zb
        s = _sigmoid(_dot(zb, wg_ref[...]) + bg_ref[...])
        zg_ref[...] = (z * s).astype(BF16)
        dzg = _dot_nt(dys_ref[...], wo_ref[...])
        dt = dzg * z * s * (1.0 - s)
        dtb = dt.astype(BF16)
        dt_ref[...] = dtb
        dz = dzg * s + _dot_nt(dtb, wg_ref[...])
        dy_ref[...] = dz * _gelu_grad(y_t, th)
        _acc(db_ref, jnp.sum(dt, 0, keepdims=True))

    outs = [((n, SSM_WIDTH), F32)] + [((n, SSM_WIDTH), BF16)] * 3
    return _row_call(body, "glu_bwd", n, ROW_T, [dy_ssm, y], [w_glu, b_glu, w_o_ssm], outs, [((1, SSM_WIDTH), F32)])


def _ssm_bwd(u, dy, st, bblk, cblk, lam, d_row):
    n = u.shape[0]
    t = min(SCAN_T, n)
    nc = n // t
    n_lev = int(math.log2(t))
    kb = 512

    def body(u_ref, dy_ref, st_ref, bblk_ref, cblk_ref, lam_ref, d_ref, du_ref, xs_ref, as_ref, dlam_ref, dd_ref,
             buf_a, buf_b, buf_c, buf_d, pw_ref, pwc_ref, carry_ref):
        @pl.when(pl.program_id(0) == 0)
        def _():
            carry_ref[...] = jnp.zeros_like(carry_ref)

        _scan_powers(lam_ref, pw_ref, n_lev, False)
        _scan_powers(lam_ref, pwc_ref, n_lev, True)
        lr, li = lam_ref[0:1, :], lam_ref[1:2, :]
        u_t = u_ref[...]
        ub = u_t.astype(BF16)
        dy_t = dy_ref[...]
        dyb = dy_t.astype(BF16)
        for c in range(0, 2 * N_STATE, kb):
            buf_a[:, c:c + kb] = _dot(ub, bblk_ref[:, c:c + kb])
        s_re, s_im = st_ref[0, 0:1, 0:N_STATE], st_ref[0, 0:1, N_STATE:]
        buf_a[0:1, 0:N_STATE] += lr * s_re - li * s_im
        buf_a[0:1, N_STATE:] += lr * s_im + li * s_re
        xs = _scan(buf_a, buf_b, pw_ref, t, False)
        for c in range(0, 2 * N_STATE, kb):
            buf_c[:, c:c + kb] = _dot_nt(dyb, cblk_ref[c:c + kb, :])
        a_re, a_im = carry_ref[0:1, 0:N_STATE], carry_ref[0:1, N_STATE:]
        buf_c[t - 1:t, 0:N_STATE] += lr * a_re + li * a_im
        buf_c[t - 1:t, N_STATE:] += lr * a_im - li * a_re
        ad = _scan(buf_c, buf_d, pwc_ref, t, True)
        carry_ref[...] = jnp.broadcast_to(ad[0:1, :], carry_ref.shape)
        du = d_ref[...] * dy_t
        for c in range(0, 2 * N_STATE, kb):
            adb = ad[:, c:c + kb].astype(BF16)
            as_ref[:, c:c + kb] = adb
            xs_ref[:, c:c + kb] = xs[:, c:c + kb].astype(BF16)
            du += _dot_nt(adb, bblk_ref[:, c:c + kb])
        du_ref[...] = du.astype(BF16)
        for c in range(0, N_STATE, kb):
            re, im = pl.ds(c, kb), pl.ds(N_STATE + c, kb)
            xr, xi = xs[pl.ds(0, t - 1), re], xs[pl.ds(0, t - 1), im]
            ar, ai = ad[pl.ds(1, t - 1), re], ad[pl.ds(1, t - 1), im]
            x0r, x0i = st_ref[0, 0:1, re], st_ref[0, 0:1, im]
            a0r, a0i = ad[0:1, re], ad[0:1, im]
            dlam_part_re = jnp.sum(ar * xr + ai * xi, 0, keepdims=True) + a0r * x0r + a0i * x0i
            dlam_part_im = jnp.sum(ai * xr - ar * xi, 0, keepdims=True) + a0i * x0r - a0r * x0i

            @pl.when(pl.program_id(0) == 0)
            def _(c=c):
                dlam_ref[0:1, c:c + kb] = jnp.zeros((1, kb), F32)
                dlam_ref[1:2, c:c + kb] = jnp.zeros((1, kb), F32)

            dlam_ref[0:1, c:c + kb] += dlam_part_re
            dlam_ref[1:2, c:c + kb] += dlam_part_im
        _acc(dd_ref, jnp.sum(dy_t * u_t, 0, keepdims=True))

    rev = lambda i: (nc - 1 - i, 0)
    consts = [bblk, cblk, lam, d_row]
    return pl.pallas_call(
        body,
        name="ssm_bwd",
        grid=(nc,),
        in_specs=[pl.BlockSpec((t, SSM_WIDTH), rev), pl.BlockSpec((t, SSM_WIDTH), rev),
                  pl.BlockSpec((1, 8, 2 * N_STATE), lambda i: (nc - 1 - i, 0, 0))] + [_const(a) for a in consts],
        out_specs=[pl.BlockSpec((t, SSM_WIDTH), rev), pl.BlockSpec((t, 2 * N_STATE), rev),
                   pl.BlockSpec((t, 2 * N_STATE), rev), pl.BlockSpec((2, N_STATE), lambda i: (0, 0)),
                   pl.BlockSpec((1, SSM_WIDTH), lambda i: (0, 0))],
        out_shape=[_sds((n, SSM_WIDTH), BF16), _sds((n, 2 * N_STATE), BF16), _sds((n, 2 * N_STATE), BF16),
                   _sds((2, N_STATE), F32), _sds((1, SSM_WIDTH), F32)],
        scratch_shapes=[pltpu.VMEM((t, 2 * N_STATE), F32)] * 4
        + [pltpu.VMEM((2 * n_lev, N_STATE), F32)] * 2 + [pltpu.VMEM((8, 2 * N_STATE), F32)],
        compiler_params=_params(("arbitrary",)),
    )(u, dy, st, *consts)


def _in_proj_bwd(pieces, dh, x, norm_mix, w_in_pad):
    n = x.shape[0]

    def body(du_ref, dql_ref, dkvl_ref, dgs_ref, dgm_ref, dh_ref, x_ref, g_ref, w_ref, dx_ref, dp_ref, dg_ref):
        dxn = jnp.zeros((dh_ref.shape[0], D_MODEL), F32)
        for ref, (a, b) in zip((du_ref, dql_ref, dkvl_ref, dgs_ref, dgm_ref), IN_SEGS):
            piece = ref[...]
            dp_ref[:, a:b] = piece
            dxn += _dot_nt(piece, w_ref[:, a:b])
        x_t = x_ref[...]
        inv = lax.rsqrt(jnp.sum(x_t * x_t, -1, keepdims=True) * (1.0 / D_MODEL) + EPS)
        dx, dg = _rms_bwd(dxn, x_t, g_ref[...], inv, D_MODEL)
        dx_ref[...] = dh_ref[...] + dx
        _acc(dg_ref, jnp.sum(dg, 0, keepdims=True))

    outs = [((n, D_MODEL), F32), ((n, D_IN_PAD), BF16)]
    return _row_call(body, "in_proj_bwd", n, ROW_T, list(pieces) + [dh, x], [norm_mix, w_in_pad], outs,
                     [((1, D_MODEL), F32)])


def _block_diag(a, rows_per_group, cols_per_group):
    eye = jnp.eye(SSM_GROUPS, dtype=a.dtype)
    return (a[:, :, None, :] * eye[:, None, :, None]).reshape(SSM_GROUPS * rows_per_group, SSM_GROUPS * cols_per_group)


def _block_diag_extract(m, rows_per_group, cols_per_group):
    m4 = m.reshape(SSM_GROUPS, rows_per_group, SSM_GROUPS, cols_per_group)
    eye = jnp.eye(SSM_GROUPS, dtype=m.dtype)
    return jnp.sum(m4 * eye[:, None, :, None], axis=2)


def _pad_heads(w):
    r = w.shape[0]
    return jnp.pad(w.reshape(r, N_HEADS, QK_HEAD), ((0, 0), (0, 0), (0, QK_PAD - QK_HEAD))).reshape(r, N_HEADS * QK_PAD)


def _unpad_heads(w):
    r = w.shape[0]
    return w.reshape(r, N_HEADS, QK_PAD)[:, :, :QK_HEAD].reshape(r, N_HEADS * QK_HEAD)


def _split_kv(w):
    r = w.shape[0]
    w3 = w.reshape(r, N_HEADS, QK_NOPE + V_HEAD)
    return jnp.concatenate([w3[:, :, :QK_NOPE].reshape(r, -1), w3[:, :, QK_NOPE:].reshape(r, -1)], axis=1)


def _unsplit_kv(w):
    r = w.shape[0]
    k3 = w[:, :N_HEADS * QK_NOPE].reshape(r, N_HEADS, QK_NOPE)
    v3 = w[:, N_HEADS * QK_NOPE:].reshape(r, N_HEADS, V_HEAD)
    return jnp.concatenate([k3, v3], axis=2).reshape(r, N_HEADS * (QK_NOPE + V_HEAD))


def _pad_in(w):
    return jnp.concatenate([w[:, :KV_END], jnp.zeros((w.shape[0], D_IN_PAD - D_IN), w.dtype), w[:, KV_END:]], axis=1)


def _unpad_in(w):
    return jnp.concatenate([w[:, :KV_END], w[:, KV_END + D_IN_PAD - D_IN:]], axis=1)


def _pad_gain(g):
    return jnp.pad(g, ((0, 0), (0, QK_PAD - QK_HEAD)))


def _local_step(x, pos_col, target, wts, small):
    w_in_pad = _pad_in(wts["w_in"])
    wq = _pad_heads(wts["w_q_b"])
    wkv = _split_kv(wts["w_kv_b"])
    gq, gk = _pad_gain(small["q_norm"]), _pad_gain(small["k_norm"])

    a_re = small["ssm_a_re"].reshape(1, N_STATE)
    a_im = small["ssm_a_im"].reshape(1, N_STATE)
    log_dt = jnp.repeat(small["ssm_log_dt"].reshape(SSM_GROUPS), SSM_STATE).reshape(1, N_STATE)
    b_re_x = _block_diag(jnp.transpose(small["ssm_b_re"][0], (0, 2, 1)), SSM_GROUP_CH, SSM_STATE)
    b_im_x = _block_diag(jnp.transpose(small["ssm_b_im"][0], (0, 2, 1)), SSM_GROUP_CH, SSM_STATE)
    c_re_x = _block_diag(jnp.transpose(small["ssm_c_re"][0], (0, 2, 1)), SSM_STATE, SSM_GROUP_CH)
    c_im_x = _block_diag(jnp.transpose(small["ssm_c_im"][0], (0, 2, 1)), SSM_STATE, SSM_GROUP_CH)
    cblk = jnp.concatenate([c_re_x, -c_im_x], axis=0).astype(BF16)
    d_row = small["ssm_d"].reshape(1, SSM_WIDTH)

    cos_t, sin_t = _rope_tables(pos_col)
    xn, u, ql, kvl, gs, gm = _in_proj(x, small["norm_mix"], w_in_pad)
    lam, bblk = _ssm_prep(a_re, a_im, log_dt, b_re_x, b_im_x)
    y, y_ssm, st = _ssm_fwd(u, bblk, cblk, lam, d_row, wts["w_glu"], small["b_glu"], wts["w_o_ssm"])
    q, k, v = _qkv_prep(ql, kvl, small["q_a_norm"], small["kv_a_norm"], wq, wkv, gq, gk, cos_t, sin_t)
    attn, lse = _attn_fwd(q, k, v)
    h, mixed, y_mla = _merge(attn, gs, gm, y_ssm, x, wts["w_o_mla"], wts["w_out"])
    hn, dout, loss = _mlp_fwd_loss(h, target, small["norm_mlp"], wts["w_up"], wts["w_down"])

    hid, da, dh, d_norm_mlp = _mlp_bwd(dout, hn, h, small["norm_mlp"], wts["w_up"], wts["w_down"])
    g_w_down = _matmul_tn(hid, dout, "dw_down")
    g_w_up = _matmul_tn(hn, da, "dw_up")
    dgs, dgm, dy_ssm, dy_mla, dattn = _merge_bwd(dh, gs, gm, y_ssm, y_mla, wts["w_out"], wts["w_o_mla"])
    g_w_out = _matmul_tn(mixed, dh, "dw_out")
    g_w_o_mla = _matmul_tn(attn, dy_mla, "dw_o_mla")
    dq, dk, dv = _attn_bwd(q, k, v, attn, lse, dattn)
    dql, dkvl, qa, dq_pre, ca, dkv_pre, d_q_a_norm, d_kv_a_norm, d_gq, d_gk = _qkv_prep_bwd(
        ql, kvl, dq, dk, dv, small["q_a_norm"], small["kv_a_norm"], wq, wkv, gq, gk, cos_t, sin_t)
    g_wq = _matmul_tn(qa, dq_pre, "dw_q_b")
    g_wkv = _matmul_tn(ca, dkv_pre, "dw_kv_b")
    dy, zg, z, dt, d_b_glu = _glu_bwd(dy_ssm, y, wts["w_glu"], small["b_glu"], wts["w_o_ssm"])
    g_w_o_ssm = _matmul_tn(zg, dy_ssm, "dw_o_ssm")
    g_w_glu = _matmul_tn(z, dt, "dw_glu")
    du, xs, ads, dlam, d_d = _ssm_bwd(u, dy, st, bblk, cblk, lam, d_row)
    d_bblk = _matmul_tn(u, ads, "d_bblk")
    d_cblk = _matmul_tn(xs, dy, "d_cblk")
    d_a_re, d_a_im, d_log_dt, d_b_re_x, d_b_im_x = _ssm_prep_bwd(a_re, a_im, log_dt, b_re_x, b_im_x, dlam, d_bblk)
    dx, dproj, d_norm_mix = _in_proj_bwd((du, dql, dkvl, dgs, dgm), dh, x, small["norm_mix"], w_in_pad)
    g_w_in = _matmul_tn(xn, dproj, "dw_in")

    big = {
        "w_in": _unpad_in(g_w_in), "w_q_b": _unpad_heads(g_wq), "w_kv_b": _unsplit_kv(g_wkv), "w_o_mla": g_w_o_mla,
        "w_glu": g_w_glu, "w_o_ssm": g_w_o_ssm, "w_out": g_w_out, "w_up": g_w_up, "w_down": g_w_down,
    }
    tr = lambda m: jnp.transpose(m, (0, 2, 1))[None]
    sm = {
        "norm_mix": d_norm_mix, "q_a_norm": d_q_a_norm, "kv_a_norm": d_kv_a_norm,
        "q_norm": d_gq[:, :QK_HEAD], "k_norm": d_gk[:, :QK_HEAD],
        "ssm_a_re": d_a_re.reshape(1, SSM_GROUPS, SSM_STATE), "ssm_a_im": d_a_im.reshape(1, SSM_GROUPS, SSM_STATE),
        "ssm_log_dt": d_log_dt.reshape(1, SSM_GROUPS),
        "ssm_b_re": tr(_block_diag_extract(d_b_re_x, SSM_GROUP_CH, SSM_STATE)),
        "ssm_b_im": tr(_block_diag_extract(d_b_im_x, SSM_GROUP_CH, SSM_STATE)),
        "ssm_c_re": tr(_block_diag_extract(d_cblk[:N_STATE], SSM_STATE, SSM_GROUP_CH)),
        "ssm_c_im": tr(-_block_diag_extract(d_cblk[N_STATE:], SSM_STATE, SSM_GROUP_CH)),
        "ssm_d": d_d.reshape(1, SSM_GROUPS, SSM_GROUP_CH), "b_glu": d_b_glu, "norm_mlp": d_norm_mlp,
    }
    return loss, dx, big, sm


ANY = pl.BlockSpec(memory_space=pl.ANY)


def _place():
    x, y, c = lax.axis_index("x"), lax.axis_index("y"), lax.axis_index("c")
    chips = [(x, y), (1 - x, y), (x, 1 - y), (1 - x, 1 - y)]
    return x, y, c, chips


def _all_gather(block, name):
    rows, lanes = block.shape

    def body(x_ref, out_ref, send_sems, recv_sems, local_sem):
        x, y, c, chips = _place()
        me, sibling = (x, y, c), (x, y, 1 - c)

        def slot(px, py, pc):
            return out_ref.at[4 * px + 2 * py + pc]

        def copy(k, blk, to, src=None):
            return pltpu.make_async_remote_copy(
                src_ref=slot(*blk) if src is None else src, dst_ref=slot(*blk),
                send_sem=send_sems.at[k], recv_sem=recv_sems.at[k], device_id=to, device_id_type=MESH)

        mine = pltpu.make_async_copy(x_ref, slot(*me), local_sem)
        mine.start()
        first = [copy(0, me, sibling, src=x_ref)]
        first += [copy(1 + j, me, (*chip, c), src=x_ref) for j, chip in enumerate(chips[1:])]
        for cp in first:
            cp.start()
        passed = [copy(4 + j, (*chip, c), sibling) for j, chip in enumerate(chips[1:])]
        for j, chip in enumerate(chips[1:]):
            copy(1 + j, (*chip, c), me).wait_recv()
            passed[j].start()
        copy(0, sibling, me).wait_recv()
        for j, chip in enumerate(chips[1:]):
            copy(4 + j, (*chip, 1 - c), me).wait_recv()
        for cp in first + passed:
            cp.wait_send()
        mine.wait()

    return pl.pallas_call(
        body,
        name=name,
        in_specs=[ANY],
        out_specs=ANY,
        out_shape=_sds((N_DEV, rows, lanes), block.dtype),
        scratch_shapes=[pltpu.SemaphoreType.DMA((7,)), pltpu.SemaphoreType.DMA((7,)), pltpu.SemaphoreType.DMA],
    )(block)


RS_CHUNKS = 8


def _reduce_scatter(parts, name):
    _, rows, lanes = parts.shape
    ch = rows // RS_CHUNKS

    def body(p_ref, out_ref, land_a, send_b, land_b, va, vb, v16, w16, sa, ra, sb, rb):
        x, y, c, chips = _place()
        sibling = (x, y, 1 - c)

        def blk(chip, core):
            return p_ref.at[4 * chip[0] + 2 * chip[1] + core]

        to_sib = [pltpu.make_async_remote_copy(
            src_ref=blk(chips[k], 1 - c), dst_ref=land_a.at[k], send_sem=sa.at[k], recv_sem=ra.at[k],
            device_id=sibling, device_id_type=MESH) for k in range(4)]
        for cp in to_sib:
            cp.start()
        to_chip = [pltpu.make_async_remote_copy(
            src_ref=send_b.at[j], dst_ref=land_b.at[j], send_sem=sb.at[j], recv_sem=rb.at[j],
            device_id=(*chips[1 + j], c), device_id_type=MESH) for j in range(3)]

        for k in (1, 2, 3, 0):
            to_sib[k].wait_recv()

            def chip_sum(i, carry, k=k):
                r = pl.ds(pl.multiple_of(i * ch, 16), ch)
                pltpu.sync_copy(blk(chips[k], c).at[r], va)
                pltpu.sync_copy(land_a.at[k, r], vb)
                if k == 0:
                    va[...] = va[...] + vb[...]
                    pltpu.sync_copy(va, out_ref.at[r])
                else:
                    v16[...] = (va[...] + vb[...]).astype(BF16)
                    pltpu.sync_copy(v16, send_b.at[k - 1, r])
                return carry

            lax.fori_loop(0, RS_CHUNKS, chip_sum, 0)
            if k != 0:
                to_chip[k - 1].start()

        for cp in to_chip:
            cp.wait_recv()

        def final_sum(i, carry):
            r = pl.ds(pl.multiple_of(i * ch, 16), ch)
            pltpu.sync_copy(out_ref.at[r], va)
            acc = va[...]
            for j in range(3):
                pltpu.sync_copy(land_b.at[j, r], w16)
                acc = acc + w16[...].astype(F32)
            va[...] = acc
            pltpu.sync_copy(va, out_ref.at[r])
            return carry

        lax.fori_loop(0, RS_CHUNKS, final_sum, 0)
        for cp in to_sib + to_chip:
            cp.wait_send()

    outs = pl.pallas_call(
        body,
        name=name,
        in_specs=[ANY],
        out_specs=[ANY] * 4,
        out_shape=[_sds((rows, lanes), F32), _sds((4, rows, lanes), F32), _sds((3, rows, lanes), BF16),
                   _sds((3, rows, lanes), BF16)],
        scratch_shapes=[pltpu.VMEM((ch, lanes), F32), pltpu.VMEM((ch, lanes), F32), pltpu.VMEM((ch, lanes), BF16),
                        pltpu.VMEM((ch, lanes), BF16)]
        + [pltpu.SemaphoreType.DMA((4,))] * 2 + [pltpu.SemaphoreType.DMA((3,))] * 2,
    )(parts)
    return outs[0]


def _adamw_math(w, g, m, v):
    m = ADAM_B1 * m + (1.0 - ADAM_B1) * g
    v = ADAM_B2 * v + (1.0 - ADAM_B2) * (g * g)
    m_hat = m / (1.0 - ADAM_B1 ** ADAM_STEP)
    v_hat = v / (1.0 - ADAM_B2 ** ADAM_STEP)
    delta = -ADAM_LR * (m_hat / (jnp.sqrt(v_hat) + ADAM_EPS) + ADAM_WD * w)
    return delta, m, v


def _adamw(w, g, m, v, name):
    r, n = w.shape
    tm = max(t for t in range(8, min(r, 256) + 1, 8) if r % t == 0)

    def body(w_ref, g_ref, m_ref, v_ref, d_ref, nm_ref, nv_ref):
        d_ref[...], nm_ref[...], nv_ref[...] = _adamw_math(w_ref[...], g_ref[...], m_ref[...], v_ref[...])

    return _row_call(body, name, r, tm, [w, g, m, v], [], [((r, n), F32)] * 3)


def _adamw_small(gathered, w, m, v):
    def body(ga_ref, w_ref, m_ref, v_ref, g_ref, d_ref, nm_ref, nv_ref):
        g = ga_ref[0]
        for dev in range(1, N_DEV):
            g = g + ga_ref[dev]
        g_ref[...] = g
        d_ref[...], nm_ref[...], nv_ref[...] = _adamw_math(w_ref[...], g, m_ref[...], v_ref[...])

    return pl.pallas_call(body, name="adamw_small", out_shape=[_sds(w.shape, F32)] * 4, compiler_params=_params())(
        gathered, w, m, v)


BIG = (("w_in", (1024, 408), True), ("w_q_b", (384, 192), True), ("w_kv_b", (256, 256), True),
       ("w_o_mla", (128, 1024), False), ("w_glu", (64, 512), False), ("w_o_ssm", (512, 128), True),
       ("w_out", (128, 1024), False), ("w_up", (1024, 512), True), ("w_down", (512, 1024), False))
SMALL = ("norm_mix", "q_a_norm", "kv_a_norm", "q_norm", "k_norm", "ssm_a_re", "ssm_a_im", "ssm_log_dt", "ssm_b_re",
         "ssm_b_im", "ssm_c_re", "ssm_c_im", "ssm_d", "b_glu", "norm_mlp")
WEIGHT_ORDER = ("norm_mix", "w_in", "q_a_norm", "kv_a_norm", "w_q_b", "w_kv_b", "q_norm", "k_norm", "w_o_mla",
                "ssm_a_re", "ssm_a_im", "ssm_log_dt", "ssm_b_re", "ssm_b_im", "ssm_c_re", "ssm_c_im", "ssm_d", "w_glu",
                "b_glu", "w_o_ssm", "w_out", "norm_mlp", "w_up", "w_down")
LANES = 128


def _flat_rows(a):
    return a.reshape(-1, LANES)


def _pack_small(vals):
    flat = jnp.concatenate([vals[n].reshape(-1) for n in SMALL])
    rows = -(-flat.shape[0] // (8 * LANES)) * 8
    return jnp.pad(flat, (0, rows * LANES - flat.shape[0])).reshape(rows, LANES)


def _unpack_small(packed, like):
    flat, out, off = packed.reshape(-1), {}, 0
    for n in SMALL:
        size = like[n].size
        out[n] = flat[off:off + size].reshape(like[n].shape)
        off += size
    return out


def kernel(x, positions, norm_mix, w_in, q_a_norm, kv_a_norm, w_q_b, w_kv_b, q_norm, k_norm, w_o_mla, ssm_a_re, ssm_a_im, ssm_log_dt, ssm_b_re, ssm_b_im, ssm_c_re, ssm_c_im, ssm_d, w_glu, b_glu, w_o_ssm, w_out, norm_mlp, w_up, w_down, loss_target, m_norm_mix, m_w_in, m_q_a_norm, m_kv_a_norm, m_w_q_b, m_w_kv_b, m_q_norm, m_k_norm, m_w_o_mla, m_ssm_a_re, m_ssm_a_im, m_ssm_log_dt, m_ssm_b_re, m_ssm_b_im, m_ssm_c_re, m_ssm_c_im, m_ssm_d, m_w_glu, m_b_glu, m_w_o_ssm, m_w_out, m_norm_mlp, m_w_up, m_w_down, v_norm_mix, v_w_in, v_q_a_norm, v_kv_a_norm, v_w_q_b, v_w_kv_b, v_q_norm, v_k_norm, v_w_o_mla, v_ssm_a_re, v_ssm_a_im, v_ssm_log_dt, v_ssm_b_re, v_ssm_b_im, v_ssm_c_re, v_ssm_c_im, v_ssm_d, v_w_glu, v_b_glu, v_w_o_ssm, v_w_out, v_norm_mlp, v_w_up, v_w_down):
    given = dict(locals())
    w = {n: given[n] for n in WEIGHT_ORDER}
    m = {n: given["m_" + n] for n in WEIGHT_ORDER}
    v = {n: given["v_" + n] for n in WEIGHT_ORDER}

    mine = jnp.concatenate([_flat_rows(w[n][0]) for n, _, _ in BIG], axis=0).astype(BF16)
    gathered = _all_gather(mine, "gather_weights")
    wts, off = {}, 0
    for n, (r, c), by_col in BIG:
        rows = r * c // LANES
        shards = gathered[:, off:off + rows, :].reshape(N_DEV, r, c)
        wts[n] = jnp.transpose(shards, (1, 0, 2)).reshape(r, N_DEV * c) if by_col else shards.reshape(N_DEV * r, c)
        off += rows

    small = {n: w[n] for n in SMALL}
    loss, dx, g_big, g_small = _local_step(x[0], positions.reshape(-1, 1), loss_target[0], wts, small)

    parts = []
    for n, (r, c), by_col in BIG:
        g = g_big[n]
        g = jnp.transpose(g.reshape(r, N_DEV, c), (1, 0, 2)) if by_col else g.reshape(N_DEV, r, c)
        parts.append(g.reshape(N_DEV, r * c // LANES, LANES))
    g_mine = _reduce_scatter(jnp.concatenate(parts, axis=1), "reduce_grads")

    grads, deltas, new_m, new_v, off = {}, {}, {}, {}, 0
    for n, (r, c), _ in BIG:
        rows = r * c // LANES
        g = g_mine[off:off + rows].reshape(r, c)
        off += rows
        d, nm, nv = _adamw(w[n][0], g, m[n][0], v[n][0], "adamw_" + n)
        grads[n], deltas[n], new_m[n], new_v[n] = g[None], d[None], nm[None], nv[None]

    g_all = _all_gather(_pack_small(g_small), "gather_small_grads")
    packed = _adamw_small(g_all, _pack_small(small), _pack_small({n: m[n] for n in SMALL}),
                          _pack_small({n: v[n] for n in SMALL}))
    for dst, src in zip((grads, deltas, new_m, new_v), packed):
        dst.update(_unpack_small(src, small))

    total = lax.psum(loss[0, 0], ("x", "y", "c"))
    return (total, dx[None], *[grads[n] for n in WEIGHT_ORDER], *[deltas[n] for n in WEIGHT_ORDER],
            *[new_m[n] for n in WEIGHT_ORDER], *[new_v[n] for n in WEIGHT_ORDER])
```

```python
import functools
import math

import numpy as np
import jax
import jax.numpy as jnp
from jax import lax
from jax.experimental import pallas as pl
from jax.experimental.pallas import tpu as pltpu

F32 = jnp.float32
BF16 = jnp.bfloat16

D_MODEL = 1024
SSM_GROUPS = 32
SSM_GROUP_CH = 16
SSM_WIDTH = 512
SSM_STATE = 64
N_STATE = SSM_GROUPS * SSM_STATE
N_HEADS = 8
QK_NOPE = 128
QK_ROPE = 64
QK_HEAD = 192
QK_PAD = 256
V_HEAD = 128
Q_LORA = 384
KV_LORA = 256
KV_LAT_PAD = 384
ROPE_THETA = 10000.0
D_FF = 4096
EPS = 1e-6
ATT_SCALE = QK_HEAD ** -0.5
N_DEV = 8

IN_SEGS = ((0, 512), (512, 896), (896, 1280), (1280, 2304), (2304, 3328))
D_IN = 3264
D_IN_PAD = 3328
KV_END = 1216

ADAM_LR = 0.001
ADAM_B1 = 0.9
ADAM_B2 = 0.999
ADAM_EPS = 1e-08
ADAM_WD = 0.01
ADAM_STEP = 10

VMEM_LIMIT = 56 * 1024 * 1024
MESH = pl.DeviceIdType.MESH

SCAN_T = 256
SCAN_CB = 256
ATT_T = 512
ROW_T = 256


def _params(sem=None):
    return pltpu.CompilerParams(dimension_semantics=sem, vmem_limit_bytes=VMEM_LIMIT)


def _rows(arr, tm):
    if arr.ndim == 2:
        return pl.BlockSpec((tm, arr.shape[1]), lambda i: (i, 0))
    return pl.BlockSpec((arr.shape[0], tm, arr.shape[2]), lambda i: (0, i, 0))


def _const(arr):
    nd = arr.ndim
    return pl.BlockSpec(arr.shape, lambda i: (0,) * nd)


def _sds(shape, dtype):
    return jax.ShapeDtypeStruct(shape, dtype)


def _row_call(body, name, n_rows, tm, row_ins, const_ins, row_outs, acc_outs=()):
    outs = [_sds(s, d) for s, d in row_outs] + [_sds(s, d) for s, d in acc_outs]
    out_specs = [_rows(o, tm) for o in outs[: len(row_outs)]] + [_const(o) for o in outs[len(row_outs):]]
    return pl.pallas_call(
        body,
        name=name,
        grid=(n_rows // tm,),
        in_specs=[_rows(a, tm) for a in row_ins] + [_const(a) for a in const_ins],
        out_specs=out_specs,
        out_shape=outs,
        compiler_params=_params(("arbitrary",)),
    )(*row_ins, *const_ins)


def _dot(a, b):
    return jnp.dot(a, b, preferred_element_type=F32)


def _dot_nt(a, b):
    return lax.dot_general(a, b, (((1,), (1,)), ((), ())), preferred_element_type=F32)


def _dot_tn(a, b):
    return lax.dot_general(a, b, (((0,), (0,)), ((), ())), preferred_element_type=F32)


def _rms(x, g, n):
    inv = lax.rsqrt(jnp.sum(x * x, -1, keepdims=True) * (1.0 / n) + EPS)
    return x * inv * g, inv


def _rms_bwd(dy, x, g, inv, n):
    xh = x * inv
    dxh = dy * g
    dx = inv * (dxh - xh * (jnp.sum(dxh * xh, -1, keepdims=True) * (1.0 / n)))
    return dx, dy * xh


def _sigmoid(x):
    return 1.0 / (1.0 + jnp.exp(-x))


_GELU_C = math.sqrt(2.0 / math.pi)


def _gelu(y):
    th = jnp.tanh(_GELU_C * (y + 0.044715 * (y * y * y)))
    return 0.5 * y * (1.0 + th), th


def _gelu_grad(y, th):
    return 0.5 * (1.0 + th) + 0.5 * y * (1.0 - th * th) * (_GELU_C * (1.0 + 3.0 * 0.044715 * (y * y)))


def _acc(ref, val):
    @pl.when(pl.program_id(0) == 0)
    def _():
        ref[...] = jnp.zeros_like(ref)

    ref[...] += val


def _tile(n, limit):
    if n <= limit:
        return n
    return max(t for t in range(128, limit + 1, 128) if n % t == 0)


def _matmul_tn(a, b, name, tm=512, tk=512):
    k_dim, m = a.shape
    n = b.shape[1]
    tm, tk = _tile(m, tm), _tile(k_dim, tk)

    def body(a_ref, b_ref, o_ref):
        @pl.when(pl.program_id(1) == 0)
        def _():
            o_ref[...] = jnp.zeros_like(o_ref)

        o_ref[...] += _dot_tn(a_ref[...].astype(BF16), b_ref[...].astype(BF16))

    return pl.pallas_call(
        body,
        name=name,
        grid=(m // tm, k_dim // tk),
        in_specs=[pl.BlockSpec((tk, tm), lambda i, k: (k, i)), pl.BlockSpec((tk, n), lambda i, k: (k, 0))],
        out_specs=pl.BlockSpec((tm, n), lambda i, k: (i, 0)),
        out_shape=_sds((m, n), F32),
        compiler_params=_params(("parallel", "arbitrary")),
    )(a, b)


def _rope_tables(pos_col):
    n = pos_col.shape[0]
    half = QK_ROPE // 2
    inv_freq = (ROPE_THETA ** (-np.arange(half, dtype=np.float32) / half)).astype(np.float32)
    freq_row = jnp.asarray(np.concatenate([inv_freq, inv_freq, np.zeros(64, np.float32)])[None, :])

    def body(p_ref, f_ref, c_ref, s_ref):
        ang = p_ref[...].astype(F32) * f_ref[...]
        c_ref[...] = jnp.cos(ang)
        s_ref[...] = jnp.sin(ang)

    return _row_call(body, "rope_tables", n, min(n, 1024), [pos_col], [freq_row], [((n, 128), F32)] * 2)


def _rope_rot(v):
    lane = lax.broadcasted_iota(jnp.int32, v.shape, 1)
    return jnp.where(lane < 32, -pltpu.roll(v, 96, 1), jnp.where(lane < 64, pltpu.roll(v, 32, 1), 0.0))


def _rope_rot_t(v):
    lane = lax.broadcasted_iota(jnp.int32, v.shape, 1)
    return jnp.where(lane < 32, pltpu.roll(v, 96, 1), jnp.where(lane < 64, -pltpu.roll(v, 32, 1), 0.0))


def _in_proj(x, norm_mix, w_in_pad):
    n = x.shape[0]

    def body(x_ref, g_ref, w_ref, xn_ref, u_ref, ql_ref, kvl_ref, gs_ref, gm_ref):
        xn, _ = _rms(x_ref[...], g_ref[...], D_MODEL)
        xb = xn.astype(BF16)
        xn_ref[...] = xb
        for ref, (a, b) in zip((u_ref, ql_ref, kvl_ref, gs_ref, gm_ref), IN_SEGS):
            ref[...] = _dot(xb, w_ref[:, a:b])

    outs = [((n, D_MODEL), BF16)] + [((n, b - a), F32) for a, b in IN_SEGS]
    return _row_call(body, "in_proj", n, ROW_T, [x], [norm_mix, w_in_pad], outs)


def _ssm_prep_fn(a_re, a_im, log_dt, b_re_x, b_im_x):
    dt = jnp.exp(log_dt)
    mag = jnp.exp(a_re * dt)
    lr = mag * jnp.cos(a_im * dt)
    li = mag * jnp.sin(a_im * dt)
    den = a_re * a_re + a_im * a_im
    fr = ((lr - 1.0) * a_re + li * a_im) / den
    fi = (li * a_re - (lr - 1.0) * a_im) / den
    return lr, li, fr * b_re_x - fi * b_im_x, fr * b_im_x + fi * b_re_x


def _ssm_prep(a_re, a_im, log_dt, b_re_x, b_im_x):
    def body(ar, ai, ld, br, bi, lam_ref, bblk_ref):
        lr, li, bbr, bbi = _ssm_prep_fn(ar[...], ai[...], ld[...], br[...], bi[...])
        lam_ref[0:1, :] = lr
        lam_ref[1:2, :] = li
        bblk_ref[:, 0:N_STATE] = bbr.astype(BF16)
        bblk_ref[:, N_STATE:] = bbi.astype(BF16)

    return pl.pallas_call(
        body,
        name="ssm_prep",
        out_shape=[_sds((2, N_STATE), F32), _sds((SSM_WIDTH, 2 * N_STATE), BF16)],
        compiler_params=_params(),
    )(a_re, a_im, log_dt, b_re_x, b_im_x)


def _ssm_prep_bwd(a_re, a_im, log_dt, b_re_x, b_im_x, dlam, dbblk):
    def body(ar, ai, ld, br, bi, dl, db, dar, dai, dld, dbr, dbi):
        _, vjp = jax.vjp(_ssm_prep_fn, ar[...], ai[...], ld[...], br[...], bi[...])
        g = vjp((dl[0:1, :], dl[1:2, :], db[:, 0:N_STATE], db[:, N_STATE:]))
        dar[...] = g[0]
        dai[...] = g[1]
        grp = lax.broadcasted_iota(jnp.int32, (SSM_GROUPS, N_STATE), 0)
        lane = lax.broadcasted_iota(jnp.int32, (SSM_GROUPS, N_STATE), 1)
        sel = (lane // SSM_STATE) == grp
        dld[...] = jnp.sum(jnp.where(sel, jnp.broadcast_to(g[2], (SSM_GROUPS, N_STATE)), 0.0), axis=1, keepdims=True)
        dbr[...] = g[3]
        dbi[...] = g[4]

    return pl.pallas_call(
        body,
        name="ssm_prep_bwd",
        out_shape=[_sds((1, N_STATE), F32), _sds((1, N_STATE), F32), _sds((SSM_GROUPS, 1), F32),
                   _sds((SSM_WIDTH, N_STATE), F32), _sds((SSM_WIDTH, N_STATE), F32)],
        compiler_params=_params(),
    )(a_re, a_im, log_dt, b_re_x, b_im_x, dlam, dbblk)


def _scan_powers(lam_ref, pw_ref, n_lev, conj):
    pr = lam_ref[0:1, :]
    pi = lam_ref[1:2, :]
    if conj:
        pi = -pi
    for lev in range(n_lev):
        pw_ref[2 * lev:2 * lev + 1, :] = pr
        pw_ref[2 * lev + 1:2 * lev + 2, :] = pi
        pr, pi = pr * pr - pi * pi, 2.0 * pr * pi


def _scan(buf_a, buf_b, pw_ref, t, reverse):
    src, dst = buf_a, buf_b
    n_lev = int(math.log2(t))
    for lev in range(n_lev):
        d = 1 << lev
        keep = pl.ds(t - d, d) if reverse else pl.ds(0, d)
        upd = pl.ds(0, t - d) if reverse else pl.ds(d, t - d)
        frm = pl.ds(d, t - d) if reverse else pl.ds(0, t - d)

        def col_block(cb, carry, src=src, dst=dst, lev=lev, keep=keep, upd=upd, frm=frm):
            c0 = pl.multiple_of(cb * SCAN_CB, SCAN_CB)
            re = pl.ds(c0, SCAN_CB)
            im = pl.ds(pl.multiple_of(N_STATE + cb * SCAN_CB, SCAN_CB), SCAN_CB)
            pr = pw_ref[pl.ds(2 * lev, 1), re]
            pi = pw_ref[pl.ds(2 * lev + 1, 1), re]
            sr = src[frm, re]
            si = src[frm, im]
            dst[upd, re] = src[upd, re] + pr * sr - pi * si
            dst[upd, im] = src[upd, im] + pr * si + pi * sr
            dst[keep, re] = src[keep, re]
            dst[keep, im] = src[keep, im]
            return carry

        lax.fori_loop(0, N_STATE // SCAN_CB, col_block, 0)
        src, dst = dst, src
    return src


def _ssm_fwd(u, bblk, cblk, lam, d_row, w_glu, b_glu, w_o_ssm):
    n = u.shape[0]
    t = min(SCAN_T, n)
    n_lev = int(math.log2(t))
    kb = 512

    def body(u_ref, bblk_ref, cblk_ref, lam_ref, d_ref, wg_ref, bg_ref, wo_ref, y_ref, ys_ref, st_ref,
             buf_a, buf_b, pw_ref, carry_ref):
        @pl.when(pl.program_id(0) == 0)
        def _():
            carry_ref[...] = jnp.zeros_like(carry_ref)

        _scan_powers(lam_ref, pw_ref, n_lev, False)
        st_ref[0] = carry_ref[...]
        u_t = u_ref[...]
        ub = u_t.astype(BF16)
        for c in range(0, 2 * N_STATE, kb):
            buf_a[:, c:c + kb] = _dot(ub, bblk_ref[:, c:c + kb])
        lr, li = lam_ref[0:1, :], lam_ref[1:2, :]
        cr, ci = carry_ref[0:1, 0:N_STATE], carry_ref[0:1, N_STATE:]
        buf_a[0:1, 0:N_STATE] += lr * cr - li * ci
        buf_a[0:1, N_STATE:] += lr * ci + li * cr
        res = _scan(buf_a, buf_b, pw_ref, t, False)
        carry_ref[...] = jnp.broadcast_to(res[t - 1:t, :], carry_ref.shape)
        y = d_ref[...] * u_t
        for c in range(0, 2 * N_STATE, kb):
            y += _dot(res[:, c:c + kb].astype(BF16), cblk_ref[c:c + kb, :])
        y_ref[...] = y
        z, _ = _gelu(y)
        s = _sigmoid(_dot(z.astype(BF16), wg_ref[...]) + bg_ref[...])
        ys_ref[...] = _dot((z * s).astype(BF16), wo_ref[...])

    consts = [bblk, cblk, lam, d_row, w_glu, b_glu, w_o_ssm]
    return pl.pallas_call(
        body,
        name="ssm_fwd",
        grid=(n // t,),
        in_specs=[_rows(u, t)] + [_const(a) for a in consts],
        out_specs=[pl.BlockSpec((t, SSM_WIDTH), lambda i: (i, 0)), pl.BlockSpec((t, D_MODEL), lambda i: (i, 0)),
                   pl.BlockSpec((1, 8, 2 * N_STATE), lambda i: (i, 0, 0))],
        out_shape=[_sds((n, SSM_WIDTH), F32), _sds((n, D_MODEL), F32), _sds((n // t, 8, 2 * N_STATE), F32)],
        scratch_shapes=[pltpu.VMEM((t, 2 * N_STATE), F32), pltpu.VMEM((t, 2 * N_STATE), F32),
                        pltpu.VMEM((2 * n_lev, N_STATE), F32), pltpu.VMEM((8, 2 * N_STATE), F32)],
        compiler_params=_params(("arbitrary",)),
    )(u, *consts)


def _head_norm_rope(slab, gain, cos_t, sin_t):
    xn, inv = _rms(slab, gain, QK_HEAD)
    lo, hi = xn[:, 0:128], xn[:, 128:256]
    return jnp.concatenate([lo, hi * cos_t + _rope_rot(hi) * sin_t], axis=-1), inv


def _head_norm_rope_bwd(g, slab, gain, inv, cos_t, sin_t):
    g_lo, g_hi = g[:, 0:128], g[:, 128:256]
    g_n = jnp.concatenate([g_lo, g_hi * cos_t + _rope_rot_t(g_hi * sin_t)], axis=-1)
    return _rms_bwd(g_n, slab, gain, inv, QK_HEAD)


def _qkv_prep(ql, kvl, q_a_norm, kv_a_norm, wq, wkv, gq, gk, cos_t, sin_t):
    n = ql.shape[0]

    def body(ql_ref, kvl_ref, cos_ref, sin_ref, qa_ref, ka_ref, wq_ref, wkv_ref, gq_ref, gk_ref, q_ref, k_ref, v_ref):
        cos_t, sin_t = cos_ref[...], sin_ref[...]
        qa, _ = _rms(ql_ref[...], qa_ref[...], Q_LORA)
        q_pre = _dot(qa.astype(BF16), wq_ref[...])
        kvl_t = kvl_ref[...]
        ca, _ = _rms(kvl_t[:, 0:KV_LORA], ka_ref[...], KV_LORA)
        kv_pre = _dot(ca.astype(BF16), wkv_ref[...])
        kpe = kvl_t[:, KV_LORA:KV_LAT_PAD]
        for h in range(N_HEADS):
            qh, _ = _head_norm_rope(q_pre[:, h * QK_PAD:(h + 1) * QK_PAD], gq_ref[...], cos_t, sin_t)
            q_ref[h] = (qh * ATT_SCALE).astype(BF16)
            k_slab = jnp.concatenate([kv_pre[:, h * QK_NOPE:(h + 1) * QK_NOPE], kpe], axis=-1)
            kh, _ = _head_norm_rope(k_slab, gk_ref[...], cos_t, sin_t)
            k_ref[h] = kh.astype(BF16)
            v_ref[h] = kv_pre[:, N_HEADS * QK_NOPE + h * V_HEAD:N_HEADS * QK_NOPE + (h + 1) * V_HEAD].astype(BF16)

    outs = [((N_HEADS, n, QK_PAD), BF16), ((N_HEADS, n, QK_PAD), BF16), ((N_HEADS, n, V_HEAD), BF16)]
    return _row_call(body, "qkv_prep", n, ROW_T, [ql, kvl, cos_t, sin_t], [q_a_norm, kv_a_norm, wq, wkv, gq, gk], outs)


def _causal_mask(s, t):
    row = lax.broadcasted_iota(jnp.int32, (t, t), 0)
    col = lax.broadcasted_iota(jnp.int32, (t, t), 1)
    return jnp.where(col <= row, s, -jnp.inf)


def _attn_fwd(q, k, v):
    n = q.shape[1]
    t = min(ATT_T, n)

    def body(q_ref, k_ref, v_ref, o_ref, lse_ref):
        i = pl.program_id(1)
        qt = q_ref[0]

        def kv_tile(j, carry):
            m, l, acc = carry
            r0 = pl.multiple_of(j * t, t)
            s = _dot_nt(qt, k_ref[0, pl.ds(r0, t), :])
            m_new = jnp.maximum(m, jnp.max(s, -1, keepdims=True))
            alpha = jnp.exp(m - m_new)
            p = jnp.exp(s - m_new)
            l = alpha * l + jnp.sum(p, -1, keepdims=True)
            acc = alpha * acc + _dot(p.astype(BF16), v_ref[0, pl.ds(r0, t), :])
            return m_new, l, acc

        init = (jnp.full((t, 1), -jnp.inf, F32), jnp.zeros((t, 1), F32), jnp.zeros((t, V_HEAD), F32))
        m, l, acc = lax.fori_loop(0, i, kv_tile, init)
        r0 = pl.multiple_of(i * t, t)
        s = _causal_mask(_dot_nt(qt, k_ref[0, pl.ds(r0, t), :]), t)
        m_new = jnp.maximum(m, jnp.max(s, -1, keepdims=True))
        alpha = jnp.exp(m - m_new)
        p = jnp.exp(s - m_new)
        l = alpha * l + jnp.sum(p, -1, keepdims=True)
        acc = alpha * acc + _dot(p.astype(BF16), v_ref[0, pl.ds(r0, t), :])
        o_ref[...] = acc / l
        lse_ref[0] = m_new + jnp.log(l)

    return pl.pallas_call(
        body,
        name="attn_fwd",
        grid=(N_HEADS, n // t),
        in_specs=[pl.BlockSpec((1, t, QK_PAD), lambda h, i: (h, i, 0)),
                  pl.BlockSpec((1, n, QK_PAD), lambda h, i: (h, 0, 0)),
                  pl.BlockSpec((1, n, V_HEAD), lambda h, i: (h, 0, 0))],
        out_specs=[pl.BlockSpec((t, V_HEAD), lambda h, i: (i, h)), pl.BlockSpec((1, t, 1), lambda h, i: (h, i, 0))],
        out_shape=[_sds((n, N_HEADS * V_HEAD), F32), _sds((N_HEADS, n, 1), F32)],
        compiler_params=_params(("parallel", "arbitrary")),
    )(q, k, v)


def _merge(attn, gs, gm, y_ssm, x, w_o_mla, w_out):
    n = x.shape[0]

    def body(at_ref, gs_ref, gm_ref, ys_ref, x_ref, wo_ref, wout_ref, h_ref, mx_ref, ym_ref):
        y_mla = _dot(at_ref[...].astype(BF16), wo_ref[...])
        ym_ref[...] = y_mla
        mixed = (_sigmoid(gs_ref[...]) * ys_ref[...] + _sigmoid(gm_ref[...]) * y_mla).astype(BF16)
        mx_ref[...] = mixed
        h_ref[...] = x_ref[...] + _dot(mixed, wout_ref[...])

    outs = [((n, D_MODEL), F32), ((n, D_MODEL), BF16), ((n, D_MODEL), F32)]
    return _row_call(body, "merge", n, ROW_T, [attn, gs, gm, y_ssm, x], [w_o_mla, w_out], outs)


def _mlp_fwd_loss(h, target, norm_mlp, w_up, w_down):
    n = h.shape[0]

    def body(h_ref, t_ref, g_ref, wu_ref, wd_ref, hn_ref, do_ref, loss_ref):
        h_t = h_ref[...]
        hn, _ = _rms(h_t, g_ref[...], D_MODEL)
        hb = hn.astype(BF16)
        hn_ref[...] = hb
        a = jnp.maximum(_dot(hb, wu_ref[...]), 0.0)
        out = h_t + _dot((a * a).astype(BF16), wd_ref[...])
        err = out - t_ref[...]
        do_ref[...] = err * (1.0 / D_MODEL)
        _acc(loss_ref, jnp.broadcast_to(jnp.sum(err * err) * (0.5 / D_MODEL), loss_ref.shape))

    outs = [((n, D_MODEL), BF16), ((n, D_MODEL), F32)]
    return _row_call(body, "mlp_fwd_loss", n, ROW_T, [h, target], [norm_mlp, w_up, w_down], outs, [((8, 128), F32)])


def _mlp_bwd(dout, hn, h, norm_mlp, w_up, w_down):
    n = h.shape[0]

    def body(do_ref, hn_ref, h_ref, g_ref, wu_ref, wd_ref, hid_ref, da_ref, dh_ref, dg_ref):
        dout_t = do_ref[...]
        a = jnp.maximum(_dot(hn_ref[...], wu_ref[...]), 0.0)
        hid_ref[...] = (a * a).astype(BF16)
        da = (_dot_nt(dout_t.astype(BF16), wd_ref[...]) * (2.0 * a)).astype(BF16)
        da_ref[...] = da
        dhn = _dot_nt(da, wu_ref[...])
        h_t = h_ref[...]
        inv = lax.rsqrt(jnp.sum(h_t * h_t, -1, keepdims=True) * (1.0 / D_MODEL) + EPS)
        dx, dg = _rms_bwd(dhn, h_t, g_ref[...], inv, D_MODEL)
        dh_ref[...] = dout_t + dx
        _acc(dg_ref, jnp.sum(dg, 0, keepdims=True))

    outs = [((n, D_FF), BF16), ((n, D_FF), BF16), ((n, D_MODEL), F32)]
    return _row_call(body, "mlp_bwd", n, ROW_T, [dout, hn, h], [norm_mlp, w_up, w_down], outs, [((1, D_MODEL), F32)])


def _merge_bwd(dh, gs, gm, y_ssm, y_mla, w_out, w_o_mla):
    n = dh.shape[0]

    def body(dh_ref, gs_ref, gm_ref, ys_ref, ym_ref, wout_ref, wo_ref, dgs_ref, dgm_ref, dys_ref, dym_ref, dat_ref):
        dmix = _dot_nt(dh_ref[...].astype(BF16), wout_ref[...])
        sgs, sgm = _sigmoid(gs_ref[...]), _sigmoid(gm_ref[...])
        dgs_ref[...] = (dmix * ys_ref[...] * sgs * (1.0 - sgs)).astype(BF16)
        dgm_ref[...] = (dmix * ym_ref[...] * sgm * (1.0 - sgm)).astype(BF16)
        dys_ref[...] = (dmix * sgs).astype(BF16)
        dym = (dmix * sgm).astype(BF16)
        dym_ref[...] = dym
        dat_ref[...] = _dot_nt(dym, wo_ref[...])

    outs = [((n, D_MODEL), BF16)] * 4 + [((n, D_MODEL), F32)]
    return _row_call(body, "merge_bwd", n, ROW_T, [dh, gs, gm, y_ssm, y_mla], [w_out, w_o_mla], outs)


def _attn_bwd(q, k, v, out, lse, dout):
    n = q.shape[1]
    t = min(ATT_T, n)
    nt = n // t

    def body(q_ref, k_ref, v_ref, o_ref, lse_ref, do_ref, dq_ref, dk_ref, dv_ref, delta_ref):
        j = pl.program_id(1)

        @pl.when(j == 0)
        def _():
            dq_ref[...] = jnp.zeros_like(dq_ref)
            delta_ref[...] = jnp.sum(do_ref[...] * o_ref[...], -1, keepdims=True)

        kt = k_ref[0]
        vt = v_ref[0]

        def q_tile(i, carry, diag):
            dk, dv = carry
            r0 = pl.multiple_of(i * t, t)
            rows = pl.ds(r0, t)
            qt = q_ref[0, rows, :]
            s = _dot_nt(qt, kt)
            if diag:
                s = _causal_mask(s, t)
            p = jnp.exp(s - lse_ref[0, rows, :])
            dot = do_ref[rows, :].astype(BF16)
            dv = dv + _dot_tn(p.astype(BF16), dot)
            ds = (p * (_dot_nt(dot, vt) - delta_ref[rows, :])).astype(BF16)
            dk = dk + _dot_tn(ds, qt)
            dq_ref[0, rows, :] += _dot(ds, kt)
            return dk, dv

        carry = q_tile(j, (jnp.zeros((t, QK_PAD), F32), jnp.zeros((t, V_HEAD), F32)), True)
        dk, dv = lax.fori_loop(j + 1, nt, functools.partial(q_tile, diag=False), carry)
        dk_ref[0] = dk
        dv_ref[0] = dv

    return pl.pallas_call(
        body,
        name="attn_bwd",
        grid=(N_HEADS, nt),
        in_specs=[pl.BlockSpec((1, n, QK_PAD), lambda h, j: (h, 0, 0)),
                  pl.BlockSpec((1, t, QK_PAD), lambda h, j: (h, j, 0)),
                  pl.BlockSpec((1, t, V_HEAD), lambda h, j: (h, j, 0)),
                  pl.BlockSpec((n, V_HEAD), lambda h, j: (0, h)),
                  pl.BlockSpec((1, n, 1), lambda h, j: (h, 0, 0)),
                  pl.BlockSpec((n, V_HEAD), lambda h, j: (0, h))],
        out_specs=[pl.BlockSpec((1, n, QK_PAD), lambda h, j: (h, 0, 0)),
                   pl.BlockSpec((1, t, QK_PAD), lambda h, j: (h, j, 0)),
                   pl.BlockSpec((1, t, V_HEAD), lambda h, j: (h, j, 0))],
        out_shape=[_sds((N_HEADS, n, QK_PAD), F32), _sds((N_HEADS, n, QK_PAD), F32), _sds((N_HEADS, n, V_HEAD), F32)],
        scratch_shapes=[pltpu.VMEM((n, 1), F32)],
        compiler_params=_params(("parallel", "arbitrary")),
    )(q, k, v, out, lse, dout)


def _qkv_prep_bwd(ql, kvl, dq, dk, dv, q_a_norm, kv_a_norm, wq, wkv, gq, gk, cos_t, sin_t):
    n = ql.shape[0]

    def body(ql_ref, kvl_ref, cos_ref, sin_ref, dq_ref, dk_ref, dv_ref, qa_ref, ka_ref, wq_ref, wkv_ref, gq_ref, gk_ref,
             dql_ref, dkvl_ref, qab_ref, dqp_ref, cab_ref, dkvp_ref, dqa_ref, dka_ref, dgq_ref, dgk_ref):
        cos_t, sin_t = cos_ref[...], sin_ref[...]
        ql_t = ql_ref[...]
        qa, inv_qa = _rms(ql_t, qa_ref[...], Q_LORA)
        qab = qa.astype(BF16)
        qab_ref[...] = qab
        q_pre = _dot(qab, wq_ref[...])
        kvl_t = kvl_ref[...]
        ckv = kvl_t[:, 0:KV_LORA]
        ca, inv_ca = _rms(ckv, ka_ref[...], KV_LORA)
        cab = ca.astype(BF16)
        cab_ref[...] = cab
        kv_pre = _dot(cab, wkv_ref[...])
        kpe = kvl_t[:, KV_LORA:KV_LAT_PAD]
        dgq = jnp.zeros((1, QK_PAD), F32)
        dgk = jnp.zeros((1, QK_PAD), F32)
        dkpe = jnp.zeros_like(kpe)
        for h in range(N_HEADS):
            q_slab = q_pre[:, h * QK_PAD:(h + 1) * QK_PAD]
            inv = lax.rsqrt(jnp.sum(q_slab * q_slab, -1, keepdims=True) * (1.0 / QK_HEAD) + EPS)
            d_slab, dg = _head_norm_rope_bwd(dq_ref[h] * ATT_SCALE, q_slab, gq_ref[...], inv, cos_t, sin_t)
            dqp_ref[:, h * QK_PAD:(h + 1) * QK_PAD] = d_slab.astype(BF16)
            dgq += jnp.sum(dg, 0, keepdims=True)
            k_slab = jnp.concatenate([kv_pre[:, h * QK_NOPE:(h + 1) * QK_NOPE], kpe], axis=-1)
            inv = lax.rsqrt(jnp.sum(k_slab * k_slab, -1, keepdims=True) * (1.0 / QK_HEAD) + EPS)
            d_slab, dg = _head_norm_rope_bwd(dk_ref[h], k_slab, gk_ref[...], inv, cos_t, sin_t)
            dkvp_ref[:, h * QK_NOPE:(h + 1) * QK_NOPE] = d_slab[:, 0:QK_NOPE].astype(BF16)
            dkpe += d_slab[:, QK_NOPE:QK_PAD]
            dgk += jnp.sum(dg, 0, keepdims=True)
            dkvp_ref[:, N_HEADS * QK_NOPE + h * V_HEAD:N_HEADS * QK_NOPE + (h + 1) * V_HEAD] = dv_ref[h].astype(BF16)
        dqa = _dot_nt(dqp_ref[...], wq_ref[...])
        dx, dg = _rms_bwd(dqa, ql_t, qa_ref[...], inv_qa, Q_LORA)
        dql_ref[...] = dx.astype(BF16)
        _acc(dqa_ref, jnp.sum(dg, 0, keepdims=True))
        dca = _dot_nt(dkvp_ref[...], wkv_ref[...])
        dx, dg = _rms_bwd(dca, ckv, ka_ref[...], inv_ca, KV_LORA)
        dkvl_ref[:, 0:KV_LORA] = dx.astype(BF16)
        dkvl_ref[:, KV_LORA:KV_LAT_PAD] = dkpe.astype(BF16)
        _acc(dka_ref, jnp.sum(dg, 0, keepdims=True))
        _acc(dgq_ref, dgq)
        _acc(dgk_ref, dgk)

    row_outs = [((n, Q_LORA), BF16), ((n, KV_LAT_PAD), BF16), ((n, Q_LORA), BF16), ((n, N_HEADS * QK_PAD), BF16),
                ((n, KV_LORA), BF16), ((n, N_HEADS * (QK_NOPE + V_HEAD)), BF16)]
    acc_outs = [((1, Q_LORA), F32), ((1, KV_LORA), F32), ((1, QK_PAD), F32), ((1, QK_PAD), F32)]
    return _row_call(body, "qkv_prep_bwd", n, ROW_T, [ql, kvl, cos_t, sin_t, dq, dk, dv],
                     [q_a_norm, kv_a_norm, wq, wkv, gq, gk], row_outs, acc_outs)


def _glu_bwd(dy_ssm, y, w_glu, b_glu, w_o_ssm):
    n = y.shape[0]

    def body(dys_ref, y_ref, wg_ref, bg_ref, wo_ref, dy_ref, zg_ref, z_ref, dt_ref, db_ref):
        y_t = y_ref[...]
        z, th = _gelu(y_t)
        zb = z.astype(BF16)
        z_ref[...] = zb
        s = _sigmoid(_dot(zb, wg_ref[...]) + bg_ref[...])
        zg_ref[...] = (z * s).astype(BF16)
        dzg = _dot_nt(dys_ref[...], wo_ref[...])
        dt = dzg * z * s * (1.0 - s)
        dtb = dt.astype(BF16)
        dt_ref[...] = dtb
        dz = dzg * s + _dot_nt(dtb, wg_ref[...])
        dy_ref[...] = dz * _gelu_grad(y_t, th)
        _acc(db_ref, jnp.sum(dt, 0, keepdims=True))

    outs = [((n, SSM_WIDTH), F32)] + [((n, SSM_WIDTH), BF16)] * 3
    return _row_call(body, "glu_bwd", n, ROW_T, [dy_ssm, y], [w_glu, b_glu, w_o_ssm], outs, [((1, SSM_WIDTH), F32)])


def _ssm_bwd(u, dy, st, bblk, cblk, lam, d_row):
    n = u.shape[0]
    t = min(SCAN_T, n)
    nc = n // t
    n_lev = int(math.log2(t))
    kb = 512

    def body(u_ref, dy_ref, st_ref, bblk_ref, cblk_ref, lam_ref, d_ref, du_ref, xs_ref, as_ref, dlam_ref, dd_ref,
             buf_a, buf_b, buf_c, buf_d, pw_ref, pwc_ref, carry_ref):
        @pl.when(pl.program_id(0) == 0)
        def _():
            carry_ref[...] = jnp.zeros_like(carry_ref)

        _scan_powers(lam_ref, pw_ref, n_lev, False)
        _scan_powers(lam_ref, pwc_ref, n_lev, True)
        lr, li = lam_ref[0:1, :], lam_ref[1:2, :]
        u_t = u_ref[...]
        ub = u_t.astype(BF16)
        dy_t = dy_ref[...]
        dyb = dy_t.astype(BF16)
        for c in range(0, 2 * N_STATE, kb):
            buf_a[:, c:c + kb] = _dot(ub, bblk_ref[:, c:c + kb])
        s_re, s_im = st_ref[0, 0:1, 0:N_STATE], st_ref[0, 0:1, N_STATE:]
        buf_a[0:1, 0:N_STATE] += lr * s_re - li * s_im
        buf_a[0:1, N_STATE:] += lr * s_im + li * s_re
        xs = _scan(buf_a, buf_b, pw_ref, t, False)
        for c in range(0, 2 * N_STATE, kb):
            buf_c[:, c:c + kb] = _dot_nt(dyb, cblk_ref[c:c + kb, :])
        a_re, a_im = carry_ref[0:1, 0:N_STATE], carry_ref[0:1, N_STATE:]
        buf_c[t - 1:t, 0:N_STATE] += lr * a_re + li * a_im
        buf_c[t - 1:t, N_STATE:] += lr * a_im - li * a_re
        ad = _scan(buf_c, buf_d, pwc_ref, t, True)
        carry_ref[...] = jnp.broadcast_to(ad[0:1, :], carry_ref.shape)
        du = d_ref[...] * dy_t
        for c in range(0, 2 * N_STATE, kb):
            adb = ad[:, c:c + kb].astype(BF16)
            as_ref[:, c:c + kb] = adb
            xs_ref[:, c:c + kb] = xs[:, c:c + kb].astype(BF16)
            du += _dot_nt(adb, bblk_ref[:, c:c + kb])
        du_ref[...] = du.astype(BF16)
        for c in range(0, N_STATE, kb):
            re, im = pl.ds(c, kb), pl.ds(N_STATE + c, kb)
            xr, xi = xs[pl.ds(0, t - 1), re], xs[pl.ds(0, t - 1), im]
            ar, ai = ad[pl.ds(1, t - 1), re], ad[pl.ds(1, t - 1), im]
            x0r, x0i = st_ref[0, 0:1, re], st_ref[0, 0:1, im]
            a0r, a0i = ad[0:1, re], ad[0:1, im]
            dlam_part_re = jnp.sum(ar * xr + ai * xi, 0, keepdims=True) + a0r * x0r + a0i * x0i
            dlam_part_im = jnp.sum(ai * xr - ar * xi, 0, keepdims=True) + a0i * x0r - a0r * x0i

            @pl.when(pl.program_id(0) == 0)
            def _(c=c):
                dlam_ref[0:1, c:c + kb] = jnp.zeros((1, kb), F32)
                dlam_ref[1:2, c:c + kb] = jnp.zeros((1, kb), F32)

            dlam_ref[0:1, c:c + kb] += dlam_part_re
            dlam_ref[1:2, c:c + kb] += dlam_part_im
        _acc(dd_ref, jnp.sum(dy_t * u_t, 0, keepdims=True))

    rev = lambda i: (nc - 1 - i, 0)
    consts = [bblk, cblk, lam, d_row]
    return pl.pallas_call(
        body,
        name="ssm_bwd",
        grid=(nc,),
        in_specs=[pl.BlockSpec((t, SSM_WIDTH), rev), pl.BlockSpec((t, SSM_WIDTH), rev),
                  pl.BlockSpec((1, 8, 2 * N_STATE), lambda i: (nc - 1 - i, 0, 0))] + [_const(a) for a in consts],
        out_specs=[pl.BlockSpec((t, SSM_WIDTH), rev), pl.BlockSpec((t, 2 * N_STATE), rev),
                   pl.BlockSpec((t, 2 * N_STATE), rev), pl.BlockSpec((2, N_STATE), lambda i: (0, 0)),
                   pl.BlockSpec((1, SSM_WIDTH), lambda i: (0, 0))],
        out_shape=[_sds((n, SSM_WIDTH), BF16), _sds((n, 2 * N_STATE), BF16), _sds((n, 2 * N_STATE), BF16),
                   _sds((2, N_STATE), F32), _sds((1, SSM_WIDTH), F32)],
        scratch_shapes=[pltpu.VMEM((t, 2 * N_STATE), F32)] * 4
        + [pltpu.VMEM((2 * n_lev, N_STATE), F32)] * 2 + [pltpu.VMEM((8, 2 * N_STATE), F32)],
        compiler_params=_params(("arbitrary",)),
    )(u, dy, st, *consts)


def _in_proj_bwd(pieces, dh, x, norm_mix, w_in_pad):
    n = x.shape[0]

    def body(du_ref, dql_ref, dkvl_ref, dgs_ref, dgm_ref, dh_ref, x_ref, g_ref, w_ref, dx_ref, dp_ref, dg_ref):
        dxn = jnp.zeros((dh_ref.shape[0], D_MODEL), F32)
        for ref, (a, b) in zip((du_ref, dql_ref, dkvl_ref, dgs_ref, dgm_ref), IN_SEGS):
            piece = ref[...]
            dp_ref[:, a:b] = piece
            dxn += _dot_nt(piece, w_ref[:, a:b])
        x_t = x_ref[...]
        inv = lax.rsqrt(jnp.sum(x_t * x_t, -1, keepdims=True) * (1.0 / D_MODEL) + EPS)
        dx, dg = _rms_bwd(dxn, x_t, g_ref[...], inv, D_MODEL)
        dx_ref[...] = dh_ref[...] + dx
        _acc(dg_ref, jnp.sum(dg, 0, keepdims=True))

    outs = [((n, D_MODEL), F32), ((n, D_IN_PAD), BF16)]
    return _row_call(body, "in_proj_bwd", n, ROW_T, list(pieces) + [dh, x], [norm_mix, w_in_pad], outs,
                     [((1, D_MODEL), F32)])


def _block_diag(a, rows_per_group, cols_per_group):
    eye = jnp.eye(SSM_GROUPS, dtype=a.dtype)
    return (a[:, :, None, :] * eye[:, None, :, None]).reshape(SSM_GROUPS * rows_per_group, SSM_GROUPS * cols_per_group)


def _block_diag_extract(m, rows_per_group, cols_per_group):
    m4 = m.reshape(SSM_GROUPS, rows_per_group, SSM_GROUPS, cols_per_group)
    eye = jnp.eye(SSM_GROUPS, dtype=m.dtype)
    return jnp.sum(m4 * eye[:, None, :, None], axis=2)


def _pad_heads(w):
    r = w.shape[0]
    return jnp.pad(w.reshape(r, N_HEADS, QK_HEAD), ((0, 0), (0, 0), (0, QK_PAD - QK_HEAD))).reshape(r, N_HEADS * QK_PAD)


def _unpad_heads(w):
    r = w.shape[0]
    return w.reshape(r, N_HEADS, QK_PAD)[:, :, :QK_HEAD].reshape(r, N_HEADS * QK_HEAD)


def _split_kv(w):
    r = w.shape[0]
    w3 = w.reshape(r, N_HEADS, QK_NOPE + V_HEAD)
    return jnp.concatenate([w3[:, :, :QK_NOPE].reshape(r, -1), w3[:, :, QK_NOPE:].reshape(r, -1)], axis=1)


def _unsplit_kv(w):
    r = w.shape[0]
    k3 = w[:, :N_HEADS * QK_NOPE].reshape(r, N_HEADS, QK_NOPE)
    v3 = w[:, N_HEADS * QK_NOPE:].reshape(r, N_HEADS, V_HEAD)
    return jnp.concatenate([k3, v3], axis=2).reshape(r, N_HEADS * (QK_NOPE + V_HEAD))


def _pad_in(w):
    return jnp.concatenate([w[:, :KV_END], jnp.zeros((w.shape[0], D_IN_PAD - D_IN), w.dtype), w[:, KV_END:]], axis=1)


def _unpad_in(w):
    return jnp.concatenate([w[:, :KV_END], w[:, KV_END + D_IN_PAD - D_IN:]], axis=1)


def _pad_gain(g):
    return jnp.pad(g, ((0, 0), (0, QK_PAD - QK_HEAD)))


def _local_step(x, pos_col, target, wts, small):
    w_in_pad = _pad_in(wts["w_in"])
    wq = _pad_heads(wts["w_q_b"])
    wkv = _split_kv(wts["w_kv_b"])
    gq, gk = _pad_gain(small["q_norm"]), _pad_gain(small["k_norm"])

    a_re = small["ssm_a_re"].reshape(1, N_STATE)
    a_im = small["ssm_a_im"].reshape(1, N_STATE)
    log_dt = jnp.repeat(small["ssm_log_dt"].reshape(SSM_GROUPS), SSM_STATE).reshape(1, N_STATE)
    b_re_x = _block_diag(jnp.transpose(small["ssm_b_re"][0], (0, 2, 1)), SSM_GROUP_CH, SSM_STATE)
    b_im_x = _block_diag(jnp.transpose(small["ssm_b_im"][0], (0, 2, 1)), SSM_GROUP_CH, SSM_STATE)
    c_re_x = _block_diag(jnp.transpose(small["ssm_c_re"][0], (0, 2, 1)), SSM_STATE, SSM_GROUP_CH)
    c_im_x = _block_diag(jnp.transpose(small["ssm_c_im"][0], (0, 2, 1)), SSM_STATE, SSM_GROUP_CH)
    cblk = jnp.concatenate([c_re_x, -c_im_x], axis=0).astype(BF16)
    d_row = small["ssm_d"].reshape(1, SSM_WIDTH)

    cos_t, sin_t = _rope_tables(pos_col)
    xn, u, ql, kvl, gs, gm = _in_proj(x, small["norm_mix"], w_in_pad)
    lam, bblk = _ssm_prep(a_re, a_im, log_dt, b_re_x, b_im_x)
    y, y_ssm, st = _ssm_fwd(u, bblk, cblk, lam, d_row, wts["w_glu"], small["b_glu"], wts["w_o_ssm"])
    q, k, v = _qkv_prep(ql, kvl, small["q_a_norm"], small["kv_a_norm"], wq, wkv, gq, gk, cos_t, sin_t)
    attn, lse = _attn_fwd(q, k, v)
    h, mixed, y_mla = _merge(attn, gs, gm, y_ssm, x, wts["w_o_mla"], wts["w_out"])
    hn, dout, loss = _mlp_fwd_loss(h, target, small["norm_mlp"], wts["w_up"], wts["w_down"])

    hid, da, dh, d_norm_mlp = _mlp_bwd(dout, hn, h, small["norm_mlp"], wts["w_up"], wts["w_down"])
    g_w_down = _matmul_tn(hid, dout, "dw_down")
    g_w_up = _matmul_tn(hn, da, "dw_up")
    dgs, dgm, dy_ssm, dy_mla, dattn = _merge_bwd(dh, gs, gm, y_ssm, y_mla, wts["w_out"], wts["w_o_mla"])
    g_w_out = _matmul_tn(mixed, dh, "dw_out")
    g_w_o_mla = _matmul_tn(attn, dy_mla, "dw_o_mla")
    dq, dk, dv = _attn_bwd(q, k, v, attn, lse, dattn)
    dql, dkvl, qa, dq_pre, ca, dkv_pre, d_q_a_norm, d_kv_a_norm, d_gq, d_gk = _qkv_prep_bwd(
        ql, kvl, dq, dk, dv, small["q_a_norm"], small["kv_a_norm"], wq, wkv, gq, gk, cos_t, sin_t)
    g_wq = _matmul_tn(qa, dq_pre, "dw_q_b")
    g_wkv = _matmul_tn(ca, dkv_pre, "dw_kv_b")
    dy, zg, z, dt, d_b_glu = _glu_bwd(dy_ssm, y, wts["w_glu"], small["b_glu"], wts["w_o_ssm"])
    g_w_o_ssm = _matmul_tn(zg, dy_ssm, "dw_o_ssm")
    g_w_glu = _matmul_tn(z, dt, "dw_glu")
    du, xs, ads, dlam, d_d = _ssm_bwd(u, dy, st, bblk, cblk, lam, d_row)
    d_bblk = _matmul_tn(u, ads, "d_bblk")
    d_cblk = _matmul_tn(xs, dy, "d_cblk")
    d_a_re, d_a_im, d_log_dt, d_b_re_x, d_b_im_x = _ssm_prep_bwd(a_re, a_im, log_dt, b_re_x, b_im_x, dlam, d_bblk)
    dx, dproj, d_norm_mix = _in_proj_bwd((du, dql, dkvl, dgs, dgm), dh, x, small["norm_mix"], w_in_pad)
    g_w_in = _matmul_tn(xn, dproj, "dw_in")

    big = {
        "w_in": _unpad_in(g_w_in), "w_q_b": _unpad_heads(g_wq), "w_kv_b": _unsplit_kv(g_wkv), "w_o_mla": g_w_o_mla,
        "w_glu": g_w_glu, "w_o_ssm": g_w_o_ssm, "w_out": g_w_out, "w_up": g_w_up, "w_down": g_w_down,
    }
    tr = lambda m: jnp.transpose(m, (0, 2, 1))[None]
    sm = {
        "norm_mix": d_norm_mix, "q_a_norm": d_q_a_norm, "kv_a_norm": d_kv_a_norm,
        "q_norm": d_gq[:, :QK_HEAD], "k_norm": d_gk[:, :QK_HEAD],
        "ssm_a_re": d_a_re.reshape(1, SSM_GROUPS, SSM_STATE), "ssm_a_im": d_a_im.reshape(1, SSM_GROUPS, SSM_STATE),
        "ssm_log_dt": d_log_dt.reshape(1, SSM_GROUPS),
        "ssm_b_re": tr(_block_diag_extract(d_b_re_x, SSM_GROUP_CH, SSM_STATE)),
        "ssm_b_im": tr(_block_diag_extract(d_b_im_x, SSM_GROUP_CH, SSM_STATE)),
        "ssm_c_re": tr(_block_diag_extract(d_cblk[:N_STATE], SSM_STATE, SSM_GROUP_CH)),
        "ssm_c_im": tr(-_block_diag_extract(d_cblk[N_STATE:], SSM_STATE, SSM_GROUP_CH)),
        "ssm_d": d_d.reshape(1, SSM_GROUPS, SSM_GROUP_CH), "b_glu": d_b_glu, "norm_mlp": d_norm_mlp,
    }
    return loss, dx, big, sm


ANY = pl.BlockSpec(memory_space=pl.ANY)


def _place():
    x, y, c = lax.axis_index("x"), lax.axis_index("y"), lax.axis_index("c")
    chips = [(x, y), (1 - x, y), (x, 1 - y), (1 - x, 1 - y)]
    return x, y, c, chips


def _all_gather(block, name):
    rows, lanes = block.shape

    def body(x_ref, out_ref, send_sems, recv_sems, local_sem):
        x, y, c, chips = _place()
        me, sibling = (x, y, c), (x, y, 1 - c)

        def slot(px, py, pc):
            return out_ref.at[4 * px + 2 * py + pc]

        def copy(k, blk, to, src=None):
            return pltpu.make_async_remote_copy(
                src_ref=slot(*blk) if src is None else src, dst_ref=slot(*blk),
                send_sem=send_sems.at[k], recv_sem=recv_sems.at[k], device_id=to, device_id_type=MESH)

        mine = pltpu.make_async_copy(x_ref, slot(*me), local_sem)
        mine.start()
        first = [copy(0, me, sibling, src=x_ref)]
        first += [copy(1 + j, me, (*chip, c), src=x_ref) for j, chip in enumerate(chips[1:])]
        for cp in first:
            cp.start()
        passed = [copy(4 + j, (*chip, c), sibling) for j, chip in enumerate(chips[1:])]
        for j, chip in enumerate(chips[1:]):
            copy(1 + j, (*chip, c), me).wait_recv()
            passed[j].start()
        copy(0, sibling, me).wait_recv()
        for j, chip in enumerate(chips[1:]):
            copy(4 + j, (*chip, 1 - c), me).wait_recv()
        for cp in first + passed:
            cp.wait_send()
        mine.wait()

    return pl.pallas_call(
        body,
        name=name,
        in_specs=[ANY],
        out_specs=ANY,
        out_shape=_sds((N_DEV, rows, lanes), block.dtype),
        scratch_shapes=[pltpu.SemaphoreType.DMA((7,)), pltpu.SemaphoreType.DMA((7,)), pltpu.SemaphoreType.DMA],
    )(block)


RS_CHUNKS = 8


def _reduce_scatter(parts, name):
    _, rows, lanes = parts.shape
    ch = rows // RS_CHUNKS

    def body(p_ref, out_ref, land_a, send_b, land_b, va, vb, v16, w16, sa, ra, sb, rb):
        x, y, c, chips = _place()
        sibling = (x, y, 1 - c)

        def blk(chip, core):
            return p_ref.at[4 * chip[0] + 2 * chip[1] + core]

        to_sib = [pltpu.make_async_remote_copy(
            src_ref=blk(chips[k], 1 - c), dst_ref=land_a.at[k], send_sem=sa.at[k], recv_sem=ra.at[k],
            device_id=sibling, device_id_type=MESH) for k in range(4)]
        for cp in to_sib:
            cp.start()
        to_chip = [pltpu.make_async_remote_copy(
            src_ref=send_b.at[j], dst_ref=land_b.at[j], send_sem=sb.at[j], recv_sem=rb.at[j],
            device_id=(*chips[1 + j], c), device_id_type=MESH) for j in range(3)]

        for k in (1, 2, 3, 0):
            to_sib[k].wait_recv()

            def chip_sum(i, carry, k=k):
                r = pl.ds(pl.multiple_of(i * ch, 16), ch)
                pltpu.sync_copy(blk(chips[k], c).at[r], va)
                pltpu.sync_copy(land_a.at[k, r], vb)
                if k == 0:
                    va[...] = va[...] + vb[...]
                    pltpu.sync_copy(va, out_ref.at[r])
                else:
                    v16[...] = (va[...] + vb[...]).astype(BF16)
                    pltpu.sync_copy(v16, send_b.at[k - 1, r])
                return carry

            lax.fori_loop(0, RS_CHUNKS, chip_sum, 0)
            if k != 0:
                to_chip[k - 1].start()

        for cp in to_chip:
            cp.wait_recv()

        def final_sum(i, carry):
            r = pl.ds(pl.multiple_of(i * ch, 16), ch)
            pltpu.sync_copy(out_ref.at[r], va)
            acc = va[...]
            for j in range(3):
                pltpu.sync_copy(land_b.at[j, r], w16)
                acc = acc + w16[...].astype(F32)
            va[...] = acc
            pltpu.sync_copy(va, out_ref.at[r])
            return carry

        lax.fori_loop(0, RS_CHUNKS, final_sum, 0)
        for cp in to_sib + to_chip:
            cp.wait_send()

    outs = pl.pallas_call(
        body,
        name=name,
        in_specs=[ANY],
        out_specs=[ANY] * 4,
        out_shape=[_sds((rows, lanes), F32), _sds((4, rows, lanes), F32), _sds((3, rows, lanes), BF16),
                   _sds((3, rows, lanes), BF16)],
        scratch_shapes=[pltpu.VMEM((ch, lanes), F32), pltpu.VMEM((ch, lanes), F32), pltpu.VMEM((ch, lanes), BF16),
                        pltpu.VMEM((ch, lanes), BF16)]
        + [pltpu.SemaphoreType.DMA((4,))] * 2 + [pltpu.SemaphoreType.DMA((3,))] * 2,
    )(parts)
    return outs[0]


def _adamw_math(w, g, m, v):
    m = ADAM_B1 * m + (1.0 - ADAM_B1) * g
    v = ADAM_B2 * v + (1.0 - ADAM_B2) * (g * g)
    m_hat = m / (1.0 - ADAM_B1 ** ADAM_STEP)
    v_hat = v / (1.0 - ADAM_B2 ** ADAM_STEP)
    delta = -ADAM_LR * (m_hat / (jnp.sqrt(v_hat) + ADAM_EPS) + ADAM_WD * w)
    return delta, m, v


def _adamw(w, g, m, v, name):
    r, n = w.shape
    tm = max(t for t in range(8, min(r, 256) + 1, 8) if r % t == 0)

    def body(w_ref, g_ref, m_ref, v_ref, d_ref, nm_ref, nv_ref):
        d_ref[...], nm_ref[...], nv_ref[...] = _adamw_math(w_ref[...], g_ref[...], m_ref[...], v_ref[...])

    return _row_call(body, name, r, tm, [w, g, m, v], [], [((r, n), F32)] * 3)


def _adamw_small(gathered, w, m, v):
    def body(ga_ref, w_ref, m_ref, v_ref, g_ref, d_ref, nm_ref, nv_ref):
        g = ga_ref[0]
        for dev in range(1, N_DEV):
            g = g + ga_ref[dev]
        g_ref[...] = g
        d_ref[...], nm_ref[...], nv_ref[...] = _adamw_math(w_ref[...], g, m_ref[...], v_ref[...])

    return pl.pallas_call(body, name="adamw_small", out_shape=[_sds(w.shape, F32)] * 4, compiler_params=_params())(
        gathered, w, m, v)


BIG = (("w_in", (1024, 408), True), ("w_q_b", (384, 192), True), ("w_kv_b", (256, 256), True),
       ("w_o_mla", (128, 1024), False), ("w_glu", (64, 512), False), ("w_o_ssm", (512, 128), True),
       ("w_out", (128, 1024), False), ("w_up", (1024, 512), True), ("w_down", (512, 1024), False))
SMALL = ("norm_mix", "q_a_norm", "kv_a_norm", "q_norm", "k_norm", "ssm_a_re", "ssm_a_im", "ssm_log_dt", "ssm_b_re",
         "ssm_b_im", "ssm_c_re", "ssm_c_im", "ssm_d", "b_glu", "norm_mlp")
WEIGHT_ORDER = ("norm_mix", "w_in", "q_a_norm", "kv_a_norm", "w_q_b", "w_kv_b", "q_norm", "k_norm", "w_o_mla",
                "ssm_a_re", "ssm_a_im", "ssm_log_dt", "ssm_b_re", "ssm_b_im", "ssm_c_re", "ssm_c_im", "ssm_d", "w_glu",
                "b_glu", "w_o_ssm", "w_out", "norm_mlp", "w_up", "w_down")
LANES = 128


def _flat_rows(a):
    return a.reshape(-1, LANES)


def _pack_small(vals):
    flat = jnp.concatenate([vals[n].reshape(-1) for n in SMALL])
    rows = -(-flat.shape[0] // (8 * LANES)) * 8
    return jnp.pad(flat, (0, rows * LANES - flat.shape[0])).reshape(rows, LANES)


def _unpack_small(packed, like):
    flat, out, off = packed.reshape(-1), {}, 0
    for n in SMALL:
        size = like[n].size
        out[n] = flat[off:off + size].reshape(like[n].shape)
        off += size
    return out


def kernel(x, positions, norm_mix, w_in, q_a_norm, kv_a_norm, w_q_b, w_kv_b, q_norm, k_norm, w_o_mla, ssm_a_re, ssm_a_im, ssm_log_dt, ssm_b_re, ssm_b_im, ssm_c_re, ssm_c_im, ssm_d, w_glu, b_glu, w_o_ssm, w_out, norm_mlp, w_up, w_down, loss_target, m_norm_mix, m_w_in, m_q_a_norm, m_kv_a_norm, m_w_q_b, m_w_kv_b, m_q_norm, m_k_norm, m_w_o_mla, m_ssm_a_re, m_ssm_a_im, m_ssm_log_dt, m_ssm_b_re, m_ssm_b_im, m_ssm_c_re, m_ssm_c_im, m_ssm_d, m_w_glu, m_b_glu, m_w_o_ssm, m_w_out, m_norm_mlp, m_w_up, m_w_down, v_norm_mix, v_w_in, v_q_a_norm, v_kv_a_norm, v_w_q_b, v_w_kv_b, v_q_norm, v_k_norm, v_w_o_mla, v_ssm_a_re, v_ssm_a_im, v_ssm_log_dt, v_ssm_b_re, v_ssm_b_im, v_ssm_c_re, v_ssm_c_im, v_ssm_d, v_w_glu, v_b_glu, v_w_o_ssm, v_w_out, v_norm_mlp, v_w_up, v_w_down):
    given = dict(locals())
    w = {n: given[n] for n in WEIGHT_ORDER}
    m = {n: given["m_" + n] for n in WEIGHT_ORDER}
    v = {n: given["v_" + n] for n in WEIGHT_ORDER}

    mine = jnp.concatenate([_flat_rows(w[n][0]) for n, _, _ in BIG], axis=0).astype(BF16)
    gathered = _all_gather(mine, "gather_weights")
    wts, off = {}, 0
    for n, (r, c), by_col in BIG:
        rows = r * c // LANES
        shards = gathered[:, off:off + rows, :].reshape(N_DEV, r, c)
        wts[n] = jnp.transpose(shards, (1, 0, 2)).reshape(r, N_DEV * c) if by_col else shards.reshape(N_DEV * r, c)
        off += rows

    small = {n: w[n] for n in SMALL}
    loss, dx, g_big, g_small = _local_step(x[0], positions.reshape(-1, 1), loss_target[0], wts, small)

    parts = []
    for n, (r, c), by_col in BIG:
        g = g_big[n]
        g = jnp.transpose(g.reshape(r, N_DEV, c), (1, 0, 2)) if by_col else g.reshape(N_DEV, r, c)
        parts.append(g.reshape(N_DEV, r * c // LANES, LANES))
    g_mine = _reduce_scatter(jnp.concatenate(parts, axis=1), "reduce_grads")

    grads, deltas, new_m, new_v, off = {}, {}, {}, {}, 0
    for n, (r, c), _ in BIG:
        rows = r * c // LANES
        g = g_mine[off:off + rows].reshape(r, c)
        off += rows
        d, nm, nv = _adamw(w[n][0], g, m[n][0], v[n][0], "adamw_" + n)
        grads[n], deltas[n], new_m[n], new_v[n] = g[None], d[None], nm[None], nv[None]

    g_all = _all_gather(_pack_small(g_small), "gather_small_grads")
    packed = _adamw_small(g_all, _pack_small(small), _pack_small({n: m[n] for n in SMALL}),
                          _pack_small({n: v[n] for n in SMALL}))
    for dst, src in zip((grads, deltas, new_m, new_v), packed):
        dst.update(_unpack_small(src, small))

    total = lax.psum(loss[0, 0], ("x", "y", "c"))
    return (total, dx[None], *[grads[n] for n in WEIGHT_ORDER], *[deltas[n] for n in WEIGHT_ORDER],
            *[new_m[n] for n in WEIGHT_ORDER], *[new_v[n] for n in WEIGHT_ORDER])
```

```python
import functools
import math

import numpy as np
import jax
import jax.numpy as jnp
from jax import lax
from jax.experimental import pallas as pl
from jax.experimental.pallas import tpu as pltpu

F32 = jnp.float32
BF16 = jnp.bfloat16

D_MODEL = 1024
SSM_GROUPS = 32
SSM_GROUP_CH = 16
SSM_WIDTH = 512
SSM_STATE = 64
N_STATE = SSM_GROUPS * SSM_STATE
N_HEADS = 8
QK_NOPE = 128
QK_ROPE = 64
QK_HEAD = 192
QK_PAD = 256
V_HEAD = 128
Q_LORA = 384
KV_LORA = 256
KV_LAT_PAD = 384
ROPE_THETA = 10000.0
D_FF = 4096
EPS = 1e-6
ATT_SCALE = QK_HEAD ** -0.5
N_DEV = 8
FF_SHARD = D_FF // N_DEV
OUT_SHARD = D_MODEL // N_DEV

IN_SEGS = ((0, 512), (512, 896), (896, 1280), (1280, 2304), (2304, 3328))
D_IN = 3264
D_IN_PAD = 3328
KV_END = 1216

ADAM_LR = 0.001
ADAM_B1 = 0.9
ADAM_B2 = 0.999
ADAM_EPS = 1e-08
ADAM_WD = 0.01
ADAM_STEP = 10

VMEM_LIMIT = 56 * 1024 * 1024
MESH = pl.DeviceIdType.MESH
ANY = pl.BlockSpec(memory_space=pl.ANY)
LANES = 128

SCAN_T = 256
SCAN_CB = 256
ATT_T = 512
ROW_T = 256


def _params(sem=None):
    return pltpu.CompilerParams(dimension_semantics=sem, vmem_limit_bytes=VMEM_LIMIT)


def _rows(arr, tm):
    if arr.ndim == 2:
        return pl.BlockSpec((tm, arr.shape[1]), lambda i: (i, 0))
    return pl.BlockSpec((arr.shape[0], tm, arr.shape[2]), lambda i: (0, i, 0))


def _const(arr):
    nd = arr.ndim
    return pl.BlockSpec(arr.shape, lambda *_: (0,) * nd)


def _sds(shape, dtype):
    return jax.ShapeDtypeStruct(shape, dtype)


PEERS = tuple((dx, dy, dc) for dx in (0, 1) for dy in (0, 1) for dc in (0, 1) if (dx, dy, dc) != (0, 0, 0))


def _here():
    x, y, c = lax.axis_index("x"), lax.axis_index("y"), lax.axis_index("c")
    return x, y, c, 4 * x + 2 * y + c


def _xchg_start(scatter, srcs, dsts, send, recv, local):
    x, y, c, me = _here()
    for e, sc in enumerate(scatter):
        src, dst = srcs[e], dsts[e]
        pltpu.make_async_copy(src.at[me] if sc else src, dst.at[me], local.at[e]).start()
        for dx, dy, dc in PEERS:
            px, py, pc = (1 - x if dx else x), (1 - y if dy else y), (1 - c if dc else c)
            pltpu.make_async_remote_copy(
                src_ref=src.at[4 * px + 2 * py + pc] if sc else src, dst_ref=dst.at[me],
                send_sem=send.at[e], recv_sem=recv.at[e], device_id=(px, py, pc), device_id_type=MESH).start()


def _xchg_wait(scatter, srcs, dsts, send, recv, local):
    x, y, c, me = _here()
    for e, sc in enumerate(scatter):
        src, dst = srcs[e], dsts[e]
        pltpu.make_async_copy(src.at[me] if sc else src, dst.at[me], local.at[e]).wait()
        span = dst.at[pl.ds(0, N_DEV - 1)]
        both = pltpu.make_async_remote_copy(src_ref=span, dst_ref=span, send_sem=send.at[e], recv_sem=recv.at[e],
                                            device_id=(x, y, c), device_id_type=MESH)
        both.wait_send()
        both.wait_recv()


def _call(body, name, grid, ins, in_specs, outs, out_specs, scratch=(), xch=()):
    n_in, n_out, ne = len(ins), len(outs), len(xch)
    scatter = [sc for _, sc in xch]
    x_outs = [_sds((N_DEV,) + (a.shape[1:] if sc else a.shape), a.dtype) for a, sc in xch]
    sems = [pltpu.SemaphoreType.DMA((ne,))] * 3 if ne else []

    def wrapped(*refs):
        in_refs, x_src = refs[:n_in], refs[n_in:n_in + ne]
        out_refs = refs[n_in + ne:n_in + ne + n_out]
        x_dst = refs[n_in + ne + n_out:n_in + 2 * ne + n_out]
        rest = refs[n_in + 2 * ne + n_out:]
        if ne:
            x_sems, rest = rest[len(rest) - 3:], rest[:len(rest) - 3]
            first = functools.reduce(jnp.logical_and, [pl.program_id(d) == 0 for d in range(len(grid))])
            last = functools.reduce(jnp.logical_and, [pl.program_id(d) == grid[d] - 1 for d in range(len(grid))])

            @pl.when(first)
            def _():
                _xchg_start(scatter, x_src, x_dst, *x_sems)

        body(*in_refs, *out_refs, *rest)
        if ne:
            @pl.when(last)
            def _():
                _xchg_wait(scatter, x_src, x_dst, *x_sems)

    return pl.pallas_call(
        wrapped,
        name=name,
        grid=grid,
        in_specs=list(in_specs) + [ANY] * ne,
        out_specs=list(out_specs) + [ANY] * ne,
        out_shape=list(outs) + x_outs,
        scratch_shapes=list(scratch) + sems,
        compiler_params=_params(("arbitrary",) * len(grid)),
    )(*ins, *[a for a, _ in xch])


def _row_call(body, name, n_rows, tm, row_ins, const_ins, row_outs, acc_outs=(), xch=()):
    outs = [_sds(s, d) for s, d in row_outs] + [_sds(s, d) for s, d in acc_outs]
    out_specs = [_rows(o, tm) for o in outs[: len(row_outs)]] + [_const(o) for o in outs[len(row_outs):]]
    in_specs = [_rows(a, tm) for a in row_ins] + [_const(a) for a in const_ins]
    return _call(body, name, (n_rows // tm,), list(row_ins) + list(const_ins), in_specs, outs, out_specs, xch=xch)


def _dot(a, b):
    return jnp.dot(a, b, preferred_element_type=F32)


def _dot_nt(a, b):
    return lax.dot_general(a, b, (((1,), (1,)), ((), ())), preferred_element_type=F32)


def _dot_tn(a, b):
    return lax.dot_general(a, b, (((0,), (0,)), ((), ())), preferred_element_type=F32)


def _rms(x, g, n):
    inv = lax.rsqrt(jnp.sum(x * x, -1, keepdims=True) * (1.0 / n) + EPS)
    return x * inv * g, inv


def _rms_bwd(dy, x, g, inv, n):
    xh = x * inv
    dxh = dy * g
    dx = inv * (dxh - xh * (jnp.sum(dxh * xh, -1, keepdims=True) * (1.0 / n)))
    return dx, dy * xh


def _sigmoid(x):
    return 1.0 / (1.0 + jnp.exp(-x))


_GELU_C = math.sqrt(2.0 / math.pi)


def _gelu(y):
    th = jnp.tanh(_GELU_C * (y + 0.044715 * (y * y * y)))
    return 0.5 * y * (1.0 + th), th


def _gelu_grad(y, th):
    return 0.5 * (1.0 + th) + 0.5 * y * (1.0 - th * th) * (_GELU_C * (1.0 + 3.0 * 0.044715 * (y * y)))


def _acc(ref, val):
    @pl.when(pl.program_id(0) == 0)
    def _():
        ref[...] = jnp.zeros_like(ref)

    ref[...] += val


def _tile(n, limit):
    if n <= limit:
        return n
    return max(t for t in range(128, limit + 1, 128) if n % t == 0)


def _matmul_tn(a, b, name, tm=512, tk=512):
    k_dim, m = a.shape
    n = b.shape[1]
    tm, tk = _tile(m, tm), _tile(k_dim, tk)

    def body(a_ref, b_ref, o_ref):
        @pl.when(pl.program_id(1) == 0)
        def _():
            o_ref[...] = jnp.zeros_like(o_ref)

        o_ref[...] += _dot_tn(a_ref[...].astype(BF16), b_ref[...].astype(BF16))

    return pl.pallas_call(
        body,
        name=name,
        grid=(m // tm, k_dim // tk),
        in_specs=[pl.BlockSpec((tk, tm), lambda i, k: (k, i)), pl.BlockSpec((tk, n), lambda i, k: (k, 0))],
        out_specs=pl.BlockSpec((tm, n), lambda i, k: (i, 0)),
        out_shape=_sds((m, n), F32),
        compiler_params=_params(("parallel", "arbitrary")),
    )(a, b)


def _matmul_tn_shards(a, b, name, by_col, tm=512, tk=512):
    k_dim, m = a.shape
    n = b.shape[1]
    tm, tk = _tile(m, tm), _tile(k_dim, tk)
    nk = k_dim // tk
    if by_col:
        r, c = m, n // N_DEV
        out_spec = pl.BlockSpec((N_DEV, tm, c), lambda i, k: (0, i, 0))
    else:
        r, c = m // N_DEV, n
        per = tm // r
        out_spec = pl.BlockSpec((per, r, c), lambda i, k: (i, 0, 0))

    def body(a_ref, b_ref, o_ref, acc_ref):
        k = pl.program_id(1)

        @pl.when(k == 0)
        def _():
            acc_ref[...] = jnp.zeros_like(acc_ref)

        acc_ref[...] += _dot_tn(a_ref[...].astype(BF16), b_ref[...].astype(BF16))

        @pl.when(k == nk - 1)
        def _():
            if by_col:
                for j in range(N_DEV):
                    o_ref[j] = acc_ref[:, j * c:(j + 1) * c].astype(BF16)
            else:
                for s in range(per):
                    o_ref[s] = acc_ref[s * r:(s + 1) * r, :].astype(BF16)

    return pl.pallas_call(
        body,
        name=name,
        grid=(m // tm, nk),
        in_specs=[pl.BlockSpec((tk, tm), lambda i, k: (k, i)), pl.BlockSpec((tk, n), lambda i, k: (k, 0))],
        out_specs=out_spec,
        out_shape=_sds((N_DEV, r, c), BF16),
        scratch_shapes=[pltpu.VMEM((tm, n), F32)],
        compiler_params=_params(("parallel", "arbitrary")),
    )(a, b)


def _rope_tables(pos_col):
    n = pos_col.shape[0]
    half = QK_ROPE // 2
    inv_freq = (ROPE_THETA ** (-np.arange(half, dtype=np.float32) / half)).astype(np.float32)
    freq_row = jnp.asarray(np.concatenate([inv_freq, inv_freq, np.zeros(64, np.float32)])[None, :])

    def body(p_ref, f_ref, c_ref, s_ref):
        ang = p_ref[...].astype(F32) * f_ref[...]
        c_ref[...] = jnp.cos(ang)
        s_ref[...] = jnp.sin(ang)

    return _row_call(body, "rope_tables", n, min(n, 1024), [pos_col], [freq_row], [((n, 128), F32)] * 2)


def _rope_rot(v):
    lane = lax.broadcasted_iota(jnp.int32, v.shape, 1)
    return jnp.where(lane < 32, -pltpu.roll(v, 96, 1), jnp.where(lane < 64, pltpu.roll(v, 32, 1), 0.0))


def _rope_rot_t(v):
    lane = lax.broadcasted_iota(jnp.int32, v.shape, 1)
    return jnp.where(lane < 32, pltpu.roll(v, 96, 1), jnp.where(lane < 64, -pltpu.roll(v, 32, 1), 0.0))


def _in_proj(x, norm_mix, w_in_pad, xch):
    n = x.shape[0]

    def body(x_ref, g_ref, w_ref, xn_ref, u_ref, ql_ref, kvl_ref, gs_ref, gm_ref):
        xn, _ = _rms(x_ref[...], g_ref[...], D_MODEL)
        xb = xn.astype(BF16)
        xn_ref[...] = xb
        for ref, (a, b) in zip((u_ref, ql_ref, kvl_ref, gs_ref, gm_ref), IN_SEGS):
            ref[...] = _dot(xb, w_ref[:, a:b])

    outs = [((n, D_MODEL), BF16)] + [((n, b - a), F32) for a, b in IN_SEGS]
    return _row_call(body, "in_proj", n, ROW_T, [x], [norm_mix, w_in_pad], outs, xch=xch)


def _ssm_prep_fn(a_re, a_im, log_dt, b_re_x, b_im_x):
    dt = jnp.exp(log_dt)
    mag = jnp.exp(a_re * dt)
    lr = mag * jnp.cos(a_im * dt)
    li = mag * jnp.sin(a_im * dt)
    den = a_re * a_re + a_im * a_im
    fr = ((lr - 1.0) * a_re + li * a_im) / den
    fi = (li * a_re - (lr - 1.0) * a_im) / den
    return lr, li, fr * b_re_x - fi * b_im_x, fr * b_im_x + fi * b_re_x


def _ssm_prep(a_re, a_im, log_dt, b_re_x, b_im_x):
    def body(ar, ai, ld, br, bi, lam_ref, bblk_ref):
        lr, li, bbr, bbi = _ssm_prep_fn(ar[...], ai[...], ld[...], br[...], bi[...])
        lam_ref[0:1, :] = lr
        lam_ref[1:2, :] = li
        bblk_ref[:, 0:N_STATE] = bbr.astype(BF16)
        bblk_ref[:, N_STATE:] = bbi.astype(BF16)

    return pl.pallas_call(
        body,
        name="ssm_prep",
        out_shape=[_sds((2, N_STATE), F32), _sds((SSM_WIDTH, 2 * N_STATE), BF16)],
        compiler_params=_params(),
    )(a_re, a_im, log_dt, b_re_x, b_im_x)


def _ssm_prep_bwd(a_re, a_im, log_dt, b_re_x, b_im_x, dlam, dbblk):
    def body(ar, ai, ld, br, bi, dl, db, dar, dai, dld, dbr, dbi):
        _, vjp = jax.vjp(_ssm_prep_fn, ar[...], ai[...], ld[...], br[...], bi[...])
        g = vjp((dl[0:1, :], dl[1:2, :], db[:, 0:N_STATE], db[:, N_STATE:]))
        dar[...] = g[0]
        dai[...] = g[1]
        grp = lax.broadcasted_iota(jnp.int32, (SSM_GROUPS, N_STATE), 0)
        lane = lax.broadcasted_iota(jnp.int32, (SSM_GROUPS, N_STATE), 1)
        sel = (lane // SSM_STATE) == grp
        dld[...] = jnp.sum(jnp.where(sel, jnp.broadcast_to(g[2], (SSM_GROUPS, N_STATE)), 0.0), axis=1, keepdims=True)
        dbr[...] = g[3]
        dbi[...] = g[4]

    return pl.pallas_call(
        body,
        name="ssm_prep_bwd",
        out_shape=[_sds((1, N_STATE), F32), _sds((1, N_STATE), F32), _sds((SSM_GROUPS, 1), F32),
                   _sds((SSM_WIDTH, N_STATE), F32), _sds((SSM_WIDTH, N_STATE), F32)],
        compiler_params=_params(),
    )(a_re, a_im, log_dt, b_re_x, b_im_x, dlam, dbblk)


def _scan_powers(lam_ref, pw_ref, n_lev, conj):
    pr = lam_ref[0:1, :]
    pi = lam_ref[1:2, :]
    if conj:
        pi = -pi
    for lev in range(n_lev):
        pw_ref[2 * lev:2 * lev + 1, :] = pr
        pw_ref[2 * lev + 1:2 * lev + 2, :] = pi
        pr, pi = pr * pr - pi * pi, 2.0 * pr * pi


def _scan(buf_a, buf_b, pw_ref, t, reverse):
    src, dst = buf_a, buf_b
    n_lev = int(math.log2(t))
    for lev in range(n_lev):
        d = 1 << lev
        keep = pl.ds(t - d, d) if reverse else pl.ds(0, d)
        upd = pl.ds(0, t - d) if reverse else pl.ds(d, t - d)
        frm = pl.ds(d, t - d) if reverse else pl.ds(0, t - d)

        def col_block(cb, carry, src=src, dst=dst, lev=lev, keep=keep, upd=upd, frm=frm):
            c0 = pl.multiple_of(cb * SCAN_CB, SCAN_CB)
            re = pl.ds(c0, SCAN_CB)
            im = pl.ds(pl.multiple_of(N_STATE + cb * SCAN_CB, SCAN_CB), SCAN_CB)
            pr = pw_ref[pl.ds(2 * lev, 1), re]
            pi = pw_ref[pl.ds(2 * lev + 1, 1), re]
            sr = src[frm, re]
            si = src[frm, im]
            dst[upd, re] = src[upd, re] + pr * sr - pi * si
            dst[upd, im] = src[upd, im] + pr * si + pi * sr
            dst[keep, re] = src[keep, re]
            dst[keep, im] = src[keep, im]
            return carry

        lax.fori_loop(0, N_STATE // SCAN_CB, col_block, 0)
        src, dst = dst, src
    return src


def _ssm_fwd(u, bblk, cblk, lam, d_row, w_glu, b_glu, w_o_ssm, xch):
    n = u.shape[0]
    t = min(SCAN_T, n)
    n_lev = int(math.log2(t))
    kb = 512

    def body(u_ref, bblk_ref, cblk_ref, lam_ref, d_ref, wg_ref, bg_ref, wo_ref, y_ref, ys_ref, st_ref,
             buf_a, buf_b, pw_ref, carry_ref):
        @pl.when(pl.program_id(0) == 0)
        def _():
            carry_ref[...] = jnp.zeros_like(carry_ref)

        _scan_powers(lam_ref, pw_ref, n_lev, False)
        st_ref[0] = carry_ref[...]
        u_t = u_ref[...]
        ub = u_t.astype(BF16)
        for c in range(0, 2 * N_STATE, kb):
            buf_a[:, c:c + kb] = _dot(ub, bblk_ref[:, c:c + kb])
        lr, li = lam_ref[0:1, :], lam_ref[1:2, :]
        cr, ci = carry_ref[0:1, 0:N_STATE], carry_ref[0:1, N_STATE:]
        buf_a[0:1, 0:N_STATE] += lr * cr - li * ci
        buf_a[0:1, N_STATE:] += lr * ci + li * cr
        res = _scan(buf_a, buf_b, pw_ref, t, False)
        carry_ref[...] = jnp.broadcast_to(res[t - 1:t, :], carry_ref.shape)
        y = d_ref[...] * u_t
        for c in range(0, 2 * N_STATE, kb):
            y += _dot(res[:, c:c + kb].astype(BF16), cblk_ref[c:c + kb, :])
        y_ref[...] = y
        z, _ = _gelu(y)
        s = _sigmoid(_dot(z.astype(BF16), wg_ref[...]) + bg_ref[...])
        zgb = (z * s).astype(BF16)
        for j in range(N_DEV):
            ys_ref[:, j * OUT_SHARD:(j + 1) * OUT_SHARD] = _dot(zgb, wo_ref[j])

    consts = [bblk, cblk, lam, d_row, w_glu, b_glu, w_o_ssm]
    return _call(
        body, "ssm_fwd", (n // t,), [u] + consts, [_rows(u, t)] + [_const(a) for a in consts],
        [_sds((n, SSM_WIDTH), F32), _sds((n, D_MODEL), F32), _sds((n // t, 8, 2 * N_STATE), F32)],
        [pl.BlockSpec((t, SSM_WIDTH), lambda i: (i, 0)), pl.BlockSpec((t, D_MODEL), lambda i: (i, 0)),
         pl.BlockSpec((1, 8, 2 * N_STATE), lambda i: (i, 0, 0))],
        scratch=[pltpu.VMEM((t, 2 * N_STATE), F32), pltpu.VMEM((t, 2 * N_STATE), F32),
                 pltpu.VMEM((2 * n_lev, N_STATE), F32), pltpu.VMEM((8, 2 * N_STATE), F32)],
        xch=xch)


def _head_norm_rope(slab, gain, cos_t, sin_t):
    xn, inv = _rms(slab, gain, QK_HEAD)
    lo, hi = xn[:, 0:128], xn[:, 128:256]
    return jnp.concatenate([lo, hi * cos_t + _rope_rot(hi) * sin_t], axis=-1), inv


def _head_norm_rope_bwd(g, slab, gain, inv, cos_t, sin_t):
    g_lo, g_hi = g[:, 0:128], g[:, 128:256]
    g_n = jnp.concatenate([g_lo, g_hi * cos_t + _rope_rot_t(g_hi * sin_t)], axis=-1)
    return _rms_bwd(g_n, slab, gain, inv, QK_HEAD)


def _qkv_prep(ql, kvl, q_a_norm, kv_a_norm, wq, wkv, gq, gk, cos_t, sin_t):
    n = ql.shape[0]

    def body(ql_ref, kvl_ref, cos_ref, sin_ref, qa_ref, ka_ref, wq_ref, wkv_ref, gq_ref, gk_ref, q_ref, k_ref, v_ref):
        cos_t, sin_t = cos_ref[...], sin_ref[...]
        qa, _ = _rms(ql_ref[...], qa_ref[...], Q_LORA)
        qab = qa.astype(BF16)
        kvl_t = kvl_ref[...]
        ca, _ = _rms(kvl_t[:, 0:KV_LORA], ka_ref[...], KV_LORA)
        cab = ca.astype(BF16)
        kpe = kvl_t[:, KV_LORA:KV_LAT_PAD]
        for h in range(N_HEADS):
            qh, _ = _head_norm_rope(_dot(qab, wq_ref[h]), gq_ref[...], cos_t, sin_t)
            q_ref[h] = (qh * ATT_SCALE).astype(BF16)
            kv_h = _dot(cab, wkv_ref[h])
            kh, _ = _head_norm_rope(jnp.concatenate([kv_h[:, 0:QK_NOPE], kpe], axis=-1), gk_ref[...], cos_t, sin_t)
            k_ref[h] = kh.astype(BF16)
            v_ref[h] = kv_h[:, QK_NOPE:].astype(BF16)

    outs = [((N_HEADS, n, QK_PAD), BF16), ((N_HEADS, n, QK_PAD), BF16), ((N_HEADS, n, V_HEAD), BF16)]
    return _row_call(body, "qkv_prep", n, ROW_T, [ql, kvl, cos_t, sin_t], [q_a_norm, kv_a_norm, wq, wkv, gq, gk], outs)


def _causal_mask(s, t):
    row = lax.broadcasted_iota(jnp.int32, (t, t), 0)
    col = lax.broadcasted_iota(jnp.int32, (t, t), 1)
    return jnp.where(col <= row, s, -jnp.inf)


def _attn_fwd(q, k, v, xch):
    n = q.shape[1]
    t = min(ATT_T, n)

    def body(q_ref, k_ref, v_ref, o_ref, lse_ref):
        i = pl.program_id(1)
        qt = q_ref[0]

        def kv_tile(j, carry, diag):
            m, l, acc = carry
            r0 = pl.multiple_of(j * t, t)
            s = _dot_nt(qt, k_ref[0, pl.ds(r0, t), :])
            if diag:
                s = _causal_mask(s, t)
            m_new = jnp.maximum(m, jnp.max(s, -1, keepdims=True))
            alpha = jnp.exp(m - m_new)
            p = jnp.exp(s - m_new)
            l = alpha * l + jnp.sum(p, -1, keepdims=True)
            acc = alpha * acc + _dot(p.astype(BF16), v_ref[0, pl.ds(r0, t), :])
            return m_new, l, acc

        init = (jnp.full((t, 1), -jnp.inf, F32), jnp.zeros((t, 1), F32), jnp.zeros((t, V_HEAD), F32))
        carry = lax.fori_loop(0, i, functools.partial(kv_tile, diag=False), init)
        m, l, acc = kv_tile(i, carry, True)
        o_ref[...] = acc / l
        lse_ref[0] = m + jnp.log(l)

    return _call(
        body, "attn_fwd", (N_HEADS, n // t), [q, k, v],
        [pl.BlockSpec((1, t, QK_PAD), lambda h, i: (h, i, 0)), pl.BlockSpec((1, n, QK_PAD), lambda h, i: (h, 0, 0)),
         pl.BlockSpec((1, n, V_HEAD), lambda h, i: (h, 0, 0))],
        [_sds((n, N_HEADS * V_HEAD), F32), _sds((N_HEADS, n, 1), F32)],
        [pl.BlockSpec((t, V_HEAD), lambda h, i: (i, h)), pl.BlockSpec((1, t, 1), lambda h, i: (h, i, 0))],
        xch=xch)


def _merge(attn, gs, gm, y_ssm, x, w_o_mla, w_out):
    n = x.shape[0]

    def body(at_ref, gs_ref, gm_ref, ys_ref, x_ref, wo_ref, wout_ref, h_ref, mx_ref, ym_ref):
        y_mla = _dot(at_ref[...].astype(BF16), wo_ref[...])
        ym_ref[...] = y_mla
        mixed = (_sigmoid(gs_ref[...]) * ys_ref[...] + _sigmoid(gm_ref[...]) * y_mla).astype(BF16)
        mx_ref[...] = mixed
        h_ref[...] = x_ref[...] + _dot(mixed, wout_ref[...])

    outs = [((n, D_MODEL), F32), ((n, D_MODEL), BF16), ((n, D_MODEL), F32)]
    return _row_call(body, "merge", n, ROW_T, [attn, gs, gm, y_ssm, x], [w_o_mla, w_out], outs)


def _mlp_fwd_loss(h, target, norm_mlp, w_up, w_down):
    n = h.shape[0]

    def body(h_ref, t_ref, g_ref, wu_ref, wd_ref, hn_ref, do_ref, loss_ref):
        h_t = h_ref[...]
        hn, _ = _rms(h_t, g_ref[...], D_MODEL)
        hb = hn.astype(BF16)
        hn_ref[...] = hb
        out = h_t
        for j in range(N_DEV):
            a = jnp.maximum(_dot(hb, wu_ref[j]), 0.0)
            out += _dot((a * a).astype(BF16), wd_ref[j])
        err = out - t_ref[...]
        do_ref[...] = err * (1.0 / D_MODEL)
        _acc(loss_ref, jnp.broadcast_to(jnp.sum(err * err) * (0.5 / D_MODEL), loss_ref.shape))

    outs = [((n, D_MODEL), BF16), ((n, D_MODEL), F32)]
    return _row_call(body, "mlp_fwd_loss", n, ROW_T, [h, target], [norm_mlp, w_up, w_down], outs, [((8, 128), F32)])


def _mlp_bwd(dout, hn, h, norm_mlp, w_up, w_down):
    n = h.shape[0]

    def body(do_ref, hn_ref, h_ref, g_ref, wu_ref, wd_ref, hid_ref, da_ref, dh_ref, dg_ref):
        dout_t = do_ref[...]
        doutb = dout_t.astype(BF16)
        hb = hn_ref[...]
        dhn = jnp.zeros_like(dout_t)
        for j in range(N_DEV):
            cols = slice(j * FF_SHARD, (j + 1) * FF_SHARD)
            a = jnp.maximum(_dot(hb, wu_ref[j]), 0.0)
            hid_ref[:, cols] = (a * a).astype(BF16)
            da = (_dot_nt(doutb, wd_ref[j]) * (2.0 * a)).astype(BF16)
            da_ref[:, cols] = da
            dhn += _dot_nt(da, wu_ref[j])
        h_t = h_ref[...]
        inv = lax.rsqrt(jnp.sum(h_t * h_t, -1, keepdims=True) * (1.0 / D_MODEL) + EPS)
        dx, dg = _rms_bwd(dhn, h_t, g_ref[...], inv, D_MODEL)
        dh_ref[...] = dout_t + dx
        _acc(dg_ref, jnp.sum(dg, 0, keepdims=True))

    outs = [((n, D_FF), BF16), ((n, D_FF), BF16), ((n, D_MODEL), F32)]
    return _row_call(body, "mlp_bwd", n, ROW_T, [dout, hn, h], [norm_mlp, w_up, w_down], outs, [((1, D_MODEL), F32)])


def _merge_bwd(dh, gs, gm, y_ssm, y_mla, w_out, w_o_mla):
    n = dh.shape[0]

    def body(dh_ref, gs_ref, gm_ref, ys_ref, ym_ref, wout_ref, wo_ref, dgs_ref, dgm_ref, dys_ref, dym_ref, dat_ref):
        dmix = _dot_nt(dh_ref[...].astype(BF16), wout_ref[...])
        sgs, sgm = _sigmoid(gs_ref[...]), _sigmoid(gm_ref[...])
        dgs_ref[...] = (dmix * ys_ref[...] * sgs * (1.0 - sgs)).astype(BF16)
        dgm_ref[...] = (dmix * ym_ref[...] * sgm * (1.0 - sgm)).astype(BF16)
        dys_ref[...] = (dmix * sgs).astype(BF16)
        dym = (dmix * sgm).astype(BF16)
        dym_ref[...] = dym
        dat_ref[...] = _dot_nt(dym, wo_ref[...])

    outs = [((n, D_MODEL), BF16)] * 4 + [((n, D_MODEL), F32)]
    return _row_call(body, "merge_bwd", n, ROW_T, [dh, gs, gm, y_ssm, y_mla], [w_out, w_o_mla], outs)


def _attn_bwd(q, k, v, out, lse, dout, xch):
    n = q.shape[1]
    t = min(ATT_T, n)
    nt = n // t

    def body(q_ref, k_ref, v_ref, o_ref, lse_ref, do_ref, dq_ref, dk_ref, dv_ref, delta_ref):
        j = pl.program_id(1)

        @pl.when(j == 0)
        def _():
            dq_ref[...] = jnp.zeros_like(dq_ref)
            delta_ref[...] = jnp.sum(do_ref[...] * o_ref[...], -1, keepdims=True)

        kt = k_ref[0]
        vt = v_ref[0]

        def q_tile(i, carry, diag):
            dk, dv = carry
            r0 = pl.multiple_of(i * t, t)
            rows = pl.ds(r0, t)
            qt = q_ref[0, rows, :]
            s = _dot_nt(qt, kt)
            if diag:
                s = _causal_mask(s, t)
            p = jnp.exp(s - lse_ref[0, rows, :])
            dot = do_ref[rows, :].astype(BF16)
            dv = dv + _dot_tn(p.astype(BF16), dot)
            ds = (p * (_dot_nt(dot, vt) - delta_ref[rows, :])).astype(BF16)
            dk = dk + _dot_tn(ds, qt)
            dq_ref[0, rows, :] += _dot(ds, kt)
            return dk, dv

        carry = q_tile(j, (jnp.zeros((t, QK_PAD), F32), jnp.zeros((t, V_HEAD), F32)), True)
        dk, dv = lax.fori_loop(j + 1, nt, functools.partial(q_tile, diag=False), carry)
        dk_ref[0] = dk
        dv_ref[0] = dv

    return _call(
        body, "attn_bwd", (N_HEADS, nt), [q, k, v, out, lse, dout],
        [pl.BlockSpec((1, n, QK_PAD), lambda h, j: (h, 0, 0)), pl.BlockSpec((1, t, QK_PAD), lambda h, j: (h, j, 0)),
         pl.BlockSpec((1, t, V_HEAD), lambda h, j: (h, j, 0)), pl.BlockSpec((n, V_HEAD), lambda h, j: (0, h)),
         pl.BlockSpec((1, n, 1), lambda h, j: (h, 0, 0)), pl.BlockSpec((n, V_HEAD), lambda h, j: (0, h))],
        [_sds((N_HEADS, n, QK_PAD), F32), _sds((N_HEADS, n, QK_PAD), F32), _sds((N_HEADS, n, V_HEAD), F32)],
        [pl.BlockSpec((1, n, QK_PAD), lambda h, j: (h, 0, 0)), pl.BlockSpec((1, t, QK_PAD), lambda h, j: (h, j, 0)),
         pl.BlockSpec((1, t, V_HEAD), lambda h, j: (h, j, 0))],
        scratch=[pltpu.VMEM((n, 1), F32)],
        xch=xch)


def _qkv_prep_bwd(ql, kvl, dq, dk, dv, q_a_norm, kv_a_norm, wq, wkv, gq, gk, cos_t, sin_t, xch):
    n = ql.shape[0]

    def body(ql_ref, kvl_ref, cos_ref, sin_ref, dq_ref, dk_ref, dv_ref, qa_ref, ka_ref, wq_ref, wkv_ref, gq_ref, gk_ref,
             dql_ref, dkvl_ref, qab_ref, dqp_ref, cab_ref, dkvp_ref, dqa_ref, dka_ref, dgq_ref, dgk_ref):
        cos_t, sin_t = cos_ref[...], sin_ref[...]
        ql_t = ql_ref[...]
        qa, inv_qa = _rms(ql_t, qa_ref[...], Q_LORA)
        qab = qa.astype(BF16)
        qab_ref[...] = qab
        kvl_t = kvl_ref[...]
        ckv = kvl_t[:, 0:KV_LORA]
        ca, inv_ca = _rms(ckv, ka_ref[...], KV_LORA)
        cab = ca.astype(BF16)
        cab_ref[...] = cab
        kpe = kvl_t[:, KV_LORA:KV_LAT_PAD]
        dgq = jnp.zeros((1, QK_PAD), F32)
        dgk = jnp.zeros((1, QK_PAD), F32)
        dkpe = jnp.zeros_like(kpe)
        dqa = jnp.zeros_like(ql_t)
        dca = jnp.zeros_like(ckv)
        for h in range(N_HEADS):
            q_slab = _dot(qab, wq_ref[h])
            inv = lax.rsqrt(jnp.sum(q_slab * q_slab, -1, keepdims=True) * (1.0 / QK_HEAD) + EPS)
            d_slab, dg = _head_norm_rope_bwd(dq_ref[h] * ATT_SCALE, q_slab, gq_ref[...], inv, cos_t, sin_t)
            dqp = d_slab.astype(BF16)
            dqp_ref[:, h * QK_PAD:(h + 1) * QK_PAD] = dqp
            dqa += _dot_nt(dqp, wq_ref[h])
            dgq += jnp.sum(dg, 0, keepdims=True)
            kv_h = _dot(cab, wkv_ref[h])
            k_slab = jnp.concatenate([kv_h[:, 0:QK_NOPE], kpe], axis=-1)
            inv = lax.rsqrt(jnp.sum(k_slab * k_slab, -1, keepdims=True) * (1.0 / QK_HEAD) + EPS)
            d_slab, dg = _head_norm_rope_bwd(dk_ref[h], k_slab, gk_ref[...], inv, cos_t, sin_t)
            dkvp = jnp.concatenate([d_slab[:, 0:QK_NOPE], dv_ref[h]], axis=-1).astype(BF16)
            dkvp_ref[:, h * QK_PAD:(h + 1) * QK_PAD] = dkvp
            dca += _dot_nt(dkvp, wkv_ref[h])
            dkpe += d_slab[:, QK_NOPE:QK_PAD]
            dgk += jnp.sum(dg, 0, keepdims=True)
        dx, dg = _rms_bwd(dqa, ql_t, qa_ref[...], inv_qa, Q_LORA)
        dql_ref[...] = dx.astype(BF16)
        _acc(dqa_ref, jnp.sum(dg, 0, keepdims=True))
        dx, dg = _rms_bwd(dca, ckv, ka_ref[...], inv_ca, KV_LORA)
        dkvl_ref[:, 0:KV_LORA] = dx.astype(BF16)
        dkvl_ref[:, KV_LORA:KV_LAT_PAD] = dkpe.astype(BF16)
        _acc(dka_ref, jnp.sum(dg, 0, keepdims=True))
        _acc(dgq_ref, dgq)
        _acc(dgk_ref, dgk)

    row_outs = [((n, Q_LORA), BF16), ((n, KV_LAT_PAD), BF16), ((n, Q_LORA), BF16), ((n, N_HEADS * QK_PAD), BF16),
                ((n, KV_LORA), BF16), ((n, N_HEADS * (QK_NOPE + V_HEAD)), BF16)]
    acc_outs = [((1, Q_LORA), F32), ((1, KV_LORA), F32), ((1, QK_PAD), F32), ((1, QK_PAD), F32)]
    return _row_call(body, "qkv_prep_bwd", n, ROW_T, [ql, kvl, cos_t, sin_t, dq, dk, dv],
                     [q_a_norm, kv_a_norm, wq, wkv, gq, gk], row_outs, acc_outs, xch=xch)


def _glu_bwd(dy_ssm, y, w_glu, b_glu, w_o_ssm):
    n = y.shape[0]

    def body(dys_ref, y_ref, wg_ref, bg_ref, wo_ref, dy_ref, zg_ref, z_ref, dt_ref, db_ref):
        y_t = y_ref[...]
        z, th = _gelu(y_t)
        zb = z.astype(BF16)
        z_ref[...] = zb
        s = _sigmoid(_dot(zb, wg_ref[...]) + bg_ref[...])
        zg_ref[...] = (z * s).astype(BF16)
        dys = dys_ref[...]
        dzg = jnp.zeros_like(y_t)
        for j in range(N_DEV):
            dzg += _dot_nt(dys[:, j * OUT_SHARD:(j + 1) * OUT_SHARD], wo_ref[j])
        dt = dzg * z * s * (1.0 - s)
        dtb = dt.astype(BF16)
        dt_ref[...] = dtb
        dz = dzg * s + _dot_nt(dtb, wg_ref[...])
        dy_ref[...] = dz * _gelu_grad(y_t, th)
        _acc(db_ref, jnp.sum(dt, 0, keepdims=True))

    outs = [((n, SSM_WIDTH), F32)] + [((n, SSM_WIDTH), BF16)] * 3
    return _row_call(body, "glu_bwd", n, ROW_T, [dy_ssm, y], [w_glu, b_glu, w_o_ssm], outs, [((1, SSM_WIDTH), F32)])


def _ssm_bwd(u, dy, st, bblk, cblk, lam, d_row, xch):
    n = u.shape[0]
    t = min(SCAN_T, n)
    nc = n // t
    n_lev = int(math.log2(t))
    kb = 512

    def body(u_ref, dy_ref, st_ref, bblk_ref, cblk_ref, lam_ref, d_ref, du_ref, xs_ref, as_ref, dlam_ref, dd_ref,
             buf_a, buf_b, buf_c, buf_d, pw_ref, pwc_ref, carry_ref):
        @pl.when(pl.program_id(0) == 0)
        def _():
            carry_ref[...] = jnp.zeros_like(carry_ref)

        _scan_powers(lam_ref, pw_ref, n_lev, False)
        _scan_powers(lam_ref, pwc_ref, n_lev, True)
        lr, li = lam_ref[0:1, :], lam_ref[1:2, :]
        u_t = u_ref[...]
        ub = u_t.astype(BF16)
        dy_t = dy_ref[...]
        dyb = dy_t.astype(BF16)
        for c in range(0, 2 * N_STATE, kb):
            buf_a[:, c:c + kb] = _dot(ub, bblk_ref[:, c:c + kb])
        s_re, s_im = st_ref[0, 0:1, 0:N_STATE], st_ref[0, 0:1, N_STATE:]
        buf_a[0:1, 0:N_STATE] += lr * s_re - li * s_im
        buf_a[0:1, N_STATE:] += lr * s_im + li * s_re
        xs = _scan(buf_a, buf_b, pw_ref, t, False)
        for c in range(0, 2 * N_STATE, kb):
            buf_c[:, c:c + kb] = _dot_nt(dyb, cblk_ref[c:c + kb, :])
        a_re, a_im = carry_ref[0:1, 0:N_STATE], carry_ref[0:1, N_STATE:]
        buf_c[t - 1:t, 0:N_STATE] += lr * a_re + li * a_im
        buf_c[t - 1:t, N_STATE:] += lr * a_im - li * a_re
        ad = _scan(buf_c, buf_d, pwc_ref, t, True)
        carry_ref[...] = jnp.broadcast_to(ad[0:1, :], carry_ref.shape)
        du = d_ref[...] * dy_t
        for c in range(0, 2 * N_STATE, kb):
            adb = ad[:, c:c + kb].astype(BF16)
            as_ref[:, c:c + kb] = adb
            xs_ref[:, c:c + kb] = xs[:, c:c + kb].astype(BF16)
            du += _dot_nt(adb, bblk_ref[:, c:c + kb])
        du_ref[...] = du.astype(BF16)
        for c in range(0, N_STATE, kb):
            re, im = pl.ds(c, kb), pl.ds(N_STATE + c, kb)
            xr, xi = xs[pl.ds(0, t - 1), re], xs[pl.ds(0, t - 1), im]
            ar, ai = ad[pl.ds(1, t - 1), re], ad[pl.ds(1, t - 1), im]
            x0r, x0i = st_ref[0, 0:1, re], st_ref[0, 0:1, im]
            a0r, a0i = ad[0:1, re], ad[0:1, im]
            dlam_part_re = jnp.sum(ar * xr + ai * xi, 0, keepdims=True) + a0r * x0r + a0i * x0i
            dlam_part_im = jnp.sum(ai * xr - ar * xi, 0, keepdims=True) + a0i * x0r - a0r * x0i

            @pl.when(pl.program_id(0) == 0)
            def _(c=c):
                dlam_ref[0:1, c:c + kb] = jnp.zeros((1, kb), F32)
                dlam_ref[1:2, c:c + kb] = jnp.zeros((1, kb), F32)

            dlam_ref[0:1, c:c + kb] += dlam_part_re
            dlam_ref[1:2, c:c + kb] += dlam_part_im
        _acc(dd_ref, jnp.sum(dy_t * u_t, 0, keepdims=True))

    rev = lambda i: (nc - 1 - i, 0)
    consts = [bblk, cblk, lam, d_row]
    return _call(
        body, "ssm_bwd", (nc,), [u, dy, st] + consts,
        [pl.BlockSpec((t, SSM_WIDTH), rev), pl.BlockSpec((t, SSM_WIDTH), rev),
         pl.BlockSpec((1, 8, 2 * N_STATE), lambda i: (nc - 1 - i, 0, 0))] + [_const(a) for a in consts],
        [_sds((n, SSM_WIDTH), BF16), _sds((n, 2 * N_STATE), BF16), _sds((n, 2 * N_STATE), BF16),
         _sds((2, N_STATE), F32), _sds((1, SSM_WIDTH), F32)],
        [pl.BlockSpec((t, SSM_WIDTH), rev), pl.BlockSpec((t, 2 * N_STATE), rev), pl.BlockSpec((t, 2 * N_STATE), rev),
         pl.BlockSpec((2, N_STATE), lambda i: (0, 0)), pl.BlockSpec((1, SSM_WIDTH), lambda i: (0, 0))],
        scratch=[pltpu.VMEM((t, 2 * N_STATE), F32)] * 4
        + [pltpu.VMEM((2 * n_lev, N_STATE), F32)] * 2 + [pltpu.VMEM((8, 2 * N_STATE), F32)],
        xch=xch)


def _in_proj_bwd(pieces, dh, x, norm_mix, w_in_pad):
    n = x.shape[0]

    def body(du_ref, dql_ref, dkvl_ref, dgs_ref, dgm_ref, dh_ref, x_ref, g_ref, w_ref, dx_ref, dp_ref, dg_ref):
        dxn = jnp.zeros((dh_ref.shape[0], D_MODEL), F32)
        for ref, (a, b) in zip((du_ref, dql_ref, dkvl_ref, dgs_ref, dgm_ref), IN_SEGS):
            piece = ref[...]
            dp_ref[:, a:b] = piece
            dxn += _dot_nt(piece, w_ref[:, a:b])
        x_t = x_ref[...]
        inv = lax.rsqrt(jnp.sum(x_t * x_t, -1, keepdims=True) * (1.0 / D_MODEL) + EPS)
        dx, dg = _rms_bwd(dxn, x_t, g_ref[...], inv, D_MODEL)
        dx_ref[...] = dh_ref[...] + dx
        _acc(dg_ref, jnp.sum(dg, 0, keepdims=True))

    outs = [((n, D_MODEL), F32), ((n, D_IN_PAD), BF16)]
    return _row_call(body, "in_proj_bwd", n, ROW_T, list(pieces) + [dh, x], [norm_mix, w_in_pad], outs,
                     [((1, D_MODEL), F32)])


def _block_diag(a, rows_per_group, cols_per_group):
    eye = jnp.eye(SSM_GROUPS, dtype=a.dtype)
    return (a[:, :, None, :] * eye[:, None, :, None]).reshape(SSM_GROUPS * rows_per_group, SSM_GROUPS * cols_per_group)


def _block_diag_extract(m, rows_per_group, cols_per_group):
    m4 = m.reshape(SSM_GROUPS, rows_per_group, SSM_GROUPS, cols_per_group)
    eye = jnp.eye(SSM_GROUPS, dtype=m.dtype)
    return jnp.sum(m4 * eye[:, None, :, None], axis=2)


def _pad_in(w):
    return jnp.concatenate([w[:, :KV_END], jnp.zeros((w.shape[0], D_IN_PAD - D_IN), w.dtype), w[:, KV_END:]], axis=1)


def _unpad_in(w):
    return jnp.concatenate([w[:, :KV_END], w[:, KV_END + D_IN_PAD - D_IN:]], axis=1)


def _pad_gain(g):
    return jnp.pad(g, ((0, 0), (0, QK_PAD - QK_HEAD)))


def _place():
    x, y, c = lax.axis_index("x"), lax.axis_index("y"), lax.axis_index("c")
    chips = [(x, y), (1 - x, y), (x, 1 - y), (1 - x, 1 - y)]
    return x, y, c, chips


def _all_gather(block, name):
    rows, lanes = block.shape

    def body(x_ref, out_ref, send_sems, recv_sems, local_sem):
        x, y, c, chips = _place()
        me, sibling = (x, y, c), (x, y, 1 - c)

        def slot(px, py, pc):
            return out_ref.at[4 * px + 2 * py + pc]

        def copy(k, blk, to, src=None):
            return pltpu.make_async_remote_copy(
                src_ref=slot(*blk) if src is None else src, dst_ref=slot(*blk),
                send_sem=send_sems.at[k], recv_sem=recv_sems.at[k], device_id=to, device_id_type=MESH)

        mine = pltpu.make_async_copy(x_ref, slot(*me), local_sem)
        mine.start()
        first = [copy(0, me, sibling, src=x_ref)]
        first += [copy(1 + j, me, (*chip, c), src=x_ref) for j, chip in enumerate(chips[1:])]
        for cp in first:
            cp.start()
        passed = [copy(4 + j, (*chip, c), sibling) for j, chip in enumerate(chips[1:])]
        for j, chip in enumerate(chips[1:]):
            copy(1 + j, (*chip, c), me).wait_recv()
            passed[j].start()
        copy(0, sibling, me).wait_recv()
        for j, chip in enumerate(chips[1:]):
            copy(4 + j, (*chip, 1 - c), me).wait_recv()
        for cp in first + passed:
            cp.wait_send()
        mine.wait()

    return pl.pallas_call(
        body,
        name=name,
        in_specs=[ANY],
        out_specs=ANY,
        out_shape=_sds((N_DEV, rows, lanes), block.dtype),
        scratch_shapes=[pltpu.SemaphoreType.DMA((7,)), pltpu.SemaphoreType.DMA((7,)), pltpu.SemaphoreType.DMA],
    )(block)


RS_CHUNKS = 4


def _reduce_scatter(parts, name):
    _, rows, lanes = parts.shape
    ch = rows // RS_CHUNKS

    def body(p_ref, out_ref, land_a, send_b, land_b, va, vb, v16, w16, sa, ra, sb, rb):
        x, y, c, chips = _place()
        sibling = (x, y, 1 - c)

        def blk(chip, core):
            return p_ref.at[4 * chip[0] + 2 * chip[1] + core]

        to_sib = [pltpu.make_async_remote_copy(
            src_ref=blk(chips[k], 1 - c), dst_ref=land_a.at[k], send_sem=sa.at[k], recv_sem=ra.at[k],
            device_id=sibling, device_id_type=MESH) for k in range(4)]
        for cp in to_sib:
            cp.start()
        to_chip = [pltpu.make_async_remote_copy(
            src_ref=send_b.at[j], dst_ref=land_b.at[j], send_sem=sb.at[j], recv_sem=rb.at[j],
            device_id=(*chips[1 + j], c), device_id_type=MESH) for j in range(3)]

        for k in (1, 2, 3, 0):
            to_sib[k].wait_recv()

            def chip_sum(i, carry, k=k):
                r = pl.ds(pl.multiple_of(i * ch, 16), ch)
                pltpu.sync_copy(blk(chips[k], c).at[r], va)
                pltpu.sync_copy(land_a.at[k, r], vb)
                if k == 0:
                    va[...] = va[...] + vb[...]
                    pltpu.sync_copy(va, out_ref.at[r])
                else:
                    v16[...] = (va[...] + vb[...]).astype(BF16)
                    pltpu.sync_copy(v16, send_b.at[k - 1, r])
                return carry

            lax.fori_loop(0, RS_CHUNKS, chip_sum, 0)
            if k != 0:
                to_chip[k - 1].start()

        for cp in to_chip:
            cp.wait_recv()

        def final_sum(i, carry):
            r = pl.ds(pl.multiple_of(i * ch, 16), ch)
            pltpu.sync_copy(out_ref.at[r], va)
            acc = va[...]
            for j in range(3):
                pltpu.sync_copy(land_b.at[j, r], w16)
                acc = acc + w16[...].astype(F32)
            va[...] = acc
            pltpu.sync_copy(va, out_ref.at[r])
            return carry

        lax.fori_loop(0, RS_CHUNKS, final_sum, 0)
        for cp in to_sib + to_chip:
            cp.wait_send()

    outs = pl.pallas_call(
        body,
        name=name,
        in_specs=[ANY],
        out_specs=[ANY] * 4,
        out_shape=[_sds((rows, lanes), F32), _sds((4, rows, lanes), F32), _sds((3, rows, lanes), BF16),
                   _sds((3, rows, lanes), BF16)],
        scratch_shapes=[pltpu.VMEM((ch, lanes), F32), pltpu.VMEM((ch, lanes), F32), pltpu.VMEM((ch, lanes), BF16),
                        pltpu.VMEM((ch, lanes), BF16)]
        + [pltpu.SemaphoreType.DMA((4,))] * 2 + [pltpu.SemaphoreType.DMA((3,))] * 2,
    )(parts)
    return outs[0]


def _adamw_math(w, g, m, v):
    m = ADAM_B1 * m + (1.0 - ADAM_B1) * g
    v = ADAM_B2 * v + (1.0 - ADAM_B2) * (g * g)
    m_hat = m / (1.0 - ADAM_B1 ** ADAM_STEP)
    v_hat = v / (1.0 - ADAM_B2 ** ADAM_STEP)
    delta = -ADAM_LR * (m_hat / (jnp.sqrt(v_hat) + ADAM_EPS) + ADAM_WD * w)
    return delta, m, v


def _row_tile(r):
    return max(t for t in range(8, min(r, 256) + 1, 8) if r % t == 0)


def _adamw(w, g, m, v, name):
    r, n = w.shape

    def body(w_ref, g_ref, m_ref, v_ref, d_ref, nm_ref, nv_ref):
        d_ref[...], nm_ref[...], nv_ref[...] = _adamw_math(w_ref[...], g_ref[...], m_ref[...], v_ref[...])

    return _row_call(body, name, r, _row_tile(r), [w, g, m, v], [], [((r, n), F32)] * 3)


def _adamw_sum(landed, w, m, v, name):
    r, n = w.shape

    def body(l_ref, w_ref, m_ref, v_ref, g_ref, d_ref, nm_ref, nv_ref):
        g = l_ref[0].astype(F32)
        for dev in range(1, N_DEV):
            g = g + l_ref[dev].astype(F32)
        g_ref[...] = g
        d_ref[...], nm_ref[...], nv_ref[...] = _adamw_math(w_ref[...], g, m_ref[...], v_ref[...])

    tm = max(t for t in range(16, min(r, 256) + 1, 16) if r % t == 0)
    return _row_call(body, name, r, tm, [landed, w, m, v], [], [((r, n), F32)] * 4)


def _adamw_small(gathered, w, m, v):
    def body(ga_ref, w_ref, m_ref, v_ref, g_ref, d_ref, nm_ref, nv_ref):
        g = ga_ref[0]
        for dev in range(1, N_DEV):
            g = g + ga_ref[dev]
        g_ref[...] = g
        d_ref[...], nm_ref[...], nv_ref[...] = _adamw_math(w_ref[...], g, m_ref[...], v_ref[...])

    return pl.pallas_call(body, name="adamw_small", out_shape=[_sds(w.shape, F32)] * 4, compiler_params=_params())(
        gathered, w, m, v)


SMALL = ("norm_mix", "q_a_norm", "kv_a_norm", "q_norm", "k_norm", "ssm_a_re", "ssm_a_im", "ssm_log_dt", "ssm_b_re",
         "ssm_b_im", "ssm_c_re", "ssm_c_im", "ssm_d", "b_glu", "norm_mlp")
WEIGHT_ORDER = ("norm_mix", "w_in", "q_a_norm", "kv_a_norm", "w_q_b", "w_kv_b", "q_norm", "k_norm", "w_o_mla",
                "ssm_a_re", "ssm_a_im", "ssm_log_dt", "ssm_b_re", "ssm_b_im", "ssm_c_re", "ssm_c_im", "ssm_d", "w_glu",
                "b_glu", "w_o_ssm", "w_out", "norm_mlp", "w_up", "w_down")
IN_SHARD = D_IN // N_DEV
Q_SHARD = QK_HEAD


def _pack_small(vals):
    flat = jnp.concatenate([vals[n].reshape(-1) for n in SMALL])
    rows = -(-flat.shape[0] // (8 * LANES)) * 8
    return jnp.pad(flat, (0, rows * LANES - flat.shape[0])).reshape(rows, LANES)


def _unpack_small(packed, like):
    flat, out, off = packed.reshape(-1), {}, 0
    for n in SMALL:
        size = like[n].size
        out[n] = flat[off:off + size].reshape(like[n].shape)
        off += size
    return out


def _step(x, pos_col, target, w, small):
    bf = {n: a.astype(BF16) for n, a in w.items()}
    gq, gk = _pad_gain(small["q_norm"]), _pad_gain(small["k_norm"])
    a_re = small["ssm_a_re"].reshape(1, N_STATE)
    a_im = small["ssm_a_im"].reshape(1, N_STATE)
    log_dt = jnp.repeat(small["ssm_log_dt"].reshape(SSM_GROUPS), SSM_STATE).reshape(1, N_STATE)
    b_re_x = _block_diag(jnp.transpose(small["ssm_b_re"][0], (0, 2, 1)), SSM_GROUP_CH, SSM_STATE)
    b_im_x = _block_diag(jnp.transpose(small["ssm_b_im"][0], (0, 2, 1)), SSM_GROUP_CH, SSM_STATE)
    c_re_x = _block_diag(jnp.transpose(small["ssm_c_re"][0], (0, 2, 1)), SSM_STATE, SSM_GROUP_CH)
    c_im_x = _block_diag(jnp.transpose(small["ssm_c_im"][0], (0, 2, 1)), SSM_STATE, SSM_GROUP_CH)
    cblk = jnp.concatenate([c_re_x, -c_im_x], axis=0).astype(BF16)
    d_row = small["ssm_d"].reshape(1, SSM_WIDTH)

    w_in_all = _all_gather(bf["w_in"].reshape(-1, LANES), "gather_w_in").reshape(N_DEV, D_MODEL, IN_SHARD)
    w_in_pad = _pad_in(jnp.transpose(w_in_all, (1, 0, 2)).reshape(D_MODEL, D_IN))
    cos_t, sin_t = _rope_tables(pos_col)
    lam, bblk = _ssm_prep(a_re, a_im, log_dt, b_re_x, b_im_x)
    wq_mine = jnp.pad(bf["w_q_b"], ((0, 0), (0, QK_PAD - QK_HEAD)))
    xn, u, ql, kvl, gs, gm, w_glu, w_o_ssm, wq, wkv = _in_proj(
        x, small["norm_mix"], w_in_pad,
        xch=[(bf["w_glu"], False), (bf["w_o_ssm"], False), (wq_mine, False), (bf["w_kv_b"], False)])
    w_glu = w_glu.reshape(SSM_WIDTH, SSM_WIDTH)
    y, y_ssm, st, w_o_mla, w_out = _ssm_fwd(u, bblk, cblk, lam, d_row, w_glu, small["b_glu"], w_o_ssm,
                                            xch=[(bf["w_o_mla"], False), (bf["w_out"], False)])
    w_o_mla, w_out = w_o_mla.reshape(D_MODEL, D_MODEL), w_out.reshape(D_MODEL, D_MODEL)
    q, k, v = _qkv_prep(ql, kvl, small["q_a_norm"], small["kv_a_norm"], wq, wkv, gq, gk, cos_t, sin_t)
    attn, lse, w_up, w_down = _attn_fwd(q, k, v, xch=[(bf["w_up"], False), (bf["w_down"], False)])
    h, mixed, y_mla = _merge(attn, gs, gm, y_ssm, x, w_o_mla, w_out)
    hn, dout, loss = _mlp_fwd_loss(h, target, small["norm_mlp"], w_up, w_down)

    hid, da, dh, d_norm_mlp = _mlp_bwd(dout, hn, h, small["norm_mlp"], w_up, w_down)
    p_w_down = _matmul_tn_shards(hid, dout, "dw_down", False)
    p_w_up = _matmul_tn_shards(hn, da, "dw_up", True)
    dgs, dgm, dy_ssm, dy_mla, dattn = _merge_bwd(dh, gs, gm, y_ssm, y_mla, w_out, w_o_mla)
    p_w_out = _matmul_tn_shards(mixed, dh, "dw_out", False)
    p_w_o_mla = _matmul_tn_shards(attn, dy_mla, "dw_o_mla", False)
    dq, dk, dv, l_w_up, l_w_down = _attn_bwd(q, k, v, attn, lse, dattn, xch=[(p_w_up, True), (p_w_down, True)])
    (dql, dkvl, qa, dq_pre, ca, dkv_pre, d_q_a_norm, d_kv_a_norm, d_gq, d_gk, l_w_out, l_w_o_mla) = _qkv_prep_bwd(
        ql, kvl, dq, dk, dv, small["q_a_norm"], small["kv_a_norm"], wq, wkv, gq, gk, cos_t, sin_t,
        xch=[(p_w_out, True), (p_w_o_mla, True)])
    p_wq = _matmul_tn_shards(qa, dq_pre, "dw_q_b", True)
    p_wkv = _matmul_tn_shards(ca, dkv_pre, "dw_kv_b", True)
    dy, zg, z, dt, d_b_glu = _glu_bwd(dy_ssm, y, w_glu, small["b_glu"], w_o_ssm)
    p_w_o_ssm = _matmul_tn_shards(zg, dy_ssm, "dw_o_ssm", True)
    p_w_glu = _matmul_tn_shards(z, dt, "dw_glu", False)
    du, xs, ads, dlam, d_d, l_wq, l_wkv, l_w_glu, l_w_o_ssm = _ssm_bwd(
        u, dy, st, bblk, cblk, lam, d_row, xch=[(p_wq, True), (p_wkv, True), (p_w_glu, True), (p_w_o_ssm, True)])
    d_bblk = _matmul_tn(u, ads, "d_bblk")
    d_cblk_t = _matmul_tn(dy, xs, "d_cblk")
    d_a_re, d_a_im, d_log_dt, d_b_re_x, d_b_im_x = _ssm_prep_bwd(a_re, a_im, log_dt, b_re_x, b_im_x, dlam, d_bblk)
    dx, dproj, d_norm_mix = _in_proj_bwd((du, dql, dkvl, dgs, dgm), dh, x, small["norm_mix"], w_in_pad)
    g_w_in = _unpad_in(_matmul_tn(xn, dproj, "dw_in"))
    parts = jnp.transpose(g_w_in.reshape(D_MODEL, N_DEV, IN_SHARD), (1, 0, 2)).reshape(N_DEV, -1, LANES)
    g_w_in_mine = _reduce_scatter(parts, "reduce_w_in").reshape(D_MODEL, IN_SHARD)

    tr = lambda mat: jnp.transpose(mat, (0, 2, 1))[None]
    g_small = {
        "norm_mix": d_norm_mix, "q_a_norm": d_q_a_norm, "kv_a_norm": d_kv_a_norm,
        "q_norm": d_gq[:, :QK_HEAD], "k_norm": d_gk[:, :QK_HEAD],
        "ssm_a_re": d_a_re, "ssm_a_im": d_a_im, "ssm_log_dt": d_log_dt,
        "ssm_b_re": tr(_block_diag_extract(d_b_re_x, SSM_GROUP_CH, SSM_STATE)),
        "ssm_b_im": tr(_block_diag_extract(d_b_im_x, SSM_GROUP_CH, SSM_STATE)),
        "ssm_c_re": _block_diag_extract(d_cblk_t[:, :N_STATE], SSM_GROUP_CH, SSM_STATE),
        "ssm_c_im": -_block_diag_extract(d_cblk_t[:, N_STATE:], SSM_GROUP_CH, SSM_STATE),
        "ssm_d": d_d, "b_glu": d_b_glu, "norm_mlp": d_norm_mlp,
    }
    g_small_all = _all_gather(_pack_small(g_small), "gather_small_grads")
    landed = {"w_q_b": l_wq[:, :, :QK_HEAD], "w_kv_b": l_wkv, "w_o_mla": l_w_o_mla, "w_glu": l_w_glu,
              "w_o_ssm": l_w_o_ssm, "w_out": l_w_out, "w_up": l_w_up, "w_down": l_w_down}
    return loss, dx, landed, g_w_in_mine, g_small_all


def kernel(x, positions, norm_mix, w_in, q_a_norm, kv_a_norm, w_q_b, w_kv_b, q_norm, k_norm, w_o_mla, ssm_a_re, ssm_a_im, ssm_log_dt, ssm_b_re, ssm_b_im, ssm_c_re, ssm_c_im, ssm_d, w_glu, b_glu, w_o_ssm, w_out, norm_mlp, w_up, w_down, loss_target, m_norm_mix, m_w_in, m_q_a_norm, m_kv_a_norm, m_w_q_b, m_w_kv_b, m_q_norm, m_k_norm, m_w_o_mla, m_ssm_a_re, m_ssm_a_im, m_ssm_log_dt, m_ssm_b_re, m_ssm_b_im, m_ssm_c_re, m_ssm_c_im, m_ssm_d, m_w_glu, m_b_glu, m_w_o_ssm, m_w_out, m_norm_mlp, m_w_up, m_w_down, v_norm_mix, v_w_in, v_q_a_norm, v_kv_a_norm, v_w_q_b, v_w_kv_b, v_q_norm, v_k_norm, v_w_o_mla, v_ssm_a_re, v_ssm_a_im, v_ssm_log_dt, v_ssm_b_re, v_ssm_b_im, v_ssm_c_re, v_ssm_c_im, v_ssm_d, v_w_glu, v_b_glu, v_w_o_ssm, v_w_out, v_norm_mlp, v_w_up, v_w_down):
    given = dict(locals())
    w = {n: given[n] for n in WEIGHT_ORDER}
    m = {n: given["m_" + n] for n in WEIGHT_ORDER}
    v = {n: given["v_" + n] for n in WEIGHT_ORDER}
    big = [n for n in WEIGHT_ORDER if n not in SMALL]
    small = {n: w[n] for n in SMALL}

    loss, dx, landed, g_w_in, g_small_all = _step(
        x[0], positions.reshape(-1, 1), loss_target[0], {n: w[n][0] for n in big}, small)

    grads, deltas, new_m, new_v = {}, {}, {}, {}
    for n in big:
        if n == "w_in":
            g = g_w_in
            d, nm, nv = _adamw(w[n][0], g, m[n][0], v[n][0], "adamw_" + n)
        else:
            g, d, nm, nv = _adamw_sum(landed[n], w[n][0], m[n][0], v[n][0], "adamw_" + n)
        grads[n], deltas[n], new_m[n], new_v[n] = g[None], d[None], nm[None], nv[None]

    packed = _adamw_small(g_small_all, _pack_small(small), _pack_small({n: m[n] for n in SMALL}),
                          _pack_small({n: v[n] for n in SMALL}))
    for dst, src in zip((grads, deltas, new_m, new_v), packed):
        dst.update(_unpack_small(src, small))

    total = lax.psum(loss[0, 0], ("x", "y", "c"))
    return (total, dx[None], *[grads[n] for n in WEIGHT_ORDER], *[deltas[n] for n in WEIGHT_ORDER],
            *[new_m[n] for n in WEIGHT_ORDER], *[new_v[n] for n in WEIGHT_ORDER])
```

```python
import functools
import math

import numpy as np
import jax
import jax.numpy as jnp
from jax import lax
from jax.experimental import pallas as pl
from jax.experimental.pallas import tpu as pltpu

F32 = jnp.float32
BF16 = jnp.bfloat16

D_MODEL = 1024
SSM_GROUPS = 32
SSM_GROUP_CH = 16
SSM_WIDTH = 512
SSM_STATE = 64
N_STATE = SSM_GROUPS * SSM_STATE
N_HEADS = 8
QK_NOPE = 128
QK_ROPE = 64
QK_HEAD = 192
QK_PAD = 256
V_HEAD = 128
Q_LORA = 384
KV_LORA = 256
KV_LAT_PAD = 384
ROPE_THETA = 10000.0
D_FF = 4096
EPS = 1e-6
ATT_SCALE = QK_HEAD ** -0.5
N_DEV = 8
FF_SHARD = D_FF // N_DEV
OUT_SHARD = D_MODEL // N_DEV

IN_SEGS = ((0, 512), (512, 896), (896, 1280), (1280, 2304), (2304, 3328))
D_IN = 3264
D_IN_PAD = 3328
KV_END = 1216

ADAM_LR = 0.001
ADAM_B1 = 0.9
ADAM_B2 = 0.999
ADAM_EPS = 1e-08
ADAM_WD = 0.01
ADAM_STEP = 10

VMEM_LIMIT = 56 * 1024 * 1024
MESH = pl.DeviceIdType.MESH
ANY = pl.BlockSpec(memory_space=pl.ANY)
LANES = 128

SCAN_T = 256
SUBCHUNKS = 8
SCAN_CG = 512
ATT_T = 512
ROW_T = 256


def _params(sem=None):
    return pltpu.CompilerParams(dimension_semantics=sem, vmem_limit_bytes=VMEM_LIMIT)


def _rows(arr, tm):
    if arr.ndim == 2:
        return pl.BlockSpec((tm, arr.shape[1]), lambda i: (i, 0))
    return pl.BlockSpec((arr.shape[0], tm, arr.shape[2]), lambda i: (0, i, 0))


def _const(arr):
    nd = arr.ndim
    return pl.BlockSpec(arr.shape, lambda *_: (0,) * nd)


def _sds(shape, dtype):
    return jax.ShapeDtypeStruct(shape, dtype)


PEERS = tuple((dx, dy, dc) for dx in (0, 1) for dy in (0, 1) for dc in (0, 1) if (dx, dy, dc) != (0, 0, 0))


def _here():
    x, y, c = lax.axis_index("x"), lax.axis_index("y"), lax.axis_index("c")
    return x, y, c, 4 * x + 2 * y + c


def _xchg_start(scatter, srcs, dsts, send, recv, local):
    x, y, c, me = _here()
    for e, sc in enumerate(scatter):
        src, dst = srcs[e], dsts[e]
        pltpu.make_async_copy(src.at[me] if sc else src, dst.at[me], local.at[e]).start()
        for dx, dy, dc in PEERS:
            px, py, pc = (1 - x if dx else x), (1 - y if dy else y), (1 - c if dc else c)
            pltpu.make_async_remote_copy(
                src_ref=src.at[4 * px + 2 * py + pc] if sc else src, dst_ref=dst.at[me],
                send_sem=send.at[e], recv_sem=recv.at[e], device_id=(px, py, pc), device_id_type=MESH).start()


def _xchg_wait(scatter, srcs, dsts, send, recv, local):
    x, y, c, me = _here()
    for e, sc in enumerate(scatter):
        src, dst = srcs[e], dsts[e]
        pltpu.make_async_copy(src.at[me] if sc else src, dst.at[me], local.at[e]).wait()
        span = dst.at[pl.ds(0, N_DEV - 1)]
        both = pltpu.make_async_remote_copy(src_ref=span, dst_ref=span, send_sem=send.at[e], recv_sem=recv.at[e],
                                            device_id=(x, y, c), device_id_type=MESH)
        both.wait_send()
        both.wait_recv()


def _call(body, name, grid, ins, in_specs, outs, out_specs, scratch=(), xch=()):
    n_in, n_out, ne = len(ins), len(outs), len(xch)
    scatter = [sc for _, sc in xch]
    x_outs = [_sds((N_DEV,) + (a.shape[1:] if sc else a.shape), a.dtype) for a, sc in xch]
    sems = [pltpu.SemaphoreType.DMA((ne,))] * 3 if ne else []

    def wrapped(*refs):
        in_refs, x_src = refs[:n_in], refs[n_in:n_in + ne]
        out_refs = refs[n_in + ne:n_in + ne + n_out]
        x_dst = refs[n_in + ne + n_out:n_in + 2 * ne + n_out]
        rest = refs[n_in + 2 * ne + n_out:]
        if ne:
            x_sems, rest = rest[len(rest) - 3:], rest[:len(rest) - 3]
            first = functools.reduce(jnp.logical_and, [pl.program_id(d) == 0 for d in range(len(grid))])
            last = functools.reduce(jnp.logical_and, [pl.program_id(d) == grid[d] - 1 for d in range(len(grid))])

            @pl.when(first)
            def _():
                _xchg_start(scatter, x_src, x_dst, *x_sems)

        body(*in_refs, *out_refs, *rest)
        if ne:
            @pl.when(last)
            def _():
                _xchg_wait(scatter, x_src, x_dst, *x_sems)

    return pl.pallas_call(
        wrapped,
        name=name,
        grid=grid,
        in_specs=list(in_specs) + [ANY] * ne,
        out_specs=list(out_specs) + [ANY] * ne,
        out_shape=list(outs) + x_outs,
        scratch_shapes=list(scratch) + sems,
        compiler_params=_params(("arbitrary",) * len(grid)),
    )(*ins, *[a for a, _ in xch])


def _row_call(body, name, n_rows, tm, row_ins, const_ins, row_outs, acc_outs=(), xch=()):
    outs = [_sds(s, d) for s, d in row_outs] + [_sds(s, d) for s, d in acc_outs]
    out_specs = [_rows(o, tm) for o in outs[: len(row_outs)]] + [_const(o) for o in outs[len(row_outs):]]
    in_specs = [_rows(a, tm) for a in row_ins] + [_const(a) for a in const_ins]
    return _call(body, name, (n_rows // tm,), list(row_ins) + list(const_ins), in_specs, outs, out_specs, xch=xch)


def _dot(a, b):
    return jnp.dot(a, b, preferred_element_type=F32)


def _dot_nt(a, b):
    return lax.dot_general(a, b, (((1,), (1,)), ((), ())), preferred_element_type=F32)


def _dot_tn(a, b):
    return lax.dot_general(a, b, (((0,), (0,)), ((), ())), preferred_element_type=F32)


def _rms(x, g, n):
    inv = lax.rsqrt(jnp.sum(x * x, -1, keepdims=True) * (1.0 / n) + EPS)
    return x * inv * g, inv


def _rms_bwd(dy, x, g, inv, n):
    xh = x * inv
    dxh = dy * g
    dx = inv * (dxh - xh * (jnp.sum(dxh * xh, -1, keepdims=True) * (1.0 / n)))
    return dx, dy * xh


def _sigmoid(x):
    return 1.0 / (1.0 + jnp.exp(-x))


_GELU_C = math.sqrt(2.0 / math.pi)


def _gelu(y):
    th = jnp.tanh(_GELU_C * (y + 0.044715 * (y * y * y)))
    return 0.5 * y * (1.0 + th), th


def _gelu_grad(y, th):
    return 0.5 * (1.0 + th) + 0.5 * y * (1.0 - th * th) * (_GELU_C * (1.0 + 3.0 * 0.044715 * (y * y)))


def _acc(ref, val):
    @pl.when(pl.program_id(0) == 0)
    def _():
        ref[...] = jnp.zeros_like(ref)

    ref[...] += val


def _tile(n, limit):
    if n <= limit:
        return n
    return max(t for t in range(128, limit + 1, 128) if n % t == 0)


def _matmul_tn(a, b, name, tm=512, tk=512):
    k_dim, m = a.shape
    n = b.shape[1]
    tm, tk = _tile(m, tm), _tile(k_dim, tk)

    def body(a_ref, b_ref, o_ref):
        @pl.when(pl.program_id(1) == 0)
        def _():
            o_ref[...] = jnp.zeros_like(o_ref)

        o_ref[...] += _dot_tn(a_ref[...].astype(BF16), b_ref[...].astype(BF16))

    return pl.pallas_call(
        body,
        name=name,
        grid=(m // tm, k_dim // tk),
        in_specs=[pl.BlockSpec((tk, tm), lambda i, k: (k, i)), pl.BlockSpec((tk, n), lambda i, k: (k, 0))],
        out_specs=pl.BlockSpec((tm, n), lambda i, k: (i, 0)),
        out_shape=_sds((m, n), F32),
        compiler_params=_params(("parallel", "arbitrary")),
    )(a, b)


def _matmul_tn_shards(a, b, name, by_col, tm=512, tk=512):
    k_dim, m = a.shape
    n = b.shape[1]
    tm, tk = _tile(m, tm), _tile(k_dim, tk)
    nk = k_dim // tk
    if by_col:
        r, c = m, n // N_DEV
        out_spec = pl.BlockSpec((N_DEV, tm, c), lambda i, k: (0, i, 0))
    else:
        r, c = m // N_DEV, n
        per = tm // r
        out_spec = pl.BlockSpec((per, r, c), lambda i, k: (i, 0, 0))

    def body(a_ref, b_ref, o_ref, acc_ref):
        k = pl.program_id(1)

        @pl.when(k == 0)
        def _():
            acc_ref[...] = jnp.zeros_like(acc_ref)

        acc_ref[...] += _dot_tn(a_ref[...].astype(BF16), b_ref[...].astype(BF16))

        @pl.when(k == nk - 1)
        def _():
            if by_col:
                for j in range(N_DEV):
                    o_ref[j] = acc_ref[:, j * c:(j + 1) * c].astype(BF16)
            else:
                for s in range(per):
                    o_ref[s] = acc_ref[s * r:(s + 1) * r, :].astype(BF16)

    return pl.pallas_call(
        body,
        name=name,
        grid=(m // tm, nk),
        in_specs=[pl.BlockSpec((tk, tm), lambda i, k: (k, i)), pl.BlockSpec((tk, n), lambda i, k: (k, 0))],
        out_specs=out_spec,
        out_shape=_sds((N_DEV, r, c), BF16),
        scratch_shapes=[pltpu.VMEM((tm, n), F32)],
        compiler_params=_params(("parallel", "arbitrary")),
    )(a, b)


def _rope_tables(pos_col):
    n = pos_col.shape[0]
    half = QK_ROPE // 2
    inv_freq = (ROPE_THETA ** (-np.arange(half, dtype=np.float32) / half)).astype(np.float32)
    freq_row = jnp.asarray(np.concatenate([inv_freq, inv_freq, np.zeros(64, np.float32)])[None, :])

    def body(p_ref, f_ref, c_ref, s_ref):
        ang = p_ref[...].astype(F32) * f_ref[...]
        c_ref[...] = jnp.cos(ang)
        s_ref[...] = jnp.sin(ang)

    return _row_call(body, "rope_tables", n, min(n, 1024), [pos_col], [freq_row], [((n, 128), F32)] * 2)


def _rope_rot(v):
    lane = lax.broadcasted_iota(jnp.int32, v.shape, 1)
    return jnp.where(lane < 32, -pltpu.roll(v, 96, 1), jnp.where(lane < 64, pltpu.roll(v, 32, 1), 0.0))


def _rope_rot_t(v):
    lane = lax.broadcasted_iota(jnp.int32, v.shape, 1)
    return jnp.where(lane < 32, pltpu.roll(v, 96, 1), jnp.where(lane < 64, -pltpu.roll(v, 32, 1), 0.0))


def _in_proj(x, norm_mix, w_in_pad, xch):
    n = x.shape[0]

    def body(x_ref, g_ref, w_ref, xn_ref, u_ref, ql_ref, kvl_ref, gs_ref, gm_ref):
        xn, _ = _rms(x_ref[...], g_ref[...], D_MODEL)
        xb = xn.astype(BF16)
        xn_ref[...] = xb
        for ref, (a, b) in zip((u_ref, ql_ref, kvl_ref, gs_ref, gm_ref), IN_SEGS):
            ref[...] = _dot(xb, w_ref[:, a:b])

    outs = [((n, D_MODEL), BF16)] + [((n, b - a), F32) for a, b in IN_SEGS]
    return _row_call(body, "in_proj", n, ROW_T, [x], [norm_mix, w_in_pad], outs, xch=xch)


def _ssm_prep_fn(a_re, a_im, log_dt, b_re_x, b_im_x):
    dt = jnp.exp(log_dt)
    mag = jnp.exp(a_re * dt)
    lr = mag * jnp.cos(a_im * dt)
    li = mag * jnp.sin(a_im * dt)
    den = a_re * a_re + a_im * a_im
    fr = ((lr - 1.0) * a_re + li * a_im) / den
    fi = (li * a_re - (lr - 1.0) * a_im) / den
    return lr, li, fr * b_re_x - fi * b_im_x, fr * b_im_x + fi * b_re_x


def _ssm_prep(a_re, a_im, log_dt, b_re_x, b_im_x):
    def body(ar, ai, ld, br, bi, lam_ref, bblk_ref):
        lr, li, bbr, bbi = _ssm_prep_fn(ar[...], ai[...], ld[...], br[...], bi[...])
        lam_ref[0:1, :] = lr
        lam_ref[1:2, :] = li
        bblk_ref[:, 0:N_STATE] = bbr.astype(BF16)
        bblk_ref[:, N_STATE:] = bbi.astype(BF16)

    return pl.pallas_call(
        body,
        name="ssm_prep",
        out_shape=[_sds((2, N_STATE), F32), _sds((SSM_WIDTH, 2 * N_STATE), BF16)],
        compiler_params=_params(),
    )(a_re, a_im, log_dt, b_re_x, b_im_x)


def _ssm_prep_bwd(a_re, a_im, log_dt, b_re_x, b_im_x, dlam, dbblk):
    def body(ar, ai, ld, br, bi, dl, db, dar, dai, dld, dbr, dbi):
        _, vjp = jax.vjp(_ssm_prep_fn, ar[...], ai[...], ld[...], br[...], bi[...])
        g = vjp((dl[0:1, :], dl[1:2, :], db[:, 0:N_STATE], db[:, N_STATE:]))
        dar[...] = g[0]
        dai[...] = g[1]
        grp = lax.broadcasted_iota(jnp.int32, (SSM_GROUPS, N_STATE), 0)
        lane = lax.broadcasted_iota(jnp.int32, (SSM_GROUPS, N_STATE), 1)
        sel = (lane // SSM_STATE) == grp
        dld[...] = jnp.sum(jnp.where(sel, jnp.broadcast_to(g[2], (SSM_GROUPS, N_STATE)), 0.0), axis=1, keepdims=True)
        dbr[...] = g[3]
        dbi[...] = g[4]

    return pl.pallas_call(
        body,
        name="ssm_prep_bwd",
        out_shape=[_sds((1, N_STATE), F32), _sds((1, N_STATE), F32), _sds((SSM_GROUPS, 1), F32),
                   _sds((SSM_WIDTH, N_STATE), F32), _sds((SSM_WIDTH, N_STATE), F32)],
        compiler_params=_params(),
    )(a_re, a_im, log_dt, b_re_x, b_im_x, dlam, dbblk)


def _perm_matrix(t):
    run = t // SUBCHUNKS
    p = np.zeros((t, t), np.float32)
    r = np.arange(t)
    p[r, (r % SUBCHUNKS) * run + r // SUBCHUNKS] = 1.0
    return jnp.asarray(p, dtype=BF16)


def _unpermute(p, a):
    hi = a.astype(BF16)
    r1 = a - hi.astype(F32)
    mid = r1.astype(BF16)
    lo = (r1 - mid.astype(F32)).astype(BF16)
    return _dot_tn(p, hi) + _dot_tn(p, mid) + _dot_tn(p, lo)


def _power_table(lam_ref, pw_ref, n):
    lr, li = lam_ref[0:1, :], lam_ref[1:2, :]
    pw_ref[0:1, 0:N_STATE] = lr
    pw_ref[0:1, N_STATE:] = li

    def step(i, carry):
        pr, pi = carry
        pr, pi = pr * lr - pi * li, pr * li + pi * lr
        pw_ref[pl.ds(i, 1), 0:N_STATE] = pr
        pw_ref[pl.ds(i, 1), N_STATE:] = pi
        return pr, pi

    lax.fori_loop(1, n, step, (lr, li))


def _col_groups():
    return [(pl.ds(c, SCAN_CG), pl.ds(N_STATE + c, SCAN_CG)) for c in range(0, N_STATE, SCAN_CG)]


def _run_scan(buf, lam_ref, t, reverse):
    nblk = t // 8
    for re, im in _col_groups():
        lr = jnp.broadcast_to(lam_ref[0:1, re], (8, SCAN_CG))
        li = jnp.broadcast_to(lam_ref[1:2, re], (8, SCAN_CG))
        if reverse:
            li = -li
        first = pl.ds((nblk - 1) * 8 if reverse else 0, 8)

        def step(k, carry, re=re, im=im, lr=lr, li=li):
            pr, pi = carry
            i = (nblk - 2 - k) if reverse else (k + 1)
            r = pl.ds(pl.multiple_of(i * 8, 8), 8)
            xr = buf[r, re] + lr * pr - li * pi
            xi = buf[r, im] + lr * pi + li * pr
            buf[r, re] = xr
            buf[r, im] = xi
            return xr, xi

        lax.fori_loop(0, nblk - 1, step, (buf[first, re], buf[first, im]))


def _run_carries(buf, pw_ref, carry_ref, s_ref, t, reverse):
    nblk = t // 8
    run = t // SUBCHUNKS
    edge = buf[pl.ds(0 if reverse else (nblk - 1) * 8, 8), :]
    pr, pi = pw_ref[run - 1:run, 0:N_STATE], pw_ref[run - 1:run, N_STATE:]
    if reverse:
        pi = -pi
    sr, si = carry_ref[0:1, 0:N_STATE], carry_ref[0:1, N_STATE:]
    for s in (range(SUBCHUNKS - 1, -1, -1) if reverse else range(SUBCHUNKS)):
        s_ref[s:s + 1, 0:N_STATE] = sr
        s_ref[s:s + 1, N_STATE:] = si
        er, ei = edge[s:s + 1, 0:N_STATE], edge[s:s + 1, N_STATE:]
        sr, si = er + pr * sr - pi * si, ei + pr * si + pi * sr
    carry_ref[:, 0:N_STATE] = jnp.broadcast_to(sr, (8, N_STATE))
    carry_ref[:, N_STATE:] = jnp.broadcast_to(si, (8, N_STATE))


def _run_fix(buf, pw_ref, s_ref, t, reverse):
    nblk = t // 8
    for re, im in _col_groups():
        sr, si = s_ref[:, re], s_ref[:, im]

        def step(i, carry, re=re, im=im, sr=sr, si=si):
            r = pl.ds(pl.multiple_of(i * 8, 8), 8)
            row = pl.ds((nblk - 1 - i) if reverse else i, 1)
            pr, pi = pw_ref[row, re], pw_ref[row, im]
            if reverse:
                pi = -pi
            buf[r, re] += pr * sr - pi * si
            buf[r, im] += pr * si + pi * sr
            return carry

        lax.fori_loop(0, nblk, step, 0)


def _ssm_fwd(u, bblk, cblk, lam, d_row, w_glu, b_glu, w_o_ssm, xch):
    n = u.shape[0]
    t = min(SCAN_T, n)
    kb = 512
    perm = _perm_matrix(t)

    def body(u_ref, p_ref, bblk_ref, cblk_ref, lam_ref, d_ref, wg_ref, bg_ref, wo_ref, y_ref, ys_ref, st_ref,
             buf, pw_ref, carry_ref, s_ref):
        @pl.when(pl.program_id(0) == 0)
        def _():
            carry_ref[...] = jnp.zeros_like(carry_ref)
            _power_table(lam_ref, pw_ref, t // SUBCHUNKS)

        st_ref[0] = carry_ref[...]
        u_t = u_ref[...]
        p = p_ref[...]
        ub = _dot(p, u_t.astype(BF16)).astype(BF16)
        for c in range(0, 2 * N_STATE, kb):
            buf[:, c:c + kb] = _dot(ub, bblk_ref[:, c:c + kb])
        _run_scan(buf, lam_ref, t, False)
        _run_carries(buf, pw_ref, carry_ref, s_ref, t, False)
        _run_fix(buf, pw_ref, s_ref, t, False)
        yp = jnp.zeros((t, SSM_WIDTH), F32)
        for c in range(0, 2 * N_STATE, kb):
            yp += _dot(buf[:, c:c + kb].astype(BF16), cblk_ref[c:c + kb, :])
        y = d_ref[...] * u_t + _unpermute(p, yp)
        y_ref[...] = y
        z, _ = _gelu(y)
        s = _sigmoid(_dot(z.astype(BF16), wg_ref[...]) + bg_ref[...])
        zgb = (z * s).astype(BF16)
        for j in range(N_DEV):
            ys_ref[:, j * OUT_SHARD:(j + 1) * OUT_SHARD] = _dot(zgb, wo_ref[j])

    consts = [perm, bblk, cblk, lam, d_row, w_glu, b_glu, w_o_ssm]
    return _call(
        body, "ssm_fwd", (n // t,), [u] + consts, [_rows(u, t)] + [_const(a) for a in consts],
        [_sds((n, SSM_WIDTH), F32), _sds((n, D_MODEL), F32), _sds((n // t, 8, 2 * N_STATE), F32)],
        [pl.BlockSpec((t, SSM_WIDTH), lambda i: (i, 0)), pl.BlockSpec((t, D_MODEL), lambda i: (i, 0)),
         pl.BlockSpec((1, 8, 2 * N_STATE), lambda i: (i, 0, 0))],
        scratch=[pltpu.VMEM((t, 2 * N_STATE), F32), pltpu.VMEM((t // SUBCHUNKS, 2 * N_STATE), F32),
                 pltpu.VMEM((8, 2 * N_STATE), F32), pltpu.VMEM((8, 2 * N_STATE), F32)],
        xch=xch)


def _head_norm_rope(slab, gain, cos_t, sin_t):
    xn, inv = _rms(slab, gain, QK_HEAD)
    lo, hi = xn[:, 0:128], xn[:, 128:256]
    return jnp.concatenate([lo, hi * cos_t + _rope_rot(hi) * sin_t], axis=-1), inv


def _head_norm_rope_bwd(g, slab, gain, inv, cos_t, sin_t):
    g_lo, g_hi = g[:, 0:128], g[:, 128:256]
    g_n = jnp.concatenate([g_lo, g_hi * cos_t + _rope_rot_t(g_hi * sin_t)], axis=-1)
    return _rms_bwd(g_n, slab, gain, inv, QK_HEAD)


def _qkv_prep(ql, kvl, q_a_norm, kv_a_norm, wq, wkv, gq, gk, cos_t, sin_t):
    n = ql.shape[0]

    def body(ql_ref, kvl_ref, cos_ref, sin_ref, qa_ref, ka_ref, wq_ref, wkv_ref, gq_ref, gk_ref, q_ref, k_ref, v_ref):
        cos_t, sin_t = cos_ref[...], sin_ref[...]
        qa, _ = _rms(ql_ref[...], qa_ref[...], Q_LORA)
        qab = qa.astype(BF16)
        kvl_t = kvl_ref[...]
        ca, _ = _rms(kvl_t[:, 0:KV_LORA], ka_ref[...], KV_LORA)
        cab = ca.astype(BF16)
        kpe = kvl_t[:, KV_LORA:KV_LAT_PAD]
        for h in range(N_HEADS):
            qh, _ = _head_norm_rope(_dot(qab, wq_ref[h]), gq_ref[...], cos_t, sin_t)
            q_ref[h] = (qh * ATT_SCALE).astype(BF16)
            kv_h = _dot(cab, wkv_ref[h])
            kh, _ = _head_norm_rope(jnp.concatenate([kv_h[:, 0:QK_NOPE], kpe], axis=-1), gk_ref[...], cos_t, sin_t)
            k_ref[h] = kh.astype(BF16)
            v_ref[h] = kv_h[:, QK_NOPE:].astype(BF16)

    outs = [((N_HEADS, n, QK_PAD), BF16), ((N_HEADS, n, QK_PAD), BF16), ((N_HEADS, n, V_HEAD), BF16)]
    return _row_call(body, "qkv_prep", n, ROW_T, [ql, kvl, cos_t, sin_t], [q_a_norm, kv_a_norm, wq, wkv, gq, gk], outs)


def _causal_mask(s, t):
    row = lax.broadcasted_iota(jnp.int32, (t, t), 0)
    col = lax.broadcasted_iota(jnp.int32, (t, t), 1)
    return jnp.where(col <= row, s, -jnp.inf)


def _attn_fwd(q, k, v, xch):
    n = q.shape[1]
    t = min(ATT_T, n)

    def body(q_ref, k_ref, v_ref, o_ref, lse_ref):
        i = pl.program_id(1)
        qt = q_ref[0]

        def kv_tile(j, carry, diag):
            m, l, acc = carry
            r0 = pl.multiple_of(j * t, t)
            s = _dot_nt(qt, k_ref[0, pl.ds(r0, t), :])
            if diag:
                s = _causal_mask(s, t)
            m_new = jnp.maximum(m, jnp.max(s, -1, keepdims=True))
            alpha = jnp.exp(m - m_new)
            p = jnp.exp(s - m_new)
            l = alpha * l + jnp.sum(p, -1, keepdims=True)
            acc = alpha * acc + _dot(p.astype(BF16), v_ref[0, pl.ds(r0, t), :])
            return m_new, l, acc

        init = (jnp.full((t, 1), -jnp.inf, F32), jnp.zeros((t, 1), F32), jnp.zeros((t, V_HEAD), F32))
        carry = lax.fori_loop(0, i, functools.partial(kv_tile, diag=False), init)
        m, l, acc = kv_tile(i, carry, True)
        o_ref[...] = acc / l
        lse_ref[0] = m + jnp.log(l)

    return _call(
        body, "attn_fwd", (N_HEADS, n // t), [q, k, v],
        [pl.BlockSpec((1, t, QK_PAD), lambda h, i: (h, i, 0)), pl.BlockSpec((1, n, QK_PAD), lambda h, i: (h, 0, 0)),
         pl.BlockSpec((1, n, V_HEAD), lambda h, i: (h, 0, 0))],
        [_sds((n, N_HEADS * V_HEAD), F32), _sds((N_HEADS, n, 1), F32)],
        [pl.BlockSpec((t, V_HEAD), lambda h, i: (i, h)), pl.BlockSpec((1, t, 1), lambda h, i: (h, i, 0))],
        xch=xch)


def _merge(attn, gs, gm, y_ssm, x, w_o_mla, w_out):
    n = x.shape[0]

    def body(at_ref, gs_ref, gm_ref, ys_ref, x_ref, wo_ref, wout_ref, h_ref, mx_ref, ym_ref):
        y_mla = _dot(at_ref[...].astype(BF16), wo_ref[...])
        ym_ref[...] = y_mla
        mixed = (_sigmoid(gs_ref[...]) * ys_ref[...] + _sigmoid(gm_ref[...]) * y_mla).astype(BF16)
        mx_ref[...] = mixed
        h_ref[...] = x_ref[...] + _dot(mixed, wout_ref[...])

    outs = [((n, D_MODEL), F32), ((n, D_MODEL), BF16), ((n, D_MODEL), F32)]
    return _row_call(body, "merge", n, ROW_T, [attn, gs, gm, y_ssm, x], [w_o_mla, w_out], outs)


def _mlp_fwd_loss(h, target, norm_mlp, w_up, w_down):
    n = h.shape[0]

    def body(h_ref, t_ref, g_ref, wu_ref, wd_ref, hn_ref, do_ref, loss_ref):
        h_t = h_ref[...]
        hn, _ = _rms(h_t, g_ref[...], D_MODEL)
        hb = hn.astype(BF16)
        hn_ref[...] = hb
        out = h_t
        for j in range(N_DEV):
            a = jnp.maximum(_dot(hb, wu_ref[j]), 0.0)
            out += _dot((a * a).astype(BF16), wd_ref[j])
        err = out - t_ref[...]
        do_ref[...] = err * (1.0 / D_MODEL)
        _acc(loss_ref, jnp.broadcast_to(jnp.sum(err * err) * (0.5 / D_MODEL), loss_ref.shape))

    outs = [((n, D_MODEL), BF16), ((n, D_MODEL), F32)]
    return _row_call(body, "mlp_fwd_loss", n, ROW_T, [h, target], [norm_mlp, w_up, w_down], outs, [((8, 128), F32)])


def _mlp_bwd(dout, hn, h, norm_mlp, w_up, w_down):
    n = h.shape[0]

    def body(do_ref, hn_ref, h_ref, g_ref, wu_ref, wd_ref, hid_ref, da_ref, dh_ref, dg_ref):
        dout_t = do_ref[...]
        doutb = dout_t.astype(BF16)
        hb = hn_ref[...]
        dhn = jnp.zeros_like(dout_t)
        for j in range(N_DEV):
            cols = slice(j * FF_SHARD, (j + 1) * FF_SHARD)
            a = jnp.maximum(_dot(hb, wu_ref[j]), 0.0)
            hid_ref[:, cols] = (a * a).astype(BF16)
            da = (_dot_nt(doutb, wd_ref[j]) * (2.0 * a)).astype(BF16)
            da_ref[:, cols] = da
            dhn += _dot_nt(da, wu_ref[j])
        h_t = h_ref[...]
        inv = lax.rsqrt(jnp.sum(h_t * h_t, -1, keepdims=True) * (1.0 / D_MODEL) + EPS)
        dx, dg = _rms_bwd(dhn, h_t, g_ref[...], inv, D_MODEL)
        dh_ref[...] = dout_t + dx
        _acc(dg_ref, jnp.sum(dg, 0, keepdims=True))

    outs = [((n, D_FF), BF16), ((n, D_FF), BF16), ((n, D_MODEL), F32)]
    return _row_call(body, "mlp_bwd", n, ROW_T, [dout, hn, h], [norm_mlp, w_up, w_down], outs, [((1, D_MODEL), F32)])


def _merge_bwd(dh, gs, gm, y_ssm, y_mla, w_out, w_o_mla):
    n = dh.shape[0]

    def body(dh_ref, gs_ref, gm_ref, ys_ref, ym_ref, wout_ref, wo_ref, dgs_ref, dgm_ref, dys_ref, dym_ref, dat_ref):
        dmix = _dot_nt(dh_ref[...].astype(BF16), wout_ref[...])
        sgs, sgm = _sigmoid(gs_ref[...]), _sigmoid(gm_ref[...])
        dgs_ref[...] = (dmix * ys_ref[...] * sgs * (1.0 - sgs)).astype(BF16)
        dgm_ref[...] = (dmix * ym_ref[...] * sgm * (1.0 - sgm)).astype(BF16)
        dys_ref[...] = (dmix * sgs).astype(BF16)
        dym = (dmix * sgm).astype(BF16)
        dym_ref[...] = dym
        dat_ref[...] = _dot_nt(dym, wo_ref[...])

    outs = [((n, D_MODEL), BF16)] * 4 + [((n, D_MODEL), F32)]
    return _row_call(body, "merge_bwd", n, ROW_T, [dh, gs, gm, y_ssm, y_mla], [w_out, w_o_mla], outs)


def _attn_bwd(q, k, v, out, lse, dout, xch):
    n = q.shape[1]
    t = min(ATT_T, n)
    nt = n // t

    def body(q_ref, k_ref, v_ref, o_ref, lse_ref, do_ref, dq_ref, dk_ref, dv_ref, delta_ref):
        j = pl.program_id(1)

        @pl.when(j == 0)
        def _():
            dq_ref[...] = jnp.zeros_like(dq_ref)
            delta_ref[...] = jnp.sum(do_ref[...] * o_ref[...], -1, keepdims=True)

        kt = k_ref[0]
        vt = v_ref[0]

        def q_tile(i, carry, diag):
            dk, dv = carry
            r0 = pl.multiple_of(i * t, t)
            rows = pl.ds(r0, t)
            qt = q_ref[0, rows, :]
            s = _dot_nt(qt, kt)
            if diag:
                s = _causal_mask(s, t)
            p = jnp.exp(s - lse_ref[0, rows, :])
            dot = do_ref[rows, :].astype(BF16)
            dv = dv + _dot_tn(p.astype(BF16), dot)
            ds = (p * (_dot_nt(dot, vt) - delta_ref[rows, :])).astype(BF16)
            dk = dk + _dot_tn(ds, qt)
            dq_ref[0, rows, :] += _dot(ds, kt)
            return dk, dv

        carry = q_tile(j, (jnp.zeros((t, QK_PAD), F32), jnp.zeros((t, V_HEAD), F32)), True)
        dk, dv = lax.fori_loop(j + 1, nt, functools.partial(q_tile, diag=False), carry)
        dk_ref[0] = dk
        dv_ref[0] = dv

    return _call(
        body, "attn_bwd", (N_HEADS, nt), [q, k, v, out, lse, dout],
        [pl.BlockSpec((1, n, QK_PAD), lambda h, j: (h, 0, 0)), pl.BlockSpec((1, t, QK_PAD), lambda h, j: (h, j, 0)),
         pl.BlockSpec((1, t, V_HEAD), lambda h, j: (h, j, 0)), pl.BlockSpec((n, V_HEAD), lambda h, j: (0, h)),
         pl.BlockSpec((1, n, 1), lambda h, j: (h, 0, 0)), pl.BlockSpec((n, V_HEAD), lambda h, j: (0, h))],
        [_sds((N_HEADS, n, QK_PAD), F32), _sds((N_HEADS, n, QK_PAD), F32), _sds((N_HEADS, n, V_HEAD), F32)],
        [pl.BlockSpec((1, n, QK_PAD), lambda h, j: (h, 0, 0)), pl.BlockSpec((1, t, QK_PAD), lambda h, j: (h, j, 0)),
         pl.BlockSpec((1, t, V_HEAD), lambda h, j: (h, j, 0))],
        scratch=[pltpu.VMEM((n, 1), F32)],
        xch=xch)


def _qkv_prep_bwd(ql, kvl, dq, dk, dv, q_a_norm, kv_a_norm, wq, wkv, gq, gk, cos_t, sin_t, xch):
    n = ql.shape[0]

    def body(ql_ref, kvl_ref, cos_ref, sin_ref, dq_ref, dk_ref, dv_ref, qa_ref, ka_ref, wq_ref, wkv_ref, gq_ref, gk_ref,
             dql_ref, dkvl_ref, qab_ref, dqp_ref, cab_ref, dkvp_ref, dqa_ref, dka_ref, dgq_ref, dgk_ref):
        cos_t, sin_t = cos_ref[...], sin_ref[...]
        ql_t = ql_ref[...]
        qa, inv_qa = _rms(ql_t, qa_ref[...], Q_LORA)
        qab = qa.astype(BF16)
        qab_ref[...] = qab
        kvl_t = kvl_ref[...]
        ckv = kvl_t[:, 0:KV_LORA]
        ca, inv_ca = _rms(ckv, ka_ref[...], KV_LORA)
        cab = ca.astype(BF16)
        cab_ref[...] = cab
        kpe = kvl_t[:, KV_LORA:KV_LAT_PAD]
        dgq = jnp.zeros((1, QK_PAD), F32)
        dgk = jnp.zeros((1, QK_PAD), F32)
        dkpe = jnp.zeros_like(kpe)
        dqa = jnp.zeros_like(ql_t)
        dca = jnp.zeros_like(ckv)
        for h in range(N_HEADS):
            q_slab = _dot(qab, wq_ref[h])
            inv = lax.rsqrt(jnp.sum(q_slab * q_slab, -1, keepdims=True) * (1.0 / QK_HEAD) + EPS)
            d_slab, dg = _head_norm_rope_bwd(dq_ref[h] * ATT_SCALE, q_slab, gq_ref[...], inv, cos_t, sin_t)
            dqp = d_slab.astype(BF16)
            dqp_ref[:, h * QK_PAD:(h + 1) * QK_PAD] = dqp
            dqa += _dot_nt(dqp, wq_ref[h])
            dgq += jnp.sum(dg, 0, keepdims=True)
            kv_h = _dot(cab, wkv_ref[h])
            k_slab = jnp.concatenate([kv_h[:, 0:QK_NOPE], kpe], axis=-1)
            inv = lax.rsqrt(jnp.sum(k_slab * k_slab, -1, keepdims=True) * (1.0 / QK_HEAD) + EPS)
            d_slab, dg = _head_norm_rope_bwd(dk_ref[h], k_slab, gk_ref[...], inv, cos_t, sin_t)
            dkvp = jnp.concatenate([d_slab[:, 0:QK_NOPE], dv_ref[h]], axis=-1).astype(BF16)
            dkvp_ref[:, h * QK_PAD:(h + 1) * QK_PAD] = dkvp
            dca += _dot_nt(dkvp, wkv_ref[h])
            dkpe += d_slab[:, QK_NOPE:QK_PAD]
            dgk += jnp.sum(dg, 0, keepdims=True)
        dx, dg = _rms_bwd(dqa, ql_t, qa_ref[...], inv_qa, Q_LORA)
        dql_ref[...] = dx.astype(BF16)
        _acc(dqa_ref, jnp.sum(dg, 0, keepdims=True))
        dx, dg = _rms_bwd(dca, ckv, ka_ref[...], inv_ca, KV_LORA)
        dkvl_ref[:, 0:KV_LORA] = dx.astype(BF16)
        dkvl_ref[:, KV_LORA:KV_LAT_PAD] = dkpe.astype(BF16)
        _acc(dka_ref, jnp.sum(dg, 0, keepdims=True))
        _acc(dgq_ref, dgq)
        _acc(dgk_ref, dgk)

    row_outs = [((n, Q_LORA), BF16), ((n, KV_LAT_PAD), BF16), ((n, Q_LORA), BF16), ((n, N_HEADS * QK_PAD), BF16),
                ((n, KV_LORA), BF16), ((n, N_HEADS * (QK_NOPE + V_HEAD)), BF16)]
    acc_outs = [((1, Q_LORA), F32), ((1, KV_LORA), F32), ((1, QK_PAD), F32), ((1, QK_PAD), F32)]
    return _row_call(body, "qkv_prep_bwd", n, ROW_T, [ql, kvl, cos_t, sin_t, dq, dk, dv],
                     [q_a_norm, kv_a_norm, wq, wkv, gq, gk], row_outs, acc_outs, xch=xch)


def _glu_bwd(dy_ssm, y, w_glu, b_glu, w_o_ssm):
    n = y.shape[0]

    def body(dys_ref, y_ref, wg_ref, bg_ref, wo_ref, dy_ref, zg_ref, z_ref, dt_ref, db_ref):
        y_t = y_ref[...]
        z, th = _gelu(y_t)
        zb = z.astype(BF16)
        z_ref[...] = zb
        s = _sigmoid(_dot(zb, wg_ref[...]) + bg_ref[...])
        zg_ref[...] = (z * s).astype(BF16)
        dys = dys_ref[...]
        dzg = jnp.zeros_like(y_t)
        for j in range(N_DEV):
            dzg += _dot_nt(dys[:, j * OUT_SHARD:(j + 1) * OUT_SHARD], wo_ref[j])
        dt = dzg * z * s * (1.0 - s)
        dtb = dt.astype(BF16)
        dt_ref[...] = dtb
        dz = dzg * s + _dot_nt(dtb, wg_ref[...])
        dy_ref[...] = dz * _gelu_grad(y_t, th)
        _acc(db_ref, jnp.sum(dt, 0, keepdims=True))

    outs = [((n, SSM_WIDTH), F32)] + [((n, SSM_WIDTH), BF16)] * 3
    return _row_call(body, "glu_bwd", n, ROW_T, [dy_ssm, y], [w_glu, b_glu, w_o_ssm], outs, [((1, SSM_WIDTH), F32)])


def _ssm_bwd(u, dy, st, bblk, cblk, lam, d_row, xch):
    n = u.shape[0]
    t = min(SCAN_T, n)
    nc = n // t
    kb = 512
    perm = _perm_matrix(t)

    def body(u_ref, dy_ref, st_ref, p_ref, bblk_ref, cblk_ref, lam_ref, d_ref,
             du_ref, xs_ref, as_ref, up_ref, dyp_ref, dlam_ref, dd_ref,
             buf_x, buf_a, pw_ref, carry_ref, xcarry_ref, sx_ref, sa_ref):
        @pl.when(pl.program_id(0) == 0)
        def _():
            carry_ref[...] = jnp.zeros_like(carry_ref)
            _power_table(lam_ref, pw_ref, t // SUBCHUNKS)

        u_t = u_ref[...]
        dy_t = dy_ref[...]
        p = p_ref[...]
        ub = _dot(p, u_t.astype(BF16)).astype(BF16)
        dyb = _dot(p, dy_t.astype(BF16)).astype(BF16)
        up_ref[...] = ub
        dyp_ref[...] = dyb
        for c in range(0, 2 * N_STATE, kb):
            buf_x[:, c:c + kb] = _dot(ub, bblk_ref[:, c:c + kb])
        xcarry_ref[...] = st_ref[0]
        _run_scan(buf_x, lam_ref, t, False)
        _run_carries(buf_x, pw_ref, xcarry_ref, sx_ref, t, False)
        _run_fix(buf_x, pw_ref, sx_ref, t, False)
        for c in range(0, 2 * N_STATE, kb):
            buf_a[:, c:c + kb] = _dot_nt(dyb, cblk_ref[c:c + kb, :])
        _run_scan(buf_a, lam_ref, t, True)
        _run_carries(buf_a, pw_ref, carry_ref, sa_ref, t, True)
        _run_fix(buf_a, pw_ref, sa_ref, t, True)
        dup = jnp.zeros((t, SSM_WIDTH), F32)
        for c in range(0, 2 * N_STATE, kb):
            adb = buf_a[:, c:c + kb].astype(BF16)
            as_ref[:, c:c + kb] = adb
            xs_ref[:, c:c + kb] = buf_x[:, c:c + kb].astype(BF16)
            dup += _dot_nt(adb, bblk_ref[:, c:c + kb])
        du_ref[...] = (d_ref[...] * dy_t + _unpermute(p, dup)).astype(BF16)
        for c in range(0, N_STATE, kb):
            re, im = pl.ds(c, kb), pl.ds(N_STATE + c, kb)
            xr, xi = buf_x[pl.ds(0, t - 8), re], buf_x[pl.ds(0, t - 8), im]
            ar, ai = buf_a[pl.ds(8, t - 8), re], buf_a[pl.ds(8, t - 8), im]
            x0r, x0i = sx_ref[:, re], sx_ref[:, im]
            a0r, a0i = buf_a[0:8, re], buf_a[0:8, im]
            dlam_part_re = (jnp.sum(ar * xr + ai * xi, 0, keepdims=True)
                            + jnp.sum(a0r * x0r + a0i * x0i, 0, keepdims=True))
            dlam_part_im = (jnp.sum(ai * xr - ar * xi, 0, keepdims=True)
                            + jnp.sum(a0i * x0r - a0r * x0i, 0, keepdims=True))

            @pl.when(pl.program_id(0) == 0)
            def _(c=c):
                dlam_ref[0:1, c:c + kb] = jnp.zeros((1, kb), F32)
                dlam_ref[1:2, c:c + kb] = jnp.zeros((1, kb), F32)

            dlam_ref[0:1, c:c + kb] += dlam_part_re
            dlam_ref[1:2, c:c + kb] += dlam_part_im
        _acc(dd_ref, jnp.sum(dy_t * u_t, 0, keepdims=True))

    rev = lambda i: (nc - 1 - i, 0)
    consts = [perm, bblk, cblk, lam, d_row]
    return _call(
        body, "ssm_bwd", (nc,), [u, dy, st] + consts,
        [pl.BlockSpec((t, SSM_WIDTH), rev), pl.BlockSpec((t, SSM_WIDTH), rev),
         pl.BlockSpec((1, 8, 2 * N_STATE), lambda i: (nc - 1 - i, 0, 0))] + [_const(a) for a in consts],
        [_sds((n, SSM_WIDTH), BF16), _sds((n, 2 * N_STATE), BF16), _sds((n, 2 * N_STATE), BF16),
         _sds((n, SSM_WIDTH), BF16), _sds((n, SSM_WIDTH), BF16), _sds((2, N_STATE), F32), _sds((1, SSM_WIDTH), F32)],
        [pl.BlockSpec((t, SSM_WIDTH), rev), pl.BlockSpec((t, 2 * N_STATE), rev), pl.BlockSpec((t, 2 * N_STATE), rev),
         pl.BlockSpec((t, SSM_WIDTH), rev), pl.BlockSpec((t, SSM_WIDTH), rev),
         pl.BlockSpec((2, N_STATE), lambda i: (0, 0)), pl.BlockSpec((1, SSM_WIDTH), lambda i: (0, 0))],
        scratch=[pltpu.VMEM((t, 2 * N_STATE), F32)] * 2 + [pltpu.VMEM((t // SUBCHUNKS, 2 * N_STATE), F32)]
        + [pltpu.VMEM((8, 2 * N_STATE), F32)] * 4,
        xch=xch)


def _in_proj_bwd(pieces, dh, x, norm_mix, w_in_pad):
    n = x.shape[0]

    def body(du_ref, dql_ref, dkvl_ref, dgs_ref, dgm_ref, dh_ref, x_ref, g_ref, w_ref, dx_ref, dp_ref, dg_ref):
        dxn = jnp.zeros((dh_ref.shape[0], D_MODEL), F32)
        for ref, (a, b) in zip((du_ref, dql_ref, dkvl_ref, dgs_ref, dgm_ref), IN_SEGS):
            piece = ref[...]
            dp_ref[:, a:b] = piece
            dxn += _dot_nt(piece, w_ref[:, a:b])
        x_t = x_ref[...]
        inv = lax.rsqrt(jnp.sum(x_t * x_t, -1, keepdims=True) * (1.0 / D_MODEL) + EPS)
        dx, dg = _rms_bwd(dxn, x_t, g_ref[...], inv, D_MODEL)
        dx_ref[...] = dh_ref[...] + dx
        _acc(dg_ref, jnp.sum(dg, 0, keepdims=True))

    outs = [((n, D_MODEL), F32), ((n, D_IN_PAD), BF16)]
    return _row_call(body, "in_proj_bwd", n, ROW_T, list(pieces) + [dh, x], [norm_mix, w_in_pad], outs,
                     [((1, D_MODEL), F32)])


def _block_diag(a, rows_per_group, cols_per_group):
    eye = jnp.eye(SSM_GROUPS, dtype=a.dtype)
    return (a[:, :, None, :] * eye[:, None, :, None]).reshape(SSM_GROUPS * rows_per_group, SSM_GROUPS * cols_per_group)


def _block_diag_extract(m, rows_per_group, cols_per_group):
    m4 = m.reshape(SSM_GROUPS, rows_per_group, SSM_GROUPS, cols_per_group)
    eye = jnp.eye(SSM_GROUPS, dtype=m.dtype)
    return jnp.sum(m4 * eye[:, None, :, None], axis=2)


def _pad_in(w):
    return jnp.concatenate([w[:, :KV_END], jnp.zeros((w.shape[0], D_IN_PAD - D_IN), w.dtype), w[:, KV_END:]], axis=1)


def _unpad_in(w):
    return jnp.concatenate([w[:, :KV_END], w[:, KV_END + D_IN_PAD - D_IN:]], axis=1)


def _pad_gain(g):
    return jnp.pad(g, ((0, 0), (0, QK_PAD - QK_HEAD)))


def _place():
    x, y, c = lax.axis_index("x"), lax.axis_index("y"), lax.axis_index("c")
    chips = [(x, y), (1 - x, y), (x, 1 - y), (1 - x, 1 - y)]
    return x, y, c, chips


def _all_gather(block, name):
    rows, lanes = block.shape

    def body(x_ref, out_ref, send_sems, recv_sems, local_sem):
        x, y, c, chips = _place()
        me, sibling = (x, y, c), (x, y, 1 - c)

        def slot(px, py, pc):
            return out_ref.at[4 * px + 2 * py + pc]

        def copy(k, blk, to, src=None):
            return pltpu.make_async_remote_copy(
                src_ref=slot(*blk) if src is None else src, dst_ref=slot(*blk),
                send_sem=send_sems.at[k], recv_sem=recv_sems.at[k], device_id=to, device_id_type=MESH)

        mine = pltpu.make_async_copy(x_ref, slot(*me), local_sem)
        mine.start()
        first = [copy(0, me, sibling, src=x_ref)]
        first += [copy(1 + j, me, (*chip, c), src=x_ref) for j, chip in enumerate(chips[1:])]
        for cp in first:
            cp.start()
        passed = [copy(4 + j, (*chip, c), sibling) for j, chip in enumerate(chips[1:])]
        for j, chip in enumerate(chips[1:]):
            copy(1 + j, (*chip, c), me).wait_recv()
            passed[j].start()
        copy(0, sibling, me).wait_recv()
        for j, chip in enumerate(chips[1:]):
            copy(4 + j, (*chip, 1 - c), me).wait_recv()
        for cp in first + passed:
            cp.wait_send()
        mine.wait()

    return pl.pallas_call(
        body,
        name=name,
        in_specs=[ANY],
        out_specs=ANY,
        out_shape=_sds((N_DEV, rows, lanes), block.dtype),
        scratch_shapes=[pltpu.SemaphoreType.DMA((7,)), pltpu.SemaphoreType.DMA((7,)), pltpu.SemaphoreType.DMA],
    )(block)


RS_CHUNKS = 4


def _reduce_scatter(parts, name):
    _, rows, lanes = parts.shape
    ch = rows // RS_CHUNKS

    def body(p_ref, out_ref, land_a, send_b, land_b, va, vb, v16, w16, sa, ra, sb, rb):
        x, y, c, chips = _place()
        sibling = (x, y, 1 - c)

        def blk(chip, core):
            return p_ref.at[4 * chip[0] + 2 * chip[1] + core]

        to_sib = [pltpu.make_async_remote_copy(
            src_ref=blk(chips[k], 1 - c), dst_ref=land_a.at[k], send_sem=sa.at[k], recv_sem=ra.at[k],
            device_id=sibling, device_id_type=MESH) for k in range(4)]
        for cp in to_sib:
            cp.start()
        to_chip = [pltpu.make_async_remote_copy(
            src_ref=send_b.at[j], dst_ref=land_b.at[j], send_sem=sb.at[j], recv_sem=rb.at[j],
            device_id=(*chips[1 + j], c), device_id_type=MESH) for j in range(3)]

        for k in (1, 2, 3, 0):
            to_sib[k].wait_recv()

            def chip_sum(i, carry, k=k):
                r = pl.ds(pl.multiple_of(i * ch, 16), ch)
                pltpu.sync_copy(blk(chips[k], c).at[r], va)
                pltpu.sync_copy(land_a.at[k, r], vb)
                if k == 0:
                    va[...] = va[...] + vb[...]
                    pltpu.sync_copy(va, out_ref.at[r])
                else:
                    v16[...] = (va[...] + vb[...]).astype(BF16)
                    pltpu.sync_copy(v16, send_b.at[k - 1, r])
                return carry

            lax.fori_loop(0, RS_CHUNKS, chip_sum, 0)
            if k != 0:
                to_chip[k - 1].start()

        for cp in to_chip:
            cp.wait_recv()

        def final_sum(i, carry):
            r = pl.ds(pl.multiple_of(i * ch, 16), ch)
            pltpu.sync_copy(out_ref.at[r], va)
            acc = va[...]
            for j in range(3):
                pltpu.sync_copy(land_b.at[j, r], w16)
                acc = acc + w16[...].astype(F32)
            va[...] = acc
            pltpu.sync_copy(va, out_ref.at[r])
            return carry

        lax.fori_loop(0, RS_CHUNKS, final_sum, 0)
        for cp in to_sib + to_chip:
            cp.wait_send()

    outs = pl.pallas_call(
        body,
        name=name,
        in_specs=[ANY],
        out_specs=[ANY] * 4,
        out_shape=[_sds((rows, lanes), F32), _sds((4, rows, lanes), F32), _sds((3, rows, lanes), BF16),
                   _sds((3, rows, lanes), BF16)],
        scratch_shapes=[pltpu.VMEM((ch, lanes), F32), pltpu.VMEM((ch, lanes), F32), pltpu.VMEM((ch, lanes), BF16),
                        pltpu.VMEM((ch, lanes), BF16)]
        + [pltpu.SemaphoreType.DMA((4,))] * 2 + [pltpu.SemaphoreType.DMA((3,))] * 2,
    )(parts)
    return outs[0]


def _adamw_math(w, g, m, v):
    m = ADAM_B1 * m + (1.0 - ADAM_B1) * g
    v = ADAM_B2 * v + (1.0 - ADAM_B2) * (g * g)
    m_hat = m / (1.0 - ADAM_B1 ** ADAM_STEP)
    v_hat = v / (1.0 - ADAM_B2 ** ADAM_STEP)
    delta = -ADAM_LR * (m_hat / (jnp.sqrt(v_hat) + ADAM_EPS) + ADAM_WD * w)
    return delta, m, v


def _row_tile(r):
    return max(t for t in range(8, min(r, 256) + 1, 8) if r % t == 0)


def _adamw(w, g, m, v, name):
    r, n = w.shape

    def body(w_ref, g_ref, m_ref, v_ref, d_ref, nm_ref, nv_ref):
        d_ref[...], nm_ref[...], nv_ref[...] = _adamw_math(w_ref[...], g_ref[...], m_ref[...], v_ref[...])

    return _row_call(body, name, r, _row_tile(r), [w, g, m, v], [], [((r, n), F32)] * 3)


def _adamw_sum(landed, w, m, v, name):
    r, n = w.shape

    def body(l_ref, w_ref, m_ref, v_ref, g_ref, d_ref, nm_ref, nv_ref):
        g = l_ref[0].astype(F32)
        for dev in range(1, N_DEV):
            g = g + l_ref[dev].astype(F32)
        g_ref[...] = g
        d_ref[...], nm_ref[...], nv_ref[...] = _adamw_math(w_ref[...], g, m_ref[...], v_ref[...])

    tm = max(t for t in range(16, min(r, 256) + 1, 16) if r % t == 0)
    return _row_call(body, name, r, tm, [landed, w, m, v], [], [((r, n), F32)] * 4)


def _adamw_small(gathered, w, m, v):
    def body(ga_ref, w_ref, m_ref, v_ref, g_ref, d_ref, nm_ref, nv_ref):
        g = ga_ref[0]
        for dev in range(1, N_DEV):
            g = g + ga_ref[dev]
        g_ref[...] = g
        d_ref[...], nm_ref[...], nv_ref[...] = _adamw_math(w_ref[...], g, m_ref[...], v_ref[...])

    return pl.pallas_call(body, name="adamw_small", out_shape=[_sds(w.shape, F32)] * 4, compiler_params=_params())(
        gathered, w, m, v)


SMALL = ("norm_mix", "q_a_norm", "kv_a_norm", "q_norm", "k_norm", "ssm_a_re", "ssm_a_im", "ssm_log_dt", "ssm_b_re",
         "ssm_b_im", "ssm_c_re", "ssm_c_im", "ssm_d", "b_glu", "norm_mlp")
WEIGHT_ORDER = ("norm_mix", "w_in", "q_a_norm", "kv_a_norm", "w_q_b", "w_kv_b", "q_norm", "k_norm", "w_o_mla",
                "ssm_a_re", "ssm_a_im", "ssm_log_dt", "ssm_b_re", "ssm_b_im", "ssm_c_re", "ssm_c_im", "ssm_d", "w_glu",
                "b_glu", "w_o_ssm", "w_out", "norm_mlp", "w_up", "w_down")
IN_SHARD = D_IN // N_DEV
Q_SHARD = QK_HEAD


def _pack_small(vals):
    flat = jnp.concatenate([vals[n].reshape(-1) for n in SMALL])
    rows = -(-flat.shape[0] // (8 * LANES)) * 8
    return jnp.pad(flat, (0, rows * LANES - flat.shape[0])).reshape(rows, LANES)


def _unpack_small(packed, like):
    flat, out, off = packed.reshape(-1), {}, 0
    for n in SMALL:
        size = like[n].size
        out[n] = flat[off:off + size].reshape(like[n].shape)
        off += size
    return out


def _step(x, pos_col, target, w, small):
    bf = {n: a.astype(BF16) for n, a in w.items()}
    gq, gk = _pad_gain(small["q_norm"]), _pad_gain(small["k_norm"])
    a_re = small["ssm_a_re"].reshape(1, N_STATE)
    a_im = small["ssm_a_im"].reshape(1, N_STATE)
    log_dt = jnp.repeat(small["ssm_log_dt"].reshape(SSM_GROUPS), SSM_STATE).reshape(1, N_STATE)
    b_re_x = _block_diag(jnp.transpose(small["ssm_b_re"][0], (0, 2, 1)), SSM_GROUP_CH, SSM_STATE)
    b_im_x = _block_diag(jnp.transpose(small["ssm_b_im"][0], (0, 2, 1)), SSM_GROUP_CH, SSM_STATE)
    c_re_x = _block_diag(jnp.transpose(small["ssm_c_re"][0], (0, 2, 1)), SSM_STATE, SSM_GROUP_CH)
    c_im_x = _block_diag(jnp.transpose(small["ssm_c_im"][0], (0, 2, 1)), SSM_STATE, SSM_GROUP_CH)
    cblk = jnp.concatenate([c_re_x, -c_im_x], axis=0).astype(BF16)
    d_row = small["ssm_d"].reshape(1, SSM_WIDTH)

    w_in_all = _all_gather(bf["w_in"].reshape(-1, LANES), "gather_w_in").reshape(N_DEV, D_MODEL, IN_SHARD)
    w_in_pad = _pad_in(jnp.transpose(w_in_all, (1, 0, 2)).reshape(D_MODEL, D_IN))
    cos_t, sin_t = _rope_tables(pos_col)
    lam, bblk = _ssm_prep(a_re, a_im, log_dt, b_re_x, b_im_x)
    wq_mine = jnp.pad(bf["w_q_b"], ((0, 0), (0, QK_PAD - QK_HEAD)))
    xn, u, ql, kvl, gs, gm, w_glu, w_o_ssm = _in_proj(
        x, small["norm_mix"], w_in_pad, xch=[(bf["w_glu"], False), (bf["w_o_ssm"], False)])
    w_glu = w_glu.reshape(SSM_WIDTH, SSM_WIDTH)
    y, y_ssm, st, wq, wkv, w_o_mla, w_out = _ssm_fwd(
        u, bblk, cblk, lam, d_row, w_glu, small["b_glu"], w_o_ssm,
        xch=[(wq_mine, False), (bf["w_kv_b"], False), (bf["w_o_mla"], False), (bf["w_out"], False)])
    w_o_mla, w_out = w_o_mla.reshape(D_MODEL, D_MODEL), w_out.reshape(D_MODEL, D_MODEL)
    q, k, v = _qkv_prep(ql, kvl, small["q_a_norm"], small["kv_a_norm"], wq, wkv, gq, gk, cos_t, sin_t)
    attn, lse, w_up, w_down = _attn_fwd(q, k, v, xch=[(bf["w_up"], False), (bf["w_down"], False)])
    h, mixed, y_mla = _merge(attn, gs, gm, y_ssm, x, w_o_mla, w_out)
    hn, dout, loss = _mlp_fwd_loss(h, target, small["norm_mlp"], w_up, w_down)

    hid, da, dh, d_norm_mlp = _mlp_bwd(dout, hn, h, small["norm_mlp"], w_up, w_down)
    p_w_down = _matmul_tn_shards(hid, dout, "dw_down", False)
    p_w_up = _matmul_tn_shards(hn, da, "dw_up", True)
    dgs, dgm, dy_ssm, dy_mla, dattn = _merge_bwd(dh, gs, gm, y_ssm, y_mla, w_out, w_o_mla)
    p_w_out = _matmul_tn_shards(mixed, dh, "dw_out", False)
    p_w_o_mla = _matmul_tn_shards(attn, dy_mla, "dw_o_mla", False)
    dq, dk, dv, l_w_up, l_w_down, l_w_out, l_w_o_mla = _attn_bwd(
        q, k, v, attn, lse, dattn, xch=[(p_w_up, True), (p_w_down, True), (p_w_out, True), (p_w_o_mla, True)])
    dql, dkvl, qa, dq_pre, ca, dkv_pre, d_q_a_norm, d_kv_a_norm, d_gq, d_gk = _qkv_prep_bwd(
        ql, kvl, dq, dk, dv, small["q_a_norm"], small["kv_a_norm"], wq, wkv, gq, gk, cos_t, sin_t, xch=[])
    p_wq = _matmul_tn_shards(qa, dq_pre, "dw_q_b", True)
    p_wkv = _matmul_tn_shards(ca, dkv_pre, "dw_kv_b", True)
    dy, zg, z, dt, d_b_glu = _glu_bwd(dy_ssm, y, w_glu, small["b_glu"], w_o_ssm)
    p_w_o_ssm = _matmul_tn_shards(zg, dy_ssm, "dw_o_ssm", True)
    p_w_glu = _matmul_tn_shards(z, dt, "dw_glu", False)
    du, xs, ads, u_scan, dy_scan, dlam, d_d, l_wq, l_wkv, l_w_glu, l_w_o_ssm = _ssm_bwd(
        u, dy, st, bblk, cblk, lam, d_row, xch=[(p_wq, True), (p_wkv, True), (p_w_glu, True), (p_w_o_ssm, True)])
    d_bblk = _matmul_tn(u_scan, ads, "d_bblk")
    d_cblk_t = _matmul_tn(dy_scan, xs, "d_cblk")
    d_a_re, d_a_im, d_log_dt, d_b_re_x, d_b_im_x = _ssm_prep_bwd(a_re, a_im, log_dt, b_re_x, b_im_x, dlam, d_bblk)
    dx, dproj, d_norm_mix = _in_proj_bwd((du, dql, dkvl, dgs, dgm), dh, x, small["norm_mix"], w_in_pad)
    g_w_in = _unpad_in(_matmul_tn(xn, dproj, "dw_in"))
    parts = jnp.transpose(g_w_in.reshape(D_MODEL, N_DEV, IN_SHARD), (1, 0, 2)).reshape(N_DEV, -1, LANES)
    g_w_in_mine = _reduce_scatter(parts, "reduce_w_in").reshape(D_MODEL, IN_SHARD)

    tr = lambda mat: jnp.transpose(mat, (0, 2, 1))[None]
    g_small = {
        "norm_mix": d_norm_mix, "q_a_norm": d_q_a_norm, "kv_a_norm": d_kv_a_norm,
        "q_norm": d_gq[:, :QK_HEAD], "k_norm": d_gk[:, :QK_HEAD],
        "ssm_a_re": d_a_re, "ssm_a_im": d_a_im, "ssm_log_dt": d_log_dt,
        "ssm_b_re": tr(_block_diag_extract(d_b_re_x, SSM_GROUP_CH, SSM_STATE)),
        "ssm_b_im": tr(_block_diag_extract(d_b_im_x, SSM_GROUP_CH, SSM_STATE)),
        "ssm_c_re": _block_diag_extract(d_cblk_t[:, :N_STATE], SSM_GROUP_CH, SSM_STATE),
        "ssm_c_im": -_block_diag_extract(d_cblk_t[:, N_STATE:], SSM_GROUP_CH, SSM_STATE),
        "ssm_d": d_d, "b_glu": d_b_glu, "norm_mlp": d_norm_mlp,
    }
    g_small_all = _all_gather(_pack_small(g_small), "gather_small_grads")
    landed = {"w_q_b": l_wq[:, :, :QK_HEAD], "w_kv_b": l_wkv, "w_o_mla": l_w_o_mla, "w_glu": l_w_glu,
              "w_o_ssm": l_w_o_ssm, "w_out": l_w_out, "w_up": l_w_up, "w_down": l_w_down}
    return loss, dx, landed, g_w_in_mine, g_small_all


def kernel(x, positions, norm_mix, w_in, q_a_norm, kv_a_norm, w_q_b, w_kv_b, q_norm, k_norm, w_o_mla, ssm_a_re, ssm_a_im, ssm_log_dt, ssm_b_re, ssm_b_im, ssm_c_re, ssm_c_im, ssm_d, w_glu, b_glu, w_o_ssm, w_out, norm_mlp, w_up, w_down, loss_target, m_norm_mix, m_w_in, m_q_a_norm, m_kv_a_norm, m_w_q_b, m_w_kv_b, m_q_norm, m_k_norm, m_w_o_mla, m_ssm_a_re, m_ssm_a_im, m_ssm_log_dt, m_ssm_b_re, m_ssm_b_im, m_ssm_c_re, m_ssm_c_im, m_ssm_d, m_w_glu, m_b_glu, m_w_o_ssm, m_w_out, m_norm_mlp, m_w_up, m_w_down, v_norm_mix, v_w_in, v_q_a_norm, v_kv_a_norm, v_w_q_b, v_w_kv_b, v_q_norm, v_k_norm, v_w_o_mla, v_ssm_a_re, v_ssm_a_im, v_ssm_log_dt, v_ssm_b_re, v_ssm_b_im, v_ssm_c_re, v_ssm_c_im, v_ssm_d, v_w_glu, v_b_glu, v_w_o_ssm, v_w_out, v_norm_mlp, v_w_up, v_w_down):
    given = dict(locals())
    w = {n: given[n] for n in WEIGHT_ORDER}
    m = {n: given["m_" + n] for n in WEIGHT_ORDER}
    v = {n: given["v_" + n] for n in WEIGHT_ORDER}
    big = [n for n in WEIGHT_ORDER if n not in SMALL]
    small = {n: w[n] for n in SMALL}

    loss, dx, landed, g_w_in, g_small_all = _step(
        x[0], positions.reshape(-1, 1), loss_target[0], {n: w[n][0] for n in big}, small)

    grads, deltas, new_m, new_v = {}, {}, {}, {}
    for n in big:
        if n == "w_in":
            g = g_w_in
            d, nm, nv = _adamw(w[n][0], g, m[n][0], v[n][0], "adamw_" + n)
        else:
            g, d, nm, nv = _adamw_sum(landed[n], w[n][0], m[n][0], v[n][0], "adamw_" + n)
        grads[n], deltas[n], new_m[n], new_v[n] = g[None], d[None], nm[None], nv[None]

    packed = _adamw_small(g_small_all, _pack_small(small), _pack_small({n: m[n] for n in SMALL}),
                          _pack_small({n: v[n] for n in SMALL}))
    for dst, src in zip((grads, deltas, new_m, new_v), packed):
        dst.update(_unpack_small(src, small))

    total = lax.psum(loss[0, 0], ("x", "y", "c"))
    return (total, dx[None], *[grads[n] for n in WEIGHT_ORDER], *[deltas[n] for n in WEIGHT_ORDER],
            *[new_m[n] for n in WEIGHT_ORDER], *[new_v[n] for n in WEIGHT_ORDER])
```

```python
import functools
import math

import numpy as np
import jax
import jax.numpy as jnp
from jax import lax
from jax.experimental import pallas as pl
from jax.experimental.pallas import tpu as pltpu

F32 = jnp.float32
BF16 = jnp.bfloat16

D_MODEL = 1024
SSM_GROUPS = 32
SSM_GROUP_CH = 16
SSM_WIDTH = 512
SSM_STATE = 64
N_STATE = SSM_GROUPS * SSM_STATE
N_HEADS = 8
QK_NOPE = 128
QK_ROPE = 64
QK_HEAD = 192
QK_PAD = 256
V_HEAD = 128
Q_LORA = 384
KV_LORA = 256
KV_LAT_PAD = 384
ROPE_THETA = 10000.0
D_FF = 4096
EPS = 1e-6
ATT_SCALE = QK_HEAD ** -0.5
N_DEV = 8
FF_SHARD = D_FF // N_DEV
OUT_SHARD = D_MODEL // N_DEV

IN_SEGS = ((0, 512), (512, 896), (896, 1280), (1280, 2304), (2304, 3328))
D_IN = 3264
D_IN_PAD = 3328
KV_END = 1216

ADAM_LR = 0.001
ADAM_B1 = 0.9
ADAM_B2 = 0.999
ADAM_EPS = 1e-08
ADAM_WD = 0.01
ADAM_STEP = 10

VMEM_LIMIT = 56 * 1024 * 1024
MESH = pl.DeviceIdType.MESH
ANY = pl.BlockSpec(memory_space=pl.ANY)
LANES = 128

SCAN_T = 256
SUBCHUNKS = 8
SCAN_CG = 512
ATT_T = 512
ATT_SUB = 2
ROW_T = 256


def _params(sem=None):
    return pltpu.CompilerParams(dimension_semantics=sem, vmem_limit_bytes=VMEM_LIMIT)


def _rows(arr, tm):
    if arr.ndim == 2:
        return pl.BlockSpec((tm, arr.shape[1]), lambda i: (i, 0))
    return pl.BlockSpec((arr.shape[0], tm, arr.shape[2]), lambda i: (0, i, 0))


def _const(arr):
    nd = arr.ndim
    return pl.BlockSpec(arr.shape, lambda *_: (0,) * nd)


def _sds(shape, dtype):
    return jax.ShapeDtypeStruct(shape, dtype)


PEERS = tuple((dx, dy, dc) for dx in (0, 1) for dy in (0, 1) for dc in (0, 1) if (dx, dy, dc) != (0, 0, 0))


def _here():
    x, y, c = lax.axis_index("x"), lax.axis_index("y"), lax.axis_index("c")
    return x, y, c, 4 * x + 2 * y + c


def _xchg_start(scatter, srcs, dsts, send, recv, local):
    x, y, c, me = _here()
    for e, sc in enumerate(scatter):
        src, dst = srcs[e], dsts[e]
        pltpu.make_async_copy(src.at[me] if sc else src, dst.at[me], local.at[e]).start()
        for dx, dy, dc in PEERS:
            px, py, pc = (1 - x if dx else x), (1 - y if dy else y), (1 - c if dc else c)
            pltpu.make_async_remote_copy(
                src_ref=src.at[4 * px + 2 * py + pc] if sc else src, dst_ref=dst.at[me],
                send_sem=send.at[e], recv_sem=recv.at[e], device_id=(px, py, pc), device_id_type=MESH).start()


def _xchg_wait(scatter, srcs, dsts, send, recv, local):
    x, y, c, me = _here()
    for e, sc in enumerate(scatter):
        src, dst = srcs[e], dsts[e]
        pltpu.make_async_copy(src.at[me] if sc else src, dst.at[me], local.at[e]).wait()
        span = dst.at[pl.ds(0, N_DEV - 1)]
        both = pltpu.make_async_remote_copy(src_ref=span, dst_ref=span, send_sem=send.at[e], recv_sem=recv.at[e],
                                            device_id=(x, y, c), device_id_type=MESH)
        both.wait_send()
        both.wait_recv()


def _call(body, name, grid, ins, in_specs, outs, out_specs, scratch=(), xch=()):
    n_in, n_out, ne = len(ins), len(outs), len(xch)
    scatter = [sc for _, sc in xch]
    x_outs = [_sds((N_DEV,) + (a.shape[1:] if sc else a.shape), a.dtype) for a, sc in xch]
    sems = [pltpu.SemaphoreType.DMA((ne,))] * 3 if ne else []

    def wrapped(*refs):
        in_refs, x_src = refs[:n_in], refs[n_in:n_in + ne]
        out_refs = refs[n_in + ne:n_in + ne + n_out]
        x_dst = refs[n_in + ne + n_out:n_in + 2 * ne + n_out]
        rest = refs[n_in + 2 * ne + n_out:]
        if ne:
            x_sems, rest = rest[len(rest) - 3:], rest[:len(rest) - 3]
            first = functools.reduce(jnp.logical_and, [pl.program_id(d) == 0 for d in range(len(grid))])
            last = functools.reduce(jnp.logical_and, [pl.program_id(d) == grid[d] - 1 for d in range(len(grid))])

            @pl.when(first)
            def _():
                _xchg_start(scatter, x_src, x_dst, *x_sems)

        body(*in_refs, *out_refs, *rest)
        if ne:
            @pl.when(last)
            def _():
                _xchg_wait(scatter, x_src, x_dst, *x_sems)

    return pl.pallas_call(
        wrapped,
        name=name,
        grid=grid,
        in_specs=list(in_specs) + [ANY] * ne,
        out_specs=list(out_specs) + [ANY] * ne,
        out_shape=list(outs) + x_outs,
        scratch_shapes=list(scratch) + sems,
        compiler_params=_params(("arbitrary",) * len(grid)),
    )(*ins, *[a for a, _ in xch])


def _row_call(body, name, n_rows, tm, row_ins, const_ins, row_outs, acc_outs=(), xch=()):
    outs = [_sds(s, d) for s, d in row_outs] + [_sds(s, d) for s, d in acc_outs]
    out_specs = [_rows(o, tm) for o in outs[: len(row_outs)]] + [_const(o) for o in outs[len(row_outs):]]
    in_specs = [_rows(a, tm) for a in row_ins] + [_const(a) for a in const_ins]
    return _call(body, name, (n_rows // tm,), list(row_ins) + list(const_ins), in_specs, outs, out_specs, xch=xch)


def _dot(a, b):
    return jnp.dot(a, b, preferred_element_type=F32)


def _dot_nt(a, b):
    return lax.dot_general(a, b, (((1,), (1,)), ((), ())), preferred_element_type=F32)


def _dot_tn(a, b):
    return lax.dot_general(a, b, (((0,), (0,)), ((), ())), preferred_element_type=F32)


def _rms(x, g, n):
    inv = lax.rsqrt(jnp.sum(x * x, -1, keepdims=True) * (1.0 / n) + EPS)
    return x * inv * g, inv


def _rms_bwd(dy, x, g, inv, n):
    xh = x * inv
    dxh = dy * g
    dx = inv * (dxh - xh * (jnp.sum(dxh * xh, -1, keepdims=True) * (1.0 / n)))
    return dx, dy * xh


def _sigmoid(x):
    return 1.0 / (1.0 + jnp.exp(-x))


_GELU_C = math.sqrt(2.0 / math.pi)


def _gelu(y):
    th = jnp.tanh(_GELU_C * (y + 0.044715 * (y * y * y)))
    return 0.5 * y * (1.0 + th), th


def _gelu_grad(y, th):
    return 0.5 * (1.0 + th) + 0.5 * y * (1.0 - th * th) * (_GELU_C * (1.0 + 3.0 * 0.044715 * (y * y)))


def _acc(ref, val):
    @pl.when(pl.program_id(0) == 0)
    def _():
        ref[...] = jnp.zeros_like(ref)

    ref[...] += val


def _tile(n, limit):
    if n <= limit:
        return n
    return max(t for t in range(128, limit + 1, 128) if n % t == 0)


def _matmul_tn(a, b, name, tm=512, tk=512):
    k_dim, m = a.shape
    n = b.shape[1]
    tm, tk = _tile(m, tm), _tile(k_dim, tk)

    def body(a_ref, b_ref, o_ref):
        @pl.when(pl.program_id(1) == 0)
        def _():
            o_ref[...] = jnp.zeros_like(o_ref)

        o_ref[...] += _dot_tn(a_ref[...].astype(BF16), b_ref[...].astype(BF16))

    return pl.pallas_call(
        body,
        name=name,
        grid=(m // tm, k_dim // tk),
        in_specs=[pl.BlockSpec((tk, tm), lambda i, k: (k, i)), pl.BlockSpec((tk, n), lambda i, k: (k, 0))],
        out_specs=pl.BlockSpec((tm, n), lambda i, k: (i, 0)),
        out_shape=_sds((m, n), F32),
        compiler_params=_params(("parallel", "arbitrary")),
    )(a, b)


def _matmul_tn_shards(a, b, name, by_col, tm=512, tk=512):
    k_dim, m = a.shape
    n = b.shape[1]
    tm, tk = _tile(m, tm), _tile(k_dim, tk)
    nk = k_dim // tk
    if by_col:
        r, c = m, n // N_DEV
        out_spec = pl.BlockSpec((N_DEV, tm, c), lambda i, k: (0, i, 0))
    else:
        r, c = m // N_DEV, n
        per = tm // r
        out_spec = pl.BlockSpec((per, r, c), lambda i, k: (i, 0, 0))

    def body(a_ref, b_ref, o_ref, acc_ref):
        k = pl.program_id(1)

        @pl.when(k == 0)
        def _():
            acc_ref[...] = jnp.zeros_like(acc_ref)

        acc_ref[...] += _dot_tn(a_ref[...].astype(BF16), b_ref[...].astype(BF16))

        @pl.when(k == nk - 1)
        def _():
            if by_col:
                for j in range(N_DEV):
                    o_ref[j] = acc_ref[:, j * c:(j + 1) * c].astype(BF16)
            else:
                for s in range(per):
                    o_ref[s] = acc_ref[s * r:(s + 1) * r, :].astype(BF16)

    return pl.pallas_call(
        body,
        name=name,
        grid=(m // tm, nk),
        in_specs=[pl.BlockSpec((tk, tm), lambda i, k: (k, i)), pl.BlockSpec((tk, n), lambda i, k: (k, 0))],
        out_specs=out_spec,
        out_shape=_sds((N_DEV, r, c), BF16),
        scratch_shapes=[pltpu.VMEM((tm, n), F32)],
        compiler_params=_params(("parallel", "arbitrary")),
    )(a, b)


def _rope_tables(pos_col):
    n = pos_col.shape[0]
    half = QK_ROPE // 2
    inv_freq = (ROPE_THETA ** (-np.arange(half, dtype=np.float32) / half)).astype(np.float32)
    freq_row = jnp.asarray(np.concatenate([inv_freq, inv_freq, np.zeros(64, np.float32)])[None, :])

    def body(p_ref, f_ref, c_ref, s_ref):
        ang = p_ref[...].astype(F32) * f_ref[...]
        c_ref[...] = jnp.cos(ang)
        s_ref[...] = jnp.sin(ang)

    return _row_call(body, "rope_tables", n, min(n, 1024), [pos_col], [freq_row], [((n, 128), F32)] * 2)


def _rope_rot(v):
    lane = lax.broadcasted_iota(jnp.int32, v.shape, 1)
    return jnp.where(lane < 32, -pltpu.roll(v, 96, 1), jnp.where(lane < 64, pltpu.roll(v, 32, 1), 0.0))


def _rope_rot_t(v):
    lane = lax.broadcasted_iota(jnp.int32, v.shape, 1)
    return jnp.where(lane < 32, pltpu.roll(v, 96, 1), jnp.where(lane < 64, -pltpu.roll(v, 32, 1), 0.0))


def _in_proj(x, norm_mix, w_in_pad, xch):
    n = x.shape[0]

    def body(x_ref, g_ref, w_ref, xn_ref, u_ref, ql_ref, kvl_ref, gs_ref, gm_ref):
        xn, _ = _rms(x_ref[...], g_ref[...], D_MODEL)
        xb = xn.astype(BF16)
        xn_ref[...] = xb
        for ref, (a, b) in zip((u_ref, ql_ref, kvl_ref, gs_ref, gm_ref), IN_SEGS):
            ref[...] = _dot(xb, w_ref[:, a:b])

    outs = [((n, D_MODEL), BF16)] + [((n, b - a), F32) for a, b in IN_SEGS]
    return _row_call(body, "in_proj", n, ROW_T, [x], [norm_mix, w_in_pad], outs, xch=xch)


def _ssm_prep_fn(a_re, a_im, log_dt, b_re_x, b_im_x):
    dt = jnp.exp(log_dt)
    mag = jnp.exp(a_re * dt)
    lr = mag * jnp.cos(a_im * dt)
    li = mag * jnp.sin(a_im * dt)
    den = a_re * a_re + a_im * a_im
    fr = ((lr - 1.0) * a_re + li * a_im) / den
    fi = (li * a_re - (lr - 1.0) * a_im) / den
    return lr, li, fr * b_re_x - fi * b_im_x, fr * b_im_x + fi * b_re_x


def _ssm_prep(a_re, a_im, log_dt, b_re_x, b_im_x):
    def body(ar, ai, ld, br, bi, lam_ref, bblk_ref):
        lr, li, bbr, bbi = _ssm_prep_fn(ar[...], ai[...], ld[...], br[...], bi[...])
        lam_ref[0:1, :] = lr
        lam_ref[1:2, :] = li
        bblk_ref[:, 0:N_STATE] = bbr.astype(BF16)
        bblk_ref[:, N_STATE:] = bbi.astype(BF16)

    return pl.pallas_call(
        body,
        name="ssm_prep",
        out_shape=[_sds((2, N_STATE), F32), _sds((SSM_WIDTH, 2 * N_STATE), BF16)],
        compiler_params=_params(),
    )(a_re, a_im, log_dt, b_re_x, b_im_x)


def _ssm_prep_bwd(a_re, a_im, log_dt, b_re_x, b_im_x, dlam, dbblk):
    def body(ar, ai, ld, br, bi, dl, db, dar, dai, dld, dbr, dbi):
        _, vjp = jax.vjp(_ssm_prep_fn, ar[...], ai[...], ld[...], br[...], bi[...])
        g = vjp((dl[0:1, :], dl[1:2, :], db[:, 0:N_STATE], db[:, N_STATE:]))
        dar[...] = g[0]
        dai[...] = g[1]
        grp = lax.broadcasted_iota(jnp.int32, (SSM_GROUPS, N_STATE), 0)
        lane = lax.broadcasted_iota(jnp.int32, (SSM_GROUPS, N_STATE), 1)
        sel = (lane // SSM_STATE) == grp
        dld[...] = jnp.sum(jnp.where(sel, jnp.broadcast_to(g[2], (SSM_GROUPS, N_STATE)), 0.0), axis=1, keepdims=True)
        dbr[...] = g[3]
        dbi[...] = g[4]

    return pl.pallas_call(
        body,
        name="ssm_prep_bwd",
        out_shape=[_sds((1, N_STATE), F32), _sds((1, N_STATE), F32), _sds((SSM_GROUPS, 1), F32),
                   _sds((SSM_WIDTH, N_STATE), F32), _sds((SSM_WIDTH, N_STATE), F32)],
        compiler_params=_params(),
    )(a_re, a_im, log_dt, b_re_x, b_im_x, dlam, dbblk)


def _perm_matrix(t):
    run = t // SUBCHUNKS
    p = np.zeros((t, t), np.float32)
    r = np.arange(t)
    p[r, (r % SUBCHUNKS) * run + r // SUBCHUNKS] = 1.0
    return jnp.asarray(p, dtype=BF16)


def _unpermute(p, a):
    hi = a.astype(BF16)
    r1 = a - hi.astype(F32)
    mid = r1.astype(BF16)
    lo = (r1 - mid.astype(F32)).astype(BF16)
    return _dot_tn(p, hi) + _dot_tn(p, mid) + _dot_tn(p, lo)


def _power_table(lam_ref, pw_ref, n):
    lr, li = lam_ref[0:1, :], lam_ref[1:2, :]
    pw_ref[0:1, 0:N_STATE] = lr
    pw_ref[0:1, N_STATE:] = li

    def step(i, carry):
        pr, pi = carry
        pr, pi = pr * lr - pi * li, pr * li + pi * lr
        pw_ref[pl.ds(i, 1), 0:N_STATE] = pr
        pw_ref[pl.ds(i, 1), N_STATE:] = pi
        return pr, pi

    lax.fori_loop(1, n, step, (lr, li))


def _col_groups():
    return [(pl.ds(c, SCAN_CG), pl.ds(N_STATE + c, SCAN_CG)) for c in range(0, N_STATE, SCAN_CG)]


def _run_scan(buf, lam_ref, t, reverse):
    nblk = t // 8
    for re, im in _col_groups():
        lr = jnp.broadcast_to(lam_ref[0:1, re], (8, SCAN_CG))
        li = jnp.broadcast_to(lam_ref[1:2, re], (8, SCAN_CG))
        if reverse:
            li = -li
        first = pl.ds((nblk - 1) * 8 if reverse else 0, 8)

        def step(k, carry, re=re, im=im, lr=lr, li=li):
            pr, pi = carry
            i = (nblk - 2 - k) if reverse else (k + 1)
            r = pl.ds(pl.multiple_of(i * 8, 8), 8)
            xr = buf[r, re] + lr * pr - li * pi
            xi = buf[r, im] + lr * pi + li * pr
            buf[r, re] = xr
            buf[r, im] = xi
            return xr, xi

        lax.fori_loop(0, nblk - 1, step, (buf[first, re], buf[first, im]))


def _run_carries(buf, pw_ref, carry_ref, s_ref, t, reverse):
    nblk = t // 8
    run = t // SUBCHUNKS
    edge = buf[pl.ds(0 if reverse else (nblk - 1) * 8, 8), :]
    pr, pi = pw_ref[run - 1:run, 0:N_STATE], pw_ref[run - 1:run, N_STATE:]
    if reverse:
        pi = -pi
    sr, si = carry_ref[0:1, 0:N_STATE], carry_ref[0:1, N_STATE:]
    for s in (range(SUBCHUNKS - 1, -1, -1) if reverse else range(SUBCHUNKS)):
        s_ref[s:s + 1, 0:N_STATE] = sr
        s_ref[s:s + 1, N_STATE:] = si
        er, ei = edge[s:s + 1, 0:N_STATE], edge[s:s + 1, N_STATE:]
        sr, si = er + pr * sr - pi * si, ei + pr * si + pi * sr
    carry_ref[:, 0:N_STATE] = jnp.broadcast_to(sr, (8, N_STATE))
    carry_ref[:, N_STATE:] = jnp.broadcast_to(si, (8, N_STATE))


def _run_fix(buf, pw_ref, s_ref, t, reverse):
    nblk = t // 8
    for re, im in _col_groups():
        sr, si = s_ref[:, re], s_ref[:, im]

        def step(i, carry, re=re, im=im, sr=sr, si=si):
            r = pl.ds(pl.multiple_of(i * 8, 8), 8)
            row = pl.ds((nblk - 1 - i) if reverse else i, 1)
            pr, pi = pw_ref[row, re], pw_ref[row, im]
            if reverse:
                pi = -pi
            buf[r, re] += pr * sr - pi * si
            buf[r, im] += pr * si + pi * sr
            return carry

        lax.fori_loop(0, nblk, step, 0)


def _ssm_fwd(u, bblk, cblk, lam, d_row, w_glu, b_glu, w_o_ssm, xch):
    n = u.shape[0]
    t = min(SCAN_T, n)
    kb = 512
    perm = _perm_matrix(t)

    def body(u_ref, p_ref, bblk_ref, cblk_ref, lam_ref, d_ref, wg_ref, bg_ref, wo_ref, y_ref, ys_ref, st_ref,
             buf, pw_ref, carry_ref, s_ref):
        @pl.when(pl.program_id(0) == 0)
        def _():
            carry_ref[...] = jnp.zeros_like(carry_ref)
            _power_table(lam_ref, pw_ref, t // SUBCHUNKS)

        st_ref[0] = carry_ref[...]
        u_t = u_ref[...]
        p = p_ref[...]
        ub = _dot(p, u_t.astype(BF16)).astype(BF16)
        for c in range(0, 2 * N_STATE, kb):
            buf[:, c:c + kb] = _dot(ub, bblk_ref[:, c:c + kb])
        _run_scan(buf, lam_ref, t, False)
        _run_carries(buf, pw_ref, carry_ref, s_ref, t, False)
        _run_fix(buf, pw_ref, s_ref, t, False)
        yp = jnp.zeros((t, SSM_WIDTH), F32)
        for c in range(0, 2 * N_STATE, kb):
            yp += _dot(buf[:, c:c + kb].astype(BF16), cblk_ref[c:c + kb, :])
        y = d_ref[...] * u_t + _unpermute(p, yp)
        y_ref[...] = y
        z, _ = _gelu(y)
        s = _sigmoid(_dot(z.astype(BF16), wg_ref[...]) + bg_ref[...])
        zgb = (z * s).astype(BF16)
        for j in range(N_DEV):
            ys_ref[:, j * OUT_SHARD:(j + 1) * OUT_SHARD] = _dot(zgb, wo_ref[j])

    consts = [perm, bblk, cblk, lam, d_row, w_glu, b_glu, w_o_ssm]
    return _call(
        body, "ssm_fwd", (n // t,), [u] + consts, [_rows(u, t)] + [_const(a) for a in consts],
        [_sds((n, SSM_WIDTH), F32), _sds((n, D_MODEL), F32), _sds((n // t, 8, 2 * N_STATE), F32)],
        [pl.BlockSpec((t, SSM_WIDTH), lambda i: (i, 0)), pl.BlockSpec((t, D_MODEL), lambda i: (i, 0)),
         pl.BlockSpec((1, 8, 2 * N_STATE), lambda i: (i, 0, 0))],
        scratch=[pltpu.VMEM((t, 2 * N_STATE), F32), pltpu.VMEM((t // SUBCHUNKS, 2 * N_STATE), F32),
                 pltpu.VMEM((8, 2 * N_STATE), F32), pltpu.VMEM((8, 2 * N_STATE), F32)],
        xch=xch)


def _head_norm_rope(slab, gain, cos_t, sin_t):
    xn, inv = _rms(slab, gain, QK_HEAD)
    lo, hi = xn[:, 0:128], xn[:, 128:256]
    return jnp.concatenate([lo, hi * cos_t + _rope_rot(hi) * sin_t], axis=-1), inv


def _head_norm_rope_bwd(g, slab, gain, inv, cos_t, sin_t):
    g_lo, g_hi = g[:, 0:128], g[:, 128:256]
    g_n = jnp.concatenate([g_lo, g_hi * cos_t + _rope_rot_t(g_hi * sin_t)], axis=-1)
    return _rms_bwd(g_n, slab, gain, inv, QK_HEAD)


def _qkv_prep(ql, kvl, q_a_norm, kv_a_norm, wq, wkv, gq, gk, cos_t, sin_t):
    n = ql.shape[0]
    tm = ROW_T

    def body(ql_ref, kvl_ref, cos_ref, sin_ref, qa_ref, ka_ref, wq_ref, wkv_ref, gq_ref, gk_ref,
             q_ref, k_ref, v_ref, kt_ref, vt_ref):
        cos_t, sin_t = cos_ref[...], sin_ref[...]
        qa, _ = _rms(ql_ref[...], qa_ref[...], Q_LORA)
        qab = qa.astype(BF16)
        kvl_t = kvl_ref[...]
        ca, _ = _rms(kvl_t[:, 0:KV_LORA], ka_ref[...], KV_LORA)
        cab = ca.astype(BF16)
        kpe = kvl_t[:, KV_LORA:KV_LAT_PAD]
        for h in range(N_HEADS):
            qh, _ = _head_norm_rope(_dot(qab, wq_ref[h]), gq_ref[...], cos_t, sin_t)
            q_ref[h] = (qh * ATT_SCALE).astype(BF16)
            kv_h = _dot(cab, wkv_ref[h])
            kh, _ = _head_norm_rope(jnp.concatenate([kv_h[:, 0:QK_NOPE], kpe], axis=-1), gk_ref[...], cos_t, sin_t)
            k_ref[h] = kh.astype(BF16)
            kt_ref[h] = kh.T.astype(BF16)
            vh = kv_h[:, QK_NOPE:]
            v_ref[h] = vh.astype(BF16)
            vt_ref[h] = vh.T.astype(BF16)

    row_ins, consts = [ql, kvl, cos_t, sin_t], [q_a_norm, kv_a_norm, wq, wkv, gq, gk]
    outs = [_sds((N_HEADS, n, QK_PAD), BF16), _sds((N_HEADS, n, QK_PAD), BF16), _sds((N_HEADS, n, V_HEAD), BF16),
            _sds((N_HEADS, QK_PAD, n), BF16), _sds((N_HEADS, V_HEAD, n), BF16)]
    out_specs = [_rows(o, tm) for o in outs[:3]] + [
        pl.BlockSpec((N_HEADS, QK_PAD, tm), lambda i: (0, 0, i)), pl.BlockSpec((N_HEADS, V_HEAD, tm), lambda i: (0, 0, i))]
    return _call(body, "qkv_prep", (n // tm,), row_ins + consts,
                 [_rows(a, tm) for a in row_ins] + [_const(a) for a in consts], outs, out_specs)


def _causal_mask_t(st, t):
    key = lax.broadcasted_iota(jnp.int32, (t, t), 0)
    qry = lax.broadcasted_iota(jnp.int32, (t, t), 1)
    return jnp.where(key <= qry, st, -jnp.inf)


def _attn_fwd(q, k, vt, xch):
    n = q.shape[1]
    t = min(ATT_T, n)

    def body(q_ref, k_ref, vt_ref, o_ref, lse_ref):
        i = pl.program_id(1)
        qt = q_ref[0]

        def kv_tile(j, carry, diag):
            m, l, acc = carry
            ts = t // ATT_SUB
            sts = []
            for a in range(ATT_SUB):
                r0 = pl.multiple_of(j * t + a * ts, ts)
                st = _dot_nt(k_ref[0, pl.ds(r0, ts), :], qt)
                if diag:
                    key = lax.broadcasted_iota(jnp.int32, (ts, t), 0) + a * ts
                    qry = lax.broadcasted_iota(jnp.int32, (ts, t), 1)
                    st = jnp.where(key <= qry, st, -jnp.inf)
                sts.append(st)
            for a, st in enumerate(sts):
                r0 = pl.multiple_of(j * t + a * ts, ts)
                m_new = jnp.maximum(m, jnp.max(st, 0, keepdims=True))
                alpha = jnp.exp(m - m_new)
                pt = jnp.exp(st - m_new)
                l = alpha * l + jnp.sum(pt, 0, keepdims=True)
                acc = alpha * acc + _dot(vt_ref[0, :, pl.ds(r0, ts)], pt.astype(BF16))
                m = m_new
            return m, l, acc

        init = (jnp.full((1, t), -jnp.inf, F32), jnp.zeros((1, t), F32), jnp.zeros((V_HEAD, t), F32))
        carry = lax.fori_loop(0, i, functools.partial(kv_tile, diag=False), init)
        m, l, acc = kv_tile(i, carry, True)
        o_ref[...] = (acc / l).T
        lse_ref[0] = m + jnp.log(l)

    return _call(
        body, "attn_fwd", (N_HEADS, n // t), [q, k, vt],
        [pl.BlockSpec((1, t, QK_PAD), lambda h, i: (h, i, 0)), pl.BlockSpec((1, n, QK_PAD), lambda h, i: (h, 0, 0)),
         pl.BlockSpec((1, V_HEAD, n), lambda h, i: (h, 0, 0))],
        [_sds((n, N_HEADS * V_HEAD), F32), _sds((N_HEADS, 1, n), F32)],
        [pl.BlockSpec((t, V_HEAD), lambda h, i: (i, h)), pl.BlockSpec((1, 1, t), lambda h, i: (h, 0, i))],
        xch=xch)


def _merge(attn, gs, gm, y_ssm, x, w_o_mla, w_out):
    n = x.shape[0]

    def body(at_ref, gs_ref, gm_ref, ys_ref, x_ref, wo_ref, wout_ref, h_ref, mx_ref, ym_ref):
        y_mla = _dot(at_ref[...].astype(BF16), wo_ref[...])
        ym_ref[...] = y_mla
        mixed = (_sigmoid(gs_ref[...]) * ys_ref[...] + _sigmoid(gm_ref[...]) * y_mla).astype(BF16)
        mx_ref[...] = mixed
        h_ref[...] = x_ref[...] + _dot(mixed, wout_ref[...])

    outs = [((n, D_MODEL), F32), ((n, D_MODEL), BF16), ((n, D_MODEL), F32)]
    return _row_call(body, "merge", n, ROW_T, [attn, gs, gm, y_ssm, x], [w_o_mla, w_out], outs)


def _mlp_fwd_loss(h, target, norm_mlp, w_up, w_down):
    n = h.shape[0]

    def body(h_ref, t_ref, g_ref, wu_ref, wd_ref, hn_ref, do_ref, loss_ref):
        h_t = h_ref[...]
        hn, _ = _rms(h_t, g_ref[...], D_MODEL)
        hb = hn.astype(BF16)
        hn_ref[...] = hb
        out = h_t
        for j in range(N_DEV):
            a = jnp.maximum(_dot(hb, wu_ref[j]), 0.0)
            out += _dot((a * a).astype(BF16), wd_ref[j])
        err = out - t_ref[...]
        do_ref[...] = err * (1.0 / D_MODEL)
        _acc(loss_ref, jnp.broadcast_to(jnp.sum(err * err) * (0.5 / D_MODEL), loss_ref.shape))

    outs = [((n, D_MODEL), BF16), ((n, D_MODEL), F32)]
    return _row_call(body, "mlp_fwd_loss", n, ROW_T, [h, target], [norm_mlp, w_up, w_down], outs, [((8, 128), F32)])


def _mlp_bwd(dout, hn, h, norm_mlp, w_up, w_down):
    n = h.shape[0]

    def body(do_ref, hn_ref, h_ref, g_ref, wu_ref, wd_ref, hid_ref, da_ref, dh_ref, dg_ref):
        dout_t = do_ref[...]
        doutb = dout_t.astype(BF16)
        hb = hn_ref[...]
        dhn = jnp.zeros_like(dout_t)
        for j in range(N_DEV):
            cols = slice(j * FF_SHARD, (j + 1) * FF_SHARD)
            a = jnp.maximum(_dot(hb, wu_ref[j]), 0.0)
            hid_ref[:, cols] = (a * a).astype(BF16)
            da = (_dot_nt(doutb, wd_ref[j]) * (2.0 * a)).astype(BF16)
            da_ref[:, cols] = da
            dhn += _dot_nt(da, wu_ref[j])
        h_t = h_ref[...]
        inv = lax.rsqrt(jnp.sum(h_t * h_t, -1, keepdims=True) * (1.0 / D_MODEL) + EPS)
        dx, dg = _rms_bwd(dhn, h_t, g_ref[...], inv, D_MODEL)
        dh_ref[...] = dout_t + dx
        _acc(dg_ref, jnp.sum(dg, 0, keepdims=True))

    outs = [((n, D_FF), BF16), ((n, D_FF), BF16), ((n, D_MODEL), F32)]
    return _row_call(body, "mlp_bwd", n, ROW_T, [dout, hn, h], [norm_mlp, w_up, w_down], outs, [((1, D_MODEL), F32)])


def _merge_bwd(dh, gs, gm, y_ssm, y_mla, w_out, w_o_mla):
    n = dh.shape[0]

    def body(dh_ref, gs_ref, gm_ref, ys_ref, ym_ref, wout_ref, wo_ref, dgs_ref, dgm_ref, dys_ref, dym_ref, dat_ref):
        dmix = _dot_nt(dh_ref[...].astype(BF16), wout_ref[...])
        sgs, sgm = _sigmoid(gs_ref[...]), _sigmoid(gm_ref[...])
        dgs_ref[...] = (dmix * ys_ref[...] * sgs * (1.0 - sgs)).astype(BF16)
        dgm_ref[...] = (dmix * ym_ref[...] * sgm * (1.0 - sgm)).astype(BF16)
        dys_ref[...] = (dmix * sgs).astype(BF16)
        dym = (dmix * sgm).astype(BF16)
        dym_ref[...] = dym
        dat_ref[...] = _dot_nt(dym, wo_ref[...])

    outs = [((n, D_MODEL), BF16)] * 4 + [((n, D_MODEL), F32)]
    return _row_call(body, "merge_bwd", n, ROW_T, [dh, gs, gm, y_ssm, y_mla], [w_out, w_o_mla], outs)


def _attn_bwd(q, k, kt, v, out, lse, dout, xch):
    n = q.shape[1]
    t = min(ATT_T, n)
    nt = n // t

    def body(q_ref, k_ref, kt_ref, v_ref, o_ref, lse_ref, do_ref, dq_ref, dk_ref, dv_ref, delta_ref, dqt_ref):
        j = pl.program_id(1)

        @pl.when(j == 0)
        def _():
            dqt_ref[...] = jnp.zeros_like(dqt_ref)
            prod = do_ref[...] * o_ref[...]
            delta_ref[...] = lax.dot_general(jnp.ones((8, V_HEAD), F32), prod, (((1,), (1,)), ((), ())),
                                             precision=lax.Precision.HIGHEST, preferred_element_type=F32)

        k_t = k_ref[0]
        kt_t = kt_ref[0]
        v_t = v_ref[0]

        def q_tile(i, carry, diag):
            dk, dv = carry
            r0 = pl.multiple_of(i * t, t)
            rows = pl.ds(r0, t)
            qt = q_ref[0, rows, :]
            st = _dot_nt(k_t, qt)
            if diag:
                st = _causal_mask_t(st, t)
            pt = jnp.exp(st - lse_ref[0, :, rows])
            dob = do_ref[rows, :].astype(BF16)
            dv = dv + _dot(pt.astype(BF16), dob)
            dst = (pt * (_dot_nt(v_t, dob) - delta_ref[0:1, rows])).astype(BF16)
            dk = dk + _dot(dst, qt)
            dqt_ref[:, rows] += _dot(kt_t, dst)
            return dk, dv

        carry = q_tile(j, (jnp.zeros((t, QK_PAD), F32), jnp.zeros((t, V_HEAD), F32)), True)
        dk, dv = lax.fori_loop(j + 1, nt, functools.partial(q_tile, diag=False), carry)
        dk_ref[0] = dk
        dv_ref[0] = dv

        @pl.when(j == nt - 1)
        def _():
            for c in range(0, n, t):
                dq_ref[0, c:c + t, :] = dqt_ref[:, c:c + t].T

    return _call(
        body, "attn_bwd", (N_HEADS, nt), [q, k, kt, v, out, lse, dout],
        [pl.BlockSpec((1, n, QK_PAD), lambda h, j: (h, 0, 0)), pl.BlockSpec((1, t, QK_PAD), lambda h, j: (h, j, 0)),
         pl.BlockSpec((1, QK_PAD, t), lambda h, j: (h, 0, j)), pl.BlockSpec((1, t, V_HEAD), lambda h, j: (h, j, 0)),
         pl.BlockSpec((n, V_HEAD), lambda h, j: (0, h)), pl.BlockSpec((1, 1, n), lambda h, j: (h, 0, 0)),
         pl.BlockSpec((n, V_HEAD), lambda h, j: (0, h))],
        [_sds((N_HEADS, n, QK_PAD), F32), _sds((N_HEADS, n, QK_PAD), F32), _sds((N_HEADS, n, V_HEAD), F32)],
        [pl.BlockSpec((1, n, QK_PAD), lambda h, j: (h, 0, 0)), pl.BlockSpec((1, t, QK_PAD), lambda h, j: (h, j, 0)),
         pl.BlockSpec((1, t, V_HEAD), lambda h, j: (h, j, 0))],
        scratch=[pltpu.VMEM((8, n), F32), pltpu.VMEM((QK_PAD, n), F32)],
        xch=xch)


def _qkv_prep_bwd(ql, kvl, dq, dk, dv, q_a_norm, kv_a_norm, wq, wkv, gq, gk, cos_t, sin_t, xch):
    n = ql.shape[0]

    def body(ql_ref, kvl_ref, cos_ref, sin_ref, dq_ref, dk_ref, dv_ref, qa_ref, ka_ref, wq_ref, wkv_ref, gq_ref, gk_ref,
             dql_ref, dkvl_ref, qab_ref, dqp_ref, cab_ref, dkvp_ref, dqa_ref, dka_ref, dgq_ref, dgk_ref):
        cos_t, sin_t = cos_ref[...], sin_ref[...]
        ql_t = ql_ref[...]
        qa, inv_qa = _rms(ql_t, qa_ref[...], Q_LORA)
        qab = qa.astype(BF16)
        qab_ref[...] = qab
        kvl_t = kvl_ref[...]
        ckv = kvl_t[:, 0:KV_LORA]
        ca, inv_ca = _rms(ckv, ka_ref[...], KV_LORA)
        cab = ca.astype(BF16)
        cab_ref[...] = cab
        kpe = kvl_t[:, KV_LORA:KV_LAT_PAD]
        dgq = jnp.zeros((1, QK_PAD), F32)
        dgk = jnp.zeros((1, QK_PAD), F32)
        dkpe = jnp.zeros_like(kpe)
        dqa = jnp.zeros_like(ql_t)
        dca = jnp.zeros_like(ckv)
        for h in range(N_HEADS):
            q_slab = _dot(qab, wq_ref[h])
            inv = lax.rsqrt(jnp.sum(q_slab * q_slab, -1, keepdims=True) * (1.0 / QK_HEAD) + EPS)
            d_slab, dg = _head_norm_rope_bwd(dq_ref[h] * ATT_SCALE, q_slab, gq_ref[...], inv, cos_t, sin_t)
            dqp = d_slab.astype(BF16)
            dqp_ref[:, h * QK_PAD:(h + 1) * QK_PAD] = dqp
            dqa += _dot_nt(dqp, wq_ref[h])
            dgq += jnp.sum(dg, 0, keepdims=True)
            kv_h = _dot(cab, wkv_ref[h])
            k_slab = jnp.concatenate([kv_h[:, 0:QK_NOPE], kpe], axis=-1)
            inv = lax.rsqrt(jnp.sum(k_slab * k_slab, -1, keepdims=True) * (1.0 / QK_HEAD) + EPS)
            d_slab, dg = _head_norm_rope_bwd(dk_ref[h], k_slab, gk_ref[...], inv, cos_t, sin_t)
            dkvp = jnp.concatenate([d_slab[:, 0:QK_NOPE], dv_ref[h]], axis=-1).astype(BF16)
            dkvp_ref[:, h * QK_PAD:(h + 1) * QK_PAD] = dkvp
            dca += _dot_nt(dkvp, wkv_ref[h])
            dkpe += d_slab[:, QK_NOPE:QK_PAD]
            dgk += jnp.sum(dg, 0, keepdims=True)
        dx, dg = _rms_bwd(dqa, ql_t, qa_ref[...], inv_qa, Q_LORA)
        dql_ref[...] = dx.astype(BF16)
        _acc(dqa_ref, jnp.sum(dg, 0, keepdims=True))
        dx, dg = _rms_bwd(dca, ckv, ka_ref[...], inv_ca, KV_LORA)
        dkvl_ref[:, 0:KV_LORA] = dx.astype(BF16)
        dkvl_ref[:, KV_LORA:KV_LAT_PAD] = dkpe.astype(BF16)
        _acc(dka_ref, jnp.sum(dg, 0, keepdims=True))
        _acc(dgq_ref, dgq)
        _acc(dgk_ref, dgk)

    row_outs = [((n, Q_LORA), BF16), ((n, KV_LAT_PAD), BF16), ((n, Q_LORA), BF16), ((n, N_HEADS * QK_PAD), BF16),
                ((n, KV_LORA), BF16), ((n, N_HEADS * (QK_NOPE + V_HEAD)), BF16)]
    acc_outs = [((1, Q_LORA), F32), ((1, KV_LORA), F32), ((1, QK_PAD), F32), ((1, QK_PAD), F32)]
    return _row_call(body, "qkv_prep_bwd", n, ROW_T, [ql, kvl, cos_t, sin_t, dq, dk, dv],
                     [q_a_norm, kv_a_norm, wq, wkv, gq, gk], row_outs, acc_outs, xch=xch)


def _glu_bwd(dy_ssm, y, w_glu, b_glu, w_o_ssm):
    n = y.shape[0]

    def body(dys_ref, y_ref, wg_ref, bg_ref, wo_ref, dy_ref, zg_ref, z_ref, dt_ref, db_ref):
        y_t = y_ref[...]
        z, th = _gelu(y_t)
        zb = z.astype(BF16)
        z_ref[...] = zb
        s = _sigmoid(_dot(zb, wg_ref[...]) + bg_ref[...])
        zg_ref[...] = (z * s).astype(BF16)
        dys = dys_ref[...]
        dzg = jnp.zeros_like(y_t)
        for j in range(N_DEV):
            dzg += _dot_nt(dys[:, j * OUT_SHARD:(j + 1) * OUT_SHARD], wo_ref[j])
        dt = dzg * z * s * (1.0 - s)
        dtb = dt.astype(BF16)
        dt_ref[...] = dtb
        dz = dzg * s + _dot_nt(dtb, wg_ref[...])
        dy_ref[...] = dz * _gelu_grad(y_t, th)
        _acc(db_ref, jnp.sum(dt, 0, keepdims=True))

    outs = [((n, SSM_WIDTH), F32)] + [((n, SSM_WIDTH), BF16)] * 3
    return _row_call(body, "glu_bwd", n, ROW_T, [dy_ssm, y], [w_glu, b_glu, w_o_ssm], outs, [((1, SSM_WIDTH), F32)])


def _ssm_bwd(u, dy, st, bblk, cblk, lam, d_row, xch):
    n = u.shape[0]
    t = min(SCAN_T, n)
    nc = n // t
    kb = 512
    perm = _perm_matrix(t)

    def body(u_ref, dy_ref, st_ref, p_ref, bblk_ref, cblk_ref, lam_ref, d_ref,
             du_ref, xs_ref, as_ref, up_ref, dyp_ref, dlam_ref, dd_ref,
             buf_x, buf_a, pw_ref, carry_ref, xcarry_ref, sx_ref, sa_ref):
        @pl.when(pl.program_id(0) == 0)
        def _():
            carry_ref[...] = jnp.zeros_like(carry_ref)
            _power_table(lam_ref, pw_ref, t // SUBCHUNKS)

        u_t = u_ref[...]
        dy_t = dy_ref[...]
        p = p_ref[...]
        ub = _dot(p, u_t.astype(BF16)).astype(BF16)
        dyb = _dot(p, dy_t.astype(BF16)).astype(BF16)
        up_ref[...] = ub
        dyp_ref[...] = dyb
        for c in range(0, 2 * N_STATE, kb):
            buf_x[:, c:c + kb] = _dot(ub, bblk_ref[:, c:c + kb])
        xcarry_ref[...] = st_ref[0]
        _run_scan(buf_x, lam_ref, t, False)
        _run_carries(buf_x, pw_ref, xcarry_ref, sx_ref, t, False)
        _run_fix(buf_x, pw_ref, sx_ref, t, False)
        for c in range(0, 2 * N_STATE, kb):
            buf_a[:, c:c + kb] = _dot_nt(dyb, cblk_ref[c:c + kb, :])
        _run_scan(buf_a, lam_ref, t, True)
        _run_carries(buf_a, pw_ref, carry_ref, sa_ref, t, True)
        _run_fix(buf_a, pw_ref, sa_ref, t, True)
        dup = jnp.zeros((t, SSM_WIDTH), F32)
        for c in range(0, 2 * N_STATE, kb):
            adb = buf_a[:, c:c + kb].astype(BF16)
            as_ref[:, c:c + kb] = adb
            xs_ref[:, c:c + kb] = buf_x[:, c:c + kb].astype(BF16)
            dup += _dot_nt(adb, bblk_ref[:, c:c + kb])
        du_ref[...] = (d_ref[...] * dy_t + _unpermute(p, dup)).astype(BF16)
        for c in range(0, N_STATE, kb):
            re, im = pl.ds(c, kb), pl.ds(N_STATE + c, kb)
            xr, xi = buf_x[pl.ds(0, t - 8), re], buf_x[pl.ds(0, t - 8), im]
            ar, ai = buf_a[pl.ds(8, t - 8), re], buf_a[pl.ds(8, t - 8), im]
            x0r, x0i = sx_ref[:, re], sx_ref[:, im]
            a0r, a0i = buf_a[0:8, re], buf_a[0:8, im]
            dlam_part_re = (jnp.sum(ar * xr + ai * xi, 0, keepdims=True)
                            + jnp.sum(a0r * x0r + a0i * x0i, 0, keepdims=True))
            dlam_part_im = (jnp.sum(ai * xr - ar * xi, 0, keepdims=True)
                            + jnp.sum(a0i * x0r - a0r * x0i, 0, keepdims=True))

            @pl.when(pl.program_id(0) == 0)
            def _(c=c):
                dlam_ref[0:1, c:c + kb] = jnp.zeros((1, kb), F32)
                dlam_ref[1:2, c:c + kb] = jnp.zeros((1, kb), F32)

            dlam_ref[0:1, c:c + kb] += dlam_part_re
            dlam_ref[1:2, c:c + kb] += dlam_part_im
        _acc(dd_ref, jnp.sum(dy_t * u_t, 0, keepdims=True))

    rev = lambda i: (nc - 1 - i, 0)
    consts = [perm, bblk, cblk, lam, d_row]
    return _call(
        body, "ssm_bwd", (nc,), [u, dy, st] + consts,
        [pl.BlockSpec((t, SSM_WIDTH), rev), pl.BlockSpec((t, SSM_WIDTH), rev),
         pl.BlockSpec((1, 8, 2 * N_STATE), lambda i: (nc - 1 - i, 0, 0))] + [_const(a) for a in consts],
        [_sds((n, SSM_WIDTH), BF16), _sds((n, 2 * N_STATE), BF16), _sds((n, 2 * N_STATE), BF16),
         _sds((n, SSM_WIDTH), BF16), _sds((n, SSM_WIDTH), BF16), _sds((2, N_STATE), F32), _sds((1, SSM_WIDTH), F32)],
        [pl.BlockSpec((t, SSM_WIDTH), rev), pl.BlockSpec((t, 2 * N_STATE), rev), pl.BlockSpec((t, 2 * N_STATE), rev),
         pl.BlockSpec((t, SSM_WIDTH), rev), pl.BlockSpec((t, SSM_WIDTH), rev),
         pl.BlockSpec((2, N_STATE), lambda i: (0, 0)), pl.BlockSpec((1, SSM_WIDTH), lambda i: (0, 0))],
        scratch=[pltpu.VMEM((t, 2 * N_STATE), F32)] * 2 + [pltpu.VMEM((t // SUBCHUNKS, 2 * N_STATE), F32)]
        + [pltpu.VMEM((8, 2 * N_STATE), F32)] * 4,
        xch=xch)


def _in_proj_bwd(pieces, dh, x, norm_mix, w_in_pad):
    n = x.shape[0]

    def body(du_ref, dql_ref, dkvl_ref, dgs_ref, dgm_ref, dh_ref, x_ref, g_ref, w_ref, dx_ref, dp_ref, dg_ref):
        dxn = jnp.zeros((dh_ref.shape[0], D_MODEL), F32)
        for ref, (a, b) in zip((du_ref, dql_ref, dkvl_ref, dgs_ref, dgm_ref), IN_SEGS):
            piece = ref[...]
            dp_ref[:, a:b] = piece
            dxn += _dot_nt(piece, w_ref[:, a:b])
        x_t = x_ref[...]
        inv = lax.rsqrt(jnp.sum(x_t * x_t, -1, keepdims=True) * (1.0 / D_MODEL) + EPS)
        dx, dg = _rms_bwd(dxn, x_t, g_ref[...], inv, D_MODEL)
        dx_ref[...] = dh_ref[...] + dx
        _acc(dg_ref, jnp.sum(dg, 0, keepdims=True))

    outs = [((n, D_MODEL), F32), ((n, D_IN_PAD), BF16)]
    return _row_call(body, "in_proj_bwd", n, ROW_T, list(pieces) + [dh, x], [norm_mix, w_in_pad], outs,
                     [((1, D_MODEL), F32)])


def _block_diag(a, rows_per_group, cols_per_group):
    eye = jnp.eye(SSM_GROUPS, dtype=a.dtype)
    return (a[:, :, None, :] * eye[:, None, :, None]).reshape(SSM_GROUPS * rows_per_group, SSM_GROUPS * cols_per_group)


def _block_diag_extract(m, rows_per_group, cols_per_group):
    m4 = m.reshape(SSM_GROUPS, rows_per_group, SSM_GROUPS, cols_per_group)
    eye = jnp.eye(SSM_GROUPS, dtype=m.dtype)
    return jnp.sum(m4 * eye[:, None, :, None], axis=2)


def _pad_in(w):
    return jnp.concatenate([w[:, :KV_END], jnp.zeros((w.shape[0], D_IN_PAD - D_IN), w.dtype), w[:, KV_END:]], axis=1)


def _unpad_in(w):
    return jnp.concatenate([w[:, :KV_END], w[:, KV_END + D_IN_PAD - D_IN:]], axis=1)


def _pad_gain(g):
    return jnp.pad(g, ((0, 0), (0, QK_PAD - QK_HEAD)))


def _place():
    x, y, c = lax.axis_index("x"), lax.axis_index("y"), lax.axis_index("c")
    chips = [(x, y), (1 - x, y), (x, 1 - y), (1 - x, 1 - y)]
    return x, y, c, chips


def _all_gather(block, name):
    rows, lanes = block.shape

    def body(x_ref, out_ref, send_sems, recv_sems, local_sem):
        x, y, c, chips = _place()
        me, sibling = (x, y, c), (x, y, 1 - c)

        def slot(px, py, pc):
            return out_ref.at[4 * px + 2 * py + pc]

        def copy(k, blk, to, src=None):
            return pltpu.make_async_remote_copy(
                src_ref=slot(*blk) if src is None else src, dst_ref=slot(*blk),
                send_sem=send_sems.at[k], recv_sem=recv_sems.at[k], device_id=to, device_id_type=MESH)

        mine = pltpu.make_async_copy(x_ref, slot(*me), local_sem)
        mine.start()
        first = [copy(0, me, sibling, src=x_ref)]
        first += [copy(1 + j, me, (*chip, c), src=x_ref) for j, chip in enumerate(chips[1:])]
        for cp in first:
            cp.start()
        passed = [copy(4 + j, (*chip, c), sibling) for j, chip in enumerate(chips[1:])]
        for j, chip in enumerate(chips[1:]):
            copy(1 + j, (*chip, c), me).wait_recv()
            passed[j].start()
        copy(0, sibling, me).wait_recv()
        for j, chip in enumerate(chips[1:]):
            copy(4 + j, (*chip, 1 - c), me).wait_recv()
        for cp in first + passed:
            cp.wait_send()
        mine.wait()

    return pl.pallas_call(
        body,
        name=name,
        in_specs=[ANY],
        out_specs=ANY,
        out_shape=_sds((N_DEV, rows, lanes), block.dtype),
        scratch_shapes=[pltpu.SemaphoreType.DMA((7,)), pltpu.SemaphoreType.DMA((7,)), pltpu.SemaphoreType.DMA],
    )(block)


RS_CHUNKS = 4


def _reduce_scatter(parts, name):
    _, rows, lanes = parts.shape
    ch = rows // RS_CHUNKS

    def body(p_ref, out_ref, land_a, send_b, land_b, va, vb, v16, w16, sa, ra, sb, rb):
        x, y, c, chips = _place()
        sibling = (x, y, 1 - c)

        def blk(chip, core):
            return p_ref.at[4 * chip[0] + 2 * chip[1] + core]

        to_sib = [pltpu.make_async_remote_copy(
            src_ref=blk(chips[k], 1 - c), dst_ref=land_a.at[k], send_sem=sa.at[k], recv_sem=ra.at[k],
            device_id=sibling, device_id_type=MESH) for k in range(4)]
        for cp in to_sib:
            cp.start()
        to_chip = [pltpu.make_async_remote_copy(
            src_ref=send_b.at[j], dst_ref=land_b.at[j], send_sem=sb.at[j], recv_sem=rb.at[j],
            device_id=(*chips[1 + j], c), device_id_type=MESH) for j in range(3)]

        for k in (1, 2, 3, 0):
            to_sib[k].wait_recv()

            def chip_sum(i, carry, k=k):
                r = pl.ds(pl.multiple_of(i * ch, 16), ch)
                pltpu.sync_copy(blk(chips[k], c).at[r], va)
                pltpu.sync_copy(land_a.at[k, r], vb)
                if k == 0:
                    va[...] = va[...] + vb[...]
                    pltpu.sync_copy(va, out_ref.at[r])
                else:
                    v16[...] = (va[...] + vb[...]).astype(BF16)
                    pltpu.sync_copy(v16, send_b.at[k - 1, r])
                return carry

            lax.fori_loop(0, RS_CHUNKS, chip_sum, 0)
            if k != 0:
                to_chip[k - 1].start()

        for cp in to_chip:
            cp.wait_recv()

        def final_sum(i, carry):
            r = pl.ds(pl.multiple_of(i * ch, 16), ch)
            pltpu.sync_copy(out_ref.at[r], va)
            acc = va[...]
            for j in range(3):
                pltpu.sync_copy(land_b.at[j, r], w16)
                acc = acc + w16[...].astype(F32)
            va[...] = acc
            pltpu.sync_copy(va, out_ref.at[r])
            return carry

        lax.fori_loop(0, RS_CHUNKS, final_sum, 0)
        for cp in to_sib + to_chip:
            cp.wait_send()

    outs = pl.pallas_call(
        body,
        name=name,
        in_specs=[ANY],
        out_specs=[ANY] * 4,
        out_shape=[_sds((rows, lanes), F32), _sds((4, rows, lanes), F32), _sds((3, rows, lanes), BF16),
                   _sds((3, rows, lanes), BF16)],
        scratch_shapes=[pltpu.VMEM((ch, lanes), F32), pltpu.VMEM((ch, lanes), F32), pltpu.VMEM((ch, lanes), BF16),
                        pltpu.VMEM((ch, lanes), BF16)]
        + [pltpu.SemaphoreType.DMA((4,))] * 2 + [pltpu.SemaphoreType.DMA((3,))] * 2,
    )(parts)
    return outs[0]


def _adamw_math(w, g, m, v):
    m = ADAM_B1 * m + (1.0 - ADAM_B1) * g
    v = ADAM_B2 * v + (1.0 - ADAM_B2) * (g * g)
    m_hat = m / (1.0 - ADAM_B1 ** ADAM_STEP)
    v_hat = v / (1.0 - ADAM_B2 ** ADAM_STEP)
    delta = -ADAM_LR * (m_hat / (jnp.sqrt(v_hat) + ADAM_EPS) + ADAM_WD * w)
    return delta, m, v


def _row_tile(r):
    return max(t for t in range(8, min(r, 256) + 1, 8) if r % t == 0)


def _adamw(w, g, m, v, name):
    r, n = w.shape

    def body(w_ref, g_ref, m_ref, v_ref, d_ref, nm_ref, nv_ref):
        d_ref[...], nm_ref[...], nv_ref[...] = _adamw_math(w_ref[...], g_ref[...], m_ref[...], v_ref[...])

    return _row_call(body, name, r, _row_tile(r), [w, g, m, v], [], [((r, n), F32)] * 3)


def _adamw_sum(landed, w, m, v, name):
    r, n = w.shape

    def body(l_ref, w_ref, m_ref, v_ref, g_ref, d_ref, nm_ref, nv_ref):
        g = l_ref[0].astype(F32)
        for dev in range(1, N_DEV):
            g = g + l_ref[dev].astype(F32)
        g_ref[...] = g
        d_ref[...], nm_ref[...], nv_ref[...] = _adamw_math(w_ref[...], g, m_ref[...], v_ref[...])

    tm = max(t for t in range(16, min(r, 256) + 1, 16) if r % t == 0)
    return _row_call(body, name, r, tm, [landed, w, m, v], [], [((r, n), F32)] * 4)


def _adamw_small(gathered, w, m, v):
    def body(ga_ref, w_ref, m_ref, v_ref, g_ref, d_ref, nm_ref, nv_ref):
        g = ga_ref[0]
        for dev in range(1, N_DEV):
            g = g + ga_ref[dev]
        g_ref[...] = g
        d_ref[...], nm_ref[...], nv_ref[...] = _adamw_math(w_ref[...], g, m_ref[...], v_ref[...])

    return pl.pallas_call(body, name="adamw_small", out_shape=[_sds(w.shape, F32)] * 4, compiler_params=_params())(
        gathered, w, m, v)


SMALL = ("norm_mix", "q_a_norm", "kv_a_norm", "q_norm", "k_norm", "ssm_a_re", "ssm_a_im", "ssm_log_dt", "ssm_b_re",
         "ssm_b_im", "ssm_c_re", "ssm_c_im", "ssm_d", "b_glu", "norm_mlp")
WEIGHT_ORDER = ("norm_mix", "w_in", "q_a_norm", "kv_a_norm", "w_q_b", "w_kv_b", "q_norm", "k_norm", "w_o_mla",
                "ssm_a_re", "ssm_a_im", "ssm_log_dt", "ssm_b_re", "ssm_b_im", "ssm_c_re", "ssm_c_im", "ssm_d", "w_glu",
                "b_glu", "w_o_ssm", "w_out", "norm_mlp", "w_up", "w_down")
IN_SHARD = D_IN // N_DEV
Q_SHARD = QK_HEAD


def _pack_small(vals):
    flat = jnp.concatenate([vals[n].reshape(-1) for n in SMALL])
    rows = -(-flat.shape[0] // (8 * LANES)) * 8
    return jnp.pad(flat, (0, rows * LANES - flat.shape[0])).reshape(rows, LANES)


def _unpack_small(packed, like):
    flat, out, off = packed.reshape(-1), {}, 0
    for n in SMALL:
        size = like[n].size
        out[n] = flat[off:off + size].reshape(like[n].shape)
        off += size
    return out


def _step(x, pos_col, target, w, small):
    bf = {n: a.astype(BF16) for n, a in w.items()}
    gq, gk = _pad_gain(small["q_norm"]), _pad_gain(small["k_norm"])
    a_re = small["ssm_a_re"].reshape(1, N_STATE)
    a_im = small["ssm_a_im"].reshape(1, N_STATE)
    log_dt = jnp.repeat(small["ssm_log_dt"].reshape(SSM_GROUPS), SSM_STATE).reshape(1, N_STATE)
    b_re_x = _block_diag(jnp.transpose(small["ssm_b_re"][0], (0, 2, 1)), SSM_GROUP_CH, SSM_STATE)
    b_im_x = _block_diag(jnp.transpose(small["ssm_b_im"][0], (0, 2, 1)), SSM_GROUP_CH, SSM_STATE)
    c_re_x = _block_diag(jnp.transpose(small["ssm_c_re"][0], (0, 2, 1)), SSM_STATE, SSM_GROUP_CH)
    c_im_x = _block_diag(jnp.transpose(small["ssm_c_im"][0], (0, 2, 1)), SSM_STATE, SSM_GROUP_CH)
    cblk = jnp.concatenate([c_re_x, -c_im_x], axis=0).astype(BF16)
    d_row = small["ssm_d"].reshape(1, SSM_WIDTH)

    w_in_all = _all_gather(bf["w_in"].reshape(-1, LANES), "gather_w_in").reshape(N_DEV, D_MODEL, IN_SHARD)
    w_in_pad = _pad_in(jnp.transpose(w_in_all, (1, 0, 2)).reshape(D_MODEL, D_IN))
    cos_t, sin_t = _rope_tables(pos_col)
    lam, bblk = _ssm_prep(a_re, a_im, log_dt, b_re_x, b_im_x)
    wq_mine = jnp.pad(bf["w_q_b"], ((0, 0), (0, QK_PAD - QK_HEAD)))
    xn, u, ql, kvl, gs, gm, w_glu, w_o_ssm = _in_proj(
        x, small["norm_mix"], w_in_pad, xch=[(bf["w_glu"], False), (bf["w_o_ssm"], False)])
    w_glu = w_glu.reshape(SSM_WIDTH, SSM_WIDTH)
    y, y_ssm, st, wq, wkv, w_o_mla, w_out = _ssm_fwd(
        u, bblk, cblk, lam, d_row, w_glu, small["b_glu"], w_o_ssm,
        xch=[(wq_mine, False), (bf["w_kv_b"], False), (bf["w_o_mla"], False), (bf["w_out"], False)])
    w_o_mla, w_out = w_o_mla.reshape(D_MODEL, D_MODEL), w_out.reshape(D_MODEL, D_MODEL)
    q, k, v, kt, vt = _qkv_prep(ql, kvl, small["q_a_norm"], small["kv_a_norm"], wq, wkv, gq, gk, cos_t, sin_t)
    attn, lse, w_up, w_down = _attn_fwd(q, k, vt, xch=[(bf["w_up"], False), (bf["w_down"], False)])
    h, mixed, y_mla = _merge(attn, gs, gm, y_ssm, x, w_o_mla, w_out)
    hn, dout, loss = _mlp_fwd_loss(h, target, small["norm_mlp"], w_up, w_down)

    hid, da, dh, d_norm_mlp = _mlp_bwd(dout, hn, h, small["norm_mlp"], w_up, w_down)
    p_w_down = _matmul_tn_shards(hid, dout, "dw_down", False)
    p_w_up = _matmul_tn_shards(hn, da, "dw_up", True)
    dgs, dgm, dy_ssm, dy_mla, dattn = _merge_bwd(dh, gs, gm, y_ssm, y_mla, w_out, w_o_mla)
    p_w_out = _matmul_tn_shards(mixed, dh, "dw_out", False)
    p_w_o_mla = _matmul_tn_shards(attn, dy_mla, "dw_o_mla", False)
    dq, dk, dv, l_w_up, l_w_down, l_w_out, l_w_o_mla = _attn_bwd(
        q, k, kt, v, attn, lse, dattn, xch=[(p_w_up, True), (p_w_down, True), (p_w_out, True), (p_w_o_mla, True)])
    dql, dkvl, qa, dq_pre, ca, dkv_pre, d_q_a_norm, d_kv_a_norm, d_gq, d_gk = _qkv_prep_bwd(
        ql, kvl, dq, dk, dv, small["q_a_norm"], small["kv_a_norm"], wq, wkv, gq, gk, cos_t, sin_t, xch=[])
    p_wq = _matmul_tn_shards(qa, dq_pre, "dw_q_b", True)
    p_wkv = _matmul_tn_shards(ca, dkv_pre, "dw_kv_b", True)
    dy, zg, z, dt, d_b_glu = _glu_bwd(dy_ssm, y, w_glu, small["b_glu"], w_o_ssm)
    p_w_o_ssm = _matmul_tn_shards(zg, dy_ssm, "dw_o_ssm", True)
    p_w_glu = _matmul_tn_shards(z, dt, "dw_glu", False)
    du, xs, ads, u_scan, dy_scan, dlam, d_d, l_wq, l_wkv, l_w_glu, l_w_o_ssm = _ssm_bwd(
        u, dy, st, bblk, cblk, lam, d_row, xch=[(p_wq, True), (p_wkv, True), (p_w_glu, True), (p_w_o_ssm, True)])
    d_bblk = _matmul_tn(u_scan, ads, "d_bblk")
    d_cblk_t = _matmul_tn(dy_scan, xs, "d_cblk")
    d_a_re, d_a_im, d_log_dt, d_b_re_x, d_b_im_x = _ssm_prep_bwd(a_re, a_im, log_dt, b_re_x, b_im_x, dlam, d_bblk)
    dx, dproj, d_norm_mix = _in_proj_bwd((du, dql, dkvl, dgs, dgm), dh, x, small["norm_mix"], w_in_pad)
    g_w_in = _unpad_in(_matmul_tn(xn, dproj, "dw_in"))
    parts = jnp.transpose(g_w_in.reshape(D_MODEL, N_DEV, IN_SHARD), (1, 0, 2)).reshape(N_DEV, -1, LANES)
    g_w_in_mine = _reduce_scatter(parts, "reduce_w_in").reshape(D_MODEL, IN_SHARD)

    tr = lambda mat: jnp.transpose(mat, (0, 2, 1))[None]
    g_small = {
        "norm_mix": d_norm_mix, "q_a_norm": d_q_a_norm, "kv_a_norm": d_kv_a_norm,
        "q_norm": d_gq[:, :QK_HEAD], "k_norm": d_gk[:, :QK_HEAD],
        "ssm_a_re": d_a_re, "ssm_a_im": d_a_im, "ssm_log_dt": d_log_dt,
        "ssm_b_re": tr(_block_diag_extract(d_b_re_x, SSM_GROUP_CH, SSM_STATE)),
        "ssm_b_im": tr(_block_diag_extract(d_b_im_x, SSM_GROUP_CH, SSM_STATE)),
        "ssm_c_re": _block_diag_extract(d_cblk_t[:, :N_STATE], SSM_GROUP_CH, SSM_STATE),
        "ssm_c_im": -_block_diag_extract(d_cblk_t[:, N_STATE:], SSM_GROUP_CH, SSM_STATE),
        "ssm_d": d_d, "b_glu": d_b_glu, "norm_mlp": d_norm_mlp,
    }
    g_small_all = _all_gather(_pack_small(g_small), "gather_small_grads")
    landed = {"w_q_b": l_wq[:, :, :QK_HEAD], "w_kv_b": l_wkv, "w_o_mla": l_w_o_mla, "w_glu": l_w_glu,
              "w_o_ssm": l_w_o_ssm, "w_out": l_w_out, "w_up": l_w_up, "w_down": l_w_down}
    return loss, dx, landed, g_w_in_mine, g_small_all


def kernel(x, positions, norm_mix, w_in, q_a_norm, kv_a_norm, w_q_b, w_kv_b, q_norm, k_norm, w_o_mla, ssm_a_re, ssm_a_im, ssm_log_dt, ssm_b_re, ssm_b_im, ssm_c_re, ssm_c_im, ssm_d, w_glu, b_glu, w_o_ssm, w_out, norm_mlp, w_up, w_down, loss_target, m_norm_mix, m_w_in, m_q_a_norm, m_kv_a_norm, m_w_q_b, m_w_kv_b, m_q_norm, m_k_norm, m_w_o_mla, m_ssm_a_re, m_ssm_a_im, m_ssm_log_dt, m_ssm_b_re, m_ssm_b_im, m_ssm_c_re, m_ssm_c_im, m_ssm_d, m_w_glu, m_b_glu, m_w_o_ssm, m_w_out, m_norm_mlp, m_w_up, m_w_down, v_norm_mix, v_w_in, v_q_a_norm, v_kv_a_norm, v_w_q_b, v_w_kv_b, v_q_norm, v_k_norm, v_w_o_mla, v_ssm_a_re, v_ssm_a_im, v_ssm_log_dt, v_ssm_b_re, v_ssm_b_im, v_ssm_c_re, v_ssm_c_im, v_ssm_d, v_w_glu, v_b_glu, v_w_o_ssm, v_w_out, v_norm_mlp, v_w_up, v_w_down):
    given = dict(locals())
    w = {n: given[n] for n in WEIGHT_ORDER}
    m = {n: given["m_" + n] for n in WEIGHT_ORDER}
    v = {n: given["v_" + n] for n in WEIGHT_ORDER}
    big = [n for n in WEIGHT_ORDER if n not in SMALL]
    small = {n: w[n] for n in SMALL}

    loss, dx, landed, g_w_in, g_small_all = _step(
        x[0], positions.reshape(-1, 1), loss_target[0], {n: w[n][0] for n in big}, small)

    grads, deltas, new_m, new_v = {}, {}, {}, {}
    for n in big:
        if n == "w_in":
            g = g_w_in
            d, nm, nv = _adamw(w[n][0], g, m[n][0], v[n][0], "adamw_" + n)
        else:
            g, d, nm, nv = _adamw_sum(landed[n], w[n][0], m[n][0], v[n][0], "adamw_" + n)
        grads[n], deltas[n], new_m[n], new_v[n] = g[None], d[None], nm[None], nv[None]

    packed = _adamw_small(g_small_all, _pack_small(small), _pack_small({n: m[n] for n in SMALL}),
                          _pack_small({n: v[n] for n in SMALL}))
    for dst, src in zip((grads, deltas, new_m, new_v), packed):
        dst.update(_unpack_small(src, small))

    total = lax.psum(loss[0, 0], ("x", "y", "c"))
    return (total, dx[None], *[grads[n] for n in WEIGHT_ORDER], *[deltas[n] for n in WEIGHT_ORDER],
            *[new_m[n] for n in WEIGHT_ORDER], *[new_v[n] for n in WEIGHT_ORDER])
```

```python
import functools
import math

import numpy as np
import jax
import jax.numpy as jnp
from jax import lax
from jax.experimental import pallas as pl
from jax.experimental.pallas import tpu as pltpu

F32 = jnp.float32
BF16 = jnp.bfloat16

D_MODEL = 1024
SSM_GROUPS = 32
SSM_GROUP_CH = 16
SSM_WIDTH = 512
SSM_STATE = 64
N_STATE = SSM_GROUPS * SSM_STATE
N_HEADS = 8
QK_NOPE = 128
QK_ROPE = 64
QK_HEAD = 192
QK_PAD = 256
V_HEAD = 128
Q_LORA = 384
KV_LORA = 256
KV_LAT_PAD = 384
ROPE_THETA = 10000.0
D_FF = 4096
EPS = 1e-6
ATT_SCALE = QK_HEAD ** -0.5
N_DEV = 8
FF_SHARD = D_FF // N_DEV
OUT_SHARD = D_MODEL // N_DEV

IN_SEGS = ((0, 512), (512, 896), (896, 1280), (1280, 2304), (2304, 3328))
D_IN = 3264
D_IN_PAD = 3328
KV_END = 1216

ADAM_LR = 0.001
ADAM_B1 = 0.9
ADAM_B2 = 0.999
ADAM_EPS = 1e-08
ADAM_WD = 0.01
ADAM_STEP = 10

VMEM_LIMIT = 56 * 1024 * 1024
MESH = pl.DeviceIdType.MESH
ANY = pl.BlockSpec(memory_space=pl.ANY)
LANES = 128

SCAN_T = 256
SUBCHUNKS = 8
SCAN_CG = 512
ATT_T = 512
ATT_SUB = 2
ROW_T = 256


def _params(sem=None):
    return pltpu.CompilerParams(dimension_semantics=sem, vmem_limit_bytes=VMEM_LIMIT)


def _rows(arr, tm):
    if arr.ndim == 2:
        return pl.BlockSpec((tm, arr.shape[1]), lambda i: (i, 0))
    return pl.BlockSpec((arr.shape[0], tm, arr.shape[2]), lambda i: (0, i, 0))


def _const(arr):
    nd = arr.ndim
    return pl.BlockSpec(arr.shape, lambda *_: (0,) * nd)


def _sds(shape, dtype):
    return jax.ShapeDtypeStruct(shape, dtype)


PEERS = tuple((dx, dy, dc) for dx in (0, 1) for dy in (0, 1) for dc in (0, 1) if (dx, dy, dc) != (0, 0, 0))


def _here():
    x, y, c = lax.axis_index("x"), lax.axis_index("y"), lax.axis_index("c")
    return x, y, c, 4 * x + 2 * y + c


def _xchg_start(scatter, srcs, dsts, send, recv, local):
    x, y, c, me = _here()
    for e, sc in enumerate(scatter):
        src, dst = srcs[e], dsts[e]
        pltpu.make_async_copy(src.at[me] if sc else src, dst.at[me], local.at[e]).start()
        for dx, dy, dc in PEERS:
            px, py, pc = (1 - x if dx else x), (1 - y if dy else y), (1 - c if dc else c)
            pltpu.make_async_remote_copy(
                src_ref=src.at[4 * px + 2 * py + pc] if sc else src, dst_ref=dst.at[me],
                send_sem=send.at[e], recv_sem=recv.at[e], device_id=(px, py, pc), device_id_type=MESH).start()


def _xchg_wait(scatter, srcs, dsts, send, recv, local):
    x, y, c, me = _here()
    for e, sc in enumerate(scatter):
        src, dst = srcs[e], dsts[e]
        pltpu.make_async_copy(src.at[me] if sc else src, dst.at[me], local.at[e]).wait()
        span = dst.at[pl.ds(0, N_DEV - 1)]
        both = pltpu.make_async_remote_copy(src_ref=span, dst_ref=span, send_sem=send.at[e], recv_sem=recv.at[e],
                                            device_id=(x, y, c), device_id_type=MESH)
        both.wait_send()
        both.wait_recv()


def _call(body, name, grid, ins, in_specs, outs, out_specs, scratch=(), xch=()):
    n_in, n_out, ne = len(ins), len(outs), len(xch)
    scatter = [sc for _, sc in xch]
    x_outs = [_sds((N_DEV,) + (a.shape[1:] if sc else a.shape), a.dtype) for a, sc in xch]
    sems = [pltpu.SemaphoreType.DMA((ne,))] * 3 if ne else []

    def wrapped(*refs):
        in_refs, x_src = refs[:n_in], refs[n_in:n_in + ne]
        out_refs = refs[n_in + ne:n_in + ne + n_out]
        x_dst = refs[n_in + ne + n_out:n_in + 2 * ne + n_out]
        rest = refs[n_in + 2 * ne + n_out:]
        if ne:
            x_sems, rest = rest[len(rest) - 3:], rest[:len(rest) - 3]
            first = functools.reduce(jnp.logical_and, [pl.program_id(d) == 0 for d in range(len(grid))])
            last = functools.reduce(jnp.logical_and, [pl.program_id(d) == grid[d] - 1 for d in range(len(grid))])

            @pl.when(first)
            def _():
                _xchg_start(scatter, x_src, x_dst, *x_sems)

        body(*in_refs, *out_refs, *rest)
        if ne:
            @pl.when(last)
            def _():
                _xchg_wait(scatter, x_src, x_dst, *x_sems)

    return pl.pallas_call(
        wrapped,
        name=name,
        grid=grid,
        in_specs=list(in_specs) + [ANY] * ne,
        out_specs=list(out_specs) + [ANY] * ne,
        out_shape=list(outs) + x_outs,
        scratch_shapes=list(scratch) + sems,
        compiler_params=_params(("arbitrary",) * len(grid)),
    )(*ins, *[a for a, _ in xch])


def _row_call(body, name, n_rows, tm, row_ins, const_ins, row_outs, acc_outs=(), xch=()):
    outs = [_sds(s, d) for s, d in row_outs] + [_sds(s, d) for s, d in acc_outs]
    out_specs = [_rows(o, tm) for o in outs[: len(row_outs)]] + [_const(o) for o in outs[len(row_outs):]]
    in_specs = [_rows(a, tm) for a in row_ins] + [_const(a) for a in const_ins]
    return _call(body, name, (n_rows // tm,), list(row_ins) + list(const_ins), in_specs, outs, out_specs, xch=xch)


def _dot(a, b):
    return jnp.dot(a, b, preferred_element_type=F32)


def _dot_nt(a, b):
    return lax.dot_general(a, b, (((1,), (1,)), ((), ())), preferred_element_type=F32)


def _dot_tn(a, b):
    return lax.dot_general(a, b, (((0,), (0,)), ((), ())), preferred_element_type=F32)


def _rms(x, g, n):
    inv = lax.rsqrt(jnp.sum(x * x, -1, keepdims=True) * (1.0 / n) + EPS)
    return x * inv * g, inv


def _rms_bwd(dy, x, g, inv, n):
    xh = x * inv
    dxh = dy * g
    dx = inv * (dxh - xh * (jnp.sum(dxh * xh, -1, keepdims=True) * (1.0 / n)))
    return dx, dy * xh


def _sigmoid(x):
    return 1.0 / (1.0 + jnp.exp(-x))


_GELU_C = math.sqrt(2.0 / math.pi)


def _gelu(y):
    th = jnp.tanh(_GELU_C * (y + 0.044715 * (y * y * y)))
    return 0.5 * y * (1.0 + th), th


def _gelu_grad(y, th):
    return 0.5 * (1.0 + th) + 0.5 * y * (1.0 - th * th) * (_GELU_C * (1.0 + 3.0 * 0.044715 * (y * y)))


def _acc(ref, val):
    @pl.when(pl.program_id(0) == 0)
    def _():
        ref[...] = jnp.zeros_like(ref)

    ref[...] += val


def _tile(n, limit):
    if n <= limit:
        return n
    return max(t for t in range(128, limit + 1, 128) if n % t == 0)


def _matmul_tn(a, b, name, tm=512, tk=512):
    k_dim, m = a.shape
    n = b.shape[1]
    tm, tk = _tile(m, tm), _tile(k_dim, tk)

    def body(a_ref, b_ref, o_ref):
        @pl.when(pl.program_id(1) == 0)
        def _():
            o_ref[...] = jnp.zeros_like(o_ref)

        o_ref[...] += _dot_tn(a_ref[...].astype(BF16), b_ref[...].astype(BF16))

    return pl.pallas_call(
        body,
        name=name,
        grid=(m // tm, k_dim // tk),
        in_specs=[pl.BlockSpec((tk, tm), lambda i, k: (k, i)), pl.BlockSpec((tk, n), lambda i, k: (k, 0))],
        out_specs=pl.BlockSpec((tm, n), lambda i, k: (i, 0)),
        out_shape=_sds((m, n), F32),
        compiler_params=_params(("parallel", "arbitrary")),
    )(a, b)


def _matmul_tn_shards(a, b, name, by_col, tm=512, tk=512):
    k_dim, m = a.shape
    n = b.shape[1]
    tm, tk = _tile(m, tm), _tile(k_dim, tk)
    nk = k_dim // tk
    if by_col:
        r, c = m, n // N_DEV
        out_spec = pl.BlockSpec((N_DEV, tm, c), lambda i, k: (0, i, 0))
    else:
        r, c = m // N_DEV, n
        per = tm // r
        out_spec = pl.BlockSpec((per, r, c), lambda i, k: (i, 0, 0))

    def body(a_ref, b_ref, o_ref, acc_ref):
        k = pl.program_id(1)

        @pl.when(k == 0)
        def _():
            acc_ref[...] = jnp.zeros_like(acc_ref)

        acc_ref[...] += _dot_tn(a_ref[...].astype(BF16), b_ref[...].astype(BF16))

        @pl.when(k == nk - 1)
        def _():
            if by_col:
                for j in range(N_DEV):
                    o_ref[j] = acc_ref[:, j * c:(j + 1) * c].astype(BF16)
            else:
                for s in range(per):
                    o_ref[s] = acc_ref[s * r:(s + 1) * r, :].astype(BF16)

    return pl.pallas_call(
        body,
        name=name,
        grid=(m // tm, nk),
        in_specs=[pl.BlockSpec((tk, tm), lambda i, k: (k, i)), pl.BlockSpec((tk, n), lambda i, k: (k, 0))],
        out_specs=out_spec,
        out_shape=_sds((N_DEV, r, c), BF16),
        scratch_shapes=[pltpu.VMEM((tm, n), F32)],
        compiler_params=_params(("parallel", "arbitrary")),
    )(a, b)


def _rope_tables(pos_col):
    n = pos_col.shape[0]
    half = QK_ROPE // 2
    inv_freq = (ROPE_THETA ** (-np.arange(half, dtype=np.float32) / half)).astype(np.float32)
    freq_row = jnp.asarray(np.concatenate([inv_freq, inv_freq, np.zeros(64, np.float32)])[None, :])

    def body(p_ref, f_ref, c_ref, s_ref):
        ang = p_ref[...].astype(F32) * f_ref[...]
        c_ref[...] = jnp.cos(ang)
        s_ref[...] = jnp.sin(ang)

    return _row_call(body, "rope_tables", n, min(n, 1024), [pos_col], [freq_row], [((n, 128), F32)] * 2)


def _rope_rot(v):
    lane = lax.broadcasted_iota(jnp.int32, v.shape, 1)
    return jnp.where(lane < 32, -pltpu.roll(v, 96, 1), jnp.where(lane < 64, pltpu.roll(v, 32, 1), 0.0))


def _rope_rot_t(v):
    lane = lax.broadcasted_iota(jnp.int32, v.shape, 1)
    return jnp.where(lane < 32, pltpu.roll(v, 96, 1), jnp.where(lane < 64, -pltpu.roll(v, 32, 1), 0.0))


def _in_proj(x, norm_mix, w_in_pad, xch):
    n = x.shape[0]

    def body(x_ref, g_ref, w_ref, xn_ref, u_ref, ql_ref, kvl_ref, gs_ref, gm_ref):
        xn, _ = _rms(x_ref[...], g_ref[...], D_MODEL)
        xb = xn.astype(BF16)
        xn_ref[...] = xb
        for ref, (a, b) in zip((u_ref, ql_ref, kvl_ref, gs_ref, gm_ref), IN_SEGS):
            ref[...] = _dot(xb, w_ref[:, a:b])

    outs = [((n, D_MODEL), BF16)] + [((n, b - a), F32) for a, b in IN_SEGS]
    return _row_call(body, "in_proj", n, ROW_T, [x], [norm_mix, w_in_pad], outs, xch=xch)


def _ssm_prep_fn(a_re, a_im, log_dt, b_re_x, b_im_x):
    dt = jnp.exp(log_dt)
    mag = jnp.exp(a_re * dt)
    lr = mag * jnp.cos(a_im * dt)
    li = mag * jnp.sin(a_im * dt)
    den = a_re * a_re + a_im * a_im
    fr = ((lr - 1.0) * a_re + li * a_im) / den
    fi = (li * a_re - (lr - 1.0) * a_im) / den
    return lr, li, fr * b_re_x - fi * b_im_x, fr * b_im_x + fi * b_re_x


def _dot_exact(a, b, dims):
    return lax.dot_general(a, b, (dims, ((), ())), precision=lax.Precision.HIGHEST, preferred_element_type=F32)


def _lane_repeat(width, n):
    src = lax.broadcasted_iota(jnp.int32, (width, n), 0)
    dst = lax.broadcasted_iota(jnp.int32, (width, n), 1)
    return (dst % width == src).astype(F32)


def _same_group(rows, rows_per_group, cols, cols_per_group):
    row = lax.broadcasted_iota(jnp.int32, (rows, cols), 0)
    col = lax.broadcasted_iota(jnp.int32, (rows, cols), 1)
    return (row // rows_per_group) == (col // cols_per_group)


def _expand_b(bt):
    tiled = _dot_exact(bt, _lane_repeat(SSM_STATE, N_STATE), ((1,), (0,)))
    return jnp.where(_same_group(SSM_WIDTH, SSM_GROUP_CH, N_STATE, SSM_STATE), tiled, 0.0)


def _collect_b(m):
    masked = jnp.where(_same_group(SSM_WIDTH, SSM_GROUP_CH, N_STATE, SSM_STATE), m, 0.0)
    return _dot_exact(masked, _lane_repeat(SSM_STATE, N_STATE), ((1,), (1,)))


def _ssm_prep(a_re, a_im, log_dt, bt_re, bt_im, c2_re, c2_im):
    def body(ar, ai, ld, br, bi, cr, ci, lam_ref, bblk_ref, cblk_ref):
        lr, li, bbr, bbi = _ssm_prep_fn(ar[...], ai[...], ld[...], _expand_b(br[...]), _expand_b(bi[...]))
        lam_ref[0:1, :] = lr
        lam_ref[1:2, :] = li
        bblk_ref[:, 0:N_STATE] = bbr.astype(BF16)
        bblk_ref[:, N_STATE:] = bbi.astype(BF16)
        rep = _lane_repeat(SSM_GROUP_CH, SSM_WIDTH)
        own = _same_group(N_STATE, SSM_STATE, SSM_WIDTH, SSM_GROUP_CH)
        cblk_ref[0:N_STATE, :] = jnp.where(own, _dot_exact(cr[...], rep, ((1,), (0,))), 0.0).astype(BF16)
        cblk_ref[N_STATE:, :] = jnp.where(own, -_dot_exact(ci[...], rep, ((1,), (0,))), 0.0).astype(BF16)

    return pl.pallas_call(
        body,
        name="ssm_prep",
        out_shape=[_sds((2, N_STATE), F32), _sds((SSM_WIDTH, 2 * N_STATE), BF16),
                   _sds((2 * N_STATE, SSM_WIDTH), BF16)],
        compiler_params=_params(),
    )(a_re, a_im, log_dt, bt_re, bt_im, c2_re, c2_im)


def _ssm_prep_bwd(a_re, a_im, log_dt, bt_re, bt_im, dlam, dbblk, dcblk_t):
    def body(ar, ai, ld, br, bi, dl, db, dc, dar, dai, dld, dbr, dbi, dcr, dci):
        _, vjp = jax.vjp(_ssm_prep_fn, ar[...], ai[...], ld[...], _expand_b(br[...]), _expand_b(bi[...]))
        g = vjp((dl[0:1, :], dl[1:2, :], db[:, 0:N_STATE], db[:, N_STATE:]))
        dar[...] = g[0]
        dai[...] = g[1]
        grp = lax.broadcasted_iota(jnp.int32, (SSM_GROUPS, N_STATE), 0)
        lane = lax.broadcasted_iota(jnp.int32, (SSM_GROUPS, N_STATE), 1)
        sel = (lane // SSM_STATE) == grp
        dld[...] = jnp.sum(jnp.where(sel, jnp.broadcast_to(g[2], (SSM_GROUPS, N_STATE)), 0.0), axis=1, keepdims=True)
        dbr[...] = _collect_b(g[3])
        dbi[...] = _collect_b(g[4])
        dcr[...] = _collect_b(dc[:, 0:N_STATE])
        dci[...] = -_collect_b(dc[:, N_STATE:])

    small = _sds((SSM_WIDTH, SSM_STATE), F32)
    return pl.pallas_call(
        body,
        name="ssm_prep_bwd",
        out_shape=[_sds((1, N_STATE), F32), _sds((1, N_STATE), F32), _sds((SSM_GROUPS, 1), F32), small, small, small, small],
        compiler_params=_params(),
    )(a_re, a_im, log_dt, bt_re, bt_im, dlam, dbblk, dcblk_t)


def _perm_matrix(t):
    run = t // SUBCHUNKS
    p = np.zeros((t, t), np.float32)
    r = np.arange(t)
    p[r, (r % SUBCHUNKS) * run + r // SUBCHUNKS] = 1.0
    return jnp.asarray(p, dtype=BF16)


def _unpermute(p, a):
    hi = a.astype(BF16)
    r1 = a - hi.astype(F32)
    mid = r1.astype(BF16)
    lo = (r1 - mid.astype(F32)).astype(BF16)
    return _dot_tn(p, hi) + _dot_tn(p, mid) + _dot_tn(p, lo)


def _power_table(lam_ref, pw_ref, n):
    lr, li = lam_ref[0:1, :], lam_ref[1:2, :]
    pw_ref[0:1, 0:N_STATE] = lr
    pw_ref[0:1, N_STATE:] = li

    def step(i, carry):
        pr, pi = carry
        pr, pi = pr * lr - pi * li, pr * li + pi * lr
        pw_ref[pl.ds(i, 1), 0:N_STATE] = pr
        pw_ref[pl.ds(i, 1), N_STATE:] = pi
        return pr, pi

    lax.fori_loop(1, n, step, (lr, li))


def _col_groups():
    return [(pl.ds(c, SCAN_CG), pl.ds(N_STATE + c, SCAN_CG)) for c in range(0, N_STATE, SCAN_CG)]


def _run_scan(buf, lam_ref, t, reverse):
    nblk = t // 8
    for re, im in _col_groups():
        lr = jnp.broadcast_to(lam_ref[0:1, re], (8, SCAN_CG))
        li = jnp.broadcast_to(lam_ref[1:2, re], (8, SCAN_CG))
        if reverse:
            li = -li
        first = pl.ds((nblk - 1) * 8 if reverse else 0, 8)

        def step(k, carry, re=re, im=im, lr=lr, li=li):
            pr, pi = carry
            i = (nblk - 2 - k) if reverse else (k + 1)
            r = pl.ds(pl.multiple_of(i * 8, 8), 8)
            xr = buf[r, re] + lr * pr - li * pi
            xi = buf[r, im] + lr * pi + li * pr
            buf[r, re] = xr
            buf[r, im] = xi
            return xr, xi

        lax.fori_loop(0, nblk - 1, step, (buf[first, re], buf[first, im]))


def _run_carries(buf, pw_ref, carry_ref, s_ref, t, reverse):
    nblk = t // 8
    run = t // SUBCHUNKS
    edge = buf[pl.ds(0 if reverse else (nblk - 1) * 8, 8), :]
    pr, pi = pw_ref[run - 1:run, 0:N_STATE], pw_ref[run - 1:run, N_STATE:]
    if reverse:
        pi = -pi
    sr, si = carry_ref[0:1, 0:N_STATE], carry_ref[0:1, N_STATE:]
    for s in (range(SUBCHUNKS - 1, -1, -1) if reverse else range(SUBCHUNKS)):
        s_ref[s:s + 1, 0:N_STATE] = sr
        s_ref[s:s + 1, N_STATE:] = si
        er, ei = edge[s:s + 1, 0:N_STATE], edge[s:s + 1, N_STATE:]
        sr, si = er + pr * sr - pi * si, ei + pr * si + pi * sr
    carry_ref[:, 0:N_STATE] = jnp.broadcast_to(sr, (8, N_STATE))
    carry_ref[:, N_STATE:] = jnp.broadcast_to(si, (8, N_STATE))


def _run_fix(buf, pw_ref, s_ref, t, reverse):
    nblk = t // 8
    for re, im in _col_groups():
        sr, si = s_ref[:, re], s_ref[:, im]

        def step(i, carry, re=re, im=im, sr=sr, si=si):
            r = pl.ds(pl.multiple_of(i * 8, 8), 8)
            row = pl.ds((nblk - 1 - i) if reverse else i, 1)
            pr, pi = pw_ref[row, re], pw_ref[row, im]
            if reverse:
                pi = -pi
            buf[r, re] += pr * sr - pi * si
            buf[r, im] += pr * si + pi * sr
            return carry

        lax.fori_loop(0, nblk, step, 0)


def _ssm_fwd(u, bblk, cblk, lam, d_row, w_glu, b_glu, w_o_ssm, xch):
    n = u.shape[0]
    t = min(SCAN_T, n)
    kb = 512
    perm = _perm_matrix(t)

    def body(u_ref, p_ref, bblk_ref, cblk_ref, lam_ref, d_ref, wg_ref, bg_ref, wo_ref, y_ref, ys_ref, st_ref,
             buf, pw_ref, carry_ref, s_ref):
        @pl.when(pl.program_id(0) == 0)
        def _():
            carry_ref[...] = jnp.zeros_like(carry_ref)
            _power_table(lam_ref, pw_ref, t // SUBCHUNKS)

        st_ref[0] = carry_ref[...]
        u_t = u_ref[...]
        p = p_ref[...]
        ub = _dot(p, u_t.astype(BF16)).astype(BF16)
        for c in range(0, 2 * N_STATE, kb):
            buf[:, c:c + kb] = _dot(ub, bblk_ref[:, c:c + kb])
        _run_scan(buf, lam_ref, t, False)
        _run_carries(buf, pw_ref, carry_ref, s_ref, t, False)
        _run_fix(buf, pw_ref, s_ref, t, False)
        yp = jnp.zeros((t, SSM_WIDTH), F32)
        for c in range(0, 2 * N_STATE, kb):
            yp += _dot(buf[:, c:c + kb].astype(BF16), cblk_ref[c:c + kb, :])
        y = d_ref[...] * u_t + _unpermute(p, yp)
        y_ref[...] = y
        z, _ = _gelu(y)
        s = _sigmoid(_dot(z.astype(BF16), wg_ref[...]) + bg_ref[...])
        zgb = (z * s).astype(BF16)
        for j in range(N_DEV):
            ys_ref[:, j * OUT_SHARD:(j + 1) * OUT_SHARD] = _dot(zgb, wo_ref[j])

    consts = [perm, bblk, cblk, lam, d_row, w_glu, b_glu, w_o_ssm]
    return _call(
        body, "ssm_fwd", (n // t,), [u] + consts, [_rows(u, t)] + [_const(a) for a in consts],
        [_sds((n, SSM_WIDTH), F32), _sds((n, D_MODEL), F32), _sds((n // t, 8, 2 * N_STATE), F32)],
        [pl.BlockSpec((t, SSM_WIDTH), lambda i: (i, 0)), pl.BlockSpec((t, D_MODEL), lambda i: (i, 0)),
         pl.BlockSpec((1, 8, 2 * N_STATE), lambda i: (i, 0, 0))],
        scratch=[pltpu.VMEM((t, 2 * N_STATE), F32), pltpu.VMEM((t // SUBCHUNKS, 2 * N_STATE), F32),
                 pltpu.VMEM((8, 2 * N_STATE), F32), pltpu.VMEM((8, 2 * N_STATE), F32)],
        xch=xch)


def _head_norm_rope(slab, gain, cos_t, sin_t):
    xn, inv = _rms(slab, gain, QK_HEAD)
    lo, hi = xn[:, 0:128], xn[:, 128:256]
    return jnp.concatenate([lo, hi * cos_t + _rope_rot(hi) * sin_t], axis=-1), inv


def _head_norm_rope_bwd(g, slab, gain, inv, cos_t, sin_t):
    g_lo, g_hi = g[:, 0:128], g[:, 128:256]
    g_n = jnp.concatenate([g_lo, g_hi * cos_t + _rope_rot_t(g_hi * sin_t)], axis=-1)
    return _rms_bwd(g_n, slab, gain, inv, QK_HEAD)


def _qkv_prep(ql, kvl, q_a_norm, kv_a_norm, wq, wkv, gq, gk, cos_t, sin_t):
    n = ql.shape[0]
    tm = ROW_T

    def body(ql_ref, kvl_ref, cos_ref, sin_ref, qa_ref, ka_ref, wq_ref, wkv_ref, gq_ref, gk_ref,
             q_ref, k_ref, v_ref, kt_ref, vt_ref):
        cos_t, sin_t = cos_ref[...], sin_ref[...]
        qa, _ = _rms(ql_ref[...], qa_ref[...], Q_LORA)
        qab = qa.astype(BF16)
        kvl_t = kvl_ref[...]
        ca, _ = _rms(kvl_t[:, 0:KV_LORA], ka_ref[...], KV_LORA)
        cab = ca.astype(BF16)
        kpe = kvl_t[:, KV_LORA:KV_LAT_PAD]
        q_pre = _dot(qab, wq_ref[...])
        kv_pre = _dot(cab, wkv_ref[...])
        for h in range(N_HEADS):
            qh, _ = _head_norm_rope(q_pre[:, h * QK_PAD:(h + 1) * QK_PAD], gq_ref[...], cos_t, sin_t)
            q_ref[h] = (qh * ATT_SCALE).astype(BF16)
            kv_h = kv_pre[:, h * QK_PAD:(h + 1) * QK_PAD]
            kh, _ = _head_norm_rope(jnp.concatenate([kv_h[:, 0:QK_NOPE], kpe], axis=-1), gk_ref[...], cos_t, sin_t)
            k_ref[h] = kh.astype(BF16)
            kt_ref[h] = kh.T.astype(BF16)
            vh = kv_h[:, QK_NOPE:]
            v_ref[h] = vh.astype(BF16)
            vt_ref[h] = vh.T.astype(BF16)

    row_ins, consts = [ql, kvl, cos_t, sin_t], [q_a_norm, kv_a_norm, wq, wkv, gq, gk]
    outs = [_sds((N_HEADS, n, QK_PAD), BF16), _sds((N_HEADS, n, QK_PAD), BF16), _sds((N_HEADS, n, V_HEAD), BF16),
            _sds((N_HEADS, QK_PAD, n), BF16), _sds((N_HEADS, V_HEAD, n), BF16)]
    out_specs = [_rows(o, tm) for o in outs[:3]] + [
        pl.BlockSpec((N_HEADS, QK_PAD, tm), lambda i: (0, 0, i)), pl.BlockSpec((N_HEADS, V_HEAD, tm), lambda i: (0, 0, i))]
    return _call(body, "qkv_prep", (n // tm,), row_ins + consts,
                 [_rows(a, tm) for a in row_ins] + [_const(a) for a in consts], outs, out_specs)


def _causal_mask_t(st, t):
    key = lax.broadcasted_iota(jnp.int32, (t, t), 0)
    qry = lax.broadcasted_iota(jnp.int32, (t, t), 1)
    return jnp.where(key <= qry, st, -jnp.inf)


def _attn_fwd(q, k, vt, xch):
    n = q.shape[1]
    t = min(ATT_T, n)

    def body(q_ref, k_ref, vt_ref, o_ref, lse_ref):
        i = pl.program_id(1)
        qt = q_ref[0]

        def kv_tile(j, carry, diag):
            m, l, acc = carry
            ts = t // ATT_SUB
            sts = []
            for a in range(ATT_SUB):
                r0 = pl.multiple_of(j * t + a * ts, ts)
                st = _dot_nt(k_ref[0, pl.ds(r0, ts), :], qt)
                if diag:
                    key = lax.broadcasted_iota(jnp.int32, (ts, t), 0) + a * ts
                    qry = lax.broadcasted_iota(jnp.int32, (ts, t), 1)
                    st = jnp.where(key <= qry, st, -jnp.inf)
                sts.append(st)
            for a, st in enumerate(sts):
                r0 = pl.multiple_of(j * t + a * ts, ts)
                m_new = jnp.maximum(m, jnp.max(st, 0, keepdims=True))
                alpha = jnp.exp(m - m_new)
                pt = jnp.exp(st - m_new)
                l = alpha * l + jnp.sum(pt, 0, keepdims=True)
                acc = alpha * acc + _dot(vt_ref[0, :, pl.ds(r0, ts)], pt.astype(BF16))
                m = m_new
            return m, l, acc

        init = (jnp.full((1, t), -jnp.inf, F32), jnp.zeros((1, t), F32), jnp.zeros((V_HEAD, t), F32))
        carry = lax.fori_loop(0, i, functools.partial(kv_tile, diag=False), init)
        m, l, acc = kv_tile(i, carry, True)
        o_ref[...] = (acc / l).T
        lse_ref[0] = m + jnp.log(l)

    return _call(
        body, "attn_fwd", (N_HEADS, n // t), [q, k, vt],
        [pl.BlockSpec((1, t, QK_PAD), lambda h, i: (h, i, 0)), pl.BlockSpec((1, n, QK_PAD), lambda h, i: (h, 0, 0)),
         pl.BlockSpec((1, V_HEAD, n), lambda h, i: (h, 0, 0))],
        [_sds((n, N_HEADS * V_HEAD), F32), _sds((N_HEADS, 1, n), F32)],
        [pl.BlockSpec((t, V_HEAD), lambda h, i: (i, h)), pl.BlockSpec((1, 1, t), lambda h, i: (h, 0, i))],
        xch=xch)


def _merge(attn, gs, gm, y_ssm, x, w_o_mla, w_out):
    n = x.shape[0]

    def body(at_ref, gs_ref, gm_ref, ys_ref, x_ref, wo_ref, wout_ref, h_ref, mx_ref, ym_ref):
        y_mla = _dot(at_ref[...].astype(BF16), wo_ref[...])
        ym_ref[...] = y_mla
        mixed = (_sigmoid(gs_ref[...]) * ys_ref[...] + _sigmoid(gm_ref[...]) * y_mla).astype(BF16)
        mx_ref[...] = mixed
        h_ref[...] = x_ref[...] + _dot(mixed, wout_ref[...])

    outs = [((n, D_MODEL), F32), ((n, D_MODEL), BF16), ((n, D_MODEL), F32)]
    return _row_call(body, "merge", n, ROW_T, [attn, gs, gm, y_ssm, x], [w_o_mla, w_out], outs)


def _mlp_fwd_loss(h, target, norm_mlp, w_up, w_down):
    n = h.shape[0]

    def body(h_ref, t_ref, g_ref, wu_ref, wd_ref, hn_ref, do_ref, loss_ref):
        h_t = h_ref[...]
        hn, _ = _rms(h_t, g_ref[...], D_MODEL)
        hb = hn.astype(BF16)
        hn_ref[...] = hb
        out = h_t
        for j in range(N_DEV):
            a = jnp.maximum(_dot(hb, wu_ref[j]), 0.0)
            out += _dot((a * a).astype(BF16), wd_ref[j])
        err = out - t_ref[...]
        do_ref[...] = err * (1.0 / D_MODEL)
        _acc(loss_ref, jnp.broadcast_to(jnp.sum(err * err) * (0.5 / D_MODEL), loss_ref.shape))

    outs = [((n, D_MODEL), BF16), ((n, D_MODEL), F32)]
    return _row_call(body, "mlp_fwd_loss", n, ROW_T, [h, target], [norm_mlp, w_up, w_down], outs, [((8, 128), F32)])


def _mlp_bwd(dout, hn, h, norm_mlp, w_up, w_down):
    n = h.shape[0]

    def body(do_ref, hn_ref, h_ref, g_ref, wu_ref, wd_ref, hid_ref, da_ref, dh_ref, dg_ref):
        dout_t = do_ref[...]
        doutb = dout_t.astype(BF16)
        hb = hn_ref[...]
        dhn = jnp.zeros_like(dout_t)
        for j in range(N_DEV):
            cols = slice(j * FF_SHARD, (j + 1) * FF_SHARD)
            a = jnp.maximum(_dot(hb, wu_ref[j]), 0.0)
            hid_ref[:, cols] = (a * a).astype(BF16)
            da = (_dot_nt(doutb, wd_ref[j]) * (2.0 * a)).astype(BF16)
            da_ref[:, cols] = da
            dhn += _dot_nt(da, wu_ref[j])
        h_t = h_ref[...]
        inv = lax.rsqrt(jnp.sum(h_t * h_t, -1, keepdims=True) * (1.0 / D_MODEL) + EPS)
        dx, dg = _rms_bwd(dhn, h_t, g_ref[...], inv, D_MODEL)
        dh_ref[...] = dout_t + dx
        _acc(dg_ref, jnp.sum(dg, 0, keepdims=True))

    outs = [((n, D_FF), BF16), ((n, D_FF), BF16), ((n, D_MODEL), F32)]
    return _row_call(body, "mlp_bwd", n, ROW_T, [dout, hn, h], [norm_mlp, w_up, w_down], outs, [((1, D_MODEL), F32)])


def _merge_bwd(dh, gs, gm, y_ssm, y_mla, w_out, w_o_mla):
    n = dh.shape[0]

    def body(dh_ref, gs_ref, gm_ref, ys_ref, ym_ref, wout_ref, wo_ref, dgs_ref, dgm_ref, dys_ref, dym_ref, dat_ref):
        dmix = _dot_nt(dh_ref[...].astype(BF16), wout_ref[...])
        sgs, sgm = _sigmoid(gs_ref[...]), _sigmoid(gm_ref[...])
        dgs_ref[...] = (dmix * ys_ref[...] * sgs * (1.0 - sgs)).astype(BF16)
        dgm_ref[...] = (dmix * ym_ref[...] * sgm * (1.0 - sgm)).astype(BF16)
        dys_ref[...] = (dmix * sgs).astype(BF16)
        dym = (dmix * sgm).astype(BF16)
        dym_ref[...] = dym
        dat_ref[...] = _dot_nt(dym, wo_ref[...])

    outs = [((n, D_MODEL), BF16)] * 4 + [((n, D_MODEL), F32)]
    return _row_call(body, "merge_bwd", n, ROW_T, [dh, gs, gm, y_ssm, y_mla], [w_out, w_o_mla], outs)


def _attn_bwd(q, k, kt, v, out, lse, dout, xch):
    n = q.shape[1]
    t = min(ATT_T, n)
    nt = n // t

    def body(q_ref, k_ref, kt_ref, v_ref, o_ref, lse_ref, do_ref, dq_ref, dk_ref, dv_ref, delta_ref, dqt_ref):
        j = pl.program_id(1)

        @pl.when(j == 0)
        def _():
            dqt_ref[...] = jnp.zeros_like(dqt_ref)
            prod = do_ref[...] * o_ref[...]
            delta_ref[...] = lax.dot_general(jnp.ones((8, V_HEAD), F32), prod, (((1,), (1,)), ((), ())),
                                             precision=lax.Precision.HIGHEST, preferred_element_type=F32)

        k_t = k_ref[0]
        kt_t = kt_ref[0]
        v_t = v_ref[0]

        def q_tile(i, carry, diag):
            dk, dv = carry
            r0 = pl.multiple_of(i * t, t)
            rows = pl.ds(r0, t)
            qt = q_ref[0, rows, :]
            st = _dot_nt(k_t, qt)
            if diag:
                st = _causal_mask_t(st, t)
            pt = jnp.exp(st - lse_ref[0, :, rows])
            dob = do_ref[rows, :].astype(BF16)
            dv = dv + _dot(pt.astype(BF16), dob)
            dst = (pt * (_dot_nt(v_t, dob) - delta_ref[0:1, rows])).astype(BF16)
            dk = dk + _dot(dst, qt)
            dqt_ref[:, rows] += _dot(kt_t, dst)
            return dk, dv

        carry = q_tile(j, (jnp.zeros((t, QK_PAD), F32), jnp.zeros((t, V_HEAD), F32)), True)
        dk, dv = lax.fori_loop(j + 1, nt, functools.partial(q_tile, diag=False), carry)
        dk_ref[0] = dk
        dv_ref[0] = dv

        @pl.when(j == nt - 1)
        def _():
            for c in range(0, n, t):
                dq_ref[0, c:c + t, :] = dqt_ref[:, c:c + t].T

    return _call(
        body, "attn_bwd", (N_HEADS, nt), [q, k, kt, v, out, lse, dout],
        [pl.BlockSpec((1, n, QK_PAD), lambda h, j: (h, 0, 0)), pl.BlockSpec((1, t, QK_PAD), lambda h, j: (h, j, 0)),
         pl.BlockSpec((1, QK_PAD, t), lambda h, j: (h, 0, j)), pl.BlockSpec((1, t, V_HEAD), lambda h, j: (h, j, 0)),
         pl.BlockSpec((n, V_HEAD), lambda h, j: (0, h)), pl.BlockSpec((1, 1, n), lambda h, j: (h, 0, 0)),
         pl.BlockSpec((n, V_HEAD), lambda h, j: (0, h))],
        [_sds((N_HEADS, n, QK_PAD), F32), _sds((N_HEADS, n, QK_PAD), F32), _sds((N_HEADS, n, V_HEAD), F32)],
        [pl.BlockSpec((1, n, QK_PAD), lambda h, j: (h, 0, 0)), pl.BlockSpec((1, t, QK_PAD), lambda h, j: (h, j, 0)),
         pl.BlockSpec((1, t, V_HEAD), lambda h, j: (h, j, 0))],
        scratch=[pltpu.VMEM((8, n), F32), pltpu.VMEM((QK_PAD, n), F32)],
        xch=xch)


def _qkv_prep_bwd(ql, kvl, dq, dk, dv, q_a_norm, kv_a_norm, wq, wkv, gq, gk, cos_t, sin_t, xch):
    n = ql.shape[0]

    def body(ql_ref, kvl_ref, cos_ref, sin_ref, dq_ref, dk_ref, dv_ref, qa_ref, ka_ref, wq_ref, wkv_ref, gq_ref, gk_ref,
             dql_ref, dkvl_ref, qab_ref, dqp_ref, cab_ref, dkvp_ref, dqa_ref, dka_ref, dgq_ref, dgk_ref):
        cos_t, sin_t = cos_ref[...], sin_ref[...]
        ql_t = ql_ref[...]
        qa, inv_qa = _rms(ql_t, qa_ref[...], Q_LORA)
        qab = qa.astype(BF16)
        qab_ref[...] = qab
        kvl_t = kvl_ref[...]
        ckv = kvl_t[:, 0:KV_LORA]
        ca, inv_ca = _rms(ckv, ka_ref[...], KV_LORA)
        cab = ca.astype(BF16)
        cab_ref[...] = cab
        kpe = kvl_t[:, KV_LORA:KV_LAT_PAD]
        dgq = jnp.zeros((1, QK_PAD), F32)
        dgk = jnp.zeros((1, QK_PAD), F32)
        dkpe = jnp.zeros_like(kpe)
        q_pre = _dot(qab, wq_ref[...])
        kv_pre = _dot(cab, wkv_ref[...])
        for h in range(N_HEADS):
            head = slice(h * QK_PAD, (h + 1) * QK_PAD)
            q_slab = q_pre[:, head]
            inv = lax.rsqrt(jnp.sum(q_slab * q_slab, -1, keepdims=True) * (1.0 / QK_HEAD) + EPS)
            d_slab, dg = _head_norm_rope_bwd(dq_ref[h] * ATT_SCALE, q_slab, gq_ref[...], inv, cos_t, sin_t)
            dqp_ref[:, head] = d_slab.astype(BF16)
            dgq += jnp.sum(dg, 0, keepdims=True)
            k_slab = jnp.concatenate([kv_pre[:, h * QK_PAD:h * QK_PAD + QK_NOPE], kpe], axis=-1)
            inv = lax.rsqrt(jnp.sum(k_slab * k_slab, -1, keepdims=True) * (1.0 / QK_HEAD) + EPS)
            d_slab, dg = _head_norm_rope_bwd(dk_ref[h], k_slab, gk_ref[...], inv, cos_t, sin_t)
            dkvp_ref[:, head] = jnp.concatenate([d_slab[:, 0:QK_NOPE], dv_ref[h]], axis=-1).astype(BF16)
            dkpe += d_slab[:, QK_NOPE:QK_PAD]
            dgk += jnp.sum(dg, 0, keepdims=True)
        dqa = _dot_nt(dqp_ref[...], wq_ref[...])
        dx, dg = _rms_bwd(dqa, ql_t, qa_ref[...], inv_qa, Q_LORA)
        dql_ref[...] = dx.astype(BF16)
        _acc(dqa_ref, jnp.sum(dg, 0, keepdims=True))
        dca = _dot_nt(dkvp_ref[...], wkv_ref[...])
        dx, dg = _rms_bwd(dca, ckv, ka_ref[...], inv_ca, KV_LORA)
        dkvl_ref[:, 0:KV_LORA] = dx.astype(BF16)
        dkvl_ref[:, KV_LORA:KV_LAT_PAD] = dkpe.astype(BF16)
        _acc(dka_ref, jnp.sum(dg, 0, keepdims=True))
        _acc(dgq_ref, dgq)
        _acc(dgk_ref, dgk)

    row_outs = [((n, Q_LORA), BF16), ((n, KV_LAT_PAD), BF16), ((n, Q_LORA), BF16), ((n, N_HEADS * QK_PAD), BF16),
                ((n, KV_LORA), BF16), ((n, N_HEADS * (QK_NOPE + V_HEAD)), BF16)]
    acc_outs = [((1, Q_LORA), F32), ((1, KV_LORA), F32), ((1, QK_PAD), F32), ((1, QK_PAD), F32)]
    return _row_call(body, "qkv_prep_bwd", n, ROW_T, [ql, kvl, cos_t, sin_t, dq, dk, dv],
                     [q_a_norm, kv_a_norm, wq, wkv, gq, gk], row_outs, acc_outs, xch=xch)


def _glu_bwd(dy_ssm, y, w_glu, b_glu, w_o_ssm):
    n = y.shape[0]

    def body(dys_ref, y_ref, wg_ref, bg_ref, wo_ref, dy_ref, zg_ref, z_ref, dt_ref, db_ref):
        y_t = y_ref[...]
        z, th = _gelu(y_t)
        zb = z.astype(BF16)
        z_ref[...] = zb
        s = _sigmoid(_dot(zb, wg_ref[...]) + bg_ref[...])
        zg_ref[...] = (z * s).astype(BF16)
        dys = dys_ref[...]
        dzg = jnp.zeros_like(y_t)
        for j in range(N_DEV):
            dzg += _dot_nt(dys[:, j * OUT_SHARD:(j + 1) * OUT_SHARD], wo_ref[j])
        dt = dzg * z * s * (1.0 - s)
        dtb = dt.astype(BF16)
        dt_ref[...] = dtb
        dz = dzg * s + _dot_nt(dtb, wg_ref[...])
        dy_ref[...] = dz * _gelu_grad(y_t, th)
        _acc(db_ref, jnp.sum(dt, 0, keepdims=True))

    outs = [((n, SSM_WIDTH), F32)] + [((n, SSM_WIDTH), BF16)] * 3
    return _row_call(body, "glu_bwd", n, ROW_T, [dy_ssm, y], [w_glu, b_glu, w_o_ssm], outs, [((1, SSM_WIDTH), F32)])


def _ssm_bwd(u, dy, st, bblk, cblk, lam, d_row, xch):
    n = u.shape[0]
    t = min(SCAN_T, n)
    nc = n // t
    kb = 512
    perm = _perm_matrix(t)

    def body(u_ref, dy_ref, st_ref, p_ref, bblk_ref, cblk_ref, lam_ref, d_ref,
             du_ref, xs_ref, as_ref, up_ref, dyp_ref, dlam_ref, dd_ref,
             buf_x, buf_a, pw_ref, carry_ref, xcarry_ref, sx_ref, sa_ref):
        @pl.when(pl.program_id(0) == 0)
        def _():
            carry_ref[...] = jnp.zeros_like(carry_ref)
            _power_table(lam_ref, pw_ref, t // SUBCHUNKS)

        u_t = u_ref[...]
        dy_t = dy_ref[...]
        p = p_ref[...]
        ub = _dot(p, u_t.astype(BF16)).astype(BF16)
        dyb = _dot(p, dy_t.astype(BF16)).astype(BF16)
        up_ref[...] = ub
        dyp_ref[...] = dyb
        for c in range(0, 2 * N_STATE, kb):
            buf_x[:, c:c + kb] = _dot(ub, bblk_ref[:, c:c + kb])
        xcarry_ref[...] = st_ref[0]
        _run_scan(buf_x, lam_ref, t, False)
        _run_carries(buf_x, pw_ref, xcarry_ref, sx_ref, t, False)
        _run_fix(buf_x, pw_ref, sx_ref, t, False)
        for c in range(0, 2 * N_STATE, kb):
            buf_a[:, c:c + kb] = _dot_nt(dyb, cblk_ref[c:c + kb, :])
        _run_scan(buf_a, lam_ref, t, True)
        _run_carries(buf_a, pw_ref, carry_ref, sa_ref, t, True)
        _run_fix(buf_a, pw_ref, sa_ref, t, True)
        dup = jnp.zeros((t, SSM_WIDTH), F32)
        for c in range(0, 2 * N_STATE, kb):
            adb = buf_a[:, c:c + kb].astype(BF16)
            as_ref[:, c:c + kb] = adb
            xs_ref[:, c:c + kb] = buf_x[:, c:c + kb].astype(BF16)
            dup += _dot_nt(adb, bblk_ref[:, c:c + kb])
        du_ref[...] = (d_ref[...] * dy_t + _unpermute(p, dup)).astype(BF16)
        for c in range(0, N_STATE, kb):
            re, im = pl.ds(c, kb), pl.ds(N_STATE + c, kb)
            xr, xi = buf_x[pl.ds(0, t - 8), re], buf_x[pl.ds(0, t - 8), im]
            ar, ai = buf_a[pl.ds(8, t - 8), re], buf_a[pl.ds(8, t - 8), im]
            x0r, x0i = sx_ref[:, re], sx_ref[:, im]
            a0r, a0i = buf_a[0:8, re], buf_a[0:8, im]
            dlam_part_re = (jnp.sum(ar * xr + ai * xi, 0, keepdims=True)
                            + jnp.sum(a0r * x0r + a0i * x0i, 0, keepdims=True))
            dlam_part_im = (jnp.sum(ai * xr - ar * xi, 0, keepdims=True)
                            + jnp.sum(a0i * x0r - a0r * x0i, 0, keepdims=True))

            @pl.when(pl.program_id(0) == 0)
            def _(c=c):
                dlam_ref[0:1, c:c + kb] = jnp.zeros((1, kb), F32)
                dlam_ref[1:2, c:c + kb] = jnp.zeros((1, kb), F32)

            dlam_ref[0:1, c:c + kb] += dlam_part_re
            dlam_ref[1:2, c:c + kb] += dlam_part_im
        _acc(dd_ref, jnp.sum(dy_t * u_t, 0, keepdims=True))

    rev = lambda i: (nc - 1 - i, 0)
    consts = [perm, bblk, cblk, lam, d_row]
    return _call(
        body, "ssm_bwd", (nc,), [u, dy, st] + consts,
        [pl.BlockSpec((t, SSM_WIDTH), rev), pl.BlockSpec((t, SSM_WIDTH), rev),
         pl.BlockSpec((1, 8, 2 * N_STATE), lambda i: (nc - 1 - i, 0, 0))] + [_const(a) for a in consts],
        [_sds((n, SSM_WIDTH), BF16), _sds((n, 2 * N_STATE), BF16), _sds((n, 2 * N_STATE), BF16),
         _sds((n, SSM_WIDTH), BF16), _sds((n, SSM_WIDTH), BF16), _sds((2, N_STATE), F32), _sds((1, SSM_WIDTH), F32)],
        [pl.BlockSpec((t, SSM_WIDTH), rev), pl.BlockSpec((t, 2 * N_STATE), rev), pl.BlockSpec((t, 2 * N_STATE), rev),
         pl.BlockSpec((t, SSM_WIDTH), rev), pl.BlockSpec((t, SSM_WIDTH), rev),
         pl.BlockSpec((2, N_STATE), lambda i: (0, 0)), pl.BlockSpec((1, SSM_WIDTH), lambda i: (0, 0))],
        scratch=[pltpu.VMEM((t, 2 * N_STATE), F32)] * 2 + [pltpu.VMEM((t // SUBCHUNKS, 2 * N_STATE), F32)]
        + [pltpu.VMEM((8, 2 * N_STATE), F32)] * 4,
        xch=xch)


def _in_proj_bwd(pieces, dh, x, norm_mix, w_in_pad):
    n = x.shape[0]

    def body(du_ref, dql_ref, dkvl_ref, dgs_ref, dgm_ref, dh_ref, x_ref, g_ref, w_ref, dx_ref, dp_ref, dg_ref):
        dxn = jnp.zeros((dh_ref.shape[0], D_MODEL), F32)
        for ref, (a, b) in zip((du_ref, dql_ref, dkvl_ref, dgs_ref, dgm_ref), IN_SEGS):
            piece = ref[...]
            dp_ref[:, a:b] = piece
            dxn += _dot_nt(piece, w_ref[:, a:b])
        x_t = x_ref[...]
        inv = lax.rsqrt(jnp.sum(x_t * x_t, -1, keepdims=True) * (1.0 / D_MODEL) + EPS)
        dx, dg = _rms_bwd(dxn, x_t, g_ref[...], inv, D_MODEL)
        dx_ref[...] = dh_ref[...] + dx
        _acc(dg_ref, jnp.sum(dg, 0, keepdims=True))

    outs = [((n, D_MODEL), F32), ((n, D_IN_PAD), BF16)]
    return _row_call(body, "in_proj_bwd", n, ROW_T, list(pieces) + [dh, x], [norm_mix, w_in_pad], outs,
                     [((1, D_MODEL), F32)])


def _swap_minor(a):
    g, r, c = a.shape[1:]
    return jnp.transpose(a[0], (0, 2, 1)).reshape(g * c, r)


def _pad_in(w):
    return jnp.concatenate([w[:, :KV_END], jnp.zeros((w.shape[0], D_IN_PAD - D_IN), w.dtype), w[:, KV_END:]], axis=1)


def _unpad_in(w):
    return jnp.concatenate([w[:, :KV_END], w[:, KV_END + D_IN_PAD - D_IN:]], axis=1)


def _pad_gain(g):
    return jnp.pad(g, ((0, 0), (0, QK_PAD - QK_HEAD)))


def _place():
    x, y, c = lax.axis_index("x"), lax.axis_index("y"), lax.axis_index("c")
    chips = [(x, y), (1 - x, y), (x, 1 - y), (1 - x, 1 - y)]
    return x, y, c, chips


def _all_gather(block, name):
    rows, lanes = block.shape

    def body(x_ref, out_ref, send_sems, recv_sems, local_sem):
        x, y, c, chips = _place()
        me, sibling = (x, y, c), (x, y, 1 - c)

        def slot(px, py, pc):
            return out_ref.at[4 * px + 2 * py + pc]

        def copy(k, blk, to, src=None):
            return pltpu.make_async_remote_copy(
                src_ref=slot(*blk) if src is None else src, dst_ref=slot(*blk),
                send_sem=send_sems.at[k], recv_sem=recv_sems.at[k], device_id=to, device_id_type=MESH)

        mine = pltpu.make_async_copy(x_ref, slot(*me), local_sem)
        mine.start()
        first = [copy(0, me, sibling, src=x_ref)]
        first += [copy(1 + j, me, (*chip, c), src=x_ref) for j, chip in enumerate(chips[1:])]
        for cp in first:
            cp.start()
        passed = [copy(4 + j, (*chip, c), sibling) for j, chip in enumerate(chips[1:])]
        for j, chip in enumerate(chips[1:]):
            copy(1 + j, (*chip, c), me).wait_recv()
            passed[j].start()
        copy(0, sibling, me).wait_recv()
        for j, chip in enumerate(chips[1:]):
            copy(4 + j, (*chip, 1 - c), me).wait_recv()
        for cp in first + passed:
            cp.wait_send()
        mine.wait()

    return pl.pallas_call(
        body,
        name=name,
        in_specs=[ANY],
        out_specs=ANY,
        out_shape=_sds((N_DEV, rows, lanes), block.dtype),
        scratch_shapes=[pltpu.SemaphoreType.DMA((7,)), pltpu.SemaphoreType.DMA((7,)), pltpu.SemaphoreType.DMA],
    )(block)


RS_CHUNKS = 4


def _reduce_scatter(parts, name):
    _, rows, lanes = parts.shape
    ch = rows // RS_CHUNKS

    def body(p_ref, out_ref, land_a, send_b, land_b, va, vb, v16, w16, sa, ra, sb, rb):
        x, y, c, chips = _place()
        sibling = (x, y, 1 - c)

        def blk(chip, core):
            return p_ref.at[4 * chip[0] + 2 * chip[1] + core]

        to_sib = [pltpu.make_async_remote_copy(
            src_ref=blk(chips[k], 1 - c), dst_ref=land_a.at[k], send_sem=sa.at[k], recv_sem=ra.at[k],
            device_id=sibling, device_id_type=MESH) for k in range(4)]
        for cp in to_sib:
            cp.start()
        to_chip = [pltpu.make_async_remote_copy(
            src_ref=send_b.at[j], dst_ref=land_b.at[j], send_sem=sb.at[j], recv_sem=rb.at[j],
            device_id=(*chips[1 + j], c), device_id_type=MESH) for j in range(3)]

        for k in (1, 2, 3, 0):
            to_sib[k].wait_recv()

            def chip_sum(i, carry, k=k):
                r = pl.ds(pl.multiple_of(i * ch, 16), ch)
                pltpu.sync_copy(blk(chips[k], c).at[r], va)
                pltpu.sync_copy(land_a.at[k, r], vb)
                if k == 0:
                    va[...] = va[...] + vb[...]
                    pltpu.sync_copy(va, out_ref.at[r])
                else:
                    v16[...] = (va[...] + vb[...]).astype(BF16)
                    pltpu.sync_copy(v16, send_b.at[k - 1, r])
                return carry

            lax.fori_loop(0, RS_CHUNKS, chip_sum, 0)
            if k != 0:
                to_chip[k - 1].start()

        for cp in to_chip:
            cp.wait_recv()

        def final_sum(i, carry):
            r = pl.ds(pl.multiple_of(i * ch, 16), ch)
            pltpu.sync_copy(out_ref.at[r], va)
            acc = va[...]
            for j in range(3):
                pltpu.sync_copy(land_b.at[j, r], w16)
                acc = acc + w16[...].astype(F32)
            va[...] = acc
            pltpu.sync_copy(va, out_ref.at[r])
            return carry

        lax.fori_loop(0, RS_CHUNKS, final_sum, 0)
        for cp in to_sib + to_chip:
            cp.wait_send()

    outs = pl.pallas_call(
        body,
        name=name,
        in_specs=[ANY],
        out_specs=[ANY] * 4,
        out_shape=[_sds((rows, lanes), F32), _sds((4, rows, lanes), F32), _sds((3, rows, lanes), BF16),
                   _sds((3, rows, lanes), BF16)],
        scratch_shapes=[pltpu.VMEM((ch, lanes), F32), pltpu.VMEM((ch, lanes), F32), pltpu.VMEM((ch, lanes), BF16),
                        pltpu.VMEM((ch, lanes), BF16)]
        + [pltpu.SemaphoreType.DMA((4,))] * 2 + [pltpu.SemaphoreType.DMA((3,))] * 2,
    )(parts)
    return outs[0]


def _adamw_math(w, g, m, v):
    m = ADAM_B1 * m + (1.0 - ADAM_B1) * g
    v = ADAM_B2 * v + (1.0 - ADAM_B2) * (g * g)
    m_hat = m / (1.0 - ADAM_B1 ** ADAM_STEP)
    v_hat = v / (1.0 - ADAM_B2 ** ADAM_STEP)
    delta = -ADAM_LR * (m_hat / (jnp.sqrt(v_hat) + ADAM_EPS) + ADAM_WD * w)
    return delta, m, v


def _row_tile(r):
    return max(t for t in range(8, min(r, 256) + 1, 8) if r % t == 0)


def _adamw(w, g, m, v, name):
    r, n = w.shape

    def body(w_ref, g_ref, m_ref, v_ref, d_ref, nm_ref, nv_ref):
        d_ref[...], nm_ref[...], nv_ref[...] = _adamw_math(w_ref[...], g_ref[...], m_ref[...], v_ref[...])

    return _row_call(body, name, r, _row_tile(r), [w, g, m, v], [], [((r, n), F32)] * 3)


def _adamw_sum(landed, w, m, v, name):
    r, n = w.shape

    def body(l_ref, w_ref, m_ref, v_ref, g_ref, d_ref, nm_ref, nv_ref):
        g = l_ref[0].astype(F32)
        for dev in range(1, N_DEV):
            g = g + l_ref[dev].astype(F32)
        g_ref[...] = g
        d_ref[...], nm_ref[...], nv_ref[...] = _adamw_math(w_ref[...], g, m_ref[...], v_ref[...])

    tm = max(t for t in range(16, min(r, 256) + 1, 16) if r % t == 0)
    return _row_call(body, name, r, tm, [landed, w, m, v], [], [((r, n), F32)] * 4)


def _adamw_small(gathered, w, m, v):
    def body(ga_ref, w_ref, m_ref, v_ref, g_ref, d_ref, nm_ref, nv_ref):
        g = ga_ref[0]
        for dev in range(1, N_DEV):
            g = g + ga_ref[dev]
        g_ref[...] = g
        d_ref[...], nm_ref[...], nv_ref[...] = _adamw_math(w_ref[...], g, m_ref[...], v_ref[...])

    return pl.pallas_call(body, name="adamw_small", out_shape=[_sds(w.shape, F32)] * 4, compiler_params=_params())(
        gathered, w, m, v)


SMALL = ("norm_mix", "q_a_norm", "kv_a_norm", "q_norm", "k_norm", "ssm_a_re", "ssm_a_im", "ssm_log_dt", "ssm_b_re",
         "ssm_b_im", "ssm_c_re", "ssm_c_im", "ssm_d", "b_glu", "norm_mlp")
WEIGHT_ORDER = ("norm_mix", "w_in", "q_a_norm", "kv_a_norm", "w_q_b", "w_kv_b", "q_norm", "k_norm", "w_o_mla",
                "ssm_a_re", "ssm_a_im", "ssm_log_dt", "ssm_b_re", "ssm_b_im", "ssm_c_re", "ssm_c_im", "ssm_d", "w_glu",
                "b_glu", "w_o_ssm", "w_out", "norm_mlp", "w_up", "w_down")
IN_SHARD = D_IN // N_DEV
Q_SHARD = QK_HEAD


def _pack_small(vals):
    parts = []
    for n in SMALL:
        flat = vals[n].reshape(-1)
        size = -(-flat.shape[0] // (8 * LANES)) * 8 * LANES
        parts.append(jnp.pad(flat, (0, size - flat.shape[0])).reshape(-1, LANES))
    return jnp.concatenate(parts, axis=0)


def _unpack_small(packed, like):
    out, off = {}, 0
    for n in SMALL:
        size = like[n].size
        rows = -(-size // (8 * LANES)) * 8
        out[n] = packed[off:off + rows].reshape(-1)[:size].reshape(like[n].shape)
        off += rows
    return out


def _step(x, pos_col, target, w, small):
    bf = {n: a.astype(BF16) for n, a in w.items()}
    gq, gk = _pad_gain(small["q_norm"]), _pad_gain(small["k_norm"])
    a_re = small["ssm_a_re"].reshape(1, N_STATE)
    a_im = small["ssm_a_im"].reshape(1, N_STATE)
    log_dt = jnp.repeat(small["ssm_log_dt"].reshape(SSM_GROUPS), SSM_STATE).reshape(1, N_STATE)
    bt_re, bt_im = _swap_minor(small["ssm_b_re"]), _swap_minor(small["ssm_b_im"])
    c2_re, c2_im = _swap_minor(small["ssm_c_re"]), _swap_minor(small["ssm_c_im"])
    d_row = small["ssm_d"].reshape(1, SSM_WIDTH)

    w_in_all = _all_gather(bf["w_in"], "gather_w_in")
    w_in_pad = _pad_in(jnp.transpose(w_in_all, (1, 0, 2)).reshape(D_MODEL, D_IN))
    cos_t, sin_t = _rope_tables(pos_col)
    lam, bblk, cblk = _ssm_prep(a_re, a_im, log_dt, bt_re, bt_im, c2_re, c2_im)
    wq_mine = jnp.pad(bf["w_q_b"], ((0, 0), (0, QK_PAD - QK_HEAD)))
    xn, u, ql, kvl, gs, gm, w_glu, w_o_ssm = _in_proj(
        x, small["norm_mix"], w_in_pad, xch=[(bf["w_glu"], False), (bf["w_o_ssm"], False)])
    w_glu = w_glu.reshape(SSM_WIDTH, SSM_WIDTH)
    y, y_ssm, st, wq, wkv, w_o_mla, w_out = _ssm_fwd(
        u, bblk, cblk, lam, d_row, w_glu, small["b_glu"], w_o_ssm,
        xch=[(wq_mine, False), (bf["w_kv_b"], False), (bf["w_o_mla"], False), (bf["w_out"], False)])
    w_o_mla, w_out = w_o_mla.reshape(D_MODEL, D_MODEL), w_out.reshape(D_MODEL, D_MODEL)
    wq = jnp.transpose(wq, (1, 0, 2)).reshape(Q_LORA, N_HEADS * QK_PAD)
    wkv = jnp.transpose(wkv, (1, 0, 2)).reshape(KV_LORA, N_HEADS * QK_PAD)
    q, k, v, kt, vt = _qkv_prep(ql, kvl, small["q_a_norm"], small["kv_a_norm"], wq, wkv, gq, gk, cos_t, sin_t)
    attn, lse, w_up, w_down = _attn_fwd(q, k, vt, xch=[(bf["w_up"], False), (bf["w_down"], False)])
    h, mixed, y_mla = _merge(attn, gs, gm, y_ssm, x, w_o_mla, w_out)
    hn, dout, loss = _mlp_fwd_loss(h, target, small["norm_mlp"], w_up, w_down)

    hid, da, dh, d_norm_mlp = _mlp_bwd(dout, hn, h, small["norm_mlp"], w_up, w_down)
    p_w_down = _matmul_tn_shards(hid, dout, "dw_down", False)
    p_w_up = _matmul_tn_shards(hn, da, "dw_up", True)
    dgs, dgm, dy_ssm, dy_mla, dattn = _merge_bwd(dh, gs, gm, y_ssm, y_mla, w_out, w_o_mla)
    p_w_out = _matmul_tn_shards(mixed, dh, "dw_out", False)
    p_w_o_mla = _matmul_tn_shards(attn, dy_mla, "dw_o_mla", False)
    dq, dk, dv, l_w_up, l_w_down, l_w_out, l_w_o_mla = _attn_bwd(
        q, k, kt, v, attn, lse, dattn, xch=[(p_w_up, True), (p_w_down, True), (p_w_out, True), (p_w_o_mla, True)])
    dql, dkvl, qa, dq_pre, ca, dkv_pre, d_q_a_norm, d_kv_a_norm, d_gq, d_gk = _qkv_prep_bwd(
        ql, kvl, dq, dk, dv, small["q_a_norm"], small["kv_a_norm"], wq, wkv, gq, gk, cos_t, sin_t, xch=[])
    p_wq = _matmul_tn_shards(qa, dq_pre, "dw_q_b", True)
    p_wkv = _matmul_tn_shards(ca, dkv_pre, "dw_kv_b", True)
    dy, zg, z, dt, d_b_glu = _glu_bwd(dy_ssm, y, w_glu, small["b_glu"], w_o_ssm)
    p_w_o_ssm = _matmul_tn_shards(zg, dy_ssm, "dw_o_ssm", True)
    p_w_glu = _matmul_tn_shards(z, dt, "dw_glu", False)
    du, xs, ads, u_scan, dy_scan, dlam, d_d, l_wq, l_wkv, l_w_glu, l_w_o_ssm = _ssm_bwd(
        u, dy, st, bblk, cblk, lam, d_row, xch=[(p_wq, True), (p_wkv, True), (p_w_glu, True), (p_w_o_ssm, True)])
    d_bblk = _matmul_tn(u_scan, ads, "d_bblk")
    d_cblk_t = _matmul_tn(dy_scan, xs, "d_cblk")
    d_a_re, d_a_im, d_log_dt, d_bt_re, d_bt_im, d_c_re, d_c_im = _ssm_prep_bwd(
        a_re, a_im, log_dt, bt_re, bt_im, dlam, d_bblk, d_cblk_t)
    dx, dproj, d_norm_mix = _in_proj_bwd((du, dql, dkvl, dgs, dgm), dh, x, small["norm_mix"], w_in_pad)
    g_w_in = _unpad_in(_matmul_tn(xn, dproj, "dw_in"))
    parts = jnp.transpose(g_w_in.reshape(D_MODEL, N_DEV, IN_SHARD), (1, 0, 2))
    g_w_in_mine = _reduce_scatter(parts, "reduce_w_in")

    tr = lambda mat: jnp.transpose(mat.reshape(SSM_GROUPS, SSM_GROUP_CH, SSM_STATE), (0, 2, 1))
    g_small = {
        "norm_mix": d_norm_mix, "q_a_norm": d_q_a_norm, "kv_a_norm": d_kv_a_norm,
        "q_norm": d_gq[:, :QK_HEAD], "k_norm": d_gk[:, :QK_HEAD],
        "ssm_a_re": d_a_re, "ssm_a_im": d_a_im, "ssm_log_dt": d_log_dt,
        "ssm_b_re": tr(d_bt_re), "ssm_b_im": tr(d_bt_im), "ssm_c_re": d_c_re, "ssm_c_im": d_c_im,
        "ssm_d": d_d, "b_glu": d_b_glu, "norm_mlp": d_norm_mlp,
    }
    g_small_all = _all_gather(_pack_small(g_small), "gather_small_grads")
    landed = {"w_q_b": l_wq[:, :, :QK_HEAD], "w_kv_b": l_wkv, "w_o_mla": l_w_o_mla, "w_glu": l_w_glu,
              "w_o_ssm": l_w_o_ssm, "w_out": l_w_out, "w_up": l_w_up, "w_down": l_w_down}
    return loss, dx, landed, g_w_in_mine, g_small_all


def kernel(x, positions, norm_mix, w_in, q_a_norm, kv_a_norm, w_q_b, w_kv_b, q_norm, k_norm, w_o_mla, ssm_a_re, ssm_a_im, ssm_log_dt, ssm_b_re, ssm_b_im, ssm_c_re, ssm_c_im, ssm_d, w_glu, b_glu, w_o_ssm, w_out, norm_mlp, w_up, w_down, loss_target, m_norm_mix, m_w_in, m_q_a_norm, m_kv_a_norm, m_w_q_b, m_w_kv_b, m_q_norm, m_k_norm, m_w_o_mla, m_ssm_a_re, m_ssm_a_im, m_ssm_log_dt, m_ssm_b_re, m_ssm_b_im, m_ssm_c_re, m_ssm_c_im, m_ssm_d, m_w_glu, m_b_glu, m_w_o_ssm, m_w_out, m_norm_mlp, m_w_up, m_w_down, v_norm_mix, v_w_in, v_q_a_norm, v_kv_a_norm, v_w_q_b, v_w_kv_b, v_q_norm, v_k_norm, v_w_o_mla, v_ssm_a_re, v_ssm_a_im, v_ssm_log_dt, v_ssm_b_re, v_ssm_b_im, v_ssm_c_re, v_ssm_c_im, v_ssm_d, v_w_glu, v_b_glu, v_w_o_ssm, v_w_out, v_norm_mlp, v_w_up, v_w_down):
    given = dict(locals())
    w = {n: given[n] for n in WEIGHT_ORDER}
    m = {n: given["m_" + n] for n in WEIGHT_ORDER}
    v = {n: given["v_" + n] for n in WEIGHT_ORDER}
    big = [n for n in WEIGHT_ORDER if n not in SMALL]
    small = {n: w[n] for n in SMALL}

    loss, dx, landed, g_w_in, g_small_all = _step(
        x[0], positions.reshape(-1, 1), loss_target[0], {n: w[n][0] for n in big}, small)

    grads, deltas, new_m, new_v = {}, {}, {}, {}
    for n in big:
        if n == "w_in":
            g = g_w_in
            d, nm, nv = _adamw(w[n][0], g, m[n][0], v[n][0], "adamw_" + n)
        else:
            g, d, nm, nv = _adamw_sum(landed[n], w[n][0], m[n][0], v[n][0], "adamw_" + n)
        grads[n], deltas[n], new_m[n], new_v[n] = g[None], d[None], nm[None], nv[None]

    packed = _adamw_small(g_small_all, _pack_small(small), _pack_small({n: m[n] for n in SMALL}),
                          _pack_small({n: v[n] for n in SMALL}))
    for dst, src in zip((grads, deltas, new_m, new_v), packed):
        dst.update(_unpack_small(src, small))

    total = lax.psum(loss[0, 0], ("x", "y", "c"))
    return (total, dx[None], *[grads[n] for n in WEIGHT_ORDER], *[deltas[n] for n in WEIGHT_ORDER],
            *[new_m[n] for n in WEIGHT_ORDER], *[new_v[n] for n in WEIGHT_ORDER])
```

```python
import functools
import math

import numpy as np
import jax
import jax.numpy as jnp
from jax import lax
from jax.experimental import pallas as pl
from jax.experimental.pallas import tpu as pltpu

F32 = jnp.float32
BF16 = jnp.bfloat16

D_MODEL = 1024
SSM_GROUPS = 32
SSM_GROUP_CH = 16
SSM_WIDTH = 512
SSM_STATE = 64
N_STATE = SSM_GROUPS * SSM_STATE
N_HEADS = 8
QK_NOPE = 128
QK_ROPE = 64
QK_HEAD = 192
QK_PAD = 256
V_HEAD = 128
Q_LORA = 384
KV_LORA = 256
KV_LAT_PAD = 384
ROPE_THETA = 10000.0
D_FF = 4096
EPS = 1e-6
ATT_SCALE = QK_HEAD ** -0.5
N_DEV = 8
FF_SHARD = D_FF // N_DEV
OUT_SHARD = D_MODEL // N_DEV

IN_SEGS = ((0, 512), (512, 896), (896, 1280), (1280, 2304), (2304, 3328))
D_IN = 3264
D_IN_PAD = 3328
KV_END = 1216

ADAM_LR = 0.001
ADAM_B1 = 0.9
ADAM_B2 = 0.999
ADAM_EPS = 1e-08
ADAM_WD = 0.01
ADAM_STEP = 10

VMEM_LIMIT = 56 * 1024 * 1024
MESH = pl.DeviceIdType.MESH
ANY = pl.BlockSpec(memory_space=pl.ANY)
LANES = 128

SCAN_T = 256
SUBCHUNKS = 8
SCAN_CG = 512
ATT_T = 512
ATT_SUB = 2
ROW_T = 256
MM_T = 512


def _params(sem=None):
    return pltpu.CompilerParams(dimension_semantics=sem, vmem_limit_bytes=VMEM_LIMIT)


def _rows(arr, tm):
    if arr.ndim == 2:
        return pl.BlockSpec((tm, arr.shape[1]), lambda i: (i, 0))
    return pl.BlockSpec((arr.shape[0], tm, arr.shape[2]), lambda i: (0, i, 0))


def _const(arr):
    nd = arr.ndim
    return pl.BlockSpec(arr.shape, lambda *_: (0,) * nd, pipeline_mode=pl.Buffered(1))


def _sds(shape, dtype):
    return jax.ShapeDtypeStruct(shape, dtype)


PEERS = tuple((dx, dy, dc) for dx in (0, 1) for dy in (0, 1) for dc in (0, 1) if (dx, dy, dc) != (0, 0, 0))


def _here():
    x, y, c = lax.axis_index("x"), lax.axis_index("y"), lax.axis_index("c")
    return x, y, c, 4 * x + 2 * y + c


def _xchg_start(scatter, srcs, dsts, send, recv, local):
    x, y, c, me = _here()
    for e, sc in enumerate(scatter):
        src, dst = srcs[e], dsts[e]
        pltpu.make_async_copy(src.at[me] if sc else src, dst.at[me], local.at[e]).start()
        for dx, dy, dc in PEERS:
            px, py, pc = (1 - x if dx else x), (1 - y if dy else y), (1 - c if dc else c)
            pltpu.make_async_remote_copy(
                src_ref=src.at[4 * px + 2 * py + pc] if sc else src, dst_ref=dst.at[me],
                send_sem=send.at[e], recv_sem=recv.at[e], device_id=(px, py, pc), device_id_type=MESH).start()


def _xchg_wait(scatter, srcs, dsts, send, recv, local):
    x, y, c, me = _here()
    for e, sc in enumerate(scatter):
        src, dst = srcs[e], dsts[e]
        pltpu.make_async_copy(src.at[me] if sc else src, dst.at[me], local.at[e]).wait()
        span = dst.at[pl.ds(0, N_DEV - 1)]
        both = pltpu.make_async_remote_copy(src_ref=span, dst_ref=span, send_sem=send.at[e], recv_sem=recv.at[e],
                                            device_id=(x, y, c), device_id_type=MESH)
        both.wait_send()
        both.wait_recv()


def _call(body, name, grid, ins, in_specs, outs, out_specs, scratch=(), xch=()):
    n_in, n_out, ne = len(ins), len(outs), len(xch)
    scatter = [sc for _, sc in xch]
    x_outs = [_sds((N_DEV,) + (a.shape[1:] if sc else a.shape), a.dtype) for a, sc in xch]
    sems = [pltpu.SemaphoreType.DMA((ne,))] * 3 if ne else []

    def wrapped(*refs):
        in_refs, x_src = refs[:n_in], refs[n_in:n_in + ne]
        out_refs = refs[n_in + ne:n_in + ne + n_out]
        x_dst = refs[n_in + ne + n_out:n_in + 2 * ne + n_out]
        rest = refs[n_in + 2 * ne + n_out:]
        if ne:
            x_sems, rest = rest[len(rest) - 3:], rest[:len(rest) - 3]
            first = functools.reduce(jnp.logical_and, [pl.program_id(d) == 0 for d in range(len(grid))])
            last = functools.reduce(jnp.logical_and, [pl.program_id(d) == grid[d] - 1 for d in range(len(grid))])

            @pl.when(first)
            def _():
                _xchg_start(scatter, x_src, x_dst, *x_sems)

        body(*in_refs, *out_refs, *rest)
        if ne:
            @pl.when(last)
            def _():
                _xchg_wait(scatter, x_src, x_dst, *x_sems)

    return pl.pallas_call(
        wrapped,
        name=name,
        grid=grid,
        in_specs=list(in_specs) + [ANY] * ne,
        out_specs=list(out_specs) + [ANY] * ne,
        out_shape=list(outs) + x_outs,
        scratch_shapes=list(scratch) + sems,
        compiler_params=_params(("arbitrary",) * len(grid)),
    )(*ins, *[a for a, _ in xch])


def _row_call(body, name, n_rows, tm, row_ins, const_ins, row_outs, acc_outs=(), xch=()):
    outs = [_sds(s, d) for s, d in row_outs] + [_sds(s, d) for s, d in acc_outs]
    out_specs = [_rows(o, tm) for o in outs[: len(row_outs)]] + [
        pl.BlockSpec(o.shape, lambda i, nd=len(o.shape): (0,) * nd) for o in outs[len(row_outs):]]
    in_specs = [_rows(a, tm) for a in row_ins] + [_const(a) for a in const_ins]
    return _call(body, name, (n_rows // tm,), list(row_ins) + list(const_ins), in_specs, outs, out_specs, xch=xch)


def _dot(a, b):
    return jnp.dot(a, b, preferred_element_type=F32)


def _dot_nt(a, b):
    return lax.dot_general(a, b, (((1,), (1,)), ((), ())), preferred_element_type=F32)


def _dot_tn(a, b):
    return lax.dot_general(a, b, (((0,), (0,)), ((), ())), preferred_element_type=F32)


def _rms(x, g, n):
    inv = lax.rsqrt(jnp.sum(x * x, -1, keepdims=True) * (1.0 / n) + EPS)
    return x * inv * g, inv


def _rms_bwd(dy, x, g, inv, n):
    xh = x * inv
    dxh = dy * g
    dx = inv * (dxh - xh * (jnp.sum(dxh * xh, -1, keepdims=True) * (1.0 / n)))
    return dx, dy * xh


def _sigmoid(x):
    return 1.0 / (1.0 + jnp.exp(-x))


_GELU_C = math.sqrt(2.0 / math.pi)


def _gelu(y):
    th = jnp.tanh(_GELU_C * (y + 0.044715 * (y * y * y)))
    return 0.5 * y * (1.0 + th), th


def _gelu_grad(y, th):
    return 0.5 * (1.0 + th) + 0.5 * y * (1.0 - th * th) * (_GELU_C * (1.0 + 3.0 * 0.044715 * (y * y)))


def _acc(ref, val):
    @pl.when(pl.program_id(0) == 0)
    def _():
        ref[...] = jnp.zeros_like(ref)

    ref[...] += val


def _tile(n, limit):
    if n <= limit:
        return n
    return max(t for t in range(128, limit + 1, 128) if n % t == 0)


def _matmul_tn(a, b, name, tm=512, tk=512, xch=()):
    k_dim, m = a.shape
    n = b.shape[1]
    tm, tk = _tile(m, tm), _tile(k_dim, tk)

    def body(a_ref, b_ref, o_ref):
        @pl.when(pl.program_id(1) == 0)
        def _():
            o_ref[...] = jnp.zeros_like(o_ref)

        o_ref[...] += _dot_tn(a_ref[...].astype(BF16), b_ref[...].astype(BF16))

    outs = _call(
        body, name, (m // tm, k_dim // tk), [a, b],
        [pl.BlockSpec((tk, tm), lambda i, k: (k, i)), pl.BlockSpec((tk, n), lambda i, k: (k, 0))],
        [_sds((m, n), F32)], [pl.BlockSpec((tm, n), lambda i, k: (i, 0))], xch=xch)
    return outs if xch else outs[0]


def _matmul_tn_shards(a, b, name, by_col, tm=512, tk=512):
    k_dim, m = a.shape
    n = b.shape[1]
    tm, tk = _tile(m, tm), _tile(k_dim, tk)
    nk = k_dim // tk
    if by_col:
        r, c = m, n // N_DEV
        out_spec = pl.BlockSpec((N_DEV, tm, c), lambda i, k: (0, i, 0))
    else:
        r, c = m // N_DEV, n
        per = tm // r
        out_spec = pl.BlockSpec((per, r, c), lambda i, k: (i, 0, 0))

    def body(a_ref, b_ref, o_ref, acc_ref):
        k = pl.program_id(1)

        @pl.when(k == 0)
        def _():
            acc_ref[...] = jnp.zeros_like(acc_ref)

        acc_ref[...] += _dot_tn(a_ref[...].astype(BF16), b_ref[...].astype(BF16))

        @pl.when(k == nk - 1)
        def _():
            if by_col:
                for j in range(N_DEV):
                    o_ref[j] = acc_ref[:, j * c:(j + 1) * c].astype(BF16)
            else:
                for s in range(per):
                    o_ref[s] = acc_ref[s * r:(s + 1) * r, :].astype(BF16)

    return pl.pallas_call(
        body,
        name=name,
        grid=(m // tm, nk),
        in_specs=[pl.BlockSpec((tk, tm), lambda i, k: (k, i)), pl.BlockSpec((tk, n), lambda i, k: (k, 0))],
        out_specs=out_spec,
        out_shape=_sds((N_DEV, r, c), BF16),
        scratch_shapes=[pltpu.VMEM((tm, n), F32)],
        compiler_params=_params(("parallel", "arbitrary")),
    )(a, b)


def _rope_tables(pos_col):
    n = pos_col.shape[0]
    half = QK_ROPE // 2
    inv_freq = (ROPE_THETA ** (-np.arange(half, dtype=np.float32) / half)).astype(np.float32)
    freq_row = jnp.asarray(np.concatenate([inv_freq, inv_freq, np.zeros(64, np.float32)])[None, :])

    def body(p_ref, f_ref, c_ref, s_ref):
        ang = p_ref[...].astype(F32) * f_ref[...]
        c_ref[...] = jnp.cos(ang)
        s_ref[...] = jnp.sin(ang)

    return _row_call(body, "rope_tables", n, min(n, 1024), [pos_col], [freq_row], [((n, 128), F32)] * 2)


def _rope_rot(v):
    lane = lax.broadcasted_iota(jnp.int32, v.shape, 1)
    return jnp.where(lane < 32, -pltpu.roll(v, 96, 1), jnp.where(lane < 64, pltpu.roll(v, 32, 1), 0.0))


def _rope_rot_t(v):
    lane = lax.broadcasted_iota(jnp.int32, v.shape, 1)
    return jnp.where(lane < 32, pltpu.roll(v, 96, 1), jnp.where(lane < 64, -pltpu.roll(v, 32, 1), 0.0))


def _in_proj(x, norm_mix, w_in_pad, xch):
    n = x.shape[0]

    def body(x_ref, g_ref, w_ref, xn_ref, u_ref, ql_ref, kvl_ref, gs_ref, gm_ref):
        xn, _ = _rms(x_ref[...], g_ref[...], D_MODEL)
        xb = xn.astype(BF16)
        xn_ref[...] = xb
        for ref, (a, b) in zip((u_ref, ql_ref, kvl_ref, gs_ref, gm_ref), IN_SEGS):
            ref[...] = _dot(xb, w_ref[:, a:b])

    outs = [((n, D_MODEL), BF16)] + [((n, b - a), F32) for a, b in IN_SEGS]
    return _row_call(body, "in_proj", n, MM_T, [x], [norm_mix, w_in_pad], outs, xch=xch)


def _ssm_prep_fn(a_re, a_im, log_dt, b_re_x, b_im_x):
    dt = jnp.exp(log_dt)
    mag = jnp.exp(a_re * dt)
    lr = mag * jnp.cos(a_im * dt)
    li = mag * jnp.sin(a_im * dt)
    den = a_re * a_re + a_im * a_im
    fr = ((lr - 1.0) * a_re + li * a_im) / den
    fi = (li * a_re - (lr - 1.0) * a_im) / den
    return lr, li, fr * b_re_x - fi * b_im_x, fr * b_im_x + fi * b_re_x


def _dot_exact(a, b, dims):
    return lax.dot_general(a, b, (dims, ((), ())), precision=lax.Precision.HIGHEST, preferred_element_type=F32)


def _lane_repeat(width, n):
    src = lax.broadcasted_iota(jnp.int32, (width, n), 0)
    dst = lax.broadcasted_iota(jnp.int32, (width, n), 1)
    return (dst % width == src).astype(F32)


def _same_group(rows, rows_per_group, cols, cols_per_group):
    row = lax.broadcasted_iota(jnp.int32, (rows, cols), 0)
    col = lax.broadcasted_iota(jnp.int32, (rows, cols), 1)
    return (row // rows_per_group) == (col // cols_per_group)


def _expand_b(bt):
    tiled = _dot_exact(bt, _lane_repeat(SSM_STATE, N_STATE), ((1,), (0,)))
    return jnp.where(_same_group(SSM_WIDTH, SSM_GROUP_CH, N_STATE, SSM_STATE), tiled, 0.0)


def _collect_b(m):
    masked = jnp.where(_same_group(SSM_WIDTH, SSM_GROUP_CH, N_STATE, SSM_STATE), m, 0.0)
    return _dot_exact(masked, _lane_repeat(SSM_STATE, N_STATE), ((1,), (1,)))


def _ssm_prep(a_re, a_im, log_dt, bt_re, bt_im, c2_re, c2_im):
    def body(ar, ai, ld, br, bi, cr, ci, lam_ref, bblk_ref, cblk_ref):
        lr, li, bbr, bbi = _ssm_prep_fn(ar[...], ai[...], ld[...], _expand_b(br[...]), _expand_b(bi[...]))
        lam_ref[0:1, :] = lr
        lam_ref[1:2, :] = li
        bblk_ref[:, 0:N_STATE] = bbr.astype(BF16)
        bblk_ref[:, N_STATE:] = bbi.astype(BF16)
        rep = _lane_repeat(SSM_GROUP_CH, SSM_WIDTH)
        own = _same_group(N_STATE, SSM_STATE, SSM_WIDTH, SSM_GROUP_CH)
        cblk_ref[0:N_STATE, :] = jnp.where(own, _dot_exact(cr[...], rep, ((1,), (0,))), 0.0).astype(BF16)
        cblk_ref[N_STATE:, :] = jnp.where(own, -_dot_exact(ci[...], rep, ((1,), (0,))), 0.0).astype(BF16)

    return pl.pallas_call(
        body,
        name="ssm_prep",
        out_shape=[_sds((2, N_STATE), F32), _sds((SSM_WIDTH, 2 * N_STATE), BF16),
                   _sds((2 * N_STATE, SSM_WIDTH), BF16)],
        compiler_params=_params(),
    )(a_re, a_im, log_dt, bt_re, bt_im, c2_re, c2_im)


def _ssm_prep_bwd(a_re, a_im, log_dt, bt_re, bt_im, dlam, dbblk, dcblk_t):
    def body(ar, ai, ld, br, bi, dl, db, dc, dar, dai, dld, dbr, dbi, dcr, dci):
        _, vjp = jax.vjp(_ssm_prep_fn, ar[...], ai[...], ld[...], _expand_b(br[...]), _expand_b(bi[...]))
        g = vjp((dl[0:1, :], dl[1:2, :], db[:, 0:N_STATE], db[:, N_STATE:]))
        dar[...] = g[0]
        dai[...] = g[1]
        grp = lax.broadcasted_iota(jnp.int32, (SSM_GROUPS, N_STATE), 0)
        lane = lax.broadcasted_iota(jnp.int32, (SSM_GROUPS, N_STATE), 1)
        sel = (lane // SSM_STATE) == grp
        dld[...] = jnp.sum(jnp.where(sel, jnp.broadcast_to(g[2], (SSM_GROUPS, N_STATE)), 0.0), axis=1, keepdims=True)
        dbr[...] = _collect_b(g[3])
        dbi[...] = _collect_b(g[4])
        dcr[...] = _collect_b(dc[:, 0:N_STATE])
        dci[...] = -_collect_b(dc[:, N_STATE:])

    small = _sds((SSM_WIDTH, SSM_STATE), F32)
    return pl.pallas_call(
        body,
        name="ssm_prep_bwd",
        out_shape=[_sds((1, N_STATE), F32), _sds((1, N_STATE), F32), _sds((SSM_GROUPS, 1), F32), small, small, small, small],
        compiler_params=_params(),
    )(a_re, a_im, log_dt, bt_re, bt_im, dlam, dbblk, dcblk_t)


def _perm_matrix(t):
    run = t // SUBCHUNKS
    p = np.zeros((t, t), np.float32)
    r = np.arange(t)
    p[r, (r % SUBCHUNKS) * run + r // SUBCHUNKS] = 1.0
    return jnp.asarray(p, dtype=BF16)


def _unpermute(p, a):
    hi = a.astype(BF16)
    r1 = a - hi.astype(F32)
    mid = r1.astype(BF16)
    lo = (r1 - mid.astype(F32)).astype(BF16)
    return _dot_tn(p, hi) + _dot_tn(p, mid) + _dot_tn(p, lo)


def _power_table(lam_ref, pw_ref, n):
    lr, li = lam_ref[0:1, :], lam_ref[1:2, :]
    pw_ref[0:1, 0:N_STATE] = lr
    pw_ref[0:1, N_STATE:] = li

    def step(i, carry):
        pr, pi = carry
        pr, pi = pr * lr - pi * li, pr * li + pi * lr
        pw_ref[pl.ds(i, 1), 0:N_STATE] = pr
        pw_ref[pl.ds(i, 1), N_STATE:] = pi
        return pr, pi

    lax.fori_loop(1, n, step, (lr, li))


def _col_groups():
    return [(pl.ds(c, SCAN_CG), pl.ds(N_STATE + c, SCAN_CG)) for c in range(0, N_STATE, SCAN_CG)]


def _run_scan(buf, lam_ref, t, reverse):
    nblk = t // 8
    for re, im in _col_groups():
        lr = jnp.broadcast_to(lam_ref[0:1, re], (8, SCAN_CG))
        li = jnp.broadcast_to(lam_ref[1:2, re], (8, SCAN_CG))
        if reverse:
            li = -li
        first = pl.ds((nblk - 1) * 8 if reverse else 0, 8)

        def step(k, carry, re=re, im=im, lr=lr, li=li):
            pr, pi = carry
            i = (nblk - 2 - k) if reverse else (k + 1)
            r = pl.ds(pl.multiple_of(i * 8, 8), 8)
            xr = buf[r, re] + lr * pr - li * pi
            xi = buf[r, im] + lr * pi + li * pr
            buf[r, re] = xr
            buf[r, im] = xi
            return xr, xi

        lax.fori_loop(0, nblk - 1, step, (buf[first, re], buf[first, im]))


def _run_carries(buf, pw_ref, carry_ref, s_ref, t, reverse):
    nblk = t // 8
    run = t // SUBCHUNKS
    edge = buf[pl.ds(0 if reverse else (nblk - 1) * 8, 8), :]
    pr, pi = pw_ref[run - 1:run, 0:N_STATE], pw_ref[run - 1:run, N_STATE:]
    if reverse:
        pi = -pi
    sr, si = carry_ref[0:1, 0:N_STATE], carry_ref[0:1, N_STATE:]
    for s in (range(SUBCHUNKS - 1, -1, -1) if reverse else range(SUBCHUNKS)):
        s_ref[s:s + 1, 0:N_STATE] = sr
        s_ref[s:s + 1, N_STATE:] = si
        er, ei = edge[s:s + 1, 0:N_STATE], edge[s:s + 1, N_STATE:]
        sr, si = er + pr * sr - pi * si, ei + pr * si + pi * sr
    carry_ref[:, 0:N_STATE] = jnp.broadcast_to(sr, (8, N_STATE))
    carry_ref[:, N_STATE:] = jnp.broadcast_to(si, (8, N_STATE))


def _run_fix(buf, pw_ref, s_ref, t, reverse):
    nblk = t // 8
    for re, im in _col_groups():
        sr, si = s_ref[:, re], s_ref[:, im]

        def step(i, carry, re=re, im=im, sr=sr, si=si):
            r = pl.ds(pl.multiple_of(i * 8, 8), 8)
            row = pl.ds((nblk - 1 - i) if reverse else i, 1)
            pr, pi = pw_ref[row, re], pw_ref[row, im]
            if reverse:
                pi = -pi
            buf[r, re] += pr * sr - pi * si
            buf[r, im] += pr * si + pi * sr
            return carry

        lax.fori_loop(0, nblk, step, 0)


def _ssm_fwd(u, bblk, cblk, lam, d_row, w_glu, b_glu, w_o_ssm, xch):
    n = u.shape[0]
    t = min(SCAN_T, n)
    kb = 512
    perm = _perm_matrix(t)

    def body(u_ref, p_ref, bblk_ref, cblk_ref, lam_ref, d_ref, wg_ref, bg_ref, wo_ref, y_ref, ys_ref, st_ref,
             buf, pw_ref, carry_ref, s_ref):
        @pl.when(pl.program_id(0) == 0)
        def _():
            carry_ref[...] = jnp.zeros_like(carry_ref)
            _power_table(lam_ref, pw_ref, t // SUBCHUNKS)

        st_ref[0] = carry_ref[...]
        u_t = u_ref[...]
        p = p_ref[...]
        ub = _dot(p, u_t.astype(BF16)).astype(BF16)
        for c in range(0, 2 * N_STATE, kb):
            buf[:, c:c + kb] = _dot(ub, bblk_ref[:, c:c + kb])
        _run_scan(buf, lam_ref, t, False)
        _run_carries(buf, pw_ref, carry_ref, s_ref, t, False)
        _run_fix(buf, pw_ref, s_ref, t, False)
        yp = jnp.zeros((t, SSM_WIDTH), F32)
        for c in range(0, 2 * N_STATE, kb):
            yp += _dot(buf[:, c:c + kb].astype(BF16), cblk_ref[c:c + kb, :])
        y = d_ref[...] * u_t + _unpermute(p, yp)
        y_ref[...] = y
        z, _ = _gelu(y)
        s = _sigmoid(_dot(z.astype(BF16), wg_ref[...]) + bg_ref[...])
        zgb = (z * s).astype(BF16)
        for j in range(N_DEV):
            ys_ref[:, j * OUT_SHARD:(j + 1) * OUT_SHARD] = _dot(zgb, wo_ref[j])

    consts = [perm, bblk, cblk, lam, d_row, w_glu, b_glu, w_o_ssm]
    return _call(
        body, "ssm_fwd", (n // t,), [u] + consts, [_rows(u, t)] + [_const(a) for a in consts],
        [_sds((n, SSM_WIDTH), F32), _sds((n, D_MODEL), F32), _sds((n // t, 8, 2 * N_STATE), F32)],
        [pl.BlockSpec((t, SSM_WIDTH), lambda i: (i, 0)), pl.BlockSpec((t, D_MODEL), lambda i: (i, 0)),
         pl.BlockSpec((1, 8, 2 * N_STATE), lambda i: (i, 0, 0))],
        scratch=[pltpu.VMEM((t, 2 * N_STATE), F32), pltpu.VMEM((t // SUBCHUNKS, 2 * N_STATE), F32),
                 pltpu.VMEM((8, 2 * N_STATE), F32), pltpu.VMEM((8, 2 * N_STATE), F32)],
        xch=xch)


def _head_norm_rope(slab, gain, cos_t, sin_t):
    xn, inv = _rms(slab, gain, QK_HEAD)
    lo, hi = xn[:, 0:128], xn[:, 128:256]
    return jnp.concatenate([lo, hi * cos_t + _rope_rot(hi) * sin_t], axis=-1), inv


def _head_norm_rope_bwd(g, slab, gain, inv, cos_t, sin_t):
    g_lo, g_hi = g[:, 0:128], g[:, 128:256]
    g_n = jnp.concatenate([g_lo, g_hi * cos_t + _rope_rot_t(g_hi * sin_t)], axis=-1)
    return _rms_bwd(g_n, slab, gain, inv, QK_HEAD)


def _qkv_prep(ql, kvl, q_a_norm, kv_a_norm, wq, wkv, gq, gk, cos_t, sin_t):
    n = ql.shape[0]
    tm = ROW_T

    def body(ql_ref, kvl_ref, cos_ref, sin_ref, qa_ref, ka_ref, wq_ref, wkv_ref, gq_ref, gk_ref,
             q_ref, k_ref, v_ref, kt_ref, vt_ref):
        cos_t, sin_t = cos_ref[...], sin_ref[...]
        qa, _ = _rms(ql_ref[...], qa_ref[...], Q_LORA)
        qab = qa.astype(BF16)
        kvl_t = kvl_ref[...]
        ca, _ = _rms(kvl_t[:, 0:KV_LORA], ka_ref[...], KV_LORA)
        cab = ca.astype(BF16)
        kpe = kvl_t[:, KV_LORA:KV_LAT_PAD]
        q_pre = _dot(qab, wq_ref[...])
        kv_pre = _dot(cab, wkv_ref[...])
        for h in range(N_HEADS):
            qh, _ = _head_norm_rope(q_pre[:, h * QK_PAD:(h + 1) * QK_PAD], gq_ref[...], cos_t, sin_t)
            q_ref[h] = (qh * ATT_SCALE).astype(BF16)
            kv_h = kv_pre[:, h * QK_PAD:(h + 1) * QK_PAD]
            kh, _ = _head_norm_rope(jnp.concatenate([kv_h[:, 0:QK_NOPE], kpe], axis=-1), gk_ref[...], cos_t, sin_t)
            k_ref[h] = kh.astype(BF16)
            kt_ref[h] = kh.T.astype(BF16)
            vh = kv_h[:, QK_NOPE:]
            v_ref[h] = vh.astype(BF16)
            vt_ref[h] = vh.T.astype(BF16)

    row_ins, consts = [ql, kvl, cos_t, sin_t], [q_a_norm, kv_a_norm, wq, wkv, gq, gk]
    outs = [_sds((N_HEADS, n, QK_PAD), BF16), _sds((N_HEADS, n, QK_PAD), BF16), _sds((N_HEADS, n, V_HEAD), BF16),
            _sds((N_HEADS, QK_PAD, n), BF16), _sds((N_HEADS, V_HEAD, n), BF16)]
    out_specs = [_rows(o, tm) for o in outs[:3]] + [
        pl.BlockSpec((N_HEADS, QK_PAD, tm), lambda i: (0, 0, i)), pl.BlockSpec((N_HEADS, V_HEAD, tm), lambda i: (0, 0, i))]
    return _call(body, "qkv_prep", (n // tm,), row_ins + consts,
                 [_rows(a, tm) for a in row_ins] + [_const(a) for a in consts], outs, out_specs)


def _causal_mask_t(st, t):
    key = lax.broadcasted_iota(jnp.int32, (t, t), 0)
    qry = lax.broadcasted_iota(jnp.int32, (t, t), 1)
    return jnp.where(key <= qry, st, -jnp.inf)


def _attn_fwd(q, k, vt, xch):
    n = q.shape[1]
    t = min(ATT_T, n)

    def body(q_ref, k_ref, vt_ref, o_ref, lse_ref):
        i = pl.program_id(1)
        qt = q_ref[0]

        def kv_tile(j, carry, diag):
            m, l, acc = carry
            ts = t // ATT_SUB
            sts = []
            for a in range(ATT_SUB):
                r0 = pl.multiple_of(j * t + a * ts, ts)
                st = _dot_nt(k_ref[0, pl.ds(r0, ts), :], qt)
                if diag:
                    key = lax.broadcasted_iota(jnp.int32, (ts, t), 0) + a * ts
                    qry = lax.broadcasted_iota(jnp.int32, (ts, t), 1)
                    st = jnp.where(key <= qry, st, -jnp.inf)
                sts.append(st)
            for a, st in enumerate(sts):
                r0 = pl.multiple_of(j * t + a * ts, ts)
                m_new = jnp.maximum(m, jnp.max(st, 0, keepdims=True))
                alpha = jnp.exp(m - m_new)
                pt = jnp.exp(st - m_new)
                l = alpha * l + jnp.sum(pt, 0, keepdims=True)
                acc = alpha * acc + _dot(vt_ref[0, :, pl.ds(r0, ts)], pt.astype(BF16))
                m = m_new
            return m, l, acc

        init = (jnp.full((1, t), -jnp.inf, F32), jnp.zeros((1, t), F32), jnp.zeros((V_HEAD, t), F32))
        carry = lax.fori_loop(0, i, functools.partial(kv_tile, diag=False), init)
        m, l, acc = kv_tile(i, carry, True)
        o_ref[...] = (acc / l).T
        lse_ref[0] = m + jnp.log(l)

    return _call(
        body, "attn_fwd", (N_HEADS, n // t), [q, k, vt],
        [pl.BlockSpec((1, t, QK_PAD), lambda h, i: (h, i, 0)), pl.BlockSpec((1, n, QK_PAD), lambda h, i: (h, 0, 0)),
         pl.BlockSpec((1, V_HEAD, n), lambda h, i: (h, 0, 0))],
        [_sds((n, N_HEADS * V_HEAD), F32), _sds((N_HEADS, 1, n), F32)],
        [pl.BlockSpec((t, V_HEAD), lambda h, i: (i, h)), pl.BlockSpec((1, 1, t), lambda h, i: (h, 0, i))],
        xch=xch)


def _merge(attn, gs, gm, y_ssm, x, w_o_mla, w_out):
    n = x.shape[0]

    def body(at_ref, gs_ref, gm_ref, ys_ref, x_ref, wo_ref, wout_ref, h_ref, mx_ref, ym_ref):
        y_mla = _dot(at_ref[...].astype(BF16), wo_ref[...])
        ym_ref[...] = y_mla
        mixed = (_sigmoid(gs_ref[...]) * ys_ref[...] + _sigmoid(gm_ref[...]) * y_mla).astype(BF16)
        mx_ref[...] = mixed
        h_ref[...] = x_ref[...] + _dot(mixed, wout_ref[...])

    outs = [((n, D_MODEL), F32), ((n, D_MODEL), BF16), ((n, D_MODEL), F32)]
    return _row_call(body, "merge", n, MM_T, [attn, gs, gm, y_ssm, x], [w_o_mla, w_out], outs)


def _mlp_fwd_loss(h, target, norm_mlp, w_up, w_down):
    n = h.shape[0]

    def body(h_ref, t_ref, g_ref, wu_ref, wd_ref, hn_ref, do_ref, loss_ref):
        h_t = h_ref[...]
        hn, _ = _rms(h_t, g_ref[...], D_MODEL)
        hb = hn.astype(BF16)
        hn_ref[...] = hb
        out = h_t
        for j in range(N_DEV):
            a = jnp.maximum(_dot(hb, wu_ref[j]), 0.0)
            out += _dot((a * a).astype(BF16), wd_ref[j])
        err = out - t_ref[...]
        do_ref[...] = err * (1.0 / D_MODEL)
        _acc(loss_ref, jnp.broadcast_to(jnp.sum(err * err) * (0.5 / D_MODEL), loss_ref.shape))

    outs = [((n, D_MODEL), BF16), ((n, D_MODEL), F32)]
    return _row_call(body, "mlp_fwd_loss", n, MM_T, [h, target], [norm_mlp, w_up, w_down], outs, [((8, 128), F32)])


def _mlp_bwd(dout, hn, h, norm_mlp, w_up, w_down):
    n = h.shape[0]

    def body(do_ref, hn_ref, h_ref, g_ref, wu_ref, wd_ref, hid_ref, da_ref, dh_ref, dg_ref):
        dout_t = do_ref[...]
        doutb = dout_t.astype(BF16)
        hb = hn_ref[...]
        dhn = jnp.zeros_like(dout_t)
        for j in range(N_DEV):
            cols = slice(j * FF_SHARD, (j + 1) * FF_SHARD)
            a = jnp.maximum(_dot(hb, wu_ref[j]), 0.0)
            hid_ref[:, cols] = (a * a).astype(BF16)
            da = (_dot_nt(doutb, wd_ref[j]) * (2.0 * a)).astype(BF16)
            da_ref[:, cols] = da
            dhn += _dot_nt(da, wu_ref[j])
        h_t = h_ref[...]
        inv = lax.rsqrt(jnp.sum(h_t * h_t, -1, keepdims=True) * (1.0 / D_MODEL) + EPS)
        dx, dg = _rms_bwd(dhn, h_t, g_ref[...], inv, D_MODEL)
        dh_ref[...] = dout_t + dx
        _acc(dg_ref, jnp.sum(dg, 0, keepdims=True))

    outs = [((n, D_FF), BF16), ((n, D_FF), BF16), ((n, D_MODEL), F32)]
    return _row_call(body, "mlp_bwd", n, MM_T, [dout, hn, h], [norm_mlp, w_up, w_down], outs, [((1, D_MODEL), F32)])


def _merge_bwd(dh, gs, gm, y_ssm, y_mla, w_out, w_o_mla):
    n = dh.shape[0]

    def body(dh_ref, gs_ref, gm_ref, ys_ref, ym_ref, wout_ref, wo_ref, dgs_ref, dgm_ref, dys_ref, dym_ref, dat_ref):
        dmix = _dot_nt(dh_ref[...].astype(BF16), wout_ref[...])
        sgs, sgm = _sigmoid(gs_ref[...]), _sigmoid(gm_ref[...])
        dgs_ref[...] = (dmix * ys_ref[...] * sgs * (1.0 - sgs)).astype(BF16)
        dgm_ref[...] = (dmix * ym_ref[...] * sgm * (1.0 - sgm)).astype(BF16)
        dys_ref[...] = (dmix * sgs).astype(BF16)
        dym = (dmix * sgm).astype(BF16)
        dym_ref[...] = dym
        dat_ref[...] = _dot_nt(dym, wo_ref[...])

    outs = [((n, D_MODEL), BF16)] * 4 + [((n, D_MODEL), F32)]
    return _row_call(body, "merge_bwd", n, MM_T, [dh, gs, gm, y_ssm, y_mla], [w_out, w_o_mla], outs)


def _attn_bwd(q, k, kt, v, out, lse, dout, xch):
    n = q.shape[1]
    t = min(ATT_T, n)
    nt = n // t

    def body(q_ref, k_ref, kt_ref, v_ref, o_ref, lse_ref, do_ref, dq_ref, dk_ref, dv_ref, delta_ref, dqt_ref):
        j = pl.program_id(1)

        @pl.when(j == 0)
        def _():
            dqt_ref[...] = jnp.zeros_like(dqt_ref)
            prod = do_ref[...] * o_ref[...]
            delta_ref[...] = lax.dot_general(jnp.ones((8, V_HEAD), F32), prod, (((1,), (1,)), ((), ())),
                                             precision=lax.Precision.HIGHEST, preferred_element_type=F32)

        k_t = k_ref[0]
        kt_t = kt_ref[0]
        v_t = v_ref[0]

        def q_tile(i, carry, diag):
            dk, dv = carry
            r0 = pl.multiple_of(i * t, t)
            rows = pl.ds(r0, t)
            qt = q_ref[0, rows, :]
            st = _dot_nt(k_t, qt)
            if diag:
                st = _causal_mask_t(st, t)
            pt = jnp.exp(st - lse_ref[0, :, rows])
            dob = do_ref[rows, :].astype(BF16)
            dv = dv + _dot(pt.astype(BF16), dob)
            dst = (pt * (_dot_nt(v_t, dob) - delta_ref[0:1, rows])).astype(BF16)
            dk = dk + _dot(dst, qt)
            dqt_ref[:, rows] += _dot(kt_t, dst)
            return dk, dv

        carry = q_tile(j, (jnp.zeros((t, QK_PAD), F32), jnp.zeros((t, V_HEAD), F32)), True)
        dk, dv = lax.fori_loop(j + 1, nt, functools.partial(q_tile, diag=False), carry)
        dk_ref[0] = dk
        dv_ref[0] = dv

        @pl.when(j == nt - 1)
        def _():
            for c in range(0, n, t):
                dq_ref[0, c:c + t, :] = dqt_ref[:, c:c + t].T

    return _call(
        body, "attn_bwd", (N_HEADS, nt), [q, k, kt, v, out, lse, dout],
        [pl.BlockSpec((1, n, QK_PAD), lambda h, j: (h, 0, 0)), pl.BlockSpec((1, t, QK_PAD), lambda h, j: (h, j, 0)),
         pl.BlockSpec((1, QK_PAD, t), lambda h, j: (h, 0, j)), pl.BlockSpec((1, t, V_HEAD), lambda h, j: (h, j, 0)),
         pl.BlockSpec((n, V_HEAD), lambda h, j: (0, h)), pl.BlockSpec((1, 1, n), lambda h, j: (h, 0, 0)),
         pl.BlockSpec((n, V_HEAD), lambda h, j: (0, h))],
        [_sds((N_HEADS, n, QK_PAD), F32), _sds((N_HEADS, n, QK_PAD), F32), _sds((N_HEADS, n, V_HEAD), F32)],
        [pl.BlockSpec((1, n, QK_PAD), lambda h, j: (h, 0, 0)), pl.BlockSpec((1, t, QK_PAD), lambda h, j: (h, j, 0)),
         pl.BlockSpec((1, t, V_HEAD), lambda h, j: (h, j, 0))],
        scratch=[pltpu.VMEM((8, n), F32), pltpu.VMEM((QK_PAD, n), F32)],
        xch=xch)


def _qkv_prep_bwd(ql, kvl, dq, dk, dv, q_a_norm, kv_a_norm, wq, wkv, gq, gk, cos_t, sin_t, xch):
    n = ql.shape[0]

    def body(ql_ref, kvl_ref, cos_ref, sin_ref, dq_ref, dk_ref, dv_ref, qa_ref, ka_ref, wq_ref, wkv_ref, gq_ref, gk_ref,
             dql_ref, dkvl_ref, qab_ref, dqp_ref, cab_ref, dkvp_ref, dqa_ref, dka_ref, dgq_ref, dgk_ref):
        cos_t, sin_t = cos_ref[...], sin_ref[...]
        ql_t = ql_ref[...]
        qa, inv_qa = _rms(ql_t, qa_ref[...], Q_LORA)
        qab = qa.astype(BF16)
        qab_ref[...] = qab
        kvl_t = kvl_ref[...]
        ckv = kvl_t[:, 0:KV_LORA]
        ca, inv_ca = _rms(ckv, ka_ref[...], KV_LORA)
        cab = ca.astype(BF16)
        cab_ref[...] = cab
        kpe = kvl_t[:, KV_LORA:KV_LAT_PAD]
        dgq = jnp.zeros((1, QK_PAD), F32)
        dgk = jnp.zeros((1, QK_PAD), F32)
        dkpe = jnp.zeros_like(kpe)
        q_pre = _dot(qab, wq_ref[...])
        kv_pre = _dot(cab, wkv_ref[...])
        for h in range(N_HEADS):
            head = slice(h * QK_PAD, (h + 1) * QK_PAD)
            q_slab = q_pre[:, head]
            inv = lax.rsqrt(jnp.sum(q_slab * q_slab, -1, keepdims=True) * (1.0 / QK_HEAD) + EPS)
            d_slab, dg = _head_norm_rope_bwd(dq_ref[h] * ATT_SCALE, q_slab, gq_ref[...], inv, cos_t, sin_t)
            dqp_ref[:, head] = d_slab.astype(BF16)
            dgq += jnp.sum(dg, 0, keepdims=True)
            k_slab = jnp.concatenate([kv_pre[:, h * QK_PAD:h * QK_PAD + QK_NOPE], kpe], axis=-1)
            inv = lax.rsqrt(jnp.sum(k_slab * k_slab, -1, keepdims=True) * (1.0 / QK_HEAD) + EPS)
            d_slab, dg = _head_norm_rope_bwd(dk_ref[h], k_slab, gk_ref[...], inv, cos_t, sin_t)
            dkvp_ref[:, head] = jnp.concatenate([d_slab[:, 0:QK_NOPE], dv_ref[h]], axis=-1).astype(BF16)
            dkpe += d_slab[:, QK_NOPE:QK_PAD]
            dgk += jnp.sum(dg, 0, keepdims=True)
        dqa = _dot_nt(dqp_ref[...], wq_ref[...])
        dx, dg = _rms_bwd(dqa, ql_t, qa_ref[...], inv_qa, Q_LORA)
        dql_ref[...] = dx.astype(BF16)
        _acc(dqa_ref, jnp.sum(dg, 0, keepdims=True))
        dca = _dot_nt(dkvp_ref[...], wkv_ref[...])
        dx, dg = _rms_bwd(dca, ckv, ka_ref[...], inv_ca, KV_LORA)
        dkvl_ref[:, 0:KV_LORA] = dx.astype(BF16)
        dkvl_ref[:, KV_LORA:KV_LAT_PAD] = dkpe.astype(BF16)
        _acc(dka_ref, jnp.sum(dg, 0, keepdims=True))
        _acc(dgq_ref, dgq)
        _acc(dgk_ref, dgk)

    row_outs = [((n, Q_LORA), BF16), ((n, KV_LAT_PAD), BF16), ((n, Q_LORA), BF16), ((n, N_HEADS * QK_PAD), BF16),
                ((n, KV_LORA), BF16), ((n, N_HEADS * (QK_NOPE + V_HEAD)), BF16)]
    acc_outs = [((1, Q_LORA), F32), ((1, KV_LORA), F32), ((1, QK_PAD), F32), ((1, QK_PAD), F32)]
    return _row_call(body, "qkv_prep_bwd", n, ROW_T, [ql, kvl, cos_t, sin_t, dq, dk, dv],
                     [q_a_norm, kv_a_norm, wq, wkv, gq, gk], row_outs, acc_outs, xch=xch)


def _glu_bwd(dy_ssm, y, w_glu, b_glu, w_o_ssm):
    n = y.shape[0]

    def body(dys_ref, y_ref, wg_ref, bg_ref, wo_ref, dy_ref, zg_ref, z_ref, dt_ref, db_ref):
        y_t = y_ref[...]
        z, th = _gelu(y_t)
        zb = z.astype(BF16)
        z_ref[...] = zb
        s = _sigmoid(_dot(zb, wg_ref[...]) + bg_ref[...])
        zg_ref[...] = (z * s).astype(BF16)
        dys = dys_ref[...]
        dzg = jnp.zeros_like(y_t)
        for j in range(N_DEV):
            dzg += _dot_nt(dys[:, j * OUT_SHARD:(j + 1) * OUT_SHARD], wo_ref[j])
        dt = dzg * z * s * (1.0 - s)
        dtb = dt.astype(BF16)
        dt_ref[...] = dtb
        dz = dzg * s + _dot_nt(dtb, wg_ref[...])
        dy_ref[...] = dz * _gelu_grad(y_t, th)
        _acc(db_ref, jnp.sum(dt, 0, keepdims=True))

    outs = [((n, SSM_WIDTH), F32)] + [((n, SSM_WIDTH), BF16)] * 3
    return _row_call(body, "glu_bwd", n, ROW_T, [dy_ssm, y], [w_glu, b_glu, w_o_ssm], outs, [((1, SSM_WIDTH), F32)])


def _ssm_bwd(u, dy, st, bblk, cblk, lam, d_row, xch):
    n = u.shape[0]
    t = min(SCAN_T, n)
    nc = n // t
    kb = 512
    perm = _perm_matrix(t)

    def body(u_ref, dy_ref, st_ref, p_ref, bblk_ref, cblk_ref, lam_ref, d_ref,
             du_ref, xs_ref, as_ref, up_ref, dyp_ref, dlam_ref, dd_ref,
             buf_x, buf_a, pw_ref, carry_ref, xcarry_ref, sx_ref, sa_ref):
        @pl.when(pl.program_id(0) == 0)
        def _():
            carry_ref[...] = jnp.zeros_like(carry_ref)
            _power_table(lam_ref, pw_ref, t // SUBCHUNKS)

        u_t = u_ref[...]
        dy_t = dy_ref[...]
        p = p_ref[...]
        ub = _dot(p, u_t.astype(BF16)).astype(BF16)
        dyb = _dot(p, dy_t.astype(BF16)).astype(BF16)
        up_ref[...] = ub
        dyp_ref[...] = dyb
        for c in range(0, 2 * N_STATE, kb):
            buf_x[:, c:c + kb] = _dot(ub, bblk_ref[:, c:c + kb])
        xcarry_ref[...] = st_ref[0]
        _run_scan(buf_x, lam_ref, t, False)
        _run_carries(buf_x, pw_ref, xcarry_ref, sx_ref, t, False)
        _run_fix(buf_x, pw_ref, sx_ref, t, False)
        for c in range(0, 2 * N_STATE, kb):
            buf_a[:, c:c + kb] = _dot_nt(dyb, cblk_ref[c:c + kb, :])
        _run_scan(buf_a, lam_ref, t, True)
        _run_carries(buf_a, pw_ref, carry_ref, sa_ref, t, True)
        _run_fix(buf_a, pw_ref, sa_ref, t, True)
        dup = jnp.zeros((t, SSM_WIDTH), F32)
        for c in range(0, 2 * N_STATE, kb):
            adb = buf_a[:, c:c + kb].astype(BF16)
            as_ref[:, c:c + kb] = adb
            xs_ref[:, c:c + kb] = buf_x[:, c:c + kb].astype(BF16)
            dup += _dot_nt(adb, bblk_ref[:, c:c + kb])
        du_ref[...] = (d_ref[...] * dy_t + _unpermute(p, dup)).astype(BF16)
        for c in range(0, N_STATE, kb):
            re, im = pl.ds(c, kb), pl.ds(N_STATE + c, kb)
            xr, xi = buf_x[pl.ds(0, t - 8), re], buf_x[pl.ds(0, t - 8), im]
            ar, ai = buf_a[pl.ds(8, t - 8), re], buf_a[pl.ds(8, t - 8), im]
            x0r, x0i = sx_ref[:, re], sx_ref[:, im]
            a0r, a0i = buf_a[0:8, re], buf_a[0:8, im]
            dlam_part_re = (jnp.sum(ar * xr + ai * xi, 0, keepdims=True)
                            + jnp.sum(a0r * x0r + a0i * x0i, 0, keepdims=True))
            dlam_part_im = (jnp.sum(ai * xr - ar * xi, 0, keepdims=True)
                            + jnp.sum(a0i * x0r - a0r * x0i, 0, keepdims=True))

            @pl.when(pl.program_id(0) == 0)
            def _(c=c):
                dlam_ref[0:1, c:c + kb] = jnp.zeros((1, kb), F32)
                dlam_ref[1:2, c:c + kb] = jnp.zeros((1, kb), F32)

            dlam_ref[0:1, c:c + kb] += dlam_part_re
            dlam_ref[1:2, c:c + kb] += dlam_part_im
        _acc(dd_ref, jnp.sum(dy_t * u_t, 0, keepdims=True))

    rev = lambda i: (nc - 1 - i, 0)
    consts = [perm, bblk, cblk, lam, d_row]
    return _call(
        body, "ssm_bwd", (nc,), [u, dy, st] + consts,
        [pl.BlockSpec((t, SSM_WIDTH), rev), pl.BlockSpec((t, SSM_WIDTH), rev),
         pl.BlockSpec((1, 8, 2 * N_STATE), lambda i: (nc - 1 - i, 0, 0))] + [_const(a) for a in consts],
        [_sds((n, SSM_WIDTH), BF16), _sds((n, 2 * N_STATE), BF16), _sds((n, 2 * N_STATE), BF16),
         _sds((n, SSM_WIDTH), BF16), _sds((n, SSM_WIDTH), BF16), _sds((2, N_STATE), F32), _sds((1, SSM_WIDTH), F32)],
        [pl.BlockSpec((t, SSM_WIDTH), rev), pl.BlockSpec((t, 2 * N_STATE), rev), pl.BlockSpec((t, 2 * N_STATE), rev),
         pl.BlockSpec((t, SSM_WIDTH), rev), pl.BlockSpec((t, SSM_WIDTH), rev),
         pl.BlockSpec((2, N_STATE), lambda i: (0, 0)), pl.BlockSpec((1, SSM_WIDTH), lambda i: (0, 0))],
        scratch=[pltpu.VMEM((t, 2 * N_STATE), F32)] * 2 + [pltpu.VMEM((t // SUBCHUNKS, 2 * N_STATE), F32)]
        + [pltpu.VMEM((8, 2 * N_STATE), F32)] * 4,
        xch=xch)


def _in_proj_bwd(pieces, dh, x, norm_mix, w_in_pad):
    n = x.shape[0]

    def body(du_ref, dql_ref, dkvl_ref, dgs_ref, dgm_ref, dh_ref, x_ref, g_ref, w_ref, dx_ref, dp_ref, dg_ref):
        dxn = jnp.zeros((dh_ref.shape[0], D_MODEL), F32)
        for ref, (a, b) in zip((du_ref, dql_ref, dkvl_ref, dgs_ref, dgm_ref), IN_SEGS):
            piece = ref[...]
            dp_ref[:, a:b] = piece
            dxn += _dot_nt(piece, w_ref[:, a:b])
        x_t = x_ref[...]
        inv = lax.rsqrt(jnp.sum(x_t * x_t, -1, keepdims=True) * (1.0 / D_MODEL) + EPS)
        dx, dg = _rms_bwd(dxn, x_t, g_ref[...], inv, D_MODEL)
        dx_ref[...] = dh_ref[...] + dx
        _acc(dg_ref, jnp.sum(dg, 0, keepdims=True))

    outs = [((n, D_MODEL), F32), ((n, D_IN_PAD), BF16)]
    return _row_call(body, "in_proj_bwd", n, MM_T, list(pieces) + [dh, x], [norm_mix, w_in_pad], outs,
                     [((1, D_MODEL), F32)])


def _swap_minor(a):
    g, r, c = a.shape[1:]
    return jnp.transpose(a[0], (0, 2, 1)).reshape(g * c, r)


def _pad_in(w):
    return jnp.concatenate([w[:, :KV_END], jnp.zeros((w.shape[0], D_IN_PAD - D_IN), w.dtype), w[:, KV_END:]], axis=1)


def _unpad_in(w):
    return jnp.concatenate([w[:, :KV_END], w[:, KV_END + D_IN_PAD - D_IN:]], axis=1)


def _pad_gain(g):
    return jnp.pad(g, ((0, 0), (0, QK_PAD - QK_HEAD)))


def _place():
    x, y, c = lax.axis_index("x"), lax.axis_index("y"), lax.axis_index("c")
    chips = [(x, y), (1 - x, y), (x, 1 - y), (1 - x, 1 - y)]
    return x, y, c, chips


def _all_gather(block, name):
    rows, lanes = block.shape

    def body(x_ref, out_ref, send_sems, recv_sems, local_sem):
        x, y, c, chips = _place()
        me, sibling = (x, y, c), (x, y, 1 - c)

        def slot(px, py, pc):
            return out_ref.at[4 * px + 2 * py + pc]

        def copy(k, blk, to, src=None):
            return pltpu.make_async_remote_copy(
                src_ref=slot(*blk) if src is None else src, dst_ref=slot(*blk),
                send_sem=send_sems.at[k], recv_sem=recv_sems.at[k], device_id=to, device_id_type=MESH)

        mine = pltpu.make_async_copy(x_ref, slot(*me), local_sem)
        mine.start()
        first = [copy(0, me, sibling, src=x_ref)]
        first += [copy(1 + j, me, (*chip, c), src=x_ref) for j, chip in enumerate(chips[1:])]
        for cp in first:
            cp.start()
        passed = [copy(4 + j, (*chip, c), sibling) for j, chip in enumerate(chips[1:])]
        for j, chip in enumerate(chips[1:]):
            copy(1 + j, (*chip, c), me).wait_recv()
            passed[j].start()
        copy(0, sibling, me).wait_recv()
        for j, chip in enumerate(chips[1:]):
            copy(4 + j, (*chip, 1 - c), me).wait_recv()
        for cp in first + passed:
            cp.wait_send()
        mine.wait()

    return pl.pallas_call(
        body,
        name=name,
        in_specs=[ANY],
        out_specs=ANY,
        out_shape=_sds((N_DEV, rows, lanes), block.dtype),
        scratch_shapes=[pltpu.SemaphoreType.DMA((7,)), pltpu.SemaphoreType.DMA((7,)), pltpu.SemaphoreType.DMA],
    )(block)


def _reduce_scatter(parts, name):
    _, rows, lanes = parts.shape

    def body(p_ref, out_ref, own, land_a, send_b, land_b, sa, ra, sb, rb, lo):
        x, y, c, chips = _place()
        sibling = (x, y, 1 - c)

        def blk(chip, core):
            return p_ref.at[4 * chip[0] + 2 * chip[1] + core]

        to_sib = [pltpu.make_async_remote_copy(
            src_ref=blk(chips[k], 1 - c), dst_ref=land_a.at[k], send_sem=sa.at[k], recv_sem=ra.at[k],
            device_id=sibling, device_id_type=MESH) for k in range(4)]
        for cp in to_sib:
            cp.start()
        loads = [pltpu.make_async_copy(blk(chips[k], c), own.at[k], lo.at[k]) for k in range(4)]
        for cp in loads:
            cp.start()
        to_chip = [pltpu.make_async_remote_copy(
            src_ref=send_b.at[j], dst_ref=land_b.at[j], send_sem=sb.at[j], recv_sem=rb.at[j],
            device_id=(*chips[1 + j], c), device_id_type=MESH) for j in range(3)]
        for k in (1, 2, 3):
            to_sib[k].wait_recv()
            loads[k].wait()
            send_b[k - 1] = (own[k] + land_a[k]).astype(BF16)
            to_chip[k - 1].start()
        to_sib[0].wait_recv()
        loads[0].wait()
        acc = own[0] + land_a[0]
        for j in range(3):
            to_chip[j].wait_recv()
            acc = acc + land_b[j].astype(F32)
        out_ref[...] = acc
        for cp in to_sib + to_chip:
            cp.wait_send()

    return pl.pallas_call(
        body,
        name=name,
        in_specs=[ANY],
        out_specs=pl.BlockSpec(memory_space=pltpu.VMEM),
        out_shape=_sds((rows, lanes), F32),
        scratch_shapes=[pltpu.VMEM((4, rows, lanes), F32), pltpu.VMEM((4, rows, lanes), F32),
                        pltpu.VMEM((3, rows, lanes), BF16), pltpu.VMEM((3, rows, lanes), BF16)]
        + [pltpu.SemaphoreType.DMA((4,))] * 2 + [pltpu.SemaphoreType.DMA((3,))] * 2 + [pltpu.SemaphoreType.DMA((4,))],
        compiler_params=_params(),
    )(parts)


def _adamw_math(w, g, m, v):
    m = ADAM_B1 * m + (1.0 - ADAM_B1) * g
    v = ADAM_B2 * v + (1.0 - ADAM_B2) * (g * g)
    m_hat = m / (1.0 - ADAM_B1 ** ADAM_STEP)
    v_hat = v / (1.0 - ADAM_B2 ** ADAM_STEP)
    delta = -ADAM_LR * (m_hat / (jnp.sqrt(v_hat) + ADAM_EPS) + ADAM_WD * w)
    return delta, m, v


def _row_tile(r):
    return max(t for t in range(8, min(r, 256) + 1, 8) if r % t == 0)


def _adamw(w, g, m, v, name):
    r, n = w.shape

    def body(w_ref, g_ref, m_ref, v_ref, d_ref, nm_ref, nv_ref):
        d_ref[...], nm_ref[...], nv_ref[...] = _adamw_math(w_ref[...], g_ref[...], m_ref[...], v_ref[...])

    return _row_call(body, name, r, _row_tile(r), [w, g, m, v], [], [((r, n), F32)] * 3)


def _adamw_sum(landed, w, m, v, name):
    r, n = w.shape

    def body(l_ref, w_ref, m_ref, v_ref, g_ref, d_ref, nm_ref, nv_ref):
        g = l_ref[0].astype(F32)
        for dev in range(1, N_DEV):
            g = g + l_ref[dev].astype(F32)
        g_ref[...] = g
        d_ref[...], nm_ref[...], nv_ref[...] = _adamw_math(w_ref[...], g, m_ref[...], v_ref[...])

    tm = max(t for t in range(16, min(r, 256) + 1, 16) if r % t == 0)
    return _row_call(body, name, r, tm, [landed, w, m, v], [], [((r, n), F32)] * 4)


def _adamw_small(gathered, w, m, v):
    def body(ga_ref, w_ref, m_ref, v_ref, g_ref, d_ref, nm_ref, nv_ref):
        g = ga_ref[0]
        for dev in range(1, N_DEV):
            g = g + ga_ref[dev]
        g_ref[...] = g
        d_ref[...], nm_ref[...], nv_ref[...] = _adamw_math(w_ref[...], g, m_ref[...], v_ref[...])

    return pl.pallas_call(body, name="adamw_small", out_shape=[_sds(w.shape, F32)] * 4, compiler_params=_params())(
        gathered, w, m, v)


SMALL = ("norm_mix", "q_a_norm", "kv_a_norm", "q_norm", "k_norm", "ssm_a_re", "ssm_a_im", "ssm_log_dt", "ssm_b_re",
         "ssm_b_im", "ssm_c_re", "ssm_c_im", "ssm_d", "b_glu", "norm_mlp")
WEIGHT_ORDER = ("norm_mix", "w_in", "q_a_norm", "kv_a_norm", "w_q_b", "w_kv_b", "q_norm", "k_norm", "w_o_mla",
                "ssm_a_re", "ssm_a_im", "ssm_log_dt", "ssm_b_re", "ssm_b_im", "ssm_c_re", "ssm_c_im", "ssm_d", "w_glu",
                "b_glu", "w_o_ssm", "w_out", "norm_mlp", "w_up", "w_down")
IN_SHARD = D_IN // N_DEV
Q_SHARD = QK_HEAD


def _pack_small(vals):
    parts = []
    for n in SMALL:
        flat = vals[n].reshape(-1)
        size = -(-flat.shape[0] // (8 * LANES)) * 8 * LANES
        parts.append(jnp.pad(flat, (0, size - flat.shape[0])).reshape(-1, LANES))
    return jnp.concatenate(parts, axis=0)


def _unpack_small(packed, like):
    out, off = {}, 0
    for n in SMALL:
        size = like[n].size
        rows = -(-size // (8 * LANES)) * 8
        out[n] = packed[off:off + rows].reshape(-1)[:size].reshape(like[n].shape)
        off += rows
    return out


def _step(x, pos_col, target, w, small):
    bf = {n: a.astype(BF16) for n, a in w.items()}
    gq, gk = _pad_gain(small["q_norm"]), _pad_gain(small["k_norm"])
    a_re = small["ssm_a_re"].reshape(1, N_STATE)
    a_im = small["ssm_a_im"].reshape(1, N_STATE)
    log_dt = jnp.repeat(small["ssm_log_dt"].reshape(SSM_GROUPS), SSM_STATE).reshape(1, N_STATE)
    bt_re, bt_im = _swap_minor(small["ssm_b_re"]), _swap_minor(small["ssm_b_im"])
    c2_re, c2_im = _swap_minor(small["ssm_c_re"]), _swap_minor(small["ssm_c_im"])
    d_row = small["ssm_d"].reshape(1, SSM_WIDTH)

    w_in_all = _all_gather(bf["w_in"], "gather_w_in")
    w_in_pad = _pad_in(jnp.transpose(w_in_all, (1, 0, 2)).reshape(D_MODEL, D_IN))
    cos_t, sin_t = _rope_tables(pos_col)
    lam, bblk, cblk = _ssm_prep(a_re, a_im, log_dt, bt_re, bt_im, c2_re, c2_im)
    wq_mine = jnp.pad(bf["w_q_b"], ((0, 0), (0, QK_PAD - QK_HEAD)))
    xn, u, ql, kvl, gs, gm, w_glu, w_o_ssm = _in_proj(
        x, small["norm_mix"], w_in_pad, xch=[(bf["w_glu"], False), (bf["w_o_ssm"], False)])
    w_glu = w_glu.reshape(SSM_WIDTH, SSM_WIDTH)
    y, y_ssm, st, wq, wkv, w_o_mla, w_out = _ssm_fwd(
        u, bblk, cblk, lam, d_row, w_glu, small["b_glu"], w_o_ssm,
        xch=[(wq_mine, False), (bf["w_kv_b"], False), (bf["w_o_mla"], False), (bf["w_out"], False)])
    w_o_mla, w_out = w_o_mla.reshape(D_MODEL, D_MODEL), w_out.reshape(D_MODEL, D_MODEL)
    wq = jnp.transpose(wq, (1, 0, 2)).reshape(Q_LORA, N_HEADS * QK_PAD)
    wkv = jnp.transpose(wkv, (1, 0, 2)).reshape(KV_LORA, N_HEADS * QK_PAD)
    q, k, v, kt, vt = _qkv_prep(ql, kvl, small["q_a_norm"], small["kv_a_norm"], wq, wkv, gq, gk, cos_t, sin_t)
    attn, lse, w_up, w_down = _attn_fwd(q, k, vt, xch=[(bf["w_up"], False), (bf["w_down"], False)])
    h, mixed, y_mla = _merge(attn, gs, gm, y_ssm, x, w_o_mla, w_out)
    hn, dout, loss = _mlp_fwd_loss(h, target, small["norm_mlp"], w_up, w_down)

    hid, da, dh, d_norm_mlp = _mlp_bwd(dout, hn, h, small["norm_mlp"], w_up, w_down)
    p_w_down = _matmul_tn_shards(hid, dout, "dw_down", False)
    p_w_up = _matmul_tn_shards(hn, da, "dw_up", True)
    dgs, dgm, dy_ssm, dy_mla, dattn = _merge_bwd(dh, gs, gm, y_ssm, y_mla, w_out, w_o_mla)
    p_w_out = _matmul_tn_shards(mixed, dh, "dw_out", False)
    p_w_o_mla = _matmul_tn_shards(attn, dy_mla, "dw_o_mla", False)
    dq, dk, dv, l_w_up, l_w_down, l_w_out, l_w_o_mla = _attn_bwd(
        q, k, kt, v, attn, lse, dattn, xch=[(p_w_up, True), (p_w_down, True), (p_w_out, True), (p_w_o_mla, True)])
    dql, dkvl, qa, dq_pre, ca, dkv_pre, d_q_a_norm, d_kv_a_norm, d_gq, d_gk = _qkv_prep_bwd(
        ql, kvl, dq, dk, dv, small["q_a_norm"], small["kv_a_norm"], wq, wkv, gq, gk, cos_t, sin_t, xch=[])
    p_wq = _matmul_tn_shards(qa, dq_pre, "dw_q_b", True)
    p_wkv = _matmul_tn_shards(ca, dkv_pre, "dw_kv_b", True)
    dy, zg, z, dt, d_b_glu = _glu_bwd(dy_ssm, y, w_glu, small["b_glu"], w_o_ssm)
    p_w_o_ssm = _matmul_tn_shards(zg, dy_ssm, "dw_o_ssm", True)
    p_w_glu = _matmul_tn_shards(z, dt, "dw_glu", False)
    du, xs, ads, u_scan, dy_scan, dlam, d_d, l_wq, l_wkv, l_w_glu, l_w_o_ssm = _ssm_bwd(
        u, dy, st, bblk, cblk, lam, d_row, xch=[(p_wq, True), (p_wkv, True), (p_w_glu, True), (p_w_o_ssm, True)])
    d_bblk = _matmul_tn(u_scan, ads, "d_bblk")
    d_cblk_t = _matmul_tn(dy_scan, xs, "d_cblk")
    d_a_re, d_a_im, d_log_dt, d_bt_re, d_bt_im, d_c_re, d_c_im = _ssm_prep_bwd(
        a_re, a_im, log_dt, bt_re, bt_im, dlam, d_bblk, d_cblk_t)
    dx, dproj, d_norm_mix = _in_proj_bwd((du, dql, dkvl, dgs, dgm), dh, x, small["norm_mix"], w_in_pad)
    tr = lambda mat: jnp.transpose(mat.reshape(SSM_GROUPS, SSM_GROUP_CH, SSM_STATE), (0, 2, 1))
    g_small = {
        "norm_mix": d_norm_mix, "q_a_norm": d_q_a_norm, "kv_a_norm": d_kv_a_norm,
        "q_norm": d_gq[:, :QK_HEAD], "k_norm": d_gk[:, :QK_HEAD],
        "ssm_a_re": d_a_re, "ssm_a_im": d_a_im, "ssm_log_dt": d_log_dt,
        "ssm_b_re": tr(d_bt_re), "ssm_b_im": tr(d_bt_im), "ssm_c_re": d_c_re, "ssm_c_im": d_c_im,
        "ssm_d": d_d, "b_glu": d_b_glu, "norm_mlp": d_norm_mlp,
    }
    g_w_in_pad, g_small_all = _matmul_tn(xn, dproj, "dw_in", xch=[(_pack_small(g_small), False)])
    parts = jnp.transpose(_unpad_in(g_w_in_pad).reshape(D_MODEL, N_DEV, IN_SHARD), (1, 0, 2))
    g_w_in_mine = _reduce_scatter(parts, "reduce_w_in")
    landed = {"w_q_b": l_wq[:, :, :QK_HEAD], "w_kv_b": l_wkv, "w_o_mla": l_w_o_mla, "w_glu": l_w_glu,
              "w_o_ssm": l_w_o_ssm, "w_out": l_w_out, "w_up": l_w_up, "w_down": l_w_down}
    return loss, dx, landed, g_w_in_mine, g_small_all


def kernel(x, positions, norm_mix, w_in, q_a_norm, kv_a_norm, w_q_b, w_kv_b, q_norm, k_norm, w_o_mla, ssm_a_re, ssm_a_im, ssm_log_dt, ssm_b_re, ssm_b_im, ssm_c_re, ssm_c_im, ssm_d, w_glu, b_glu, w_o_ssm, w_out, norm_mlp, w_up, w_down, loss_target, m_norm_mix, m_w_in, m_q_a_norm, m_kv_a_norm, m_w_q_b, m_w_kv_b, m_q_norm, m_k_norm, m_w_o_mla, m_ssm_a_re, m_ssm_a_im, m_ssm_log_dt, m_ssm_b_re, m_ssm_b_im, m_ssm_c_re, m_ssm_c_im, m_ssm_d, m_w_glu, m_b_glu, m_w_o_ssm, m_w_out, m_norm_mlp, m_w_up, m_w_down, v_norm_mix, v_w_in, v_q_a_norm, v_kv_a_norm, v_w_q_b, v_w_kv_b, v_q_norm, v_k_norm, v_w_o_mla, v_ssm_a_re, v_ssm_a_im, v_ssm_log_dt, v_ssm_b_re, v_ssm_b_im, v_ssm_c_re, v_ssm_c_im, v_ssm_d, v_w_glu, v_b_glu, v_w_o_ssm, v_w_out, v_norm_mlp, v_w_up, v_w_down):
    given = dict(locals())
    w = {n: given[n] for n in WEIGHT_ORDER}
    m = {n: given["m_" + n] for n in WEIGHT_ORDER}
    v = {n: given["v_" + n] for n in WEIGHT_ORDER}
    big = [n for n in WEIGHT_ORDER if n not in SMALL]
    small = {n: w[n] for n in SMALL}

    loss, dx, landed, g_w_in, g_small_all = _step(
        x[0], positions.reshape(-1, 1), loss_target[0], {n: w[n][0] for n in big}, small)

    grads, deltas, new_m, new_v = {}, {}, {}, {}
    for n in big:
        if n == "w_in":
            g = g_w_in
            d, nm, nv = _adamw(w[n][0], g, m[n][0], v[n][0], "adamw_" + n)
        else:
            g, d, nm, nv = _adamw_sum(landed[n], w[n][0], m[n][0], v[n][0], "adamw_" + n)
        grads[n], deltas[n], new_m[n], new_v[n] = g[None], d[None], nm[None], nv[None]

    packed = _adamw_small(g_small_all, _pack_small(small), _pack_small({n: m[n] for n in SMALL}),
                          _pack_small({n: v[n] for n in SMALL}))
    for dst, src in zip((grads, deltas, new_m, new_v), packed):
        dst.update(_unpack_small(src, small))

    total = lax.psum(loss[0, 0], ("x", "y", "c"))
    return (total, dx[None], *[grads[n] for n in WEIGHT_ORDER], *[deltas[n] for n in WEIGHT_ORDER],
            *[new_m[n] for n in WEIGHT_ORDER], *[new_v[n] for n in WEIGHT_ORDER])
```

```python
import functools
import math

import numpy as np
import jax
import jax.numpy as jnp
from jax import lax
from jax.experimental import pallas as pl
from jax.experimental.pallas import tpu as pltpu

F32 = jnp.float32
BF16 = jnp.bfloat16

D_MODEL = 1024
SSM_GROUPS = 32
SSM_GROUP_CH = 16
SSM_WIDTH = 512
SSM_STATE = 64
N_STATE = SSM_GROUPS * SSM_STATE
N_HEADS = 8
QK_NOPE = 128
QK_ROPE = 64
QK_HEAD = 192
QK_PAD = 256
V_HEAD = 128
Q_LORA = 384
KV_LORA = 256
KV_LAT_PAD = 384
ROPE_THETA = 10000.0
D_FF = 4096
EPS = 1e-6
ATT_SCALE = QK_HEAD ** -0.5
N_DEV = 8
FF_SHARD = D_FF // N_DEV
OUT_SHARD = D_MODEL // N_DEV

IN_SEGS = ((0, 512), (512, 896), (896, 1280), (1280, 2304), (2304, 3328))
D_IN = 3264
D_IN_PAD = 3328
KV_END = 1216

ADAM_LR = 0.001
ADAM_B1 = 0.9
ADAM_B2 = 0.999
ADAM_EPS = 1e-08
ADAM_WD = 0.01
ADAM_STEP = 10

VMEM_LIMIT = 56 * 1024 * 1024
MESH = pl.DeviceIdType.MESH
ANY = pl.BlockSpec(memory_space=pl.ANY)
LANES = 128

SCAN_T = 256
SUBCHUNKS = 8
SCAN_CG = 512
ATT_T = 512
ATT_SUB = 2
ROW_T = 256
MM_T = 512


def _params(sem=None):
    return pltpu.CompilerParams(dimension_semantics=sem, vmem_limit_bytes=VMEM_LIMIT)


def _rows(arr, tm):
    if arr.ndim == 2:
        return pl.BlockSpec((tm, arr.shape[1]), lambda i: (i, 0))
    return pl.BlockSpec((arr.shape[0], tm, arr.shape[2]), lambda i: (0, i, 0))


def _const(arr):
    nd = arr.ndim
    return pl.BlockSpec(arr.shape, lambda *_: (0,) * nd, pipeline_mode=pl.Buffered(1))


def _sds(shape, dtype):
    return jax.ShapeDtypeStruct(shape, dtype)


PEERS = tuple((dx, dy, dc) for dx in (0, 1) for dy in (0, 1) for dc in (0, 1) if (dx, dy, dc) != (0, 0, 0))


def _here():
    x, y, c = lax.axis_index("x"), lax.axis_index("y"), lax.axis_index("c")
    return x, y, c, 4 * x + 2 * y + c


def _xchg_start(scatter, srcs, dsts, send, recv, local):
    x, y, c, me = _here()
    for e, sc in enumerate(scatter):
        src, dst = srcs[e], dsts[e]
        pltpu.make_async_copy(src.at[me] if sc else src, dst.at[me], local.at[e]).start()
        for dx, dy, dc in PEERS:
            px, py, pc = (1 - x if dx else x), (1 - y if dy else y), (1 - c if dc else c)
            pltpu.make_async_remote_copy(
                src_ref=src.at[4 * px + 2 * py + pc] if sc else src, dst_ref=dst.at[me],
                send_sem=send.at[e], recv_sem=recv.at[e], device_id=(px, py, pc), device_id_type=MESH).start()


def _xchg_wait(scatter, srcs, dsts, send, recv, local):
    x, y, c, me = _here()
    for e, sc in enumerate(scatter):
        src, dst = srcs[e], dsts[e]
        pltpu.make_async_copy(src.at[me] if sc else src, dst.at[me], local.at[e]).wait()
        span = dst.at[pl.ds(0, N_DEV - 1)]
        both = pltpu.make_async_remote_copy(src_ref=span, dst_ref=span, send_sem=send.at[e], recv_sem=recv.at[e],
                                            device_id=(x, y, c), device_id_type=MESH)
        both.wait_send()
        both.wait_recv()


def _call(body, name, grid, ins, in_specs, outs, out_specs, scratch=(), xch=()):
    n_in, n_out, ne = len(ins), len(outs), len(xch)
    scatter = [sc for _, sc in xch]
    x_outs = [_sds((N_DEV,) + (a.shape[1:] if sc else a.shape), a.dtype) for a, sc in xch]
    sems = [pltpu.SemaphoreType.DMA((ne,))] * 3 if ne else []

    def wrapped(*refs):
        in_refs, x_src = refs[:n_in], refs[n_in:n_in + ne]
        out_refs = refs[n_in + ne:n_in + ne + n_out]
        x_dst = refs[n_in + ne + n_out:n_in + 2 * ne + n_out]
        rest = refs[n_in + 2 * ne + n_out:]
        if ne:
            x_sems, rest = rest[len(rest) - 3:], rest[:len(rest) - 3]
            first = functools.reduce(jnp.logical_and, [pl.program_id(d) == 0 for d in range(len(grid))])
            last = functools.reduce(jnp.logical_and, [pl.program_id(d) == grid[d] - 1 for d in range(len(grid))])

            @pl.when(first)
            def _():
                _xchg_start(scatter, x_src, x_dst, *x_sems)

        body(*in_refs, *out_refs, *rest)
        if ne:
            @pl.when(last)
            def _():
                _xchg_wait(scatter, x_src, x_dst, *x_sems)

    return pl.pallas_call(
        wrapped,
        name=name,
        grid=grid,
        in_specs=list(in_specs) + [ANY] * ne,
        out_specs=list(out_specs) + [ANY] * ne,
        out_shape=list(outs) + x_outs,
        scratch_shapes=list(scratch) + sems,
        compiler_params=_params(("arbitrary",) * len(grid)),
    )(*ins, *[a for a, _ in xch])


def _row_call(body, name, n_rows, tm, row_ins, const_ins, row_outs, acc_outs=(), xch=()):
    outs = [_sds(s, d) for s, d in row_outs] + [_sds(s, d) for s, d in acc_outs]
    out_specs = [_rows(o, tm) for o in outs[: len(row_outs)]] + [
        pl.BlockSpec(o.shape, lambda i, nd=len(o.shape): (0,) * nd) for o in outs[len(row_outs):]]
    in_specs = [_rows(a, tm) for a in row_ins] + [_const(a) for a in const_ins]
    return _call(body, name, (n_rows // tm,), list(row_ins) + list(const_ins), in_specs, outs, out_specs, xch=xch)


def _dot(a, b):
    return jnp.dot(a, b, preferred_element_type=F32)


def _dot_nt(a, b):
    return lax.dot_general(a, b, (((1,), (1,)), ((), ())), preferred_element_type=F32)


def _dot_tn(a, b):
    return lax.dot_general(a, b, (((0,), (0,)), ((), ())), preferred_element_type=F32)


def _rms(x, g, n):
    inv = lax.rsqrt(jnp.sum(x * x, -1, keepdims=True) * (1.0 / n) + EPS)
    return x * inv * g, inv


def _rms_bwd(dy, x, g, inv, n):
    xh = x * inv
    dxh = dy * g
    dx = inv * (dxh - xh * (jnp.sum(dxh * xh, -1, keepdims=True) * (1.0 / n)))
    return dx, dy * xh


def _sigmoid(x):
    return 1.0 / (1.0 + jnp.exp(-x))


_GELU_C = math.sqrt(2.0 / math.pi)


def _gelu(y):
    th = jnp.tanh(_GELU_C * (y + 0.044715 * (y * y * y)))
    return 0.5 * y * (1.0 + th), th


def _gelu_grad(y, th):
    return 0.5 * (1.0 + th) + 0.5 * y * (1.0 - th * th) * (_GELU_C * (1.0 + 3.0 * 0.044715 * (y * y)))


def _acc(ref, val):
    @pl.when(pl.program_id(0) == 0)
    def _():
        ref[...] = jnp.zeros_like(ref)

    ref[...] += val


def _tile(n, limit):
    if n <= limit:
        return n
    return max(t for t in range(128, limit + 1, 128) if n % t == 0)


def _matmul_tn(a, b, name, tm=512, tk=512, xch=()):
    k_dim, m = a.shape
    n = b.shape[1]
    tm, tk = _tile(m, tm), _tile(k_dim, tk)

    def body(a_ref, b_ref, o_ref):
        @pl.when(pl.program_id(1) == 0)
        def _():
            o_ref[...] = jnp.zeros_like(o_ref)

        o_ref[...] += _dot_tn(a_ref[...].astype(BF16), b_ref[...].astype(BF16))

    outs = _call(
        body, name, (m // tm, k_dim // tk), [a, b],
        [pl.BlockSpec((tk, tm), lambda i, k: (k, i)), pl.BlockSpec((tk, n), lambda i, k: (k, 0))],
        [_sds((m, n), F32)], [pl.BlockSpec((tm, n), lambda i, k: (i, 0))], xch=xch)
    return outs if xch else outs[0]


def _matmul_tn_shards(a, b, name, by_col, tm=512, tk=512):
    k_dim, m = a.shape
    n = b.shape[1]
    tm, tk = _tile(m, tm), _tile(k_dim, tk)
    nk = k_dim // tk
    if by_col:
        r, c = m, n // N_DEV
        out_spec = pl.BlockSpec((N_DEV, tm, c), lambda i, k: (0, i, 0))
    else:
        r, c = m // N_DEV, n
        per = tm // r
        out_spec = pl.BlockSpec((per, r, c), lambda i, k: (i, 0, 0))

    def body(a_ref, b_ref, o_ref, acc_ref):
        k = pl.program_id(1)

        @pl.when(k == 0)
        def _():
            acc_ref[...] = jnp.zeros_like(acc_ref)

        acc_ref[...] += _dot_tn(a_ref[...].astype(BF16), b_ref[...].astype(BF16))

        @pl.when(k == nk - 1)
        def _():
            if by_col:
                for j in range(N_DEV):
                    o_ref[j] = acc_ref[:, j * c:(j + 1) * c].astype(BF16)
            else:
                for s in range(per):
                    o_ref[s] = acc_ref[s * r:(s + 1) * r, :].astype(BF16)

    return pl.pallas_call(
        body,
        name=name,
        grid=(m // tm, nk),
        in_specs=[pl.BlockSpec((tk, tm), lambda i, k: (k, i)), pl.BlockSpec((tk, n), lambda i, k: (k, 0))],
        out_specs=out_spec,
        out_shape=_sds((N_DEV, r, c), BF16),
        scratch_shapes=[pltpu.VMEM((tm, n), F32)],
        compiler_params=_params(("parallel", "arbitrary")),
    )(a, b)


def _rope_tables(pos_col):
    n = pos_col.shape[0]
    half = QK_ROPE // 2
    inv_freq = (ROPE_THETA ** (-np.arange(half, dtype=np.float32) / half)).astype(np.float32)
    freq_row = jnp.asarray(np.concatenate([inv_freq, inv_freq, np.zeros(64, np.float32)])[None, :])

    def body(p_ref, f_ref, c_ref, s_ref):
        ang = p_ref[...].astype(F32) * f_ref[...]
        c_ref[...] = jnp.cos(ang)
        s_ref[...] = jnp.sin(ang)

    return _row_call(body, "rope_tables", n, min(n, 1024), [pos_col], [freq_row], [((n, 128), F32)] * 2)


def _rope_rot(v):
    lane = lax.broadcasted_iota(jnp.int32, v.shape, 1)
    return jnp.where(lane < 32, -pltpu.roll(v, 96, 1), jnp.where(lane < 64, pltpu.roll(v, 32, 1), 0.0))


def _rope_rot_t(v):
    lane = lax.broadcasted_iota(jnp.int32, v.shape, 1)
    return jnp.where(lane < 32, pltpu.roll(v, 96, 1), jnp.where(lane < 64, -pltpu.roll(v, 32, 1), 0.0))


def _in_proj(x, norm_mix, w_in_pad, xch):
    n = x.shape[0]

    def body(x_ref, g_ref, w_ref, xn_ref, u_ref, ql_ref, kvl_ref, gs_ref, gm_ref):
        xn, _ = _rms(x_ref[...], g_ref[...], D_MODEL)
        xb = xn.astype(BF16)
        xn_ref[...] = xb
        for ref, (a, b) in zip((u_ref, ql_ref, kvl_ref, gs_ref, gm_ref), IN_SEGS):
            ref[...] = _dot(xb, w_ref[:, a:b])

    outs = [((n, D_MODEL), BF16)] + [((n, b - a), F32) for a, b in IN_SEGS]
    return _row_call(body, "in_proj", n, MM_T, [x], [norm_mix, w_in_pad], outs, xch=xch)


def _ssm_prep_fn(a_re, a_im, log_dt, b_re_x, b_im_x):
    dt = jnp.exp(log_dt)
    mag = jnp.exp(a_re * dt)
    lr = mag * jnp.cos(a_im * dt)
    li = mag * jnp.sin(a_im * dt)
    den = a_re * a_re + a_im * a_im
    fr = ((lr - 1.0) * a_re + li * a_im) / den
    fi = (li * a_re - (lr - 1.0) * a_im) / den
    return lr, li, fr * b_re_x - fi * b_im_x, fr * b_im_x + fi * b_re_x


def _dot_exact(a, b, dims):
    return lax.dot_general(a, b, (dims, ((), ())), precision=lax.Precision.HIGHEST, preferred_element_type=F32)


def _lane_repeat(width, n):
    src = lax.broadcasted_iota(jnp.int32, (width, n), 0)
    dst = lax.broadcasted_iota(jnp.int32, (width, n), 1)
    return (dst % width == src).astype(F32)


def _same_group(rows, rows_per_group, cols, cols_per_group):
    row = lax.broadcasted_iota(jnp.int32, (rows, cols), 0)
    col = lax.broadcasted_iota(jnp.int32, (rows, cols), 1)
    return (row // rows_per_group) == (col // cols_per_group)


def _expand_b(bt):
    tiled = _dot_exact(bt, _lane_repeat(SSM_STATE, N_STATE), ((1,), (0,)))
    return jnp.where(_same_group(SSM_WIDTH, SSM_GROUP_CH, N_STATE, SSM_STATE), tiled, 0.0)


def _collect_b(m):
    masked = jnp.where(_same_group(SSM_WIDTH, SSM_GROUP_CH, N_STATE, SSM_STATE), m, 0.0)
    return _dot_exact(masked, _lane_repeat(SSM_STATE, N_STATE), ((1,), (1,)))


def _ssm_prep(a_re, a_im, log_dt, bt_re, bt_im, c2_re, c2_im):
    def body(ar, ai, ld, br, bi, cr, ci, lam_ref, bblk_ref, cblk_ref):
        lr, li, bbr, bbi = _ssm_prep_fn(ar[...], ai[...], ld[...], _expand_b(br[...]), _expand_b(bi[...]))
        lam_ref[0:1, :] = lr
        lam_ref[1:2, :] = li
        bblk_ref[:, 0:N_STATE] = bbr.astype(BF16)
        bblk_ref[:, N_STATE:] = bbi.astype(BF16)
        rep = _lane_repeat(SSM_GROUP_CH, SSM_WIDTH)
        own = _same_group(N_STATE, SSM_STATE, SSM_WIDTH, SSM_GROUP_CH)
        cblk_ref[0:N_STATE, :] = jnp.where(own, _dot_exact(cr[...], rep, ((1,), (0,))), 0.0).astype(BF16)
        cblk_ref[N_STATE:, :] = jnp.where(own, -_dot_exact(ci[...], rep, ((1,), (0,))), 0.0).astype(BF16)

    return pl.pallas_call(
        body,
        name="ssm_prep",
        out_shape=[_sds((2, N_STATE), F32), _sds((SSM_WIDTH, 2 * N_STATE), BF16),
                   _sds((2 * N_STATE, SSM_WIDTH), BF16)],
        compiler_params=_params(),
    )(a_re, a_im, log_dt, bt_re, bt_im, c2_re, c2_im)


def _ssm_prep_bwd(a_re, a_im, log_dt, bt_re, bt_im, dlam, dbblk, dcblk_t):
    def body(ar, ai, ld, br, bi, dl, db, dc, dar, dai, dld, dbr, dbi, dcr, dci):
        _, vjp = jax.vjp(_ssm_prep_fn, ar[...], ai[...], ld[...], _expand_b(br[...]), _expand_b(bi[...]))
        g = vjp((dl[0:1, :], dl[1:2, :], db[:, 0:N_STATE], db[:, N_STATE:]))
        dar[...] = g[0]
        dai[...] = g[1]
        grp = lax.broadcasted_iota(jnp.int32, (SSM_GROUPS, N_STATE), 0)
        lane = lax.broadcasted_iota(jnp.int32, (SSM_GROUPS, N_STATE), 1)
        sel = (lane // SSM_STATE) == grp
        dld[...] = jnp.sum(jnp.where(sel, jnp.broadcast_to(g[2], (SSM_GROUPS, N_STATE)), 0.0), axis=1, keepdims=True)
        dbr[...] = _collect_b(g[3])
        dbi[...] = _collect_b(g[4])
        dcr[...] = _collect_b(dc[:, 0:N_STATE])
        dci[...] = -_collect_b(dc[:, N_STATE:])

    small = _sds((SSM_WIDTH, SSM_STATE), F32)
    return pl.pallas_call(
        body,
        name="ssm_prep_bwd",
        out_shape=[_sds((1, N_STATE), F32), _sds((1, N_STATE), F32), _sds((SSM_GROUPS, 1), F32), small, small, small, small],
        compiler_params=_params(),
    )(a_re, a_im, log_dt, bt_re, bt_im, dlam, dbblk, dcblk_t)


def _perm_matrix(t):
    run = t // SUBCHUNKS
    p = np.zeros((t, t), np.float32)
    r = np.arange(t)
    p[r, (r % SUBCHUNKS) * run + r // SUBCHUNKS] = 1.0
    return jnp.asarray(p, dtype=BF16)


def _unpermute(p, a):
    hi = a.astype(BF16)
    r1 = a - hi.astype(F32)
    mid = r1.astype(BF16)
    lo = (r1 - mid.astype(F32)).astype(BF16)
    return _dot_tn(p, hi) + _dot_tn(p, mid) + _dot_tn(p, lo)


def _power_table(lam_ref, pw_ref, n):
    lr, li = lam_ref[0:1, :], lam_ref[1:2, :]
    pw_ref[0:1, 0:N_STATE] = lr
    pw_ref[0:1, N_STATE:] = li

    def step(i, carry):
        pr, pi = carry
        pr, pi = pr * lr - pi * li, pr * li + pi * lr
        pw_ref[pl.ds(i, 1), 0:N_STATE] = pr
        pw_ref[pl.ds(i, 1), N_STATE:] = pi
        return pr, pi

    lax.fori_loop(1, n, step, (lr, li))


def _col_groups():
    return [(pl.ds(c, SCAN_CG), pl.ds(N_STATE + c, SCAN_CG)) for c in range(0, N_STATE, SCAN_CG)]


def _run_scan(buf, lam_ref, t, reverse):
    nblk = t // 8
    for re, im in _col_groups():
        lr = jnp.broadcast_to(lam_ref[0:1, re], (8, SCAN_CG))
        li = jnp.broadcast_to(lam_ref[1:2, re], (8, SCAN_CG))
        if reverse:
            li = -li
        first = pl.ds((nblk - 1) * 8 if reverse else 0, 8)

        def step(k, carry, re=re, im=im, lr=lr, li=li):
            pr, pi = carry
            i = (nblk - 2 - k) if reverse else (k + 1)
            r = pl.ds(pl.multiple_of(i * 8, 8), 8)
            xr = buf[r, re] + lr * pr - li * pi
            xi = buf[r, im] + lr * pi + li * pr
            buf[r, re] = xr
            buf[r, im] = xi
            return xr, xi

        lax.fori_loop(0, nblk - 1, step, (buf[first, re], buf[first, im]))


def _run_carries(buf, pw_ref, carry_ref, s_ref, t, reverse):
    nblk = t // 8
    run = t // SUBCHUNKS
    edge = buf[pl.ds(0 if reverse else (nblk - 1) * 8, 8), :]
    pr, pi = pw_ref[run - 1:run, 0:N_STATE], pw_ref[run - 1:run, N_STATE:]
    if reverse:
        pi = -pi
    sr, si = carry_ref[0:1, 0:N_STATE], carry_ref[0:1, N_STATE:]
    for s in (range(SUBCHUNKS - 1, -1, -1) if reverse else range(SUBCHUNKS)):
        s_ref[s:s + 1, 0:N_STATE] = sr
        s_ref[s:s + 1, N_STATE:] = si
        er, ei = edge[s:s + 1, 0:N_STATE], edge[s:s + 1, N_STATE:]
        sr, si = er + pr * sr - pi * si, ei + pr * si + pi * sr
    carry_ref[:, 0:N_STATE] = jnp.broadcast_to(sr, (8, N_STATE))
    carry_ref[:, N_STATE:] = jnp.broadcast_to(si, (8, N_STATE))


def _run_fix(buf, pw_ref, s_ref, t, reverse):
    nblk = t // 8
    for re, im in _col_groups():
        sr, si = s_ref[:, re], s_ref[:, im]

        def step(i, carry, re=re, im=im, sr=sr, si=si):
            r = pl.ds(pl.multiple_of(i * 8, 8), 8)
            row = pl.ds((nblk - 1 - i) if reverse else i, 1)
            pr, pi = pw_ref[row, re], pw_ref[row, im]
            if reverse:
                pi = -pi
            buf[r, re] += pr * sr - pi * si
            buf[r, im] += pr * si + pi * sr
            return carry

        lax.fori_loop(0, nblk, step, 0)


def _ssm_fwd(u, bblk, cblk, lam, d_row, w_glu, b_glu, w_o_ssm, xch):
    n = u.shape[0]
    t = min(SCAN_T, n)
    kb = 512
    perm = _perm_matrix(t)

    def body(u_ref, p_ref, bblk_ref, cblk_ref, lam_ref, d_ref, wg_ref, bg_ref, wo_ref, y_ref, ys_ref, st_ref,
             buf, pw_ref, carry_ref, s_ref):
        @pl.when(pl.program_id(0) == 0)
        def _():
            carry_ref[...] = jnp.zeros_like(carry_ref)
            _power_table(lam_ref, pw_ref, t // SUBCHUNKS)

        st_ref[0] = carry_ref[...]
        u_t = u_ref[...]
        p = p_ref[...]
        ub = _dot(p, u_t.astype(BF16)).astype(BF16)
        for c in range(0, 2 * N_STATE, kb):
            buf[:, c:c + kb] = _dot(ub, bblk_ref[:, c:c + kb])
        _run_scan(buf, lam_ref, t, False)
        _run_carries(buf, pw_ref, carry_ref, s_ref, t, False)
        _run_fix(buf, pw_ref, s_ref, t, False)
        yp = jnp.zeros((t, SSM_WIDTH), F32)
        for c in range(0, 2 * N_STATE, kb):
            yp += _dot(buf[:, c:c + kb].astype(BF16), cblk_ref[c:c + kb, :])
        y = d_ref[...] * u_t + _unpermute(p, yp)
        y_ref[...] = y
        z, _ = _gelu(y)
        s = _sigmoid(_dot(z.astype(BF16), wg_ref[...]) + bg_ref[...])
        zgb = (z * s).astype(BF16)
        for j in range(N_DEV):
            ys_ref[:, j * OUT_SHARD:(j + 1) * OUT_SHARD] = _dot(zgb, wo_ref[j])

    consts = [perm, bblk, cblk, lam, d_row, w_glu, b_glu, w_o_ssm]
    return _call(
        body, "ssm_fwd", (n // t,), [u] + consts, [_rows(u, t)] + [_const(a) for a in consts],
        [_sds((n, SSM_WIDTH), F32), _sds((n, D_MODEL), F32), _sds((n // t, 8, 2 * N_STATE), F32)],
        [pl.BlockSpec((t, SSM_WIDTH), lambda i: (i, 0)), pl.BlockSpec((t, D_MODEL), lambda i: (i, 0)),
         pl.BlockSpec((1, 8, 2 * N_STATE), lambda i: (i, 0, 0))],
        scratch=[pltpu.VMEM((t, 2 * N_STATE), F32), pltpu.VMEM((t // SUBCHUNKS, 2 * N_STATE), F32),
                 pltpu.VMEM((8, 2 * N_STATE), F32), pltpu.VMEM((8, 2 * N_STATE), F32)],
        xch=xch)


def _head_norm_rope(slab, gain, cos_t, sin_t):
    xn, inv = _rms(slab, gain, QK_HEAD)
    lo, hi = xn[:, 0:128], xn[:, 128:256]
    return jnp.concatenate([lo, hi * cos_t + _rope_rot(hi) * sin_t], axis=-1), inv


def _head_norm_rope_bwd(g, slab, gain, inv, cos_t, sin_t):
    g_lo, g_hi = g[:, 0:128], g[:, 128:256]
    g_n = jnp.concatenate([g_lo, g_hi * cos_t + _rope_rot_t(g_hi * sin_t)], axis=-1)
    return _rms_bwd(g_n, slab, gain, inv, QK_HEAD)


def _qkv_prep(ql, kvl, q_a_norm, kv_a_norm, wq, wkv, gq, gk, cos_t, sin_t):
    n = ql.shape[0]
    tm = ROW_T

    def body(ql_ref, kvl_ref, cos_ref, sin_ref, qa_ref, ka_ref, wq_ref, wkv_ref, gq_ref, gk_ref,
             q_ref, k_ref, v_ref, kt_ref, vt_ref):
        cos_t, sin_t = cos_ref[...], sin_ref[...]
        qa, _ = _rms(ql_ref[...], qa_ref[...], Q_LORA)
        qab = qa.astype(BF16)
        kvl_t = kvl_ref[...]
        ca, _ = _rms(kvl_t[:, 0:KV_LORA], ka_ref[...], KV_LORA)
        cab = ca.astype(BF16)
        kpe = kvl_t[:, KV_LORA:KV_LAT_PAD]
        q_pre = _dot(qab, wq_ref[...])
        kv_pre = _dot(cab, wkv_ref[...])
        for h in range(N_HEADS):
            qh, _ = _head_norm_rope(q_pre[:, h * QK_PAD:(h + 1) * QK_PAD], gq_ref[...], cos_t, sin_t)
            q_ref[h] = (qh * ATT_SCALE).astype(BF16)
            kv_h = kv_pre[:, h * QK_PAD:(h + 1) * QK_PAD]
            kh, _ = _head_norm_rope(jnp.concatenate([kv_h[:, 0:QK_NOPE], kpe], axis=-1), gk_ref[...], cos_t, sin_t)
            k_ref[h] = kh.astype(BF16)
            kt_ref[h] = kh.T.astype(BF16)
            vh = kv_h[:, QK_NOPE:]
            v_ref[h] = vh.astype(BF16)
            vt_ref[h] = vh.T.astype(BF16)

    row_ins, consts = [ql, kvl, cos_t, sin_t], [q_a_norm, kv_a_norm, wq, wkv, gq, gk]
    outs = [_sds((N_HEADS, n, QK_PAD), BF16), _sds((N_HEADS, n, QK_PAD), BF16), _sds((N_HEADS, n, V_HEAD), BF16),
            _sds((N_HEADS, QK_PAD, n), BF16), _sds((N_HEADS, V_HEAD, n), BF16)]
    out_specs = [_rows(o, tm) for o in outs[:3]] + [
        pl.BlockSpec((N_HEADS, QK_PAD, tm), lambda i: (0, 0, i)), pl.BlockSpec((N_HEADS, V_HEAD, tm), lambda i: (0, 0, i))]
    return _call(body, "qkv_prep", (n // tm,), row_ins + consts,
                 [_rows(a, tm) for a in row_ins] + [_const(a) for a in consts], outs, out_specs)


def _causal_mask_t(st, t):
    key = lax.broadcasted_iota(jnp.int32, (t, t), 0)
    qry = lax.broadcasted_iota(jnp.int32, (t, t), 1)
    return jnp.where(key <= qry, st, -jnp.inf)


def _attn_fwd(q, k, vt, xch):
    n = q.shape[1]
    t = min(ATT_T, n)

    def body(q_ref, k_ref, vt_ref, o_ref, lse_ref):
        i = pl.program_id(1)
        qt = q_ref[0]

        def kv_tile(j, carry, diag):
            m, l, acc = carry
            ts = t // ATT_SUB
            sts = []
            for a in range(ATT_SUB):
                r0 = pl.multiple_of(j * t + a * ts, ts)
                st = _dot_nt(k_ref[0, pl.ds(r0, ts), :], qt)
                if diag:
                    key = lax.broadcasted_iota(jnp.int32, (ts, t), 0) + a * ts
                    qry = lax.broadcasted_iota(jnp.int32, (ts, t), 1)
                    st = jnp.where(key <= qry, st, -jnp.inf)
                sts.append(st)
            for a, st in enumerate(sts):
                r0 = pl.multiple_of(j * t + a * ts, ts)
                m_new = jnp.maximum(m, jnp.max(st, 0, keepdims=True))
                alpha = jnp.exp(m - m_new)
                pt = jnp.exp(st - m_new)
                l = alpha * l + jnp.sum(pt, 0, keepdims=True)
                acc = alpha * acc + _dot(vt_ref[0, :, pl.ds(r0, ts)], pt.astype(BF16))
                m = m_new
            return m, l, acc

        init = (jnp.full((1, t), -jnp.inf, F32), jnp.zeros((1, t), F32), jnp.zeros((V_HEAD, t), F32))
        carry = lax.fori_loop(0, i, functools.partial(kv_tile, diag=False), init)
        m, l, acc = kv_tile(i, carry, True)
        o_ref[...] = (acc / l).T
        lse_ref[0] = m + jnp.log(l)

    return _call(
        body, "attn_fwd", (N_HEADS, n // t), [q, k, vt],
        [pl.BlockSpec((1, t, QK_PAD), lambda h, i: (h, i, 0)), pl.BlockSpec((1, n, QK_PAD), lambda h, i: (h, 0, 0)),
         pl.BlockSpec((1, V_HEAD, n), lambda h, i: (h, 0, 0))],
        [_sds((n, N_HEADS * V_HEAD), F32), _sds((N_HEADS, 1, n), F32)],
        [pl.BlockSpec((t, V_HEAD), lambda h, i: (i, h)), pl.BlockSpec((1, 1, t), lambda h, i: (h, 0, i))],
        xch=xch)


def _merge(attn, gs, gm, y_ssm, x, w_o_mla, w_out):
    n = x.shape[0]

    def body(at_ref, gs_ref, gm_ref, ys_ref, x_ref, wo_ref, wout_ref, h_ref, mx_ref, ym_ref):
        y_mla = _dot(at_ref[...].astype(BF16), wo_ref[...])
        ym_ref[...] = y_mla
        mixed = (_sigmoid(gs_ref[...]) * ys_ref[...] + _sigmoid(gm_ref[...]) * y_mla).astype(BF16)
        mx_ref[...] = mixed
        h_ref[...] = x_ref[...] + _dot(mixed, wout_ref[...])

    outs = [((n, D_MODEL), F32), ((n, D_MODEL), BF16), ((n, D_MODEL), F32)]
    return _row_call(body, "merge", n, MM_T, [attn, gs, gm, y_ssm, x], [w_o_mla, w_out], outs)


def _mlp_fwd_loss(h, target, norm_mlp, w_up, w_down):
    n = h.shape[0]

    def body(h_ref, t_ref, g_ref, wu_ref, wd_ref, hn_ref, do_ref, loss_ref):
        h_t = h_ref[...]
        hn, _ = _rms(h_t, g_ref[...], D_MODEL)
        hb = hn.astype(BF16)
        hn_ref[...] = hb
        out = h_t
        for j in range(N_DEV):
            a = jnp.maximum(_dot(hb, wu_ref[j]), 0.0)
            out += _dot((a * a).astype(BF16), wd_ref[j])
        err = out - t_ref[...]
        do_ref[...] = err * (1.0 / D_MODEL)
        _acc(loss_ref, jnp.broadcast_to(jnp.sum(err * err) * (0.5 / D_MODEL), loss_ref.shape))

    outs = [((n, D_MODEL), BF16), ((n, D_MODEL), F32)]
    return _row_call(body, "mlp_fwd_loss", n, MM_T, [h, target], [norm_mlp, w_up, w_down], outs, [((8, 128), F32)])


def _mlp_bwd(dout, hn, h, norm_mlp, w_up, w_down):
    n = h.shape[0]

    def body(do_ref, hn_ref, h_ref, g_ref, wu_ref, wd_ref, hid_ref, da_ref, dh_ref, dg_ref):
        dout_t = do_ref[...]
        doutb = dout_t.astype(BF16)
        hb = hn_ref[...]
        dhn = jnp.zeros_like(dout_t)
        for j in range(N_DEV):
            cols = slice(j * FF_SHARD, (j + 1) * FF_SHARD)
            a = jnp.maximum(_dot(hb, wu_ref[j]), 0.0)
            hid_ref[:, cols] = (a * a).astype(BF16)
            da = (_dot_nt(doutb, wd_ref[j]) * (2.0 * a)).astype(BF16)
            da_ref[:, cols] = da
            dhn += _dot_nt(da, wu_ref[j])
        h_t = h_ref[...]
        inv = lax.rsqrt(jnp.sum(h_t * h_t, -1, keepdims=True) * (1.0 / D_MODEL) + EPS)
        dx, dg = _rms_bwd(dhn, h_t, g_ref[...], inv, D_MODEL)
        dh_ref[...] = dout_t + dx
        _acc(dg_ref, jnp.sum(dg, 0, keepdims=True))

    outs = [((n, D_FF), BF16), ((n, D_FF), BF16), ((n, D_MODEL), F32)]
    return _row_call(body, "mlp_bwd", n, MM_T, [dout, hn, h], [norm_mlp, w_up, w_down], outs, [((1, D_MODEL), F32)])


def _merge_bwd(dh, gs, gm, y_ssm, y_mla, w_out, w_o_mla):
    n = dh.shape[0]

    def body(dh_ref, gs_ref, gm_ref, ys_ref, ym_ref, wout_ref, wo_ref, dgs_ref, dgm_ref, dys_ref, dym_ref, dat_ref):
        dmix = _dot_nt(dh_ref[...].astype(BF16), wout_ref[...])
        sgs, sgm = _sigmoid(gs_ref[...]), _sigmoid(gm_ref[...])
        dgs_ref[...] = (dmix * ys_ref[...] * sgs * (1.0 - sgs)).astype(BF16)
        dgm_ref[...] = (dmix * ym_ref[...] * sgm * (1.0 - sgm)).astype(BF16)
        dys_ref[...] = (dmix * sgs).astype(BF16)
        dym = (dmix * sgm).astype(BF16)
        dym_ref[...] = dym
        dat_ref[...] = _dot_nt(dym, wo_ref[...])

    outs = [((n, D_MODEL), BF16)] * 4 + [((n, D_MODEL), F32)]
    return _row_call(body, "merge_bwd", n, MM_T, [dh, gs, gm, y_ssm, y_mla], [w_out, w_o_mla], outs)


def _attn_bwd(q, k, kt, v, out, lse, dout, xch):
    n = q.shape[1]
    t = min(ATT_T, n)
    nt = n // t

    def body(q_ref, k_ref, kt_ref, v_ref, o_ref, lse_ref, do_ref, dq_ref, dk_ref, dv_ref, delta_ref, dqt_ref):
        j = pl.program_id(1)

        @pl.when(j == 0)
        def _():
            dqt_ref[...] = jnp.zeros_like(dqt_ref)
            prod = do_ref[...] * o_ref[...]
            delta_ref[...] = lax.dot_general(jnp.ones((8, V_HEAD), F32), prod, (((1,), (1,)), ((), ())),
                                             precision=lax.Precision.HIGHEST, preferred_element_type=F32)

        k_t = k_ref[0]
        kt_t = kt_ref[0]
        v_t = v_ref[0]

        def q_tile(i, carry, diag):
            dk, dv = carry
            r0 = pl.multiple_of(i * t, t)
            rows = pl.ds(r0, t)
            qt = q_ref[0, rows, :]
            st = _dot_nt(k_t, qt)
            if diag:
                st = _causal_mask_t(st, t)
            pt = jnp.exp(st - lse_ref[0, :, rows])
            dob = do_ref[rows, :].astype(BF16)
            dv = dv + _dot(pt.astype(BF16), dob)
            dst = (pt * (_dot_nt(v_t, dob) - delta_ref[0:1, rows])).astype(BF16)
            dk = dk + _dot(dst, qt)
            dqt_ref[:, rows] += _dot(kt_t, dst)
            return dk, dv

        carry = q_tile(j, (jnp.zeros((t, QK_PAD), F32), jnp.zeros((t, V_HEAD), F32)), True)
        dk, dv = lax.fori_loop(j + 1, nt, functools.partial(q_tile, diag=False), carry)
        dk_ref[0] = dk
        dv_ref[0] = dv

        @pl.when(j == nt - 1)
        def _():
            for c in range(0, n, t):
                dq_ref[0, c:c + t, :] = dqt_ref[:, c:c + t].T

    return _call(
        body, "attn_bwd", (N_HEADS, nt), [q, k, kt, v, out, lse, dout],
        [pl.BlockSpec((1, n, QK_PAD), lambda h, j: (h, 0, 0)), pl.BlockSpec((1, t, QK_PAD), lambda h, j: (h, j, 0)),
         pl.BlockSpec((1, QK_PAD, t), lambda h, j: (h, 0, j)), pl.BlockSpec((1, t, V_HEAD), lambda h, j: (h, j, 0)),
         pl.BlockSpec((n, V_HEAD), lambda h, j: (0, h)), pl.BlockSpec((1, 1, n), lambda h, j: (h, 0, 0)),
         pl.BlockSpec((n, V_HEAD), lambda h, j: (0, h))],
        [_sds((N_HEADS, n, QK_PAD), F32), _sds((N_HEADS, n, QK_PAD), F32), _sds((N_HEADS, n, V_HEAD), F32)],
        [pl.BlockSpec((1, n, QK_PAD), lambda h, j: (h, 0, 0)), pl.BlockSpec((1, t, QK_PAD), lambda h, j: (h, j, 0)),
         pl.BlockSpec((1, t, V_HEAD), lambda h, j: (h, j, 0))],
        scratch=[pltpu.VMEM((8, n), F32), pltpu.VMEM((QK_PAD, n), F32)],
        xch=xch)


def _qkv_prep_bwd(ql, kvl, dq, dk, dv, q_a_norm, kv_a_norm, wq, wkv, gq, gk, cos_t, sin_t, xch):
    n = ql.shape[0]

    def body(ql_ref, kvl_ref, cos_ref, sin_ref, dq_ref, dk_ref, dv_ref, qa_ref, ka_ref, wq_ref, wkv_ref, gq_ref, gk_ref,
             dql_ref, dkvl_ref, qab_ref, dqp_ref, cab_ref, dkvp_ref, dqa_ref, dka_ref, dgq_ref, dgk_ref):
        cos_t, sin_t = cos_ref[...], sin_ref[...]
        ql_t = ql_ref[...]
        qa, inv_qa = _rms(ql_t, qa_ref[...], Q_LORA)
        qab = qa.astype(BF16)
        qab_ref[...] = qab
        kvl_t = kvl_ref[...]
        ckv = kvl_t[:, 0:KV_LORA]
        ca, inv_ca = _rms(ckv, ka_ref[...], KV_LORA)
        cab = ca.astype(BF16)
        cab_ref[...] = cab
        kpe = kvl_t[:, KV_LORA:KV_LAT_PAD]
        dgq = jnp.zeros((1, QK_PAD), F32)
        dgk = jnp.zeros((1, QK_PAD), F32)
        dkpe = jnp.zeros_like(kpe)
        q_pre = _dot(qab, wq_ref[...])
        kv_pre = _dot(cab, wkv_ref[...])
        for h in range(N_HEADS):
            head = slice(h * QK_PAD, (h + 1) * QK_PAD)
            q_slab = q_pre[:, head]
            inv = lax.rsqrt(jnp.sum(q_slab * q_slab, -1, keepdims=True) * (1.0 / QK_HEAD) + EPS)
            d_slab, dg = _head_norm_rope_bwd(dq_ref[h] * ATT_SCALE, q_slab, gq_ref[...], inv, cos_t, sin_t)
            dqp_ref[:, head] = d_slab.astype(BF16)
            dgq += jnp.sum(dg, 0, keepdims=True)
            k_slab = jnp.concatenate([kv_pre[:, h * QK_PAD:h * QK_PAD + QK_NOPE], kpe], axis=-1)
            inv = lax.rsqrt(jnp.sum(k_slab * k_slab, -1, keepdims=True) * (1.0 / QK_HEAD) + EPS)
            d_slab, dg = _head_norm_rope_bwd(dk_ref[h], k_slab, gk_ref[...], inv, cos_t, sin_t)
            dkvp_ref[:, head] = jnp.concatenate([d_slab[:, 0:QK_NOPE], dv_ref[h]], axis=-1).astype(BF16)
            dkpe += d_slab[:, QK_NOPE:QK_PAD]
            dgk += jnp.sum(dg, 0, keepdims=True)
        dqa = _dot_nt(dqp_ref[...], wq_ref[...])
        dx, dg = _rms_bwd(dqa, ql_t, qa_ref[...], inv_qa, Q_LORA)
        dql_ref[...] = dx.astype(BF16)
        _acc(dqa_ref, jnp.sum(dg, 0, keepdims=True))
        dca = _dot_nt(dkvp_ref[...], wkv_ref[...])
        dx, dg = _rms_bwd(dca, ckv, ka_ref[...], inv_ca, KV_LORA)
        dkvl_ref[:, 0:KV_LORA] = dx.astype(BF16)
        dkvl_ref[:, KV_LORA:KV_LAT_PAD] = dkpe.astype(BF16)
        _acc(dka_ref, jnp.sum(dg, 0, keepdims=True))
        _acc(dgq_ref, dgq)
        _acc(dgk_ref, dgk)

    row_outs = [((n, Q_LORA), BF16), ((n, KV_LAT_PAD), BF16), ((n, Q_LORA), BF16), ((n, N_HEADS * QK_PAD), BF16),
                ((n, KV_LORA), BF16), ((n, N_HEADS * (QK_NOPE + V_HEAD)), BF16)]
    acc_outs = [((1, Q_LORA), F32), ((1, KV_LORA), F32), ((1, QK_PAD), F32), ((1, QK_PAD), F32)]
    return _row_call(body, "qkv_prep_bwd", n, ROW_T, [ql, kvl, cos_t, sin_t, dq, dk, dv],
                     [q_a_norm, kv_a_norm, wq, wkv, gq, gk], row_outs, acc_outs, xch=xch)


def _glu_bwd(dy_ssm, y, w_glu, b_glu, w_o_ssm):
    n = y.shape[0]

    def body(dys_ref, y_ref, wg_ref, bg_ref, wo_ref, dy_ref, zg_ref, z_ref, dt_ref, db_ref):
        y_t = y_ref[...]
        z, th = _gelu(y_t)
        zb = z.astype(BF16)
        z_ref[...] = zb
        s = _sigmoid(_dot(zb, wg_ref[...]) + bg_ref[...])
        zg_ref[...] = (z * s).astype(BF16)
        dys = dys_ref[...]
        dzg = jnp.zeros_like(y_t)
        for j in range(N_DEV):
            dzg += _dot_nt(dys[:, j * OUT_SHARD:(j + 1) * OUT_SHARD], wo_ref[j])
        dt = dzg * z * s * (1.0 - s)
        dtb = dt.astype(BF16)
        dt_ref[...] = dtb
        dz = dzg * s + _dot_nt(dtb, wg_ref[...])
        dy_ref[...] = dz * _gelu_grad(y_t, th)
        _acc(db_ref, jnp.sum(dt, 0, keepdims=True))

    outs = [((n, SSM_WIDTH), F32)] + [((n, SSM_WIDTH), BF16)] * 3
    return _row_call(body, "glu_bwd", n, ROW_T, [dy_ssm, y], [w_glu, b_glu, w_o_ssm], outs, [((1, SSM_WIDTH), F32)])


def _ssm_bwd(u, dy, st, bblk, cblk, lam, d_row, xch):
    n = u.shape[0]
    t = min(SCAN_T, n)
    nc = n // t
    kb = 512
    perm = _perm_matrix(t)

    def body(u_ref, dy_ref, st_ref, p_ref, bblk_ref, cblk_ref, lam_ref, d_ref,
             du_ref, xs_ref, as_ref, up_ref, dyp_ref, dlam_ref, dd_ref,
             buf_x, buf_a, pw_ref, carry_ref, xcarry_ref, sx_ref, sa_ref):
        @pl.when(pl.program_id(0) == 0)
        def _():
            carry_ref[...] = jnp.zeros_like(carry_ref)
            _power_table(lam_ref, pw_ref, t // SUBCHUNKS)

        u_t = u_ref[...]
        dy_t = dy_ref[...]
        p = p_ref[...]
        ub = _dot(p, u_t.astype(BF16)).astype(BF16)
        dyb = _dot(p, dy_t.astype(BF16)).astype(BF16)
        up_ref[...] = ub
        dyp_ref[...] = dyb
        for c in range(0, 2 * N_STATE, kb):
            buf_x[:, c:c + kb] = _dot(ub, bblk_ref[:, c:c + kb])
        xcarry_ref[...] = st_ref[0]
        _run_scan(buf_x, lam_ref, t, False)
        _run_carries(buf_x, pw_ref, xcarry_ref, sx_ref, t, False)
        _run_fix(buf_x, pw_ref, sx_ref, t, False)
        for c in range(0, 2 * N_STATE, kb):
            buf_a[:, c:c + kb] = _dot_nt(dyb, cblk_ref[c:c + kb, :])
        _run_scan(buf_a, lam_ref, t, True)
        _run_carries(buf_a, pw_ref, carry_ref, sa_ref, t, True)
        _run_fix(buf_a, pw_ref, sa_ref, t, True)
        dup = jnp.zeros((t, SSM_WIDTH), F32)
        for c in range(0, 2 * N_STATE, kb):
            adb = buf_a[:, c:c + kb].astype(BF16)
            as_ref[:, c:c + kb] = adb
            xs_ref[:, c:c + kb] = buf_x[:, c:c + kb].astype(BF16)
            dup += _dot_nt(adb, bblk_ref[:, c:c + kb])
        du_ref[...] = (d_ref[...] * dy_t + _unpermute(p, dup)).astype(BF16)
        for c in range(0, N_STATE, kb):
            re, im = pl.ds(c, kb), pl.ds(N_STATE + c, kb)
            xr, xi = buf_x[pl.ds(0, t - 8), re], buf_x[pl.ds(0, t - 8), im]
            ar, ai = buf_a[pl.ds(8, t - 8), re], buf_a[pl.ds(8, t - 8), im]
            x0r, x0i = sx_ref[:, re], sx_ref[:, im]
            a0r, a0i = buf_a[0:8, re], buf_a[0:8, im]
            dlam_part_re = (jnp.sum(ar * xr + ai * xi, 0, keepdims=True)
                            + jnp.sum(a0r * x0r + a0i * x0i, 0, keepdims=True))
            dlam_part_im = (jnp.sum(ai * xr - ar * xi, 0, keepdims=True)
                            + jnp.sum(a0i * x0r - a0r * x0i, 0, keepdims=True))

            @pl.when(pl.program_id(0) == 0)
            def _(c=c):
                dlam_ref[0:1, c:c + kb] = jnp.zeros((1, kb), F32)
                dlam_ref[1:2, c:c + kb] = jnp.zeros((1, kb), F32)

            dlam_ref[0:1, c:c + kb] += dlam_part_re
            dlam_ref[1:2, c:c + kb] += dlam_part_im
        _acc(dd_ref, jnp.sum(dy_t * u_t, 0, keepdims=True))

    rev = lambda i: (nc - 1 - i, 0)
    consts = [perm, bblk, cblk, lam, d_row]
    return _call(
        body, "ssm_bwd", (nc,), [u, dy, st] + consts,
        [pl.BlockSpec((t, SSM_WIDTH), rev), pl.BlockSpec((t, SSM_WIDTH), rev),
         pl.BlockSpec((1, 8, 2 * N_STATE), lambda i: (nc - 1 - i, 0, 0))] + [_const(a) for a in consts],
        [_sds((n, SSM_WIDTH), BF16), _sds((n, 2 * N_STATE), BF16), _sds((n, 2 * N_STATE), BF16),
         _sds((n, SSM_WIDTH), BF16), _sds((n, SSM_WIDTH), BF16), _sds((2, N_STATE), F32), _sds((1, SSM_WIDTH), F32)],
        [pl.BlockSpec((t, SSM_WIDTH), rev), pl.BlockSpec((t, 2 * N_STATE), rev), pl.BlockSpec((t, 2 * N_STATE), rev),
         pl.BlockSpec((t, SSM_WIDTH), rev), pl.BlockSpec((t, SSM_WIDTH), rev),
         pl.BlockSpec((2, N_STATE), lambda i: (0, 0)), pl.BlockSpec((1, SSM_WIDTH), lambda i: (0, 0))],
        scratch=[pltpu.VMEM((t, 2 * N_STATE), F32)] * 2 + [pltpu.VMEM((t // SUBCHUNKS, 2 * N_STATE), F32)]
        + [pltpu.VMEM((8, 2 * N_STATE), F32)] * 4,
        xch=xch)


def _in_proj_bwd(pieces, dh, x, norm_mix, w_in_pad):
    n = x.shape[0]

    def body(du_ref, dql_ref, dkvl_ref, dgs_ref, dgm_ref, dh_ref, x_ref, g_ref, w_ref, dx_ref, dp_ref, dg_ref):
        dxn = jnp.zeros((dh_ref.shape[0], D_MODEL), F32)
        for ref, (a, b) in zip((du_ref, dql_ref, dkvl_ref, dgs_ref, dgm_ref), IN_SEGS):
            piece = ref[...]
            dp_ref[:, a:b] = piece
            dxn += _dot_nt(piece, w_ref[:, a:b])
        x_t = x_ref[...]
        inv = lax.rsqrt(jnp.sum(x_t * x_t, -1, keepdims=True) * (1.0 / D_MODEL) + EPS)
        dx, dg = _rms_bwd(dxn, x_t, g_ref[...], inv, D_MODEL)
        dx_ref[...] = dh_ref[...] + dx
        _acc(dg_ref, jnp.sum(dg, 0, keepdims=True))

    outs = [((n, D_MODEL), F32), ((n, D_IN_PAD), BF16)]
    return _row_call(body, "in_proj_bwd", n, MM_T, list(pieces) + [dh, x], [norm_mix, w_in_pad], outs,
                     [((1, D_MODEL), F32)])


def _swap_minor(a):
    g, r, c = a.shape[1:]
    return jnp.transpose(a[0], (0, 2, 1)).reshape(g * c, r)


def _pad_in(w):
    return jnp.concatenate([w[:, :KV_END], jnp.zeros((w.shape[0], D_IN_PAD - D_IN), w.dtype), w[:, KV_END:]], axis=1)


def _unpad_in(w):
    return jnp.concatenate([w[:, :KV_END], w[:, KV_END + D_IN_PAD - D_IN:]], axis=1)


def _pad_gain(g):
    return jnp.pad(g, ((0, 0), (0, QK_PAD - QK_HEAD)))


def _place():
    x, y, c = lax.axis_index("x"), lax.axis_index("y"), lax.axis_index("c")
    chips = [(x, y), (1 - x, y), (x, 1 - y), (1 - x, 1 - y)]
    return x, y, c, chips


def _all_gather(block, name):
    rows, lanes = block.shape

    def body(x_ref, out_ref, send_sems, recv_sems, local_sem):
        x, y, c, chips = _place()
        me, sibling = (x, y, c), (x, y, 1 - c)

        def slot(px, py, pc):
            return out_ref.at[4 * px + 2 * py + pc]

        def copy(k, blk, to, src=None):
            return pltpu.make_async_remote_copy(
                src_ref=slot(*blk) if src is None else src, dst_ref=slot(*blk),
                send_sem=send_sems.at[k], recv_sem=recv_sems.at[k], device_id=to, device_id_type=MESH)

        mine = pltpu.make_async_copy(x_ref, slot(*me), local_sem)
        mine.start()
        first = [copy(0, me, sibling, src=x_ref)]
        first += [copy(1 + j, me, (*chip, c), src=x_ref) for j, chip in enumerate(chips[1:])]
        for cp in first:
            cp.start()
        passed = [copy(4 + j, (*chip, c), sibling) for j, chip in enumerate(chips[1:])]
        for j, chip in enumerate(chips[1:]):
            copy(1 + j, (*chip, c), me).wait_recv()
            passed[j].start()
        copy(0, sibling, me).wait_recv()
        for j, chip in enumerate(chips[1:]):
            copy(4 + j, (*chip, 1 - c), me).wait_recv()
        for cp in first + passed:
            cp.wait_send()
        mine.wait()

    return pl.pallas_call(
        body,
        name=name,
        in_specs=[ANY],
        out_specs=ANY,
        out_shape=_sds((N_DEV, rows, lanes), block.dtype),
        scratch_shapes=[pltpu.SemaphoreType.DMA((7,)), pltpu.SemaphoreType.DMA((7,)), pltpu.SemaphoreType.DMA],
    )(block)


def _reduce_scatter(parts, name):
    _, rows, lanes = parts.shape

    def body(p_ref, out_ref, own, land_a, send_b, land_b, sa, ra, sb, rb, lo):
        x, y, c, chips = _place()
        sibling = (x, y, 1 - c)

        def blk(chip, core):
            return p_ref.at[4 * chip[0] + 2 * chip[1] + core]

        to_sib = [pltpu.make_async_remote_copy(
            src_ref=blk(chips[k], 1 - c), dst_ref=land_a.at[k], send_sem=sa.at[k], recv_sem=ra.at[k],
            device_id=sibling, device_id_type=MESH) for k in range(4)]
        for cp in to_sib:
            cp.start()
        loads = [pltpu.make_async_copy(blk(chips[k], c), own.at[k], lo.at[k]) for k in range(4)]
        for cp in loads:
            cp.start()
        to_chip = [pltpu.make_async_remote_copy(
            src_ref=send_b.at[j], dst_ref=land_b.at[j], send_sem=sb.at[j], recv_sem=rb.at[j],
            device_id=(*chips[1 + j], c), device_id_type=MESH) for j in range(3)]
        for k in (1, 2, 3):
            to_sib[k].wait_recv()
            loads[k].wait()
            send_b[k - 1] = (own[k] + land_a[k]).astype(BF16)
            to_chip[k - 1].start()
        to_sib[0].wait_recv()
        loads[0].wait()
        acc = own[0] + land_a[0]
        for j in range(3):
            to_chip[j].wait_recv()
            acc = acc + land_b[j].astype(F32)
        out_ref[...] = acc
        for cp in to_sib + to_chip:
            cp.wait_send()

    return pl.pallas_call(
        body,
        name=name,
        in_specs=[ANY],
        out_specs=pl.BlockSpec(memory_space=pltpu.VMEM),
        out_shape=_sds((rows, lanes), F32),
        scratch_shapes=[pltpu.VMEM((4, rows, lanes), F32), pltpu.VMEM((4, rows, lanes), F32),
                        pltpu.VMEM((3, rows, lanes), BF16), pltpu.VMEM((3, rows, lanes), BF16)]
        + [pltpu.SemaphoreType.DMA((4,))] * 2 + [pltpu.SemaphoreType.DMA((3,))] * 2 + [pltpu.SemaphoreType.DMA((4,))],
        compiler_params=_params(),
    )(parts)


def _adamw_math(w, g, m, v):
    m = ADAM_B1 * m + (1.0 - ADAM_B1) * g
    v = ADAM_B2 * v + (1.0 - ADAM_B2) * (g * g)
    m_hat = m / (1.0 - ADAM_B1 ** ADAM_STEP)
    v_hat = v / (1.0 - ADAM_B2 ** ADAM_STEP)
    delta = -ADAM_LR * (m_hat / (jnp.sqrt(v_hat) + ADAM_EPS) + ADAM_WD * w)
    return delta, m, v


def _row_tile(r):
    return max(t for t in range(8, min(r, 256) + 1, 8) if r % t == 0)


def _adamw(w, g, m, v, name):
    r, n = w.shape

    def body(w_ref, g_ref, m_ref, v_ref, d_ref, nm_ref, nv_ref):
        d_ref[...], nm_ref[...], nv_ref[...] = _adamw_math(w_ref[...], g_ref[...], m_ref[...], v_ref[...])

    return _row_call(body, name, r, _row_tile(r), [w, g, m, v], [], [((r, n), F32)] * 3)


def _adamw_sum(landed, w, m, v, name):
    r, n = w.shape

    def body(l_ref, w_ref, m_ref, v_ref, g_ref, d_ref, nm_ref, nv_ref):
        g = l_ref[0].astype(F32)
        for dev in range(1, N_DEV):
            g = g + l_ref[dev].astype(F32)
        g_ref[...] = g
        d_ref[...], nm_ref[...], nv_ref[...] = _adamw_math(w_ref[...], g, m_ref[...], v_ref[...])

    tm = max(t for t in range(16, min(r, 256) + 1, 16) if r % t == 0)
    return _row_call(body, name, r, tm, [landed, w, m, v], [], [((r, n), F32)] * 4)


def _adamw_small(gathered, w, m, v, row_counts):
    n_rows = w.shape[0]

    def body(ga_ref, w_ref, m_ref, v_ref, loss_ref, *out_refs):
        g = ga_ref[0]
        for dev in range(1, N_DEV):
            g = g + ga_ref[dev]
        loss_ref[...] = g[n_rows:n_rows + 8]
        g = g[0:n_rows]
        d, nm, nv = _adamw_math(w_ref[...], g, m_ref[...], v_ref[...])
        off = 0
        for p, rows in enumerate(row_counts):
            for k, val in enumerate((g, d, nm, nv)):
                out_refs[4 * p + k][...] = val[off:off + rows]
            off += rows

    outs = [_sds((8, LANES), F32)] + [_sds((rows, LANES), F32) for rows in row_counts for _ in range(4)]
    return pl.pallas_call(body, name="adamw_small", out_shape=outs, compiler_params=_params())(gathered, w, m, v)


SMALL = ("norm_mix", "q_a_norm", "kv_a_norm", "q_norm", "k_norm", "ssm_a_re", "ssm_a_im", "ssm_log_dt", "ssm_b_re",
         "ssm_b_im", "ssm_c_re", "ssm_c_im", "ssm_d", "b_glu", "norm_mlp")
WEIGHT_ORDER = ("norm_mix", "w_in", "q_a_norm", "kv_a_norm", "w_q_b", "w_kv_b", "q_norm", "k_norm", "w_o_mla",
                "ssm_a_re", "ssm_a_im", "ssm_log_dt", "ssm_b_re", "ssm_b_im", "ssm_c_re", "ssm_c_im", "ssm_d", "w_glu",
                "b_glu", "w_o_ssm", "w_out", "norm_mlp", "w_up", "w_down")
IN_SHARD = D_IN // N_DEV
Q_SHARD = QK_HEAD


def _pack_small(vals):
    parts = []
    for n in SMALL:
        flat = vals[n].reshape(-1)
        size = -(-flat.shape[0] // (8 * LANES)) * 8 * LANES
        parts.append(jnp.pad(flat, (0, size - flat.shape[0])).reshape(-1, LANES))
    return jnp.concatenate(parts, axis=0)


def _small_rows(like):
    return [-(-like[n].size // (8 * LANES)) * 8 for n in SMALL]


def _step(x, pos_col, target, w, small):
    bf = {n: a.astype(BF16) for n, a in w.items()}
    gq, gk = _pad_gain(small["q_norm"]), _pad_gain(small["k_norm"])
    a_re = small["ssm_a_re"].reshape(1, N_STATE)
    a_im = small["ssm_a_im"].reshape(1, N_STATE)
    log_dt = jnp.repeat(small["ssm_log_dt"].reshape(SSM_GROUPS), SSM_STATE).reshape(1, N_STATE)
    bt_re, bt_im = _swap_minor(small["ssm_b_re"]), _swap_minor(small["ssm_b_im"])
    c2_re, c2_im = _swap_minor(small["ssm_c_re"]), _swap_minor(small["ssm_c_im"])
    d_row = small["ssm_d"].reshape(1, SSM_WIDTH)

    w_in_all = _all_gather(bf["w_in"], "gather_w_in")
    w_in_pad = _pad_in(jnp.transpose(w_in_all, (1, 0, 2)).reshape(D_MODEL, D_IN))
    cos_t, sin_t = _rope_tables(pos_col)
    lam, bblk, cblk = _ssm_prep(a_re, a_im, log_dt, bt_re, bt_im, c2_re, c2_im)
    wq_mine = jnp.pad(bf["w_q_b"], ((0, 0), (0, QK_PAD - QK_HEAD)))
    xn, u, ql, kvl, gs, gm, w_glu, w_o_ssm = _in_proj(
        x, small["norm_mix"], w_in_pad, xch=[(bf["w_glu"], False), (bf["w_o_ssm"], False)])
    w_glu = w_glu.reshape(SSM_WIDTH, SSM_WIDTH)
    y, y_ssm, st, wq, wkv, w_o_mla, w_out = _ssm_fwd(
        u, bblk, cblk, lam, d_row, w_glu, small["b_glu"], w_o_ssm,
        xch=[(wq_mine, False), (bf["w_kv_b"], False), (bf["w_o_mla"], False), (bf["w_out"], False)])
    w_o_mla, w_out = w_o_mla.reshape(D_MODEL, D_MODEL), w_out.reshape(D_MODEL, D_MODEL)
    wq = jnp.transpose(wq, (1, 0, 2)).reshape(Q_LORA, N_HEADS * QK_PAD)
    wkv = jnp.transpose(wkv, (1, 0, 2)).reshape(KV_LORA, N_HEADS * QK_PAD)
    q, k, v, kt, vt = _qkv_prep(ql, kvl, small["q_a_norm"], small["kv_a_norm"], wq, wkv, gq, gk, cos_t, sin_t)
    attn, lse, w_up, w_down = _attn_fwd(q, k, vt, xch=[(bf["w_up"], False), (bf["w_down"], False)])
    h, mixed, y_mla = _merge(attn, gs, gm, y_ssm, x, w_o_mla, w_out)
    hn, dout, loss = _mlp_fwd_loss(h, target, small["norm_mlp"], w_up, w_down)

    hid, da, dh, d_norm_mlp = _mlp_bwd(dout, hn, h, small["norm_mlp"], w_up, w_down)
    p_w_down = _matmul_tn_shards(hid, dout, "dw_down", False)
    p_w_up = _matmul_tn_shards(hn, da, "dw_up", True)
    dgs, dgm, dy_ssm, dy_mla, dattn = _merge_bwd(dh, gs, gm, y_ssm, y_mla, w_out, w_o_mla)
    p_w_out = _matmul_tn_shards(mixed, dh, "dw_out", False)
    p_w_o_mla = _matmul_tn_shards(attn, dy_mla, "dw_o_mla", False)
    dq, dk, dv, l_w_up, l_w_down, l_w_out, l_w_o_mla = _attn_bwd(
        q, k, kt, v, attn, lse, dattn, xch=[(p_w_up, True), (p_w_down, True), (p_w_out, True), (p_w_o_mla, True)])
    dql, dkvl, qa, dq_pre, ca, dkv_pre, d_q_a_norm, d_kv_a_norm, d_gq, d_gk = _qkv_prep_bwd(
        ql, kvl, dq, dk, dv, small["q_a_norm"], small["kv_a_norm"], wq, wkv, gq, gk, cos_t, sin_t, xch=[])
    p_wq = _matmul_tn_shards(qa, dq_pre, "dw_q_b", True)
    p_wkv = _matmul_tn_shards(ca, dkv_pre, "dw_kv_b", True)
    dy, zg, z, dt, d_b_glu = _glu_bwd(dy_ssm, y, w_glu, small["b_glu"], w_o_ssm)
    p_w_o_ssm = _matmul_tn_shards(zg, dy_ssm, "dw_o_ssm", True)
    p_w_glu = _matmul_tn_shards(z, dt, "dw_glu", False)
    du, xs, ads, u_scan, dy_scan, dlam, d_d, l_wq, l_wkv, l_w_glu, l_w_o_ssm = _ssm_bwd(
        u, dy, st, bblk, cblk, lam, d_row, xch=[(p_wq, True), (p_wkv, True), (p_w_glu, True), (p_w_o_ssm, True)])
    d_bblk = _matmul_tn(u_scan, ads, "d_bblk")
    d_cblk_t = _matmul_tn(dy_scan, xs, "d_cblk")
    d_a_re, d_a_im, d_log_dt, d_bt_re, d_bt_im, d_c_re, d_c_im = _ssm_prep_bwd(
        a_re, a_im, log_dt, bt_re, bt_im, dlam, d_bblk, d_cblk_t)
    dx, dproj, d_norm_mix = _in_proj_bwd((du, dql, dkvl, dgs, dgm), dh, x, small["norm_mix"], w_in_pad)
    tr = lambda mat: jnp.transpose(mat.reshape(SSM_GROUPS, SSM_GROUP_CH, SSM_STATE), (0, 2, 1))
    g_small = {
        "norm_mix": d_norm_mix, "q_a_norm": d_q_a_norm, "kv_a_norm": d_kv_a_norm,
        "q_norm": d_gq[:, :QK_HEAD], "k_norm": d_gk[:, :QK_HEAD],
        "ssm_a_re": d_a_re, "ssm_a_im": d_a_im, "ssm_log_dt": d_log_dt,
        "ssm_b_re": tr(d_bt_re), "ssm_b_im": tr(d_bt_im), "ssm_c_re": d_c_re, "ssm_c_im": d_c_im,
        "ssm_d": d_d, "b_glu": d_b_glu, "norm_mlp": d_norm_mlp,
    }
    g_w_in_pad, g_small_all = _matmul_tn(
        xn, dproj, "dw_in", xch=[(jnp.concatenate([_pack_small(g_small), loss], axis=0), False)])
    parts = jnp.transpose(_unpad_in(g_w_in_pad).reshape(D_MODEL, N_DEV, IN_SHARD), (1, 0, 2))
    g_w_in_mine = _reduce_scatter(parts, "reduce_w_in")
    landed = {"w_q_b": l_wq[:, :, :QK_HEAD], "w_kv_b": l_wkv, "w_o_mla": l_w_o_mla, "w_glu": l_w_glu,
              "w_o_ssm": l_w_o_ssm, "w_out": l_w_out, "w_up": l_w_up, "w_down": l_w_down}
    return dx, landed, g_w_in_mine, g_small_all


def kernel(x, positions, norm_mix, w_in, q_a_norm, kv_a_norm, w_q_b, w_kv_b, q_norm, k_norm, w_o_mla, ssm_a_re, ssm_a_im, ssm_log_dt, ssm_b_re, ssm_b_im, ssm_c_re, ssm_c_im, ssm_d, w_glu, b_glu, w_o_ssm, w_out, norm_mlp, w_up, w_down, loss_target, m_norm_mix, m_w_in, m_q_a_norm, m_kv_a_norm, m_w_q_b, m_w_kv_b, m_q_norm, m_k_norm, m_w_o_mla, m_ssm_a_re, m_ssm_a_im, m_ssm_log_dt, m_ssm_b_re, m_ssm_b_im, m_ssm_c_re, m_ssm_c_im, m_ssm_d, m_w_glu, m_b_glu, m_w_o_ssm, m_w_out, m_norm_mlp, m_w_up, m_w_down, v_norm_mix, v_w_in, v_q_a_norm, v_kv_a_norm, v_w_q_b, v_w_kv_b, v_q_norm, v_k_norm, v_w_o_mla, v_ssm_a_re, v_ssm_a_im, v_ssm_log_dt, v_ssm_b_re, v_ssm_b_im, v_ssm_c_re, v_ssm_c_im, v_ssm_d, v_w_glu, v_b_glu, v_w_o_ssm, v_w_out, v_norm_mlp, v_w_up, v_w_down):
    given = dict(locals())
    w = {n: given[n] for n in WEIGHT_ORDER}
    m = {n: given["m_" + n] for n in WEIGHT_ORDER}
    v = {n: given["v_" + n] for n in WEIGHT_ORDER}
    big = [n for n in WEIGHT_ORDER if n not in SMALL]
    small = {n: w[n] for n in SMALL}

    dx, landed, g_w_in, g_small_all = _step(
        x[0], positions.reshape(-1, 1), loss_target[0], {n: w[n][0] for n in big}, small)

    grads, deltas, new_m, new_v = {}, {}, {}, {}
    for n in big:
        if n == "w_in":
            g = g_w_in
            d, nm, nv = _adamw(w[n][0], g, m[n][0], v[n][0], "adamw_" + n)
        else:
            g, d, nm, nv = _adamw_sum(landed[n], w[n][0], m[n][0], v[n][0], "adamw_" + n)
        grads[n], deltas[n], new_m[n], new_v[n] = g[None], d[None], nm[None], nv[None]

    outs = _adamw_small(g_small_all, _pack_small(small), _pack_small({n: m[n] for n in SMALL}),
                        _pack_small({n: v[n] for n in SMALL}), _small_rows(small))
    for p, n in enumerate(SMALL):
        for k, dst in enumerate((grads, deltas, new_m, new_v)):
            dst[n] = outs[1 + 4 * p + k].reshape(-1)[:small[n].size].reshape(small[n].shape)

    return (outs[0][0, 0], dx[None], *[grads[n] for n in WEIGHT_ORDER], *[deltas[n] for n in WEIGHT_ORDER],
            *[new_m[n] for n in WEIGHT_ORDER], *[new_v[n] for n in WEIGHT_ORDER])
```

```python
import functools
import math

import numpy as np
import jax
import jax.numpy as jnp
from jax import lax
from jax.experimental import pallas as pl
from jax.experimental.pallas import tpu as pltpu

F32 = jnp.float32
BF16 = jnp.bfloat16

D_MODEL = 1024
SSM_GROUPS = 32
SSM_GROUP_CH = 16
SSM_WIDTH = 512
SSM_STATE = 64
N_STATE = SSM_GROUPS * SSM_STATE
N_HEADS = 8
QK_NOPE = 128
QK_ROPE = 64
QK_HEAD = 192
QK_PAD = 256
V_HEAD = 128
Q_LORA = 384
KV_LORA = 256
KV_LAT_PAD = 384
ROPE_THETA = 10000.0
D_FF = 4096
EPS = 1e-6
ATT_SCALE = QK_HEAD ** -0.5
N_DEV = 8
FF_SHARD = D_FF // N_DEV
OUT_SHARD = D_MODEL // N_DEV

IN_SEGS = ((0, 512), (512, 896), (896, 1280), (1280, 2304), (2304, 3328))
D_IN = 3264
D_IN_PAD = 3328
KV_END = 1216

ADAM_LR = 0.001
ADAM_B1 = 0.9
ADAM_B2 = 0.999
ADAM_EPS = 1e-08
ADAM_WD = 0.01
ADAM_STEP = 10

VMEM_LIMIT = 56 * 1024 * 1024
MESH = pl.DeviceIdType.MESH
ANY = pl.BlockSpec(memory_space=pl.ANY)
LANES = 128

SCAN_T = 256
SUBCHUNKS = 8
SCAN_CG = 512
ATT_T = 512
ATT_SUB = 2
ROW_T = 256
MM_T = 512


def _params(sem=None):
    return pltpu.CompilerParams(dimension_semantics=sem, vmem_limit_bytes=VMEM_LIMIT)


def _rows(arr, tm):
    if arr.ndim == 2:
        return pl.BlockSpec((tm, arr.shape[1]), lambda i: (i, 0))
    return pl.BlockSpec((arr.shape[0], tm, arr.shape[2]), lambda i: (0, i, 0))


def _const(arr):
    nd = arr.ndim
    return pl.BlockSpec(arr.shape, lambda *_: (0,) * nd, pipeline_mode=pl.Buffered(1))


def _sds(shape, dtype):
    return jax.ShapeDtypeStruct(shape, dtype)


PEERS = tuple((dx, dy, dc) for dx in (0, 1) for dy in (0, 1) for dc in (0, 1) if (dx, dy, dc) != (0, 0, 0))


def _here():
    x, y, c = lax.axis_index("x"), lax.axis_index("y"), lax.axis_index("c")
    return x, y, c, 4 * x + 2 * y + c


def _xchg_start(scatter, srcs, dsts, send, recv, local):
    x, y, c, me = _here()
    for e, sc in enumerate(scatter):
        src, dst = srcs[e], dsts[e]
        pltpu.make_async_copy(src.at[me] if sc else src, dst.at[me], local.at[e]).start()
        for dx, dy, dc in PEERS:
            px, py, pc = (1 - x if dx else x), (1 - y if dy else y), (1 - c if dc else c)
            pltpu.make_async_remote_copy(
                src_ref=src.at[4 * px + 2 * py + pc] if sc else src, dst_ref=dst.at[me],
                send_sem=send.at[e], recv_sem=recv.at[e], device_id=(px, py, pc), device_id_type=MESH).start()


def _xchg_wait(scatter, srcs, dsts, send, recv, local):
    x, y, c, me = _here()
    for e, sc in enumerate(scatter):
        src, dst = srcs[e], dsts[e]
        pltpu.make_async_copy(src.at[me] if sc else src, dst.at[me], local.at[e]).wait()
        span = dst.at[pl.ds(0, N_DEV - 1)]
        both = pltpu.make_async_remote_copy(src_ref=span, dst_ref=span, send_sem=send.at[e], recv_sem=recv.at[e],
                                            device_id=(x, y, c), device_id_type=MESH)
        both.wait_send()
        both.wait_recv()


def _call(body, name, grid, ins, in_specs, outs, out_specs, scratch=(), xch=()):
    n_in, n_out, ne = len(ins), len(outs), len(xch)
    scatter = [sc for _, sc in xch]
    x_outs = [_sds((N_DEV,) + (a.shape[1:] if sc else a.shape), a.dtype) for a, sc in xch]
    sems = [pltpu.SemaphoreType.DMA((ne,))] * 3 if ne else []

    def wrapped(*refs):
        in_refs, x_src = refs[:n_in], refs[n_in:n_in + ne]
        out_refs = refs[n_in + ne:n_in + ne + n_out]
        x_dst = refs[n_in + ne + n_out:n_in + 2 * ne + n_out]
        rest = refs[n_in + 2 * ne + n_out:]
        if ne:
            x_sems, rest = rest[len(rest) - 3:], rest[:len(rest) - 3]
            first = functools.reduce(jnp.logical_and, [pl.program_id(d) == 0 for d in range(len(grid))])
            last = functools.reduce(jnp.logical_and, [pl.program_id(d) == grid[d] - 1 for d in range(len(grid))])

            @pl.when(first)
            def _():
                _xchg_start(scatter, x_src, x_dst, *x_sems)

        body(*in_refs, *out_refs, *rest)
        if ne:
            @pl.when(last)
            def _():
                _xchg_wait(scatter, x_src, x_dst, *x_sems)

    return pl.pallas_call(
        wrapped,
        name=name,
        grid=grid,
        in_specs=list(in_specs) + [ANY] * ne,
        out_specs=list(out_specs) + [ANY] * ne,
        out_shape=list(outs) + x_outs,
        scratch_shapes=list(scratch) + sems,
        compiler_params=_params(("arbitrary",) * len(grid)),
    )(*ins, *[a for a, _ in xch])


def _row_call(body, name, n_rows, tm, row_ins, const_ins, row_outs, acc_outs=(), xch=()):
    outs = [_sds(s, d) for s, d in row_outs] + [_sds(s, d) for s, d in acc_outs]
    out_specs = [_rows(o, tm) for o in outs[: len(row_outs)]] + [
        pl.BlockSpec(o.shape, lambda i, nd=len(o.shape): (0,) * nd) for o in outs[len(row_outs):]]
    in_specs = [_rows(a, tm) for a in row_ins] + [_const(a) for a in const_ins]
    return _call(body, name, (n_rows // tm,), list(row_ins) + list(const_ins), in_specs, outs, out_specs, xch=xch)


def _dot(a, b):
    return jnp.dot(a, b, preferred_element_type=F32)


def _dot_nt(a, b):
    return lax.dot_general(a, b, (((1,), (1,)), ((), ())), preferred_element_type=F32)


def _dot_tn(a, b):
    return lax.dot_general(a, b, (((0,), (0,)), ((), ())), preferred_element_type=F32)


def _rms(x, g, n):
    inv = lax.rsqrt(jnp.sum(x * x, -1, keepdims=True) * (1.0 / n) + EPS)
    return x * inv * g, inv


def _rms_bwd(dy, x, g, inv, n):
    xh = x * inv
    dxh = dy * g
    dx = inv * (dxh - xh * (jnp.sum(dxh * xh, -1, keepdims=True) * (1.0 / n)))
    return dx, dy * xh


def _sigmoid(x):
    return 1.0 / (1.0 + jnp.exp(-x))


_GELU_C = math.sqrt(2.0 / math.pi)


def _gelu(y):
    th = jnp.tanh(_GELU_C * (y + 0.044715 * (y * y * y)))
    return 0.5 * y * (1.0 + th), th


def _gelu_grad(y, th):
    return 0.5 * (1.0 + th) + 0.5 * y * (1.0 - th * th) * (_GELU_C * (1.0 + 3.0 * 0.044715 * (y * y)))


def _acc(ref, val):
    @pl.when(pl.program_id(0) == 0)
    def _():
        ref[...] = jnp.zeros_like(ref)

    ref[...] += val


def _tile(n, limit):
    if n <= limit:
        return n
    return max(t for t in range(128, limit + 1, 128) if n % t == 0)


def _matmul_tn(a, b, name, tm=512, tk=512, xch=()):
    k_dim, m = a.shape
    n = b.shape[1]
    tm, tk = _tile(m, tm), _tile(k_dim, tk)

    def body(a_ref, b_ref, o_ref):
        @pl.when(pl.program_id(1) == 0)
        def _():
            o_ref[...] = jnp.zeros_like(o_ref)

        o_ref[...] += _dot_tn(a_ref[...].astype(BF16), b_ref[...].astype(BF16))

    outs = _call(
        body, name, (m // tm, k_dim // tk), [a, b],
        [pl.BlockSpec((tk, tm), lambda i, k: (k, i)), pl.BlockSpec((tk, n), lambda i, k: (k, 0))],
        [_sds((m, n), F32)], [pl.BlockSpec((tm, n), lambda i, k: (i, 0))], xch=xch)
    return outs if xch else outs[0]


def _matmul_tn_shards(a, b, name, by_col, tm=512, tk=512):
    k_dim, m = a.shape
    n = b.shape[1]
    tm, tk = _tile(m, tm), _tile(k_dim, tk)
    nk = k_dim // tk
    if by_col:
        r, c = m, n // N_DEV
        out_spec = pl.BlockSpec((N_DEV, tm, c), lambda i, k: (0, i, 0))
    else:
        r, c = m // N_DEV, n
        per = tm // r
        out_spec = pl.BlockSpec((per, r, c), lambda i, k: (i, 0, 0))

    def body(a_ref, b_ref, o_ref, acc_ref):
        k = pl.program_id(1)

        @pl.when(k == 0)
        def _():
            acc_ref[...] = jnp.zeros_like(acc_ref)

        acc_ref[...] += _dot_tn(a_ref[...].astype(BF16), b_ref[...].astype(BF16))

        @pl.when(k == nk - 1)
        def _():
            if by_col:
                for j in range(N_DEV):
                    o_ref[j] = acc_ref[:, j * c:(j + 1) * c].astype(BF16)
            else:
                for s in range(per):
                    o_ref[s] = acc_ref[s * r:(s + 1) * r, :].astype(BF16)

    return pl.pallas_call(
        body,
        name=name,
        grid=(m // tm, nk),
        in_specs=[pl.BlockSpec((tk, tm), lambda i, k: (k, i)), pl.BlockSpec((tk, n), lambda i, k: (k, 0))],
        out_specs=out_spec,
        out_shape=_sds((N_DEV, r, c), BF16),
        scratch_shapes=[pltpu.VMEM((tm, n), F32)],
        compiler_params=_params(("parallel", "arbitrary")),
    )(a, b)


def _rope_tables(pos_col):
    n = pos_col.shape[0]
    half = QK_ROPE // 2
    inv_freq = (ROPE_THETA ** (-np.arange(half, dtype=np.float32) / half)).astype(np.float32)
    freq_row = jnp.asarray(np.concatenate([inv_freq, inv_freq, np.zeros(64, np.float32)])[None, :])

    def body(p_ref, f_ref, c_ref, s_ref):
        ang = p_ref[...].astype(F32) * f_ref[...]
        c_ref[...] = jnp.cos(ang)
        s_ref[...] = jnp.sin(ang)

    return _row_call(body, "rope_tables", n, min(n, 1024), [pos_col], [freq_row], [((n, 128), F32)] * 2)


def _rope_rot(v):
    lane = lax.broadcasted_iota(jnp.int32, v.shape, 1)
    return jnp.where(lane < 32, -pltpu.roll(v, 96, 1), jnp.where(lane < 64, pltpu.roll(v, 32, 1), 0.0))


def _rope_rot_t(v):
    lane = lax.broadcasted_iota(jnp.int32, v.shape, 1)
    return jnp.where(lane < 32, pltpu.roll(v, 96, 1), jnp.where(lane < 64, -pltpu.roll(v, 32, 1), 0.0))


def _in_proj(x, norm_mix, w_in_pad, xch):
    n = x.shape[0]

    def body(x_ref, g_ref, w_ref, xn_ref, u_ref, ql_ref, kvl_ref, gs_ref, gm_ref):
        xn, _ = _rms(x_ref[...], g_ref[...], D_MODEL)
        xb = xn.astype(BF16)
        xn_ref[...] = xb
        for ref, (a, b) in zip((u_ref, ql_ref, kvl_ref, gs_ref, gm_ref), IN_SEGS):
            ref[...] = _dot(xb, w_ref[:, a:b])

    outs = [((n, D_MODEL), BF16)] + [((n, b - a), F32) for a, b in IN_SEGS]
    return _row_call(body, "in_proj", n, MM_T, [x], [norm_mix, w_in_pad], outs, xch=xch)


def _ssm_prep_fn(a_re, a_im, log_dt, b_re_x, b_im_x):
    dt = jnp.exp(log_dt)
    mag = jnp.exp(a_re * dt)
    lr = mag * jnp.cos(a_im * dt)
    li = mag * jnp.sin(a_im * dt)
    den = a_re * a_re + a_im * a_im
    fr = ((lr - 1.0) * a_re + li * a_im) / den
    fi = (li * a_re - (lr - 1.0) * a_im) / den
    return lr, li, fr * b_re_x - fi * b_im_x, fr * b_im_x + fi * b_re_x


def _dot_exact(a, b, dims):
    return lax.dot_general(a, b, (dims, ((), ())), precision=lax.Precision.HIGHEST, preferred_element_type=F32)


def _lane_repeat(width, n):
    src = lax.broadcasted_iota(jnp.int32, (width, n), 0)
    dst = lax.broadcasted_iota(jnp.int32, (width, n), 1)
    return (dst % width == src).astype(F32)


def _same_group(rows, rows_per_group, cols, cols_per_group):
    row = lax.broadcasted_iota(jnp.int32, (rows, cols), 0)
    col = lax.broadcasted_iota(jnp.int32, (rows, cols), 1)
    return (row // rows_per_group) == (col // cols_per_group)


def _expand_b(bt):
    tiled = _dot_exact(bt, _lane_repeat(SSM_STATE, N_STATE), ((1,), (0,)))
    return jnp.where(_same_group(SSM_WIDTH, SSM_GROUP_CH, N_STATE, SSM_STATE), tiled, 0.0)


def _collect_b(m):
    masked = jnp.where(_same_group(SSM_WIDTH, SSM_GROUP_CH, N_STATE, SSM_STATE), m, 0.0)
    return _dot_exact(masked, _lane_repeat(SSM_STATE, N_STATE), ((1,), (1,)))


def _ssm_prep(a_re, a_im, log_dt, bt_re, bt_im, c2_re, c2_im):
    def body(ar, ai, ld, br, bi, cr, ci, lam_ref, bblk_ref, cblk_ref):
        lr, li, bbr, bbi = _ssm_prep_fn(ar[...], ai[...], ld[...], _expand_b(br[...]), _expand_b(bi[...]))
        lam_ref[0:1, :] = lr
        lam_ref[1:2, :] = li
        bblk_ref[:, 0:N_STATE] = bbr.astype(BF16)
        bblk_ref[:, N_STATE:] = bbi.astype(BF16)
        rep = _lane_repeat(SSM_GROUP_CH, SSM_WIDTH)
        own = _same_group(N_STATE, SSM_STATE, SSM_WIDTH, SSM_GROUP_CH)
        cblk_ref[0:N_STATE, :] = jnp.where(own, _dot_exact(cr[...], rep, ((1,), (0,))), 0.0).astype(BF16)
        cblk_ref[N_STATE:, :] = jnp.where(own, -_dot_exact(ci[...], rep, ((1,), (0,))), 0.0).astype(BF16)

    return pl.pallas_call(
        body,
        name="ssm_prep",
        out_shape=[_sds((2, N_STATE), F32), _sds((SSM_WIDTH, 2 * N_STATE), BF16),
                   _sds((2 * N_STATE, SSM_WIDTH), BF16)],
        compiler_params=_params(),
    )(a_re, a_im, log_dt, bt_re, bt_im, c2_re, c2_im)


def _ssm_prep_bwd(a_re, a_im, log_dt, bt_re, bt_im, dlam, dbblk, dcblk_t):
    def body(ar, ai, ld, br, bi, dl, db, dc, dar, dai, dld, dbr, dbi, dcr, dci):
        _, vjp = jax.vjp(_ssm_prep_fn, ar[...], ai[...], ld[...], _expand_b(br[...]), _expand_b(bi[...]))
        g = vjp((dl[0:1, :], dl[1:2, :], db[:, 0:N_STATE], db[:, N_STATE:]))
        dar[...] = g[0]
        dai[...] = g[1]
        grp = lax.broadcasted_iota(jnp.int32, (SSM_GROUPS, N_STATE), 0)
        lane = lax.broadcasted_iota(jnp.int32, (SSM_GROUPS, N_STATE), 1)
        sel = (lane // SSM_STATE) == grp
        dld[...] = jnp.sum(jnp.where(sel, jnp.broadcast_to(g[2], (SSM_GROUPS, N_STATE)), 0.0), axis=1, keepdims=True)
        dbr[...] = _collect_b(g[3])
        dbi[...] = _collect_b(g[4])
        dcr[...] = _collect_b(dc[:, 0:N_STATE])
        dci[...] = -_collect_b(dc[:, N_STATE:])

    small = _sds((SSM_WIDTH, SSM_STATE), F32)
    return pl.pallas_call(
        body,
        name="ssm_prep_bwd",
        out_shape=[_sds((1, N_STATE), F32), _sds((1, N_STATE), F32), _sds((SSM_GROUPS, 1), F32), small, small, small, small],
        compiler_params=_params(),
    )(a_re, a_im, log_dt, bt_re, bt_im, dlam, dbblk, dcblk_t)


def _perm_matrix(t):
    run = t // SUBCHUNKS
    p = np.zeros((t, t), np.float32)
    r = np.arange(t)
    p[r, (r % SUBCHUNKS) * run + r // SUBCHUNKS] = 1.0
    return jnp.asarray(p, dtype=BF16)


def _unpermute(p, a):
    hi = a.astype(BF16)
    r1 = a - hi.astype(F32)
    mid = r1.astype(BF16)
    lo = (r1 - mid.astype(F32)).astype(BF16)
    return _dot_tn(p, hi) + _dot_tn(p, mid) + _dot_tn(p, lo)


def _power_table(lam_ref, pw_ref, n):
    lr, li = lam_ref[0:1, :], lam_ref[1:2, :]
    pw_ref[0:1, 0:N_STATE] = lr
    pw_ref[0:1, N_STATE:] = li

    def step(i, carry):
        pr, pi = carry
        pr, pi = pr * lr - pi * li, pr * li + pi * lr
        pw_ref[pl.ds(i, 1), 0:N_STATE] = pr
        pw_ref[pl.ds(i, 1), N_STATE:] = pi
        return pr, pi

    lax.fori_loop(1, n, step, (lr, li))


def _col_groups():
    return [(pl.ds(c, SCAN_CG), pl.ds(N_STATE + c, SCAN_CG)) for c in range(0, N_STATE, SCAN_CG)]


def _run_scan(buf, lam_ref, t, reverse):
    nblk = t // 8
    for re, im in _col_groups():
        lr = jnp.broadcast_to(lam_ref[0:1, re], (8, SCAN_CG))
        li = jnp.broadcast_to(lam_ref[1:2, re], (8, SCAN_CG))
        if reverse:
            li = -li
        first = pl.ds((nblk - 1) * 8 if reverse else 0, 8)

        def step(k, carry, re=re, im=im, lr=lr, li=li):
            pr, pi = carry
            i = (nblk - 2 - k) if reverse else (k + 1)
            r = pl.ds(pl.multiple_of(i * 8, 8), 8)
            xr = buf[r, re] + lr * pr - li * pi
            xi = buf[r, im] + lr * pi + li * pr
            buf[r, re] = xr
            buf[r, im] = xi
            return xr, xi

        lax.fori_loop(0, nblk - 1, step, (buf[first, re], buf[first, im]))


def _run_carries(buf, pw_ref, carry_ref, s_ref, t, reverse):
    nblk = t // 8
    run = t // SUBCHUNKS
    edge = buf[pl.ds(0 if reverse else (nblk - 1) * 8, 8), :]
    pr, pi = pw_ref[run - 1:run, 0:N_STATE], pw_ref[run - 1:run, N_STATE:]
    if reverse:
        pi = -pi
    sr, si = carry_ref[0:1, 0:N_STATE], carry_ref[0:1, N_STATE:]
    for s in (range(SUBCHUNKS - 1, -1, -1) if reverse else range(SUBCHUNKS)):
        s_ref[s:s + 1, 0:N_STATE] = sr
        s_ref[s:s + 1, N_STATE:] = si
        er, ei = edge[s:s + 1, 0:N_STATE], edge[s:s + 1, N_STATE:]
        sr, si = er + pr * sr - pi * si, ei + pr * si + pi * sr
    carry_ref[:, 0:N_STATE] = jnp.broadcast_to(sr, (8, N_STATE))
    carry_ref[:, N_STATE:] = jnp.broadcast_to(si, (8, N_STATE))


def _run_fix(buf, pw_ref, s_ref, t, reverse):
    nblk = t // 8
    for re, im in _col_groups():
        sr, si = s_ref[:, re], s_ref[:, im]

        def step(i, carry, re=re, im=im, sr=sr, si=si):
            r = pl.ds(pl.multiple_of(i * 8, 8), 8)
            row = pl.ds((nblk - 1 - i) if reverse else i, 1)
            pr, pi = pw_ref[row, re], pw_ref[row, im]
            if reverse:
                pi = -pi
            buf[r, re] += pr * sr - pi * si
            buf[r, im] += pr * si + pi * sr
            return carry

        lax.fori_loop(0, nblk, step, 0)


STATE_BLOCKS = 2 * N_STATE // LANES
CH_BLOCKS = SSM_WIDTH // LANES


def _state_block(b):
    pair = b % (N_STATE // LANES)
    k = (pair * 2 * SSM_GROUP_CH) // LANES
    return slice(b * LANES, (b + 1) * LANES), slice(k * LANES, (k + 1) * LANES)


def _channel_block(c):
    w = N_STATE // CH_BLOCKS
    return slice(c * LANES, (c + 1) * LANES), slice(c * w, (c + 1) * w), slice(N_STATE + c * w, N_STATE + (c + 1) * w)


def _to_states(vb, w_ref, buf, nt):
    for b in range(STATE_BLOCKS):
        lanes, ch = _state_block(b)
        buf[:, lanes] = _dot_nt(vb[:, ch], w_ref[lanes, ch]) if nt else _dot(vb[:, ch], w_ref[ch, lanes])


def _to_channels(buf, w_ref, nt):
    outs = []
    for c in range(CH_BLOCKS):
        ch, re, im = _channel_block(c)
        xr, xi = buf[:, re].astype(BF16), buf[:, im].astype(BF16)
        if nt:
            outs.append(_dot_nt(xr, w_ref[ch, re]) + _dot_nt(xi, w_ref[ch, im]))
        else:
            outs.append(_dot(xr, w_ref[re, ch]) + _dot(xi, w_ref[im, ch]))
    return jnp.concatenate(outs, axis=-1)


def _ssm_fwd(u, bblk, cblk, lam, d_row, w_glu, b_glu, w_o_ssm, xch):
    n = u.shape[0]
    t = min(SCAN_T, n)
    perm = _perm_matrix(t)

    def body(u_ref, p_ref, bblk_ref, cblk_ref, lam_ref, d_ref, wg_ref, bg_ref, wo_ref, y_ref, ys_ref, st_ref,
             buf, pw_ref, carry_ref, s_ref):
        @pl.when(pl.program_id(0) == 0)
        def _():
            carry_ref[...] = jnp.zeros_like(carry_ref)
            _power_table(lam_ref, pw_ref, t // SUBCHUNKS)

        st_ref[0] = carry_ref[...]
        u_t = u_ref[...]
        p = p_ref[...]
        ub = _dot(p, u_t.astype(BF16)).astype(BF16)
        _to_states(ub, bblk_ref, buf, False)
        _run_scan(buf, lam_ref, t, False)
        _run_carries(buf, pw_ref, carry_ref, s_ref, t, False)
        _run_fix(buf, pw_ref, s_ref, t, False)
        y = d_ref[...] * u_t + _unpermute(p, _to_channels(buf, cblk_ref, False))
        y_ref[...] = y
        z, _ = _gelu(y)
        s = _sigmoid(_dot(z.astype(BF16), wg_ref[...]) + bg_ref[...])
        zgb = (z * s).astype(BF16)
        for j in range(N_DEV):
            ys_ref[:, j * OUT_SHARD:(j + 1) * OUT_SHARD] = _dot(zgb, wo_ref[j])

    consts = [perm, bblk, cblk, lam, d_row, w_glu, b_glu, w_o_ssm]
    return _call(
        body, "ssm_fwd", (n // t,), [u] + consts, [_rows(u, t)] + [_const(a) for a in consts],
        [_sds((n, SSM_WIDTH), F32), _sds((n, D_MODEL), F32), _sds((n // t, 8, 2 * N_STATE), F32)],
        [pl.BlockSpec((t, SSM_WIDTH), lambda i: (i, 0)), pl.BlockSpec((t, D_MODEL), lambda i: (i, 0)),
         pl.BlockSpec((1, 8, 2 * N_STATE), lambda i: (i, 0, 0))],
        scratch=[pltpu.VMEM((t, 2 * N_STATE), F32), pltpu.VMEM((t // SUBCHUNKS, 2 * N_STATE), F32),
                 pltpu.VMEM((8, 2 * N_STATE), F32), pltpu.VMEM((8, 2 * N_STATE), F32)],
        xch=xch)


def _head_norm_rope(slab, gain, cos_t, sin_t):
    xn, inv = _rms(slab, gain, QK_HEAD)
    lo, hi = xn[:, 0:128], xn[:, 128:256]
    return jnp.concatenate([lo, hi * cos_t + _rope_rot(hi) * sin_t], axis=-1), inv


def _head_norm_rope_bwd(g, slab, gain, inv, cos_t, sin_t):
    g_lo, g_hi = g[:, 0:128], g[:, 128:256]
    g_n = jnp.concatenate([g_lo, g_hi * cos_t + _rope_rot_t(g_hi * sin_t)], axis=-1)
    return _rms_bwd(g_n, slab, gain, inv, QK_HEAD)


def _qkv_prep(ql, kvl, q_a_norm, kv_a_norm, wq, wkv, gq, gk, cos_t, sin_t):
    n = ql.shape[0]
    tm = ROW_T

    def body(ql_ref, kvl_ref, cos_ref, sin_ref, qa_ref, ka_ref, wq_ref, wkv_ref, gq_ref, gk_ref,
             q_ref, k_ref, v_ref, kt_ref, vt_ref):
        cos_t, sin_t = cos_ref[...], sin_ref[...]
        qa, _ = _rms(ql_ref[...], qa_ref[...], Q_LORA)
        qab = qa.astype(BF16)
        kvl_t = kvl_ref[...]
        ca, _ = _rms(kvl_t[:, 0:KV_LORA], ka_ref[...], KV_LORA)
        cab = ca.astype(BF16)
        kpe = kvl_t[:, KV_LORA:KV_LAT_PAD]
        q_pre = _dot(qab, wq_ref[...])
        kv_pre = _dot(cab, wkv_ref[...])
        for h in range(N_HEADS):
            qh, _ = _head_norm_rope(q_pre[:, h * QK_PAD:(h + 1) * QK_PAD], gq_ref[...], cos_t, sin_t)
            q_ref[h] = (qh * ATT_SCALE).astype(BF16)
            kv_h = kv_pre[:, h * QK_PAD:(h + 1) * QK_PAD]
            kh, _ = _head_norm_rope(jnp.concatenate([kv_h[:, 0:QK_NOPE], kpe], axis=-1), gk_ref[...], cos_t, sin_t)
            k_ref[h] = kh.astype(BF16)
            kt_ref[h] = kh.T.astype(BF16)
            vh = kv_h[:, QK_NOPE:]
            v_ref[h] = vh.astype(BF16)
            vt_ref[h] = vh.T.astype(BF16)

    row_ins, consts = [ql, kvl, cos_t, sin_t], [q_a_norm, kv_a_norm, wq, wkv, gq, gk]
    outs = [_sds((N_HEADS, n, QK_PAD), BF16), _sds((N_HEADS, n, QK_PAD), BF16), _sds((N_HEADS, n, V_HEAD), BF16),
            _sds((N_HEADS, QK_PAD, n), BF16), _sds((N_HEADS, V_HEAD, n), BF16)]
    out_specs = [_rows(o, tm) for o in outs[:3]] + [
        pl.BlockSpec((N_HEADS, QK_PAD, tm), lambda i: (0, 0, i)), pl.BlockSpec((N_HEADS, V_HEAD, tm), lambda i: (0, 0, i))]
    return _call(body, "qkv_prep", (n // tm,), row_ins + consts,
                 [_rows(a, tm) for a in row_ins] + [_const(a) for a in consts], outs, out_specs)


def _causal_mask_t(st, t):
    key = lax.broadcasted_iota(jnp.int32, (t, t), 0)
    qry = lax.broadcasted_iota(jnp.int32, (t, t), 1)
    return jnp.where(key <= qry, st, -jnp.inf)


def _attn_fwd(q, k, vt, xch):
    n = q.shape[1]
    t = min(ATT_T, n)

    def body(q_ref, k_ref, vt_ref, o_ref, lse_ref):
        i = pl.program_id(1)
        qt = q_ref[0]

        def kv_tile(j, carry, diag):
            m, l, acc = carry
            ts = t // ATT_SUB
            sts = []
            for a in range(ATT_SUB):
                r0 = pl.multiple_of(j * t + a * ts, ts)
                st = _dot_nt(k_ref[0, pl.ds(r0, ts), :], qt)
                if diag:
                    key = lax.broadcasted_iota(jnp.int32, (ts, t), 0) + a * ts
                    qry = lax.broadcasted_iota(jnp.int32, (ts, t), 1)
                    st = jnp.where(key <= qry, st, -jnp.inf)
                sts.append(st)
            for a, st in enumerate(sts):
                r0 = pl.multiple_of(j * t + a * ts, ts)
                m_new = jnp.maximum(m, jnp.max(st, 0, keepdims=True))
                alpha = jnp.exp(m - m_new)
                pt = jnp.exp(st - m_new)
                l = alpha * l + jnp.sum(pt, 0, keepdims=True)
                acc = alpha * acc + _dot(vt_ref[0, :, pl.ds(r0, ts)], pt.astype(BF16))
                m = m_new
            return m, l, acc

        init = (jnp.full((1, t), -jnp.inf, F32), jnp.zeros((1, t), F32), jnp.zeros((V_HEAD, t), F32))
        carry = lax.fori_loop(0, i, functools.partial(kv_tile, diag=False), init)
        m, l, acc = kv_tile(i, carry, True)
        o_ref[...] = (acc / l).T
        lse_ref[0] = m + jnp.log(l)

    return _call(
        body, "attn_fwd", (N_HEADS, n // t), [q, k, vt],
        [pl.BlockSpec((1, t, QK_PAD), lambda h, i: (h, i, 0)), pl.BlockSpec((1, n, QK_PAD), lambda h, i: (h, 0, 0)),
         pl.BlockSpec((1, V_HEAD, n), lambda h, i: (h, 0, 0))],
        [_sds((n, N_HEADS * V_HEAD), F32), _sds((N_HEADS, 1, n), F32)],
        [pl.BlockSpec((t, V_HEAD), lambda h, i: (i, h)), pl.BlockSpec((1, 1, t), lambda h, i: (h, 0, i))],
        xch=xch)


def _merge(attn, gs, gm, y_ssm, x, w_o_mla, w_out):
    n = x.shape[0]

    def body(at_ref, gs_ref, gm_ref, ys_ref, x_ref, wo_ref, wout_ref, h_ref, mx_ref, ym_ref):
        y_mla = _dot(at_ref[...].astype(BF16), wo_ref[...])
        ym_ref[...] = y_mla
        mixed = (_sigmoid(gs_ref[...]) * ys_ref[...] + _sigmoid(gm_ref[...]) * y_mla).astype(BF16)
        mx_ref[...] = mixed
        h_ref[...] = x_ref[...] + _dot(mixed, wout_ref[...])

    outs = [((n, D_MODEL), F32), ((n, D_MODEL), BF16), ((n, D_MODEL), F32)]
    return _row_call(body, "merge", n, MM_T, [attn, gs, gm, y_ssm, x], [w_o_mla, w_out], outs)


def _mlp_fwd_loss(h, target, norm_mlp, w_up, w_down):
    n = h.shape[0]

    def body(h_ref, t_ref, g_ref, wu_ref, wd_ref, hn_ref, do_ref, loss_ref):
        h_t = h_ref[...]
        hn, _ = _rms(h_t, g_ref[...], D_MODEL)
        hb = hn.astype(BF16)
        hn_ref[...] = hb
        out = h_t
        for j in range(N_DEV):
            a = jnp.maximum(_dot(hb, wu_ref[j]), 0.0)
            out += _dot((a * a).astype(BF16), wd_ref[j])
        err = out - t_ref[...]
        do_ref[...] = err * (1.0 / D_MODEL)
        _acc(loss_ref, jnp.broadcast_to(jnp.sum(err * err) * (0.5 / D_MODEL), loss_ref.shape))

    outs = [((n, D_MODEL), BF16), ((n, D_MODEL), F32)]
    return _row_call(body, "mlp_fwd_loss", n, MM_T, [h, target], [norm_mlp, w_up, w_down], outs, [((8, 128), F32)])


def _mlp_bwd(dout, hn, h, norm_mlp, w_up, w_down):
    n = h.shape[0]

    def body(do_ref, hn_ref, h_ref, g_ref, wu_ref, wd_ref, hid_ref, da_ref, dh_ref, dg_ref):
        dout_t = do_ref[...]
        doutb = dout_t.astype(BF16)
        hb = hn_ref[...]
        dhn = jnp.zeros_like(dout_t)
        for j in range(N_DEV):
            cols = slice(j * FF_SHARD, (j + 1) * FF_SHARD)
            a = jnp.maximum(_dot(hb, wu_ref[j]), 0.0)
            hid_ref[:, cols] = (a * a).astype(BF16)
            da = (_dot_nt(doutb, wd_ref[j]) * (2.0 * a)).astype(BF16)
            da_ref[:, cols] = da
            dhn += _dot_nt(da, wu_ref[j])
        h_t = h_ref[...]
        inv = lax.rsqrt(jnp.sum(h_t * h_t, -1, keepdims=True) * (1.0 / D_MODEL) + EPS)
        dx, dg = _rms_bwd(dhn, h_t, g_ref[...], inv, D_MODEL)
        dh_ref[...] = dout_t + dx
        _acc(dg_ref, jnp.sum(dg, 0, keepdims=True))

    outs = [((n, D_FF), BF16), ((n, D_FF), BF16), ((n, D_MODEL), F32)]
    return _row_call(body, "mlp_bwd", n, MM_T, [dout, hn, h], [norm_mlp, w_up, w_down], outs, [((1, D_MODEL), F32)])


def _merge_bwd(dh, gs, gm, y_ssm, y_mla, w_out, w_o_mla):
    n = dh.shape[0]

    def body(dh_ref, gs_ref, gm_ref, ys_ref, ym_ref, wout_ref, wo_ref, dgs_ref, dgm_ref, dys_ref, dym_ref, dat_ref):
        dmix = _dot_nt(dh_ref[...].astype(BF16), wout_ref[...])
        sgs, sgm = _sigmoid(gs_ref[...]), _sigmoid(gm_ref[...])
        dgs_ref[...] = (dmix * ys_ref[...] * sgs * (1.0 - sgs)).astype(BF16)
        dgm_ref[...] = (dmix * ym_ref[...] * sgm * (1.0 - sgm)).astype(BF16)
        dys_ref[...] = (dmix * sgs).astype(BF16)
        dym = (dmix * sgm).astype(BF16)
        dym_ref[...] = dym
        dat_ref[...] = _dot_nt(dym, wo_ref[...])

    outs = [((n, D_MODEL), BF16)] * 4 + [((n, D_MODEL), F32)]
    return _row_call(body, "merge_bwd", n, MM_T, [dh, gs, gm, y_ssm, y_mla], [w_out, w_o_mla], outs)


def _attn_bwd(q, k, kt, v, out, lse, dout, xch):
    n = q.shape[1]
    t = min(ATT_T, n)
    nt = n // t

    def body(q_ref, k_ref, kt_ref, v_ref, o_ref, lse_ref, do_ref, dq_ref, dk_ref, dv_ref, delta_ref, dqt_ref):
        j = pl.program_id(1)

        @pl.when(j == 0)
        def _():
            dqt_ref[...] = jnp.zeros_like(dqt_ref)
            prod = do_ref[...] * o_ref[...]
            delta_ref[...] = lax.dot_general(jnp.ones((8, V_HEAD), F32), prod, (((1,), (1,)), ((), ())),
                                             precision=lax.Precision.HIGHEST, preferred_element_type=F32)

        k_t = k_ref[0]
        kt_t = kt_ref[0]
        v_t = v_ref[0]

        def q_tile(i, carry, diag):
            dk, dv = carry
            r0 = pl.multiple_of(i * t, t)
            rows = pl.ds(r0, t)
            qt = q_ref[0, rows, :]
            st = _dot_nt(k_t, qt)
            if diag:
                st = _causal_mask_t(st, t)
            pt = jnp.exp(st - lse_ref[0, :, rows])
            dob = do_ref[rows, :].astype(BF16)
            dv = dv + _dot(pt.astype(BF16), dob)
            dst = (pt * (_dot_nt(v_t, dob) - delta_ref[0:1, rows])).astype(BF16)
            dk = dk + _dot(dst, qt)
            dqt_ref[:, rows] += _dot(kt_t, dst)
            return dk, dv

        carry = q_tile(j, (jnp.zeros((t, QK_PAD), F32), jnp.zeros((t, V_HEAD), F32)), True)
        dk, dv = lax.fori_loop(j + 1, nt, functools.partial(q_tile, diag=False), carry)
        dk_ref[0] = dk
        dv_ref[0] = dv

        @pl.when(j == nt - 1)
        def _():
            for c in range(0, n, t):
                dq_ref[0, c:c + t, :] = dqt_ref[:, c:c + t].T

    return _call(
        body, "attn_bwd", (N_HEADS, nt), [q, k, kt, v, out, lse, dout],
        [pl.BlockSpec((1, n, QK_PAD), lambda h, j: (h, 0, 0)), pl.BlockSpec((1, t, QK_PAD), lambda h, j: (h, j, 0)),
         pl.BlockSpec((1, QK_PAD, t), lambda h, j: (h, 0, j)), pl.BlockSpec((1, t, V_HEAD), lambda h, j: (h, j, 0)),
         pl.BlockSpec((n, V_HEAD), lambda h, j: (0, h)), pl.BlockSpec((1, 1, n), lambda h, j: (h, 0, 0)),
         pl.BlockSpec((n, V_HEAD), lambda h, j: (0, h))],
        [_sds((N_HEADS, n, QK_PAD), F32), _sds((N_HEADS, n, QK_PAD), F32), _sds((N_HEADS, n, V_HEAD), F32)],
        [pl.BlockSpec((1, n, QK_PAD), lambda h, j: (h, 0, 0)), pl.BlockSpec((1, t, QK_PAD), lambda h, j: (h, j, 0)),
         pl.BlockSpec((1, t, V_HEAD), lambda h, j: (h, j, 0))],
        scratch=[pltpu.VMEM((8, n), F32), pltpu.VMEM((QK_PAD, n), F32)],
        xch=xch)


def _qkv_prep_bwd(ql, kvl, dq, dk, dv, q_a_norm, kv_a_norm, wq, wkv, gq, gk, cos_t, sin_t, xch):
    n = ql.shape[0]

    def body(ql_ref, kvl_ref, cos_ref, sin_ref, dq_ref, dk_ref, dv_ref, qa_ref, ka_ref, wq_ref, wkv_ref, gq_ref, gk_ref,
             dql_ref, dkvl_ref, qab_ref, dqp_ref, cab_ref, dkvp_ref, dqa_ref, dka_ref, dgq_ref, dgk_ref):
        cos_t, sin_t = cos_ref[...], sin_ref[...]
        ql_t = ql_ref[...]
        qa, inv_qa = _rms(ql_t, qa_ref[...], Q_LORA)
        qab = qa.astype(BF16)
        qab_ref[...] = qab
        kvl_t = kvl_ref[...]
        ckv = kvl_t[:, 0:KV_LORA]
        ca, inv_ca = _rms(ckv, ka_ref[...], KV_LORA)
        cab = ca.astype(BF16)
        cab_ref[...] = cab
        kpe = kvl_t[:, KV_LORA:KV_LAT_PAD]
        dgq = jnp.zeros((1, QK_PAD), F32)
        dgk = jnp.zeros((1, QK_PAD), F32)
        dkpe = jnp.zeros_like(kpe)
        q_pre = _dot(qab, wq_ref[...])
        kv_pre = _dot(cab, wkv_ref[...])
        for h in range(N_HEADS):
            head = slice(h * QK_PAD, (h + 1) * QK_PAD)
            q_slab = q_pre[:, head]
            inv = lax.rsqrt(jnp.sum(q_slab * q_slab, -1, keepdims=True) * (1.0 / QK_HEAD) + EPS)
            d_slab, dg = _head_norm_rope_bwd(dq_ref[h] * ATT_SCALE, q_slab, gq_ref[...], inv, cos_t, sin_t)
            dqp_ref[:, head] = d_slab.astype(BF16)
            dgq += jnp.sum(dg, 0, keepdims=True)
            k_slab = jnp.concatenate([kv_pre[:, h * QK_PAD:h * QK_PAD + QK_NOPE], kpe], axis=-1)
            inv = lax.rsqrt(jnp.sum(k_slab * k_slab, -1, keepdims=True) * (1.0 / QK_HEAD) + EPS)
            d_slab, dg = _head_norm_rope_bwd(dk_ref[h], k_slab, gk_ref[...], inv, cos_t, sin_t)
            dkvp_ref[:, head] = jnp.concatenate([d_slab[:, 0:QK_NOPE], dv_ref[h]], axis=-1).astype(BF16)
            dkpe += d_slab[:, QK_NOPE:QK_PAD]
            dgk += jnp.sum(dg, 0, keepdims=True)
        dqa = _dot_nt(dqp_ref[...], wq_ref[...])
        dx, dg = _rms_bwd(dqa, ql_t, qa_ref[...], inv_qa, Q_LORA)
        dql_ref[...] = dx.astype(BF16)
        _acc(dqa_ref, jnp.sum(dg, 0, keepdims=True))
        dca = _dot_nt(dkvp_ref[...], wkv_ref[...])
        dx, dg = _rms_bwd(dca, ckv, ka_ref[...], inv_ca, KV_LORA)
        dkvl_ref[:, 0:KV_LORA] = dx.astype(BF16)
        dkvl_ref[:, KV_LORA:KV_LAT_PAD] = dkpe.astype(BF16)
        _acc(dka_ref, jnp.sum(dg, 0, keepdims=True))
        _acc(dgq_ref, dgq)
        _acc(dgk_ref, dgk)

    row_outs = [((n, Q_LORA), BF16), ((n, KV_LAT_PAD), BF16), ((n, Q_LORA), BF16), ((n, N_HEADS * QK_PAD), BF16),
                ((n, KV_LORA), BF16), ((n, N_HEADS * (QK_NOPE + V_HEAD)), BF16)]
    acc_outs = [((1, Q_LORA), F32), ((1, KV_LORA), F32), ((1, QK_PAD), F32), ((1, QK_PAD), F32)]
    return _row_call(body, "qkv_prep_bwd", n, ROW_T, [ql, kvl, cos_t, sin_t, dq, dk, dv],
                     [q_a_norm, kv_a_norm, wq, wkv, gq, gk], row_outs, acc_outs, xch=xch)


def _glu_bwd(dy_ssm, y, w_glu, b_glu, w_o_ssm):
    n = y.shape[0]

    def body(dys_ref, y_ref, wg_ref, bg_ref, wo_ref, dy_ref, zg_ref, z_ref, dt_ref, db_ref):
        y_t = y_ref[...]
        z, th = _gelu(y_t)
        zb = z.astype(BF16)
        z_ref[...] = zb
        s = _sigmoid(_dot(zb, wg_ref[...]) + bg_ref[...])
        zg_ref[...] = (z * s).astype(BF16)
        dys = dys_ref[...]
        dzg = jnp.zeros_like(y_t)
        for j in range(N_DEV):
            dzg += _dot_nt(dys[:, j * OUT_SHARD:(j + 1) * OUT_SHARD], wo_ref[j])
        dt = dzg * z * s * (1.0 - s)
        dtb = dt.astype(BF16)
        dt_ref[...] = dtb
        dz = dzg * s + _dot_nt(dtb, wg_ref[...])
        dy_ref[...] = dz * _gelu_grad(y_t, th)
        _acc(db_ref, jnp.sum(dt, 0, keepdims=True))

    outs = [((n, SSM_WIDTH), F32)] + [((n, SSM_WIDTH), BF16)] * 3
    return _row_call(body, "glu_bwd", n, ROW_T, [dy_ssm, y], [w_glu, b_glu, w_o_ssm], outs, [((1, SSM_WIDTH), F32)])


def _ssm_bwd(u, dy, st, bblk, cblk, lam, d_row, xch):
    n = u.shape[0]
    t = min(SCAN_T, n)
    nc = n // t
    kb = 512
    perm = _perm_matrix(t)

    def body(u_ref, dy_ref, st_ref, p_ref, bblk_ref, cblk_ref, lam_ref, d_ref,
             du_ref, dlam_ref, dd_ref, db_ref, dct_ref,
             buf_x, buf_a, pw_ref, carry_ref, xcarry_ref, sx_ref, sa_ref, db_acc, dct_acc):
        @pl.when(pl.program_id(0) == 0)
        def _():
            carry_ref[...] = jnp.zeros_like(carry_ref)
            db_acc[...] = jnp.zeros_like(db_acc)
            dct_acc[...] = jnp.zeros_like(dct_acc)
            _power_table(lam_ref, pw_ref, t // SUBCHUNKS)

        u_t = u_ref[...]
        dy_t = dy_ref[...]
        p = p_ref[...]
        ub = _dot(p, u_t.astype(BF16)).astype(BF16)
        dyb = _dot(p, dy_t.astype(BF16)).astype(BF16)
        _to_states(ub, bblk_ref, buf_x, False)
        xcarry_ref[...] = st_ref[0]
        _run_scan(buf_x, lam_ref, t, False)
        _run_carries(buf_x, pw_ref, xcarry_ref, sx_ref, t, False)
        _run_fix(buf_x, pw_ref, sx_ref, t, False)
        _to_states(dyb, cblk_ref, buf_a, True)
        _run_scan(buf_a, lam_ref, t, True)
        _run_carries(buf_a, pw_ref, carry_ref, sa_ref, t, True)
        _run_fix(buf_a, pw_ref, sa_ref, t, True)
        du_ref[...] = (d_ref[...] * dy_t + _unpermute(p, _to_channels(buf_a, bblk_ref, True))).astype(BF16)
        for b in range(STATE_BLOCKS):
            lanes, ch = _state_block(b)
            db_acc[ch, lanes] += _dot_tn(ub[:, ch], buf_a[:, lanes].astype(BF16))
            dct_acc[ch, lanes] += _dot_tn(dyb[:, ch], buf_x[:, lanes].astype(BF16))
        for c in range(0, N_STATE, kb):
            re, im = pl.ds(c, kb), pl.ds(N_STATE + c, kb)
            xr, xi = buf_x[pl.ds(0, t - 8), re], buf_x[pl.ds(0, t - 8), im]
            ar, ai = buf_a[pl.ds(8, t - 8), re], buf_a[pl.ds(8, t - 8), im]
            x0r, x0i = sx_ref[:, re], sx_ref[:, im]
            a0r, a0i = buf_a[0:8, re], buf_a[0:8, im]
            dlam_part_re = (jnp.sum(ar * xr + ai * xi, 0, keepdims=True)
                            + jnp.sum(a0r * x0r + a0i * x0i, 0, keepdims=True))
            dlam_part_im = (jnp.sum(ai * xr - ar * xi, 0, keepdims=True)
                            + jnp.sum(a0i * x0r - a0r * x0i, 0, keepdims=True))

            @pl.when(pl.program_id(0) == 0)
            def _(c=c):
                dlam_ref[0:1, c:c + kb] = jnp.zeros((1, kb), F32)
                dlam_ref[1:2, c:c + kb] = jnp.zeros((1, kb), F32)

            dlam_ref[0:1, c:c + kb] += dlam_part_re
            dlam_ref[1:2, c:c + kb] += dlam_part_im
        _acc(dd_ref, jnp.sum(dy_t * u_t, 0, keepdims=True))

        @pl.when(pl.program_id(0) == nc - 1)
        def _():
            pltpu.sync_copy(db_acc, db_ref)
            pltpu.sync_copy(dct_acc, dct_ref)

    rev = lambda i: (nc - 1 - i, 0)
    consts = [perm, bblk, cblk, lam, d_row]
    wide = (SSM_WIDTH, 2 * N_STATE)
    return _call(
        body, "ssm_bwd", (nc,), [u, dy, st] + consts,
        [pl.BlockSpec((t, SSM_WIDTH), rev), pl.BlockSpec((t, SSM_WIDTH), rev),
         pl.BlockSpec((1, 8, 2 * N_STATE), lambda i: (nc - 1 - i, 0, 0))] + [_const(a) for a in consts],
        [_sds((n, SSM_WIDTH), BF16), _sds((2, N_STATE), F32), _sds((1, SSM_WIDTH), F32), _sds(wide, F32), _sds(wide, F32)],
        [pl.BlockSpec((t, SSM_WIDTH), rev), pl.BlockSpec((2, N_STATE), lambda i: (0, 0)),
         pl.BlockSpec((1, SSM_WIDTH), lambda i: (0, 0)), ANY, ANY],
        scratch=[pltpu.VMEM((t, 2 * N_STATE), F32)] * 2 + [pltpu.VMEM((t // SUBCHUNKS, 2 * N_STATE), F32)]
        + [pltpu.VMEM((8, 2 * N_STATE), F32)] * 4 + [pltpu.VMEM(wide, F32)] * 2,
        xch=xch)


def _in_proj_bwd(pieces, dh, x, norm_mix, w_in_pad):
    n = x.shape[0]

    def body(du_ref, dql_ref, dkvl_ref, dgs_ref, dgm_ref, dh_ref, x_ref, g_ref, w_ref, dx_ref, dp_ref, dg_ref):
        dxn = jnp.zeros((dh_ref.shape[0], D_MODEL), F32)
        for ref, (a, b) in zip((du_ref, dql_ref, dkvl_ref, dgs_ref, dgm_ref), IN_SEGS):
            piece = ref[...]
            dp_ref[:, a:b] = piece
            dxn += _dot_nt(piece, w_ref[:, a:b])
        x_t = x_ref[...]
        inv = lax.rsqrt(jnp.sum(x_t * x_t, -1, keepdims=True) * (1.0 / D_MODEL) + EPS)
        dx, dg = _rms_bwd(dxn, x_t, g_ref[...], inv, D_MODEL)
        dx_ref[...] = dh_ref[...] + dx
        _acc(dg_ref, jnp.sum(dg, 0, keepdims=True))

    outs = [((n, D_MODEL), F32), ((n, D_IN_PAD), BF16)]
    return _row_call(body, "in_proj_bwd", n, MM_T, list(pieces) + [dh, x], [norm_mix, w_in_pad], outs,
                     [((1, D_MODEL), F32)])


def _swap_minor(a):
    g, r, c = a.shape[1:]
    return jnp.transpose(a[0], (0, 2, 1)).reshape(g * c, r)


def _pad_in(w):
    return jnp.concatenate([w[:, :KV_END], jnp.zeros((w.shape[0], D_IN_PAD - D_IN), w.dtype), w[:, KV_END:]], axis=1)


def _unpad_in(w):
    return jnp.concatenate([w[:, :KV_END], w[:, KV_END + D_IN_PAD - D_IN:]], axis=1)


def _pad_gain(g):
    return jnp.pad(g, ((0, 0), (0, QK_PAD - QK_HEAD)))


def _place():
    x, y, c = lax.axis_index("x"), lax.axis_index("y"), lax.axis_index("c")
    chips = [(x, y), (1 - x, y), (x, 1 - y), (1 - x, 1 - y)]
    return x, y, c, chips


def _all_gather(block, name):
    rows, lanes = block.shape

    def body(x_ref, out_ref, send_sems, recv_sems, local_sem):
        x, y, c, chips = _place()
        me, sibling = (x, y, c), (x, y, 1 - c)

        def slot(px, py, pc):
            return out_ref.at[4 * px + 2 * py + pc]

        def copy(k, blk, to, src=None):
            return pltpu.make_async_remote_copy(
                src_ref=slot(*blk) if src is None else src, dst_ref=slot(*blk),
                send_sem=send_sems.at[k], recv_sem=recv_sems.at[k], device_id=to, device_id_type=MESH)

        mine = pltpu.make_async_copy(x_ref, slot(*me), local_sem)
        mine.start()
        first = [copy(0, me, sibling, src=x_ref)]
        first += [copy(1 + j, me, (*chip, c), src=x_ref) for j, chip in enumerate(chips[1:])]
        for cp in first:
            cp.start()
        passed = [copy(4 + j, (*chip, c), sibling) for j, chip in enumerate(chips[1:])]
        for j, chip in enumerate(chips[1:]):
            copy(1 + j, (*chip, c), me).wait_recv()
            passed[j].start()
        copy(0, sibling, me).wait_recv()
        for j, chip in enumerate(chips[1:]):
            copy(4 + j, (*chip, 1 - c), me).wait_recv()
        for cp in first + passed:
            cp.wait_send()
        mine.wait()

    return pl.pallas_call(
        body,
        name=name,
        in_specs=[ANY],
        out_specs=ANY,
        out_shape=_sds((N_DEV, rows, lanes), block.dtype),
        scratch_shapes=[pltpu.SemaphoreType.DMA((7,)), pltpu.SemaphoreType.DMA((7,)), pltpu.SemaphoreType.DMA],
    )(block)


def _reduce_scatter(parts, name):
    _, rows, lanes = parts.shape

    def body(p_ref, out_ref, own, land_a, send_b, land_b, sa, ra, sb, rb, lo):
        x, y, c, chips = _place()
        sibling = (x, y, 1 - c)

        def blk(chip, core):
            return p_ref.at[4 * chip[0] + 2 * chip[1] + core]

        to_sib = [pltpu.make_async_remote_copy(
            src_ref=blk(chips[k], 1 - c), dst_ref=land_a.at[k], send_sem=sa.at[k], recv_sem=ra.at[k],
            device_id=sibling, device_id_type=MESH) for k in range(4)]
        for cp in to_sib:
            cp.start()
        loads = [pltpu.make_async_copy(blk(chips[k], c), own.at[k], lo.at[k]) for k in range(4)]
        for cp in loads:
            cp.start()
        to_chip = [pltpu.make_async_remote_copy(
            src_ref=send_b.at[j], dst_ref=land_b.at[j], send_sem=sb.at[j], recv_sem=rb.at[j],
            device_id=(*chips[1 + j], c), device_id_type=MESH) for j in range(3)]
        for k in (1, 2, 3):
            to_sib[k].wait_recv()
            loads[k].wait()
            send_b[k - 1] = (own[k] + land_a[k]).astype(BF16)
            to_chip[k - 1].start()
        to_sib[0].wait_recv()
        loads[0].wait()
        acc = own[0] + land_a[0]
        for j in range(3):
            to_chip[j].wait_recv()
            acc = acc + land_b[j].astype(F32)
        out_ref[...] = acc
        for cp in to_sib + to_chip:
            cp.wait_send()

    return pl.pallas_call(
        body,
        name=name,
        in_specs=[ANY],
        out_specs=pl.BlockSpec(memory_space=pltpu.VMEM),
        out_shape=_sds((rows, lanes), F32),
        scratch_shapes=[pltpu.VMEM((4, rows, lanes), F32), pltpu.VMEM((4, rows, lanes), F32),
                        pltpu.VMEM((3, rows, lanes), BF16), pltpu.VMEM((3, rows, lanes), BF16)]
        + [pltpu.SemaphoreType.DMA((4,))] * 2 + [pltpu.SemaphoreType.DMA((3,))] * 2 + [pltpu.SemaphoreType.DMA((4,))],
        compiler_params=_params(),
    )(parts)


def _adamw_math(w, g, m, v):
    m = ADAM_B1 * m + (1.0 - ADAM_B1) * g
    v = ADAM_B2 * v + (1.0 - ADAM_B2) * (g * g)
    m_hat = m / (1.0 - ADAM_B1 ** ADAM_STEP)
    v_hat = v / (1.0 - ADAM_B2 ** ADAM_STEP)
    delta = -ADAM_LR * (m_hat / (jnp.sqrt(v_hat) + ADAM_EPS) + ADAM_WD * w)
    return delta, m, v


def _row_tile(r):
    return max(t for t in range(8, min(r, 256) + 1, 8) if r % t == 0)


def _adamw(w, g, m, v, name):
    r, n = w.shape

    def body(w_ref, g_ref, m_ref, v_ref, d_ref, nm_ref, nv_ref):
        d_ref[...], nm_ref[...], nv_ref[...] = _adamw_math(w_ref[...], g_ref[...], m_ref[...], v_ref[...])

    return _row_call(body, name, r, _row_tile(r), [w, g, m, v], [], [((r, n), F32)] * 3)


def _adamw_sum(landed, w, m, v, name):
    r, n = w.shape

    def body(l_ref, w_ref, m_ref, v_ref, g_ref, d_ref, nm_ref, nv_ref):
        g = l_ref[0].astype(F32)
        for dev in range(1, N_DEV):
            g = g + l_ref[dev].astype(F32)
        g_ref[...] = g
        d_ref[...], nm_ref[...], nv_ref[...] = _adamw_math(w_ref[...], g, m_ref[...], v_ref[...])

    tm = max(t for t in range(16, min(r, 256) + 1, 16) if r % t == 0)
    return _row_call(body, name, r, tm, [landed, w, m, v], [], [((r, n), F32)] * 4)


def _adamw_small(gathered, w, m, v, row_counts):
    n_rows = w.shape[0]

    def body(ga_ref, w_ref, m_ref, v_ref, loss_ref, *out_refs):
        g = ga_ref[0]
        for dev in range(1, N_DEV):
            g = g + ga_ref[dev]
        loss_ref[...] = g[n_rows:n_rows + 8]
        g = g[0:n_rows]
        d, nm, nv = _adamw_math(w_ref[...], g, m_ref[...], v_ref[...])
        off = 0
        for p, rows in enumerate(row_counts):
            for k, val in enumerate((g, d, nm, nv)):
                out_refs[4 * p + k][...] = val[off:off + rows]
            off += rows

    outs = [_sds((8, LANES), F32)] + [_sds((rows, LANES), F32) for rows in row_counts for _ in range(4)]
    return pl.pallas_call(body, name="adamw_small", out_shape=outs, compiler_params=_params())(gathered, w, m, v)


SMALL = ("norm_mix", "q_a_norm", "kv_a_norm", "q_norm", "k_norm", "ssm_a_re", "ssm_a_im", "ssm_log_dt", "ssm_b_re",
         "ssm_b_im", "ssm_c_re", "ssm_c_im", "ssm_d", "b_glu", "norm_mlp")
WEIGHT_ORDER = ("norm_mix", "w_in", "q_a_norm", "kv_a_norm", "w_q_b", "w_kv_b", "q_norm", "k_norm", "w_o_mla",
                "ssm_a_re", "ssm_a_im", "ssm_log_dt", "ssm_b_re", "ssm_b_im", "ssm_c_re", "ssm_c_im", "ssm_d", "w_glu",
                "b_glu", "w_o_ssm", "w_out", "norm_mlp", "w_up", "w_down")
IN_SHARD = D_IN // N_DEV
Q_SHARD = QK_HEAD


def _pack_small(vals):
    parts = []
    for n in SMALL:
        flat = vals[n].reshape(-1)
        size = -(-flat.shape[0] // (8 * LANES)) * 8 * LANES
        parts.append(jnp.pad(flat, (0, size - flat.shape[0])).reshape(-1, LANES))
    return jnp.concatenate(parts, axis=0)


def _small_rows(like):
    return [-(-like[n].size // (8 * LANES)) * 8 for n in SMALL]


def _step(x, pos_col, target, w, small):
    bf = {n: a.astype(BF16) for n, a in w.items()}
    gq, gk = _pad_gain(small["q_norm"]), _pad_gain(small["k_norm"])
    a_re = small["ssm_a_re"].reshape(1, N_STATE)
    a_im = small["ssm_a_im"].reshape(1, N_STATE)
    log_dt = jnp.repeat(small["ssm_log_dt"].reshape(SSM_GROUPS), SSM_STATE).reshape(1, N_STATE)
    bt_re, bt_im = _swap_minor(small["ssm_b_re"]), _swap_minor(small["ssm_b_im"])
    c2_re, c2_im = _swap_minor(small["ssm_c_re"]), _swap_minor(small["ssm_c_im"])
    d_row = small["ssm_d"].reshape(1, SSM_WIDTH)

    w_in_all = _all_gather(bf["w_in"], "gather_w_in")
    w_in_pad = _pad_in(jnp.transpose(w_in_all, (1, 0, 2)).reshape(D_MODEL, D_IN))
    cos_t, sin_t = _rope_tables(pos_col)
    lam, bblk, cblk = _ssm_prep(a_re, a_im, log_dt, bt_re, bt_im, c2_re, c2_im)
    wq_mine = jnp.pad(bf["w_q_b"], ((0, 0), (0, QK_PAD - QK_HEAD)))
    xn, u, ql, kvl, gs, gm, w_glu, w_o_ssm = _in_proj(
        x, small["norm_mix"], w_in_pad, xch=[(bf["w_glu"], False), (bf["w_o_ssm"], False)])
    w_glu = w_glu.reshape(SSM_WIDTH, SSM_WIDTH)
    y, y_ssm, st, wq, wkv, w_o_mla, w_out = _ssm_fwd(
        u, bblk, cblk, lam, d_row, w_glu, small["b_glu"], w_o_ssm,
        xch=[(wq_mine, False), (bf["w_kv_b"], False), (bf["w_o_mla"], False), (bf["w_out"], False)])
    w_o_mla, w_out = w_o_mla.reshape(D_MODEL, D_MODEL), w_out.reshape(D_MODEL, D_MODEL)
    wq = jnp.transpose(wq, (1, 0, 2)).reshape(Q_LORA, N_HEADS * QK_PAD)
    wkv = jnp.transpose(wkv, (1, 0, 2)).reshape(KV_LORA, N_HEADS * QK_PAD)
    q, k, v, kt, vt = _qkv_prep(ql, kvl, small["q_a_norm"], small["kv_a_norm"], wq, wkv, gq, gk, cos_t, sin_t)
    attn, lse, w_up, w_down = _attn_fwd(q, k, vt, xch=[(bf["w_up"], False), (bf["w_down"], False)])
    h, mixed, y_mla = _merge(attn, gs, gm, y_ssm, x, w_o_mla, w_out)
    hn, dout, loss = _mlp_fwd_loss(h, target, small["norm_mlp"], w_up, w_down)

    hid, da, dh, d_norm_mlp = _mlp_bwd(dout, hn, h, small["norm_mlp"], w_up, w_down)
    p_w_down = _matmul_tn_shards(hid, dout, "dw_down", False)
    p_w_up = _matmul_tn_shards(hn, da, "dw_up", True)
    dgs, dgm, dy_ssm, dy_mla, dattn = _merge_bwd(dh, gs, gm, y_ssm, y_mla, w_out, w_o_mla)
    p_w_out = _matmul_tn_shards(mixed, dh, "dw_out", False)
    p_w_o_mla = _matmul_tn_shards(attn, dy_mla, "dw_o_mla", False)
    dq, dk, dv, l_w_up, l_w_down, l_w_out, l_w_o_mla = _attn_bwd(
        q, k, kt, v, attn, lse, dattn, xch=[(p_w_up, True), (p_w_down, True), (p_w_out, True), (p_w_o_mla, True)])
    dql, dkvl, qa, dq_pre, ca, dkv_pre, d_q_a_norm, d_kv_a_norm, d_gq, d_gk = _qkv_prep_bwd(
        ql, kvl, dq, dk, dv, small["q_a_norm"], small["kv_a_norm"], wq, wkv, gq, gk, cos_t, sin_t, xch=[])
    p_wq = _matmul_tn_shards(qa, dq_pre, "dw_q_b", True)
    p_wkv = _matmul_tn_shards(ca, dkv_pre, "dw_kv_b", True)
    dy, zg, z, dt, d_b_glu = _glu_bwd(dy_ssm, y, w_glu, small["b_glu"], w_o_ssm)
    p_w_o_ssm = _matmul_tn_shards(zg, dy_ssm, "dw_o_ssm", True)
    p_w_glu = _matmul_tn_shards(z, dt, "dw_glu", False)
    du, dlam, d_d, d_bblk, d_cblk_t, l_wq, l_wkv, l_w_glu, l_w_o_ssm = _ssm_bwd(
        u, dy, st, bblk, cblk, lam, d_row, xch=[(p_wq, True), (p_wkv, True), (p_w_glu, True), (p_w_o_ssm, True)])
    d_a_re, d_a_im, d_log_dt, d_bt_re, d_bt_im, d_c_re, d_c_im = _ssm_prep_bwd(
        a_re, a_im, log_dt, bt_re, bt_im, dlam, d_bblk, d_cblk_t)
    dx, dproj, d_norm_mix = _in_proj_bwd((du, dql, dkvl, dgs, dgm), dh, x, small["norm_mix"], w_in_pad)
    tr = lambda mat: jnp.transpose(mat.reshape(SSM_GROUPS, SSM_GROUP_CH, SSM_STATE), (0, 2, 1))
    g_small = {
        "norm_mix": d_norm_mix, "q_a_norm": d_q_a_norm, "kv_a_norm": d_kv_a_norm,
        "q_norm": d_gq[:, :QK_HEAD], "k_norm": d_gk[:, :QK_HEAD],
        "ssm_a_re": d_a_re, "ssm_a_im": d_a_im, "ssm_log_dt": d_log_dt,
        "ssm_b_re": tr(d_bt_re), "ssm_b_im": tr(d_bt_im), "ssm_c_re": d_c_re, "ssm_c_im": d_c_im,
        "ssm_d": d_d, "b_glu": d_b_glu, "norm_mlp": d_norm_mlp,
    }
    g_w_in_pad, g_small_all = _matmul_tn(
        xn, dproj, "dw_in", xch=[(jnp.concatenate([_pack_small(g_small), loss], axis=0), False)])
    parts = jnp.transpose(_unpad_in(g_w_in_pad).reshape(D_MODEL, N_DEV, IN_SHARD), (1, 0, 2))
    g_w_in_mine = _reduce_scatter(parts, "reduce_w_in")
    landed = {"w_q_b": l_wq[:, :, :QK_HEAD], "w_kv_b": l_wkv, "w_o_mla": l_w_o_mla, "w_glu": l_w_glu,
              "w_o_ssm": l_w_o_ssm, "w_out": l_w_out, "w_up": l_w_up, "w_down": l_w_down}
    return dx, landed, g_w_in_mine, g_small_all


def kernel(x, positions, norm_mix, w_in, q_a_norm, kv_a_norm, w_q_b, w_kv_b, q_norm, k_norm, w_o_mla, ssm_a_re, ssm_a_im, ssm_log_dt, ssm_b_re, ssm_b_im, ssm_c_re, ssm_c_im, ssm_d, w_glu, b_glu, w_o_ssm, w_out, norm_mlp, w_up, w_down, loss_target, m_norm_mix, m_w_in, m_q_a_norm, m_kv_a_norm, m_w_q_b, m_w_kv_b, m_q_norm, m_k_norm, m_w_o_mla, m_ssm_a_re, m_ssm_a_im, m_ssm_log_dt, m_ssm_b_re, m_ssm_b_im, m_ssm_c_re, m_ssm_c_im, m_ssm_d, m_w_glu, m_b_glu, m_w_o_ssm, m_w_out, m_norm_mlp, m_w_up, m_w_down, v_norm_mix, v_w_in, v_q_a_norm, v_kv_a_norm, v_w_q_b, v_w_kv_b, v_q_norm, v_k_norm, v_w_o_mla, v_ssm_a_re, v_ssm_a_im, v_ssm_log_dt, v_ssm_b_re, v_ssm_b_im, v_ssm_c_re, v_ssm_c_im, v_ssm_d, v_w_glu, v_b_glu, v_w_o_ssm, v_w_out, v_norm_mlp, v_w_up, v_w_down):
    given = dict(locals())
    w = {n: given[n] for n in WEIGHT_ORDER}
    m = {n: given["m_" + n] for n in WEIGHT_ORDER}
    v = {n: given["v_" + n] for n in WEIGHT_ORDER}
    big = [n for n in WEIGHT_ORDER if n not in SMALL]
    small = {n: w[n] for n in SMALL}

    dx, landed, g_w_in, g_small_all = _step(
        x[0], positions.reshape(-1, 1), loss_target[0], {n: w[n][0] for n in big}, small)

    grads, deltas, new_m, new_v = {}, {}, {}, {}
    for n in big:
        if n == "w_in":
            g = g_w_in
            d, nm, nv = _adamw(w[n][0], g, m[n][0], v[n][0], "adamw_" + n)
        else:
            g, d, nm, nv = _adamw_sum(landed[n], w[n][0], m[n][0], v[n][0], "adamw_" + n)
        grads[n], deltas[n], new_m[n], new_v[n] = g[None], d[None], nm[None], nv[None]

    outs = _adamw_small(g_small_all, _pack_small(small), _pack_small({n: m[n] for n in SMALL}),
                        _pack_small({n: v[n] for n in SMALL}), _small_rows(small))
    for p, n in enumerate(SMALL):
        for k, dst in enumerate((grads, deltas, new_m, new_v)):
            dst[n] = outs[1 + 4 * p + k].reshape(-1)[:small[n].size].reshape(small[n].shape)

    return (outs[0][0, 0], dx[None], *[grads[n] for n in WEIGHT_ORDER], *[deltas[n] for n in WEIGHT_ORDER],
            *[new_m[n] for n in WEIGHT_ORDER], *[new_v[n] for n in WEIGHT_ORDER])
```

```python
import functools
import math

import numpy as np
import jax
import jax.numpy as jnp
from jax import lax
from jax.experimental import pallas as pl
from jax.experimental.pallas import tpu as pltpu

F32 = jnp.float32
BF16 = jnp.bfloat16

D_MODEL = 1024
SSM_GROUPS = 32
SSM_GROUP_CH = 16
SSM_WIDTH = 512
SSM_STATE = 64
N_STATE = SSM_GROUPS * SSM_STATE
N_HEADS = 8
QK_NOPE = 128
QK_ROPE = 64
QK_HEAD = 192
QK_PAD = 256
V_HEAD = 128
Q_LORA = 384
KV_LORA = 256
KV_LAT_PAD = 384
ROPE_THETA = 10000.0
D_FF = 4096
EPS = 1e-6
ATT_SCALE = QK_HEAD ** -0.5
N_DEV = 8
FF_SHARD = D_FF // N_DEV
OUT_SHARD = D_MODEL // N_DEV

IN_SEGS = ((0, 512), (512, 896), (896, 1280), (1280, 2304), (2304, 3328))
D_IN = 3264
D_IN_PAD = 3328
KV_END = 1216

ADAM_LR = 0.001
ADAM_B1 = 0.9
ADAM_B2 = 0.999
ADAM_EPS = 1e-08
ADAM_WD = 0.01
ADAM_STEP = 10

VMEM_LIMIT = 56 * 1024 * 1024
MESH = pl.DeviceIdType.MESH
ANY = pl.BlockSpec(memory_space=pl.ANY)
LANES = 128

SCAN_T = 256
SUBCHUNKS = 8
SCAN_CG = 512
ATT_T = 512
ATT_SUB = 2
ROW_T = 256
MM_T = 512


def _params(sem=None):
    return pltpu.CompilerParams(dimension_semantics=sem, vmem_limit_bytes=VMEM_LIMIT)


def _rows(arr, tm):
    if arr.ndim == 2:
        return pl.BlockSpec((tm, arr.shape[1]), lambda i: (i, 0))
    return pl.BlockSpec((arr.shape[0], tm, arr.shape[2]), lambda i: (0, i, 0))


def _const(arr):
    nd = arr.ndim
    return pl.BlockSpec(arr.shape, lambda *_: (0,) * nd, pipeline_mode=pl.Buffered(1))


def _sds(shape, dtype):
    return jax.ShapeDtypeStruct(shape, dtype)


PEERS = tuple((dx, dy, dc) for dx in (0, 1) for dy in (0, 1) for dc in (0, 1) if (dx, dy, dc) != (0, 0, 0))


def _here():
    x, y, c = lax.axis_index("x"), lax.axis_index("y"), lax.axis_index("c")
    return x, y, c, 4 * x + 2 * y + c


def _xchg_start(scatter, srcs, dsts, send, recv, local):
    x, y, c, me = _here()
    for e, sc in enumerate(scatter):
        src, dst = srcs[e], dsts[e]
        pltpu.make_async_copy(src.at[me] if sc else src, dst.at[me], local.at[e]).start()
        for dx, dy, dc in PEERS:
            px, py, pc = (1 - x if dx else x), (1 - y if dy else y), (1 - c if dc else c)
            pltpu.make_async_remote_copy(
                src_ref=src.at[4 * px + 2 * py + pc] if sc else src, dst_ref=dst.at[me],
                send_sem=send.at[e], recv_sem=recv.at[e], device_id=(px, py, pc), device_id_type=MESH).start()


def _xchg_wait(scatter, srcs, dsts, send, recv, local):
    x, y, c, me = _here()
    for e, sc in enumerate(scatter):
        src, dst = srcs[e], dsts[e]
        pltpu.make_async_copy(src.at[me] if sc else src, dst.at[me], local.at[e]).wait()
        span = dst.at[pl.ds(0, N_DEV - 1)]
        both = pltpu.make_async_remote_copy(src_ref=span, dst_ref=span, send_sem=send.at[e], recv_sem=recv.at[e],
                                            device_id=(x, y, c), device_id_type=MESH)
        both.wait_send()
        both.wait_recv()


def _call(body, name, grid, ins, in_specs, outs, out_specs, scratch=(), xch=()):
    n_in, n_out, ne = len(ins), len(outs), len(xch)
    scatter = [sc for _, sc in xch]
    x_outs = [_sds((N_DEV,) + (a.shape[1:] if sc else a.shape), a.dtype) for a, sc in xch]
    sems = [pltpu.SemaphoreType.DMA((ne,))] * 3 if ne else []

    def wrapped(*refs):
        in_refs, x_src = refs[:n_in], refs[n_in:n_in + ne]
        out_refs = refs[n_in + ne:n_in + ne + n_out]
        x_dst = refs[n_in + ne + n_out:n_in + 2 * ne + n_out]
        rest = refs[n_in + 2 * ne + n_out:]
        if ne:
            x_sems, rest = rest[len(rest) - 3:], rest[:len(rest) - 3]
            first = functools.reduce(jnp.logical_and, [pl.program_id(d) == 0 for d in range(len(grid))])
            last = functools.reduce(jnp.logical_and, [pl.program_id(d) == grid[d] - 1 for d in range(len(grid))])

            @pl.when(first)
            def _():
                _xchg_start(scatter, x_src, x_dst, *x_sems)

        body(*in_refs, *out_refs, *rest)
        if ne:
            @pl.when(last)
            def _():
                _xchg_wait(scatter, x_src, x_dst, *x_sems)

    return pl.pallas_call(
        wrapped,
        name=name,
        grid=grid,
        in_specs=list(in_specs) + [ANY] * ne,
        out_specs=list(out_specs) + [ANY] * ne,
        out_shape=list(outs) + x_outs,
        scratch_shapes=list(scratch) + sems,
        compiler_params=_params(("arbitrary",) * len(grid)),
    )(*ins, *[a for a, _ in xch])


def _row_call(body, name, n_rows, tm, row_ins, const_ins, row_outs, acc_outs=(), xch=(), col_outs=()):
    outs = [_sds(s, d) for s, d in list(row_outs) + list(col_outs) + list(acc_outs)]
    n_row, n_col = len(row_outs), len(col_outs)
    out_specs = [_rows(o, tm) for o in outs[:n_row]] + [
        pl.BlockSpec((o.shape[0], tm), lambda i: (0, i)) for o in outs[n_row:n_row + n_col]] + [
        pl.BlockSpec(o.shape, lambda i, nd=len(o.shape): (0,) * nd) for o in outs[n_row + n_col:]]
    in_specs = [_rows(a, tm) for a in row_ins] + [_const(a) for a in const_ins]
    return _call(body, name, (n_rows // tm,), list(row_ins) + list(const_ins), in_specs, outs, out_specs, xch=xch)


def _dot(a, b):
    return jnp.dot(a, b, preferred_element_type=F32)


def _dot_nt(a, b):
    return lax.dot_general(a, b, (((1,), (1,)), ((), ())), preferred_element_type=F32)


def _dot_tn(a, b):
    return lax.dot_general(a, b, (((0,), (0,)), ((), ())), preferred_element_type=F32)


def _rms(x, g, n):
    inv = lax.rsqrt(jnp.sum(x * x, -1, keepdims=True) * (1.0 / n) + EPS)
    return x * inv * g, inv


def _rms_bwd(dy, x, g, inv, n):
    xh = x * inv
    dxh = dy * g
    dx = inv * (dxh - xh * (jnp.sum(dxh * xh, -1, keepdims=True) * (1.0 / n)))
    return dx, dy * xh


def _sigmoid(x):
    return 1.0 / (1.0 + jnp.exp(-x))


_GELU_C = math.sqrt(2.0 / math.pi)


def _gelu(y):
    th = jnp.tanh(_GELU_C * (y + 0.044715 * (y * y * y)))
    return 0.5 * y * (1.0 + th), th


def _gelu_grad(y, th):
    return 0.5 * (1.0 + th) + 0.5 * y * (1.0 - th * th) * (_GELU_C * (1.0 + 3.0 * 0.044715 * (y * y)))


def _acc(ref, val):
    @pl.when(pl.program_id(0) == 0)
    def _():
        ref[...] = jnp.zeros_like(ref)

    ref[...] += val


def _tile(n, limit):
    if n <= limit:
        return n
    return max(t for t in range(128, limit + 1, 128) if n % t == 0)


def _lhs(a, turned, tm, tk):
    m, k_dim = a.shape if turned else a.shape[::-1]
    tm, tk = _tile(m, tm), _tile(k_dim, tk)
    if turned:
        return m, k_dim, tm, tk, pl.BlockSpec((tm, tk), lambda i, k: (i, k)), _dot
    return m, k_dim, tm, tk, pl.BlockSpec((tk, tm), lambda i, k: (k, i)), _dot_tn


def _matmul_tn(a, b, name, tm=512, tk=512, xch=(), turned=False):
    m, k_dim, tm, tk, a_spec, dot = _lhs(a, turned, tm, tk)
    n = b.shape[1]

    def body(a_ref, b_ref, o_ref):
        @pl.when(pl.program_id(1) == 0)
        def _():
            o_ref[...] = jnp.zeros_like(o_ref)

        o_ref[...] += dot(a_ref[...].astype(BF16), b_ref[...].astype(BF16))

    outs = _call(
        body, name, (m // tm, k_dim // tk), [a, b], [a_spec, pl.BlockSpec((tk, n), lambda i, k: (k, 0))],
        [_sds((m, n), F32)], [pl.BlockSpec((tm, n), lambda i, k: (i, 0))], xch=xch)
    return outs if xch else outs[0]


def _matmul_tn_shards(a, b, name, by_col, tm=512, tk=512, turned=False):
    m, k_dim, tm, tk, a_spec, dot = _lhs(a, turned, tm, tk)
    n = b.shape[1]
    nk = k_dim // tk
    if by_col:
        r, c = m, n // N_DEV
        out_spec = pl.BlockSpec((N_DEV, tm, c), lambda i, k: (0, i, 0))
    else:
        r, c = m // N_DEV, n
        per = tm // r
        out_spec = pl.BlockSpec((per, r, c), lambda i, k: (i, 0, 0))

    def body(a_ref, b_ref, o_ref, acc_ref):
        k = pl.program_id(1)

        @pl.when(k == 0)
        def _():
            acc_ref[...] = jnp.zeros_like(acc_ref)

        acc_ref[...] += dot(a_ref[...].astype(BF16), b_ref[...].astype(BF16))

        @pl.when(k == nk - 1)
        def _():
            if by_col:
                for j in range(N_DEV):
                    o_ref[j] = acc_ref[:, j * c:(j + 1) * c].astype(BF16)
            else:
                for s in range(per):
                    o_ref[s] = acc_ref[s * r:(s + 1) * r, :].astype(BF16)

    return pl.pallas_call(
        body,
        name=name,
        grid=(m // tm, nk),
        in_specs=[a_spec, pl.BlockSpec((tk, n), lambda i, k: (k, 0))],
        out_specs=out_spec,
        out_shape=_sds((N_DEV, r, c), BF16),
        scratch_shapes=[pltpu.VMEM((tm, n), F32)],
        compiler_params=_params(("parallel", "arbitrary")),
    )(a, b)


def _rope_tables(pos_col):
    n = pos_col.shape[0]
    half = QK_ROPE // 2
    inv_freq = (ROPE_THETA ** (-np.arange(half, dtype=np.float32) / half)).astype(np.float32)
    freq_row = jnp.asarray(np.concatenate([inv_freq, inv_freq, np.zeros(64, np.float32)])[None, :])

    def body(p_ref, f_ref, c_ref, s_ref):
        ang = p_ref[...].astype(F32) * f_ref[...]
        c_ref[...] = jnp.cos(ang)
        s_ref[...] = jnp.sin(ang)

    return _row_call(body, "rope_tables", n, min(n, 1024), [pos_col], [freq_row], [((n, 128), F32)] * 2)


def _rope_rot(v):
    lane = lax.broadcasted_iota(jnp.int32, v.shape, 1)
    return jnp.where(lane < 32, -pltpu.roll(v, 96, 1), jnp.where(lane < 64, pltpu.roll(v, 32, 1), 0.0))


def _rope_rot_t(v):
    lane = lax.broadcasted_iota(jnp.int32, v.shape, 1)
    return jnp.where(lane < 32, pltpu.roll(v, 96, 1), jnp.where(lane < 64, -pltpu.roll(v, 32, 1), 0.0))


def _in_proj(x, norm_mix, w_in_pad, xch):
    n = x.shape[0]

    def body(x_ref, g_ref, w_ref, u_ref, ql_ref, kvl_ref, gs_ref, gm_ref, xnt_ref):
        xn, _ = _rms(x_ref[...], g_ref[...], D_MODEL)
        xb = xn.astype(BF16)
        xnt_ref[...] = xn.T.astype(BF16)
        for ref, (a, b) in zip((u_ref, ql_ref, kvl_ref, gs_ref, gm_ref), IN_SEGS):
            ref[...] = _dot(xb, w_ref[:, a:b])

    outs = [((n, b - a), F32) for a, b in IN_SEGS]
    return _row_call(body, "in_proj", n, MM_T, [x], [norm_mix, w_in_pad], outs, xch=xch,
                     col_outs=[((D_MODEL, n), BF16)])


def _ssm_prep_fn(a_re, a_im, log_dt, b_re_x, b_im_x):
    dt = jnp.exp(log_dt)
    mag = jnp.exp(a_re * dt)
    lr = mag * jnp.cos(a_im * dt)
    li = mag * jnp.sin(a_im * dt)
    den = a_re * a_re + a_im * a_im
    fr = ((lr - 1.0) * a_re + li * a_im) / den
    fi = (li * a_re - (lr - 1.0) * a_im) / den
    return lr, li, fr * b_re_x - fi * b_im_x, fr * b_im_x + fi * b_re_x


def _dot_exact(a, b, dims):
    return lax.dot_general(a, b, (dims, ((), ())), precision=lax.Precision.HIGHEST, preferred_element_type=F32)


def _lane_repeat(width, n):
    src = lax.broadcasted_iota(jnp.int32, (width, n), 0)
    dst = lax.broadcasted_iota(jnp.int32, (width, n), 1)
    return (dst % width == src).astype(F32)


def _same_group(rows, rows_per_group, cols, cols_per_group):
    row = lax.broadcasted_iota(jnp.int32, (rows, cols), 0)
    col = lax.broadcasted_iota(jnp.int32, (rows, cols), 1)
    return (row // rows_per_group) == (col // cols_per_group)


def _expand_b(bt):
    tiled = _dot_exact(bt, _lane_repeat(SSM_STATE, N_STATE), ((1,), (0,)))
    return jnp.where(_same_group(SSM_WIDTH, SSM_GROUP_CH, N_STATE, SSM_STATE), tiled, 0.0)


def _collect_b(m):
    masked = jnp.where(_same_group(SSM_WIDTH, SSM_GROUP_CH, N_STATE, SSM_STATE), m, 0.0)
    return _dot_exact(masked, _lane_repeat(SSM_STATE, N_STATE), ((1,), (1,)))


def _ssm_prep(a_re, a_im, log_dt, bt_re, bt_im, c2_re, c2_im):
    def body(ar, ai, ld, br, bi, cr, ci, lam_ref, bblk_ref, cblk_ref):
        lr, li, bbr, bbi = _ssm_prep_fn(ar[...], ai[...], ld[...], _expand_b(br[...]), _expand_b(bi[...]))
        lam_ref[0:1, :] = lr
        lam_ref[1:2, :] = li
        bblk_ref[:, 0:N_STATE] = bbr.astype(BF16)
        bblk_ref[:, N_STATE:] = bbi.astype(BF16)
        rep = _lane_repeat(SSM_GROUP_CH, SSM_WIDTH)
        own = _same_group(N_STATE, SSM_STATE, SSM_WIDTH, SSM_GROUP_CH)
        cblk_ref[0:N_STATE, :] = jnp.where(own, _dot_exact(cr[...], rep, ((1,), (0,))), 0.0).astype(BF16)
        cblk_ref[N_STATE:, :] = jnp.where(own, -_dot_exact(ci[...], rep, ((1,), (0,))), 0.0).astype(BF16)

    return pl.pallas_call(
        body,
        name="ssm_prep",
        out_shape=[_sds((2, N_STATE), F32), _sds((SSM_WIDTH, 2 * N_STATE), BF16),
                   _sds((2 * N_STATE, SSM_WIDTH), BF16)],
        compiler_params=_params(),
    )(a_re, a_im, log_dt, bt_re, bt_im, c2_re, c2_im)


def _ssm_prep_bwd(a_re, a_im, log_dt, bt_re, bt_im, dlam, dbblk, dcblk_t):
    def body(ar, ai, ld, br, bi, dl, db, dc, dar, dai, dld, dbr, dbi, dcr, dci):
        _, vjp = jax.vjp(_ssm_prep_fn, ar[...], ai[...], ld[...], _expand_b(br[...]), _expand_b(bi[...]))
        g = vjp((dl[0:1, :], dl[1:2, :], db[:, 0:N_STATE], db[:, N_STATE:]))
        dar[...] = g[0]
        dai[...] = g[1]
        grp = lax.broadcasted_iota(jnp.int32, (SSM_GROUPS, N_STATE), 0)
        lane = lax.broadcasted_iota(jnp.int32, (SSM_GROUPS, N_STATE), 1)
        sel = (lane // SSM_STATE) == grp
        dld[...] = jnp.sum(jnp.where(sel, jnp.broadcast_to(g[2], (SSM_GROUPS, N_STATE)), 0.0), axis=1, keepdims=True)
        dbr[...] = _collect_b(g[3])
        dbi[...] = _collect_b(g[4])
        dcr[...] = _collect_b(dc[:, 0:N_STATE])
        dci[...] = -_collect_b(dc[:, N_STATE:])

    small = _sds((SSM_WIDTH, SSM_STATE), F32)
    return pl.pallas_call(
        body,
        name="ssm_prep_bwd",
        out_shape=[_sds((1, N_STATE), F32), _sds((1, N_STATE), F32), _sds((SSM_GROUPS, 1), F32), small, small, small, small],
        compiler_params=_params(),
    )(a_re, a_im, log_dt, bt_re, bt_im, dlam, dbblk, dcblk_t)


def _perm_matrix(t):
    run = t // SUBCHUNKS
    p = np.zeros((t, t), np.float32)
    r = np.arange(t)
    p[r, (r % SUBCHUNKS) * run + r // SUBCHUNKS] = 1.0
    return jnp.asarray(p, dtype=BF16)


def _unpermute(p, a):
    hi = a.astype(BF16)
    r1 = a - hi.astype(F32)
    mid = r1.astype(BF16)
    lo = (r1 - mid.astype(F32)).astype(BF16)
    return _dot_tn(p, hi) + _dot_tn(p, mid) + _dot_tn(p, lo)


def _power_table(lam_ref, pw_ref, n):
    lr, li = lam_ref[0:1, :], lam_ref[1:2, :]
    pw_ref[0:1, 0:N_STATE] = lr
    pw_ref[0:1, N_STATE:] = li

    def step(i, carry):
        pr, pi = carry
        pr, pi = pr * lr - pi * li, pr * li + pi * lr
        pw_ref[pl.ds(i, 1), 0:N_STATE] = pr
        pw_ref[pl.ds(i, 1), N_STATE:] = pi
        return pr, pi

    lax.fori_loop(1, n, step, (lr, li))


def _col_groups():
    return [(pl.ds(c, SCAN_CG), pl.ds(N_STATE + c, SCAN_CG)) for c in range(0, N_STATE, SCAN_CG)]


def _run_scan(buf, lam_ref, t, reverse):
    nblk = t // 8
    for re, im in _col_groups():
        lr = jnp.broadcast_to(lam_ref[0:1, re], (8, SCAN_CG))
        li = jnp.broadcast_to(lam_ref[1:2, re], (8, SCAN_CG))
        if reverse:
            li = -li
        first = pl.ds((nblk - 1) * 8 if reverse else 0, 8)

        def step(k, carry, re=re, im=im, lr=lr, li=li):
            pr, pi = carry
            i = (nblk - 2 - k) if reverse else (k + 1)
            r = pl.ds(pl.multiple_of(i * 8, 8), 8)
            xr = buf[r, re] + lr * pr - li * pi
            xi = buf[r, im] + lr * pi + li * pr
            buf[r, re] = xr
            buf[r, im] = xi
            return xr, xi

        lax.fori_loop(0, nblk - 1, step, (buf[first, re], buf[first, im]))


def _run_carries(buf, pw_ref, carry_ref, s_ref, t, reverse):
    nblk = t // 8
    run = t // SUBCHUNKS
    edge = buf[pl.ds(0 if reverse else (nblk - 1) * 8, 8), :]
    pr, pi = pw_ref[run - 1:run, 0:N_STATE], pw_ref[run - 1:run, N_STATE:]
    if reverse:
        pi = -pi
    sr, si = carry_ref[0:1, 0:N_STATE], carry_ref[0:1, N_STATE:]
    for s in (range(SUBCHUNKS - 1, -1, -1) if reverse else range(SUBCHUNKS)):
        s_ref[s:s + 1, 0:N_STATE] = sr
        s_ref[s:s + 1, N_STATE:] = si
        er, ei = edge[s:s + 1, 0:N_STATE], edge[s:s + 1, N_STATE:]
        sr, si = er + pr * sr - pi * si, ei + pr * si + pi * sr
    carry_ref[:, 0:N_STATE] = jnp.broadcast_to(sr, (8, N_STATE))
    carry_ref[:, N_STATE:] = jnp.broadcast_to(si, (8, N_STATE))


def _run_fix(buf, pw_ref, s_ref, t, reverse):
    nblk = t // 8
    for re, im in _col_groups():
        sr, si = s_ref[:, re], s_ref[:, im]

        def step(i, carry, re=re, im=im, sr=sr, si=si):
            r = pl.ds(pl.multiple_of(i * 8, 8), 8)
            row = pl.ds((nblk - 1 - i) if reverse else i, 1)
            pr, pi = pw_ref[row, re], pw_ref[row, im]
            if reverse:
                pi = -pi
            buf[r, re] += pr * sr - pi * si
            buf[r, im] += pr * si + pi * sr
            return carry

        lax.fori_loop(0, nblk, step, 0)


STATE_BLOCKS = 2 * N_STATE // LANES
CH_BLOCKS = SSM_WIDTH // LANES


def _state_block(b):
    pair = b % (N_STATE // LANES)
    k = (pair * 2 * SSM_GROUP_CH) // LANES
    return slice(b * LANES, (b + 1) * LANES), slice(k * LANES, (k + 1) * LANES)


def _channel_block(c):
    w = N_STATE // CH_BLOCKS
    return slice(c * LANES, (c + 1) * LANES), slice(c * w, (c + 1) * w), slice(N_STATE + c * w, N_STATE + (c + 1) * w)


def _to_states(vb, w_ref, buf, nt):
    for b in range(STATE_BLOCKS):
        lanes, ch = _state_block(b)
        buf[:, lanes] = _dot_nt(vb[:, ch], w_ref[lanes, ch]) if nt else _dot(vb[:, ch], w_ref[ch, lanes])


def _to_channels(buf, w_ref, nt):
    outs = []
    for c in range(CH_BLOCKS):
        ch, re, im = _channel_block(c)
        xr, xi = buf[:, re].astype(BF16), buf[:, im].astype(BF16)
        if nt:
            outs.append(_dot_nt(xr, w_ref[ch, re]) + _dot_nt(xi, w_ref[ch, im]))
        else:
            outs.append(_dot(xr, w_ref[re, ch]) + _dot(xi, w_ref[im, ch]))
    return jnp.concatenate(outs, axis=-1)


def _ssm_fwd(u, bblk, cblk, lam, d_row, w_glu, b_glu, w_o_ssm, xch):
    n = u.shape[0]
    t = min(SCAN_T, n)
    perm = _perm_matrix(t)

    def body(u_ref, p_ref, bblk_ref, cblk_ref, lam_ref, d_ref, wg_ref, bg_ref, wo_ref, y_ref, ys_ref, st_ref,
             buf, pw_ref, carry_ref, s_ref):
        @pl.when(pl.program_id(0) == 0)
        def _():
            carry_ref[...] = jnp.zeros_like(carry_ref)
            _power_table(lam_ref, pw_ref, t // SUBCHUNKS)

        st_ref[0] = carry_ref[...]
        u_t = u_ref[...]
        p = p_ref[...]
        ub = _dot(p, u_t.astype(BF16)).astype(BF16)
        _to_states(ub, bblk_ref, buf, False)
        _run_scan(buf, lam_ref, t, False)
        _run_carries(buf, pw_ref, carry_ref, s_ref, t, False)
        _run_fix(buf, pw_ref, s_ref, t, False)
        y = d_ref[...] * u_t + _unpermute(p, _to_channels(buf, cblk_ref, False))
        y_ref[...] = y
        z, _ = _gelu(y)
        s = _sigmoid(_dot(z.astype(BF16), wg_ref[...]) + bg_ref[...])
        zgb = (z * s).astype(BF16)
        for j in range(N_DEV):
            ys_ref[:, j * OUT_SHARD:(j + 1) * OUT_SHARD] = _dot(zgb, wo_ref[j])

    consts = [perm, bblk, cblk, lam, d_row, w_glu, b_glu, w_o_ssm]
    return _call(
        body, "ssm_fwd", (n // t,), [u] + consts, [_rows(u, t)] + [_const(a) for a in consts],
        [_sds((n, SSM_WIDTH), F32), _sds((n, D_MODEL), F32), _sds((n // t, 8, 2 * N_STATE), F32)],
        [pl.BlockSpec((t, SSM_WIDTH), lambda i: (i, 0)), pl.BlockSpec((t, D_MODEL), lambda i: (i, 0)),
         pl.BlockSpec((1, 8, 2 * N_STATE), lambda i: (i, 0, 0))],
        scratch=[pltpu.VMEM((t, 2 * N_STATE), F32), pltpu.VMEM((t // SUBCHUNKS, 2 * N_STATE), F32),
                 pltpu.VMEM((8, 2 * N_STATE), F32), pltpu.VMEM((8, 2 * N_STATE), F32)],
        xch=xch)


def _head_norm_rope(slab, gain, cos_t, sin_t):
    xn, inv = _rms(slab, gain, QK_HEAD)
    lo, hi = xn[:, 0:128], xn[:, 128:256]
    return jnp.concatenate([lo, hi * cos_t + _rope_rot(hi) * sin_t], axis=-1), inv


def _head_norm_rope_bwd(g, slab, gain, inv, cos_t, sin_t):
    g_lo, g_hi = g[:, 0:128], g[:, 128:256]
    g_n = jnp.concatenate([g_lo, g_hi * cos_t + _rope_rot_t(g_hi * sin_t)], axis=-1)
    return _rms_bwd(g_n, slab, gain, inv, QK_HEAD)


def _qkv_prep(ql, kvl, q_a_norm, kv_a_norm, wq, wkv, gq, gk, cos_t, sin_t):
    n = ql.shape[0]
    tm = ROW_T

    def body(ql_ref, kvl_ref, cos_ref, sin_ref, qa_ref, ka_ref, wq_ref, wkv_ref, gq_ref, gk_ref,
             q_ref, k_ref, v_ref, kt_ref, vt_ref):
        cos_t, sin_t = cos_ref[...], sin_ref[...]
        qa, _ = _rms(ql_ref[...], qa_ref[...], Q_LORA)
        qab = qa.astype(BF16)
        kvl_t = kvl_ref[...]
        ca, _ = _rms(kvl_t[:, 0:KV_LORA], ka_ref[...], KV_LORA)
        cab = ca.astype(BF16)
        kpe = kvl_t[:, KV_LORA:KV_LAT_PAD]
        q_pre = _dot(qab, wq_ref[...])
        kv_pre = _dot(cab, wkv_ref[...])
        for h in range(N_HEADS):
            qh, _ = _head_norm_rope(q_pre[:, h * QK_PAD:(h + 1) * QK_PAD], gq_ref[...], cos_t, sin_t)
            q_ref[h] = (qh * ATT_SCALE).astype(BF16)
            kv_h = kv_pre[:, h * QK_PAD:(h + 1) * QK_PAD]
            kh, _ = _head_norm_rope(jnp.concatenate([kv_h[:, 0:QK_NOPE], kpe], axis=-1), gk_ref[...], cos_t, sin_t)
            k_ref[h] = kh.astype(BF16)
            kt_ref[h] = kh.T.astype(BF16)
            vh = kv_h[:, QK_NOPE:]
            v_ref[h] = vh.astype(BF16)
            vt_ref[h] = vh.T.astype(BF16)

    row_ins, consts = [ql, kvl, cos_t, sin_t], [q_a_norm, kv_a_norm, wq, wkv, gq, gk]
    outs = [_sds((N_HEADS, n, QK_PAD), BF16), _sds((N_HEADS, n, QK_PAD), BF16), _sds((N_HEADS, n, V_HEAD), BF16),
            _sds((N_HEADS, QK_PAD, n), BF16), _sds((N_HEADS, V_HEAD, n), BF16)]
    out_specs = [_rows(o, tm) for o in outs[:3]] + [
        pl.BlockSpec((N_HEADS, QK_PAD, tm), lambda i: (0, 0, i)), pl.BlockSpec((N_HEADS, V_HEAD, tm), lambda i: (0, 0, i))]
    return _call(body, "qkv_prep", (n // tm,), row_ins + consts,
                 [_rows(a, tm) for a in row_ins] + [_const(a) for a in consts], outs, out_specs)


def _causal_mask_t(st, t):
    key = lax.broadcasted_iota(jnp.int32, (t, t), 0)
    qry = lax.broadcasted_iota(jnp.int32, (t, t), 1)
    return jnp.where(key <= qry, st, -jnp.inf)


def _attn_fwd(q, k, vt, xch):
    n = q.shape[1]
    t = min(ATT_T, n)

    def body(q_ref, k_ref, vt_ref, o_ref, lse_ref, ot_ref):
        i = pl.program_id(1)
        qt = q_ref[0]

        def kv_tile(j, carry, diag):
            m, l, acc = carry
            ts = t // ATT_SUB
            sts = []
            for a in range(ATT_SUB):
                r0 = pl.multiple_of(j * t + a * ts, ts)
                st = _dot_nt(k_ref[0, pl.ds(r0, ts), :], qt)
                if diag:
                    key = lax.broadcasted_iota(jnp.int32, (ts, t), 0) + a * ts
                    qry = lax.broadcasted_iota(jnp.int32, (ts, t), 1)
                    st = jnp.where(key <= qry, st, -jnp.inf)
                sts.append(st)
            for a, st in enumerate(sts):
                r0 = pl.multiple_of(j * t + a * ts, ts)
                m_new = jnp.maximum(m, jnp.max(st, 0, keepdims=True))
                alpha = jnp.exp(m - m_new)
                pt = jnp.exp(st - m_new)
                l = alpha * l + jnp.sum(pt, 0, keepdims=True)
                acc = alpha * acc + _dot(vt_ref[0, :, pl.ds(r0, ts)], pt.astype(BF16))
                m = m_new
            return m, l, acc

        init = (jnp.full((1, t), -jnp.inf, F32), jnp.zeros((1, t), F32), jnp.zeros((V_HEAD, t), F32))
        carry = lax.fori_loop(0, i, functools.partial(kv_tile, diag=False), init)
        m, l, acc = kv_tile(i, carry, True)
        out_t = acc / l
        o_ref[...] = out_t.T
        ot_ref[...] = out_t.astype(BF16)
        lse_ref[0] = m + jnp.log(l)

    return _call(
        body, "attn_fwd", (N_HEADS, n // t), [q, k, vt],
        [pl.BlockSpec((1, t, QK_PAD), lambda h, i: (h, i, 0)), pl.BlockSpec((1, n, QK_PAD), lambda h, i: (h, 0, 0)),
         pl.BlockSpec((1, V_HEAD, n), lambda h, i: (h, 0, 0))],
        [_sds((n, N_HEADS * V_HEAD), F32), _sds((N_HEADS, 1, n), F32), _sds((N_HEADS * V_HEAD, n), BF16)],
        [pl.BlockSpec((t, V_HEAD), lambda h, i: (i, h)), pl.BlockSpec((1, 1, t), lambda h, i: (h, 0, i)),
         pl.BlockSpec((V_HEAD, t), lambda h, i: (h, i))],
        xch=xch)


def _merge(attn, gs, gm, y_ssm, x, w_o_mla, w_out):
    n = x.shape[0]

    def body(at_ref, gs_ref, gm_ref, ys_ref, x_ref, wo_ref, wout_ref, h_ref, ym_ref, mxt_ref):
        y_mla = _dot(at_ref[...].astype(BF16), wo_ref[...])
        ym_ref[...] = y_mla
        mixed = _sigmoid(gs_ref[...]) * ys_ref[...] + _sigmoid(gm_ref[...]) * y_mla
        mxt_ref[...] = mixed.T.astype(BF16)
        h_ref[...] = x_ref[...] + _dot(mixed.astype(BF16), wout_ref[...])

    outs = [((n, D_MODEL), F32), ((n, D_MODEL), F32)]
    return _row_call(body, "merge", n, MM_T, [attn, gs, gm, y_ssm, x], [w_o_mla, w_out], outs,
                     col_outs=[((D_MODEL, n), BF16)])


def _mlp_fwd_loss(h, target, norm_mlp, w_up, w_down):
    n = h.shape[0]

    def body(h_ref, t_ref, g_ref, wu_ref, wd_ref, hn_ref, do_ref, hnt_ref, loss_ref):
        h_t = h_ref[...]
        hn, _ = _rms(h_t, g_ref[...], D_MODEL)
        hb = hn.astype(BF16)
        hn_ref[...] = hb
        hnt_ref[...] = hn.T.astype(BF16)
        out = h_t
        for j in range(N_DEV):
            a = jnp.maximum(_dot(hb, wu_ref[j]), 0.0)
            out += _dot((a * a).astype(BF16), wd_ref[j])
        err = out - t_ref[...]
        do_ref[...] = err * (1.0 / D_MODEL)
        _acc(loss_ref, jnp.broadcast_to(jnp.sum(err * err) * (0.5 / D_MODEL), loss_ref.shape))

    outs = [((n, D_MODEL), BF16), ((n, D_MODEL), F32)]
    return _row_call(body, "mlp_fwd_loss", n, MM_T, [h, target], [norm_mlp, w_up, w_down], outs, [((8, 128), F32)],
                     col_outs=[((D_MODEL, n), BF16)])


def _mlp_bwd(dout, hn, h, norm_mlp, w_up, w_down):
    n = h.shape[0]

    def body(do_ref, hn_ref, h_ref, g_ref, wu_ref, wd_ref, da_ref, dh_ref, hidt_ref, dg_ref):
        dout_t = do_ref[...]
        doutb = dout_t.astype(BF16)
        hb = hn_ref[...]
        dhn = jnp.zeros_like(dout_t)
        for j in range(N_DEV):
            cols = slice(j * FF_SHARD, (j + 1) * FF_SHARD)
            a = jnp.maximum(_dot(hb, wu_ref[j]), 0.0)
            hidt_ref[cols, :] = (a * a).T.astype(BF16)
            da = (_dot_nt(doutb, wd_ref[j]) * (2.0 * a)).astype(BF16)
            da_ref[:, cols] = da
            dhn += _dot_nt(da, wu_ref[j])
        h_t = h_ref[...]
        inv = lax.rsqrt(jnp.sum(h_t * h_t, -1, keepdims=True) * (1.0 / D_MODEL) + EPS)
        dx, dg = _rms_bwd(dhn, h_t, g_ref[...], inv, D_MODEL)
        dh_ref[...] = dout_t + dx
        _acc(dg_ref, jnp.sum(dg, 0, keepdims=True))

    outs = [((n, D_FF), BF16), ((n, D_MODEL), F32)]
    return _row_call(body, "mlp_bwd", n, MM_T, [dout, hn, h], [norm_mlp, w_up, w_down], outs, [((1, D_MODEL), F32)],
                     col_outs=[((D_FF, n), BF16)])


def _merge_bwd(dh, gs, gm, y_ssm, y_mla, w_out, w_o_mla):
    n = dh.shape[0]

    def body(dh_ref, gs_ref, gm_ref, ys_ref, ym_ref, wout_ref, wo_ref, dgs_ref, dgm_ref, dys_ref, dym_ref, dat_ref):
        dmix = _dot_nt(dh_ref[...].astype(BF16), wout_ref[...])
        sgs, sgm = _sigmoid(gs_ref[...]), _sigmoid(gm_ref[...])
        dgs_ref[...] = (dmix * ys_ref[...] * sgs * (1.0 - sgs)).astype(BF16)
        dgm_ref[...] = (dmix * ym_ref[...] * sgm * (1.0 - sgm)).astype(BF16)
        dys_ref[...] = (dmix * sgs).astype(BF16)
        dym = (dmix * sgm).astype(BF16)
        dym_ref[...] = dym
        dat_ref[...] = _dot_nt(dym, wo_ref[...])

    outs = [((n, D_MODEL), BF16)] * 4 + [((n, D_MODEL), F32)]
    return _row_call(body, "merge_bwd", n, MM_T, [dh, gs, gm, y_ssm, y_mla], [w_out, w_o_mla], outs)


def _attn_bwd(q, k, kt, v, out, lse, dout, xch):
    n = q.shape[1]
    t = min(ATT_T, n)
    nt = n // t

    def body(q_ref, k_ref, kt_ref, v_ref, o_ref, lse_ref, do_ref, dq_ref, dk_ref, dv_ref, delta_ref, dqt_ref):
        j = pl.program_id(1)

        @pl.when(j == 0)
        def _():
            dqt_ref[...] = jnp.zeros_like(dqt_ref)
            prod = do_ref[...] * o_ref[...]
            delta_ref[...] = lax.dot_general(jnp.ones((8, V_HEAD), F32), prod, (((1,), (1,)), ((), ())),
                                             precision=lax.Precision.HIGHEST, preferred_element_type=F32)

        k_t = k_ref[0]
        kt_t = kt_ref[0]
        v_t = v_ref[0]

        def q_tile(i, carry, diag):
            dk, dv = carry
            r0 = pl.multiple_of(i * t, t)
            rows = pl.ds(r0, t)
            qt = q_ref[0, rows, :]
            st = _dot_nt(k_t, qt)
            if diag:
                st = _causal_mask_t(st, t)
            pt = jnp.exp(st - lse_ref[0, :, rows])
            dob = do_ref[rows, :].astype(BF16)
            dv = dv + _dot(pt.astype(BF16), dob)
            dst = (pt * (_dot_nt(v_t, dob) - delta_ref[0:1, rows])).astype(BF16)
            dk = dk + _dot(dst, qt)
            dqt_ref[:, rows] += _dot(kt_t, dst)
            return dk, dv

        carry = q_tile(j, (jnp.zeros((t, QK_PAD), F32), jnp.zeros((t, V_HEAD), F32)), True)
        dk, dv = lax.fori_loop(j + 1, nt, functools.partial(q_tile, diag=False), carry)
        dk_ref[0] = dk
        dv_ref[0] = dv

        @pl.when(j == nt - 1)
        def _():
            for c in range(0, n, t):
                dq_ref[0, c:c + t, :] = dqt_ref[:, c:c + t].T

    return _call(
        body, "attn_bwd", (N_HEADS, nt), [q, k, kt, v, out, lse, dout],
        [pl.BlockSpec((1, n, QK_PAD), lambda h, j: (h, 0, 0)), pl.BlockSpec((1, t, QK_PAD), lambda h, j: (h, j, 0)),
         pl.BlockSpec((1, QK_PAD, t), lambda h, j: (h, 0, j)), pl.BlockSpec((1, t, V_HEAD), lambda h, j: (h, j, 0)),
         pl.BlockSpec((n, V_HEAD), lambda h, j: (0, h)), pl.BlockSpec((1, 1, n), lambda h, j: (h, 0, 0)),
         pl.BlockSpec((n, V_HEAD), lambda h, j: (0, h))],
        [_sds((N_HEADS, n, QK_PAD), F32), _sds((N_HEADS, n, QK_PAD), F32), _sds((N_HEADS, n, V_HEAD), F32)],
        [pl.BlockSpec((1, n, QK_PAD), lambda h, j: (h, 0, 0)), pl.BlockSpec((1, t, QK_PAD), lambda h, j: (h, j, 0)),
         pl.BlockSpec((1, t, V_HEAD), lambda h, j: (h, j, 0))],
        scratch=[pltpu.VMEM((8, n), F32), pltpu.VMEM((QK_PAD, n), F32)],
        xch=xch)


def _qkv_prep_bwd(ql, kvl, dq, dk, dv, q_a_norm, kv_a_norm, wq, wkv, gq, gk, cos_t, sin_t, xch):
    n = ql.shape[0]

    def body(ql_ref, kvl_ref, cos_ref, sin_ref, dq_ref, dk_ref, dv_ref, qa_ref, ka_ref, wq_ref, wkv_ref, gq_ref, gk_ref,
             dql_ref, dkvl_ref, qab_ref, dqp_ref, cab_ref, dkvp_ref, dqa_ref, dka_ref, dgq_ref, dgk_ref):
        cos_t, sin_t = cos_ref[...], sin_ref[...]
        ql_t = ql_ref[...]
        qa, inv_qa = _rms(ql_t, qa_ref[...], Q_LORA)
        qab = qa.astype(BF16)
        qab_ref[...] = qab
        kvl_t = kvl_ref[...]
        ckv = kvl_t[:, 0:KV_LORA]
        ca, inv_ca = _rms(ckv, ka_ref[...], KV_LORA)
        cab = ca.astype(BF16)
        cab_ref[...] = cab
        kpe = kvl_t[:, KV_LORA:KV_LAT_PAD]
        dgq = jnp.zeros((1, QK_PAD), F32)
        dgk = jnp.zeros((1, QK_PAD), F32)
        dkpe = jnp.zeros_like(kpe)
        q_pre = _dot(qab, wq_ref[...])
        kv_pre = _dot(cab, wkv_ref[...])
        for h in range(N_HEADS):
            head = slice(h * QK_PAD, (h + 1) * QK_PAD)
            q_slab = q_pre[:, head]
            inv = lax.rsqrt(jnp.sum(q_slab * q_slab, -1, keepdims=True) * (1.0 / QK_HEAD) + EPS)
            d_slab, dg = _head_norm_rope_bwd(dq_ref[h] * ATT_SCALE, q_slab, gq_ref[...], inv, cos_t, sin_t)
            dqp_ref[:, head] = d_slab.astype(BF16)
            dgq += jnp.sum(dg, 0, keepdims=True)
            k_slab = jnp.concatenate([kv_pre[:, h * QK_PAD:h * QK_PAD + QK_NOPE], kpe], axis=-1)
            inv = lax.rsqrt(jnp.sum(k_slab * k_slab, -1, keepdims=True) * (1.0 / QK_HEAD) + EPS)
            d_slab, dg = _head_norm_rope_bwd(dk_ref[h], k_slab, gk_ref[...], inv, cos_t, sin_t)
            dkvp_ref[:, head] = jnp.concatenate([d_slab[:, 0:QK_NOPE], dv_ref[h]], axis=-1).astype(BF16)
            dkpe += d_slab[:, QK_NOPE:QK_PAD]
            dgk += jnp.sum(dg, 0, keepdims=True)
        dqa = _dot_nt(dqp_ref[...], wq_ref[...])
        dx, dg = _rms_bwd(dqa, ql_t, qa_ref[...], inv_qa, Q_LORA)
        dql_ref[...] = dx.astype(BF16)
        _acc(dqa_ref, jnp.sum(dg, 0, keepdims=True))
        dca = _dot_nt(dkvp_ref[...], wkv_ref[...])
        dx, dg = _rms_bwd(dca, ckv, ka_ref[...], inv_ca, KV_LORA)
        dkvl_ref[:, 0:KV_LORA] = dx.astype(BF16)
        dkvl_ref[:, KV_LORA:KV_LAT_PAD] = dkpe.astype(BF16)
        _acc(dka_ref, jnp.sum(dg, 0, keepdims=True))
        _acc(dgq_ref, dgq)
        _acc(dgk_ref, dgk)

    row_outs = [((n, Q_LORA), BF16), ((n, KV_LAT_PAD), BF16), ((n, Q_LORA), BF16), ((n, N_HEADS * QK_PAD), BF16),
                ((n, KV_LORA), BF16), ((n, N_HEADS * (QK_NOPE + V_HEAD)), BF16)]
    acc_outs = [((1, Q_LORA), F32), ((1, KV_LORA), F32), ((1, QK_PAD), F32), ((1, QK_PAD), F32)]
    return _row_call(body, "qkv_prep_bwd", n, ROW_T, [ql, kvl, cos_t, sin_t, dq, dk, dv],
                     [q_a_norm, kv_a_norm, wq, wkv, gq, gk], row_outs, acc_outs, xch=xch)


def _glu_bwd(dy_ssm, y, w_glu, b_glu, w_o_ssm):
    n = y.shape[0]

    def body(dys_ref, y_ref, wg_ref, bg_ref, wo_ref, dy_ref, zg_ref, z_ref, dt_ref, db_ref):
        y_t = y_ref[...]
        z, th = _gelu(y_t)
        zb = z.astype(BF16)
        z_ref[...] = zb
        s = _sigmoid(_dot(zb, wg_ref[...]) + bg_ref[...])
        zg_ref[...] = (z * s).astype(BF16)
        dys = dys_ref[...]
        dzg = jnp.zeros_like(y_t)
        for j in range(N_DEV):
            dzg += _dot_nt(dys[:, j * OUT_SHARD:(j + 1) * OUT_SHARD], wo_ref[j])
        dt = dzg * z * s * (1.0 - s)
        dtb = dt.astype(BF16)
        dt_ref[...] = dtb
        dz = dzg * s + _dot_nt(dtb, wg_ref[...])
        dy_ref[...] = dz * _gelu_grad(y_t, th)
        _acc(db_ref, jnp.sum(dt, 0, keepdims=True))

    outs = [((n, SSM_WIDTH), F32)] + [((n, SSM_WIDTH), BF16)] * 3
    return _row_call(body, "glu_bwd", n, ROW_T, [dy_ssm, y], [w_glu, b_glu, w_o_ssm], outs, [((1, SSM_WIDTH), F32)])


def _ssm_bwd(u, dy, st, bblk, cblk, lam, d_row, xch):
    n = u.shape[0]
    t = min(SCAN_T, n)
    nc = n // t
    kb = 512
    perm = _perm_matrix(t)

    def body(u_ref, dy_ref, st_ref, p_ref, bblk_ref, cblk_ref, lam_ref, d_ref,
             du_ref, dlam_ref, dd_ref, db_ref, dct_ref,
             buf_x, buf_a, pw_ref, carry_ref, xcarry_ref, sx_ref, sa_ref, db_acc, dct_acc):
        @pl.when(pl.program_id(0) == 0)
        def _():
            carry_ref[...] = jnp.zeros_like(carry_ref)
            db_acc[...] = jnp.zeros_like(db_acc)
            dct_acc[...] = jnp.zeros_like(dct_acc)
            _power_table(lam_ref, pw_ref, t // SUBCHUNKS)

        u_t = u_ref[...]
        dy_t = dy_ref[...]
        p = p_ref[...]
        ub = _dot(p, u_t.astype(BF16)).astype(BF16)
        dyb = _dot(p, dy_t.astype(BF16)).astype(BF16)
        _to_states(ub, bblk_ref, buf_x, False)
        xcarry_ref[...] = st_ref[0]
        _run_scan(buf_x, lam_ref, t, False)
        _run_carries(buf_x, pw_ref, xcarry_ref, sx_ref, t, False)
        _run_fix(buf_x, pw_ref, sx_ref, t, False)
        _to_states(dyb, cblk_ref, buf_a, True)
        _run_scan(buf_a, lam_ref, t, True)
        _run_carries(buf_a, pw_ref, carry_ref, sa_ref, t, True)
        _run_fix(buf_a, pw_ref, sa_ref, t, True)
        du_ref[...] = (d_ref[...] * dy_t + _unpermute(p, _to_channels(buf_a, bblk_ref, True))).astype(BF16)
        for b in range(STATE_BLOCKS):
            lanes, ch = _state_block(b)
            db_acc[ch, lanes] += _dot_tn(ub[:, ch], buf_a[:, lanes].astype(BF16))
            dct_acc[ch, lanes] += _dot_tn(dyb[:, ch], buf_x[:, lanes].astype(BF16))
        for c in range(0, N_STATE, kb):
            re, im = pl.ds(c, kb), pl.ds(N_STATE + c, kb)
            xr, xi = buf_x[pl.ds(0, t - 8), re], buf_x[pl.ds(0, t - 8), im]
            ar, ai = buf_a[pl.ds(8, t - 8), re], buf_a[pl.ds(8, t - 8), im]
            x0r, x0i = sx_ref[:, re], sx_ref[:, im]
            a0r, a0i = buf_a[0:8, re], buf_a[0:8, im]
            dlam_part_re = (jnp.sum(ar * xr + ai * xi, 0, keepdims=True)
                            + jnp.sum(a0r * x0r + a0i * x0i, 0, keepdims=True))
            dlam_part_im = (jnp.sum(ai * xr - ar * xi, 0, keepdims=True)
                            + jnp.sum(a0i * x0r - a0r * x0i, 0, keepdims=True))

            @pl.when(pl.program_id(0) == 0)
            def _(c=c):
                dlam_ref[0:1, c:c + kb] = jnp.zeros((1, kb), F32)
                dlam_ref[1:2, c:c + kb] = jnp.zeros((1, kb), F32)

            dlam_ref[0:1, c:c + kb] += dlam_part_re
            dlam_ref[1:2, c:c + kb] += dlam_part_im
        _acc(dd_ref, jnp.sum(dy_t * u_t, 0, keepdims=True))

        @pl.when(pl.program_id(0) == nc - 1)
        def _():
            pltpu.sync_copy(db_acc, db_ref)
            pltpu.sync_copy(dct_acc, dct_ref)

    rev = lambda i: (nc - 1 - i, 0)
    consts = [perm, bblk, cblk, lam, d_row]
    wide = (SSM_WIDTH, 2 * N_STATE)
    return _call(
        body, "ssm_bwd", (nc,), [u, dy, st] + consts,
        [pl.BlockSpec((t, SSM_WIDTH), rev), pl.BlockSpec((t, SSM_WIDTH), rev),
         pl.BlockSpec((1, 8, 2 * N_STATE), lambda i: (nc - 1 - i, 0, 0))] + [_const(a) for a in consts],
        [_sds((n, SSM_WIDTH), BF16), _sds((2, N_STATE), F32), _sds((1, SSM_WIDTH), F32), _sds(wide, F32), _sds(wide, F32)],
        [pl.BlockSpec((t, SSM_WIDTH), rev), pl.BlockSpec((2, N_STATE), lambda i: (0, 0)),
         pl.BlockSpec((1, SSM_WIDTH), lambda i: (0, 0)), ANY, ANY],
        scratch=[pltpu.VMEM((t, 2 * N_STATE), F32)] * 2 + [pltpu.VMEM((t // SUBCHUNKS, 2 * N_STATE), F32)]
        + [pltpu.VMEM((8, 2 * N_STATE), F32)] * 4 + [pltpu.VMEM(wide, F32)] * 2,
        xch=xch)


def _in_proj_bwd(pieces, dh, x, norm_mix, w_in_pad):
    n = x.shape[0]

    def body(du_ref, dql_ref, dkvl_ref, dgs_ref, dgm_ref, dh_ref, x_ref, g_ref, w_ref, dx_ref, dp_ref, dg_ref):
        dxn = jnp.zeros((dh_ref.shape[0], D_MODEL), F32)
        for ref, (a, b) in zip((du_ref, dql_ref, dkvl_ref, dgs_ref, dgm_ref), IN_SEGS):
            piece = ref[...]
            dp_ref[:, a:b] = piece
            dxn += _dot_nt(piece, w_ref[:, a:b])
        x_t = x_ref[...]
        inv = lax.rsqrt(jnp.sum(x_t * x_t, -1, keepdims=True) * (1.0 / D_MODEL) + EPS)
        dx, dg = _rms_bwd(dxn, x_t, g_ref[...], inv, D_MODEL)
        dx_ref[...] = dh_ref[...] + dx
        _acc(dg_ref, jnp.sum(dg, 0, keepdims=True))

    outs = [((n, D_MODEL), F32), ((n, D_IN_PAD), BF16)]
    return _row_call(body, "in_proj_bwd", n, MM_T, list(pieces) + [dh, x], [norm_mix, w_in_pad], outs,
                     [((1, D_MODEL), F32)])


def _swap_minor(a):
    g, r, c = a.shape[1:]
    return jnp.transpose(a[0], (0, 2, 1)).reshape(g * c, r)


def _pad_in(w):
    return jnp.concatenate([w[:, :KV_END], jnp.zeros((w.shape[0], D_IN_PAD - D_IN), w.dtype), w[:, KV_END:]], axis=1)


def _unpad_in(w):
    return jnp.concatenate([w[:, :KV_END], w[:, KV_END + D_IN_PAD - D_IN:]], axis=1)


def _pad_gain(g):
    return jnp.pad(g, ((0, 0), (0, QK_PAD - QK_HEAD)))


def _place():
    x, y, c = lax.axis_index("x"), lax.axis_index("y"), lax.axis_index("c")
    chips = [(x, y), (1 - x, y), (x, 1 - y), (1 - x, 1 - y)]
    return x, y, c, chips


def _all_gather(block, name):
    rows, lanes = block.shape

    def body(x_ref, out_ref, send_sems, recv_sems, local_sem):
        x, y, c, chips = _place()
        me, sibling = (x, y, c), (x, y, 1 - c)

        def slot(px, py, pc):
            return out_ref.at[4 * px + 2 * py + pc]

        def copy(k, blk, to, src=None):
            return pltpu.make_async_remote_copy(
                src_ref=slot(*blk) if src is None else src, dst_ref=slot(*blk),
                send_sem=send_sems.at[k], recv_sem=recv_sems.at[k], device_id=to, device_id_type=MESH)

        mine = pltpu.make_async_copy(x_ref, slot(*me), local_sem)
        mine.start()
        first = [copy(0, me, sibling, src=x_ref)]
        first += [copy(1 + j, me, (*chip, c), src=x_ref) for j, chip in enumerate(chips[1:])]
        for cp in first:
            cp.start()
        passed = [copy(4 + j, (*chip, c), sibling) for j, chip in enumerate(chips[1:])]
        for j, chip in enumerate(chips[1:]):
            copy(1 + j, (*chip, c), me).wait_recv()
            passed[j].start()
        copy(0, sibling, me).wait_recv()
        for j, chip in enumerate(chips[1:]):
            copy(4 + j, (*chip, 1 - c), me).wait_recv()
        for cp in first + passed:
            cp.wait_send()
        mine.wait()

    return pl.pallas_call(
        body,
        name=name,
        in_specs=[ANY],
        out_specs=ANY,
        out_shape=_sds((N_DEV, rows, lanes), block.dtype),
        scratch_shapes=[pltpu.SemaphoreType.DMA((7,)), pltpu.SemaphoreType.DMA((7,)), pltpu.SemaphoreType.DMA],
    )(block)


def _reduce_scatter(parts, name):
    _, rows, lanes = parts.shape

    def body(p_ref, out_ref, own, land_a, send_b, land_b, sa, ra, sb, rb, lo):
        x, y, c, chips = _place()
        sibling = (x, y, 1 - c)

        def blk(chip, core):
            return p_ref.at[4 * chip[0] + 2 * chip[1] + core]

        to_sib = [pltpu.make_async_remote_copy(
            src_ref=blk(chips[k], 1 - c), dst_ref=land_a.at[k], send_sem=sa.at[k], recv_sem=ra.at[k],
            device_id=sibling, device_id_type=MESH) for k in range(4)]
        for cp in to_sib:
            cp.start()
        loads = [pltpu.make_async_copy(blk(chips[k], c), own.at[k], lo.at[k]) for k in range(4)]
        for cp in loads:
            cp.start()
        to_chip = [pltpu.make_async_remote_copy(
            src_ref=send_b.at[j], dst_ref=land_b.at[j], send_sem=sb.at[j], recv_sem=rb.at[j],
            device_id=(*chips[1 + j], c), device_id_type=MESH) for j in range(3)]
        for k in (1, 2, 3):
            to_sib[k].wait_recv()
            loads[k].wait()
            send_b[k - 1] = (own[k] + land_a[k]).astype(BF16)
            to_chip[k - 1].start()
        to_sib[0].wait_recv()
        loads[0].wait()
        acc = own[0] + land_a[0]
        for j in range(3):
            to_chip[j].wait_recv()
            acc = acc + land_b[j].astype(F32)
        out_ref[...] = acc
        for cp in to_sib + to_chip:
            cp.wait_send()

    return pl.pallas_call(
        body,
        name=name,
        in_specs=[ANY],
        out_specs=pl.BlockSpec(memory_space=pltpu.VMEM),
        out_shape=_sds((rows, lanes), F32),
        scratch_shapes=[pltpu.VMEM((4, rows, lanes), F32), pltpu.VMEM((4, rows, lanes), F32),
                        pltpu.VMEM((3, rows, lanes), BF16), pltpu.VMEM((3, rows, lanes), BF16)]
        + [pltpu.SemaphoreType.DMA((4,))] * 2 + [pltpu.SemaphoreType.DMA((3,))] * 2 + [pltpu.SemaphoreType.DMA((4,))],
        compiler_params=_params(),
    )(parts)


def _adamw_math(w, g, m, v):
    m = ADAM_B1 * m + (1.0 - ADAM_B1) * g
    v = ADAM_B2 * v + (1.0 - ADAM_B2) * (g * g)
    m_hat = m / (1.0 - ADAM_B1 ** ADAM_STEP)
    v_hat = v / (1.0 - ADAM_B2 ** ADAM_STEP)
    delta = -ADAM_LR * (m_hat / (jnp.sqrt(v_hat) + ADAM_EPS) + ADAM_WD * w)
    return delta, m, v


def _row_tile(r):
    return max(t for t in range(8, min(r, 256) + 1, 8) if r % t == 0)


def _adamw(w, g, m, v, name):
    r, n = w.shape

    def body(w_ref, g_ref, m_ref, v_ref, d_ref, nm_ref, nv_ref):
        d_ref[...], nm_ref[...], nv_ref[...] = _adamw_math(w_ref[...], g_ref[...], m_ref[...], v_ref[...])

    return _row_call(body, name, r, _row_tile(r), [w, g, m, v], [], [((r, n), F32)] * 3)


def _adamw_sum(landed, w, m, v, name):
    r, n = w.shape

    def body(l_ref, w_ref, m_ref, v_ref, g_ref, d_ref, nm_ref, nv_ref):
        g = l_ref[0].astype(F32)
        for dev in range(1, N_DEV):
            g = g + l_ref[dev].astype(F32)
        g_ref[...] = g
        d_ref[...], nm_ref[...], nv_ref[...] = _adamw_math(w_ref[...], g, m_ref[...], v_ref[...])

    tm = max(t for t in range(16, min(r, 256) + 1, 16) if r % t == 0)
    return _row_call(body, name, r, tm, [landed, w, m, v], [], [((r, n), F32)] * 4)


def _adamw_small(gathered, w, m, v, row_counts):
    n_rows = w.shape[0]

    def body(ga_ref, w_ref, m_ref, v_ref, loss_ref, *out_refs):
        g = ga_ref[0]
        for dev in range(1, N_DEV):
            g = g + ga_ref[dev]
        loss_ref[...] = g[n_rows:n_rows + 8]
        g = g[0:n_rows]
        d, nm, nv = _adamw_math(w_ref[...], g, m_ref[...], v_ref[...])
        off = 0
        for p, rows in enumerate(row_counts):
            for k, val in enumerate((g, d, nm, nv)):
                out_refs[4 * p + k][...] = val[off:off + rows]
            off += rows

    outs = [_sds((8, LANES), F32)] + [_sds((rows, LANES), F32) for rows in row_counts for _ in range(4)]
    return pl.pallas_call(body, name="adamw_small", out_shape=outs, compiler_params=_params())(gathered, w, m, v)


SMALL = ("norm_mix", "q_a_norm", "kv_a_norm", "q_norm", "k_norm", "ssm_a_re", "ssm_a_im", "ssm_log_dt", "ssm_b_re",
         "ssm_b_im", "ssm_c_re", "ssm_c_im", "ssm_d", "b_glu", "norm_mlp")
WEIGHT_ORDER = ("norm_mix", "w_in", "q_a_norm", "kv_a_norm", "w_q_b", "w_kv_b", "q_norm", "k_norm", "w_o_mla",
                "ssm_a_re", "ssm_a_im", "ssm_log_dt", "ssm_b_re", "ssm_b_im", "ssm_c_re", "ssm_c_im", "ssm_d", "w_glu",
                "b_glu", "w_o_ssm", "w_out", "norm_mlp", "w_up", "w_down")
IN_SHARD = D_IN // N_DEV
Q_SHARD = QK_HEAD


def _pack_small(vals):
    parts = []
    for n in SMALL:
        flat = vals[n].reshape(-1)
        size = -(-flat.shape[0] // (8 * LANES)) * 8 * LANES
        parts.append(jnp.pad(flat, (0, size - flat.shape[0])).reshape(-1, LANES))
    return jnp.concatenate(parts, axis=0)


def _small_rows(like):
    return [-(-like[n].size // (8 * LANES)) * 8 for n in SMALL]


def _step(x, pos_col, target, w, small):
    bf = {n: a.astype(BF16) for n, a in w.items()}
    gq, gk = _pad_gain(small["q_norm"]), _pad_gain(small["k_norm"])
    a_re = small["ssm_a_re"].reshape(1, N_STATE)
    a_im = small["ssm_a_im"].reshape(1, N_STATE)
    log_dt = jnp.repeat(small["ssm_log_dt"].reshape(SSM_GROUPS), SSM_STATE).reshape(1, N_STATE)
    bt_re, bt_im = _swap_minor(small["ssm_b_re"]), _swap_minor(small["ssm_b_im"])
    c2_re, c2_im = _swap_minor(small["ssm_c_re"]), _swap_minor(small["ssm_c_im"])
    d_row = small["ssm_d"].reshape(1, SSM_WIDTH)

    w_in_all = _all_gather(bf["w_in"], "gather_w_in")
    w_in_pad = _pad_in(jnp.transpose(w_in_all, (1, 0, 2)).reshape(D_MODEL, D_IN))
    cos_t, sin_t = _rope_tables(pos_col)
    lam, bblk, cblk = _ssm_prep(a_re, a_im, log_dt, bt_re, bt_im, c2_re, c2_im)
    wq_mine = jnp.pad(bf["w_q_b"], ((0, 0), (0, QK_PAD - QK_HEAD)))
    u, ql, kvl, gs, gm, xn_t, w_glu, w_o_ssm = _in_proj(
        x, small["norm_mix"], w_in_pad, xch=[(bf["w_glu"], False), (bf["w_o_ssm"], False)])
    w_glu = w_glu.reshape(SSM_WIDTH, SSM_WIDTH)
    y, y_ssm, st, wq, wkv, w_o_mla, w_out = _ssm_fwd(
        u, bblk, cblk, lam, d_row, w_glu, small["b_glu"], w_o_ssm,
        xch=[(wq_mine, False), (bf["w_kv_b"], False), (bf["w_o_mla"], False), (bf["w_out"], False)])
    w_o_mla, w_out = w_o_mla.reshape(D_MODEL, D_MODEL), w_out.reshape(D_MODEL, D_MODEL)
    wq = jnp.transpose(wq, (1, 0, 2)).reshape(Q_LORA, N_HEADS * QK_PAD)
    wkv = jnp.transpose(wkv, (1, 0, 2)).reshape(KV_LORA, N_HEADS * QK_PAD)
    q, k, v, kt, vt = _qkv_prep(ql, kvl, small["q_a_norm"], small["kv_a_norm"], wq, wkv, gq, gk, cos_t, sin_t)
    attn, lse, attn_t, w_up, w_down = _attn_fwd(q, k, vt, xch=[(bf["w_up"], False), (bf["w_down"], False)])
    h, y_mla, mixed_t = _merge(attn, gs, gm, y_ssm, x, w_o_mla, w_out)
    hn, dout, hn_t, loss = _mlp_fwd_loss(h, target, small["norm_mlp"], w_up, w_down)

    da, dh, hid_t, d_norm_mlp = _mlp_bwd(dout, hn, h, small["norm_mlp"], w_up, w_down)
    p_w_down = _matmul_tn_shards(hid_t, dout, "dw_down", False, turned=True)
    p_w_up = _matmul_tn_shards(hn_t, da, "dw_up", True, turned=True)
    dgs, dgm, dy_ssm, dy_mla, dattn = _merge_bwd(dh, gs, gm, y_ssm, y_mla, w_out, w_o_mla)
    p_w_out = _matmul_tn_shards(mixed_t, dh, "dw_out", False, turned=True)
    p_w_o_mla = _matmul_tn_shards(attn_t, dy_mla, "dw_o_mla", False, turned=True)
    dq, dk, dv, l_w_up, l_w_down, l_w_out, l_w_o_mla = _attn_bwd(
        q, k, kt, v, attn, lse, dattn, xch=[(p_w_up, True), (p_w_down, True), (p_w_out, True), (p_w_o_mla, True)])
    dql, dkvl, qa, dq_pre, ca, dkv_pre, d_q_a_norm, d_kv_a_norm, d_gq, d_gk = _qkv_prep_bwd(
        ql, kvl, dq, dk, dv, small["q_a_norm"], small["kv_a_norm"], wq, wkv, gq, gk, cos_t, sin_t, xch=[])
    p_wq = _matmul_tn_shards(qa, dq_pre, "dw_q_b", True)
    p_wkv = _matmul_tn_shards(ca, dkv_pre, "dw_kv_b", True)
    dy, zg, z, dt, d_b_glu = _glu_bwd(dy_ssm, y, w_glu, small["b_glu"], w_o_ssm)
    p_w_o_ssm = _matmul_tn_shards(zg, dy_ssm, "dw_o_ssm", True)
    p_w_glu = _matmul_tn_shards(z, dt, "dw_glu", False)
    du, dlam, d_d, d_bblk, d_cblk_t, l_wq, l_wkv, l_w_glu, l_w_o_ssm = _ssm_bwd(
        u, dy, st, bblk, cblk, lam, d_row, xch=[(p_wq, True), (p_wkv, True), (p_w_glu, True), (p_w_o_ssm, True)])
    d_a_re, d_a_im, d_log_dt, d_bt_re, d_bt_im, d_c_re, d_c_im = _ssm_prep_bwd(
        a_re, a_im, log_dt, bt_re, bt_im, dlam, d_bblk, d_cblk_t)
    dx, dproj, d_norm_mix = _in_proj_bwd((du, dql, dkvl, dgs, dgm), dh, x, small["norm_mix"], w_in_pad)
    tr = lambda mat: jnp.transpose(mat.reshape(SSM_GROUPS, SSM_GROUP_CH, SSM_STATE), (0, 2, 1))
    g_small = {
        "norm_mix": d_norm_mix, "q_a_norm": d_q_a_norm, "kv_a_norm": d_kv_a_norm,
        "q_norm": d_gq[:, :QK_HEAD], "k_norm": d_gk[:, :QK_HEAD],
        "ssm_a_re": d_a_re, "ssm_a_im": d_a_im, "ssm_log_dt": d_log_dt,
        "ssm_b_re": tr(d_bt_re), "ssm_b_im": tr(d_bt_im), "ssm_c_re": d_c_re, "ssm_c_im": d_c_im,
        "ssm_d": d_d, "b_glu": d_b_glu, "norm_mlp": d_norm_mlp,
    }
    g_w_in_pad, g_small_all = _matmul_tn(
        xn_t, dproj, "dw_in", xch=[(jnp.concatenate([_pack_small(g_small), loss], axis=0), False)], turned=True)
    parts = jnp.transpose(_unpad_in(g_w_in_pad).reshape(D_MODEL, N_DEV, IN_SHARD), (1, 0, 2))
    g_w_in_mine = _reduce_scatter(parts, "reduce_w_in")
    landed = {"w_q_b": l_wq[:, :, :QK_HEAD], "w_kv_b": l_wkv, "w_o_mla": l_w_o_mla, "w_glu": l_w_glu,
              "w_o_ssm": l_w_o_ssm, "w_out": l_w_out, "w_up": l_w_up, "w_down": l_w_down}
    return dx, landed, g_w_in_mine, g_small_all


def kernel(x, positions, norm_mix, w_in, q_a_norm, kv_a_norm, w_q_b, w_kv_b, q_norm, k_norm, w_o_mla, ssm_a_re, ssm_a_im, ssm_log_dt, ssm_b_re, ssm_b_im, ssm_c_re, ssm_c_im, ssm_d, w_glu, b_glu, w_o_ssm, w_out, norm_mlp, w_up, w_down, loss_target, m_norm_mix, m_w_in, m_q_a_norm, m_kv_a_norm, m_w_q_b, m_w_kv_b, m_q_norm, m_k_norm, m_w_o_mla, m_ssm_a_re, m_ssm_a_im, m_ssm_log_dt, m_ssm_b_re, m_ssm_b_im, m_ssm_c_re, m_ssm_c_im, m_ssm_d, m_w_glu, m_b_glu, m_w_o_ssm, m_w_out, m_norm_mlp, m_w_up, m_w_down, v_norm_mix, v_w_in, v_q_a_norm, v_kv_a_norm, v_w_q_b, v_w_kv_b, v_q_norm, v_k_norm, v_w_o_mla, v_ssm_a_re, v_ssm_a_im, v_ssm_log_dt, v_ssm_b_re, v_ssm_b_im, v_ssm_c_re, v_ssm_c_im, v_ssm_d, v_w_glu, v_b_glu, v_w_o_ssm, v_w_out, v_norm_mlp, v_w_up, v_w_down):
    given = dict(locals())
    w = {n: given[n] for n in WEIGHT_ORDER}
    m = {n: given["m_" + n] for n in WEIGHT_ORDER}
    v = {n: given["v_" + n] for n in WEIGHT_ORDER}
    big = [n for n in WEIGHT_ORDER if n not in SMALL]
    small = {n: w[n] for n in SMALL}

    dx, landed, g_w_in, g_small_all = _step(
        x[0], positions.reshape(-1, 1), loss_target[0], {n: w[n][0] for n in big}, small)

    grads, deltas, new_m, new_v = {}, {}, {}, {}
    for n in big:
        if n == "w_in":
            g = g_w_in
            d, nm, nv = _adamw(w[n][0], g, m[n][0], v[n][0], "adamw_" + n)
        else:
            g, d, nm, nv = _adamw_sum(landed[n], w[n][0], m[n][0], v[n][0], "adamw_" + n)
        grads[n], deltas[n], new_m[n], new_v[n] = g[None], d[None], nm[None], nv[None]

    outs = _adamw_small(g_small_all, _pack_small(small), _pack_small({n: m[n] for n in SMALL}),
                        _pack_small({n: v[n] for n in SMALL}), _small_rows(small))
    for p, n in enumerate(SMALL):
        for k, dst in enumerate((grads, deltas, new_m, new_v)):
            dst[n] = outs[1 + 4 * p + k].reshape(-1)[:small[n].size].reshape(small[n].shape)

    return (outs[0][0, 0], dx[None], *[grads[n] for n in WEIGHT_ORDER], *[deltas[n] for n in WEIGHT_ORDER],
            *[new_m[n] for n in WEIGHT_ORDER], *[new_v[n] for n in WEIGHT_ORDER])
```

```python
import functools
import math

import numpy as np
import jax
import jax.numpy as jnp
from jax import lax
from jax.experimental import pallas as pl
from jax.experimental.pallas import tpu as pltpu

F32 = jnp.float32
BF16 = jnp.bfloat16

D_MODEL = 1024
SSM_GROUPS = 32
SSM_GROUP_CH = 16
SSM_WIDTH = 512
SSM_STATE = 64
N_STATE = SSM_GROUPS * SSM_STATE
N_HEADS = 8
QK_NOPE = 128
QK_ROPE = 64
QK_HEAD = 192
QK_PAD = 256
V_HEAD = 128
Q_LORA = 384
KV_LORA = 256
KV_LAT_PAD = 384
ROPE_THETA = 10000.0
D_FF = 4096
EPS = 1e-6
ATT_SCALE = QK_HEAD ** -0.5
N_DEV = 8
FF_SHARD = D_FF // N_DEV
OUT_SHARD = D_MODEL // N_DEV

IN_SEGS = ((0, 512), (512, 896), (896, 1280), (1280, 2304), (2304, 3328))
D_IN = 3264
D_IN_PAD = 3328
KV_END = 1216

ADAM_LR = 0.001
ADAM_B1 = 0.9
ADAM_B2 = 0.999
ADAM_EPS = 1e-08
ADAM_WD = 0.01
ADAM_STEP = 10

VMEM_LIMIT = 56 * 1024 * 1024
MESH = pl.DeviceIdType.MESH
ANY = pl.BlockSpec(memory_space=pl.ANY)
LANES = 128

SCAN_T = 256
SUBCHUNKS = 8
SCAN_CG = 512
ATT_T = 512
ATT_SUB = 2
ROW_T = 256
MM_T = 512


def _params(sem=None):
    return pltpu.CompilerParams(dimension_semantics=sem, vmem_limit_bytes=VMEM_LIMIT)


def _rows(arr, tm):
    if arr.ndim == 2:
        return pl.BlockSpec((tm, arr.shape[1]), lambda i: (i, 0))
    return pl.BlockSpec((arr.shape[0], tm, arr.shape[2]), lambda i: (0, i, 0))


def _const(arr):
    nd = arr.ndim
    return pl.BlockSpec(arr.shape, lambda *_: (0,) * nd, pipeline_mode=pl.Buffered(1))


def _sds(shape, dtype):
    return jax.ShapeDtypeStruct(shape, dtype)


PEERS = tuple((dx, dy, dc) for dx in (0, 1) for dy in (0, 1) for dc in (0, 1) if (dx, dy, dc) != (0, 0, 0))


def _here():
    x, y, c = lax.axis_index("x"), lax.axis_index("y"), lax.axis_index("c")
    return x, y, c, 4 * x + 2 * y + c


def _xchg_start(scatter, srcs, dsts, send, recv, local):
    x, y, c, me = _here()
    for e, sc in enumerate(scatter):
        src, dst = srcs[e], dsts[e]
        pltpu.make_async_copy(src.at[me] if sc else src, dst.at[me], local.at[e]).start()
        for dx, dy, dc in PEERS:
            px, py, pc = (1 - x if dx else x), (1 - y if dy else y), (1 - c if dc else c)
            pltpu.make_async_remote_copy(
                src_ref=src.at[4 * px + 2 * py + pc] if sc else src, dst_ref=dst.at[me],
                send_sem=send.at[e], recv_sem=recv.at[e], device_id=(px, py, pc), device_id_type=MESH).start()


def _xchg_wait(scatter, srcs, dsts, send, recv, local):
    x, y, c, me = _here()
    for e, sc in enumerate(scatter):
        src, dst = srcs[e], dsts[e]
        pltpu.make_async_copy(src.at[me] if sc else src, dst.at[me], local.at[e]).wait()
        span = dst.at[pl.ds(0, N_DEV - 1)]
        both = pltpu.make_async_remote_copy(src_ref=span, dst_ref=span, send_sem=send.at[e], recv_sem=recv.at[e],
                                            device_id=(x, y, c), device_id_type=MESH)
        both.wait_send()
        both.wait_recv()


def _call(body, name, grid, ins, in_specs, outs, out_specs, scratch=(), xch=()):
    n_in, n_out, ne = len(ins), len(outs), len(xch)
    scatter = [sc for _, sc in xch]
    x_outs = [_sds((N_DEV,) + (a.shape[1:] if sc else a.shape), a.dtype) for a, sc in xch]
    sems = [pltpu.SemaphoreType.DMA((ne,))] * 3 if ne else []

    def wrapped(*refs):
        in_refs, x_src = refs[:n_in], refs[n_in:n_in + ne]
        out_refs = refs[n_in + ne:n_in + ne + n_out]
        x_dst = refs[n_in + ne + n_out:n_in + 2 * ne + n_out]
        rest = refs[n_in + 2 * ne + n_out:]
        if ne:
            x_sems, rest = rest[len(rest) - 3:], rest[:len(rest) - 3]
            first = functools.reduce(jnp.logical_and, [pl.program_id(d) == 0 for d in range(len(grid))])
            last = functools.reduce(jnp.logical_and, [pl.program_id(d) == grid[d] - 1 for d in range(len(grid))])

            @pl.when(first)
            def _():
                _xchg_start(scatter, x_src, x_dst, *x_sems)

        body(*in_refs, *out_refs, *rest)
        if ne:
            @pl.when(last)
            def _():
                _xchg_wait(scatter, x_src, x_dst, *x_sems)

    return pl.pallas_call(
        wrapped,
        name=name,
        grid=grid,
        in_specs=list(in_specs) + [ANY] * ne,
        out_specs=list(out_specs) + [ANY] * ne,
        out_shape=list(outs) + x_outs,
        scratch_shapes=list(scratch) + sems,
        compiler_params=_params(("arbitrary",) * len(grid)),
    )(*ins, *[a for a, _ in xch])


def _row_call(body, name, n_rows, tm, row_ins, const_ins, row_outs, acc_outs=(), xch=(), col_outs=()):
    outs = [_sds(s, d) for s, d in list(row_outs) + list(col_outs) + list(acc_outs)]
    n_row, n_col = len(row_outs), len(col_outs)
    out_specs = [_rows(o, tm) for o in outs[:n_row]] + [
        pl.BlockSpec((o.shape[0], tm), lambda i: (0, i)) for o in outs[n_row:n_row + n_col]] + [
        pl.BlockSpec(o.shape, lambda i, nd=len(o.shape): (0,) * nd) for o in outs[n_row + n_col:]]
    in_specs = [_rows(a, tm) for a in row_ins] + [_const(a) for a in const_ins]
    return _call(body, name, (n_rows // tm,), list(row_ins) + list(const_ins), in_specs, outs, out_specs, xch=xch)


def _dot(a, b):
    return jnp.dot(a, b, preferred_element_type=F32)


def _dot_nt(a, b):
    return lax.dot_general(a, b, (((1,), (1,)), ((), ())), preferred_element_type=F32)


def _dot_tn(a, b):
    return lax.dot_general(a, b, (((0,), (0,)), ((), ())), preferred_element_type=F32)


def _rms(x, g, n):
    inv = lax.rsqrt(jnp.sum(x * x, -1, keepdims=True) * (1.0 / n) + EPS)
    return x * inv * g, inv


def _rms_bwd(dy, x, g, inv, n):
    xh = x * inv
    dxh = dy * g
    dx = inv * (dxh - xh * (jnp.sum(dxh * xh, -1, keepdims=True) * (1.0 / n)))
    return dx, dy * xh


def _sigmoid(x):
    return 1.0 / (1.0 + jnp.exp(-x))


_GELU_C = math.sqrt(2.0 / math.pi)


def _gelu(y):
    th = jnp.tanh(_GELU_C * (y + 0.044715 * (y * y * y)))
    return 0.5 * y * (1.0 + th), th


def _gelu_grad(y, th):
    return 0.5 * (1.0 + th) + 0.5 * y * (1.0 - th * th) * (_GELU_C * (1.0 + 3.0 * 0.044715 * (y * y)))


def _acc(ref, val):
    @pl.when(pl.program_id(0) == 0)
    def _():
        ref[...] = jnp.zeros_like(ref)

    ref[...] += val


def _tile(n, limit):
    if n <= limit:
        return n
    return max(t for t in range(128, limit + 1, 128) if n % t == 0)


def _lhs(a, turned, tm, tk):
    m, k_dim = a.shape if turned else a.shape[::-1]
    tm, tk = _tile(m, tm), _tile(k_dim, tk)
    if turned:
        return m, k_dim, tm, tk, pl.BlockSpec((tm, tk), lambda i, k: (i, k)), _dot
    return m, k_dim, tm, tk, pl.BlockSpec((tk, tm), lambda i, k: (k, i)), _dot_tn


def _matmul_tn(a, b, name, tm=512, tk=512, xch=(), turned=False):
    m, k_dim, tm, tk, a_spec, dot = _lhs(a, turned, tm, tk)
    n = b.shape[1]

    def body(a_ref, b_ref, o_ref):
        @pl.when(pl.program_id(1) == 0)
        def _():
            o_ref[...] = jnp.zeros_like(o_ref)

        o_ref[...] += dot(a_ref[...].astype(BF16), b_ref[...].astype(BF16))

    outs = _call(
        body, name, (m // tm, k_dim // tk), [a, b], [a_spec, pl.BlockSpec((tk, n), lambda i, k: (k, 0))],
        [_sds((m, n), F32)], [pl.BlockSpec((tm, n), lambda i, k: (i, 0))], xch=xch)
    return outs if xch else outs[0]


def _matmul_tn_shards(a, b, name, by_col, tm=512, tk=512, turned=False):
    m, k_dim, tm, tk, a_spec, dot = _lhs(a, turned, tm, tk)
    n = b.shape[1]
    nk = k_dim // tk
    if by_col:
        r, c = m, n // N_DEV
        out_spec = pl.BlockSpec((N_DEV, tm, c), lambda i, k: (0, i, 0))
    else:
        r, c = m // N_DEV, n
        per = tm // r
        out_spec = pl.BlockSpec((per, r, c), lambda i, k: (i, 0, 0))

    def body(a_ref, b_ref, o_ref, acc_ref):
        k = pl.program_id(1)

        @pl.when(k == 0)
        def _():
            acc_ref[...] = jnp.zeros_like(acc_ref)

        acc_ref[...] += dot(a_ref[...].astype(BF16), b_ref[...].astype(BF16))

        @pl.when(k == nk - 1)
        def _():
            if by_col:
                for j in range(N_DEV):
                    o_ref[j] = acc_ref[:, j * c:(j + 1) * c].astype(BF16)
            else:
                for s in range(per):
                    o_ref[s] = acc_ref[s * r:(s + 1) * r, :].astype(BF16)

    return pl.pallas_call(
        body,
        name=name,
        grid=(m // tm, nk),
        in_specs=[a_spec, pl.BlockSpec((tk, n), lambda i, k: (k, 0))],
        out_specs=out_spec,
        out_shape=_sds((N_DEV, r, c), BF16),
        scratch_shapes=[pltpu.VMEM((tm, n), F32)],
        compiler_params=_params(("parallel", "arbitrary")),
    )(a, b)


def _rope_tables(pos_col):
    n = pos_col.shape[0]
    half = QK_ROPE // 2
    inv_freq = (ROPE_THETA ** (-np.arange(half, dtype=np.float32) / half)).astype(np.float32)
    freq_row = jnp.asarray(np.concatenate([inv_freq, inv_freq, np.zeros(64, np.float32)])[None, :])

    def body(p_ref, f_ref, c_ref, s_ref):
        ang = p_ref[...].astype(F32) * f_ref[...]
        c_ref[...] = jnp.cos(ang)
        s_ref[...] = jnp.sin(ang)

    return _row_call(body, "rope_tables", n, min(n, 1024), [pos_col], [freq_row], [((n, 128), F32)] * 2)


def _rope_rot(v):
    lane = lax.broadcasted_iota(jnp.int32, v.shape, 1)
    return jnp.where(lane < 32, -pltpu.roll(v, 96, 1), jnp.where(lane < 64, pltpu.roll(v, 32, 1), 0.0))


def _rope_rot_t(v):
    lane = lax.broadcasted_iota(jnp.int32, v.shape, 1)
    return jnp.where(lane < 32, pltpu.roll(v, 96, 1), jnp.where(lane < 64, -pltpu.roll(v, 32, 1), 0.0))


def _in_proj(x, norm_mix, w_in_pad, xch):
    n = x.shape[0]

    def body(x_ref, g_ref, w_ref, u_ref, ql_ref, kvl_ref, gs_ref, gm_ref, xnt_ref):
        xn, _ = _rms(x_ref[...], g_ref[...], D_MODEL)
        xb = xn.astype(BF16)
        xnt_ref[...] = xn.T.astype(BF16)
        for ref, (a, b) in zip((u_ref, ql_ref, kvl_ref, gs_ref, gm_ref), IN_SEGS):
            ref[...] = _dot(xb, w_ref[:, a:b])

    outs = [((n, b - a), F32) for a, b in IN_SEGS]
    return _row_call(body, "in_proj", n, MM_T, [x], [norm_mix, w_in_pad], outs, xch=xch,
                     col_outs=[((D_MODEL, n), BF16)])


def _ssm_prep_fn(a_re, a_im, log_dt, b_re_x, b_im_x):
    dt = jnp.exp(log_dt)
    mag = jnp.exp(a_re * dt)
    lr = mag * jnp.cos(a_im * dt)
    li = mag * jnp.sin(a_im * dt)
    den = a_re * a_re + a_im * a_im
    fr = ((lr - 1.0) * a_re + li * a_im) / den
    fi = (li * a_re - (lr - 1.0) * a_im) / den
    return lr, li, fr * b_re_x - fi * b_im_x, fr * b_im_x + fi * b_re_x


def _dot_exact(a, b, dims):
    return lax.dot_general(a, b, (dims, ((), ())), precision=lax.Precision.HIGHEST, preferred_element_type=F32)


def _lane_repeat(width, n):
    src = lax.broadcasted_iota(jnp.int32, (width, n), 0)
    dst = lax.broadcasted_iota(jnp.int32, (width, n), 1)
    return (dst % width == src).astype(F32)


def _same_group(rows, rows_per_group, cols, cols_per_group):
    row = lax.broadcasted_iota(jnp.int32, (rows, cols), 0)
    col = lax.broadcasted_iota(jnp.int32, (rows, cols), 1)
    return (row // rows_per_group) == (col // cols_per_group)


def _expand_b(bt):
    tiled = _dot_exact(bt, _lane_repeat(SSM_STATE, N_STATE), ((1,), (0,)))
    return jnp.where(_same_group(SSM_WIDTH, SSM_GROUP_CH, N_STATE, SSM_STATE), tiled, 0.0)


def _collect_b(m):
    masked = jnp.where(_same_group(SSM_WIDTH, SSM_GROUP_CH, N_STATE, SSM_STATE), m, 0.0)
    return _dot_exact(masked, _lane_repeat(SSM_STATE, N_STATE), ((1,), (1,)))


def _ssm_prep(a_re, a_im, log_dt, bt_re, bt_im, c2_re, c2_im):
    def body(ar, ai, ld, br, bi, cr, ci, lam_ref, bblk_ref, cblk_ref):
        lr, li, bbr, bbi = _ssm_prep_fn(ar[...], ai[...], ld[...], _expand_b(br[...]), _expand_b(bi[...]))
        lam_ref[0:1, :] = lr
        lam_ref[1:2, :] = li
        bblk_ref[:, 0:N_STATE] = bbr.astype(BF16)
        bblk_ref[:, N_STATE:] = bbi.astype(BF16)
        rep = _lane_repeat(SSM_GROUP_CH, SSM_WIDTH)
        own = _same_group(N_STATE, SSM_STATE, SSM_WIDTH, SSM_GROUP_CH)
        cblk_ref[0:N_STATE, :] = jnp.where(own, _dot_exact(cr[...], rep, ((1,), (0,))), 0.0).astype(BF16)
        cblk_ref[N_STATE:, :] = jnp.where(own, -_dot_exact(ci[...], rep, ((1,), (0,))), 0.0).astype(BF16)

    return pl.pallas_call(
        body,
        name="ssm_prep",
        out_shape=[_sds((2, N_STATE), F32), _sds((SSM_WIDTH, 2 * N_STATE), BF16),
                   _sds((2 * N_STATE, SSM_WIDTH), BF16)],
        compiler_params=_params(),
    )(a_re, a_im, log_dt, bt_re, bt_im, c2_re, c2_im)


def _ssm_prep_bwd(a_re, a_im, log_dt, bt_re, bt_im, dlam, dbblk, dcblk_t):
    def body(ar, ai, ld, br, bi, dl, db, dc, dar, dai, dld, dbr, dbi, dcr, dci):
        _, vjp = jax.vjp(_ssm_prep_fn, ar[...], ai[...], ld[...], _expand_b(br[...]), _expand_b(bi[...]))
        g = vjp((dl[0:1, :], dl[1:2, :], db[:, 0:N_STATE], db[:, N_STATE:]))
        dar[...] = g[0]
        dai[...] = g[1]
        grp = lax.broadcasted_iota(jnp.int32, (SSM_GROUPS, N_STATE), 0)
        lane = lax.broadcasted_iota(jnp.int32, (SSM_GROUPS, N_STATE), 1)
        sel = (lane // SSM_STATE) == grp
        dld[...] = jnp.sum(jnp.where(sel, jnp.broadcast_to(g[2], (SSM_GROUPS, N_STATE)), 0.0), axis=1, keepdims=True)
        dbr[...] = _collect_b(g[3])
        dbi[...] = _collect_b(g[4])
        dcr[...] = _collect_b(dc[:, 0:N_STATE])
        dci[...] = -_collect_b(dc[:, N_STATE:])

    small = _sds((SSM_WIDTH, SSM_STATE), F32)
    return pl.pallas_call(
        body,
        name="ssm_prep_bwd",
        out_shape=[_sds((1, N_STATE), F32), _sds((1, N_STATE), F32), _sds((SSM_GROUPS, 1), F32), small, small, small, small],
        compiler_params=_params(),
    )(a_re, a_im, log_dt, bt_re, bt_im, dlam, dbblk, dcblk_t)


def _perm_matrix(t):
    run = t // SUBCHUNKS
    p = np.zeros((t, t), np.float32)
    r = np.arange(t)
    p[r, (r % SUBCHUNKS) * run + r // SUBCHUNKS] = 1.0
    return jnp.asarray(p, dtype=BF16)


def _unpermute(p, a):
    hi = a.astype(BF16)
    r1 = a - hi.astype(F32)
    mid = r1.astype(BF16)
    lo = (r1 - mid.astype(F32)).astype(BF16)
    return _dot_tn(p, hi) + _dot_tn(p, mid) + _dot_tn(p, lo)


def _power_table(lam_ref, pw_ref, n):
    lr, li = lam_ref[0:1, :], lam_ref[1:2, :]
    pw_ref[0:1, 0:N_STATE] = lr
    pw_ref[0:1, N_STATE:] = li

    def step(i, carry):
        pr, pi = carry
        pr, pi = pr * lr - pi * li, pr * li + pi * lr
        pw_ref[pl.ds(i, 1), 0:N_STATE] = pr
        pw_ref[pl.ds(i, 1), N_STATE:] = pi
        return pr, pi

    lax.fori_loop(1, n, step, (lr, li))


def _col_groups():
    return [(pl.ds(c, SCAN_CG), pl.ds(N_STATE + c, SCAN_CG)) for c in range(0, N_STATE, SCAN_CG)]


def _run_scan(buf, lam_ref, t, reverse):
    nblk = t // 8
    for re, im in _col_groups():
        lr = jnp.broadcast_to(lam_ref[0:1, re], (8, SCAN_CG))
        li = jnp.broadcast_to(lam_ref[1:2, re], (8, SCAN_CG))
        if reverse:
            li = -li
        first = pl.ds((nblk - 1) * 8 if reverse else 0, 8)

        def step(k, carry, re=re, im=im, lr=lr, li=li):
            pr, pi = carry
            i = (nblk - 2 - k) if reverse else (k + 1)
            r = pl.ds(pl.multiple_of(i * 8, 8), 8)
            xr = buf[r, re] + lr * pr - li * pi
            xi = buf[r, im] + lr * pi + li * pr
            buf[r, re] = xr
            buf[r, im] = xi
            return xr, xi

        lax.fori_loop(0, nblk - 1, step, (buf[first, re], buf[first, im]))


def _run_carries(buf, pw_ref, carry_ref, s_ref, t, reverse):
    nblk = t // 8
    run = t // SUBCHUNKS
    edge = buf[pl.ds(0 if reverse else (nblk - 1) * 8, 8), :]
    pr, pi = pw_ref[run - 1:run, 0:N_STATE], pw_ref[run - 1:run, N_STATE:]
    if reverse:
        pi = -pi
    sr, si = carry_ref[0:1, 0:N_STATE], carry_ref[0:1, N_STATE:]
    for s in (range(SUBCHUNKS - 1, -1, -1) if reverse else range(SUBCHUNKS)):
        s_ref[s:s + 1, 0:N_STATE] = sr
        s_ref[s:s + 1, N_STATE:] = si
        er, ei = edge[s:s + 1, 0:N_STATE], edge[s:s + 1, N_STATE:]
        sr, si = er + pr * sr - pi * si, ei + pr * si + pi * sr
    carry_ref[:, 0:N_STATE] = jnp.broadcast_to(sr, (8, N_STATE))
    carry_ref[:, N_STATE:] = jnp.broadcast_to(si, (8, N_STATE))


def _run_fix(buf, pw_ref, s_ref, t, reverse):
    nblk = t // 8
    for re, im in _col_groups():
        sr, si = s_ref[:, re], s_ref[:, im]

        def step(i, carry, re=re, im=im, sr=sr, si=si):
            r = pl.ds(pl.multiple_of(i * 8, 8), 8)
            row = pl.ds((nblk - 1 - i) if reverse else i, 1)
            pr, pi = pw_ref[row, re], pw_ref[row, im]
            if reverse:
                pi = -pi
            buf[r, re] += pr * sr - pi * si
            buf[r, im] += pr * si + pi * sr
            return carry

        lax.fori_loop(0, nblk, step, 0)


STATE_BLOCKS = 2 * N_STATE // LANES
CH_BLOCKS = SSM_WIDTH // LANES


def _state_block(b):
    pair = b % (N_STATE // LANES)
    k = (pair * 2 * SSM_GROUP_CH) // LANES
    return slice(b * LANES, (b + 1) * LANES), slice(k * LANES, (k + 1) * LANES)


def _channel_block(c):
    w = N_STATE // CH_BLOCKS
    return slice(c * LANES, (c + 1) * LANES), slice(c * w, (c + 1) * w), slice(N_STATE + c * w, N_STATE + (c + 1) * w)


def _to_states(vb, w_ref, buf, nt):
    for b in range(STATE_BLOCKS):
        lanes, ch = _state_block(b)
        buf[:, lanes] = _dot_nt(vb[:, ch], w_ref[lanes, ch]) if nt else _dot(vb[:, ch], w_ref[ch, lanes])


def _to_channels(buf, w_ref, nt):
    outs = []
    for c in range(CH_BLOCKS):
        ch, re, im = _channel_block(c)
        xr, xi = buf[:, re].astype(BF16), buf[:, im].astype(BF16)
        if nt:
            outs.append(_dot_nt(xr, w_ref[ch, re]) + _dot_nt(xi, w_ref[ch, im]))
        else:
            outs.append(_dot(xr, w_ref[re, ch]) + _dot(xi, w_ref[im, ch]))
    return jnp.concatenate(outs, axis=-1)


def _ssm_fwd(u, bblk, cblk, lam, d_row, w_glu, b_glu, w_o_ssm, xch):
    n = u.shape[0]
    t = min(SCAN_T, n)
    perm = _perm_matrix(t)

    def body(u_ref, p_ref, bblk_ref, cblk_ref, lam_ref, d_ref, wg_ref, bg_ref, wo_ref, y_ref, ys_ref, st_ref,
             buf, pw_ref, carry_ref, s_ref):
        @pl.when(pl.program_id(0) == 0)
        def _():
            carry_ref[...] = jnp.zeros_like(carry_ref)
            _power_table(lam_ref, pw_ref, t // SUBCHUNKS)

        st_ref[0] = carry_ref[...]
        u_t = u_ref[...]
        p = p_ref[...]
        ub = _dot(p, u_t.astype(BF16)).astype(BF16)
        _to_states(ub, bblk_ref, buf, False)
        _run_scan(buf, lam_ref, t, False)
        _run_carries(buf, pw_ref, carry_ref, s_ref, t, False)
        _run_fix(buf, pw_ref, s_ref, t, False)
        y = d_ref[...] * u_t + _unpermute(p, _to_channels(buf, cblk_ref, False))
        y_ref[...] = y
        z, _ = _gelu(y)
        s = _sigmoid(_dot(z.astype(BF16), wg_ref[...]) + bg_ref[...])
        zgb = (z * s).astype(BF16)
        for j in range(N_DEV):
            ys_ref[:, j * OUT_SHARD:(j + 1) * OUT_SHARD] = _dot(zgb, wo_ref[j])

    consts = [perm, bblk, cblk, lam, d_row, w_glu, b_glu, w_o_ssm]
    return _call(
        body, "ssm_fwd", (n // t,), [u] + consts, [_rows(u, t)] + [_const(a) for a in consts],
        [_sds((n, SSM_WIDTH), F32), _sds((n, D_MODEL), F32), _sds((n // t, 8, 2 * N_STATE), F32)],
        [pl.BlockSpec((t, SSM_WIDTH), lambda i: (i, 0)), pl.BlockSpec((t, D_MODEL), lambda i: (i, 0)),
         pl.BlockSpec((1, 8, 2 * N_STATE), lambda i: (i, 0, 0))],
        scratch=[pltpu.VMEM((t, 2 * N_STATE), F32), pltpu.VMEM((t // SUBCHUNKS, 2 * N_STATE), F32),
                 pltpu.VMEM((8, 2 * N_STATE), F32), pltpu.VMEM((8, 2 * N_STATE), F32)],
        xch=xch)


def _head_norm_rope(slab, gain, cos_t, sin_t):
    xn, inv = _rms(slab, gain, QK_HEAD)
    lo, hi = xn[:, 0:128], xn[:, 128:256]
    return jnp.concatenate([lo, hi * cos_t + _rope_rot(hi) * sin_t], axis=-1), inv


def _head_norm_rope_bwd(g, slab, gain, inv, cos_t, sin_t):
    g_lo, g_hi = g[:, 0:128], g[:, 128:256]
    g_n = jnp.concatenate([g_lo, g_hi * cos_t + _rope_rot_t(g_hi * sin_t)], axis=-1)
    return _rms_bwd(g_n, slab, gain, inv, QK_HEAD)


def _qkv_prep(ql, kvl, q_a_norm, kv_a_norm, wq, wkv, gq, gk, cos_t, sin_t):
    n = ql.shape[0]
    tm = ROW_T

    def body(ql_ref, kvl_ref, cos_ref, sin_ref, qa_ref, ka_ref, wq_ref, wkv_ref, gq_ref, gk_ref,
             q_ref, k_ref, v_ref, kt_ref, vt_ref):
        cos_t, sin_t = cos_ref[...], sin_ref[...]
        qa, _ = _rms(ql_ref[...], qa_ref[...], Q_LORA)
        qab = qa.astype(BF16)
        kvl_t = kvl_ref[...]
        ca, _ = _rms(kvl_t[:, 0:KV_LORA], ka_ref[...], KV_LORA)
        cab = ca.astype(BF16)
        kpe = kvl_t[:, KV_LORA:KV_LAT_PAD]
        q_pre = _dot(qab, wq_ref[...])
        kv_pre = _dot(cab, wkv_ref[...])
        for h in range(N_HEADS):
            qh, _ = _head_norm_rope(q_pre[:, h * QK_PAD:(h + 1) * QK_PAD], gq_ref[...], cos_t, sin_t)
            q_ref[h] = (qh * ATT_SCALE).astype(BF16)
            kv_h = kv_pre[:, h * QK_PAD:(h + 1) * QK_PAD]
            kh, _ = _head_norm_rope(jnp.concatenate([kv_h[:, 0:QK_NOPE], kpe], axis=-1), gk_ref[...], cos_t, sin_t)
            k_ref[h] = kh.astype(BF16)
            kt_ref[h] = kh.T.astype(BF16)
            vh = kv_h[:, QK_NOPE:]
            v_ref[h] = vh.astype(BF16)
            vt_ref[h] = vh.T.astype(BF16)

    row_ins, consts = [ql, kvl, cos_t, sin_t], [q_a_norm, kv_a_norm, wq, wkv, gq, gk]
    outs = [_sds((N_HEADS, n, QK_PAD), BF16), _sds((N_HEADS, n, QK_PAD), BF16), _sds((N_HEADS, n, V_HEAD), BF16),
            _sds((N_HEADS, QK_PAD, n), BF16), _sds((N_HEADS, V_HEAD, n), BF16)]
    out_specs = [_rows(o, tm) for o in outs[:3]] + [
        pl.BlockSpec((N_HEADS, QK_PAD, tm), lambda i: (0, 0, i)), pl.BlockSpec((N_HEADS, V_HEAD, tm), lambda i: (0, 0, i))]
    return _call(body, "qkv_prep", (n // tm,), row_ins + consts,
                 [_rows(a, tm) for a in row_ins] + [_const(a) for a in consts], outs, out_specs)


def _causal_mask_t(st, t):
    key = lax.broadcasted_iota(jnp.int32, (t, t), 0)
    qry = lax.broadcasted_iota(jnp.int32, (t, t), 1)
    return jnp.where(key <= qry, st, -jnp.inf)


def _attn_fwd(q, k, vt, xch):
    n = q.shape[1]
    t = min(ATT_T, n)

    def body(q_ref, k_ref, vt_ref, o_ref, lse_ref, ot_ref):
        i = pl.program_id(1)
        qt = q_ref[0]

        def kv_tile(j, carry, diag):
            m, l, acc = carry
            ts = t // ATT_SUB
            sts = []
            for a in range(ATT_SUB):
                r0 = pl.multiple_of(j * t + a * ts, ts)
                st = _dot_nt(k_ref[0, pl.ds(r0, ts), :], qt)
                if diag:
                    key = lax.broadcasted_iota(jnp.int32, (ts, t), 0) + a * ts
                    qry = lax.broadcasted_iota(jnp.int32, (ts, t), 1)
                    st = jnp.where(key <= qry, st, -jnp.inf)
                sts.append(st)
            for a, st in enumerate(sts):
                r0 = pl.multiple_of(j * t + a * ts, ts)
                m_new = jnp.maximum(m, jnp.max(st, 0, keepdims=True))
                alpha = jnp.exp(m - m_new)
                pt = jnp.exp(st - m_new)
                l = alpha * l + jnp.sum(pt, 0, keepdims=True)
                acc = alpha * acc + _dot(vt_ref[0, :, pl.ds(r0, ts)], pt.astype(BF16))
                m = m_new
            return m, l, acc

        init = (jnp.full((1, t), -jnp.inf, F32), jnp.zeros((1, t), F32), jnp.zeros((V_HEAD, t), F32))
        carry = lax.fori_loop(0, i, functools.partial(kv_tile, diag=False), init)
        m, l, acc = kv_tile(i, carry, True)
        out_t = acc / l
        o_ref[...] = out_t.T
        ot_ref[...] = out_t.astype(BF16)
        lse_ref[0] = m + jnp.log(l)

    return _call(
        body, "attn_fwd", (N_HEADS, n // t), [q, k, vt],
        [pl.BlockSpec((1, t, QK_PAD), lambda h, i: (h, i, 0)), pl.BlockSpec((1, n, QK_PAD), lambda h, i: (h, 0, 0)),
         pl.BlockSpec((1, V_HEAD, n), lambda h, i: (h, 0, 0))],
        [_sds((n, N_HEADS * V_HEAD), F32), _sds((N_HEADS, 1, n), F32), _sds((N_HEADS * V_HEAD, n), BF16)],
        [pl.BlockSpec((t, V_HEAD), lambda h, i: (i, h)), pl.BlockSpec((1, 1, t), lambda h, i: (h, 0, i)),
         pl.BlockSpec((V_HEAD, t), lambda h, i: (h, i))],
        xch=xch)


def _merge(attn, gs, gm, y_ssm, x, w_o_mla, w_out):
    n = x.shape[0]

    def body(at_ref, gs_ref, gm_ref, ys_ref, x_ref, wo_ref, wout_ref, h_ref, ym_ref, mxt_ref):
        y_mla = _dot(at_ref[...].astype(BF16), wo_ref[...])
        ym_ref[...] = y_mla
        mixed = _sigmoid(gs_ref[...]) * ys_ref[...] + _sigmoid(gm_ref[...]) * y_mla
        mxt_ref[...] = mixed.T.astype(BF16)
        h_ref[...] = x_ref[...] + _dot(mixed.astype(BF16), wout_ref[...])

    outs = [((n, D_MODEL), F32), ((n, D_MODEL), F32)]
    return _row_call(body, "merge", n, MM_T, [attn, gs, gm, y_ssm, x], [w_o_mla, w_out], outs,
                     col_outs=[((D_MODEL, n), BF16)])


def _mlp_fwd_loss(h, target, norm_mlp, w_up, w_down):
    n = h.shape[0]

    def body(h_ref, t_ref, g_ref, wu_ref, wd_ref, hn_ref, do_ref, hnt_ref, loss_ref):
        h_t = h_ref[...]
        hn, _ = _rms(h_t, g_ref[...], D_MODEL)
        hb = hn.astype(BF16)
        hn_ref[...] = hb
        hnt_ref[...] = hn.T.astype(BF16)
        out = h_t
        for j in range(N_DEV):
            a = jnp.maximum(_dot(hb, wu_ref[j]), 0.0)
            out += _dot((a * a).astype(BF16), wd_ref[j])
        err = out - t_ref[...]
        do_ref[...] = err * (1.0 / D_MODEL)
        _acc(loss_ref, jnp.broadcast_to(jnp.sum(err * err) * (0.5 / D_MODEL), loss_ref.shape))

    outs = [((n, D_MODEL), BF16), ((n, D_MODEL), F32)]
    return _row_call(body, "mlp_fwd_loss", n, MM_T, [h, target], [norm_mlp, w_up, w_down], outs, [((8, 128), F32)],
                     col_outs=[((D_MODEL, n), BF16)])


def _mlp_bwd(dout, hn, h, norm_mlp, w_up, w_down):
    n = h.shape[0]

    def body(do_ref, hn_ref, h_ref, g_ref, wu_ref, wd_ref, da_ref, dh_ref, dob_ref, hidt_ref, dg_ref):
        dout_t = do_ref[...]
        doutb = dout_t.astype(BF16)
        dob_ref[...] = doutb
        hb = hn_ref[...]
        dhn = jnp.zeros_like(dout_t)
        for j in range(N_DEV):
            cols = slice(j * FF_SHARD, (j + 1) * FF_SHARD)
            a = jnp.maximum(_dot(hb, wu_ref[j]), 0.0)
            hidt_ref[cols, :] = (a * a).T.astype(BF16)
            da = (_dot_nt(doutb, wd_ref[j]) * (2.0 * a)).astype(BF16)
            da_ref[:, cols] = da
            dhn += _dot_nt(da, wu_ref[j])
        h_t = h_ref[...]
        inv = lax.rsqrt(jnp.sum(h_t * h_t, -1, keepdims=True) * (1.0 / D_MODEL) + EPS)
        dx, dg = _rms_bwd(dhn, h_t, g_ref[...], inv, D_MODEL)
        dh_ref[...] = dout_t + dx
        _acc(dg_ref, jnp.sum(dg, 0, keepdims=True))

    outs = [((n, D_FF), BF16), ((n, D_MODEL), F32), ((n, D_MODEL), BF16)]
    return _row_call(body, "mlp_bwd", n, MM_T, [dout, hn, h], [norm_mlp, w_up, w_down], outs, [((1, D_MODEL), F32)],
                     col_outs=[((D_FF, n), BF16)])


def _merge_bwd(dh, gs, gm, y_ssm, y_mla, w_out, w_o_mla):
    n = dh.shape[0]

    def body(dh_ref, gs_ref, gm_ref, ys_ref, ym_ref, wout_ref, wo_ref, dgs_ref, dgm_ref, dys_ref, dym_ref, dat_ref):
        dmix = _dot_nt(dh_ref[...].astype(BF16), wout_ref[...])
        sgs, sgm = _sigmoid(gs_ref[...]), _sigmoid(gm_ref[...])
        dgs_ref[...] = (dmix * ys_ref[...] * sgs * (1.0 - sgs)).astype(BF16)
        dgm_ref[...] = (dmix * ym_ref[...] * sgm * (1.0 - sgm)).astype(BF16)
        dys_ref[...] = (dmix * sgs).astype(BF16)
        dym = (dmix * sgm).astype(BF16)
        dym_ref[...] = dym
        dat_ref[...] = _dot_nt(dym, wo_ref[...])

    outs = [((n, D_MODEL), BF16)] * 4 + [((n, D_MODEL), F32)]
    return _row_call(body, "merge_bwd", n, MM_T, [dh, gs, gm, y_ssm, y_mla], [w_out, w_o_mla], outs)


def _attn_bwd(q, k, kt, v, out, lse, dout, xch):
    n = q.shape[1]
    t = min(ATT_T, n)
    nt = n // t

    def body(q_ref, k_ref, kt_ref, v_ref, o_ref, lse_ref, do_ref, dq_ref, dk_ref, dv_ref, delta_ref, dqt_ref):
        j = pl.program_id(1)

        @pl.when(j == 0)
        def _():
            dqt_ref[...] = jnp.zeros_like(dqt_ref)
            prod = do_ref[...] * o_ref[...]
            delta_ref[...] = lax.dot_general(jnp.ones((8, V_HEAD), F32), prod, (((1,), (1,)), ((), ())),
                                             precision=lax.Precision.HIGHEST, preferred_element_type=F32)

        k_t = k_ref[0]
        kt_t = kt_ref[0]
        v_t = v_ref[0]

        def q_tile(i, carry, diag):
            dk, dv = carry
            r0 = pl.multiple_of(i * t, t)
            rows = pl.ds(r0, t)
            qt = q_ref[0, rows, :]
            st = _dot_nt(k_t, qt)
            if diag:
                st = _causal_mask_t(st, t)
            pt = jnp.exp(st - lse_ref[0, :, rows])
            dob = do_ref[rows, :].astype(BF16)
            dv = dv + _dot(pt.astype(BF16), dob)
            dst = (pt * (_dot_nt(v_t, dob) - delta_ref[0:1, rows])).astype(BF16)
            dk = dk + _dot(dst, qt)
            dqt_ref[:, rows] += _dot(kt_t, dst)
            return dk, dv

        carry = q_tile(j, (jnp.zeros((t, QK_PAD), F32), jnp.zeros((t, V_HEAD), F32)), True)
        dk, dv = lax.fori_loop(j + 1, nt, functools.partial(q_tile, diag=False), carry)
        dk_ref[0] = dk
        dv_ref[0] = dv

        @pl.when(j == nt - 1)
        def _():
            for c in range(0, n, t):
                dq_ref[0, c:c + t, :] = dqt_ref[:, c:c + t].T

    return _call(
        body, "attn_bwd", (N_HEADS, nt), [q, k, kt, v, out, lse, dout],
        [pl.BlockSpec((1, n, QK_PAD), lambda h, j: (h, 0, 0)), pl.BlockSpec((1, t, QK_PAD), lambda h, j: (h, j, 0)),
         pl.BlockSpec((1, QK_PAD, t), lambda h, j: (h, 0, j)), pl.BlockSpec((1, t, V_HEAD), lambda h, j: (h, j, 0)),
         pl.BlockSpec((n, V_HEAD), lambda h, j: (0, h)), pl.BlockSpec((1, 1, n), lambda h, j: (h, 0, 0)),
         pl.BlockSpec((n, V_HEAD), lambda h, j: (0, h))],
        [_sds((N_HEADS, n, QK_PAD), F32), _sds((N_HEADS, n, QK_PAD), F32), _sds((N_HEADS, n, V_HEAD), F32)],
        [pl.BlockSpec((1, n, QK_PAD), lambda h, j: (h, 0, 0)), pl.BlockSpec((1, t, QK_PAD), lambda h, j: (h, j, 0)),
         pl.BlockSpec((1, t, V_HEAD), lambda h, j: (h, j, 0))],
        scratch=[pltpu.VMEM((8, n), F32), pltpu.VMEM((QK_PAD, n), F32)],
        xch=xch)


def _qkv_prep_bwd(ql, kvl, dq, dk, dv, q_a_norm, kv_a_norm, wq, wkv, gq, gk, cos_t, sin_t, xch):
    n = ql.shape[0]

    def body(ql_ref, kvl_ref, cos_ref, sin_ref, dq_ref, dk_ref, dv_ref, qa_ref, ka_ref, wq_ref, wkv_ref, gq_ref, gk_ref,
             dql_ref, dkvl_ref, qab_ref, dqp_ref, cab_ref, dkvp_ref, dqa_ref, dka_ref, dgq_ref, dgk_ref):
        cos_t, sin_t = cos_ref[...], sin_ref[...]
        ql_t = ql_ref[...]
        qa, inv_qa = _rms(ql_t, qa_ref[...], Q_LORA)
        qab = qa.astype(BF16)
        qab_ref[...] = qab
        kvl_t = kvl_ref[...]
        ckv = kvl_t[:, 0:KV_LORA]
        ca, inv_ca = _rms(ckv, ka_ref[...], KV_LORA)
        cab = ca.astype(BF16)
        cab_ref[...] = cab
        kpe = kvl_t[:, KV_LORA:KV_LAT_PAD]
        dgq = jnp.zeros((1, QK_PAD), F32)
        dgk = jnp.zeros((1, QK_PAD), F32)
        dkpe = jnp.zeros_like(kpe)
        q_pre = _dot(qab, wq_ref[...])
        kv_pre = _dot(cab, wkv_ref[...])
        for h in range(N_HEADS):
            head = slice(h * QK_PAD, (h + 1) * QK_PAD)
            q_slab = q_pre[:, head]
            inv = lax.rsqrt(jnp.sum(q_slab * q_slab, -1, keepdims=True) * (1.0 / QK_HEAD) + EPS)
            d_slab, dg = _head_norm_rope_bwd(dq_ref[h] * ATT_SCALE, q_slab, gq_ref[...], inv, cos_t, sin_t)
            dqp_ref[:, head] = d_slab.astype(BF16)
            dgq += jnp.sum(dg, 0, keepdims=True)
            k_slab = jnp.concatenate([kv_pre[:, h * QK_PAD:h * QK_PAD + QK_NOPE], kpe], axis=-1)
            inv = lax.rsqrt(jnp.sum(k_slab * k_slab, -1, keepdims=True) * (1.0 / QK_HEAD) + EPS)
            d_slab, dg = _head_norm_rope_bwd(dk_ref[h], k_slab, gk_ref[...], inv, cos_t, sin_t)
            dkvp_ref[:, head] = jnp.concatenate([d_slab[:, 0:QK_NOPE], dv_ref[h]], axis=-1).astype(BF16)
            dkpe += d_slab[:, QK_NOPE:QK_PAD]
            dgk += jnp.sum(dg, 0, keepdims=True)
        dqa = _dot_nt(dqp_ref[...], wq_ref[...])
        dx, dg = _rms_bwd(dqa, ql_t, qa_ref[...], inv_qa, Q_LORA)
        dql_ref[...] = dx.astype(BF16)
        _acc(dqa_ref, jnp.sum(dg, 0, keepdims=True))
        dca = _dot_nt(dkvp_ref[...], wkv_ref[...])
        dx, dg = _rms_bwd(dca, ckv, ka_ref[...], inv_ca, KV_LORA)
        dkvl_ref[:, 0:KV_LORA] = dx.astype(BF16)
        dkvl_ref[:, KV_LORA:KV_LAT_PAD] = dkpe.astype(BF16)
        _acc(dka_ref, jnp.sum(dg, 0, keepdims=True))
        _acc(dgq_ref, dgq)
        _acc(dgk_ref, dgk)

    row_outs = [((n, Q_LORA), BF16), ((n, KV_LAT_PAD), BF16), ((n, Q_LORA), BF16), ((n, N_HEADS * QK_PAD), BF16),
                ((n, KV_LORA), BF16), ((n, N_HEADS * (QK_NOPE + V_HEAD)), BF16)]
    acc_outs = [((1, Q_LORA), F32), ((1, KV_LORA), F32), ((1, QK_PAD), F32), ((1, QK_PAD), F32)]
    return _row_call(body, "qkv_prep_bwd", n, ROW_T, [ql, kvl, cos_t, sin_t, dq, dk, dv],
                     [q_a_norm, kv_a_norm, wq, wkv, gq, gk], row_outs, acc_outs, xch=xch)


def _glu_bwd(dy_ssm, y, w_glu, b_glu, w_o_ssm):
    n = y.shape[0]

    def body(dys_ref, y_ref, wg_ref, bg_ref, wo_ref, dy_ref, zg_ref, z_ref, dt_ref, db_ref):
        y_t = y_ref[...]
        z, th = _gelu(y_t)
        zb = z.astype(BF16)
        z_ref[...] = zb
        s = _sigmoid(_dot(zb, wg_ref[...]) + bg_ref[...])
        zg_ref[...] = (z * s).astype(BF16)
        dys = dys_ref[...]
        dzg = jnp.zeros_like(y_t)
        for j in range(N_DEV):
            dzg += _dot_nt(dys[:, j * OUT_SHARD:(j + 1) * OUT_SHARD], wo_ref[j])
        dt = dzg * z * s * (1.0 - s)
        dtb = dt.astype(BF16)
        dt_ref[...] = dtb
        dz = dzg * s + _dot_nt(dtb, wg_ref[...])
        dy_ref[...] = dz * _gelu_grad(y_t, th)
        _acc(db_ref, jnp.sum(dt, 0, keepdims=True))

    outs = [((n, SSM_WIDTH), F32)] + [((n, SSM_WIDTH), BF16)] * 3
    return _row_call(body, "glu_bwd", n, ROW_T, [dy_ssm, y], [w_glu, b_glu, w_o_ssm], outs, [((1, SSM_WIDTH), F32)])


def _ssm_bwd(u, dy, st, bblk, cblk, lam, d_row, xch):
    n = u.shape[0]
    t = min(SCAN_T, n)
    nc = n // t
    kb = 512
    perm = _perm_matrix(t)

    def body(u_ref, dy_ref, st_ref, p_ref, bblk_ref, cblk_ref, lam_ref, d_ref,
             du_ref, dlam_ref, dd_ref, db_ref, dct_ref,
             buf_x, buf_a, pw_ref, carry_ref, xcarry_ref, sx_ref, sa_ref, db_acc, dct_acc):
        @pl.when(pl.program_id(0) == 0)
        def _():
            carry_ref[...] = jnp.zeros_like(carry_ref)
            db_acc[...] = jnp.zeros_like(db_acc)
            dct_acc[...] = jnp.zeros_like(dct_acc)
            _power_table(lam_ref, pw_ref, t // SUBCHUNKS)

        u_t = u_ref[...]
        dy_t = dy_ref[...]
        p = p_ref[...]
        ub = _dot(p, u_t.astype(BF16)).astype(BF16)
        dyb = _dot(p, dy_t.astype(BF16)).astype(BF16)
        _to_states(ub, bblk_ref, buf_x, False)
        xcarry_ref[...] = st_ref[0]
        _run_scan(buf_x, lam_ref, t, False)
        _run_carries(buf_x, pw_ref, xcarry_ref, sx_ref, t, False)
        _run_fix(buf_x, pw_ref, sx_ref, t, False)
        _to_states(dyb, cblk_ref, buf_a, True)
        _run_scan(buf_a, lam_ref, t, True)
        _run_carries(buf_a, pw_ref, carry_ref, sa_ref, t, True)
        _run_fix(buf_a, pw_ref, sa_ref, t, True)
        du_ref[...] = (d_ref[...] * dy_t + _unpermute(p, _to_channels(buf_a, bblk_ref, True))).astype(BF16)
        for b in range(STATE_BLOCKS):
            lanes, ch = _state_block(b)
            db_acc[ch, lanes] += _dot_tn(ub[:, ch], buf_a[:, lanes].astype(BF16))
            dct_acc[ch, lanes] += _dot_tn(dyb[:, ch], buf_x[:, lanes].astype(BF16))
        for c in range(0, N_STATE, kb):
            re, im = pl.ds(c, kb), pl.ds(N_STATE + c, kb)
            xr, xi = buf_x[pl.ds(0, t - 8), re], buf_x[pl.ds(0, t - 8), im]
            ar, ai = buf_a[pl.ds(8, t - 8), re], buf_a[pl.ds(8, t - 8), im]
            x0r, x0i = sx_ref[:, re], sx_ref[:, im]
            a0r, a0i = buf_a[0:8, re], buf_a[0:8, im]
            dlam_part_re = (jnp.sum(ar * xr + ai * xi, 0, keepdims=True)
                            + jnp.sum(a0r * x0r + a0i * x0i, 0, keepdims=True))
            dlam_part_im = (jnp.sum(ai * xr - ar * xi, 0, keepdims=True)
                            + jnp.sum(a0i * x0r - a0r * x0i, 0, keepdims=True))

            @pl.when(pl.program_id(0) == 0)
            def _(c=c):
                dlam_ref[0:1, c:c + kb] = jnp.zeros((1, kb), F32)
                dlam_ref[1:2, c:c + kb] = jnp.zeros((1, kb), F32)

            dlam_ref[0:1, c:c + kb] += dlam_part_re
            dlam_ref[1:2, c:c + kb] += dlam_part_im
        _acc(dd_ref, jnp.sum(dy_t * u_t, 0, keepdims=True))

        @pl.when(pl.program_id(0) == nc - 1)
        def _():
            pltpu.sync_copy(db_acc, db_ref)
            pltpu.sync_copy(dct_acc, dct_ref)

    rev = lambda i: (nc - 1 - i, 0)
    consts = [perm, bblk, cblk, lam, d_row]
    wide = (SSM_WIDTH, 2 * N_STATE)
    return _call(
        body, "ssm_bwd", (nc,), [u, dy, st] + consts,
        [pl.BlockSpec((t, SSM_WIDTH), rev), pl.BlockSpec((t, SSM_WIDTH), rev),
         pl.BlockSpec((1, 8, 2 * N_STATE), lambda i: (nc - 1 - i, 0, 0))] + [_const(a) for a in consts],
        [_sds((n, SSM_WIDTH), BF16), _sds((2, N_STATE), F32), _sds((1, SSM_WIDTH), F32), _sds(wide, F32), _sds(wide, F32)],
        [pl.BlockSpec((t, SSM_WIDTH), rev), pl.BlockSpec((2, N_STATE), lambda i: (0, 0)),
         pl.BlockSpec((1, SSM_WIDTH), lambda i: (0, 0)), ANY, ANY],
        scratch=[pltpu.VMEM((t, 2 * N_STATE), F32)] * 2 + [pltpu.VMEM((t // SUBCHUNKS, 2 * N_STATE), F32)]
        + [pltpu.VMEM((8, 2 * N_STATE), F32)] * 4 + [pltpu.VMEM(wide, F32)] * 2,
        xch=xch)


def _in_proj_bwd(pieces, dh, x, norm_mix, w_in_pad):
    n = x.shape[0]

    def body(du_ref, dql_ref, dkvl_ref, dgs_ref, dgm_ref, dh_ref, x_ref, g_ref, w_ref, dx_ref, dp_ref, dg_ref):
        dxn = jnp.zeros((dh_ref.shape[0], D_MODEL), F32)
        for ref, (a, b) in zip((du_ref, dql_ref, dkvl_ref, dgs_ref, dgm_ref), IN_SEGS):
            piece = ref[...]
            dp_ref[:, a:b] = piece
            dxn += _dot_nt(piece, w_ref[:, a:b])
        x_t = x_ref[...]
        inv = lax.rsqrt(jnp.sum(x_t * x_t, -1, keepdims=True) * (1.0 / D_MODEL) + EPS)
        dx, dg = _rms_bwd(dxn, x_t, g_ref[...], inv, D_MODEL)
        dx_ref[...] = dh_ref[...] + dx
        _acc(dg_ref, jnp.sum(dg, 0, keepdims=True))

    outs = [((n, D_MODEL), F32), ((n, D_IN_PAD), BF16)]
    return _row_call(body, "in_proj_bwd", n, MM_T, list(pieces) + [dh, x], [norm_mix, w_in_pad], outs,
                     [((1, D_MODEL), F32)])


def _swap_minor(a):
    g, r, c = a.shape[1:]
    return jnp.transpose(a[0], (0, 2, 1)).reshape(g * c, r)


def _pad_in(w):
    return jnp.concatenate([w[:, :KV_END], jnp.zeros((w.shape[0], D_IN_PAD - D_IN), w.dtype), w[:, KV_END:]], axis=1)


def _unpad_in(w):
    return jnp.concatenate([w[:, :KV_END], w[:, KV_END + D_IN_PAD - D_IN:]], axis=1)


def _pad_gain(g):
    return jnp.pad(g, ((0, 0), (0, QK_PAD - QK_HEAD)))


def _place():
    x, y, c = lax.axis_index("x"), lax.axis_index("y"), lax.axis_index("c")
    chips = [(x, y), (1 - x, y), (x, 1 - y), (1 - x, 1 - y)]
    return x, y, c, chips


def _all_gather(block, name):
    rows, lanes = block.shape

    def body(x_ref, out_ref, send_sems, recv_sems, local_sem):
        x, y, c, chips = _place()
        me, sibling = (x, y, c), (x, y, 1 - c)

        def slot(px, py, pc):
            return out_ref.at[4 * px + 2 * py + pc]

        def copy(k, blk, to, src=None):
            return pltpu.make_async_remote_copy(
                src_ref=slot(*blk) if src is None else src, dst_ref=slot(*blk),
                send_sem=send_sems.at[k], recv_sem=recv_sems.at[k], device_id=to, device_id_type=MESH)

        mine = pltpu.make_async_copy(x_ref, slot(*me), local_sem)
        mine.start()
        first = [copy(0, me, sibling, src=x_ref)]
        first += [copy(1 + j, me, (*chip, c), src=x_ref) for j, chip in enumerate(chips[1:])]
        for cp in first:
            cp.start()
        passed = [copy(4 + j, (*chip, c), sibling) for j, chip in enumerate(chips[1:])]
        for j, chip in enumerate(chips[1:]):
            copy(1 + j, (*chip, c), me).wait_recv()
            passed[j].start()
        copy(0, sibling, me).wait_recv()
        for j, chip in enumerate(chips[1:]):
            copy(4 + j, (*chip, 1 - c), me).wait_recv()
        for cp in first + passed:
            cp.wait_send()
        mine.wait()

    return pl.pallas_call(
        body,
        name=name,
        in_specs=[ANY],
        out_specs=ANY,
        out_shape=_sds((N_DEV, rows, lanes), block.dtype),
        scratch_shapes=[pltpu.SemaphoreType.DMA((7,)), pltpu.SemaphoreType.DMA((7,)), pltpu.SemaphoreType.DMA],
    )(block)


def _reduce_scatter(parts, name):
    _, rows, lanes = parts.shape

    def body(p_ref, out_ref, own, land_a, send_b, land_b, sa, ra, sb, rb, lo):
        x, y, c, chips = _place()
        sibling = (x, y, 1 - c)

        def blk(chip, core):
            return p_ref.at[4 * chip[0] + 2 * chip[1] + core]

        to_sib = [pltpu.make_async_remote_copy(
            src_ref=blk(chips[k], 1 - c), dst_ref=land_a.at[k], send_sem=sa.at[k], recv_sem=ra.at[k],
            device_id=sibling, device_id_type=MESH) for k in range(4)]
        for cp in to_sib:
            cp.start()
        loads = [pltpu.make_async_copy(blk(chips[k], c), own.at[k], lo.at[k]) for k in range(4)]
        for cp in loads:
            cp.start()
        to_chip = [pltpu.make_async_remote_copy(
            src_ref=send_b.at[j], dst_ref=land_b.at[j], send_sem=sb.at[j], recv_sem=rb.at[j],
            device_id=(*chips[1 + j], c), device_id_type=MESH) for j in range(3)]
        for k in (1, 2, 3):
            to_sib[k].wait_recv()
            loads[k].wait()
            send_b[k - 1] = (own[k] + land_a[k]).astype(BF16)
            to_chip[k - 1].start()
        to_sib[0].wait_recv()
        loads[0].wait()
        acc = own[0] + land_a[0]
        for j in range(3):
            to_chip[j].wait_recv()
            acc = acc + land_b[j].astype(F32)
        out_ref[...] = acc
        for cp in to_sib + to_chip:
            cp.wait_send()

    return pl.pallas_call(
        body,
        name=name,
        in_specs=[ANY],
        out_specs=pl.BlockSpec(memory_space=pltpu.VMEM),
        out_shape=_sds((rows, lanes), F32),
        scratch_shapes=[pltpu.VMEM((4, rows, lanes), F32), pltpu.VMEM((4, rows, lanes), F32),
                        pltpu.VMEM((3, rows, lanes), BF16), pltpu.VMEM((3, rows, lanes), BF16)]
        + [pltpu.SemaphoreType.DMA((4,))] * 2 + [pltpu.SemaphoreType.DMA((3,))] * 2 + [pltpu.SemaphoreType.DMA((4,))],
        compiler_params=_params(),
    )(parts)


def _adamw_math(w, g, m, v):
    m = ADAM_B1 * m + (1.0 - ADAM_B1) * g
    v = ADAM_B2 * v + (1.0 - ADAM_B2) * (g * g)
    m_hat = m / (1.0 - ADAM_B1 ** ADAM_STEP)
    v_hat = v / (1.0 - ADAM_B2 ** ADAM_STEP)
    delta = -ADAM_LR * (m_hat / (jnp.sqrt(v_hat) + ADAM_EPS) + ADAM_WD * w)
    return delta, m, v


def _row_tile(r):
    return max(t for t in range(8, min(r, 256) + 1, 8) if r % t == 0)


def _adamw(w, g, m, v, name):
    r, n = w.shape

    def body(w_ref, g_ref, m_ref, v_ref, d_ref, nm_ref, nv_ref):
        d_ref[...], nm_ref[...], nv_ref[...] = _adamw_math(w_ref[...], g_ref[...], m_ref[...], v_ref[...])

    return _row_call(body, name, r, _row_tile(r), [w, g, m, v], [], [((r, n), F32)] * 3)


def _adamw_sum(landed, w, m, v, name):
    r, n = w.shape

    def body(l_ref, w_ref, m_ref, v_ref, g_ref, d_ref, nm_ref, nv_ref):
        g = l_ref[0].astype(F32)
        for dev in range(1, N_DEV):
            g = g + l_ref[dev].astype(F32)
        g_ref[...] = g
        d_ref[...], nm_ref[...], nv_ref[...] = _adamw_math(w_ref[...], g, m_ref[...], v_ref[...])

    tm = max(t for t in range(16, min(r, 256) + 1, 16) if r % t == 0)
    return _row_call(body, name, r, tm, [landed, w, m, v], [], [((r, n), F32)] * 4)


def _adamw_small(gathered, w, m, v, row_counts):
    n_rows = w.shape[0]

    def body(ga_ref, w_ref, m_ref, v_ref, loss_ref, *out_refs):
        g = ga_ref[0]
        for dev in range(1, N_DEV):
            g = g + ga_ref[dev]
        loss_ref[...] = g[n_rows:n_rows + 8]
        g = g[0:n_rows]
        d, nm, nv = _adamw_math(w_ref[...], g, m_ref[...], v_ref[...])
        off = 0
        for p, rows in enumerate(row_counts):
            for k, val in enumerate((g, d, nm, nv)):
                out_refs[4 * p + k][...] = val[off:off + rows]
            off += rows

    outs = [_sds((8, LANES), F32)] + [_sds((rows, LANES), F32) for rows in row_counts for _ in range(4)]
    return pl.pallas_call(body, name="adamw_small", out_shape=outs, compiler_params=_params())(gathered, w, m, v)


SMALL = ("norm_mix", "q_a_norm", "kv_a_norm", "q_norm", "k_norm", "ssm_a_re", "ssm_a_im", "ssm_log_dt", "ssm_b_re",
         "ssm_b_im", "ssm_c_re", "ssm_c_im", "ssm_d", "b_glu", "norm_mlp")
WEIGHT_ORDER = ("norm_mix", "w_in", "q_a_norm", "kv_a_norm", "w_q_b", "w_kv_b", "q_norm", "k_norm", "w_o_mla",
                "ssm_a_re", "ssm_a_im", "ssm_log_dt", "ssm_b_re", "ssm_b_im", "ssm_c_re", "ssm_c_im", "ssm_d", "w_glu",
                "b_glu", "w_o_ssm", "w_out", "norm_mlp", "w_up", "w_down")
IN_SHARD = D_IN // N_DEV
Q_SHARD = QK_HEAD


def _pack_small(vals):
    parts = []
    for n in SMALL:
        flat = vals[n].reshape(-1)
        size = -(-flat.shape[0] // (8 * LANES)) * 8 * LANES
        parts.append(jnp.pad(flat, (0, size - flat.shape[0])).reshape(-1, LANES))
    return jnp.concatenate(parts, axis=0)


def _small_rows(like):
    return [-(-like[n].size // (8 * LANES)) * 8 for n in SMALL]


def _step(x, pos_col, target, w, small):
    bf = {n: a.astype(BF16) for n, a in w.items()}
    gq, gk = _pad_gain(small["q_norm"]), _pad_gain(small["k_norm"])
    a_re = small["ssm_a_re"].reshape(1, N_STATE)
    a_im = small["ssm_a_im"].reshape(1, N_STATE)
    log_dt = jnp.repeat(small["ssm_log_dt"].reshape(SSM_GROUPS), SSM_STATE).reshape(1, N_STATE)
    bt_re, bt_im = _swap_minor(small["ssm_b_re"]), _swap_minor(small["ssm_b_im"])
    c2_re, c2_im = _swap_minor(small["ssm_c_re"]), _swap_minor(small["ssm_c_im"])
    d_row = small["ssm_d"].reshape(1, SSM_WIDTH)

    w_in_all = _all_gather(bf["w_in"], "gather_w_in")
    w_in_pad = _pad_in(jnp.transpose(w_in_all, (1, 0, 2)).reshape(D_MODEL, D_IN))
    cos_t, sin_t = _rope_tables(pos_col)
    lam, bblk, cblk = _ssm_prep(a_re, a_im, log_dt, bt_re, bt_im, c2_re, c2_im)
    wq_mine = jnp.pad(bf["w_q_b"], ((0, 0), (0, QK_PAD - QK_HEAD)))
    u, ql, kvl, gs, gm, xn_t, w_glu, w_o_ssm = _in_proj(
        x, small["norm_mix"], w_in_pad, xch=[(bf["w_glu"], False), (bf["w_o_ssm"], False)])
    w_glu = w_glu.reshape(SSM_WIDTH, SSM_WIDTH)
    y, y_ssm, st, wq, wkv, w_o_mla, w_out = _ssm_fwd(
        u, bblk, cblk, lam, d_row, w_glu, small["b_glu"], w_o_ssm,
        xch=[(wq_mine, False), (bf["w_kv_b"], False), (bf["w_o_mla"], False), (bf["w_out"], False)])
    w_o_mla, w_out = w_o_mla.reshape(D_MODEL, D_MODEL), w_out.reshape(D_MODEL, D_MODEL)
    wq = jnp.transpose(wq, (1, 0, 2)).reshape(Q_LORA, N_HEADS * QK_PAD)
    wkv = jnp.transpose(wkv, (1, 0, 2)).reshape(KV_LORA, N_HEADS * QK_PAD)
    q, k, v, kt, vt = _qkv_prep(ql, kvl, small["q_a_norm"], small["kv_a_norm"], wq, wkv, gq, gk, cos_t, sin_t)
    attn, lse, attn_t, w_up, w_down = _attn_fwd(q, k, vt, xch=[(bf["w_up"], False), (bf["w_down"], False)])
    h, y_mla, mixed_t = _merge(attn, gs, gm, y_ssm, x, w_o_mla, w_out)
    hn, dout, hn_t, loss = _mlp_fwd_loss(h, target, small["norm_mlp"], w_up, w_down)

    da, dh, dout_b, hid_t, d_norm_mlp = _mlp_bwd(dout, hn, h, small["norm_mlp"], w_up, w_down)
    p_w_down = _matmul_tn_shards(hid_t, dout_b, "dw_down", False, tm=1024, turned=True)
    p_w_up = _matmul_tn_shards(hn_t, da, "dw_up", True, turned=True)
    dgs, dgm, dy_ssm, dy_mla, dattn = _merge_bwd(dh, gs, gm, y_ssm, y_mla, w_out, w_o_mla)
    p_w_out = _matmul_tn_shards(mixed_t, dh, "dw_out", False, tm=1024, turned=True)
    p_w_o_mla = _matmul_tn_shards(attn_t, dy_mla, "dw_o_mla", False, turned=True)
    dq, dk, dv, l_w_up, l_w_down, l_w_out, l_w_o_mla = _attn_bwd(
        q, k, kt, v, attn, lse, dattn, xch=[(p_w_up, True), (p_w_down, True), (p_w_out, True), (p_w_o_mla, True)])
    dql, dkvl, qa, dq_pre, ca, dkv_pre, d_q_a_norm, d_kv_a_norm, d_gq, d_gk = _qkv_prep_bwd(
        ql, kvl, dq, dk, dv, small["q_a_norm"], small["kv_a_norm"], wq, wkv, gq, gk, cos_t, sin_t, xch=[])
    p_wq = _matmul_tn_shards(qa, dq_pre, "dw_q_b", True)
    p_wkv = _matmul_tn_shards(ca, dkv_pre, "dw_kv_b", True)
    dy, zg, z, dt, d_b_glu = _glu_bwd(dy_ssm, y, w_glu, small["b_glu"], w_o_ssm)
    p_w_o_ssm = _matmul_tn_shards(zg, dy_ssm, "dw_o_ssm", True)
    p_w_glu = _matmul_tn_shards(z, dt, "dw_glu", False)
    du, dlam, d_d, d_bblk, d_cblk_t, l_wq, l_wkv, l_w_glu, l_w_o_ssm = _ssm_bwd(
        u, dy, st, bblk, cblk, lam, d_row, xch=[(p_wq, True), (p_wkv, True), (p_w_glu, True), (p_w_o_ssm, True)])
    d_a_re, d_a_im, d_log_dt, d_bt_re, d_bt_im, d_c_re, d_c_im = _ssm_prep_bwd(
        a_re, a_im, log_dt, bt_re, bt_im, dlam, d_bblk, d_cblk_t)
    dx, dproj, d_norm_mix = _in_proj_bwd((du, dql, dkvl, dgs, dgm), dh, x, small["norm_mix"], w_in_pad)
    tr = lambda mat: jnp.transpose(mat.reshape(SSM_GROUPS, SSM_GROUP_CH, SSM_STATE), (0, 2, 1))
    g_small = {
        "norm_mix": d_norm_mix, "q_a_norm": d_q_a_norm, "kv_a_norm": d_kv_a_norm,
        "q_norm": d_gq[:, :QK_HEAD], "k_norm": d_gk[:, :QK_HEAD],
        "ssm_a_re": d_a_re, "ssm_a_im": d_a_im, "ssm_log_dt": d_log_dt,
        "ssm_b_re": tr(d_bt_re), "ssm_b_im": tr(d_bt_im), "ssm_c_re": d_c_re, "ssm_c_im": d_c_im,
        "ssm_d": d_d, "b_glu": d_b_glu, "norm_mlp": d_norm_mlp,
    }
    g_w_in_pad, g_small_all = _matmul_tn(
        xn_t, dproj, "dw_in", xch=[(jnp.concatenate([_pack_small(g_small), loss], axis=0), False)], turned=True)
    parts = jnp.transpose(_unpad_in(g_w_in_pad).reshape(D_MODEL, N_DEV, IN_SHARD), (1, 0, 2))
    g_w_in_mine = _reduce_scatter(parts, "reduce_w_in")
    landed = {"w_q_b": l_wq[:, :, :QK_HEAD], "w_kv_b": l_wkv, "w_o_mla": l_w_o_mla, "w_glu": l_w_glu,
              "w_o_ssm": l_w_o_ssm, "w_out": l_w_out, "w_up": l_w_up, "w_down": l_w_down}
    return dx, landed, g_w_in_mine, g_small_all


def kernel(x, positions, norm_mix, w_in, q_a_norm, kv_a_norm, w_q_b, w_kv_b, q_norm, k_norm, w_o_mla, ssm_a_re, ssm_a_im, ssm_log_dt, ssm_b_re, ssm_b_im, ssm_c_re, ssm_c_im, ssm_d, w_glu, b_glu, w_o_ssm, w_out, norm_mlp, w_up, w_down, loss_target, m_norm_mix, m_w_in, m_q_a_norm, m_kv_a_norm, m_w_q_b, m_w_kv_b, m_q_norm, m_k_norm, m_w_o_mla, m_ssm_a_re, m_ssm_a_im, m_ssm_log_dt, m_ssm_b_re, m_ssm_b_im, m_ssm_c_re, m_ssm_c_im, m_ssm_d, m_w_glu, m_b_glu, m_w_o_ssm, m_w_out, m_norm_mlp, m_w_up, m_w_down, v_norm_mix, v_w_in, v_q_a_norm, v_kv_a_norm, v_w_q_b, v_w_kv_b, v_q_norm, v_k_norm, v_w_o_mla, v_ssm_a_re, v_ssm_a_im, v_ssm_log_dt, v_ssm_b_re, v_ssm_b_im, v_ssm_c_re, v_ssm_c_im, v_ssm_d, v_w_glu, v_b_glu, v_w_o_ssm, v_w_out, v_norm_mlp, v_w_up, v_w_down):
    given = dict(locals())
    w = {n: given[n] for n in WEIGHT_ORDER}
    m = {n: given["m_" + n] for n in WEIGHT_ORDER}
    v = {n: given["v_" + n] for n in WEIGHT_ORDER}
    big = [n for n in WEIGHT_ORDER if n not in SMALL]
    small = {n: w[n] for n in SMALL}

    dx, landed, g_w_in, g_small_all = _step(
        x[0], positions.reshape(-1, 1), loss_target[0], {n: w[n][0] for n in big}, small)

    grads, deltas, new_m, new_v = {}, {}, {}, {}
    for n in big:
        if n == "w_in":
            g = g_w_in
            d, nm, nv = _adamw(w[n][0], g, m[n][0], v[n][0], "adamw_" + n)
        else:
            g, d, nm, nv = _adamw_sum(landed[n], w[n][0], m[n][0], v[n][0], "adamw_" + n)
        grads[n], deltas[n], new_m[n], new_v[n] = g[None], d[None], nm[None], nv[None]

    outs = _adamw_small(g_small_all, _pack_small(small), _pack_small({n: m[n] for n in SMALL}),
                        _pack_small({n: v[n] for n in SMALL}), _small_rows(small))
    for p, n in enumerate(SMALL):
        for k, dst in enumerate((grads, deltas, new_m, new_v)):
            dst[n] = outs[1 + 4 * p + k].reshape(-1)[:small[n].size].reshape(small[n].shape)

    return (outs[0][0, 0], dx[None], *[grads[n] for n in WEIGHT_ORDER], *[deltas[n] for n in WEIGHT_ORDER],
            *[new_m[n] for n in WEIGHT_ORDER], *[new_v[n] for n in WEIGHT_ORDER])
```

```python
import functools
import math

import numpy as np
import jax
import jax.numpy as jnp
from jax import lax
from jax.experimental import pallas as pl
from jax.experimental.pallas import tpu as pltpu

F32 = jnp.float32
BF16 = jnp.bfloat16

D_MODEL = 1024
SSM_GROUPS = 32
SSM_GROUP_CH = 16
SSM_WIDTH = 512
SSM_STATE = 64
N_STATE = SSM_GROUPS * SSM_STATE
N_HEADS = 8
QK_NOPE = 128
QK_ROPE = 64
QK_HEAD = 192
QK_PAD = 256
V_HEAD = 128
Q_LORA = 384
KV_LORA = 256
KV_LAT_PAD = 384
ROPE_THETA = 10000.0
D_FF = 4096
EPS = 1e-6
ATT_SCALE = QK_HEAD ** -0.5
N_DEV = 8
FF_SHARD = D_FF // N_DEV
OUT_SHARD = D_MODEL // N_DEV

IN_SEGS = ((0, 512), (512, 896), (896, 1280), (1280, 2304), (2304, 3328))
D_IN = 3264
D_IN_PAD = 3328
KV_END = 1216

ADAM_LR = 0.001
ADAM_B1 = 0.9
ADAM_B2 = 0.999
ADAM_EPS = 1e-08
ADAM_WD = 0.01
ADAM_STEP = 10

VMEM_LIMIT = 56 * 1024 * 1024
MESH = pl.DeviceIdType.MESH
ANY = pl.BlockSpec(memory_space=pl.ANY)
LANES = 128

SCAN_T = 256
SUBCHUNKS = 8
SCAN_CG = 512
ATT_T = 512
ATT_SUB = 2
ROW_T = 256
MM_T = 512


def _params(sem=None):
    return pltpu.CompilerParams(dimension_semantics=sem, vmem_limit_bytes=VMEM_LIMIT)


def _rows(arr, tm):
    if arr.ndim == 2:
        return pl.BlockSpec((tm, arr.shape[1]), lambda i: (i, 0))
    return pl.BlockSpec((arr.shape[0], tm, arr.shape[2]), lambda i: (0, i, 0))


def _const(arr):
    nd = arr.ndim
    return pl.BlockSpec(arr.shape, lambda *_: (0,) * nd, pipeline_mode=pl.Buffered(1))


def _sds(shape, dtype):
    return jax.ShapeDtypeStruct(shape, dtype)


PEERS = tuple((dx, dy, dc) for dx in (0, 1) for dy in (0, 1) for dc in (0, 1) if (dx, dy, dc) != (0, 0, 0))


def _here():
    x, y, c = lax.axis_index("x"), lax.axis_index("y"), lax.axis_index("c")
    return x, y, c, 4 * x + 2 * y + c


def _xchg_start(scatter, srcs, dsts, send, recv, local):
    x, y, c, me = _here()
    for e, sc in enumerate(scatter):
        src, dst = srcs[e], dsts[e]
        pltpu.make_async_copy(src.at[me] if sc else src, dst.at[me], local.at[e]).start()
        for dx, dy, dc in PEERS:
            px, py, pc = (1 - x if dx else x), (1 - y if dy else y), (1 - c if dc else c)
            pltpu.make_async_remote_copy(
                src_ref=src.at[4 * px + 2 * py + pc] if sc else src, dst_ref=dst.at[me],
                send_sem=send.at[e], recv_sem=recv.at[e], device_id=(px, py, pc), device_id_type=MESH).start()


def _xchg_wait(scatter, srcs, dsts, send, recv, local):
    x, y, c, me = _here()
    for e, sc in enumerate(scatter):
        src, dst = srcs[e], dsts[e]
        pltpu.make_async_copy(src.at[me] if sc else src, dst.at[me], local.at[e]).wait()
        span = dst.at[pl.ds(0, N_DEV - 1)]
        both = pltpu.make_async_remote_copy(src_ref=span, dst_ref=span, send_sem=send.at[e], recv_sem=recv.at[e],
                                            device_id=(x, y, c), device_id_type=MESH)
        both.wait_send()
        both.wait_recv()


def _call(body, name, grid, ins, in_specs, outs, out_specs, scratch=(), xch=()):
    n_in, n_out, ne = len(ins), len(outs), len(xch)
    scatter = [sc for _, sc in xch]
    x_outs = [_sds((N_DEV,) + (a.shape[1:] if sc else a.shape), a.dtype) for a, sc in xch]
    sems = [pltpu.SemaphoreType.DMA((ne,))] * 3 if ne else []

    def wrapped(*refs):
        in_refs, x_src = refs[:n_in], refs[n_in:n_in + ne]
        out_refs = refs[n_in + ne:n_in + ne + n_out]
        x_dst = refs[n_in + ne + n_out:n_in + 2 * ne + n_out]
        rest = refs[n_in + 2 * ne + n_out:]
        if ne:
            x_sems, rest = rest[len(rest) - 3:], rest[:len(rest) - 3]
            first = functools.reduce(jnp.logical_and, [pl.program_id(d) == 0 for d in range(len(grid))])
            last = functools.reduce(jnp.logical_and, [pl.program_id(d) == grid[d] - 1 for d in range(len(grid))])

            @pl.when(first)
            def _():
                _xchg_start(scatter, x_src, x_dst, *x_sems)

        body(*in_refs, *out_refs, *rest)
        if ne:
            @pl.when(last)
            def _():
                _xchg_wait(scatter, x_src, x_dst, *x_sems)

    return pl.pallas_call(
        wrapped,
        name=name,
        grid=grid,
        in_specs=list(in_specs) + [ANY] * ne,
        out_specs=list(out_specs) + [ANY] * ne,
        out_shape=list(outs) + x_outs,
        scratch_shapes=list(scratch) + sems,
        compiler_params=_params(("arbitrary",) * len(grid)),
    )(*ins, *[a for a, _ in xch])


def _row_call(body, name, n_rows, tm, row_ins, const_ins, row_outs, acc_outs=(), xch=(), col_outs=(), scratch=()):
    outs = [_sds(s, d) for s, d in list(row_outs) + list(col_outs) + list(acc_outs)]
    n_row, n_col = len(row_outs), len(col_outs)
    out_specs = [_rows(o, tm) for o in outs[:n_row]] + [
        pl.BlockSpec((o.shape[0], tm), lambda i: (0, i)) for o in outs[n_row:n_row + n_col]] + [
        pl.BlockSpec(o.shape, lambda i, nd=len(o.shape): (0,) * nd) for o in outs[n_row + n_col:]]
    in_specs = [_rows(a, tm) for a in row_ins] + [_const(a) for a in const_ins]
    return _call(body, name, (n_rows // tm,), list(row_ins) + list(const_ins), in_specs, outs, out_specs,
                 scratch=scratch, xch=xch)


def _dot(a, b):
    return jnp.dot(a, b, preferred_element_type=F32)


def _dot_nt(a, b):
    return lax.dot_general(a, b, (((1,), (1,)), ((), ())), preferred_element_type=F32)


def _dot_tn(a, b):
    return lax.dot_general(a, b, (((0,), (0,)), ((), ())), preferred_element_type=F32)


def _rms(x, g, n):
    inv = lax.rsqrt(jnp.sum(x * x, -1, keepdims=True) * (1.0 / n) + EPS)
    return x * inv * g, inv


def _rms_bwd(dy, x, g, inv, n):
    xh = x * inv
    dxh = dy * g
    dx = inv * (dxh - xh * (jnp.sum(dxh * xh, -1, keepdims=True) * (1.0 / n)))
    return dx, dy * xh


def _sigmoid(x):
    return 1.0 / (1.0 + jnp.exp(-x))


_GELU_C = math.sqrt(2.0 / math.pi)


def _gelu(y):
    th = jnp.tanh(_GELU_C * (y + 0.044715 * (y * y * y)))
    return 0.5 * y * (1.0 + th), th


def _gelu_grad(y, th):
    return 0.5 * (1.0 + th) + 0.5 * y * (1.0 - th * th) * (_GELU_C * (1.0 + 3.0 * 0.044715 * (y * y)))


def _acc(ref, val):
    @pl.when(pl.program_id(0) == 0)
    def _():
        ref[...] = jnp.zeros_like(ref)

    ref[...] += val


def _tile(n, limit):
    if n <= limit:
        return n
    return max(t for t in range(128, limit + 1, 128) if n % t == 0)


def _lhs(a, turned, tm, tk):
    m, k_dim = a.shape if turned else a.shape[::-1]
    tm, tk = _tile(m, tm), _tile(k_dim, tk)
    if turned:
        return m, k_dim, tm, tk, pl.BlockSpec((tm, tk), lambda i, k: (i, k)), _dot
    return m, k_dim, tm, tk, pl.BlockSpec((tk, tm), lambda i, k: (k, i)), _dot_tn


def _matmul_tn_shards(a, b, name, by_col, tm=512, tk=512, turned=False):
    m, k_dim, tm, tk, a_spec, dot = _lhs(a, turned, tm, tk)
    n = b.shape[1]
    nk = k_dim // tk
    if by_col:
        r, c = m, n // N_DEV
        out_spec = pl.BlockSpec((N_DEV, tm, c), lambda i, k: (0, i, 0))
    else:
        r, c = m // N_DEV, n
        per = tm // r
        out_spec = pl.BlockSpec((per, r, c), lambda i, k: (i, 0, 0))

    def body(a_ref, b_ref, o_ref, acc_ref):
        k = pl.program_id(1)

        @pl.when(k == 0)
        def _():
            acc_ref[...] = jnp.zeros_like(acc_ref)

        acc_ref[...] += dot(a_ref[...].astype(BF16), b_ref[...].astype(BF16))

        @pl.when(k == nk - 1)
        def _():
            if by_col:
                for j in range(N_DEV):
                    o_ref[j] = acc_ref[:, j * c:(j + 1) * c].astype(BF16)
            else:
                for s in range(per):
                    o_ref[s] = acc_ref[s * r:(s + 1) * r, :].astype(BF16)

    return pl.pallas_call(
        body,
        name=name,
        grid=(m // tm, nk),
        in_specs=[a_spec, pl.BlockSpec((tk, n), lambda i, k: (k, 0))],
        out_specs=out_spec,
        out_shape=_sds((N_DEV, r, c), BF16),
        scratch_shapes=[pltpu.VMEM((tm, n), F32)],
        compiler_params=_params(("parallel", "arbitrary")),
    )(a, b)


def _rope_tables(pos_col):
    n = pos_col.shape[0]
    half = QK_ROPE // 2
    inv_freq = (ROPE_THETA ** (-np.arange(half, dtype=np.float32) / half)).astype(np.float32)
    freq_row = jnp.asarray(np.concatenate([inv_freq, inv_freq, np.zeros(64, np.float32)])[None, :])

    def body(p_ref, f_ref, c_ref, s_ref):
        ang = p_ref[...].astype(F32) * f_ref[...]
        c_ref[...] = jnp.cos(ang)
        s_ref[...] = jnp.sin(ang)

    return _row_call(body, "rope_tables", n, min(n, 1024), [pos_col], [freq_row], [((n, 128), F32)] * 2)


def _rope_rot(v):
    lane = lax.broadcasted_iota(jnp.int32, v.shape, 1)
    return jnp.where(lane < 32, -pltpu.roll(v, 96, 1), jnp.where(lane < 64, pltpu.roll(v, 32, 1), 0.0))


def _rope_rot_t(v):
    lane = lax.broadcasted_iota(jnp.int32, v.shape, 1)
    return jnp.where(lane < 32, pltpu.roll(v, 96, 1), jnp.where(lane < 64, -pltpu.roll(v, 32, 1), 0.0))


def _in_proj(x, norm_mix, w_in_pad, xch):
    n = x.shape[0]

    def body(x_ref, g_ref, w_ref, u_ref, ql_ref, kvl_ref, gs_ref, gm_ref, xnt_ref):
        xn, _ = _rms(x_ref[...], g_ref[...], D_MODEL)
        xb = xn.astype(BF16)
        xnt_ref[...] = xn.T.astype(BF16)
        for ref, (a, b) in zip((u_ref, ql_ref, kvl_ref, gs_ref, gm_ref), IN_SEGS):
            ref[...] = _dot(xb, w_ref[:, a:b])

    outs = [((n, b - a), F32) for a, b in IN_SEGS]
    return _row_call(body, "in_proj", n, MM_T, [x], [norm_mix, w_in_pad], outs, xch=xch,
                     col_outs=[((D_MODEL, n), BF16)])


def _ssm_prep_fn(a_re, a_im, log_dt, b_re_x, b_im_x):
    dt = jnp.exp(log_dt)
    mag = jnp.exp(a_re * dt)
    lr = mag * jnp.cos(a_im * dt)
    li = mag * jnp.sin(a_im * dt)
    den = a_re * a_re + a_im * a_im
    fr = ((lr - 1.0) * a_re + li * a_im) / den
    fi = (li * a_re - (lr - 1.0) * a_im) / den
    return lr, li, fr * b_re_x - fi * b_im_x, fr * b_im_x + fi * b_re_x


def _dot_exact(a, b, dims):
    return lax.dot_general(a, b, (dims, ((), ())), precision=lax.Precision.HIGHEST, preferred_element_type=F32)


def _lane_repeat(width, n):
    src = lax.broadcasted_iota(jnp.int32, (width, n), 0)
    dst = lax.broadcasted_iota(jnp.int32, (width, n), 1)
    return (dst % width == src).astype(F32)


def _same_group(rows, rows_per_group, cols, cols_per_group):
    row = lax.broadcasted_iota(jnp.int32, (rows, cols), 0)
    col = lax.broadcasted_iota(jnp.int32, (rows, cols), 1)
    return (row // rows_per_group) == (col // cols_per_group)


def _expand_b(bt):
    tiled = _dot_exact(bt, _lane_repeat(SSM_STATE, N_STATE), ((1,), (0,)))
    return jnp.where(_same_group(SSM_WIDTH, SSM_GROUP_CH, N_STATE, SSM_STATE), tiled, 0.0)


def _collect_b(m):
    masked = jnp.where(_same_group(SSM_WIDTH, SSM_GROUP_CH, N_STATE, SSM_STATE), m, 0.0)
    return _dot_exact(masked, _lane_repeat(SSM_STATE, N_STATE), ((1,), (1,)))


def _ssm_prep(a_re, a_im, log_dt, bt_re, bt_im, c2_re, c2_im):
    def body(ar, ai, ld, br, bi, cr, ci, lam_ref, bblk_ref, cblk_ref):
        lr, li, bbr, bbi = _ssm_prep_fn(ar[...], ai[...], ld[...], _expand_b(br[...]), _expand_b(bi[...]))
        lam_ref[0:1, :] = lr
        lam_ref[1:2, :] = li
        bblk_ref[:, 0:N_STATE] = bbr.astype(BF16)
        bblk_ref[:, N_STATE:] = bbi.astype(BF16)
        rep = _lane_repeat(SSM_GROUP_CH, SSM_WIDTH)
        own = _same_group(N_STATE, SSM_STATE, SSM_WIDTH, SSM_GROUP_CH)
        cblk_ref[0:N_STATE, :] = jnp.where(own, _dot_exact(cr[...], rep, ((1,), (0,))), 0.0).astype(BF16)
        cblk_ref[N_STATE:, :] = jnp.where(own, -_dot_exact(ci[...], rep, ((1,), (0,))), 0.0).astype(BF16)

    return pl.pallas_call(
        body,
        name="ssm_prep",
        out_shape=[_sds((2, N_STATE), F32), _sds((SSM_WIDTH, 2 * N_STATE), BF16),
                   _sds((2 * N_STATE, SSM_WIDTH), BF16)],
        compiler_params=_params(),
    )(a_re, a_im, log_dt, bt_re, bt_im, c2_re, c2_im)


def _ssm_prep_bwd(a_re, a_im, log_dt, bt_re, bt_im, dlam, dbblk, dcblk_t):
    def body(ar, ai, ld, br, bi, dl, db, dc, dar, dai, dld, dbr, dbi, dcr, dci):
        _, vjp = jax.vjp(_ssm_prep_fn, ar[...], ai[...], ld[...], _expand_b(br[...]), _expand_b(bi[...]))
        g = vjp((dl[0:1, :], dl[1:2, :], db[:, 0:N_STATE], db[:, N_STATE:]))
        dar[...] = g[0]
        dai[...] = g[1]
        grp = lax.broadcasted_iota(jnp.int32, (SSM_GROUPS, N_STATE), 0)
        lane = lax.broadcasted_iota(jnp.int32, (SSM_GROUPS, N_STATE), 1)
        sel = (lane // SSM_STATE) == grp
        dld[...] = jnp.sum(jnp.where(sel, jnp.broadcast_to(g[2], (SSM_GROUPS, N_STATE)), 0.0), axis=1, keepdims=True)
        dbr[...] = _collect_b(g[3])
        dbi[...] = _collect_b(g[4])
        dcr[...] = _collect_b(dc[:, 0:N_STATE])
        dci[...] = -_collect_b(dc[:, N_STATE:])

    small = _sds((SSM_WIDTH, SSM_STATE), F32)
    return pl.pallas_call(
        body,
        name="ssm_prep_bwd",
        out_shape=[_sds((1, N_STATE), F32), _sds((1, N_STATE), F32), _sds((SSM_GROUPS, 1), F32), small, small, small, small],
        compiler_params=_params(),
    )(a_re, a_im, log_dt, bt_re, bt_im, dlam, dbblk, dcblk_t)


def _perm_matrix(t):
    run = t // SUBCHUNKS
    p = np.zeros((t, t), np.float32)
    r = np.arange(t)
    p[r, (r % SUBCHUNKS) * run + r // SUBCHUNKS] = 1.0
    return jnp.asarray(p, dtype=BF16)


def _unpermute(p, a):
    hi = a.astype(BF16)
    r1 = a - hi.astype(F32)
    mid = r1.astype(BF16)
    lo = (r1 - mid.astype(F32)).astype(BF16)
    return _dot_tn(p, hi) + _dot_tn(p, mid) + _dot_tn(p, lo)


def _power_table(lam_ref, pw_ref, n):
    lr, li = lam_ref[0:1, :], lam_ref[1:2, :]
    pw_ref[0:1, 0:N_STATE] = lr
    pw_ref[0:1, N_STATE:] = li

    def step(i, carry):
        pr, pi = carry
        pr, pi = pr * lr - pi * li, pr * li + pi * lr
        pw_ref[pl.ds(i, 1), 0:N_STATE] = pr
        pw_ref[pl.ds(i, 1), N_STATE:] = pi
        return pr, pi

    lax.fori_loop(1, n, step, (lr, li))


def _col_groups():
    return [(pl.ds(c, SCAN_CG), pl.ds(N_STATE + c, SCAN_CG)) for c in range(0, N_STATE, SCAN_CG)]


def _run_scan(buf, lam_ref, t, reverse):
    nblk = t // 8
    for re, im in _col_groups():
        lr = jnp.broadcast_to(lam_ref[0:1, re], (8, SCAN_CG))
        li = jnp.broadcast_to(lam_ref[1:2, re], (8, SCAN_CG))
        if reverse:
            li = -li
        first = pl.ds((nblk - 1) * 8 if reverse else 0, 8)

        def step(k, carry, re=re, im=im, lr=lr, li=li):
            pr, pi = carry
            i = (nblk - 2 - k) if reverse else (k + 1)
            r = pl.ds(pl.multiple_of(i * 8, 8), 8)
            xr = buf[r, re] + lr * pr - li * pi
            xi = buf[r, im] + lr * pi + li * pr
            buf[r, re] = xr
            buf[r, im] = xi
            return xr, xi

        lax.fori_loop(0, nblk - 1, step, (buf[first, re], buf[first, im]))


def _run_carries(buf, pw_ref, carry_ref, s_ref, t, reverse):
    nblk = t // 8
    run = t // SUBCHUNKS
    edge = buf[pl.ds(0 if reverse else (nblk - 1) * 8, 8), :]
    pr, pi = pw_ref[run - 1:run, 0:N_STATE], pw_ref[run - 1:run, N_STATE:]
    if reverse:
        pi = -pi
    sr, si = carry_ref[0:1, 0:N_STATE], carry_ref[0:1, N_STATE:]
    for s in (range(SUBCHUNKS - 1, -1, -1) if reverse else range(SUBCHUNKS)):
        s_ref[s:s + 1, 0:N_STATE] = sr
        s_ref[s:s + 1, N_STATE:] = si
        er, ei = edge[s:s + 1, 0:N_STATE], edge[s:s + 1, N_STATE:]
        sr, si = er + pr * sr - pi * si, ei + pr * si + pi * sr
    carry_ref[:, 0:N_STATE] = jnp.broadcast_to(sr, (8, N_STATE))
    carry_ref[:, N_STATE:] = jnp.broadcast_to(si, (8, N_STATE))


def _run_fix(buf, pw_ref, s_ref, t, reverse):
    nblk = t // 8
    for re, im in _col_groups():
        sr, si = s_ref[:, re], s_ref[:, im]

        def step(i, carry, re=re, im=im, sr=sr, si=si):
            r = pl.ds(pl.multiple_of(i * 8, 8), 8)
            row = pl.ds((nblk - 1 - i) if reverse else i, 1)
            pr, pi = pw_ref[row, re], pw_ref[row, im]
            if reverse:
                pi = -pi
            buf[r, re] += pr * sr - pi * si
            buf[r, im] += pr * si + pi * sr
            return carry

        lax.fori_loop(0, nblk, step, 0)


STATE_BLOCKS = 2 * N_STATE // LANES
CH_BLOCKS = SSM_WIDTH // LANES


def _state_block(b):
    pair = b % (N_STATE // LANES)
    k = (pair * 2 * SSM_GROUP_CH) // LANES
    return slice(b * LANES, (b + 1) * LANES), slice(k * LANES, (k + 1) * LANES)


def _channel_block(c):
    w = N_STATE // CH_BLOCKS
    return slice(c * LANES, (c + 1) * LANES), slice(c * w, (c + 1) * w), slice(N_STATE + c * w, N_STATE + (c + 1) * w)


def _to_states(vb, w_ref, buf, nt):
    for b in range(STATE_BLOCKS):
        lanes, ch = _state_block(b)
        buf[:, lanes] = _dot_nt(vb[:, ch], w_ref[lanes, ch]) if nt else _dot(vb[:, ch], w_ref[ch, lanes])


def _to_channels(buf, w_ref, nt):
    outs = []
    for c in range(CH_BLOCKS):
        ch, re, im = _channel_block(c)
        xr, xi = buf[:, re].astype(BF16), buf[:, im].astype(BF16)
        if nt:
            outs.append(_dot_nt(xr, w_ref[ch, re]) + _dot_nt(xi, w_ref[ch, im]))
        else:
            outs.append(_dot(xr, w_ref[re, ch]) + _dot(xi, w_ref[im, ch]))
    return jnp.concatenate(outs, axis=-1)


def _ssm_fwd(u, bblk, cblk, lam, d_row, w_glu, b_glu, w_o_ssm, xch):
    n = u.shape[0]
    t = min(SCAN_T, n)
    perm = _perm_matrix(t)

    def body(u_ref, p_ref, bblk_ref, cblk_ref, lam_ref, d_ref, wg_ref, bg_ref, wo_ref, y_ref, ys_ref, st_ref,
             buf, pw_ref, carry_ref, s_ref):
        @pl.when(pl.program_id(0) == 0)
        def _():
            carry_ref[...] = jnp.zeros_like(carry_ref)
            _power_table(lam_ref, pw_ref, t // SUBCHUNKS)

        st_ref[0] = carry_ref[...]
        u_t = u_ref[...]
        p = p_ref[...]
        ub = _dot(p, u_t.astype(BF16)).astype(BF16)
        _to_states(ub, bblk_ref, buf, False)
        _run_scan(buf, lam_ref, t, False)
        _run_carries(buf, pw_ref, carry_ref, s_ref, t, False)
        _run_fix(buf, pw_ref, s_ref, t, False)
        y = d_ref[...] * u_t + _unpermute(p, _to_channels(buf, cblk_ref, False))
        y_ref[...] = y
        z, _ = _gelu(y)
        s = _sigmoid(_dot(z.astype(BF16), wg_ref[...]) + bg_ref[...])
        zgb = (z * s).astype(BF16)
        for j in range(N_DEV):
            ys_ref[:, j * OUT_SHARD:(j + 1) * OUT_SHARD] = _dot(zgb, wo_ref[j])

    consts = [perm, bblk, cblk, lam, d_row, w_glu, b_glu, w_o_ssm]
    return _call(
        body, "ssm_fwd", (n // t,), [u] + consts, [_rows(u, t)] + [_const(a) for a in consts],
        [_sds((n, SSM_WIDTH), F32), _sds((n, D_MODEL), F32), _sds((n // t, 8, 2 * N_STATE), F32)],
        [pl.BlockSpec((t, SSM_WIDTH), lambda i: (i, 0)), pl.BlockSpec((t, D_MODEL), lambda i: (i, 0)),
         pl.BlockSpec((1, 8, 2 * N_STATE), lambda i: (i, 0, 0))],
        scratch=[pltpu.VMEM((t, 2 * N_STATE), F32), pltpu.VMEM((t // SUBCHUNKS, 2 * N_STATE), F32),
                 pltpu.VMEM((8, 2 * N_STATE), F32), pltpu.VMEM((8, 2 * N_STATE), F32)],
        xch=xch)


def _head_norm_rope(slab, gain, cos_t, sin_t):
    xn, inv = _rms(slab, gain, QK_HEAD)
    lo, hi = xn[:, 0:128], xn[:, 128:256]
    return jnp.concatenate([lo, hi * cos_t + _rope_rot(hi) * sin_t], axis=-1), inv


def _head_norm_rope_bwd(g, slab, gain, inv, cos_t, sin_t):
    g_lo, g_hi = g[:, 0:128], g[:, 128:256]
    g_n = jnp.concatenate([g_lo, g_hi * cos_t + _rope_rot_t(g_hi * sin_t)], axis=-1)
    return _rms_bwd(g_n, slab, gain, inv, QK_HEAD)


def _qkv_prep(ql, kvl, q_a_norm, kv_a_norm, wq, wkv, gq, gk, cos_t, sin_t):
    n = ql.shape[0]
    tm = ROW_T

    def body(ql_ref, kvl_ref, cos_ref, sin_ref, qa_ref, ka_ref, wq_ref, wkv_ref, gq_ref, gk_ref,
             q_ref, k_ref, v_ref, kt_ref, vt_ref):
        cos_t, sin_t = cos_ref[...], sin_ref[...]
        qa, _ = _rms(ql_ref[...], qa_ref[...], Q_LORA)
        qab = qa.astype(BF16)
        kvl_t = kvl_ref[...]
        ca, _ = _rms(kvl_t[:, 0:KV_LORA], ka_ref[...], KV_LORA)
        cab = ca.astype(BF16)
        kpe = kvl_t[:, KV_LORA:KV_LAT_PAD]
        q_pre = _dot(qab, wq_ref[...])
        kv_pre = _dot(cab, wkv_ref[...])
        for h in range(N_HEADS):
            qh, _ = _head_norm_rope(q_pre[:, h * QK_PAD:(h + 1) * QK_PAD], gq_ref[...], cos_t, sin_t)
            q_ref[h] = (qh * ATT_SCALE).astype(BF16)
            kv_h = kv_pre[:, h * QK_PAD:(h + 1) * QK_PAD]
            kh, _ = _head_norm_rope(jnp.concatenate([kv_h[:, 0:QK_NOPE], kpe], axis=-1), gk_ref[...], cos_t, sin_t)
            k_ref[h] = kh.astype(BF16)
            kt_ref[h] = kh.T.astype(BF16)
            vh = kv_h[:, QK_NOPE:]
            v_ref[h] = vh.astype(BF16)
            vt_ref[h] = vh.T.astype(BF16)

    row_ins, consts = [ql, kvl, cos_t, sin_t], [q_a_norm, kv_a_norm, wq, wkv, gq, gk]
    outs = [_sds((N_HEADS, n, QK_PAD), BF16), _sds((N_HEADS, n, QK_PAD), BF16), _sds((N_HEADS, n, V_HEAD), BF16),
            _sds((N_HEADS, QK_PAD, n), BF16), _sds((N_HEADS, V_HEAD, n), BF16)]
    out_specs = [_rows(o, tm) for o in outs[:3]] + [
        pl.BlockSpec((N_HEADS, QK_PAD, tm), lambda i: (0, 0, i)), pl.BlockSpec((N_HEADS, V_HEAD, tm), lambda i: (0, 0, i))]
    return _call(body, "qkv_prep", (n // tm,), row_ins + consts,
                 [_rows(a, tm) for a in row_ins] + [_const(a) for a in consts], outs, out_specs)


def _causal_mask_t(st, t):
    key = lax.broadcasted_iota(jnp.int32, (t, t), 0)
    qry = lax.broadcasted_iota(jnp.int32, (t, t), 1)
    return jnp.where(key <= qry, st, -jnp.inf)


def _attn_fwd(q, k, vt, xch):
    n = q.shape[1]
    t = min(ATT_T, n)

    def body(q_ref, k_ref, vt_ref, o_ref, lse_ref, ot_ref):
        i = pl.program_id(1)
        qt = q_ref[0]

        def kv_tile(j, carry, diag):
            m, l, acc = carry
            ts = t // ATT_SUB
            sts = []
            for a in range(ATT_SUB):
                r0 = pl.multiple_of(j * t + a * ts, ts)
                st = _dot_nt(k_ref[0, pl.ds(r0, ts), :], qt)
                if diag:
                    key = lax.broadcasted_iota(jnp.int32, (ts, t), 0) + a * ts
                    qry = lax.broadcasted_iota(jnp.int32, (ts, t), 1)
                    st = jnp.where(key <= qry, st, -jnp.inf)
                sts.append(st)
            for a, st in enumerate(sts):
                r0 = pl.multiple_of(j * t + a * ts, ts)
                m_new = jnp.maximum(m, jnp.max(st, 0, keepdims=True))
                alpha = jnp.exp(m - m_new)
                pt = jnp.exp(st - m_new)
                l = alpha * l + jnp.sum(pt, 0, keepdims=True)
                acc = alpha * acc + _dot(vt_ref[0, :, pl.ds(r0, ts)], pt.astype(BF16))
                m = m_new
            return m, l, acc

        init = (jnp.full((1, t), -jnp.inf, F32), jnp.zeros((1, t), F32), jnp.zeros((V_HEAD, t), F32))
        carry = lax.fori_loop(0, i, functools.partial(kv_tile, diag=False), init)
        m, l, acc = kv_tile(i, carry, True)
        out_t = acc / l
        o_ref[...] = out_t.T
        ot_ref[...] = out_t.astype(BF16)
        lse_ref[0] = m + jnp.log(l)

    return _call(
        body, "attn_fwd", (N_HEADS, n // t), [q, k, vt],
        [pl.BlockSpec((1, t, QK_PAD), lambda h, i: (h, i, 0)), pl.BlockSpec((1, n, QK_PAD), lambda h, i: (h, 0, 0)),
         pl.BlockSpec((1, V_HEAD, n), lambda h, i: (h, 0, 0))],
        [_sds((n, N_HEADS * V_HEAD), F32), _sds((N_HEADS, 1, n), F32), _sds((N_HEADS * V_HEAD, n), BF16)],
        [pl.BlockSpec((t, V_HEAD), lambda h, i: (i, h)), pl.BlockSpec((1, 1, t), lambda h, i: (h, 0, i)),
         pl.BlockSpec((V_HEAD, t), lambda h, i: (h, i))],
        xch=xch)


def _merge(attn, gs, gm, y_ssm, x, w_o_mla, w_out):
    n = x.shape[0]

    def body(at_ref, gs_ref, gm_ref, ys_ref, x_ref, wo_ref, wout_ref, h_ref, ym_ref, mxt_ref):
        y_mla = _dot(at_ref[...].astype(BF16), wo_ref[...])
        ym_ref[...] = y_mla
        mixed = _sigmoid(gs_ref[...]) * ys_ref[...] + _sigmoid(gm_ref[...]) * y_mla
        mxt_ref[...] = mixed.T.astype(BF16)
        h_ref[...] = x_ref[...] + _dot(mixed.astype(BF16), wout_ref[...])

    outs = [((n, D_MODEL), F32), ((n, D_MODEL), F32)]
    return _row_call(body, "merge", n, MM_T, [attn, gs, gm, y_ssm, x], [w_o_mla, w_out], outs,
                     col_outs=[((D_MODEL, n), BF16)])


def _mlp_fwd_loss(h, target, norm_mlp, w_up, w_down):
    n = h.shape[0]

    def body(h_ref, t_ref, g_ref, wu_ref, wd_ref, hn_ref, do_ref, hnt_ref, loss_ref):
        h_t = h_ref[...]
        hn, _ = _rms(h_t, g_ref[...], D_MODEL)
        hb = hn.astype(BF16)
        hn_ref[...] = hb
        hnt_ref[...] = hn.T.astype(BF16)
        out = h_t
        for j in range(N_DEV):
            a = jnp.maximum(_dot(hb, wu_ref[j]), 0.0)
            out += _dot((a * a).astype(BF16), wd_ref[j])
        err = out - t_ref[...]
        do_ref[...] = err * (1.0 / D_MODEL)
        _acc(loss_ref, jnp.broadcast_to(jnp.sum(err * err) * (0.5 / D_MODEL), loss_ref.shape))

    outs = [((n, D_MODEL), BF16), ((n, D_MODEL), F32)]
    return _row_call(body, "mlp_fwd_loss", n, MM_T, [h, target], [norm_mlp, w_up, w_down], outs, [((8, 128), F32)],
                     col_outs=[((D_MODEL, n), BF16)])


def _mlp_bwd(dout, hn, h, norm_mlp, w_up, w_down):
    n = h.shape[0]

    def body(do_ref, hn_ref, h_ref, g_ref, wu_ref, wd_ref, da_ref, dh_ref, dob_ref, hidt_ref, dg_ref):
        dout_t = do_ref[...]
        doutb = dout_t.astype(BF16)
        dob_ref[...] = doutb
        hb = hn_ref[...]
        dhn = jnp.zeros_like(dout_t)
        for j in range(N_DEV):
            cols = slice(j * FF_SHARD, (j + 1) * FF_SHARD)
            a = jnp.maximum(_dot(hb, wu_ref[j]), 0.0)
            hidt_ref[cols, :] = (a * a).T.astype(BF16)
            da = (_dot_nt(doutb, wd_ref[j]) * (2.0 * a)).astype(BF16)
            da_ref[:, cols] = da
            dhn += _dot_nt(da, wu_ref[j])
        h_t = h_ref[...]
        inv = lax.rsqrt(jnp.sum(h_t * h_t, -1, keepdims=True) * (1.0 / D_MODEL) + EPS)
        dx, dg = _rms_bwd(dhn, h_t, g_ref[...], inv, D_MODEL)
        dh_ref[...] = dout_t + dx
        _acc(dg_ref, jnp.sum(dg, 0, keepdims=True))

    outs = [((n, D_FF), BF16), ((n, D_MODEL), F32), ((n, D_MODEL), BF16)]
    return _row_call(body, "mlp_bwd", n, MM_T, [dout, hn, h], [norm_mlp, w_up, w_down], outs, [((1, D_MODEL), F32)],
                     col_outs=[((D_FF, n), BF16)])


def _merge_bwd(dh, gs, gm, y_ssm, y_mla, w_out, w_o_mla):
    n = dh.shape[0]

    def body(dh_ref, gs_ref, gm_ref, ys_ref, ym_ref, wout_ref, wo_ref, dgs_ref, dgm_ref, dys_ref, dym_ref, dat_ref):
        dmix = _dot_nt(dh_ref[...].astype(BF16), wout_ref[...])
        sgs, sgm = _sigmoid(gs_ref[...]), _sigmoid(gm_ref[...])
        dgs_ref[...] = (dmix * ys_ref[...] * sgs * (1.0 - sgs)).astype(BF16)
        dgm_ref[...] = (dmix * ym_ref[...] * sgm * (1.0 - sgm)).astype(BF16)
        dys_ref[...] = (dmix * sgs).astype(BF16)
        dym = (dmix * sgm).astype(BF16)
        dym_ref[...] = dym
        dat_ref[...] = _dot_nt(dym, wo_ref[...])

    outs = [((n, D_MODEL), BF16)] * 4 + [((n, D_MODEL), F32)]
    return _row_call(body, "merge_bwd", n, MM_T, [dh, gs, gm, y_ssm, y_mla], [w_out, w_o_mla], outs)


def _attn_bwd(q, k, kt, v, out, lse, dout, xch):
    n = q.shape[1]
    t = min(ATT_T, n)
    nt = n // t

    def body(q_ref, k_ref, kt_ref, v_ref, o_ref, lse_ref, do_ref, dq_ref, dk_ref, dv_ref, delta_ref, dqt_ref):
        j = pl.program_id(1)

        @pl.when(j == 0)
        def _():
            dqt_ref[...] = jnp.zeros_like(dqt_ref)
            prod = do_ref[...] * o_ref[...]
            delta_ref[...] = lax.dot_general(jnp.ones((8, V_HEAD), F32), prod, (((1,), (1,)), ((), ())),
                                             precision=lax.Precision.HIGHEST, preferred_element_type=F32)

        k_t = k_ref[0]
        kt_t = kt_ref[0]
        v_t = v_ref[0]

        def q_tile(i, carry, diag):
            dk, dv = carry
            r0 = pl.multiple_of(i * t, t)
            rows = pl.ds(r0, t)
            qt = q_ref[0, rows, :]
            st = _dot_nt(k_t, qt)
            if diag:
                st = _causal_mask_t(st, t)
            pt = jnp.exp(st - lse_ref[0, :, rows])
            dob = do_ref[rows, :].astype(BF16)
            dv = dv + _dot(pt.astype(BF16), dob)
            dst = (pt * (_dot_nt(v_t, dob) - delta_ref[0:1, rows])).astype(BF16)
            dk = dk + _dot(dst, qt)
            dqt_ref[:, rows] += _dot(kt_t, dst)
            return dk, dv

        carry = q_tile(j, (jnp.zeros((t, QK_PAD), F32), jnp.zeros((t, V_HEAD), F32)), True)
        dk, dv = lax.fori_loop(j + 1, nt, functools.partial(q_tile, diag=False), carry)
        dk_ref[0] = dk
        dv_ref[0] = dv

        @pl.when(j == nt - 1)
        def _():
            for c in range(0, n, t):
                dq_ref[0, c:c + t, :] = dqt_ref[:, c:c + t].T

    return _call(
        body, "attn_bwd", (N_HEADS, nt), [q, k, kt, v, out, lse, dout],
        [pl.BlockSpec((1, n, QK_PAD), lambda h, j: (h, 0, 0)), pl.BlockSpec((1, t, QK_PAD), lambda h, j: (h, j, 0)),
         pl.BlockSpec((1, QK_PAD, t), lambda h, j: (h, 0, j)), pl.BlockSpec((1, t, V_HEAD), lambda h, j: (h, j, 0)),
         pl.BlockSpec((n, V_HEAD), lambda h, j: (0, h)), pl.BlockSpec((1, 1, n), lambda h, j: (h, 0, 0)),
         pl.BlockSpec((n, V_HEAD), lambda h, j: (0, h))],
        [_sds((N_HEADS, n, QK_PAD), F32), _sds((N_HEADS, n, QK_PAD), F32), _sds((N_HEADS, n, V_HEAD), F32)],
        [pl.BlockSpec((1, n, QK_PAD), lambda h, j: (h, 0, 0)), pl.BlockSpec((1, t, QK_PAD), lambda h, j: (h, j, 0)),
         pl.BlockSpec((1, t, V_HEAD), lambda h, j: (h, j, 0))],
        scratch=[pltpu.VMEM((8, n), F32), pltpu.VMEM((QK_PAD, n), F32)],
        xch=xch)


def _qkv_prep_bwd(ql, kvl, dq, dk, dv, q_a_norm, kv_a_norm, wq, wkv, gq, gk, cos_t, sin_t, xch):
    n = ql.shape[0]

    def body(ql_ref, kvl_ref, cos_ref, sin_ref, dq_ref, dk_ref, dv_ref, qa_ref, ka_ref, wq_ref, wkv_ref, gq_ref, gk_ref,
             dql_ref, dkvl_ref, dqa_ref, dka_ref, dgq_ref, dgk_ref, dwq_ref, dwkv_ref, dqp_ref, dkvp_ref):
        cos_t, sin_t = cos_ref[...], sin_ref[...]
        ql_t = ql_ref[...]
        qa, inv_qa = _rms(ql_t, qa_ref[...], Q_LORA)
        qab = qa.astype(BF16)
        kvl_t = kvl_ref[...]
        ckv = kvl_t[:, 0:KV_LORA]
        ca, inv_ca = _rms(ckv, ka_ref[...], KV_LORA)
        cab = ca.astype(BF16)
        kpe = kvl_t[:, KV_LORA:KV_LAT_PAD]
        dgq = jnp.zeros((1, QK_PAD), F32)
        dgk = jnp.zeros((1, QK_PAD), F32)
        dkpe = jnp.zeros_like(kpe)
        q_pre = _dot(qab, wq_ref[...])
        kv_pre = _dot(cab, wkv_ref[...])
        for h in range(N_HEADS):
            head = slice(h * QK_PAD, (h + 1) * QK_PAD)
            q_slab = q_pre[:, head]
            inv = lax.rsqrt(jnp.sum(q_slab * q_slab, -1, keepdims=True) * (1.0 / QK_HEAD) + EPS)
            d_slab, dg = _head_norm_rope_bwd(dq_ref[h] * ATT_SCALE, q_slab, gq_ref[...], inv, cos_t, sin_t)
            dqp_ref[:, head] = d_slab.astype(BF16)
            dgq += jnp.sum(dg, 0, keepdims=True)
            k_slab = jnp.concatenate([kv_pre[:, h * QK_PAD:h * QK_PAD + QK_NOPE], kpe], axis=-1)
            inv = lax.rsqrt(jnp.sum(k_slab * k_slab, -1, keepdims=True) * (1.0 / QK_HEAD) + EPS)
            d_slab, dg = _head_norm_rope_bwd(dk_ref[h], k_slab, gk_ref[...], inv, cos_t, sin_t)
            dkvp_ref[:, head] = jnp.concatenate([d_slab[:, 0:QK_NOPE], dv_ref[h]], axis=-1).astype(BF16)
            dkpe += d_slab[:, QK_NOPE:QK_PAD]
            dgk += jnp.sum(dg, 0, keepdims=True)
        dqa = _dot_nt(dqp_ref[...], wq_ref[...])
        dx, dg = _rms_bwd(dqa, ql_t, qa_ref[...], inv_qa, Q_LORA)
        dql_ref[...] = dx.astype(BF16)
        _acc(dqa_ref, jnp.sum(dg, 0, keepdims=True))
        dca = _dot_nt(dkvp_ref[...], wkv_ref[...])
        dx, dg = _rms_bwd(dca, ckv, ka_ref[...], inv_ca, KV_LORA)
        dkvl_ref[:, 0:KV_LORA] = dx.astype(BF16)
        dkvl_ref[:, KV_LORA:KV_LAT_PAD] = dkpe.astype(BF16)
        _acc(dka_ref, jnp.sum(dg, 0, keepdims=True))
        _acc(dgq_ref, dgq)
        _acc(dgk_ref, dgk)
        _acc(dwq_ref, _dot_tn(qab, dqp_ref[...]))
        _acc(dwkv_ref, _dot_tn(cab, dkvp_ref[...]))

    wide = N_HEADS * QK_PAD
    row_outs = [((n, Q_LORA), BF16), ((n, KV_LAT_PAD), BF16)]
    acc_outs = [((1, Q_LORA), F32), ((1, KV_LORA), F32), ((1, QK_PAD), F32), ((1, QK_PAD), F32),
                ((Q_LORA, wide), F32), ((KV_LORA, wide), F32)]
    return _row_call(body, "qkv_prep_bwd", n, ROW_T, [ql, kvl, cos_t, sin_t, dq, dk, dv],
                     [q_a_norm, kv_a_norm, wq, wkv, gq, gk], row_outs, acc_outs, xch=xch,
                     scratch=[pltpu.VMEM((ROW_T, wide), BF16), pltpu.VMEM((ROW_T, wide), BF16)])


def _glu_bwd(dy_ssm, y, w_glu, b_glu, w_o_ssm):
    n = y.shape[0]

    def body(dys_ref, y_ref, wg_ref, bg_ref, wo_ref, dy_ref, db_ref, dwg_ref, dwo_ref):
        y_t = y_ref[...]
        z, th = _gelu(y_t)
        zb = z.astype(BF16)
        s = _sigmoid(_dot(zb, wg_ref[...]) + bg_ref[...])
        dys = dys_ref[...]
        dzg = jnp.zeros_like(y_t)
        for j in range(N_DEV):
            dzg += _dot_nt(dys[:, j * OUT_SHARD:(j + 1) * OUT_SHARD], wo_ref[j])
        dt = dzg * z * s * (1.0 - s)
        dtb = dt.astype(BF16)
        dz = dzg * s + _dot_nt(dtb, wg_ref[...])
        dy_ref[...] = dz * _gelu_grad(y_t, th)
        _acc(db_ref, jnp.sum(dt, 0, keepdims=True))
        _acc(dwg_ref, _dot_tn(zb, dtb))
        _acc(dwo_ref, _dot_tn((z * s).astype(BF16), dys))

    acc_outs = [((1, SSM_WIDTH), F32), ((SSM_WIDTH, SSM_WIDTH), F32), ((SSM_WIDTH, D_MODEL), F32)]
    return _row_call(body, "glu_bwd", n, ROW_T, [dy_ssm, y], [w_glu, b_glu, w_o_ssm], [((n, SSM_WIDTH), F32)], acc_outs)


def _ssm_bwd(u, dy, st, bblk, cblk, lam, d_row, xch):
    n = u.shape[0]
    t = min(SCAN_T, n)
    nc = n // t
    kb = 512
    perm = _perm_matrix(t)

    def body(u_ref, dy_ref, st_ref, p_ref, bblk_ref, cblk_ref, lam_ref, d_ref,
             du_ref, dlam_ref, dd_ref, db_ref, dct_ref,
             buf_x, buf_a, pw_ref, carry_ref, xcarry_ref, sx_ref, sa_ref, db_acc, dct_acc):
        @pl.when(pl.program_id(0) == 0)
        def _():
            carry_ref[...] = jnp.zeros_like(carry_ref)
            db_acc[...] = jnp.zeros_like(db_acc)
            dct_acc[...] = jnp.zeros_like(dct_acc)
            _power_table(lam_ref, pw_ref, t // SUBCHUNKS)

        u_t = u_ref[...]
        dy_t = dy_ref[...]
        p = p_ref[...]
        ub = _dot(p, u_t.astype(BF16)).astype(BF16)
        dyb = _dot(p, dy_t.astype(BF16)).astype(BF16)
        _to_states(ub, bblk_ref, buf_x, False)
        xcarry_ref[...] = st_ref[0]
        _run_scan(buf_x, lam_ref, t, False)
        _run_carries(buf_x, pw_ref, xcarry_ref, sx_ref, t, False)
        _run_fix(buf_x, pw_ref, sx_ref, t, False)
        _to_states(dyb, cblk_ref, buf_a, True)
        _run_scan(buf_a, lam_ref, t, True)
        _run_carries(buf_a, pw_ref, carry_ref, sa_ref, t, True)
        _run_fix(buf_a, pw_ref, sa_ref, t, True)
        du_ref[...] = (d_ref[...] * dy_t + _unpermute(p, _to_channels(buf_a, bblk_ref, True))).astype(BF16)
        for b in range(STATE_BLOCKS):
            lanes, ch = _state_block(b)
            db_acc[ch, lanes] += _dot_tn(ub[:, ch], buf_a[:, lanes].astype(BF16))
            dct_acc[ch, lanes] += _dot_tn(dyb[:, ch], buf_x[:, lanes].astype(BF16))
        for c in range(0, N_STATE, kb):
            re, im = pl.ds(c, kb), pl.ds(N_STATE + c, kb)
            xr, xi = buf_x[pl.ds(0, t - 8), re], buf_x[pl.ds(0, t - 8), im]
            ar, ai = buf_a[pl.ds(8, t - 8), re], buf_a[pl.ds(8, t - 8), im]
            x0r, x0i = sx_ref[:, re], sx_ref[:, im]
            a0r, a0i = buf_a[0:8, re], buf_a[0:8, im]
            dlam_part_re = (jnp.sum(ar * xr + ai * xi, 0, keepdims=True)
                            + jnp.sum(a0r * x0r + a0i * x0i, 0, keepdims=True))
            dlam_part_im = (jnp.sum(ai * xr - ar * xi, 0, keepdims=True)
                            + jnp.sum(a0i * x0r - a0r * x0i, 0, keepdims=True))

            @pl.when(pl.program_id(0) == 0)
            def _(c=c):
                dlam_ref[0:1, c:c + kb] = jnp.zeros((1, kb), F32)
                dlam_ref[1:2, c:c + kb] = jnp.zeros((1, kb), F32)

            dlam_ref[0:1, c:c + kb] += dlam_part_re
            dlam_ref[1:2, c:c + kb] += dlam_part_im
        _acc(dd_ref, jnp.sum(dy_t * u_t, 0, keepdims=True))

        @pl.when(pl.program_id(0) == nc - 1)
        def _():
            pltpu.sync_copy(db_acc, db_ref)
            pltpu.sync_copy(dct_acc, dct_ref)

    rev = lambda i: (nc - 1 - i, 0)
    consts = [perm, bblk, cblk, lam, d_row]
    wide = (SSM_WIDTH, 2 * N_STATE)
    return _call(
        body, "ssm_bwd", (nc,), [u, dy, st] + consts,
        [pl.BlockSpec((t, SSM_WIDTH), rev), pl.BlockSpec((t, SSM_WIDTH), rev),
         pl.BlockSpec((1, 8, 2 * N_STATE), lambda i: (nc - 1 - i, 0, 0))] + [_const(a) for a in consts],
        [_sds((n, SSM_WIDTH), BF16), _sds((2, N_STATE), F32), _sds((1, SSM_WIDTH), F32), _sds(wide, F32), _sds(wide, F32)],
        [pl.BlockSpec((t, SSM_WIDTH), rev), pl.BlockSpec((2, N_STATE), lambda i: (0, 0)),
         pl.BlockSpec((1, SSM_WIDTH), lambda i: (0, 0)), ANY, ANY],
        scratch=[pltpu.VMEM((t, 2 * N_STATE), F32)] * 2 + [pltpu.VMEM((t // SUBCHUNKS, 2 * N_STATE), F32)]
        + [pltpu.VMEM((8, 2 * N_STATE), F32)] * 4 + [pltpu.VMEM(wide, F32)] * 2,
        xch=xch)


def _in_proj_bwd(pieces, dh, x, xn_t, norm_mix, w_in_pad):
    n = x.shape[0]
    tm = min(MM_T, n)
    nt = n // tm

    def body(du_ref, dql_ref, dkvl_ref, dgs_ref, dgm_ref, dh_ref, x_ref, xnt_ref, g_ref, w_ref,
             dx_ref, dg_ref, dw_ref, acc_ref):
        @pl.when(pl.program_id(0) == 0)
        def _():
            acc_ref[...] = jnp.zeros_like(acc_ref)

        xnt = xnt_ref[...]
        dxn = jnp.zeros((tm, D_MODEL), F32)
        for ref, (a, b) in zip((du_ref, dql_ref, dkvl_ref, dgs_ref, dgm_ref), IN_SEGS):
            piece = ref[...]
            dxn += _dot_nt(piece, w_ref[:, a:b])
            acc_ref[:, a:b] += _dot(xnt, piece)
        x_t = x_ref[...]
        inv = lax.rsqrt(jnp.sum(x_t * x_t, -1, keepdims=True) * (1.0 / D_MODEL) + EPS)
        dx, dg = _rms_bwd(dxn, x_t, g_ref[...], inv, D_MODEL)
        dx_ref[...] = dh_ref[...] + dx
        _acc(dg_ref, jnp.sum(dg, 0, keepdims=True))

        @pl.when(pl.program_id(0) == nt - 1)
        def _():
            pltpu.sync_copy(acc_ref, dw_ref)

    row_ins, consts = list(pieces) + [dh, x], [norm_mix, w_in_pad]
    in_specs = ([_rows(a, tm) for a in row_ins] + [pl.BlockSpec((D_MODEL, tm), lambda i: (0, i))]
                + [_const(a) for a in consts])
    return _call(
        body, "in_proj_bwd", (nt,), row_ins + [xn_t] + consts, in_specs,
        [_sds((n, D_MODEL), F32), _sds((1, D_MODEL), F32), _sds((D_MODEL, D_IN_PAD), F32)],
        [pl.BlockSpec((tm, D_MODEL), lambda i: (i, 0)), pl.BlockSpec((1, D_MODEL), lambda i: (0, 0)), ANY],
        scratch=[pltpu.VMEM((D_MODEL, D_IN_PAD), F32)])


def _swap_minor(a):
    g, r, c = a.shape[1:]
    return jnp.transpose(a[0], (0, 2, 1)).reshape(g * c, r)


def _pad_in(w):
    return jnp.concatenate([w[:, :KV_END], jnp.zeros((w.shape[0], D_IN_PAD - D_IN), w.dtype), w[:, KV_END:]], axis=1)


def _unpad_in(w):
    return jnp.concatenate([w[:, :KV_END], w[:, KV_END + D_IN_PAD - D_IN:]], axis=1)


def _pad_gain(g):
    return jnp.pad(g, ((0, 0), (0, QK_PAD - QK_HEAD)))


def _place():
    x, y, c = lax.axis_index("x"), lax.axis_index("y"), lax.axis_index("c")
    chips = [(x, y), (1 - x, y), (x, 1 - y), (1 - x, 1 - y)]
    return x, y, c, chips


def _all_gather(block, name):
    rows, lanes = block.shape

    def body(x_ref, out_ref, send_sems, recv_sems, local_sem):
        x, y, c, chips = _place()
        me, sibling = (x, y, c), (x, y, 1 - c)

        def slot(px, py, pc):
            return out_ref.at[4 * px + 2 * py + pc]

        def copy(k, blk, to, src=None):
            return pltpu.make_async_remote_copy(
                src_ref=slot(*blk) if src is None else src, dst_ref=slot(*blk),
                send_sem=send_sems.at[k], recv_sem=recv_sems.at[k], device_id=to, device_id_type=MESH)

        mine = pltpu.make_async_copy(x_ref, slot(*me), local_sem)
        mine.start()
        first = [copy(0, me, sibling, src=x_ref)]
        first += [copy(1 + j, me, (*chip, c), src=x_ref) for j, chip in enumerate(chips[1:])]
        for cp in first:
            cp.start()
        passed = [copy(4 + j, (*chip, c), sibling) for j, chip in enumerate(chips[1:])]
        for j, chip in enumerate(chips[1:]):
            copy(1 + j, (*chip, c), me).wait_recv()
            passed[j].start()
        copy(0, sibling, me).wait_recv()
        for j, chip in enumerate(chips[1:]):
            copy(4 + j, (*chip, 1 - c), me).wait_recv()
        for cp in first + passed:
            cp.wait_send()
        mine.wait()

    return pl.pallas_call(
        body,
        name=name,
        in_specs=[ANY],
        out_specs=ANY,
        out_shape=_sds((N_DEV, rows, lanes), block.dtype),
        scratch_shapes=[pltpu.SemaphoreType.DMA((7,)), pltpu.SemaphoreType.DMA((7,)), pltpu.SemaphoreType.DMA],
    )(block)


def _reduce_scatter(parts, gather, name):
    _, rows, lanes = parts.shape

    def body(p_ref, g_ref, out_ref, ga_ref, own, land_a, send_b, land_b, sa, ra, sb, rb, lo, *g_sems):
        x, y, c, chips = _place()
        sibling = (x, y, 1 - c)
        _xchg_start([False], [g_ref], [ga_ref], *g_sems)

        def blk(chip, core):
            return p_ref.at[4 * chip[0] + 2 * chip[1] + core]

        to_sib = [pltpu.make_async_remote_copy(
            src_ref=blk(chips[k], 1 - c), dst_ref=land_a.at[k], send_sem=sa.at[k], recv_sem=ra.at[k],
            device_id=sibling, device_id_type=MESH) for k in range(4)]
        for cp in to_sib:
            cp.start()
        loads = [pltpu.make_async_copy(blk(chips[k], c), own.at[k], lo.at[k]) for k in range(4)]
        for cp in loads:
            cp.start()
        to_chip = [pltpu.make_async_remote_copy(
            src_ref=send_b.at[j], dst_ref=land_b.at[j], send_sem=sb.at[j], recv_sem=rb.at[j],
            device_id=(*chips[1 + j], c), device_id_type=MESH) for j in range(3)]
        for k in (1, 2, 3):
            to_sib[k].wait_recv()
            loads[k].wait()
            send_b[k - 1] = (own[k] + land_a[k]).astype(BF16)
            to_chip[k - 1].start()
        to_sib[0].wait_recv()
        loads[0].wait()
        acc = own[0] + land_a[0]
        for j in range(3):
            to_chip[j].wait_recv()
            acc = acc + land_b[j].astype(F32)
        out_ref[...] = acc
        for cp in to_sib + to_chip:
            cp.wait_send()
        _xchg_wait([False], [g_ref], [ga_ref], *g_sems)

    return pl.pallas_call(
        body,
        name=name,
        in_specs=[ANY, ANY],
        out_specs=[pl.BlockSpec(memory_space=pltpu.VMEM), ANY],
        out_shape=[_sds((rows, lanes), F32), _sds((N_DEV,) + gather.shape, gather.dtype)],
        scratch_shapes=[pltpu.VMEM((4, rows, lanes), F32), pltpu.VMEM((4, rows, lanes), F32),
                        pltpu.VMEM((3, rows, lanes), BF16), pltpu.VMEM((3, rows, lanes), BF16)]
        + [pltpu.SemaphoreType.DMA((4,))] * 2 + [pltpu.SemaphoreType.DMA((3,))] * 2 + [pltpu.SemaphoreType.DMA((4,))]
        + [pltpu.SemaphoreType.DMA((1,))] * 3,
        compiler_params=_params(),
    )(parts, gather)


def _adamw_math(w, g, m, v):
    m = ADAM_B1 * m + (1.0 - ADAM_B1) * g
    v = ADAM_B2 * v + (1.0 - ADAM_B2) * (g * g)
    m_hat = m / (1.0 - ADAM_B1 ** ADAM_STEP)
    v_hat = v / (1.0 - ADAM_B2 ** ADAM_STEP)
    delta = -ADAM_LR * (m_hat / (jnp.sqrt(v_hat) + ADAM_EPS) + ADAM_WD * w)
    return delta, m, v


def _row_tile(r):
    return max(t for t in range(8, min(r, 256) + 1, 8) if r % t == 0)


def _adamw(w, g, m, v, name):
    r, n = w.shape

    def body(w_ref, g_ref, m_ref, v_ref, d_ref, nm_ref, nv_ref):
        d_ref[...], nm_ref[...], nv_ref[...] = _adamw_math(w_ref[...], g_ref[...], m_ref[...], v_ref[...])

    return _row_call(body, name, r, _row_tile(r), [w, g, m, v], [], [((r, n), F32)] * 3)


def _adamw_sum(landed, w, m, v, name):
    r, n = w.shape

    def body(l_ref, w_ref, m_ref, v_ref, g_ref, d_ref, nm_ref, nv_ref):
        g = l_ref[0].astype(F32)
        for dev in range(1, N_DEV):
            g = g + l_ref[dev].astype(F32)
        g_ref[...] = g
        d_ref[...], nm_ref[...], nv_ref[...] = _adamw_math(w_ref[...], g, m_ref[...], v_ref[...])

    tm = max(t for t in range(16, min(r, 256) + 1, 16) if r % t == 0)
    return _row_call(body, name, r, tm, [landed, w, m, v], [], [((r, n), F32)] * 4)


def _adamw_small(gathered, w, m, v, row_counts):
    n_rows = w.shape[0]

    def body(ga_ref, w_ref, m_ref, v_ref, loss_ref, *out_refs):
        g = ga_ref[0]
        for dev in range(1, N_DEV):
            g = g + ga_ref[dev]
        loss_ref[...] = g[n_rows:n_rows + 8]
        g = g[0:n_rows]
        d, nm, nv = _adamw_math(w_ref[...], g, m_ref[...], v_ref[...])
        off = 0
        for p, rows in enumerate(row_counts):
            for k, val in enumerate((g, d, nm, nv)):
                out_refs[4 * p + k][...] = val[off:off + rows]
            off += rows

    outs = [_sds((8, LANES), F32)] + [_sds((rows, LANES), F32) for rows in row_counts for _ in range(4)]
    return pl.pallas_call(body, name="adamw_small", out_shape=outs, compiler_params=_params())(gathered, w, m, v)


SMALL = ("norm_mix", "q_a_norm", "kv_a_norm", "q_norm", "k_norm", "ssm_a_re", "ssm_a_im", "ssm_log_dt", "ssm_b_re",
         "ssm_b_im", "ssm_c_re", "ssm_c_im", "ssm_d", "b_glu", "norm_mlp")
WEIGHT_ORDER = ("norm_mix", "w_in", "q_a_norm", "kv_a_norm", "w_q_b", "w_kv_b", "q_norm", "k_norm", "w_o_mla",
                "ssm_a_re", "ssm_a_im", "ssm_log_dt", "ssm_b_re", "ssm_b_im", "ssm_c_re", "ssm_c_im", "ssm_d", "w_glu",
                "b_glu", "w_o_ssm", "w_out", "norm_mlp", "w_up", "w_down")
IN_SHARD = D_IN // N_DEV
Q_SHARD = QK_HEAD


def _pack_small(vals):
    parts = []
    for n in SMALL:
        flat = vals[n].reshape(-1)
        size = -(-flat.shape[0] // (8 * LANES)) * 8 * LANES
        parts.append(jnp.pad(flat, (0, size - flat.shape[0])).reshape(-1, LANES))
    return jnp.concatenate(parts, axis=0)


def _small_rows(like):
    return [-(-like[n].size // (8 * LANES)) * 8 for n in SMALL]


def _step(x, pos_col, target, w, small):
    bf = {n: a.astype(BF16) for n, a in w.items()}
    gq, gk = _pad_gain(small["q_norm"]), _pad_gain(small["k_norm"])
    a_re = small["ssm_a_re"].reshape(1, N_STATE)
    a_im = small["ssm_a_im"].reshape(1, N_STATE)
    log_dt = jnp.repeat(small["ssm_log_dt"].reshape(SSM_GROUPS), SSM_STATE).reshape(1, N_STATE)
    bt_re, bt_im = _swap_minor(small["ssm_b_re"]), _swap_minor(small["ssm_b_im"])
    c2_re, c2_im = _swap_minor(small["ssm_c_re"]), _swap_minor(small["ssm_c_im"])
    d_row = small["ssm_d"].reshape(1, SSM_WIDTH)

    w_in_all = _all_gather(bf["w_in"], "gather_w_in")
    w_in_pad = _pad_in(jnp.transpose(w_in_all, (1, 0, 2)).reshape(D_MODEL, D_IN))
    cos_t, sin_t = _rope_tables(pos_col)
    lam, bblk, cblk = _ssm_prep(a_re, a_im, log_dt, bt_re, bt_im, c2_re, c2_im)
    wq_mine = jnp.pad(bf["w_q_b"], ((0, 0), (0, QK_PAD - QK_HEAD)))
    u, ql, kvl, gs, gm, xn_t, w_glu, w_o_ssm = _in_proj(
        x, small["norm_mix"], w_in_pad, xch=[(bf["w_glu"], False), (bf["w_o_ssm"], False)])
    w_glu = w_glu.reshape(SSM_WIDTH, SSM_WIDTH)
    y, y_ssm, st, wq, wkv, w_o_mla, w_out = _ssm_fwd(
        u, bblk, cblk, lam, d_row, w_glu, small["b_glu"], w_o_ssm,
        xch=[(wq_mine, False), (bf["w_kv_b"], False), (bf["w_o_mla"], False), (bf["w_out"], False)])
    w_o_mla, w_out = w_o_mla.reshape(D_MODEL, D_MODEL), w_out.reshape(D_MODEL, D_MODEL)
    wq = jnp.transpose(wq, (1, 0, 2)).reshape(Q_LORA, N_HEADS * QK_PAD)
    wkv = jnp.transpose(wkv, (1, 0, 2)).reshape(KV_LORA, N_HEADS * QK_PAD)
    q, k, v, kt, vt = _qkv_prep(ql, kvl, small["q_a_norm"], small["kv_a_norm"], wq, wkv, gq, gk, cos_t, sin_t)
    attn, lse, attn_t, w_up, w_down = _attn_fwd(q, k, vt, xch=[(bf["w_up"], False), (bf["w_down"], False)])
    h, y_mla, mixed_t = _merge(attn, gs, gm, y_ssm, x, w_o_mla, w_out)
    hn, dout, hn_t, loss = _mlp_fwd_loss(h, target, small["norm_mlp"], w_up, w_down)

    da, dh, dout_b, hid_t, d_norm_mlp = _mlp_bwd(dout, hn, h, small["norm_mlp"], w_up, w_down)
    p_w_down = _matmul_tn_shards(hid_t, dout_b, "dw_down", False, tm=1024, turned=True)
    p_w_up = _matmul_tn_shards(hn_t, da, "dw_up", True, turned=True)
    dgs, dgm, dy_ssm, dy_mla, dattn = _merge_bwd(dh, gs, gm, y_ssm, y_mla, w_out, w_o_mla)
    p_w_out = _matmul_tn_shards(mixed_t, dh, "dw_out", False, tm=1024, turned=True)
    p_w_o_mla = _matmul_tn_shards(attn_t, dy_mla, "dw_o_mla", False, turned=True)
    dq, dk, dv, l_w_up, l_w_down, l_w_out, l_w_o_mla = _attn_bwd(
        q, k, kt, v, attn, lse, dattn, xch=[(p_w_up, True), (p_w_down, True), (p_w_out, True), (p_w_o_mla, True)])
    dql, dkvl, d_q_a_norm, d_kv_a_norm, d_gq, d_gk, g_wq, g_wkv = _qkv_prep_bwd(
        ql, kvl, dq, dk, dv, small["q_a_norm"], small["kv_a_norm"], wq, wkv, gq, gk, cos_t, sin_t, xch=[])
    p_wq = jnp.transpose(g_wq.reshape(Q_LORA, N_HEADS, QK_PAD), (1, 0, 2)).astype(BF16)
    p_wkv = jnp.transpose(g_wkv.reshape(KV_LORA, N_HEADS, QK_PAD), (1, 0, 2)).astype(BF16)
    dy, d_b_glu, g_w_glu, g_w_o_ssm = _glu_bwd(dy_ssm, y, w_glu, small["b_glu"], w_o_ssm)
    p_w_o_ssm = jnp.transpose(g_w_o_ssm.reshape(SSM_WIDTH, N_DEV, OUT_SHARD), (1, 0, 2)).astype(BF16)
    p_w_glu = g_w_glu.reshape(N_DEV, SSM_WIDTH // N_DEV, SSM_WIDTH).astype(BF16)
    du, dlam, d_d, d_bblk, d_cblk_t, l_wq, l_wkv, l_w_glu, l_w_o_ssm = _ssm_bwd(
        u, dy, st, bblk, cblk, lam, d_row, xch=[(p_wq, True), (p_wkv, True), (p_w_glu, True), (p_w_o_ssm, True)])
    d_a_re, d_a_im, d_log_dt, d_bt_re, d_bt_im, d_c_re, d_c_im = _ssm_prep_bwd(
        a_re, a_im, log_dt, bt_re, bt_im, dlam, d_bblk, d_cblk_t)
    dx, d_norm_mix, g_w_in_pad = _in_proj_bwd((du, dql, dkvl, dgs, dgm), dh, x, xn_t, small["norm_mix"], w_in_pad)
    tr = lambda mat: jnp.transpose(mat.reshape(SSM_GROUPS, SSM_GROUP_CH, SSM_STATE), (0, 2, 1))
    g_small = {
        "norm_mix": d_norm_mix, "q_a_norm": d_q_a_norm, "kv_a_norm": d_kv_a_norm,
        "q_norm": d_gq[:, :QK_HEAD], "k_norm": d_gk[:, :QK_HEAD],
        "ssm_a_re": d_a_re, "ssm_a_im": d_a_im, "ssm_log_dt": d_log_dt,
        "ssm_b_re": tr(d_bt_re), "ssm_b_im": tr(d_bt_im), "ssm_c_re": d_c_re, "ssm_c_im": d_c_im,
        "ssm_d": d_d, "b_glu": d_b_glu, "norm_mlp": d_norm_mlp,
    }
    parts = jnp.transpose(_unpad_in(g_w_in_pad).reshape(D_MODEL, N_DEV, IN_SHARD), (1, 0, 2))
    g_w_in_mine, g_small_all = _reduce_scatter(
        parts, jnp.concatenate([_pack_small(g_small), loss], axis=0), "reduce_w_in")
    landed = {"w_q_b": l_wq[:, :, :QK_HEAD], "w_kv_b": l_wkv, "w_o_mla": l_w_o_mla, "w_glu": l_w_glu,
              "w_o_ssm": l_w_o_ssm, "w_out": l_w_out, "w_up": l_w_up, "w_down": l_w_down}
    return dx, landed, g_w_in_mine, g_small_all


def kernel(x, positions, norm_mix, w_in, q_a_norm, kv_a_norm, w_q_b, w_kv_b, q_norm, k_norm, w_o_mla, ssm_a_re, ssm_a_im, ssm_log_dt, ssm_b_re, ssm_b_im, ssm_c_re, ssm_c_im, ssm_d, w_glu, b_glu, w_o_ssm, w_out, norm_mlp, w_up, w_down, loss_target, m_norm_mix, m_w_in, m_q_a_norm, m_kv_a_norm, m_w_q_b, m_w_kv_b, m_q_norm, m_k_norm, m_w_o_mla, m_ssm_a_re, m_ssm_a_im, m_ssm_log_dt, m_ssm_b_re, m_ssm_b_im, m_ssm_c_re, m_ssm_c_im, m_ssm_d, m_w_glu, m_b_glu, m_w_o_ssm, m_w_out, m_norm_mlp, m_w_up, m_w_down, v_norm_mix, v_w_in, v_q_a_norm, v_kv_a_norm, v_w_q_b, v_w_kv_b, v_q_norm, v_k_norm, v_w_o_mla, v_ssm_a_re, v_ssm_a_im, v_ssm_log_dt, v_ssm_b_re, v_ssm_b_im, v_ssm_c_re, v_ssm_c_im, v_ssm_d, v_w_glu, v_b_glu, v_w_o_ssm, v_w_out, v_norm_mlp, v_w_up, v_w_down):
    given = dict(locals())
    w = {n: given[n] for n in WEIGHT_ORDER}
    m = {n: given["m_" + n] for n in WEIGHT_ORDER}
    v = {n: given["v_" + n] for n in WEIGHT_ORDER}
    big = [n for n in WEIGHT_ORDER if n not in SMALL]
    small = {n: w[n] for n in SMALL}

    dx, landed, g_w_in, g_small_all = _step(
        x[0], positions.reshape(-1, 1), loss_target[0], {n: w[n][0] for n in big}, small)

    grads, deltas, new_m, new_v = {}, {}, {}, {}
    for n in big:
        if n == "w_in":
            g = g_w_in
            d, nm, nv = _adamw(w[n][0], g, m[n][0], v[n][0], "adamw_" + n)
        else:
            g, d, nm, nv = _adamw_sum(landed[n], w[n][0], m[n][0], v[n][0], "adamw_" + n)
        grads[n], deltas[n], new_m[n], new_v[n] = g[None], d[None], nm[None], nv[None]

    outs = _adamw_small(g_small_all, _pack_small(small), _pack_small({n: m[n] for n in SMALL}),
                        _pack_small({n: v[n] for n in SMALL}), _small_rows(small))
    for p, n in enumerate(SMALL):
        for k, dst in enumerate((grads, deltas, new_m, new_v)):
            dst[n] = outs[1 + 4 * p + k].reshape(-1)[:small[n].size].reshape(small[n].shape)

    return (outs[0][0, 0], dx[None], *[grads[n] for n in WEIGHT_ORDER], *[deltas[n] for n in WEIGHT_ORDER],
            *[new_m[n] for n in WEIGHT_ORDER], *[new_v[n] for n in WEIGHT_ORDER])
```

```python
import functools
import math

import numpy as np
import jax
import jax.numpy as jnp
from jax import lax
from jax.experimental import pallas as pl
from jax.experimental.pallas import tpu as pltpu

F32 = jnp.float32
BF16 = jnp.bfloat16

D_MODEL = 1024
SSM_GROUPS = 32
SSM_GROUP_CH = 16
SSM_WIDTH = 512
SSM_STATE = 64
N_STATE = SSM_GROUPS * SSM_STATE
N_HEADS = 8
QK_NOPE = 128
QK_ROPE = 64
QK_HEAD = 192
QK_PAD = 256
V_HEAD = 128
Q_LORA = 384
KV_LORA = 256
KV_LAT_PAD = 384
ROPE_THETA = 10000.0
D_FF = 4096
EPS = 1e-6
ATT_SCALE = QK_HEAD ** -0.5
N_DEV = 8
FF_SHARD = D_FF // N_DEV
OUT_SHARD = D_MODEL // N_DEV

IN_SEGS = ((0, 512), (512, 896), (896, 1280), (1280, 2304), (2304, 3328))
D_IN = 3264
D_IN_PAD = 3328
KV_END = 1216

ADAM_LR = 0.001
ADAM_B1 = 0.9
ADAM_B2 = 0.999
ADAM_EPS = 1e-08
ADAM_WD = 0.01
ADAM_STEP = 10

VMEM_LIMIT = 56 * 1024 * 1024
MESH = pl.DeviceIdType.MESH
ANY = pl.BlockSpec(memory_space=pl.ANY)
LANES = 128

SCAN_T = 256
SUBCHUNKS = 8
SCAN_CG = 512
ATT_T = 512
ATT_SUB = 2
ROW_T = 256
MM_T = 512


def _params(sem=None):
    return pltpu.CompilerParams(dimension_semantics=sem, vmem_limit_bytes=VMEM_LIMIT)


def _rows(arr, tm):
    if arr.ndim == 2:
        return pl.BlockSpec((tm, arr.shape[1]), lambda i: (i, 0))
    return pl.BlockSpec((arr.shape[0], tm, arr.shape[2]), lambda i: (0, i, 0))


def _const(arr):
    nd = arr.ndim
    return pl.BlockSpec(arr.shape, lambda *_: (0,) * nd, pipeline_mode=pl.Buffered(1))


def _sds(shape, dtype):
    return jax.ShapeDtypeStruct(shape, dtype)


PEERS = tuple((dx, dy, dc) for dx in (0, 1) for dy in (0, 1) for dc in (0, 1) if (dx, dy, dc) != (0, 0, 0))


def _here():
    x, y, c = lax.axis_index("x"), lax.axis_index("y"), lax.axis_index("c")
    return x, y, c, 4 * x + 2 * y + c


def _xchg_start(scatter, srcs, dsts, send, recv, local):
    x, y, c, me = _here()
    for e, sc in enumerate(scatter):
        src, dst = srcs[e], dsts[e]
        pltpu.make_async_copy(src.at[me] if sc else src, dst.at[me], local.at[e]).start()
        for dx, dy, dc in PEERS:
            px, py, pc = (1 - x if dx else x), (1 - y if dy else y), (1 - c if dc else c)
            pltpu.make_async_remote_copy(
                src_ref=src.at[4 * px + 2 * py + pc] if sc else src, dst_ref=dst.at[me],
                send_sem=send.at[e], recv_sem=recv.at[e], device_id=(px, py, pc), device_id_type=MESH).start()


def _xchg_wait(scatter, srcs, dsts, send, recv, local):
    x, y, c, me = _here()
    for e, sc in enumerate(scatter):
        src, dst = srcs[e], dsts[e]
        pltpu.make_async_copy(src.at[me] if sc else src, dst.at[me], local.at[e]).wait()
        span = dst.at[pl.ds(0, N_DEV - 1)]
        both = pltpu.make_async_remote_copy(src_ref=span, dst_ref=span, send_sem=send.at[e], recv_sem=recv.at[e],
                                            device_id=(x, y, c), device_id_type=MESH)
        both.wait_send()
        both.wait_recv()


def _call(body, name, grid, ins, in_specs, outs, out_specs, scratch=(), xch=()):
    n_in, n_out, ne = len(ins), len(outs), len(xch)
    scatter = [sc for _, sc in xch]
    x_outs = [_sds((N_DEV,) + (a.shape[1:] if sc else a.shape), a.dtype) for a, sc in xch]
    sems = [pltpu.SemaphoreType.DMA((ne,))] * 3 if ne else []

    def wrapped(*refs):
        in_refs, x_src = refs[:n_in], refs[n_in:n_in + ne]
        out_refs = refs[n_in + ne:n_in + ne + n_out]
        x_dst = refs[n_in + ne + n_out:n_in + 2 * ne + n_out]
        rest = refs[n_in + 2 * ne + n_out:]
        if ne:
            x_sems, rest = rest[len(rest) - 3:], rest[:len(rest) - 3]
            first = functools.reduce(jnp.logical_and, [pl.program_id(d) == 0 for d in range(len(grid))])
            last = functools.reduce(jnp.logical_and, [pl.program_id(d) == grid[d] - 1 for d in range(len(grid))])

            @pl.when(first)
            def _():
                _xchg_start(scatter, x_src, x_dst, *x_sems)

        body(*in_refs, *out_refs, *rest)
        if ne:
            @pl.when(last)
            def _():
                _xchg_wait(scatter, x_src, x_dst, *x_sems)

    return pl.pallas_call(
        wrapped,
        name=name,
        grid=grid,
        in_specs=list(in_specs) + [ANY] * ne,
        out_specs=list(out_specs) + [ANY] * ne,
        out_shape=list(outs) + x_outs,
        scratch_shapes=list(scratch) + sems,
        compiler_params=_params(("arbitrary",) * len(grid)),
    )(*ins, *[a for a, _ in xch])


def _row_call(body, name, n_rows, tm, row_ins, const_ins, row_outs, acc_outs=(), xch=(), col_outs=(), scratch=()):
    outs = [_sds(s, d) for s, d in list(row_outs) + list(col_outs) + list(acc_outs)]
    n_row, n_col = len(row_outs), len(col_outs)
    out_specs = [_rows(o, tm) for o in outs[:n_row]] + [
        pl.BlockSpec((o.shape[0], tm), lambda i: (0, i)) for o in outs[n_row:n_row + n_col]] + [
        pl.BlockSpec(o.shape, lambda i, nd=len(o.shape): (0,) * nd) for o in outs[n_row + n_col:]]
    in_specs = [_rows(a, tm) for a in row_ins] + [_const(a) for a in const_ins]
    return _call(body, name, (n_rows // tm,), list(row_ins) + list(const_ins), in_specs, outs, out_specs,
                 scratch=scratch, xch=xch)


def _dot(a, b):
    return jnp.dot(a, b, preferred_element_type=F32)


def _dot_nt(a, b):
    return lax.dot_general(a, b, (((1,), (1,)), ((), ())), preferred_element_type=F32)


def _dot_tn(a, b):
    return lax.dot_general(a, b, (((0,), (0,)), ((), ())), preferred_element_type=F32)


def _rms(x, g, n):
    inv = lax.rsqrt(jnp.sum(x * x, -1, keepdims=True) * (1.0 / n) + EPS)
    return x * inv * g, inv


def _rms_bwd(dy, x, g, inv, n):
    xh = x * inv
    dxh = dy * g
    dx = inv * (dxh - xh * (jnp.sum(dxh * xh, -1, keepdims=True) * (1.0 / n)))
    return dx, dy * xh


def _sigmoid(x):
    return 1.0 / (1.0 + jnp.exp(-x))


_GELU_C = math.sqrt(2.0 / math.pi)


def _gelu(y):
    th = jnp.tanh(_GELU_C * (y + 0.044715 * (y * y * y)))
    return 0.5 * y * (1.0 + th), th


def _gelu_grad(y, th):
    return 0.5 * (1.0 + th) + 0.5 * y * (1.0 - th * th) * (_GELU_C * (1.0 + 3.0 * 0.044715 * (y * y)))


def _acc(ref, val):
    @pl.when(pl.program_id(0) == 0)
    def _():
        ref[...] = jnp.zeros_like(ref)

    ref[...] += val


def _tile(n, limit):
    if n <= limit:
        return n
    return max(t for t in range(128, limit + 1, 128) if n % t == 0)


def _lhs(a, turned, tm, tk):
    m, k_dim = a.shape if turned else a.shape[::-1]
    tm, tk = _tile(m, tm), _tile(k_dim, tk)
    if turned:
        return m, k_dim, tm, tk, pl.BlockSpec((tm, tk), lambda i, k: (i, k)), _dot
    return m, k_dim, tm, tk, pl.BlockSpec((tk, tm), lambda i, k: (k, i)), _dot_tn


def _matmul_tn_shards(a, b, name, by_col, tm=512, tk=512, turned=False):
    m, k_dim, tm, tk, a_spec, dot = _lhs(a, turned, tm, tk)
    n = b.shape[1]
    nk = k_dim // tk
    if by_col:
        r, c = m, n // N_DEV
        out_spec = pl.BlockSpec((N_DEV, tm, c), lambda i, k: (0, i, 0))
    else:
        r, c = m // N_DEV, n
        per = tm // r
        out_spec = pl.BlockSpec((per, r, c), lambda i, k: (i, 0, 0))

    def body(a_ref, b_ref, o_ref, acc_ref):
        k = pl.program_id(1)

        @pl.when(k == 0)
        def _():
            acc_ref[...] = jnp.zeros_like(acc_ref)

        acc_ref[...] += dot(a_ref[...].astype(BF16), b_ref[...].astype(BF16))

        @pl.when(k == nk - 1)
        def _():
            if by_col:
                for j in range(N_DEV):
                    o_ref[j] = acc_ref[:, j * c:(j + 1) * c].astype(BF16)
            else:
                for s in range(per):
                    o_ref[s] = acc_ref[s * r:(s + 1) * r, :].astype(BF16)

    return pl.pallas_call(
        body,
        name=name,
        grid=(m // tm, nk),
        in_specs=[a_spec, pl.BlockSpec((tk, n), lambda i, k: (k, 0))],
        out_specs=out_spec,
        out_shape=_sds((N_DEV, r, c), BF16),
        scratch_shapes=[pltpu.VMEM((tm, n), F32)],
        compiler_params=_params(("parallel", "arbitrary")),
    )(a, b)


def _rope_tables(pos_col):
    n = pos_col.shape[0]
    half = QK_ROPE // 2
    inv_freq = (ROPE_THETA ** (-np.arange(half, dtype=np.float32) / half)).astype(np.float32)
    freq_row = jnp.asarray(np.concatenate([inv_freq, inv_freq, np.zeros(64, np.float32)])[None, :])

    def body(p_ref, f_ref, c_ref, s_ref):
        ang = p_ref[...].astype(F32) * f_ref[...]
        c_ref[...] = jnp.cos(ang)
        s_ref[...] = jnp.sin(ang)

    return _row_call(body, "rope_tables", n, min(n, 1024), [pos_col], [freq_row], [((n, 128), F32)] * 2)


def _rope_rot(v):
    lane = lax.broadcasted_iota(jnp.int32, v.shape, 1)
    return jnp.where(lane < 32, -pltpu.roll(v, 96, 1), jnp.where(lane < 64, pltpu.roll(v, 32, 1), 0.0))


def _rope_rot_t(v):
    lane = lax.broadcasted_iota(jnp.int32, v.shape, 1)
    return jnp.where(lane < 32, pltpu.roll(v, 96, 1), jnp.where(lane < 64, -pltpu.roll(v, 32, 1), 0.0))


def _in_proj(x, norm_mix, w_in_pad, xch):
    n = x.shape[0]

    def body(x_ref, g_ref, w_ref, u_ref, ql_ref, kvl_ref, gs_ref, gm_ref, xnt_ref):
        xn, _ = _rms(x_ref[...], g_ref[...], D_MODEL)
        xb = xn.astype(BF16)
        xnt_ref[...] = xn.T.astype(BF16)
        for ref, (a, b) in zip((u_ref, ql_ref, kvl_ref, gs_ref, gm_ref), IN_SEGS):
            ref[...] = _dot(xb, w_ref[:, a:b])

    outs = [((n, b - a), F32) for a, b in IN_SEGS]
    return _row_call(body, "in_proj", n, MM_T, [x], [norm_mix, w_in_pad], outs, xch=xch,
                     col_outs=[((D_MODEL, n), BF16)])


def _ssm_prep_fn(a_re, a_im, log_dt, b_re_x, b_im_x):
    dt = jnp.exp(log_dt)
    mag = jnp.exp(a_re * dt)
    lr = mag * jnp.cos(a_im * dt)
    li = mag * jnp.sin(a_im * dt)
    den = a_re * a_re + a_im * a_im
    fr = ((lr - 1.0) * a_re + li * a_im) / den
    fi = (li * a_re - (lr - 1.0) * a_im) / den
    return lr, li, fr * b_re_x - fi * b_im_x, fr * b_im_x + fi * b_re_x


def _dot_exact(a, b, dims):
    return lax.dot_general(a, b, (dims, ((), ())), precision=lax.Precision.HIGHEST, preferred_element_type=F32)


def _lane_repeat(width, n):
    src = lax.broadcasted_iota(jnp.int32, (width, n), 0)
    dst = lax.broadcasted_iota(jnp.int32, (width, n), 1)
    return (dst % width == src).astype(F32)


def _same_group(rows, rows_per_group, cols, cols_per_group):
    row = lax.broadcasted_iota(jnp.int32, (rows, cols), 0)
    col = lax.broadcasted_iota(jnp.int32, (rows, cols), 1)
    return (row // rows_per_group) == (col // cols_per_group)


def _expand_b(bt):
    tiled = _dot_exact(bt, _lane_repeat(SSM_STATE, N_STATE), ((1,), (0,)))
    return jnp.where(_same_group(SSM_WIDTH, SSM_GROUP_CH, N_STATE, SSM_STATE), tiled, 0.0)


def _collect_b(m):
    masked = jnp.where(_same_group(SSM_WIDTH, SSM_GROUP_CH, N_STATE, SSM_STATE), m, 0.0)
    return _dot_exact(masked, _lane_repeat(SSM_STATE, N_STATE), ((1,), (1,)))


def _ssm_prep(a_re, a_im, log_dt, bt_re, bt_im, c2_re, c2_im):
    def body(ar, ai, ld, br, bi, cr, ci, lam_ref, bblk_ref, cblk_ref):
        lr, li, bbr, bbi = _ssm_prep_fn(ar[...], ai[...], ld[...], _expand_b(br[...]), _expand_b(bi[...]))
        lam_ref[0:1, :] = lr
        lam_ref[1:2, :] = li
        bblk_ref[:, 0:N_STATE] = bbr.astype(BF16)
        bblk_ref[:, N_STATE:] = bbi.astype(BF16)
        rep = _lane_repeat(SSM_GROUP_CH, SSM_WIDTH)
        own = _same_group(N_STATE, SSM_STATE, SSM_WIDTH, SSM_GROUP_CH)
        cblk_ref[0:N_STATE, :] = jnp.where(own, _dot_exact(cr[...], rep, ((1,), (0,))), 0.0).astype(BF16)
        cblk_ref[N_STATE:, :] = jnp.where(own, -_dot_exact(ci[...], rep, ((1,), (0,))), 0.0).astype(BF16)

    return pl.pallas_call(
        body,
        name="ssm_prep",
        out_shape=[_sds((2, N_STATE), F32), _sds((SSM_WIDTH, 2 * N_STATE), BF16),
                   _sds((2 * N_STATE, SSM_WIDTH), BF16)],
        compiler_params=_params(),
    )(a_re, a_im, log_dt, bt_re, bt_im, c2_re, c2_im)


def _ssm_prep_bwd(a_re, a_im, log_dt, bt_re, bt_im, dlam, dbblk, dcblk_t):
    def body(ar, ai, ld, br, bi, dl, db, dc, dar, dai, dld, dbr, dbi, dcr, dci):
        _, vjp = jax.vjp(_ssm_prep_fn, ar[...], ai[...], ld[...], _expand_b(br[...]), _expand_b(bi[...]))
        g = vjp((dl[0:1, :], dl[1:2, :], db[:, 0:N_STATE], db[:, N_STATE:]))
        dar[...] = g[0]
        dai[...] = g[1]
        grp = lax.broadcasted_iota(jnp.int32, (SSM_GROUPS, N_STATE), 0)
        lane = lax.broadcasted_iota(jnp.int32, (SSM_GROUPS, N_STATE), 1)
        sel = (lane // SSM_STATE) == grp
        dld[...] = jnp.sum(jnp.where(sel, jnp.broadcast_to(g[2], (SSM_GROUPS, N_STATE)), 0.0), axis=1, keepdims=True)
        dbr[...] = _collect_b(g[3])
        dbi[...] = _collect_b(g[4])
        dcr[...] = _collect_b(dc[:, 0:N_STATE])
        dci[...] = -_collect_b(dc[:, N_STATE:])

    small = _sds((SSM_WIDTH, SSM_STATE), F32)
    return pl.pallas_call(
        body,
        name="ssm_prep_bwd",
        out_shape=[_sds((1, N_STATE), F32), _sds((1, N_STATE), F32), _sds((SSM_GROUPS, 1), F32), small, small, small, small],
        compiler_params=_params(),
    )(a_re, a_im, log_dt, bt_re, bt_im, dlam, dbblk, dcblk_t)


def _perm_matrix(t):
    run = t // SUBCHUNKS
    p = np.zeros((t, t), np.float32)
    r = np.arange(t)
    p[r, (r % SUBCHUNKS) * run + r // SUBCHUNKS] = 1.0
    return jnp.asarray(p, dtype=BF16)


def _unpermute(p, a):
    hi = a.astype(BF16)
    r1 = a - hi.astype(F32)
    mid = r1.astype(BF16)
    lo = (r1 - mid.astype(F32)).astype(BF16)
    return _dot_tn(p, hi) + _dot_tn(p, mid) + _dot_tn(p, lo)


def _power_table(lam_ref, pw_ref, n):
    lr, li = lam_ref[0:1, :], lam_ref[1:2, :]
    pw_ref[0:1, 0:N_STATE] = lr
    pw_ref[0:1, N_STATE:] = li

    def step(i, carry):
        pr, pi = carry
        pr, pi = pr * lr - pi * li, pr * li + pi * lr
        pw_ref[pl.ds(i, 1), 0:N_STATE] = pr
        pw_ref[pl.ds(i, 1), N_STATE:] = pi
        return pr, pi

    lax.fori_loop(1, n, step, (lr, li))


def _col_groups():
    return [(pl.ds(c, SCAN_CG), pl.ds(N_STATE + c, SCAN_CG)) for c in range(0, N_STATE, SCAN_CG)]


def _run_scan(buf, lam_ref, t, reverse):
    nblk = t // 8
    for re, im in _col_groups():
        lr = jnp.broadcast_to(lam_ref[0:1, re], (8, SCAN_CG))
        li = jnp.broadcast_to(lam_ref[1:2, re], (8, SCAN_CG))
        if reverse:
            li = -li
        first = pl.ds((nblk - 1) * 8 if reverse else 0, 8)

        def step(k, carry, re=re, im=im, lr=lr, li=li):
            pr, pi = carry
            i = (nblk - 2 - k) if reverse else (k + 1)
            r = pl.ds(pl.multiple_of(i * 8, 8), 8)
            xr = buf[r, re] + lr * pr - li * pi
            xi = buf[r, im] + lr * pi + li * pr
            buf[r, re] = xr
            buf[r, im] = xi
            return xr, xi

        lax.fori_loop(0, nblk - 1, step, (buf[first, re], buf[first, im]))


def _run_carries(buf, pw_ref, carry_ref, s_ref, t, reverse):
    nblk = t // 8
    run = t // SUBCHUNKS
    edge = buf[pl.ds(0 if reverse else (nblk - 1) * 8, 8), :]
    pr, pi = pw_ref[run - 1:run, 0:N_STATE], pw_ref[run - 1:run, N_STATE:]
    if reverse:
        pi = -pi
    sr, si = carry_ref[0:1, 0:N_STATE], carry_ref[0:1, N_STATE:]
    for s in (range(SUBCHUNKS - 1, -1, -1) if reverse else range(SUBCHUNKS)):
        s_ref[s:s + 1, 0:N_STATE] = sr
        s_ref[s:s + 1, N_STATE:] = si
        er, ei = edge[s:s + 1, 0:N_STATE], edge[s:s + 1, N_STATE:]
        sr, si = er + pr * sr - pi * si, ei + pr * si + pi * sr
    carry_ref[:, 0:N_STATE] = jnp.broadcast_to(sr, (8, N_STATE))
    carry_ref[:, N_STATE:] = jnp.broadcast_to(si, (8, N_STATE))


def _run_fix(buf, pw_ref, s_ref, t, reverse):
    nblk = t // 8
    for re, im in _col_groups():
        sr, si = s_ref[:, re], s_ref[:, im]

        def step(i, carry, re=re, im=im, sr=sr, si=si):
            r = pl.ds(pl.multiple_of(i * 8, 8), 8)
            row = pl.ds((nblk - 1 - i) if reverse else i, 1)
            pr, pi = pw_ref[row, re], pw_ref[row, im]
            if reverse:
                pi = -pi
            buf[r, re] += pr * sr - pi * si
            buf[r, im] += pr * si + pi * sr
            return carry

        lax.fori_loop(0, nblk, step, 0)


STATE_BLOCKS = 2 * N_STATE // LANES
CH_BLOCKS = SSM_WIDTH // LANES


def _state_block(b):
    pair = b % (N_STATE // LANES)
    k = (pair * 2 * SSM_GROUP_CH) // LANES
    return slice(b * LANES, (b + 1) * LANES), slice(k * LANES, (k + 1) * LANES)


def _channel_block(c):
    w = N_STATE // CH_BLOCKS
    return slice(c * LANES, (c + 1) * LANES), slice(c * w, (c + 1) * w), slice(N_STATE + c * w, N_STATE + (c + 1) * w)


def _to_states(vb, w_ref, buf, nt):
    for b in range(STATE_BLOCKS):
        lanes, ch = _state_block(b)
        buf[:, lanes] = _dot_nt(vb[:, ch], w_ref[lanes, ch]) if nt else _dot(vb[:, ch], w_ref[ch, lanes])


def _to_channels(buf, w_ref, nt):
    outs = []
    for c in range(CH_BLOCKS):
        ch, re, im = _channel_block(c)
        xr, xi = buf[:, re].astype(BF16), buf[:, im].astype(BF16)
        if nt:
            outs.append(_dot_nt(xr, w_ref[ch, re]) + _dot_nt(xi, w_ref[ch, im]))
        else:
            outs.append(_dot(xr, w_ref[re, ch]) + _dot(xi, w_ref[im, ch]))
    return jnp.concatenate(outs, axis=-1)


def _ssm_fwd(u, bblk, cblk, lam, d_row, w_glu, b_glu, w_o_ssm, xch):
    n = u.shape[0]
    t = min(SCAN_T, n)
    perm = _perm_matrix(t)

    def body(u_ref, p_ref, bblk_ref, cblk_ref, lam_ref, d_ref, wg_ref, bg_ref, wo_ref, y_ref, ys_ref, st_ref,
             buf, pw_ref, carry_ref, s_ref):
        @pl.when(pl.program_id(0) == 0)
        def _():
            carry_ref[...] = jnp.zeros_like(carry_ref)
            _power_table(lam_ref, pw_ref, t // SUBCHUNKS)

        st_ref[0] = carry_ref[...]
        u_t = u_ref[...]
        p = p_ref[...]
        ub = _dot(p, u_t.astype(BF16)).astype(BF16)
        _to_states(ub, bblk_ref, buf, False)
        _run_scan(buf, lam_ref, t, False)
        _run_carries(buf, pw_ref, carry_ref, s_ref, t, False)
        _run_fix(buf, pw_ref, s_ref, t, False)
        y = d_ref[...] * u_t + _unpermute(p, _to_channels(buf, cblk_ref, False))
        y_ref[...] = y
        z, _ = _gelu(y)
        s = _sigmoid(_dot(z.astype(BF16), wg_ref[...]) + bg_ref[...])
        zgb = (z * s).astype(BF16)
        for j in range(N_DEV):
            ys_ref[:, j * OUT_SHARD:(j + 1) * OUT_SHARD] = _dot(zgb, wo_ref[j])

    consts = [perm, bblk, cblk, lam, d_row, w_glu, b_glu, w_o_ssm]
    return _call(
        body, "ssm_fwd", (n // t,), [u] + consts, [_rows(u, t)] + [_const(a) for a in consts],
        [_sds((n, SSM_WIDTH), F32), _sds((n, D_MODEL), F32), _sds((n // t, 8, 2 * N_STATE), F32)],
        [pl.BlockSpec((t, SSM_WIDTH), lambda i: (i, 0)), pl.BlockSpec((t, D_MODEL), lambda i: (i, 0)),
         pl.BlockSpec((1, 8, 2 * N_STATE), lambda i: (i, 0, 0))],
        scratch=[pltpu.VMEM((t, 2 * N_STATE), F32), pltpu.VMEM((t // SUBCHUNKS, 2 * N_STATE), F32),
                 pltpu.VMEM((8, 2 * N_STATE), F32), pltpu.VMEM((8, 2 * N_STATE), F32)],
        xch=xch)


def _head_norm_rope(slab, gain, cos_t, sin_t):
    xn, inv = _rms(slab, gain, QK_HEAD)
    lo, hi = xn[:, 0:128], xn[:, 128:256]
    return jnp.concatenate([lo, hi * cos_t + _rope_rot(hi) * sin_t], axis=-1), inv


def _head_norm_rope_bwd(g, slab, gain, inv, cos_t, sin_t):
    g_lo, g_hi = g[:, 0:128], g[:, 128:256]
    g_n = jnp.concatenate([g_lo, g_hi * cos_t + _rope_rot_t(g_hi * sin_t)], axis=-1)
    return _rms_bwd(g_n, slab, gain, inv, QK_HEAD)


def _qkv_prep(ql, kvl, q_a_norm, kv_a_norm, wq, wkv, gq, gk, cos_t, sin_t):
    n = ql.shape[0]
    tm = ROW_T

    def body(ql_ref, kvl_ref, cos_ref, sin_ref, qa_ref, ka_ref, wq_ref, wkv_ref, gq_ref, gk_ref,
             q_ref, k_ref, v_ref, kt_ref, vt_ref):
        cos_t, sin_t = cos_ref[...], sin_ref[...]
        qa, _ = _rms(ql_ref[...], qa_ref[...], Q_LORA)
        qab = qa.astype(BF16)
        kvl_t = kvl_ref[...]
        ca, _ = _rms(kvl_t[:, 0:KV_LORA], ka_ref[...], KV_LORA)
        cab = ca.astype(BF16)
        kpe = kvl_t[:, KV_LORA:KV_LAT_PAD]
        q_pre = _dot(qab, wq_ref[...])
        kv_pre = _dot(cab, wkv_ref[...])
        for h in range(N_HEADS):
            qh, _ = _head_norm_rope(q_pre[:, h * QK_PAD:(h + 1) * QK_PAD], gq_ref[...], cos_t, sin_t)
            q_ref[h] = (qh * ATT_SCALE).astype(BF16)
            kv_h = kv_pre[:, h * QK_PAD:(h + 1) * QK_PAD]
            kh, _ = _head_norm_rope(jnp.concatenate([kv_h[:, 0:QK_NOPE], kpe], axis=-1), gk_ref[...], cos_t, sin_t)
            k_ref[h] = kh.astype(BF16)
            kt_ref[h] = kh.T.astype(BF16)
            vh = kv_h[:, QK_NOPE:]
            v_ref[h] = vh.astype(BF16)
            vt_ref[h] = vh.T.astype(BF16)

    row_ins, consts = [ql, kvl, cos_t, sin_t], [q_a_norm, kv_a_norm, wq, wkv, gq, gk]
    outs = [_sds((N_HEADS, n, QK_PAD), BF16), _sds((N_HEADS, n, QK_PAD), BF16), _sds((N_HEADS, n, V_HEAD), BF16),
            _sds((N_HEADS, QK_PAD, n), BF16), _sds((N_HEADS, V_HEAD, n), BF16)]
    out_specs = [_rows(o, tm) for o in outs[:3]] + [
        pl.BlockSpec((N_HEADS, QK_PAD, tm), lambda i: (0, 0, i)), pl.BlockSpec((N_HEADS, V_HEAD, tm), lambda i: (0, 0, i))]
    return _call(body, "qkv_prep", (n // tm,), row_ins + consts,
                 [_rows(a, tm) for a in row_ins] + [_const(a) for a in consts], outs, out_specs)


def _causal_mask_t(st, t):
    key = lax.broadcasted_iota(jnp.int32, (t, t), 0)
    qry = lax.broadcasted_iota(jnp.int32, (t, t), 1)
    return jnp.where(key <= qry, st, -jnp.inf)


def _attn_fwd(q, k, vt, xch):
    n = q.shape[1]
    t = min(ATT_T, n)

    def body(q_ref, k_ref, vt_ref, o_ref, lse_ref, ot_ref):
        i = pl.program_id(1)
        qt = q_ref[0]

        def kv_tile(j, carry, diag):
            m, l, acc = carry
            ts = t // ATT_SUB
            sts = []
            for a in range(ATT_SUB):
                r0 = pl.multiple_of(j * t + a * ts, ts)
                st = _dot_nt(k_ref[0, pl.ds(r0, ts), :], qt)
                if diag:
                    key = lax.broadcasted_iota(jnp.int32, (ts, t), 0) + a * ts
                    qry = lax.broadcasted_iota(jnp.int32, (ts, t), 1)
                    st = jnp.where(key <= qry, st, -jnp.inf)
                sts.append(st)
            for a, st in enumerate(sts):
                r0 = pl.multiple_of(j * t + a * ts, ts)
                m_new = jnp.maximum(m, jnp.max(st, 0, keepdims=True))
                alpha = jnp.exp(m - m_new)
                pt = jnp.exp(st - m_new)
                l = alpha * l + jnp.sum(pt, 0, keepdims=True)
                acc = alpha * acc + _dot(vt_ref[0, :, pl.ds(r0, ts)], pt.astype(BF16))
                m = m_new
            return m, l, acc

        init = (jnp.full((1, t), -jnp.inf, F32), jnp.zeros((1, t), F32), jnp.zeros((V_HEAD, t), F32))
        carry = lax.fori_loop(0, i, functools.partial(kv_tile, diag=False), init)
        m, l, acc = kv_tile(i, carry, True)
        out_t = acc / l
        o_ref[...] = out_t.T
        ot_ref[...] = out_t.astype(BF16)
        lse_ref[0] = m + jnp.log(l)

    return _call(
        body, "attn_fwd", (N_HEADS, n // t), [q, k, vt],
        [pl.BlockSpec((1, t, QK_PAD), lambda h, i: (h, i, 0)), pl.BlockSpec((1, n, QK_PAD), lambda h, i: (h, 0, 0)),
         pl.BlockSpec((1, V_HEAD, n), lambda h, i: (h, 0, 0))],
        [_sds((n, N_HEADS * V_HEAD), F32), _sds((N_HEADS, 1, n), F32), _sds((N_HEADS * V_HEAD, n), BF16)],
        [pl.BlockSpec((t, V_HEAD), lambda h, i: (i, h)), pl.BlockSpec((1, 1, t), lambda h, i: (h, 0, i)),
         pl.BlockSpec((V_HEAD, t), lambda h, i: (h, i))],
        xch=xch)


def _merge(attn, gs, gm, y_ssm, x, w_o_mla, w_out):
    n = x.shape[0]

    def body(at_ref, gs_ref, gm_ref, ys_ref, x_ref, wo_ref, wout_ref, h_ref, ym_ref, mxt_ref):
        y_mla = _dot(at_ref[...].astype(BF16), wo_ref[...])
        ym_ref[...] = y_mla
        mixed = _sigmoid(gs_ref[...]) * ys_ref[...] + _sigmoid(gm_ref[...]) * y_mla
        mxt_ref[...] = mixed.T.astype(BF16)
        h_ref[...] = x_ref[...] + _dot(mixed.astype(BF16), wout_ref[...])

    outs = [((n, D_MODEL), F32), ((n, D_MODEL), F32)]
    return _row_call(body, "merge", n, MM_T, [attn, gs, gm, y_ssm, x], [w_o_mla, w_out], outs,
                     col_outs=[((D_MODEL, n), BF16)])


def _mlp_fwd_loss(h, target, norm_mlp, w_up, w_down):
    n = h.shape[0]

    def body(h_ref, t_ref, g_ref, wu_ref, wd_ref, hn_ref, do_ref, hnt_ref, loss_ref):
        h_t = h_ref[...]
        hn, _ = _rms(h_t, g_ref[...], D_MODEL)
        hb = hn.astype(BF16)
        hn_ref[...] = hb
        hnt_ref[...] = hn.T.astype(BF16)
        out = h_t
        for j in range(N_DEV):
            a = jnp.maximum(_dot(hb, wu_ref[j]), 0.0)
            out += _dot((a * a).astype(BF16), wd_ref[j])
        err = out - t_ref[...]
        do_ref[...] = err * (1.0 / D_MODEL)
        _acc(loss_ref, jnp.broadcast_to(jnp.sum(err * err) * (0.5 / D_MODEL), loss_ref.shape))

    outs = [((n, D_MODEL), BF16), ((n, D_MODEL), F32)]
    return _row_call(body, "mlp_fwd_loss", n, MM_T, [h, target], [norm_mlp, w_up, w_down], outs, [((8, 128), F32)],
                     col_outs=[((D_MODEL, n), BF16)])


def _mlp_bwd(dout, hn, h, norm_mlp, w_up, w_down):
    n = h.shape[0]

    def body(do_ref, hn_ref, h_ref, g_ref, wu_ref, wd_ref, da_ref, dh_ref, dob_ref, hidt_ref, dg_ref):
        dout_t = do_ref[...]
        doutb = dout_t.astype(BF16)
        dob_ref[...] = doutb
        hb = hn_ref[...]
        dhn = jnp.zeros_like(dout_t)
        for j in range(N_DEV):
            cols = slice(j * FF_SHARD, (j + 1) * FF_SHARD)
            a = jnp.maximum(_dot(hb, wu_ref[j]), 0.0)
            hidt_ref[cols, :] = (a * a).T.astype(BF16)
            da = (_dot_nt(doutb, wd_ref[j]) * (2.0 * a)).astype(BF16)
            da_ref[:, cols] = da
            dhn += _dot_nt(da, wu_ref[j])
        h_t = h_ref[...]
        inv = lax.rsqrt(jnp.sum(h_t * h_t, -1, keepdims=True) * (1.0 / D_MODEL) + EPS)
        dx, dg = _rms_bwd(dhn, h_t, g_ref[...], inv, D_MODEL)
        dh_ref[...] = dout_t + dx
        _acc(dg_ref, jnp.sum(dg, 0, keepdims=True))

    outs = [((n, D_FF), BF16), ((n, D_MODEL), F32), ((n, D_MODEL), BF16)]
    return _row_call(body, "mlp_bwd", n, MM_T, [dout, hn, h], [norm_mlp, w_up, w_down], outs, [((1, D_MODEL), F32)],
                     col_outs=[((D_FF, n), BF16)])


def _merge_bwd(dh, gs, gm, y_ssm, y_mla, w_out, w_o_mla):
    n = dh.shape[0]

    def body(dh_ref, gs_ref, gm_ref, ys_ref, ym_ref, wout_ref, wo_ref, dgs_ref, dgm_ref, dys_ref, dym_ref, dat_ref):
        dmix = _dot_nt(dh_ref[...].astype(BF16), wout_ref[...])
        sgs, sgm = _sigmoid(gs_ref[...]), _sigmoid(gm_ref[...])
        dgs_ref[...] = (dmix * ys_ref[...] * sgs * (1.0 - sgs)).astype(BF16)
        dgm_ref[...] = (dmix * ym_ref[...] * sgm * (1.0 - sgm)).astype(BF16)
        dys_ref[...] = (dmix * sgs).astype(BF16)
        dym = (dmix * sgm).astype(BF16)
        dym_ref[...] = dym
        dat_ref[...] = _dot_nt(dym, wo_ref[...])

    outs = [((n, D_MODEL), BF16)] * 4 + [((n, D_MODEL), F32)]
    return _row_call(body, "merge_bwd", n, MM_T, [dh, gs, gm, y_ssm, y_mla], [w_out, w_o_mla], outs)


def _attn_bwd(q, k, kt, v, out, lse, dout, xch):
    n = q.shape[1]
    t = min(ATT_T, n)
    nt = n // t

    def body(q_ref, k_ref, kt_ref, v_ref, o_ref, lse_ref, do_ref, dq_ref, dk_ref, dv_ref, delta_ref, dqt_ref):
        j = pl.program_id(1)

        @pl.when(j == 0)
        def _():
            dqt_ref[...] = jnp.zeros_like(dqt_ref)
            prod = do_ref[...] * o_ref[...]
            delta_ref[...] = lax.dot_general(jnp.ones((8, V_HEAD), F32), prod, (((1,), (1,)), ((), ())),
                                             precision=lax.Precision.HIGHEST, preferred_element_type=F32)

        k_t = k_ref[0]
        kt_t = kt_ref[0]
        v_t = v_ref[0]

        def q_tile(i, carry, diag):
            dk, dv = carry
            r0 = pl.multiple_of(i * t, t)
            rows = pl.ds(r0, t)
            qt = q_ref[0, rows, :]
            st = _dot_nt(k_t, qt)
            if diag:
                st = _causal_mask_t(st, t)
            pt = jnp.exp(st - lse_ref[0, :, rows])
            dob = do_ref[rows, :].astype(BF16)
            dv = dv + _dot(pt.astype(BF16), dob)
            dst = (pt * (_dot_nt(v_t, dob) - delta_ref[0:1, rows])).astype(BF16)
            dk = dk + _dot(dst, qt)
            dqt_ref[:, rows] += _dot(kt_t, dst)
            return dk, dv

        carry = q_tile(j, (jnp.zeros((t, QK_PAD), F32), jnp.zeros((t, V_HEAD), F32)), True)
        dk, dv = lax.fori_loop(j + 1, nt, functools.partial(q_tile, diag=False), carry)
        dk_ref[0] = dk
        dv_ref[0] = dv

        @pl.when(j == nt - 1)
        def _():
            for c in range(0, n, t):
                dq_ref[0, c:c + t, :] = dqt_ref[:, c:c + t].T

    return _call(
        body, "attn_bwd", (N_HEADS, nt), [q, k, kt, v, out, lse, dout],
        [pl.BlockSpec((1, n, QK_PAD), lambda h, j: (h, 0, 0)), pl.BlockSpec((1, t, QK_PAD), lambda h, j: (h, j, 0)),
         pl.BlockSpec((1, QK_PAD, t), lambda h, j: (h, 0, j)), pl.BlockSpec((1, t, V_HEAD), lambda h, j: (h, j, 0)),
         pl.BlockSpec((n, V_HEAD), lambda h, j: (0, h)), pl.BlockSpec((1, 1, n), lambda h, j: (h, 0, 0)),
         pl.BlockSpec((n, V_HEAD), lambda h, j: (0, h))],
        [_sds((N_HEADS, n, QK_PAD), F32), _sds((N_HEADS, n, QK_PAD), F32), _sds((N_HEADS, n, V_HEAD), F32)],
        [pl.BlockSpec((1, n, QK_PAD), lambda h, j: (h, 0, 0)), pl.BlockSpec((1, t, QK_PAD), lambda h, j: (h, j, 0)),
         pl.BlockSpec((1, t, V_HEAD), lambda h, j: (h, j, 0))],
        scratch=[pltpu.VMEM((8, n), F32), pltpu.VMEM((QK_PAD, n), F32)],
        xch=xch)


def _qkv_prep_bwd(ql, kvl, dq, dk, dv, q_a_norm, kv_a_norm, wq, wkv, gq, gk, cos_t, sin_t, xch):
    n = ql.shape[0]

    def body(ql_ref, kvl_ref, cos_ref, sin_ref, dq_ref, dk_ref, dv_ref, qa_ref, ka_ref, wq_ref, wkv_ref, gq_ref, gk_ref,
             dql_ref, dkvl_ref, dqa_ref, dka_ref, dgq_ref, dgk_ref, dwq_ref, dwkv_ref, dqp_ref, dkvp_ref):
        cos_t, sin_t = cos_ref[...], sin_ref[...]
        ql_t = ql_ref[...]
        qa, inv_qa = _rms(ql_t, qa_ref[...], Q_LORA)
        qab = qa.astype(BF16)
        kvl_t = kvl_ref[...]
        ckv = kvl_t[:, 0:KV_LORA]
        ca, inv_ca = _rms(ckv, ka_ref[...], KV_LORA)
        cab = ca.astype(BF16)
        kpe = kvl_t[:, KV_LORA:KV_LAT_PAD]
        dgq = jnp.zeros((1, QK_PAD), F32)
        dgk = jnp.zeros((1, QK_PAD), F32)
        dkpe = jnp.zeros_like(kpe)
        q_pre = _dot(qab, wq_ref[...])
        kv_pre = _dot(cab, wkv_ref[...])
        for h in range(N_HEADS):
            head = slice(h * QK_PAD, (h + 1) * QK_PAD)
            q_slab = q_pre[:, head]
            inv = lax.rsqrt(jnp.sum(q_slab * q_slab, -1, keepdims=True) * (1.0 / QK_HEAD) + EPS)
            d_slab, dg = _head_norm_rope_bwd(dq_ref[h] * ATT_SCALE, q_slab, gq_ref[...], inv, cos_t, sin_t)
            dqp_ref[:, head] = d_slab.astype(BF16)
            dgq += jnp.sum(dg, 0, keepdims=True)
            k_slab = jnp.concatenate([kv_pre[:, h * QK_PAD:h * QK_PAD + QK_NOPE], kpe], axis=-1)
            inv = lax.rsqrt(jnp.sum(k_slab * k_slab, -1, keepdims=True) * (1.0 / QK_HEAD) + EPS)
            d_slab, dg = _head_norm_rope_bwd(dk_ref[h], k_slab, gk_ref[...], inv, cos_t, sin_t)
            dkvp_ref[:, head] = jnp.concatenate([d_slab[:, 0:QK_NOPE], dv_ref[h]], axis=-1).astype(BF16)
            dkpe += d_slab[:, QK_NOPE:QK_PAD]
            dgk += jnp.sum(dg, 0, keepdims=True)
        dqa = _dot_nt(dqp_ref[...], wq_ref[...])
        dx, dg = _rms_bwd(dqa, ql_t, qa_ref[...], inv_qa, Q_LORA)
        dql_ref[...] = dx.astype(BF16)
        _acc(dqa_ref, jnp.sum(dg, 0, keepdims=True))
        dca = _dot_nt(dkvp_ref[...], wkv_ref[...])
        dx, dg = _rms_bwd(dca, ckv, ka_ref[...], inv_ca, KV_LORA)
        dkvl_ref[:, 0:KV_LORA] = dx.astype(BF16)
        dkvl_ref[:, KV_LORA:KV_LAT_PAD] = dkpe.astype(BF16)
        _acc(dka_ref, jnp.sum(dg, 0, keepdims=True))
        _acc(dgq_ref, dgq)
        _acc(dgk_ref, dgk)
        _acc(dwq_ref, _dot_tn(qab, dqp_ref[...]))
        _acc(dwkv_ref, _dot_tn(cab, dkvp_ref[...]))

    wide = N_HEADS * QK_PAD
    row_outs = [((n, Q_LORA), BF16), ((n, KV_LAT_PAD), BF16)]
    acc_outs = [((1, Q_LORA), F32), ((1, KV_LORA), F32), ((1, QK_PAD), F32), ((1, QK_PAD), F32),
                ((Q_LORA, wide), F32), ((KV_LORA, wide), F32)]
    return _row_call(body, "qkv_prep_bwd", n, ROW_T, [ql, kvl, cos_t, sin_t, dq, dk, dv],
                     [q_a_norm, kv_a_norm, wq, wkv, gq, gk], row_outs, acc_outs, xch=xch,
                     scratch=[pltpu.VMEM((ROW_T, wide), BF16), pltpu.VMEM((ROW_T, wide), BF16)])


def _glu_bwd(dy_ssm, y, w_glu, b_glu, w_o_ssm):
    n = y.shape[0]

    def body(dys_ref, y_ref, wg_ref, bg_ref, wo_ref, dy_ref, db_ref, dwg_ref, dwo_ref):
        y_t = y_ref[...]
        z, th = _gelu(y_t)
        zb = z.astype(BF16)
        s = _sigmoid(_dot(zb, wg_ref[...]) + bg_ref[...])
        dys = dys_ref[...]
        dzg = jnp.zeros_like(y_t)
        for j in range(N_DEV):
            dzg += _dot_nt(dys[:, j * OUT_SHARD:(j + 1) * OUT_SHARD], wo_ref[j])
        dt = dzg * z * s * (1.0 - s)
        dtb = dt.astype(BF16)
        dz = dzg * s + _dot_nt(dtb, wg_ref[...])
        dy_ref[...] = dz * _gelu_grad(y_t, th)
        _acc(db_ref, jnp.sum(dt, 0, keepdims=True))
        _acc(dwg_ref, _dot_tn(zb, dtb))
        _acc(dwo_ref, _dot_tn((z * s).astype(BF16), dys))

    acc_outs = [((1, SSM_WIDTH), F32), ((SSM_WIDTH, SSM_WIDTH), F32), ((SSM_WIDTH, D_MODEL), F32)]
    return _row_call(body, "glu_bwd", n, ROW_T, [dy_ssm, y], [w_glu, b_glu, w_o_ssm], [((n, SSM_WIDTH), F32)], acc_outs)


def _ssm_bwd(u, dy, st, bblk, cblk, lam, d_row, xch):
    n = u.shape[0]
    t = min(SCAN_T, n)
    nc = n // t
    kb = 512
    perm = _perm_matrix(t)

    def body(u_ref, dy_ref, st_ref, p_ref, bblk_ref, cblk_ref, lam_ref, d_ref,
             du_ref, dlam_ref, dd_ref, db_ref, dct_ref,
             buf_x, buf_a, pw_ref, carry_ref, xcarry_ref, sx_ref, sa_ref, db_acc, dct_acc):
        @pl.when(pl.program_id(0) == 0)
        def _():
            carry_ref[...] = jnp.zeros_like(carry_ref)
            db_acc[...] = jnp.zeros_like(db_acc)
            dct_acc[...] = jnp.zeros_like(dct_acc)
            _power_table(lam_ref, pw_ref, t // SUBCHUNKS)

        u_t = u_ref[...]
        dy_t = dy_ref[...]
        p = p_ref[...]
        ub = _dot(p, u_t.astype(BF16)).astype(BF16)
        dyb = _dot(p, dy_t.astype(BF16)).astype(BF16)
        _to_states(ub, bblk_ref, buf_x, False)
        xcarry_ref[...] = st_ref[0]
        _run_scan(buf_x, lam_ref, t, False)
        _run_carries(buf_x, pw_ref, xcarry_ref, sx_ref, t, False)
        _run_fix(buf_x, pw_ref, sx_ref, t, False)
        _to_states(dyb, cblk_ref, buf_a, True)
        _run_scan(buf_a, lam_ref, t, True)
        _run_carries(buf_a, pw_ref, carry_ref, sa_ref, t, True)
        _run_fix(buf_a, pw_ref, sa_ref, t, True)
        du_ref[...] = (d_ref[...] * dy_t + _unpermute(p, _to_channels(buf_a, bblk_ref, True))).astype(BF16)
        for b in range(STATE_BLOCKS):
            lanes, ch = _state_block(b)
            db_acc[ch, lanes] += _dot_tn(ub[:, ch], buf_a[:, lanes].astype(BF16))
            dct_acc[ch, lanes] += _dot_tn(dyb[:, ch], buf_x[:, lanes].astype(BF16))
        for c in range(0, N_STATE, kb):
            re, im = pl.ds(c, kb), pl.ds(N_STATE + c, kb)
            xr, xi = buf_x[pl.ds(0, t - 8), re], buf_x[pl.ds(0, t - 8), im]
            ar, ai = buf_a[pl.ds(8, t - 8), re], buf_a[pl.ds(8, t - 8), im]
            x0r, x0i = sx_ref[:, re], sx_ref[:, im]
            a0r, a0i = buf_a[0:8, re], buf_a[0:8, im]
            dlam_part_re = (jnp.sum(ar * xr + ai * xi, 0, keepdims=True)
                            + jnp.sum(a0r * x0r + a0i * x0i, 0, keepdims=True))
            dlam_part_im = (jnp.sum(ai * xr - ar * xi, 0, keepdims=True)
                            + jnp.sum(a0i * x0r - a0r * x0i, 0, keepdims=True))

            @pl.when(pl.program_id(0) == 0)
            def _(c=c):
                dlam_ref[0:1, c:c + kb] = jnp.zeros((1, kb), F32)
                dlam_ref[1:2, c:c + kb] = jnp.zeros((1, kb), F32)

            dlam_ref[0:1, c:c + kb] += dlam_part_re
            dlam_ref[1:2, c:c + kb] += dlam_part_im
        _acc(dd_ref, jnp.sum(dy_t * u_t, 0, keepdims=True))

        @pl.when(pl.program_id(0) == nc - 1)
        def _():
            pltpu.sync_copy(db_acc, db_ref)
            pltpu.sync_copy(dct_acc, dct_ref)

    rev = lambda i: (nc - 1 - i, 0)
    consts = [perm, bblk, cblk, lam, d_row]
    wide = (SSM_WIDTH, 2 * N_STATE)
    return _call(
        body, "ssm_bwd", (nc,), [u, dy, st] + consts,
        [pl.BlockSpec((t, SSM_WIDTH), rev), pl.BlockSpec((t, SSM_WIDTH), rev),
         pl.BlockSpec((1, 8, 2 * N_STATE), lambda i: (nc - 1 - i, 0, 0))] + [_const(a) for a in consts],
        [_sds((n, SSM_WIDTH), BF16), _sds((2, N_STATE), F32), _sds((1, SSM_WIDTH), F32), _sds(wide, F32), _sds(wide, F32)],
        [pl.BlockSpec((t, SSM_WIDTH), rev), pl.BlockSpec((2, N_STATE), lambda i: (0, 0)),
         pl.BlockSpec((1, SSM_WIDTH), lambda i: (0, 0)), ANY, ANY],
        scratch=[pltpu.VMEM((t, 2 * N_STATE), F32)] * 2 + [pltpu.VMEM((t // SUBCHUNKS, 2 * N_STATE), F32)]
        + [pltpu.VMEM((8, 2 * N_STATE), F32)] * 4 + [pltpu.VMEM(wide, F32)] * 2,
        xch=xch)


def _in_proj_bwd(pieces, dh, x, xn_t, norm_mix, w_in_pad, xch):
    n = x.shape[0]
    tm = min(MM_T, n)
    nt = n // tm

    def body(du_ref, dql_ref, dkvl_ref, dgs_ref, dgm_ref, dh_ref, x_ref, xnt_ref, g_ref, w_ref,
             dx_ref, dg_ref, dw_ref, acc_ref):
        @pl.when(pl.program_id(0) == 0)
        def _():
            acc_ref[...] = jnp.zeros_like(acc_ref)

        xnt = xnt_ref[...]
        dxn = jnp.zeros((tm, D_MODEL), F32)
        for ref, (a, b) in zip((du_ref, dql_ref, dkvl_ref, dgs_ref, dgm_ref), IN_SEGS):
            piece = ref[...]
            dxn += _dot_nt(piece, w_ref[:, a:b])
            acc_ref[:, a:b] += _dot(xnt, piece)
        x_t = x_ref[...]
        inv = lax.rsqrt(jnp.sum(x_t * x_t, -1, keepdims=True) * (1.0 / D_MODEL) + EPS)
        dx, dg = _rms_bwd(dxn, x_t, g_ref[...], inv, D_MODEL)
        dx_ref[...] = dh_ref[...] + dx
        _acc(dg_ref, jnp.sum(dg, 0, keepdims=True))

        @pl.when(pl.program_id(0) == nt - 1)
        def _():
            pltpu.sync_copy(acc_ref, dw_ref)

    row_ins, consts = list(pieces) + [dh, x], [norm_mix, w_in_pad]
    in_specs = ([_rows(a, tm) for a in row_ins] + [pl.BlockSpec((D_MODEL, tm), lambda i: (0, i))]
                + [_const(a) for a in consts])
    return _call(
        body, "in_proj_bwd", (nt,), row_ins + [xn_t] + consts, in_specs,
        [_sds((n, D_MODEL), F32), _sds((1, D_MODEL), F32), _sds((D_MODEL, D_IN_PAD), F32)],
        [pl.BlockSpec((tm, D_MODEL), lambda i: (i, 0)), pl.BlockSpec((1, D_MODEL), lambda i: (0, 0)), ANY],
        scratch=[pltpu.VMEM((D_MODEL, D_IN_PAD), F32)], xch=xch)


def _swap_minor(a):
    g, r, c = a.shape[1:]
    return jnp.transpose(a[0], (0, 2, 1)).reshape(g * c, r)


def _pad_in(w):
    return jnp.concatenate([w[:, :KV_END], jnp.zeros((w.shape[0], D_IN_PAD - D_IN), w.dtype), w[:, KV_END:]], axis=1)


def _unpad_in(w):
    return jnp.concatenate([w[:, :KV_END], w[:, KV_END + D_IN_PAD - D_IN:]], axis=1)


def _pad_gain(g):
    return jnp.pad(g, ((0, 0), (0, QK_PAD - QK_HEAD)))


def _place():
    x, y, c = lax.axis_index("x"), lax.axis_index("y"), lax.axis_index("c")
    chips = [(x, y), (1 - x, y), (x, 1 - y), (1 - x, 1 - y)]
    return x, y, c, chips


def _all_gather(block, name):
    rows, lanes = block.shape

    def body(x_ref, out_ref, send_sems, recv_sems, local_sem):
        x, y, c, chips = _place()
        me, sibling = (x, y, c), (x, y, 1 - c)

        def slot(px, py, pc):
            return out_ref.at[4 * px + 2 * py + pc]

        def copy(k, blk, to, src=None):
            return pltpu.make_async_remote_copy(
                src_ref=slot(*blk) if src is None else src, dst_ref=slot(*blk),
                send_sem=send_sems.at[k], recv_sem=recv_sems.at[k], device_id=to, device_id_type=MESH)

        mine = pltpu.make_async_copy(x_ref, slot(*me), local_sem)
        mine.start()
        first = [copy(0, me, sibling, src=x_ref)]
        first += [copy(1 + j, me, (*chip, c), src=x_ref) for j, chip in enumerate(chips[1:])]
        for cp in first:
            cp.start()
        passed = [copy(4 + j, (*chip, c), sibling) for j, chip in enumerate(chips[1:])]
        for j, chip in enumerate(chips[1:]):
            copy(1 + j, (*chip, c), me).wait_recv()
            passed[j].start()
        copy(0, sibling, me).wait_recv()
        for j, chip in enumerate(chips[1:]):
            copy(4 + j, (*chip, 1 - c), me).wait_recv()
        for cp in first + passed:
            cp.wait_send()
        mine.wait()

    return pl.pallas_call(
        body,
        name=name,
        in_specs=[ANY],
        out_specs=ANY,
        out_shape=_sds((N_DEV, rows, lanes), block.dtype),
        scratch_shapes=[pltpu.SemaphoreType.DMA((7,)), pltpu.SemaphoreType.DMA((7,)), pltpu.SemaphoreType.DMA],
    )(block)


def _reduce_scatter(parts, gather, name):
    _, rows, lanes = parts.shape

    def body(p_ref, g_ref, out_ref, ga_ref, own, land_a, send_b, land_b, sa, ra, sb, rb, lo, *g_sems):
        x, y, c, chips = _place()
        sibling = (x, y, 1 - c)
        _xchg_start([False], [g_ref], [ga_ref], *g_sems)

        def blk(chip, core):
            return p_ref.at[4 * chip[0] + 2 * chip[1] + core]

        to_sib = [pltpu.make_async_remote_copy(
            src_ref=blk(chips[k], 1 - c), dst_ref=land_a.at[k], send_sem=sa.at[k], recv_sem=ra.at[k],
            device_id=sibling, device_id_type=MESH) for k in range(4)]
        for cp in to_sib:
            cp.start()
        loads = [pltpu.make_async_copy(blk(chips[k], c), own.at[k], lo.at[k]) for k in range(4)]
        for cp in loads:
            cp.start()
        to_chip = [pltpu.make_async_remote_copy(
            src_ref=send_b.at[j], dst_ref=land_b.at[j], send_sem=sb.at[j], recv_sem=rb.at[j],
            device_id=(*chips[1 + j], c), device_id_type=MESH) for j in range(3)]
        for k in (1, 2, 3):
            to_sib[k].wait_recv()
            loads[k].wait()
            send_b[k - 1] = (own[k] + land_a[k]).astype(BF16)
            to_chip[k - 1].start()
        to_sib[0].wait_recv()
        loads[0].wait()
        acc = own[0] + land_a[0]
        for j in range(3):
            to_chip[j].wait_recv()
            acc = acc + land_b[j].astype(F32)
        out_ref[...] = acc
        for cp in to_sib + to_chip:
            cp.wait_send()
        _xchg_wait([False], [g_ref], [ga_ref], *g_sems)

    return pl.pallas_call(
        body,
        name=name,
        in_specs=[ANY, ANY],
        out_specs=[pl.BlockSpec(memory_space=pltpu.VMEM), ANY],
        out_shape=[_sds((rows, lanes), F32), _sds((N_DEV,) + gather.shape, gather.dtype)],
        scratch_shapes=[pltpu.VMEM((4, rows, lanes), F32), pltpu.VMEM((4, rows, lanes), F32),
                        pltpu.VMEM((3, rows, lanes), BF16), pltpu.VMEM((3, rows, lanes), BF16)]
        + [pltpu.SemaphoreType.DMA((4,))] * 2 + [pltpu.SemaphoreType.DMA((3,))] * 2 + [pltpu.SemaphoreType.DMA((4,))]
        + [pltpu.SemaphoreType.DMA((1,))] * 3,
        compiler_params=_params(),
    )(parts, gather)


def _adamw_math(w, g, m, v):
    m = ADAM_B1 * m + (1.0 - ADAM_B1) * g
    v = ADAM_B2 * v + (1.0 - ADAM_B2) * (g * g)
    m_hat = m / (1.0 - ADAM_B1 ** ADAM_STEP)
    v_hat = v / (1.0 - ADAM_B2 ** ADAM_STEP)
    delta = -ADAM_LR * (m_hat / (jnp.sqrt(v_hat) + ADAM_EPS) + ADAM_WD * w)
    return delta, m, v


def _row_tile(r):
    return max(t for t in range(8, min(r, 256) + 1, 8) if r % t == 0)


def _adamw(w, g, m, v, name):
    r, n = w.shape

    def body(w_ref, g_ref, m_ref, v_ref, d_ref, nm_ref, nv_ref):
        d_ref[...], nm_ref[...], nv_ref[...] = _adamw_math(w_ref[...], g_ref[...], m_ref[...], v_ref[...])

    return _row_call(body, name, r, _row_tile(r), [w, g, m, v], [], [((r, n), F32)] * 3)


def _adamw_sum(landed, w, m, v, name):
    r, n = w.shape

    def body(l_ref, w_ref, m_ref, v_ref, g_ref, d_ref, nm_ref, nv_ref):
        g = l_ref[0].astype(F32)
        for dev in range(1, N_DEV):
            g = g + l_ref[dev].astype(F32)
        g_ref[...] = g
        d_ref[...], nm_ref[...], nv_ref[...] = _adamw_math(w_ref[...], g, m_ref[...], v_ref[...])

    tm = max(t for t in range(16, min(r, 256) + 1, 16) if r % t == 0)
    return _row_call(body, name, r, tm, [landed, w, m, v], [], [((r, n), F32)] * 4)


def _adamw_small(first, rest, w, m, v, row_counts):
    n_rest = w.shape[0] - first.shape[1]

    def body(f_ref, r_ref, w_ref, m_ref, v_ref, loss_ref, *out_refs):
        gf, gr = f_ref[0], r_ref[0]
        for dev in range(1, N_DEV):
            gf, gr = gf + f_ref[dev], gr + r_ref[dev]
        loss_ref[...] = gr[n_rest:n_rest + 8]
        g = jnp.concatenate([gf, gr[0:n_rest]], axis=0)
        d, nm, nv = _adamw_math(w_ref[...], g, m_ref[...], v_ref[...])
        off = 0
        for p, rows in enumerate(row_counts):
            for k, val in enumerate((g, d, nm, nv)):
                out_refs[4 * p + k][...] = val[off:off + rows]
            off += rows

    outs = [_sds((8, LANES), F32)] + [_sds((rows, LANES), F32) for rows in row_counts for _ in range(4)]
    return pl.pallas_call(body, name="adamw_small", out_shape=outs, compiler_params=_params())(first, rest, w, m, v)


SMALL = ("norm_mix", "q_a_norm", "kv_a_norm", "q_norm", "k_norm", "ssm_a_re", "ssm_a_im", "ssm_log_dt", "ssm_b_re",
         "ssm_b_im", "ssm_c_re", "ssm_c_im", "ssm_d", "b_glu", "norm_mlp")
WEIGHT_ORDER = ("norm_mix", "w_in", "q_a_norm", "kv_a_norm", "w_q_b", "w_kv_b", "q_norm", "k_norm", "w_o_mla",
                "ssm_a_re", "ssm_a_im", "ssm_log_dt", "ssm_b_re", "ssm_b_im", "ssm_c_re", "ssm_c_im", "ssm_d", "w_glu",
                "b_glu", "w_o_ssm", "w_out", "norm_mlp", "w_up", "w_down")
IN_SHARD = D_IN // N_DEV
Q_SHARD = QK_HEAD


def _pack_small(vals, names=SMALL):
    parts = []
    for n in names:
        flat = vals[n].reshape(-1)
        size = -(-flat.shape[0] // (8 * LANES)) * 8 * LANES
        parts.append(jnp.pad(flat, (0, size - flat.shape[0])).reshape(-1, LANES))
    return jnp.concatenate(parts, axis=0)


def _small_rows(like):
    return [-(-like[n].size // (8 * LANES)) * 8 for n in SMALL]


def _step(x, pos_col, target, w, small):
    bf = {n: a.astype(BF16) for n, a in w.items()}
    gq, gk = _pad_gain(small["q_norm"]), _pad_gain(small["k_norm"])
    a_re = small["ssm_a_re"].reshape(1, N_STATE)
    a_im = small["ssm_a_im"].reshape(1, N_STATE)
    log_dt = jnp.repeat(small["ssm_log_dt"].reshape(SSM_GROUPS), SSM_STATE).reshape(1, N_STATE)
    bt_re, bt_im = _swap_minor(small["ssm_b_re"]), _swap_minor(small["ssm_b_im"])
    c2_re, c2_im = _swap_minor(small["ssm_c_re"]), _swap_minor(small["ssm_c_im"])
    d_row = small["ssm_d"].reshape(1, SSM_WIDTH)

    w_in_all = _all_gather(bf["w_in"], "gather_w_in")
    w_in_pad = _pad_in(jnp.transpose(w_in_all, (1, 0, 2)).reshape(D_MODEL, D_IN))
    cos_t, sin_t = _rope_tables(pos_col)
    lam, bblk, cblk = _ssm_prep(a_re, a_im, log_dt, bt_re, bt_im, c2_re, c2_im)
    wq_mine = jnp.pad(bf["w_q_b"], ((0, 0), (0, QK_PAD - QK_HEAD)))
    u, ql, kvl, gs, gm, xn_t, w_glu, w_o_ssm = _in_proj(
        x, small["norm_mix"], w_in_pad, xch=[(bf["w_glu"], False), (bf["w_o_ssm"], False)])
    w_glu = w_glu.reshape(SSM_WIDTH, SSM_WIDTH)
    y, y_ssm, st, wq, wkv, w_o_mla, w_out = _ssm_fwd(
        u, bblk, cblk, lam, d_row, w_glu, small["b_glu"], w_o_ssm,
        xch=[(wq_mine, False), (bf["w_kv_b"], False), (bf["w_o_mla"], False), (bf["w_out"], False)])
    w_o_mla, w_out = w_o_mla.reshape(D_MODEL, D_MODEL), w_out.reshape(D_MODEL, D_MODEL)
    wq = jnp.transpose(wq, (1, 0, 2)).reshape(Q_LORA, N_HEADS * QK_PAD)
    wkv = jnp.transpose(wkv, (1, 0, 2)).reshape(KV_LORA, N_HEADS * QK_PAD)
    q, k, v, kt, vt = _qkv_prep(ql, kvl, small["q_a_norm"], small["kv_a_norm"], wq, wkv, gq, gk, cos_t, sin_t)
    attn, lse, attn_t, w_up, w_down = _attn_fwd(q, k, vt, xch=[(bf["w_up"], False), (bf["w_down"], False)])
    h, y_mla, mixed_t = _merge(attn, gs, gm, y_ssm, x, w_o_mla, w_out)
    hn, dout, hn_t, loss = _mlp_fwd_loss(h, target, small["norm_mlp"], w_up, w_down)

    da, dh, dout_b, hid_t, d_norm_mlp = _mlp_bwd(dout, hn, h, small["norm_mlp"], w_up, w_down)
    p_w_down = _matmul_tn_shards(hid_t, dout_b, "dw_down", False, tm=1024, turned=True)
    p_w_up = _matmul_tn_shards(hn_t, da, "dw_up", True, turned=True)
    dgs, dgm, dy_ssm, dy_mla, dattn = _merge_bwd(dh, gs, gm, y_ssm, y_mla, w_out, w_o_mla)
    p_w_out = _matmul_tn_shards(mixed_t, dh, "dw_out", False, tm=1024, turned=True)
    p_w_o_mla = _matmul_tn_shards(attn_t, dy_mla, "dw_o_mla", False, turned=True)
    dq, dk, dv, l_w_up, l_w_down, l_w_out, l_w_o_mla = _attn_bwd(
        q, k, kt, v, attn, lse, dattn, xch=[(p_w_up, True), (p_w_down, True), (p_w_out, True), (p_w_o_mla, True)])
    dql, dkvl, d_q_a_norm, d_kv_a_norm, d_gq, d_gk, g_wq, g_wkv = _qkv_prep_bwd(
        ql, kvl, dq, dk, dv, small["q_a_norm"], small["kv_a_norm"], wq, wkv, gq, gk, cos_t, sin_t, xch=[])
    p_wq = jnp.transpose(g_wq.reshape(Q_LORA, N_HEADS, QK_PAD), (1, 0, 2)).astype(BF16)
    p_wkv = jnp.transpose(g_wkv.reshape(KV_LORA, N_HEADS, QK_PAD), (1, 0, 2)).astype(BF16)
    dy, d_b_glu, g_w_glu, g_w_o_ssm = _glu_bwd(dy_ssm, y, w_glu, small["b_glu"], w_o_ssm)
    p_w_o_ssm = jnp.transpose(g_w_o_ssm.reshape(SSM_WIDTH, N_DEV, OUT_SHARD), (1, 0, 2)).astype(BF16)
    p_w_glu = g_w_glu.reshape(N_DEV, SSM_WIDTH // N_DEV, SSM_WIDTH).astype(BF16)
    du, dlam, d_d, d_bblk, d_cblk_t, l_wq, l_wkv, l_w_glu, l_w_o_ssm = _ssm_bwd(
        u, dy, st, bblk, cblk, lam, d_row, xch=[(p_wq, True), (p_wkv, True), (p_w_glu, True), (p_w_o_ssm, True)])
    d_a_re, d_a_im, d_log_dt, d_bt_re, d_bt_im, d_c_re, d_c_im = _ssm_prep_bwd(
        a_re, a_im, log_dt, bt_re, bt_im, dlam, d_bblk, d_cblk_t)
    tr = lambda mat: jnp.transpose(mat.reshape(SSM_GROUPS, SSM_GROUP_CH, SSM_STATE), (0, 2, 1))
    g_small = {
        "q_a_norm": d_q_a_norm, "kv_a_norm": d_kv_a_norm, "q_norm": d_gq[:, :QK_HEAD], "k_norm": d_gk[:, :QK_HEAD],
        "ssm_a_re": d_a_re, "ssm_a_im": d_a_im, "ssm_log_dt": d_log_dt,
        "ssm_b_re": tr(d_bt_re), "ssm_b_im": tr(d_bt_im), "ssm_c_re": d_c_re, "ssm_c_im": d_c_im,
        "ssm_d": d_d, "b_glu": d_b_glu, "norm_mlp": d_norm_mlp,
    }
    rest = jnp.concatenate([_pack_small(g_small, SMALL[1:]), loss], axis=0)
    dx, d_norm_mix, g_w_in_pad, g_rest_all = _in_proj_bwd(
        (du, dql, dkvl, dgs, dgm), dh, x, xn_t, small["norm_mix"], w_in_pad, xch=[(rest, False)])
    parts = jnp.transpose(_unpad_in(g_w_in_pad).reshape(D_MODEL, N_DEV, IN_SHARD), (1, 0, 2))
    g_w_in_mine, g_first_all = _reduce_scatter(parts, _pack_small({SMALL[0]: d_norm_mix}, SMALL[:1]), "reduce_w_in")
    landed = {"w_q_b": l_wq[:, :, :QK_HEAD], "w_kv_b": l_wkv, "w_o_mla": l_w_o_mla, "w_glu": l_w_glu,
              "w_o_ssm": l_w_o_ssm, "w_out": l_w_out, "w_up": l_w_up, "w_down": l_w_down}
    return dx, landed, g_w_in_mine, g_first_all, g_rest_all


def kernel(x, positions, norm_mix, w_in, q_a_norm, kv_a_norm, w_q_b, w_kv_b, q_norm, k_norm, w_o_mla, ssm_a_re, ssm_a_im, ssm_log_dt, ssm_b_re, ssm_b_im, ssm_c_re, ssm_c_im, ssm_d, w_glu, b_glu, w_o_ssm, w_out, norm_mlp, w_up, w_down, loss_target, m_norm_mix, m_w_in, m_q_a_norm, m_kv_a_norm, m_w_q_b, m_w_kv_b, m_q_norm, m_k_norm, m_w_o_mla, m_ssm_a_re, m_ssm_a_im, m_ssm_log_dt, m_ssm_b_re, m_ssm_b_im, m_ssm_c_re, m_ssm_c_im, m_ssm_d, m_w_glu, m_b_glu, m_w_o_ssm, m_w_out, m_norm_mlp, m_w_up, m_w_down, v_norm_mix, v_w_in, v_q_a_norm, v_kv_a_norm, v_w_q_b, v_w_kv_b, v_q_norm, v_k_norm, v_w_o_mla, v_ssm_a_re, v_ssm_a_im, v_ssm_log_dt, v_ssm_b_re, v_ssm_b_im, v_ssm_c_re, v_ssm_c_im, v_ssm_d, v_w_glu, v_b_glu, v_w_o_ssm, v_w_out, v_norm_mlp, v_w_up, v_w_down):
    given = dict(locals())
    w = {n: given[n] for n in WEIGHT_ORDER}
    m = {n: given["m_" + n] for n in WEIGHT_ORDER}
    v = {n: given["v_" + n] for n in WEIGHT_ORDER}
    big = [n for n in WEIGHT_ORDER if n not in SMALL]
    small = {n: w[n] for n in SMALL}

    dx, landed, g_w_in, g_first_all, g_rest_all = _step(
        x[0], positions.reshape(-1, 1), loss_target[0], {n: w[n][0] for n in big}, small)

    grads, deltas, new_m, new_v = {}, {}, {}, {}
    for n in big:
        if n == "w_in":
            g = g_w_in
            d, nm, nv = _adamw(w[n][0], g, m[n][0], v[n][0], "adamw_" + n)
        else:
            g, d, nm, nv = _adamw_sum(landed[n], w[n][0], m[n][0], v[n][0], "adamw_" + n)
        grads[n], deltas[n], new_m[n], new_v[n] = g[None], d[None], nm[None], nv[None]

    outs = _adamw_small(g_first_all, g_rest_all, _pack_small(small), _pack_small({n: m[n] for n in SMALL}),
                        _pack_small({n: v[n] for n in SMALL}), _small_rows(small))
    for p, n in enumerate(SMALL):
        for k, dst in enumerate((grads, deltas, new_m, new_v)):
            dst[n] = outs[1 + 4 * p + k].reshape(-1)[:small[n].size].reshape(small[n].shape)

    return (outs[0][0, 0], dx[None], *[grads[n] for n in WEIGHT_ORDER], *[deltas[n] for n in WEIGHT_ORDER],
            *[new_m[n] for n in WEIGHT_ORDER], *[new_v[n] for n in WEIGHT_ORDER])
```

```python
import functools
import math

import numpy as np
import jax
import jax.numpy as jnp
from jax import lax
from jax.experimental import pallas as pl
from jax.experimental.pallas import tpu as pltpu

F32 = jnp.float32
BF16 = jnp.bfloat16

D_MODEL = 1024
SSM_GROUPS = 32
SSM_GROUP_CH = 16
SSM_WIDTH = 512
SSM_STATE = 64
N_STATE = SSM_GROUPS * SSM_STATE
N_HEADS = 8
QK_NOPE = 128
QK_ROPE = 64
QK_HEAD = 192
QK_PAD = 256
V_HEAD = 128
Q_LORA = 384
KV_LORA = 256
KV_LAT_PAD = 384
ROPE_THETA = 10000.0
D_FF = 4096
EPS = 1e-6
ATT_SCALE = QK_HEAD ** -0.5
N_DEV = 8
FF_SHARD = D_FF // N_DEV
OUT_SHARD = D_MODEL // N_DEV

IN_SEGS = ((0, 512), (512, 896), (896, 1280), (1280, 2304), (2304, 3328))
D_IN = 3264
D_IN_PAD = 3328
KV_END = 1216

ADAM_LR = 0.001
ADAM_B1 = 0.9
ADAM_B2 = 0.999
ADAM_EPS = 1e-08
ADAM_WD = 0.01
ADAM_STEP = 10

VMEM_LIMIT = 56 * 1024 * 1024
MESH = pl.DeviceIdType.MESH
ANY = pl.BlockSpec(memory_space=pl.ANY)
LANES = 128

SCAN_T = 256
SUBCHUNKS = 8
SCAN_CG = 512
ATT_T = 512
ATT_SUB = 2
ROW_T = 256
MM_T = 512


def _params(sem=None):
    return pltpu.CompilerParams(dimension_semantics=sem, vmem_limit_bytes=VMEM_LIMIT)


def _rows(arr, tm):
    if arr.ndim == 2:
        return pl.BlockSpec((tm, arr.shape[1]), lambda i: (i, 0))
    return pl.BlockSpec((arr.shape[0], tm, arr.shape[2]), lambda i: (0, i, 0))


def _const(arr):
    nd = arr.ndim
    return pl.BlockSpec(arr.shape, lambda *_: (0,) * nd, pipeline_mode=pl.Buffered(1))


def _sds(shape, dtype):
    return jax.ShapeDtypeStruct(shape, dtype)


PEERS = tuple((dx, dy, dc) for dx in (0, 1) for dy in (0, 1) for dc in (0, 1) if (dx, dy, dc) != (0, 0, 0))


def _here():
    x, y, c = lax.axis_index("x"), lax.axis_index("y"), lax.axis_index("c")
    return x, y, c, 4 * x + 2 * y + c


def _xchg_start(scatter, srcs, dsts, send, recv, local):
    x, y, c, me = _here()
    for e, sc in enumerate(scatter):
        src, dst = srcs[e], dsts[e]
        pltpu.make_async_copy(src.at[me] if sc else src, dst.at[me], local.at[e]).start()
        for dx, dy, dc in PEERS:
            px, py, pc = (1 - x if dx else x), (1 - y if dy else y), (1 - c if dc else c)
            pltpu.make_async_remote_copy(
                src_ref=src.at[4 * px + 2 * py + pc] if sc else src, dst_ref=dst.at[me],
                send_sem=send.at[e], recv_sem=recv.at[e], device_id=(px, py, pc), device_id_type=MESH).start()


def _xchg_wait(scatter, srcs, dsts, send, recv, local):
    x, y, c, me = _here()
    for e, sc in enumerate(scatter):
        src, dst = srcs[e], dsts[e]
        pltpu.make_async_copy(src.at[me] if sc else src, dst.at[me], local.at[e]).wait()
        span = dst.at[pl.ds(0, N_DEV - 1)]
        both = pltpu.make_async_remote_copy(src_ref=span, dst_ref=span, send_sem=send.at[e], recv_sem=recv.at[e],
                                            device_id=(x, y, c), device_id_type=MESH)
        both.wait_send()
        both.wait_recv()


def _call(body, name, grid, ins, in_specs, outs, out_specs, scratch=(), xch=()):
    n_in, n_out, ne = len(ins), len(outs), len(xch)
    scatter = [sc for _, sc in xch]
    x_outs = [_sds((N_DEV,) + (a.shape[1:] if sc else a.shape), a.dtype) for a, sc in xch]
    sems = [pltpu.SemaphoreType.DMA((ne,))] * 3 if ne else []

    def wrapped(*refs):
        in_refs, x_src = refs[:n_in], refs[n_in:n_in + ne]
        out_refs = refs[n_in + ne:n_in + ne + n_out]
        x_dst = refs[n_in + ne + n_out:n_in + 2 * ne + n_out]
        rest = refs[n_in + 2 * ne + n_out:]
        if ne:
            x_sems, rest = rest[len(rest) - 3:], rest[:len(rest) - 3]
            first = functools.reduce(jnp.logical_and, [pl.program_id(d) == 0 for d in range(len(grid))])
            last = functools.reduce(jnp.logical_and, [pl.program_id(d) == grid[d] - 1 for d in range(len(grid))])

            @pl.when(first)
            def _():
                _xchg_start(scatter, x_src, x_dst, *x_sems)

        body(*in_refs, *out_refs, *rest)
        if ne:
            @pl.when(last)
            def _():
                _xchg_wait(scatter, x_src, x_dst, *x_sems)

    return pl.pallas_call(
        wrapped,
        name=name,
        grid=grid,
        in_specs=list(in_specs) + [ANY] * ne,
        out_specs=list(out_specs) + [ANY] * ne,
        out_shape=list(outs) + x_outs,
        scratch_shapes=list(scratch) + sems,
        compiler_params=_params(("arbitrary",) * len(grid)),
    )(*ins, *[a for a, _ in xch])


def _row_call(body, name, n_rows, tm, row_ins, const_ins, row_outs, acc_outs=(), xch=(), col_outs=(), scratch=()):
    outs = [_sds(s, d) for s, d in list(row_outs) + list(col_outs) + list(acc_outs)]
    n_row, n_col = len(row_outs), len(col_outs)
    out_specs = [_rows(o, tm) for o in outs[:n_row]] + [
        pl.BlockSpec((o.shape[0], tm), lambda i: (0, i)) for o in outs[n_row:n_row + n_col]] + [
        pl.BlockSpec(o.shape, lambda i, nd=len(o.shape): (0,) * nd) for o in outs[n_row + n_col:]]
    in_specs = [_rows(a, tm) for a in row_ins] + [_const(a) for a in const_ins]
    return _call(body, name, (n_rows // tm,), list(row_ins) + list(const_ins), in_specs, outs, out_specs,
                 scratch=scratch, xch=xch)


def _dot(a, b):
    return jnp.dot(a, b, preferred_element_type=F32)


def _dot_nt(a, b):
    return lax.dot_general(a, b, (((1,), (1,)), ((), ())), preferred_element_type=F32)


def _dot_tn(a, b):
    return lax.dot_general(a, b, (((0,), (0,)), ((), ())), preferred_element_type=F32)


def _rms(x, g, n):
    inv = lax.rsqrt(jnp.sum(x * x, -1, keepdims=True) * (1.0 / n) + EPS)
    return x * inv * g, inv


def _rms_bwd(dy, x, g, inv, n):
    xh = x * inv
    dxh = dy * g
    dx = inv * (dxh - xh * (jnp.sum(dxh * xh, -1, keepdims=True) * (1.0 / n)))
    return dx, dy * xh


def _sigmoid(x):
    return 1.0 / (1.0 + jnp.exp(-x))


_GELU_C = math.sqrt(2.0 / math.pi)


def _gelu(y):
    th = jnp.tanh(_GELU_C * (y + 0.044715 * (y * y * y)))
    return 0.5 * y * (1.0 + th), th


def _gelu_grad(y, th):
    return 0.5 * (1.0 + th) + 0.5 * y * (1.0 - th * th) * (_GELU_C * (1.0 + 3.0 * 0.044715 * (y * y)))


def _acc(ref, val):
    @pl.when(pl.program_id(0) == 0)
    def _():
        ref[...] = jnp.zeros_like(ref)

    ref[...] += val


def _tile(n, limit):
    if n <= limit:
        return n
    return max(t for t in range(128, limit + 1, 128) if n % t == 0)


def _lhs(a, turned, tm, tk):
    m, k_dim = a.shape if turned else a.shape[::-1]
    tm, tk = _tile(m, tm), _tile(k_dim, tk)
    if turned:
        return m, k_dim, tm, tk, pl.BlockSpec((tm, tk), lambda i, k: (i, k)), _dot
    return m, k_dim, tm, tk, pl.BlockSpec((tk, tm), lambda i, k: (k, i)), _dot_tn


def _matmul_tn_shards(a, b, name, by_col, tm=512, tk=512, turned=False):
    m, k_dim, tm, tk, a_spec, dot = _lhs(a, turned, tm, tk)
    n = b.shape[1]
    nk = k_dim // tk
    if by_col:
        r, c = m, n // N_DEV
        out_spec = pl.BlockSpec((N_DEV, tm, c), lambda i, k: (0, i, 0))
    else:
        r, c = m // N_DEV, n
        per = tm // r
        out_spec = pl.BlockSpec((per, r, c), lambda i, k: (i, 0, 0))

    def body(a_ref, b_ref, o_ref, acc_ref):
        k = pl.program_id(1)

        @pl.when(k == 0)
        def _():
            acc_ref[...] = jnp.zeros_like(acc_ref)

        acc_ref[...] += dot(a_ref[...].astype(BF16), b_ref[...].astype(BF16))

        @pl.when(k == nk - 1)
        def _():
            if by_col:
                for j in range(N_DEV):
                    o_ref[j] = acc_ref[:, j * c:(j + 1) * c].astype(BF16)
            else:
                for s in range(per):
                    o_ref[s] = acc_ref[s * r:(s + 1) * r, :].astype(BF16)

    return pl.pallas_call(
        body,
        name=name,
        grid=(m // tm, nk),
        in_specs=[a_spec, pl.BlockSpec((tk, n), lambda i, k: (k, 0))],
        out_specs=out_spec,
        out_shape=_sds((N_DEV, r, c), BF16),
        scratch_shapes=[pltpu.VMEM((tm, n), F32)],
        compiler_params=_params(("parallel", "arbitrary")),
    )(a, b)


def _rope_tables(pos_col):
    n = pos_col.shape[0]
    half = QK_ROPE // 2
    inv_freq = (ROPE_THETA ** (-np.arange(half, dtype=np.float32) / half)).astype(np.float32)
    freq_row = jnp.asarray(np.concatenate([inv_freq, inv_freq, np.zeros(64, np.float32)])[None, :])

    def body(p_ref, f_ref, c_ref, s_ref):
        ang = p_ref[...].astype(F32) * f_ref[...]
        c_ref[...] = jnp.cos(ang)
        s_ref[...] = jnp.sin(ang)

    return _row_call(body, "rope_tables", n, min(n, 1024), [pos_col], [freq_row], [((n, 128), F32)] * 2)


def _rope_rot(v):
    lane = lax.broadcasted_iota(jnp.int32, v.shape, 1)
    return jnp.where(lane < 32, -pltpu.roll(v, 96, 1), jnp.where(lane < 64, pltpu.roll(v, 32, 1), 0.0))


def _rope_rot_t(v):
    lane = lax.broadcasted_iota(jnp.int32, v.shape, 1)
    return jnp.where(lane < 32, pltpu.roll(v, 96, 1), jnp.where(lane < 64, -pltpu.roll(v, 32, 1), 0.0))


def _in_proj(x, norm_mix, w_in_pad, xch):
    n = x.shape[0]

    def body(x_ref, g_ref, w_ref, u_ref, ql_ref, kvl_ref, gs_ref, gm_ref, xnt_ref):
        xn, _ = _rms(x_ref[...], g_ref[...], D_MODEL)
        xb = xn.astype(BF16)
        xnt_ref[...] = xn.T.astype(BF16)
        for ref, (a, b) in zip((u_ref, ql_ref, kvl_ref, gs_ref, gm_ref), IN_SEGS):
            ref[...] = _dot(xb, w_ref[:, a:b])

    outs = [((n, b - a), F32) for a, b in IN_SEGS]
    return _row_call(body, "in_proj", n, MM_T, [x], [norm_mix, w_in_pad], outs, xch=xch,
                     col_outs=[((D_MODEL, n), BF16)])


def _ssm_prep_fn(a_re, a_im, log_dt, b_re_x, b_im_x):
    dt = jnp.exp(log_dt)
    mag = jnp.exp(a_re * dt)
    lr = mag * jnp.cos(a_im * dt)
    li = mag * jnp.sin(a_im * dt)
    den = a_re * a_re + a_im * a_im
    fr = ((lr - 1.0) * a_re + li * a_im) / den
    fi = (li * a_re - (lr - 1.0) * a_im) / den
    return lr, li, fr * b_re_x - fi * b_im_x, fr * b_im_x + fi * b_re_x


def _dot_exact(a, b, dims):
    return lax.dot_general(a, b, (dims, ((), ())), precision=lax.Precision.HIGHEST, preferred_element_type=F32)


def _lane_repeat(width, n):
    src = lax.broadcasted_iota(jnp.int32, (width, n), 0)
    dst = lax.broadcasted_iota(jnp.int32, (width, n), 1)
    return (dst % width == src).astype(F32)


def _same_group(rows, rows_per_group, cols, cols_per_group):
    row = lax.broadcasted_iota(jnp.int32, (rows, cols), 0)
    col = lax.broadcasted_iota(jnp.int32, (rows, cols), 1)
    return (row // rows_per_group) == (col // cols_per_group)


def _expand_b(bt):
    tiled = _dot_exact(bt, _lane_repeat(SSM_STATE, N_STATE), ((1,), (0,)))
    return jnp.where(_same_group(SSM_WIDTH, SSM_GROUP_CH, N_STATE, SSM_STATE), tiled, 0.0)


def _collect_b(m):
    masked = jnp.where(_same_group(SSM_WIDTH, SSM_GROUP_CH, N_STATE, SSM_STATE), m, 0.0)
    return _dot_exact(masked, _lane_repeat(SSM_STATE, N_STATE), ((1,), (1,)))


def _ssm_prep(a_re, a_im, log_dt, bt_re, bt_im, c2_re, c2_im):
    def body(ar, ai, ld, br, bi, cr, ci, lam_ref, bblk_ref, cblk_ref):
        lr, li, bbr, bbi = _ssm_prep_fn(ar[...], ai[...], ld[...], _expand_b(br[...]), _expand_b(bi[...]))
        lam_ref[0:1, :] = lr
        lam_ref[1:2, :] = li
        bblk_ref[:, 0:N_STATE] = bbr.astype(BF16)
        bblk_ref[:, N_STATE:] = bbi.astype(BF16)
        rep = _lane_repeat(SSM_GROUP_CH, SSM_WIDTH)
        own = _same_group(N_STATE, SSM_STATE, SSM_WIDTH, SSM_GROUP_CH)
        cblk_ref[0:N_STATE, :] = jnp.where(own, _dot_exact(cr[...], rep, ((1,), (0,))), 0.0).astype(BF16)
        cblk_ref[N_STATE:, :] = jnp.where(own, -_dot_exact(ci[...], rep, ((1,), (0,))), 0.0).astype(BF16)

    return pl.pallas_call(
        body,
        name="ssm_prep",
        out_shape=[_sds((2, N_STATE), F32), _sds((SSM_WIDTH, 2 * N_STATE), BF16),
                   _sds((2 * N_STATE, SSM_WIDTH), BF16)],
        compiler_params=_params(),
    )(a_re, a_im, log_dt, bt_re, bt_im, c2_re, c2_im)


def _ssm_prep_bwd(a_re, a_im, log_dt, bt_re, bt_im, dlam, dbblk, dcblk_t):
    def body(ar, ai, ld, br, bi, dl, db, dc, dar, dai, dld, dbr, dbi, dcr, dci):
        _, vjp = jax.vjp(_ssm_prep_fn, ar[...], ai[...], ld[...], _expand_b(br[...]), _expand_b(bi[...]))
        g = vjp((dl[0:1, :], dl[1:2, :], db[:, 0:N_STATE], db[:, N_STATE:]))
        dar[...] = g[0]
        dai[...] = g[1]
        grp = lax.broadcasted_iota(jnp.int32, (SSM_GROUPS, N_STATE), 0)
        lane = lax.broadcasted_iota(jnp.int32, (SSM_GROUPS, N_STATE), 1)
        sel = (lane // SSM_STATE) == grp
        dld[...] = jnp.sum(jnp.where(sel, jnp.broadcast_to(g[2], (SSM_GROUPS, N_STATE)), 0.0), axis=1, keepdims=True)
        dbr[...] = _collect_b(g[3])
        dbi[...] = _collect_b(g[4])
        dcr[...] = _collect_b(dc[:, 0:N_STATE])
        dci[...] = -_collect_b(dc[:, N_STATE:])

    small = _sds((SSM_WIDTH, SSM_STATE), F32)
    return pl.pallas_call(
        body,
        name="ssm_prep_bwd",
        out_shape=[_sds((1, N_STATE), F32), _sds((1, N_STATE), F32), _sds((SSM_GROUPS, 1), F32), small, small, small, small],
        compiler_params=_params(),
    )(a_re, a_im, log_dt, bt_re, bt_im, dlam, dbblk, dcblk_t)


def _perm_matrix(t):
    run = t // SUBCHUNKS
    p = np.zeros((t, t), np.float32)
    r = np.arange(t)
    p[r, (r % SUBCHUNKS) * run + r // SUBCHUNKS] = 1.0
    return jnp.asarray(p, dtype=BF16)


def _unpermute(p, a):
    hi = a.astype(BF16)
    r1 = a - hi.astype(F32)
    mid = r1.astype(BF16)
    lo = (r1 - mid.astype(F32)).astype(BF16)
    return _dot_tn(p, hi) + _dot_tn(p, mid) + _dot_tn(p, lo)


def _power_table(lam_ref, pw_ref, n):
    lr, li = lam_ref[0:1, :], lam_ref[1:2, :]
    pw_ref[0:1, 0:N_STATE] = lr
    pw_ref[0:1, N_STATE:] = li

    def step(i, carry):
        pr, pi = carry
        pr, pi = pr * lr - pi * li, pr * li + pi * lr
        pw_ref[pl.ds(i, 1), 0:N_STATE] = pr
        pw_ref[pl.ds(i, 1), N_STATE:] = pi
        return pr, pi

    lax.fori_loop(1, n, step, (lr, li))


def _col_groups():
    return [(pl.ds(c, SCAN_CG), pl.ds(N_STATE + c, SCAN_CG)) for c in range(0, N_STATE, SCAN_CG)]


def _run_scan(buf, lam_ref, t, reverse):
    nblk = t // 8
    for re, im in _col_groups():
        lr = jnp.broadcast_to(lam_ref[0:1, re], (8, SCAN_CG))
        li = jnp.broadcast_to(lam_ref[1:2, re], (8, SCAN_CG))
        if reverse:
            li = -li
        first = pl.ds((nblk - 1) * 8 if reverse else 0, 8)

        def step(k, carry, re=re, im=im, lr=lr, li=li):
            pr, pi = carry
            i = (nblk - 2 - k) if reverse else (k + 1)
            r = pl.ds(pl.multiple_of(i * 8, 8), 8)
            xr = buf[r, re] + lr * pr - li * pi
            xi = buf[r, im] + lr * pi + li * pr
            buf[r, re] = xr
            buf[r, im] = xi
            return xr, xi

        lax.fori_loop(0, nblk - 1, step, (buf[first, re], buf[first, im]))


def _run_carries(buf, pw_ref, carry_ref, s_ref, t, reverse):
    nblk = t // 8
    run = t // SUBCHUNKS
    edge = buf[pl.ds(0 if reverse else (nblk - 1) * 8, 8), :]
    pr, pi = pw_ref[run - 1:run, 0:N_STATE], pw_ref[run - 1:run, N_STATE:]
    if reverse:
        pi = -pi
    sr, si = carry_ref[0:1, 0:N_STATE], carry_ref[0:1, N_STATE:]
    for s in (range(SUBCHUNKS - 1, -1, -1) if reverse else range(SUBCHUNKS)):
        s_ref[s:s + 1, 0:N_STATE] = sr
        s_ref[s:s + 1, N_STATE:] = si
        er, ei = edge[s:s + 1, 0:N_STATE], edge[s:s + 1, N_STATE:]
        sr, si = er + pr * sr - pi * si, ei + pr * si + pi * sr
    carry_ref[:, 0:N_STATE] = jnp.broadcast_to(sr, (8, N_STATE))
    carry_ref[:, N_STATE:] = jnp.broadcast_to(si, (8, N_STATE))


def _run_fix(buf, pw_ref, s_ref, t, reverse):
    nblk = t // 8
    for re, im in _col_groups():
        sr, si = s_ref[:, re], s_ref[:, im]

        def step(i, carry, re=re, im=im, sr=sr, si=si):
            r = pl.ds(pl.multiple_of(i * 8, 8), 8)
            row = pl.ds((nblk - 1 - i) if reverse else i, 1)
            pr, pi = pw_ref[row, re], pw_ref[row, im]
            if reverse:
                pi = -pi
            buf[r, re] += pr * sr - pi * si
            buf[r, im] += pr * si + pi * sr
            return carry

        lax.fori_loop(0, nblk, step, 0)


STATE_BLOCKS = 2 * N_STATE // LANES
CH_BLOCKS = SSM_WIDTH // LANES


def _state_block(b):
    pair = b % (N_STATE // LANES)
    k = (pair * 2 * SSM_GROUP_CH) // LANES
    return slice(b * LANES, (b + 1) * LANES), slice(k * LANES, (k + 1) * LANES)


def _channel_block(c):
    w = N_STATE // CH_BLOCKS
    return slice(c * LANES, (c + 1) * LANES), slice(c * w, (c + 1) * w), slice(N_STATE + c * w, N_STATE + (c + 1) * w)


def _to_states(vb, w_ref, buf, nt):
    for b in range(STATE_BLOCKS):
        lanes, ch = _state_block(b)
        buf[:, lanes] = _dot_nt(vb[:, ch], w_ref[lanes, ch]) if nt else _dot(vb[:, ch], w_ref[ch, lanes])


def _to_channels(buf, w_ref, nt):
    outs = []
    for c in range(CH_BLOCKS):
        ch, re, im = _channel_block(c)
        xr, xi = buf[:, re].astype(BF16), buf[:, im].astype(BF16)
        if nt:
            outs.append(_dot_nt(xr, w_ref[ch, re]) + _dot_nt(xi, w_ref[ch, im]))
        else:
            outs.append(_dot(xr, w_ref[re, ch]) + _dot(xi, w_ref[im, ch]))
    return jnp.concatenate(outs, axis=-1)


def _ssm_fwd(u, bblk, cblk, lam, d_row, w_glu, b_glu, w_o_ssm, xch):
    n = u.shape[0]
    t = min(SCAN_T, n)
    perm = _perm_matrix(t)

    def body(u_ref, p_ref, bblk_ref, cblk_ref, lam_ref, d_ref, wg_ref, bg_ref, wo_ref, y_ref, ys_ref, st_ref,
             buf, pw_ref, carry_ref, s_ref):
        @pl.when(pl.program_id(0) == 0)
        def _():
            carry_ref[...] = jnp.zeros_like(carry_ref)
            _power_table(lam_ref, pw_ref, t // SUBCHUNKS)

        st_ref[0] = carry_ref[...]
        u_t = u_ref[...]
        p = p_ref[...]
        ub = _dot(p, u_t.astype(BF16)).astype(BF16)
        _to_states(ub, bblk_ref, buf, False)
        _run_scan(buf, lam_ref, t, False)
        _run_carries(buf, pw_ref, carry_ref, s_ref, t, False)
        _run_fix(buf, pw_ref, s_ref, t, False)
        y = d_ref[...] * u_t + _unpermute(p, _to_channels(buf, cblk_ref, False))
        y_ref[...] = y
        z, _ = _gelu(y)
        s = _sigmoid(_dot(z.astype(BF16), wg_ref[...]) + bg_ref[...])
        zgb = (z * s).astype(BF16)
        for j in range(N_DEV):
            ys_ref[:, j * OUT_SHARD:(j + 1) * OUT_SHARD] = _dot(zgb, wo_ref[j])

    consts = [perm, bblk, cblk, lam, d_row, w_glu, b_glu, w_o_ssm]
    return _call(
        body, "ssm_fwd", (n // t,), [u] + consts, [_rows(u, t)] + [_const(a) for a in consts],
        [_sds((n, SSM_WIDTH), F32), _sds((n, D_MODEL), F32), _sds((n // t, 8, 2 * N_STATE), F32)],
        [pl.BlockSpec((t, SSM_WIDTH), lambda i: (i, 0)), pl.BlockSpec((t, D_MODEL), lambda i: (i, 0)),
         pl.BlockSpec((1, 8, 2 * N_STATE), lambda i: (i, 0, 0))],
        scratch=[pltpu.VMEM((t, 2 * N_STATE), F32), pltpu.VMEM((t // SUBCHUNKS, 2 * N_STATE), F32),
                 pltpu.VMEM((8, 2 * N_STATE), F32), pltpu.VMEM((8, 2 * N_STATE), F32)],
        xch=xch)


def _head_norm_rope(slab, gain, cos_t, sin_t):
    xn, inv = _rms(slab, gain, QK_HEAD)
    lo, hi = xn[:, 0:128], xn[:, 128:256]
    return jnp.concatenate([lo, hi * cos_t + _rope_rot(hi) * sin_t], axis=-1), inv


def _head_norm_rope_bwd(g, slab, gain, inv, cos_t, sin_t):
    g_lo, g_hi = g[:, 0:128], g[:, 128:256]
    g_n = jnp.concatenate([g_lo, g_hi * cos_t + _rope_rot_t(g_hi * sin_t)], axis=-1)
    return _rms_bwd(g_n, slab, gain, inv, QK_HEAD)


def _qkv_prep(ql, kvl, q_a_norm, kv_a_norm, wq, wkv, gq, gk, cos_t, sin_t):
    n = ql.shape[0]
    tm = ROW_T

    def body(ql_ref, kvl_ref, cos_ref, sin_ref, qa_ref, ka_ref, wq_ref, wkv_ref, gq_ref, gk_ref,
             q_ref, k_ref, v_ref, kt_ref, vt_ref):
        cos_t, sin_t = cos_ref[...], sin_ref[...]
        qa, _ = _rms(ql_ref[...], qa_ref[...], Q_LORA)
        qab = qa.astype(BF16)
        kvl_t = kvl_ref[...]
        ca, _ = _rms(kvl_t[:, 0:KV_LORA], ka_ref[...], KV_LORA)
        cab = ca.astype(BF16)
        kpe = kvl_t[:, KV_LORA:KV_LAT_PAD]
        q_pre = _dot(qab, wq_ref[...])
        kv_pre = _dot(cab, wkv_ref[...])
        for h in range(N_HEADS):
            qh, _ = _head_norm_rope(q_pre[:, h * QK_PAD:(h + 1) * QK_PAD], gq_ref[...], cos_t, sin_t)
            q_ref[h] = (qh * ATT_SCALE).astype(BF16)
            kv_h = kv_pre[:, h * QK_PAD:(h + 1) * QK_PAD]
            kh, _ = _head_norm_rope(jnp.concatenate([kv_h[:, 0:QK_NOPE], kpe], axis=-1), gk_ref[...], cos_t, sin_t)
            k_ref[h] = kh.astype(BF16)
            kt_ref[h] = kh.T.astype(BF16)
            vh = kv_h[:, QK_NOPE:]
            v_ref[h] = vh.astype(BF16)
            vt_ref[h] = vh.T.astype(BF16)

    row_ins, consts = [ql, kvl, cos_t, sin_t], [q_a_norm, kv_a_norm, wq, wkv, gq, gk]
    outs = [_sds((N_HEADS, n, QK_PAD), BF16), _sds((N_HEADS, n, QK_PAD), BF16), _sds((N_HEADS, n, V_HEAD), BF16),
            _sds((N_HEADS, QK_PAD, n), BF16), _sds((N_HEADS, V_HEAD, n), BF16)]
    out_specs = [_rows(o, tm) for o in outs[:3]] + [
        pl.BlockSpec((N_HEADS, QK_PAD, tm), lambda i: (0, 0, i)), pl.BlockSpec((N_HEADS, V_HEAD, tm), lambda i: (0, 0, i))]
    return _call(body, "qkv_prep", (n // tm,), row_ins + consts,
                 [_rows(a, tm) for a in row_ins] + [_const(a) for a in consts], outs, out_specs)


def _causal_mask_t(st, t):
    key = lax.broadcasted_iota(jnp.int32, (t, t), 0)
    qry = lax.broadcasted_iota(jnp.int32, (t, t), 1)
    return jnp.where(key <= qry, st, -jnp.inf)


def _attn_fwd(q, k, vt, xch):
    n = q.shape[1]
    t = min(ATT_T, n)

    def body(q_ref, k_ref, vt_ref, o_ref, lse_ref, ot_ref):
        i = pl.program_id(1)
        qt = q_ref[0]

        def kv_tile(j, carry, diag):
            m, l, acc = carry
            ts = t // ATT_SUB
            sts = []
            for a in range(ATT_SUB):
                r0 = pl.multiple_of(j * t + a * ts, ts)
                st = _dot_nt(k_ref[0, pl.ds(r0, ts), :], qt)
                if diag:
                    key = lax.broadcasted_iota(jnp.int32, (ts, t), 0) + a * ts
                    qry = lax.broadcasted_iota(jnp.int32, (ts, t), 1)
                    st = jnp.where(key <= qry, st, -jnp.inf)
                sts.append(st)
            for a, st in enumerate(sts):
                r0 = pl.multiple_of(j * t + a * ts, ts)
                m_new = jnp.maximum(m, jnp.max(st, 0, keepdims=True))
                alpha = jnp.exp(m - m_new)
                pt = jnp.exp(st - m_new)
                l = alpha * l + jnp.sum(pt, 0, keepdims=True)
                acc = alpha * acc + _dot(vt_ref[0, :, pl.ds(r0, ts)], pt.astype(BF16))
                m = m_new
            return m, l, acc

        init = (jnp.full((1, t), -jnp.inf, F32), jnp.zeros((1, t), F32), jnp.zeros((V_HEAD, t), F32))
        carry = lax.fori_loop(0, i, functools.partial(kv_tile, diag=False), init)
        m, l, acc = kv_tile(i, carry, True)
        out_t = acc / l
        o_ref[...] = out_t.T
        ot_ref[...] = out_t.astype(BF16)
        lse_ref[0] = m + jnp.log(l)

    return _call(
        body, "attn_fwd", (N_HEADS, n // t), [q, k, vt],
        [pl.BlockSpec((1, t, QK_PAD), lambda h, i: (h, i, 0)), pl.BlockSpec((1, n, QK_PAD), lambda h, i: (h, 0, 0)),
         pl.BlockSpec((1, V_HEAD, n), lambda h, i: (h, 0, 0))],
        [_sds((n, N_HEADS * V_HEAD), F32), _sds((N_HEADS, 1, n), F32), _sds((N_HEADS * V_HEAD, n), BF16)],
        [pl.BlockSpec((t, V_HEAD), lambda h, i: (i, h)), pl.BlockSpec((1, 1, t), lambda h, i: (h, 0, i)),
         pl.BlockSpec((V_HEAD, t), lambda h, i: (h, i))],
        xch=xch)


def _merge(attn, gs, gm, y_ssm, x, w_o_mla, w_out):
    n = x.shape[0]

    def body(at_ref, gs_ref, gm_ref, ys_ref, x_ref, wo_ref, wout_ref, h_ref, ym_ref, mxt_ref):
        y_mla = _dot(at_ref[...].astype(BF16), wo_ref[...])
        ym_ref[...] = y_mla
        mixed = _sigmoid(gs_ref[...]) * ys_ref[...] + _sigmoid(gm_ref[...]) * y_mla
        mxt_ref[...] = mixed.T.astype(BF16)
        h_ref[...] = x_ref[...] + _dot(mixed.astype(BF16), wout_ref[...])

    outs = [((n, D_MODEL), F32), ((n, D_MODEL), F32)]
    return _row_call(body, "merge", n, MM_T, [attn, gs, gm, y_ssm, x], [w_o_mla, w_out], outs,
                     col_outs=[((D_MODEL, n), BF16)])


def _mlp_fwd_loss(h, target, norm_mlp, w_up, w_down):
    n = h.shape[0]

    def body(h_ref, t_ref, g_ref, wu_ref, wd_ref, hn_ref, do_ref, hnt_ref, loss_ref):
        h_t = h_ref[...]
        hn, _ = _rms(h_t, g_ref[...], D_MODEL)
        hb = hn.astype(BF16)
        hn_ref[...] = hb
        hnt_ref[...] = hn.T.astype(BF16)
        out = h_t
        for j in range(N_DEV):
            a = jnp.maximum(_dot(hb, wu_ref[j]), 0.0)
            out += _dot((a * a).astype(BF16), wd_ref[j])
        err = out - t_ref[...]
        do_ref[...] = err * (1.0 / D_MODEL)
        _acc(loss_ref, jnp.broadcast_to(jnp.sum(err * err) * (0.5 / D_MODEL), loss_ref.shape))

    outs = [((n, D_MODEL), BF16), ((n, D_MODEL), F32)]
    return _row_call(body, "mlp_fwd_loss", n, MM_T, [h, target], [norm_mlp, w_up, w_down], outs, [((8, 128), F32)],
                     col_outs=[((D_MODEL, n), BF16)])


def _mlp_bwd(dout, hn, h, norm_mlp, w_up, w_down):
    n = h.shape[0]

    def body(do_ref, hn_ref, h_ref, g_ref, wu_ref, wd_ref, da_ref, dh_ref, dob_ref, hidt_ref, dg_ref):
        dout_t = do_ref[...]
        doutb = dout_t.astype(BF16)
        dob_ref[...] = doutb
        hb = hn_ref[...]
        dhn = jnp.zeros_like(dout_t)
        for j in range(N_DEV):
            cols = slice(j * FF_SHARD, (j + 1) * FF_SHARD)
            a = jnp.maximum(_dot(hb, wu_ref[j]), 0.0)
            hidt_ref[cols, :] = (a * a).T.astype(BF16)
            da = (_dot_nt(doutb, wd_ref[j]) * (2.0 * a)).astype(BF16)
            da_ref[:, cols] = da
            dhn += _dot_nt(da, wu_ref[j])
        h_t = h_ref[...]
        inv = lax.rsqrt(jnp.sum(h_t * h_t, -1, keepdims=True) * (1.0 / D_MODEL) + EPS)
        dx, dg = _rms_bwd(dhn, h_t, g_ref[...], inv, D_MODEL)
        dh_ref[...] = dout_t + dx
        _acc(dg_ref, jnp.sum(dg, 0, keepdims=True))

    outs = [((n, D_FF), BF16), ((n, D_MODEL), F32), ((n, D_MODEL), BF16)]
    return _row_call(body, "mlp_bwd", n, MM_T, [dout, hn, h], [norm_mlp, w_up, w_down], outs, [((1, D_MODEL), F32)],
                     col_outs=[((D_FF, n), BF16)])


def _merge_bwd(dh, gs, gm, y_ssm, y_mla, w_out, w_o_mla):
    n = dh.shape[0]

    def body(dh_ref, gs_ref, gm_ref, ys_ref, ym_ref, wout_ref, wo_ref, dgs_ref, dgm_ref, dys_ref, dym_ref, dat_ref):
        dmix = _dot_nt(dh_ref[...].astype(BF16), wout_ref[...])
        sgs, sgm = _sigmoid(gs_ref[...]), _sigmoid(gm_ref[...])
        dgs_ref[...] = (dmix * ys_ref[...] * sgs * (1.0 - sgs)).astype(BF16)
        dgm_ref[...] = (dmix * ym_ref[...] * sgm * (1.0 - sgm)).astype(BF16)
        dys_ref[...] = (dmix * sgs).astype(BF16)
        dym = (dmix * sgm).astype(BF16)
        dym_ref[...] = dym
        dat_ref[...] = _dot_nt(dym, wo_ref[...])

    outs = [((n, D_MODEL), BF16)] * 4 + [((n, D_MODEL), F32)]
    return _row_call(body, "merge_bwd", n, MM_T, [dh, gs, gm, y_ssm, y_mla], [w_out, w_o_mla], outs)


def _attn_bwd(q, k, kt, v, out, lse, dout, xch):
    n = q.shape[1]
    t = min(ATT_T, n)
    nt = n // t

    def body(q_ref, k_ref, kt_ref, v_ref, o_ref, lse_ref, do_ref, dq_ref, dk_ref, dv_ref, delta_ref, dqt_ref):
        j = pl.program_id(1)

        @pl.when(j == 0)
        def _():
            dqt_ref[...] = jnp.zeros_like(dqt_ref)
            prod = do_ref[...] * o_ref[...]
            delta_ref[...] = lax.dot_general(jnp.ones((8, V_HEAD), F32), prod, (((1,), (1,)), ((), ())),
                                             precision=lax.Precision.HIGHEST, preferred_element_type=F32)

        k_t = k_ref[0]
        kt_t = kt_ref[0]
        v_t = v_ref[0]

        def q_tile(i, carry, diag):
            dk, dv = carry
            r0 = pl.multiple_of(i * t, t)
            rows = pl.ds(r0, t)
            qt = q_ref[0, rows, :]
            st = _dot_nt(k_t, qt)
            if diag:
                st = _causal_mask_t(st, t)
            pt = jnp.exp(st - lse_ref[0, :, rows])
            dob = do_ref[rows, :].astype(BF16)
            dv = dv + _dot(pt.astype(BF16), dob)
            dst = (pt * (_dot_nt(v_t, dob) - delta_ref[0:1, rows])).astype(BF16)
            dk = dk + _dot(dst, qt)
            dqt_ref[:, rows] += _dot(kt_t, dst)
            return dk, dv

        carry = q_tile(j, (jnp.zeros((t, QK_PAD), F32), jnp.zeros((t, V_HEAD), F32)), True)
        dk, dv = lax.fori_loop(j + 1, nt, functools.partial(q_tile, diag=False), carry)
        dk_ref[0] = dk
        dv_ref[0] = dv

        @pl.when(j == nt - 1)
        def _():
            for c in range(0, n, t):
                dq_ref[0, c:c + t, :] = dqt_ref[:, c:c + t].T

    return _call(
        body, "attn_bwd", (N_HEADS, nt), [q, k, kt, v, out, lse, dout],
        [pl.BlockSpec((1, n, QK_PAD), lambda h, j: (h, 0, 0)), pl.BlockSpec((1, t, QK_PAD), lambda h, j: (h, j, 0)),
         pl.BlockSpec((1, QK_PAD, t), lambda h, j: (h, 0, j)), pl.BlockSpec((1, t, V_HEAD), lambda h, j: (h, j, 0)),
         pl.BlockSpec((n, V_HEAD), lambda h, j: (0, h)), pl.BlockSpec((1, 1, n), lambda h, j: (h, 0, 0)),
         pl.BlockSpec((n, V_HEAD), lambda h, j: (0, h))],
        [_sds((N_HEADS, n, QK_PAD), F32), _sds((N_HEADS, n, QK_PAD), F32), _sds((N_HEADS, n, V_HEAD), F32)],
        [pl.BlockSpec((1, n, QK_PAD), lambda h, j: (h, 0, 0)), pl.BlockSpec((1, t, QK_PAD), lambda h, j: (h, j, 0)),
         pl.BlockSpec((1, t, V_HEAD), lambda h, j: (h, j, 0))],
        scratch=[pltpu.VMEM((8, n), F32), pltpu.VMEM((QK_PAD, n), F32)],
        xch=xch)


def _qkv_prep_bwd(ql, kvl, dq, dk, dv, q_a_norm, kv_a_norm, wq, wkv, gq, gk, cos_t, sin_t, xch):
    n = ql.shape[0]

    def body(ql_ref, kvl_ref, cos_ref, sin_ref, dq_ref, dk_ref, dv_ref, qa_ref, ka_ref, wq_ref, wkv_ref, gq_ref, gk_ref,
             dql_ref, dkvl_ref, dqa_ref, dka_ref, dgq_ref, dgk_ref, dwq_ref, dwkv_ref, dqp_ref, dkvp_ref):
        cos_t, sin_t = cos_ref[...], sin_ref[...]
        ql_t = ql_ref[...]
        qa, inv_qa = _rms(ql_t, qa_ref[...], Q_LORA)
        qab = qa.astype(BF16)
        kvl_t = kvl_ref[...]
        ckv = kvl_t[:, 0:KV_LORA]
        ca, inv_ca = _rms(ckv, ka_ref[...], KV_LORA)
        cab = ca.astype(BF16)
        kpe = kvl_t[:, KV_LORA:KV_LAT_PAD]
        dgq = jnp.zeros((1, QK_PAD), F32)
        dgk = jnp.zeros((1, QK_PAD), F32)
        dkpe = jnp.zeros_like(kpe)
        q_pre = _dot(qab, wq_ref[...])
        kv_pre = _dot(cab, wkv_ref[...])
        for h in range(N_HEADS):
            head = slice(h * QK_PAD, (h + 1) * QK_PAD)
            q_slab = q_pre[:, head]
            inv = lax.rsqrt(jnp.sum(q_slab * q_slab, -1, keepdims=True) * (1.0 / QK_HEAD) + EPS)
            d_slab, dg = _head_norm_rope_bwd(dq_ref[h] * ATT_SCALE, q_slab, gq_ref[...], inv, cos_t, sin_t)
            dqp_ref[:, head] = d_slab.astype(BF16)
            dgq += jnp.sum(dg, 0, keepdims=True)
            k_slab = jnp.concatenate([kv_pre[:, h * QK_PAD:h * QK_PAD + QK_NOPE], kpe], axis=-1)
            inv = lax.rsqrt(jnp.sum(k_slab * k_slab, -1, keepdims=True) * (1.0 / QK_HEAD) + EPS)
            d_slab, dg = _head_norm_rope_bwd(dk_ref[h], k_slab, gk_ref[...], inv, cos_t, sin_t)
            dkvp_ref[:, head] = jnp.concatenate([d_slab[:, 0:QK_NOPE], dv_ref[h]], axis=-1).astype(BF16)
            dkpe += d_slab[:, QK_NOPE:QK_PAD]
            dgk += jnp.sum(dg, 0, keepdims=True)
        dqa = _dot_nt(dqp_ref[...], wq_ref[...])
        dx, dg = _rms_bwd(dqa, ql_t, qa_ref[...], inv_qa, Q_LORA)
        dql_ref[...] = dx.astype(BF16)
        _acc(dqa_ref, jnp.sum(dg, 0, keepdims=True))
        dca = _dot_nt(dkvp_ref[...], wkv_ref[...])
        dx, dg = _rms_bwd(dca, ckv, ka_ref[...], inv_ca, KV_LORA)
        dkvl_ref[:, 0:KV_LORA] = dx.astype(BF16)
        dkvl_ref[:, KV_LORA:KV_LAT_PAD] = dkpe.astype(BF16)
        _acc(dka_ref, jnp.sum(dg, 0, keepdims=True))
        _acc(dgq_ref, dgq)
        _acc(dgk_ref, dgk)
        _acc(dwq_ref, _dot_tn(qab, dqp_ref[...]))
        _acc(dwkv_ref, _dot_tn(cab, dkvp_ref[...]))

    wide = N_HEADS * QK_PAD
    row_outs = [((n, Q_LORA), BF16), ((n, KV_LAT_PAD), BF16)]
    acc_outs = [((1, Q_LORA), F32), ((1, KV_LORA), F32), ((1, QK_PAD), F32), ((1, QK_PAD), F32),
                ((Q_LORA, wide), F32), ((KV_LORA, wide), F32)]
    return _row_call(body, "qkv_prep_bwd", n, ROW_T, [ql, kvl, cos_t, sin_t, dq, dk, dv],
                     [q_a_norm, kv_a_norm, wq, wkv, gq, gk], row_outs, acc_outs, xch=xch,
                     scratch=[pltpu.VMEM((ROW_T, wide), BF16), pltpu.VMEM((ROW_T, wide), BF16)])


def _glu_bwd(dy_ssm, y, w_glu, b_glu, w_o_ssm):
    n = y.shape[0]

    def body(dys_ref, y_ref, wg_ref, bg_ref, wo_ref, dy_ref, db_ref, dwg_ref, dwo_ref):
        y_t = y_ref[...]
        z, th = _gelu(y_t)
        zb = z.astype(BF16)
        s = _sigmoid(_dot(zb, wg_ref[...]) + bg_ref[...])
        dys = dys_ref[...]
        dzg = jnp.zeros_like(y_t)
        for j in range(N_DEV):
            dzg += _dot_nt(dys[:, j * OUT_SHARD:(j + 1) * OUT_SHARD], wo_ref[j])
        dt = dzg * z * s * (1.0 - s)
        dtb = dt.astype(BF16)
        dz = dzg * s + _dot_nt(dtb, wg_ref[...])
        dy_ref[...] = dz * _gelu_grad(y_t, th)
        _acc(db_ref, jnp.sum(dt, 0, keepdims=True))
        _acc(dwg_ref, _dot_tn(zb, dtb))
        _acc(dwo_ref, _dot_tn((z * s).astype(BF16), dys))

    acc_outs = [((1, SSM_WIDTH), F32), ((SSM_WIDTH, SSM_WIDTH), F32), ((SSM_WIDTH, D_MODEL), F32)]
    return _row_call(body, "glu_bwd", n, ROW_T, [dy_ssm, y], [w_glu, b_glu, w_o_ssm], [((n, SSM_WIDTH), F32)], acc_outs)


def _ssm_bwd(u, dy, st, bblk, cblk, lam, d_row, xch):
    n = u.shape[0]
    t = min(SCAN_T, n)
    nc = n // t
    kb = 512
    perm = _perm_matrix(t)

    def body(u_ref, dy_ref, st_ref, p_ref, bblk_ref, cblk_ref, lam_ref, d_ref,
             du_ref, dlam_ref, dd_ref, db_ref, dct_ref,
             buf_x, buf_a, pw_ref, carry_ref, xcarry_ref, sx_ref, sa_ref, db_acc, dct_acc):
        @pl.when(pl.program_id(0) == 0)
        def _():
            carry_ref[...] = jnp.zeros_like(carry_ref)
            db_acc[...] = jnp.zeros_like(db_acc)
            dct_acc[...] = jnp.zeros_like(dct_acc)
            _power_table(lam_ref, pw_ref, t // SUBCHUNKS)

        u_t = u_ref[...]
        dy_t = dy_ref[...]
        p = p_ref[...]
        ub = _dot(p, u_t.astype(BF16)).astype(BF16)
        dyb = _dot(p, dy_t.astype(BF16)).astype(BF16)
        _to_states(ub, bblk_ref, buf_x, False)
        xcarry_ref[...] = st_ref[0]
        _run_scan(buf_x, lam_ref, t, False)
        _run_carries(buf_x, pw_ref, xcarry_ref, sx_ref, t, False)
        _run_fix(buf_x, pw_ref, sx_ref, t, False)
        _to_states(dyb, cblk_ref, buf_a, True)
        _run_scan(buf_a, lam_ref, t, True)
        _run_carries(buf_a, pw_ref, carry_ref, sa_ref, t, True)
        _run_fix(buf_a, pw_ref, sa_ref, t, True)
        du_ref[...] = (d_ref[...] * dy_t + _unpermute(p, _to_channels(buf_a, bblk_ref, True))).astype(BF16)
        for b in range(STATE_BLOCKS):
            lanes, ch = _state_block(b)
            db_acc[ch, lanes] += _dot_tn(ub[:, ch], buf_a[:, lanes].astype(BF16))
            dct_acc[ch, lanes] += _dot_tn(dyb[:, ch], buf_x[:, lanes].astype(BF16))
        for c in range(0, N_STATE, kb):
            re, im = pl.ds(c, kb), pl.ds(N_STATE + c, kb)
            xr, xi = buf_x[pl.ds(0, t - 8), re], buf_x[pl.ds(0, t - 8), im]
            ar, ai = buf_a[pl.ds(8, t - 8), re], buf_a[pl.ds(8, t - 8), im]
            x0r, x0i = sx_ref[:, re], sx_ref[:, im]
            a0r, a0i = buf_a[0:8, re], buf_a[0:8, im]
            dlam_part_re = (jnp.sum(ar * xr + ai * xi, 0, keepdims=True)
                            + jnp.sum(a0r * x0r + a0i * x0i, 0, keepdims=True))
            dlam_part_im = (jnp.sum(ai * xr - ar * xi, 0, keepdims=True)
                            + jnp.sum(a0i * x0r - a0r * x0i, 0, keepdims=True))

            @pl.when(pl.program_id(0) == 0)
            def _(c=c):
                dlam_ref[0:1, c:c + kb] = jnp.zeros((1, kb), F32)
                dlam_ref[1:2, c:c + kb] = jnp.zeros((1, kb), F32)

            dlam_ref[0:1, c:c + kb] += dlam_part_re
            dlam_ref[1:2, c:c + kb] += dlam_part_im
        _acc(dd_ref, jnp.sum(dy_t * u_t, 0, keepdims=True))

        @pl.when(pl.program_id(0) == nc - 1)
        def _():
            pltpu.sync_copy(db_acc, db_ref)
            pltpu.sync_copy(dct_acc, dct_ref)

    rev = lambda i: (nc - 1 - i, 0)
    consts = [perm, bblk, cblk, lam, d_row]
    wide = (SSM_WIDTH, 2 * N_STATE)
    return _call(
        body, "ssm_bwd", (nc,), [u, dy, st] + consts,
        [pl.BlockSpec((t, SSM_WIDTH), rev), pl.BlockSpec((t, SSM_WIDTH), rev),
         pl.BlockSpec((1, 8, 2 * N_STATE), lambda i: (nc - 1 - i, 0, 0))] + [_const(a) for a in consts],
        [_sds((n, SSM_WIDTH), BF16), _sds((2, N_STATE), F32), _sds((1, SSM_WIDTH), F32), _sds(wide, F32), _sds(wide, F32)],
        [pl.BlockSpec((t, SSM_WIDTH), rev), pl.BlockSpec((2, N_STATE), lambda i: (0, 0)),
         pl.BlockSpec((1, SSM_WIDTH), lambda i: (0, 0)), ANY, ANY],
        scratch=[pltpu.VMEM((t, 2 * N_STATE), F32)] * 2 + [pltpu.VMEM((t // SUBCHUNKS, 2 * N_STATE), F32)]
        + [pltpu.VMEM((8, 2 * N_STATE), F32)] * 4 + [pltpu.VMEM(wide, F32)] * 2,
        xch=xch)


def _in_proj_bwd(pieces, dh, x, xn_t, norm_mix, w_in_pad, xch):
    n = x.shape[0]
    tm = min(MM_T, n)
    nt = n // tm

    def body(du_ref, dql_ref, dkvl_ref, dgs_ref, dgm_ref, dh_ref, x_ref, xnt_ref, g_ref, w_ref,
             dx_ref, dg_ref, dw_ref, acc_ref):
        @pl.when(pl.program_id(0) == 0)
        def _():
            acc_ref[...] = jnp.zeros_like(acc_ref)

        xnt = xnt_ref[...]
        dxn = jnp.zeros((tm, D_MODEL), F32)
        for ref, (a, b) in zip((du_ref, dql_ref, dkvl_ref, dgs_ref, dgm_ref), IN_SEGS):
            piece = ref[...]
            dxn += _dot_nt(piece, w_ref[:, a:b])
            acc_ref[:, a:b] += _dot(xnt, piece)
        x_t = x_ref[...]
        inv = lax.rsqrt(jnp.sum(x_t * x_t, -1, keepdims=True) * (1.0 / D_MODEL) + EPS)
        dx, dg = _rms_bwd(dxn, x_t, g_ref[...], inv, D_MODEL)
        dx_ref[...] = dh_ref[...] + dx
        _acc(dg_ref, jnp.sum(dg, 0, keepdims=True))

        @pl.when(pl.program_id(0) == nt - 1)
        def _():
            pltpu.sync_copy(acc_ref, dw_ref)

    row_ins, consts = list(pieces) + [dh, x], [norm_mix, w_in_pad]
    in_specs = ([_rows(a, tm) for a in row_ins] + [pl.BlockSpec((D_MODEL, tm), lambda i: (0, i))]
                + [_const(a) for a in consts])
    return _call(
        body, "in_proj_bwd", (nt,), row_ins + [xn_t] + consts, in_specs,
        [_sds((n, D_MODEL), F32), _sds((1, D_MODEL), F32), _sds((D_MODEL, D_IN_PAD), F32)],
        [pl.BlockSpec((tm, D_MODEL), lambda i: (i, 0)), pl.BlockSpec((1, D_MODEL), lambda i: (0, 0)), ANY],
        scratch=[pltpu.VMEM((D_MODEL, D_IN_PAD), F32)], xch=xch)


def _swap_minor(a):
    g, r, c = a.shape[1:]
    return jnp.transpose(a[0], (0, 2, 1)).reshape(g * c, r)


def _pad_in(w):
    return jnp.concatenate([w[:, :KV_END], jnp.zeros((w.shape[0], D_IN_PAD - D_IN), w.dtype), w[:, KV_END:]], axis=1)


def _unpad_in(w):
    return jnp.concatenate([w[:, :KV_END], w[:, KV_END + D_IN_PAD - D_IN:]], axis=1)


def _pad_gain(g):
    return jnp.pad(g, ((0, 0), (0, QK_PAD - QK_HEAD)))


def _place():
    x, y, c = lax.axis_index("x"), lax.axis_index("y"), lax.axis_index("c")
    chips = [(x, y), (1 - x, y), (x, 1 - y), (1 - x, 1 - y)]
    return x, y, c, chips


def _all_gather(block, name):
    rows, lanes = block.shape

    def body(x_ref, out_ref, send_sems, recv_sems, local_sem):
        x, y, c, chips = _place()
        me, sibling = (x, y, c), (x, y, 1 - c)

        def slot(px, py, pc):
            return out_ref.at[4 * px + 2 * py + pc]

        def copy(k, blk, to, src=None):
            return pltpu.make_async_remote_copy(
                src_ref=slot(*blk) if src is None else src, dst_ref=slot(*blk),
                send_sem=send_sems.at[k], recv_sem=recv_sems.at[k], device_id=to, device_id_type=MESH)

        mine = pltpu.make_async_copy(x_ref, slot(*me), local_sem)
        mine.start()
        first = [copy(0, me, sibling, src=x_ref)]
        first += [copy(1 + j, me, (*chip, c), src=x_ref) for j, chip in enumerate(chips[1:])]
        for cp in first:
            cp.start()
        passed = [copy(4 + j, (*chip, c), sibling) for j, chip in enumerate(chips[1:])]
        for j, chip in enumerate(chips[1:]):
            copy(1 + j, (*chip, c), me).wait_recv()
            passed[j].start()
        copy(0, sibling, me).wait_recv()
        for j, chip in enumerate(chips[1:]):
            copy(4 + j, (*chip, 1 - c), me).wait_recv()
        for cp in first + passed:
            cp.wait_send()
        mine.wait()

    return pl.pallas_call(
        body,
        name=name,
        in_specs=[ANY],
        out_specs=ANY,
        out_shape=_sds((N_DEV, rows, lanes), block.dtype),
        scratch_shapes=[pltpu.SemaphoreType.DMA((7,)), pltpu.SemaphoreType.DMA((7,)), pltpu.SemaphoreType.DMA],
    )(block)


def _reduce_scatter(parts, gather, name):
    _, rows, lanes = parts.shape

    def body(p_ref, g_ref, out_ref, ga_ref, own, land_a, send_b, land_b, sa, ra, sb, rb, lo, *g_sems):
        x, y, c, chips = _place()
        sibling = (x, y, 1 - c)
        _xchg_start([False], [g_ref], [ga_ref], *g_sems)

        def blk(chip, core):
            return p_ref.at[4 * chip[0] + 2 * chip[1] + core]

        to_sib = [pltpu.make_async_remote_copy(
            src_ref=blk(chips[k], 1 - c), dst_ref=land_a.at[k], send_sem=sa.at[k], recv_sem=ra.at[k],
            device_id=sibling, device_id_type=MESH) for k in range(4)]
        for cp in to_sib:
            cp.start()
        loads = [pltpu.make_async_copy(blk(chips[k], c), own.at[k], lo.at[k]) for k in range(4)]
        for cp in loads:
            cp.start()
        to_chip = [pltpu.make_async_remote_copy(
            src_ref=send_b.at[j], dst_ref=land_b.at[j], send_sem=sb.at[j], recv_sem=rb.at[j],
            device_id=(*chips[1 + j], c), device_id_type=MESH) for j in range(3)]
        for k in (1, 2, 3):
            to_sib[k].wait_recv()
            loads[k].wait()
            send_b[k - 1] = (own[k] + land_a[k]).astype(BF16)
            to_chip[k - 1].start()
        to_sib[0].wait_recv()
        loads[0].wait()
        acc = own[0] + land_a[0]
        for j in range(3):
            to_chip[j].wait_recv()
            acc = acc + land_b[j].astype(F32)
        out_ref[...] = acc
        for cp in to_sib + to_chip:
            cp.wait_send()
        _xchg_wait([False], [g_ref], [ga_ref], *g_sems)

    return pl.pallas_call(
        body,
        name=name,
        in_specs=[ANY, ANY],
        out_specs=[pl.BlockSpec(memory_space=pltpu.VMEM), ANY],
        out_shape=[_sds((rows, lanes), F32), _sds((N_DEV,) + gather.shape, gather.dtype)],
        scratch_shapes=[pltpu.VMEM((4, rows, lanes), F32), pltpu.VMEM((4, rows, lanes), F32),
                        pltpu.VMEM((3, rows, lanes), BF16), pltpu.VMEM((3, rows, lanes), BF16)]
        + [pltpu.SemaphoreType.DMA((4,))] * 2 + [pltpu.SemaphoreType.DMA((3,))] * 2 + [pltpu.SemaphoreType.DMA((4,))]
        + [pltpu.SemaphoreType.DMA((1,))] * 3,
        compiler_params=_params(),
    )(parts, gather)


def _adamw_math(w, g, m, v):
    m = ADAM_B1 * m + (1.0 - ADAM_B1) * g
    v = ADAM_B2 * v + (1.0 - ADAM_B2) * (g * g)
    m_hat = m / (1.0 - ADAM_B1 ** ADAM_STEP)
    v_hat = v / (1.0 - ADAM_B2 ** ADAM_STEP)
    delta = -ADAM_LR * (m_hat / (jnp.sqrt(v_hat) + ADAM_EPS) + ADAM_WD * w)
    return delta, m, v


def _row_tile(r):
    return max(t for t in range(8, min(r, 256) + 1, 8) if r % t == 0)


def _adamw(w, g, m, v, name):
    r, n = w.shape

    def body(w_ref, g_ref, m_ref, v_ref, d_ref, nm_ref, nv_ref):
        d_ref[...], nm_ref[...], nv_ref[...] = _adamw_math(w_ref[...], g_ref[...], m_ref[...], v_ref[...])

    return _row_call(body, name, r, _row_tile(r), [w, g, m, v], [], [((r, n), F32)] * 3)


def _adamw_sum(landed, w, m, v, name):
    r, n = w.shape

    def body(l_ref, w_ref, m_ref, v_ref, g_ref, d_ref, nm_ref, nv_ref):
        g = l_ref[0].astype(F32)
        for dev in range(1, N_DEV):
            g = g + l_ref[dev].astype(F32)
        g_ref[...] = g
        d_ref[...], nm_ref[...], nv_ref[...] = _adamw_math(w_ref[...], g, m_ref[...], v_ref[...])

    tm = max(t for t in range(16, min(r, 256) + 1, 16) if r % t == 0)
    return _row_call(body, name, r, tm, [landed, w, m, v], [], [((r, n), F32)] * 4)


def _adamw_small(first, rest, w, m, v, row_counts):
    n_rest = w.shape[0] - first.shape[1]

    def body(f_ref, r_ref, w_ref, m_ref, v_ref, loss_ref, *out_refs):
        gf, gr = f_ref[0], r_ref[0]
        for dev in range(1, N_DEV):
            gf, gr = gf + f_ref[dev], gr + r_ref[dev]
        loss_ref[...] = gr[n_rest:n_rest + 8]
        g = jnp.concatenate([gf, gr[0:n_rest]], axis=0)
        d, nm, nv = _adamw_math(w_ref[...], g, m_ref[...], v_ref[...])
        off = 0
        for p, rows in enumerate(row_counts):
            for k, val in enumerate((g, d, nm, nv)):
                out_refs[4 * p + k][...] = val[off:off + rows]
            off += rows

    outs = [_sds((8, LANES), F32)] + [_sds((rows, LANES), F32) for rows in row_counts for _ in range(4)]
    return pl.pallas_call(body, name="adamw_small", out_shape=outs, compiler_params=_params())(first, rest, w, m, v)


SMALL = ("norm_mix", "q_a_norm", "kv_a_norm", "q_norm", "k_norm", "ssm_a_re", "ssm_a_im", "ssm_log_dt", "ssm_b_re",
         "ssm_b_im", "ssm_c_re", "ssm_c_im", "ssm_d", "b_glu", "norm_mlp")
WEIGHT_ORDER = ("norm_mix", "w_in", "q_a_norm", "kv_a_norm", "w_q_b", "w_kv_b", "q_norm", "k_norm", "w_o_mla",
                "ssm_a_re", "ssm_a_im", "ssm_log_dt", "ssm_b_re", "ssm_b_im", "ssm_c_re", "ssm_c_im", "ssm_d", "w_glu",
                "b_glu", "w_o_ssm", "w_out", "norm_mlp", "w_up", "w_down")
IN_SHARD = D_IN // N_DEV


def _pack_small(vals, names=SMALL):
    parts = []
    for n in names:
        flat = vals[n].reshape(-1)
        size = -(-flat.shape[0] // (8 * LANES)) * 8 * LANES
        parts.append(jnp.pad(flat, (0, size - flat.shape[0])).reshape(-1, LANES))
    return jnp.concatenate(parts, axis=0)


def _small_rows(like):
    return [-(-like[n].size // (8 * LANES)) * 8 for n in SMALL]


def _step(x, pos_col, target, w, small):
    bf = {n: a.astype(BF16) for n, a in w.items()}
    gq, gk = _pad_gain(small["q_norm"]), _pad_gain(small["k_norm"])
    a_re = small["ssm_a_re"].reshape(1, N_STATE)
    a_im = small["ssm_a_im"].reshape(1, N_STATE)
    log_dt = jnp.repeat(small["ssm_log_dt"].reshape(SSM_GROUPS), SSM_STATE).reshape(1, N_STATE)
    bt_re, bt_im = _swap_minor(small["ssm_b_re"]), _swap_minor(small["ssm_b_im"])
    c2_re, c2_im = _swap_minor(small["ssm_c_re"]), _swap_minor(small["ssm_c_im"])
    d_row = small["ssm_d"].reshape(1, SSM_WIDTH)

    w_in_all = _all_gather(bf["w_in"], "gather_w_in")
    w_in_pad = _pad_in(jnp.transpose(w_in_all, (1, 0, 2)).reshape(D_MODEL, D_IN))
    cos_t, sin_t = _rope_tables(pos_col)
    lam, bblk, cblk = _ssm_prep(a_re, a_im, log_dt, bt_re, bt_im, c2_re, c2_im)
    wq_mine = jnp.pad(bf["w_q_b"], ((0, 0), (0, QK_PAD - QK_HEAD)))
    u, ql, kvl, gs, gm, xn_t, w_glu, w_o_ssm = _in_proj(
        x, small["norm_mix"], w_in_pad, xch=[(bf["w_glu"], False), (bf["w_o_ssm"], False)])
    w_glu = w_glu.reshape(SSM_WIDTH, SSM_WIDTH)
    y, y_ssm, st, wq, wkv, w_o_mla, w_out = _ssm_fwd(
        u, bblk, cblk, lam, d_row, w_glu, small["b_glu"], w_o_ssm,
        xch=[(wq_mine, False), (bf["w_kv_b"], False), (bf["w_o_mla"], False), (bf["w_out"], False)])
    w_o_mla, w_out = w_o_mla.reshape(D_MODEL, D_MODEL), w_out.reshape(D_MODEL, D_MODEL)
    wq = jnp.transpose(wq, (1, 0, 2)).reshape(Q_LORA, N_HEADS * QK_PAD)
    wkv = jnp.transpose(wkv, (1, 0, 2)).reshape(KV_LORA, N_HEADS * QK_PAD)
    q, k, v, kt, vt = _qkv_prep(ql, kvl, small["q_a_norm"], small["kv_a_norm"], wq, wkv, gq, gk, cos_t, sin_t)
    attn, lse, attn_t, w_up, w_down = _attn_fwd(q, k, vt, xch=[(bf["w_up"], False), (bf["w_down"], False)])
    h, y_mla, mixed_t = _merge(attn, gs, gm, y_ssm, x, w_o_mla, w_out)
    hn, dout, hn_t, loss = _mlp_fwd_loss(h, target, small["norm_mlp"], w_up, w_down)

    da, dh, dout_b, hid_t, d_norm_mlp = _mlp_bwd(dout, hn, h, small["norm_mlp"], w_up, w_down)
    p_w_down = _matmul_tn_shards(hid_t, dout_b, "dw_down", False, tm=1024, turned=True)
    p_w_up = _matmul_tn_shards(hn_t, da, "dw_up", True, turned=True)
    dgs, dgm, dy_ssm, dy_mla, dattn = _merge_bwd(dh, gs, gm, y_ssm, y_mla, w_out, w_o_mla)
    p_w_out = _matmul_tn_shards(mixed_t, dh, "dw_out", False, tm=1024, turned=True)
    p_w_o_mla = _matmul_tn_shards(attn_t, dy_mla, "dw_o_mla", False, turned=True)
    dq, dk, dv, l_w_up, l_w_down, l_w_out, l_w_o_mla = _attn_bwd(
        q, k, kt, v, attn, lse, dattn, xch=[(p_w_up, True), (p_w_down, True), (p_w_out, True), (p_w_o_mla, True)])
    dql, dkvl, d_q_a_norm, d_kv_a_norm, d_gq, d_gk, g_wq, g_wkv = _qkv_prep_bwd(
        ql, kvl, dq, dk, dv, small["q_a_norm"], small["kv_a_norm"], wq, wkv, gq, gk, cos_t, sin_t, xch=[])
    p_wq = jnp.transpose(g_wq.reshape(Q_LORA, N_HEADS, QK_PAD), (1, 0, 2)).astype(BF16)
    p_wkv = jnp.transpose(g_wkv.reshape(KV_LORA, N_HEADS, QK_PAD), (1, 0, 2)).astype(BF16)
    dy, d_b_glu, g_w_glu, g_w_o_ssm = _glu_bwd(dy_ssm, y, w_glu, small["b_glu"], w_o_ssm)
    p_w_o_ssm = jnp.transpose(g_w_o_ssm.reshape(SSM_WIDTH, N_DEV, OUT_SHARD), (1, 0, 2)).astype(BF16)
    p_w_glu = g_w_glu.reshape(N_DEV, SSM_WIDTH // N_DEV, SSM_WIDTH).astype(BF16)
    du, dlam, d_d, d_bblk, d_cblk_t, l_wq, l_wkv, l_w_glu, l_w_o_ssm = _ssm_bwd(
        u, dy, st, bblk, cblk, lam, d_row, xch=[(p_wq, True), (p_wkv, True), (p_w_glu, True), (p_w_o_ssm, True)])
    d_a_re, d_a_im, d_log_dt, d_bt_re, d_bt_im, d_c_re, d_c_im = _ssm_prep_bwd(
        a_re, a_im, log_dt, bt_re, bt_im, dlam, d_bblk, d_cblk_t)
    tr = lambda mat: jnp.transpose(mat.reshape(SSM_GROUPS, SSM_GROUP_CH, SSM_STATE), (0, 2, 1))
    g_small = {
        "q_a_norm": d_q_a_norm, "kv_a_norm": d_kv_a_norm, "q_norm": d_gq[:, :QK_HEAD], "k_norm": d_gk[:, :QK_HEAD],
        "ssm_a_re": d_a_re, "ssm_a_im": d_a_im, "ssm_log_dt": d_log_dt,
        "ssm_b_re": tr(d_bt_re), "ssm_b_im": tr(d_bt_im), "ssm_c_re": d_c_re, "ssm_c_im": d_c_im,
        "ssm_d": d_d, "b_glu": d_b_glu, "norm_mlp": d_norm_mlp,
    }
    rest = jnp.concatenate([_pack_small(g_small, SMALL[1:]), loss], axis=0)
    dx, d_norm_mix, g_w_in_pad, g_rest_all = _in_proj_bwd(
        (du, dql, dkvl, dgs, dgm), dh, x, xn_t, small["norm_mix"], w_in_pad, xch=[(rest, False)])
    parts = jnp.transpose(_unpad_in(g_w_in_pad).reshape(D_MODEL, N_DEV, IN_SHARD), (1, 0, 2))
    g_w_in_mine, g_first_all = _reduce_scatter(parts, _pack_small({SMALL[0]: d_norm_mix}, SMALL[:1]), "reduce_w_in")
    landed = {"w_q_b": l_wq[:, :, :QK_HEAD], "w_kv_b": l_wkv, "w_o_mla": l_w_o_mla, "w_glu": l_w_glu,
              "w_o_ssm": l_w_o_ssm, "w_out": l_w_out, "w_up": l_w_up, "w_down": l_w_down}
    return dx, landed, g_w_in_mine, g_first_all, g_rest_all


def kernel(x, positions, norm_mix, w_in, q_a_norm, kv_a_norm, w_q_b, w_kv_b, q_norm, k_norm, w_o_mla, ssm_a_re, ssm_a_im, ssm_log_dt, ssm_b_re, ssm_b_im, ssm_c_re, ssm_c_im, ssm_d, w_glu, b_glu, w_o_ssm, w_out, norm_mlp, w_up, w_down, loss_target, m_norm_mix, m_w_in, m_q_a_norm, m_kv_a_norm, m_w_q_b, m_w_kv_b, m_q_norm, m_k_norm, m_w_o_mla, m_ssm_a_re, m_ssm_a_im, m_ssm_log_dt, m_ssm_b_re, m_ssm_b_im, m_ssm_c_re, m_ssm_c_im, m_ssm_d, m_w_glu, m_b_glu, m_w_o_ssm, m_w_out, m_norm_mlp, m_w_up, m_w_down, v_norm_mix, v_w_in, v_q_a_norm, v_kv_a_norm, v_w_q_b, v_w_kv_b, v_q_norm, v_k_norm, v_w_o_mla, v_ssm_a_re, v_ssm_a_im, v_ssm_log_dt, v_ssm_b_re, v_ssm_b_im, v_ssm_c_re, v_ssm_c_im, v_ssm_d, v_w_glu, v_b_glu, v_w_o_ssm, v_w_out, v_norm_mlp, v_w_up, v_w_down):
    given = dict(locals())
    w = {n: given[n] for n in WEIGHT_ORDER}
    m = {n: given["m_" + n] for n in WEIGHT_ORDER}
    v = {n: given["v_" + n] for n in WEIGHT_ORDER}
    big = [n for n in WEIGHT_ORDER if n not in SMALL]
    small = {n: w[n] for n in SMALL}

    dx, landed, g_w_in, g_first_all, g_rest_all = _step(
        x[0], positions.reshape(-1, 1), loss_target[0], {n: w[n][0] for n in big}, small)

    grads, deltas, new_m, new_v = {}, {}, {}, {}
    for n in big:
        if n in ("w_in", "w_q_b"):
            wt, mt, vt = jnp.transpose(w[n][0]), jnp.transpose(m[n][0]), jnp.transpose(v[n][0])
            if n == "w_in":
                g = jnp.transpose(g_w_in)
                d, nm, nv = _adamw(wt, g, mt, vt, "adamw_" + n)
            else:
                g, d, nm, nv = _adamw_sum(jnp.transpose(landed[n], (0, 2, 1)), wt, mt, vt, "adamw_" + n)
            g, d, nm, nv = (jnp.transpose(a) for a in (g, d, nm, nv))
        else:
            g, d, nm, nv = _adamw_sum(landed[n], w[n][0], m[n][0], v[n][0], "adamw_" + n)
        grads[n], deltas[n], new_m[n], new_v[n] = g[None], d[None], nm[None], nv[None]

    outs = _adamw_small(g_first_all, g_rest_all, _pack_small(small), _pack_small({n: m[n] for n in SMALL}),
                        _pack_small({n: v[n] for n in SMALL}), _small_rows(small))
    for p, n in enumerate(SMALL):
        for k, dst in enumerate((grads, deltas, new_m, new_v)):
            dst[n] = outs[1 + 4 * p + k].reshape(-1)[:small[n].size].reshape(small[n].shape)

    return (outs[0][0, 0], dx[None], *[grads[n] for n in WEIGHT_ORDER], *[deltas[n] for n in WEIGHT_ORDER],
            *[new_m[n] for n in WEIGHT_ORDER], *[new_v[n] for n in WEIGHT_ORDER])
```

```python
import functools
import math

import numpy as np
import jax
import jax.numpy as jnp
from jax import lax
from jax.experimental import pallas as pl
from jax.experimental.pallas import tpu as pltpu

F32 = jnp.float32
BF16 = jnp.bfloat16

D_MODEL = 1024
SSM_GROUPS = 32
SSM_GROUP_CH = 16
SSM_WIDTH = 512
SSM_STATE = 64
N_STATE = SSM_GROUPS * SSM_STATE
N_HEADS = 8
QK_NOPE = 128
QK_ROPE = 64
QK_HEAD = 192
QK_PAD = 256
V_HEAD = 128
Q_LORA = 384
KV_LORA = 256
KV_LAT_PAD = 384
ROPE_THETA = 10000.0
D_FF = 4096
EPS = 1e-6
ATT_SCALE = QK_HEAD ** -0.5
N_DEV = 8
FF_SHARD = D_FF // N_DEV
OUT_SHARD = D_MODEL // N_DEV

IN_SEGS = ((0, 512), (512, 896), (896, 1280), (1280, 2304), (2304, 3328))
D_IN = 3264
D_IN_PAD = 3328
KV_END = 1216

ADAM_LR = 0.001
ADAM_B1 = 0.9
ADAM_B2 = 0.999
ADAM_EPS = 1e-08
ADAM_WD = 0.01
ADAM_STEP = 10

VMEM_LIMIT = 56 * 1024 * 1024
MESH = pl.DeviceIdType.MESH
ANY = pl.BlockSpec(memory_space=pl.ANY)
LANES = 128

SCAN_T = 256
SUBCHUNKS = 8
SCAN_CG = 512
ATT_T = 512
ATT_SUB = 1
ATT_HEADS = 4
ROW_T = 256
MM_T = 512


def _params(sem=None):
    return pltpu.CompilerParams(dimension_semantics=sem, vmem_limit_bytes=VMEM_LIMIT)


def _rows(arr, tm):
    if arr.ndim == 2:
        return pl.BlockSpec((tm, arr.shape[1]), lambda i: (i, 0))
    return pl.BlockSpec((arr.shape[0], tm, arr.shape[2]), lambda i: (0, i, 0))


def _const(arr):
    nd = arr.ndim
    return pl.BlockSpec(arr.shape, lambda *_: (0,) * nd, pipeline_mode=pl.Buffered(1))


def _sds(shape, dtype):
    return jax.ShapeDtypeStruct(shape, dtype)


PEERS = tuple((dx, dy, dc) for dx in (0, 1) for dy in (0, 1) for dc in (0, 1) if (dx, dy, dc) != (0, 0, 0))


def _here():
    x, y, c = lax.axis_index("x"), lax.axis_index("y"), lax.axis_index("c")
    return x, y, c, 4 * x + 2 * y + c


def _xchg_start(scatter, srcs, dsts, send, recv, local):
    x, y, c, me = _here()
    for e, sc in enumerate(scatter):
        src, dst = srcs[e], dsts[e]
        pltpu.make_async_copy(src.at[me] if sc else src, dst.at[me], local.at[e]).start()
        for dx, dy, dc in PEERS:
            px, py, pc = (1 - x if dx else x), (1 - y if dy else y), (1 - c if dc else c)
            pltpu.make_async_remote_copy(
                src_ref=src.at[4 * px + 2 * py + pc] if sc else src, dst_ref=dst.at[me],
                send_sem=send.at[e], recv_sem=recv.at[e], device_id=(px, py, pc), device_id_type=MESH).start()


def _xchg_wait(scatter, srcs, dsts, send, recv, local):
    x, y, c, me = _here()
    for e, sc in enumerate(scatter):
        src, dst = srcs[e], dsts[e]
        pltpu.make_async_copy(src.at[me] if sc else src, dst.at[me], local.at[e]).wait()
        span = dst.at[pl.ds(0, N_DEV - 1)]
        both = pltpu.make_async_remote_copy(src_ref=span, dst_ref=span, send_sem=send.at[e], recv_sem=recv.at[e],
                                            device_id=(x, y, c), device_id_type=MESH)
        both.wait_send()
        both.wait_recv()


def _call(body, name, grid, ins, in_specs, outs, out_specs, scratch=(), xch=()):
    n_in, n_out, ne = len(ins), len(outs), len(xch)
    scatter = [sc for _, sc in xch]
    x_outs = [_sds((N_DEV,) + (a.shape[1:] if sc else a.shape), a.dtype) for a, sc in xch]
    sems = [pltpu.SemaphoreType.DMA((ne,))] * 3 if ne else []

    def wrapped(*refs):
        in_refs, x_src = refs[:n_in], refs[n_in:n_in + ne]
        out_refs = refs[n_in + ne:n_in + ne + n_out]
        x_dst = refs[n_in + ne + n_out:n_in + 2 * ne + n_out]
        rest = refs[n_in + 2 * ne + n_out:]
        if ne:
            x_sems, rest = rest[len(rest) - 3:], rest[:len(rest) - 3]
            first = functools.reduce(jnp.logical_and, [pl.program_id(d) == 0 for d in range(len(grid))])
            last = functools.reduce(jnp.logical_and, [pl.program_id(d) == grid[d] - 1 for d in range(len(grid))])

            @pl.when(first)
            def _():
                _xchg_start(scatter, x_src, x_dst, *x_sems)

        body(*in_refs, *out_refs, *rest)
        if ne:
            @pl.when(last)
            def _():
                _xchg_wait(scatter, x_src, x_dst, *x_sems)

    return pl.pallas_call(
        wrapped,
        name=name,
        grid=grid,
        in_specs=list(in_specs) + [ANY] * ne,
        out_specs=list(out_specs) + [ANY] * ne,
        out_shape=list(outs) + x_outs,
        scratch_shapes=list(scratch) + sems,
        compiler_params=_params(("arbitrary",) * len(grid)),
    )(*ins, *[a for a, _ in xch])


def _row_call(body, name, n_rows, tm, row_ins, const_ins, row_outs, acc_outs=(), xch=(), col_outs=(), scratch=()):
    outs = [_sds(s, d) for s, d in list(row_outs) + list(col_outs) + list(acc_outs)]
    n_row, n_col = len(row_outs), len(col_outs)
    out_specs = [_rows(o, tm) for o in outs[:n_row]] + [
        pl.BlockSpec((o.shape[0], tm), lambda i: (0, i)) for o in outs[n_row:n_row + n_col]] + [
        pl.BlockSpec(o.shape, lambda i, nd=len(o.shape): (0,) * nd) for o in outs[n_row + n_col:]]
    in_specs = [_rows(a, tm) for a in row_ins] + [_const(a) for a in const_ins]
    return _call(body, name, (n_rows // tm,), list(row_ins) + list(const_ins), in_specs, outs, out_specs,
                 scratch=scratch, xch=xch)


def _dot(a, b):
    return jnp.dot(a, b, preferred_element_type=F32)


def _dot_nt(a, b):
    return lax.dot_general(a, b, (((1,), (1,)), ((), ())), preferred_element_type=F32)


def _dot_tn(a, b):
    return lax.dot_general(a, b, (((0,), (0,)), ((), ())), preferred_element_type=F32)


def _rms(x, g, n):
    inv = lax.rsqrt(jnp.sum(x * x, -1, keepdims=True) * (1.0 / n) + EPS)
    return x * inv * g, inv


def _rms_bwd(dy, x, g, inv, n):
    xh = x * inv
    dxh = dy * g
    dx = inv * (dxh - xh * (jnp.sum(dxh * xh, -1, keepdims=True) * (1.0 / n)))
    return dx, dy * xh


def _sigmoid(x):
    return 1.0 / (1.0 + jnp.exp(-x))


_GELU_C = math.sqrt(2.0 / math.pi)


def _gelu(y):
    th = jnp.tanh(_GELU_C * (y + 0.044715 * (y * y * y)))
    return 0.5 * y * (1.0 + th), th


def _gelu_grad(y, th):
    return 0.5 * (1.0 + th) + 0.5 * y * (1.0 - th * th) * (_GELU_C * (1.0 + 3.0 * 0.044715 * (y * y)))


def _acc(ref, val):
    @pl.when(pl.program_id(0) == 0)
    def _():
        ref[...] = jnp.zeros_like(ref)

    ref[...] += val


def _tile(n, limit):
    if n <= limit:
        return n
    return max(t for t in range(128, limit + 1, 128) if n % t == 0)


def _lhs(a, turned, tm, tk):
    m, k_dim = a.shape if turned else a.shape[::-1]
    tm, tk = _tile(m, tm), _tile(k_dim, tk)
    if turned:
        return m, k_dim, tm, tk, pl.BlockSpec((tm, tk), lambda i, k: (i, k)), _dot
    return m, k_dim, tm, tk, pl.BlockSpec((tk, tm), lambda i, k: (k, i)), _dot_tn


def _matmul_tn_shards(a, b, name, by_col, tm=512, tk=512, turned=False):
    m, k_dim, tm, tk, a_spec, dot = _lhs(a, turned, tm, tk)
    n = b.shape[1]
    nk = k_dim // tk
    if by_col:
        r, c = m, n // N_DEV
        out_spec = pl.BlockSpec((N_DEV, tm, c), lambda i, k: (0, i, 0))
    else:
        r, c = m // N_DEV, n
        per = tm // r
        out_spec = pl.BlockSpec((per, r, c), lambda i, k: (i, 0, 0))

    def body(a_ref, b_ref, o_ref, acc_ref):
        k = pl.program_id(1)

        @pl.when(k == 0)
        def _():
            acc_ref[...] = jnp.zeros_like(acc_ref)

        acc_ref[...] += dot(a_ref[...].astype(BF16), b_ref[...].astype(BF16))

        @pl.when(k == nk - 1)
        def _():
            if by_col:
                for j in range(N_DEV):
                    o_ref[j] = acc_ref[:, j * c:(j + 1) * c].astype(BF16)
            else:
                for s in range(per):
                    o_ref[s] = acc_ref[s * r:(s + 1) * r, :].astype(BF16)

    return pl.pallas_call(
        body,
        name=name,
        grid=(m // tm, nk),
        in_specs=[a_spec, pl.BlockSpec((tk, n), lambda i, k: (k, 0))],
        out_specs=out_spec,
        out_shape=_sds((N_DEV, r, c), BF16),
        scratch_shapes=[pltpu.VMEM((tm, n), F32)],
        compiler_params=_params(("parallel", "arbitrary")),
    )(a, b)


def _rope_tables(pos_col):
    n = pos_col.shape[0]
    half = QK_ROPE // 2
    inv_freq = (ROPE_THETA ** (-np.arange(half, dtype=np.float32) / half)).astype(np.float32)
    freq_row = jnp.asarray(np.concatenate([inv_freq, inv_freq, np.zeros(64, np.float32)])[None, :])

    def body(p_ref, f_ref, c_ref, s_ref):
        ang = p_ref[...].astype(F32) * f_ref[...]
        c_ref[...] = jnp.cos(ang)
        s_ref[...] = jnp.sin(ang)

    return _row_call(body, "rope_tables", n, min(n, 1024), [pos_col], [freq_row], [((n, 128), F32)] * 2)


def _rope_rot(v):
    lane = lax.broadcasted_iota(jnp.int32, v.shape, 1)
    return jnp.where(lane < 32, -pltpu.roll(v, 96, 1), jnp.where(lane < 64, pltpu.roll(v, 32, 1), 0.0))


def _rope_rot_t(v):
    lane = lax.broadcasted_iota(jnp.int32, v.shape, 1)
    return jnp.where(lane < 32, pltpu.roll(v, 96, 1), jnp.where(lane < 64, -pltpu.roll(v, 32, 1), 0.0))


def _in_proj(x, norm_mix, w_in_pad, xch):
    n = x.shape[0]

    def body(x_ref, g_ref, w_ref, u_ref, ql_ref, kvl_ref, gs_ref, gm_ref, xnt_ref):
        xn, _ = _rms(x_ref[...], g_ref[...], D_MODEL)
        xb = xn.astype(BF16)
        xnt_ref[...] = xn.T.astype(BF16)
        for ref, (a, b) in zip((u_ref, ql_ref, kvl_ref, gs_ref, gm_ref), IN_SEGS):
            ref[...] = _dot(xb, w_ref[:, a:b])

    outs = [((n, b - a), F32) for a, b in IN_SEGS]
    return _row_call(body, "in_proj", n, MM_T, [x], [norm_mix, w_in_pad], outs, xch=xch,
                     col_outs=[((D_MODEL, n), BF16)])


def _ssm_prep_fn(a_re, a_im, log_dt, b_re_x, b_im_x):
    dt = jnp.exp(log_dt)
    mag = jnp.exp(a_re * dt)
    lr = mag * jnp.cos(a_im * dt)
    li = mag * jnp.sin(a_im * dt)
    den = a_re * a_re + a_im * a_im
    fr = ((lr - 1.0) * a_re + li * a_im) / den
    fi = (li * a_re - (lr - 1.0) * a_im) / den
    return lr, li, fr * b_re_x - fi * b_im_x, fr * b_im_x + fi * b_re_x


def _dot_exact(a, b, dims):
    return lax.dot_general(a, b, (dims, ((), ())), precision=lax.Precision.HIGHEST, preferred_element_type=F32)


def _lane_repeat(width, n):
    src = lax.broadcasted_iota(jnp.int32, (width, n), 0)
    dst = lax.broadcasted_iota(jnp.int32, (width, n), 1)
    return (dst % width == src).astype(F32)


def _same_group(rows, rows_per_group, cols, cols_per_group):
    row = lax.broadcasted_iota(jnp.int32, (rows, cols), 0)
    col = lax.broadcasted_iota(jnp.int32, (rows, cols), 1)
    return (row // rows_per_group) == (col // cols_per_group)


def _expand_b(bt):
    tiled = _dot_exact(bt, _lane_repeat(SSM_STATE, N_STATE), ((1,), (0,)))
    return jnp.where(_same_group(SSM_WIDTH, SSM_GROUP_CH, N_STATE, SSM_STATE), tiled, 0.0)


def _collect_b(m):
    masked = jnp.where(_same_group(SSM_WIDTH, SSM_GROUP_CH, N_STATE, SSM_STATE), m, 0.0)
    return _dot_exact(masked, _lane_repeat(SSM_STATE, N_STATE), ((1,), (1,)))


def _ssm_prep(a_re, a_im, log_dt, bt_re, bt_im, c2_re, c2_im):
    def body(ar, ai, ld, br, bi, cr, ci, lam_ref, bblk_ref, cblk_ref):
        lr, li, bbr, bbi = _ssm_prep_fn(ar[...], ai[...], ld[...], _expand_b(br[...]), _expand_b(bi[...]))
        lam_ref[0:1, :] = lr
        lam_ref[1:2, :] = li
        bblk_ref[:, 0:N_STATE] = bbr.astype(BF16)
        bblk_ref[:, N_STATE:] = bbi.astype(BF16)
        rep = _lane_repeat(SSM_GROUP_CH, SSM_WIDTH)
        own = _same_group(N_STATE, SSM_STATE, SSM_WIDTH, SSM_GROUP_CH)
        cblk_ref[0:N_STATE, :] = jnp.where(own, _dot_exact(cr[...], rep, ((1,), (0,))), 0.0).astype(BF16)
        cblk_ref[N_STATE:, :] = jnp.where(own, -_dot_exact(ci[...], rep, ((1,), (0,))), 0.0).astype(BF16)

    return pl.pallas_call(
        body,
        name="ssm_prep",
        out_shape=[_sds((2, N_STATE), F32), _sds((SSM_WIDTH, 2 * N_STATE), BF16),
                   _sds((2 * N_STATE, SSM_WIDTH), BF16)],
        compiler_params=_params(),
    )(a_re, a_im, log_dt, bt_re, bt_im, c2_re, c2_im)


def _ssm_prep_bwd(a_re, a_im, log_dt, bt_re, bt_im, dlam, dbblk, dcblk_t):
    def body(ar, ai, ld, br, bi, dl, db, dc, dar, dai, dld, dbr, dbi, dcr, dci):
        _, vjp = jax.vjp(_ssm_prep_fn, ar[...], ai[...], ld[...], _expand_b(br[...]), _expand_b(bi[...]))
        g = vjp((dl[0:1, :], dl[1:2, :], db[:, 0:N_STATE], db[:, N_STATE:]))
        dar[...] = g[0]
        dai[...] = g[1]
        grp = lax.broadcasted_iota(jnp.int32, (SSM_GROUPS, N_STATE), 0)
        lane = lax.broadcasted_iota(jnp.int32, (SSM_GROUPS, N_STATE), 1)
        sel = (lane // SSM_STATE) == grp
        dld[...] = jnp.sum(jnp.where(sel, jnp.broadcast_to(g[2], (SSM_GROUPS, N_STATE)), 0.0), axis=1, keepdims=True)
        dbr[...] = _collect_b(g[3])
        dbi[...] = _collect_b(g[4])
        dcr[...] = _collect_b(dc[:, 0:N_STATE])
        dci[...] = -_collect_b(dc[:, N_STATE:])

    small = _sds((SSM_WIDTH, SSM_STATE), F32)
    return pl.pallas_call(
        body,
        name="ssm_prep_bwd",
        out_shape=[_sds((1, N_STATE), F32), _sds((1, N_STATE), F32), _sds((SSM_GROUPS, 1), F32), small, small, small, small],
        compiler_params=_params(),
    )(a_re, a_im, log_dt, bt_re, bt_im, dlam, dbblk, dcblk_t)


def _perm_matrix(t):
    run = t // SUBCHUNKS
    p = np.zeros((t, t), np.float32)
    r = np.arange(t)
    p[r, (r % SUBCHUNKS) * run + r // SUBCHUNKS] = 1.0
    return jnp.asarray(p, dtype=BF16)


def _unpermute(p, a):
    hi = a.astype(BF16)
    r1 = a - hi.astype(F32)
    mid = r1.astype(BF16)
    lo = (r1 - mid.astype(F32)).astype(BF16)
    return _dot_tn(p, hi) + _dot_tn(p, mid) + _dot_tn(p, lo)


def _power_table(lam_ref, pw_ref, n):
    lr, li = lam_ref[0:1, :], lam_ref[1:2, :]
    pw_ref[0:1, 0:N_STATE] = lr
    pw_ref[0:1, N_STATE:] = li

    def step(i, carry):
        pr, pi = carry
        pr, pi = pr * lr - pi * li, pr * li + pi * lr
        pw_ref[pl.ds(i, 1), 0:N_STATE] = pr
        pw_ref[pl.ds(i, 1), N_STATE:] = pi
        return pr, pi

    lax.fori_loop(1, n, step, (lr, li))


def _col_groups():
    return [(pl.ds(c, SCAN_CG), pl.ds(N_STATE + c, SCAN_CG)) for c in range(0, N_STATE, SCAN_CG)]


def _run_scan(buf, lam_ref, t, reverse):
    nblk = t // 8
    for re, im in _col_groups():
        lr = jnp.broadcast_to(lam_ref[0:1, re], (8, SCAN_CG))
        li = jnp.broadcast_to(lam_ref[1:2, re], (8, SCAN_CG))
        if reverse:
            li = -li
        first = pl.ds((nblk - 1) * 8 if reverse else 0, 8)

        def step(k, carry, re=re, im=im, lr=lr, li=li):
            pr, pi = carry
            i = (nblk - 2 - k) if reverse else (k + 1)
            r = pl.ds(pl.multiple_of(i * 8, 8), 8)
            xr = buf[r, re] + lr * pr - li * pi
            xi = buf[r, im] + lr * pi + li * pr
            buf[r, re] = xr
            buf[r, im] = xi
            return xr, xi

        lax.fori_loop(0, nblk - 1, step, (buf[first, re], buf[first, im]))


def _run_carries(buf, pw_ref, carry_ref, s_ref, t, reverse):
    nblk = t // 8
    run = t // SUBCHUNKS
    edge = buf[pl.ds(0 if reverse else (nblk - 1) * 8, 8), :]
    pr, pi = pw_ref[run - 1:run, 0:N_STATE], pw_ref[run - 1:run, N_STATE:]
    if reverse:
        pi = -pi
    sr, si = carry_ref[0:1, 0:N_STATE], carry_ref[0:1, N_STATE:]
    for s in (range(SUBCHUNKS - 1, -1, -1) if reverse else range(SUBCHUNKS)):
        s_ref[s:s + 1, 0:N_STATE] = sr
        s_ref[s:s + 1, N_STATE:] = si
        er, ei = edge[s:s + 1, 0:N_STATE], edge[s:s + 1, N_STATE:]
        sr, si = er + pr * sr - pi * si, ei + pr * si + pi * sr
    carry_ref[:, 0:N_STATE] = jnp.broadcast_to(sr, (8, N_STATE))
    carry_ref[:, N_STATE:] = jnp.broadcast_to(si, (8, N_STATE))


def _run_fix(buf, pw_ref, s_ref, t, reverse):
    nblk = t // 8
    for re, im in _col_groups():
        sr, si = s_ref[:, re], s_ref[:, im]

        def step(i, carry, re=re, im=im, sr=sr, si=si):
            r = pl.ds(pl.multiple_of(i * 8, 8), 8)
            row = pl.ds((nblk - 1 - i) if reverse else i, 1)
            pr, pi = pw_ref[row, re], pw_ref[row, im]
            if reverse:
                pi = -pi
            buf[r, re] += pr * sr - pi * si
            buf[r, im] += pr * si + pi * sr
            return carry

        lax.fori_loop(0, nblk, step, 0)


STATE_BLOCKS = 2 * N_STATE // LANES
CH_BLOCKS = SSM_WIDTH // LANES


def _state_block(b):
    pair = b % (N_STATE // LANES)
    k = (pair * 2 * SSM_GROUP_CH) // LANES
    return slice(b * LANES, (b + 1) * LANES), slice(k * LANES, (k + 1) * LANES)


def _channel_block(c):
    w = N_STATE // CH_BLOCKS
    return slice(c * LANES, (c + 1) * LANES), slice(c * w, (c + 1) * w), slice(N_STATE + c * w, N_STATE + (c + 1) * w)


def _to_states(vb, w_ref, buf, nt):
    for b in range(STATE_BLOCKS):
        lanes, ch = _state_block(b)
        buf[:, lanes] = _dot_nt(vb[:, ch], w_ref[lanes, ch]) if nt else _dot(vb[:, ch], w_ref[ch, lanes])


def _to_channels(buf, w_ref, nt):
    outs = []
    for c in range(CH_BLOCKS):
        ch, re, im = _channel_block(c)
        xr, xi = buf[:, re].astype(BF16), buf[:, im].astype(BF16)
        if nt:
            outs.append(_dot_nt(xr, w_ref[ch, re]) + _dot_nt(xi, w_ref[ch, im]))
        else:
            outs.append(_dot(xr, w_ref[re, ch]) + _dot(xi, w_ref[im, ch]))
    return jnp.concatenate(outs, axis=-1)


def _ssm_fwd(u, bblk, cblk, lam, d_row, w_glu, b_glu, w_o_ssm, xch):
    n = u.shape[0]
    t = min(SCAN_T, n)
    perm = _perm_matrix(t)

    def body(u_ref, p_ref, bblk_ref, cblk_ref, lam_ref, d_ref, wg_ref, bg_ref, wo_ref, y_ref, ys_ref, st_ref,
             buf, pw_ref, carry_ref, s_ref):
        @pl.when(pl.program_id(0) == 0)
        def _():
            carry_ref[...] = jnp.zeros_like(carry_ref)
            _power_table(lam_ref, pw_ref, t // SUBCHUNKS)

        st_ref[0] = carry_ref[...]
        u_t = u_ref[...]
        p = p_ref[...]
        ub = _dot(p, u_t.astype(BF16)).astype(BF16)
        _to_states(ub, bblk_ref, buf, False)
        _run_scan(buf, lam_ref, t, False)
        _run_carries(buf, pw_ref, carry_ref, s_ref, t, False)
        _run_fix(buf, pw_ref, s_ref, t, False)
        y = d_ref[...] * u_t + _unpermute(p, _to_channels(buf, cblk_ref, False))
        y_ref[...] = y
        z, _ = _gelu(y)
        s = _sigmoid(_dot(z.astype(BF16), wg_ref[...]) + bg_ref[...])
        zgb = (z * s).astype(BF16)
        for j in range(N_DEV):
            ys_ref[:, j * OUT_SHARD:(j + 1) * OUT_SHARD] = _dot(zgb, wo_ref[j])

    consts = [perm, bblk, cblk, lam, d_row, w_glu, b_glu, w_o_ssm]
    return _call(
        body, "ssm_fwd", (n // t,), [u] + consts, [_rows(u, t)] + [_const(a) for a in consts],
        [_sds((n, SSM_WIDTH), F32), _sds((n, D_MODEL), F32), _sds((n // t, 8, 2 * N_STATE), F32)],
        [pl.BlockSpec((t, SSM_WIDTH), lambda i: (i, 0)), pl.BlockSpec((t, D_MODEL), lambda i: (i, 0)),
         pl.BlockSpec((1, 8, 2 * N_STATE), lambda i: (i, 0, 0))],
        scratch=[pltpu.VMEM((t, 2 * N_STATE), F32), pltpu.VMEM((t // SUBCHUNKS, 2 * N_STATE), F32),
                 pltpu.VMEM((8, 2 * N_STATE), F32), pltpu.VMEM((8, 2 * N_STATE), F32)],
        xch=xch)


def _head_norm_rope(slab, gain, cos_t, sin_t):
    xn, inv = _rms(slab, gain, QK_HEAD)
    lo, hi = xn[:, 0:128], xn[:, 128:256]
    return jnp.concatenate([lo, hi * cos_t + _rope_rot(hi) * sin_t], axis=-1), inv


def _head_norm_rope_bwd(g, slab, gain, inv, cos_t, sin_t):
    g_lo, g_hi = g[:, 0:128], g[:, 128:256]
    g_n = jnp.concatenate([g_lo, g_hi * cos_t + _rope_rot_t(g_hi * sin_t)], axis=-1)
    return _rms_bwd(g_n, slab, gain, inv, QK_HEAD)


def _qkv_prep(ql, kvl, q_a_norm, kv_a_norm, wq, wkv, gq, gk, cos_t, sin_t):
    n = ql.shape[0]
    tm = ROW_T

    def body(ql_ref, kvl_ref, cos_ref, sin_ref, qa_ref, ka_ref, wq_ref, wkv_ref, gq_ref, gk_ref,
             q_ref, k_ref, v_ref, kt_ref, vt_ref):
        cos_t, sin_t = cos_ref[...], sin_ref[...]
        qa, _ = _rms(ql_ref[...], qa_ref[...], Q_LORA)
        qab = qa.astype(BF16)
        kvl_t = kvl_ref[...]
        ca, _ = _rms(kvl_t[:, 0:KV_LORA], ka_ref[...], KV_LORA)
        cab = ca.astype(BF16)
        kpe = kvl_t[:, KV_LORA:KV_LAT_PAD]
        q_pre = _dot(qab, wq_ref[...])
        kv_pre = _dot(cab, wkv_ref[...])
        for h in range(N_HEADS):
            qh, _ = _head_norm_rope(q_pre[:, h * QK_PAD:(h + 1) * QK_PAD], gq_ref[...], cos_t, sin_t)
            q_ref[h] = (qh * ATT_SCALE).astype(BF16)
            kv_h = kv_pre[:, h * QK_PAD:(h + 1) * QK_PAD]
            kh, _ = _head_norm_rope(jnp.concatenate([kv_h[:, 0:QK_NOPE], kpe], axis=-1), gk_ref[...], cos_t, sin_t)
            k_ref[h] = kh.astype(BF16)
            kt_ref[h] = kh.T.astype(BF16)
            vh = kv_h[:, QK_NOPE:]
            v_ref[h] = vh.astype(BF16)
            vt_ref[h] = vh.T.astype(BF16)

    row_ins, consts = [ql, kvl, cos_t, sin_t], [q_a_norm, kv_a_norm, wq, wkv, gq, gk]
    outs = [_sds((N_HEADS, n, QK_PAD), BF16), _sds((N_HEADS, n, QK_PAD), BF16), _sds((N_HEADS, n, V_HEAD), BF16),
            _sds((N_HEADS, QK_PAD, n), BF16), _sds((N_HEADS, V_HEAD, n), BF16)]
    out_specs = [_rows(o, tm) for o in outs[:3]] + [
        pl.BlockSpec((N_HEADS, QK_PAD, tm), lambda i: (0, 0, i)), pl.BlockSpec((N_HEADS, V_HEAD, tm), lambda i: (0, 0, i))]
    return _call(body, "qkv_prep", (n // tm,), row_ins + consts,
                 [_rows(a, tm) for a in row_ins] + [_const(a) for a in consts], outs, out_specs)


def _causal_mask_t(st, t):
    key = lax.broadcasted_iota(jnp.int32, (t, t), 0)
    qry = lax.broadcasted_iota(jnp.int32, (t, t), 1)
    return jnp.where(key <= qry, st, -jnp.inf)


def _attn_fwd(q, k, vt, xch):
    n = q.shape[1]
    t = min(ATT_T, n)

    hp = ATT_HEADS

    def body(q_ref, k_ref, vt_ref, o_ref, lse_ref, ot_ref):
        i = pl.program_id(1)
        qts = [q_ref[g] for g in range(hp)]

        def kv_tile(j, carry, diag):
            ts = t // ATT_SUB
            sts = []
            for g in range(hp):
                for a in range(ATT_SUB):
                    r0 = pl.multiple_of(j * t + a * ts, ts)
                    st = _dot_nt(k_ref[g, pl.ds(r0, ts), :], qts[g])
                    if diag:
                        key = lax.broadcasted_iota(jnp.int32, (ts, t), 0) + a * ts
                        qry = lax.broadcasted_iota(jnp.int32, (ts, t), 1)
                        st = jnp.where(key <= qry, st, -jnp.inf)
                    sts.append(st)
            out = []
            for g in range(hp):
                m, l, acc = carry[g]
                for a in range(ATT_SUB):
                    st = sts[g * ATT_SUB + a]
                    r0 = pl.multiple_of(j * t + a * ts, ts)
                    m_new = jnp.maximum(m, jnp.max(st, 0, keepdims=True))
                    alpha = jnp.exp(m - m_new)
                    pt = jnp.exp(st - m_new)
                    l = alpha * l + jnp.sum(pt, 0, keepdims=True)
                    acc = alpha * acc + _dot(vt_ref[g, :, pl.ds(r0, ts)], pt.astype(BF16))
                    m = m_new
                out.append((m, l, acc))
            return tuple(out)

        one = (jnp.full((1, t), -jnp.inf, F32), jnp.zeros((1, t), F32), jnp.zeros((V_HEAD, t), F32))
        carry = lax.fori_loop(0, i, functools.partial(kv_tile, diag=False), (one,) * hp)
        for g, (m, l, acc) in enumerate(kv_tile(i, carry, True)):
            out_t = acc / l
            o_ref[:, g * V_HEAD:(g + 1) * V_HEAD] = out_t.T
            ot_ref[g * V_HEAD:(g + 1) * V_HEAD, :] = out_t.astype(BF16)
            lse_ref[g] = m + jnp.log(l)

    return _call(
        body, "attn_fwd", (N_HEADS // hp, n // t), [q, k, vt],
        [pl.BlockSpec((hp, t, QK_PAD), lambda h, i: (h, i, 0)), pl.BlockSpec((hp, n, QK_PAD), lambda h, i: (h, 0, 0)),
         pl.BlockSpec((hp, V_HEAD, n), lambda h, i: (h, 0, 0))],
        [_sds((n, N_HEADS * V_HEAD), F32), _sds((N_HEADS, 1, n), F32), _sds((N_HEADS * V_HEAD, n), BF16)],
        [pl.BlockSpec((t, hp * V_HEAD), lambda h, i: (i, h)), pl.BlockSpec((hp, 1, t), lambda h, i: (h, 0, i)),
         pl.BlockSpec((hp * V_HEAD, t), lambda h, i: (h, i))],
        xch=xch)


def _merge(attn, gs, gm, y_ssm, x, w_o_mla, w_out):
    n = x.shape[0]

    def body(at_ref, gs_ref, gm_ref, ys_ref, x_ref, wo_ref, wout_ref, h_ref, ym_ref, mxt_ref):
        y_mla = _dot(at_ref[...].astype(BF16), wo_ref[...])
        ym_ref[...] = y_mla
        mixed = _sigmoid(gs_ref[...]) * ys_ref[...] + _sigmoid(gm_ref[...]) * y_mla
        mxt_ref[...] = mixed.T.astype(BF16)
        h_ref[...] = x_ref[...] + _dot(mixed.astype(BF16), wout_ref[...])

    outs = [((n, D_MODEL), F32), ((n, D_MODEL), F32)]
    return _row_call(body, "merge", n, MM_T, [attn, gs, gm, y_ssm, x], [w_o_mla, w_out], outs,
                     col_outs=[((D_MODEL, n), BF16)])


def _mlp_fwd_loss(h, target, norm_mlp, w_up, w_down):
    n = h.shape[0]

    def body(h_ref, t_ref, g_ref, wu_ref, wd_ref, hn_ref, do_ref, hnt_ref, loss_ref):
        h_t = h_ref[...]
        hn, _ = _rms(h_t, g_ref[...], D_MODEL)
        hb = hn.astype(BF16)
        hn_ref[...] = hb
        hnt_ref[...] = hn.T.astype(BF16)
        out = h_t
        for j in range(N_DEV):
            a = jnp.maximum(_dot(hb, wu_ref[j]), 0.0)
            out += _dot((a * a).astype(BF16), wd_ref[j])
        err = out - t_ref[...]
        do_ref[...] = err * (1.0 / D_MODEL)
        _acc(loss_ref, jnp.broadcast_to(jnp.sum(err * err) * (0.5 / D_MODEL), loss_ref.shape))

    outs = [((n, D_MODEL), BF16), ((n, D_MODEL), F32)]
    return _row_call(body, "mlp_fwd_loss", n, MM_T, [h, target], [norm_mlp, w_up, w_down], outs, [((8, 128), F32)],
                     col_outs=[((D_MODEL, n), BF16)])


def _mlp_bwd(dout, hn, h, norm_mlp, w_up, w_down):
    n = h.shape[0]

    def body(do_ref, hn_ref, h_ref, g_ref, wu_ref, wd_ref, da_ref, dh_ref, dob_ref, hidt_ref, dg_ref):
        dout_t = do_ref[...]
        doutb = dout_t.astype(BF16)
        dob_ref[...] = doutb
        hb = hn_ref[...]
        dhn = jnp.zeros_like(dout_t)
        for j in range(N_DEV):
            cols = slice(j * FF_SHARD, (j + 1) * FF_SHARD)
            a = jnp.maximum(_dot(hb, wu_ref[j]), 0.0)
            hidt_ref[cols, :] = (a * a).T.astype(BF16)
            da = (_dot_nt(doutb, wd_ref[j]) * (2.0 * a)).astype(BF16)
            da_ref[:, cols] = da
            dhn += _dot_nt(da, wu_ref[j])
        h_t = h_ref[...]
        inv = lax.rsqrt(jnp.sum(h_t * h_t, -1, keepdims=True) * (1.0 / D_MODEL) + EPS)
        dx, dg = _rms_bwd(dhn, h_t, g_ref[...], inv, D_MODEL)
        dh_ref[...] = dout_t + dx
        _acc(dg_ref, jnp.sum(dg, 0, keepdims=True))

    outs = [((n, D_FF), BF16), ((n, D_MODEL), F32), ((n, D_MODEL), BF16)]
    return _row_call(body, "mlp_bwd", n, MM_T, [dout, hn, h], [norm_mlp, w_up, w_down], outs, [((1, D_MODEL), F32)],
                     col_outs=[((D_FF, n), BF16)])


def _merge_bwd(dh, gs, gm, y_ssm, y_mla, w_out, w_o_mla):
    n = dh.shape[0]

    def body(dh_ref, gs_ref, gm_ref, ys_ref, ym_ref, wout_ref, wo_ref, dgs_ref, dgm_ref, dys_ref, dym_ref, dat_ref):
        dmix = _dot_nt(dh_ref[...].astype(BF16), wout_ref[...])
        sgs, sgm = _sigmoid(gs_ref[...]), _sigmoid(gm_ref[...])
        dgs_ref[...] = (dmix * ys_ref[...] * sgs * (1.0 - sgs)).astype(BF16)
        dgm_ref[...] = (dmix * ym_ref[...] * sgm * (1.0 - sgm)).astype(BF16)
        dys_ref[...] = (dmix * sgs).astype(BF16)
        dym = (dmix * sgm).astype(BF16)
        dym_ref[...] = dym
        dat_ref[...] = _dot_nt(dym, wo_ref[...])

    outs = [((n, D_MODEL), BF16)] * 4 + [((n, D_MODEL), F32)]
    return _row_call(body, "merge_bwd", n, MM_T, [dh, gs, gm, y_ssm, y_mla], [w_out, w_o_mla], outs)


def _attn_bwd(q, k, kt, v, out, lse, dout, xch):
    n = q.shape[1]
    t = min(ATT_T, n)
    nt = n // t

    def body(q_ref, k_ref, kt_ref, v_ref, o_ref, lse_ref, do_ref, dq_ref, dk_ref, dv_ref, delta_ref, dqt_ref):
        j = pl.program_id(1)

        @pl.when(j == 0)
        def _():
            dqt_ref[...] = jnp.zeros_like(dqt_ref)
            prod = do_ref[...] * o_ref[...]
            delta_ref[...] = lax.dot_general(jnp.ones((8, V_HEAD), F32), prod, (((1,), (1,)), ((), ())),
                                             precision=lax.Precision.HIGHEST, preferred_element_type=F32)

        k_t = k_ref[0]
        kt_t = kt_ref[0]
        v_t = v_ref[0]

        def q_tile(i, carry, diag):
            dk, dv = carry
            r0 = pl.multiple_of(i * t, t)
            rows = pl.ds(r0, t)
            qt = q_ref[0, rows, :]
            st = _dot_nt(k_t, qt)
            if diag:
                st = _causal_mask_t(st, t)
            pt = jnp.exp(st - lse_ref[0, :, rows])
            dob = do_ref[rows, :].astype(BF16)
            dv = dv + _dot(pt.astype(BF16), dob)
            dst = (pt * (_dot_nt(v_t, dob) - delta_ref[0:1, rows])).astype(BF16)
            dk = dk + _dot(dst, qt)
            dqt_ref[:, rows] += _dot(kt_t, dst)
            return dk, dv

        carry = q_tile(j, (jnp.zeros((t, QK_PAD), F32), jnp.zeros((t, V_HEAD), F32)), True)
        dk, dv = lax.fori_loop(j + 1, nt, functools.partial(q_tile, diag=False), carry)
        dk_ref[0] = dk
        dv_ref[0] = dv

        @pl.when(j == nt - 1)
        def _():
            for c in range(0, n, t):
                dq_ref[0, c:c + t, :] = dqt_ref[:, c:c + t].T

    return _call(
        body, "attn_bwd", (N_HEADS, nt), [q, k, kt, v, out, lse, dout],
        [pl.BlockSpec((1, n, QK_PAD), lambda h, j: (h, 0, 0)), pl.BlockSpec((1, t, QK_PAD), lambda h, j: (h, j, 0)),
         pl.BlockSpec((1, QK_PAD, t), lambda h, j: (h, 0, j)), pl.BlockSpec((1, t, V_HEAD), lambda h, j: (h, j, 0)),
         pl.BlockSpec((n, V_HEAD), lambda h, j: (0, h)), pl.BlockSpec((1, 1, n), lambda h, j: (h, 0, 0)),
         pl.BlockSpec((n, V_HEAD), lambda h, j: (0, h))],
        [_sds((N_HEADS, n, QK_PAD), F32), _sds((N_HEADS, n, QK_PAD), F32), _sds((N_HEADS, n, V_HEAD), F32)],
        [pl.BlockSpec((1, n, QK_PAD), lambda h, j: (h, 0, 0)), pl.BlockSpec((1, t, QK_PAD), lambda h, j: (h, j, 0)),
         pl.BlockSpec((1, t, V_HEAD), lambda h, j: (h, j, 0))],
        scratch=[pltpu.VMEM((8, n), F32), pltpu.VMEM((QK_PAD, n), F32)],
        xch=xch)


def _qkv_prep_bwd(ql, kvl, dq, dk, dv, q_a_norm, kv_a_norm, wq, wkv, gq, gk, cos_t, sin_t, xch):
    n = ql.shape[0]

    def body(ql_ref, kvl_ref, cos_ref, sin_ref, dq_ref, dk_ref, dv_ref, qa_ref, ka_ref, wq_ref, wkv_ref, gq_ref, gk_ref,
             dql_ref, dkvl_ref, dqa_ref, dka_ref, dgq_ref, dgk_ref, dwq_ref, dwkv_ref, dqp_ref, dkvp_ref):
        cos_t, sin_t = cos_ref[...], sin_ref[...]
        ql_t = ql_ref[...]
        qa, inv_qa = _rms(ql_t, qa_ref[...], Q_LORA)
        qab = qa.astype(BF16)
        kvl_t = kvl_ref[...]
        ckv = kvl_t[:, 0:KV_LORA]
        ca, inv_ca = _rms(ckv, ka_ref[...], KV_LORA)
        cab = ca.astype(BF16)
        kpe = kvl_t[:, KV_LORA:KV_LAT_PAD]
        dgq = jnp.zeros((1, QK_PAD), F32)
        dgk = jnp.zeros((1, QK_PAD), F32)
        dkpe = jnp.zeros_like(kpe)
        q_pre = _dot(qab, wq_ref[...])
        kv_pre = _dot(cab, wkv_ref[...])
        for h in range(N_HEADS):
            head = slice(h * QK_PAD, (h + 1) * QK_PAD)
            q_slab = q_pre[:, head]
            inv = lax.rsqrt(jnp.sum(q_slab * q_slab, -1, keepdims=True) * (1.0 / QK_HEAD) + EPS)
            d_slab, dg = _head_norm_rope_bwd(dq_ref[h] * ATT_SCALE, q_slab, gq_ref[...], inv, cos_t, sin_t)
            dqp_ref[:, head] = d_slab.astype(BF16)
            dgq += jnp.sum(dg, 0, keepdims=True)
            k_slab = jnp.concatenate([kv_pre[:, h * QK_PAD:h * QK_PAD + QK_NOPE], kpe], axis=-1)
            inv = lax.rsqrt(jnp.sum(k_slab * k_slab, -1, keepdims=True) * (1.0 / QK_HEAD) + EPS)
            d_slab, dg = _head_norm_rope_bwd(dk_ref[h], k_slab, gk_ref[...], inv, cos_t, sin_t)
            dkvp_ref[:, head] = jnp.concatenate([d_slab[:, 0:QK_NOPE], dv_ref[h]], axis=-1).astype(BF16)
            dkpe += d_slab[:, QK_NOPE:QK_PAD]
            dgk += jnp.sum(dg, 0, keepdims=True)
        dqa = _dot_nt(dqp_ref[...], wq_ref[...])
        dx, dg = _rms_bwd(dqa, ql_t, qa_ref[...], inv_qa, Q_LORA)
        dql_ref[...] = dx.astype(BF16)
        _acc(dqa_ref, jnp.sum(dg, 0, keepdims=True))
        dca = _dot_nt(dkvp_ref[...], wkv_ref[...])
        dx, dg = _rms_bwd(dca, ckv, ka_ref[...], inv_ca, KV_LORA)
        dkvl_ref[:, 0:KV_LORA] = dx.astype(BF16)
        dkvl_ref[:, KV_LORA:KV_LAT_PAD] = dkpe.astype(BF16)
        _acc(dka_ref, jnp.sum(dg, 0, keepdims=True))
        _acc(dgq_ref, dgq)
        _acc(dgk_ref, dgk)
        _acc(dwq_ref, _dot_tn(qab, dqp_ref[...]))
        _acc(dwkv_ref, _dot_tn(cab, dkvp_ref[...]))

    wide = N_HEADS * QK_PAD
    row_outs = [((n, Q_LORA), BF16), ((n, KV_LAT_PAD), BF16)]
    acc_outs = [((1, Q_LORA), F32), ((1, KV_LORA), F32), ((1, QK_PAD), F32), ((1, QK_PAD), F32),
                ((Q_LORA, wide), F32), ((KV_LORA, wide), F32)]
    return _row_call(body, "qkv_prep_bwd", n, ROW_T, [ql, kvl, cos_t, sin_t, dq, dk, dv],
                     [q_a_norm, kv_a_norm, wq, wkv, gq, gk], row_outs, acc_outs, xch=xch,
                     scratch=[pltpu.VMEM((ROW_T, wide), BF16), pltpu.VMEM((ROW_T, wide), BF16)])


def _glu_bwd(dy_ssm, y, w_glu, b_glu, w_o_ssm):
    n = y.shape[0]

    def body(dys_ref, y_ref, wg_ref, bg_ref, wo_ref, dy_ref, db_ref, dwg_ref, dwo_ref):
        y_t = y_ref[...]
        z, th = _gelu(y_t)
        zb = z.astype(BF16)
        s = _sigmoid(_dot(zb, wg_ref[...]) + bg_ref[...])
        dys = dys_ref[...]
        dzg = jnp.zeros_like(y_t)
        for j in range(N_DEV):
            dzg += _dot_nt(dys[:, j * OUT_SHARD:(j + 1) * OUT_SHARD], wo_ref[j])
        dt = dzg * z * s * (1.0 - s)
        dtb = dt.astype(BF16)
        dz = dzg * s + _dot_nt(dtb, wg_ref[...])
        dy_ref[...] = dz * _gelu_grad(y_t, th)
        _acc(db_ref, jnp.sum(dt, 0, keepdims=True))
        _acc(dwg_ref, _dot_tn(zb, dtb))
        _acc(dwo_ref, _dot_tn((z * s).astype(BF16), dys))

    acc_outs = [((1, SSM_WIDTH), F32), ((SSM_WIDTH, SSM_WIDTH), F32), ((SSM_WIDTH, D_MODEL), F32)]
    return _row_call(body, "glu_bwd", n, ROW_T, [dy_ssm, y], [w_glu, b_glu, w_o_ssm], [((n, SSM_WIDTH), F32)], acc_outs)


def _ssm_bwd(u, dy, st, bblk, cblk, lam, d_row, xch):
    n = u.shape[0]
    t = min(SCAN_T, n)
    nc = n // t
    kb = 512
    perm = _perm_matrix(t)

    def body(u_ref, dy_ref, st_ref, p_ref, bblk_ref, cblk_ref, lam_ref, d_ref,
             du_ref, dlam_ref, dd_ref, db_ref, dct_ref,
             buf_x, buf_a, pw_ref, carry_ref, xcarry_ref, sx_ref, sa_ref, db_acc, dct_acc):
        @pl.when(pl.program_id(0) == 0)
        def _():
            carry_ref[...] = jnp.zeros_like(carry_ref)
            db_acc[...] = jnp.zeros_like(db_acc)
            dct_acc[...] = jnp.zeros_like(dct_acc)
            _power_table(lam_ref, pw_ref, t // SUBCHUNKS)

        u_t = u_ref[...]
        dy_t = dy_ref[...]
        p = p_ref[...]
        ub = _dot(p, u_t.astype(BF16)).astype(BF16)
        dyb = _dot(p, dy_t.astype(BF16)).astype(BF16)
        _to_states(ub, bblk_ref, buf_x, False)
        xcarry_ref[...] = st_ref[0]
        _run_scan(buf_x, lam_ref, t, False)
        _run_carries(buf_x, pw_ref, xcarry_ref, sx_ref, t, False)
        _run_fix(buf_x, pw_ref, sx_ref, t, False)
        _to_states(dyb, cblk_ref, buf_a, True)
        _run_scan(buf_a, lam_ref, t, True)
        _run_carries(buf_a, pw_ref, carry_ref, sa_ref, t, True)
        _run_fix(buf_a, pw_ref, sa_ref, t, True)
        du_ref[...] = (d_ref[...] * dy_t + _unpermute(p, _to_channels(buf_a, bblk_ref, True))).astype(BF16)
        for b in range(STATE_BLOCKS):
            lanes, ch = _state_block(b)
            db_acc[ch, lanes] += _dot_tn(ub[:, ch], buf_a[:, lanes].astype(BF16))
            dct_acc[ch, lanes] += _dot_tn(dyb[:, ch], buf_x[:, lanes].astype(BF16))
        for c in range(0, N_STATE, kb):
            re, im = pl.ds(c, kb), pl.ds(N_STATE + c, kb)
            xr, xi = buf_x[pl.ds(0, t - 8), re], buf_x[pl.ds(0, t - 8), im]
            ar, ai = buf_a[pl.ds(8, t - 8), re], buf_a[pl.ds(8, t - 8), im]
            x0r, x0i = sx_ref[:, re], sx_ref[:, im]
            a0r, a0i = buf_a[0:8, re], buf_a[0:8, im]
            dlam_part_re = (jnp.sum(ar * xr + ai * xi, 0, keepdims=True)
                            + jnp.sum(a0r * x0r + a0i * x0i, 0, keepdims=True))
            dlam_part_im = (jnp.sum(ai * xr - ar * xi, 0, keepdims=True)
                            + jnp.sum(a0i * x0r - a0r * x0i, 0, keepdims=True))

            @pl.when(pl.program_id(0) == 0)
            def _(c=c):
                dlam_ref[0:1, c:c + kb] = jnp.zeros((1, kb), F32)
                dlam_ref[1:2, c:c + kb] = jnp.zeros((1, kb), F32)

            dlam_ref[0:1, c:c + kb] += dlam_part_re
            dlam_ref[1:2, c:c + kb] += dlam_part_im
        _acc(dd_ref, jnp.sum(dy_t * u_t, 0, keepdims=True))

        @pl.when(pl.program_id(0) == nc - 1)
        def _():
            pltpu.sync_copy(db_acc, db_ref)
            pltpu.sync_copy(dct_acc, dct_ref)

    rev = lambda i: (nc - 1 - i, 0)
    consts = [perm, bblk, cblk, lam, d_row]
    wide = (SSM_WIDTH, 2 * N_STATE)
    return _call(
        body, "ssm_bwd", (nc,), [u, dy, st] + consts,
        [pl.BlockSpec((t, SSM_WIDTH), rev), pl.BlockSpec((t, SSM_WIDTH), rev),
         pl.BlockSpec((1, 8, 2 * N_STATE), lambda i: (nc - 1 - i, 0, 0))] + [_const(a) for a in consts],
        [_sds((n, SSM_WIDTH), BF16), _sds((2, N_STATE), F32), _sds((1, SSM_WIDTH), F32), _sds(wide, F32), _sds(wide, F32)],
        [pl.BlockSpec((t, SSM_WIDTH), rev), pl.BlockSpec((2, N_STATE), lambda i: (0, 0)),
         pl.BlockSpec((1, SSM_WIDTH), lambda i: (0, 0)), ANY, ANY],
        scratch=[pltpu.VMEM((t, 2 * N_STATE), F32)] * 2 + [pltpu.VMEM((t // SUBCHUNKS, 2 * N_STATE), F32)]
        + [pltpu.VMEM((8, 2 * N_STATE), F32)] * 4 + [pltpu.VMEM(wide, F32)] * 2,
        xch=xch)


def _in_proj_bwd(pieces, dh, x, xn_t, norm_mix, w_in_pad, xch):
    n = x.shape[0]
    tm = min(MM_T, n)
    nt = n // tm

    def body(du_ref, dql_ref, dkvl_ref, dgs_ref, dgm_ref, dh_ref, x_ref, xnt_ref, g_ref, w_ref,
             dx_ref, dg_ref, dw_ref, acc_ref):
        @pl.when(pl.program_id(0) == 0)
        def _():
            acc_ref[...] = jnp.zeros_like(acc_ref)

        xnt = xnt_ref[...]
        dxn = jnp.zeros((tm, D_MODEL), F32)
        for ref, (a, b) in zip((du_ref, dql_ref, dkvl_ref, dgs_ref, dgm_ref), IN_SEGS):
            piece = ref[...]
            dxn += _dot_nt(piece, w_ref[:, a:b])
            acc_ref[:, a:b] += _dot(xnt, piece)
        x_t = x_ref[...]
        inv = lax.rsqrt(jnp.sum(x_t * x_t, -1, keepdims=True) * (1.0 / D_MODEL) + EPS)
        dx, dg = _rms_bwd(dxn, x_t, g_ref[...], inv, D_MODEL)
        dx_ref[...] = dh_ref[...] + dx
        _acc(dg_ref, jnp.sum(dg, 0, keepdims=True))

        @pl.when(pl.program_id(0) == nt - 1)
        def _():
            pltpu.sync_copy(acc_ref, dw_ref)

    row_ins, consts = list(pieces) + [dh, x], [norm_mix, w_in_pad]
    in_specs = ([_rows(a, tm) for a in row_ins] + [pl.BlockSpec((D_MODEL, tm), lambda i: (0, i))]
                + [_const(a) for a in consts])
    return _call(
        body, "in_proj_bwd", (nt,), row_ins + [xn_t] + consts, in_specs,
        [_sds((n, D_MODEL), F32), _sds((1, D_MODEL), F32), _sds((D_MODEL, D_IN_PAD), F32)],
        [pl.BlockSpec((tm, D_MODEL), lambda i: (i, 0)), pl.BlockSpec((1, D_MODEL), lambda i: (0, 0)), ANY],
        scratch=[pltpu.VMEM((D_MODEL, D_IN_PAD), F32)], xch=xch)


def _swap_minor(a):
    g, r, c = a.shape[1:]
    return jnp.transpose(a[0], (0, 2, 1)).reshape(g * c, r)


def _pad_in(w):
    return jnp.concatenate([w[:, :KV_END], jnp.zeros((w.shape[0], D_IN_PAD - D_IN), w.dtype), w[:, KV_END:]], axis=1)


def _unpad_in(w):
    return jnp.concatenate([w[:, :KV_END], w[:, KV_END + D_IN_PAD - D_IN:]], axis=1)


def _pad_gain(g):
    return jnp.pad(g, ((0, 0), (0, QK_PAD - QK_HEAD)))


def _place():
    x, y, c = lax.axis_index("x"), lax.axis_index("y"), lax.axis_index("c")
    chips = [(x, y), (1 - x, y), (x, 1 - y), (1 - x, 1 - y)]
    return x, y, c, chips


def _all_gather(block, name):
    rows, lanes = block.shape

    def body(x_ref, out_ref, send_sems, recv_sems, local_sem):
        x, y, c, chips = _place()
        me, sibling = (x, y, c), (x, y, 1 - c)

        def slot(px, py, pc):
            return out_ref.at[4 * px + 2 * py + pc]

        def copy(k, blk, to, src=None):
            return pltpu.make_async_remote_copy(
                src_ref=slot(*blk) if src is None else src, dst_ref=slot(*blk),
                send_sem=send_sems.at[k], recv_sem=recv_sems.at[k], device_id=to, device_id_type=MESH)

        mine = pltpu.make_async_copy(x_ref, slot(*me), local_sem)
        mine.start()
        first = [copy(0, me, sibling, src=x_ref)]
        first += [copy(1 + j, me, (*chip, c), src=x_ref) for j, chip in enumerate(chips[1:])]
        for cp in first:
            cp.start()
        passed = [copy(4 + j, (*chip, c), sibling) for j, chip in enumerate(chips[1:])]
        for j, chip in enumerate(chips[1:]):
            copy(1 + j, (*chip, c), me).wait_recv()
            passed[j].start()
        copy(0, sibling, me).wait_recv()
        for j, chip in enumerate(chips[1:]):
            copy(4 + j, (*chip, 1 - c), me).wait_recv()
        for cp in first + passed:
            cp.wait_send()
        mine.wait()

    return pl.pallas_call(
        body,
        name=name,
        in_specs=[ANY],
        out_specs=ANY,
        out_shape=_sds((N_DEV, rows, lanes), block.dtype),
        scratch_shapes=[pltpu.SemaphoreType.DMA((7,)), pltpu.SemaphoreType.DMA((7,)), pltpu.SemaphoreType.DMA],
    )(block)


def _reduce_scatter(parts, gather, name):
    _, rows, lanes = parts.shape

    def body(p_ref, g_ref, out_ref, ga_ref, own, land_a, send_b, land_b, sa, ra, sb, rb, lo, *g_sems):
        x, y, c, chips = _place()
        sibling = (x, y, 1 - c)
        _xchg_start([False], [g_ref], [ga_ref], *g_sems)

        def blk(chip, core):
            return p_ref.at[4 * chip[0] + 2 * chip[1] + core]

        to_sib = [pltpu.make_async_remote_copy(
            src_ref=blk(chips[k], 1 - c), dst_ref=land_a.at[k], send_sem=sa.at[k], recv_sem=ra.at[k],
            device_id=sibling, device_id_type=MESH) for k in range(4)]
        for cp in to_sib:
            cp.start()
        loads = [pltpu.make_async_copy(blk(chips[k], c), own.at[k], lo.at[k]) for k in range(4)]
        for cp in loads:
            cp.start()
        to_chip = [pltpu.make_async_remote_copy(
            src_ref=send_b.at[j], dst_ref=land_b.at[j], send_sem=sb.at[j], recv_sem=rb.at[j],
            device_id=(*chips[1 + j], c), device_id_type=MESH) for j in range(3)]
        for k in (1, 2, 3):
            to_sib[k].wait_recv()
            loads[k].wait()
            send_b[k - 1] = (own[k] + land_a[k]).astype(BF16)
            to_chip[k - 1].start()
        to_sib[0].wait_recv()
        loads[0].wait()
        acc = own[0] + land_a[0]
        for j in range(3):
            to_chip[j].wait_recv()
            acc = acc + land_b[j].astype(F32)
        out_ref[...] = acc
        for cp in to_sib + to_chip:
            cp.wait_send()
        _xchg_wait([False], [g_ref], [ga_ref], *g_sems)

    return pl.pallas_call(
        body,
        name=name,
        in_specs=[ANY, ANY],
        out_specs=[pl.BlockSpec(memory_space=pltpu.VMEM), ANY],
        out_shape=[_sds((rows, lanes), F32), _sds((N_DEV,) + gather.shape, gather.dtype)],
        scratch_shapes=[pltpu.VMEM((4, rows, lanes), F32), pltpu.VMEM((4, rows, lanes), F32),
                        pltpu.VMEM((3, rows, lanes), BF16), pltpu.VMEM((3, rows, lanes), BF16)]
        + [pltpu.SemaphoreType.DMA((4,))] * 2 + [pltpu.SemaphoreType.DMA((3,))] * 2 + [pltpu.SemaphoreType.DMA((4,))]
        + [pltpu.SemaphoreType.DMA((1,))] * 3,
        compiler_params=_params(),
    )(parts, gather)


def _adamw_math(w, g, m, v):
    m = ADAM_B1 * m + (1.0 - ADAM_B1) * g
    v = ADAM_B2 * v + (1.0 - ADAM_B2) * (g * g)
    m_hat = m / (1.0 - ADAM_B1 ** ADAM_STEP)
    v_hat = v / (1.0 - ADAM_B2 ** ADAM_STEP)
    delta = -ADAM_LR * (m_hat / (jnp.sqrt(v_hat) + ADAM_EPS) + ADAM_WD * w)
    return delta, m, v


def _row_tile(r):
    return max(t for t in range(8, min(r, 256) + 1, 8) if r % t == 0)


def _adamw(w, g, m, v, name):
    r, n = w.shape

    def body(w_ref, g_ref, m_ref, v_ref, d_ref, nm_ref, nv_ref):
        d_ref[...], nm_ref[...], nv_ref[...] = _adamw_math(w_ref[...], g_ref[...], m_ref[...], v_ref[...])

    return _row_call(body, name, r, _row_tile(r), [w, g, m, v], [], [((r, n), F32)] * 3)


def _adamw_sum(landed, w, m, v, name):
    r, n = w.shape

    def body(l_ref, w_ref, m_ref, v_ref, g_ref, d_ref, nm_ref, nv_ref):
        g = l_ref[0].astype(F32)
        for dev in range(1, N_DEV):
            g = g + l_ref[dev].astype(F32)
        g_ref[...] = g
        d_ref[...], nm_ref[...], nv_ref[...] = _adamw_math(w_ref[...], g, m_ref[...], v_ref[...])

    tm = max(t for t in range(16, min(r, 256) + 1, 16) if r % t == 0)
    return _row_call(body, name, r, tm, [landed, w, m, v], [], [((r, n), F32)] * 4)


def _adamw_small(first, rest, w, m, v, row_counts):
    n_rest = w.shape[0] - first.shape[1]

    def body(f_ref, r_ref, w_ref, m_ref, v_ref, loss_ref, *out_refs):
        gf, gr = f_ref[0], r_ref[0]
        for dev in range(1, N_DEV):
            gf, gr = gf + f_ref[dev], gr + r_ref[dev]
        loss_ref[...] = gr[n_rest:n_rest + 8]
        g = jnp.concatenate([gf, gr[0:n_rest]], axis=0)
        d, nm, nv = _adamw_math(w_ref[...], g, m_ref[...], v_ref[...])
        off = 0
        for p, rows in enumerate(row_counts):
            for k, val in enumerate((g, d, nm, nv)):
                out_refs[4 * p + k][...] = val[off:off + rows]
            off += rows

    outs = [_sds((8, LANES), F32)] + [_sds((rows, LANES), F32) for rows in row_counts for _ in range(4)]
    return pl.pallas_call(body, name="adamw_small", out_shape=outs, compiler_params=_params())(first, rest, w, m, v)


SMALL = ("norm_mix", "q_a_norm", "kv_a_norm", "q_norm", "k_norm", "ssm_a_re", "ssm_a_im", "ssm_log_dt", "ssm_b_re",
         "ssm_b_im", "ssm_c_re", "ssm_c_im", "ssm_d", "b_glu", "norm_mlp")
WEIGHT_ORDER = ("norm_mix", "w_in", "q_a_norm", "kv_a_norm", "w_q_b", "w_kv_b", "q_norm", "k_norm", "w_o_mla",
                "ssm_a_re", "ssm_a_im", "ssm_log_dt", "ssm_b_re", "ssm_b_im", "ssm_c_re", "ssm_c_im", "ssm_d", "w_glu",
                "b_glu", "w_o_ssm", "w_out", "norm_mlp", "w_up", "w_down")
IN_SHARD = D_IN // N_DEV


def _pack_small(vals, names=SMALL):
    parts = []
    for n in names:
        flat = vals[n].reshape(-1)
        size = -(-flat.shape[0] // (8 * LANES)) * 8 * LANES
        parts.append(jnp.pad(flat, (0, size - flat.shape[0])).reshape(-1, LANES))
    return jnp.concatenate(parts, axis=0)


def _small_rows(like):
    return [-(-like[n].size // (8 * LANES)) * 8 for n in SMALL]


def _step(x, pos_col, target, w, small):
    bf = {n: a.astype(BF16) for n, a in w.items()}
    gq, gk = _pad_gain(small["q_norm"]), _pad_gain(small["k_norm"])
    a_re = small["ssm_a_re"].reshape(1, N_STATE)
    a_im = small["ssm_a_im"].reshape(1, N_STATE)
    log_dt = jnp.repeat(small["ssm_log_dt"].reshape(SSM_GROUPS), SSM_STATE).reshape(1, N_STATE)
    bt_re, bt_im = _swap_minor(small["ssm_b_re"]), _swap_minor(small["ssm_b_im"])
    c2_re, c2_im = _swap_minor(small["ssm_c_re"]), _swap_minor(small["ssm_c_im"])
    d_row = small["ssm_d"].reshape(1, SSM_WIDTH)

    w_in_all = _all_gather(bf["w_in"], "gather_w_in")
    w_in_pad = _pad_in(jnp.transpose(w_in_all, (1, 0, 2)).reshape(D_MODEL, D_IN))
    cos_t, sin_t = _rope_tables(pos_col)
    lam, bblk, cblk = _ssm_prep(a_re, a_im, log_dt, bt_re, bt_im, c2_re, c2_im)
    wq_mine = jnp.pad(bf["w_q_b"], ((0, 0), (0, QK_PAD - QK_HEAD)))
    u, ql, kvl, gs, gm, xn_t, w_glu, w_o_ssm = _in_proj(
        x, small["norm_mix"], w_in_pad, xch=[(bf["w_glu"], False), (bf["w_o_ssm"], False)])
    w_glu = w_glu.reshape(SSM_WIDTH, SSM_WIDTH)
    y, y_ssm, st, wq, wkv, w_o_mla, w_out = _ssm_fwd(
        u, bblk, cblk, lam, d_row, w_glu, small["b_glu"], w_o_ssm,
        xch=[(wq_mine, False), (bf["w_kv_b"], False), (bf["w_o_mla"], False), (bf["w_out"], False)])
    w_o_mla, w_out = w_o_mla.reshape(D_MODEL, D_MODEL), w_out.reshape(D_MODEL, D_MODEL)
    wq = jnp.transpose(wq, (1, 0, 2)).reshape(Q_LORA, N_HEADS * QK_PAD)
    wkv = jnp.transpose(wkv, (1, 0, 2)).reshape(KV_LORA, N_HEADS * QK_PAD)
    q, k, v, kt, vt = _qkv_prep(ql, kvl, small["q_a_norm"], small["kv_a_norm"], wq, wkv, gq, gk, cos_t, sin_t)
    attn, lse, attn_t, w_up, w_down = _attn_fwd(q, k, vt, xch=[(bf["w_up"], False), (bf["w_down"], False)])
    h, y_mla, mixed_t = _merge(attn, gs, gm, y_ssm, x, w_o_mla, w_out)
    hn, dout, hn_t, loss = _mlp_fwd_loss(h, target, small["norm_mlp"], w_up, w_down)

    da, dh, dout_b, hid_t, d_norm_mlp = _mlp_bwd(dout, hn, h, small["norm_mlp"], w_up, w_down)
    p_w_down = _matmul_tn_shards(hid_t, dout_b, "dw_down", False, tm=1024, turned=True)
    p_w_up = _matmul_tn_shards(hn_t, da, "dw_up", True, turned=True)
    dgs, dgm, dy_ssm, dy_mla, dattn = _merge_bwd(dh, gs, gm, y_ssm, y_mla, w_out, w_o_mla)
    p_w_out = _matmul_tn_shards(mixed_t, dh, "dw_out", False, tm=1024, turned=True)
    p_w_o_mla = _matmul_tn_shards(attn_t, dy_mla, "dw_o_mla", False, turned=True)
    dq, dk, dv, l_w_up, l_w_down, l_w_out, l_w_o_mla = _attn_bwd(
        q, k, kt, v, attn, lse, dattn, xch=[(p_w_up, True), (p_w_down, True), (p_w_out, True), (p_w_o_mla, True)])
    dql, dkvl, d_q_a_norm, d_kv_a_norm, d_gq, d_gk, g_wq, g_wkv = _qkv_prep_bwd(
        ql, kvl, dq, dk, dv, small["q_a_norm"], small["kv_a_norm"], wq, wkv, gq, gk, cos_t, sin_t, xch=[])
    p_wq = jnp.transpose(g_wq.reshape(Q_LORA, N_HEADS, QK_PAD), (1, 0, 2)).astype(BF16)
    p_wkv = jnp.transpose(g_wkv.reshape(KV_LORA, N_HEADS, QK_PAD), (1, 0, 2)).astype(BF16)
    dy, d_b_glu, g_w_glu, g_w_o_ssm = _glu_bwd(dy_ssm, y, w_glu, small["b_glu"], w_o_ssm)
    p_w_o_ssm = jnp.transpose(g_w_o_ssm.reshape(SSM_WIDTH, N_DEV, OUT_SHARD), (1, 0, 2)).astype(BF16)
    p_w_glu = g_w_glu.reshape(N_DEV, SSM_WIDTH // N_DEV, SSM_WIDTH).astype(BF16)
    du, dlam, d_d, d_bblk, d_cblk_t, l_wq, l_wkv, l_w_glu, l_w_o_ssm = _ssm_bwd(
        u, dy, st, bblk, cblk, lam, d_row, xch=[(p_wq, True), (p_wkv, True), (p_w_glu, True), (p_w_o_ssm, True)])
    d_a_re, d_a_im, d_log_dt, d_bt_re, d_bt_im, d_c_re, d_c_im = _ssm_prep_bwd(
        a_re, a_im, log_dt, bt_re, bt_im, dlam, d_bblk, d_cblk_t)
    tr = lambda mat: jnp.transpose(mat.reshape(SSM_GROUPS, SSM_GROUP_CH, SSM_STATE), (0, 2, 1))
    g_small = {
        "q_a_norm": d_q_a_norm, "kv_a_norm": d_kv_a_norm, "q_norm": d_gq[:, :QK_HEAD], "k_norm": d_gk[:, :QK_HEAD],
        "ssm_a_re": d_a_re, "ssm_a_im": d_a_im, "ssm_log_dt": d_log_dt,
        "ssm_b_re": tr(d_bt_re), "ssm_b_im": tr(d_bt_im), "ssm_c_re": d_c_re, "ssm_c_im": d_c_im,
        "ssm_d": d_d, "b_glu": d_b_glu, "norm_mlp": d_norm_mlp,
    }
    rest = jnp.concatenate([_pack_small(g_small, SMALL[1:]), loss], axis=0)
    dx, d_norm_mix, g_w_in_pad, g_rest_all = _in_proj_bwd(
        (du, dql, dkvl, dgs, dgm), dh, x, xn_t, small["norm_mix"], w_in_pad, xch=[(rest, False)])
    parts = jnp.transpose(_unpad_in(g_w_in_pad).reshape(D_MODEL, N_DEV, IN_SHARD), (1, 0, 2))
    g_w_in_mine, g_first_all = _reduce_scatter(parts, _pack_small({SMALL[0]: d_norm_mix}, SMALL[:1]), "reduce_w_in")
    landed = {"w_q_b": l_wq[:, :, :QK_HEAD], "w_kv_b": l_wkv, "w_o_mla": l_w_o_mla, "w_glu": l_w_glu,
              "w_o_ssm": l_w_o_ssm, "w_out": l_w_out, "w_up": l_w_up, "w_down": l_w_down}
    return dx, landed, g_w_in_mine, g_first_all, g_rest_all


def kernel(x, positions, norm_mix, w_in, q_a_norm, kv_a_norm, w_q_b, w_kv_b, q_norm, k_norm, w_o_mla, ssm_a_re, ssm_a_im, ssm_log_dt, ssm_b_re, ssm_b_im, ssm_c_re, ssm_c_im, ssm_d, w_glu, b_glu, w_o_ssm, w_out, norm_mlp, w_up, w_down, loss_target, m_norm_mix, m_w_in, m_q_a_norm, m_kv_a_norm, m_w_q_b, m_w_kv_b, m_q_norm, m_k_norm, m_w_o_mla, m_ssm_a_re, m_ssm_a_im, m_ssm_log_dt, m_ssm_b_re, m_ssm_b_im, m_ssm_c_re, m_ssm_c_im, m_ssm_d, m_w_glu, m_b_glu, m_w_o_ssm, m_w_out, m_norm_mlp, m_w_up, m_w_down, v_norm_mix, v_w_in, v_q_a_norm, v_kv_a_norm, v_w_q_b, v_w_kv_b, v_q_norm, v_k_norm, v_w_o_mla, v_ssm_a_re, v_ssm_a_im, v_ssm_log_dt, v_ssm_b_re, v_ssm_b_im, v_ssm_c_re, v_ssm_c_im, v_ssm_d, v_w_glu, v_b_glu, v_w_o_ssm, v_w_out, v_norm_mlp, v_w_up, v_w_down):
    given = dict(locals())
    w = {n: given[n] for n in WEIGHT_ORDER}
    m = {n: given["m_" + n] for n in WEIGHT_ORDER}
    v = {n: given["v_" + n] for n in WEIGHT_ORDER}
    big = [n for n in WEIGHT_ORDER if n not in SMALL]
    small = {n: w[n] for n in SMALL}

    dx, landed, g_w_in, g_first_all, g_rest_all = _step(
        x[0], positions.reshape(-1, 1), loss_target[0], {n: w[n][0] for n in big}, small)

    grads, deltas, new_m, new_v = {}, {}, {}, {}
    for n in big:
        if n in ("w_in", "w_q_b"):
            wt, mt, vt = jnp.transpose(w[n][0]), jnp.transpose(m[n][0]), jnp.transpose(v[n][0])
            if n == "w_in":
                g = jnp.transpose(g_w_in)
                d, nm, nv = _adamw(wt, g, mt, vt, "adamw_" + n)
            else:
                g, d, nm, nv = _adamw_sum(jnp.transpose(landed[n], (0, 2, 1)), wt, mt, vt, "adamw_" + n)
            g, d, nm, nv = (jnp.transpose(a) for a in (g, d, nm, nv))
        else:
            g, d, nm, nv = _adamw_sum(landed[n], w[n][0], m[n][0], v[n][0], "adamw_" + n)
        grads[n], deltas[n], new_m[n], new_v[n] = g[None], d[None], nm[None], nv[None]

    outs = _adamw_small(g_first_all, g_rest_all, _pack_small(small), _pack_small({n: m[n] for n in SMALL}),
                        _pack_small({n: v[n] for n in SMALL}), _small_rows(small))
    for p, n in enumerate(SMALL):
        for k, dst in enumerate((grads, deltas, new_m, new_v)):
            dst[n] = outs[1 + 4 * p + k].reshape(-1)[:small[n].size].reshape(small[n].shape)

    return (outs[0][0, 0], dx[None], *[grads[n] for n in WEIGHT_ORDER], *[deltas[n] for n in WEIGHT_ORDER],
            *[new_m[n] for n in WEIGHT_ORDER], *[new_v[n] for n in WEIGHT_ORDER])
```

```python
import functools
import math

import numpy as np
import jax
import jax.numpy as jnp
from jax import lax
from jax.experimental import pallas as pl
from jax.experimental.pallas import tpu as pltpu

F32 = jnp.float32
BF16 = jnp.bfloat16

D_MODEL = 1024
SSM_GROUPS = 32
SSM_GROUP_CH = 16
SSM_WIDTH = 512
SSM_STATE = 64
N_STATE = SSM_GROUPS * SSM_STATE
N_HEADS = 8
QK_NOPE = 128
QK_ROPE = 64
QK_HEAD = 192
QK_PAD = 256
V_HEAD = 128
Q_LORA = 384
KV_LORA = 256
KV_LAT_PAD = 384
ROPE_THETA = 10000.0
D_FF = 4096
EPS = 1e-6
ATT_SCALE = QK_HEAD ** -0.5
N_DEV = 8
FF_SHARD = D_FF // N_DEV
OUT_SHARD = D_MODEL // N_DEV

IN_SEGS = ((0, 512), (512, 896), (896, 1280), (1280, 2304), (2304, 3328))
D_IN = 3264
D_IN_PAD = 3328
KV_END = 1216

ADAM_LR = 0.001
ADAM_B1 = 0.9
ADAM_B2 = 0.999
ADAM_EPS = 1e-08
ADAM_WD = 0.01
ADAM_STEP = 10

VMEM_LIMIT = 56 * 1024 * 1024
MESH = pl.DeviceIdType.MESH
ANY = pl.BlockSpec(memory_space=pl.ANY)
LANES = 128

SCAN_T = 256
SUBCHUNKS = 8
SCAN_CG = 512
ATT_T = 512
ATT_SUB = 1
ATT_HEADS = 4
ATT_BWD_HEADS = 2
ROW_T = 256
MM_T = 512


def _params(sem=None):
    return pltpu.CompilerParams(dimension_semantics=sem, vmem_limit_bytes=VMEM_LIMIT)


def _rows(arr, tm):
    if arr.ndim == 2:
        return pl.BlockSpec((tm, arr.shape[1]), lambda i: (i, 0))
    return pl.BlockSpec((arr.shape[0], tm, arr.shape[2]), lambda i: (0, i, 0))


def _const(arr):
    nd = arr.ndim
    return pl.BlockSpec(arr.shape, lambda *_: (0,) * nd, pipeline_mode=pl.Buffered(1))


def _sds(shape, dtype):
    return jax.ShapeDtypeStruct(shape, dtype)


PEERS = tuple((dx, dy, dc) for dx in (0, 1) for dy in (0, 1) for dc in (0, 1) if (dx, dy, dc) != (0, 0, 0))


def _here():
    x, y, c = lax.axis_index("x"), lax.axis_index("y"), lax.axis_index("c")
    return x, y, c, 4 * x + 2 * y + c


def _xchg_start(scatter, srcs, dsts, send, recv, local):
    x, y, c, me = _here()
    for e, sc in enumerate(scatter):
        src, dst = srcs[e], dsts[e]
        pltpu.make_async_copy(src.at[me] if sc else src, dst.at[me], local.at[e]).start()
        for dx, dy, dc in PEERS:
            px, py, pc = (1 - x if dx else x), (1 - y if dy else y), (1 - c if dc else c)
            pltpu.make_async_remote_copy(
                src_ref=src.at[4 * px + 2 * py + pc] if sc else src, dst_ref=dst.at[me],
                send_sem=send.at[e], recv_sem=recv.at[e], device_id=(px, py, pc), device_id_type=MESH).start()


def _xchg_wait(scatter, srcs, dsts, send, recv, local):
    x, y, c, me = _here()
    for e, sc in enumerate(scatter):
        src, dst = srcs[e], dsts[e]
        pltpu.make_async_copy(src.at[me] if sc else src, dst.at[me], local.at[e]).wait()
        span = dst.at[pl.ds(0, N_DEV - 1)]
        both = pltpu.make_async_remote_copy(src_ref=span, dst_ref=span, send_sem=send.at[e], recv_sem=recv.at[e],
                                            device_id=(x, y, c), device_id_type=MESH)
        both.wait_send()
        both.wait_recv()


def _call(body, name, grid, ins, in_specs, outs, out_specs, scratch=(), xch=()):
    n_in, n_out, ne = len(ins), len(outs), len(xch)
    scatter = [sc for _, sc in xch]
    x_outs = [_sds((N_DEV,) + (a.shape[1:] if sc else a.shape), a.dtype) for a, sc in xch]
    sems = [pltpu.SemaphoreType.DMA((ne,))] * 3 if ne else []

    def wrapped(*refs):
        in_refs, x_src = refs[:n_in], refs[n_in:n_in + ne]
        out_refs = refs[n_in + ne:n_in + ne + n_out]
        x_dst = refs[n_in + ne + n_out:n_in + 2 * ne + n_out]
        rest = refs[n_in + 2 * ne + n_out:]
        if ne:
            x_sems, rest = rest[len(rest) - 3:], rest[:len(rest) - 3]
            first = functools.reduce(jnp.logical_and, [pl.program_id(d) == 0 for d in range(len(grid))])
            last = functools.reduce(jnp.logical_and, [pl.program_id(d) == grid[d] - 1 for d in range(len(grid))])

            @pl.when(first)
            def _():
                _xchg_start(scatter, x_src, x_dst, *x_sems)

        body(*in_refs, *out_refs, *rest)
        if ne:
            @pl.when(last)
            def _():
                _xchg_wait(scatter, x_src, x_dst, *x_sems)

    return pl.pallas_call(
        wrapped,
        name=name,
        grid=grid,
        in_specs=list(in_specs) + [ANY] * ne,
        out_specs=list(out_specs) + [ANY] * ne,
        out_shape=list(outs) + x_outs,
        scratch_shapes=list(scratch) + sems,
        compiler_params=_params(("arbitrary",) * len(grid)),
    )(*ins, *[a for a, _ in xch])


def _row_call(body, name, n_rows, tm, row_ins, const_ins, row_outs, acc_outs=(), xch=(), col_outs=(), scratch=()):
    outs = [_sds(s, d) for s, d in list(row_outs) + list(col_outs) + list(acc_outs)]
    n_row, n_col = len(row_outs), len(col_outs)
    out_specs = [_rows(o, tm) for o in outs[:n_row]] + [
        pl.BlockSpec(o.shape[:-1] + (tm,), lambda i, nd=len(o.shape): (0,) * (nd - 1) + (i,))
        for o in outs[n_row:n_row + n_col]] + [
        pl.BlockSpec(o.shape, lambda i, nd=len(o.shape): (0,) * nd) for o in outs[n_row + n_col:]]
    in_specs = [_rows(a, tm) for a in row_ins] + [_const(a) for a in const_ins]
    return _call(body, name, (n_rows // tm,), list(row_ins) + list(const_ins), in_specs, outs, out_specs,
                 scratch=scratch, xch=xch)


def _dot(a, b):
    return jnp.dot(a, b, preferred_element_type=F32)


def _dot_nt(a, b):
    return lax.dot_general(a, b, (((1,), (1,)), ((), ())), preferred_element_type=F32)


def _dot_tn(a, b):
    return lax.dot_general(a, b, (((0,), (0,)), ((), ())), preferred_element_type=F32)


def _rms(x, g, n):
    inv = lax.rsqrt(jnp.sum(x * x, -1, keepdims=True) * (1.0 / n) + EPS)
    return x * inv * g, inv


def _rms_bwd(dy, x, g, inv, n):
    xh = x * inv
    dxh = dy * g
    dx = inv * (dxh - xh * (jnp.sum(dxh * xh, -1, keepdims=True) * (1.0 / n)))
    return dx, dy * xh


def _sigmoid(x):
    return 1.0 / (1.0 + jnp.exp(-x))


_GELU_C = math.sqrt(2.0 / math.pi)


def _gelu(y):
    th = jnp.tanh(_GELU_C * (y + 0.044715 * (y * y * y)))
    return 0.5 * y * (1.0 + th), th


def _gelu_grad(y, th):
    return 0.5 * (1.0 + th) + 0.5 * y * (1.0 - th * th) * (_GELU_C * (1.0 + 3.0 * 0.044715 * (y * y)))


def _acc(ref, val):
    @pl.when(pl.program_id(0) == 0)
    def _():
        ref[...] = jnp.zeros_like(ref)

    ref[...] += val


def _tile(n, limit):
    if n <= limit:
        return n
    return max(t for t in range(128, limit + 1, 128) if n % t == 0)


def _lhs(a, turned, tm, tk):
    m, k_dim = a.shape if turned else a.shape[::-1]
    tm, tk = _tile(m, tm), _tile(k_dim, tk)
    if turned:
        return m, k_dim, tm, tk, pl.BlockSpec((tm, tk), lambda i, k: (i, k)), _dot
    return m, k_dim, tm, tk, pl.BlockSpec((tk, tm), lambda i, k: (k, i)), _dot_tn


def _matmul_tn_shards(a, b, name, by_col, tm=512, tk=512, turned=False):
    m, k_dim, tm, tk, a_spec, dot = _lhs(a, turned, tm, tk)
    n = b.shape[1]
    nk = k_dim // tk
    if by_col:
        r, c = m, n // N_DEV
        out_spec = pl.BlockSpec((N_DEV, tm, c), lambda i, k: (0, i, 0))
    else:
        r, c = m // N_DEV, n
        per = tm // r
        out_spec = pl.BlockSpec((per, r, c), lambda i, k: (i, 0, 0))

    def body(a_ref, b_ref, o_ref, acc_ref):
        k = pl.program_id(1)

        @pl.when(k == 0)
        def _():
            acc_ref[...] = jnp.zeros_like(acc_ref)

        acc_ref[...] += dot(a_ref[...].astype(BF16), b_ref[...].astype(BF16))

        @pl.when(k == nk - 1)
        def _():
            if by_col:
                for j in range(N_DEV):
                    o_ref[j] = acc_ref[:, j * c:(j + 1) * c].astype(BF16)
            else:
                for s in range(per):
                    o_ref[s] = acc_ref[s * r:(s + 1) * r, :].astype(BF16)

    return pl.pallas_call(
        body,
        name=name,
        grid=(m // tm, nk),
        in_specs=[a_spec, pl.BlockSpec((tk, n), lambda i, k: (k, 0))],
        out_specs=out_spec,
        out_shape=_sds((N_DEV, r, c), BF16),
        scratch_shapes=[pltpu.VMEM((tm, n), F32)],
        compiler_params=_params(("parallel", "arbitrary")),
    )(a, b)


def _rope_tables(pos_col):
    n = pos_col.shape[0]
    half = QK_ROPE // 2
    inv_freq = (ROPE_THETA ** (-np.arange(half, dtype=np.float32) / half)).astype(np.float32)
    freq_row = jnp.asarray(np.concatenate([inv_freq, inv_freq, np.zeros(64, np.float32)])[None, :])

    def body(p_ref, f_ref, c_ref, s_ref):
        ang = p_ref[...].astype(F32) * f_ref[...]
        c_ref[...] = jnp.cos(ang)
        s_ref[...] = jnp.sin(ang)

    return _row_call(body, "rope_tables", n, min(n, 1024), [pos_col], [freq_row], [((n, 128), F32)] * 2)


def _rope_rot(v):
    lane = lax.broadcasted_iota(jnp.int32, v.shape, 1)
    return jnp.where(lane < 32, -pltpu.roll(v, 96, 1), jnp.where(lane < 64, pltpu.roll(v, 32, 1), 0.0))


def _rope_rot_t(v):
    lane = lax.broadcasted_iota(jnp.int32, v.shape, 1)
    return jnp.where(lane < 32, pltpu.roll(v, 96, 1), jnp.where(lane < 64, -pltpu.roll(v, 32, 1), 0.0))


def _in_proj(x, norm_mix, w_in_pad, xch):
    n = x.shape[0]

    def body(x_ref, g_ref, w_ref, u_ref, ql_ref, kvl_ref, gs_ref, gm_ref, xnt_ref):
        xn, _ = _rms(x_ref[...], g_ref[...], D_MODEL)
        xb = xn.astype(BF16)
        xnt_ref[...] = xn.T.astype(BF16)
        for ref, (a, b) in zip((u_ref, ql_ref, kvl_ref, gs_ref, gm_ref), IN_SEGS):
            ref[...] = _dot(xb, w_ref[:, a:b])

    outs = [((n, b - a), F32) for a, b in IN_SEGS]
    return _row_call(body, "in_proj", n, MM_T, [x], [norm_mix, w_in_pad], outs, xch=xch,
                     col_outs=[((D_MODEL, n), BF16)])


def _ssm_prep_fn(a_re, a_im, log_dt, b_re_x, b_im_x):
    dt = jnp.exp(log_dt)
    mag = jnp.exp(a_re * dt)
    lr = mag * jnp.cos(a_im * dt)
    li = mag * jnp.sin(a_im * dt)
    den = a_re * a_re + a_im * a_im
    fr = ((lr - 1.0) * a_re + li * a_im) / den
    fi = (li * a_re - (lr - 1.0) * a_im) / den
    return lr, li, fr * b_re_x - fi * b_im_x, fr * b_im_x + fi * b_re_x


def _dot_exact(a, b, dims):
    return lax.dot_general(a, b, (dims, ((), ())), precision=lax.Precision.HIGHEST, preferred_element_type=F32)


def _lane_repeat(width, n):
    src = lax.broadcasted_iota(jnp.int32, (width, n), 0)
    dst = lax.broadcasted_iota(jnp.int32, (width, n), 1)
    return (dst % width == src).astype(F32)


def _same_group(rows, rows_per_group, cols, cols_per_group):
    row = lax.broadcasted_iota(jnp.int32, (rows, cols), 0)
    col = lax.broadcasted_iota(jnp.int32, (rows, cols), 1)
    return (row // rows_per_group) == (col // cols_per_group)


def _expand_b(bt):
    tiled = _dot_exact(bt, _lane_repeat(SSM_STATE, N_STATE), ((1,), (0,)))
    return jnp.where(_same_group(SSM_WIDTH, SSM_GROUP_CH, N_STATE, SSM_STATE), tiled, 0.0)


def _collect_b(m):
    masked = jnp.where(_same_group(SSM_WIDTH, SSM_GROUP_CH, N_STATE, SSM_STATE), m, 0.0)
    return _dot_exact(masked, _lane_repeat(SSM_STATE, N_STATE), ((1,), (1,)))


def _ssm_prep(a_re, a_im, log_dt, bt_re, bt_im, c2_re, c2_im):
    def body(ar, ai, ld, br, bi, cr, ci, lam_ref, bblk_ref, cblk_ref):
        lr, li, bbr, bbi = _ssm_prep_fn(ar[...], ai[...], ld[...], _expand_b(br[...]), _expand_b(bi[...]))
        lam_ref[0:1, :] = lr
        lam_ref[1:2, :] = li
        bblk_ref[:, 0:N_STATE] = bbr.astype(BF16)
        bblk_ref[:, N_STATE:] = bbi.astype(BF16)
        rep = _lane_repeat(SSM_GROUP_CH, SSM_WIDTH)
        own = _same_group(N_STATE, SSM_STATE, SSM_WIDTH, SSM_GROUP_CH)
        cblk_ref[0:N_STATE, :] = jnp.where(own, _dot_exact(cr[...], rep, ((1,), (0,))), 0.0).astype(BF16)
        cblk_ref[N_STATE:, :] = jnp.where(own, -_dot_exact(ci[...], rep, ((1,), (0,))), 0.0).astype(BF16)

    return pl.pallas_call(
        body,
        name="ssm_prep",
        out_shape=[_sds((2, N_STATE), F32), _sds((SSM_WIDTH, 2 * N_STATE), BF16),
                   _sds((2 * N_STATE, SSM_WIDTH), BF16)],
        compiler_params=_params(),
    )(a_re, a_im, log_dt, bt_re, bt_im, c2_re, c2_im)


def _ssm_prep_bwd(a_re, a_im, log_dt, bt_re, bt_im, dlam, dbblk, dcblk_t):
    def body(ar, ai, ld, br, bi, dl, db, dc, dar, dai, dld, dbr, dbi, dcr, dci):
        _, vjp = jax.vjp(_ssm_prep_fn, ar[...], ai[...], ld[...], _expand_b(br[...]), _expand_b(bi[...]))
        g = vjp((dl[0:1, :], dl[1:2, :], db[:, 0:N_STATE], db[:, N_STATE:]))
        dar[...] = g[0]
        dai[...] = g[1]
        grp = lax.broadcasted_iota(jnp.int32, (SSM_GROUPS, N_STATE), 0)
        lane = lax.broadcasted_iota(jnp.int32, (SSM_GROUPS, N_STATE), 1)
        sel = (lane // SSM_STATE) == grp
        dld[...] = jnp.sum(jnp.where(sel, jnp.broadcast_to(g[2], (SSM_GROUPS, N_STATE)), 0.0), axis=1, keepdims=True)
        dbr[...] = _collect_b(g[3])
        dbi[...] = _collect_b(g[4])
        dcr[...] = _collect_b(dc[:, 0:N_STATE])
        dci[...] = -_collect_b(dc[:, N_STATE:])

    small = _sds((SSM_WIDTH, SSM_STATE), F32)
    return pl.pallas_call(
        body,
        name="ssm_prep_bwd",
        out_shape=[_sds((1, N_STATE), F32), _sds((1, N_STATE), F32), _sds((SSM_GROUPS, 1), F32), small, small, small, small],
        compiler_params=_params(),
    )(a_re, a_im, log_dt, bt_re, bt_im, dlam, dbblk, dcblk_t)


def _perm_matrix(t):
    run = t // SUBCHUNKS
    p = np.zeros((t, t), np.float32)
    r = np.arange(t)
    p[r, (r % SUBCHUNKS) * run + r // SUBCHUNKS] = 1.0
    return jnp.asarray(p, dtype=BF16)


def _unpermute(p, a):
    hi = a.astype(BF16)
    r1 = a - hi.astype(F32)
    mid = r1.astype(BF16)
    lo = (r1 - mid.astype(F32)).astype(BF16)
    return _dot_tn(p, hi) + _dot_tn(p, mid) + _dot_tn(p, lo)


def _power_table(lam_ref, pw_ref, n):
    lr, li = lam_ref[0:1, :], lam_ref[1:2, :]
    pw_ref[0:1, 0:N_STATE] = lr
    pw_ref[0:1, N_STATE:] = li

    def step(i, carry):
        pr, pi = carry
        pr, pi = pr * lr - pi * li, pr * li + pi * lr
        pw_ref[pl.ds(i, 1), 0:N_STATE] = pr
        pw_ref[pl.ds(i, 1), N_STATE:] = pi
        return pr, pi

    lax.fori_loop(1, n, step, (lr, li))


def _col_groups():
    return [(pl.ds(c, SCAN_CG), pl.ds(N_STATE + c, SCAN_CG)) for c in range(0, N_STATE, SCAN_CG)]


def _run_scan(buf, lam_ref, t, reverse):
    nblk = t // 8
    for re, im in _col_groups():
        lr = jnp.broadcast_to(lam_ref[0:1, re], (8, SCAN_CG))
        li = jnp.broadcast_to(lam_ref[1:2, re], (8, SCAN_CG))
        if reverse:
            li = -li
        first = pl.ds((nblk - 1) * 8 if reverse else 0, 8)

        def step(k, carry, re=re, im=im, lr=lr, li=li):
            pr, pi = carry
            i = (nblk - 2 - k) if reverse else (k + 1)
            r = pl.ds(pl.multiple_of(i * 8, 8), 8)
            xr = buf[r, re] + lr * pr - li * pi
            xi = buf[r, im] + lr * pi + li * pr
            buf[r, re] = xr
            buf[r, im] = xi
            return xr, xi

        lax.fori_loop(0, nblk - 1, step, (buf[first, re], buf[first, im]))


def _run_carries(buf, pw_ref, carry_ref, s_ref, t, reverse):
    nblk = t // 8
    run = t // SUBCHUNKS
    edge = buf[pl.ds(0 if reverse else (nblk - 1) * 8, 8), :]
    pr, pi = pw_ref[run - 1:run, 0:N_STATE], pw_ref[run - 1:run, N_STATE:]
    if reverse:
        pi = -pi
    sr, si = carry_ref[0:1, 0:N_STATE], carry_ref[0:1, N_STATE:]
    for s in (range(SUBCHUNKS - 1, -1, -1) if reverse else range(SUBCHUNKS)):
        s_ref[s:s + 1, 0:N_STATE] = sr
        s_ref[s:s + 1, N_STATE:] = si
        er, ei = edge[s:s + 1, 0:N_STATE], edge[s:s + 1, N_STATE:]
        sr, si = er + pr * sr - pi * si, ei + pr * si + pi * sr
    carry_ref[:, 0:N_STATE] = jnp.broadcast_to(sr, (8, N_STATE))
    carry_ref[:, N_STATE:] = jnp.broadcast_to(si, (8, N_STATE))


def _run_fix(buf, pw_ref, s_ref, t, reverse):
    nblk = t // 8
    for re, im in _col_groups():
        sr, si = s_ref[:, re], s_ref[:, im]

        def step(i, carry, re=re, im=im, sr=sr, si=si):
            r = pl.ds(pl.multiple_of(i * 8, 8), 8)
            row = pl.ds((nblk - 1 - i) if reverse else i, 1)
            pr, pi = pw_ref[row, re], pw_ref[row, im]
            if reverse:
                pi = -pi
            buf[r, re] += pr * sr - pi * si
            buf[r, im] += pr * si + pi * sr
            return carry

        lax.fori_loop(0, nblk, step, 0)


STATE_BLOCKS = 2 * N_STATE // LANES
CH_BLOCKS = SSM_WIDTH // LANES


def _state_block(b):
    pair = b % (N_STATE // LANES)
    k = (pair * 2 * SSM_GROUP_CH) // LANES
    return slice(b * LANES, (b + 1) * LANES), slice(k * LANES, (k + 1) * LANES)


def _channel_block(c):
    w = N_STATE // CH_BLOCKS
    return slice(c * LANES, (c + 1) * LANES), slice(c * w, (c + 1) * w), slice(N_STATE + c * w, N_STATE + (c + 1) * w)


def _to_states(vb, w_ref, buf, nt):
    for b in range(STATE_BLOCKS):
        lanes, ch = _state_block(b)
        buf[:, lanes] = _dot_nt(vb[:, ch], w_ref[lanes, ch]) if nt else _dot(vb[:, ch], w_ref[ch, lanes])


def _to_channels(buf, w_ref, nt):
    outs = []
    for c in range(CH_BLOCKS):
        ch, re, im = _channel_block(c)
        xr, xi = buf[:, re].astype(BF16), buf[:, im].astype(BF16)
        if nt:
            outs.append(_dot_nt(xr, w_ref[ch, re]) + _dot_nt(xi, w_ref[ch, im]))
        else:
            outs.append(_dot(xr, w_ref[re, ch]) + _dot(xi, w_ref[im, ch]))
    return jnp.concatenate(outs, axis=-1)


def _ssm_fwd(u, bblk, cblk, lam, d_row, w_glu, b_glu, w_o_ssm, xch):
    n = u.shape[0]
    t = min(SCAN_T, n)
    perm = _perm_matrix(t)

    def body(u_ref, p_ref, bblk_ref, cblk_ref, lam_ref, d_ref, wg_ref, bg_ref, wo_ref, y_ref, ys_ref, st_ref,
             buf, pw_ref, carry_ref, s_ref):
        @pl.when(pl.program_id(0) == 0)
        def _():
            carry_ref[...] = jnp.zeros_like(carry_ref)
            _power_table(lam_ref, pw_ref, t // SUBCHUNKS)

        st_ref[0] = carry_ref[...]
        u_t = u_ref[...]
        p = p_ref[...]
        ub = _dot(p, u_t.astype(BF16)).astype(BF16)
        _to_states(ub, bblk_ref, buf, False)
        _run_scan(buf, lam_ref, t, False)
        _run_carries(buf, pw_ref, carry_ref, s_ref, t, False)
        _run_fix(buf, pw_ref, s_ref, t, False)
        y = d_ref[...] * u_t + _unpermute(p, _to_channels(buf, cblk_ref, False))
        y_ref[...] = y
        z, _ = _gelu(y)
        s = _sigmoid(_dot(z.astype(BF16), wg_ref[...]) + bg_ref[...])
        zgb = (z * s).astype(BF16)
        for j in range(N_DEV):
            ys_ref[:, j * OUT_SHARD:(j + 1) * OUT_SHARD] = _dot(zgb, wo_ref[j])

    consts = [perm, bblk, cblk, lam, d_row, w_glu, b_glu, w_o_ssm]
    return _call(
        body, "ssm_fwd", (n // t,), [u] + consts, [_rows(u, t)] + [_const(a) for a in consts],
        [_sds((n, SSM_WIDTH), F32), _sds((n, D_MODEL), F32), _sds((n // t, 8, 2 * N_STATE), F32)],
        [pl.BlockSpec((t, SSM_WIDTH), lambda i: (i, 0)), pl.BlockSpec((t, D_MODEL), lambda i: (i, 0)),
         pl.BlockSpec((1, 8, 2 * N_STATE), lambda i: (i, 0, 0))],
        scratch=[pltpu.VMEM((t, 2 * N_STATE), F32), pltpu.VMEM((t // SUBCHUNKS, 2 * N_STATE), F32),
                 pltpu.VMEM((8, 2 * N_STATE), F32), pltpu.VMEM((8, 2 * N_STATE), F32)],
        xch=xch)


def _head_norm_rope(slab, gain, cos_t, sin_t):
    xn, inv = _rms(slab, gain, QK_HEAD)
    lo, hi = xn[:, 0:128], xn[:, 128:256]
    return jnp.concatenate([lo, hi * cos_t + _rope_rot(hi) * sin_t], axis=-1), inv


def _head_norm_rope_bwd(g, slab, gain, inv, cos_t, sin_t):
    g_lo, g_hi = g[:, 0:128], g[:, 128:256]
    g_n = jnp.concatenate([g_lo, g_hi * cos_t + _rope_rot_t(g_hi * sin_t)], axis=-1)
    return _rms_bwd(g_n, slab, gain, inv, QK_HEAD)


def _qkv_prep(ql, kvl, q_a_norm, kv_a_norm, wq, wkv, gq, gk, cos_t, sin_t):
    n = ql.shape[0]
    tm = ROW_T

    def body(ql_ref, kvl_ref, cos_ref, sin_ref, qa_ref, ka_ref, wq_ref, wkv_ref, gq_ref, gk_ref,
             q_ref, k_ref, v_ref, kt_ref, vt_ref):
        cos_t, sin_t = cos_ref[...], sin_ref[...]
        qa, _ = _rms(ql_ref[...], qa_ref[...], Q_LORA)
        qab = qa.astype(BF16)
        kvl_t = kvl_ref[...]
        ca, _ = _rms(kvl_t[:, 0:KV_LORA], ka_ref[...], KV_LORA)
        cab = ca.astype(BF16)
        kpe = kvl_t[:, KV_LORA:KV_LAT_PAD]
        q_pre = _dot(qab, wq_ref[...])
        kv_pre = _dot(cab, wkv_ref[...])
        for h in range(N_HEADS):
            qh, _ = _head_norm_rope(q_pre[:, h * QK_PAD:(h + 1) * QK_PAD], gq_ref[...], cos_t, sin_t)
            q_ref[h] = (qh * ATT_SCALE).astype(BF16)
            kv_h = kv_pre[:, h * QK_PAD:(h + 1) * QK_PAD]
            kh, _ = _head_norm_rope(jnp.concatenate([kv_h[:, 0:QK_NOPE], kpe], axis=-1), gk_ref[...], cos_t, sin_t)
            k_ref[h] = kh.astype(BF16)
            kt_ref[h] = kh.T.astype(BF16)
            vh = kv_h[:, QK_NOPE:]
            v_ref[h] = vh.astype(BF16)
            vt_ref[h] = vh.T.astype(BF16)

    row_ins, consts = [ql, kvl, cos_t, sin_t], [q_a_norm, kv_a_norm, wq, wkv, gq, gk]
    outs = [_sds((N_HEADS, n, QK_PAD), BF16), _sds((N_HEADS, n, QK_PAD), BF16), _sds((N_HEADS, n, V_HEAD), BF16),
            _sds((N_HEADS, QK_PAD, n), BF16), _sds((N_HEADS, V_HEAD, n), BF16)]
    out_specs = [_rows(o, tm) for o in outs[:3]] + [
        pl.BlockSpec((N_HEADS, QK_PAD, tm), lambda i: (0, 0, i)), pl.BlockSpec((N_HEADS, V_HEAD, tm), lambda i: (0, 0, i))]
    return _call(body, "qkv_prep", (n // tm,), row_ins + consts,
                 [_rows(a, tm) for a in row_ins] + [_const(a) for a in consts], outs, out_specs)


def _causal_mask_t(st, t):
    key = lax.broadcasted_iota(jnp.int32, (t, t), 0)
    qry = lax.broadcasted_iota(jnp.int32, (t, t), 1)
    return jnp.where(key <= qry, st, -jnp.inf)


def _attn_fwd(q, k, vt, xch):
    n = q.shape[1]
    t = min(ATT_T, n)

    hp = ATT_HEADS

    def body(q_ref, k_ref, vt_ref, o_ref, lse_ref, ot_ref):
        i = pl.program_id(1)
        qts = [q_ref[g] for g in range(hp)]

        def kv_tile(j, carry, diag):
            ts = t // ATT_SUB
            sts = []
            for g in range(hp):
                for a in range(ATT_SUB):
                    r0 = pl.multiple_of(j * t + a * ts, ts)
                    st = _dot_nt(k_ref[g, pl.ds(r0, ts), :], qts[g])
                    if diag:
                        key = lax.broadcasted_iota(jnp.int32, (ts, t), 0) + a * ts
                        qry = lax.broadcasted_iota(jnp.int32, (ts, t), 1)
                        st = jnp.where(key <= qry, st, -jnp.inf)
                    sts.append(st)
            out = []
            for g in range(hp):
                m, l, acc = carry[g]
                for a in range(ATT_SUB):
                    st = sts[g * ATT_SUB + a]
                    r0 = pl.multiple_of(j * t + a * ts, ts)
                    m_new = jnp.maximum(m, jnp.max(st, 0, keepdims=True))
                    alpha = jnp.exp(m - m_new)
                    pt = jnp.exp(st - m_new)
                    l = alpha * l + jnp.sum(pt, 0, keepdims=True)
                    acc = alpha * acc + _dot(vt_ref[g, :, pl.ds(r0, ts)], pt.astype(BF16))
                    m = m_new
                out.append((m, l, acc))
            return tuple(out)

        one = (jnp.full((1, t), -jnp.inf, F32), jnp.zeros((1, t), F32), jnp.zeros((V_HEAD, t), F32))
        carry = lax.fori_loop(0, i, functools.partial(kv_tile, diag=False), (one,) * hp)
        for g, (m, l, acc) in enumerate(kv_tile(i, carry, True)):
            out_t = acc / l
            o_ref[:, g * V_HEAD:(g + 1) * V_HEAD] = out_t.T
            ot_ref[g * V_HEAD:(g + 1) * V_HEAD, :] = out_t.astype(BF16)
            lse_ref[g] = m + jnp.log(l)

    return _call(
        body, "attn_fwd", (N_HEADS // hp, n // t), [q, k, vt],
        [pl.BlockSpec((hp, t, QK_PAD), lambda h, i: (h, i, 0)), pl.BlockSpec((hp, n, QK_PAD), lambda h, i: (h, 0, 0)),
         pl.BlockSpec((hp, V_HEAD, n), lambda h, i: (h, 0, 0))],
        [_sds((n, N_HEADS * V_HEAD), F32), _sds((N_HEADS, 1, n), F32), _sds((N_HEADS * V_HEAD, n), BF16)],
        [pl.BlockSpec((t, hp * V_HEAD), lambda h, i: (i, h)), pl.BlockSpec((hp, 1, t), lambda h, i: (h, 0, i)),
         pl.BlockSpec((hp * V_HEAD, t), lambda h, i: (h, i))],
        xch=xch)


def _merge(attn, gs, gm, y_ssm, x, w_o_mla, w_out):
    n = x.shape[0]

    def body(at_ref, gs_ref, gm_ref, ys_ref, x_ref, wo_ref, wout_ref, h_ref, ym_ref, mxt_ref):
        y_mla = _dot(at_ref[...].astype(BF16), wo_ref[...])
        ym_ref[...] = y_mla
        mixed = _sigmoid(gs_ref[...]) * ys_ref[...] + _sigmoid(gm_ref[...]) * y_mla
        mxt_ref[...] = mixed.T.astype(BF16)
        h_ref[...] = x_ref[...] + _dot(mixed.astype(BF16), wout_ref[...])

    outs = [((n, D_MODEL), F32), ((n, D_MODEL), F32)]
    return _row_call(body, "merge", n, MM_T, [attn, gs, gm, y_ssm, x], [w_o_mla, w_out], outs,
                     col_outs=[((D_MODEL, n), BF16)])


def _mlp_fwd_loss(h, target, norm_mlp, w_up, w_down):
    n = h.shape[0]

    def body(h_ref, t_ref, g_ref, wu_ref, wd_ref, hn_ref, do_ref, hnt_ref, loss_ref):
        h_t = h_ref[...]
        hn, _ = _rms(h_t, g_ref[...], D_MODEL)
        hb = hn.astype(BF16)
        hn_ref[...] = hb
        hnt_ref[...] = hn.T.astype(BF16)
        out = h_t
        for j in range(N_DEV):
            a = jnp.maximum(_dot(hb, wu_ref[j]), 0.0)
            out += _dot((a * a).astype(BF16), wd_ref[j])
        err = out - t_ref[...]
        do_ref[...] = err * (1.0 / D_MODEL)
        _acc(loss_ref, jnp.broadcast_to(jnp.sum(err * err) * (0.5 / D_MODEL), loss_ref.shape))

    outs = [((n, D_MODEL), BF16), ((n, D_MODEL), F32)]
    return _row_call(body, "mlp_fwd_loss", n, MM_T, [h, target], [norm_mlp, w_up, w_down], outs, [((8, 128), F32)],
                     col_outs=[((D_MODEL, n), BF16)])


def _mlp_bwd(dout, hn, h, norm_mlp, w_up, w_down):
    n = h.shape[0]

    def body(do_ref, hn_ref, h_ref, g_ref, wu_ref, wd_ref, da_ref, dh_ref, dob_ref, hidt_ref, dg_ref):
        dout_t = do_ref[...]
        doutb = dout_t.astype(BF16)
        dob_ref[...] = doutb
        hb = hn_ref[...]
        dhn = jnp.zeros_like(dout_t)
        for j in range(N_DEV):
            cols = slice(j * FF_SHARD, (j + 1) * FF_SHARD)
            a = jnp.maximum(_dot(hb, wu_ref[j]), 0.0)
            hidt_ref[cols, :] = (a * a).T.astype(BF16)
            da = (_dot_nt(doutb, wd_ref[j]) * (2.0 * a)).astype(BF16)
            da_ref[:, cols] = da
            dhn += _dot_nt(da, wu_ref[j])
        h_t = h_ref[...]
        inv = lax.rsqrt(jnp.sum(h_t * h_t, -1, keepdims=True) * (1.0 / D_MODEL) + EPS)
        dx, dg = _rms_bwd(dhn, h_t, g_ref[...], inv, D_MODEL)
        dh_ref[...] = dout_t + dx
        _acc(dg_ref, jnp.sum(dg, 0, keepdims=True))

    outs = [((n, D_FF), BF16), ((n, D_MODEL), F32), ((n, D_MODEL), BF16)]
    return _row_call(body, "mlp_bwd", n, MM_T, [dout, hn, h], [norm_mlp, w_up, w_down], outs, [((1, D_MODEL), F32)],
                     col_outs=[((D_FF, n), BF16)])


def _merge_bwd(dh, gs, gm, y_ssm, y_mla, attn, w_out, w_o_mla):
    n = dh.shape[0]

    def body(dh_ref, gs_ref, gm_ref, ys_ref, ym_ref, at_ref, wout_ref, wo_ref,
             dgs_ref, dgm_ref, dys_ref, dym_ref, dat_ref, delta_ref):
        dmix = _dot_nt(dh_ref[...].astype(BF16), wout_ref[...])
        sgs, sgm = _sigmoid(gs_ref[...]), _sigmoid(gm_ref[...])
        dgs_ref[...] = (dmix * ys_ref[...] * sgs * (1.0 - sgs)).astype(BF16)
        dgm_ref[...] = (dmix * ym_ref[...] * sgm * (1.0 - sgm)).astype(BF16)
        dys_ref[...] = (dmix * sgs).astype(BF16)
        dym = (dmix * sgm).astype(BF16)
        dym_ref[...] = dym
        dattn = _dot_nt(dym, wo_ref[...])
        dat_ref[...] = dattn.astype(BF16)
        prod = dattn * at_ref[...]
        ones = jnp.ones((8, V_HEAD), F32)
        for h in range(N_HEADS):
            delta_ref[h] = _dot_exact(ones, prod[:, h * V_HEAD:(h + 1) * V_HEAD], ((1,), (1,)))[0:1, :]

    outs = [((n, D_MODEL), BF16)] * 5
    return _row_call(body, "merge_bwd", n, MM_T, [dh, gs, gm, y_ssm, y_mla, attn], [w_out, w_o_mla], outs,
                     col_outs=[((N_HEADS, 1, n), F32)])


def _attn_bwd(q, k, kt, v, lse, delta, dout, xch):
    n = q.shape[1]
    t = min(ATT_T, n)
    nt = n // t
    hp = ATT_BWD_HEADS

    def body(q_ref, k_ref, kt_ref, v_ref, lse_ref, delta_ref, do_ref, dq_ref, dk_ref, dv_ref, dqt_ref):
        j = pl.program_id(1)

        @pl.when(j == 0)
        def _():
            dqt_ref[...] = jnp.zeros_like(dqt_ref)

        def q_tile(i, carry, diag):
            r0 = pl.multiple_of(i * t, t)
            rows = pl.ds(r0, t)
            qts = [q_ref[g, rows, :] for g in range(hp)]
            sts = [_dot_nt(k_ref[g], qts[g]) for g in range(hp)]
            out = []
            for g in range(hp):
                dk, dv = carry[g]
                st = _causal_mask_t(sts[g], t) if diag else sts[g]
                pt = jnp.exp(st - lse_ref[g, :, rows])
                dob = do_ref[rows, g * V_HEAD:(g + 1) * V_HEAD]
                dv = dv + _dot(pt.astype(BF16), dob)
                dst = (pt * (_dot_nt(v_ref[g], dob) - delta_ref[g, :, rows])).astype(BF16)
                dk = dk + _dot(dst, qts[g])
                dqt_ref[g, :, rows] += _dot(kt_ref[g], dst)
                out.append((dk, dv))
            return tuple(out)

        zero = (jnp.zeros((t, QK_PAD), F32), jnp.zeros((t, V_HEAD), F32))
        carry = q_tile(j, (zero,) * hp, True)
        carry = lax.fori_loop(j + 1, nt, functools.partial(q_tile, diag=False), carry)
        for g, (dk, dv) in enumerate(carry):
            dk_ref[g] = dk
            dv_ref[g] = dv

        @pl.when(j == nt - 1)
        def _():
            for g in range(hp):
                for c in range(0, n, t):
                    dq_ref[g, c:c + t, :] = dqt_ref[g, :, c:c + t].T

    return _call(
        body, "attn_bwd", (N_HEADS // hp, nt), [q, k, kt, v, lse, delta, dout],
        [pl.BlockSpec((hp, n, QK_PAD), lambda h, j: (h, 0, 0)), pl.BlockSpec((hp, t, QK_PAD), lambda h, j: (h, j, 0)),
         pl.BlockSpec((hp, QK_PAD, t), lambda h, j: (h, 0, j)), pl.BlockSpec((hp, t, V_HEAD), lambda h, j: (h, j, 0)),
         pl.BlockSpec((hp, 1, n), lambda h, j: (h, 0, 0)), pl.BlockSpec((hp, 1, n), lambda h, j: (h, 0, 0)),
         pl.BlockSpec((n, hp * V_HEAD), lambda h, j: (0, h))],
        [_sds((N_HEADS, n, QK_PAD), F32), _sds((N_HEADS, n, QK_PAD), F32), _sds((N_HEADS, n, V_HEAD), F32)],
        [pl.BlockSpec((hp, n, QK_PAD), lambda h, j: (h, 0, 0)), pl.BlockSpec((hp, t, QK_PAD), lambda h, j: (h, j, 0)),
         pl.BlockSpec((hp, t, V_HEAD), lambda h, j: (h, j, 0))],
        scratch=[pltpu.VMEM((hp, QK_PAD, n), F32)],
        xch=xch)


def _qkv_prep_bwd(ql, kvl, dq, dk, dv, q_a_norm, kv_a_norm, wq, wkv, gq, gk, cos_t, sin_t, xch):
    n = ql.shape[0]

    def body(ql_ref, kvl_ref, cos_ref, sin_ref, dq_ref, dk_ref, dv_ref, qa_ref, ka_ref, wq_ref, wkv_ref, gq_ref, gk_ref,
             dql_ref, dkvl_ref, dqa_ref, dka_ref, dgq_ref, dgk_ref, dwq_ref, dwkv_ref, dqp_ref, dkvp_ref):
        cos_t, sin_t = cos_ref[...], sin_ref[...]
        ql_t = ql_ref[...]
        qa, inv_qa = _rms(ql_t, qa_ref[...], Q_LORA)
        qab = qa.astype(BF16)
        kvl_t = kvl_ref[...]
        ckv = kvl_t[:, 0:KV_LORA]
        ca, inv_ca = _rms(ckv, ka_ref[...], KV_LORA)
        cab = ca.astype(BF16)
        kpe = kvl_t[:, KV_LORA:KV_LAT_PAD]
        dgq = jnp.zeros((1, QK_PAD), F32)
        dgk = jnp.zeros((1, QK_PAD), F32)
        dkpe = jnp.zeros_like(kpe)
        q_pre = _dot(qab, wq_ref[...])
        kv_pre = _dot(cab, wkv_ref[...])
        for h in range(N_HEADS):
            head = slice(h * QK_PAD, (h + 1) * QK_PAD)
            q_slab = q_pre[:, head]
            inv = lax.rsqrt(jnp.sum(q_slab * q_slab, -1, keepdims=True) * (1.0 / QK_HEAD) + EPS)
            d_slab, dg = _head_norm_rope_bwd(dq_ref[h] * ATT_SCALE, q_slab, gq_ref[...], inv, cos_t, sin_t)
            dqp_ref[:, head] = d_slab.astype(BF16)
            dgq += jnp.sum(dg, 0, keepdims=True)
            k_slab = jnp.concatenate([kv_pre[:, h * QK_PAD:h * QK_PAD + QK_NOPE], kpe], axis=-1)
            inv = lax.rsqrt(jnp.sum(k_slab * k_slab, -1, keepdims=True) * (1.0 / QK_HEAD) + EPS)
            d_slab, dg = _head_norm_rope_bwd(dk_ref[h], k_slab, gk_ref[...], inv, cos_t, sin_t)
            dkvp_ref[:, head] = jnp.concatenate([d_slab[:, 0:QK_NOPE], dv_ref[h]], axis=-1).astype(BF16)
            dkpe += d_slab[:, QK_NOPE:QK_PAD]
            dgk += jnp.sum(dg, 0, keepdims=True)
        dqa = _dot_nt(dqp_ref[...], wq_ref[...])
        dx, dg = _rms_bwd(dqa, ql_t, qa_ref[...], inv_qa, Q_LORA)
        dql_ref[...] = dx.astype(BF16)
        _acc(dqa_ref, jnp.sum(dg, 0, keepdims=True))
        dca = _dot_nt(dkvp_ref[...], wkv_ref[...])
        dx, dg = _rms_bwd(dca, ckv, ka_ref[...], inv_ca, KV_LORA)
        dkvl_ref[:, 0:KV_LORA] = dx.astype(BF16)
        dkvl_ref[:, KV_LORA:KV_LAT_PAD] = dkpe.astype(BF16)
        _acc(dka_ref, jnp.sum(dg, 0, keepdims=True))
        _acc(dgq_ref, dgq)
        _acc(dgk_ref, dgk)
        _acc(dwq_ref, _dot_tn(qab, dqp_ref[...]))
        _acc(dwkv_ref, _dot_tn(cab, dkvp_ref[...]))

    wide = N_HEADS * QK_PAD
    row_outs = [((n, Q_LORA), BF16), ((n, KV_LAT_PAD), BF16)]
    acc_outs = [((1, Q_LORA), F32), ((1, KV_LORA), F32), ((1, QK_PAD), F32), ((1, QK_PAD), F32),
                ((Q_LORA, wide), F32), ((KV_LORA, wide), F32)]
    return _row_call(body, "qkv_prep_bwd", n, ROW_T, [ql, kvl, cos_t, sin_t, dq, dk, dv],
                     [q_a_norm, kv_a_norm, wq, wkv, gq, gk], row_outs, acc_outs, xch=xch,
                     scratch=[pltpu.VMEM((ROW_T, wide), BF16), pltpu.VMEM((ROW_T, wide), BF16)])


def _glu_bwd(dy_ssm, y, w_glu, b_glu, w_o_ssm):
    n = y.shape[0]

    def body(dys_ref, y_ref, wg_ref, bg_ref, wo_ref, dy_ref, db_ref, dwg_ref, dwo_ref):
        y_t = y_ref[...]
        z, th = _gelu(y_t)
        zb = z.astype(BF16)
        s = _sigmoid(_dot(zb, wg_ref[...]) + bg_ref[...])
        dys = dys_ref[...]
        dzg = jnp.zeros_like(y_t)
        for j in range(N_DEV):
            dzg += _dot_nt(dys[:, j * OUT_SHARD:(j + 1) * OUT_SHARD], wo_ref[j])
        dt = dzg * z * s * (1.0 - s)
        dtb = dt.astype(BF16)
        dz = dzg * s + _dot_nt(dtb, wg_ref[...])
        dy_ref[...] = dz * _gelu_grad(y_t, th)
        _acc(db_ref, jnp.sum(dt, 0, keepdims=True))
        _acc(dwg_ref, _dot_tn(zb, dtb))
        _acc(dwo_ref, _dot_tn((z * s).astype(BF16), dys))

    acc_outs = [((1, SSM_WIDTH), F32), ((SSM_WIDTH, SSM_WIDTH), F32), ((SSM_WIDTH, D_MODEL), F32)]
    return _row_call(body, "glu_bwd", n, ROW_T, [dy_ssm, y], [w_glu, b_glu, w_o_ssm], [((n, SSM_WIDTH), F32)], acc_outs)


def _ssm_bwd(u, dy, st, bblk, cblk, lam, d_row, xch):
    n = u.shape[0]
    t = min(SCAN_T, n)
    nc = n // t
    kb = 512
    perm = _perm_matrix(t)

    def body(u_ref, dy_ref, st_ref, p_ref, bblk_ref, cblk_ref, lam_ref, d_ref,
             du_ref, dlam_ref, dd_ref, db_ref, dct_ref,
             buf_x, buf_a, pw_ref, carry_ref, xcarry_ref, sx_ref, sa_ref, db_acc, dct_acc):
        @pl.when(pl.program_id(0) == 0)
        def _():
            carry_ref[...] = jnp.zeros_like(carry_ref)
            db_acc[...] = jnp.zeros_like(db_acc)
            dct_acc[...] = jnp.zeros_like(dct_acc)
            _power_table(lam_ref, pw_ref, t // SUBCHUNKS)

        u_t = u_ref[...]
        dy_t = dy_ref[...]
        p = p_ref[...]
        ub = _dot(p, u_t.astype(BF16)).astype(BF16)
        dyb = _dot(p, dy_t.astype(BF16)).astype(BF16)
        _to_states(ub, bblk_ref, buf_x, False)
        xcarry_ref[...] = st_ref[0]
        _run_scan(buf_x, lam_ref, t, False)
        _run_carries(buf_x, pw_ref, xcarry_ref, sx_ref, t, False)
        _run_fix(buf_x, pw_ref, sx_ref, t, False)
        _to_states(dyb, cblk_ref, buf_a, True)
        _run_scan(buf_a, lam_ref, t, True)
        _run_carries(buf_a, pw_ref, carry_ref, sa_ref, t, True)
        _run_fix(buf_a, pw_ref, sa_ref, t, True)
        du_ref[...] = (d_ref[...] * dy_t + _unpermute(p, _to_channels(buf_a, bblk_ref, True))).astype(BF16)
        for b in range(STATE_BLOCKS):
            lanes, ch = _state_block(b)
            db_acc[ch, lanes] += _dot_tn(ub[:, ch], buf_a[:, lanes].astype(BF16))
            dct_acc[ch, lanes] += _dot_tn(dyb[:, ch], buf_x[:, lanes].astype(BF16))
        for c in range(0, N_STATE, kb):
            re, im = pl.ds(c, kb), pl.ds(N_STATE + c, kb)
            xr, xi = buf_x[pl.ds(0, t - 8), re], buf_x[pl.ds(0, t - 8), im]
            ar, ai = buf_a[pl.ds(8, t - 8), re], buf_a[pl.ds(8, t - 8), im]
            x0r, x0i = sx_ref[:, re], sx_ref[:, im]
            a0r, a0i = buf_a[0:8, re], buf_a[0:8, im]
            dlam_part_re = (jnp.sum(ar * xr + ai * xi, 0, keepdims=True)
                            + jnp.sum(a0r * x0r + a0i * x0i, 0, keepdims=True))
            dlam_part_im = (jnp.sum(ai * xr - ar * xi, 0, keepdims=True)
                            + jnp.sum(a0i * x0r - a0r * x0i, 0, keepdims=True))

            @pl.when(pl.program_id(0) == 0)
            def _(c=c):
                dlam_ref[0:1, c:c + kb] = jnp.zeros((1, kb), F32)
                dlam_ref[1:2, c:c + kb] = jnp.zeros((1, kb), F32)

            dlam_ref[0:1, c:c + kb] += dlam_part_re
            dlam_ref[1:2, c:c + kb] += dlam_part_im
        _acc(dd_ref, jnp.sum(dy_t * u_t, 0, keepdims=True))

        @pl.when(pl.program_id(0) == nc - 1)
        def _():
            pltpu.sync_copy(db_acc, db_ref)
            pltpu.sync_copy(dct_acc, dct_ref)

    rev = lambda i: (nc - 1 - i, 0)
    consts = [perm, bblk, cblk, lam, d_row]
    wide = (SSM_WIDTH, 2 * N_STATE)
    return _call(
        body, "ssm_bwd", (nc,), [u, dy, st] + consts,
        [pl.BlockSpec((t, SSM_WIDTH), rev), pl.BlockSpec((t, SSM_WIDTH), rev),
         pl.BlockSpec((1, 8, 2 * N_STATE), lambda i: (nc - 1 - i, 0, 0))] + [_const(a) for a in consts],
        [_sds((n, SSM_WIDTH), BF16), _sds((2, N_STATE), F32), _sds((1, SSM_WIDTH), F32), _sds(wide, F32), _sds(wide, F32)],
        [pl.BlockSpec((t, SSM_WIDTH), rev), pl.BlockSpec((2, N_STATE), lambda i: (0, 0)),
         pl.BlockSpec((1, SSM_WIDTH), lambda i: (0, 0)), ANY, ANY],
        scratch=[pltpu.VMEM((t, 2 * N_STATE), F32)] * 2 + [pltpu.VMEM((t // SUBCHUNKS, 2 * N_STATE), F32)]
        + [pltpu.VMEM((8, 2 * N_STATE), F32)] * 4 + [pltpu.VMEM(wide, F32)] * 2,
        xch=xch)


def _in_proj_bwd(pieces, dh, x, xn_t, norm_mix, w_in_pad, xch):
    n = x.shape[0]
    tm = min(MM_T, n)
    nt = n // tm

    def body(du_ref, dql_ref, dkvl_ref, dgs_ref, dgm_ref, dh_ref, x_ref, xnt_ref, g_ref, w_ref,
             dx_ref, dg_ref, dw_ref, acc_ref):
        @pl.when(pl.program_id(0) == 0)
        def _():
            acc_ref[...] = jnp.zeros_like(acc_ref)

        xnt = xnt_ref[...]
        dxn = jnp.zeros((tm, D_MODEL), F32)
        for ref, (a, b) in zip((du_ref, dql_ref, dkvl_ref, dgs_ref, dgm_ref), IN_SEGS):
            piece = ref[...]
            dxn += _dot_nt(piece, w_ref[:, a:b])
            acc_ref[:, a:b] += _dot(xnt, piece)
        x_t = x_ref[...]
        inv = lax.rsqrt(jnp.sum(x_t * x_t, -1, keepdims=True) * (1.0 / D_MODEL) + EPS)
        dx, dg = _rms_bwd(dxn, x_t, g_ref[...], inv, D_MODEL)
        dx_ref[...] = dh_ref[...] + dx
        _acc(dg_ref, jnp.sum(dg, 0, keepdims=True))

        @pl.when(pl.program_id(0) == nt - 1)
        def _():
            pltpu.sync_copy(acc_ref, dw_ref)

    row_ins, consts = list(pieces) + [dh, x], [norm_mix, w_in_pad]
    in_specs = ([_rows(a, tm) for a in row_ins] + [pl.BlockSpec((D_MODEL, tm), lambda i: (0, i))]
                + [_const(a) for a in consts])
    return _call(
        body, "in_proj_bwd", (nt,), row_ins + [xn_t] + consts, in_specs,
        [_sds((n, D_MODEL), F32), _sds((1, D_MODEL), F32), _sds((D_MODEL, D_IN_PAD), F32)],
        [pl.BlockSpec((tm, D_MODEL), lambda i: (i, 0)), pl.BlockSpec((1, D_MODEL), lambda i: (0, 0)), ANY],
        scratch=[pltpu.VMEM((D_MODEL, D_IN_PAD), F32)], xch=xch)


def _swap_minor(a):
    g, r, c = a.shape[1:]
    return jnp.transpose(a[0], (0, 2, 1)).reshape(g * c, r)


def _pad_in(w):
    return jnp.concatenate([w[:, :KV_END], jnp.zeros((w.shape[0], D_IN_PAD - D_IN), w.dtype), w[:, KV_END:]], axis=1)


def _unpad_in(w):
    return jnp.concatenate([w[:, :KV_END], w[:, KV_END + D_IN_PAD - D_IN:]], axis=1)


def _pad_gain(g):
    return jnp.pad(g, ((0, 0), (0, QK_PAD - QK_HEAD)))


def _place():
    x, y, c = lax.axis_index("x"), lax.axis_index("y"), lax.axis_index("c")
    chips = [(x, y), (1 - x, y), (x, 1 - y), (1 - x, 1 - y)]
    return x, y, c, chips


def _all_gather(block, name):
    rows, lanes = block.shape

    def body(x_ref, out_ref, send_sems, recv_sems, local_sem):
        x, y, c, chips = _place()
        me, sibling = (x, y, c), (x, y, 1 - c)

        def slot(px, py, pc):
            return out_ref.at[4 * px + 2 * py + pc]

        def copy(k, blk, to, src=None):
            return pltpu.make_async_remote_copy(
                src_ref=slot(*blk) if src is None else src, dst_ref=slot(*blk),
                send_sem=send_sems.at[k], recv_sem=recv_sems.at[k], device_id=to, device_id_type=MESH)

        mine = pltpu.make_async_copy(x_ref, slot(*me), local_sem)
        mine.start()
        first = [copy(0, me, sibling, src=x_ref)]
        first += [copy(1 + j, me, (*chip, c), src=x_ref) for j, chip in enumerate(chips[1:])]
        for cp in first:
            cp.start()
        passed = [copy(4 + j, (*chip, c), sibling) for j, chip in enumerate(chips[1:])]
        for j, chip in enumerate(chips[1:]):
            copy(1 + j, (*chip, c), me).wait_recv()
            passed[j].start()
        copy(0, sibling, me).wait_recv()
        for j, chip in enumerate(chips[1:]):
            copy(4 + j, (*chip, 1 - c), me).wait_recv()
        for cp in first + passed:
            cp.wait_send()
        mine.wait()

    return pl.pallas_call(
        body,
        name=name,
        in_specs=[ANY],
        out_specs=ANY,
        out_shape=_sds((N_DEV, rows, lanes), block.dtype),
        scratch_shapes=[pltpu.SemaphoreType.DMA((7,)), pltpu.SemaphoreType.DMA((7,)), pltpu.SemaphoreType.DMA],
    )(block)


def _reduce_scatter(parts, gather, name):
    _, rows, lanes = parts.shape

    def body(p_ref, g_ref, out_ref, ga_ref, own, land_a, send_b, land_b, sa, ra, sb, rb, lo, *g_sems):
        x, y, c, chips = _place()
        sibling = (x, y, 1 - c)
        _xchg_start([False], [g_ref], [ga_ref], *g_sems)

        def blk(chip, core):
            return p_ref.at[4 * chip[0] + 2 * chip[1] + core]

        to_sib = [pltpu.make_async_remote_copy(
            src_ref=blk(chips[k], 1 - c), dst_ref=land_a.at[k], send_sem=sa.at[k], recv_sem=ra.at[k],
            device_id=sibling, device_id_type=MESH) for k in range(4)]
        for cp in to_sib:
            cp.start()
        loads = [pltpu.make_async_copy(blk(chips[k], c), own.at[k], lo.at[k]) for k in range(4)]
        for cp in loads:
            cp.start()
        to_chip = [pltpu.make_async_remote_copy(
            src_ref=send_b.at[j], dst_ref=land_b.at[j], send_sem=sb.at[j], recv_sem=rb.at[j],
            device_id=(*chips[1 + j], c), device_id_type=MESH) for j in range(3)]
        for k in (1, 2, 3):
            to_sib[k].wait_recv()
            loads[k].wait()
            send_b[k - 1] = (own[k] + land_a[k]).astype(BF16)
            to_chip[k - 1].start()
        to_sib[0].wait_recv()
        loads[0].wait()
        acc = own[0] + land_a[0]
        for j in range(3):
            to_chip[j].wait_recv()
            acc = acc + land_b[j].astype(F32)
        out_ref[...] = acc
        for cp in to_sib + to_chip:
            cp.wait_send()
        _xchg_wait([False], [g_ref], [ga_ref], *g_sems)

    return pl.pallas_call(
        body,
        name=name,
        in_specs=[ANY, ANY],
        out_specs=[pl.BlockSpec(memory_space=pltpu.VMEM), ANY],
        out_shape=[_sds((rows, lanes), F32), _sds((N_DEV,) + gather.shape, gather.dtype)],
        scratch_shapes=[pltpu.VMEM((4, rows, lanes), F32), pltpu.VMEM((4, rows, lanes), F32),
                        pltpu.VMEM((3, rows, lanes), BF16), pltpu.VMEM((3, rows, lanes), BF16)]
        + [pltpu.SemaphoreType.DMA((4,))] * 2 + [pltpu.SemaphoreType.DMA((3,))] * 2 + [pltpu.SemaphoreType.DMA((4,))]
        + [pltpu.SemaphoreType.DMA((1,))] * 3,
        compiler_params=_params(),
    )(parts, gather)


def _adamw_math(w, g, m, v):
    m = ADAM_B1 * m + (1.0 - ADAM_B1) * g
    v = ADAM_B2 * v + (1.0 - ADAM_B2) * (g * g)
    m_hat = m / (1.0 - ADAM_B1 ** ADAM_STEP)
    v_hat = v / (1.0 - ADAM_B2 ** ADAM_STEP)
    delta = -ADAM_LR * (m_hat / (jnp.sqrt(v_hat) + ADAM_EPS) + ADAM_WD * w)
    return delta, m, v


def _row_tile(r):
    return max(t for t in range(8, min(r, 256) + 1, 8) if r % t == 0)


def _adamw(w, g, m, v, name):
    r, n = w.shape

    def body(w_ref, g_ref, m_ref, v_ref, d_ref, nm_ref, nv_ref):
        d_ref[...], nm_ref[...], nv_ref[...] = _adamw_math(w_ref[...], g_ref[...], m_ref[...], v_ref[...])

    return _row_call(body, name, r, _row_tile(r), [w, g, m, v], [], [((r, n), F32)] * 3)


def _adamw_sum(landed, w, m, v, name):
    r, n = w.shape

    def body(l_ref, w_ref, m_ref, v_ref, g_ref, d_ref, nm_ref, nv_ref):
        g = l_ref[0].astype(F32)
        for dev in range(1, N_DEV):
            g = g + l_ref[dev].astype(F32)
        g_ref[...] = g
        d_ref[...], nm_ref[...], nv_ref[...] = _adamw_math(w_ref[...], g, m_ref[...], v_ref[...])

    tm = max(t for t in range(16, min(r, 256) + 1, 16) if r % t == 0)
    return _row_call(body, name, r, tm, [landed, w, m, v], [], [((r, n), F32)] * 4)


def _adamw_small(first, rest, w, m, v, row_counts):
    n_rest = w.shape[0] - first.shape[1]

    def body(f_ref, r_ref, w_ref, m_ref, v_ref, loss_ref, *out_refs):
        gf, gr = f_ref[0], r_ref[0]
        for dev in range(1, N_DEV):
            gf, gr = gf + f_ref[dev], gr + r_ref[dev]
        loss_ref[...] = gr[n_rest:n_rest + 8]
        g = jnp.concatenate([gf, gr[0:n_rest]], axis=0)
        d, nm, nv = _adamw_math(w_ref[...], g, m_ref[...], v_ref[...])
        off = 0
        for p, rows in enumerate(row_counts):
            for k, val in enumerate((g, d, nm, nv)):
                out_refs[4 * p + k][...] = val[off:off + rows]
            off += rows

    outs = [_sds((8, LANES), F32)] + [_sds((rows, LANES), F32) for rows in row_counts for _ in range(4)]
    return pl.pallas_call(body, name="adamw_small", out_shape=outs, compiler_params=_params())(first, rest, w, m, v)


SMALL = ("norm_mix", "q_a_norm", "kv_a_norm", "q_norm", "k_norm", "ssm_a_re", "ssm_a_im", "ssm_log_dt", "ssm_b_re",
         "ssm_b_im", "ssm_c_re", "ssm_c_im", "ssm_d", "b_glu", "norm_mlp")
WEIGHT_ORDER = ("norm_mix", "w_in", "q_a_norm", "kv_a_norm", "w_q_b", "w_kv_b", "q_norm", "k_norm", "w_o_mla",
                "ssm_a_re", "ssm_a_im", "ssm_log_dt", "ssm_b_re", "ssm_b_im", "ssm_c_re", "ssm_c_im", "ssm_d", "w_glu",
                "b_glu", "w_o_ssm", "w_out", "norm_mlp", "w_up", "w_down")
IN_SHARD = D_IN // N_DEV


def _pack_small(vals, names=SMALL):
    parts = []
    for n in names:
        flat = vals[n].reshape(-1)
        size = -(-flat.shape[0] // (8 * LANES)) * 8 * LANES
        parts.append(jnp.pad(flat, (0, size - flat.shape[0])).reshape(-1, LANES))
    return jnp.concatenate(parts, axis=0)


def _small_rows(like):
    return [-(-like[n].size // (8 * LANES)) * 8 for n in SMALL]


def _step(x, pos_col, target, w, small):
    bf = {n: a.astype(BF16) for n, a in w.items()}
    gq, gk = _pad_gain(small["q_norm"]), _pad_gain(small["k_norm"])
    a_re = small["ssm_a_re"].reshape(1, N_STATE)
    a_im = small["ssm_a_im"].reshape(1, N_STATE)
    log_dt = jnp.repeat(small["ssm_log_dt"].reshape(SSM_GROUPS), SSM_STATE).reshape(1, N_STATE)
    bt_re, bt_im = _swap_minor(small["ssm_b_re"]), _swap_minor(small["ssm_b_im"])
    c2_re, c2_im = _swap_minor(small["ssm_c_re"]), _swap_minor(small["ssm_c_im"])
    d_row = small["ssm_d"].reshape(1, SSM_WIDTH)

    w_in_all = _all_gather(bf["w_in"], "gather_w_in")
    w_in_pad = _pad_in(jnp.transpose(w_in_all, (1, 0, 2)).reshape(D_MODEL, D_IN))
    cos_t, sin_t = _rope_tables(pos_col)
    lam, bblk, cblk = _ssm_prep(a_re, a_im, log_dt, bt_re, bt_im, c2_re, c2_im)
    wq_mine = jnp.pad(bf["w_q_b"], ((0, 0), (0, QK_PAD - QK_HEAD)))
    u, ql, kvl, gs, gm, xn_t, w_glu, w_o_ssm = _in_proj(
        x, small["norm_mix"], w_in_pad, xch=[(bf["w_glu"], False), (bf["w_o_ssm"], False)])
    w_glu = w_glu.reshape(SSM_WIDTH, SSM_WIDTH)
    y, y_ssm, st, wq, wkv, w_o_mla, w_out = _ssm_fwd(
        u, bblk, cblk, lam, d_row, w_glu, small["b_glu"], w_o_ssm,
        xch=[(wq_mine, False), (bf["w_kv_b"], False), (bf["w_o_mla"], False), (bf["w_out"], False)])
    w_o_mla, w_out = w_o_mla.reshape(D_MODEL, D_MODEL), w_out.reshape(D_MODEL, D_MODEL)
    wq = jnp.transpose(wq, (1, 0, 2)).reshape(Q_LORA, N_HEADS * QK_PAD)
    wkv = jnp.transpose(wkv, (1, 0, 2)).reshape(KV_LORA, N_HEADS * QK_PAD)
    q, k, v, kt, vt = _qkv_prep(ql, kvl, small["q_a_norm"], small["kv_a_norm"], wq, wkv, gq, gk, cos_t, sin_t)
    attn, lse, attn_t, w_up, w_down = _attn_fwd(q, k, vt, xch=[(bf["w_up"], False), (bf["w_down"], False)])
    h, y_mla, mixed_t = _merge(attn, gs, gm, y_ssm, x, w_o_mla, w_out)
    hn, dout, hn_t, loss = _mlp_fwd_loss(h, target, small["norm_mlp"], w_up, w_down)

    da, dh, dout_b, hid_t, d_norm_mlp = _mlp_bwd(dout, hn, h, small["norm_mlp"], w_up, w_down)
    p_w_down = _matmul_tn_shards(hid_t, dout_b, "dw_down", False, tm=1024, turned=True)
    p_w_up = _matmul_tn_shards(hn_t, da, "dw_up", True, turned=True)
    dgs, dgm, dy_ssm, dy_mla, dattn, delta = _merge_bwd(dh, gs, gm, y_ssm, y_mla, attn, w_out, w_o_mla)
    p_w_out = _matmul_tn_shards(mixed_t, dh, "dw_out", False, tm=1024, turned=True)
    p_w_o_mla = _matmul_tn_shards(attn_t, dy_mla, "dw_o_mla", False, turned=True)
    dq, dk, dv, l_w_up, l_w_down, l_w_out, l_w_o_mla = _attn_bwd(
        q, k, kt, v, lse, delta, dattn, xch=[(p_w_up, True), (p_w_down, True), (p_w_out, True), (p_w_o_mla, True)])
    dql, dkvl, d_q_a_norm, d_kv_a_norm, d_gq, d_gk, g_wq, g_wkv = _qkv_prep_bwd(
        ql, kvl, dq, dk, dv, small["q_a_norm"], small["kv_a_norm"], wq, wkv, gq, gk, cos_t, sin_t, xch=[])
    p_wq = jnp.transpose(g_wq.reshape(Q_LORA, N_HEADS, QK_PAD), (1, 0, 2)).astype(BF16)
    p_wkv = jnp.transpose(g_wkv.reshape(KV_LORA, N_HEADS, QK_PAD), (1, 0, 2)).astype(BF16)
    dy, d_b_glu, g_w_glu, g_w_o_ssm = _glu_bwd(dy_ssm, y, w_glu, small["b_glu"], w_o_ssm)
    p_w_o_ssm = jnp.transpose(g_w_o_ssm.reshape(SSM_WIDTH, N_DEV, OUT_SHARD), (1, 0, 2)).astype(BF16)
    p_w_glu = g_w_glu.reshape(N_DEV, SSM_WIDTH // N_DEV, SSM_WIDTH).astype(BF16)
    du, dlam, d_d, d_bblk, d_cblk_t, l_wq, l_wkv, l_w_glu, l_w_o_ssm = _ssm_bwd(
        u, dy, st, bblk, cblk, lam, d_row, xch=[(p_wq, True), (p_wkv, True), (p_w_glu, True), (p_w_o_ssm, True)])
    d_a_re, d_a_im, d_log_dt, d_bt_re, d_bt_im, d_c_re, d_c_im = _ssm_prep_bwd(
        a_re, a_im, log_dt, bt_re, bt_im, dlam, d_bblk, d_cblk_t)
    tr = lambda mat: jnp.transpose(mat.reshape(SSM_GROUPS, SSM_GROUP_CH, SSM_STATE), (0, 2, 1))
    g_small = {
        "q_a_norm": d_q_a_norm, "kv_a_norm": d_kv_a_norm, "q_norm": d_gq[:, :QK_HEAD], "k_norm": d_gk[:, :QK_HEAD],
        "ssm_a_re": d_a_re, "ssm_a_im": d_a_im, "ssm_log_dt": d_log_dt,
        "ssm_b_re": tr(d_bt_re), "ssm_b_im": tr(d_bt_im), "ssm_c_re": d_c_re, "ssm_c_im": d_c_im,
        "ssm_d": d_d, "b_glu": d_b_glu, "norm_mlp": d_norm_mlp,
    }
    rest = jnp.concatenate([_pack_small(g_small, SMALL[1:]), loss], axis=0)
    dx, d_norm_mix, g_w_in_pad, g_rest_all = _in_proj_bwd(
        (du, dql, dkvl, dgs, dgm), dh, x, xn_t, small["norm_mix"], w_in_pad, xch=[(rest, False)])
    parts = jnp.transpose(_unpad_in(g_w_in_pad).reshape(D_MODEL, N_DEV, IN_SHARD), (1, 0, 2))
    g_w_in_mine, g_first_all = _reduce_scatter(parts, _pack_small({SMALL[0]: d_norm_mix}, SMALL[:1]), "reduce_w_in")
    landed = {"w_q_b": l_wq[:, :, :QK_HEAD], "w_kv_b": l_wkv, "w_o_mla": l_w_o_mla, "w_glu": l_w_glu,
              "w_o_ssm": l_w_o_ssm, "w_out": l_w_out, "w_up": l_w_up, "w_down": l_w_down}
    return dx, landed, g_w_in_mine, g_first_all, g_rest_all


def kernel(x, positions, norm_mix, w_in, q_a_norm, kv_a_norm, w_q_b, w_kv_b, q_norm, k_norm, w_o_mla, ssm_a_re, ssm_a_im, ssm_log_dt, ssm_b_re, ssm_b_im, ssm_c_re, ssm_c_im, ssm_d, w_glu, b_glu, w_o_ssm, w_out, norm_mlp, w_up, w_down, loss_target, m_norm_mix, m_w_in, m_q_a_norm, m_kv_a_norm, m_w_q_b, m_w_kv_b, m_q_norm, m_k_norm, m_w_o_mla, m_ssm_a_re, m_ssm_a_im, m_ssm_log_dt, m_ssm_b_re, m_ssm_b_im, m_ssm_c_re, m_ssm_c_im, m_ssm_d, m_w_glu, m_b_glu, m_w_o_ssm, m_w_out, m_norm_mlp, m_w_up, m_w_down, v_norm_mix, v_w_in, v_q_a_norm, v_kv_a_norm, v_w_q_b, v_w_kv_b, v_q_norm, v_k_norm, v_w_o_mla, v_ssm_a_re, v_ssm_a_im, v_ssm_log_dt, v_ssm_b_re, v_ssm_b_im, v_ssm_c_re, v_ssm_c_im, v_ssm_d, v_w_glu, v_b_glu, v_w_o_ssm, v_w_out, v_norm_mlp, v_w_up, v_w_down):
    given = dict(locals())
    w = {n: given[n] for n in WEIGHT_ORDER}
    m = {n: given["m_" + n] for n in WEIGHT_ORDER}
    v = {n: given["v_" + n] for n in WEIGHT_ORDER}
    big = [n for n in WEIGHT_ORDER if n not in SMALL]
    small = {n: w[n] for n in SMALL}

    dx, landed, g_w_in, g_first_all, g_rest_all = _step(
        x[0], positions.reshape(-1, 1), loss_target[0], {n: w[n][0] for n in big}, small)

    grads, deltas, new_m, new_v = {}, {}, {}, {}
    for n in big:
        if n in ("w_in", "w_q_b"):
            wt, mt, vt = jnp.transpose(w[n][0]), jnp.transpose(m[n][0]), jnp.transpose(v[n][0])
            if n == "w_in":
                g = jnp.transpose(g_w_in)
                d, nm, nv = _adamw(wt, g, mt, vt, "adamw_" + n)
            else:
                g, d, nm, nv = _adamw_sum(jnp.transpose(landed[n], (0, 2, 1)), wt, mt, vt, "adamw_" + n)
            g, d, nm, nv = (jnp.transpose(a) for a in (g, d, nm, nv))
        else:
            g, d, nm, nv = _adamw_sum(landed[n], w[n][0], m[n][0], v[n][0], "adamw_" + n)
        grads[n], deltas[n], new_m[n], new_v[n] = g[None], d[None], nm[None], nv[None]

    outs = _adamw_small(g_first_all, g_rest_all, _pack_small(small), _pack_small({n: m[n] for n in SMALL}),
                        _pack_small({n: v[n] for n in SMALL}), _small_rows(small))
    for p, n in enumerate(SMALL):
        for k, dst in enumerate((grads, deltas, new_m, new_v)):
            dst[n] = outs[1 + 4 * p + k].reshape(-1)[:small[n].size].reshape(small[n].shape)

    return (outs[0][0, 0], dx[None], *[grads[n] for n in WEIGHT_ORDER], *[deltas[n] for n in WEIGHT_ORDER],
            *[new_m[n] for n in WEIGHT_ORDER], *[new_v[n] for n in WEIGHT_ORDER])
```

```python
import functools
import math

import numpy as np
import jax
import jax.numpy as jnp
from jax import lax
from jax.experimental import pallas as pl
from jax.experimental.pallas import tpu as pltpu

F32 = jnp.float32
BF16 = jnp.bfloat16

D_MODEL = 1024
SSM_GROUPS = 32
SSM_GROUP_CH = 16
SSM_WIDTH = 512
SSM_STATE = 64
N_STATE = SSM_GROUPS * SSM_STATE
N_HEADS = 8
QK_NOPE = 128
QK_ROPE = 64
QK_HEAD = 192
QK_PAD = 256
V_HEAD = 128
Q_LORA = 384
KV_LORA = 256
KV_LAT_PAD = 384
ROPE_THETA = 10000.0
D_FF = 4096
EPS = 1e-6
ATT_SCALE = QK_HEAD ** -0.5
N_DEV = 8
FF_SHARD = D_FF // N_DEV
OUT_SHARD = D_MODEL // N_DEV

IN_SEGS = ((0, 512), (512, 896), (896, 1280), (1280, 2304), (2304, 3328))
D_IN = 3264
D_IN_PAD = 3328
KV_END = 1216

ADAM_LR = 0.001
ADAM_B1 = 0.9
ADAM_B2 = 0.999
ADAM_EPS = 1e-08
ADAM_WD = 0.01
ADAM_STEP = 10

VMEM_LIMIT = 56 * 1024 * 1024
MESH = pl.DeviceIdType.MESH
ANY = pl.BlockSpec(memory_space=pl.ANY)
LANES = 128

SCAN_T = 256
SUBCHUNKS = 8
SCAN_CG = 1024
ATT_T = 512
ATT_SUB = 1
ATT_HEADS = 4
ATT_BWD_HEADS = 2
ROW_T = 256
MM_T = 512


def _params(sem=None):
    return pltpu.CompilerParams(dimension_semantics=sem, vmem_limit_bytes=VMEM_LIMIT)


def _rows(arr, tm):
    if arr.ndim == 2:
        return pl.BlockSpec((tm, arr.shape[1]), lambda i: (i, 0))
    return pl.BlockSpec((arr.shape[0], tm, arr.shape[2]), lambda i: (0, i, 0))


def _const(arr):
    nd = arr.ndim
    return pl.BlockSpec(arr.shape, lambda *_: (0,) * nd, pipeline_mode=pl.Buffered(1))


def _sds(shape, dtype):
    return jax.ShapeDtypeStruct(shape, dtype)


PEERS = tuple((dx, dy, dc) for dx in (0, 1) for dy in (0, 1) for dc in (0, 1) if (dx, dy, dc) != (0, 0, 0))


def _here():
    x, y, c = lax.axis_index("x"), lax.axis_index("y"), lax.axis_index("c")
    return x, y, c, 4 * x + 2 * y + c


def _xchg_start(scatter, srcs, dsts, send, recv, local):
    x, y, c, me = _here()
    for e, sc in enumerate(scatter):
        src, dst = srcs[e], dsts[e]
        pltpu.make_async_copy(src.at[me] if sc else src, dst.at[me], local.at[e]).start()
        for dx, dy, dc in PEERS:
            px, py, pc = (1 - x if dx else x), (1 - y if dy else y), (1 - c if dc else c)
            pltpu.make_async_remote_copy(
                src_ref=src.at[4 * px + 2 * py + pc] if sc else src, dst_ref=dst.at[me],
                send_sem=send.at[e], recv_sem=recv.at[e], device_id=(px, py, pc), device_id_type=MESH).start()


def _xchg_wait(scatter, srcs, dsts, send, recv, local):
    x, y, c, me = _here()
    for e, sc in enumerate(scatter):
        src, dst = srcs[e], dsts[e]
        pltpu.make_async_copy(src.at[me] if sc else src, dst.at[me], local.at[e]).wait()
        span = dst.at[pl.ds(0, N_DEV - 1)]
        both = pltpu.make_async_remote_copy(src_ref=span, dst_ref=span, send_sem=send.at[e], recv_sem=recv.at[e],
                                            device_id=(x, y, c), device_id_type=MESH)
        both.wait_send()
        both.wait_recv()


def _call(body, name, grid, ins, in_specs, outs, out_specs, scratch=(), xch=()):
    n_in, n_out, ne = len(ins), len(outs), len(xch)
    scatter = [sc for _, sc in xch]
    x_outs = [_sds((N_DEV,) + (a.shape[1:] if sc else a.shape), a.dtype) for a, sc in xch]
    sems = [pltpu.SemaphoreType.DMA((ne,))] * 3 if ne else []

    def wrapped(*refs):
        in_refs, x_src = refs[:n_in], refs[n_in:n_in + ne]
        out_refs = refs[n_in + ne:n_in + ne + n_out]
        x_dst = refs[n_in + ne + n_out:n_in + 2 * ne + n_out]
        rest = refs[n_in + 2 * ne + n_out:]
        if ne:
            x_sems, rest = rest[len(rest) - 3:], rest[:len(rest) - 3]
            first = functools.reduce(jnp.logical_and, [pl.program_id(d) == 0 for d in range(len(grid))])
            last = functools.reduce(jnp.logical_and, [pl.program_id(d) == grid[d] - 1 for d in range(len(grid))])

            @pl.when(first)
            def _():
                _xchg_start(scatter, x_src, x_dst, *x_sems)

        body(*in_refs, *out_refs, *rest)
        if ne:
            @pl.when(last)
            def _():
                _xchg_wait(scatter, x_src, x_dst, *x_sems)

    return pl.pallas_call(
        wrapped,
        name=name,
        grid=grid,
        in_specs=list(in_specs) + [ANY] * ne,
        out_specs=list(out_specs) + [ANY] * ne,
        out_shape=list(outs) + x_outs,
        scratch_shapes=list(scratch) + sems,
        compiler_params=_params(("arbitrary",) * len(grid)),
    )(*ins, *[a for a, _ in xch])


def _row_call(body, name, n_rows, tm, row_ins, const_ins, row_outs, acc_outs=(), xch=(), col_outs=(), scratch=()):
    outs = [_sds(s, d) for s, d in list(row_outs) + list(col_outs) + list(acc_outs)]
    n_row, n_col = len(row_outs), len(col_outs)
    out_specs = [_rows(o, tm) for o in outs[:n_row]] + [
        pl.BlockSpec(o.shape[:-1] + (tm,), lambda i, nd=len(o.shape): (0,) * (nd - 1) + (i,))
        for o in outs[n_row:n_row + n_col]] + [
        pl.BlockSpec(o.shape, lambda i, nd=len(o.shape): (0,) * nd) for o in outs[n_row + n_col:]]
    in_specs = [_rows(a, tm) for a in row_ins] + [_const(a) for a in const_ins]
    return _call(body, name, (n_rows // tm,), list(row_ins) + list(const_ins), in_specs, outs, out_specs,
                 scratch=scratch, xch=xch)


def _dot(a, b):
    return jnp.dot(a, b, preferred_element_type=F32)


def _dot_nt(a, b):
    return lax.dot_general(a, b, (((1,), (1,)), ((), ())), preferred_element_type=F32)


def _dot_tn(a, b):
    return lax.dot_general(a, b, (((0,), (0,)), ((), ())), preferred_element_type=F32)


def _rms(x, g, n):
    inv = lax.rsqrt(jnp.sum(x * x, -1, keepdims=True) * (1.0 / n) + EPS)
    return x * inv * g, inv


def _rms_bwd(dy, x, g, inv, n):
    xh = x * inv
    dxh = dy * g
    dx = inv * (dxh - xh * (jnp.sum(dxh * xh, -1, keepdims=True) * (1.0 / n)))
    return dx, dy * xh


def _sigmoid(x):
    return 1.0 / (1.0 + jnp.exp(-x))


_GELU_C = math.sqrt(2.0 / math.pi)


def _gelu(y):
    th = jnp.tanh(_GELU_C * (y + 0.044715 * (y * y * y)))
    return 0.5 * y * (1.0 + th), th


def _gelu_grad(y, th):
    return 0.5 * (1.0 + th) + 0.5 * y * (1.0 - th * th) * (_GELU_C * (1.0 + 3.0 * 0.044715 * (y * y)))


def _acc(ref, val):
    @pl.when(pl.program_id(0) == 0)
    def _():
        ref[...] = jnp.zeros_like(ref)

    ref[...] += val


def _tile(n, limit):
    if n <= limit:
        return n
    return max(t for t in range(128, limit + 1, 128) if n % t == 0)


def _lhs(a, turned, tm, tk):
    m, k_dim = a.shape if turned else a.shape[::-1]
    tm, tk = _tile(m, tm), _tile(k_dim, tk)
    if turned:
        return m, k_dim, tm, tk, pl.BlockSpec((tm, tk), lambda i, k: (i, k)), _dot
    return m, k_dim, tm, tk, pl.BlockSpec((tk, tm), lambda i, k: (k, i)), _dot_tn


def _matmul_tn_shards(a, b, name, by_col, tm=512, tk=512, turned=False):
    m, k_dim, tm, tk, a_spec, dot = _lhs(a, turned, tm, tk)
    n = b.shape[1]
    nk = k_dim // tk
    if by_col:
        r, c = m, n // N_DEV
        out_spec = pl.BlockSpec((N_DEV, tm, c), lambda i, k: (0, i, 0))
    else:
        r, c = m // N_DEV, n
        per = tm // r
        out_spec = pl.BlockSpec((per, r, c), lambda i, k: (i, 0, 0))

    def body(a_ref, b_ref, o_ref, acc_ref):
        k = pl.program_id(1)

        @pl.when(k == 0)
        def _():
            acc_ref[...] = jnp.zeros_like(acc_ref)

        acc_ref[...] += dot(a_ref[...].astype(BF16), b_ref[...].astype(BF16))

        @pl.when(k == nk - 1)
        def _():
            if by_col:
                for j in range(N_DEV):
                    o_ref[j] = acc_ref[:, j * c:(j + 1) * c].astype(BF16)
            else:
                for s in range(per):
                    o_ref[s] = acc_ref[s * r:(s + 1) * r, :].astype(BF16)

    return pl.pallas_call(
        body,
        name=name,
        grid=(m // tm, nk),
        in_specs=[a_spec, pl.BlockSpec((tk, n), lambda i, k: (k, 0))],
        out_specs=out_spec,
        out_shape=_sds((N_DEV, r, c), BF16),
        scratch_shapes=[pltpu.VMEM((tm, n), F32)],
        compiler_params=_params(("parallel", "arbitrary")),
    )(a, b)


def _rope_tables(pos_col):
    n = pos_col.shape[0]
    half = QK_ROPE // 2
    inv_freq = (ROPE_THETA ** (-np.arange(half, dtype=np.float32) / half)).astype(np.float32)
    freq_row = jnp.asarray(np.concatenate([inv_freq, inv_freq, np.zeros(64, np.float32)])[None, :])

    def body(p_ref, f_ref, c_ref, s_ref):
        ang = p_ref[...].astype(F32) * f_ref[...]
        c_ref[...] = jnp.cos(ang)
        s_ref[...] = jnp.sin(ang)

    return _row_call(body, "rope_tables", n, min(n, 1024), [pos_col], [freq_row], [((n, 128), F32)] * 2)


def _rope_rot(v):
    lane = lax.broadcasted_iota(jnp.int32, v.shape, 1)
    return jnp.where(lane < 32, -pltpu.roll(v, 96, 1), jnp.where(lane < 64, pltpu.roll(v, 32, 1), 0.0))


def _rope_rot_t(v):
    lane = lax.broadcasted_iota(jnp.int32, v.shape, 1)
    return jnp.where(lane < 32, pltpu.roll(v, 96, 1), jnp.where(lane < 64, -pltpu.roll(v, 32, 1), 0.0))


def _in_proj(x, norm_mix, w_in_pad, xch):
    n = x.shape[0]

    def body(x_ref, g_ref, w_ref, u_ref, ql_ref, kvl_ref, gs_ref, gm_ref, xnt_ref):
        xn, _ = _rms(x_ref[...], g_ref[...], D_MODEL)
        xb = xn.astype(BF16)
        xnt_ref[...] = xn.T.astype(BF16)
        for ref, (a, b) in zip((u_ref, ql_ref, kvl_ref, gs_ref, gm_ref), IN_SEGS):
            ref[...] = _dot(xb, w_ref[:, a:b])

    outs = [((n, b - a), F32) for a, b in IN_SEGS]
    return _row_call(body, "in_proj", n, MM_T, [x], [norm_mix, w_in_pad], outs, xch=xch,
                     col_outs=[((D_MODEL, n), BF16)])


def _ssm_prep_fn(a_re, a_im, log_dt, b_re_x, b_im_x):
    dt = jnp.exp(log_dt)
    mag = jnp.exp(a_re * dt)
    lr = mag * jnp.cos(a_im * dt)
    li = mag * jnp.sin(a_im * dt)
    den = a_re * a_re + a_im * a_im
    fr = ((lr - 1.0) * a_re + li * a_im) / den
    fi = (li * a_re - (lr - 1.0) * a_im) / den
    return lr, li, fr * b_re_x - fi * b_im_x, fr * b_im_x + fi * b_re_x


def _dot_exact(a, b, dims):
    return lax.dot_general(a, b, (dims, ((), ())), precision=lax.Precision.HIGHEST, preferred_element_type=F32)


def _lane_repeat(width, n):
    src = lax.broadcasted_iota(jnp.int32, (width, n), 0)
    dst = lax.broadcasted_iota(jnp.int32, (width, n), 1)
    return (dst % width == src).astype(F32)


def _same_group(rows, rows_per_group, cols, cols_per_group):
    row = lax.broadcasted_iota(jnp.int32, (rows, cols), 0)
    col = lax.broadcasted_iota(jnp.int32, (rows, cols), 1)
    return (row // rows_per_group) == (col // cols_per_group)


def _expand_b(bt):
    tiled = _dot_exact(bt, _lane_repeat(SSM_STATE, N_STATE), ((1,), (0,)))
    return jnp.where(_same_group(SSM_WIDTH, SSM_GROUP_CH, N_STATE, SSM_STATE), tiled, 0.0)


def _collect_b(m):
    masked = jnp.where(_same_group(SSM_WIDTH, SSM_GROUP_CH, N_STATE, SSM_STATE), m, 0.0)
    return _dot_exact(masked, _lane_repeat(SSM_STATE, N_STATE), ((1,), (1,)))


def _ssm_prep(a_re, a_im, log_dt, bt_re, bt_im, c2_re, c2_im):
    def body(ar, ai, ld, br, bi, cr, ci, lam_ref, bblk_ref, cblk_ref):
        lr, li, bbr, bbi = _ssm_prep_fn(ar[...], ai[...], ld[...], _expand_b(br[...]), _expand_b(bi[...]))
        lam_ref[0:1, :] = lr
        lam_ref[1:2, :] = li
        bblk_ref[:, 0:N_STATE] = bbr.astype(BF16)
        bblk_ref[:, N_STATE:] = bbi.astype(BF16)
        rep = _lane_repeat(SSM_GROUP_CH, SSM_WIDTH)
        own = _same_group(N_STATE, SSM_STATE, SSM_WIDTH, SSM_GROUP_CH)
        cblk_ref[0:N_STATE, :] = jnp.where(own, _dot_exact(cr[...], rep, ((1,), (0,))), 0.0).astype(BF16)
        cblk_ref[N_STATE:, :] = jnp.where(own, -_dot_exact(ci[...], rep, ((1,), (0,))), 0.0).astype(BF16)

    return pl.pallas_call(
        body,
        name="ssm_prep",
        out_shape=[_sds((2, N_STATE), F32), _sds((SSM_WIDTH, 2 * N_STATE), BF16),
                   _sds((2 * N_STATE, SSM_WIDTH), BF16)],
        compiler_params=_params(),
    )(a_re, a_im, log_dt, bt_re, bt_im, c2_re, c2_im)


def _ssm_prep_bwd(a_re, a_im, log_dt, bt_re, bt_im, dlam, dbblk, dcblk_t):
    def body(ar, ai, ld, br, bi, dl, db, dc, dar, dai, dld, dbr, dbi, dcr, dci):
        _, vjp = jax.vjp(_ssm_prep_fn, ar[...], ai[...], ld[...], _expand_b(br[...]), _expand_b(bi[...]))
        g = vjp((dl[0:1, :], dl[1:2, :], db[:, 0:N_STATE], db[:, N_STATE:]))
        dar[...] = g[0]
        dai[...] = g[1]
        grp = lax.broadcasted_iota(jnp.int32, (SSM_GROUPS, N_STATE), 0)
        lane = lax.broadcasted_iota(jnp.int32, (SSM_GROUPS, N_STATE), 1)
        sel = (lane // SSM_STATE) == grp
        dld[...] = jnp.sum(jnp.where(sel, jnp.broadcast_to(g[2], (SSM_GROUPS, N_STATE)), 0.0), axis=1, keepdims=True)
        dbr[...] = _collect_b(g[3])
        dbi[...] = _collect_b(g[4])
        dcr[...] = _collect_b(dc[:, 0:N_STATE])
        dci[...] = -_collect_b(dc[:, N_STATE:])

    small = _sds((SSM_WIDTH, SSM_STATE), F32)
    return pl.pallas_call(
        body,
        name="ssm_prep_bwd",
        out_shape=[_sds((1, N_STATE), F32), _sds((1, N_STATE), F32), _sds((SSM_GROUPS, 1), F32), small, small, small, small],
        compiler_params=_params(),
    )(a_re, a_im, log_dt, bt_re, bt_im, dlam, dbblk, dcblk_t)


def _perm_matrix(t):
    run = t // SUBCHUNKS
    p = np.zeros((t, t), np.float32)
    r = np.arange(t)
    p[r, (r % SUBCHUNKS) * run + r // SUBCHUNKS] = 1.0
    return jnp.asarray(p, dtype=BF16)


def _unpermute(p, a):
    hi = a.astype(BF16)
    r1 = a - hi.astype(F32)
    mid = r1.astype(BF16)
    lo = (r1 - mid.astype(F32)).astype(BF16)
    return _dot_tn(p, hi) + _dot_tn(p, mid) + _dot_tn(p, lo)


def _power_table(lam_ref, pw_ref, n):
    lr, li = lam_ref[0:1, :], lam_ref[1:2, :]
    pw_ref[0:1, 0:N_STATE] = lr
    pw_ref[0:1, N_STATE:] = li

    def step(i, carry):
        pr, pi = carry
        pr, pi = pr * lr - pi * li, pr * li + pi * lr
        pw_ref[pl.ds(i, 1), 0:N_STATE] = pr
        pw_ref[pl.ds(i, 1), N_STATE:] = pi
        return pr, pi

    lax.fori_loop(1, n, step, (lr, li))


def _col_groups():
    return [(pl.ds(c, SCAN_CG), pl.ds(N_STATE + c, SCAN_CG)) for c in range(0, N_STATE, SCAN_CG)]


def _run_scan(buf, lam_ref, t, reverse):
    nblk = t // 8
    for re, im in _col_groups():
        lr = jnp.broadcast_to(lam_ref[0:1, re], (8, SCAN_CG))
        li = jnp.broadcast_to(lam_ref[1:2, re], (8, SCAN_CG))
        if reverse:
            li = -li
        first = pl.ds((nblk - 1) * 8 if reverse else 0, 8)

        def step(k, carry, re=re, im=im, lr=lr, li=li):
            pr, pi = carry
            i = (nblk - 2 - k) if reverse else (k + 1)
            r = pl.ds(pl.multiple_of(i * 8, 8), 8)
            xr = buf[r, re] + lr * pr - li * pi
            xi = buf[r, im] + lr * pi + li * pr
            buf[r, re] = xr
            buf[r, im] = xi
            return xr, xi

        lax.fori_loop(0, nblk - 1, step, (buf[first, re], buf[first, im]))


def _run_carries(buf, pw_ref, carry_ref, s_ref, t, reverse):
    nblk = t // 8
    run = t // SUBCHUNKS
    edge = buf[pl.ds(0 if reverse else (nblk - 1) * 8, 8), :]
    pr, pi = pw_ref[run - 1:run, 0:N_STATE], pw_ref[run - 1:run, N_STATE:]
    if reverse:
        pi = -pi
    sr, si = carry_ref[0:1, 0:N_STATE], carry_ref[0:1, N_STATE:]
    for s in (range(SUBCHUNKS - 1, -1, -1) if reverse else range(SUBCHUNKS)):
        s_ref[s:s + 1, 0:N_STATE] = sr
        s_ref[s:s + 1, N_STATE:] = si
        er, ei = edge[s:s + 1, 0:N_STATE], edge[s:s + 1, N_STATE:]
        sr, si = er + pr * sr - pi * si, ei + pr * si + pi * sr
    carry_ref[:, 0:N_STATE] = jnp.broadcast_to(sr, (8, N_STATE))
    carry_ref[:, N_STATE:] = jnp.broadcast_to(si, (8, N_STATE))


def _run_fix(buf, pw_ref, s_ref, t, reverse):
    nblk = t // 8
    for re, im in _col_groups():
        sr, si = s_ref[:, re], s_ref[:, im]

        def step(i, carry, re=re, im=im, sr=sr, si=si):
            r = pl.ds(pl.multiple_of(i * 8, 8), 8)
            row = pl.ds((nblk - 1 - i) if reverse else i, 1)
            pr, pi = pw_ref[row, re], pw_ref[row, im]
            if reverse:
                pi = -pi
            buf[r, re] += pr * sr - pi * si
            buf[r, im] += pr * si + pi * sr
            return carry

        lax.fori_loop(0, nblk, step, 0)


STATE_BLOCKS = 2 * N_STATE // LANES
CH_BLOCKS = SSM_WIDTH // LANES


def _state_block(b):
    pair = b % (N_STATE // LANES)
    k = (pair * 2 * SSM_GROUP_CH) // LANES
    return slice(b * LANES, (b + 1) * LANES), slice(k * LANES, (k + 1) * LANES)


def _channel_block(c):
    w = N_STATE // CH_BLOCKS
    return slice(c * LANES, (c + 1) * LANES), slice(c * w, (c + 1) * w), slice(N_STATE + c * w, N_STATE + (c + 1) * w)


def _to_states(vb, w_ref, buf, nt):
    for b in range(STATE_BLOCKS):
        lanes, ch = _state_block(b)
        buf[:, lanes] = _dot_nt(vb[:, ch], w_ref[lanes, ch]) if nt else _dot(vb[:, ch], w_ref[ch, lanes])


def _to_channels(buf, w_ref, nt):
    outs = []
    for c in range(CH_BLOCKS):
        ch, re, im = _channel_block(c)
        xr, xi = buf[:, re].astype(BF16), buf[:, im].astype(BF16)
        if nt:
            outs.append(_dot_nt(xr, w_ref[ch, re]) + _dot_nt(xi, w_ref[ch, im]))
        else:
            outs.append(_dot(xr, w_ref[re, ch]) + _dot(xi, w_ref[im, ch]))
    return jnp.concatenate(outs, axis=-1)


def _ssm_fwd(u, bblk, cblk, lam, d_row, w_glu, b_glu, w_o_ssm, xch):
    n = u.shape[0]
    t = min(SCAN_T, n)
    perm = _perm_matrix(t)

    def body(u_ref, p_ref, bblk_ref, cblk_ref, lam_ref, d_ref, wg_ref, bg_ref, wo_ref, y_ref, ys_ref, st_ref,
             buf, pw_ref, carry_ref, s_ref):
        @pl.when(pl.program_id(0) == 0)
        def _():
            carry_ref[...] = jnp.zeros_like(carry_ref)
            _power_table(lam_ref, pw_ref, t // SUBCHUNKS)

        st_ref[0] = carry_ref[...]
        u_t = u_ref[...]
        p = p_ref[...]
        ub = _dot(p, u_t.astype(BF16)).astype(BF16)
        _to_states(ub, bblk_ref, buf, False)
        _run_scan(buf, lam_ref, t, False)
        _run_carries(buf, pw_ref, carry_ref, s_ref, t, False)
        _run_fix(buf, pw_ref, s_ref, t, False)
        y = d_ref[...] * u_t + _unpermute(p, _to_channels(buf, cblk_ref, False))
        y_ref[...] = y
        z, _ = _gelu(y)
        s = _sigmoid(_dot(z.astype(BF16), wg_ref[...]) + bg_ref[...])
        zgb = (z * s).astype(BF16)
        for j in range(N_DEV):
            ys_ref[:, j * OUT_SHARD:(j + 1) * OUT_SHARD] = _dot(zgb, wo_ref[j])

    consts = [perm, bblk, cblk, lam, d_row, w_glu, b_glu, w_o_ssm]
    return _call(
        body, "ssm_fwd", (n // t,), [u] + consts, [_rows(u, t)] + [_const(a) for a in consts],
        [_sds((n, SSM_WIDTH), F32), _sds((n, D_MODEL), F32), _sds((n // t, 8, 2 * N_STATE), F32)],
        [pl.BlockSpec((t, SSM_WIDTH), lambda i: (i, 0)), pl.BlockSpec((t, D_MODEL), lambda i: (i, 0)),
         pl.BlockSpec((1, 8, 2 * N_STATE), lambda i: (i, 0, 0))],
        scratch=[pltpu.VMEM((t, 2 * N_STATE), F32), pltpu.VMEM((t // SUBCHUNKS, 2 * N_STATE), F32),
                 pltpu.VMEM((8, 2 * N_STATE), F32), pltpu.VMEM((8, 2 * N_STATE), F32)],
        xch=xch)


def _head_norm_rope(slab, gain, cos_t, sin_t):
    xn, inv = _rms(slab, gain, QK_HEAD)
    lo, hi = xn[:, 0:128], xn[:, 128:256]
    return jnp.concatenate([lo, hi * cos_t + _rope_rot(hi) * sin_t], axis=-1), inv


def _head_norm_rope_bwd(g, slab, gain, inv, cos_t, sin_t):
    g_lo, g_hi = g[:, 0:128], g[:, 128:256]
    g_n = jnp.concatenate([g_lo, g_hi * cos_t + _rope_rot_t(g_hi * sin_t)], axis=-1)
    return _rms_bwd(g_n, slab, gain, inv, QK_HEAD)


def _qkv_prep(ql, kvl, q_a_norm, kv_a_norm, wq, wkv, gq, gk, cos_t, sin_t):
    n = ql.shape[0]
    tm = ROW_T

    def body(ql_ref, kvl_ref, cos_ref, sin_ref, qa_ref, ka_ref, wq_ref, wkv_ref, gq_ref, gk_ref,
             q_ref, k_ref, v_ref, kt_ref, vt_ref):
        cos_t, sin_t = cos_ref[...], sin_ref[...]
        qa, _ = _rms(ql_ref[...], qa_ref[...], Q_LORA)
        qab = qa.astype(BF16)
        kvl_t = kvl_ref[...]
        ca, _ = _rms(kvl_t[:, 0:KV_LORA], ka_ref[...], KV_LORA)
        cab = ca.astype(BF16)
        kpe = kvl_t[:, KV_LORA:KV_LAT_PAD]
        q_pre = _dot(qab, wq_ref[...])
        kv_pre = _dot(cab, wkv_ref[...])
        for h in range(N_HEADS):
            qh, _ = _head_norm_rope(q_pre[:, h * QK_PAD:(h + 1) * QK_PAD], gq_ref[...], cos_t, sin_t)
            q_ref[h] = (qh * ATT_SCALE).astype(BF16)
            kv_h = kv_pre[:, h * QK_PAD:(h + 1) * QK_PAD]
            kh, _ = _head_norm_rope(jnp.concatenate([kv_h[:, 0:QK_NOPE], kpe], axis=-1), gk_ref[...], cos_t, sin_t)
            k_ref[h] = kh.astype(BF16)
            kt_ref[h] = kh.T.astype(BF16)
            vh = kv_h[:, QK_NOPE:]
            v_ref[h] = vh.astype(BF16)
            vt_ref[h] = vh.T.astype(BF16)

    row_ins, consts = [ql, kvl, cos_t, sin_t], [q_a_norm, kv_a_norm, wq, wkv, gq, gk]
    outs = [_sds((N_HEADS, n, QK_PAD), BF16), _sds((N_HEADS, n, QK_PAD), BF16), _sds((N_HEADS, n, V_HEAD), BF16),
            _sds((N_HEADS, QK_PAD, n), BF16), _sds((N_HEADS, V_HEAD, n), BF16)]
    out_specs = [_rows(o, tm) for o in outs[:3]] + [
        pl.BlockSpec((N_HEADS, QK_PAD, tm), lambda i: (0, 0, i)), pl.BlockSpec((N_HEADS, V_HEAD, tm), lambda i: (0, 0, i))]
    return _call(body, "qkv_prep", (n // tm,), row_ins + consts,
                 [_rows(a, tm) for a in row_ins] + [_const(a) for a in consts], outs, out_specs)


def _causal_mask_t(st, t):
    key = lax.broadcasted_iota(jnp.int32, (t, t), 0)
    qry = lax.broadcasted_iota(jnp.int32, (t, t), 1)
    return jnp.where(key <= qry, st, -jnp.inf)


def _attn_fwd(q, k, vt, xch):
    n = q.shape[1]
    t = min(ATT_T, n)

    hp = ATT_HEADS

    def body(q_ref, k_ref, vt_ref, o_ref, lse_ref, ot_ref):
        i = pl.program_id(1)
        qts = [q_ref[g] for g in range(hp)]

        def kv_tile(j, carry, diag):
            ts = t // ATT_SUB
            sts = []
            for g in range(hp):
                for a in range(ATT_SUB):
                    r0 = pl.multiple_of(j * t + a * ts, ts)
                    st = _dot_nt(k_ref[g, pl.ds(r0, ts), :], qts[g])
                    if diag:
                        key = lax.broadcasted_iota(jnp.int32, (ts, t), 0) + a * ts
                        qry = lax.broadcasted_iota(jnp.int32, (ts, t), 1)
                        st = jnp.where(key <= qry, st, -jnp.inf)
                    sts.append(st)
            out = []
            for g in range(hp):
                m, l, acc = carry[g]
                for a in range(ATT_SUB):
                    st = sts[g * ATT_SUB + a]
                    r0 = pl.multiple_of(j * t + a * ts, ts)
                    m_new = jnp.maximum(m, jnp.max(st, 0, keepdims=True))
                    alpha = jnp.exp(m - m_new)
                    pt = jnp.exp(st - m_new)
                    l = alpha * l + jnp.sum(pt, 0, keepdims=True)
                    acc = alpha * acc + _dot(vt_ref[g, :, pl.ds(r0, ts)], pt.astype(BF16))
                    m = m_new
                out.append((m, l, acc))
            return tuple(out)

        one = (jnp.full((1, t), -jnp.inf, F32), jnp.zeros((1, t), F32), jnp.zeros((V_HEAD, t), F32))
        carry = lax.fori_loop(0, i, functools.partial(kv_tile, diag=False), (one,) * hp)
        for g, (m, l, acc) in enumerate(kv_tile(i, carry, True)):
            out_t = acc / l
            o_ref[:, g * V_HEAD:(g + 1) * V_HEAD] = out_t.T
            ot_ref[g * V_HEAD:(g + 1) * V_HEAD, :] = out_t.astype(BF16)
            lse_ref[g] = m + jnp.log(l)

    return _call(
        body, "attn_fwd", (N_HEADS // hp, n // t), [q, k, vt],
        [pl.BlockSpec((hp, t, QK_PAD), lambda h, i: (h, i, 0)), pl.BlockSpec((hp, n, QK_PAD), lambda h, i: (h, 0, 0)),
         pl.BlockSpec((hp, V_HEAD, n), lambda h, i: (h, 0, 0))],
        [_sds((n, N_HEADS * V_HEAD), F32), _sds((N_HEADS, 1, n), F32), _sds((N_HEADS * V_HEAD, n), BF16)],
        [pl.BlockSpec((t, hp * V_HEAD), lambda h, i: (i, h)), pl.BlockSpec((hp, 1, t), lambda h, i: (h, 0, i)),
         pl.BlockSpec((hp * V_HEAD, t), lambda h, i: (h, i))],
        xch=xch)


def _merge(attn, gs, gm, y_ssm, x, w_o_mla, w_out):
    n = x.shape[0]

    def body(at_ref, gs_ref, gm_ref, ys_ref, x_ref, wo_ref, wout_ref, h_ref, ym_ref, mxt_ref):
        y_mla = _dot(at_ref[...].astype(BF16), wo_ref[...])
        ym_ref[...] = y_mla
        mixed = _sigmoid(gs_ref[...]) * ys_ref[...] + _sigmoid(gm_ref[...]) * y_mla
        mxt_ref[...] = mixed.T.astype(BF16)
        h_ref[...] = x_ref[...] + _dot(mixed.astype(BF16), wout_ref[...])

    outs = [((n, D_MODEL), F32), ((n, D_MODEL), F32)]
    return _row_call(body, "merge", n, MM_T, [attn, gs, gm, y_ssm, x], [w_o_mla, w_out], outs,
                     col_outs=[((D_MODEL, n), BF16)])


def _mlp_fwd_loss(h, target, norm_mlp, w_up, w_down):
    n = h.shape[0]

    def body(h_ref, t_ref, g_ref, wu_ref, wd_ref, hn_ref, do_ref, hnt_ref, loss_ref):
        h_t = h_ref[...]
        hn, _ = _rms(h_t, g_ref[...], D_MODEL)
        hb = hn.astype(BF16)
        hn_ref[...] = hb
        hnt_ref[...] = hn.T.astype(BF16)
        out = h_t
        for j in range(N_DEV):
            a = jnp.maximum(_dot(hb, wu_ref[j]), 0.0)
            out += _dot((a * a).astype(BF16), wd_ref[j])
        err = out - t_ref[...]
        do_ref[...] = err * (1.0 / D_MODEL)
        _acc(loss_ref, jnp.broadcast_to(jnp.sum(err * err) * (0.5 / D_MODEL), loss_ref.shape))

    outs = [((n, D_MODEL), BF16), ((n, D_MODEL), F32)]
    return _row_call(body, "mlp_fwd_loss", n, MM_T, [h, target], [norm_mlp, w_up, w_down], outs, [((8, 128), F32)],
                     col_outs=[((D_MODEL, n), BF16)])


def _mlp_bwd(dout, hn, h, norm_mlp, w_up, w_down):
    n = h.shape[0]

    def body(do_ref, hn_ref, h_ref, g_ref, wu_ref, wd_ref, da_ref, dh_ref, dob_ref, hidt_ref, dg_ref):
        dout_t = do_ref[...]
        doutb = dout_t.astype(BF16)
        dob_ref[...] = doutb
        hb = hn_ref[...]
        dhn = jnp.zeros_like(dout_t)
        for j in range(N_DEV):
            cols = slice(j * FF_SHARD, (j + 1) * FF_SHARD)
            a = jnp.maximum(_dot(hb, wu_ref[j]), 0.0)
            hidt_ref[cols, :] = (a * a).T.astype(BF16)
            da = (_dot_nt(doutb, wd_ref[j]) * (2.0 * a)).astype(BF16)
            da_ref[:, cols] = da
            dhn += _dot_nt(da, wu_ref[j])
        h_t = h_ref[...]
        inv = lax.rsqrt(jnp.sum(h_t * h_t, -1, keepdims=True) * (1.0 / D_MODEL) + EPS)
        dx, dg = _rms_bwd(dhn, h_t, g_ref[...], inv, D_MODEL)
        dh_ref[...] = dout_t + dx
        _acc(dg_ref, jnp.sum(dg, 0, keepdims=True))

    outs = [((n, D_FF), BF16), ((n, D_MODEL), F32), ((n, D_MODEL), BF16)]
    return _row_call(body, "mlp_bwd", n, MM_T, [dout, hn, h], [norm_mlp, w_up, w_down], outs, [((1, D_MODEL), F32)],
                     col_outs=[((D_FF, n), BF16)])


def _merge_bwd(dh, gs, gm, y_ssm, y_mla, attn, w_out, w_o_mla):
    n = dh.shape[0]

    def body(dh_ref, gs_ref, gm_ref, ys_ref, ym_ref, at_ref, wout_ref, wo_ref,
             dgs_ref, dgm_ref, dys_ref, dym_ref, dat_ref, delta_ref):
        dmix = _dot_nt(dh_ref[...].astype(BF16), wout_ref[...])
        sgs, sgm = _sigmoid(gs_ref[...]), _sigmoid(gm_ref[...])
        dgs_ref[...] = (dmix * ys_ref[...] * sgs * (1.0 - sgs)).astype(BF16)
        dgm_ref[...] = (dmix * ym_ref[...] * sgm * (1.0 - sgm)).astype(BF16)
        dys_ref[...] = (dmix * sgs).astype(BF16)
        dym = (dmix * sgm).astype(BF16)
        dym_ref[...] = dym
        dattn = _dot_nt(dym, wo_ref[...])
        dat_ref[...] = dattn.astype(BF16)
        prod = dattn * at_ref[...]
        ones = jnp.ones((8, V_HEAD), F32)
        for h in range(N_HEADS):
            delta_ref[h] = _dot_exact(ones, prod[:, h * V_HEAD:(h + 1) * V_HEAD], ((1,), (1,)))[0:1, :]

    outs = [((n, D_MODEL), BF16)] * 5
    return _row_call(body, "merge_bwd", n, MM_T, [dh, gs, gm, y_ssm, y_mla, attn], [w_out, w_o_mla], outs,
                     col_outs=[((N_HEADS, 1, n), F32)])


def _attn_bwd(q, k, kt, v, lse, delta, dout, xch):
    n = q.shape[1]
    t = min(ATT_T, n)
    nt = n // t
    hp = ATT_BWD_HEADS

    def body(q_ref, k_ref, kt_ref, v_ref, lse_ref, delta_ref, do_ref, dq_ref, dk_ref, dv_ref, dqt_ref):
        j = pl.program_id(1)

        @pl.when(j == 0)
        def _():
            dqt_ref[...] = jnp.zeros_like(dqt_ref)

        def q_tile(i, carry, diag):
            r0 = pl.multiple_of(i * t, t)
            rows = pl.ds(r0, t)
            qts = [q_ref[g, rows, :] for g in range(hp)]
            sts = [_dot_nt(k_ref[g], qts[g]) for g in range(hp)]
            out = []
            for g in range(hp):
                dk, dv = carry[g]
                st = _causal_mask_t(sts[g], t) if diag else sts[g]
                pt = jnp.exp(st - lse_ref[g, :, rows])
                dob = do_ref[rows, g * V_HEAD:(g + 1) * V_HEAD]
                dv = dv + _dot(pt.astype(BF16), dob)
                dst = (pt * (_dot_nt(v_ref[g], dob) - delta_ref[g, :, rows])).astype(BF16)
                dk = dk + _dot(dst, qts[g])
                dqt_ref[g, :, rows] += _dot(kt_ref[g], dst)
                out.append((dk, dv))
            return tuple(out)

        zero = (jnp.zeros((t, QK_PAD), F32), jnp.zeros((t, V_HEAD), F32))
        carry = q_tile(j, (zero,) * hp, True)
        carry = lax.fori_loop(j + 1, nt, functools.partial(q_tile, diag=False), carry)
        for g, (dk, dv) in enumerate(carry):
            dk_ref[g] = dk
            dv_ref[g] = dv

        @pl.when(j == nt - 1)
        def _():
            for g in range(hp):
                for c in range(0, n, t):
                    dq_ref[g, c:c + t, :] = dqt_ref[g, :, c:c + t].T

    return _call(
        body, "attn_bwd", (N_HEADS // hp, nt), [q, k, kt, v, lse, delta, dout],
        [pl.BlockSpec((hp, n, QK_PAD), lambda h, j: (h, 0, 0)), pl.BlockSpec((hp, t, QK_PAD), lambda h, j: (h, j, 0)),
         pl.BlockSpec((hp, QK_PAD, t), lambda h, j: (h, 0, j)), pl.BlockSpec((hp, t, V_HEAD), lambda h, j: (h, j, 0)),
         pl.BlockSpec((hp, 1, n), lambda h, j: (h, 0, 0)), pl.BlockSpec((hp, 1, n), lambda h, j: (h, 0, 0)),
         pl.BlockSpec((n, hp * V_HEAD), lambda h, j: (0, h))],
        [_sds((N_HEADS, n, QK_PAD), F32), _sds((N_HEADS, n, QK_PAD), F32), _sds((N_HEADS, n, V_HEAD), F32)],
        [pl.BlockSpec((hp, n, QK_PAD), lambda h, j: (h, 0, 0)), pl.BlockSpec((hp, t, QK_PAD), lambda h, j: (h, j, 0)),
         pl.BlockSpec((hp, t, V_HEAD), lambda h, j: (h, j, 0))],
        scratch=[pltpu.VMEM((hp, QK_PAD, n), F32)],
        xch=xch)


def _qkv_prep_bwd(ql, kvl, dq, dk, dv, q_a_norm, kv_a_norm, wq, wkv, gq, gk, cos_t, sin_t, xch):
    n = ql.shape[0]

    def body(ql_ref, kvl_ref, cos_ref, sin_ref, dq_ref, dk_ref, dv_ref, qa_ref, ka_ref, wq_ref, wkv_ref, gq_ref, gk_ref,
             dql_ref, dkvl_ref, dqa_ref, dka_ref, dgq_ref, dgk_ref, dwq_ref, dwkv_ref, dqp_ref, dkvp_ref):
        cos_t, sin_t = cos_ref[...], sin_ref[...]
        ql_t = ql_ref[...]
        qa, inv_qa = _rms(ql_t, qa_ref[...], Q_LORA)
        qab = qa.astype(BF16)
        kvl_t = kvl_ref[...]
        ckv = kvl_t[:, 0:KV_LORA]
        ca, inv_ca = _rms(ckv, ka_ref[...], KV_LORA)
        cab = ca.astype(BF16)
        kpe = kvl_t[:, KV_LORA:KV_LAT_PAD]
        dgq = jnp.zeros((1, QK_PAD), F32)
        dgk = jnp.zeros((1, QK_PAD), F32)
        dkpe = jnp.zeros_like(kpe)
        q_pre = _dot(qab, wq_ref[...])
        kv_pre = _dot(cab, wkv_ref[...])
        for h in range(N_HEADS):
            head = slice(h * QK_PAD, (h + 1) * QK_PAD)
            q_slab = q_pre[:, head]
            inv = lax.rsqrt(jnp.sum(q_slab * q_slab, -1, keepdims=True) * (1.0 / QK_HEAD) + EPS)
            d_slab, dg = _head_norm_rope_bwd(dq_ref[h] * ATT_SCALE, q_slab, gq_ref[...], inv, cos_t, sin_t)
            dqp_ref[:, head] = d_slab.astype(BF16)
            dgq += jnp.sum(dg, 0, keepdims=True)
            k_slab = jnp.concatenate([kv_pre[:, h * QK_PAD:h * QK_PAD + QK_NOPE], kpe], axis=-1)
            inv = lax.rsqrt(jnp.sum(k_slab * k_slab, -1, keepdims=True) * (1.0 / QK_HEAD) + EPS)
            d_slab, dg = _head_norm_rope_bwd(dk_ref[h], k_slab, gk_ref[...], inv, cos_t, sin_t)
            dkvp_ref[:, head] = jnp.concatenate([d_slab[:, 0:QK_NOPE], dv_ref[h]], axis=-1).astype(BF16)
            dkpe += d_slab[:, QK_NOPE:QK_PAD]
            dgk += jnp.sum(dg, 0, keepdims=True)
        dqa = _dot_nt(dqp_ref[...], wq_ref[...])
        dx, dg = _rms_bwd(dqa, ql_t, qa_ref[...], inv_qa, Q_LORA)
        dql_ref[...] = dx.astype(BF16)
        _acc(dqa_ref, jnp.sum(dg, 0, keepdims=True))
        dca = _dot_nt(dkvp_ref[...], wkv_ref[...])
        dx, dg = _rms_bwd(dca, ckv, ka_ref[...], inv_ca, KV_LORA)
        dkvl_ref[:, 0:KV_LORA] = dx.astype(BF16)
        dkvl_ref[:, KV_LORA:KV_LAT_PAD] = dkpe.astype(BF16)
        _acc(dka_ref, jnp.sum(dg, 0, keepdims=True))
        _acc(dgq_ref, dgq)
        _acc(dgk_ref, dgk)
        _acc(dwq_ref, _dot_tn(qab, dqp_ref[...]))
        _acc(dwkv_ref, _dot_tn(cab, dkvp_ref[...]))

    wide = N_HEADS * QK_PAD
    row_outs = [((n, Q_LORA), BF16), ((n, KV_LAT_PAD), BF16)]
    acc_outs = [((1, Q_LORA), F32), ((1, KV_LORA), F32), ((1, QK_PAD), F32), ((1, QK_PAD), F32),
                ((Q_LORA, wide), F32), ((KV_LORA, wide), F32)]
    return _row_call(body, "qkv_prep_bwd", n, ROW_T, [ql, kvl, cos_t, sin_t, dq, dk, dv],
                     [q_a_norm, kv_a_norm, wq, wkv, gq, gk], row_outs, acc_outs, xch=xch,
                     scratch=[pltpu.VMEM((ROW_T, wide), BF16), pltpu.VMEM((ROW_T, wide), BF16)])


def _glu_bwd(dy_ssm, y, w_glu, b_glu, w_o_ssm):
    n = y.shape[0]

    def body(dys_ref, y_ref, wg_ref, bg_ref, wo_ref, dy_ref, db_ref, dwg_ref, dwo_ref):
        y_t = y_ref[...]
        z, th = _gelu(y_t)
        zb = z.astype(BF16)
        s = _sigmoid(_dot(zb, wg_ref[...]) + bg_ref[...])
        dys = dys_ref[...]
        dzg = jnp.zeros_like(y_t)
        for j in range(N_DEV):
            dzg += _dot_nt(dys[:, j * OUT_SHARD:(j + 1) * OUT_SHARD], wo_ref[j])
        dt = dzg * z * s * (1.0 - s)
        dtb = dt.astype(BF16)
        dz = dzg * s + _dot_nt(dtb, wg_ref[...])
        dy_ref[...] = dz * _gelu_grad(y_t, th)
        _acc(db_ref, jnp.sum(dt, 0, keepdims=True))
        _acc(dwg_ref, _dot_tn(zb, dtb))
        _acc(dwo_ref, _dot_tn((z * s).astype(BF16), dys))

    acc_outs = [((1, SSM_WIDTH), F32), ((SSM_WIDTH, SSM_WIDTH), F32), ((SSM_WIDTH, D_MODEL), F32)]
    return _row_call(body, "glu_bwd", n, ROW_T, [dy_ssm, y], [w_glu, b_glu, w_o_ssm], [((n, SSM_WIDTH), F32)], acc_outs)


def _ssm_bwd(u, dy, st, bblk, cblk, lam, d_row, xch):
    n = u.shape[0]
    t = min(SCAN_T, n)
    nc = n // t
    kb = 512
    perm = _perm_matrix(t)

    def body(u_ref, dy_ref, st_ref, p_ref, bblk_ref, cblk_ref, lam_ref, d_ref,
             du_ref, dlam_ref, dd_ref, db_ref, dct_ref,
             buf_x, buf_a, pw_ref, carry_ref, xcarry_ref, sx_ref, sa_ref, db_acc, dct_acc):
        @pl.when(pl.program_id(0) == 0)
        def _():
            carry_ref[...] = jnp.zeros_like(carry_ref)
            db_acc[...] = jnp.zeros_like(db_acc)
            dct_acc[...] = jnp.zeros_like(dct_acc)
            _power_table(lam_ref, pw_ref, t // SUBCHUNKS)

        u_t = u_ref[...]
        dy_t = dy_ref[...]
        p = p_ref[...]
        ub = _dot(p, u_t.astype(BF16)).astype(BF16)
        dyb = _dot(p, dy_t.astype(BF16)).astype(BF16)
        _to_states(ub, bblk_ref, buf_x, False)
        xcarry_ref[...] = st_ref[0]
        _run_scan(buf_x, lam_ref, t, False)
        _run_carries(buf_x, pw_ref, xcarry_ref, sx_ref, t, False)
        _run_fix(buf_x, pw_ref, sx_ref, t, False)
        _to_states(dyb, cblk_ref, buf_a, True)
        _run_scan(buf_a, lam_ref, t, True)
        _run_carries(buf_a, pw_ref, carry_ref, sa_ref, t, True)
        _run_fix(buf_a, pw_ref, sa_ref, t, True)
        du_ref[...] = (d_ref[...] * dy_t + _unpermute(p, _to_channels(buf_a, bblk_ref, True))).astype(BF16)
        for b in range(STATE_BLOCKS):
            lanes, ch = _state_block(b)
            db_acc[ch, lanes] += _dot_tn(ub[:, ch], buf_a[:, lanes].astype(BF16))
            dct_acc[ch, lanes] += _dot_tn(dyb[:, ch], buf_x[:, lanes].astype(BF16))
        for c in range(0, N_STATE, kb):
            re, im = pl.ds(c, kb), pl.ds(N_STATE + c, kb)
            xr, xi = buf_x[pl.ds(0, t - 8), re], buf_x[pl.ds(0, t - 8), im]
            ar, ai = buf_a[pl.ds(8, t - 8), re], buf_a[pl.ds(8, t - 8), im]
            x0r, x0i = sx_ref[:, re], sx_ref[:, im]
            a0r, a0i = buf_a[0:8, re], buf_a[0:8, im]
            dlam_part_re = (jnp.sum(ar * xr + ai * xi, 0, keepdims=True)
                            + jnp.sum(a0r * x0r + a0i * x0i, 0, keepdims=True))
            dlam_part_im = (jnp.sum(ai * xr - ar * xi, 0, keepdims=True)
                            + jnp.sum(a0i * x0r - a0r * x0i, 0, keepdims=True))

            @pl.when(pl.program_id(0) == 0)
            def _(c=c):
                dlam_ref[0:1, c:c + kb] = jnp.zeros((1, kb), F32)
                dlam_ref[1:2, c:c + kb] = jnp.zeros((1, kb), F32)

            dlam_ref[0:1, c:c + kb] += dlam_part_re
            dlam_ref[1:2, c:c + kb] += dlam_part_im
        _acc(dd_ref, jnp.sum(dy_t * u_t, 0, keepdims=True))

        @pl.when(pl.program_id(0) == nc - 1)
        def _():
            pltpu.sync_copy(db_acc, db_ref)
            pltpu.sync_copy(dct_acc, dct_ref)

    rev = lambda i: (nc - 1 - i, 0)
    consts = [perm, bblk, cblk, lam, d_row]
    wide = (SSM_WIDTH, 2 * N_STATE)
    return _call(
        body, "ssm_bwd", (nc,), [u, dy, st] + consts,
        [pl.BlockSpec((t, SSM_WIDTH), rev), pl.BlockSpec((t, SSM_WIDTH), rev),
         pl.BlockSpec((1, 8, 2 * N_STATE), lambda i: (nc - 1 - i, 0, 0))] + [_const(a) for a in consts],
        [_sds((n, SSM_WIDTH), BF16), _sds((2, N_STATE), F32), _sds((1, SSM_WIDTH), F32), _sds(wide, F32), _sds(wide, F32)],
        [pl.BlockSpec((t, SSM_WIDTH), rev), pl.BlockSpec((2, N_STATE), lambda i: (0, 0)),
         pl.BlockSpec((1, SSM_WIDTH), lambda i: (0, 0)), ANY, ANY],
        scratch=[pltpu.VMEM((t, 2 * N_STATE), F32)] * 2 + [pltpu.VMEM((t // SUBCHUNKS, 2 * N_STATE), F32)]
        + [pltpu.VMEM((8, 2 * N_STATE), F32)] * 4 + [pltpu.VMEM(wide, F32)] * 2,
        xch=xch)


def _in_proj_bwd(pieces, dh, x, xn_t, norm_mix, w_in_pad, xch):
    n = x.shape[0]
    tm = min(MM_T, n)
    nt = n // tm

    def body(du_ref, dql_ref, dkvl_ref, dgs_ref, dgm_ref, dh_ref, x_ref, xnt_ref, g_ref, w_ref,
             dx_ref, dg_ref, dw_ref, acc_ref):
        @pl.when(pl.program_id(0) == 0)
        def _():
            acc_ref[...] = jnp.zeros_like(acc_ref)

        xnt = xnt_ref[...]
        dxn = jnp.zeros((tm, D_MODEL), F32)
        for ref, (a, b) in zip((du_ref, dql_ref, dkvl_ref, dgs_ref, dgm_ref), IN_SEGS):
            piece = ref[...]
            dxn += _dot_nt(piece, w_ref[:, a:b])
            acc_ref[:, a:b] += _dot(xnt, piece)
        x_t = x_ref[...]
        inv = lax.rsqrt(jnp.sum(x_t * x_t, -1, keepdims=True) * (1.0 / D_MODEL) + EPS)
        dx, dg = _rms_bwd(dxn, x_t, g_ref[...], inv, D_MODEL)
        dx_ref[...] = dh_ref[...] + dx
        _acc(dg_ref, jnp.sum(dg, 0, keepdims=True))

        @pl.when(pl.program_id(0) == nt - 1)
        def _():
            pltpu.sync_copy(acc_ref, dw_ref)

    row_ins, consts = list(pieces) + [dh, x], [norm_mix, w_in_pad]
    in_specs = ([_rows(a, tm) for a in row_ins] + [pl.BlockSpec((D_MODEL, tm), lambda i: (0, i))]
                + [_const(a) for a in consts])
    return _call(
        body, "in_proj_bwd", (nt,), row_ins + [xn_t] + consts, in_specs,
        [_sds((n, D_MODEL), F32), _sds((1, D_MODEL), F32), _sds((D_MODEL, D_IN_PAD), F32)],
        [pl.BlockSpec((tm, D_MODEL), lambda i: (i, 0)), pl.BlockSpec((1, D_MODEL), lambda i: (0, 0)), ANY],
        scratch=[pltpu.VMEM((D_MODEL, D_IN_PAD), F32)], xch=xch)


def _swap_minor(a):
    g, r, c = a.shape[1:]
    return jnp.transpose(a[0], (0, 2, 1)).reshape(g * c, r)


def _pad_in(w):
    return jnp.concatenate([w[:, :KV_END], jnp.zeros((w.shape[0], D_IN_PAD - D_IN), w.dtype), w[:, KV_END:]], axis=1)


def _unpad_in(w):
    return jnp.concatenate([w[:, :KV_END], w[:, KV_END + D_IN_PAD - D_IN:]], axis=1)


def _pad_gain(g):
    return jnp.pad(g, ((0, 0), (0, QK_PAD - QK_HEAD)))


def _place():
    x, y, c = lax.axis_index("x"), lax.axis_index("y"), lax.axis_index("c")
    chips = [(x, y), (1 - x, y), (x, 1 - y), (1 - x, 1 - y)]
    return x, y, c, chips


def _all_gather(block, name):
    rows, lanes = block.shape

    def body(x_ref, out_ref, send_sems, recv_sems, local_sem):
        x, y, c, chips = _place()
        me, sibling = (x, y, c), (x, y, 1 - c)

        def slot(px, py, pc):
            return out_ref.at[4 * px + 2 * py + pc]

        def copy(k, blk, to, src=None):
            return pltpu.make_async_remote_copy(
                src_ref=slot(*blk) if src is None else src, dst_ref=slot(*blk),
                send_sem=send_sems.at[k], recv_sem=recv_sems.at[k], device_id=to, device_id_type=MESH)

        mine = pltpu.make_async_copy(x_ref, slot(*me), local_sem)
        mine.start()
        first = [copy(0, me, sibling, src=x_ref)]
        first += [copy(1 + j, me, (*chip, c), src=x_ref) for j, chip in enumerate(chips[1:])]
        for cp in first:
            cp.start()
        passed = [copy(4 + j, (*chip, c), sibling) for j, chip in enumerate(chips[1:])]
        for j, chip in enumerate(chips[1:]):
            copy(1 + j, (*chip, c), me).wait_recv()
            passed[j].start()
        copy(0, sibling, me).wait_recv()
        for j, chip in enumerate(chips[1:]):
            copy(4 + j, (*chip, 1 - c), me).wait_recv()
        for cp in first + passed:
            cp.wait_send()
        mine.wait()

    return pl.pallas_call(
        body,
        name=name,
        in_specs=[ANY],
        out_specs=ANY,
        out_shape=_sds((N_DEV, rows, lanes), block.dtype),
        scratch_shapes=[pltpu.SemaphoreType.DMA((7,)), pltpu.SemaphoreType.DMA((7,)), pltpu.SemaphoreType.DMA],
    )(block)


def _reduce_scatter(parts, gather, name):
    _, rows, lanes = parts.shape

    def body(p_ref, g_ref, out_ref, ga_ref, own, land_a, send_b, land_b, sa, ra, sb, rb, lo, *g_sems):
        x, y, c, chips = _place()
        sibling = (x, y, 1 - c)
        _xchg_start([False], [g_ref], [ga_ref], *g_sems)

        def blk(chip, core):
            return p_ref.at[4 * chip[0] + 2 * chip[1] + core]

        to_sib = [pltpu.make_async_remote_copy(
            src_ref=blk(chips[k], 1 - c), dst_ref=land_a.at[k], send_sem=sa.at[k], recv_sem=ra.at[k],
            device_id=sibling, device_id_type=MESH) for k in range(4)]
        for cp in to_sib:
            cp.start()
        loads = [pltpu.make_async_copy(blk(chips[k], c), own.at[k], lo.at[k]) for k in range(4)]
        for cp in loads:
            cp.start()
        to_chip = [pltpu.make_async_remote_copy(
            src_ref=send_b.at[j], dst_ref=land_b.at[j], send_sem=sb.at[j], recv_sem=rb.at[j],
            device_id=(*chips[1 + j], c), device_id_type=MESH) for j in range(3)]
        for k in (1, 2, 3):
            to_sib[k].wait_recv()
            loads[k].wait()
            send_b[k - 1] = (own[k] + land_a[k]).astype(BF16)
            to_chip[k - 1].start()
        to_sib[0].wait_recv()
        loads[0].wait()
        acc = own[0] + land_a[0]
        for j in range(3):
            to_chip[j].wait_recv()
            acc = acc + land_b[j].astype(F32)
        out_ref[...] = acc
        for cp in to_sib + to_chip:
            cp.wait_send()
        _xchg_wait([False], [g_ref], [ga_ref], *g_sems)

    return pl.pallas_call(
        body,
        name=name,
        in_specs=[ANY, ANY],
        out_specs=[pl.BlockSpec(memory_space=pltpu.VMEM), ANY],
        out_shape=[_sds((rows, lanes), F32), _sds((N_DEV,) + gather.shape, gather.dtype)],
        scratch_shapes=[pltpu.VMEM((4, rows, lanes), F32), pltpu.VMEM((4, rows, lanes), F32),
                        pltpu.VMEM((3, rows, lanes), BF16), pltpu.VMEM((3, rows, lanes), BF16)]
        + [pltpu.SemaphoreType.DMA((4,))] * 2 + [pltpu.SemaphoreType.DMA((3,))] * 2 + [pltpu.SemaphoreType.DMA((4,))]
        + [pltpu.SemaphoreType.DMA((1,))] * 3,
        compiler_params=_params(),
    )(parts, gather)


def _adamw_math(w, g, m, v):
    m = ADAM_B1 * m + (1.0 - ADAM_B1) * g
    v = ADAM_B2 * v + (1.0 - ADAM_B2) * (g * g)
    m_hat = m / (1.0 - ADAM_B1 ** ADAM_STEP)
    v_hat = v / (1.0 - ADAM_B2 ** ADAM_STEP)
    delta = -ADAM_LR * (m_hat / (jnp.sqrt(v_hat) + ADAM_EPS) + ADAM_WD * w)
    return delta, m, v


def _row_tile(r):
    return max(t for t in range(8, min(r, 256) + 1, 8) if r % t == 0)


def _adamw(w, g, m, v, name):
    r, n = w.shape

    def body(w_ref, g_ref, m_ref, v_ref, d_ref, nm_ref, nv_ref):
        d_ref[...], nm_ref[...], nv_ref[...] = _adamw_math(w_ref[...], g_ref[...], m_ref[...], v_ref[...])

    return _row_call(body, name, r, _row_tile(r), [w, g, m, v], [], [((r, n), F32)] * 3)


def _adamw_sum(landed, w, m, v, name):
    r, n = w.shape

    def body(l_ref, w_ref, m_ref, v_ref, g_ref, d_ref, nm_ref, nv_ref):
        g = l_ref[0].astype(F32)
        for dev in range(1, N_DEV):
            g = g + l_ref[dev].astype(F32)
        g_ref[...] = g
        d_ref[...], nm_ref[...], nv_ref[...] = _adamw_math(w_ref[...], g, m_ref[...], v_ref[...])

    tm = max(t for t in range(16, min(r, 256) + 1, 16) if r % t == 0)
    return _row_call(body, name, r, tm, [landed, w, m, v], [], [((r, n), F32)] * 4)


def _adamw_small(first, rest, w, m, v, row_counts):
    n_rest = w.shape[0] - first.shape[1]

    def body(f_ref, r_ref, w_ref, m_ref, v_ref, loss_ref, *out_refs):
        gf, gr = f_ref[0], r_ref[0]
        for dev in range(1, N_DEV):
            gf, gr = gf + f_ref[dev], gr + r_ref[dev]
        loss_ref[...] = gr[n_rest:n_rest + 8]
        g = jnp.concatenate([gf, gr[0:n_rest]], axis=0)
        d, nm, nv = _adamw_math(w_ref[...], g, m_ref[...], v_ref[...])
        off = 0
        for p, rows in enumerate(row_counts):
            for k, val in enumerate((g, d, nm, nv)):
                out_refs[4 * p + k][...] = val[off:off + rows]
            off += rows

    outs = [_sds((8, LANES), F32)] + [_sds((rows, LANES), F32) for rows in row_counts for _ in range(4)]
    return pl.pallas_call(body, name="adamw_small", out_shape=outs, compiler_params=_params())(first, rest, w, m, v)


SMALL = ("norm_mix", "q_a_norm", "kv_a_norm", "q_norm", "k_norm", "ssm_a_re", "ssm_a_im", "ssm_log_dt", "ssm_b_re",
         "ssm_b_im", "ssm_c_re", "ssm_c_im", "ssm_d", "b_glu", "norm_mlp")
WEIGHT_ORDER = ("norm_mix", "w_in", "q_a_norm", "kv_a_norm", "w_q_b", "w_kv_b", "q_norm", "k_norm", "w_o_mla",
                "ssm_a_re", "ssm_a_im", "ssm_log_dt", "ssm_b_re", "ssm_b_im", "ssm_c_re", "ssm_c_im", "ssm_d", "w_glu",
                "b_glu", "w_o_ssm", "w_out", "norm_mlp", "w_up", "w_down")
IN_SHARD = D_IN // N_DEV


def _pack_small(vals, names=SMALL):
    parts = []
    for n in names:
        flat = vals[n].reshape(-1)
        size = -(-flat.shape[0] // (8 * LANES)) * 8 * LANES
        parts.append(jnp.pad(flat, (0, size - flat.shape[0])).reshape(-1, LANES))
    return jnp.concatenate(parts, axis=0)


def _small_rows(like):
    return [-(-like[n].size // (8 * LANES)) * 8 for n in SMALL]


def _step(x, pos_col, target, w, small):
    bf = {n: a.astype(BF16) for n, a in w.items()}
    gq, gk = _pad_gain(small["q_norm"]), _pad_gain(small["k_norm"])
    a_re = small["ssm_a_re"].reshape(1, N_STATE)
    a_im = small["ssm_a_im"].reshape(1, N_STATE)
    log_dt = jnp.repeat(small["ssm_log_dt"].reshape(SSM_GROUPS), SSM_STATE).reshape(1, N_STATE)
    bt_re, bt_im = _swap_minor(small["ssm_b_re"]), _swap_minor(small["ssm_b_im"])
    c2_re, c2_im = _swap_minor(small["ssm_c_re"]), _swap_minor(small["ssm_c_im"])
    d_row = small["ssm_d"].reshape(1, SSM_WIDTH)

    w_in_all = _all_gather(bf["w_in"], "gather_w_in")
    w_in_pad = _pad_in(jnp.transpose(w_in_all, (1, 0, 2)).reshape(D_MODEL, D_IN))
    cos_t, sin_t = _rope_tables(pos_col)
    lam, bblk, cblk = _ssm_prep(a_re, a_im, log_dt, bt_re, bt_im, c2_re, c2_im)
    wq_mine = jnp.pad(bf["w_q_b"], ((0, 0), (0, QK_PAD - QK_HEAD)))
    u, ql, kvl, gs, gm, xn_t, w_glu, w_o_ssm = _in_proj(
        x, small["norm_mix"], w_in_pad, xch=[(bf["w_glu"], False), (bf["w_o_ssm"], False)])
    w_glu = w_glu.reshape(SSM_WIDTH, SSM_WIDTH)
    y, y_ssm, st, wq, wkv, w_o_mla, w_out = _ssm_fwd(
        u, bblk, cblk, lam, d_row, w_glu, small["b_glu"], w_o_ssm,
        xch=[(wq_mine, False), (bf["w_kv_b"], False), (bf["w_o_mla"], False), (bf["w_out"], False)])
    w_o_mla, w_out = w_o_mla.reshape(D_MODEL, D_MODEL), w_out.reshape(D_MODEL, D_MODEL)
    wq = jnp.transpose(wq, (1, 0, 2)).reshape(Q_LORA, N_HEADS * QK_PAD)
    wkv = jnp.transpose(wkv, (1, 0, 2)).reshape(KV_LORA, N_HEADS * QK_PAD)
    q, k, v, kt, vt = _qkv_prep(ql, kvl, small["q_a_norm"], small["kv_a_norm"], wq, wkv, gq, gk, cos_t, sin_t)
    attn, lse, attn_t, w_up, w_down = _attn_fwd(q, k, vt, xch=[(bf["w_up"], False), (bf["w_down"], False)])
    h, y_mla, mixed_t = _merge(attn, gs, gm, y_ssm, x, w_o_mla, w_out)
    hn, dout, hn_t, loss = _mlp_fwd_loss(h, target, small["norm_mlp"], w_up, w_down)

    da, dh, dout_b, hid_t, d_norm_mlp = _mlp_bwd(dout, hn, h, small["norm_mlp"], w_up, w_down)
    p_w_down = _matmul_tn_shards(hid_t, dout_b, "dw_down", False, tm=1024, turned=True)
    p_w_up = _matmul_tn_shards(hn_t, da, "dw_up", True, turned=True)
    dgs, dgm, dy_ssm, dy_mla, dattn, delta = _merge_bwd(dh, gs, gm, y_ssm, y_mla, attn, w_out, w_o_mla)
    p_w_out = _matmul_tn_shards(mixed_t, dh, "dw_out", False, tm=1024, turned=True)
    p_w_o_mla = _matmul_tn_shards(attn_t, dy_mla, "dw_o_mla", False, turned=True)
    dq, dk, dv, l_w_up, l_w_down, l_w_out, l_w_o_mla = _attn_bwd(
        q, k, kt, v, lse, delta, dattn, xch=[(p_w_up, True), (p_w_down, True), (p_w_out, True), (p_w_o_mla, True)])
    dql, dkvl, d_q_a_norm, d_kv_a_norm, d_gq, d_gk, g_wq, g_wkv = _qkv_prep_bwd(
        ql, kvl, dq, dk, dv, small["q_a_norm"], small["kv_a_norm"], wq, wkv, gq, gk, cos_t, sin_t, xch=[])
    p_wq = jnp.transpose(g_wq.reshape(Q_LORA, N_HEADS, QK_PAD), (1, 0, 2)).astype(BF16)
    p_wkv = jnp.transpose(g_wkv.reshape(KV_LORA, N_HEADS, QK_PAD), (1, 0, 2)).astype(BF16)
    dy, d_b_glu, g_w_glu, g_w_o_ssm = _glu_bwd(dy_ssm, y, w_glu, small["b_glu"], w_o_ssm)
    p_w_o_ssm = jnp.transpose(g_w_o_ssm.reshape(SSM_WIDTH, N_DEV, OUT_SHARD), (1, 0, 2)).astype(BF16)
    p_w_glu = g_w_glu.reshape(N_DEV, SSM_WIDTH // N_DEV, SSM_WIDTH).astype(BF16)
    du, dlam, d_d, d_bblk, d_cblk_t, l_wq, l_wkv, l_w_glu, l_w_o_ssm = _ssm_bwd(
        u, dy, st, bblk, cblk, lam, d_row, xch=[(p_wq, True), (p_wkv, True), (p_w_glu, True), (p_w_o_ssm, True)])
    d_a_re, d_a_im, d_log_dt, d_bt_re, d_bt_im, d_c_re, d_c_im = _ssm_prep_bwd(
        a_re, a_im, log_dt, bt_re, bt_im, dlam, d_bblk, d_cblk_t)
    tr = lambda mat: jnp.transpose(mat.reshape(SSM_GROUPS, SSM_GROUP_CH, SSM_STATE), (0, 2, 1))
    g_small = {
        "q_a_norm": d_q_a_norm, "kv_a_norm": d_kv_a_norm, "q_norm": d_gq[:, :QK_HEAD], "k_norm": d_gk[:, :QK_HEAD],
        "ssm_a_re": d_a_re, "ssm_a_im": d_a_im, "ssm_log_dt": d_log_dt,
        "ssm_b_re": tr(d_bt_re), "ssm_b_im": tr(d_bt_im), "ssm_c_re": d_c_re, "ssm_c_im": d_c_im,
        "ssm_d": d_d, "b_glu": d_b_glu, "norm_mlp": d_norm_mlp,
    }
    rest = jnp.concatenate([_pack_small(g_small, SMALL[1:]), loss], axis=0)
    dx, d_norm_mix, g_w_in_pad, g_rest_all = _in_proj_bwd(
        (du, dql, dkvl, dgs, dgm), dh, x, xn_t, small["norm_mix"], w_in_pad, xch=[(rest, False)])
    parts = jnp.transpose(_unpad_in(g_w_in_pad).reshape(D_MODEL, N_DEV, IN_SHARD), (1, 0, 2))
    g_w_in_mine, g_first_all = _reduce_scatter(parts, _pack_small({SMALL[0]: d_norm_mix}, SMALL[:1]), "reduce_w_in")
    landed = {"w_q_b": l_wq[:, :, :QK_HEAD], "w_kv_b": l_wkv, "w_o_mla": l_w_o_mla, "w_glu": l_w_glu,
              "w_o_ssm": l_w_o_ssm, "w_out": l_w_out, "w_up": l_w_up, "w_down": l_w_down}
    return dx, landed, g_w_in_mine, g_first_all, g_rest_all


def kernel(x, positions, norm_mix, w_in, q_a_norm, kv_a_norm, w_q_b, w_kv_b, q_norm, k_norm, w_o_mla, ssm_a_re, ssm_a_im, ssm_log_dt, ssm_b_re, ssm_b_im, ssm_c_re, ssm_c_im, ssm_d, w_glu, b_glu, w_o_ssm, w_out, norm_mlp, w_up, w_down, loss_target, m_norm_mix, m_w_in, m_q_a_norm, m_kv_a_norm, m_w_q_b, m_w_kv_b, m_q_norm, m_k_norm, m_w_o_mla, m_ssm_a_re, m_ssm_a_im, m_ssm_log_dt, m_ssm_b_re, m_ssm_b_im, m_ssm_c_re, m_ssm_c_im, m_ssm_d, m_w_glu, m_b_glu, m_w_o_ssm, m_w_out, m_norm_mlp, m_w_up, m_w_down, v_norm_mix, v_w_in, v_q_a_norm, v_kv_a_norm, v_w_q_b, v_w_kv_b, v_q_norm, v_k_norm, v_w_o_mla, v_ssm_a_re, v_ssm_a_im, v_ssm_log_dt, v_ssm_b_re, v_ssm_b_im, v_ssm_c_re, v_ssm_c_im, v_ssm_d, v_w_glu, v_b_glu, v_w_o_ssm, v_w_out, v_norm_mlp, v_w_up, v_w_down):
    given = dict(locals())
    w = {n: given[n] for n in WEIGHT_ORDER}
    m = {n: given["m_" + n] for n in WEIGHT_ORDER}
    v = {n: given["v_" + n] for n in WEIGHT_ORDER}
    big = [n for n in WEIGHT_ORDER if n not in SMALL]
    small = {n: w[n] for n in SMALL}

    dx, landed, g_w_in, g_first_all, g_rest_all = _step(
        x[0], positions.reshape(-1, 1), loss_target[0], {n: w[n][0] for n in big}, small)

    grads, deltas, new_m, new_v = {}, {}, {}, {}
    for n in big:
        if n in ("w_in", "w_q_b"):
            wt, mt, vt = jnp.transpose(w[n][0]), jnp.transpose(m[n][0]), jnp.transpose(v[n][0])
            if n == "w_in":
                g = jnp.transpose(g_w_in)
                d, nm, nv = _adamw(wt, g, mt, vt, "adamw_" + n)
            else:
                g, d, nm, nv = _adamw_sum(jnp.transpose(landed[n], (0, 2, 1)), wt, mt, vt, "adamw_" + n)
            g, d, nm, nv = (jnp.transpose(a) for a in (g, d, nm, nv))
        else:
            g, d, nm, nv = _adamw_sum(landed[n], w[n][0], m[n][0], v[n][0], "adamw_" + n)
        grads[n], deltas[n], new_m[n], new_v[n] = g[None], d[None], nm[None], nv[None]

    outs = _adamw_small(g_first_all, g_rest_all, _pack_small(small), _pack_small({n: m[n] for n in SMALL}),
                        _pack_small({n: v[n] for n in SMALL}), _small_rows(small))
    for p, n in enumerate(SMALL):
        for k, dst in enumerate((grads, deltas, new_m, new_v)):
            dst[n] = outs[1 + 4 * p + k].reshape(-1)[:small[n].size].reshape(small[n].shape)

    return (outs[0][0, 0], dx[None], *[grads[n] for n in WEIGHT_ORDER], *[deltas[n] for n in WEIGHT_ORDER],
            *[new_m[n] for n in WEIGHT_ORDER], *[new_v[n] for n in WEIGHT_ORDER])
```

```python
import functools
import math

import numpy as np
import jax
import jax.numpy as jnp
from jax import lax
from jax.experimental import pallas as pl
from jax.experimental.pallas import tpu as pltpu

F32 = jnp.float32
BF16 = jnp.bfloat16

D_MODEL = 1024
SSM_GROUPS = 32
SSM_GROUP_CH = 16
SSM_WIDTH = 512
SSM_STATE = 64
N_STATE = SSM_GROUPS * SSM_STATE
N_HEADS = 8
QK_NOPE = 128
QK_ROPE = 64
QK_HEAD = 192
QK_PAD = 256
V_HEAD = 128
Q_LORA = 384
KV_LORA = 256
KV_LAT_PAD = 384
ROPE_THETA = 10000.0
D_FF = 4096
EPS = 1e-6
ATT_SCALE = QK_HEAD ** -0.5
N_DEV = 8
FF_SHARD = D_FF // N_DEV
OUT_SHARD = D_MODEL // N_DEV

IN_SEGS = ((0, 512), (512, 896), (896, 1280), (1280, 2304), (2304, 3328))
D_IN = 3264
D_IN_PAD = 3328
KV_END = 1216

ADAM_LR = 0.001
ADAM_B1 = 0.9
ADAM_B2 = 0.999
ADAM_EPS = 1e-08
ADAM_WD = 0.01
ADAM_STEP = 10

VMEM_LIMIT = 56 * 1024 * 1024
MESH = pl.DeviceIdType.MESH
ANY = pl.BlockSpec(memory_space=pl.ANY)
LANES = 128

SCAN_T = 256
SUBCHUNKS = 8
SCAN_CG = 1024
ATT_T = 512
ATT_SUB = 1
ATT_HEADS = 4
ATT_BWD_HEADS = 2
ROW_T = 256
MM_T = 512


def _params(sem=None):
    return pltpu.CompilerParams(dimension_semantics=sem, vmem_limit_bytes=VMEM_LIMIT)


def _rows(arr, tm):
    if arr.ndim == 2:
        return pl.BlockSpec((tm, arr.shape[1]), lambda i: (i, 0))
    return pl.BlockSpec((arr.shape[0], tm, arr.shape[2]), lambda i: (0, i, 0))


def _const(arr):
    nd = arr.ndim
    return pl.BlockSpec(arr.shape, lambda *_: (0,) * nd, pipeline_mode=pl.Buffered(1))


def _sds(shape, dtype):
    return jax.ShapeDtypeStruct(shape, dtype)


PEERS = tuple((dx, dy, dc) for dx in (0, 1) for dy in (0, 1) for dc in (0, 1) if (dx, dy, dc) != (0, 0, 0))


def _here():
    x, y, c = lax.axis_index("x"), lax.axis_index("y"), lax.axis_index("c")
    return x, y, c, 4 * x + 2 * y + c


def _xchg_start(scatter, srcs, dsts, send, recv, local):
    x, y, c, me = _here()
    for e, sc in enumerate(scatter):
        src, dst = srcs[e], dsts[e]
        pltpu.make_async_copy(src.at[me] if sc else src, dst.at[me], local.at[e]).start()
        for dx, dy, dc in PEERS:
            px, py, pc = (1 - x if dx else x), (1 - y if dy else y), (1 - c if dc else c)
            pltpu.make_async_remote_copy(
                src_ref=src.at[4 * px + 2 * py + pc] if sc else src, dst_ref=dst.at[me],
                send_sem=send.at[e], recv_sem=recv.at[e], device_id=(px, py, pc), device_id_type=MESH).start()


def _xchg_wait(scatter, srcs, dsts, send, recv, local):
    x, y, c, me = _here()
    for e, sc in enumerate(scatter):
        src, dst = srcs[e], dsts[e]
        pltpu.make_async_copy(src.at[me] if sc else src, dst.at[me], local.at[e]).wait()
        span = dst.at[pl.ds(0, N_DEV - 1)]
        both = pltpu.make_async_remote_copy(src_ref=span, dst_ref=span, send_sem=send.at[e], recv_sem=recv.at[e],
                                            device_id=(x, y, c), device_id_type=MESH)
        both.wait_send()
        both.wait_recv()


def _call(body, name, grid, ins, in_specs, outs, out_specs, scratch=(), xch=()):
    n_in, n_out, ne = len(ins), len(outs), len(xch)
    scatter = [sc for _, sc in xch]
    x_outs = [_sds((N_DEV,) + (a.shape[1:] if sc else a.shape), a.dtype) for a, sc in xch]
    sems = [pltpu.SemaphoreType.DMA((ne,))] * 3 if ne else []

    def wrapped(*refs):
        in_refs, x_src = refs[:n_in], refs[n_in:n_in + ne]
        out_refs = refs[n_in + ne:n_in + ne + n_out]
        x_dst = refs[n_in + ne + n_out:n_in + 2 * ne + n_out]
        rest = refs[n_in + 2 * ne + n_out:]
        if ne:
            x_sems, rest = rest[len(rest) - 3:], rest[:len(rest) - 3]
            first = functools.reduce(jnp.logical_and, [pl.program_id(d) == 0 for d in range(len(grid))])
            last = functools.reduce(jnp.logical_and, [pl.program_id(d) == grid[d] - 1 for d in range(len(grid))])

            @pl.when(first)
            def _():
                _xchg_start(scatter, x_src, x_dst, *x_sems)

        body(*in_refs, *out_refs, *rest)
        if ne:
            @pl.when(last)
            def _():
                _xchg_wait(scatter, x_src, x_dst, *x_sems)

    return pl.pallas_call(
        wrapped,
        name=name,
        grid=grid,
        in_specs=list(in_specs) + [ANY] * ne,
        out_specs=list(out_specs) + [ANY] * ne,
        out_shape=list(outs) + x_outs,
        scratch_shapes=list(scratch) + sems,
        compiler_params=_params(("arbitrary",) * len(grid)),
    )(*ins, *[a for a, _ in xch])


def _row_call(body, name, n_rows, tm, row_ins, const_ins, row_outs, acc_outs=(), xch=(), col_outs=(), scratch=()):
    outs = [_sds(s, d) for s, d in list(row_outs) + list(col_outs) + list(acc_outs)]
    n_row, n_col = len(row_outs), len(col_outs)
    out_specs = [_rows(o, tm) for o in outs[:n_row]] + [
        pl.BlockSpec(o.shape[:-1] + (tm,), lambda i, nd=len(o.shape): (0,) * (nd - 1) + (i,))
        for o in outs[n_row:n_row + n_col]] + [
        pl.BlockSpec(o.shape, lambda i, nd=len(o.shape): (0,) * nd) for o in outs[n_row + n_col:]]
    in_specs = [_rows(a, tm) for a in row_ins] + [_const(a) for a in const_ins]
    return _call(body, name, (n_rows // tm,), list(row_ins) + list(const_ins), in_specs, outs, out_specs,
                 scratch=scratch, xch=xch)


def _dot(a, b):
    return jnp.dot(a, b, preferred_element_type=F32)


def _dot_nt(a, b):
    return lax.dot_general(a, b, (((1,), (1,)), ((), ())), preferred_element_type=F32)


def _dot_tn(a, b):
    return lax.dot_general(a, b, (((0,), (0,)), ((), ())), preferred_element_type=F32)


def _rms(x, g, n):
    inv = lax.rsqrt(jnp.sum(x * x, -1, keepdims=True) * (1.0 / n) + EPS)
    return x * inv * g, inv


def _rms_bwd(dy, x, g, inv, n):
    xh = x * inv
    dxh = dy * g
    dx = inv * (dxh - xh * (jnp.sum(dxh * xh, -1, keepdims=True) * (1.0 / n)))
    return dx, dy * xh


def _sigmoid(x):
    return 1.0 / (1.0 + jnp.exp(-x))


_GELU_C = math.sqrt(2.0 / math.pi)


def _gelu(y):
    th = jnp.tanh(_GELU_C * (y + 0.044715 * (y * y * y)))
    return 0.5 * y * (1.0 + th), th


def _gelu_grad(y, th):
    return 0.5 * (1.0 + th) + 0.5 * y * (1.0 - th * th) * (_GELU_C * (1.0 + 3.0 * 0.044715 * (y * y)))


def _acc(ref, val):
    @pl.when(pl.program_id(0) == 0)
    def _():
        ref[...] = jnp.zeros_like(ref)

    ref[...] += val


def _tile(n, limit):
    if n <= limit:
        return n
    return max(t for t in range(128, limit + 1, 128) if n % t == 0)


def _lhs(a, turned, tm, tk):
    m, k_dim = a.shape if turned else a.shape[::-1]
    tm, tk = _tile(m, tm), _tile(k_dim, tk)
    if turned:
        return m, k_dim, tm, tk, pl.BlockSpec((tm, tk), lambda i, k: (i, k)), _dot
    return m, k_dim, tm, tk, pl.BlockSpec((tk, tm), lambda i, k: (k, i)), _dot_tn


def _matmul_tn_shards(a, b, name, by_col, tm=512, tk=512, turned=False):
    m, k_dim, tm, tk, a_spec, dot = _lhs(a, turned, tm, tk)
    n = b.shape[1]
    nk = k_dim // tk
    if by_col:
        r, c = m, n // N_DEV
        out_spec = pl.BlockSpec((N_DEV, tm, c), lambda i, k: (0, i, 0))
    else:
        r, c = m // N_DEV, n
        per = tm // r
        out_spec = pl.BlockSpec((per, r, c), lambda i, k: (i, 0, 0))

    def body(a_ref, b_ref, o_ref, acc_ref):
        k = pl.program_id(1)

        @pl.when(k == 0)
        def _():
            acc_ref[...] = jnp.zeros_like(acc_ref)

        acc_ref[...] += dot(a_ref[...].astype(BF16), b_ref[...].astype(BF16))

        @pl.when(k == nk - 1)
        def _():
            if by_col:
                for j in range(N_DEV):
                    o_ref[j] = acc_ref[:, j * c:(j + 1) * c].astype(BF16)
            else:
                for s in range(per):
                    o_ref[s] = acc_ref[s * r:(s + 1) * r, :].astype(BF16)

    return pl.pallas_call(
        body,
        name=name,
        grid=(m // tm, nk),
        in_specs=[a_spec, pl.BlockSpec((tk, n), lambda i, k: (k, 0))],
        out_specs=out_spec,
        out_shape=_sds((N_DEV, r, c), BF16),
        scratch_shapes=[pltpu.VMEM((tm, n), F32)],
        compiler_params=_params(("parallel", "arbitrary")),
    )(a, b)


def _rope_tables(pos_col):
    n = pos_col.shape[0]
    half = QK_ROPE // 2
    inv_freq = (ROPE_THETA ** (-np.arange(half, dtype=np.float32) / half)).astype(np.float32)
    freq_row = jnp.asarray(np.concatenate([inv_freq, inv_freq, np.zeros(64, np.float32)])[None, :])

    def body(p_ref, f_ref, c_ref, s_ref):
        ang = p_ref[...].astype(F32) * f_ref[...]
        c_ref[...] = jnp.cos(ang)
        s_ref[...] = jnp.sin(ang)

    return _row_call(body, "rope_tables", n, min(n, 1024), [pos_col], [freq_row], [((n, 128), F32)] * 2)


def _rope_rot(v):
    lane = lax.broadcasted_iota(jnp.int32, v.shape, 1)
    return jnp.where(lane < 32, -pltpu.roll(v, 96, 1), jnp.where(lane < 64, pltpu.roll(v, 32, 1), 0.0))


def _rope_rot_t(v):
    lane = lax.broadcasted_iota(jnp.int32, v.shape, 1)
    return jnp.where(lane < 32, pltpu.roll(v, 96, 1), jnp.where(lane < 64, -pltpu.roll(v, 32, 1), 0.0))


def _in_proj(x, norm_mix, w_in_pad, xch):
    n = x.shape[0]

    def body(x_ref, g_ref, w_ref, u_ref, ql_ref, kvl_ref, gs_ref, gm_ref, xnt_ref):
        xn, _ = _rms(x_ref[...], g_ref[...], D_MODEL)
        xb = xn.astype(BF16)
        xnt_ref[...] = xn.T.astype(BF16)
        for ref, (a, b) in zip((u_ref, ql_ref, kvl_ref, gs_ref, gm_ref), IN_SEGS):
            ref[...] = _dot(xb, w_ref[:, a:b])

    outs = [((n, b - a), F32) for a, b in IN_SEGS]
    return _row_call(body, "in_proj", n, MM_T, [x], [norm_mix, w_in_pad], outs, xch=xch,
                     col_outs=[((D_MODEL, n), BF16)])


def _ssm_prep_fn(a_re, a_im, log_dt, b_re_x, b_im_x):
    dt = jnp.exp(log_dt)
    mag = jnp.exp(a_re * dt)
    lr = mag * jnp.cos(a_im * dt)
    li = mag * jnp.sin(a_im * dt)
    den = a_re * a_re + a_im * a_im
    fr = ((lr - 1.0) * a_re + li * a_im) / den
    fi = (li * a_re - (lr - 1.0) * a_im) / den
    return lr, li, fr * b_re_x - fi * b_im_x, fr * b_im_x + fi * b_re_x


def _dot_exact(a, b, dims):
    return lax.dot_general(a, b, (dims, ((), ())), precision=lax.Precision.HIGHEST, preferred_element_type=F32)


def _lane_repeat(width, n):
    src = lax.broadcasted_iota(jnp.int32, (width, n), 0)
    dst = lax.broadcasted_iota(jnp.int32, (width, n), 1)
    return (dst % width == src).astype(F32)


def _same_group(rows, rows_per_group, cols, cols_per_group):
    row = lax.broadcasted_iota(jnp.int32, (rows, cols), 0)
    col = lax.broadcasted_iota(jnp.int32, (rows, cols), 1)
    return (row // rows_per_group) == (col // cols_per_group)


def _expand_b(bt):
    tiled = _dot_exact(bt, _lane_repeat(SSM_STATE, N_STATE), ((1,), (0,)))
    return jnp.where(_same_group(SSM_WIDTH, SSM_GROUP_CH, N_STATE, SSM_STATE), tiled, 0.0)


def _collect_b(m):
    masked = jnp.where(_same_group(SSM_WIDTH, SSM_GROUP_CH, N_STATE, SSM_STATE), m, 0.0)
    return _dot_exact(masked, _lane_repeat(SSM_STATE, N_STATE), ((1,), (1,)))


def _ssm_prep(a_re, a_im, log_dt, bt_re, bt_im, c2_re, c2_im, w_in_shard):
    def work(ar, ai, ld, br, bi, cr, ci, lam_ref, bblk_ref, cblk_ref):
        lr, li, bbr, bbi = _ssm_prep_fn(ar[...], ai[...], ld[...], _expand_b(br[...]), _expand_b(bi[...]))
        lam_ref[0:1, :] = lr
        lam_ref[1:2, :] = li
        bblk_ref[:, 0:N_STATE] = bbr.astype(BF16)
        bblk_ref[:, N_STATE:] = bbi.astype(BF16)
        rep = _lane_repeat(SSM_GROUP_CH, SSM_WIDTH)
        own = _same_group(N_STATE, SSM_STATE, SSM_WIDTH, SSM_GROUP_CH)
        cblk_ref[0:N_STATE, :] = jnp.where(own, _dot_exact(cr[...], rep, ((1,), (0,))), 0.0).astype(BF16)
        cblk_ref[N_STATE:, :] = jnp.where(own, -_dot_exact(ci[...], rep, ((1,), (0,))), 0.0).astype(BF16)

    gathered, lam, bblk, cblk = _all_gather(
        w_in_shard, "gather_w_in_ssm_prep", work, [a_re, a_im, log_dt, bt_re, bt_im, c2_re, c2_im],
        [_sds((2, N_STATE), F32), _sds((SSM_WIDTH, 2 * N_STATE), BF16), _sds((2 * N_STATE, SSM_WIDTH), BF16)])
    return lam, bblk, cblk, gathered


def _ssm_prep_bwd(a_re, a_im, log_dt, bt_re, bt_im, dlam, dbblk, dcblk_t):
    def body(ar, ai, ld, br, bi, dl, db, dc, dar, dai, dld, dbr, dbi, dcr, dci):
        _, vjp = jax.vjp(_ssm_prep_fn, ar[...], ai[...], ld[...], _expand_b(br[...]), _expand_b(bi[...]))
        g = vjp((dl[0:1, :], dl[1:2, :], db[:, 0:N_STATE], db[:, N_STATE:]))
        dar[...] = g[0]
        dai[...] = g[1]
        grp = lax.broadcasted_iota(jnp.int32, (SSM_GROUPS, N_STATE), 0)
        lane = lax.broadcasted_iota(jnp.int32, (SSM_GROUPS, N_STATE), 1)
        sel = (lane // SSM_STATE) == grp
        dld[...] = jnp.sum(jnp.where(sel, jnp.broadcast_to(g[2], (SSM_GROUPS, N_STATE)), 0.0), axis=1, keepdims=True)
        dbr[...] = _collect_b(g[3])
        dbi[...] = _collect_b(g[4])
        dcr[...] = _collect_b(dc[:, 0:N_STATE])
        dci[...] = -_collect_b(dc[:, N_STATE:])

    small = _sds((SSM_WIDTH, SSM_STATE), F32)
    return pl.pallas_call(
        body,
        name="ssm_prep_bwd",
        out_shape=[_sds((1, N_STATE), F32), _sds((1, N_STATE), F32), _sds((SSM_GROUPS, 1), F32), small, small, small, small],
        compiler_params=_params(),
    )(a_re, a_im, log_dt, bt_re, bt_im, dlam, dbblk, dcblk_t)


def _perm_matrix(t):
    run = t // SUBCHUNKS
    p = np.zeros((t, t), np.float32)
    r = np.arange(t)
    p[r, (r % SUBCHUNKS) * run + r // SUBCHUNKS] = 1.0
    return jnp.asarray(p, dtype=BF16)


def _unpermute(p, a):
    hi = a.astype(BF16)
    r1 = a - hi.astype(F32)
    mid = r1.astype(BF16)
    lo = (r1 - mid.astype(F32)).astype(BF16)
    return _dot_tn(p, hi) + _dot_tn(p, mid) + _dot_tn(p, lo)


def _power_table(lam_ref, pw_ref, n):
    lr, li = lam_ref[0:1, :], lam_ref[1:2, :]
    pw_ref[0:1, 0:N_STATE] = lr
    pw_ref[0:1, N_STATE:] = li

    def step(i, carry):
        pr, pi = carry
        pr, pi = pr * lr - pi * li, pr * li + pi * lr
        pw_ref[pl.ds(i, 1), 0:N_STATE] = pr
        pw_ref[pl.ds(i, 1), N_STATE:] = pi
        return pr, pi

    lax.fori_loop(1, n, step, (lr, li))


def _col_groups():
    return [(pl.ds(c, SCAN_CG), pl.ds(N_STATE + c, SCAN_CG)) for c in range(0, N_STATE, SCAN_CG)]


def _run_scan(buf, lam_ref, t, reverse):
    nblk = t // 8
    for re, im in _col_groups():
        lr = jnp.broadcast_to(lam_ref[0:1, re], (8, SCAN_CG))
        li = jnp.broadcast_to(lam_ref[1:2, re], (8, SCAN_CG))
        if reverse:
            li = -li
        first = pl.ds((nblk - 1) * 8 if reverse else 0, 8)

        def step(k, carry, re=re, im=im, lr=lr, li=li):
            pr, pi = carry
            i = (nblk - 2 - k) if reverse else (k + 1)
            r = pl.ds(pl.multiple_of(i * 8, 8), 8)
            xr = buf[r, re] + lr * pr - li * pi
            xi = buf[r, im] + lr * pi + li * pr
            buf[r, re] = xr
            buf[r, im] = xi
            return xr, xi

        lax.fori_loop(0, nblk - 1, step, (buf[first, re], buf[first, im]))


def _run_carries(buf, pw_ref, carry_ref, s_ref, t, reverse):
    nblk = t // 8
    run = t // SUBCHUNKS
    edge = buf[pl.ds(0 if reverse else (nblk - 1) * 8, 8), :]
    pr, pi = pw_ref[run - 1:run, 0:N_STATE], pw_ref[run - 1:run, N_STATE:]
    if reverse:
        pi = -pi
    sr, si = carry_ref[0:1, 0:N_STATE], carry_ref[0:1, N_STATE:]
    for s in (range(SUBCHUNKS - 1, -1, -1) if reverse else range(SUBCHUNKS)):
        s_ref[s:s + 1, 0:N_STATE] = sr
        s_ref[s:s + 1, N_STATE:] = si
        er, ei = edge[s:s + 1, 0:N_STATE], edge[s:s + 1, N_STATE:]
        sr, si = er + pr * sr - pi * si, ei + pr * si + pi * sr
    carry_ref[:, 0:N_STATE] = jnp.broadcast_to(sr, (8, N_STATE))
    carry_ref[:, N_STATE:] = jnp.broadcast_to(si, (8, N_STATE))


def _run_fix(buf, pw_ref, s_ref, t, reverse):
    nblk = t // 8
    for re, im in _col_groups():
        sr, si = s_ref[:, re], s_ref[:, im]

        def step(i, carry, re=re, im=im, sr=sr, si=si):
            r = pl.ds(pl.multiple_of(i * 8, 8), 8)
            row = pl.ds((nblk - 1 - i) if reverse else i, 1)
            pr, pi = pw_ref[row, re], pw_ref[row, im]
            if reverse:
                pi = -pi
            buf[r, re] += pr * sr - pi * si
            buf[r, im] += pr * si + pi * sr
            return carry

        lax.fori_loop(0, nblk, step, 0)


STATE_BLOCKS = 2 * N_STATE // LANES
CH_BLOCKS = SSM_WIDTH // LANES


def _state_block(b):
    pair = b % (N_STATE // LANES)
    k = (pair * 2 * SSM_GROUP_CH) // LANES
    return slice(b * LANES, (b + 1) * LANES), slice(k * LANES, (k + 1) * LANES)


def _channel_block(c):
    w = N_STATE // CH_BLOCKS
    return slice(c * LANES, (c + 1) * LANES), slice(c * w, (c + 1) * w), slice(N_STATE + c * w, N_STATE + (c + 1) * w)


def _to_states(vb, w_ref, buf, nt):
    for b in range(STATE_BLOCKS):
        lanes, ch = _state_block(b)
        buf[:, lanes] = _dot_nt(vb[:, ch], w_ref[lanes, ch]) if nt else _dot(vb[:, ch], w_ref[ch, lanes])


def _to_channels(buf, w_ref, nt):
    outs = []
    for c in range(CH_BLOCKS):
        ch, re, im = _channel_block(c)
        xr, xi = buf[:, re].astype(BF16), buf[:, im].astype(BF16)
        if nt:
            outs.append(_dot_nt(xr, w_ref[ch, re]) + _dot_nt(xi, w_ref[ch, im]))
        else:
            outs.append(_dot(xr, w_ref[re, ch]) + _dot(xi, w_ref[im, ch]))
    return jnp.concatenate(outs, axis=-1)


def _ssm_fwd(u, bblk, cblk, lam, d_row, w_glu, b_glu, w_o_ssm, xch):
    n = u.shape[0]
    t = min(SCAN_T, n)
    perm = _perm_matrix(t)

    def body(u_ref, p_ref, bblk_ref, cblk_ref, lam_ref, d_ref, wg_ref, bg_ref, wo_ref, y_ref, ys_ref, st_ref,
             buf, pw_ref, carry_ref, s_ref):
        @pl.when(pl.program_id(0) == 0)
        def _():
            carry_ref[...] = jnp.zeros_like(carry_ref)
            _power_table(lam_ref, pw_ref, t // SUBCHUNKS)

        st_ref[0] = carry_ref[...]
        u_t = u_ref[...]
        p = p_ref[...]
        ub = _dot(p, u_t.astype(BF16)).astype(BF16)
        _to_states(ub, bblk_ref, buf, False)
        _run_scan(buf, lam_ref, t, False)
        _run_carries(buf, pw_ref, carry_ref, s_ref, t, False)
        _run_fix(buf, pw_ref, s_ref, t, False)
        y = d_ref[...] * u_t + _unpermute(p, _to_channels(buf, cblk_ref, False))
        y_ref[...] = y
        z, _ = _gelu(y)
        s = _sigmoid(_dot(z.astype(BF16), wg_ref[...]) + bg_ref[...])
        zgb = (z * s).astype(BF16)
        for j in range(N_DEV):
            ys_ref[:, j * OUT_SHARD:(j + 1) * OUT_SHARD] = _dot(zgb, wo_ref[j])

    consts = [perm, bblk, cblk, lam, d_row, w_glu, b_glu, w_o_ssm]
    return _call(
        body, "ssm_fwd", (n // t,), [u] + consts, [_rows(u, t)] + [_const(a) for a in consts],
        [_sds((n, SSM_WIDTH), F32), _sds((n, D_MODEL), F32), _sds((n // t, 8, 2 * N_STATE), F32)],
        [pl.BlockSpec((t, SSM_WIDTH), lambda i: (i, 0)), pl.BlockSpec((t, D_MODEL), lambda i: (i, 0)),
         pl.BlockSpec((1, 8, 2 * N_STATE), lambda i: (i, 0, 0))],
        scratch=[pltpu.VMEM((t, 2 * N_STATE), F32), pltpu.VMEM((t // SUBCHUNKS, 2 * N_STATE), F32),
                 pltpu.VMEM((8, 2 * N_STATE), F32), pltpu.VMEM((8, 2 * N_STATE), F32)],
        xch=xch)


def _head_norm_rope(slab, gain, cos_t, sin_t):
    xn, inv = _rms(slab, gain, QK_HEAD)
    lo, hi = xn[:, 0:128], xn[:, 128:256]
    return jnp.concatenate([lo, hi * cos_t + _rope_rot(hi) * sin_t], axis=-1), inv


def _head_norm_rope_bwd(g, slab, gain, inv, cos_t, sin_t):
    g_lo, g_hi = g[:, 0:128], g[:, 128:256]
    g_n = jnp.concatenate([g_lo, g_hi * cos_t + _rope_rot_t(g_hi * sin_t)], axis=-1)
    return _rms_bwd(g_n, slab, gain, inv, QK_HEAD)


def _qkv_prep(ql, kvl, q_a_norm, kv_a_norm, wq, wkv, gq, gk, cos_t, sin_t):
    n = ql.shape[0]
    tm = ROW_T

    def body(ql_ref, kvl_ref, cos_ref, sin_ref, qa_ref, ka_ref, wq_ref, wkv_ref, gq_ref, gk_ref,
             q_ref, k_ref, v_ref, kt_ref, vt_ref):
        cos_t, sin_t = cos_ref[...], sin_ref[...]
        qa, _ = _rms(ql_ref[...], qa_ref[...], Q_LORA)
        qab = qa.astype(BF16)
        kvl_t = kvl_ref[...]
        ca, _ = _rms(kvl_t[:, 0:KV_LORA], ka_ref[...], KV_LORA)
        cab = ca.astype(BF16)
        kpe = kvl_t[:, KV_LORA:KV_LAT_PAD]
        q_pre = _dot(qab, wq_ref[...])
        kv_pre = _dot(cab, wkv_ref[...])
        for h in range(N_HEADS):
            qh, _ = _head_norm_rope(q_pre[:, h * QK_PAD:(h + 1) * QK_PAD], gq_ref[...], cos_t, sin_t)
            q_ref[h] = (qh * ATT_SCALE).astype(BF16)
            kv_h = kv_pre[:, h * QK_PAD:(h + 1) * QK_PAD]
            kh, _ = _head_norm_rope(jnp.concatenate([kv_h[:, 0:QK_NOPE], kpe], axis=-1), gk_ref[...], cos_t, sin_t)
            k_ref[h] = kh.astype(BF16)
            kt_ref[h] = kh.T.astype(BF16)
            vh = kv_h[:, QK_NOPE:]
            v_ref[h] = vh.astype(BF16)
            vt_ref[h] = vh.T.astype(BF16)

    row_ins, consts = [ql, kvl, cos_t, sin_t], [q_a_norm, kv_a_norm, wq, wkv, gq, gk]
    outs = [_sds((N_HEADS, n, QK_PAD), BF16), _sds((N_HEADS, n, QK_PAD), BF16), _sds((N_HEADS, n, V_HEAD), BF16),
            _sds((N_HEADS, QK_PAD, n), BF16), _sds((N_HEADS, V_HEAD, n), BF16)]
    out_specs = [_rows(o, tm) for o in outs[:3]] + [
        pl.BlockSpec((N_HEADS, QK_PAD, tm), lambda i: (0, 0, i)), pl.BlockSpec((N_HEADS, V_HEAD, tm), lambda i: (0, 0, i))]
    return _call(body, "qkv_prep", (n // tm,), row_ins + consts,
                 [_rows(a, tm) for a in row_ins] + [_const(a) for a in consts], outs, out_specs)


def _causal_mask_t(st, t):
    key = lax.broadcasted_iota(jnp.int32, (t, t), 0)
    qry = lax.broadcasted_iota(jnp.int32, (t, t), 1)
    return jnp.where(key <= qry, st, -jnp.inf)


def _attn_fwd(q, k, vt, xch):
    n = q.shape[1]
    t = min(ATT_T, n)

    hp = ATT_HEADS

    def body(q_ref, k_ref, vt_ref, o_ref, lse_ref, ot_ref):
        i = pl.program_id(1)
        qts = [q_ref[g] for g in range(hp)]

        def kv_tile(j, carry, diag):
            ts = t // ATT_SUB
            sts = []
            for g in range(hp):
                for a in range(ATT_SUB):
                    r0 = pl.multiple_of(j * t + a * ts, ts)
                    st = _dot_nt(k_ref[g, pl.ds(r0, ts), :], qts[g])
                    if diag:
                        key = lax.broadcasted_iota(jnp.int32, (ts, t), 0) + a * ts
                        qry = lax.broadcasted_iota(jnp.int32, (ts, t), 1)
                        st = jnp.where(key <= qry, st, -jnp.inf)
                    sts.append(st)
            out = []
            for g in range(hp):
                m, l, acc = carry[g]
                for a in range(ATT_SUB):
                    st = sts[g * ATT_SUB + a]
                    r0 = pl.multiple_of(j * t + a * ts, ts)
                    m_new = jnp.maximum(m, jnp.max(st, 0, keepdims=True))
                    alpha = jnp.exp(m - m_new)
                    pt = jnp.exp(st - m_new)
                    l = alpha * l + jnp.sum(pt, 0, keepdims=True)
                    acc = alpha * acc + _dot(vt_ref[g, :, pl.ds(r0, ts)], pt.astype(BF16))
                    m = m_new
                out.append((m, l, acc))
            return tuple(out)

        one = (jnp.full((1, t), -jnp.inf, F32), jnp.zeros((1, t), F32), jnp.zeros((V_HEAD, t), F32))
        carry = lax.fori_loop(0, i, functools.partial(kv_tile, diag=False), (one,) * hp)
        for g, (m, l, acc) in enumerate(kv_tile(i, carry, True)):
            out_t = acc / l
            o_ref[:, g * V_HEAD:(g + 1) * V_HEAD] = out_t.T
            ot_ref[g * V_HEAD:(g + 1) * V_HEAD, :] = out_t.astype(BF16)
            lse_ref[g] = m + jnp.log(l)

    return _call(
        body, "attn_fwd", (N_HEADS // hp, n // t), [q, k, vt],
        [pl.BlockSpec((hp, t, QK_PAD), lambda h, i: (h, i, 0)), pl.BlockSpec((hp, n, QK_PAD), lambda h, i: (h, 0, 0)),
         pl.BlockSpec((hp, V_HEAD, n), lambda h, i: (h, 0, 0))],
        [_sds((n, N_HEADS * V_HEAD), F32), _sds((N_HEADS, 1, n), F32), _sds((N_HEADS * V_HEAD, n), BF16)],
        [pl.BlockSpec((t, hp * V_HEAD), lambda h, i: (i, h)), pl.BlockSpec((hp, 1, t), lambda h, i: (h, 0, i)),
         pl.BlockSpec((hp * V_HEAD, t), lambda h, i: (h, i))],
        xch=xch)


def _merge(attn, gs, gm, y_ssm, x, w_o_mla, w_out):
    n = x.shape[0]

    def body(at_ref, gs_ref, gm_ref, ys_ref, x_ref, wo_ref, wout_ref, h_ref, ym_ref, mxt_ref):
        y_mla = _dot(at_ref[...].astype(BF16), wo_ref[...])
        ym_ref[...] = y_mla
        mixed = _sigmoid(gs_ref[...]) * ys_ref[...] + _sigmoid(gm_ref[...]) * y_mla
        mxt_ref[...] = mixed.T.astype(BF16)
        h_ref[...] = x_ref[...] + _dot(mixed.astype(BF16), wout_ref[...])

    outs = [((n, D_MODEL), F32), ((n, D_MODEL), F32)]
    return _row_call(body, "merge", n, MM_T, [attn, gs, gm, y_ssm, x], [w_o_mla, w_out], outs,
                     col_outs=[((D_MODEL, n), BF16)])


def _mlp_fwd_loss(h, target, norm_mlp, w_up, w_down):
    n = h.shape[0]

    def body(h_ref, t_ref, g_ref, wu_ref, wd_ref, hn_ref, do_ref, hnt_ref, loss_ref):
        h_t = h_ref[...]
        hn, _ = _rms(h_t, g_ref[...], D_MODEL)
        hb = hn.astype(BF16)
        hn_ref[...] = hb
        hnt_ref[...] = hn.T.astype(BF16)
        out = h_t
        for j in range(N_DEV):
            a = jnp.maximum(_dot(hb, wu_ref[j]), 0.0)
            out += _dot((a * a).astype(BF16), wd_ref[j])
        err = out - t_ref[...]
        do_ref[...] = err * (1.0 / D_MODEL)
        _acc(loss_ref, jnp.broadcast_to(jnp.sum(err * err) * (0.5 / D_MODEL), loss_ref.shape))

    outs = [((n, D_MODEL), BF16), ((n, D_MODEL), F32)]
    return _row_call(body, "mlp_fwd_loss", n, MM_T, [h, target], [norm_mlp, w_up, w_down], outs, [((8, 128), F32)],
                     col_outs=[((D_MODEL, n), BF16)])


def _mlp_bwd(dout, hn, h, norm_mlp, w_up, w_down):
    n = h.shape[0]

    def body(do_ref, hn_ref, h_ref, g_ref, wu_ref, wd_ref, da_ref, dh_ref, dob_ref, hidt_ref, dg_ref):
        dout_t = do_ref[...]
        doutb = dout_t.astype(BF16)
        dob_ref[...] = doutb
        hb = hn_ref[...]
        dhn = jnp.zeros_like(dout_t)
        for j in range(N_DEV):
            cols = slice(j * FF_SHARD, (j + 1) * FF_SHARD)
            a = jnp.maximum(_dot(hb, wu_ref[j]), 0.0)
            hidt_ref[cols, :] = (a * a).T.astype(BF16)
            da = (_dot_nt(doutb, wd_ref[j]) * (2.0 * a)).astype(BF16)
            da_ref[:, cols] = da
            dhn += _dot_nt(da, wu_ref[j])
        h_t = h_ref[...]
        inv = lax.rsqrt(jnp.sum(h_t * h_t, -1, keepdims=True) * (1.0 / D_MODEL) + EPS)
        dx, dg = _rms_bwd(dhn, h_t, g_ref[...], inv, D_MODEL)
        dh_ref[...] = dout_t + dx
        _acc(dg_ref, jnp.sum(dg, 0, keepdims=True))

    outs = [((n, D_FF), BF16), ((n, D_MODEL), F32), ((n, D_MODEL), BF16)]
    return _row_call(body, "mlp_bwd", n, MM_T, [dout, hn, h], [norm_mlp, w_up, w_down], outs, [((1, D_MODEL), F32)],
                     col_outs=[((D_FF, n), BF16)])


def _merge_bwd(dh, gs, gm, y_ssm, y_mla, attn, w_out, w_o_mla):
    n = dh.shape[0]

    def body(dh_ref, gs_ref, gm_ref, ys_ref, ym_ref, at_ref, wout_ref, wo_ref,
             dgs_ref, dgm_ref, dys_ref, dym_ref, dat_ref, delta_ref):
        dmix = _dot_nt(dh_ref[...].astype(BF16), wout_ref[...])
        sgs, sgm = _sigmoid(gs_ref[...]), _sigmoid(gm_ref[...])
        dgs_ref[...] = (dmix * ys_ref[...] * sgs * (1.0 - sgs)).astype(BF16)
        dgm_ref[...] = (dmix * ym_ref[...] * sgm * (1.0 - sgm)).astype(BF16)
        dys_ref[...] = (dmix * sgs).astype(BF16)
        dym = (dmix * sgm).astype(BF16)
        dym_ref[...] = dym
        dattn = _dot_nt(dym, wo_ref[...])
        dat_ref[...] = dattn.astype(BF16)
        prod = dattn * at_ref[...]
        ones = jnp.ones((8, V_HEAD), F32)
        for h in range(N_HEADS):
            delta_ref[h] = _dot_exact(ones, prod[:, h * V_HEAD:(h + 1) * V_HEAD], ((1,), (1,)))[0:1, :]

    outs = [((n, D_MODEL), BF16)] * 5
    return _row_call(body, "merge_bwd", n, MM_T, [dh, gs, gm, y_ssm, y_mla, attn], [w_out, w_o_mla], outs,
                     col_outs=[((N_HEADS, 1, n), F32)])


def _attn_bwd(q, k, kt, v, lse, delta, dout, xch):
    n = q.shape[1]
    t = min(ATT_T, n)
    nt = n // t
    hp = ATT_BWD_HEADS

    def body(q_ref, k_ref, kt_ref, v_ref, lse_ref, delta_ref, do_ref, dq_ref, dk_ref, dv_ref, dqt_ref):
        j = pl.program_id(1)

        @pl.when(j == 0)
        def _():
            dqt_ref[...] = jnp.zeros_like(dqt_ref)

        def q_tile(i, carry, diag):
            r0 = pl.multiple_of(i * t, t)
            rows = pl.ds(r0, t)
            qts = [q_ref[g, rows, :] for g in range(hp)]
            sts = [_dot_nt(k_ref[g], qts[g]) for g in range(hp)]
            out = []
            for g in range(hp):
                dk, dv = carry[g]
                st = _causal_mask_t(sts[g], t) if diag else sts[g]
                pt = jnp.exp(st - lse_ref[g, :, rows])
                dob = do_ref[rows, g * V_HEAD:(g + 1) * V_HEAD]
                dv = dv + _dot(pt.astype(BF16), dob)
                dst = (pt * (_dot_nt(v_ref[g], dob) - delta_ref[g, :, rows])).astype(BF16)
                dk = dk + _dot(dst, qts[g])
                dqt_ref[g, :, rows] += _dot(kt_ref[g], dst)
                out.append((dk, dv))
            return tuple(out)

        zero = (jnp.zeros((t, QK_PAD), F32), jnp.zeros((t, V_HEAD), F32))
        carry = q_tile(j, (zero,) * hp, True)
        carry = lax.fori_loop(j + 1, nt, functools.partial(q_tile, diag=False), carry)
        for g, (dk, dv) in enumerate(carry):
            dk_ref[g] = dk
            dv_ref[g] = dv

        @pl.when(j == nt - 1)
        def _():
            for g in range(hp):
                for c in range(0, n, t):
                    dq_ref[g, c:c + t, :] = dqt_ref[g, :, c:c + t].T

    return _call(
        body, "attn_bwd", (N_HEADS // hp, nt), [q, k, kt, v, lse, delta, dout],
        [pl.BlockSpec((hp, n, QK_PAD), lambda h, j: (h, 0, 0)), pl.BlockSpec((hp, t, QK_PAD), lambda h, j: (h, j, 0)),
         pl.BlockSpec((hp, QK_PAD, t), lambda h, j: (h, 0, j)), pl.BlockSpec((hp, t, V_HEAD), lambda h, j: (h, j, 0)),
         pl.BlockSpec((hp, 1, n), lambda h, j: (h, 0, 0)), pl.BlockSpec((hp, 1, n), lambda h, j: (h, 0, 0)),
         pl.BlockSpec((n, hp * V_HEAD), lambda h, j: (0, h))],
        [_sds((N_HEADS, n, QK_PAD), F32), _sds((N_HEADS, n, QK_PAD), F32), _sds((N_HEADS, n, V_HEAD), F32)],
        [pl.BlockSpec((hp, n, QK_PAD), lambda h, j: (h, 0, 0)), pl.BlockSpec((hp, t, QK_PAD), lambda h, j: (h, j, 0)),
         pl.BlockSpec((hp, t, V_HEAD), lambda h, j: (h, j, 0))],
        scratch=[pltpu.VMEM((hp, QK_PAD, n), F32)],
        xch=xch)


def _qkv_prep_bwd(ql, kvl, dq, dk, dv, q_a_norm, kv_a_norm, wq, wkv, gq, gk, cos_t, sin_t, xch):
    n = ql.shape[0]

    def body(ql_ref, kvl_ref, cos_ref, sin_ref, dq_ref, dk_ref, dv_ref, qa_ref, ka_ref, wq_ref, wkv_ref, gq_ref, gk_ref,
             dql_ref, dkvl_ref, dqa_ref, dka_ref, dgq_ref, dgk_ref, dwq_ref, dwkv_ref, dqp_ref, dkvp_ref):
        cos_t, sin_t = cos_ref[...], sin_ref[...]
        ql_t = ql_ref[...]
        qa, inv_qa = _rms(ql_t, qa_ref[...], Q_LORA)
        qab = qa.astype(BF16)
        kvl_t = kvl_ref[...]
        ckv = kvl_t[:, 0:KV_LORA]
        ca, inv_ca = _rms(ckv, ka_ref[...], KV_LORA)
        cab = ca.astype(BF16)
        kpe = kvl_t[:, KV_LORA:KV_LAT_PAD]
        dgq = jnp.zeros((1, QK_PAD), F32)
        dgk = jnp.zeros((1, QK_PAD), F32)
        dkpe = jnp.zeros_like(kpe)
        q_pre = _dot(qab, wq_ref[...])
        kv_pre = _dot(cab, wkv_ref[...])
        for h in range(N_HEADS):
            head = slice(h * QK_PAD, (h + 1) * QK_PAD)
            q_slab = q_pre[:, head]
            inv = lax.rsqrt(jnp.sum(q_slab * q_slab, -1, keepdims=True) * (1.0 / QK_HEAD) + EPS)
            d_slab, dg = _head_norm_rope_bwd(dq_ref[h] * ATT_SCALE, q_slab, gq_ref[...], inv, cos_t, sin_t)
            dqp_ref[:, head] = d_slab.astype(BF16)
            dgq += jnp.sum(dg, 0, keepdims=True)
            k_slab = jnp.concatenate([kv_pre[:, h * QK_PAD:h * QK_PAD + QK_NOPE], kpe], axis=-1)
            inv = lax.rsqrt(jnp.sum(k_slab * k_slab, -1, keepdims=True) * (1.0 / QK_HEAD) + EPS)
            d_slab, dg = _head_norm_rope_bwd(dk_ref[h], k_slab, gk_ref[...], inv, cos_t, sin_t)
            dkvp_ref[:, head] = jnp.concatenate([d_slab[:, 0:QK_NOPE], dv_ref[h]], axis=-1).astype(BF16)
            dkpe += d_slab[:, QK_NOPE:QK_PAD]
            dgk += jnp.sum(dg, 0, keepdims=True)
        dqa = _dot_nt(dqp_ref[...], wq_ref[...])
        dx, dg = _rms_bwd(dqa, ql_t, qa_ref[...], inv_qa, Q_LORA)
        dql_ref[...] = dx.astype(BF16)
        _acc(dqa_ref, jnp.sum(dg, 0, keepdims=True))
        dca = _dot_nt(dkvp_ref[...], wkv_ref[...])
        dx, dg = _rms_bwd(dca, ckv, ka_ref[...], inv_ca, KV_LORA)
        dkvl_ref[:, 0:KV_LORA] = dx.astype(BF16)
        dkvl_ref[:, KV_LORA:KV_LAT_PAD] = dkpe.astype(BF16)
        _acc(dka_ref, jnp.sum(dg, 0, keepdims=True))
        _acc(dgq_ref, dgq)
        _acc(dgk_ref, dgk)
        _acc(dwq_ref, _dot_tn(qab, dqp_ref[...]))
        _acc(dwkv_ref, _dot_tn(cab, dkvp_ref[...]))

    wide = N_HEADS * QK_PAD
    row_outs = [((n, Q_LORA), BF16), ((n, KV_LAT_PAD), BF16)]
    acc_outs = [((1, Q_LORA), F32), ((1, KV_LORA), F32), ((1, QK_PAD), F32), ((1, QK_PAD), F32),
                ((Q_LORA, wide), F32), ((KV_LORA, wide), F32)]
    return _row_call(body, "qkv_prep_bwd", n, ROW_T, [ql, kvl, cos_t, sin_t, dq, dk, dv],
                     [q_a_norm, kv_a_norm, wq, wkv, gq, gk], row_outs, acc_outs, xch=xch,
                     scratch=[pltpu.VMEM((ROW_T, wide), BF16), pltpu.VMEM((ROW_T, wide), BF16)])


def _glu_bwd(dy_ssm, y, w_glu, b_glu, w_o_ssm):
    n = y.shape[0]

    def body(dys_ref, y_ref, wg_ref, bg_ref, wo_ref, dy_ref, db_ref, dwg_ref, dwo_ref):
        y_t = y_ref[...]
        z, th = _gelu(y_t)
        zb = z.astype(BF16)
        s = _sigmoid(_dot(zb, wg_ref[...]) + bg_ref[...])
        dys = dys_ref[...]
        dzg = jnp.zeros_like(y_t)
        for j in range(N_DEV):
            dzg += _dot_nt(dys[:, j * OUT_SHARD:(j + 1) * OUT_SHARD], wo_ref[j])
        dt = dzg * z * s * (1.0 - s)
        dtb = dt.astype(BF16)
        dz = dzg * s + _dot_nt(dtb, wg_ref[...])
        dy_ref[...] = dz * _gelu_grad(y_t, th)
        _acc(db_ref, jnp.sum(dt, 0, keepdims=True))
        _acc(dwg_ref, _dot_tn(zb, dtb))
        _acc(dwo_ref, _dot_tn((z * s).astype(BF16), dys))

    acc_outs = [((1, SSM_WIDTH), F32), ((SSM_WIDTH, SSM_WIDTH), F32), ((SSM_WIDTH, D_MODEL), F32)]
    return _row_call(body, "glu_bwd", n, ROW_T, [dy_ssm, y], [w_glu, b_glu, w_o_ssm], [((n, SSM_WIDTH), F32)], acc_outs)


def _ssm_bwd(u, dy, st, bblk, cblk, lam, d_row, xch):
    n = u.shape[0]
    t = min(SCAN_T, n)
    nc = n // t
    kb = 512
    perm = _perm_matrix(t)

    def body(u_ref, dy_ref, st_ref, p_ref, bblk_ref, cblk_ref, lam_ref, d_ref,
             du_ref, dlam_ref, dd_ref, db_ref, dct_ref,
             buf_x, buf_a, pw_ref, carry_ref, xcarry_ref, sx_ref, sa_ref, db_acc, dct_acc):
        @pl.when(pl.program_id(0) == 0)
        def _():
            carry_ref[...] = jnp.zeros_like(carry_ref)
            db_acc[...] = jnp.zeros_like(db_acc)
            dct_acc[...] = jnp.zeros_like(dct_acc)
            _power_table(lam_ref, pw_ref, t // SUBCHUNKS)

        u_t = u_ref[...]
        dy_t = dy_ref[...]
        p = p_ref[...]
        ub = _dot(p, u_t.astype(BF16)).astype(BF16)
        dyb = _dot(p, dy_t.astype(BF16)).astype(BF16)
        _to_states(ub, bblk_ref, buf_x, False)
        xcarry_ref[...] = st_ref[0]
        _run_scan(buf_x, lam_ref, t, False)
        _run_carries(buf_x, pw_ref, xcarry_ref, sx_ref, t, False)
        _run_fix(buf_x, pw_ref, sx_ref, t, False)
        _to_states(dyb, cblk_ref, buf_a, True)
        _run_scan(buf_a, lam_ref, t, True)
        _run_carries(buf_a, pw_ref, carry_ref, sa_ref, t, True)
        _run_fix(buf_a, pw_ref, sa_ref, t, True)
        du_ref[...] = (d_ref[...] * dy_t + _unpermute(p, _to_channels(buf_a, bblk_ref, True))).astype(BF16)
        for b in range(STATE_BLOCKS):
            lanes, ch = _state_block(b)
            db_acc[ch, lanes] += _dot_tn(ub[:, ch], buf_a[:, lanes].astype(BF16))
            dct_acc[ch, lanes] += _dot_tn(dyb[:, ch], buf_x[:, lanes].astype(BF16))
        for c in range(0, N_STATE, kb):
            re, im = pl.ds(c, kb), pl.ds(N_STATE + c, kb)
            xr, xi = buf_x[pl.ds(0, t - 8), re], buf_x[pl.ds(0, t - 8), im]
            ar, ai = buf_a[pl.ds(8, t - 8), re], buf_a[pl.ds(8, t - 8), im]
            x0r, x0i = sx_ref[:, re], sx_ref[:, im]
            a0r, a0i = buf_a[0:8, re], buf_a[0:8, im]
            dlam_part_re = (jnp.sum(ar * xr + ai * xi, 0, keepdims=True)
                            + jnp.sum(a0r * x0r + a0i * x0i, 0, keepdims=True))
            dlam_part_im = (jnp.sum(ai * xr - ar * xi, 0, keepdims=True)
                            + jnp.sum(a0i * x0r - a0r * x0i, 0, keepdims=True))

            @pl.when(pl.program_id(0) == 0)
            def _(c=c):
                dlam_ref[0:1, c:c + kb] = jnp.zeros((1, kb), F32)
                dlam_ref[1:2, c:c + kb] = jnp.zeros((1, kb), F32)

            dlam_ref[0:1, c:c + kb] += dlam_part_re
            dlam_ref[1:2, c:c + kb] += dlam_part_im
        _acc(dd_ref, jnp.sum(dy_t * u_t, 0, keepdims=True))

        @pl.when(pl.program_id(0) == nc - 1)
        def _():
            pltpu.sync_copy(db_acc, db_ref)
            pltpu.sync_copy(dct_acc, dct_ref)

    rev = lambda i: (nc - 1 - i, 0)
    consts = [perm, bblk, cblk, lam, d_row]
    wide = (SSM_WIDTH, 2 * N_STATE)
    return _call(
        body, "ssm_bwd", (nc,), [u, dy, st] + consts,
        [pl.BlockSpec((t, SSM_WIDTH), rev), pl.BlockSpec((t, SSM_WIDTH), rev),
         pl.BlockSpec((1, 8, 2 * N_STATE), lambda i: (nc - 1 - i, 0, 0))] + [_const(a) for a in consts],
        [_sds((n, SSM_WIDTH), BF16), _sds((2, N_STATE), F32), _sds((1, SSM_WIDTH), F32), _sds(wide, F32), _sds(wide, F32)],
        [pl.BlockSpec((t, SSM_WIDTH), rev), pl.BlockSpec((2, N_STATE), lambda i: (0, 0)),
         pl.BlockSpec((1, SSM_WIDTH), lambda i: (0, 0)), ANY, ANY],
        scratch=[pltpu.VMEM((t, 2 * N_STATE), F32)] * 2 + [pltpu.VMEM((t // SUBCHUNKS, 2 * N_STATE), F32)]
        + [pltpu.VMEM((8, 2 * N_STATE), F32)] * 4 + [pltpu.VMEM(wide, F32)] * 2,
        xch=xch)


def _in_proj_bwd(pieces, dh, x, xn_t, norm_mix, w_in_pad, xch):
    n = x.shape[0]
    tm = min(MM_T, n)
    nt = n // tm

    def body(du_ref, dql_ref, dkvl_ref, dgs_ref, dgm_ref, dh_ref, x_ref, xnt_ref, g_ref, w_ref,
             dx_ref, dg_ref, dw_ref, acc_ref):
        @pl.when(pl.program_id(0) == 0)
        def _():
            acc_ref[...] = jnp.zeros_like(acc_ref)

        xnt = xnt_ref[...]
        dxn = jnp.zeros((tm, D_MODEL), F32)
        for ref, (a, b) in zip((du_ref, dql_ref, dkvl_ref, dgs_ref, dgm_ref), IN_SEGS):
            piece = ref[...]
            dxn += _dot_nt(piece, w_ref[:, a:b])
            acc_ref[:, a:b] += _dot(xnt, piece)
        x_t = x_ref[...]
        inv = lax.rsqrt(jnp.sum(x_t * x_t, -1, keepdims=True) * (1.0 / D_MODEL) + EPS)
        dx, dg = _rms_bwd(dxn, x_t, g_ref[...], inv, D_MODEL)
        dx_ref[...] = dh_ref[...] + dx
        _acc(dg_ref, jnp.sum(dg, 0, keepdims=True))

        @pl.when(pl.program_id(0) == nt - 1)
        def _():
            pltpu.sync_copy(acc_ref, dw_ref)

    row_ins, consts = list(pieces) + [dh, x], [norm_mix, w_in_pad]
    in_specs = ([_rows(a, tm) for a in row_ins] + [pl.BlockSpec((D_MODEL, tm), lambda i: (0, i))]
                + [_const(a) for a in consts])
    return _call(
        body, "in_proj_bwd", (nt,), row_ins + [xn_t] + consts, in_specs,
        [_sds((n, D_MODEL), F32), _sds((1, D_MODEL), F32), _sds((D_MODEL, D_IN_PAD), F32)],
        [pl.BlockSpec((tm, D_MODEL), lambda i: (i, 0)), pl.BlockSpec((1, D_MODEL), lambda i: (0, 0)), ANY],
        scratch=[pltpu.VMEM((D_MODEL, D_IN_PAD), F32)], xch=xch)


def _swap_minor(a):
    g, r, c = a.shape[1:]
    return jnp.transpose(a[0], (0, 2, 1)).reshape(g * c, r)


def _pad_in(w):
    return jnp.concatenate([w[:, :KV_END], jnp.zeros((w.shape[0], D_IN_PAD - D_IN), w.dtype), w[:, KV_END:]], axis=1)


def _unpad_in(w):
    return jnp.concatenate([w[:, :KV_END], w[:, KV_END + D_IN_PAD - D_IN:]], axis=1)


def _pad_gain(g):
    return jnp.pad(g, ((0, 0), (0, QK_PAD - QK_HEAD)))


def _place():
    x, y, c = lax.axis_index("x"), lax.axis_index("y"), lax.axis_index("c")
    chips = [(x, y), (1 - x, y), (x, 1 - y), (1 - x, 1 - y)]
    return x, y, c, chips


def _all_gather(block, name, work=None, work_ins=(), work_outs=()):
    rows, lanes = block.shape
    n_wi, n_wo = len(work_ins), len(work_outs)

    def body(x_ref, *refs):
        out_ref = refs[n_wi]
        send_sems, recv_sems, local_sem = refs[n_wi + 1 + n_wo:]
        x, y, c, chips = _place()
        me, sibling = (x, y, c), (x, y, 1 - c)

        def slot(px, py, pc):
            return out_ref.at[4 * px + 2 * py + pc]

        def copy(k, blk, to, src=None):
            return pltpu.make_async_remote_copy(
                src_ref=slot(*blk) if src is None else src, dst_ref=slot(*blk),
                send_sem=send_sems.at[k], recv_sem=recv_sems.at[k], device_id=to, device_id_type=MESH)

        mine = pltpu.make_async_copy(x_ref, slot(*me), local_sem)
        mine.start()
        first = [copy(0, me, sibling, src=x_ref)]
        first += [copy(1 + j, me, (*chip, c), src=x_ref) for j, chip in enumerate(chips[1:])]
        for cp in first:
            cp.start()
        if work is not None:
            work(*refs[:n_wi], *refs[n_wi + 1:n_wi + 1 + n_wo])
        passed = [copy(4 + j, (*chip, c), sibling) for j, chip in enumerate(chips[1:])]
        for j, chip in enumerate(chips[1:]):
            copy(1 + j, (*chip, c), me).wait_recv()
            passed[j].start()
        copy(0, sibling, me).wait_recv()
        for j, chip in enumerate(chips[1:]):
            copy(4 + j, (*chip, 1 - c), me).wait_recv()
        for cp in first + passed:
            cp.wait_send()
        mine.wait()

    vmem = pl.BlockSpec(memory_space=pltpu.VMEM)
    outs = pl.pallas_call(
        body,
        name=name,
        in_specs=[ANY] + [vmem] * n_wi,
        out_specs=[ANY] + [vmem] * n_wo,
        out_shape=[_sds((N_DEV, rows, lanes), block.dtype)] + list(work_outs),
        scratch_shapes=[pltpu.SemaphoreType.DMA((7,)), pltpu.SemaphoreType.DMA((7,)), pltpu.SemaphoreType.DMA],
        compiler_params=_params(),
    )(block, *work_ins)
    return outs if n_wo else outs[0]


def _reduce_scatter(parts, gather, name):
    _, rows, lanes = parts.shape

    def body(p_ref, g_ref, out_ref, ga_ref, own, land_a, send_b, land_b, sa, ra, sb, rb, lo, *g_sems):
        x, y, c, chips = _place()
        sibling = (x, y, 1 - c)
        _xchg_start([False], [g_ref], [ga_ref], *g_sems)

        def blk(chip, core):
            return p_ref.at[4 * chip[0] + 2 * chip[1] + core]

        to_sib = [pltpu.make_async_remote_copy(
            src_ref=blk(chips[k], 1 - c), dst_ref=land_a.at[k], send_sem=sa.at[k], recv_sem=ra.at[k],
            device_id=sibling, device_id_type=MESH) for k in range(4)]
        for cp in to_sib:
            cp.start()
        loads = [pltpu.make_async_copy(blk(chips[k], c), own.at[k], lo.at[k]) for k in range(4)]
        for cp in loads:
            cp.start()
        to_chip = [pltpu.make_async_remote_copy(
            src_ref=send_b.at[j], dst_ref=land_b.at[j], send_sem=sb.at[j], recv_sem=rb.at[j],
            device_id=(*chips[1 + j], c), device_id_type=MESH) for j in range(3)]
        for k in (1, 2, 3):
            to_sib[k].wait_recv()
            loads[k].wait()
            send_b[k - 1] = (own[k] + land_a[k]).astype(BF16)
            to_chip[k - 1].start()
        to_sib[0].wait_recv()
        loads[0].wait()
        acc = own[0] + land_a[0]
        for j in range(3):
            to_chip[j].wait_recv()
            acc = acc + land_b[j].astype(F32)
        out_ref[...] = acc
        for cp in to_sib + to_chip:
            cp.wait_send()
        _xchg_wait([False], [g_ref], [ga_ref], *g_sems)

    return pl.pallas_call(
        body,
        name=name,
        in_specs=[ANY, ANY],
        out_specs=[pl.BlockSpec(memory_space=pltpu.VMEM), ANY],
        out_shape=[_sds((rows, lanes), F32), _sds((N_DEV,) + gather.shape, gather.dtype)],
        scratch_shapes=[pltpu.VMEM((4, rows, lanes), F32), pltpu.VMEM((4, rows, lanes), F32),
                        pltpu.VMEM((3, rows, lanes), BF16), pltpu.VMEM((3, rows, lanes), BF16)]
        + [pltpu.SemaphoreType.DMA((4,))] * 2 + [pltpu.SemaphoreType.DMA((3,))] * 2 + [pltpu.SemaphoreType.DMA((4,))]
        + [pltpu.SemaphoreType.DMA((1,))] * 3,
        compiler_params=_params(),
    )(parts, gather)


def _adamw_math(w, g, m, v):
    m = ADAM_B1 * m + (1.0 - ADAM_B1) * g
    v = ADAM_B2 * v + (1.0 - ADAM_B2) * (g * g)
    m_hat = m / (1.0 - ADAM_B1 ** ADAM_STEP)
    v_hat = v / (1.0 - ADAM_B2 ** ADAM_STEP)
    delta = -ADAM_LR * (m_hat / (jnp.sqrt(v_hat) + ADAM_EPS) + ADAM_WD * w)
    return delta, m, v


def _row_tile(r):
    return max(t for t in range(8, min(r, 256) + 1, 8) if r % t == 0)


def _adamw(w, g, m, v, name):
    r, n = w.shape

    def body(w_ref, g_ref, m_ref, v_ref, d_ref, nm_ref, nv_ref):
        d_ref[...], nm_ref[...], nv_ref[...] = _adamw_math(w_ref[...], g_ref[...], m_ref[...], v_ref[...])

    return _row_call(body, name, r, _row_tile(r), [w, g, m, v], [], [((r, n), F32)] * 3)


def _adamw_sum(landed, w, m, v, name):
    r, n = w.shape

    def body(l_ref, w_ref, m_ref, v_ref, g_ref, d_ref, nm_ref, nv_ref):
        g = l_ref[0].astype(F32)
        for dev in range(1, N_DEV):
            g = g + l_ref[dev].astype(F32)
        g_ref[...] = g
        d_ref[...], nm_ref[...], nv_ref[...] = _adamw_math(w_ref[...], g, m_ref[...], v_ref[...])

    tm = max(t for t in range(16, min(r, 256) + 1, 16) if r % t == 0)
    return _row_call(body, name, r, tm, [landed, w, m, v], [], [((r, n), F32)] * 4)


def _adamw_small(first, rest, w, m, v, row_counts):
    n_rest = w.shape[0] - first.shape[1]

    def body(f_ref, r_ref, w_ref, m_ref, v_ref, loss_ref, *out_refs):
        gf, gr = f_ref[0], r_ref[0]
        for dev in range(1, N_DEV):
            gf, gr = gf + f_ref[dev], gr + r_ref[dev]
        loss_ref[...] = gr[n_rest:n_rest + 8]
        g = jnp.concatenate([gf, gr[0:n_rest]], axis=0)
        d, nm, nv = _adamw_math(w_ref[...], g, m_ref[...], v_ref[...])
        off = 0
        for p, rows in enumerate(row_counts):
            for k, val in enumerate((g, d, nm, nv)):
                out_refs[4 * p + k][...] = val[off:off + rows]
            off += rows

    outs = [_sds((8, LANES), F32)] + [_sds((rows, LANES), F32) for rows in row_counts for _ in range(4)]
    return pl.pallas_call(body, name="adamw_small", out_shape=outs, compiler_params=_params())(first, rest, w, m, v)


SMALL = ("norm_mix", "q_a_norm", "kv_a_norm", "q_norm", "k_norm", "ssm_a_re", "ssm_a_im", "ssm_log_dt", "ssm_b_re",
         "ssm_b_im", "ssm_c_re", "ssm_c_im", "ssm_d", "b_glu", "norm_mlp")
WEIGHT_ORDER = ("norm_mix", "w_in", "q_a_norm", "kv_a_norm", "w_q_b", "w_kv_b", "q_norm", "k_norm", "w_o_mla",
                "ssm_a_re", "ssm_a_im", "ssm_log_dt", "ssm_b_re", "ssm_b_im", "ssm_c_re", "ssm_c_im", "ssm_d", "w_glu",
                "b_glu", "w_o_ssm", "w_out", "norm_mlp", "w_up", "w_down")
IN_SHARD = D_IN // N_DEV


def _pack_small(vals, names=SMALL):
    parts = []
    for n in names:
        flat = vals[n].reshape(-1)
        size = -(-flat.shape[0] // (8 * LANES)) * 8 * LANES
        parts.append(jnp.pad(flat, (0, size - flat.shape[0])).reshape(-1, LANES))
    return jnp.concatenate(parts, axis=0)


def _small_rows(like):
    return [-(-like[n].size // (8 * LANES)) * 8 for n in SMALL]


def _step(x, pos_col, target, w, small):
    bf = {n: a.astype(BF16) for n, a in w.items()}
    gq, gk = _pad_gain(small["q_norm"]), _pad_gain(small["k_norm"])
    a_re = small["ssm_a_re"].reshape(1, N_STATE)
    a_im = small["ssm_a_im"].reshape(1, N_STATE)
    log_dt = jnp.repeat(small["ssm_log_dt"].reshape(SSM_GROUPS), SSM_STATE).reshape(1, N_STATE)
    bt_re, bt_im = _swap_minor(small["ssm_b_re"]), _swap_minor(small["ssm_b_im"])
    c2_re, c2_im = _swap_minor(small["ssm_c_re"]), _swap_minor(small["ssm_c_im"])
    d_row = small["ssm_d"].reshape(1, SSM_WIDTH)

    lam, bblk, cblk, w_in_all = _ssm_prep(a_re, a_im, log_dt, bt_re, bt_im, c2_re, c2_im, bf["w_in"])
    w_in_pad = _pad_in(jnp.transpose(w_in_all, (1, 0, 2)).reshape(D_MODEL, D_IN))
    cos_t, sin_t = _rope_tables(pos_col)
    wq_mine = jnp.pad(bf["w_q_b"], ((0, 0), (0, QK_PAD - QK_HEAD)))
    u, ql, kvl, gs, gm, xn_t, w_glu, w_o_ssm = _in_proj(
        x, small["norm_mix"], w_in_pad, xch=[(bf["w_glu"], False), (bf["w_o_ssm"], False)])
    w_glu = w_glu.reshape(SSM_WIDTH, SSM_WIDTH)
    y, y_ssm, st, wq, wkv, w_o_mla, w_out = _ssm_fwd(
        u, bblk, cblk, lam, d_row, w_glu, small["b_glu"], w_o_ssm,
        xch=[(wq_mine, False), (bf["w_kv_b"], False), (bf["w_o_mla"], False), (bf["w_out"], False)])
    w_o_mla, w_out = w_o_mla.reshape(D_MODEL, D_MODEL), w_out.reshape(D_MODEL, D_MODEL)
    wq = jnp.transpose(wq, (1, 0, 2)).reshape(Q_LORA, N_HEADS * QK_PAD)
    wkv = jnp.transpose(wkv, (1, 0, 2)).reshape(KV_LORA, N_HEADS * QK_PAD)
    q, k, v, kt, vt = _qkv_prep(ql, kvl, small["q_a_norm"], small["kv_a_norm"], wq, wkv, gq, gk, cos_t, sin_t)
    attn, lse, attn_t, w_up, w_down = _attn_fwd(q, k, vt, xch=[(bf["w_up"], False), (bf["w_down"], False)])
    h, y_mla, mixed_t = _merge(attn, gs, gm, y_ssm, x, w_o_mla, w_out)
    hn, dout, hn_t, loss = _mlp_fwd_loss(h, target, small["norm_mlp"], w_up, w_down)

    da, dh, dout_b, hid_t, d_norm_mlp = _mlp_bwd(dout, hn, h, small["norm_mlp"], w_up, w_down)
    p_w_down = _matmul_tn_shards(hid_t, dout_b, "dw_down", False, tm=1024, turned=True)
    p_w_up = _matmul_tn_shards(hn_t, da, "dw_up", True, turned=True)
    dgs, dgm, dy_ssm, dy_mla, dattn, delta = _merge_bwd(dh, gs, gm, y_ssm, y_mla, attn, w_out, w_o_mla)
    p_w_out = _matmul_tn_shards(mixed_t, dh, "dw_out", False, tm=1024, turned=True)
    p_w_o_mla = _matmul_tn_shards(attn_t, dy_mla, "dw_o_mla", False, turned=True)
    dq, dk, dv, l_w_up, l_w_down, l_w_out, l_w_o_mla = _attn_bwd(
        q, k, kt, v, lse, delta, dattn, xch=[(p_w_up, True), (p_w_down, True), (p_w_out, True), (p_w_o_mla, True)])
    dql, dkvl, d_q_a_norm, d_kv_a_norm, d_gq, d_gk, g_wq, g_wkv = _qkv_prep_bwd(
        ql, kvl, dq, dk, dv, small["q_a_norm"], small["kv_a_norm"], wq, wkv, gq, gk, cos_t, sin_t, xch=[])
    p_wq = jnp.transpose(g_wq.reshape(Q_LORA, N_HEADS, QK_PAD), (1, 0, 2)).astype(BF16)
    p_wkv = jnp.transpose(g_wkv.reshape(KV_LORA, N_HEADS, QK_PAD), (1, 0, 2)).astype(BF16)
    dy, d_b_glu, g_w_glu, g_w_o_ssm = _glu_bwd(dy_ssm, y, w_glu, small["b_glu"], w_o_ssm)
    p_w_o_ssm = jnp.transpose(g_w_o_ssm.reshape(SSM_WIDTH, N_DEV, OUT_SHARD), (1, 0, 2)).astype(BF16)
    p_w_glu = g_w_glu.reshape(N_DEV, SSM_WIDTH // N_DEV, SSM_WIDTH).astype(BF16)
    du, dlam, d_d, d_bblk, d_cblk_t, l_wq, l_wkv, l_w_glu, l_w_o_ssm = _ssm_bwd(
        u, dy, st, bblk, cblk, lam, d_row, xch=[(p_wq, True), (p_wkv, True), (p_w_glu, True), (p_w_o_ssm, True)])
    d_a_re, d_a_im, d_log_dt, d_bt_re, d_bt_im, d_c_re, d_c_im = _ssm_prep_bwd(
        a_re, a_im, log_dt, bt_re, bt_im, dlam, d_bblk, d_cblk_t)
    tr = lambda mat: jnp.transpose(mat.reshape(SSM_GROUPS, SSM_GROUP_CH, SSM_STATE), (0, 2, 1))
    g_small = {
        "q_a_norm": d_q_a_norm, "kv_a_norm": d_kv_a_norm, "q_norm": d_gq[:, :QK_HEAD], "k_norm": d_gk[:, :QK_HEAD],
        "ssm_a_re": d_a_re, "ssm_a_im": d_a_im, "ssm_log_dt": d_log_dt,
        "ssm_b_re": tr(d_bt_re), "ssm_b_im": tr(d_bt_im), "ssm_c_re": d_c_re, "ssm_c_im": d_c_im,
        "ssm_d": d_d, "b_glu": d_b_glu, "norm_mlp": d_norm_mlp,
    }
    rest = jnp.concatenate([_pack_small(g_small, SMALL[1:]), loss], axis=0)
    dx, d_norm_mix, g_w_in_pad, g_rest_all = _in_proj_bwd(
        (du, dql, dkvl, dgs, dgm), dh, x, xn_t, small["norm_mix"], w_in_pad, xch=[(rest, False)])
    parts = jnp.transpose(_unpad_in(g_w_in_pad).reshape(D_MODEL, N_DEV, IN_SHARD), (1, 0, 2))
    g_w_in_mine, g_first_all = _reduce_scatter(parts, _pack_small({SMALL[0]: d_norm_mix}, SMALL[:1]), "reduce_w_in")
    landed = {"w_q_b": l_wq[:, :, :QK_HEAD], "w_kv_b": l_wkv, "w_o_mla": l_w_o_mla, "w_glu": l_w_glu,
              "w_o_ssm": l_w_o_ssm, "w_out": l_w_out, "w_up": l_w_up, "w_down": l_w_down}
    return dx, landed, g_w_in_mine, g_first_all, g_rest_all


def kernel(x, positions, norm_mix, w_in, q_a_norm, kv_a_norm, w_q_b, w_kv_b, q_norm, k_norm, w_o_mla, ssm_a_re, ssm_a_im, ssm_log_dt, ssm_b_re, ssm_b_im, ssm_c_re, ssm_c_im, ssm_d, w_glu, b_glu, w_o_ssm, w_out, norm_mlp, w_up, w_down, loss_target, m_norm_mix, m_w_in, m_q_a_norm, m_kv_a_norm, m_w_q_b, m_w_kv_b, m_q_norm, m_k_norm, m_w_o_mla, m_ssm_a_re, m_ssm_a_im, m_ssm_log_dt, m_ssm_b_re, m_ssm_b_im, m_ssm_c_re, m_ssm_c_im, m_ssm_d, m_w_glu, m_b_glu, m_w_o_ssm, m_w_out, m_norm_mlp, m_w_up, m_w_down, v_norm_mix, v_w_in, v_q_a_norm, v_kv_a_norm, v_w_q_b, v_w_kv_b, v_q_norm, v_k_norm, v_w_o_mla, v_ssm_a_re, v_ssm_a_im, v_ssm_log_dt, v_ssm_b_re, v_ssm_b_im, v_ssm_c_re, v_ssm_c_im, v_ssm_d, v_w_glu, v_b_glu, v_w_o_ssm, v_w_out, v_norm_mlp, v_w_up, v_w_down):
    given = dict(locals())
    w = {n: given[n] for n in WEIGHT_ORDER}
    m = {n: given["m_" + n] for n in WEIGHT_ORDER}
    v = {n: given["v_" + n] for n in WEIGHT_ORDER}
    big = [n for n in WEIGHT_ORDER if n not in SMALL]
    small = {n: w[n] for n in SMALL}

    dx, landed, g_w_in, g_first_all, g_rest_all = _step(
        x[0], positions.reshape(-1, 1), loss_target[0], {n: w[n][0] for n in big}, small)

    grads, deltas, new_m, new_v = {}, {}, {}, {}
    for n in big:
        if n in ("w_in", "w_q_b"):
            wt, mt, vt = jnp.transpose(w[n][0]), jnp.transpose(m[n][0]), jnp.transpose(v[n][0])
            if n == "w_in":
                g = jnp.transpose(g_w_in)
                d, nm, nv = _adamw(wt, g, mt, vt, "adamw_" + n)
            else:
                g, d, nm, nv = _adamw_sum(jnp.transpose(landed[n], (0, 2, 1)), wt, mt, vt, "adamw_" + n)
            g, d, nm, nv = (jnp.transpose(a) for a in (g, d, nm, nv))
        else:
            g, d, nm, nv = _adamw_sum(landed[n], w[n][0], m[n][0], v[n][0], "adamw_" + n)
        grads[n], deltas[n], new_m[n], new_v[n] = g[None], d[None], nm[None], nv[None]

    outs = _adamw_small(g_first_all, g_rest_all, _pack_small(small), _pack_small({n: m[n] for n in SMALL}),
                        _pack_small({n: v[n] for n in SMALL}), _small_rows(small))
    for p, n in enumerate(SMALL):
        for k, dst in enumerate((grads, deltas, new_m, new_v)):
            dst[n] = outs[1 + 4 * p + k].reshape(-1)[:small[n].size].reshape(small[n].shape)

    return (outs[0][0, 0], dx[None], *[grads[n] for n in WEIGHT_ORDER], *[deltas[n] for n in WEIGHT_ORDER],
            *[new_m[n] for n in WEIGHT_ORDER], *[new_v[n] for n in WEIGHT_ORDER])
```

```python
import functools
import math

import numpy as np
import jax
import jax.numpy as jnp
from jax import lax
from jax.experimental import pallas as pl
from jax.experimental.pallas import tpu as pltpu

F32 = jnp.float32
BF16 = jnp.bfloat16

D_MODEL = 1024
SSM_GROUPS = 32
SSM_GROUP_CH = 16
SSM_WIDTH = 512
SSM_STATE = 64
N_STATE = SSM_GROUPS * SSM_STATE
N_HEADS = 8
QK_NOPE = 128
QK_ROPE = 64
QK_HEAD = 192
QK_PAD = 256
V_HEAD = 128
Q_LORA = 384
KV_LORA = 256
KV_LAT_PAD = 384
ROPE_THETA = 10000.0
D_FF = 4096
EPS = 1e-6
ATT_SCALE = QK_HEAD ** -0.5
N_DEV = 8
FF_SHARD = D_FF // N_DEV
OUT_SHARD = D_MODEL // N_DEV

IN_SEGS = ((0, 512), (512, 896), (896, 1280), (1280, 2304), (2304, 3328))
D_IN = 3264
D_IN_PAD = 3328
KV_END = 1216

ADAM_LR = 0.001
ADAM_B1 = 0.9
ADAM_B2 = 0.999
ADAM_EPS = 1e-08
ADAM_WD = 0.01
ADAM_STEP = 10

VMEM_LIMIT = 56 * 1024 * 1024
MESH = pl.DeviceIdType.MESH
ANY = pl.BlockSpec(memory_space=pl.ANY)
LANES = 128

SCAN_T = 256
SUBCHUNKS = 8
SCAN_CG = 1024
ATT_T = 512
ATT_SUB = 1
ATT_HEADS = 8
ATT_BWD_HEADS = 2
ROW_T = 256
MM_T = 512


def _params(sem=None):
    return pltpu.CompilerParams(dimension_semantics=sem, vmem_limit_bytes=VMEM_LIMIT)


def _rows(arr, tm):
    if arr.ndim == 2:
        return pl.BlockSpec((tm, arr.shape[1]), lambda i: (i, 0))
    return pl.BlockSpec((arr.shape[0], tm, arr.shape[2]), lambda i: (0, i, 0))


def _const(arr):
    nd = arr.ndim
    return pl.BlockSpec(arr.shape, lambda *_: (0,) * nd, pipeline_mode=pl.Buffered(1))


def _sds(shape, dtype):
    return jax.ShapeDtypeStruct(shape, dtype)


PEERS = tuple((dx, dy, dc) for dx in (0, 1) for dy in (0, 1) for dc in (0, 1) if (dx, dy, dc) != (0, 0, 0))


def _here():
    x, y, c = lax.axis_index("x"), lax.axis_index("y"), lax.axis_index("c")
    return x, y, c, 4 * x + 2 * y + c


def _xchg_start(scatter, srcs, dsts, send, recv, local):
    x, y, c, me = _here()
    for e, sc in enumerate(scatter):
        src, dst = srcs[e], dsts[e]
        pltpu.make_async_copy(src.at[me] if sc else src, dst.at[me], local.at[e]).start()
        for dx, dy, dc in PEERS:
            px, py, pc = (1 - x if dx else x), (1 - y if dy else y), (1 - c if dc else c)
            pltpu.make_async_remote_copy(
                src_ref=src.at[4 * px + 2 * py + pc] if sc else src, dst_ref=dst.at[me],
                send_sem=send.at[e], recv_sem=recv.at[e], device_id=(px, py, pc), device_id_type=MESH).start()


def _xchg_wait(scatter, srcs, dsts, send, recv, local):
    x, y, c, me = _here()
    for e, sc in enumerate(scatter):
        src, dst = srcs[e], dsts[e]
        pltpu.make_async_copy(src.at[me] if sc else src, dst.at[me], local.at[e]).wait()
        span = dst.at[pl.ds(0, N_DEV - 1)]
        both = pltpu.make_async_remote_copy(src_ref=span, dst_ref=span, send_sem=send.at[e], recv_sem=recv.at[e],
                                            device_id=(x, y, c), device_id_type=MESH)
        both.wait_send()
        both.wait_recv()


def _call(body, name, grid, ins, in_specs, outs, out_specs, scratch=(), xch=()):
    n_in, n_out, ne = len(ins), len(outs), len(xch)
    scatter = [sc for _, sc in xch]
    x_outs = [_sds((N_DEV,) + (a.shape[1:] if sc else a.shape), a.dtype) for a, sc in xch]
    sems = [pltpu.SemaphoreType.DMA((ne,))] * 3 if ne else []

    def wrapped(*refs):
        in_refs, x_src = refs[:n_in], refs[n_in:n_in + ne]
        out_refs = refs[n_in + ne:n_in + ne + n_out]
        x_dst = refs[n_in + ne + n_out:n_in + 2 * ne + n_out]
        rest = refs[n_in + 2 * ne + n_out:]
        if ne:
            x_sems, rest = rest[len(rest) - 3:], rest[:len(rest) - 3]
            first = functools.reduce(jnp.logical_and, [pl.program_id(d) == 0 for d in range(len(grid))])
            last = functools.reduce(jnp.logical_and, [pl.program_id(d) == grid[d] - 1 for d in range(len(grid))])

            @pl.when(first)
            def _():
                _xchg_start(scatter, x_src, x_dst, *x_sems)

        body(*in_refs, *out_refs, *rest)
        if ne:
            @pl.when(last)
            def _():
                _xchg_wait(scatter, x_src, x_dst, *x_sems)

    return pl.pallas_call(
        wrapped,
        name=name,
        grid=grid,
        in_specs=list(in_specs) + [ANY] * ne,
        out_specs=list(out_specs) + [ANY] * ne,
        out_shape=list(outs) + x_outs,
        scratch_shapes=list(scratch) + sems,
        compiler_params=_params(("arbitrary",) * len(grid)),
    )(*ins, *[a for a, _ in xch])


def _row_call(body, name, n_rows, tm, row_ins, const_ins, row_outs, acc_outs=(), xch=(), col_outs=(), scratch=()):
    outs = [_sds(s, d) for s, d in list(row_outs) + list(col_outs) + list(acc_outs)]
    n_row, n_col = len(row_outs), len(col_outs)
    out_specs = [_rows(o, tm) for o in outs[:n_row]] + [
        pl.BlockSpec(o.shape[:-1] + (tm,), lambda i, nd=len(o.shape): (0,) * (nd - 1) + (i,))
        for o in outs[n_row:n_row + n_col]] + [
        pl.BlockSpec(o.shape, lambda i, nd=len(o.shape): (0,) * nd) for o in outs[n_row + n_col:]]
    in_specs = [_rows(a, tm) for a in row_ins] + [_const(a) for a in const_ins]
    return _call(body, name, (n_rows // tm,), list(row_ins) + list(const_ins), in_specs, outs, out_specs,
                 scratch=scratch, xch=xch)


def _dot(a, b):
    return jnp.dot(a, b, preferred_element_type=F32)


def _dot_nt(a, b):
    return lax.dot_general(a, b, (((1,), (1,)), ((), ())), preferred_element_type=F32)


def _dot_tn(a, b):
    return lax.dot_general(a, b, (((0,), (0,)), ((), ())), preferred_element_type=F32)


def _rms(x, g, n):
    inv = lax.rsqrt(jnp.sum(x * x, -1, keepdims=True) * (1.0 / n) + EPS)
    return x * inv * g, inv


def _rms_bwd(dy, x, g, inv, n):
    xh = x * inv
    dxh = dy * g
    dx = inv * (dxh - xh * (jnp.sum(dxh * xh, -1, keepdims=True) * (1.0 / n)))
    return dx, dy * xh


def _sigmoid(x):
    return 1.0 / (1.0 + jnp.exp(-x))


_GELU_C = math.sqrt(2.0 / math.pi)


def _gelu(y):
    th = jnp.tanh(_GELU_C * (y + 0.044715 * (y * y * y)))
    return 0.5 * y * (1.0 + th), th


def _gelu_grad(y, th):
    return 0.5 * (1.0 + th) + 0.5 * y * (1.0 - th * th) * (_GELU_C * (1.0 + 3.0 * 0.044715 * (y * y)))


def _acc(ref, val):
    @pl.when(pl.program_id(0) == 0)
    def _():
        ref[...] = jnp.zeros_like(ref)

    ref[...] += val


def _tile(n, limit):
    if n <= limit:
        return n
    return max(t for t in range(128, limit + 1, 128) if n % t == 0)


def _lhs(a, turned, tm, tk):
    m, k_dim = a.shape if turned else a.shape[::-1]
    tm, tk = _tile(m, tm), _tile(k_dim, tk)
    if turned:
        return m, k_dim, tm, tk, pl.BlockSpec((tm, tk), lambda i, k: (i, k)), _dot
    return m, k_dim, tm, tk, pl.BlockSpec((tk, tm), lambda i, k: (k, i)), _dot_tn


def _matmul_tn_shards(a, b, name, by_col, tm=512, tk=512, turned=False):
    m, k_dim, tm, tk, a_spec, dot = _lhs(a, turned, tm, tk)
    n = b.shape[1]
    nk = k_dim // tk
    if by_col:
        r, c = m, n // N_DEV
        out_spec = pl.BlockSpec((N_DEV, tm, c), lambda i, k: (0, i, 0))
    else:
        r, c = m // N_DEV, n
        per = tm // r
        out_spec = pl.BlockSpec((per, r, c), lambda i, k: (i, 0, 0))

    def body(a_ref, b_ref, o_ref, acc_ref):
        k = pl.program_id(1)

        @pl.when(k == 0)
        def _():
            acc_ref[...] = jnp.zeros_like(acc_ref)

        acc_ref[...] += dot(a_ref[...].astype(BF16), b_ref[...].astype(BF16))

        @pl.when(k == nk - 1)
        def _():
            if by_col:
                for j in range(N_DEV):
                    o_ref[j] = acc_ref[:, j * c:(j + 1) * c].astype(BF16)
            else:
                for s in range(per):
                    o_ref[s] = acc_ref[s * r:(s + 1) * r, :].astype(BF16)

    return pl.pallas_call(
        body,
        name=name,
        grid=(m // tm, nk),
        in_specs=[a_spec, pl.BlockSpec((tk, n), lambda i, k: (k, 0))],
        out_specs=out_spec,
        out_shape=_sds((N_DEV, r, c), BF16),
        scratch_shapes=[pltpu.VMEM((tm, n), F32)],
        compiler_params=_params(("parallel", "arbitrary")),
    )(a, b)


def _rope_tables(pos_col):
    n = pos_col.shape[0]
    half = QK_ROPE // 2
    inv_freq = (ROPE_THETA ** (-np.arange(half, dtype=np.float32) / half)).astype(np.float32)
    freq_row = jnp.asarray(np.concatenate([inv_freq, inv_freq, np.zeros(64, np.float32)])[None, :])

    def body(p_ref, f_ref, c_ref, s_ref):
        ang = p_ref[...].astype(F32) * f_ref[...]
        c_ref[...] = jnp.cos(ang)
        s_ref[...] = jnp.sin(ang)

    return _row_call(body, "rope_tables", n, min(n, 1024), [pos_col], [freq_row], [((n, 128), F32)] * 2)


def _rope_rot(v):
    lane = lax.broadcasted_iota(jnp.int32, v.shape, 1)
    return jnp.where(lane < 32, -pltpu.roll(v, 96, 1), jnp.where(lane < 64, pltpu.roll(v, 32, 1), 0.0))


def _rope_rot_t(v):
    lane = lax.broadcasted_iota(jnp.int32, v.shape, 1)
    return jnp.where(lane < 32, pltpu.roll(v, 96, 1), jnp.where(lane < 64, -pltpu.roll(v, 32, 1), 0.0))


def _in_proj(x, norm_mix, w_in_pad, xch):
    n = x.shape[0]

    def body(x_ref, g_ref, w_ref, u_ref, ql_ref, kvl_ref, gs_ref, gm_ref, xnt_ref):
        xn, _ = _rms(x_ref[...], g_ref[...], D_MODEL)
        xb = xn.astype(BF16)
        xnt_ref[...] = xn.T.astype(BF16)
        for ref, (a, b) in zip((u_ref, ql_ref, kvl_ref, gs_ref, gm_ref), IN_SEGS):
            ref[...] = _dot(xb, w_ref[:, a:b])

    outs = [((n, b - a), F32) for a, b in IN_SEGS]
    return _row_call(body, "in_proj", n, MM_T, [x], [norm_mix, w_in_pad], outs, xch=xch,
                     col_outs=[((D_MODEL, n), BF16)])


def _ssm_prep_fn(a_re, a_im, log_dt, b_re_x, b_im_x):
    dt = jnp.exp(log_dt)
    mag = jnp.exp(a_re * dt)
    lr = mag * jnp.cos(a_im * dt)
    li = mag * jnp.sin(a_im * dt)
    den = a_re * a_re + a_im * a_im
    fr = ((lr - 1.0) * a_re + li * a_im) / den
    fi = (li * a_re - (lr - 1.0) * a_im) / den
    return lr, li, fr * b_re_x - fi * b_im_x, fr * b_im_x + fi * b_re_x


def _dot_exact(a, b, dims):
    return lax.dot_general(a, b, (dims, ((), ())), precision=lax.Precision.HIGHEST, preferred_element_type=F32)


def _lane_repeat(width, n):
    src = lax.broadcasted_iota(jnp.int32, (width, n), 0)
    dst = lax.broadcasted_iota(jnp.int32, (width, n), 1)
    return (dst % width == src).astype(F32)


def _same_group(rows, rows_per_group, cols, cols_per_group):
    row = lax.broadcasted_iota(jnp.int32, (rows, cols), 0)
    col = lax.broadcasted_iota(jnp.int32, (rows, cols), 1)
    return (row // rows_per_group) == (col // cols_per_group)


def _expand_b(bt):
    tiled = _dot_exact(bt, _lane_repeat(SSM_STATE, N_STATE), ((1,), (0,)))
    return jnp.where(_same_group(SSM_WIDTH, SSM_GROUP_CH, N_STATE, SSM_STATE), tiled, 0.0)


def _collect_b(m):
    masked = jnp.where(_same_group(SSM_WIDTH, SSM_GROUP_CH, N_STATE, SSM_STATE), m, 0.0)
    return _dot_exact(masked, _lane_repeat(SSM_STATE, N_STATE), ((1,), (1,)))


def _ssm_prep(a_re, a_im, log_dt, bt_re, bt_im, c2_re, c2_im):
    def body(ar, ai, ld, br, bi, cr, ci, lam_ref, bblk_ref, cblk_ref):
        lr, li, bbr, bbi = _ssm_prep_fn(ar[...], ai[...], ld[...], _expand_b(br[...]), _expand_b(bi[...]))
        lam_ref[0:1, :] = lr
        lam_ref[1:2, :] = li
        bblk_ref[:, 0:N_STATE] = bbr.astype(BF16)
        bblk_ref[:, N_STATE:] = bbi.astype(BF16)
        rep = _lane_repeat(SSM_GROUP_CH, SSM_WIDTH)
        own = _same_group(N_STATE, SSM_STATE, SSM_WIDTH, SSM_GROUP_CH)
        cblk_ref[0:N_STATE, :] = jnp.where(own, _dot_exact(cr[...], rep, ((1,), (0,))), 0.0).astype(BF16)
        cblk_ref[N_STATE:, :] = jnp.where(own, -_dot_exact(ci[...], rep, ((1,), (0,))), 0.0).astype(BF16)

    return pl.pallas_call(
        body,
        name="ssm_prep",
        out_shape=[_sds((2, N_STATE), F32), _sds((SSM_WIDTH, 2 * N_STATE), BF16),
                   _sds((2 * N_STATE, SSM_WIDTH), BF16)],
        compiler_params=_params(),
    )(a_re, a_im, log_dt, bt_re, bt_im, c2_re, c2_im)


def _ssm_prep_bwd(a_re, a_im, log_dt, bt_re, bt_im, dlam, dbblk, dcblk_t):
    def body(ar, ai, ld, br, bi, dl, db, dc, dar, dai, dld, dbr, dbi, dcr, dci):
        _, vjp = jax.vjp(_ssm_prep_fn, ar[...], ai[...], ld[...], _expand_b(br[...]), _expand_b(bi[...]))
        g = vjp((dl[0:1, :], dl[1:2, :], db[:, 0:N_STATE], db[:, N_STATE:]))
        dar[...] = g[0]
        dai[...] = g[1]
        grp = lax.broadcasted_iota(jnp.int32, (SSM_GROUPS, N_STATE), 0)
        lane = lax.broadcasted_iota(jnp.int32, (SSM_GROUPS, N_STATE), 1)
        sel = (lane // SSM_STATE) == grp
        dld[...] = jnp.sum(jnp.where(sel, jnp.broadcast_to(g[2], (SSM_GROUPS, N_STATE)), 0.0), axis=1, keepdims=True)
        dbr[...] = _collect_b(g[3])
        dbi[...] = _collect_b(g[4])
        dcr[...] = _collect_b(dc[:, 0:N_STATE])
        dci[...] = -_collect_b(dc[:, N_STATE:])

    small = _sds((SSM_WIDTH, SSM_STATE), F32)
    return pl.pallas_call(
        body,
        name="ssm_prep_bwd",
        out_shape=[_sds((1, N_STATE), F32), _sds((1, N_STATE), F32), _sds((SSM_GROUPS, 1), F32), small, small, small, small],
        compiler_params=_params(),
    )(a_re, a_im, log_dt, bt_re, bt_im, dlam, dbblk, dcblk_t)


def _perm_matrix(t):
    run = t // SUBCHUNKS
    p = np.zeros((t, t), np.float32)
    r = np.arange(t)
    p[r, (r % SUBCHUNKS) * run + r // SUBCHUNKS] = 1.0
    return jnp.asarray(p, dtype=BF16)


def _unpermute(p, a):
    hi = a.astype(BF16)
    r1 = a - hi.astype(F32)
    mid = r1.astype(BF16)
    lo = (r1 - mid.astype(F32)).astype(BF16)
    return _dot_tn(p, hi) + _dot_tn(p, mid) + _dot_tn(p, lo)


def _power_table(lam_ref, pw_ref, n):
    lr, li = lam_ref[0:1, :], lam_ref[1:2, :]
    pw_ref[0:1, 0:N_STATE] = lr
    pw_ref[0:1, N_STATE:] = li

    def step(i, carry):
        pr, pi = carry
        pr, pi = pr * lr - pi * li, pr * li + pi * lr
        pw_ref[pl.ds(i, 1), 0:N_STATE] = pr
        pw_ref[pl.ds(i, 1), N_STATE:] = pi
        return pr, pi

    lax.fori_loop(1, n, step, (lr, li))


def _col_groups():
    return [(pl.ds(c, SCAN_CG), pl.ds(N_STATE + c, SCAN_CG)) for c in range(0, N_STATE, SCAN_CG)]


def _run_scan(buf, lam_ref, t, reverse):
    nblk = t // 8
    for re, im in _col_groups():
        lr = jnp.broadcast_to(lam_ref[0:1, re], (8, SCAN_CG))
        li = jnp.broadcast_to(lam_ref[1:2, re], (8, SCAN_CG))
        if reverse:
            li = -li
        first = pl.ds((nblk - 1) * 8 if reverse else 0, 8)

        def step(k, carry, re=re, im=im, lr=lr, li=li):
            pr, pi = carry
            i = (nblk - 2 - k) if reverse else (k + 1)
            r = pl.ds(pl.multiple_of(i * 8, 8), 8)
            xr = buf[r, re] + lr * pr - li * pi
            xi = buf[r, im] + lr * pi + li * pr
            buf[r, re] = xr
            buf[r, im] = xi
            return xr, xi

        lax.fori_loop(0, nblk - 1, step, (buf[first, re], buf[first, im]))


def _run_carries(buf, pw_ref, carry_ref, s_ref, t, reverse):
    nblk = t // 8
    run = t // SUBCHUNKS
    edge = buf[pl.ds(0 if reverse else (nblk - 1) * 8, 8), :]
    pr, pi = pw_ref[run - 1:run, 0:N_STATE], pw_ref[run - 1:run, N_STATE:]
    if reverse:
        pi = -pi
    sr, si = carry_ref[0:1, 0:N_STATE], carry_ref[0:1, N_STATE:]
    for s in (range(SUBCHUNKS - 1, -1, -1) if reverse else range(SUBCHUNKS)):
        s_ref[s:s + 1, 0:N_STATE] = sr
        s_ref[s:s + 1, N_STATE:] = si
        er, ei = edge[s:s + 1, 0:N_STATE], edge[s:s + 1, N_STATE:]
        sr, si = er + pr * sr - pi * si, ei + pr * si + pi * sr
    carry_ref[:, 0:N_STATE] = jnp.broadcast_to(sr, (8, N_STATE))
    carry_ref[:, N_STATE:] = jnp.broadcast_to(si, (8, N_STATE))


def _run_fix(buf, pw_ref, s_ref, t, reverse):
    nblk = t // 8
    for re, im in _col_groups():
        sr, si = s_ref[:, re], s_ref[:, im]

        def step(i, carry, re=re, im=im, sr=sr, si=si):
            r = pl.ds(pl.multiple_of(i * 8, 8), 8)
            row = pl.ds((nblk - 1 - i) if reverse else i, 1)
            pr, pi = pw_ref[row, re], pw_ref[row, im]
            if reverse:
                pi = -pi
            buf[r, re] += pr * sr - pi * si
            buf[r, im] += pr * si + pi * sr
            return carry

        lax.fori_loop(0, nblk, step, 0)


STATE_BLOCKS = 2 * N_STATE // LANES
CH_BLOCKS = SSM_WIDTH // LANES


def _state_block(b):
    pair = b % (N_STATE // LANES)
    k = (pair * 2 * SSM_GROUP_CH) // LANES
    return slice(b * LANES, (b + 1) * LANES), slice(k * LANES, (k + 1) * LANES)


def _channel_block(c):
    w = N_STATE // CH_BLOCKS
    return slice(c * LANES, (c + 1) * LANES), slice(c * w, (c + 1) * w), slice(N_STATE + c * w, N_STATE + (c + 1) * w)


def _to_states(vb, w_ref, buf, nt):
    for b in range(STATE_BLOCKS):
        lanes, ch = _state_block(b)
        buf[:, lanes] = _dot_nt(vb[:, ch], w_ref[lanes, ch]) if nt else _dot(vb[:, ch], w_ref[ch, lanes])


def _to_channels(buf, w_ref, nt):
    outs = []
    for c in range(CH_BLOCKS):
        ch, re, im = _channel_block(c)
        xr, xi = buf[:, re].astype(BF16), buf[:, im].astype(BF16)
        if nt:
            outs.append(_dot_nt(xr, w_ref[ch, re]) + _dot_nt(xi, w_ref[ch, im]))
        else:
            outs.append(_dot(xr, w_ref[re, ch]) + _dot(xi, w_ref[im, ch]))
    return jnp.concatenate(outs, axis=-1)


def _ssm_fwd(u, bblk, cblk, lam, d_row, w_glu, b_glu, w_o_ssm, xch):
    n = u.shape[0]
    t = min(SCAN_T, n)
    perm = _perm_matrix(t)

    def body(u_ref, p_ref, bblk_ref, cblk_ref, lam_ref, d_ref, wg_ref, bg_ref, wo_ref, y_ref, ys_ref, st_ref,
             buf, pw_ref, carry_ref, s_ref):
        @pl.when(pl.program_id(0) == 0)
        def _():
            carry_ref[...] = jnp.zeros_like(carry_ref)
            _power_table(lam_ref, pw_ref, t // SUBCHUNKS)

        st_ref[0] = carry_ref[...]
        u_t = u_ref[...]
        p = p_ref[...]
        ub = _dot(p, u_t.astype(BF16)).astype(BF16)
        _to_states(ub, bblk_ref, buf, False)
        _run_scan(buf, lam_ref, t, False)
        _run_carries(buf, pw_ref, carry_ref, s_ref, t, False)
        _run_fix(buf, pw_ref, s_ref, t, False)
        y = d_ref[...] * u_t + _unpermute(p, _to_channels(buf, cblk_ref, False))
        y_ref[...] = y
        z, _ = _gelu(y)
        s = _sigmoid(_dot(z.astype(BF16), wg_ref[...]) + bg_ref[...])
        zgb = (z * s).astype(BF16)
        for j in range(N_DEV):
            ys_ref[:, j * OUT_SHARD:(j + 1) * OUT_SHARD] = _dot(zgb, wo_ref[j])

    consts = [perm, bblk, cblk, lam, d_row, w_glu, b_glu, w_o_ssm]
    return _call(
        body, "ssm_fwd", (n // t,), [u] + consts, [_rows(u, t)] + [_const(a) for a in consts],
        [_sds((n, SSM_WIDTH), F32), _sds((n, D_MODEL), F32), _sds((n // t, 8, 2 * N_STATE), F32)],
        [pl.BlockSpec((t, SSM_WIDTH), lambda i: (i, 0)), pl.BlockSpec((t, D_MODEL), lambda i: (i, 0)),
         pl.BlockSpec((1, 8, 2 * N_STATE), lambda i: (i, 0, 0))],
        scratch=[pltpu.VMEM((t, 2 * N_STATE), F32), pltpu.VMEM((t // SUBCHUNKS, 2 * N_STATE), F32),
                 pltpu.VMEM((8, 2 * N_STATE), F32), pltpu.VMEM((8, 2 * N_STATE), F32)],
        xch=xch)


def _head_norm_rope(slab, gain, cos_t, sin_t):
    xn, inv = _rms(slab, gain, QK_HEAD)
    lo, hi = xn[:, 0:128], xn[:, 128:256]
    return jnp.concatenate([lo, hi * cos_t + _rope_rot(hi) * sin_t], axis=-1), inv


def _head_norm_rope_bwd(g, slab, gain, inv, cos_t, sin_t):
    g_lo, g_hi = g[:, 0:128], g[:, 128:256]
    g_n = jnp.concatenate([g_lo, g_hi * cos_t + _rope_rot_t(g_hi * sin_t)], axis=-1)
    return _rms_bwd(g_n, slab, gain, inv, QK_HEAD)


def _qkv_prep(ql, kvl, q_a_norm, kv_a_norm, wq, wkv, gq, gk, cos_t, sin_t):
    n = ql.shape[0]
    tm = ROW_T

    def body(ql_ref, kvl_ref, cos_ref, sin_ref, qa_ref, ka_ref, wq_ref, wkv_ref, gq_ref, gk_ref,
             q_ref, k_ref, v_ref, kt_ref, vt_ref):
        cos_t, sin_t = cos_ref[...], sin_ref[...]
        qa, _ = _rms(ql_ref[...], qa_ref[...], Q_LORA)
        qab = qa.astype(BF16)
        kvl_t = kvl_ref[...]
        ca, _ = _rms(kvl_t[:, 0:KV_LORA], ka_ref[...], KV_LORA)
        cab = ca.astype(BF16)
        kpe = kvl_t[:, KV_LORA:KV_LAT_PAD]
        q_pre = _dot(qab, wq_ref[...])
        kv_pre = _dot(cab, wkv_ref[...])
        for h in range(N_HEADS):
            qh, _ = _head_norm_rope(q_pre[:, h * QK_PAD:(h + 1) * QK_PAD], gq_ref[...], cos_t, sin_t)
            q_ref[h] = (qh * ATT_SCALE).astype(BF16)
            kv_h = kv_pre[:, h * QK_PAD:(h + 1) * QK_PAD]
            kh, _ = _head_norm_rope(jnp.concatenate([kv_h[:, 0:QK_NOPE], kpe], axis=-1), gk_ref[...], cos_t, sin_t)
            khb = kh.astype(BF16)
            k_ref[h] = khb
            kt_ref[h] = khb.T
            vhb = kv_h[:, QK_NOPE:].astype(BF16)
            v_ref[h] = vhb
            vt_ref[h] = vhb.T

    row_ins, consts = [ql, kvl, cos_t, sin_t], [q_a_norm, kv_a_norm, wq, wkv, gq, gk]
    outs = [_sds((N_HEADS, n, QK_PAD), BF16), _sds((N_HEADS, n, QK_PAD), BF16), _sds((N_HEADS, n, V_HEAD), BF16),
            _sds((N_HEADS, QK_PAD, n), BF16), _sds((N_HEADS, V_HEAD, n), BF16)]
    out_specs = [_rows(o, tm) for o in outs[:3]] + [
        pl.BlockSpec((N_HEADS, QK_PAD, tm), lambda i: (0, 0, i)), pl.BlockSpec((N_HEADS, V_HEAD, tm), lambda i: (0, 0, i))]
    return _call(body, "qkv_prep", (n // tm,), row_ins + consts,
                 [_rows(a, tm) for a in row_ins] + [_const(a) for a in consts], outs, out_specs)


def _causal_mask_t(st, t):
    key = lax.broadcasted_iota(jnp.int32, (t, t), 0)
    qry = lax.broadcasted_iota(jnp.int32, (t, t), 1)
    return jnp.where(key <= qry, st, -jnp.inf)


def _attn_fwd(q, k, vt, xch):
    n = q.shape[1]
    t = min(ATT_T, n)

    hp = ATT_HEADS
    kv_mode = pl.Buffered(1) if hp == N_HEADS else None

    def body(q_ref, k_ref, vt_ref, o_ref, lse_ref, ot_ref):
        i = pl.program_id(1)
        qts = [q_ref[g] for g in range(hp)]

        def kv_tile(j, carry, diag):
            ts = t // ATT_SUB
            sts = []
            for g in range(hp):
                for a in range(ATT_SUB):
                    r0 = pl.multiple_of(j * t + a * ts, ts)
                    st = _dot_nt(k_ref[g, pl.ds(r0, ts), :], qts[g])
                    if diag:
                        key = lax.broadcasted_iota(jnp.int32, (ts, t), 0) + a * ts
                        qry = lax.broadcasted_iota(jnp.int32, (ts, t), 1)
                        st = jnp.where(key <= qry, st, -jnp.inf)
                    sts.append(st)
            out = []
            for g in range(hp):
                m, l, acc = carry[g]
                for a in range(ATT_SUB):
                    st = sts[g * ATT_SUB + a]
                    r0 = pl.multiple_of(j * t + a * ts, ts)
                    m_new = jnp.maximum(m, jnp.max(st, 0, keepdims=True))
                    alpha = jnp.exp(m - m_new)
                    pt = jnp.exp(st - m_new)
                    l = alpha * l + jnp.sum(pt, 0, keepdims=True)
                    acc = alpha * acc + _dot(vt_ref[g, :, pl.ds(r0, ts)], pt.astype(BF16))
                    m = m_new
                out.append((m, l, acc))
            return tuple(out)

        one = (jnp.full((1, t), -jnp.inf, F32), jnp.zeros((1, t), F32), jnp.zeros((V_HEAD, t), F32))
        carry = lax.fori_loop(0, i, functools.partial(kv_tile, diag=False), (one,) * hp)
        for g, (m, l, acc) in enumerate(kv_tile(i, carry, True)):
            out_t = acc / l
            o_ref[:, g * V_HEAD:(g + 1) * V_HEAD] = out_t.T
            ot_ref[g * V_HEAD:(g + 1) * V_HEAD, :] = out_t.astype(BF16)
            lse_ref[g] = m + jnp.log(l)

    return _call(
        body, "attn_fwd", (N_HEADS // hp, n // t), [q, k, vt],
        [pl.BlockSpec((hp, t, QK_PAD), lambda h, i: (h, i, 0)),
         pl.BlockSpec((hp, n, QK_PAD), lambda h, i: (h, 0, 0), pipeline_mode=kv_mode),
         pl.BlockSpec((hp, V_HEAD, n), lambda h, i: (h, 0, 0), pipeline_mode=kv_mode)],
        [_sds((n, N_HEADS * V_HEAD), F32), _sds((N_HEADS, 1, n), F32), _sds((N_HEADS * V_HEAD, n), BF16)],
        [pl.BlockSpec((t, hp * V_HEAD), lambda h, i: (i, h)), pl.BlockSpec((hp, 1, t), lambda h, i: (h, 0, i)),
         pl.BlockSpec((hp * V_HEAD, t), lambda h, i: (h, i))],
        xch=xch)


def _merge(attn, gs, gm, y_ssm, x, w_o_mla, w_out):
    n = x.shape[0]

    def body(at_ref, gs_ref, gm_ref, ys_ref, x_ref, wo_ref, wout_ref, h_ref, ym_ref, mxt_ref):
        y_mla = _dot(at_ref[...].astype(BF16), wo_ref[...])
        ym_ref[...] = y_mla
        mixed = _sigmoid(gs_ref[...]) * ys_ref[...] + _sigmoid(gm_ref[...]) * y_mla
        mxt_ref[...] = mixed.T.astype(BF16)
        h_ref[...] = x_ref[...] + _dot(mixed.astype(BF16), wout_ref[...])

    outs = [((n, D_MODEL), F32), ((n, D_MODEL), F32)]
    return _row_call(body, "merge", n, MM_T, [attn, gs, gm, y_ssm, x], [w_o_mla, w_out], outs,
                     col_outs=[((D_MODEL, n), BF16)])


def _mlp_fwd_loss(h, target, norm_mlp, w_up, w_down):
    n = h.shape[0]

    def body(h_ref, t_ref, g_ref, wu_ref, wd_ref, hn_ref, do_ref, hnt_ref, loss_ref):
        h_t = h_ref[...]
        hn, _ = _rms(h_t, g_ref[...], D_MODEL)
        hb = hn.astype(BF16)
        hn_ref[...] = hb
        hnt_ref[...] = hn.T.astype(BF16)
        out = h_t
        for j in range(N_DEV):
            a = jnp.maximum(_dot(hb, wu_ref[j]), 0.0)
            out += _dot((a * a).astype(BF16), wd_ref[j])
        err = out - t_ref[...]
        do_ref[...] = err * (1.0 / D_MODEL)
        _acc(loss_ref, jnp.broadcast_to(jnp.sum(err * err) * (0.5 / D_MODEL), loss_ref.shape))

    outs = [((n, D_MODEL), BF16), ((n, D_MODEL), F32)]
    return _row_call(body, "mlp_fwd_loss", n, MM_T, [h, target], [norm_mlp, w_up, w_down], outs, [((8, 128), F32)],
                     col_outs=[((D_MODEL, n), BF16)])


def _mlp_bwd(dout, hn, h, norm_mlp, w_up, w_down):
    n = h.shape[0]

    def body(do_ref, hn_ref, h_ref, g_ref, wu_ref, wd_ref, da_ref, dh_ref, dob_ref, hidt_ref, dg_ref):
        dout_t = do_ref[...]
        doutb = dout_t.astype(BF16)
        dob_ref[...] = doutb
        hb = hn_ref[...]
        dhn = jnp.zeros_like(dout_t)
        for j in range(N_DEV):
            cols = slice(j * FF_SHARD, (j + 1) * FF_SHARD)
            a = jnp.maximum(_dot(hb, wu_ref[j]), 0.0)
            hidt_ref[cols, :] = (a * a).T.astype(BF16)
            da = (_dot_nt(doutb, wd_ref[j]) * (2.0 * a)).astype(BF16)
            da_ref[:, cols] = da
            dhn += _dot_nt(da, wu_ref[j])
        h_t = h_ref[...]
        inv = lax.rsqrt(jnp.sum(h_t * h_t, -1, keepdims=True) * (1.0 / D_MODEL) + EPS)
        dx, dg = _rms_bwd(dhn, h_t, g_ref[...], inv, D_MODEL)
        dh_ref[...] = dout_t + dx
        _acc(dg_ref, jnp.sum(dg, 0, keepdims=True))

    outs = [((n, D_FF), BF16), ((n, D_MODEL), F32), ((n, D_MODEL), BF16)]
    return _row_call(body, "mlp_bwd", n, MM_T, [dout, hn, h], [norm_mlp, w_up, w_down], outs, [((1, D_MODEL), F32)],
                     col_outs=[((D_FF, n), BF16)])


def _merge_bwd(dh, gs, gm, y_ssm, y_mla, attn, w_out, w_o_mla):
    n = dh.shape[0]

    def body(dh_ref, gs_ref, gm_ref, ys_ref, ym_ref, at_ref, wout_ref, wo_ref,
             dgs_ref, dgm_ref, dys_ref, dym_ref, dat_ref, delta_ref):
        dmix = _dot_nt(dh_ref[...].astype(BF16), wout_ref[...])
        sgs, sgm = _sigmoid(gs_ref[...]), _sigmoid(gm_ref[...])
        dgs_ref[...] = (dmix * ys_ref[...] * sgs * (1.0 - sgs)).astype(BF16)
        dgm_ref[...] = (dmix * ym_ref[...] * sgm * (1.0 - sgm)).astype(BF16)
        dys_ref[...] = (dmix * sgs).astype(BF16)
        dym = (dmix * sgm).astype(BF16)
        dym_ref[...] = dym
        dattn = _dot_nt(dym, wo_ref[...])
        dat_ref[...] = dattn.astype(BF16)
        prod = dattn * at_ref[...]
        ones = jnp.ones((8, V_HEAD), F32)
        for h in range(N_HEADS):
            delta_ref[h] = _dot_exact(ones, prod[:, h * V_HEAD:(h + 1) * V_HEAD], ((1,), (1,)))[0:1, :]

    outs = [((n, D_MODEL), BF16)] * 5
    return _row_call(body, "merge_bwd", n, MM_T, [dh, gs, gm, y_ssm, y_mla, attn], [w_out, w_o_mla], outs,
                     col_outs=[((N_HEADS, 1, n), F32)])


def _attn_bwd(q, k, kt, v, lse, delta, dout, xch):
    n = q.shape[1]
    t = min(ATT_T, n)
    nt = n // t
    hp = ATT_BWD_HEADS

    def body(q_ref, k_ref, kt_ref, v_ref, lse_ref, delta_ref, do_ref, dq_ref, dk_ref, dv_ref, dqt_ref):
        j = pl.program_id(1)

        @pl.when(j == 0)
        def _():
            dqt_ref[...] = jnp.zeros_like(dqt_ref)

        def q_tile(i, carry, diag):
            r0 = pl.multiple_of(i * t, t)
            rows = pl.ds(r0, t)
            qts = [q_ref[g, rows, :] for g in range(hp)]
            sts = [_dot_nt(k_ref[g], qts[g]) for g in range(hp)]
            out = []
            for g in range(hp):
                dk, dv = carry[g]
                st = _causal_mask_t(sts[g], t) if diag else sts[g]
                pt = jnp.exp(st - lse_ref[g, :, rows])
                dob = do_ref[rows, g * V_HEAD:(g + 1) * V_HEAD]
                dv = dv + _dot(pt.astype(BF16), dob)
                dst = (pt * (_dot_nt(v_ref[g], dob) - delta_ref[g, :, rows])).astype(BF16)
                dk = dk + _dot(dst, qts[g])
                dqt_ref[g, :, rows] += _dot(kt_ref[g], dst)
                out.append((dk, dv))
            return tuple(out)

        zero = (jnp.zeros((t, QK_PAD), F32), jnp.zeros((t, V_HEAD), F32))
        carry = q_tile(j, (zero,) * hp, True)
        carry = lax.fori_loop(j + 1, nt, functools.partial(q_tile, diag=False), carry)
        for g, (dk, dv) in enumerate(carry):
            dk_ref[g] = dk
            dv_ref[g] = dv

        @pl.when(j == nt - 1)
        def _():
            for g in range(hp):
                for c in range(0, n, t):
                    dq_ref[g, c:c + t, :] = dqt_ref[g, :, c:c + t].T

    return _call(
        body, "attn_bwd", (N_HEADS // hp, nt), [q, k, kt, v, lse, delta, dout],
        [pl.BlockSpec((hp, n, QK_PAD), lambda h, j: (h, 0, 0)), pl.BlockSpec((hp, t, QK_PAD), lambda h, j: (h, j, 0)),
         pl.BlockSpec((hp, QK_PAD, t), lambda h, j: (h, 0, j)), pl.BlockSpec((hp, t, V_HEAD), lambda h, j: (h, j, 0)),
         pl.BlockSpec((hp, 1, n), lambda h, j: (h, 0, 0)), pl.BlockSpec((hp, 1, n), lambda h, j: (h, 0, 0)),
         pl.BlockSpec((n, hp * V_HEAD), lambda h, j: (0, h))],
        [_sds((N_HEADS, n, QK_PAD), F32), _sds((N_HEADS, n, QK_PAD), F32), _sds((N_HEADS, n, V_HEAD), F32)],
        [pl.BlockSpec((hp, n, QK_PAD), lambda h, j: (h, 0, 0)), pl.BlockSpec((hp, t, QK_PAD), lambda h, j: (h, j, 0)),
         pl.BlockSpec((hp, t, V_HEAD), lambda h, j: (h, j, 0))],
        scratch=[pltpu.VMEM((hp, QK_PAD, n), F32)],
        xch=xch)


def _qkv_prep_bwd(ql, kvl, dq, dk, dv, q_a_norm, kv_a_norm, wq, wkv, gq, gk, cos_t, sin_t, xch):
    n = ql.shape[0]

    def body(ql_ref, kvl_ref, cos_ref, sin_ref, dq_ref, dk_ref, dv_ref, qa_ref, ka_ref, wq_ref, wkv_ref, gq_ref, gk_ref,
             dql_ref, dkvl_ref, dqa_ref, dka_ref, dgq_ref, dgk_ref, dwq_ref, dwkv_ref, dqp_ref, dkvp_ref):
        cos_t, sin_t = cos_ref[...], sin_ref[...]
        ql_t = ql_ref[...]
        qa, inv_qa = _rms(ql_t, qa_ref[...], Q_LORA)
        qab = qa.astype(BF16)
        kvl_t = kvl_ref[...]
        ckv = kvl_t[:, 0:KV_LORA]
        ca, inv_ca = _rms(ckv, ka_ref[...], KV_LORA)
        cab = ca.astype(BF16)
        kpe = kvl_t[:, KV_LORA:KV_LAT_PAD]
        dgq = jnp.zeros((1, QK_PAD), F32)
        dgk = jnp.zeros((1, QK_PAD), F32)
        dkpe = jnp.zeros_like(kpe)
        q_pre = _dot(qab, wq_ref[...])
        kv_pre = _dot(cab, wkv_ref[...])
        for h in range(N_HEADS):
            head = slice(h * QK_PAD, (h + 1) * QK_PAD)
            q_slab = q_pre[:, head]
            inv = lax.rsqrt(jnp.sum(q_slab * q_slab, -1, keepdims=True) * (1.0 / QK_HEAD) + EPS)
            d_slab, dg = _head_norm_rope_bwd(dq_ref[h] * ATT_SCALE, q_slab, gq_ref[...], inv, cos_t, sin_t)
            dqp_ref[:, head] = d_slab.astype(BF16)
            dgq += jnp.sum(dg, 0, keepdims=True)
            k_slab = jnp.concatenate([kv_pre[:, h * QK_PAD:h * QK_PAD + QK_NOPE], kpe], axis=-1)
            inv = lax.rsqrt(jnp.sum(k_slab * k_slab, -1, keepdims=True) * (1.0 / QK_HEAD) + EPS)
            d_slab, dg = _head_norm_rope_bwd(dk_ref[h], k_slab, gk_ref[...], inv, cos_t, sin_t)
            dkvp_ref[:, head] = jnp.concatenate([d_slab[:, 0:QK_NOPE], dv_ref[h]], axis=-1).astype(BF16)
            dkpe += d_slab[:, QK_NOPE:QK_PAD]
            dgk += jnp.sum(dg, 0, keepdims=True)
        dqa = _dot_nt(dqp_ref[...], wq_ref[...])
        dx, dg = _rms_bwd(dqa, ql_t, qa_ref[...], inv_qa, Q_LORA)
        dql_ref[...] = dx.astype(BF16)
        _acc(dqa_ref, jnp.sum(dg, 0, keepdims=True))
        dca = _dot_nt(dkvp_ref[...], wkv_ref[...])
        dx, dg = _rms_bwd(dca, ckv, ka_ref[...], inv_ca, KV_LORA)
        dkvl_ref[:, 0:KV_LORA] = dx.astype(BF16)
        dkvl_ref[:, KV_LORA:KV_LAT_PAD] = dkpe.astype(BF16)
        _acc(dka_ref, jnp.sum(dg, 0, keepdims=True))
        _acc(dgq_ref, dgq)
        _acc(dgk_ref, dgk)
        _acc(dwq_ref, _dot_tn(qab, dqp_ref[...]))
        _acc(dwkv_ref, _dot_tn(cab, dkvp_ref[...]))

    wide = N_HEADS * QK_PAD
    row_outs = [((n, Q_LORA), BF16), ((n, KV_LAT_PAD), BF16)]
    acc_outs = [((1, Q_LORA), F32), ((1, KV_LORA), F32), ((1, QK_PAD), F32), ((1, QK_PAD), F32),
                ((Q_LORA, wide), F32), ((KV_LORA, wide), F32)]
    return _row_call(body, "qkv_prep_bwd", n, ROW_T, [ql, kvl, cos_t, sin_t, dq, dk, dv],
                     [q_a_norm, kv_a_norm, wq, wkv, gq, gk], row_outs, acc_outs, xch=xch,
                     scratch=[pltpu.VMEM((ROW_T, wide), BF16), pltpu.VMEM((ROW_T, wide), BF16)])


def _glu_bwd(dy_ssm, y, w_glu, b_glu, w_o_ssm):
    n = y.shape[0]

    def body(dys_ref, y_ref, wg_ref, bg_ref, wo_ref, dy_ref, db_ref, dwg_ref, dwo_ref):
        y_t = y_ref[...]
        z, th = _gelu(y_t)
        zb = z.astype(BF16)
        s = _sigmoid(_dot(zb, wg_ref[...]) + bg_ref[...])
        dys = dys_ref[...]
        dzg = jnp.zeros_like(y_t)
        for j in range(N_DEV):
            dzg += _dot_nt(dys[:, j * OUT_SHARD:(j + 1) * OUT_SHARD], wo_ref[j])
        dt = dzg * z * s * (1.0 - s)
        dtb = dt.astype(BF16)
        dz = dzg * s + _dot_nt(dtb, wg_ref[...])
        dy_ref[...] = dz * _gelu_grad(y_t, th)
        _acc(db_ref, jnp.sum(dt, 0, keepdims=True))
        _acc(dwg_ref, _dot_tn(zb, dtb))
        _acc(dwo_ref, _dot_tn((z * s).astype(BF16), dys))

    acc_outs = [((1, SSM_WIDTH), F32), ((SSM_WIDTH, SSM_WIDTH), F32), ((SSM_WIDTH, D_MODEL), F32)]
    return _row_call(body, "glu_bwd", n, ROW_T, [dy_ssm, y], [w_glu, b_glu, w_o_ssm], [((n, SSM_WIDTH), F32)], acc_outs)


def _ssm_bwd(u, dy, st, bblk, cblk, lam, d_row, xch):
    n = u.shape[0]
    t = min(SCAN_T, n)
    nc = n // t
    kb = 512
    perm = _perm_matrix(t)

    def body(u_ref, dy_ref, st_ref, p_ref, bblk_ref, cblk_ref, lam_ref, d_ref,
             du_ref, dlam_ref, dd_ref, db_ref, dct_ref,
             buf_x, buf_a, pw_ref, carry_ref, xcarry_ref, sx_ref, sa_ref, db_acc, dct_acc):
        @pl.when(pl.program_id(0) == 0)
        def _():
            carry_ref[...] = jnp.zeros_like(carry_ref)
            db_acc[...] = jnp.zeros_like(db_acc)
            dct_acc[...] = jnp.zeros_like(dct_acc)
            _power_table(lam_ref, pw_ref, t // SUBCHUNKS)

        u_t = u_ref[...]
        dy_t = dy_ref[...]
        p = p_ref[...]
        ub = _dot(p, u_t.astype(BF16)).astype(BF16)
        dyb = _dot(p, dy_t.astype(BF16)).astype(BF16)
        _to_states(ub, bblk_ref, buf_x, False)
        xcarry_ref[...] = st_ref[0]
        _run_scan(buf_x, lam_ref, t, False)
        _run_carries(buf_x, pw_ref, xcarry_ref, sx_ref, t, False)
        _run_fix(buf_x, pw_ref, sx_ref, t, False)
        _to_states(dyb, cblk_ref, buf_a, True)
        _run_scan(buf_a, lam_ref, t, True)
        _run_carries(buf_a, pw_ref, carry_ref, sa_ref, t, True)
        _run_fix(buf_a, pw_ref, sa_ref, t, True)
        du_ref[...] = (d_ref[...] * dy_t + _unpermute(p, _to_channels(buf_a, bblk_ref, True))).astype(BF16)
        for b in range(STATE_BLOCKS):
            lanes, ch = _state_block(b)
            db_acc[ch, lanes] += _dot_tn(ub[:, ch], buf_a[:, lanes].astype(BF16))
            dct_acc[ch, lanes] += _dot_tn(dyb[:, ch], buf_x[:, lanes].astype(BF16))
        for c in range(0, N_STATE, kb):
            re, im = pl.ds(c, kb), pl.ds(N_STATE + c, kb)
            xr, xi = buf_x[pl.ds(0, t - 8), re], buf_x[pl.ds(0, t - 8), im]
            ar, ai = buf_a[pl.ds(8, t - 8), re], buf_a[pl.ds(8, t - 8), im]
            x0r, x0i = sx_ref[:, re], sx_ref[:, im]
            a0r, a0i = buf_a[0:8, re], buf_a[0:8, im]
            dlam_part_re = (jnp.sum(ar * xr + ai * xi, 0, keepdims=True)
                            + jnp.sum(a0r * x0r + a0i * x0i, 0, keepdims=True))
            dlam_part_im = (jnp.sum(ai * xr - ar * xi, 0, keepdims=True)
                            + jnp.sum(a0i * x0r - a0r * x0i, 0, keepdims=True))

            @pl.when(pl.program_id(0) == 0)
            def _(c=c):
                dlam_ref[0:1, c:c + kb] = jnp.zeros((1, kb), F32)
                dlam_ref[1:2, c:c + kb] = jnp.zeros((1, kb), F32)

            dlam_ref[0:1, c:c + kb] += dlam_part_re
            dlam_ref[1:2, c:c + kb] += dlam_part_im
        _acc(dd_ref, jnp.sum(dy_t * u_t, 0, keepdims=True))

        @pl.when(pl.program_id(0) == nc - 1)
        def _():
            pltpu.sync_copy(db_acc, db_ref)
            pltpu.sync_copy(dct_acc, dct_ref)

    rev = lambda i: (nc - 1 - i, 0)
    consts = [perm, bblk, cblk, lam, d_row]
    wide = (SSM_WIDTH, 2 * N_STATE)
    return _call(
        body, "ssm_bwd", (nc,), [u, dy, st] + consts,
        [pl.BlockSpec((t, SSM_WIDTH), rev), pl.BlockSpec((t, SSM_WIDTH), rev),
         pl.BlockSpec((1, 8, 2 * N_STATE), lambda i: (nc - 1 - i, 0, 0))] + [_const(a) for a in consts],
        [_sds((n, SSM_WIDTH), BF16), _sds((2, N_STATE), F32), _sds((1, SSM_WIDTH), F32), _sds(wide, F32), _sds(wide, F32)],
        [pl.BlockSpec((t, SSM_WIDTH), rev), pl.BlockSpec((2, N_STATE), lambda i: (0, 0)),
         pl.BlockSpec((1, SSM_WIDTH), lambda i: (0, 0)), ANY, ANY],
        scratch=[pltpu.VMEM((t, 2 * N_STATE), F32)] * 2 + [pltpu.VMEM((t // SUBCHUNKS, 2 * N_STATE), F32)]
        + [pltpu.VMEM((8, 2 * N_STATE), F32)] * 4 + [pltpu.VMEM(wide, F32)] * 2,
        xch=xch)


def _in_proj_bwd(pieces, dh, x, xn_t, norm_mix, w_in_pad, xch):
    n = x.shape[0]
    tm = min(MM_T, n)
    nt = n // tm

    def body(du_ref, dql_ref, dkvl_ref, dgs_ref, dgm_ref, dh_ref, x_ref, xnt_ref, g_ref, w_ref,
             dx_ref, dg_ref, dw_ref, acc_ref):
        @pl.when(pl.program_id(0) == 0)
        def _():
            acc_ref[...] = jnp.zeros_like(acc_ref)

        xnt = xnt_ref[...]
        dxn = jnp.zeros((tm, D_MODEL), F32)
        for ref, (a, b) in zip((du_ref, dql_ref, dkvl_ref, dgs_ref, dgm_ref), IN_SEGS):
            piece = ref[...]
            dxn += _dot_nt(piece, w_ref[:, a:b])
            acc_ref[:, a:b] += _dot(xnt, piece)
        x_t = x_ref[...]
        inv = lax.rsqrt(jnp.sum(x_t * x_t, -1, keepdims=True) * (1.0 / D_MODEL) + EPS)
        dx, dg = _rms_bwd(dxn, x_t, g_ref[...], inv, D_MODEL)
        dx_ref[...] = dh_ref[...] + dx
        _acc(dg_ref, jnp.sum(dg, 0, keepdims=True))

        @pl.when(pl.program_id(0) == nt - 1)
        def _():
            pltpu.sync_copy(acc_ref, dw_ref)

    row_ins, consts = list(pieces) + [dh, x], [norm_mix, w_in_pad]
    in_specs = ([_rows(a, tm) for a in row_ins] + [pl.BlockSpec((D_MODEL, tm), lambda i: (0, i))]
                + [_const(a) for a in consts])
    return _call(
        body, "in_proj_bwd", (nt,), row_ins + [xn_t] + consts, in_specs,
        [_sds((n, D_MODEL), F32), _sds((1, D_MODEL), F32), _sds((D_MODEL, D_IN_PAD), F32)],
        [pl.BlockSpec((tm, D_MODEL), lambda i: (i, 0)), pl.BlockSpec((1, D_MODEL), lambda i: (0, 0)), ANY],
        scratch=[pltpu.VMEM((D_MODEL, D_IN_PAD), F32)], xch=xch)


def _swap_minor(a):
    g, r, c = a.shape[1:]
    return jnp.transpose(a[0], (0, 2, 1)).reshape(g * c, r)


def _pad_in(w):
    return jnp.concatenate([w[:, :KV_END], jnp.zeros((w.shape[0], D_IN_PAD - D_IN), w.dtype), w[:, KV_END:]], axis=1)


def _unpad_in(w):
    return jnp.concatenate([w[:, :KV_END], w[:, KV_END + D_IN_PAD - D_IN:]], axis=1)


def _pad_gain(g):
    return jnp.pad(g, ((0, 0), (0, QK_PAD - QK_HEAD)))


def _place():
    x, y, c = lax.axis_index("x"), lax.axis_index("y"), lax.axis_index("c")
    chips = [(x, y), (1 - x, y), (x, 1 - y), (1 - x, 1 - y)]
    return x, y, c, chips


def _all_gather(block, name):
    rows, lanes = block.shape

    def body(x_ref, out_ref, send_sems, recv_sems, local_sem):
        x, y, c, chips = _place()
        me, sibling = (x, y, c), (x, y, 1 - c)

        def slot(px, py, pc):
            return out_ref.at[4 * px + 2 * py + pc]

        def copy(k, blk, to, src=None):
            return pltpu.make_async_remote_copy(
                src_ref=slot(*blk) if src is None else src, dst_ref=slot(*blk),
                send_sem=send_sems.at[k], recv_sem=recv_sems.at[k], device_id=to, device_id_type=MESH)

        mine = pltpu.make_async_copy(x_ref, slot(*me), local_sem)
        mine.start()
        first = [copy(0, me, sibling, src=x_ref)]
        first += [copy(1 + j, me, (*chip, c), src=x_ref) for j, chip in enumerate(chips[1:])]
        for cp in first:
            cp.start()
        passed = [copy(4 + j, (*chip, c), sibling) for j, chip in enumerate(chips[1:])]
        for j, chip in enumerate(chips[1:]):
            copy(1 + j, (*chip, c), me).wait_recv()
            passed[j].start()
        copy(0, sibling, me).wait_recv()
        for j, chip in enumerate(chips[1:]):
            copy(4 + j, (*chip, 1 - c), me).wait_recv()
        for cp in first + passed:
            cp.wait_send()
        mine.wait()

    return pl.pallas_call(
        body,
        name=name,
        in_specs=[ANY],
        out_specs=ANY,
        out_shape=_sds((N_DEV, rows, lanes), block.dtype),
        scratch_shapes=[pltpu.SemaphoreType.DMA((7,)), pltpu.SemaphoreType.DMA((7,)), pltpu.SemaphoreType.DMA],
    )(block)


def _reduce_scatter(parts, gather, name):
    _, rows, lanes = parts.shape

    def body(p_ref, g_ref, out_ref, ga_ref, own, land_a, send_b, land_b, sa, ra, sb, rb, lo, *g_sems):
        x, y, c, chips = _place()
        sibling = (x, y, 1 - c)
        _xchg_start([False], [g_ref], [ga_ref], *g_sems)

        def blk(chip, core):
            return p_ref.at[4 * chip[0] + 2 * chip[1] + core]

        to_sib = [pltpu.make_async_remote_copy(
            src_ref=blk(chips[k], 1 - c), dst_ref=land_a.at[k], send_sem=sa.at[k], recv_sem=ra.at[k],
            device_id=sibling, device_id_type=MESH) for k in range(4)]
        for cp in to_sib:
            cp.start()
        loads = [pltpu.make_async_copy(blk(chips[k], c), own.at[k], lo.at[k]) for k in range(4)]
        for cp in loads:
            cp.start()
        to_chip = [pltpu.make_async_remote_copy(
            src_ref=send_b.at[j], dst_ref=land_b.at[j], send_sem=sb.at[j], recv_sem=rb.at[j],
            device_id=(*chips[1 + j], c), device_id_type=MESH) for j in range(3)]
        for k in (1, 2, 3):
            to_sib[k].wait_recv()
            loads[k].wait()
            send_b[k - 1] = (own[k] + land_a[k]).astype(BF16)
            to_chip[k - 1].start()
        to_sib[0].wait_recv()
        loads[0].wait()
        acc = own[0] + land_a[0]
        for j in range(3):
            to_chip[j].wait_recv()
            acc = acc + land_b[j].astype(F32)
        out_ref[...] = acc
        for cp in to_sib + to_chip:
            cp.wait_send()
        _xchg_wait([False], [g_ref], [ga_ref], *g_sems)

    return pl.pallas_call(
        body,
        name=name,
        in_specs=[ANY, ANY],
        out_specs=[pl.BlockSpec(memory_space=pltpu.VMEM), ANY],
        out_shape=[_sds((rows, lanes), F32), _sds((N_DEV,) + gather.shape, gather.dtype)],
        scratch_shapes=[pltpu.VMEM((4, rows, lanes), F32), pltpu.VMEM((4, rows, lanes), F32),
                        pltpu.VMEM((3, rows, lanes), BF16), pltpu.VMEM((3, rows, lanes), BF16)]
        + [pltpu.SemaphoreType.DMA((4,))] * 2 + [pltpu.SemaphoreType.DMA((3,))] * 2 + [pltpu.SemaphoreType.DMA((4,))]
        + [pltpu.SemaphoreType.DMA((1,))] * 3,
        compiler_params=_params(),
    )(parts, gather)


def _adamw_math(w, g, m, v):
    m = ADAM_B1 * m + (1.0 - ADAM_B1) * g
    v = ADAM_B2 * v + (1.0 - ADAM_B2) * (g * g)
    m_hat = m / (1.0 - ADAM_B1 ** ADAM_STEP)
    v_hat = v / (1.0 - ADAM_B2 ** ADAM_STEP)
    delta = -ADAM_LR * (m_hat / (jnp.sqrt(v_hat) + ADAM_EPS) + ADAM_WD * w)
    return delta, m, v


def _row_tile(r):
    return max(t for t in range(8, min(r, 256) + 1, 8) if r % t == 0)


def _adamw(w, g, m, v, name):
    r, n = w.shape

    def body(w_ref, g_ref, m_ref, v_ref, d_ref, nm_ref, nv_ref):
        d_ref[...], nm_ref[...], nv_ref[...] = _adamw_math(w_ref[...], g_ref[...], m_ref[...], v_ref[...])

    return _row_call(body, name, r, _row_tile(r), [w, g, m, v], [], [((r, n), F32)] * 3)


def _adamw_sum(landed, w, m, v, name):
    r, n = w.shape

    def body(l_ref, w_ref, m_ref, v_ref, g_ref, d_ref, nm_ref, nv_ref):
        g = l_ref[0].astype(F32)
        for dev in range(1, N_DEV):
            g = g + l_ref[dev].astype(F32)
        g_ref[...] = g
        d_ref[...], nm_ref[...], nv_ref[...] = _adamw_math(w_ref[...], g, m_ref[...], v_ref[...])

    tm = max(t for t in range(16, min(r, 256) + 1, 16) if r % t == 0)
    return _row_call(body, name, r, tm, [landed, w, m, v], [], [((r, n), F32)] * 4)


def _adamw_small(first, rest, w, m, v, row_counts):
    n_rest = w.shape[0] - first.shape[1]

    def body(f_ref, r_ref, w_ref, m_ref, v_ref, loss_ref, *out_refs):
        gf, gr = f_ref[0], r_ref[0]
        for dev in range(1, N_DEV):
            gf, gr = gf + f_ref[dev], gr + r_ref[dev]
        loss_ref[...] = gr[n_rest:n_rest + 8]
        g = jnp.concatenate([gf, gr[0:n_rest]], axis=0)
        d, nm, nv = _adamw_math(w_ref[...], g, m_ref[...], v_ref[...])
        off = 0
        for p, rows in enumerate(row_counts):
            for k, val in enumerate((g, d, nm, nv)):
                out_refs[4 * p + k][...] = val[off:off + rows]
            off += rows

    outs = [_sds((8, LANES), F32)] + [_sds((rows, LANES), F32) for rows in row_counts for _ in range(4)]
    return pl.pallas_call(body, name="adamw_small", out_shape=outs, compiler_params=_params())(first, rest, w, m, v)


SMALL = ("norm_mix", "q_a_norm", "kv_a_norm", "q_norm", "k_norm", "ssm_a_re", "ssm_a_im", "ssm_log_dt", "ssm_b_re",
         "ssm_b_im", "ssm_c_re", "ssm_c_im", "ssm_d", "b_glu", "norm_mlp")
WEIGHT_ORDER = ("norm_mix", "w_in", "q_a_norm", "kv_a_norm", "w_q_b", "w_kv_b", "q_norm", "k_norm", "w_o_mla",
                "ssm_a_re", "ssm_a_im", "ssm_log_dt", "ssm_b_re", "ssm_b_im", "ssm_c_re", "ssm_c_im", "ssm_d", "w_glu",
                "b_glu", "w_o_ssm", "w_out", "norm_mlp", "w_up", "w_down")
IN_SHARD = D_IN // N_DEV


def _pack_small(vals, names=SMALL):
    parts = []
    for n in names:
        flat = vals[n].reshape(-1)
        size = -(-flat.shape[0] // (8 * LANES)) * 8 * LANES
        parts.append(jnp.pad(flat, (0, size - flat.shape[0])).reshape(-1, LANES))
    return jnp.concatenate(parts, axis=0)


def _small_rows(like):
    return [-(-like[n].size // (8 * LANES)) * 8 for n in SMALL]


def _step(x, pos_col, target, w, small):
    bf = {n: a.astype(BF16) for n, a in w.items()}
    gq, gk = _pad_gain(small["q_norm"]), _pad_gain(small["k_norm"])
    a_re = small["ssm_a_re"].reshape(1, N_STATE)
    a_im = small["ssm_a_im"].reshape(1, N_STATE)
    log_dt = jnp.repeat(small["ssm_log_dt"].reshape(SSM_GROUPS), SSM_STATE).reshape(1, N_STATE)
    bt_re, bt_im = _swap_minor(small["ssm_b_re"]), _swap_minor(small["ssm_b_im"])
    c2_re, c2_im = _swap_minor(small["ssm_c_re"]), _swap_minor(small["ssm_c_im"])
    d_row = small["ssm_d"].reshape(1, SSM_WIDTH)

    w_in_all = _all_gather(bf["w_in"], "gather_w_in")
    w_in_pad = _pad_in(jnp.transpose(w_in_all, (1, 0, 2)).reshape(D_MODEL, D_IN))
    cos_t, sin_t = _rope_tables(pos_col)
    lam, bblk, cblk = _ssm_prep(a_re, a_im, log_dt, bt_re, bt_im, c2_re, c2_im)
    wq_mine = jnp.pad(bf["w_q_b"], ((0, 0), (0, QK_PAD - QK_HEAD)))
    u, ql, kvl, gs, gm, xn_t, w_glu, w_o_ssm = _in_proj(
        x, small["norm_mix"], w_in_pad, xch=[(bf["w_glu"], False), (bf["w_o_ssm"], False)])
    w_glu = w_glu.reshape(SSM_WIDTH, SSM_WIDTH)
    y, y_ssm, st, wq, wkv, w_o_mla, w_out = _ssm_fwd(
        u, bblk, cblk, lam, d_row, w_glu, small["b_glu"], w_o_ssm,
        xch=[(wq_mine, False), (bf["w_kv_b"], False), (bf["w_o_mla"], False), (bf["w_out"], False)])
    w_o_mla, w_out = w_o_mla.reshape(D_MODEL, D_MODEL), w_out.reshape(D_MODEL, D_MODEL)
    wq = jnp.transpose(wq, (1, 0, 2)).reshape(Q_LORA, N_HEADS * QK_PAD)
    wkv = jnp.transpose(wkv, (1, 0, 2)).reshape(KV_LORA, N_HEADS * QK_PAD)
    q, k, v, kt, vt = _qkv_prep(ql, kvl, small["q_a_norm"], small["kv_a_norm"], wq, wkv, gq, gk, cos_t, sin_t)
    attn, lse, attn_t, w_up, w_down = _attn_fwd(q, k, vt, xch=[(bf["w_up"], False), (bf["w_down"], False)])
    h, y_mla, mixed_t = _merge(attn, gs, gm, y_ssm, x, w_o_mla, w_out)
    hn, dout, hn_t, loss = _mlp_fwd_loss(h, target, small["norm_mlp"], w_up, w_down)

    da, dh, dout_b, hid_t, d_norm_mlp = _mlp_bwd(dout, hn, h, small["norm_mlp"], w_up, w_down)
    p_w_down = _matmul_tn_shards(hid_t, dout_b, "dw_down", False, tm=1024, turned=True)
    p_w_up = _matmul_tn_shards(hn_t, da, "dw_up", True, turned=True)
    dgs, dgm, dy_ssm, dy_mla, dattn, delta = _merge_bwd(dh, gs, gm, y_ssm, y_mla, attn, w_out, w_o_mla)
    p_w_out = _matmul_tn_shards(mixed_t, dh, "dw_out", False, tm=1024, turned=True)
    p_w_o_mla = _matmul_tn_shards(attn_t, dy_mla, "dw_o_mla", False, turned=True)
    dq, dk, dv, l_w_up, l_w_down, l_w_out, l_w_o_mla = _attn_bwd(
        q, k, kt, v, lse, delta, dattn, xch=[(p_w_up, True), (p_w_down, True), (p_w_out, True), (p_w_o_mla, True)])
    dql, dkvl, d_q_a_norm, d_kv_a_norm, d_gq, d_gk, g_wq, g_wkv = _qkv_prep_bwd(
        ql, kvl, dq, dk, dv, small["q_a_norm"], small["kv_a_norm"], wq, wkv, gq, gk, cos_t, sin_t, xch=[])
    p_wq = jnp.transpose(g_wq.reshape(Q_LORA, N_HEADS, QK_PAD), (1, 0, 2)).astype(BF16)
    p_wkv = jnp.transpose(g_wkv.reshape(KV_LORA, N_HEADS, QK_PAD), (1, 0, 2)).astype(BF16)
    dy, d_b_glu, g_w_glu, g_w_o_ssm = _glu_bwd(dy_ssm, y, w_glu, small["b_glu"], w_o_ssm)
    p_w_o_ssm = jnp.transpose(g_w_o_ssm.reshape(SSM_WIDTH, N_DEV, OUT_SHARD), (1, 0, 2)).astype(BF16)
    p_w_glu = g_w_glu.reshape(N_DEV, SSM_WIDTH // N_DEV, SSM_WIDTH).astype(BF16)
    du, dlam, d_d, d_bblk, d_cblk_t, l_wq, l_wkv, l_w_glu, l_w_o_ssm = _ssm_bwd(
        u, dy, st, bblk, cblk, lam, d_row, xch=[(p_wq, True), (p_wkv, True), (p_w_glu, True), (p_w_o_ssm, True)])
    d_a_re, d_a_im, d_log_dt, d_bt_re, d_bt_im, d_c_re, d_c_im = _ssm_prep_bwd(
        a_re, a_im, log_dt, bt_re, bt_im, dlam, d_bblk, d_cblk_t)
    tr = lambda mat: jnp.transpose(mat.reshape(SSM_GROUPS, SSM_GROUP_CH, SSM_STATE), (0, 2, 1))
    g_small = {
        "q_a_norm": d_q_a_norm, "kv_a_norm": d_kv_a_norm, "q_norm": d_gq[:, :QK_HEAD], "k_norm": d_gk[:, :QK_HEAD],
        "ssm_a_re": d_a_re, "ssm_a_im": d_a_im, "ssm_log_dt": d_log_dt,
        "ssm_b_re": tr(d_bt_re), "ssm_b_im": tr(d_bt_im), "ssm_c_re": d_c_re, "ssm_c_im": d_c_im,
        "ssm_d": d_d, "b_glu": d_b_glu, "norm_mlp": d_norm_mlp,
    }
    rest = jnp.concatenate([_pack_small(g_small, SMALL[1:]), loss], axis=0)
    dx, d_norm_mix, g_w_in_pad, g_rest_all = _in_proj_bwd(
        (du, dql, dkvl, dgs, dgm), dh, x, xn_t, small["norm_mix"], w_in_pad, xch=[(rest, False)])
    parts = jnp.transpose(_unpad_in(g_w_in_pad).reshape(D_MODEL, N_DEV, IN_SHARD), (1, 0, 2))
    g_w_in_mine, g_first_all = _reduce_scatter(parts, _pack_small({SMALL[0]: d_norm_mix}, SMALL[:1]), "reduce_w_in")
    landed = {"w_q_b": l_wq[:, :, :QK_HEAD], "w_kv_b": l_wkv, "w_o_mla": l_w_o_mla, "w_glu": l_w_glu,
              "w_o_ssm": l_w_o_ssm, "w_out": l_w_out, "w_up": l_w_up, "w_down": l_w_down}
    return dx, landed, g_w_in_mine, g_first_all, g_rest_all


def kernel(x, positions, norm_mix, w_in, q_a_norm, kv_a_norm, w_q_b, w_kv_b, q_norm, k_norm, w_o_mla, ssm_a_re, ssm_a_im, ssm_log_dt, ssm_b_re, ssm_b_im, ssm_c_re, ssm_c_im, ssm_d, w_glu, b_glu, w_o_ssm, w_out, norm_mlp, w_up, w_down, loss_target, m_norm_mix, m_w_in, m_q_a_norm, m_kv_a_norm, m_w_q_b, m_w_kv_b, m_q_norm, m_k_norm, m_w_o_mla, m_ssm_a_re, m_ssm_a_im, m_ssm_log_dt, m_ssm_b_re, m_ssm_b_im, m_ssm_c_re, m_ssm_c_im, m_ssm_d, m_w_glu, m_b_glu, m_w_o_ssm, m_w_out, m_norm_mlp, m_w_up, m_w_down, v_norm_mix, v_w_in, v_q_a_norm, v_kv_a_norm, v_w_q_b, v_w_kv_b, v_q_norm, v_k_norm, v_w_o_mla, v_ssm_a_re, v_ssm_a_im, v_ssm_log_dt, v_ssm_b_re, v_ssm_b_im, v_ssm_c_re, v_ssm_c_im, v_ssm_d, v_w_glu, v_b_glu, v_w_o_ssm, v_w_out, v_norm_mlp, v_w_up, v_w_down):
    given = dict(locals())
    w = {n: given[n] for n in WEIGHT_ORDER}
    m = {n: given["m_" + n] for n in WEIGHT_ORDER}
    v = {n: given["v_" + n] for n in WEIGHT_ORDER}
    big = [n for n in WEIGHT_ORDER if n not in SMALL]
    small = {n: w[n] for n in SMALL}

    dx, landed, g_w_in, g_first_all, g_rest_all = _step(
        x[0], positions.reshape(-1, 1), loss_target[0], {n: w[n][0] for n in big}, small)

    grads, deltas, new_m, new_v = {}, {}, {}, {}
    for n in big:
        if n in ("w_in", "w_q_b"):
            wt, mt, vt = jnp.transpose(w[n][0]), jnp.transpose(m[n][0]), jnp.transpose(v[n][0])
            if n == "w_in":
                g = jnp.transpose(g_w_in)
                d, nm, nv = _adamw(wt, g, mt, vt, "adamw_" + n)
            else:
                g, d, nm, nv = _adamw_sum(jnp.transpose(landed[n], (0, 2, 1)), wt, mt, vt, "adamw_" + n)
            g, d, nm, nv = (jnp.transpose(a) for a in (g, d, nm, nv))
        else:
            g, d, nm, nv = _adamw_sum(landed[n], w[n][0], m[n][0], v[n][0], "adamw_" + n)
        grads[n], deltas[n], new_m[n], new_v[n] = g[None], d[None], nm[None], nv[None]

    outs = _adamw_small(g_first_all, g_rest_all, _pack_small(small), _pack_small({n: m[n] for n in SMALL}),
                        _pack_small({n: v[n] for n in SMALL}), _small_rows(small))
    for p, n in enumerate(SMALL):
        for k, dst in enumerate((grads, deltas, new_m, new_v)):
            dst[n] = outs[1 + 4 * p + k].reshape(-1)[:small[n].size].reshape(small[n].shape)

    return (outs[0][0, 0], dx[None], *[grads[n] for n in WEIGHT_ORDER], *[deltas[n] for n in WEIGHT_ORDER],
            *[new_m[n] for n in WEIGHT_ORDER], *[new_v[n] for n in WEIGHT_ORDER])
```

```python
import functools
import math

import numpy as np
import jax
import jax.numpy as jnp
from jax import lax
from jax.experimental import pallas as pl
from jax.experimental.pallas import tpu as pltpu

F32 = jnp.float32
BF16 = jnp.bfloat16

D_MODEL = 1024
SSM_GROUPS = 32
SSM_GROUP_CH = 16
SSM_WIDTH = 512
SSM_STATE = 64
N_STATE = SSM_GROUPS * SSM_STATE
N_HEADS = 8
QK_NOPE = 128
QK_ROPE = 64
QK_HEAD = 192
QK_PAD = 256
V_HEAD = 128
Q_LORA = 384
KV_LORA = 256
KV_LAT_PAD = 384
ROPE_THETA = 10000.0
D_FF = 4096
EPS = 1e-6
ATT_SCALE = QK_HEAD ** -0.5
N_DEV = 8
FF_SHARD = D_FF // N_DEV
OUT_SHARD = D_MODEL // N_DEV

IN_SEGS = ((0, 512), (512, 896), (896, 1280), (1280, 2304), (2304, 3328))
D_IN = 3264
D_IN_PAD = 3328
KV_END = 1216

ADAM_LR = 0.001
ADAM_B1 = 0.9
ADAM_B2 = 0.999
ADAM_EPS = 1e-08
ADAM_WD = 0.01
ADAM_STEP = 10

VMEM_LIMIT = 56 * 1024 * 1024
MESH = pl.DeviceIdType.MESH
ANY = pl.BlockSpec(memory_space=pl.ANY)
LANES = 128

SCAN_T = 256
SUBCHUNKS = 8
SCAN_CG = 2048
ATT_T = 512
ATT_SUB = 1
ATT_HEADS = 4
ATT_BWD_HEADS = 2
ROW_T = 256
MM_T = 512


def _params(sem=None):
    return pltpu.CompilerParams(dimension_semantics=sem, vmem_limit_bytes=VMEM_LIMIT)


def _rows(arr, tm):
    if arr.ndim == 2:
        return pl.BlockSpec((tm, arr.shape[1]), lambda i: (i, 0))
    return pl.BlockSpec((arr.shape[0], tm, arr.shape[2]), lambda i: (0, i, 0))


def _const(arr):
    nd = arr.ndim
    return pl.BlockSpec(arr.shape, lambda *_: (0,) * nd, pipeline_mode=pl.Buffered(1))


def _sds(shape, dtype):
    return jax.ShapeDtypeStruct(shape, dtype)


PEERS = tuple((dx, dy, dc) for dx in (0, 1) for dy in (0, 1) for dc in (0, 1) if (dx, dy, dc) != (0, 0, 0))


def _here():
    x, y, c = lax.axis_index("x"), lax.axis_index("y"), lax.axis_index("c")
    return x, y, c, 4 * x + 2 * y + c


def _xchg_start(scatter, srcs, dsts, send, recv, local):
    x, y, c, me = _here()
    for e, sc in enumerate(scatter):
        src, dst = srcs[e], dsts[e]
        pltpu.make_async_copy(src.at[me] if sc else src, dst.at[me], local.at[e]).start()
        for dx, dy, dc in PEERS:
            px, py, pc = (1 - x if dx else x), (1 - y if dy else y), (1 - c if dc else c)
            pltpu.make_async_remote_copy(
                src_ref=src.at[4 * px + 2 * py + pc] if sc else src, dst_ref=dst.at[me],
                send_sem=send.at[e], recv_sem=recv.at[e], device_id=(px, py, pc), device_id_type=MESH).start()


def _xchg_wait(scatter, srcs, dsts, send, recv, local):
    x, y, c, me = _here()
    for e, sc in enumerate(scatter):
        src, dst = srcs[e], dsts[e]
        pltpu.make_async_copy(src.at[me] if sc else src, dst.at[me], local.at[e]).wait()
        span = dst.at[pl.ds(0, N_DEV - 1)]
        both = pltpu.make_async_remote_copy(src_ref=span, dst_ref=span, send_sem=send.at[e], recv_sem=recv.at[e],
                                            device_id=(x, y, c), device_id_type=MESH)
        both.wait_send()
        both.wait_recv()


def _call(body, name, grid, ins, in_specs, outs, out_specs, scratch=(), xch=()):
    n_in, n_out, ne = len(ins), len(outs), len(xch)
    scatter = [sc for _, sc in xch]
    x_outs = [_sds((N_DEV,) + (a.shape[1:] if sc else a.shape), a.dtype) for a, sc in xch]
    sems = [pltpu.SemaphoreType.DMA((ne,))] * 3 if ne else []

    def wrapped(*refs):
        in_refs, x_src = refs[:n_in], refs[n_in:n_in + ne]
        out_refs = refs[n_in + ne:n_in + ne + n_out]
        x_dst = refs[n_in + ne + n_out:n_in + 2 * ne + n_out]
        rest = refs[n_in + 2 * ne + n_out:]
        if ne:
            x_sems, rest = rest[len(rest) - 3:], rest[:len(rest) - 3]
            first = functools.reduce(jnp.logical_and, [pl.program_id(d) == 0 for d in range(len(grid))])
            last = functools.reduce(jnp.logical_and, [pl.program_id(d) == grid[d] - 1 for d in range(len(grid))])

            @pl.when(first)
            def _():
                _xchg_start(scatter, x_src, x_dst, *x_sems)

        body(*in_refs, *out_refs, *rest)
        if ne:
            @pl.when(last)
            def _():
                _xchg_wait(scatter, x_src, x_dst, *x_sems)

    return pl.pallas_call(
        wrapped,
        name=name,
        grid=grid,
        in_specs=list(in_specs) + [ANY] * ne,
        out_specs=list(out_specs) + [ANY] * ne,
        out_shape=list(outs) + x_outs,
        scratch_shapes=list(scratch) + sems,
        compiler_params=_params(("arbitrary",) * len(grid)),
    )(*ins, *[a for a, _ in xch])


def _row_call(body, name, n_rows, tm, row_ins, const_ins, row_outs, acc_outs=(), xch=(), col_outs=(), scratch=()):
    outs = [_sds(s, d) for s, d in list(row_outs) + list(col_outs) + list(acc_outs)]
    n_row, n_col = len(row_outs), len(col_outs)
    out_specs = [_rows(o, tm) for o in outs[:n_row]] + [
        pl.BlockSpec(o.shape[:-1] + (tm,), lambda i, nd=len(o.shape): (0,) * (nd - 1) + (i,))
        for o in outs[n_row:n_row + n_col]] + [
        pl.BlockSpec(o.shape, lambda i, nd=len(o.shape): (0,) * nd) for o in outs[n_row + n_col:]]
    in_specs = [_rows(a, tm) for a in row_ins] + [_const(a) for a in const_ins]
    return _call(body, name, (n_rows // tm,), list(row_ins) + list(const_ins), in_specs, outs, out_specs,
                 scratch=scratch, xch=xch)


def _dot(a, b):
    return jnp.dot(a, b, preferred_element_type=F32)


def _dot_nt(a, b):
    return lax.dot_general(a, b, (((1,), (1,)), ((), ())), preferred_element_type=F32)


def _dot_tn(a, b):
    return lax.dot_general(a, b, (((0,), (0,)), ((), ())), preferred_element_type=F32)


def _rms(x, g, n):
    inv = lax.rsqrt(jnp.sum(x * x, -1, keepdims=True) * (1.0 / n) + EPS)
    return x * inv * g, inv


def _rms_bwd(dy, x, g, inv, n):
    xh = x * inv
    dxh = dy * g
    dx = inv * (dxh - xh * (jnp.sum(dxh * xh, -1, keepdims=True) * (1.0 / n)))
    return dx, dy * xh


def _sigmoid(x):
    return 1.0 / (1.0 + jnp.exp(-x))


_GELU_C = math.sqrt(2.0 / math.pi)


def _gelu(y):
    th = jnp.tanh(_GELU_C * (y + 0.044715 * (y * y * y)))
    return 0.5 * y * (1.0 + th), th


def _gelu_grad(y, th):
    return 0.5 * (1.0 + th) + 0.5 * y * (1.0 - th * th) * (_GELU_C * (1.0 + 3.0 * 0.044715 * (y * y)))


def _acc(ref, val):
    @pl.when(pl.program_id(0) == 0)
    def _():
        ref[...] = jnp.zeros_like(ref)

    ref[...] += val


def _tile(n, limit):
    if n <= limit:
        return n
    return max(t for t in range(128, limit + 1, 128) if n % t == 0)


def _lhs(a, turned, tm, tk):
    m, k_dim = a.shape if turned else a.shape[::-1]
    tm, tk = _tile(m, tm), _tile(k_dim, tk)
    if turned:
        return m, k_dim, tm, tk, pl.BlockSpec((tm, tk), lambda i, k: (i, k)), _dot
    return m, k_dim, tm, tk, pl.BlockSpec((tk, tm), lambda i, k: (k, i)), _dot_tn


def _matmul_tn_shards(a, b, name, by_col, tm=512, tk=512, turned=False):
    m, k_dim, tm, tk, a_spec, dot = _lhs(a, turned, tm, tk)
    n = b.shape[1]
    nk = k_dim // tk
    if by_col:
        r, c = m, n // N_DEV
        out_spec = pl.BlockSpec((N_DEV, tm, c), lambda i, k: (0, i, 0))
    else:
        r, c = m // N_DEV, n
        per = tm // r
        out_spec = pl.BlockSpec((per, r, c), lambda i, k: (i, 0, 0))

    def body(a_ref, b_ref, o_ref, acc_ref):
        k = pl.program_id(1)

        @pl.when(k == 0)
        def _():
            acc_ref[...] = jnp.zeros_like(acc_ref)

        acc_ref[...] += dot(a_ref[...].astype(BF16), b_ref[...].astype(BF16))

        @pl.when(k == nk - 1)
        def _():
            if by_col:
                for j in range(N_DEV):
                    o_ref[j] = acc_ref[:, j * c:(j + 1) * c].astype(BF16)
            else:
                for s in range(per):
                    o_ref[s] = acc_ref[s * r:(s + 1) * r, :].astype(BF16)

    return pl.pallas_call(
        body,
        name=name,
        grid=(m // tm, nk),
        in_specs=[a_spec, pl.BlockSpec((tk, n), lambda i, k: (k, 0))],
        out_specs=out_spec,
        out_shape=_sds((N_DEV, r, c), BF16),
        scratch_shapes=[pltpu.VMEM((tm, n), F32)],
        compiler_params=_params(("parallel", "arbitrary")),
    )(a, b)


def _rope_tables(pos_col):
    n = pos_col.shape[0]
    half = QK_ROPE // 2
    inv_freq = (ROPE_THETA ** (-np.arange(half, dtype=np.float32) / half)).astype(np.float32)
    freq_row = jnp.asarray(np.concatenate([inv_freq, inv_freq, np.zeros(64, np.float32)])[None, :])

    def body(p_ref, f_ref, c_ref, s_ref):
        ang = p_ref[...].astype(F32) * f_ref[...]
        c_ref[...] = jnp.cos(ang)
        s_ref[...] = jnp.sin(ang)

    return _row_call(body, "rope_tables", n, min(n, 1024), [pos_col], [freq_row], [((n, 128), F32)] * 2)


def _rope_rot(v):
    lane = lax.broadcasted_iota(jnp.int32, v.shape, 1)
    return jnp.where(lane < 32, -pltpu.roll(v, 96, 1), jnp.where(lane < 64, pltpu.roll(v, 32, 1), 0.0))


def _rope_rot_t(v):
    lane = lax.broadcasted_iota(jnp.int32, v.shape, 1)
    return jnp.where(lane < 32, pltpu.roll(v, 96, 1), jnp.where(lane < 64, -pltpu.roll(v, 32, 1), 0.0))


def _in_proj(x, norm_mix, w_in_pad, xch):
    n = x.shape[0]

    def body(x_ref, g_ref, w_ref, u_ref, ql_ref, kvl_ref, gs_ref, gm_ref, xnt_ref):
        xn, _ = _rms(x_ref[...], g_ref[...], D_MODEL)
        xb = xn.astype(BF16)
        xnt_ref[...] = xn.T.astype(BF16)
        for ref, (a, b) in zip((u_ref, ql_ref, kvl_ref, gs_ref, gm_ref), IN_SEGS):
            ref[...] = _dot(xb, w_ref[:, a:b])

    outs = [((n, b - a), F32) for a, b in IN_SEGS]
    return _row_call(body, "in_proj", n, MM_T, [x], [norm_mix, w_in_pad], outs, xch=xch,
                     col_outs=[((D_MODEL, n), BF16)])


def _ssm_prep_fn(a_re, a_im, log_dt, b_re_x, b_im_x):
    dt = jnp.exp(log_dt)
    mag = jnp.exp(a_re * dt)
    lr = mag * jnp.cos(a_im * dt)
    li = mag * jnp.sin(a_im * dt)
    den = a_re * a_re + a_im * a_im
    fr = ((lr - 1.0) * a_re + li * a_im) / den
    fi = (li * a_re - (lr - 1.0) * a_im) / den
    return lr, li, fr * b_re_x - fi * b_im_x, fr * b_im_x + fi * b_re_x


def _dot_exact(a, b, dims):
    return lax.dot_general(a, b, (dims, ((), ())), precision=lax.Precision.HIGHEST, preferred_element_type=F32)


def _lane_repeat(width, n):
    src = lax.broadcasted_iota(jnp.int32, (width, n), 0)
    dst = lax.broadcasted_iota(jnp.int32, (width, n), 1)
    return (dst % width == src).astype(F32)


def _same_group(rows, rows_per_group, cols, cols_per_group):
    row = lax.broadcasted_iota(jnp.int32, (rows, cols), 0)
    col = lax.broadcasted_iota(jnp.int32, (rows, cols), 1)
    return (row // rows_per_group) == (col // cols_per_group)


def _expand_b(bt):
    tiled = _dot_exact(bt, _lane_repeat(SSM_STATE, N_STATE), ((1,), (0,)))
    return jnp.where(_same_group(SSM_WIDTH, SSM_GROUP_CH, N_STATE, SSM_STATE), tiled, 0.0)


def _collect_b(m):
    masked = jnp.where(_same_group(SSM_WIDTH, SSM_GROUP_CH, N_STATE, SSM_STATE), m, 0.0)
    return _dot_exact(masked, _lane_repeat(SSM_STATE, N_STATE), ((1,), (1,)))


def _ssm_prep(a_re, a_im, log_dt, bt_re, bt_im, c2_re, c2_im):
    def body(ar, ai, ld, br, bi, cr, ci, lam_ref, bblk_ref, cblk_ref):
        lr, li, bbr, bbi = _ssm_prep_fn(ar[...], ai[...], ld[...], _expand_b(br[...]), _expand_b(bi[...]))
        lam_ref[0:1, :] = lr
        lam_ref[1:2, :] = li
        bblk_ref[:, 0:N_STATE] = bbr.astype(BF16)
        bblk_ref[:, N_STATE:] = bbi.astype(BF16)
        rep = _lane_repeat(SSM_GROUP_CH, SSM_WIDTH)
        own = _same_group(N_STATE, SSM_STATE, SSM_WIDTH, SSM_GROUP_CH)
        cblk_ref[0:N_STATE, :] = jnp.where(own, _dot_exact(cr[...], rep, ((1,), (0,))), 0.0).astype(BF16)
        cblk_ref[N_STATE:, :] = jnp.where(own, -_dot_exact(ci[...], rep, ((1,), (0,))), 0.0).astype(BF16)

    return pl.pallas_call(
        body,
        name="ssm_prep",
        out_shape=[_sds((2, N_STATE), F32), _sds((SSM_WIDTH, 2 * N_STATE), BF16),
                   _sds((2 * N_STATE, SSM_WIDTH), BF16)],
        compiler_params=_params(),
    )(a_re, a_im, log_dt, bt_re, bt_im, c2_re, c2_im)


def _ssm_prep_bwd(a_re, a_im, log_dt, bt_re, bt_im, dlam, dbblk, dcblk_t):
    def body(ar, ai, ld, br, bi, dl, db, dc, dar, dai, dld, dbr, dbi, dcr, dci):
        _, vjp = jax.vjp(_ssm_prep_fn, ar[...], ai[...], ld[...], _expand_b(br[...]), _expand_b(bi[...]))
        g = vjp((dl[0:1, :], dl[1:2, :], db[:, 0:N_STATE], db[:, N_STATE:]))
        dar[...] = g[0]
        dai[...] = g[1]
        grp = lax.broadcasted_iota(jnp.int32, (SSM_GROUPS, N_STATE), 0)
        lane = lax.broadcasted_iota(jnp.int32, (SSM_GROUPS, N_STATE), 1)
        sel = (lane // SSM_STATE) == grp
        dld[...] = jnp.sum(jnp.where(sel, jnp.broadcast_to(g[2], (SSM_GROUPS, N_STATE)), 0.0), axis=1, keepdims=True)
        dbr[...] = _collect_b(g[3])
        dbi[...] = _collect_b(g[4])
        dcr[...] = _collect_b(dc[:, 0:N_STATE])
        dci[...] = -_collect_b(dc[:, N_STATE:])

    small = _sds((SSM_WIDTH, SSM_STATE), F32)
    return pl.pallas_call(
        body,
        name="ssm_prep_bwd",
        out_shape=[_sds((1, N_STATE), F32), _sds((1, N_STATE), F32), _sds((SSM_GROUPS, 1), F32), small, small, small, small],
        compiler_params=_params(),
    )(a_re, a_im, log_dt, bt_re, bt_im, dlam, dbblk, dcblk_t)


def _perm_matrix(t):
    run = t // SUBCHUNKS
    p = np.zeros((t, t), np.float32)
    r = np.arange(t)
    p[r, (r % SUBCHUNKS) * run + r // SUBCHUNKS] = 1.0
    return jnp.asarray(p, dtype=BF16)


def _unpermute(p, a):
    hi = a.astype(BF16)
    r1 = a - hi.astype(F32)
    mid = r1.astype(BF16)
    lo = (r1 - mid.astype(F32)).astype(BF16)
    return _dot_tn(p, hi) + _dot_tn(p, mid) + _dot_tn(p, lo)


def _power_table(lam_ref, pw_ref, n):
    lr, li = lam_ref[0:1, :], lam_ref[1:2, :]
    pw_ref[0:1, 0:N_STATE] = lr
    pw_ref[0:1, N_STATE:] = li

    def step(i, carry):
        pr, pi = carry
        pr, pi = pr * lr - pi * li, pr * li + pi * lr
        pw_ref[pl.ds(i, 1), 0:N_STATE] = pr
        pw_ref[pl.ds(i, 1), N_STATE:] = pi
        return pr, pi

    lax.fori_loop(1, n, step, (lr, li))


def _col_groups():
    return [(pl.ds(c, SCAN_CG), pl.ds(N_STATE + c, SCAN_CG)) for c in range(0, N_STATE, SCAN_CG)]


def _run_scan(buf, lam_ref, t, reverse):
    nblk = t // 8
    for re, im in _col_groups():
        lr = jnp.broadcast_to(lam_ref[0:1, re], (8, SCAN_CG))
        li = jnp.broadcast_to(lam_ref[1:2, re], (8, SCAN_CG))
        if reverse:
            li = -li
        first = pl.ds((nblk - 1) * 8 if reverse else 0, 8)

        def step(k, carry, re=re, im=im, lr=lr, li=li):
            pr, pi = carry
            i = (nblk - 2 - k) if reverse else (k + 1)
            r = pl.ds(pl.multiple_of(i * 8, 8), 8)
            xr = buf[r, re] + lr * pr - li * pi
            xi = buf[r, im] + lr * pi + li * pr
            buf[r, re] = xr
            buf[r, im] = xi
            return xr, xi

        lax.fori_loop(0, nblk - 1, step, (buf[first, re], buf[first, im]))


def _run_carries(buf, pw_ref, carry_ref, s_ref, t, reverse):
    nblk = t // 8
    run = t // SUBCHUNKS
    edge = buf[pl.ds(0 if reverse else (nblk - 1) * 8, 8), :]
    pr, pi = pw_ref[run - 1:run, 0:N_STATE], pw_ref[run - 1:run, N_STATE:]
    if reverse:
        pi = -pi
    sr, si = carry_ref[0:1, 0:N_STATE], carry_ref[0:1, N_STATE:]
    for s in (range(SUBCHUNKS - 1, -1, -1) if reverse else range(SUBCHUNKS)):
        s_ref[s:s + 1, 0:N_STATE] = sr
        s_ref[s:s + 1, N_STATE:] = si
        er, ei = edge[s:s + 1, 0:N_STATE], edge[s:s + 1, N_STATE:]
        sr, si = er + pr * sr - pi * si, ei + pr * si + pi * sr
    carry_ref[:, 0:N_STATE] = jnp.broadcast_to(sr, (8, N_STATE))
    carry_ref[:, N_STATE:] = jnp.broadcast_to(si, (8, N_STATE))


def _run_fix(buf, pw_ref, s_ref, t, reverse):
    nblk = t // 8
    for re, im in _col_groups():
        sr, si = s_ref[:, re], s_ref[:, im]

        def step(i, carry, re=re, im=im, sr=sr, si=si):
            r = pl.ds(pl.multiple_of(i * 8, 8), 8)
            row = pl.ds((nblk - 1 - i) if reverse else i, 1)
            pr, pi = pw_ref[row, re], pw_ref[row, im]
            if reverse:
                pi = -pi
            buf[r, re] += pr * sr - pi * si
            buf[r, im] += pr * si + pi * sr
            return carry

        lax.fori_loop(0, nblk, step, 0)


STATE_BLOCKS = 2 * N_STATE // LANES
CH_BLOCKS = SSM_WIDTH // LANES


def _state_block(b):
    pair = b % (N_STATE // LANES)
    k = (pair * 2 * SSM_GROUP_CH) // LANES
    return slice(b * LANES, (b + 1) * LANES), slice(k * LANES, (k + 1) * LANES)


def _channel_block(c):
    w = N_STATE // CH_BLOCKS
    return slice(c * LANES, (c + 1) * LANES), slice(c * w, (c + 1) * w), slice(N_STATE + c * w, N_STATE + (c + 1) * w)


def _to_states(vb, w_ref, buf, nt):
    for b in range(STATE_BLOCKS):
        lanes, ch = _state_block(b)
        buf[:, lanes] = _dot_nt(vb[:, ch], w_ref[lanes, ch]) if nt else _dot(vb[:, ch], w_ref[ch, lanes])


def _to_channels(buf, w_ref, nt):
    outs = []
    for c in range(CH_BLOCKS):
        ch, re, im = _channel_block(c)
        xr, xi = buf[:, re].astype(BF16), buf[:, im].astype(BF16)
        if nt:
            outs.append(_dot_nt(xr, w_ref[ch, re]) + _dot_nt(xi, w_ref[ch, im]))
        else:
            outs.append(_dot(xr, w_ref[re, ch]) + _dot(xi, w_ref[im, ch]))
    return jnp.concatenate(outs, axis=-1)


def _ssm_fwd(u, bblk, cblk, lam, d_row, w_glu, b_glu, w_o_ssm, xch):
    n = u.shape[0]
    t = min(SCAN_T, n)
    perm = _perm_matrix(t)

    def body(u_ref, p_ref, bblk_ref, cblk_ref, lam_ref, d_ref, wg_ref, bg_ref, wo_ref, y_ref, ys_ref, st_ref,
             buf, pw_ref, carry_ref, s_ref):
        @pl.when(pl.program_id(0) == 0)
        def _():
            carry_ref[...] = jnp.zeros_like(carry_ref)
            _power_table(lam_ref, pw_ref, t // SUBCHUNKS)

        st_ref[0] = carry_ref[...]
        u_t = u_ref[...]
        p = p_ref[...]
        ub = _dot(p, u_t.astype(BF16)).astype(BF16)
        _to_states(ub, bblk_ref, buf, False)
        _run_scan(buf, lam_ref, t, False)
        _run_carries(buf, pw_ref, carry_ref, s_ref, t, False)
        _run_fix(buf, pw_ref, s_ref, t, False)
        y = d_ref[...] * u_t + _unpermute(p, _to_channels(buf, cblk_ref, False))
        y_ref[...] = y
        z, _ = _gelu(y)
        s = _sigmoid(_dot(z.astype(BF16), wg_ref[...]) + bg_ref[...])
        zgb = (z * s).astype(BF16)
        for j in range(N_DEV):
            ys_ref[:, j * OUT_SHARD:(j + 1) * OUT_SHARD] = _dot(zgb, wo_ref[j])

    consts = [perm, bblk, cblk, lam, d_row, w_glu, b_glu, w_o_ssm]
    return _call(
        body, "ssm_fwd", (n // t,), [u] + consts, [_rows(u, t)] + [_const(a) for a in consts],
        [_sds((n, SSM_WIDTH), F32), _sds((n, D_MODEL), F32), _sds((n // t, 8, 2 * N_STATE), F32)],
        [pl.BlockSpec((t, SSM_WIDTH), lambda i: (i, 0)), pl.BlockSpec((t, D_MODEL), lambda i: (i, 0)),
         pl.BlockSpec((1, 8, 2 * N_STATE), lambda i: (i, 0, 0))],
        scratch=[pltpu.VMEM((t, 2 * N_STATE), F32), pltpu.VMEM((t // SUBCHUNKS, 2 * N_STATE), F32),
                 pltpu.VMEM((8, 2 * N_STATE), F32), pltpu.VMEM((8, 2 * N_STATE), F32)],
        xch=xch)


def _head_norm_rope(slab, gain, cos_t, sin_t):
    xn, inv = _rms(slab, gain, QK_HEAD)
    lo, hi = xn[:, 0:128], xn[:, 128:256]
    return jnp.concatenate([lo, hi * cos_t + _rope_rot(hi) * sin_t], axis=-1), inv


def _head_norm_rope_bwd(g, slab, gain, inv, cos_t, sin_t):
    g_lo, g_hi = g[:, 0:128], g[:, 128:256]
    g_n = jnp.concatenate([g_lo, g_hi * cos_t + _rope_rot_t(g_hi * sin_t)], axis=-1)
    return _rms_bwd(g_n, slab, gain, inv, QK_HEAD)


def _qkv_prep(ql, kvl, q_a_norm, kv_a_norm, wq, wkv, gq, gk, cos_t, sin_t):
    n = ql.shape[0]
    tm = ROW_T

    def body(ql_ref, kvl_ref, cos_ref, sin_ref, qa_ref, ka_ref, wq_ref, wkv_ref, gq_ref, gk_ref,
             q_ref, k_ref, v_ref, kt_ref, vt_ref):
        cos_t, sin_t = cos_ref[...], sin_ref[...]
        qa, _ = _rms(ql_ref[...], qa_ref[...], Q_LORA)
        qab = qa.astype(BF16)
        kvl_t = kvl_ref[...]
        ca, _ = _rms(kvl_t[:, 0:KV_LORA], ka_ref[...], KV_LORA)
        cab = ca.astype(BF16)
        kpe = kvl_t[:, KV_LORA:KV_LAT_PAD]
        q_pre = _dot(qab, wq_ref[...])
        kv_pre = _dot(cab, wkv_ref[...])
        for h in range(N_HEADS):
            qh, _ = _head_norm_rope(q_pre[:, h * QK_PAD:(h + 1) * QK_PAD], gq_ref[...], cos_t, sin_t)
            q_ref[h] = (qh * ATT_SCALE).astype(BF16)
            kv_h = kv_pre[:, h * QK_PAD:(h + 1) * QK_PAD]
            kh, _ = _head_norm_rope(jnp.concatenate([kv_h[:, 0:QK_NOPE], kpe], axis=-1), gk_ref[...], cos_t, sin_t)
            k_ref[h] = kh.astype(BF16)
            kt_ref[h] = kh.T.astype(BF16)
            vh = kv_h[:, QK_NOPE:]
            v_ref[h] = vh.astype(BF16)
            vt_ref[h] = vh.T.astype(BF16)

    row_ins, consts = [ql, kvl, cos_t, sin_t], [q_a_norm, kv_a_norm, wq, wkv, gq, gk]
    outs = [_sds((N_HEADS, n, QK_PAD), BF16), _sds((N_HEADS, n, QK_PAD), BF16), _sds((N_HEADS, n, V_HEAD), BF16),
            _sds((N_HEADS, QK_PAD, n), BF16), _sds((N_HEADS, V_HEAD, n), BF16)]
    out_specs = [_rows(o, tm) for o in outs[:3]] + [
        pl.BlockSpec((N_HEADS, QK_PAD, tm), lambda i: (0, 0, i)), pl.BlockSpec((N_HEADS, V_HEAD, tm), lambda i: (0, 0, i))]
    return _call(body, "qkv_prep", (n // tm,), row_ins + consts,
                 [_rows(a, tm) for a in row_ins] + [_const(a) for a in consts], outs, out_specs)


def _causal_mask_t(st, t):
    key = lax.broadcasted_iota(jnp.int32, (t, t), 0)
    qry = lax.broadcasted_iota(jnp.int32, (t, t), 1)
    return jnp.where(key <= qry, st, -jnp.inf)


def _attn_fwd(q, k, vt, xch):
    n = q.shape[1]
    t = min(ATT_T, n)

    hp = ATT_HEADS

    def body(q_ref, k_ref, vt_ref, o_ref, lse_ref, ot_ref):
        i = pl.program_id(1)
        qts = [q_ref[g] for g in range(hp)]

        def kv_tile(j, carry, diag):
            ts = t // ATT_SUB
            sts = []
            for g in range(hp):
                for a in range(ATT_SUB):
                    r0 = pl.multiple_of(j * t + a * ts, ts)
                    st = _dot_nt(k_ref[g, pl.ds(r0, ts), :], qts[g])
                    if diag:
                        key = lax.broadcasted_iota(jnp.int32, (ts, t), 0) + a * ts
                        qry = lax.broadcasted_iota(jnp.int32, (ts, t), 1)
                        st = jnp.where(key <= qry, st, -jnp.inf)
                    sts.append(st)
            out = []
            for g in range(hp):
                m, l, acc = carry[g]
                for a in range(ATT_SUB):
                    st = sts[g * ATT_SUB + a]
                    r0 = pl.multiple_of(j * t + a * ts, ts)
                    m_new = jnp.maximum(m, jnp.max(st, 0, keepdims=True))
                    alpha = jnp.exp(m - m_new)
                    pt = jnp.exp(st - m_new)
                    l = alpha * l + jnp.sum(pt, 0, keepdims=True)
                    acc = alpha * acc + _dot(vt_ref[g, :, pl.ds(r0, ts)], pt.astype(BF16))
                    m = m_new
                out.append((m, l, acc))
            return tuple(out)

        one = (jnp.full((1, t), -jnp.inf, F32), jnp.zeros((1, t), F32), jnp.zeros((V_HEAD, t), F32))
        carry = lax.fori_loop(0, i, functools.partial(kv_tile, diag=False), (one,) * hp)
        for g, (m, l, acc) in enumerate(kv_tile(i, carry, True)):
            out_t = acc / l
            o_ref[:, g * V_HEAD:(g + 1) * V_HEAD] = out_t.T
            ot_ref[g * V_HEAD:(g + 1) * V_HEAD, :] = out_t.astype(BF16)
            lse_ref[g] = m + jnp.log(l)

    return _call(
        body, "attn_fwd", (N_HEADS // hp, n // t), [q, k, vt],
        [pl.BlockSpec((hp, t, QK_PAD), lambda h, i: (h, i, 0)), pl.BlockSpec((hp, n, QK_PAD), lambda h, i: (h, 0, 0)),
         pl.BlockSpec((hp, V_HEAD, n), lambda h, i: (h, 0, 0))],
        [_sds((n, N_HEADS * V_HEAD), F32), _sds((N_HEADS, 1, n), F32), _sds((N_HEADS * V_HEAD, n), BF16)],
        [pl.BlockSpec((t, hp * V_HEAD), lambda h, i: (i, h)), pl.BlockSpec((hp, 1, t), lambda h, i: (h, 0, i)),
         pl.BlockSpec((hp * V_HEAD, t), lambda h, i: (h, i))],
        xch=xch)


def _merge(attn, gs, gm, y_ssm, x, w_o_mla, w_out):
    n = x.shape[0]

    def body(at_ref, gs_ref, gm_ref, ys_ref, x_ref, wo_ref, wout_ref, h_ref, ym_ref, mxt_ref):
        y_mla = _dot(at_ref[...].astype(BF16), wo_ref[...])
        ym_ref[...] = y_mla
        mixed = _sigmoid(gs_ref[...]) * ys_ref[...] + _sigmoid(gm_ref[...]) * y_mla
        mxt_ref[...] = mixed.T.astype(BF16)
        h_ref[...] = x_ref[...] + _dot(mixed.astype(BF16), wout_ref[...])

    outs = [((n, D_MODEL), F32), ((n, D_MODEL), F32)]
    return _row_call(body, "merge", n, MM_T, [attn, gs, gm, y_ssm, x], [w_o_mla, w_out], outs,
                     col_outs=[((D_MODEL, n), BF16)])


def _mlp_fwd_loss(h, target, norm_mlp, w_up, w_down):
    n = h.shape[0]

    def body(h_ref, t_ref, g_ref, wu_ref, wd_ref, hn_ref, do_ref, hnt_ref, loss_ref):
        h_t = h_ref[...]
        hn, _ = _rms(h_t, g_ref[...], D_MODEL)
        hb = hn.astype(BF16)
        hn_ref[...] = hb
        hnt_ref[...] = hn.T.astype(BF16)
        out = h_t
        for j in range(N_DEV):
            a = jnp.maximum(_dot(hb, wu_ref[j]), 0.0)
            out += _dot((a * a).astype(BF16), wd_ref[j])
        err = out - t_ref[...]
        do_ref[...] = err * (1.0 / D_MODEL)
        _acc(loss_ref, jnp.broadcast_to(jnp.sum(err * err) * (0.5 / D_MODEL), loss_ref.shape))

    outs = [((n, D_MODEL), BF16), ((n, D_MODEL), F32)]
    return _row_call(body, "mlp_fwd_loss", n, MM_T, [h, target], [norm_mlp, w_up, w_down], outs, [((8, 128), F32)],
                     col_outs=[((D_MODEL, n), BF16)])


def _mlp_bwd(dout, hn, h, norm_mlp, w_up, w_down):
    n = h.shape[0]

    def body(do_ref, hn_ref, h_ref, g_ref, wu_ref, wd_ref, da_ref, dh_ref, dob_ref, hidt_ref, dg_ref):
        dout_t = do_ref[...]
        doutb = dout_t.astype(BF16)
        dob_ref[...] = doutb
        hb = hn_ref[...]
        dhn = jnp.zeros_like(dout_t)
        for j in range(N_DEV):
            cols = slice(j * FF_SHARD, (j + 1) * FF_SHARD)
            a = jnp.maximum(_dot(hb, wu_ref[j]), 0.0)
            hidt_ref[cols, :] = (a * a).T.astype(BF16)
            da = (_dot_nt(doutb, wd_ref[j]) * (2.0 * a)).astype(BF16)
            da_ref[:, cols] = da
            dhn += _dot_nt(da, wu_ref[j])
        h_t = h_ref[...]
        inv = lax.rsqrt(jnp.sum(h_t * h_t, -1, keepdims=True) * (1.0 / D_MODEL) + EPS)
        dx, dg = _rms_bwd(dhn, h_t, g_ref[...], inv, D_MODEL)
        dh_ref[...] = dout_t + dx
        _acc(dg_ref, jnp.sum(dg, 0, keepdims=True))

    outs = [((n, D_FF), BF16), ((n, D_MODEL), F32), ((n, D_MODEL), BF16)]
    return _row_call(body, "mlp_bwd", n, MM_T, [dout, hn, h], [norm_mlp, w_up, w_down], outs, [((1, D_MODEL), F32)],
                     col_outs=[((D_FF, n), BF16)])


def _merge_bwd(dh, gs, gm, y_ssm, y_mla, attn, w_out, w_o_mla):
    n = dh.shape[0]

    def body(dh_ref, gs_ref, gm_ref, ys_ref, ym_ref, at_ref, wout_ref, wo_ref,
             dgs_ref, dgm_ref, dys_ref, dym_ref, dat_ref, delta_ref):
        dmix = _dot_nt(dh_ref[...].astype(BF16), wout_ref[...])
        sgs, sgm = _sigmoid(gs_ref[...]), _sigmoid(gm_ref[...])
        dgs_ref[...] = (dmix * ys_ref[...] * sgs * (1.0 - sgs)).astype(BF16)
        dgm_ref[...] = (dmix * ym_ref[...] * sgm * (1.0 - sgm)).astype(BF16)
        dys_ref[...] = (dmix * sgs).astype(BF16)
        dym = (dmix * sgm).astype(BF16)
        dym_ref[...] = dym
        dattn = _dot_nt(dym, wo_ref[...])
        dat_ref[...] = dattn.astype(BF16)
        prod = dattn * at_ref[...]
        ones = jnp.ones((8, V_HEAD), F32)
        for h in range(N_HEADS):
            delta_ref[h] = _dot_exact(ones, prod[:, h * V_HEAD:(h + 1) * V_HEAD], ((1,), (1,)))[0:1, :]

    outs = [((n, D_MODEL), BF16)] * 5
    return _row_call(body, "merge_bwd", n, MM_T, [dh, gs, gm, y_ssm, y_mla, attn], [w_out, w_o_mla], outs,
                     col_outs=[((N_HEADS, 1, n), F32)])


def _attn_bwd(q, k, kt, v, lse, delta, dout, xch):
    n = q.shape[1]
    t = min(ATT_T, n)
    nt = n // t
    hp = ATT_BWD_HEADS

    def body(q_ref, k_ref, kt_ref, v_ref, lse_ref, delta_ref, do_ref, dq_ref, dk_ref, dv_ref, dqt_ref):
        j = pl.program_id(1)

        @pl.when(j == 0)
        def _():
            dqt_ref[...] = jnp.zeros_like(dqt_ref)

        def q_tile(i, carry, diag):
            r0 = pl.multiple_of(i * t, t)
            rows = pl.ds(r0, t)
            qts = [q_ref[g, rows, :] for g in range(hp)]
            sts = [_dot_nt(k_ref[g], qts[g]) for g in range(hp)]
            out = []
            for g in range(hp):
                dk, dv = carry[g]
                st = _causal_mask_t(sts[g], t) if diag else sts[g]
                pt = jnp.exp(st - lse_ref[g, :, rows])
                dob = do_ref[rows, g * V_HEAD:(g + 1) * V_HEAD]
                dv = dv + _dot(pt.astype(BF16), dob)
                dst = (pt * (_dot_nt(v_ref[g], dob) - delta_ref[g, :, rows])).astype(BF16)
                dk = dk + _dot(dst, qts[g])
                dqt_ref[g, :, rows] += _dot(kt_ref[g], dst)
                out.append((dk, dv))
            return tuple(out)

        zero = (jnp.zeros((t, QK_PAD), F32), jnp.zeros((t, V_HEAD), F32))
        carry = q_tile(j, (zero,) * hp, True)
        carry = lax.fori_loop(j + 1, nt, functools.partial(q_tile, diag=False), carry)
        for g, (dk, dv) in enumerate(carry):
            dk_ref[g] = dk
            dv_ref[g] = dv

        @pl.when(j == nt - 1)
        def _():
            for g in range(hp):
                for c in range(0, n, t):
                    dq_ref[g, c:c + t, :] = dqt_ref[g, :, c:c + t].T

    return _call(
        body, "attn_bwd", (N_HEADS // hp, nt), [q, k, kt, v, lse, delta, dout],
        [pl.BlockSpec((hp, n, QK_PAD), lambda h, j: (h, 0, 0)), pl.BlockSpec((hp, t, QK_PAD), lambda h, j: (h, j, 0)),
         pl.BlockSpec((hp, QK_PAD, t), lambda h, j: (h, 0, j)), pl.BlockSpec((hp, t, V_HEAD), lambda h, j: (h, j, 0)),
         pl.BlockSpec((hp, 1, n), lambda h, j: (h, 0, 0)), pl.BlockSpec((hp, 1, n), lambda h, j: (h, 0, 0)),
         pl.BlockSpec((n, hp * V_HEAD), lambda h, j: (0, h))],
        [_sds((N_HEADS, n, QK_PAD), F32), _sds((N_HEADS, n, QK_PAD), F32), _sds((N_HEADS, n, V_HEAD), F32)],
        [pl.BlockSpec((hp, n, QK_PAD), lambda h, j: (h, 0, 0)), pl.BlockSpec((hp, t, QK_PAD), lambda h, j: (h, j, 0)),
         pl.BlockSpec((hp, t, V_HEAD), lambda h, j: (h, j, 0))],
        scratch=[pltpu.VMEM((hp, QK_PAD, n), F32)],
        xch=xch)


def _qkv_prep_bwd(ql, kvl, dq, dk, dv, q_a_norm, kv_a_norm, wq, wkv, gq, gk, cos_t, sin_t, xch):
    n = ql.shape[0]

    def body(ql_ref, kvl_ref, cos_ref, sin_ref, dq_ref, dk_ref, dv_ref, qa_ref, ka_ref, wq_ref, wkv_ref, gq_ref, gk_ref,
             dql_ref, dkvl_ref, dqa_ref, dka_ref, dgq_ref, dgk_ref, dwq_ref, dwkv_ref, dqp_ref, dkvp_ref):
        cos_t, sin_t = cos_ref[...], sin_ref[...]
        ql_t = ql_ref[...]
        qa, inv_qa = _rms(ql_t, qa_ref[...], Q_LORA)
        qab = qa.astype(BF16)
        kvl_t = kvl_ref[...]
        ckv = kvl_t[:, 0:KV_LORA]
        ca, inv_ca = _rms(ckv, ka_ref[...], KV_LORA)
        cab = ca.astype(BF16)
        kpe = kvl_t[:, KV_LORA:KV_LAT_PAD]
        dgq = jnp.zeros((1, QK_PAD), F32)
        dgk = jnp.zeros((1, QK_PAD), F32)
        dkpe = jnp.zeros_like(kpe)
        q_pre = _dot(qab, wq_ref[...])
        kv_pre = _dot(cab, wkv_ref[...])
        for h in range(N_HEADS):
            head = slice(h * QK_PAD, (h + 1) * QK_PAD)
            q_slab = q_pre[:, head]
            inv = lax.rsqrt(jnp.sum(q_slab * q_slab, -1, keepdims=True) * (1.0 / QK_HEAD) + EPS)
            d_slab, dg = _head_norm_rope_bwd(dq_ref[h] * ATT_SCALE, q_slab, gq_ref[...], inv, cos_t, sin_t)
            dqp_ref[:, head] = d_slab.astype(BF16)
            dgq += jnp.sum(dg, 0, keepdims=True)
            k_slab = jnp.concatenate([kv_pre[:, h * QK_PAD:h * QK_PAD + QK_NOPE], kpe], axis=-1)
            inv = lax.rsqrt(jnp.sum(k_slab * k_slab, -1, keepdims=True) * (1.0 / QK_HEAD) + EPS)
            d_slab, dg = _head_norm_rope_bwd(dk_ref[h], k_slab, gk_ref[...], inv, cos_t, sin_t)
            dkvp_ref[:, head] = jnp.concatenate([d_slab[:, 0:QK_NOPE], dv_ref[h]], axis=-1).astype(BF16)
            dkpe += d_slab[:, QK_NOPE:QK_PAD]
            dgk += jnp.sum(dg, 0, keepdims=True)
        dqa = _dot_nt(dqp_ref[...], wq_ref[...])
        dx, dg = _rms_bwd(dqa, ql_t, qa_ref[...], inv_qa, Q_LORA)
        dql_ref[...] = dx.astype(BF16)
        _acc(dqa_ref, jnp.sum(dg, 0, keepdims=True))
        dca = _dot_nt(dkvp_ref[...], wkv_ref[...])
        dx, dg = _rms_bwd(dca, ckv, ka_ref[...], inv_ca, KV_LORA)
        dkvl_ref[:, 0:KV_LORA] = dx.astype(BF16)
        dkvl_ref[:, KV_LORA:KV_LAT_PAD] = dkpe.astype(BF16)
        _acc(dka_ref, jnp.sum(dg, 0, keepdims=True))
        _acc(dgq_ref, dgq)
        _acc(dgk_ref, dgk)
        _acc(dwq_ref, _dot_tn(qab, dqp_ref[...]))
        _acc(dwkv_ref, _dot_tn(cab, dkvp_ref[...]))

    wide = N_HEADS * QK_PAD
    row_outs = [((n, Q_LORA), BF16), ((n, KV_LAT_PAD), BF16)]
    acc_outs = [((1, Q_LORA), F32), ((1, KV_LORA), F32), ((1, QK_PAD), F32), ((1, QK_PAD), F32),
                ((Q_LORA, wide), F32), ((KV_LORA, wide), F32)]
    return _row_call(body, "qkv_prep_bwd", n, ROW_T, [ql, kvl, cos_t, sin_t, dq, dk, dv],
                     [q_a_norm, kv_a_norm, wq, wkv, gq, gk], row_outs, acc_outs, xch=xch,
                     scratch=[pltpu.VMEM((ROW_T, wide), BF16), pltpu.VMEM((ROW_T, wide), BF16)])


def _glu_bwd(dy_ssm, y, w_glu, b_glu, w_o_ssm):
    n = y.shape[0]

    def body(dys_ref, y_ref, wg_ref, bg_ref, wo_ref, dy_ref, db_ref, dwg_ref, dwo_ref):
        y_t = y_ref[...]
        z, th = _gelu(y_t)
        zb = z.astype(BF16)
        s = _sigmoid(_dot(zb, wg_ref[...]) + bg_ref[...])
        dys = dys_ref[...]
        dzg = jnp.zeros_like(y_t)
        for j in range(N_DEV):
            dzg += _dot_nt(dys[:, j * OUT_SHARD:(j + 1) * OUT_SHARD], wo_ref[j])
        dt = dzg * z * s * (1.0 - s)
        dtb = dt.astype(BF16)
        dz = dzg * s + _dot_nt(dtb, wg_ref[...])
        dy_ref[...] = dz * _gelu_grad(y_t, th)
        _acc(db_ref, jnp.sum(dt, 0, keepdims=True))
        _acc(dwg_ref, _dot_tn(zb, dtb))
        _acc(dwo_ref, _dot_tn((z * s).astype(BF16), dys))

    acc_outs = [((1, SSM_WIDTH), F32), ((SSM_WIDTH, SSM_WIDTH), F32), ((SSM_WIDTH, D_MODEL), F32)]
    return _row_call(body, "glu_bwd", n, ROW_T, [dy_ssm, y], [w_glu, b_glu, w_o_ssm], [((n, SSM_WIDTH), F32)], acc_outs)


def _ssm_bwd(u, dy, st, bblk, cblk, lam, d_row, xch):
    n = u.shape[0]
    t = min(SCAN_T, n)
    nc = n // t
    kb = 512
    perm = _perm_matrix(t)

    def body(u_ref, dy_ref, st_ref, p_ref, bblk_ref, cblk_ref, lam_ref, d_ref,
             du_ref, dlam_ref, dd_ref, db_ref, dct_ref,
             buf_x, buf_a, pw_ref, carry_ref, xcarry_ref, sx_ref, sa_ref, db_acc, dct_acc):
        @pl.when(pl.program_id(0) == 0)
        def _():
            carry_ref[...] = jnp.zeros_like(carry_ref)
            db_acc[...] = jnp.zeros_like(db_acc)
            dct_acc[...] = jnp.zeros_like(dct_acc)
            _power_table(lam_ref, pw_ref, t // SUBCHUNKS)

        u_t = u_ref[...]
        dy_t = dy_ref[...]
        p = p_ref[...]
        ub = _dot(p, u_t.astype(BF16)).astype(BF16)
        dyb = _dot(p, dy_t.astype(BF16)).astype(BF16)
        _to_states(ub, bblk_ref, buf_x, False)
        xcarry_ref[...] = st_ref[0]
        _run_scan(buf_x, lam_ref, t, False)
        _run_carries(buf_x, pw_ref, xcarry_ref, sx_ref, t, False)
        _run_fix(buf_x, pw_ref, sx_ref, t, False)
        _to_states(dyb, cblk_ref, buf_a, True)
        _run_scan(buf_a, lam_ref, t, True)
        _run_carries(buf_a, pw_ref, carry_ref, sa_ref, t, True)
        _run_fix(buf_a, pw_ref, sa_ref, t, True)
        du_ref[...] = (d_ref[...] * dy_t + _unpermute(p, _to_channels(buf_a, bblk_ref, True))).astype(BF16)
        for b in range(STATE_BLOCKS):
            lanes, ch = _state_block(b)
            db_acc[ch, lanes] += _dot_tn(ub[:, ch], buf_a[:, lanes].astype(BF16))
            dct_acc[ch, lanes] += _dot_tn(dyb[:, ch], buf_x[:, lanes].astype(BF16))
        for c in range(0, N_STATE, kb):
            re, im = pl.ds(c, kb), pl.ds(N_STATE + c, kb)
            xr, xi = buf_x[pl.ds(0, t - 8), re], buf_x[pl.ds(0, t - 8), im]
            ar, ai = buf_a[pl.ds(8, t - 8), re], buf_a[pl.ds(8, t - 8), im]
            x0r, x0i = sx_ref[:, re], sx_ref[:, im]
            a0r, a0i = buf_a[0:8, re], buf_a[0:8, im]
            dlam_part_re = (jnp.sum(ar * xr + ai * xi, 0, keepdims=True)
                            + jnp.sum(a0r * x0r + a0i * x0i, 0, keepdims=True))
            dlam_part_im = (jnp.sum(ai * xr - ar * xi, 0, keepdims=True)
                            + jnp.sum(a0i * x0r - a0r * x0i, 0, keepdims=True))

            @pl.when(pl.program_id(0) == 0)
            def _(c=c):
                dlam_ref[0:1, c:c + kb] = jnp.zeros((1, kb), F32)
                dlam_ref[1:2, c:c + kb] = jnp.zeros((1, kb), F32)

            dlam_ref[0:1, c:c + kb] += dlam_part_re
            dlam_ref[1:2, c:c + kb] += dlam_part_im
        _acc(dd_ref, jnp.sum(dy_t * u_t, 0, keepdims=True))

        @pl.when(pl.program_id(0) == nc - 1)
        def _():
            pltpu.sync_copy(db_acc, db_ref)
            pltpu.sync_copy(dct_acc, dct_ref)

    rev = lambda i: (nc - 1 - i, 0)
    consts = [perm, bblk, cblk, lam, d_row]
    wide = (SSM_WIDTH, 2 * N_STATE)
    return _call(
        body, "ssm_bwd", (nc,), [u, dy, st] + consts,
        [pl.BlockSpec((t, SSM_WIDTH), rev), pl.BlockSpec((t, SSM_WIDTH), rev),
         pl.BlockSpec((1, 8, 2 * N_STATE), lambda i: (nc - 1 - i, 0, 0))] + [_const(a) for a in consts],
        [_sds((n, SSM_WIDTH), BF16), _sds((2, N_STATE), F32), _sds((1, SSM_WIDTH), F32), _sds(wide, F32), _sds(wide, F32)],
        [pl.BlockSpec((t, SSM_WIDTH), rev), pl.BlockSpec((2, N_STATE), lambda i: (0, 0)),
         pl.BlockSpec((1, SSM_WIDTH), lambda i: (0, 0)), ANY, ANY],
        scratch=[pltpu.VMEM((t, 2 * N_STATE), F32)] * 2 + [pltpu.VMEM((t // SUBCHUNKS, 2 * N_STATE), F32)]
        + [pltpu.VMEM((8, 2 * N_STATE), F32)] * 4 + [pltpu.VMEM(wide, F32)] * 2,
        xch=xch)


def _in_proj_bwd(pieces, dh, x, xn_t, norm_mix, w_in_pad, xch):
    n = x.shape[0]
    tm = min(MM_T, n)
    nt = n // tm

    def body(du_ref, dql_ref, dkvl_ref, dgs_ref, dgm_ref, dh_ref, x_ref, xnt_ref, g_ref, w_ref,
             dx_ref, dg_ref, dw_ref, acc_ref):
        @pl.when(pl.program_id(0) == 0)
        def _():
            acc_ref[...] = jnp.zeros_like(acc_ref)

        xnt = xnt_ref[...]
        dxn = jnp.zeros((tm, D_MODEL), F32)
        for ref, (a, b) in zip((du_ref, dql_ref, dkvl_ref, dgs_ref, dgm_ref), IN_SEGS):
            piece = ref[...]
            dxn += _dot_nt(piece, w_ref[:, a:b])
            acc_ref[:, a:b] += _dot(xnt, piece)
        x_t = x_ref[...]
        inv = lax.rsqrt(jnp.sum(x_t * x_t, -1, keepdims=True) * (1.0 / D_MODEL) + EPS)
        dx, dg = _rms_bwd(dxn, x_t, g_ref[...], inv, D_MODEL)
        dx_ref[...] = dh_ref[...] + dx
        _acc(dg_ref, jnp.sum(dg, 0, keepdims=True))

        @pl.when(pl.program_id(0) == nt - 1)
        def _():
            pltpu.sync_copy(acc_ref, dw_ref)

    row_ins, consts = list(pieces) + [dh, x], [norm_mix, w_in_pad]
    in_specs = ([_rows(a, tm) for a in row_ins] + [pl.BlockSpec((D_MODEL, tm), lambda i: (0, i))]
                + [_const(a) for a in consts])
    return _call(
        body, "in_proj_bwd", (nt,), row_ins + [xn_t] + consts, in_specs,
        [_sds((n, D_MODEL), F32), _sds((1, D_MODEL), F32), _sds((D_MODEL, D_IN_PAD), F32)],
        [pl.BlockSpec((tm, D_MODEL), lambda i: (i, 0)), pl.BlockSpec((1, D_MODEL), lambda i: (0, 0)), ANY],
        scratch=[pltpu.VMEM((D_MODEL, D_IN_PAD), F32)], xch=xch)


def _swap_minor(a):
    g, r, c = a.shape[1:]
    return jnp.transpose(a[0], (0, 2, 1)).reshape(g * c, r)


def _pad_in(w):
    return jnp.concatenate([w[:, :KV_END], jnp.zeros((w.shape[0], D_IN_PAD - D_IN), w.dtype), w[:, KV_END:]], axis=1)


def _unpad_in(w):
    return jnp.concatenate([w[:, :KV_END], w[:, KV_END + D_IN_PAD - D_IN:]], axis=1)


def _pad_gain(g):
    return jnp.pad(g, ((0, 0), (0, QK_PAD - QK_HEAD)))


def _place():
    x, y, c = lax.axis_index("x"), lax.axis_index("y"), lax.axis_index("c")
    chips = [(x, y), (1 - x, y), (x, 1 - y), (1 - x, 1 - y)]
    return x, y, c, chips


def _all_gather(block, name):
    rows, lanes = block.shape

    def body(x_ref, out_ref, send_sems, recv_sems, local_sem):
        x, y, c, chips = _place()
        me, sibling = (x, y, c), (x, y, 1 - c)

        def slot(px, py, pc):
            return out_ref.at[4 * px + 2 * py + pc]

        def copy(k, blk, to, src=None):
            return pltpu.make_async_remote_copy(
                src_ref=slot(*blk) if src is None else src, dst_ref=slot(*blk),
                send_sem=send_sems.at[k], recv_sem=recv_sems.at[k], device_id=to, device_id_type=MESH)

        mine = pltpu.make_async_copy(x_ref, slot(*me), local_sem)
        mine.start()
        first = [copy(0, me, sibling, src=x_ref)]
        first += [copy(1 + j, me, (*chip, c), src=x_ref) for j, chip in enumerate(chips[1:])]
        for cp in first:
            cp.start()
        passed = [copy(4 + j, (*chip, c), sibling) for j, chip in enumerate(chips[1:])]
        for j, chip in enumerate(chips[1:]):
            copy(1 + j, (*chip, c), me).wait_recv()
            passed[j].start()
        copy(0, sibling, me).wait_recv()
        for j, chip in enumerate(chips[1:]):
            copy(4 + j, (*chip, 1 - c), me).wait_recv()
        for cp in first + passed:
            cp.wait_send()
        mine.wait()

    return pl.pallas_call(
        body,
        name=name,
        in_specs=[ANY],
        out_specs=ANY,
        out_shape=_sds((N_DEV, rows, lanes), block.dtype),
        scratch_shapes=[pltpu.SemaphoreType.DMA((7,)), pltpu.SemaphoreType.DMA((7,)), pltpu.SemaphoreType.DMA],
    )(block)


def _reduce_scatter(parts, gather, name):
    _, rows, lanes = parts.shape

    def body(p_ref, g_ref, out_ref, ga_ref, own, land_a, send_b, land_b, sa, ra, sb, rb, lo, *g_sems):
        x, y, c, chips = _place()
        sibling = (x, y, 1 - c)
        _xchg_start([False], [g_ref], [ga_ref], *g_sems)

        def blk(chip, core):
            return p_ref.at[4 * chip[0] + 2 * chip[1] + core]

        to_sib = [pltpu.make_async_remote_copy(
            src_ref=blk(chips[k], 1 - c), dst_ref=land_a.at[k], send_sem=sa.at[k], recv_sem=ra.at[k],
            device_id=sibling, device_id_type=MESH) for k in range(4)]
        for cp in to_sib:
            cp.start()
        loads = [pltpu.make_async_copy(blk(chips[k], c), own.at[k], lo.at[k]) for k in range(4)]
        for cp in loads:
            cp.start()
        to_chip = [pltpu.make_async_remote_copy(
            src_ref=send_b.at[j], dst_ref=land_b.at[j], send_sem=sb.at[j], recv_sem=rb.at[j],
            device_id=(*chips[1 + j], c), device_id_type=MESH) for j in range(3)]
        for k in (1, 2, 3):
            to_sib[k].wait_recv()
            loads[k].wait()
            send_b[k - 1] = (own[k] + land_a[k]).astype(BF16)
            to_chip[k - 1].start()
        to_sib[0].wait_recv()
        loads[0].wait()
        acc = own[0] + land_a[0]
        for j in range(3):
            to_chip[j].wait_recv()
            acc = acc + land_b[j].astype(F32)
        out_ref[...] = acc
        for cp in to_sib + to_chip:
            cp.wait_send()
        _xchg_wait([False], [g_ref], [ga_ref], *g_sems)

    return pl.pallas_call(
        body,
        name=name,
        in_specs=[ANY, ANY],
        out_specs=[pl.BlockSpec(memory_space=pltpu.VMEM), ANY],
        out_shape=[_sds((rows, lanes), F32), _sds((N_DEV,) + gather.shape, gather.dtype)],
        scratch_shapes=[pltpu.VMEM((4, rows, lanes), F32), pltpu.VMEM((4, rows, lanes), F32),
                        pltpu.VMEM((3, rows, lanes), BF16), pltpu.VMEM((3, rows, lanes), BF16)]
        + [pltpu.SemaphoreType.DMA((4,))] * 2 + [pltpu.SemaphoreType.DMA((3,))] * 2 + [pltpu.SemaphoreType.DMA((4,))]
        + [pltpu.SemaphoreType.DMA((1,))] * 3,
        compiler_params=_params(),
    )(parts, gather)


def _adamw_math(w, g, m, v):
    m = ADAM_B1 * m + (1.0 - ADAM_B1) * g
    v = ADAM_B2 * v + (1.0 - ADAM_B2) * (g * g)
    m_hat = m / (1.0 - ADAM_B1 ** ADAM_STEP)
    v_hat = v / (1.0 - ADAM_B2 ** ADAM_STEP)
    delta = -ADAM_LR * (m_hat / (jnp.sqrt(v_hat) + ADAM_EPS) + ADAM_WD * w)
    return delta, m, v


def _row_tile(r):
    return max(t for t in range(8, min(r, 256) + 1, 8) if r % t == 0)


def _adamw(w, g, m, v, name):
    r, n = w.shape

    def body(w_ref, g_ref, m_ref, v_ref, d_ref, nm_ref, nv_ref):
        d_ref[...], nm_ref[...], nv_ref[...] = _adamw_math(w_ref[...], g_ref[...], m_ref[...], v_ref[...])

    return _row_call(body, name, r, _row_tile(r), [w, g, m, v], [], [((r, n), F32)] * 3)


def _adamw_sum(landed, w, m, v, name):
    r, n = w.shape

    def body(l_ref, w_ref, m_ref, v_ref, g_ref, d_ref, nm_ref, nv_ref):
        g = l_ref[0].astype(F32)
        for dev in range(1, N_DEV):
            g = g + l_ref[dev].astype(F32)
        g_ref[...] = g
        d_ref[...], nm_ref[...], nv_ref[...] = _adamw_math(w_ref[...], g, m_ref[...], v_ref[...])

    tm = max(t for t in range(16, min(r, 256) + 1, 16) if r % t == 0)
    return _row_call(body, name, r, tm, [landed, w, m, v], [], [((r, n), F32)] * 4)


def _adamw_small(first, rest, w, m, v, row_counts):
    n_rest = w.shape[0] - first.shape[1]

    def body(f_ref, r_ref, w_ref, m_ref, v_ref, loss_ref, *out_refs):
        gf, gr = f_ref[0], r_ref[0]
        for dev in range(1, N_DEV):
            gf, gr = gf + f_ref[dev], gr + r_ref[dev]
        loss_ref[...] = gr[n_rest:n_rest + 8]
        g = jnp.concatenate([gf, gr[0:n_rest]], axis=0)
        d, nm, nv = _adamw_math(w_ref[...], g, m_ref[...], v_ref[...])
        off = 0
        for p, rows in enumerate(row_counts):
            for k, val in enumerate((g, d, nm, nv)):
                out_refs[4 * p + k][...] = val[off:off + rows]
            off += rows

    outs = [_sds((8, LANES), F32)] + [_sds((rows, LANES), F32) for rows in row_counts for _ in range(4)]
    return pl.pallas_call(body, name="adamw_small", out_shape=outs, compiler_params=_params())(first, rest, w, m, v)


SMALL = ("norm_mix", "q_a_norm", "kv_a_norm", "q_norm", "k_norm", "ssm_a_re", "ssm_a_im", "ssm_log_dt", "ssm_b_re",
         "ssm_b_im", "ssm_c_re", "ssm_c_im", "ssm_d", "b_glu", "norm_mlp")
WEIGHT_ORDER = ("norm_mix", "w_in", "q_a_norm", "kv_a_norm", "w_q_b", "w_kv_b", "q_norm", "k_norm", "w_o_mla",
                "ssm_a_re", "ssm_a_im", "ssm_log_dt", "ssm_b_re", "ssm_b_im", "ssm_c_re", "ssm_c_im", "ssm_d", "w_glu",
                "b_glu", "w_o_ssm", "w_out", "norm_mlp", "w_up", "w_down")
IN_SHARD = D_IN // N_DEV


def _pack_small(vals, names=SMALL):
    parts = []
    for n in names:
        flat = vals[n].reshape(-1)
        size = -(-flat.shape[0] // (8 * LANES)) * 8 * LANES
        parts.append(jnp.pad(flat, (0, size - flat.shape[0])).reshape(-1, LANES))
    return jnp.concatenate(parts, axis=0)


def _small_rows(like):
    return [-(-like[n].size // (8 * LANES)) * 8 for n in SMALL]


def _step(x, pos_col, target, w, small):
    bf = {n: a.astype(BF16) for n, a in w.items()}
    gq, gk = _pad_gain(small["q_norm"]), _pad_gain(small["k_norm"])
    a_re = small["ssm_a_re"].reshape(1, N_STATE)
    a_im = small["ssm_a_im"].reshape(1, N_STATE)
    log_dt = jnp.repeat(small["ssm_log_dt"].reshape(SSM_GROUPS), SSM_STATE).reshape(1, N_STATE)
    bt_re, bt_im = _swap_minor(small["ssm_b_re"]), _swap_minor(small["ssm_b_im"])
    c2_re, c2_im = _swap_minor(small["ssm_c_re"]), _swap_minor(small["ssm_c_im"])
    d_row = small["ssm_d"].reshape(1, SSM_WIDTH)

    w_in_all = _all_gather(bf["w_in"], "gather_w_in")
    w_in_pad = _pad_in(jnp.transpose(w_in_all, (1, 0, 2)).reshape(D_MODEL, D_IN))
    cos_t, sin_t = _rope_tables(pos_col)
    lam, bblk, cblk = _ssm_prep(a_re, a_im, log_dt, bt_re, bt_im, c2_re, c2_im)
    wq_mine = jnp.pad(bf["w_q_b"], ((0, 0), (0, QK_PAD - QK_HEAD)))
    u, ql, kvl, gs, gm, xn_t, w_glu, w_o_ssm = _in_proj(
        x, small["norm_mix"], w_in_pad, xch=[(bf["w_glu"], False), (bf["w_o_ssm"], False)])
    w_glu = w_glu.reshape(SSM_WIDTH, SSM_WIDTH)
    y, y_ssm, st, wq, wkv, w_o_mla, w_out = _ssm_fwd(
        u, bblk, cblk, lam, d_row, w_glu, small["b_glu"], w_o_ssm,
        xch=[(wq_mine, False), (bf["w_kv_b"], False), (bf["w_o_mla"], False), (bf["w_out"], False)])
    w_o_mla, w_out = w_o_mla.reshape(D_MODEL, D_MODEL), w_out.reshape(D_MODEL, D_MODEL)
    wq = jnp.transpose(wq, (1, 0, 2)).reshape(Q_LORA, N_HEADS * QK_PAD)
    wkv = jnp.transpose(wkv, (1, 0, 2)).reshape(KV_LORA, N_HEADS * QK_PAD)
    q, k, v, kt, vt = _qkv_prep(ql, kvl, small["q_a_norm"], small["kv_a_norm"], wq, wkv, gq, gk, cos_t, sin_t)
    attn, lse, attn_t, w_up, w_down = _attn_fwd(q, k, vt, xch=[(bf["w_up"], False), (bf["w_down"], False)])
    h, y_mla, mixed_t = _merge(attn, gs, gm, y_ssm, x, w_o_mla, w_out)
    hn, dout, hn_t, loss = _mlp_fwd_loss(h, target, small["norm_mlp"], w_up, w_down)

    da, dh, dout_b, hid_t, d_norm_mlp = _mlp_bwd(dout, hn, h, small["norm_mlp"], w_up, w_down)
    p_w_down = _matmul_tn_shards(hid_t, dout_b, "dw_down", False, tm=1024, turned=True)
    p_w_up = _matmul_tn_shards(hn_t, da, "dw_up", True, turned=True)
    dgs, dgm, dy_ssm, dy_mla, dattn, delta = _merge_bwd(dh, gs, gm, y_ssm, y_mla, attn, w_out, w_o_mla)
    p_w_out = _matmul_tn_shards(mixed_t, dh, "dw_out", False, tm=1024, turned=True)
    p_w_o_mla = _matmul_tn_shards(attn_t, dy_mla, "dw_o_mla", False, turned=True)
    dq, dk, dv, l_w_up, l_w_down, l_w_out, l_w_o_mla = _attn_bwd(
        q, k, kt, v, lse, delta, dattn, xch=[(p_w_up, True), (p_w_down, True), (p_w_out, True), (p_w_o_mla, True)])
    dql, dkvl, d_q_a_norm, d_kv_a_norm, d_gq, d_gk, g_wq, g_wkv = _qkv_prep_bwd(
        ql, kvl, dq, dk, dv, small["q_a_norm"], small["kv_a_norm"], wq, wkv, gq, gk, cos_t, sin_t, xch=[])
    p_wq = jnp.transpose(g_wq.reshape(Q_LORA, N_HEADS, QK_PAD), (1, 0, 2)).astype(BF16)
    p_wkv = jnp.transpose(g_wkv.reshape(KV_LORA, N_HEADS, QK_PAD), (1, 0, 2)).astype(BF16)
    dy, d_b_glu, g_w_glu, g_w_o_ssm = _glu_bwd(dy_ssm, y, w_glu, small["b_glu"], w_o_ssm)
    p_w_o_ssm = jnp.transpose(g_w_o_ssm.reshape(SSM_WIDTH, N_DEV, OUT_SHARD), (1, 0, 2)).astype(BF16)
    p_w_glu = g_w_glu.reshape(N_DEV, SSM_WIDTH // N_DEV, SSM_WIDTH).astype(BF16)
    du, dlam, d_d, d_bblk, d_cblk_t, l_wq, l_wkv, l_w_glu, l_w_o_ssm = _ssm_bwd(
        u, dy, st, bblk, cblk, lam, d_row, xch=[(p_wq, True), (p_wkv, True), (p_w_glu, True), (p_w_o_ssm, True)])
    d_a_re, d_a_im, d_log_dt, d_bt_re, d_bt_im, d_c_re, d_c_im = _ssm_prep_bwd(
        a_re, a_im, log_dt, bt_re, bt_im, dlam, d_bblk, d_cblk_t)
    tr = lambda mat: jnp.transpose(mat.reshape(SSM_GROUPS, SSM_GROUP_CH, SSM_STATE), (0, 2, 1))
    g_small = {
        "q_a_norm": d_q_a_norm, "kv_a_norm": d_kv_a_norm, "q_norm": d_gq[:, :QK_HEAD], "k_norm": d_gk[:, :QK_HEAD],
        "ssm_a_re": d_a_re, "ssm_a_im": d_a_im, "ssm_log_dt": d_log_dt,
        "ssm_b_re": tr(d_bt_re), "ssm_b_im": tr(d_bt_im), "ssm_c_re": d_c_re, "ssm_c_im": d_c_im,
        "ssm_d": d_d, "b_glu": d_b_glu, "norm_mlp": d_norm_mlp,
    }
    rest = jnp.concatenate([_pack_small(g_small, SMALL[1:]), loss], axis=0)
    dx, d_norm_mix, g_w_in_pad, g_rest_all = _in_proj_bwd(
        (du, dql, dkvl, dgs, dgm), dh, x, xn_t, small["norm_mix"], w_in_pad, xch=[(rest, False)])
    parts = jnp.transpose(_unpad_in(g_w_in_pad).reshape(D_MODEL, N_DEV, IN_SHARD), (1, 0, 2))
    g_w_in_mine, g_first_all = _reduce_scatter(parts, _pack_small({SMALL[0]: d_norm_mix}, SMALL[:1]), "reduce_w_in")
    landed = {"w_q_b": l_wq[:, :, :QK_HEAD], "w_kv_b": l_wkv, "w_o_mla": l_w_o_mla, "w_glu": l_w_glu,
              "w_o_ssm": l_w_o_ssm, "w_out": l_w_out, "w_up": l_w_up, "w_down": l_w_down}
    return dx, landed, g_w_in_mine, g_first_all, g_rest_all


def kernel(x, positions, norm_mix, w_in, q_a_norm, kv_a_norm, w_q_b, w_kv_b, q_norm, k_norm, w_o_mla, ssm_a_re, ssm_a_im, ssm_log_dt, ssm_b_re, ssm_b_im, ssm_c_re, ssm_c_im, ssm_d, w_glu, b_glu, w_o_ssm, w_out, norm_mlp, w_up, w_down, loss_target, m_norm_mix, m_w_in, m_q_a_norm, m_kv_a_norm, m_w_q_b, m_w_kv_b, m_q_norm, m_k_norm, m_w_o_mla, m_ssm_a_re, m_ssm_a_im, m_ssm_log_dt, m_ssm_b_re, m_ssm_b_im, m_ssm_c_re, m_ssm_c_im, m_ssm_d, m_w_glu, m_b_glu, m_w_o_ssm, m_w_out, m_norm_mlp, m_w_up, m_w_down, v_norm_mix, v_w_in, v_q_a_norm, v_kv_a_norm, v_w_q_b, v_w_kv_b, v_q_norm, v_k_norm, v_w_o_mla, v_ssm_a_re, v_ssm_a_im, v_ssm_log_dt, v_ssm_b_re, v_ssm_b_im, v_ssm_c_re, v_ssm_c_im, v_ssm_d, v_w_glu, v_b_glu, v_w_o_ssm, v_w_out, v_norm_mlp, v_w_up, v_w_down):
    given = dict(locals())
    w = {n: given[n] for n in WEIGHT_ORDER}
    m = {n: given["m_" + n] for n in WEIGHT_ORDER}
    v = {n: given["v_" + n] for n in WEIGHT_ORDER}
    big = [n for n in WEIGHT_ORDER if n not in SMALL]
    small = {n: w[n] for n in SMALL}

    dx, landed, g_w_in, g_first_all, g_rest_all = _step(
        x[0], positions.reshape(-1, 1), loss_target[0], {n: w[n][0] for n in big}, small)

    grads, deltas, new_m, new_v = {}, {}, {}, {}
    for n in big:
        if n in ("w_in", "w_q_b"):
            wt, mt, vt = jnp.transpose(w[n][0]), jnp.transpose(m[n][0]), jnp.transpose(v[n][0])
            if n == "w_in":
                g = jnp.transpose(g_w_in)
                d, nm, nv = _adamw(wt, g, mt, vt, "adamw_" + n)
            else:
                g, d, nm, nv = _adamw_sum(jnp.transpose(landed[n], (0, 2, 1)), wt, mt, vt, "adamw_" + n)
            g, d, nm, nv = (jnp.transpose(a) for a in (g, d, nm, nv))
        else:
            g, d, nm, nv = _adamw_sum(landed[n], w[n][0], m[n][0], v[n][0], "adamw_" + n)
        grads[n], deltas[n], new_m[n], new_v[n] = g[None], d[None], nm[None], nv[None]

    outs = _adamw_small(g_first_all, g_rest_all, _pack_small(small), _pack_small({n: m[n] for n in SMALL}),
                        _pack_small({n: v[n] for n in SMALL}), _small_rows(small))
    for p, n in enumerate(SMALL):
        for k, dst in enumerate((grads, deltas, new_m, new_v)):
            dst[n] = outs[1 + 4 * p + k].reshape(-1)[:small[n].size].reshape(small[n].shape)

    return (outs[0][0, 0], dx[None], *[grads[n] for n in WEIGHT_ORDER], *[deltas[n] for n in WEIGHT_ORDER],
            *[new_m[n] for n in WEIGHT_ORDER], *[new_v[n] for n in WEIGHT_ORDER])
```

```python
import functools
import math

import numpy as np
import jax
import jax.numpy as jnp
from jax import lax
from jax.experimental import pallas as pl
from jax.experimental.pallas import tpu as pltpu

F32 = jnp.float32
BF16 = jnp.bfloat16

D_MODEL = 1024
SSM_GROUPS = 32
SSM_GROUP_CH = 16
SSM_WIDTH = 512
SSM_STATE = 64
N_STATE = SSM_GROUPS * SSM_STATE
N_HEADS = 8
QK_NOPE = 128
QK_ROPE = 64
QK_HEAD = 192
QK_PAD = 256
V_HEAD = 128
Q_LORA = 384
KV_LORA = 256
KV_LAT_PAD = 384
ROPE_THETA = 10000.0
D_FF = 4096
EPS = 1e-6
ATT_SCALE = QK_HEAD ** -0.5
N_DEV = 8
FF_SHARD = D_FF // N_DEV
OUT_SHARD = D_MODEL // N_DEV

IN_SEGS = ((0, 512), (512, 896), (896, 1280), (1280, 2304), (2304, 3328))
D_IN = 3264
D_IN_PAD = 3328
KV_END = 1216

ADAM_LR = 0.001
ADAM_B1 = 0.9
ADAM_B2 = 0.999
ADAM_EPS = 1e-08
ADAM_WD = 0.01
ADAM_STEP = 10

VMEM_LIMIT = 56 * 1024 * 1024
MESH = pl.DeviceIdType.MESH
ANY = pl.BlockSpec(memory_space=pl.ANY)
LANES = 128

SCAN_T = 512
SUBCHUNKS = 8
SCAN_CG = 2048
ATT_T = 512
ATT_SUB = 1
ATT_HEADS = 4
ATT_BWD_HEADS = 2
ROW_T = 256
MM_T = 512


def _params(sem=None):
    return pltpu.CompilerParams(dimension_semantics=sem, vmem_limit_bytes=VMEM_LIMIT)


def _rows(arr, tm):
    if arr.ndim == 2:
        return pl.BlockSpec((tm, arr.shape[1]), lambda i: (i, 0))
    return pl.BlockSpec((arr.shape[0], tm, arr.shape[2]), lambda i: (0, i, 0))


def _const(arr):
    nd = arr.ndim
    return pl.BlockSpec(arr.shape, lambda *_: (0,) * nd, pipeline_mode=pl.Buffered(1))


def _sds(shape, dtype):
    return jax.ShapeDtypeStruct(shape, dtype)


PEERS = tuple((dx, dy, dc) for dx in (0, 1) for dy in (0, 1) for dc in (0, 1) if (dx, dy, dc) != (0, 0, 0))


def _here():
    x, y, c = lax.axis_index("x"), lax.axis_index("y"), lax.axis_index("c")
    return x, y, c, 4 * x + 2 * y + c


def _xchg_start(scatter, srcs, dsts, send, recv, local):
    x, y, c, me = _here()
    for e, sc in enumerate(scatter):
        src, dst = srcs[e], dsts[e]
        pltpu.make_async_copy(src.at[me] if sc else src, dst.at[me], local.at[e]).start()
        for dx, dy, dc in PEERS:
            px, py, pc = (1 - x if dx else x), (1 - y if dy else y), (1 - c if dc else c)
            pltpu.make_async_remote_copy(
                src_ref=src.at[4 * px + 2 * py + pc] if sc else src, dst_ref=dst.at[me],
                send_sem=send.at[e], recv_sem=recv.at[e], device_id=(px, py, pc), device_id_type=MESH).start()


def _xchg_wait(scatter, srcs, dsts, send, recv, local):
    x, y, c, me = _here()
    for e, sc in enumerate(scatter):
        src, dst = srcs[e], dsts[e]
        pltpu.make_async_copy(src.at[me] if sc else src, dst.at[me], local.at[e]).wait()
        span = dst.at[pl.ds(0, N_DEV - 1)]
        both = pltpu.make_async_remote_copy(src_ref=span, dst_ref=span, send_sem=send.at[e], recv_sem=recv.at[e],
                                            device_id=(x, y, c), device_id_type=MESH)
        both.wait_send()
        both.wait_recv()


def _call(body, name, grid, ins, in_specs, outs, out_specs, scratch=(), xch=()):
    n_in, n_out, ne = len(ins), len(outs), len(xch)
    scatter = [sc for _, sc in xch]
    x_outs = [_sds((N_DEV,) + (a.shape[1:] if sc else a.shape), a.dtype) for a, sc in xch]
    sems = [pltpu.SemaphoreType.DMA((ne,))] * 3 if ne else []

    def wrapped(*refs):
        in_refs, x_src = refs[:n_in], refs[n_in:n_in + ne]
        out_refs = refs[n_in + ne:n_in + ne + n_out]
        x_dst = refs[n_in + ne + n_out:n_in + 2 * ne + n_out]
        rest = refs[n_in + 2 * ne + n_out:]
        if ne:
            x_sems, rest = rest[len(rest) - 3:], rest[:len(rest) - 3]
            first = functools.reduce(jnp.logical_and, [pl.program_id(d) == 0 for d in range(len(grid))])
            last = functools.reduce(jnp.logical_and, [pl.program_id(d) == grid[d] - 1 for d in range(len(grid))])

            @pl.when(first)
            def _():
                _xchg_start(scatter, x_src, x_dst, *x_sems)

        body(*in_refs, *out_refs, *rest)
        if ne:
            @pl.when(last)
            def _():
                _xchg_wait(scatter, x_src, x_dst, *x_sems)

    return pl.pallas_call(
        wrapped,
        name=name,
        grid=grid,
        in_specs=list(in_specs) + [ANY] * ne,
        out_specs=list(out_specs) + [ANY] * ne,
        out_shape=list(outs) + x_outs,
        scratch_shapes=list(scratch) + sems,
        compiler_params=_params(("arbitrary",) * len(grid)),
    )(*ins, *[a for a, _ in xch])


def _row_call(body, name, n_rows, tm, row_ins, const_ins, row_outs, acc_outs=(), xch=(), col_outs=(), scratch=()):
    outs = [_sds(s, d) for s, d in list(row_outs) + list(col_outs) + list(acc_outs)]
    n_row, n_col = len(row_outs), len(col_outs)
    out_specs = [_rows(o, tm) for o in outs[:n_row]] + [
        pl.BlockSpec(o.shape[:-1] + (tm,), lambda i, nd=len(o.shape): (0,) * (nd - 1) + (i,))
        for o in outs[n_row:n_row + n_col]] + [
        pl.BlockSpec(o.shape, lambda i, nd=len(o.shape): (0,) * nd) for o in outs[n_row + n_col:]]
    in_specs = [_rows(a, tm) for a in row_ins] + [_const(a) for a in const_ins]
    return _call(body, name, (n_rows // tm,), list(row_ins) + list(const_ins), in_specs, outs, out_specs,
                 scratch=scratch, xch=xch)


def _dot(a, b):
    return jnp.dot(a, b, preferred_element_type=F32)


def _dot_nt(a, b):
    return lax.dot_general(a, b, (((1,), (1,)), ((), ())), preferred_element_type=F32)


def _dot_tn(a, b):
    return lax.dot_general(a, b, (((0,), (0,)), ((), ())), preferred_element_type=F32)


def _rms(x, g, n):
    inv = lax.rsqrt(jnp.sum(x * x, -1, keepdims=True) * (1.0 / n) + EPS)
    return x * inv * g, inv


def _rms_bwd(dy, x, g, inv, n):
    xh = x * inv
    dxh = dy * g
    dx = inv * (dxh - xh * (jnp.sum(dxh * xh, -1, keepdims=True) * (1.0 / n)))
    return dx, dy * xh


def _sigmoid(x):
    return 1.0 / (1.0 + jnp.exp(-x))


_GELU_C = math.sqrt(2.0 / math.pi)


def _gelu(y):
    th = jnp.tanh(_GELU_C * (y + 0.044715 * (y * y * y)))
    return 0.5 * y * (1.0 + th), th


def _gelu_grad(y, th):
    return 0.5 * (1.0 + th) + 0.5 * y * (1.0 - th * th) * (_GELU_C * (1.0 + 3.0 * 0.044715 * (y * y)))


def _acc(ref, val):
    @pl.when(pl.program_id(0) == 0)
    def _():
        ref[...] = jnp.zeros_like(ref)

    ref[...] += val


def _tile(n, limit):
    if n <= limit:
        return n
    return max(t for t in range(128, limit + 1, 128) if n % t == 0)


def _lhs(a, turned, tm, tk):
    m, k_dim = a.shape if turned else a.shape[::-1]
    tm, tk = _tile(m, tm), _tile(k_dim, tk)
    if turned:
        return m, k_dim, tm, tk, pl.BlockSpec((tm, tk), lambda i, k: (i, k)), _dot
    return m, k_dim, tm, tk, pl.BlockSpec((tk, tm), lambda i, k: (k, i)), _dot_tn


def _matmul_tn_shards(a, b, name, by_col, tm=512, tk=512, turned=False):
    m, k_dim, tm, tk, a_spec, dot = _lhs(a, turned, tm, tk)
    n = b.shape[1]
    nk = k_dim // tk
    if by_col:
        r, c = m, n // N_DEV
        out_spec = pl.BlockSpec((N_DEV, tm, c), lambda i, k: (0, i, 0))
    else:
        r, c = m // N_DEV, n
        per = tm // r
        out_spec = pl.BlockSpec((per, r, c), lambda i, k: (i, 0, 0))

    def body(a_ref, b_ref, o_ref, acc_ref):
        k = pl.program_id(1)

        @pl.when(k == 0)
        def _():
            acc_ref[...] = jnp.zeros_like(acc_ref)

        acc_ref[...] += dot(a_ref[...].astype(BF16), b_ref[...].astype(BF16))

        @pl.when(k == nk - 1)
        def _():
            if by_col:
                for j in range(N_DEV):
                    o_ref[j] = acc_ref[:, j * c:(j + 1) * c].astype(BF16)
            else:
                for s in range(per):
                    o_ref[s] = acc_ref[s * r:(s + 1) * r, :].astype(BF16)

    return pl.pallas_call(
        body,
        name=name,
        grid=(m // tm, nk),
        in_specs=[a_spec, pl.BlockSpec((tk, n), lambda i, k: (k, 0))],
        out_specs=out_spec,
        out_shape=_sds((N_DEV, r, c), BF16),
        scratch_shapes=[pltpu.VMEM((tm, n), F32)],
        compiler_params=_params(("parallel", "arbitrary")),
    )(a, b)


def _rope_tables(pos_col):
    n = pos_col.shape[0]
    half = QK_ROPE // 2
    inv_freq = (ROPE_THETA ** (-np.arange(half, dtype=np.float32) / half)).astype(np.float32)
    freq_row = jnp.asarray(np.concatenate([inv_freq, inv_freq, np.zeros(64, np.float32)])[None, :])

    def body(p_ref, f_ref, c_ref, s_ref):
        ang = p_ref[...].astype(F32) * f_ref[...]
        c_ref[...] = jnp.cos(ang)
        s_ref[...] = jnp.sin(ang)

    return _row_call(body, "rope_tables", n, min(n, 1024), [pos_col], [freq_row], [((n, 128), F32)] * 2)


def _rope_rot(v):
    lane = lax.broadcasted_iota(jnp.int32, v.shape, 1)
    return jnp.where(lane < 32, -pltpu.roll(v, 96, 1), jnp.where(lane < 64, pltpu.roll(v, 32, 1), 0.0))


def _rope_rot_t(v):
    lane = lax.broadcasted_iota(jnp.int32, v.shape, 1)
    return jnp.where(lane < 32, pltpu.roll(v, 96, 1), jnp.where(lane < 64, -pltpu.roll(v, 32, 1), 0.0))


def _in_proj(x, norm_mix, w_in_pad, xch):
    n = x.shape[0]

    def body(x_ref, g_ref, w_ref, u_ref, ql_ref, kvl_ref, gs_ref, gm_ref, xnt_ref):
        xn, _ = _rms(x_ref[...], g_ref[...], D_MODEL)
        xb = xn.astype(BF16)
        xnt_ref[...] = xn.T.astype(BF16)
        for ref, (a, b) in zip((u_ref, ql_ref, kvl_ref, gs_ref, gm_ref), IN_SEGS):
            ref[...] = _dot(xb, w_ref[:, a:b])

    outs = [((n, b - a), F32) for a, b in IN_SEGS]
    return _row_call(body, "in_proj", n, MM_T, [x], [norm_mix, w_in_pad], outs, xch=xch,
                     col_outs=[((D_MODEL, n), BF16)])


def _ssm_prep_fn(a_re, a_im, log_dt, b_re_x, b_im_x):
    dt = jnp.exp(log_dt)
    mag = jnp.exp(a_re * dt)
    lr = mag * jnp.cos(a_im * dt)
    li = mag * jnp.sin(a_im * dt)
    den = a_re * a_re + a_im * a_im
    fr = ((lr - 1.0) * a_re + li * a_im) / den
    fi = (li * a_re - (lr - 1.0) * a_im) / den
    return lr, li, fr * b_re_x - fi * b_im_x, fr * b_im_x + fi * b_re_x


def _dot_exact(a, b, dims):
    return lax.dot_general(a, b, (dims, ((), ())), precision=lax.Precision.HIGHEST, preferred_element_type=F32)


def _lane_repeat(width, n):
    src = lax.broadcasted_iota(jnp.int32, (width, n), 0)
    dst = lax.broadcasted_iota(jnp.int32, (width, n), 1)
    return (dst % width == src).astype(F32)


def _same_group(rows, rows_per_group, cols, cols_per_group):
    row = lax.broadcasted_iota(jnp.int32, (rows, cols), 0)
    col = lax.broadcasted_iota(jnp.int32, (rows, cols), 1)
    return (row // rows_per_group) == (col // cols_per_group)


def _expand_b(bt):
    tiled = _dot_exact(bt, _lane_repeat(SSM_STATE, N_STATE), ((1,), (0,)))
    return jnp.where(_same_group(SSM_WIDTH, SSM_GROUP_CH, N_STATE, SSM_STATE), tiled, 0.0)


def _collect_b(m):
    masked = jnp.where(_same_group(SSM_WIDTH, SSM_GROUP_CH, N_STATE, SSM_STATE), m, 0.0)
    return _dot_exact(masked, _lane_repeat(SSM_STATE, N_STATE), ((1,), (1,)))


def _ssm_prep(a_re, a_im, log_dt, bt_re, bt_im, c2_re, c2_im):
    def body(ar, ai, ld, br, bi, cr, ci, lam_ref, bblk_ref, cblk_ref):
        lr, li, bbr, bbi = _ssm_prep_fn(ar[...], ai[...], ld[...], _expand_b(br[...]), _expand_b(bi[...]))
        lam_ref[0:1, :] = lr
        lam_ref[1:2, :] = li
        bblk_ref[:, 0:N_STATE] = bbr.astype(BF16)
        bblk_ref[:, N_STATE:] = bbi.astype(BF16)
        rep = _lane_repeat(SSM_GROUP_CH, SSM_WIDTH)
        own = _same_group(N_STATE, SSM_STATE, SSM_WIDTH, SSM_GROUP_CH)
        cblk_ref[0:N_STATE, :] = jnp.where(own, _dot_exact(cr[...], rep, ((1,), (0,))), 0.0).astype(BF16)
        cblk_ref[N_STATE:, :] = jnp.where(own, -_dot_exact(ci[...], rep, ((1,), (0,))), 0.0).astype(BF16)

    return pl.pallas_call(
        body,
        name="ssm_prep",
        out_shape=[_sds((2, N_STATE), F32), _sds((SSM_WIDTH, 2 * N_STATE), BF16),
                   _sds((2 * N_STATE, SSM_WIDTH), BF16)],
        compiler_params=_params(),
    )(a_re, a_im, log_dt, bt_re, bt_im, c2_re, c2_im)


def _ssm_prep_bwd(a_re, a_im, log_dt, bt_re, bt_im, dlam, dbblk, dcblk_t):
    def body(ar, ai, ld, br, bi, dl, db, dc, dar, dai, dld, dbr, dbi, dcr, dci):
        _, vjp = jax.vjp(_ssm_prep_fn, ar[...], ai[...], ld[...], _expand_b(br[...]), _expand_b(bi[...]))
        g = vjp((dl[0:1, :], dl[1:2, :], db[:, 0:N_STATE], db[:, N_STATE:]))
        dar[...] = g[0]
        dai[...] = g[1]
        grp = lax.broadcasted_iota(jnp.int32, (SSM_GROUPS, N_STATE), 0)
        lane = lax.broadcasted_iota(jnp.int32, (SSM_GROUPS, N_STATE), 1)
        sel = (lane // SSM_STATE) == grp
        dld[...] = jnp.sum(jnp.where(sel, jnp.broadcast_to(g[2], (SSM_GROUPS, N_STATE)), 0.0), axis=1, keepdims=True)
        dbr[...] = _collect_b(g[3])
        dbi[...] = _collect_b(g[4])
        dcr[...] = _collect_b(dc[:, 0:N_STATE])
        dci[...] = -_collect_b(dc[:, N_STATE:])

    small = _sds((SSM_WIDTH, SSM_STATE), F32)
    return pl.pallas_call(
        body,
        name="ssm_prep_bwd",
        out_shape=[_sds((1, N_STATE), F32), _sds((1, N_STATE), F32), _sds((SSM_GROUPS, 1), F32), small, small, small, small],
        compiler_params=_params(),
    )(a_re, a_im, log_dt, bt_re, bt_im, dlam, dbblk, dcblk_t)


def _perm_matrix(t):
    run = t // SUBCHUNKS
    p = np.zeros((t, t), np.float32)
    r = np.arange(t)
    p[r, (r % SUBCHUNKS) * run + r // SUBCHUNKS] = 1.0
    return jnp.asarray(p, dtype=BF16)


def _unpermute(p, a):
    hi = a.astype(BF16)
    r1 = a - hi.astype(F32)
    mid = r1.astype(BF16)
    lo = (r1 - mid.astype(F32)).astype(BF16)
    return _dot_tn(p, hi) + _dot_tn(p, mid) + _dot_tn(p, lo)


def _power_table(lam_ref, pw_ref, n):
    lr, li = lam_ref[0:1, :], lam_ref[1:2, :]
    pw_ref[0:1, 0:N_STATE] = lr
    pw_ref[0:1, N_STATE:] = li

    def step(i, carry):
        pr, pi = carry
        pr, pi = pr * lr - pi * li, pr * li + pi * lr
        pw_ref[pl.ds(i, 1), 0:N_STATE] = pr
        pw_ref[pl.ds(i, 1), N_STATE:] = pi
        return pr, pi

    lax.fori_loop(1, n, step, (lr, li))


def _col_groups():
    return [(pl.ds(c, SCAN_CG), pl.ds(N_STATE + c, SCAN_CG)) for c in range(0, N_STATE, SCAN_CG)]


def _run_scan(buf, lam_ref, t, reverse):
    nblk = t // 8
    for re, im in _col_groups():
        lr = jnp.broadcast_to(lam_ref[0:1, re], (8, SCAN_CG))
        li = jnp.broadcast_to(lam_ref[1:2, re], (8, SCAN_CG))
        if reverse:
            li = -li
        first = pl.ds((nblk - 1) * 8 if reverse else 0, 8)

        def step(k, carry, re=re, im=im, lr=lr, li=li):
            pr, pi = carry
            i = (nblk - 2 - k) if reverse else (k + 1)
            r = pl.ds(pl.multiple_of(i * 8, 8), 8)
            xr = buf[r, re] + lr * pr - li * pi
            xi = buf[r, im] + lr * pi + li * pr
            buf[r, re] = xr
            buf[r, im] = xi
            return xr, xi

        lax.fori_loop(0, nblk - 1, step, (buf[first, re], buf[first, im]))


def _run_carries(buf, pw_ref, carry_ref, s_ref, t, reverse):
    nblk = t // 8
    run = t // SUBCHUNKS
    edge = buf[pl.ds(0 if reverse else (nblk - 1) * 8, 8), :]
    pr, pi = pw_ref[run - 1:run, 0:N_STATE], pw_ref[run - 1:run, N_STATE:]
    if reverse:
        pi = -pi
    sr, si = carry_ref[0:1, 0:N_STATE], carry_ref[0:1, N_STATE:]
    for s in (range(SUBCHUNKS - 1, -1, -1) if reverse else range(SUBCHUNKS)):
        s_ref[s:s + 1, 0:N_STATE] = sr
        s_ref[s:s + 1, N_STATE:] = si
        er, ei = edge[s:s + 1, 0:N_STATE], edge[s:s + 1, N_STATE:]
        sr, si = er + pr * sr - pi * si, ei + pr * si + pi * sr
    carry_ref[:, 0:N_STATE] = jnp.broadcast_to(sr, (8, N_STATE))
    carry_ref[:, N_STATE:] = jnp.broadcast_to(si, (8, N_STATE))


def _run_fix(buf, pw_ref, s_ref, t, reverse):
    nblk = t // 8
    for re, im in _col_groups():
        sr, si = s_ref[:, re], s_ref[:, im]

        def step(i, carry, re=re, im=im, sr=sr, si=si):
            r = pl.ds(pl.multiple_of(i * 8, 8), 8)
            row = pl.ds((nblk - 1 - i) if reverse else i, 1)
            pr, pi = pw_ref[row, re], pw_ref[row, im]
            if reverse:
                pi = -pi
            buf[r, re] += pr * sr - pi * si
            buf[r, im] += pr * si + pi * sr
            return carry

        lax.fori_loop(0, nblk, step, 0)


STATE_BLOCKS = 2 * N_STATE // LANES
CH_BLOCKS = SSM_WIDTH // LANES


def _state_block(b):
    pair = b % (N_STATE // LANES)
    k = (pair * 2 * SSM_GROUP_CH) // LANES
    return slice(b * LANES, (b + 1) * LANES), slice(k * LANES, (k + 1) * LANES)


def _channel_block(c):
    w = N_STATE // CH_BLOCKS
    return slice(c * LANES, (c + 1) * LANES), slice(c * w, (c + 1) * w), slice(N_STATE + c * w, N_STATE + (c + 1) * w)


def _to_states(vb, w_ref, buf, nt):
    for b in range(STATE_BLOCKS):
        lanes, ch = _state_block(b)
        buf[:, lanes] = _dot_nt(vb[:, ch], w_ref[lanes, ch]) if nt else _dot(vb[:, ch], w_ref[ch, lanes])


def _to_channels(buf, w_ref, nt):
    outs = []
    for c in range(CH_BLOCKS):
        ch, re, im = _channel_block(c)
        xr, xi = buf[:, re].astype(BF16), buf[:, im].astype(BF16)
        if nt:
            outs.append(_dot_nt(xr, w_ref[ch, re]) + _dot_nt(xi, w_ref[ch, im]))
        else:
            outs.append(_dot(xr, w_ref[re, ch]) + _dot(xi, w_ref[im, ch]))
    return jnp.concatenate(outs, axis=-1)


def _ssm_fwd(u, bblk, cblk, lam, d_row, w_glu, b_glu, w_o_ssm, xch):
    n = u.shape[0]
    t = min(SCAN_T, n)
    perm = _perm_matrix(t)

    def body(u_ref, p_ref, bblk_ref, cblk_ref, lam_ref, d_ref, wg_ref, bg_ref, wo_ref, y_ref, ys_ref, st_ref,
             buf, pw_ref, carry_ref, s_ref):
        @pl.when(pl.program_id(0) == 0)
        def _():
            carry_ref[...] = jnp.zeros_like(carry_ref)
            _power_table(lam_ref, pw_ref, t // SUBCHUNKS)

        st_ref[0] = carry_ref[...]
        u_t = u_ref[...]
        p = p_ref[...]
        ub = _dot(p, u_t.astype(BF16)).astype(BF16)
        _to_states(ub, bblk_ref, buf, False)
        _run_scan(buf, lam_ref, t, False)
        _run_carries(buf, pw_ref, carry_ref, s_ref, t, False)
        _run_fix(buf, pw_ref, s_ref, t, False)
        y = d_ref[...] * u_t + _unpermute(p, _to_channels(buf, cblk_ref, False))
        y_ref[...] = y
        z, _ = _gelu(y)
        s = _sigmoid(_dot(z.astype(BF16), wg_ref[...]) + bg_ref[...])
        zgb = (z * s).astype(BF16)
        for j in range(N_DEV):
            ys_ref[:, j * OUT_SHARD:(j + 1) * OUT_SHARD] = _dot(zgb, wo_ref[j])

    consts = [perm, bblk, cblk, lam, d_row, w_glu, b_glu, w_o_ssm]
    return _call(
        body, "ssm_fwd", (n // t,), [u] + consts, [_rows(u, t)] + [_const(a) for a in consts],
        [_sds((n, SSM_WIDTH), F32), _sds((n, D_MODEL), F32), _sds((n // t, 8, 2 * N_STATE), F32)],
        [pl.BlockSpec((t, SSM_WIDTH), lambda i: (i, 0)), pl.BlockSpec((t, D_MODEL), lambda i: (i, 0)),
         pl.BlockSpec((1, 8, 2 * N_STATE), lambda i: (i, 0, 0))],
        scratch=[pltpu.VMEM((t, 2 * N_STATE), F32), pltpu.VMEM((t // SUBCHUNKS, 2 * N_STATE), F32),
                 pltpu.VMEM((8, 2 * N_STATE), F32), pltpu.VMEM((8, 2 * N_STATE), F32)],
        xch=xch)


def _head_norm_rope(slab, gain, cos_t, sin_t):
    xn, inv = _rms(slab, gain, QK_HEAD)
    lo, hi = xn[:, 0:128], xn[:, 128:256]
    return jnp.concatenate([lo, hi * cos_t + _rope_rot(hi) * sin_t], axis=-1), inv


def _head_norm_rope_bwd(g, slab, gain, inv, cos_t, sin_t):
    g_lo, g_hi = g[:, 0:128], g[:, 128:256]
    g_n = jnp.concatenate([g_lo, g_hi * cos_t + _rope_rot_t(g_hi * sin_t)], axis=-1)
    return _rms_bwd(g_n, slab, gain, inv, QK_HEAD)


def _qkv_prep(ql, kvl, q_a_norm, kv_a_norm, wq, wkv, gq, gk, cos_t, sin_t):
    n = ql.shape[0]
    tm = ROW_T

    def body(ql_ref, kvl_ref, cos_ref, sin_ref, qa_ref, ka_ref, wq_ref, wkv_ref, gq_ref, gk_ref,
             q_ref, k_ref, v_ref, kt_ref, vt_ref):
        cos_t, sin_t = cos_ref[...], sin_ref[...]
        qa, _ = _rms(ql_ref[...], qa_ref[...], Q_LORA)
        qab = qa.astype(BF16)
        kvl_t = kvl_ref[...]
        ca, _ = _rms(kvl_t[:, 0:KV_LORA], ka_ref[...], KV_LORA)
        cab = ca.astype(BF16)
        kpe = kvl_t[:, KV_LORA:KV_LAT_PAD]
        q_pre = _dot(qab, wq_ref[...])
        kv_pre = _dot(cab, wkv_ref[...])
        for h in range(N_HEADS):
            qh, _ = _head_norm_rope(q_pre[:, h * QK_PAD:(h + 1) * QK_PAD], gq_ref[...], cos_t, sin_t)
            q_ref[h] = (qh * ATT_SCALE).astype(BF16)
            kv_h = kv_pre[:, h * QK_PAD:(h + 1) * QK_PAD]
            kh, _ = _head_norm_rope(jnp.concatenate([kv_h[:, 0:QK_NOPE], kpe], axis=-1), gk_ref[...], cos_t, sin_t)
            k_ref[h] = kh.astype(BF16)
            kt_ref[h] = kh.T.astype(BF16)
            vh = kv_h[:, QK_NOPE:]
            v_ref[h] = vh.astype(BF16)
            vt_ref[h] = vh.T.astype(BF16)

    row_ins, consts = [ql, kvl, cos_t, sin_t], [q_a_norm, kv_a_norm, wq, wkv, gq, gk]
    outs = [_sds((N_HEADS, n, QK_PAD), BF16), _sds((N_HEADS, n, QK_PAD), BF16), _sds((N_HEADS, n, V_HEAD), BF16),
            _sds((N_HEADS, QK_PAD, n), BF16), _sds((N_HEADS, V_HEAD, n), BF16)]
    out_specs = [_rows(o, tm) for o in outs[:3]] + [
        pl.BlockSpec((N_HEADS, QK_PAD, tm), lambda i: (0, 0, i)), pl.BlockSpec((N_HEADS, V_HEAD, tm), lambda i: (0, 0, i))]
    return _call(body, "qkv_prep", (n // tm,), row_ins + consts,
                 [_rows(a, tm) for a in row_ins] + [_const(a) for a in consts], outs, out_specs)


def _causal_mask_t(st, t):
    key = lax.broadcasted_iota(jnp.int32, (t, t), 0)
    qry = lax.broadcasted_iota(jnp.int32, (t, t), 1)
    return jnp.where(key <= qry, st, -jnp.inf)


def _attn_fwd(q, k, vt, xch):
    n = q.shape[1]
    t = min(ATT_T, n)

    hp = ATT_HEADS

    def body(q_ref, k_ref, vt_ref, o_ref, lse_ref, ot_ref):
        i = pl.program_id(1)
        qts = [q_ref[g] for g in range(hp)]

        def kv_tile(j, carry, diag):
            ts = t // ATT_SUB
            sts = []
            for g in range(hp):
                for a in range(ATT_SUB):
                    r0 = pl.multiple_of(j * t + a * ts, ts)
                    st = _dot_nt(k_ref[g, pl.ds(r0, ts), :], qts[g])
                    if diag:
                        key = lax.broadcasted_iota(jnp.int32, (ts, t), 0) + a * ts
                        qry = lax.broadcasted_iota(jnp.int32, (ts, t), 1)
                        st = jnp.where(key <= qry, st, -jnp.inf)
                    sts.append(st)
            out = []
            for g in range(hp):
                m, l, acc = carry[g]
                for a in range(ATT_SUB):
                    st = sts[g * ATT_SUB + a]
                    r0 = pl.multiple_of(j * t + a * ts, ts)
                    m_new = jnp.maximum(m, jnp.max(st, 0, keepdims=True))
                    alpha = jnp.exp(m - m_new)
                    pt = jnp.exp(st - m_new)
                    l = alpha * l + jnp.sum(pt, 0, keepdims=True)
                    acc = alpha * acc + _dot(vt_ref[g, :, pl.ds(r0, ts)], pt.astype(BF16))
                    m = m_new
                out.append((m, l, acc))
            return tuple(out)

        one = (jnp.full((1, t), -jnp.inf, F32), jnp.zeros((1, t), F32), jnp.zeros((V_HEAD, t), F32))
        carry = lax.fori_loop(0, i, functools.partial(kv_tile, diag=False), (one,) * hp)
        for g, (m, l, acc) in enumerate(kv_tile(i, carry, True)):
            out_t = acc / l
            o_ref[:, g * V_HEAD:(g + 1) * V_HEAD] = out_t.T
            ot_ref[g * V_HEAD:(g + 1) * V_HEAD, :] = out_t.astype(BF16)
            lse_ref[g] = m + jnp.log(l)

    return _call(
        body, "attn_fwd", (N_HEADS // hp, n // t), [q, k, vt],
        [pl.BlockSpec((hp, t, QK_PAD), lambda h, i: (h, i, 0)), pl.BlockSpec((hp, n, QK_PAD), lambda h, i: (h, 0, 0)),
         pl.BlockSpec((hp, V_HEAD, n), lambda h, i: (h, 0, 0))],
        [_sds((n, N_HEADS * V_HEAD), F32), _sds((N_HEADS, 1, n), F32), _sds((N_HEADS * V_HEAD, n), BF16)],
        [pl.BlockSpec((t, hp * V_HEAD), lambda h, i: (i, h)), pl.BlockSpec((hp, 1, t), lambda h, i: (h, 0, i)),
         pl.BlockSpec((hp * V_HEAD, t), lambda h, i: (h, i))],
        xch=xch)


def _merge(attn, gs, gm, y_ssm, x, w_o_mla, w_out):
    n = x.shape[0]

    def body(at_ref, gs_ref, gm_ref, ys_ref, x_ref, wo_ref, wout_ref, h_ref, ym_ref, mxt_ref):
        y_mla = _dot(at_ref[...].astype(BF16), wo_ref[...])
        ym_ref[...] = y_mla
        mixed = _sigmoid(gs_ref[...]) * ys_ref[...] + _sigmoid(gm_ref[...]) * y_mla
        mxt_ref[...] = mixed.T.astype(BF16)
        h_ref[...] = x_ref[...] + _dot(mixed.astype(BF16), wout_ref[...])

    outs = [((n, D_MODEL), F32), ((n, D_MODEL), F32)]
    return _row_call(body, "merge", n, MM_T, [attn, gs, gm, y_ssm, x], [w_o_mla, w_out], outs,
                     col_outs=[((D_MODEL, n), BF16)])


def _mlp_fwd_loss(h, target, norm_mlp, w_up, w_down):
    n = h.shape[0]

    def body(h_ref, t_ref, g_ref, wu_ref, wd_ref, hn_ref, do_ref, hnt_ref, loss_ref):
        h_t = h_ref[...]
        hn, _ = _rms(h_t, g_ref[...], D_MODEL)
        hb = hn.astype(BF16)
        hn_ref[...] = hb
        hnt_ref[...] = hn.T.astype(BF16)
        out = h_t
        for j in range(N_DEV):
            a = jnp.maximum(_dot(hb, wu_ref[j]), 0.0)
            out += _dot((a * a).astype(BF16), wd_ref[j])
        err = out - t_ref[...]
        do_ref[...] = err * (1.0 / D_MODEL)
        _acc(loss_ref, jnp.broadcast_to(jnp.sum(err * err) * (0.5 / D_MODEL), loss_ref.shape))

    outs = [((n, D_MODEL), BF16), ((n, D_MODEL), F32)]
    return _row_call(body, "mlp_fwd_loss", n, MM_T, [h, target], [norm_mlp, w_up, w_down], outs, [((8, 128), F32)],
                     col_outs=[((D_MODEL, n), BF16)])


def _mlp_bwd(dout, hn, h, norm_mlp, w_up, w_down):
    n = h.shape[0]

    def body(do_ref, hn_ref, h_ref, g_ref, wu_ref, wd_ref, da_ref, dh_ref, dob_ref, hidt_ref, dg_ref):
        dout_t = do_ref[...]
        doutb = dout_t.astype(BF16)
        dob_ref[...] = doutb
        hb = hn_ref[...]
        dhn = jnp.zeros_like(dout_t)
        for j in range(N_DEV):
            cols = slice(j * FF_SHARD, (j + 1) * FF_SHARD)
            a = jnp.maximum(_dot(hb, wu_ref[j]), 0.0)
            hidt_ref[cols, :] = (a * a).T.astype(BF16)
            da = (_dot_nt(doutb, wd_ref[j]) * (2.0 * a)).astype(BF16)
            da_ref[:, cols] = da
            dhn += _dot_nt(da, wu_ref[j])
        h_t = h_ref[...]
        inv = lax.rsqrt(jnp.sum(h_t * h_t, -1, keepdims=True) * (1.0 / D_MODEL) + EPS)
        dx, dg = _rms_bwd(dhn, h_t, g_ref[...], inv, D_MODEL)
        dh_ref[...] = dout_t + dx
        _acc(dg_ref, jnp.sum(dg, 0, keepdims=True))

    outs = [((n, D_FF), BF16), ((n, D_MODEL), F32), ((n, D_MODEL), BF16)]
    return _row_call(body, "mlp_bwd", n, MM_T, [dout, hn, h], [norm_mlp, w_up, w_down], outs, [((1, D_MODEL), F32)],
                     col_outs=[((D_FF, n), BF16)])


def _merge_bwd(dh, gs, gm, y_ssm, y_mla, attn, w_out, w_o_mla):
    n = dh.shape[0]

    def body(dh_ref, gs_ref, gm_ref, ys_ref, ym_ref, at_ref, wout_ref, wo_ref,
             dgs_ref, dgm_ref, dys_ref, dym_ref, dat_ref, delta_ref):
        dmix = _dot_nt(dh_ref[...].astype(BF16), wout_ref[...])
        sgs, sgm = _sigmoid(gs_ref[...]), _sigmoid(gm_ref[...])
        dgs_ref[...] = (dmix * ys_ref[...] * sgs * (1.0 - sgs)).astype(BF16)
        dgm_ref[...] = (dmix * ym_ref[...] * sgm * (1.0 - sgm)).astype(BF16)
        dys_ref[...] = (dmix * sgs).astype(BF16)
        dym = (dmix * sgm).astype(BF16)
        dym_ref[...] = dym
        dattn = _dot_nt(dym, wo_ref[...])
        dat_ref[...] = dattn.astype(BF16)
        prod = dattn * at_ref[...]
        ones = jnp.ones((8, V_HEAD), F32)
        for h in range(N_HEADS):
            delta_ref[h] = _dot_exact(ones, prod[:, h * V_HEAD:(h + 1) * V_HEAD], ((1,), (1,)))[0:1, :]

    outs = [((n, D_MODEL), BF16)] * 5
    return _row_call(body, "merge_bwd", n, MM_T, [dh, gs, gm, y_ssm, y_mla, attn], [w_out, w_o_mla], outs,
                     col_outs=[((N_HEADS, 1, n), F32)])


def _attn_bwd(q, k, kt, v, lse, delta, dout, xch):
    n = q.shape[1]
    t = min(ATT_T, n)
    nt = n // t
    hp = ATT_BWD_HEADS

    def body(q_ref, k_ref, kt_ref, v_ref, lse_ref, delta_ref, do_ref, dq_ref, dk_ref, dv_ref, dqt_ref):
        j = pl.program_id(1)

        @pl.when(j == 0)
        def _():
            dqt_ref[...] = jnp.zeros_like(dqt_ref)

        def q_tile(i, carry, diag):
            r0 = pl.multiple_of(i * t, t)
            rows = pl.ds(r0, t)
            qts = [q_ref[g, rows, :] for g in range(hp)]
            sts = [_dot_nt(k_ref[g], qts[g]) for g in range(hp)]
            out = []
            for g in range(hp):
                dk, dv = carry[g]
                st = _causal_mask_t(sts[g], t) if diag else sts[g]
                pt = jnp.exp(st - lse_ref[g, :, rows])
                dob = do_ref[rows, g * V_HEAD:(g + 1) * V_HEAD]
                dv = dv + _dot(pt.astype(BF16), dob)
                dst = (pt * (_dot_nt(v_ref[g], dob) - delta_ref[g, :, rows])).astype(BF16)
                dk = dk + _dot(dst, qts[g])
                dqt_ref[g, :, rows] += _dot(kt_ref[g], dst)
                out.append((dk, dv))
            return tuple(out)

        zero = (jnp.zeros((t, QK_PAD), F32), jnp.zeros((t, V_HEAD), F32))
        carry = q_tile(j, (zero,) * hp, True)
        carry = lax.fori_loop(j + 1, nt, functools.partial(q_tile, diag=False), carry)
        for g, (dk, dv) in enumerate(carry):
            dk_ref[g] = dk
            dv_ref[g] = dv

        @pl.when(j == nt - 1)
        def _():
            for g in range(hp):
                for c in range(0, n, t):
                    dq_ref[g, c:c + t, :] = dqt_ref[g, :, c:c + t].T

    return _call(
        body, "attn_bwd", (N_HEADS // hp, nt), [q, k, kt, v, lse, delta, dout],
        [pl.BlockSpec((hp, n, QK_PAD), lambda h, j: (h, 0, 0)), pl.BlockSpec((hp, t, QK_PAD), lambda h, j: (h, j, 0)),
         pl.BlockSpec((hp, QK_PAD, t), lambda h, j: (h, 0, j)), pl.BlockSpec((hp, t, V_HEAD), lambda h, j: (h, j, 0)),
         pl.BlockSpec((hp, 1, n), lambda h, j: (h, 0, 0)), pl.BlockSpec((hp, 1, n), lambda h, j: (h, 0, 0)),
         pl.BlockSpec((n, hp * V_HEAD), lambda h, j: (0, h))],
        [_sds((N_HEADS, n, QK_PAD), F32), _sds((N_HEADS, n, QK_PAD), F32), _sds((N_HEADS, n, V_HEAD), F32)],
        [pl.BlockSpec((hp, n, QK_PAD), lambda h, j: (h, 0, 0)), pl.BlockSpec((hp, t, QK_PAD), lambda h, j: (h, j, 0)),
         pl.BlockSpec((hp, t, V_HEAD), lambda h, j: (h, j, 0))],
        scratch=[pltpu.VMEM((hp, QK_PAD, n), F32)],
        xch=xch)


def _qkv_prep_bwd(ql, kvl, dq, dk, dv, q_a_norm, kv_a_norm, wq, wkv, gq, gk, cos_t, sin_t, xch):
    n = ql.shape[0]

    def body(ql_ref, kvl_ref, cos_ref, sin_ref, dq_ref, dk_ref, dv_ref, qa_ref, ka_ref, wq_ref, wkv_ref, gq_ref, gk_ref,
             dql_ref, dkvl_ref, dqa_ref, dka_ref, dgq_ref, dgk_ref, dwq_ref, dwkv_ref, dqp_ref, dkvp_ref):
        cos_t, sin_t = cos_ref[...], sin_ref[...]
        ql_t = ql_ref[...]
        qa, inv_qa = _rms(ql_t, qa_ref[...], Q_LORA)
        qab = qa.astype(BF16)
        kvl_t = kvl_ref[...]
        ckv = kvl_t[:, 0:KV_LORA]
        ca, inv_ca = _rms(ckv, ka_ref[...], KV_LORA)
        cab = ca.astype(BF16)
        kpe = kvl_t[:, KV_LORA:KV_LAT_PAD]
        dgq = jnp.zeros((1, QK_PAD), F32)
        dgk = jnp.zeros((1, QK_PAD), F32)
        dkpe = jnp.zeros_like(kpe)
        q_pre = _dot(qab, wq_ref[...])
        kv_pre = _dot(cab, wkv_ref[...])
        for h in range(N_HEADS):
            head = slice(h * QK_PAD, (h + 1) * QK_PAD)
            q_slab = q_pre[:, head]
            inv = lax.rsqrt(jnp.sum(q_slab * q_slab, -1, keepdims=True) * (1.0 / QK_HEAD) + EPS)
            d_slab, dg = _head_norm_rope_bwd(dq_ref[h] * ATT_SCALE, q_slab, gq_ref[...], inv, cos_t, sin_t)
            dqp_ref[:, head] = d_slab.astype(BF16)
            dgq += jnp.sum(dg, 0, keepdims=True)
            k_slab = jnp.concatenate([kv_pre[:, h * QK_PAD:h * QK_PAD + QK_NOPE], kpe], axis=-1)
            inv = lax.rsqrt(jnp.sum(k_slab * k_slab, -1, keepdims=True) * (1.0 / QK_HEAD) + EPS)
            d_slab, dg = _head_norm_rope_bwd(dk_ref[h], k_slab, gk_ref[...], inv, cos_t, sin_t)
            dkvp_ref[:, head] = jnp.concatenate([d_slab[:, 0:QK_NOPE], dv_ref[h]], axis=-1).astype(BF16)
            dkpe += d_slab[:, QK_NOPE:QK_PAD]
            dgk += jnp.sum(dg, 0, keepdims=True)
        dqa = _dot_nt(dqp_ref[...], wq_ref[...])
        dx, dg = _rms_bwd(dqa, ql_t, qa_ref[...], inv_qa, Q_LORA)
        dql_ref[...] = dx.astype(BF16)
        _acc(dqa_ref, jnp.sum(dg, 0, keepdims=True))
        dca = _dot_nt(dkvp_ref[...], wkv_ref[...])
        dx, dg = _rms_bwd(dca, ckv, ka_ref[...], inv_ca, KV_LORA)
        dkvl_ref[:, 0:KV_LORA] = dx.astype(BF16)
        dkvl_ref[:, KV_LORA:KV_LAT_PAD] = dkpe.astype(BF16)
        _acc(dka_ref, jnp.sum(dg, 0, keepdims=True))
        _acc(dgq_ref, dgq)
        _acc(dgk_ref, dgk)
        _acc(dwq_ref, _dot_tn(qab, dqp_ref[...]))
        _acc(dwkv_ref, _dot_tn(cab, dkvp_ref[...]))

    wide = N_HEADS * QK_PAD
    row_outs = [((n, Q_LORA), BF16), ((n, KV_LAT_PAD), BF16)]
    acc_outs = [((1, Q_LORA), F32), ((1, KV_LORA), F32), ((1, QK_PAD), F32), ((1, QK_PAD), F32),
                ((Q_LORA, wide), F32), ((KV_LORA, wide), F32)]
    return _row_call(body, "qkv_prep_bwd", n, ROW_T, [ql, kvl, cos_t, sin_t, dq, dk, dv],
                     [q_a_norm, kv_a_norm, wq, wkv, gq, gk], row_outs, acc_outs, xch=xch,
                     scratch=[pltpu.VMEM((ROW_T, wide), BF16), pltpu.VMEM((ROW_T, wide), BF16)])


def _glu_bwd(dy_ssm, y, w_glu, b_glu, w_o_ssm):
    n = y.shape[0]

    def body(dys_ref, y_ref, wg_ref, bg_ref, wo_ref, dy_ref, db_ref, dwg_ref, dwo_ref):
        y_t = y_ref[...]
        z, th = _gelu(y_t)
        zb = z.astype(BF16)
        s = _sigmoid(_dot(zb, wg_ref[...]) + bg_ref[...])
        dys = dys_ref[...]
        dzg = jnp.zeros_like(y_t)
        for j in range(N_DEV):
            dzg += _dot_nt(dys[:, j * OUT_SHARD:(j + 1) * OUT_SHARD], wo_ref[j])
        dt = dzg * z * s * (1.0 - s)
        dtb = dt.astype(BF16)
        dz = dzg * s + _dot_nt(dtb, wg_ref[...])
        dy_ref[...] = dz * _gelu_grad(y_t, th)
        _acc(db_ref, jnp.sum(dt, 0, keepdims=True))
        _acc(dwg_ref, _dot_tn(zb, dtb))
        _acc(dwo_ref, _dot_tn((z * s).astype(BF16), dys))

    acc_outs = [((1, SSM_WIDTH), F32), ((SSM_WIDTH, SSM_WIDTH), F32), ((SSM_WIDTH, D_MODEL), F32)]
    return _row_call(body, "glu_bwd", n, ROW_T, [dy_ssm, y], [w_glu, b_glu, w_o_ssm], [((n, SSM_WIDTH), F32)], acc_outs)


def _ssm_bwd(u, dy, st, bblk, cblk, lam, d_row, xch):
    n = u.shape[0]
    t = min(SCAN_T, n)
    nc = n // t
    kb = 512
    perm = _perm_matrix(t)

    def body(u_ref, dy_ref, st_ref, p_ref, bblk_ref, cblk_ref, lam_ref, d_ref,
             du_ref, dlam_ref, dd_ref, db_ref, dct_ref,
             buf_x, buf_a, pw_ref, carry_ref, xcarry_ref, sx_ref, sa_ref, db_acc, dct_acc):
        @pl.when(pl.program_id(0) == 0)
        def _():
            carry_ref[...] = jnp.zeros_like(carry_ref)
            db_acc[...] = jnp.zeros_like(db_acc)
            dct_acc[...] = jnp.zeros_like(dct_acc)
            _power_table(lam_ref, pw_ref, t // SUBCHUNKS)

        u_t = u_ref[...]
        dy_t = dy_ref[...]
        p = p_ref[...]
        ub = _dot(p, u_t.astype(BF16)).astype(BF16)
        dyb = _dot(p, dy_t.astype(BF16)).astype(BF16)
        _to_states(ub, bblk_ref, buf_x, False)
        xcarry_ref[...] = st_ref[0]
        _run_scan(buf_x, lam_ref, t, False)
        _run_carries(buf_x, pw_ref, xcarry_ref, sx_ref, t, False)
        _run_fix(buf_x, pw_ref, sx_ref, t, False)
        _to_states(dyb, cblk_ref, buf_a, True)
        _run_scan(buf_a, lam_ref, t, True)
        _run_carries(buf_a, pw_ref, carry_ref, sa_ref, t, True)
        _run_fix(buf_a, pw_ref, sa_ref, t, True)
        du_ref[...] = (d_ref[...] * dy_t + _unpermute(p, _to_channels(buf_a, bblk_ref, True))).astype(BF16)
        for b in range(STATE_BLOCKS):
            lanes, ch = _state_block(b)
            db_acc[ch, lanes] += _dot_tn(ub[:, ch], buf_a[:, lanes].astype(BF16))
            dct_acc[ch, lanes] += _dot_tn(dyb[:, ch], buf_x[:, lanes].astype(BF16))
        for c in range(0, N_STATE, kb):
            re, im = pl.ds(c, kb), pl.ds(N_STATE + c, kb)
            xr, xi = buf_x[pl.ds(0, t - 8), re], buf_x[pl.ds(0, t - 8), im]
            ar, ai = buf_a[pl.ds(8, t - 8), re], buf_a[pl.ds(8, t - 8), im]
            x0r, x0i = sx_ref[:, re], sx_ref[:, im]
            a0r, a0i = buf_a[0:8, re], buf_a[0:8, im]
            dlam_part_re = (jnp.sum(ar * xr + ai * xi, 0, keepdims=True)
                            + jnp.sum(a0r * x0r + a0i * x0i, 0, keepdims=True))
            dlam_part_im = (jnp.sum(ai * xr - ar * xi, 0, keepdims=True)
                            + jnp.sum(a0i * x0r - a0r * x0i, 0, keepdims=True))

            @pl.when(pl.program_id(0) == 0)
            def _(c=c):
                dlam_ref[0:1, c:c + kb] = jnp.zeros((1, kb), F32)
                dlam_ref[1:2, c:c + kb] = jnp.zeros((1, kb), F32)

            dlam_ref[0:1, c:c + kb] += dlam_part_re
            dlam_ref[1:2, c:c + kb] += dlam_part_im
        _acc(dd_ref, jnp.sum(dy_t * u_t, 0, keepdims=True))

        @pl.when(pl.program_id(0) == nc - 1)
        def _():
            pltpu.sync_copy(db_acc, db_ref)
            pltpu.sync_copy(dct_acc, dct_ref)

    rev = lambda i: (nc - 1 - i, 0)
    consts = [perm, bblk, cblk, lam, d_row]
    wide = (SSM_WIDTH, 2 * N_STATE)
    return _call(
        body, "ssm_bwd", (nc,), [u, dy, st] + consts,
        [pl.BlockSpec((t, SSM_WIDTH), rev), pl.BlockSpec((t, SSM_WIDTH), rev),
         pl.BlockSpec((1, 8, 2 * N_STATE), lambda i: (nc - 1 - i, 0, 0))] + [_const(a) for a in consts],
        [_sds((n, SSM_WIDTH), BF16), _sds((2, N_STATE), F32), _sds((1, SSM_WIDTH), F32), _sds(wide, F32), _sds(wide, F32)],
        [pl.BlockSpec((t, SSM_WIDTH), rev), pl.BlockSpec((2, N_STATE), lambda i: (0, 0)),
         pl.BlockSpec((1, SSM_WIDTH), lambda i: (0, 0)), ANY, ANY],
        scratch=[pltpu.VMEM((t, 2 * N_STATE), F32)] * 2 + [pltpu.VMEM((t // SUBCHUNKS, 2 * N_STATE), F32)]
        + [pltpu.VMEM((8, 2 * N_STATE), F32)] * 4 + [pltpu.VMEM(wide, F32)] * 2,
        xch=xch)


def _in_proj_bwd(pieces, dh, x, xn_t, norm_mix, w_in_pad, xch):
    n = x.shape[0]
    tm = min(MM_T, n)
    nt = n // tm

    def body(du_ref, dql_ref, dkvl_ref, dgs_ref, dgm_ref, dh_ref, x_ref, xnt_ref, g_ref, w_ref,
             dx_ref, dg_ref, dw_ref, acc_ref):
        @pl.when(pl.program_id(0) == 0)
        def _():
            acc_ref[...] = jnp.zeros_like(acc_ref)

        xnt = xnt_ref[...]
        dxn = jnp.zeros((tm, D_MODEL), F32)
        for ref, (a, b) in zip((du_ref, dql_ref, dkvl_ref, dgs_ref, dgm_ref), IN_SEGS):
            piece = ref[...]
            dxn += _dot_nt(piece, w_ref[:, a:b])
            acc_ref[:, a:b] += _dot(xnt, piece)
        x_t = x_ref[...]
        inv = lax.rsqrt(jnp.sum(x_t * x_t, -1, keepdims=True) * (1.0 / D_MODEL) + EPS)
        dx, dg = _rms_bwd(dxn, x_t, g_ref[...], inv, D_MODEL)
        dx_ref[...] = dh_ref[...] + dx
        _acc(dg_ref, jnp.sum(dg, 0, keepdims=True))

        @pl.when(pl.program_id(0) == nt - 1)
        def _():
            pltpu.sync_copy(acc_ref, dw_ref)

    row_ins, consts = list(pieces) + [dh, x], [norm_mix, w_in_pad]
    in_specs = ([_rows(a, tm) for a in row_ins] + [pl.BlockSpec((D_MODEL, tm), lambda i: (0, i))]
                + [_const(a) for a in consts])
    return _call(
        body, "in_proj_bwd", (nt,), row_ins + [xn_t] + consts, in_specs,
        [_sds((n, D_MODEL), F32), _sds((1, D_MODEL), F32), _sds((D_MODEL, D_IN_PAD), F32)],
        [pl.BlockSpec((tm, D_MODEL), lambda i: (i, 0)), pl.BlockSpec((1, D_MODEL), lambda i: (0, 0)), ANY],
        scratch=[pltpu.VMEM((D_MODEL, D_IN_PAD), F32)], xch=xch)


def _swap_minor(a):
    g, r, c = a.shape[1:]
    return jnp.transpose(a[0], (0, 2, 1)).reshape(g * c, r)


def _pad_in(w):
    return jnp.concatenate([w[:, :KV_END], jnp.zeros((w.shape[0], D_IN_PAD - D_IN), w.dtype), w[:, KV_END:]], axis=1)


def _unpad_in(w):
    return jnp.concatenate([w[:, :KV_END], w[:, KV_END + D_IN_PAD - D_IN:]], axis=1)


def _pad_gain(g):
    return jnp.pad(g, ((0, 0), (0, QK_PAD - QK_HEAD)))


def _place():
    x, y, c = lax.axis_index("x"), lax.axis_index("y"), lax.axis_index("c")
    chips = [(x, y), (1 - x, y), (x, 1 - y), (1 - x, 1 - y)]
    return x, y, c, chips


def _all_gather(block, name):
    rows, lanes = block.shape

    def body(x_ref, out_ref, send_sems, recv_sems, local_sem):
        x, y, c, chips = _place()
        me, sibling = (x, y, c), (x, y, 1 - c)

        def slot(px, py, pc):
            return out_ref.at[4 * px + 2 * py + pc]

        def copy(k, blk, to, src=None):
            return pltpu.make_async_remote_copy(
                src_ref=slot(*blk) if src is None else src, dst_ref=slot(*blk),
                send_sem=send_sems.at[k], recv_sem=recv_sems.at[k], device_id=to, device_id_type=MESH)

        mine = pltpu.make_async_copy(x_ref, slot(*me), local_sem)
        mine.start()
        first = [copy(0, me, sibling, src=x_ref)]
        first += [copy(1 + j, me, (*chip, c), src=x_ref) for j, chip in enumerate(chips[1:])]
        for cp in first:
            cp.start()
        passed = [copy(4 + j, (*chip, c), sibling) for j, chip in enumerate(chips[1:])]
        for j, chip in enumerate(chips[1:]):
            copy(1 + j, (*chip, c), me).wait_recv()
            passed[j].start()
        copy(0, sibling, me).wait_recv()
        for j, chip in enumerate(chips[1:]):
            copy(4 + j, (*chip, 1 - c), me).wait_recv()
        for cp in first + passed:
            cp.wait_send()
        mine.wait()

    return pl.pallas_call(
        body,
        name=name,
        in_specs=[ANY],
        out_specs=ANY,
        out_shape=_sds((N_DEV, rows, lanes), block.dtype),
        scratch_shapes=[pltpu.SemaphoreType.DMA((7,)), pltpu.SemaphoreType.DMA((7,)), pltpu.SemaphoreType.DMA],
    )(block)


def _reduce_scatter(parts, gather, name):
    _, rows, lanes = parts.shape

    def body(p_ref, g_ref, out_ref, ga_ref, own, land_a, send_b, land_b, sa, ra, sb, rb, lo, *g_sems):
        x, y, c, chips = _place()
        sibling = (x, y, 1 - c)
        _xchg_start([False], [g_ref], [ga_ref], *g_sems)

        def blk(chip, core):
            return p_ref.at[4 * chip[0] + 2 * chip[1] + core]

        to_sib = [pltpu.make_async_remote_copy(
            src_ref=blk(chips[k], 1 - c), dst_ref=land_a.at[k], send_sem=sa.at[k], recv_sem=ra.at[k],
            device_id=sibling, device_id_type=MESH) for k in range(4)]
        for cp in to_sib:
            cp.start()
        loads = [pltpu.make_async_copy(blk(chips[k], c), own.at[k], lo.at[k]) for k in range(4)]
        for cp in loads:
            cp.start()
        to_chip = [pltpu.make_async_remote_copy(
            src_ref=send_b.at[j], dst_ref=land_b.at[j], send_sem=sb.at[j], recv_sem=rb.at[j],
            device_id=(*chips[1 + j], c), device_id_type=MESH) for j in range(3)]
        for k in (1, 2, 3):
            to_sib[k].wait_recv()
            loads[k].wait()
            send_b[k - 1] = (own[k] + land_a[k]).astype(BF16)
            to_chip[k - 1].start()
        to_sib[0].wait_recv()
        loads[0].wait()
        acc = own[0] + land_a[0]
        for j in range(3):
            to_chip[j].wait_recv()
            acc = acc + land_b[j].astype(F32)
        out_ref[...] = acc
        for cp in to_sib + to_chip:
            cp.wait_send()
        _xchg_wait([False], [g_ref], [ga_ref], *g_sems)

    return pl.pallas_call(
        body,
        name=name,
        in_specs=[ANY, ANY],
        out_specs=[pl.BlockSpec(memory_space=pltpu.VMEM), ANY],
        out_shape=[_sds((rows, lanes), F32), _sds((N_DEV,) + gather.shape, gather.dtype)],
        scratch_shapes=[pltpu.VMEM((4, rows, lanes), F32), pltpu.VMEM((4, rows, lanes), F32),
                        pltpu.VMEM((3, rows, lanes), BF16), pltpu.VMEM((3, rows, lanes), BF16)]
        + [pltpu.SemaphoreType.DMA((4,))] * 2 + [pltpu.SemaphoreType.DMA((3,))] * 2 + [pltpu.SemaphoreType.DMA((4,))]
        + [pltpu.SemaphoreType.DMA((1,))] * 3,
        compiler_params=_params(),
    )(parts, gather)


def _adamw_math(w, g, m, v):
    m = ADAM_B1 * m + (1.0 - ADAM_B1) * g
    v = ADAM_B2 * v + (1.0 - ADAM_B2) * (g * g)
    m_hat = m / (1.0 - ADAM_B1 ** ADAM_STEP)
    v_hat = v / (1.0 - ADAM_B2 ** ADAM_STEP)
    delta = -ADAM_LR * (m_hat / (jnp.sqrt(v_hat) + ADAM_EPS) + ADAM_WD * w)
    return delta, m, v


def _row_tile(r):
    return max(t for t in range(8, min(r, 256) + 1, 8) if r % t == 0)


def _adamw(w, g, m, v, name):
    r, n = w.shape

    def body(w_ref, g_ref, m_ref, v_ref, d_ref, nm_ref, nv_ref):
        d_ref[...], nm_ref[...], nv_ref[...] = _adamw_math(w_ref[...], g_ref[...], m_ref[...], v_ref[...])

    return _row_call(body, name, r, _row_tile(r), [w, g, m, v], [], [((r, n), F32)] * 3)


def _adamw_sum(landed, w, m, v, name):
    r, n = w.shape

    def body(l_ref, w_ref, m_ref, v_ref, g_ref, d_ref, nm_ref, nv_ref):
        g = l_ref[0].astype(F32)
        for dev in range(1, N_DEV):
            g = g + l_ref[dev].astype(F32)
        g_ref[...] = g
        d_ref[...], nm_ref[...], nv_ref[...] = _adamw_math(w_ref[...], g, m_ref[...], v_ref[...])

    tm = max(t for t in range(16, min(r, 256) + 1, 16) if r % t == 0)
    return _row_call(body, name, r, tm, [landed, w, m, v], [], [((r, n), F32)] * 4)


def _adamw_small(first, rest, w, m, v, row_counts):
    n_rest = w.shape[0] - first.shape[1]

    def body(f_ref, r_ref, w_ref, m_ref, v_ref, loss_ref, *out_refs):
        gf, gr = f_ref[0], r_ref[0]
        for dev in range(1, N_DEV):
            gf, gr = gf + f_ref[dev], gr + r_ref[dev]
        loss_ref[...] = gr[n_rest:n_rest + 8]
        g = jnp.concatenate([gf, gr[0:n_rest]], axis=0)
        d, nm, nv = _adamw_math(w_ref[...], g, m_ref[...], v_ref[...])
        off = 0
        for p, rows in enumerate(row_counts):
            for k, val in enumerate((g, d, nm, nv)):
                out_refs[4 * p + k][...] = val[off:off + rows]
            off += rows

    outs = [_sds((8, LANES), F32)] + [_sds((rows, LANES), F32) for rows in row_counts for _ in range(4)]
    return pl.pallas_call(body, name="adamw_small", out_shape=outs, compiler_params=_params())(first, rest, w, m, v)


SMALL = ("norm_mix", "q_a_norm", "kv_a_norm", "q_norm", "k_norm", "ssm_a_re", "ssm_a_im", "ssm_log_dt", "ssm_b_re",
         "ssm_b_im", "ssm_c_re", "ssm_c_im", "ssm_d", "b_glu", "norm_mlp")
WEIGHT_ORDER = ("norm_mix", "w_in", "q_a_norm", "kv_a_norm", "w_q_b", "w_kv_b", "q_norm", "k_norm", "w_o_mla",
                "ssm_a_re", "ssm_a_im", "ssm_log_dt", "ssm_b_re", "ssm_b_im", "ssm_c_re", "ssm_c_im", "ssm_d", "w_glu",
                "b_glu", "w_o_ssm", "w_out", "norm_mlp", "w_up", "w_down")
IN_SHARD = D_IN // N_DEV


def _pack_small(vals, names=SMALL):
    parts = []
    for n in names:
        flat = vals[n].reshape(-1)
        size = -(-flat.shape[0] // (8 * LANES)) * 8 * LANES
        parts.append(jnp.pad(flat, (0, size - flat.shape[0])).reshape(-1, LANES))
    return jnp.concatenate(parts, axis=0)


def _small_rows(like):
    return [-(-like[n].size // (8 * LANES)) * 8 for n in SMALL]


def _step(x, pos_col, target, w, small):
    bf = {n: a.astype(BF16) for n, a in w.items()}
    gq, gk = _pad_gain(small["q_norm"]), _pad_gain(small["k_norm"])
    a_re = small["ssm_a_re"].reshape(1, N_STATE)
    a_im = small["ssm_a_im"].reshape(1, N_STATE)
    log_dt = jnp.repeat(small["ssm_log_dt"].reshape(SSM_GROUPS), SSM_STATE).reshape(1, N_STATE)
    bt_re, bt_im = _swap_minor(small["ssm_b_re"]), _swap_minor(small["ssm_b_im"])
    c2_re, c2_im = _swap_minor(small["ssm_c_re"]), _swap_minor(small["ssm_c_im"])
    d_row = small["ssm_d"].reshape(1, SSM_WIDTH)

    w_in_all = _all_gather(bf["w_in"], "gather_w_in")
    w_in_pad = _pad_in(jnp.transpose(w_in_all, (1, 0, 2)).reshape(D_MODEL, D_IN))
    cos_t, sin_t = _rope_tables(pos_col)
    lam, bblk, cblk = _ssm_prep(a_re, a_im, log_dt, bt_re, bt_im, c2_re, c2_im)
    wq_mine = jnp.pad(bf["w_q_b"], ((0, 0), (0, QK_PAD - QK_HEAD)))
    u, ql, kvl, gs, gm, xn_t, w_glu, w_o_ssm = _in_proj(
        x, small["norm_mix"], w_in_pad, xch=[(bf["w_glu"], False), (bf["w_o_ssm"], False)])
    w_glu = w_glu.reshape(SSM_WIDTH, SSM_WIDTH)
    y, y_ssm, st, wq, wkv, w_o_mla, w_out = _ssm_fwd(
        u, bblk, cblk, lam, d_row, w_glu, small["b_glu"], w_o_ssm,
        xch=[(wq_mine, False), (bf["w_kv_b"], False), (bf["w_o_mla"], False), (bf["w_out"], False)])
    w_o_mla, w_out = w_o_mla.reshape(D_MODEL, D_MODEL), w_out.reshape(D_MODEL, D_MODEL)
    wq = jnp.transpose(wq, (1, 0, 2)).reshape(Q_LORA, N_HEADS * QK_PAD)
    wkv = jnp.transpose(wkv, (1, 0, 2)).reshape(KV_LORA, N_HEADS * QK_PAD)
    q, k, v, kt, vt = _qkv_prep(ql, kvl, small["q_a_norm"], small["kv_a_norm"], wq, wkv, gq, gk, cos_t, sin_t)
    attn, lse, attn_t, w_up, w_down = _attn_fwd(q, k, vt, xch=[(bf["w_up"], False), (bf["w_down"], False)])
    h, y_mla, mixed_t = _merge(attn, gs, gm, y_ssm, x, w_o_mla, w_out)
    hn, dout, hn_t, loss = _mlp_fwd_loss(h, target, small["norm_mlp"], w_up, w_down)

    da, dh, dout_b, hid_t, d_norm_mlp = _mlp_bwd(dout, hn, h, small["norm_mlp"], w_up, w_down)
    p_w_down = _matmul_tn_shards(hid_t, dout_b, "dw_down", False, tm=1024, turned=True)
    p_w_up = _matmul_tn_shards(hn_t, da, "dw_up", True, turned=True)
    dgs, dgm, dy_ssm, dy_mla, dattn, delta = _merge_bwd(dh, gs, gm, y_ssm, y_mla, attn, w_out, w_o_mla)
    p_w_out = _matmul_tn_shards(mixed_t, dh, "dw_out", False, tm=1024, turned=True)
    p_w_o_mla = _matmul_tn_shards(attn_t, dy_mla, "dw_o_mla", False, turned=True)
    dq, dk, dv, l_w_up, l_w_down, l_w_out, l_w_o_mla = _attn_bwd(
        q, k, kt, v, lse, delta, dattn, xch=[(p_w_up, True), (p_w_down, True), (p_w_out, True), (p_w_o_mla, True)])
    dql, dkvl, d_q_a_norm, d_kv_a_norm, d_gq, d_gk, g_wq, g_wkv = _qkv_prep_bwd(
        ql, kvl, dq, dk, dv, small["q_a_norm"], small["kv_a_norm"], wq, wkv, gq, gk, cos_t, sin_t, xch=[])
    p_wq = jnp.transpose(g_wq.reshape(Q_LORA, N_HEADS, QK_PAD), (1, 0, 2)).astype(BF16)
    p_wkv = jnp.transpose(g_wkv.reshape(KV_LORA, N_HEADS, QK_PAD), (1, 0, 2)).astype(BF16)
    dy, d_b_glu, g_w_glu, g_w_o_ssm = _glu_bwd(dy_ssm, y, w_glu, small["b_glu"], w_o_ssm)
    p_w_o_ssm = jnp.transpose(g_w_o_ssm.reshape(SSM_WIDTH, N_DEV, OUT_SHARD), (1, 0, 2)).astype(BF16)
    p_w_glu = g_w_glu.reshape(N_DEV, SSM_WIDTH // N_DEV, SSM_WIDTH).astype(BF16)
    du, dlam, d_d, d_bblk, d_cblk_t, l_wq, l_wkv, l_w_glu, l_w_o_ssm = _ssm_bwd(
        u, dy, st, bblk, cblk, lam, d_row, xch=[(p_wq, True), (p_wkv, True), (p_w_glu, True), (p_w_o_ssm, True)])
    d_a_re, d_a_im, d_log_dt, d_bt_re, d_bt_im, d_c_re, d_c_im = _ssm_prep_bwd(
        a_re, a_im, log_dt, bt_re, bt_im, dlam, d_bblk, d_cblk_t)
    tr = lambda mat: jnp.transpose(mat.reshape(SSM_GROUPS, SSM_GROUP_CH, SSM_STATE), (0, 2, 1))
    g_small = {
        "q_a_norm": d_q_a_norm, "kv_a_norm": d_kv_a_norm, "q_norm": d_gq[:, :QK_HEAD], "k_norm": d_gk[:, :QK_HEAD],
        "ssm_a_re": d_a_re, "ssm_a_im": d_a_im, "ssm_log_dt": d_log_dt,
        "ssm_b_re": tr(d_bt_re), "ssm_b_im": tr(d_bt_im), "ssm_c_re": d_c_re, "ssm_c_im": d_c_im,
        "ssm_d": d_d, "b_glu": d_b_glu, "norm_mlp": d_norm_mlp,
    }
    rest = jnp.concatenate([_pack_small(g_small, SMALL[1:]), loss], axis=0)
    dx, d_norm_mix, g_w_in_pad, g_rest_all = _in_proj_bwd(
        (du, dql, dkvl, dgs, dgm), dh, x, xn_t, small["norm_mix"], w_in_pad, xch=[(rest, False)])
    parts = jnp.transpose(_unpad_in(g_w_in_pad).reshape(D_MODEL, N_DEV, IN_SHARD), (1, 0, 2))
    g_w_in_mine, g_first_all = _reduce_scatter(parts, _pack_small({SMALL[0]: d_norm_mix}, SMALL[:1]), "reduce_w_in")
    landed = {"w_q_b": l_wq[:, :, :QK_HEAD], "w_kv_b": l_wkv, "w_o_mla": l_w_o_mla, "w_glu": l_w_glu,
              "w_o_ssm": l_w_o_ssm, "w_out": l_w_out, "w_up": l_w_up, "w_down": l_w_down}
    return dx, landed, g_w_in_mine, g_first_all, g_rest_all


def kernel(x, positions, norm_mix, w_in, q_a_norm, kv_a_norm, w_q_b, w_kv_b, q_norm, k_norm, w_o_mla, ssm_a_re, ssm_a_im, ssm_log_dt, ssm_b_re, ssm_b_im, ssm_c_re, ssm_c_im, ssm_d, w_glu, b_glu, w_o_ssm, w_out, norm_mlp, w_up, w_down, loss_target, m_norm_mix, m_w_in, m_q_a_norm, m_kv_a_norm, m_w_q_b, m_w_kv_b, m_q_norm, m_k_norm, m_w_o_mla, m_ssm_a_re, m_ssm_a_im, m_ssm_log_dt, m_ssm_b_re, m_ssm_b_im, m_ssm_c_re, m_ssm_c_im, m_ssm_d, m_w_glu, m_b_glu, m_w_o_ssm, m_w_out, m_norm_mlp, m_w_up, m_w_down, v_norm_mix, v_w_in, v_q_a_norm, v_kv_a_norm, v_w_q_b, v_w_kv_b, v_q_norm, v_k_norm, v_w_o_mla, v_ssm_a_re, v_ssm_a_im, v_ssm_log_dt, v_ssm_b_re, v_ssm_b_im, v_ssm_c_re, v_ssm_c_im, v_ssm_d, v_w_glu, v_b_glu, v_w_o_ssm, v_w_out, v_norm_mlp, v_w_up, v_w_down):
    given = dict(locals())
    w = {n: given[n] for n in WEIGHT_ORDER}
    m = {n: given["m_" + n] for n in WEIGHT_ORDER}
    v = {n: given["v_" + n] for n in WEIGHT_ORDER}
    big = [n for n in WEIGHT_ORDER if n not in SMALL]
    small = {n: w[n] for n in SMALL}

    dx, landed, g_w_in, g_first_all, g_rest_all = _step(
        x[0], positions.reshape(-1, 1), loss_target[0], {n: w[n][0] for n in big}, small)

    grads, deltas, new_m, new_v = {}, {}, {}, {}
    for n in big:
        if n in ("w_in", "w_q_b"):
            wt, mt, vt = jnp.transpose(w[n][0]), jnp.transpose(m[n][0]), jnp.transpose(v[n][0])
            if n == "w_in":
                g = jnp.transpose(g_w_in)
                d, nm, nv = _adamw(wt, g, mt, vt, "adamw_" + n)
            else:
                g, d, nm, nv = _adamw_sum(jnp.transpose(landed[n], (0, 2, 1)), wt, mt, vt, "adamw_" + n)
            g, d, nm, nv = (jnp.transpose(a) for a in (g, d, nm, nv))
        else:
            g, d, nm, nv = _adamw_sum(landed[n], w[n][0], m[n][0], v[n][0], "adamw_" + n)
        grads[n], deltas[n], new_m[n], new_v[n] = g[None], d[None], nm[None], nv[None]

    outs = _adamw_small(g_first_all, g_rest_all, _pack_small(small), _pack_small({n: m[n] for n in SMALL}),
                        _pack_small({n: v[n] for n in SMALL}), _small_rows(small))
    for p, n in enumerate(SMALL):
        for k, dst in enumerate((grads, deltas, new_m, new_v)):
            dst[n] = outs[1 + 4 * p + k].reshape(-1)[:small[n].size].reshape(small[n].shape)

    return (outs[0][0, 0], dx[None], *[grads[n] for n in WEIGHT_ORDER], *[deltas[n] for n in WEIGHT_ORDER],
            *[new_m[n] for n in WEIGHT_ORDER], *[new_v[n] for n in WEIGHT_ORDER])
```

```python
import functools
import math

import numpy as np
import jax
import jax.numpy as jnp
from jax import lax
from jax.experimental import pallas as pl
from jax.experimental.pallas import tpu as pltpu

F32 = jnp.float32
BF16 = jnp.bfloat16

D_MODEL = 1024
SSM_GROUPS = 32
SSM_GROUP_CH = 16
SSM_WIDTH = 512
SSM_STATE = 64
N_STATE = SSM_GROUPS * SSM_STATE
N_HEADS = 8
QK_NOPE = 128
QK_ROPE = 64
QK_HEAD = 192
QK_PAD = 256
V_HEAD = 128
Q_LORA = 384
KV_LORA = 256
KV_LAT_PAD = 384
ROPE_THETA = 10000.0
D_FF = 4096
EPS = 1e-6
ATT_SCALE = QK_HEAD ** -0.5
N_DEV = 8
FF_SHARD = D_FF // N_DEV
OUT_SHARD = D_MODEL // N_DEV

IN_SEGS = ((0, 512), (512, 896), (896, 1280), (1280, 2304), (2304, 3328))
D_IN = 3264
D_IN_PAD = 3328
KV_END = 1216

ADAM_LR = 0.001
ADAM_B1 = 0.9
ADAM_B2 = 0.999
ADAM_EPS = 1e-08
ADAM_WD = 0.01
ADAM_STEP = 10

VMEM_LIMIT = 56 * 1024 * 1024
MESH = pl.DeviceIdType.MESH
ANY = pl.BlockSpec(memory_space=pl.ANY)
LANES = 128

SCAN_T = 512
SUBCHUNKS = 8
SCAN_CG = 2048
ATT_T = 512
ATT_SUB = 1
ATT_HEADS = 4
ATT_BWD_HEADS = 2
ROW_T = 256
MM_T = 512


def _params(sem=None):
    return pltpu.CompilerParams(dimension_semantics=sem, vmem_limit_bytes=VMEM_LIMIT)


def _rows(arr, tm):
    if arr.ndim == 2:
        return pl.BlockSpec((tm, arr.shape[1]), lambda i: (i, 0))
    return pl.BlockSpec((arr.shape[0], tm, arr.shape[2]), lambda i: (0, i, 0))


def _const(arr):
    nd = arr.ndim
    return pl.BlockSpec(arr.shape, lambda *_: (0,) * nd, pipeline_mode=pl.Buffered(1))


def _sds(shape, dtype):
    return jax.ShapeDtypeStruct(shape, dtype)


PEERS = tuple((dx, dy, dc) for dx in (0, 1) for dy in (0, 1) for dc in (0, 1) if (dx, dy, dc) != (0, 0, 0))


def _here():
    x, y, c = lax.axis_index("x"), lax.axis_index("y"), lax.axis_index("c")
    return x, y, c, 4 * x + 2 * y + c


def _xchg_start(scatter, srcs, dsts, send, recv, local):
    x, y, c, me = _here()
    for e, sc in enumerate(scatter):
        src, dst = srcs[e], dsts[e]
        pltpu.make_async_copy(src.at[me] if sc else src, dst.at[me], local.at[e]).start()
        for dx, dy, dc in PEERS:
            px, py, pc = (1 - x if dx else x), (1 - y if dy else y), (1 - c if dc else c)
            pltpu.make_async_remote_copy(
                src_ref=src.at[4 * px + 2 * py + pc] if sc else src, dst_ref=dst.at[me],
                send_sem=send.at[e], recv_sem=recv.at[e], device_id=(px, py, pc), device_id_type=MESH).start()


def _xchg_wait(scatter, srcs, dsts, send, recv, local):
    x, y, c, me = _here()
    for e, sc in enumerate(scatter):
        src, dst = srcs[e], dsts[e]
        pltpu.make_async_copy(src.at[me] if sc else src, dst.at[me], local.at[e]).wait()
        span = dst.at[pl.ds(0, N_DEV - 1)]
        both = pltpu.make_async_remote_copy(src_ref=span, dst_ref=span, send_sem=send.at[e], recv_sem=recv.at[e],
                                            device_id=(x, y, c), device_id_type=MESH)
        both.wait_send()
        both.wait_recv()


def _call(body, name, grid, ins, in_specs, outs, out_specs, scratch=(), xch=()):
    n_in, n_out, ne = len(ins), len(outs), len(xch)
    scatter = [sc for _, sc in xch]
    x_outs = [_sds((N_DEV,) + (a.shape[1:] if sc else a.shape), a.dtype) for a, sc in xch]
    sems = [pltpu.SemaphoreType.DMA((ne,))] * 3 if ne else []

    def wrapped(*refs):
        in_refs, x_src = refs[:n_in], refs[n_in:n_in + ne]
        out_refs = refs[n_in + ne:n_in + ne + n_out]
        x_dst = refs[n_in + ne + n_out:n_in + 2 * ne + n_out]
        rest = refs[n_in + 2 * ne + n_out:]
        if ne:
            x_sems, rest = rest[len(rest) - 3:], rest[:len(rest) - 3]
            first = functools.reduce(jnp.logical_and, [pl.program_id(d) == 0 for d in range(len(grid))])
            last = functools.reduce(jnp.logical_and, [pl.program_id(d) == grid[d] - 1 for d in range(len(grid))])

            @pl.when(first)
            def _():
                _xchg_start(scatter, x_src, x_dst, *x_sems)

        body(*in_refs, *out_refs, *rest)
        if ne:
            @pl.when(last)
            def _():
                _xchg_wait(scatter, x_src, x_dst, *x_sems)

    return pl.pallas_call(
        wrapped,
        name=name,
        grid=grid,
        in_specs=list(in_specs) + [ANY] * ne,
        out_specs=list(out_specs) + [ANY] * ne,
        out_shape=list(outs) + x_outs,
        scratch_shapes=list(scratch) + sems,
        compiler_params=_params(("arbitrary",) * len(grid)),
    )(*ins, *[a for a, _ in xch])


def _row_call(body, name, n_rows, tm, row_ins, const_ins, row_outs, acc_outs=(), xch=(), col_outs=(), scratch=()):
    outs = [_sds(s, d) for s, d in list(row_outs) + list(col_outs) + list(acc_outs)]
    n_row, n_col = len(row_outs), len(col_outs)
    out_specs = [_rows(o, tm) for o in outs[:n_row]] + [
        pl.BlockSpec(o.shape[:-1] + (tm,), lambda i, nd=len(o.shape): (0,) * (nd - 1) + (i,))
        for o in outs[n_row:n_row + n_col]] + [
        pl.BlockSpec(o.shape, lambda i, nd=len(o.shape): (0,) * nd) for o in outs[n_row + n_col:]]
    in_specs = [_rows(a, tm) for a in row_ins] + [_const(a) for a in const_ins]
    return _call(body, name, (n_rows // tm,), list(row_ins) + list(const_ins), in_specs, outs, out_specs,
                 scratch=scratch, xch=xch)


def _dot(a, b):
    return jnp.dot(a, b, preferred_element_type=F32)


def _dot_nt(a, b):
    return lax.dot_general(a, b, (((1,), (1,)), ((), ())), preferred_element_type=F32)


def _dot_tn(a, b):
    return lax.dot_general(a, b, (((0,), (0,)), ((), ())), preferred_element_type=F32)


def _rms(x, g, n):
    inv = lax.rsqrt(jnp.sum(x * x, -1, keepdims=True) * (1.0 / n) + EPS)
    return x * inv * g, inv


def _rms_bwd(dy, x, g, inv, n):
    xh = x * inv
    dxh = dy * g
    dx = inv * (dxh - xh * (jnp.sum(dxh * xh, -1, keepdims=True) * (1.0 / n)))
    return dx, dy * xh


def _sigmoid(x):
    return 1.0 / (1.0 + jnp.exp(-x))


_GELU_C = math.sqrt(2.0 / math.pi)


def _gelu(y):
    th = jnp.tanh(_GELU_C * (y + 0.044715 * (y * y * y)))
    return 0.5 * y * (1.0 + th), th


def _gelu_grad(y, th):
    return 0.5 * (1.0 + th) + 0.5 * y * (1.0 - th * th) * (_GELU_C * (1.0 + 3.0 * 0.044715 * (y * y)))


def _acc(ref, val):
    @pl.when(pl.program_id(0) == 0)
    def _():
        ref[...] = jnp.zeros_like(ref)

    ref[...] += val


def _tile(n, limit):
    if n <= limit:
        return n
    return max(t for t in range(128, limit + 1, 128) if n % t == 0)


def _lhs(a, turned, tm, tk):
    m, k_dim = a.shape if turned else a.shape[::-1]
    tm, tk = _tile(m, tm), _tile(k_dim, tk)
    if turned:
        return m, k_dim, tm, tk, pl.BlockSpec((tm, tk), lambda i, k: (i, k)), _dot
    return m, k_dim, tm, tk, pl.BlockSpec((tk, tm), lambda i, k: (k, i)), _dot_tn


def _matmul_tn_shards(a, b, name, by_col, tm=512, tk=512, turned=False):
    m, k_dim, tm, tk, a_spec, dot = _lhs(a, turned, tm, tk)
    n = b.shape[1]
    nk = k_dim // tk
    if by_col:
        r, c = m, n // N_DEV
        out_spec = pl.BlockSpec((N_DEV, tm, c), lambda i, k: (0, i, 0))
    else:
        r, c = m // N_DEV, n
        per = tm // r
        out_spec = pl.BlockSpec((per, r, c), lambda i, k: (i, 0, 0))

    def body(a_ref, b_ref, o_ref, acc_ref):
        k = pl.program_id(1)

        @pl.when(k == 0)
        def _():
            acc_ref[...] = jnp.zeros_like(acc_ref)

        acc_ref[...] += dot(a_ref[...].astype(BF16), b_ref[...].astype(BF16))

        @pl.when(k == nk - 1)
        def _():
            if by_col:
                for j in range(N_DEV):
                    o_ref[j] = acc_ref[:, j * c:(j + 1) * c].astype(BF16)
            else:
                for s in range(per):
                    o_ref[s] = acc_ref[s * r:(s + 1) * r, :].astype(BF16)

    return pl.pallas_call(
        body,
        name=name,
        grid=(m // tm, nk),
        in_specs=[a_spec, pl.BlockSpec((tk, n), lambda i, k: (k, 0))],
        out_specs=out_spec,
        out_shape=_sds((N_DEV, r, c), BF16),
        scratch_shapes=[pltpu.VMEM((tm, n), F32)],
        compiler_params=_params(("parallel", "arbitrary")),
    )(a, b)


def _rope_tables(pos_col):
    n = pos_col.shape[0]
    half = QK_ROPE // 2
    inv_freq = (ROPE_THETA ** (-np.arange(half, dtype=np.float32) / half)).astype(np.float32)
    freq_row = jnp.asarray(np.concatenate([inv_freq, inv_freq, np.zeros(64, np.float32)])[None, :])

    def body(p_ref, f_ref, c_ref, s_ref):
        ang = p_ref[...].astype(F32) * f_ref[...]
        c_ref[...] = jnp.cos(ang)
        s_ref[...] = jnp.sin(ang)

    return _row_call(body, "rope_tables", n, min(n, 1024), [pos_col], [freq_row], [((n, 128), F32)] * 2)


def _rope_rot(v):
    lane = lax.broadcasted_iota(jnp.int32, v.shape, 1)
    return jnp.where(lane < 32, -pltpu.roll(v, 96, 1), jnp.where(lane < 64, pltpu.roll(v, 32, 1), 0.0))


def _rope_rot_t(v):
    lane = lax.broadcasted_iota(jnp.int32, v.shape, 1)
    return jnp.where(lane < 32, pltpu.roll(v, 96, 1), jnp.where(lane < 64, -pltpu.roll(v, 32, 1), 0.0))


def _in_proj(x, norm_mix, w_in_pad, xch):
    n = x.shape[0]

    def body(x_ref, g_ref, w_ref, u_ref, ql_ref, kvl_ref, gs_ref, gm_ref, xnt_ref):
        xn, _ = _rms(x_ref[...], g_ref[...], D_MODEL)
        xb = xn.astype(BF16)
        xnt_ref[...] = xn.T.astype(BF16)
        for ref, (a, b) in zip((u_ref, ql_ref, kvl_ref, gs_ref, gm_ref), IN_SEGS):
            ref[...] = _dot(xb, w_ref[:, a:b])

    outs = [((n, b - a), F32) for a, b in IN_SEGS]
    return _row_call(body, "in_proj", n, MM_T, [x], [norm_mix, w_in_pad], outs, xch=xch,
                     col_outs=[((D_MODEL, n), BF16)])


def _ssm_prep_fn(a_re, a_im, log_dt, b_re_x, b_im_x):
    dt = jnp.exp(log_dt)
    mag = jnp.exp(a_re * dt)
    lr = mag * jnp.cos(a_im * dt)
    li = mag * jnp.sin(a_im * dt)
    den = a_re * a_re + a_im * a_im
    fr = ((lr - 1.0) * a_re + li * a_im) / den
    fi = (li * a_re - (lr - 1.0) * a_im) / den
    return lr, li, fr * b_re_x - fi * b_im_x, fr * b_im_x + fi * b_re_x


def _dot_exact(a, b, dims):
    return lax.dot_general(a, b, (dims, ((), ())), precision=lax.Precision.HIGHEST, preferred_element_type=F32)


def _lane_repeat(width, n):
    src = lax.broadcasted_iota(jnp.int32, (width, n), 0)
    dst = lax.broadcasted_iota(jnp.int32, (width, n), 1)
    return (dst % width == src).astype(F32)


def _same_group(rows, rows_per_group, cols, cols_per_group):
    row = lax.broadcasted_iota(jnp.int32, (rows, cols), 0)
    col = lax.broadcasted_iota(jnp.int32, (rows, cols), 1)
    return (row // rows_per_group) == (col // cols_per_group)


def _expand_b(bt):
    tiled = _dot_exact(bt, _lane_repeat(SSM_STATE, N_STATE), ((1,), (0,)))
    return jnp.where(_same_group(SSM_WIDTH, SSM_GROUP_CH, N_STATE, SSM_STATE), tiled, 0.0)


def _collect_b(m):
    masked = jnp.where(_same_group(SSM_WIDTH, SSM_GROUP_CH, N_STATE, SSM_STATE), m, 0.0)
    return _dot_exact(masked, _lane_repeat(SSM_STATE, N_STATE), ((1,), (1,)))


def _ssm_prep(a_re, a_im, log_dt, bt_re, bt_im, c2_re, c2_im):
    def body(ar, ai, ld, br, bi, cr, ci, lam_ref, bblk_ref, cblk_ref):
        lr, li, bbr, bbi = _ssm_prep_fn(ar[...], ai[...], ld[...], _expand_b(br[...]), _expand_b(bi[...]))
        lam_ref[0:1, :] = lr
        lam_ref[1:2, :] = li
        bblk_ref[:, 0:N_STATE] = bbr.astype(BF16)
        bblk_ref[:, N_STATE:] = bbi.astype(BF16)
        rep = _lane_repeat(SSM_GROUP_CH, SSM_WIDTH)
        own = _same_group(N_STATE, SSM_STATE, SSM_WIDTH, SSM_GROUP_CH)
        cblk_ref[0:N_STATE, :] = jnp.where(own, _dot_exact(cr[...], rep, ((1,), (0,))), 0.0).astype(BF16)
        cblk_ref[N_STATE:, :] = jnp.where(own, -_dot_exact(ci[...], rep, ((1,), (0,))), 0.0).astype(BF16)

    return pl.pallas_call(
        body,
        name="ssm_prep",
        out_shape=[_sds((2, N_STATE), F32), _sds((SSM_WIDTH, 2 * N_STATE), BF16),
                   _sds((2 * N_STATE, SSM_WIDTH), BF16)],
        compiler_params=_params(),
    )(a_re, a_im, log_dt, bt_re, bt_im, c2_re, c2_im)


def _ssm_prep_bwd(a_re, a_im, log_dt, bt_re, bt_im, dlam, dbblk, dcblk_t):
    def body(ar, ai, ld, br, bi, dl, db, dc, dar, dai, dld, dbr, dbi, dcr, dci):
        _, vjp = jax.vjp(_ssm_prep_fn, ar[...], ai[...], ld[...], _expand_b(br[...]), _expand_b(bi[...]))
        g = vjp((dl[0:1, :], dl[1:2, :], db[:, 0:N_STATE], db[:, N_STATE:]))
        dar[...] = g[0]
        dai[...] = g[1]
        grp = lax.broadcasted_iota(jnp.int32, (SSM_GROUPS, N_STATE), 0)
        lane = lax.broadcasted_iota(jnp.int32, (SSM_GROUPS, N_STATE), 1)
        sel = (lane // SSM_STATE) == grp
        dld[...] = jnp.sum(jnp.where(sel, jnp.broadcast_to(g[2], (SSM_GROUPS, N_STATE)), 0.0), axis=1, keepdims=True)
        dbr[...] = _collect_b(g[3])
        dbi[...] = _collect_b(g[4])
        dcr[...] = _collect_b(dc[:, 0:N_STATE])
        dci[...] = -_collect_b(dc[:, N_STATE:])

    small = _sds((SSM_WIDTH, SSM_STATE), F32)
    return pl.pallas_call(
        body,
        name="ssm_prep_bwd",
        out_shape=[_sds((1, N_STATE), F32), _sds((1, N_STATE), F32), _sds((SSM_GROUPS, 1), F32), small, small, small, small],
        compiler_params=_params(),
    )(a_re, a_im, log_dt, bt_re, bt_im, dlam, dbblk, dcblk_t)


def _perm_matrix(t):
    run = t // SUBCHUNKS
    p = np.zeros((t, t), np.float32)
    r = np.arange(t)
    p[r, (r % SUBCHUNKS) * run + r // SUBCHUNKS] = 1.0
    return jnp.asarray(p, dtype=BF16)


def _unpermute(p, a):
    hi = a.astype(BF16)
    r1 = a - hi.astype(F32)
    mid = r1.astype(BF16)
    lo = (r1 - mid.astype(F32)).astype(BF16)
    return _dot_tn(p, hi) + _dot_tn(p, mid) + _dot_tn(p, lo)


def _power_table(lam_ref, pw_ref, n):
    lr, li = lam_ref[0:1, :], lam_ref[1:2, :]
    pw_ref[0:1, 0:N_STATE] = lr
    pw_ref[0:1, N_STATE:] = li

    def step(i, carry):
        pr, pi = carry
        pr, pi = pr * lr - pi * li, pr * li + pi * lr
        pw_ref[pl.ds(i, 1), 0:N_STATE] = pr
        pw_ref[pl.ds(i, 1), N_STATE:] = pi
        return pr, pi

    lax.fori_loop(1, n, step, (lr, li))


def _col_groups():
    return [(pl.ds(c, SCAN_CG), pl.ds(N_STATE + c, SCAN_CG)) for c in range(0, N_STATE, SCAN_CG)]


def _run_scan(buf, lam_ref, t, reverse):
    nblk = t // 8
    for re, im in _col_groups():
        lr = jnp.broadcast_to(lam_ref[0:1, re], (8, SCAN_CG))
        li = jnp.broadcast_to(lam_ref[1:2, re], (8, SCAN_CG))
        if reverse:
            li = -li
        first = pl.ds((nblk - 1) * 8 if reverse else 0, 8)

        def step(k, carry, re=re, im=im, lr=lr, li=li):
            pr, pi = carry
            i = (nblk - 2 - k) if reverse else (k + 1)
            r = pl.ds(pl.multiple_of(i * 8, 8), 8)
            xr = buf[r, re] + lr * pr - li * pi
            xi = buf[r, im] + lr * pi + li * pr
            buf[r, re] = xr
            buf[r, im] = xi
            return xr, xi

        lax.fori_loop(0, nblk - 1, step, (buf[first, re], buf[first, im]))


def _run_carries(buf, pw_ref, carry_ref, s_ref, t, reverse):
    nblk = t // 8
    run = t // SUBCHUNKS
    edge = buf[pl.ds(0 if reverse else (nblk - 1) * 8, 8), :]
    pr, pi = pw_ref[run - 1:run, 0:N_STATE], pw_ref[run - 1:run, N_STATE:]
    if reverse:
        pi = -pi
    sr, si = carry_ref[0:1, 0:N_STATE], carry_ref[0:1, N_STATE:]
    for s in (range(SUBCHUNKS - 1, -1, -1) if reverse else range(SUBCHUNKS)):
        s_ref[s:s + 1, 0:N_STATE] = sr
        s_ref[s:s + 1, N_STATE:] = si
        er, ei = edge[s:s + 1, 0:N_STATE], edge[s:s + 1, N_STATE:]
        sr, si = er + pr * sr - pi * si, ei + pr * si + pi * sr
    carry_ref[:, 0:N_STATE] = jnp.broadcast_to(sr, (8, N_STATE))
    carry_ref[:, N_STATE:] = jnp.broadcast_to(si, (8, N_STATE))


def _run_fix(buf, pw_ref, s_ref, t, reverse):
    nblk = t // 8
    for re, im in _col_groups():
        sr, si = s_ref[:, re], s_ref[:, im]

        def step(i, carry, re=re, im=im, sr=sr, si=si):
            r = pl.ds(pl.multiple_of(i * 8, 8), 8)
            row = pl.ds((nblk - 1 - i) if reverse else i, 1)
            pr, pi = pw_ref[row, re], pw_ref[row, im]
            if reverse:
                pi = -pi
            buf[r, re] += pr * sr - pi * si
            buf[r, im] += pr * si + pi * sr
            return carry

        lax.fori_loop(0, nblk, step, 0)


STATE_BLOCKS = 2 * N_STATE // LANES
CH_BLOCKS = SSM_WIDTH // LANES


def _state_block(b):
    pair = b % (N_STATE // LANES)
    k = (pair * 2 * SSM_GROUP_CH) // LANES
    return slice(b * LANES, (b + 1) * LANES), slice(k * LANES, (k + 1) * LANES)


def _channel_block(c):
    w = N_STATE // CH_BLOCKS
    return slice(c * LANES, (c + 1) * LANES), slice(c * w, (c + 1) * w), slice(N_STATE + c * w, N_STATE + (c + 1) * w)


def _to_states(vb, w_ref, buf, nt):
    for b in range(STATE_BLOCKS):
        lanes, ch = _state_block(b)
        buf[:, lanes] = _dot_nt(vb[:, ch], w_ref[lanes, ch]) if nt else _dot(vb[:, ch], w_ref[ch, lanes])


def _to_channels(buf, w_ref, nt):
    outs = []
    for c in range(CH_BLOCKS):
        ch, re, im = _channel_block(c)
        xr, xi = buf[:, re].astype(BF16), buf[:, im].astype(BF16)
        if nt:
            outs.append(_dot_nt(xr, w_ref[ch, re]) + _dot_nt(xi, w_ref[ch, im]))
        else:
            outs.append(_dot(xr, w_ref[re, ch]) + _dot(xi, w_ref[im, ch]))
    return jnp.concatenate(outs, axis=-1)


def _ssm_fwd(u, bblk, cblk, lam, d_row, w_glu, b_glu, w_o_ssm, xch):
    n = u.shape[0]
    t = min(SCAN_T, n)
    perm = _perm_matrix(t)

    def body(u_ref, p_ref, bblk_ref, cblk_ref, lam_ref, d_ref, wg_ref, bg_ref, wo_ref, y_ref, ys_ref, st_ref,
             buf, pw_ref, carry_ref, s_ref):
        @pl.when(pl.program_id(0) == 0)
        def _():
            carry_ref[...] = jnp.zeros_like(carry_ref)
            _power_table(lam_ref, pw_ref, t // SUBCHUNKS)

        st_ref[0] = carry_ref[...]
        u_t = u_ref[...]
        p = p_ref[...]
        ub = _dot(p, u_t.astype(BF16)).astype(BF16)
        _to_states(ub, bblk_ref, buf, False)
        _run_scan(buf, lam_ref, t, False)
        _run_carries(buf, pw_ref, carry_ref, s_ref, t, False)
        _run_fix(buf, pw_ref, s_ref, t, False)
        y = d_ref[...] * u_t + _unpermute(p, _to_channels(buf, cblk_ref, False))
        y_ref[...] = y
        z, _ = _gelu(y)
        s = _sigmoid(_dot(z.astype(BF16), wg_ref[...]) + bg_ref[...])
        zgb = (z * s).astype(BF16)
        for j in range(N_DEV):
            ys_ref[:, j * OUT_SHARD:(j + 1) * OUT_SHARD] = _dot(zgb, wo_ref[j])

    consts = [perm, bblk, cblk, lam, d_row, w_glu, b_glu, w_o_ssm]
    return _call(
        body, "ssm_fwd", (n // t,), [u] + consts, [_rows(u, t)] + [_const(a) for a in consts],
        [_sds((n, SSM_WIDTH), F32), _sds((n, D_MODEL), F32), _sds((n // t, 8, 2 * N_STATE), F32)],
        [pl.BlockSpec((t, SSM_WIDTH), lambda i: (i, 0)), pl.BlockSpec((t, D_MODEL), lambda i: (i, 0)),
         pl.BlockSpec((1, 8, 2 * N_STATE), lambda i: (i, 0, 0))],
        scratch=[pltpu.VMEM((t, 2 * N_STATE), F32), pltpu.VMEM((t // SUBCHUNKS, 2 * N_STATE), F32),
                 pltpu.VMEM((8, 2 * N_STATE), F32), pltpu.VMEM((8, 2 * N_STATE), F32)],
        xch=xch)


def _head_norm_rope(slab, gain, cos_t, sin_t):
    xn, inv = _rms(slab, gain, QK_HEAD)
    lo, hi = xn[:, 0:128], xn[:, 128:256]
    return jnp.concatenate([lo, hi * cos_t + _rope_rot(hi) * sin_t], axis=-1), inv


def _head_norm_rope_bwd(g, slab, gain, inv, cos_t, sin_t):
    g_lo, g_hi = g[:, 0:128], g[:, 128:256]
    g_n = jnp.concatenate([g_lo, g_hi * cos_t + _rope_rot_t(g_hi * sin_t)], axis=-1)
    return _rms_bwd(g_n, slab, gain, inv, QK_HEAD)


def _qkv_prep(ql, kvl, q_a_norm, kv_a_norm, wq, wkv, gq, gk, cos_t, sin_t):
    n = ql.shape[0]
    tm = MM_T

    def body(ql_ref, kvl_ref, cos_ref, sin_ref, qa_ref, ka_ref, wq_ref, wkv_ref, gq_ref, gk_ref,
             q_ref, k_ref, v_ref, kt_ref, vt_ref):
        cos_t, sin_t = cos_ref[...], sin_ref[...]
        qa, _ = _rms(ql_ref[...], qa_ref[...], Q_LORA)
        qab = qa.astype(BF16)
        kvl_t = kvl_ref[...]
        ca, _ = _rms(kvl_t[:, 0:KV_LORA], ka_ref[...], KV_LORA)
        cab = ca.astype(BF16)
        kpe = kvl_t[:, KV_LORA:KV_LAT_PAD]
        q_pre = _dot(qab, wq_ref[...])
        kv_pre = _dot(cab, wkv_ref[...])
        for h in range(N_HEADS):
            qh, _ = _head_norm_rope(q_pre[:, h * QK_PAD:(h + 1) * QK_PAD], gq_ref[...], cos_t, sin_t)
            q_ref[h] = (qh * ATT_SCALE).astype(BF16)
            kv_h = kv_pre[:, h * QK_PAD:(h + 1) * QK_PAD]
            kh, _ = _head_norm_rope(jnp.concatenate([kv_h[:, 0:QK_NOPE], kpe], axis=-1), gk_ref[...], cos_t, sin_t)
            k_ref[h] = kh.astype(BF16)
            kt_ref[h] = kh.T.astype(BF16)
            vh = kv_h[:, QK_NOPE:]
            v_ref[h] = vh.astype(BF16)
            vt_ref[h] = vh.T.astype(BF16)

    row_ins, consts = [ql, kvl, cos_t, sin_t], [q_a_norm, kv_a_norm, wq, wkv, gq, gk]
    outs = [_sds((N_HEADS, n, QK_PAD), BF16), _sds((N_HEADS, n, QK_PAD), BF16), _sds((N_HEADS, n, V_HEAD), BF16),
            _sds((N_HEADS, QK_PAD, n), BF16), _sds((N_HEADS, V_HEAD, n), BF16)]
    out_specs = [_rows(o, tm) for o in outs[:3]] + [
        pl.BlockSpec((N_HEADS, QK_PAD, tm), lambda i: (0, 0, i)), pl.BlockSpec((N_HEADS, V_HEAD, tm), lambda i: (0, 0, i))]
    return _call(body, "qkv_prep", (n // tm,), row_ins + consts,
                 [_rows(a, tm) for a in row_ins] + [_const(a) for a in consts], outs, out_specs)


def _causal_mask_t(st, t):
    key = lax.broadcasted_iota(jnp.int32, (t, t), 0)
    qry = lax.broadcasted_iota(jnp.int32, (t, t), 1)
    return jnp.where(key <= qry, st, -jnp.inf)


def _attn_fwd(q, k, vt, xch):
    n = q.shape[1]
    t = min(ATT_T, n)

    hp = ATT_HEADS

    def body(q_ref, k_ref, vt_ref, o_ref, lse_ref, ot_ref):
        i = pl.program_id(1)
        qts = [q_ref[g] for g in range(hp)]

        def kv_tile(j, carry, diag):
            ts = t // ATT_SUB
            sts = []
            for g in range(hp):
                for a in range(ATT_SUB):
                    r0 = pl.multiple_of(j * t + a * ts, ts)
                    st = _dot_nt(k_ref[g, pl.ds(r0, ts), :], qts[g])
                    if diag:
                        key = lax.broadcasted_iota(jnp.int32, (ts, t), 0) + a * ts
                        qry = lax.broadcasted_iota(jnp.int32, (ts, t), 1)
                        st = jnp.where(key <= qry, st, -jnp.inf)
                    sts.append(st)
            out = []
            for g in range(hp):
                m, l, acc = carry[g]
                for a in range(ATT_SUB):
                    st = sts[g * ATT_SUB + a]
                    r0 = pl.multiple_of(j * t + a * ts, ts)
                    m_new = jnp.maximum(m, jnp.max(st, 0, keepdims=True))
                    alpha = jnp.exp(m - m_new)
                    pt = jnp.exp(st - m_new)
                    l = alpha * l + jnp.sum(pt, 0, keepdims=True)
                    acc = alpha * acc + _dot(vt_ref[g, :, pl.ds(r0, ts)], pt.astype(BF16))
                    m = m_new
                out.append((m, l, acc))
            return tuple(out)

        one = (jnp.full((1, t), -jnp.inf, F32), jnp.zeros((1, t), F32), jnp.zeros((V_HEAD, t), F32))
        carry = lax.fori_loop(0, i, functools.partial(kv_tile, diag=False), (one,) * hp)
        for g, (m, l, acc) in enumerate(kv_tile(i, carry, True)):
            out_t = acc / l
            o_ref[:, g * V_HEAD:(g + 1) * V_HEAD] = out_t.T
            ot_ref[g * V_HEAD:(g + 1) * V_HEAD, :] = out_t.astype(BF16)
            lse_ref[g] = m + jnp.log(l)

    return _call(
        body, "attn_fwd", (N_HEADS // hp, n // t), [q, k, vt],
        [pl.BlockSpec((hp, t, QK_PAD), lambda h, i: (h, i, 0)), pl.BlockSpec((hp, n, QK_PAD), lambda h, i: (h, 0, 0)),
         pl.BlockSpec((hp, V_HEAD, n), lambda h, i: (h, 0, 0))],
        [_sds((n, N_HEADS * V_HEAD), F32), _sds((N_HEADS, 1, n), F32), _sds((N_HEADS * V_HEAD, n), BF16)],
        [pl.BlockSpec((t, hp * V_HEAD), lambda h, i: (i, h)), pl.BlockSpec((hp, 1, t), lambda h, i: (h, 0, i)),
         pl.BlockSpec((hp * V_HEAD, t), lambda h, i: (h, i))],
        xch=xch)


def _merge(attn, gs, gm, y_ssm, x, w_o_mla, w_out):
    n = x.shape[0]

    def body(at_ref, gs_ref, gm_ref, ys_ref, x_ref, wo_ref, wout_ref, h_ref, ym_ref, mxt_ref):
        y_mla = _dot(at_ref[...].astype(BF16), wo_ref[...])
        ym_ref[...] = y_mla
        mixed = _sigmoid(gs_ref[...]) * ys_ref[...] + _sigmoid(gm_ref[...]) * y_mla
        mxt_ref[...] = mixed.T.astype(BF16)
        h_ref[...] = x_ref[...] + _dot(mixed.astype(BF16), wout_ref[...])

    outs = [((n, D_MODEL), F32), ((n, D_MODEL), F32)]
    return _row_call(body, "merge", n, MM_T, [attn, gs, gm, y_ssm, x], [w_o_mla, w_out], outs,
                     col_outs=[((D_MODEL, n), BF16)])


def _mlp_fwd_loss(h, target, norm_mlp, w_up, w_down):
    n = h.shape[0]

    def body(h_ref, t_ref, g_ref, wu_ref, wd_ref, hn_ref, do_ref, hnt_ref, loss_ref):
        h_t = h_ref[...]
        hn, _ = _rms(h_t, g_ref[...], D_MODEL)
        hb = hn.astype(BF16)
        hn_ref[...] = hb
        hnt_ref[...] = hn.T.astype(BF16)
        out = h_t
        for j in range(N_DEV):
            a = jnp.maximum(_dot(hb, wu_ref[j]), 0.0)
            out += _dot((a * a).astype(BF16), wd_ref[j])
        err = out - t_ref[...]
        do_ref[...] = err * (1.0 / D_MODEL)
        _acc(loss_ref, jnp.broadcast_to(jnp.sum(err * err) * (0.5 / D_MODEL), loss_ref.shape))

    outs = [((n, D_MODEL), BF16), ((n, D_MODEL), F32)]
    return _row_call(body, "mlp_fwd_loss", n, MM_T, [h, target], [norm_mlp, w_up, w_down], outs, [((8, 128), F32)],
                     col_outs=[((D_MODEL, n), BF16)])


def _mlp_bwd(dout, hn, h, norm_mlp, w_up, w_down):
    n = h.shape[0]

    def body(do_ref, hn_ref, h_ref, g_ref, wu_ref, wd_ref, da_ref, dh_ref, dob_ref, hidt_ref, dg_ref):
        dout_t = do_ref[...]
        doutb = dout_t.astype(BF16)
        dob_ref[...] = doutb
        hb = hn_ref[...]
        dhn = jnp.zeros_like(dout_t)
        for j in range(N_DEV):
            cols = slice(j * FF_SHARD, (j + 1) * FF_SHARD)
            a = jnp.maximum(_dot(hb, wu_ref[j]), 0.0)
            hidt_ref[cols, :] = (a * a).T.astype(BF16)
            da = (_dot_nt(doutb, wd_ref[j]) * (2.0 * a)).astype(BF16)
            da_ref[:, cols] = da
            dhn += _dot_nt(da, wu_ref[j])
        h_t = h_ref[...]
        inv = lax.rsqrt(jnp.sum(h_t * h_t, -1, keepdims=True) * (1.0 / D_MODEL) + EPS)
        dx, dg = _rms_bwd(dhn, h_t, g_ref[...], inv, D_MODEL)
        dh_ref[...] = dout_t + dx
        _acc(dg_ref, jnp.sum(dg, 0, keepdims=True))

    outs = [((n, D_FF), BF16), ((n, D_MODEL), F32), ((n, D_MODEL), BF16)]
    return _row_call(body, "mlp_bwd", n, MM_T, [dout, hn, h], [norm_mlp, w_up, w_down], outs, [((1, D_MODEL), F32)],
                     col_outs=[((D_FF, n), BF16)])


def _merge_bwd(dh, gs, gm, y_ssm, y_mla, attn, w_out, w_o_mla):
    n = dh.shape[0]

    def body(dh_ref, gs_ref, gm_ref, ys_ref, ym_ref, at_ref, wout_ref, wo_ref,
             dgs_ref, dgm_ref, dys_ref, dym_ref, dat_ref, delta_ref):
        dmix = _dot_nt(dh_ref[...].astype(BF16), wout_ref[...])
        sgs, sgm = _sigmoid(gs_ref[...]), _sigmoid(gm_ref[...])
        dgs_ref[...] = (dmix * ys_ref[...] * sgs * (1.0 - sgs)).astype(BF16)
        dgm_ref[...] = (dmix * ym_ref[...] * sgm * (1.0 - sgm)).astype(BF16)
        dys_ref[...] = (dmix * sgs).astype(BF16)
        dym = (dmix * sgm).astype(BF16)
        dym_ref[...] = dym
        dattn = _dot_nt(dym, wo_ref[...])
        dat_ref[...] = dattn.astype(BF16)
        prod = dattn * at_ref[...]
        ones = jnp.ones((8, V_HEAD), F32)
        for h in range(N_HEADS):
            delta_ref[h] = _dot_exact(ones, prod[:, h * V_HEAD:(h + 1) * V_HEAD], ((1,), (1,)))[0:1, :]

    outs = [((n, D_MODEL), BF16)] * 5
    return _row_call(body, "merge_bwd", n, MM_T, [dh, gs, gm, y_ssm, y_mla, attn], [w_out, w_o_mla], outs,
                     col_outs=[((N_HEADS, 1, n), F32)])


def _attn_bwd(q, k, kt, v, lse, delta, dout, xch):
    n = q.shape[1]
    t = min(ATT_T, n)
    nt = n // t
    hp = ATT_BWD_HEADS

    def body(q_ref, k_ref, kt_ref, v_ref, lse_ref, delta_ref, do_ref, dq_ref, dk_ref, dv_ref, dqt_ref):
        j = pl.program_id(1)

        @pl.when(j == 0)
        def _():
            dqt_ref[...] = jnp.zeros_like(dqt_ref)

        def q_tile(i, carry, diag):
            r0 = pl.multiple_of(i * t, t)
            rows = pl.ds(r0, t)
            qts = [q_ref[g, rows, :] for g in range(hp)]
            sts = [_dot_nt(k_ref[g], qts[g]) for g in range(hp)]
            out = []
            for g in range(hp):
                dk, dv = carry[g]
                st = _causal_mask_t(sts[g], t) if diag else sts[g]
                pt = jnp.exp(st - lse_ref[g, :, rows])
                dob = do_ref[rows, g * V_HEAD:(g + 1) * V_HEAD]
                dv = dv + _dot(pt.astype(BF16), dob)
                dst = (pt * (_dot_nt(v_ref[g], dob) - delta_ref[g, :, rows])).astype(BF16)
                dk = dk + _dot(dst, qts[g])
                dqt_ref[g, :, rows] += _dot(kt_ref[g], dst)
                out.append((dk, dv))
            return tuple(out)

        zero = (jnp.zeros((t, QK_PAD), F32), jnp.zeros((t, V_HEAD), F32))
        carry = q_tile(j, (zero,) * hp, True)
        carry = lax.fori_loop(j + 1, nt, functools.partial(q_tile, diag=False), carry)
        for g, (dk, dv) in enumerate(carry):
            dk_ref[g] = dk
            dv_ref[g] = dv

        @pl.when(j == nt - 1)
        def _():
            for g in range(hp):
                for c in range(0, n, t):
                    dq_ref[g, c:c + t, :] = dqt_ref[g, :, c:c + t].T

    return _call(
        body, "attn_bwd", (N_HEADS // hp, nt), [q, k, kt, v, lse, delta, dout],
        [pl.BlockSpec((hp, n, QK_PAD), lambda h, j: (h, 0, 0)), pl.BlockSpec((hp, t, QK_PAD), lambda h, j: (h, j, 0)),
         pl.BlockSpec((hp, QK_PAD, t), lambda h, j: (h, 0, j)), pl.BlockSpec((hp, t, V_HEAD), lambda h, j: (h, j, 0)),
         pl.BlockSpec((hp, 1, n), lambda h, j: (h, 0, 0)), pl.BlockSpec((hp, 1, n), lambda h, j: (h, 0, 0)),
         pl.BlockSpec((n, hp * V_HEAD), lambda h, j: (0, h))],
        [_sds((N_HEADS, n, QK_PAD), F32), _sds((N_HEADS, n, QK_PAD), F32), _sds((N_HEADS, n, V_HEAD), F32)],
        [pl.BlockSpec((hp, n, QK_PAD), lambda h, j: (h, 0, 0)), pl.BlockSpec((hp, t, QK_PAD), lambda h, j: (h, j, 0)),
         pl.BlockSpec((hp, t, V_HEAD), lambda h, j: (h, j, 0))],
        scratch=[pltpu.VMEM((hp, QK_PAD, n), F32)],
        xch=xch)


def _qkv_prep_bwd(ql, kvl, dq, dk, dv, q_a_norm, kv_a_norm, wq, wkv, gq, gk, cos_t, sin_t, xch):
    n = ql.shape[0]

    def body(ql_ref, kvl_ref, cos_ref, sin_ref, dq_ref, dk_ref, dv_ref, qa_ref, ka_ref, wq_ref, wkv_ref, gq_ref, gk_ref,
             dql_ref, dkvl_ref, dqa_ref, dka_ref, dgq_ref, dgk_ref, dwq_ref, dwkv_ref, dqp_ref, dkvp_ref):
        cos_t, sin_t = cos_ref[...], sin_ref[...]
        ql_t = ql_ref[...]
        qa, inv_qa = _rms(ql_t, qa_ref[...], Q_LORA)
        qab = qa.astype(BF16)
        kvl_t = kvl_ref[...]
        ckv = kvl_t[:, 0:KV_LORA]
        ca, inv_ca = _rms(ckv, ka_ref[...], KV_LORA)
        cab = ca.astype(BF16)
        kpe = kvl_t[:, KV_LORA:KV_LAT_PAD]
        dgq = jnp.zeros((1, QK_PAD), F32)
        dgk = jnp.zeros((1, QK_PAD), F32)
        dkpe = jnp.zeros_like(kpe)
        q_pre = _dot(qab, wq_ref[...])
        kv_pre = _dot(cab, wkv_ref[...])
        for h in range(N_HEADS):
            head = slice(h * QK_PAD, (h + 1) * QK_PAD)
            q_slab = q_pre[:, head]
            inv = lax.rsqrt(jnp.sum(q_slab * q_slab, -1, keepdims=True) * (1.0 / QK_HEAD) + EPS)
            d_slab, dg = _head_norm_rope_bwd(dq_ref[h] * ATT_SCALE, q_slab, gq_ref[...], inv, cos_t, sin_t)
            dqp_ref[:, head] = d_slab.astype(BF16)
            dgq += jnp.sum(dg, 0, keepdims=True)
            k_slab = jnp.concatenate([kv_pre[:, h * QK_PAD:h * QK_PAD + QK_NOPE], kpe], axis=-1)
            inv = lax.rsqrt(jnp.sum(k_slab * k_slab, -1, keepdims=True) * (1.0 / QK_HEAD) + EPS)
            d_slab, dg = _head_norm_rope_bwd(dk_ref[h], k_slab, gk_ref[...], inv, cos_t, sin_t)
            dkvp_ref[:, head] = jnp.concatenate([d_slab[:, 0:QK_NOPE], dv_ref[h]], axis=-1).astype(BF16)
            dkpe += d_slab[:, QK_NOPE:QK_PAD]
            dgk += jnp.sum(dg, 0, keepdims=True)
        dqa = _dot_nt(dqp_ref[...], wq_ref[...])
        dx, dg = _rms_bwd(dqa, ql_t, qa_ref[...], inv_qa, Q_LORA)
        dql_ref[...] = dx.astype(BF16)
        _acc(dqa_ref, jnp.sum(dg, 0, keepdims=True))
        dca = _dot_nt(dkvp_ref[...], wkv_ref[...])
        dx, dg = _rms_bwd(dca, ckv, ka_ref[...], inv_ca, KV_LORA)
        dkvl_ref[:, 0:KV_LORA] = dx.astype(BF16)
        dkvl_ref[:, KV_LORA:KV_LAT_PAD] = dkpe.astype(BF16)
        _acc(dka_ref, jnp.sum(dg, 0, keepdims=True))
        _acc(dgq_ref, dgq)
        _acc(dgk_ref, dgk)
        _acc(dwq_ref, _dot_tn(qab, dqp_ref[...]))
        _acc(dwkv_ref, _dot_tn(cab, dkvp_ref[...]))

    wide = N_HEADS * QK_PAD
    row_outs = [((n, Q_LORA), BF16), ((n, KV_LAT_PAD), BF16)]
    acc_outs = [((1, Q_LORA), F32), ((1, KV_LORA), F32), ((1, QK_PAD), F32), ((1, QK_PAD), F32),
                ((Q_LORA, wide), F32), ((KV_LORA, wide), F32)]
    return _row_call(body, "qkv_prep_bwd", n, ROW_T, [ql, kvl, cos_t, sin_t, dq, dk, dv],
                     [q_a_norm, kv_a_norm, wq, wkv, gq, gk], row_outs, acc_outs, xch=xch,
                     scratch=[pltpu.VMEM((ROW_T, wide), BF16), pltpu.VMEM((ROW_T, wide), BF16)])


def _glu_bwd(dy_ssm, y, w_glu, b_glu, w_o_ssm):
    n = y.shape[0]

    def body(dys_ref, y_ref, wg_ref, bg_ref, wo_ref, dy_ref, db_ref, dwg_ref, dwo_ref):
        y_t = y_ref[...]
        z, th = _gelu(y_t)
        zb = z.astype(BF16)
        s = _sigmoid(_dot(zb, wg_ref[...]) + bg_ref[...])
        dys = dys_ref[...]
        dzg = jnp.zeros_like(y_t)
        for j in range(N_DEV):
            dzg += _dot_nt(dys[:, j * OUT_SHARD:(j + 1) * OUT_SHARD], wo_ref[j])
        dt = dzg * z * s * (1.0 - s)
        dtb = dt.astype(BF16)
        dz = dzg * s + _dot_nt(dtb, wg_ref[...])
        dy_ref[...] = dz * _gelu_grad(y_t, th)
        _acc(db_ref, jnp.sum(dt, 0, keepdims=True))
        _acc(dwg_ref, _dot_tn(zb, dtb))
        _acc(dwo_ref, _dot_tn((z * s).astype(BF16), dys))

    acc_outs = [((1, SSM_WIDTH), F32), ((SSM_WIDTH, SSM_WIDTH), F32), ((SSM_WIDTH, D_MODEL), F32)]
    return _row_call(body, "glu_bwd", n, MM_T, [dy_ssm, y], [w_glu, b_glu, w_o_ssm], [((n, SSM_WIDTH), F32)], acc_outs)


def _ssm_bwd(u, dy, st, bblk, cblk, lam, d_row, xch):
    n = u.shape[0]
    t = min(SCAN_T, n)
    nc = n // t
    kb = 512
    perm = _perm_matrix(t)

    def body(u_ref, dy_ref, st_ref, p_ref, bblk_ref, cblk_ref, lam_ref, d_ref,
             du_ref, dlam_ref, dd_ref, db_ref, dct_ref,
             buf_x, buf_a, pw_ref, carry_ref, xcarry_ref, sx_ref, sa_ref, db_acc, dct_acc):
        @pl.when(pl.program_id(0) == 0)
        def _():
            carry_ref[...] = jnp.zeros_like(carry_ref)
            db_acc[...] = jnp.zeros_like(db_acc)
            dct_acc[...] = jnp.zeros_like(dct_acc)
            _power_table(lam_ref, pw_ref, t // SUBCHUNKS)

        u_t = u_ref[...]
        dy_t = dy_ref[...]
        p = p_ref[...]
        ub = _dot(p, u_t.astype(BF16)).astype(BF16)
        dyb = _dot(p, dy_t.astype(BF16)).astype(BF16)
        _to_states(ub, bblk_ref, buf_x, False)
        xcarry_ref[...] = st_ref[0]
        _run_scan(buf_x, lam_ref, t, False)
        _run_carries(buf_x, pw_ref, xcarry_ref, sx_ref, t, False)
        _run_fix(buf_x, pw_ref, sx_ref, t, False)
        _to_states(dyb, cblk_ref, buf_a, True)
        _run_scan(buf_a, lam_ref, t, True)
        _run_carries(buf_a, pw_ref, carry_ref, sa_ref, t, True)
        _run_fix(buf_a, pw_ref, sa_ref, t, True)
        du_ref[...] = (d_ref[...] * dy_t + _unpermute(p, _to_channels(buf_a, bblk_ref, True))).astype(BF16)
        for b in range(STATE_BLOCKS):
            lanes, ch = _state_block(b)
            db_acc[ch, lanes] += _dot_tn(ub[:, ch], buf_a[:, lanes].astype(BF16))
            dct_acc[ch, lanes] += _dot_tn(dyb[:, ch], buf_x[:, lanes].astype(BF16))
        for c in range(0, N_STATE, kb):
            re, im = pl.ds(c, kb), pl.ds(N_STATE + c, kb)
            xr, xi = buf_x[pl.ds(0, t - 8), re], buf_x[pl.ds(0, t - 8), im]
            ar, ai = buf_a[pl.ds(8, t - 8), re], buf_a[pl.ds(8, t - 8), im]
            x0r, x0i = sx_ref[:, re], sx_ref[:, im]
            a0r, a0i = buf_a[0:8, re], buf_a[0:8, im]
            dlam_part_re = (jnp.sum(ar * xr + ai * xi, 0, keepdims=True)
                            + jnp.sum(a0r * x0r + a0i * x0i, 0, keepdims=True))
            dlam_part_im = (jnp.sum(ai * xr - ar * xi, 0, keepdims=True)
                            + jnp.sum(a0i * x0r - a0r * x0i, 0, keepdims=True))

            @pl.when(pl.program_id(0) == 0)
            def _(c=c):
                dlam_ref[0:1, c:c + kb] = jnp.zeros((1, kb), F32)
                dlam_ref[1:2, c:c + kb] = jnp.zeros((1, kb), F32)

            dlam_ref[0:1, c:c + kb] += dlam_part_re
            dlam_ref[1:2, c:c + kb] += dlam_part_im
        _acc(dd_ref, jnp.sum(dy_t * u_t, 0, keepdims=True))

        @pl.when(pl.program_id(0) == nc - 1)
        def _():
            pltpu.sync_copy(db_acc, db_ref)
            pltpu.sync_copy(dct_acc, dct_ref)

    rev = lambda i: (nc - 1 - i, 0)
    consts = [perm, bblk, cblk, lam, d_row]
    wide = (SSM_WIDTH, 2 * N_STATE)
    return _call(
        body, "ssm_bwd", (nc,), [u, dy, st] + consts,
        [pl.BlockSpec((t, SSM_WIDTH), rev), pl.BlockSpec((t, SSM_WIDTH), rev),
         pl.BlockSpec((1, 8, 2 * N_STATE), lambda i: (nc - 1 - i, 0, 0))] + [_const(a) for a in consts],
        [_sds((n, SSM_WIDTH), BF16), _sds((2, N_STATE), F32), _sds((1, SSM_WIDTH), F32), _sds(wide, F32), _sds(wide, F32)],
        [pl.BlockSpec((t, SSM_WIDTH), rev), pl.BlockSpec((2, N_STATE), lambda i: (0, 0)),
         pl.BlockSpec((1, SSM_WIDTH), lambda i: (0, 0)), ANY, ANY],
        scratch=[pltpu.VMEM((t, 2 * N_STATE), F32)] * 2 + [pltpu.VMEM((t // SUBCHUNKS, 2 * N_STATE), F32)]
        + [pltpu.VMEM((8, 2 * N_STATE), F32)] * 4 + [pltpu.VMEM(wide, F32)] * 2,
        xch=xch)


def _in_proj_bwd(pieces, dh, x, xn_t, norm_mix, w_in_pad, xch):
    n = x.shape[0]
    tm = min(MM_T, n)
    nt = n // tm

    def body(du_ref, dql_ref, dkvl_ref, dgs_ref, dgm_ref, dh_ref, x_ref, xnt_ref, g_ref, w_ref,
             dx_ref, dg_ref, dw_ref, acc_ref):
        @pl.when(pl.program_id(0) == 0)
        def _():
            acc_ref[...] = jnp.zeros_like(acc_ref)

        xnt = xnt_ref[...]
        dxn = jnp.zeros((tm, D_MODEL), F32)
        for ref, (a, b) in zip((du_ref, dql_ref, dkvl_ref, dgs_ref, dgm_ref), IN_SEGS):
            piece = ref[...]
            dxn += _dot_nt(piece, w_ref[:, a:b])
            acc_ref[:, a:b] += _dot(xnt, piece)
        x_t = x_ref[...]
        inv = lax.rsqrt(jnp.sum(x_t * x_t, -1, keepdims=True) * (1.0 / D_MODEL) + EPS)
        dx, dg = _rms_bwd(dxn, x_t, g_ref[...], inv, D_MODEL)
        dx_ref[...] = dh_ref[...] + dx
        _acc(dg_ref, jnp.sum(dg, 0, keepdims=True))

        @pl.when(pl.program_id(0) == nt - 1)
        def _():
            pltpu.sync_copy(acc_ref, dw_ref)

    row_ins, consts = list(pieces) + [dh, x], [norm_mix, w_in_pad]
    in_specs = ([_rows(a, tm) for a in row_ins] + [pl.BlockSpec((D_MODEL, tm), lambda i: (0, i))]
                + [_const(a) for a in consts])
    return _call(
        body, "in_proj_bwd", (nt,), row_ins + [xn_t] + consts, in_specs,
        [_sds((n, D_MODEL), F32), _sds((1, D_MODEL), F32), _sds((D_MODEL, D_IN_PAD), F32)],
        [pl.BlockSpec((tm, D_MODEL), lambda i: (i, 0)), pl.BlockSpec((1, D_MODEL), lambda i: (0, 0)), ANY],
        scratch=[pltpu.VMEM((D_MODEL, D_IN_PAD), F32)], xch=xch)


def _swap_minor(a):
    g, r, c = a.shape[1:]
    return jnp.transpose(a[0], (0, 2, 1)).reshape(g * c, r)


def _pad_in(w):
    return jnp.concatenate([w[:, :KV_END], jnp.zeros((w.shape[0], D_IN_PAD - D_IN), w.dtype), w[:, KV_END:]], axis=1)


def _unpad_in(w):
    return jnp.concatenate([w[:, :KV_END], w[:, KV_END + D_IN_PAD - D_IN:]], axis=1)


def _pad_gain(g):
    return jnp.pad(g, ((0, 0), (0, QK_PAD - QK_HEAD)))


def _place():
    x, y, c = lax.axis_index("x"), lax.axis_index("y"), lax.axis_index("c")
    chips = [(x, y), (1 - x, y), (x, 1 - y), (1 - x, 1 - y)]
    return x, y, c, chips


def _all_gather(block, name):
    rows, lanes = block.shape

    def body(x_ref, out_ref, send_sems, recv_sems, local_sem):
        x, y, c, chips = _place()
        me, sibling = (x, y, c), (x, y, 1 - c)

        def slot(px, py, pc):
            return out_ref.at[4 * px + 2 * py + pc]

        def copy(k, blk, to, src=None):
            return pltpu.make_async_remote_copy(
                src_ref=slot(*blk) if src is None else src, dst_ref=slot(*blk),
                send_sem=send_sems.at[k], recv_sem=recv_sems.at[k], device_id=to, device_id_type=MESH)

        mine = pltpu.make_async_copy(x_ref, slot(*me), local_sem)
        mine.start()
        first = [copy(0, me, sibling, src=x_ref)]
        first += [copy(1 + j, me, (*chip, c), src=x_ref) for j, chip in enumerate(chips[1:])]
        for cp in first:
            cp.start()
        passed = [copy(4 + j, (*chip, c), sibling) for j, chip in enumerate(chips[1:])]
        for j, chip in enumerate(chips[1:]):
            copy(1 + j, (*chip, c), me).wait_recv()
            passed[j].start()
        copy(0, sibling, me).wait_recv()
        for j, chip in enumerate(chips[1:]):
            copy(4 + j, (*chip, 1 - c), me).wait_recv()
        for cp in first + passed:
            cp.wait_send()
        mine.wait()

    return pl.pallas_call(
        body,
        name=name,
        in_specs=[ANY],
        out_specs=ANY,
        out_shape=_sds((N_DEV, rows, lanes), block.dtype),
        scratch_shapes=[pltpu.SemaphoreType.DMA((7,)), pltpu.SemaphoreType.DMA((7,)), pltpu.SemaphoreType.DMA],
    )(block)


def _reduce_scatter(parts, gather, name):
    _, rows, lanes = parts.shape

    def body(p_ref, g_ref, out_ref, ga_ref, own, land_a, send_b, land_b, sa, ra, sb, rb, lo, *g_sems):
        x, y, c, chips = _place()
        sibling = (x, y, 1 - c)
        _xchg_start([False], [g_ref], [ga_ref], *g_sems)

        def blk(chip, core):
            return p_ref.at[4 * chip[0] + 2 * chip[1] + core]

        to_sib = [pltpu.make_async_remote_copy(
            src_ref=blk(chips[k], 1 - c), dst_ref=land_a.at[k], send_sem=sa.at[k], recv_sem=ra.at[k],
            device_id=sibling, device_id_type=MESH) for k in range(4)]
        for cp in to_sib:
            cp.start()
        loads = [pltpu.make_async_copy(blk(chips[k], c), own.at[k], lo.at[k]) for k in range(4)]
        for cp in loads:
            cp.start()
        to_chip = [pltpu.make_async_remote_copy(
            src_ref=send_b.at[j], dst_ref=land_b.at[j], send_sem=sb.at[j], recv_sem=rb.at[j],
            device_id=(*chips[1 + j], c), device_id_type=MESH) for j in range(3)]
        for k in (1, 2, 3):
            to_sib[k].wait_recv()
            loads[k].wait()
            send_b[k - 1] = (own[k] + land_a[k]).astype(BF16)
            to_chip[k - 1].start()
        to_sib[0].wait_recv()
        loads[0].wait()
        acc = own[0] + land_a[0]
        for j in range(3):
            to_chip[j].wait_recv()
            acc = acc + land_b[j].astype(F32)
        out_ref[...] = acc
        for cp in to_sib + to_chip:
            cp.wait_send()
        _xchg_wait([False], [g_ref], [ga_ref], *g_sems)

    return pl.pallas_call(
        body,
        name=name,
        in_specs=[ANY, ANY],
        out_specs=[pl.BlockSpec(memory_space=pltpu.VMEM), ANY],
        out_shape=[_sds((rows, lanes), F32), _sds((N_DEV,) + gather.shape, gather.dtype)],
        scratch_shapes=[pltpu.VMEM((4, rows, lanes), F32), pltpu.VMEM((4, rows, lanes), F32),
                        pltpu.VMEM((3, rows, lanes), BF16), pltpu.VMEM((3, rows, lanes), BF16)]
        + [pltpu.SemaphoreType.DMA((4,))] * 2 + [pltpu.SemaphoreType.DMA((3,))] * 2 + [pltpu.SemaphoreType.DMA((4,))]
        + [pltpu.SemaphoreType.DMA((1,))] * 3,
        compiler_params=_params(),
    )(parts, gather)


def _adamw_math(w, g, m, v):
    m = ADAM_B1 * m + (1.0 - ADAM_B1) * g
    v = ADAM_B2 * v + (1.0 - ADAM_B2) * (g * g)
    m_hat = m / (1.0 - ADAM_B1 ** ADAM_STEP)
    v_hat = v / (1.0 - ADAM_B2 ** ADAM_STEP)
    delta = -ADAM_LR * (m_hat / (jnp.sqrt(v_hat) + ADAM_EPS) + ADAM_WD * w)
    return delta, m, v


def _row_tile(r):
    return max(t for t in range(8, min(r, 256) + 1, 8) if r % t == 0)


def _adamw(w, g, m, v, name):
    r, n = w.shape

    def body(w_ref, g_ref, m_ref, v_ref, d_ref, nm_ref, nv_ref):
        d_ref[...], nm_ref[...], nv_ref[...] = _adamw_math(w_ref[...], g_ref[...], m_ref[...], v_ref[...])

    return _row_call(body, name, r, _row_tile(r), [w, g, m, v], [], [((r, n), F32)] * 3)


def _adamw_sum(landed, w, m, v, name):
    r, n = w.shape

    def body(l_ref, w_ref, m_ref, v_ref, g_ref, d_ref, nm_ref, nv_ref):
        g = l_ref[0].astype(F32)
        for dev in range(1, N_DEV):
            g = g + l_ref[dev].astype(F32)
        g_ref[...] = g
        d_ref[...], nm_ref[...], nv_ref[...] = _adamw_math(w_ref[...], g, m_ref[...], v_ref[...])

    tm = max(t for t in range(16, min(r, 256) + 1, 16) if r % t == 0)
    return _row_call(body, name, r, tm, [landed, w, m, v], [], [((r, n), F32)] * 4)


def _adamw_small(first, rest, w, m, v, row_counts):
    n_rest = w.shape[0] - first.shape[1]

    def body(f_ref, r_ref, w_ref, m_ref, v_ref, loss_ref, *out_refs):
        gf, gr = f_ref[0], r_ref[0]
        for dev in range(1, N_DEV):
            gf, gr = gf + f_ref[dev], gr + r_ref[dev]
        loss_ref[...] = gr[n_rest:n_rest + 8]
        g = jnp.concatenate([gf, gr[0:n_rest]], axis=0)
        d, nm, nv = _adamw_math(w_ref[...], g, m_ref[...], v_ref[...])
        off = 0
        for p, rows in enumerate(row_counts):
            for k, val in enumerate((g, d, nm, nv)):
                out_refs[4 * p + k][...] = val[off:off + rows]
            off += rows

    outs = [_sds((8, LANES), F32)] + [_sds((rows, LANES), F32) for rows in row_counts for _ in range(4)]
    return pl.pallas_call(body, name="adamw_small", out_shape=outs, compiler_params=_params())(first, rest, w, m, v)


SMALL = ("norm_mix", "q_a_norm", "kv_a_norm", "q_norm", "k_norm", "ssm_a_re", "ssm_a_im", "ssm_log_dt", "ssm_b_re",
         "ssm_b_im", "ssm_c_re", "ssm_c_im", "ssm_d", "b_glu", "norm_mlp")
WEIGHT_ORDER = ("norm_mix", "w_in", "q_a_norm", "kv_a_norm", "w_q_b", "w_kv_b", "q_norm", "k_norm", "w_o_mla",
                "ssm_a_re", "ssm_a_im", "ssm_log_dt", "ssm_b_re", "ssm_b_im", "ssm_c_re", "ssm_c_im", "ssm_d", "w_glu",
                "b_glu", "w_o_ssm", "w_out", "norm_mlp", "w_up", "w_down")
IN_SHARD = D_IN // N_DEV


def _pack_small(vals, names=SMALL):
    parts = []
    for n in names:
        flat = vals[n].reshape(-1)
        size = -(-flat.shape[0] // (8 * LANES)) * 8 * LANES
        parts.append(jnp.pad(flat, (0, size - flat.shape[0])).reshape(-1, LANES))
    return jnp.concatenate(parts, axis=0)


def _small_rows(like):
    return [-(-like[n].size // (8 * LANES)) * 8 for n in SMALL]


def _step(x, pos_col, target, w, small):
    bf = {n: a.astype(BF16) for n, a in w.items()}
    gq, gk = _pad_gain(small["q_norm"]), _pad_gain(small["k_norm"])
    a_re = small["ssm_a_re"].reshape(1, N_STATE)
    a_im = small["ssm_a_im"].reshape(1, N_STATE)
    log_dt = jnp.repeat(small["ssm_log_dt"].reshape(SSM_GROUPS), SSM_STATE).reshape(1, N_STATE)
    bt_re, bt_im = _swap_minor(small["ssm_b_re"]), _swap_minor(small["ssm_b_im"])
    c2_re, c2_im = _swap_minor(small["ssm_c_re"]), _swap_minor(small["ssm_c_im"])
    d_row = small["ssm_d"].reshape(1, SSM_WIDTH)

    w_in_all = _all_gather(bf["w_in"], "gather_w_in")
    w_in_pad = _pad_in(jnp.transpose(w_in_all, (1, 0, 2)).reshape(D_MODEL, D_IN))
    cos_t, sin_t = _rope_tables(pos_col)
    lam, bblk, cblk = _ssm_prep(a_re, a_im, log_dt, bt_re, bt_im, c2_re, c2_im)
    wq_mine = jnp.pad(bf["w_q_b"], ((0, 0), (0, QK_PAD - QK_HEAD)))
    u, ql, kvl, gs, gm, xn_t, w_glu, w_o_ssm = _in_proj(
        x, small["norm_mix"], w_in_pad, xch=[(bf["w_glu"], False), (bf["w_o_ssm"], False)])
    w_glu = w_glu.reshape(SSM_WIDTH, SSM_WIDTH)
    y, y_ssm, st, wq, wkv, w_o_mla, w_out = _ssm_fwd(
        u, bblk, cblk, lam, d_row, w_glu, small["b_glu"], w_o_ssm,
        xch=[(wq_mine, False), (bf["w_kv_b"], False), (bf["w_o_mla"], False), (bf["w_out"], False)])
    w_o_mla, w_out = w_o_mla.reshape(D_MODEL, D_MODEL), w_out.reshape(D_MODEL, D_MODEL)
    wq = jnp.transpose(wq, (1, 0, 2)).reshape(Q_LORA, N_HEADS * QK_PAD)
    wkv = jnp.transpose(wkv, (1, 0, 2)).reshape(KV_LORA, N_HEADS * QK_PAD)
    q, k, v, kt, vt = _qkv_prep(ql, kvl, small["q_a_norm"], small["kv_a_norm"], wq, wkv, gq, gk, cos_t, sin_t)
    attn, lse, attn_t, w_up, w_down = _attn_fwd(q, k, vt, xch=[(bf["w_up"], False), (bf["w_down"], False)])
    h, y_mla, mixed_t = _merge(attn, gs, gm, y_ssm, x, w_o_mla, w_out)
    hn, dout, hn_t, loss = _mlp_fwd_loss(h, target, small["norm_mlp"], w_up, w_down)

    da, dh, dout_b, hid_t, d_norm_mlp = _mlp_bwd(dout, hn, h, small["norm_mlp"], w_up, w_down)
    p_w_down = _matmul_tn_shards(hid_t, dout_b, "dw_down", False, tm=1024, turned=True)
    p_w_up = _matmul_tn_shards(hn_t, da, "dw_up", True, turned=True)
    dgs, dgm, dy_ssm, dy_mla, dattn, delta = _merge_bwd(dh, gs, gm, y_ssm, y_mla, attn, w_out, w_o_mla)
    p_w_out = _matmul_tn_shards(mixed_t, dh, "dw_out", False, tm=1024, turned=True)
    p_w_o_mla = _matmul_tn_shards(attn_t, dy_mla, "dw_o_mla", False, turned=True)
    dq, dk, dv, l_w_up, l_w_down, l_w_out, l_w_o_mla = _attn_bwd(
        q, k, kt, v, lse, delta, dattn, xch=[(p_w_up, True), (p_w_down, True), (p_w_out, True), (p_w_o_mla, True)])
    dql, dkvl, d_q_a_norm, d_kv_a_norm, d_gq, d_gk, g_wq, g_wkv = _qkv_prep_bwd(
        ql, kvl, dq, dk, dv, small["q_a_norm"], small["kv_a_norm"], wq, wkv, gq, gk, cos_t, sin_t, xch=[])
    p_wq = jnp.transpose(g_wq.reshape(Q_LORA, N_HEADS, QK_PAD), (1, 0, 2)).astype(BF16)
    p_wkv = jnp.transpose(g_wkv.reshape(KV_LORA, N_HEADS, QK_PAD), (1, 0, 2)).astype(BF16)
    dy, d_b_glu, g_w_glu, g_w_o_ssm = _glu_bwd(dy_ssm, y, w_glu, small["b_glu"], w_o_ssm)
    p_w_o_ssm = jnp.transpose(g_w_o_ssm.reshape(SSM_WIDTH, N_DEV, OUT_SHARD), (1, 0, 2)).astype(BF16)
    p_w_glu = g_w_glu.reshape(N_DEV, SSM_WIDTH // N_DEV, SSM_WIDTH).astype(BF16)
    du, dlam, d_d, d_bblk, d_cblk_t, l_wq, l_wkv, l_w_glu, l_w_o_ssm = _ssm_bwd(
        u, dy, st, bblk, cblk, lam, d_row, xch=[(p_wq, True), (p_wkv, True), (p_w_glu, True), (p_w_o_ssm, True)])
    d_a_re, d_a_im, d_log_dt, d_bt_re, d_bt_im, d_c_re, d_c_im = _ssm_prep_bwd(
        a_re, a_im, log_dt, bt_re, bt_im, dlam, d_bblk, d_cblk_t)
    tr = lambda mat: jnp.transpose(mat.reshape(SSM_GROUPS, SSM_GROUP_CH, SSM_STATE), (0, 2, 1))
    g_small = {
        "q_a_norm": d_q_a_norm, "kv_a_norm": d_kv_a_norm, "q_norm": d_gq[:, :QK_HEAD], "k_norm": d_gk[:, :QK_HEAD],
        "ssm_a_re": d_a_re, "ssm_a_im": d_a_im, "ssm_log_dt": d_log_dt,
        "ssm_b_re": tr(d_bt_re), "ssm_b_im": tr(d_bt_im), "ssm_c_re": d_c_re, "ssm_c_im": d_c_im,
        "ssm_d": d_d, "b_glu": d_b_glu, "norm_mlp": d_norm_mlp,
    }
    rest = jnp.concatenate([_pack_small(g_small, SMALL[1:]), loss], axis=0)
    dx, d_norm_mix, g_w_in_pad, g_rest_all = _in_proj_bwd(
        (du, dql, dkvl, dgs, dgm), dh, x, xn_t, small["norm_mix"], w_in_pad, xch=[(rest, False)])
    parts = jnp.transpose(_unpad_in(g_w_in_pad).reshape(D_MODEL, N_DEV, IN_SHARD), (1, 0, 2))
    g_w_in_mine, g_first_all = _reduce_scatter(parts, _pack_small({SMALL[0]: d_norm_mix}, SMALL[:1]), "reduce_w_in")
    landed = {"w_q_b": l_wq[:, :, :QK_HEAD], "w_kv_b": l_wkv, "w_o_mla": l_w_o_mla, "w_glu": l_w_glu,
              "w_o_ssm": l_w_o_ssm, "w_out": l_w_out, "w_up": l_w_up, "w_down": l_w_down}
    return dx, landed, g_w_in_mine, g_first_all, g_rest_all


def kernel(x, positions, norm_mix, w_in, q_a_norm, kv_a_norm, w_q_b, w_kv_b, q_norm, k_norm, w_o_mla, ssm_a_re, ssm_a_im, ssm_log_dt, ssm_b_re, ssm_b_im, ssm_c_re, ssm_c_im, ssm_d, w_glu, b_glu, w_o_ssm, w_out, norm_mlp, w_up, w_down, loss_target, m_norm_mix, m_w_in, m_q_a_norm, m_kv_a_norm, m_w_q_b, m_w_kv_b, m_q_norm, m_k_norm, m_w_o_mla, m_ssm_a_re, m_ssm_a_im, m_ssm_log_dt, m_ssm_b_re, m_ssm_b_im, m_ssm_c_re, m_ssm_c_im, m_ssm_d, m_w_glu, m_b_glu, m_w_o_ssm, m_w_out, m_norm_mlp, m_w_up, m_w_down, v_norm_mix, v_w_in, v_q_a_norm, v_kv_a_norm, v_w_q_b, v_w_kv_b, v_q_norm, v_k_norm, v_w_o_mla, v_ssm_a_re, v_ssm_a_im, v_ssm_log_dt, v_ssm_b_re, v_ssm_b_im, v_ssm_c_re, v_ssm_c_im, v_ssm_d, v_w_glu, v_b_glu, v_w_o_ssm, v_w_out, v_norm_mlp, v_w_up, v_w_down):
    given = dict(locals())
    w = {n: given[n] for n in WEIGHT_ORDER}
    m = {n: given["m_" + n] for n in WEIGHT_ORDER}
    v = {n: given["v_" + n] for n in WEIGHT_ORDER}
    big = [n for n in WEIGHT_ORDER if n not in SMALL]
    small = {n: w[n] for n in SMALL}

    dx, landed, g_w_in, g_first_all, g_rest_all = _step(
        x[0], positions.reshape(-1, 1), loss_target[0], {n: w[n][0] for n in big}, small)

    grads, deltas, new_m, new_v = {}, {}, {}, {}
    for n in big:
        if n in ("w_in", "w_q_b"):
            wt, mt, vt = jnp.transpose(w[n][0]), jnp.transpose(m[n][0]), jnp.transpose(v[n][0])
            if n == "w_in":
                g = jnp.transpose(g_w_in)
                d, nm, nv = _adamw(wt, g, mt, vt, "adamw_" + n)
            else:
                g, d, nm, nv = _adamw_sum(jnp.transpose(landed[n], (0, 2, 1)), wt, mt, vt, "adamw_" + n)
            g, d, nm, nv = (jnp.transpose(a) for a in (g, d, nm, nv))
        else:
            g, d, nm, nv = _adamw_sum(landed[n], w[n][0], m[n][0], v[n][0], "adamw_" + n)
        grads[n], deltas[n], new_m[n], new_v[n] = g[None], d[None], nm[None], nv[None]

    outs = _adamw_small(g_first_all, g_rest_all, _pack_small(small), _pack_small({n: m[n] for n in SMALL}),
                        _pack_small({n: v[n] for n in SMALL}), _small_rows(small))
    for p, n in enumerate(SMALL):
        for k, dst in enumerate((grads, deltas, new_m, new_v)):
            dst[n] = outs[1 + 4 * p + k].reshape(-1)[:small[n].size].reshape(small[n].shape)

    return (outs[0][0, 0], dx[None], *[grads[n] for n in WEIGHT_ORDER], *[deltas[n] for n in WEIGHT_ORDER],
            *[new_m[n] for n in WEIGHT_ORDER], *[new_v[n] for n in WEIGHT_ORDER])
```

```python
import functools
import math

import numpy as np
import jax
import jax.numpy as jnp
from jax import lax
from jax.experimental import pallas as pl
from jax.experimental.pallas import tpu as pltpu

F32 = jnp.float32
BF16 = jnp.bfloat16

D_MODEL = 1024
SSM_GROUPS = 32
SSM_GROUP_CH = 16
SSM_WIDTH = 512
SSM_STATE = 64
N_STATE = SSM_GROUPS * SSM_STATE
N_HEADS = 8
QK_NOPE = 128
QK_ROPE = 64
QK_HEAD = 192
QK_PAD = 256
V_HEAD = 128
Q_LORA = 384
KV_LORA = 256
KV_LAT_PAD = 384
ROPE_THETA = 10000.0
D_FF = 4096
EPS = 1e-6
ATT_SCALE = QK_HEAD ** -0.5
N_DEV = 8
FF_SHARD = D_FF // N_DEV
OUT_SHARD = D_MODEL // N_DEV

IN_SEGS = ((0, 512), (512, 896), (896, 1280), (1280, 2304), (2304, 3328))
D_IN = 3264
D_IN_PAD = 3328
KV_END = 1216

ADAM_LR = 0.001
ADAM_B1 = 0.9
ADAM_B2 = 0.999
ADAM_EPS = 1e-08
ADAM_WD = 0.01
ADAM_STEP = 10

VMEM_LIMIT = 56 * 1024 * 1024
MESH = pl.DeviceIdType.MESH
ANY = pl.BlockSpec(memory_space=pl.ANY)
LANES = 128

SCAN_T = 512
SUBCHUNKS = 8
SCAN_CG = 2048
ATT_T = 512
ATT_SUB = 1
ATT_HEADS = 4
ATT_BWD_HEADS = 2
ROW_T = 256
MM_T = 512


def _params(sem=None):
    return pltpu.CompilerParams(dimension_semantics=sem, vmem_limit_bytes=VMEM_LIMIT)


def _rows(arr, tm):
    if arr.ndim == 2:
        return pl.BlockSpec((tm, arr.shape[1]), lambda i: (i, 0))
    return pl.BlockSpec((arr.shape[0], tm, arr.shape[2]), lambda i: (0, i, 0))


def _const(arr):
    nd = arr.ndim
    return pl.BlockSpec(arr.shape, lambda *_: (0,) * nd, pipeline_mode=pl.Buffered(1))


def _sds(shape, dtype):
    return jax.ShapeDtypeStruct(shape, dtype)


PEERS = tuple((dx, dy, dc) for dx in (0, 1) for dy in (0, 1) for dc in (0, 1) if (dx, dy, dc) != (0, 0, 0))


def _here():
    x, y, c = lax.axis_index("x"), lax.axis_index("y"), lax.axis_index("c")
    return x, y, c, 4 * x + 2 * y + c


def _xchg_start(scatter, srcs, dsts, send, recv, local):
    x, y, c, me = _here()
    for e, sc in enumerate(scatter):
        src, dst = srcs[e], dsts[e]
        pltpu.make_async_copy(src.at[me] if sc else src, dst.at[me], local.at[e]).start()
        for dx, dy, dc in PEERS:
            px, py, pc = (1 - x if dx else x), (1 - y if dy else y), (1 - c if dc else c)
            pltpu.make_async_remote_copy(
                src_ref=src.at[4 * px + 2 * py + pc] if sc else src, dst_ref=dst.at[me],
                send_sem=send.at[e], recv_sem=recv.at[e], device_id=(px, py, pc), device_id_type=MESH).start()


def _xchg_wait(scatter, srcs, dsts, send, recv, local):
    x, y, c, me = _here()
    for e, sc in enumerate(scatter):
        src, dst = srcs[e], dsts[e]
        pltpu.make_async_copy(src.at[me] if sc else src, dst.at[me], local.at[e]).wait()
        span = dst.at[pl.ds(0, N_DEV - 1)]
        both = pltpu.make_async_remote_copy(src_ref=span, dst_ref=span, send_sem=send.at[e], recv_sem=recv.at[e],
                                            device_id=(x, y, c), device_id_type=MESH)
        both.wait_send()
        both.wait_recv()


def _call(body, name, grid, ins, in_specs, outs, out_specs, scratch=(), xch=()):
    n_in, n_out, ne = len(ins), len(outs), len(xch)
    scatter = [sc for _, sc in xch]
    x_outs = [_sds((N_DEV,) + (a.shape[1:] if sc else a.shape), a.dtype) for a, sc in xch]
    sems = [pltpu.SemaphoreType.DMA((ne,))] * 3 if ne else []

    def wrapped(*refs):
        in_refs, x_src = refs[:n_in], refs[n_in:n_in + ne]
        out_refs = refs[n_in + ne:n_in + ne + n_out]
        x_dst = refs[n_in + ne + n_out:n_in + 2 * ne + n_out]
        rest = refs[n_in + 2 * ne + n_out:]
        if ne:
            x_sems, rest = rest[len(rest) - 3:], rest[:len(rest) - 3]
            first = functools.reduce(jnp.logical_and, [pl.program_id(d) == 0 for d in range(len(grid))])
            last = functools.reduce(jnp.logical_and, [pl.program_id(d) == grid[d] - 1 for d in range(len(grid))])

            @pl.when(first)
            def _():
                _xchg_start(scatter, x_src, x_dst, *x_sems)

        body(*in_refs, *out_refs, *rest)
        if ne:
            @pl.when(last)
            def _():
                _xchg_wait(scatter, x_src, x_dst, *x_sems)

    return pl.pallas_call(
        wrapped,
        name=name,
        grid=grid,
        in_specs=list(in_specs) + [ANY] * ne,
        out_specs=list(out_specs) + [ANY] * ne,
        out_shape=list(outs) + x_outs,
        scratch_shapes=list(scratch) + sems,
        compiler_params=_params(("arbitrary",) * len(grid)),
    )(*ins, *[a for a, _ in xch])


def _row_call(body, name, n_rows, tm, row_ins, const_ins, row_outs, acc_outs=(), xch=(), col_outs=(), scratch=()):
    outs = [_sds(s, d) for s, d in list(row_outs) + list(col_outs) + list(acc_outs)]
    n_row, n_col = len(row_outs), len(col_outs)
    out_specs = [_rows(o, tm) for o in outs[:n_row]] + [
        pl.BlockSpec(o.shape[:-1] + (tm,), lambda i, nd=len(o.shape): (0,) * (nd - 1) + (i,))
        for o in outs[n_row:n_row + n_col]] + [
        pl.BlockSpec(o.shape, lambda i, nd=len(o.shape): (0,) * nd) for o in outs[n_row + n_col:]]
    in_specs = [_rows(a, tm) for a in row_ins] + [_const(a) for a in const_ins]
    return _call(body, name, (n_rows // tm,), list(row_ins) + list(const_ins), in_specs, outs, out_specs,
                 scratch=scratch, xch=xch)


def _dot(a, b):
    return jnp.dot(a, b, preferred_element_type=F32)


def _dot_nt(a, b):
    return lax.dot_general(a, b, (((1,), (1,)), ((), ())), preferred_element_type=F32)


def _dot_tn(a, b):
    return lax.dot_general(a, b, (((0,), (0,)), ((), ())), preferred_element_type=F32)


def _rms(x, g, n):
    inv = lax.rsqrt(jnp.sum(x * x, -1, keepdims=True) * (1.0 / n) + EPS)
    return x * inv * g, inv


def _rms_bwd(dy, x, g, inv, n):
    xh = x * inv
    dxh = dy * g
    dx = inv * (dxh - xh * (jnp.sum(dxh * xh, -1, keepdims=True) * (1.0 / n)))
    return dx, dy * xh


def _sigmoid(x):
    return 1.0 / (1.0 + jnp.exp(-x))


_GELU_C = math.sqrt(2.0 / math.pi)


def _gelu(y):
    th = jnp.tanh(_GELU_C * (y + 0.044715 * (y * y * y)))
    return 0.5 * y * (1.0 + th), th


def _gelu_grad(y, th):
    return 0.5 * (1.0 + th) + 0.5 * y * (1.0 - th * th) * (_GELU_C * (1.0 + 3.0 * 0.044715 * (y * y)))


def _acc(ref, val):
    @pl.when(pl.program_id(0) == 0)
    def _():
        ref[...] = jnp.zeros_like(ref)

    ref[...] += val


def _tile(n, limit):
    if n <= limit:
        return n
    return max(t for t in range(128, limit + 1, 128) if n % t == 0)


def _lhs(a, turned, tm, tk):
    m, k_dim = a.shape if turned else a.shape[::-1]
    tm, tk = _tile(m, tm), _tile(k_dim, tk)
    if turned:
        return m, k_dim, tm, tk, pl.BlockSpec((tm, tk), lambda i, k: (i, k)), _dot
    return m, k_dim, tm, tk, pl.BlockSpec((tk, tm), lambda i, k: (k, i)), _dot_tn


def _matmul_tn_shards(a, b, name, by_col, tm=512, tk=512, turned=False):
    m, k_dim, tm, tk, a_spec, dot = _lhs(a, turned, tm, tk)
    n = b.shape[1]
    nk = k_dim // tk
    if by_col:
        r, c = m, n // N_DEV
        out_spec = pl.BlockSpec((N_DEV, tm, c), lambda i, k: (0, i, 0))
    else:
        r, c = m // N_DEV, n
        per = tm // r
        out_spec = pl.BlockSpec((per, r, c), lambda i, k: (i, 0, 0))

    def body(a_ref, b_ref, o_ref, acc_ref):
        k = pl.program_id(1)

        @pl.when(k == 0)
        def _():
            acc_ref[...] = jnp.zeros_like(acc_ref)

        acc_ref[...] += dot(a_ref[...].astype(BF16), b_ref[...].astype(BF16))

        @pl.when(k == nk - 1)
        def _():
            if by_col:
                for j in range(N_DEV):
                    o_ref[j] = acc_ref[:, j * c:(j + 1) * c].astype(BF16)
            else:
                for s in range(per):
                    o_ref[s] = acc_ref[s * r:(s + 1) * r, :].astype(BF16)

    return pl.pallas_call(
        body,
        name=name,
        grid=(m // tm, nk),
        in_specs=[a_spec, pl.BlockSpec((tk, n), lambda i, k: (k, 0))],
        out_specs=out_spec,
        out_shape=_sds((N_DEV, r, c), BF16),
        scratch_shapes=[pltpu.VMEM((tm, n), F32)],
        compiler_params=_params(("parallel", "arbitrary")),
    )(a, b)


def _rope_tables(pos_col):
    n = pos_col.shape[0]
    half = QK_ROPE // 2
    inv_freq = (ROPE_THETA ** (-np.arange(half, dtype=np.float32) / half)).astype(np.float32)
    freq_row = jnp.asarray(np.concatenate([inv_freq, inv_freq, np.zeros(64, np.float32)])[None, :])

    def body(p_ref, f_ref, c_ref, s_ref):
        ang = p_ref[...].astype(F32) * f_ref[...]
        c_ref[...] = jnp.cos(ang)
        s_ref[...] = jnp.sin(ang)

    return _row_call(body, "rope_tables", n, min(n, 1024), [pos_col], [freq_row], [((n, 128), F32)] * 2)


def _rope_rot(v):
    lane = lax.broadcasted_iota(jnp.int32, v.shape, 1)
    return jnp.where(lane < 32, -pltpu.roll(v, 96, 1), jnp.where(lane < 64, pltpu.roll(v, 32, 1), 0.0))


def _rope_rot_t(v):
    lane = lax.broadcasted_iota(jnp.int32, v.shape, 1)
    return jnp.where(lane < 32, pltpu.roll(v, 96, 1), jnp.where(lane < 64, -pltpu.roll(v, 32, 1), 0.0))


def _in_proj(x, norm_mix, w_in_pad, xch):
    n = x.shape[0]

    def body(x_ref, g_ref, w_ref, u_ref, ql_ref, kvl_ref, gs_ref, gm_ref, xnt_ref):
        xn, _ = _rms(x_ref[...], g_ref[...], D_MODEL)
        xb = xn.astype(BF16)
        xnt_ref[...] = xn.T.astype(BF16)
        for ref, (a, b) in zip((u_ref, ql_ref, kvl_ref, gs_ref, gm_ref), IN_SEGS):
            ref[...] = _dot(xb, w_ref[:, a:b])

    outs = [((n, b - a), F32) for a, b in IN_SEGS]
    return _row_call(body, "in_proj", n, MM_T, [x], [norm_mix, w_in_pad], outs, xch=xch,
                     col_outs=[((D_MODEL, n), BF16)])


def _ssm_prep_fn(a_re, a_im, log_dt, b_re_x, b_im_x):
    dt = jnp.exp(log_dt)
    mag = jnp.exp(a_re * dt)
    lr = mag * jnp.cos(a_im * dt)
    li = mag * jnp.sin(a_im * dt)
    den = a_re * a_re + a_im * a_im
    fr = ((lr - 1.0) * a_re + li * a_im) / den
    fi = (li * a_re - (lr - 1.0) * a_im) / den
    return lr, li, fr * b_re_x - fi * b_im_x, fr * b_im_x + fi * b_re_x


def _dot_exact(a, b, dims):
    return lax.dot_general(a, b, (dims, ((), ())), precision=lax.Precision.HIGHEST, preferred_element_type=F32)


def _lane_repeat(width, n):
    src = lax.broadcasted_iota(jnp.int32, (width, n), 0)
    dst = lax.broadcasted_iota(jnp.int32, (width, n), 1)
    return (dst % width == src).astype(F32)


def _same_group(rows, rows_per_group, cols, cols_per_group):
    row = lax.broadcasted_iota(jnp.int32, (rows, cols), 0)
    col = lax.broadcasted_iota(jnp.int32, (rows, cols), 1)
    return (row // rows_per_group) == (col // cols_per_group)


def _expand_b(bt):
    tiled = _dot_exact(bt, _lane_repeat(SSM_STATE, N_STATE), ((1,), (0,)))
    return jnp.where(_same_group(SSM_WIDTH, SSM_GROUP_CH, N_STATE, SSM_STATE), tiled, 0.0)


def _collect_b(m):
    masked = jnp.where(_same_group(SSM_WIDTH, SSM_GROUP_CH, N_STATE, SSM_STATE), m, 0.0)
    return _dot_exact(masked, _lane_repeat(SSM_STATE, N_STATE), ((1,), (1,)))


def _ssm_prep(a_re, a_im, log_dt, bt_re, bt_im, c2_re, c2_im):
    def body(ar, ai, ld, br, bi, cr, ci, lam_ref, bblk_ref, cblk_ref):
        lr, li, bbr, bbi = _ssm_prep_fn(ar[...], ai[...], ld[...], _expand_b(br[...]), _expand_b(bi[...]))
        lam_ref[0:1, :] = lr
        lam_ref[1:2, :] = li
        bblk_ref[:, 0:N_STATE] = bbr.astype(BF16)
        bblk_ref[:, N_STATE:] = bbi.astype(BF16)
        rep = _lane_repeat(SSM_GROUP_CH, SSM_WIDTH)
        own = _same_group(N_STATE, SSM_STATE, SSM_WIDTH, SSM_GROUP_CH)
        cblk_ref[0:N_STATE, :] = jnp.where(own, _dot_exact(cr[...], rep, ((1,), (0,))), 0.0).astype(BF16)
        cblk_ref[N_STATE:, :] = jnp.where(own, -_dot_exact(ci[...], rep, ((1,), (0,))), 0.0).astype(BF16)

    return pl.pallas_call(
        body,
        name="ssm_prep",
        out_shape=[_sds((2, N_STATE), F32), _sds((SSM_WIDTH, 2 * N_STATE), BF16),
                   _sds((2 * N_STATE, SSM_WIDTH), BF16)],
        compiler_params=_params(),
    )(a_re, a_im, log_dt, bt_re, bt_im, c2_re, c2_im)


def _ssm_prep_bwd(a_re, a_im, log_dt, bt_re, bt_im, dlam, dbblk, dcblk_t):
    def body(ar, ai, ld, br, bi, dl, db, dc, dar, dai, dld, dbr, dbi, dcr, dci):
        _, vjp = jax.vjp(_ssm_prep_fn, ar[...], ai[...], ld[...], _expand_b(br[...]), _expand_b(bi[...]))
        g = vjp((dl[0:1, :], dl[1:2, :], db[:, 0:N_STATE], db[:, N_STATE:]))
        dar[...] = g[0]
        dai[...] = g[1]
        grp = lax.broadcasted_iota(jnp.int32, (SSM_GROUPS, N_STATE), 0)
        lane = lax.broadcasted_iota(jnp.int32, (SSM_GROUPS, N_STATE), 1)
        sel = (lane // SSM_STATE) == grp
        dld[...] = jnp.sum(jnp.where(sel, jnp.broadcast_to(g[2], (SSM_GROUPS, N_STATE)), 0.0), axis=1, keepdims=True)
        dbr[...] = _collect_b(g[3])
        dbi[...] = _collect_b(g[4])
        dcr[...] = _collect_b(dc[:, 0:N_STATE])
        dci[...] = -_collect_b(dc[:, N_STATE:])

    small = _sds((SSM_WIDTH, SSM_STATE), F32)
    return pl.pallas_call(
        body,
        name="ssm_prep_bwd",
        out_shape=[_sds((1, N_STATE), F32), _sds((1, N_STATE), F32), _sds((SSM_GROUPS, 1), F32), small, small, small, small],
        compiler_params=_params(),
    )(a_re, a_im, log_dt, bt_re, bt_im, dlam, dbblk, dcblk_t)


def _perm_matrix(t):
    run = t // SUBCHUNKS
    p = np.zeros((t, t), np.float32)
    r = np.arange(t)
    p[r, (r % SUBCHUNKS) * run + r // SUBCHUNKS] = 1.0
    return jnp.asarray(p, dtype=BF16)


def _unpermute(p, a):
    hi = a.astype(BF16)
    r1 = a - hi.astype(F32)
    mid = r1.astype(BF16)
    lo = (r1 - mid.astype(F32)).astype(BF16)
    return _dot_tn(p, hi) + _dot_tn(p, mid) + _dot_tn(p, lo)


def _power_table(lam_ref, pw_ref, n):
    lr, li = lam_ref[0:1, :], lam_ref[1:2, :]
    pw_ref[0:1, 0:N_STATE] = lr
    pw_ref[0:1, N_STATE:] = li

    def step(i, carry):
        pr, pi = carry
        pr, pi = pr * lr - pi * li, pr * li + pi * lr
        pw_ref[pl.ds(i, 1), 0:N_STATE] = pr
        pw_ref[pl.ds(i, 1), N_STATE:] = pi
        return pr, pi

    lax.fori_loop(1, n, step, (lr, li))


def _col_groups():
    return [(pl.ds(c, SCAN_CG), pl.ds(N_STATE + c, SCAN_CG)) for c in range(0, N_STATE, SCAN_CG)]


def _run_scan(buf, lam_ref, t, reverse):
    nblk = t // 8
    for re, im in _col_groups():
        lr = jnp.broadcast_to(lam_ref[0:1, re], (8, SCAN_CG))
        li = jnp.broadcast_to(lam_ref[1:2, re], (8, SCAN_CG))
        if reverse:
            li = -li
        first = pl.ds((nblk - 1) * 8 if reverse else 0, 8)

        def step(k, carry, re=re, im=im, lr=lr, li=li):
            pr, pi = carry
            i = (nblk - 2 - k) if reverse else (k + 1)
            r = pl.ds(pl.multiple_of(i * 8, 8), 8)
            xr = buf[r, re] + lr * pr - li * pi
            xi = buf[r, im] + lr * pi + li * pr
            buf[r, re] = xr
            buf[r, im] = xi
            return xr, xi

        lax.fori_loop(0, nblk - 1, step, (buf[first, re], buf[first, im]))


def _run_carries(buf, pw_ref, carry_ref, s_ref, t, reverse):
    nblk = t // 8
    run = t // SUBCHUNKS
    edge = buf[pl.ds(0 if reverse else (nblk - 1) * 8, 8), :]
    pr, pi = pw_ref[run - 1:run, 0:N_STATE], pw_ref[run - 1:run, N_STATE:]
    if reverse:
        pi = -pi
    sr, si = carry_ref[0:1, 0:N_STATE], carry_ref[0:1, N_STATE:]
    for s in (range(SUBCHUNKS - 1, -1, -1) if reverse else range(SUBCHUNKS)):
        s_ref[s:s + 1, 0:N_STATE] = sr
        s_ref[s:s + 1, N_STATE:] = si
        er, ei = edge[s:s + 1, 0:N_STATE], edge[s:s + 1, N_STATE:]
        sr, si = er + pr * sr - pi * si, ei + pr * si + pi * sr
    carry_ref[:, 0:N_STATE] = jnp.broadcast_to(sr, (8, N_STATE))
    carry_ref[:, N_STATE:] = jnp.broadcast_to(si, (8, N_STATE))


def _run_fix(buf, pw_ref, s_ref, t, reverse):
    nblk = t // 8
    for re, im in _col_groups():
        sr, si = s_ref[:, re], s_ref[:, im]

        def step(i, carry, re=re, im=im, sr=sr, si=si):
            r = pl.ds(pl.multiple_of(i * 8, 8), 8)
            row = pl.ds((nblk - 1 - i) if reverse else i, 1)
            pr, pi = pw_ref[row, re], pw_ref[row, im]
            if reverse:
                pi = -pi
            buf[r, re] += pr * sr - pi * si
            buf[r, im] += pr * si + pi * sr
            return carry

        lax.fori_loop(0, nblk, step, 0)


STATE_BLOCKS = 2 * N_STATE // LANES
CH_BLOCKS = SSM_WIDTH // LANES


def _state_block(b):
    pair = b % (N_STATE // LANES)
    k = (pair * 2 * SSM_GROUP_CH) // LANES
    return slice(b * LANES, (b + 1) * LANES), slice(k * LANES, (k + 1) * LANES)


def _channel_block(c):
    w = N_STATE // CH_BLOCKS
    return slice(c * LANES, (c + 1) * LANES), slice(c * w, (c + 1) * w), slice(N_STATE + c * w, N_STATE + (c + 1) * w)


def _to_states(vb, w_ref, buf, nt):
    for b in range(STATE_BLOCKS):
        lanes, ch = _state_block(b)
        buf[:, lanes] = _dot_nt(vb[:, ch], w_ref[lanes, ch]) if nt else _dot(vb[:, ch], w_ref[ch, lanes])


def _to_channels(buf, w_ref, nt):
    outs = []
    for c in range(CH_BLOCKS):
        ch, re, im = _channel_block(c)
        xr, xi = buf[:, re].astype(BF16), buf[:, im].astype(BF16)
        if nt:
            outs.append(_dot_nt(xr, w_ref[ch, re]) + _dot_nt(xi, w_ref[ch, im]))
        else:
            outs.append(_dot(xr, w_ref[re, ch]) + _dot(xi, w_ref[im, ch]))
    return jnp.concatenate(outs, axis=-1)


def _ssm_fwd(u, bblk, cblk, lam, d_row, w_glu, b_glu, w_o_ssm, xch):
    n = u.shape[0]
    t = min(SCAN_T, n)
    perm = _perm_matrix(t)

    def body(u_ref, p_ref, bblk_ref, cblk_ref, lam_ref, d_ref, wg_ref, bg_ref, wo_ref, y_ref, ys_ref, st_ref,
             buf, pw_ref, carry_ref, s_ref):
        @pl.when(pl.program_id(0) == 0)
        def _():
            carry_ref[...] = jnp.zeros_like(carry_ref)
            _power_table(lam_ref, pw_ref, t // SUBCHUNKS)

        st_ref[0] = carry_ref[...]
        u_t = u_ref[...]
        p = p_ref[...]
        ub = _dot(p, u_t.astype(BF16)).astype(BF16)
        _to_states(ub, bblk_ref, buf, False)
        _run_scan(buf, lam_ref, t, False)
        _run_carries(buf, pw_ref, carry_ref, s_ref, t, False)
        _run_fix(buf, pw_ref, s_ref, t, False)
        y = d_ref[...] * u_t + _unpermute(p, _to_channels(buf, cblk_ref, False))
        y_ref[...] = y
        z, _ = _gelu(y)
        s = _sigmoid(_dot(z.astype(BF16), wg_ref[...]) + bg_ref[...])
        zgb = (z * s).astype(BF16)
        for j in range(N_DEV):
            ys_ref[:, j * OUT_SHARD:(j + 1) * OUT_SHARD] = _dot(zgb, wo_ref[j])

    consts = [perm, bblk, cblk, lam, d_row, w_glu, b_glu, w_o_ssm]
    return _call(
        body, "ssm_fwd", (n // t,), [u] + consts, [_rows(u, t)] + [_const(a) for a in consts],
        [_sds((n, SSM_WIDTH), F32), _sds((n, D_MODEL), F32), _sds((n // t, 8, 2 * N_STATE), F32)],
        [pl.BlockSpec((t, SSM_WIDTH), lambda i: (i, 0)), pl.BlockSpec((t, D_MODEL), lambda i: (i, 0)),
         pl.BlockSpec((1, 8, 2 * N_STATE), lambda i: (i, 0, 0))],
        scratch=[pltpu.VMEM((t, 2 * N_STATE), F32), pltpu.VMEM((t // SUBCHUNKS, 2 * N_STATE), F32),
                 pltpu.VMEM((8, 2 * N_STATE), F32), pltpu.VMEM((8, 2 * N_STATE), F32)],
        xch=xch)


def _head_norm_rope(slab, gain, cos_t, sin_t):
    xn, inv = _rms(slab, gain, QK_HEAD)
    lo, hi = xn[:, 0:128], xn[:, 128:256]
    return jnp.concatenate([lo, hi * cos_t + _rope_rot(hi) * sin_t], axis=-1), inv


def _head_norm_rope_bwd(g, slab, gain, inv, cos_t, sin_t):
    g_lo, g_hi = g[:, 0:128], g[:, 128:256]
    g_n = jnp.concatenate([g_lo, g_hi * cos_t + _rope_rot_t(g_hi * sin_t)], axis=-1)
    return _rms_bwd(g_n, slab, gain, inv, QK_HEAD)


def _qkv_prep(ql, kvl, q_a_norm, kv_a_norm, wq, wkv, gq, gk, cos_t, sin_t):
    n = ql.shape[0]
    tm = ROW_T

    def body(ql_ref, kvl_ref, cos_ref, sin_ref, qa_ref, ka_ref, wq_ref, wkv_ref, gq_ref, gk_ref,
             q_ref, k_ref, v_ref, kt_ref, vt_ref):
        cos_t, sin_t = cos_ref[...], sin_ref[...]
        qa, _ = _rms(ql_ref[...], qa_ref[...], Q_LORA)
        qab = qa.astype(BF16)
        kvl_t = kvl_ref[...]
        ca, _ = _rms(kvl_t[:, 0:KV_LORA], ka_ref[...], KV_LORA)
        cab = ca.astype(BF16)
        kpe = kvl_t[:, KV_LORA:KV_LAT_PAD]
        q_pre = _dot(qab, wq_ref[...])
        kv_pre = _dot(cab, wkv_ref[...])
        for h in range(N_HEADS):
            qh, _ = _head_norm_rope(q_pre[:, h * QK_PAD:(h + 1) * QK_PAD], gq_ref[...], cos_t, sin_t)
            q_ref[h] = (qh * ATT_SCALE).astype(BF16)
            kv_h = kv_pre[:, h * QK_PAD:(h + 1) * QK_PAD]
            kh, _ = _head_norm_rope(jnp.concatenate([kv_h[:, 0:QK_NOPE], kpe], axis=-1), gk_ref[...], cos_t, sin_t)
            k_ref[h] = kh.astype(BF16)
            kt_ref[h] = kh.T.astype(BF16)
            vh = kv_h[:, QK_NOPE:]
            v_ref[h] = vh.astype(BF16)
            vt_ref[h] = vh.T.astype(BF16)

    row_ins, consts = [ql, kvl, cos_t, sin_t], [q_a_norm, kv_a_norm, wq, wkv, gq, gk]
    outs = [_sds((N_HEADS, n, QK_PAD), BF16), _sds((N_HEADS, n, QK_PAD), BF16), _sds((N_HEADS, n, V_HEAD), BF16),
            _sds((N_HEADS, QK_PAD, n), BF16), _sds((N_HEADS, V_HEAD, n), BF16)]
    out_specs = [_rows(o, tm) for o in outs[:3]] + [
        pl.BlockSpec((N_HEADS, QK_PAD, tm), lambda i: (0, 0, i)), pl.BlockSpec((N_HEADS, V_HEAD, tm), lambda i: (0, 0, i))]
    return _call(body, "qkv_prep", (n // tm,), row_ins + consts,
                 [_rows(a, tm) for a in row_ins] + [_const(a) for a in consts], outs, out_specs)


def _causal_mask_t(st, t):
    key = lax.broadcasted_iota(jnp.int32, (t, t), 0)
    qry = lax.broadcasted_iota(jnp.int32, (t, t), 1)
    return jnp.where(key <= qry, st, -jnp.inf)


def _attn_fwd(q, k, vt, xch):
    n = q.shape[1]
    t = min(ATT_T, n)

    hp = ATT_HEADS

    def body(q_ref, k_ref, vt_ref, o_ref, lse_ref, ot_ref):
        i = pl.program_id(1)
        qts = [q_ref[g] for g in range(hp)]

        def kv_tile(j, carry, diag):
            ts = t // ATT_SUB
            sts = []
            for g in range(hp):
                for a in range(ATT_SUB):
                    r0 = pl.multiple_of(j * t + a * ts, ts)
                    st = _dot_nt(k_ref[g, pl.ds(r0, ts), :], qts[g])
                    if diag:
                        key = lax.broadcasted_iota(jnp.int32, (ts, t), 0) + a * ts
                        qry = lax.broadcasted_iota(jnp.int32, (ts, t), 1)
                        st = jnp.where(key <= qry, st, -jnp.inf)
                    sts.append(st)
            out = []
            for g in range(hp):
                m, l, acc = carry[g]
                for a in range(ATT_SUB):
                    st = sts[g * ATT_SUB + a]
                    r0 = pl.multiple_of(j * t + a * ts, ts)
                    m_new = jnp.maximum(m, jnp.max(st, 0, keepdims=True))
                    alpha = jnp.exp(m - m_new)
                    pt = jnp.exp(st - m_new)
                    l = alpha * l + jnp.sum(pt, 0, keepdims=True)
                    acc = alpha * acc + _dot(vt_ref[g, :, pl.ds(r0, ts)], pt.astype(BF16))
                    m = m_new
                out.append((m, l, acc))
            return tuple(out)

        one = (jnp.full((1, t), -jnp.inf, F32), jnp.zeros((1, t), F32), jnp.zeros((V_HEAD, t), F32))
        carry = lax.fori_loop(0, i, functools.partial(kv_tile, diag=False), (one,) * hp)
        def diag_tile(carry):
            half = t // 2
            tri = (lax.broadcasted_iota(jnp.int32, (half, t), 0) <= lax.broadcasted_iota(jnp.int32, (half, t), 1))
            out = []
            for g in range(hp):
                m, l, acc = carry[g]
                r0 = pl.multiple_of(i * t, t)
                st = jnp.where(tri, _dot_nt(k_ref[g, pl.ds(r0, half), :], qts[g]), -jnp.inf)
                m_new = jnp.maximum(m, jnp.max(st, 0, keepdims=True))
                alpha = jnp.exp(m - m_new)
                pt = jnp.exp(st - m_new)
                l = alpha * l + jnp.sum(pt, 0, keepdims=True)
                acc = alpha * acc + _dot(vt_ref[g, :, pl.ds(r0, half)], pt.astype(BF16))
                r1 = pl.multiple_of(i * t + half, half)
                st2 = jnp.where(tri[:, 0:half], _dot_nt(k_ref[g, pl.ds(r1, half), :], qts[g][half:, :]), -jnp.inf)
                m_hi = m_new[:, half:]
                m2 = jnp.maximum(m_hi, jnp.max(st2, 0, keepdims=True))
                a2 = jnp.exp(m_hi - m2)
                p2 = jnp.exp(st2 - m2)
                l_hi = a2 * l[:, half:] + jnp.sum(p2, 0, keepdims=True)
                acc_hi = a2 * acc[:, half:] + _dot(vt_ref[g, :, pl.ds(r1, half)], p2.astype(BF16))
                out.append((jnp.concatenate([m_new[:, 0:half], m2], axis=-1),
                            jnp.concatenate([l[:, 0:half], l_hi], axis=-1),
                            jnp.concatenate([acc[:, 0:half], acc_hi], axis=-1)))
            return out

        for g, (m, l, acc) in enumerate(diag_tile(carry)):
            out_t = acc / l
            o_ref[:, g * V_HEAD:(g + 1) * V_HEAD] = out_t.T
            ot_ref[g * V_HEAD:(g + 1) * V_HEAD, :] = out_t.astype(BF16)
            lse_ref[g] = m + jnp.log(l)

    return _call(
        body, "attn_fwd", (N_HEADS // hp, n // t), [q, k, vt],
        [pl.BlockSpec((hp, t, QK_PAD), lambda h, i: (h, i, 0)), pl.BlockSpec((hp, n, QK_PAD), lambda h, i: (h, 0, 0)),
         pl.BlockSpec((hp, V_HEAD, n), lambda h, i: (h, 0, 0))],
        [_sds((n, N_HEADS * V_HEAD), F32), _sds((N_HEADS, 1, n), F32), _sds((N_HEADS * V_HEAD, n), BF16)],
        [pl.BlockSpec((t, hp * V_HEAD), lambda h, i: (i, h)), pl.BlockSpec((hp, 1, t), lambda h, i: (h, 0, i)),
         pl.BlockSpec((hp * V_HEAD, t), lambda h, i: (h, i))],
        xch=xch)


def _merge(attn, gs, gm, y_ssm, x, w_o_mla, w_out):
    n = x.shape[0]

    def body(at_ref, gs_ref, gm_ref, ys_ref, x_ref, wo_ref, wout_ref, h_ref, ym_ref, mxt_ref):
        y_mla = _dot(at_ref[...].astype(BF16), wo_ref[...])
        ym_ref[...] = y_mla
        mixed = _sigmoid(gs_ref[...]) * ys_ref[...] + _sigmoid(gm_ref[...]) * y_mla
        mxt_ref[...] = mixed.T.astype(BF16)
        h_ref[...] = x_ref[...] + _dot(mixed.astype(BF16), wout_ref[...])

    outs = [((n, D_MODEL), F32), ((n, D_MODEL), F32)]
    return _row_call(body, "merge", n, MM_T, [attn, gs, gm, y_ssm, x], [w_o_mla, w_out], outs,
                     col_outs=[((D_MODEL, n), BF16)])


def _mlp_fwd_loss(h, target, norm_mlp, w_up, w_down):
    n = h.shape[0]

    def body(h_ref, t_ref, g_ref, wu_ref, wd_ref, hn_ref, do_ref, hnt_ref, loss_ref):
        h_t = h_ref[...]
        hn, _ = _rms(h_t, g_ref[...], D_MODEL)
        hb = hn.astype(BF16)
        hn_ref[...] = hb
        hnt_ref[...] = hn.T.astype(BF16)
        out = h_t
        for j in range(N_DEV):
            a = jnp.maximum(_dot(hb, wu_ref[j]), 0.0)
            out += _dot((a * a).astype(BF16), wd_ref[j])
        err = out - t_ref[...]
        do_ref[...] = err * (1.0 / D_MODEL)
        _acc(loss_ref, jnp.broadcast_to(jnp.sum(err * err) * (0.5 / D_MODEL), loss_ref.shape))

    outs = [((n, D_MODEL), BF16), ((n, D_MODEL), F32)]
    return _row_call(body, "mlp_fwd_loss", n, MM_T, [h, target], [norm_mlp, w_up, w_down], outs, [((8, 128), F32)],
                     col_outs=[((D_MODEL, n), BF16)])


def _mlp_bwd(dout, hn, h, norm_mlp, w_up, w_down):
    n = h.shape[0]

    def body(do_ref, hn_ref, h_ref, g_ref, wu_ref, wd_ref, da_ref, dh_ref, dob_ref, hidt_ref, dg_ref):
        dout_t = do_ref[...]
        doutb = dout_t.astype(BF16)
        dob_ref[...] = doutb
        hb = hn_ref[...]
        dhn = jnp.zeros_like(dout_t)
        for j in range(N_DEV):
            cols = slice(j * FF_SHARD, (j + 1) * FF_SHARD)
            a = jnp.maximum(_dot(hb, wu_ref[j]), 0.0)
            hidt_ref[cols, :] = (a * a).T.astype(BF16)
            da = (_dot_nt(doutb, wd_ref[j]) * (2.0 * a)).astype(BF16)
            da_ref[:, cols] = da
            dhn += _dot_nt(da, wu_ref[j])
        h_t = h_ref[...]
        inv = lax.rsqrt(jnp.sum(h_t * h_t, -1, keepdims=True) * (1.0 / D_MODEL) + EPS)
        dx, dg = _rms_bwd(dhn, h_t, g_ref[...], inv, D_MODEL)
        dh_ref[...] = dout_t + dx
        _acc(dg_ref, jnp.sum(dg, 0, keepdims=True))

    outs = [((n, D_FF), BF16), ((n, D_MODEL), F32), ((n, D_MODEL), BF16)]
    return _row_call(body, "mlp_bwd", n, MM_T, [dout, hn, h], [norm_mlp, w_up, w_down], outs, [((1, D_MODEL), F32)],
                     col_outs=[((D_FF, n), BF16)])


def _merge_bwd(dh, gs, gm, y_ssm, y_mla, attn, w_out, w_o_mla):
    n = dh.shape[0]

    def body(dh_ref, gs_ref, gm_ref, ys_ref, ym_ref, at_ref, wout_ref, wo_ref,
             dgs_ref, dgm_ref, dys_ref, dym_ref, dat_ref, delta_ref):
        dmix = _dot_nt(dh_ref[...].astype(BF16), wout_ref[...])
        sgs, sgm = _sigmoid(gs_ref[...]), _sigmoid(gm_ref[...])
        dgs_ref[...] = (dmix * ys_ref[...] * sgs * (1.0 - sgs)).astype(BF16)
        dgm_ref[...] = (dmix * ym_ref[...] * sgm * (1.0 - sgm)).astype(BF16)
        dys_ref[...] = (dmix * sgs).astype(BF16)
        dym = (dmix * sgm).astype(BF16)
        dym_ref[...] = dym
        dattn = _dot_nt(dym, wo_ref[...])
        dat_ref[...] = dattn.astype(BF16)
        prod = dattn * at_ref[...]
        ones = jnp.ones((8, V_HEAD), F32)
        for h in range(N_HEADS):
            delta_ref[h] = _dot_exact(ones, prod[:, h * V_HEAD:(h + 1) * V_HEAD], ((1,), (1,)))[0:1, :]

    outs = [((n, D_MODEL), BF16)] * 5
    return _row_call(body, "merge_bwd", n, MM_T, [dh, gs, gm, y_ssm, y_mla, attn], [w_out, w_o_mla], outs,
                     col_outs=[((N_HEADS, 1, n), F32)])


def _attn_bwd(q, k, kt, v, lse, delta, dout, xch):
    n = q.shape[1]
    t = min(ATT_T, n)
    nt = n // t
    hp = ATT_BWD_HEADS

    def body(q_ref, k_ref, kt_ref, v_ref, lse_ref, delta_ref, do_ref, dq_ref, dk_ref, dv_ref, dqt_ref):
        j = pl.program_id(1)

        @pl.when(j == 0)
        def _():
            dqt_ref[...] = jnp.zeros_like(dqt_ref)

        def q_tile(i, carry, diag):
            r0 = pl.multiple_of(i * t, t)
            rows = pl.ds(r0, t)
            qts = [q_ref[g, rows, :] for g in range(hp)]
            sts = [_dot_nt(k_ref[g], qts[g]) for g in range(hp)]
            out = []
            for g in range(hp):
                dk, dv = carry[g]
                st = _causal_mask_t(sts[g], t) if diag else sts[g]
                pt = jnp.exp(st - lse_ref[g, :, rows])
                dob = do_ref[rows, g * V_HEAD:(g + 1) * V_HEAD]
                dv = dv + _dot(pt.astype(BF16), dob)
                dst = (pt * (_dot_nt(v_ref[g], dob) - delta_ref[g, :, rows])).astype(BF16)
                dk = dk + _dot(dst, qts[g])
                dqt_ref[g, :, rows] += _dot(kt_ref[g], dst)
                out.append((dk, dv))
            return tuple(out)

        zero = (jnp.zeros((t, QK_PAD), F32), jnp.zeros((t, V_HEAD), F32))
        carry = q_tile(j, (zero,) * hp, True)
        carry = lax.fori_loop(j + 1, nt, functools.partial(q_tile, diag=False), carry)
        for g, (dk, dv) in enumerate(carry):
            dk_ref[g] = dk
            dv_ref[g] = dv

        @pl.when(j == nt - 1)
        def _():
            for g in range(hp):
                for c in range(0, n, t):
                    dq_ref[g, c:c + t, :] = dqt_ref[g, :, c:c + t].T

    return _call(
        body, "attn_bwd", (N_HEADS // hp, nt), [q, k, kt, v, lse, delta, dout],
        [pl.BlockSpec((hp, n, QK_PAD), lambda h, j: (h, 0, 0)), pl.BlockSpec((hp, t, QK_PAD), lambda h, j: (h, j, 0)),
         pl.BlockSpec((hp, QK_PAD, t), lambda h, j: (h, 0, j)), pl.BlockSpec((hp, t, V_HEAD), lambda h, j: (h, j, 0)),
         pl.BlockSpec((hp, 1, n), lambda h, j: (h, 0, 0)), pl.BlockSpec((hp, 1, n), lambda h, j: (h, 0, 0)),
         pl.BlockSpec((n, hp * V_HEAD), lambda h, j: (0, h))],
        [_sds((N_HEADS, n, QK_PAD), F32), _sds((N_HEADS, n, QK_PAD), F32), _sds((N_HEADS, n, V_HEAD), F32)],
        [pl.BlockSpec((hp, n, QK_PAD), lambda h, j: (h, 0, 0)), pl.BlockSpec((hp, t, QK_PAD), lambda h, j: (h, j, 0)),
         pl.BlockSpec((hp, t, V_HEAD), lambda h, j: (h, j, 0))],
        scratch=[pltpu.VMEM((hp, QK_PAD, n), F32)],
        xch=xch)


def _qkv_prep_bwd(ql, kvl, dq, dk, dv, q_a_norm, kv_a_norm, wq, wkv, gq, gk, cos_t, sin_t, xch):
    n = ql.shape[0]

    def body(ql_ref, kvl_ref, cos_ref, sin_ref, dq_ref, dk_ref, dv_ref, qa_ref, ka_ref, wq_ref, wkv_ref, gq_ref, gk_ref,
             dql_ref, dkvl_ref, dqa_ref, dka_ref, dgq_ref, dgk_ref, dwq_ref, dwkv_ref, dqp_ref, dkvp_ref):
        cos_t, sin_t = cos_ref[...], sin_ref[...]
        ql_t = ql_ref[...]
        qa, inv_qa = _rms(ql_t, qa_ref[...], Q_LORA)
        qab = qa.astype(BF16)
        kvl_t = kvl_ref[...]
        ckv = kvl_t[:, 0:KV_LORA]
        ca, inv_ca = _rms(ckv, ka_ref[...], KV_LORA)
        cab = ca.astype(BF16)
        kpe = kvl_t[:, KV_LORA:KV_LAT_PAD]
        dgq = jnp.zeros((1, QK_PAD), F32)
        dgk = jnp.zeros((1, QK_PAD), F32)
        dkpe = jnp.zeros_like(kpe)
        q_pre = _dot(qab, wq_ref[...])
        kv_pre = _dot(cab, wkv_ref[...])
        for h in range(N_HEADS):
            head = slice(h * QK_PAD, (h + 1) * QK_PAD)
            q_slab = q_pre[:, head]
            inv = lax.rsqrt(jnp.sum(q_slab * q_slab, -1, keepdims=True) * (1.0 / QK_HEAD) + EPS)
            d_slab, dg = _head_norm_rope_bwd(dq_ref[h] * ATT_SCALE, q_slab, gq_ref[...], inv, cos_t, sin_t)
            dqp_ref[:, head] = d_slab.astype(BF16)
            dgq += jnp.sum(dg, 0, keepdims=True)
            k_slab = jnp.concatenate([kv_pre[:, h * QK_PAD:h * QK_PAD + QK_NOPE], kpe], axis=-1)
            inv = lax.rsqrt(jnp.sum(k_slab * k_slab, -1, keepdims=True) * (1.0 / QK_HEAD) + EPS)
            d_slab, dg = _head_norm_rope_bwd(dk_ref[h], k_slab, gk_ref[...], inv, cos_t, sin_t)
            dkvp_ref[:, head] = jnp.concatenate([d_slab[:, 0:QK_NOPE], dv_ref[h]], axis=-1).astype(BF16)
            dkpe += d_slab[:, QK_NOPE:QK_PAD]
            dgk += jnp.sum(dg, 0, keepdims=True)
        dqa = _dot_nt(dqp_ref[...], wq_ref[...])
        dx, dg = _rms_bwd(dqa, ql_t, qa_ref[...], inv_qa, Q_LORA)
        dql_ref[...] = dx.astype(BF16)
        _acc(dqa_ref, jnp.sum(dg, 0, keepdims=True))
        dca = _dot_nt(dkvp_ref[...], wkv_ref[...])
        dx, dg = _rms_bwd(dca, ckv, ka_ref[...], inv_ca, KV_LORA)
        dkvl_ref[:, 0:KV_LORA] = dx.astype(BF16)
        dkvl_ref[:, KV_LORA:KV_LAT_PAD] = dkpe.astype(BF16)
        _acc(dka_ref, jnp.sum(dg, 0, keepdims=True))
        _acc(dgq_ref, dgq)
        _acc(dgk_ref, dgk)
        _acc(dwq_ref, _dot_tn(qab, dqp_ref[...]))
        _acc(dwkv_ref, _dot_tn(cab, dkvp_ref[...]))

    wide = N_HEADS * QK_PAD
    row_outs = [((n, Q_LORA), BF16), ((n, KV_LAT_PAD), BF16)]
    acc_outs = [((1, Q_LORA), F32), ((1, KV_LORA), F32), ((1, QK_PAD), F32), ((1, QK_PAD), F32),
                ((Q_LORA, wide), F32), ((KV_LORA, wide), F32)]
    return _row_call(body, "qkv_prep_bwd", n, ROW_T, [ql, kvl, cos_t, sin_t, dq, dk, dv],
                     [q_a_norm, kv_a_norm, wq, wkv, gq, gk], row_outs, acc_outs, xch=xch,
                     scratch=[pltpu.VMEM((ROW_T, wide), BF16), pltpu.VMEM((ROW_T, wide), BF16)])


def _glu_bwd(dy_ssm, y, w_glu, b_glu, w_o_ssm):
    n = y.shape[0]

    def body(dys_ref, y_ref, wg_ref, bg_ref, wo_ref, dy_ref, db_ref, dwg_ref, dwo_ref):
        y_t = y_ref[...]
        z, th = _gelu(y_t)
        zb = z.astype(BF16)
        s = _sigmoid(_dot(zb, wg_ref[...]) + bg_ref[...])
        dys = dys_ref[...]
        dzg = jnp.zeros_like(y_t)
        for j in range(N_DEV):
            dzg += _dot_nt(dys[:, j * OUT_SHARD:(j + 1) * OUT_SHARD], wo_ref[j])
        dt = dzg * z * s * (1.0 - s)
        dtb = dt.astype(BF16)
        dz = dzg * s + _dot_nt(dtb, wg_ref[...])
        dy_ref[...] = dz * _gelu_grad(y_t, th)
        _acc(db_ref, jnp.sum(dt, 0, keepdims=True))
        _acc(dwg_ref, _dot_tn(zb, dtb))
        _acc(dwo_ref, _dot_tn((z * s).astype(BF16), dys))

    acc_outs = [((1, SSM_WIDTH), F32), ((SSM_WIDTH, SSM_WIDTH), F32), ((SSM_WIDTH, D_MODEL), F32)]
    return _row_call(body, "glu_bwd", n, ROW_T, [dy_ssm, y], [w_glu, b_glu, w_o_ssm], [((n, SSM_WIDTH), F32)], acc_outs)


def _ssm_bwd(u, dy, st, bblk, cblk, lam, d_row, xch):
    n = u.shape[0]
    t = min(SCAN_T, n)
    nc = n // t
    kb = 512
    perm = _perm_matrix(t)

    def body(u_ref, dy_ref, st_ref, p_ref, bblk_ref, cblk_ref, lam_ref, d_ref,
             du_ref, dlam_ref, dd_ref, db_ref, dct_ref,
             buf_x, buf_a, pw_ref, carry_ref, xcarry_ref, sx_ref, sa_ref, db_acc, dct_acc):
        @pl.when(pl.program_id(0) == 0)
        def _():
            carry_ref[...] = jnp.zeros_like(carry_ref)
            db_acc[...] = jnp.zeros_like(db_acc)
            dct_acc[...] = jnp.zeros_like(dct_acc)
            _power_table(lam_ref, pw_ref, t // SUBCHUNKS)

        u_t = u_ref[...]
        dy_t = dy_ref[...]
        p = p_ref[...]
        ub = _dot(p, u_t.astype(BF16)).astype(BF16)
        dyb = _dot(p, dy_t.astype(BF16)).astype(BF16)
        _to_states(ub, bblk_ref, buf_x, False)
        xcarry_ref[...] = st_ref[0]
        _run_scan(buf_x, lam_ref, t, False)
        _run_carries(buf_x, pw_ref, xcarry_ref, sx_ref, t, False)
        _run_fix(buf_x, pw_ref, sx_ref, t, False)
        _to_states(dyb, cblk_ref, buf_a, True)
        _run_scan(buf_a, lam_ref, t, True)
        _run_carries(buf_a, pw_ref, carry_ref, sa_ref, t, True)
        _run_fix(buf_a, pw_ref, sa_ref, t, True)
        du_ref[...] = (d_ref[...] * dy_t + _unpermute(p, _to_channels(buf_a, bblk_ref, True))).astype(BF16)
        for b in range(STATE_BLOCKS):
            lanes, ch = _state_block(b)
            db_acc[ch, lanes] += _dot_tn(ub[:, ch], buf_a[:, lanes].astype(BF16))
            dct_acc[ch, lanes] += _dot_tn(dyb[:, ch], buf_x[:, lanes].astype(BF16))
        for c in range(0, N_STATE, kb):
            re, im = pl.ds(c, kb), pl.ds(N_STATE + c, kb)
            xr, xi = buf_x[pl.ds(0, t - 8), re], buf_x[pl.ds(0, t - 8), im]
            ar, ai = buf_a[pl.ds(8, t - 8), re], buf_a[pl.ds(8, t - 8), im]
            x0r, x0i = sx_ref[:, re], sx_ref[:, im]
            a0r, a0i = buf_a[0:8, re], buf_a[0:8, im]
            dlam_part_re = (jnp.sum(ar * xr + ai * xi, 0, keepdims=True)
                            + jnp.sum(a0r * x0r + a0i * x0i, 0, keepdims=True))
            dlam_part_im = (jnp.sum(ai * xr - ar * xi, 0, keepdims=True)
                            + jnp.sum(a0i * x0r - a0r * x0i, 0, keepdims=True))

            @pl.when(pl.program_id(0) == 0)
            def _(c=c):
                dlam_ref[0:1, c:c + kb] = jnp.zeros((1, kb), F32)
                dlam_ref[1:2, c:c + kb] = jnp.zeros((1, kb), F32)

            dlam_ref[0:1, c:c + kb] += dlam_part_re
            dlam_ref[1:2, c:c + kb] += dlam_part_im
        _acc(dd_ref, jnp.sum(dy_t * u_t, 0, keepdims=True))

        @pl.when(pl.program_id(0) == nc - 1)
        def _():
            pltpu.sync_copy(db_acc, db_ref)
            pltpu.sync_copy(dct_acc, dct_ref)

    rev = lambda i: (nc - 1 - i, 0)
    consts = [perm, bblk, cblk, lam, d_row]
    wide = (SSM_WIDTH, 2 * N_STATE)
    return _call(
        body, "ssm_bwd", (nc,), [u, dy, st] + consts,
        [pl.BlockSpec((t, SSM_WIDTH), rev), pl.BlockSpec((t, SSM_WIDTH), rev),
         pl.BlockSpec((1, 8, 2 * N_STATE), lambda i: (nc - 1 - i, 0, 0))] + [_const(a) for a in consts],
        [_sds((n, SSM_WIDTH), BF16), _sds((2, N_STATE), F32), _sds((1, SSM_WIDTH), F32), _sds(wide, F32), _sds(wide, F32)],
        [pl.BlockSpec((t, SSM_WIDTH), rev), pl.BlockSpec((2, N_STATE), lambda i: (0, 0)),
         pl.BlockSpec((1, SSM_WIDTH), lambda i: (0, 0)), ANY, ANY],
        scratch=[pltpu.VMEM((t, 2 * N_STATE), F32)] * 2 + [pltpu.VMEM((t // SUBCHUNKS, 2 * N_STATE), F32)]
        + [pltpu.VMEM((8, 2 * N_STATE), F32)] * 4 + [pltpu.VMEM(wide, F32)] * 2,
        xch=xch)


def _in_proj_bwd(pieces, dh, x, xn_t, norm_mix, w_in_pad, xch):
    n = x.shape[0]
    tm = min(MM_T, n)
    nt = n // tm

    def body(du_ref, dql_ref, dkvl_ref, dgs_ref, dgm_ref, dh_ref, x_ref, xnt_ref, g_ref, w_ref,
             dx_ref, dg_ref, dw_ref, acc_ref):
        @pl.when(pl.program_id(0) == 0)
        def _():
            acc_ref[...] = jnp.zeros_like(acc_ref)

        xnt = xnt_ref[...]
        dxn = jnp.zeros((tm, D_MODEL), F32)
        for ref, (a, b) in zip((du_ref, dql_ref, dkvl_ref, dgs_ref, dgm_ref), IN_SEGS):
            piece = ref[...]
            dxn += _dot_nt(piece, w_ref[:, a:b])
            acc_ref[:, a:b] += _dot(xnt, piece)
        x_t = x_ref[...]
        inv = lax.rsqrt(jnp.sum(x_t * x_t, -1, keepdims=True) * (1.0 / D_MODEL) + EPS)
        dx, dg = _rms_bwd(dxn, x_t, g_ref[...], inv, D_MODEL)
        dx_ref[...] = dh_ref[...] + dx
        _acc(dg_ref, jnp.sum(dg, 0, keepdims=True))

        @pl.when(pl.program_id(0) == nt - 1)
        def _():
            pltpu.sync_copy(acc_ref, dw_ref)

    row_ins, consts = list(pieces) + [dh, x], [norm_mix, w_in_pad]
    in_specs = ([_rows(a, tm) for a in row_ins] + [pl.BlockSpec((D_MODEL, tm), lambda i: (0, i))]
                + [_const(a) for a in consts])
    return _call(
        body, "in_proj_bwd", (nt,), row_ins + [xn_t] + consts, in_specs,
        [_sds((n, D_MODEL), F32), _sds((1, D_MODEL), F32), _sds((D_MODEL, D_IN_PAD), F32)],
        [pl.BlockSpec((tm, D_MODEL), lambda i: (i, 0)), pl.BlockSpec((1, D_MODEL), lambda i: (0, 0)), ANY],
        scratch=[pltpu.VMEM((D_MODEL, D_IN_PAD), F32)], xch=xch)


def _swap_minor(a):
    g, r, c = a.shape[1:]
    return jnp.transpose(a[0], (0, 2, 1)).reshape(g * c, r)


def _pad_in(w):
    return jnp.concatenate([w[:, :KV_END], jnp.zeros((w.shape[0], D_IN_PAD - D_IN), w.dtype), w[:, KV_END:]], axis=1)


def _unpad_in(w):
    return jnp.concatenate([w[:, :KV_END], w[:, KV_END + D_IN_PAD - D_IN:]], axis=1)


def _pad_gain(g):
    return jnp.pad(g, ((0, 0), (0, QK_PAD - QK_HEAD)))


def _place():
    x, y, c = lax.axis_index("x"), lax.axis_index("y"), lax.axis_index("c")
    chips = [(x, y), (1 - x, y), (x, 1 - y), (1 - x, 1 - y)]
    return x, y, c, chips


def _all_gather(block, name):
    rows, lanes = block.shape

    def body(x_ref, out_ref, send_sems, recv_sems, local_sem):
        x, y, c, chips = _place()
        me, sibling = (x, y, c), (x, y, 1 - c)

        def slot(px, py, pc):
            return out_ref.at[4 * px + 2 * py + pc]

        def copy(k, blk, to, src=None):
            return pltpu.make_async_remote_copy(
                src_ref=slot(*blk) if src is None else src, dst_ref=slot(*blk),
                send_sem=send_sems.at[k], recv_sem=recv_sems.at[k], device_id=to, device_id_type=MESH)

        mine = pltpu.make_async_copy(x_ref, slot(*me), local_sem)
        mine.start()
        first = [copy(0, me, sibling, src=x_ref)]
        first += [copy(1 + j, me, (*chip, c), src=x_ref) for j, chip in enumerate(chips[1:])]
        for cp in first:
            cp.start()
        passed = [copy(4 + j, (*chip, c), sibling) for j, chip in enumerate(chips[1:])]
        for j, chip in enumerate(chips[1:]):
            copy(1 + j, (*chip, c), me).wait_recv()
            passed[j].start()
        copy(0, sibling, me).wait_recv()
        for j, chip in enumerate(chips[1:]):
            copy(4 + j, (*chip, 1 - c), me).wait_recv()
        for cp in first + passed:
            cp.wait_send()
        mine.wait()

    return pl.pallas_call(
        body,
        name=name,
        in_specs=[ANY],
        out_specs=ANY,
        out_shape=_sds((N_DEV, rows, lanes), block.dtype),
        scratch_shapes=[pltpu.SemaphoreType.DMA((7,)), pltpu.SemaphoreType.DMA((7,)), pltpu.SemaphoreType.DMA],
    )(block)


def _reduce_scatter(parts, gather, name):
    _, rows, lanes = parts.shape

    def body(p_ref, g_ref, out_ref, ga_ref, own, land_a, send_b, land_b, sa, ra, sb, rb, lo, *g_sems):
        x, y, c, chips = _place()
        sibling = (x, y, 1 - c)
        _xchg_start([False], [g_ref], [ga_ref], *g_sems)

        def blk(chip, core):
            return p_ref.at[4 * chip[0] + 2 * chip[1] + core]

        to_sib = [pltpu.make_async_remote_copy(
            src_ref=blk(chips[k], 1 - c), dst_ref=land_a.at[k], send_sem=sa.at[k], recv_sem=ra.at[k],
            device_id=sibling, device_id_type=MESH) for k in range(4)]
        for cp in to_sib:
            cp.start()
        loads = [pltpu.make_async_copy(blk(chips[k], c), own.at[k], lo.at[k]) for k in range(4)]
        for cp in loads:
            cp.start()
        to_chip = [pltpu.make_async_remote_copy(
            src_ref=send_b.at[j], dst_ref=land_b.at[j], send_sem=sb.at[j], recv_sem=rb.at[j],
            device_id=(*chips[1 + j], c), device_id_type=MESH) for j in range(3)]
        for k in (1, 2, 3):
            to_sib[k].wait_recv()
            loads[k].wait()
            send_b[k - 1] = (own[k] + land_a[k]).astype(BF16)
            to_chip[k - 1].start()
        to_sib[0].wait_recv()
        loads[0].wait()
        acc = own[0] + land_a[0]
        for j in range(3):
            to_chip[j].wait_recv()
            acc = acc + land_b[j].astype(F32)
        out_ref[...] = acc
        for cp in to_sib + to_chip:
            cp.wait_send()
        _xchg_wait([False], [g_ref], [ga_ref], *g_sems)

    return pl.pallas_call(
        body,
        name=name,
        in_specs=[ANY, ANY],
        out_specs=[pl.BlockSpec(memory_space=pltpu.VMEM), ANY],
        out_shape=[_sds((rows, lanes), F32), _sds((N_DEV,) + gather.shape, gather.dtype)],
        scratch_shapes=[pltpu.VMEM((4, rows, lanes), F32), pltpu.VMEM((4, rows, lanes), F32),
                        pltpu.VMEM((3, rows, lanes), BF16), pltpu.VMEM((3, rows, lanes), BF16)]
        + [pltpu.SemaphoreType.DMA((4,))] * 2 + [pltpu.SemaphoreType.DMA((3,))] * 2 + [pltpu.SemaphoreType.DMA((4,))]
        + [pltpu.SemaphoreType.DMA((1,))] * 3,
        compiler_params=_params(),
    )(parts, gather)


def _adamw_math(w, g, m, v):
    m = ADAM_B1 * m + (1.0 - ADAM_B1) * g
    v = ADAM_B2 * v + (1.0 - ADAM_B2) * (g * g)
    m_hat = m / (1.0 - ADAM_B1 ** ADAM_STEP)
    v_hat = v / (1.0 - ADAM_B2 ** ADAM_STEP)
    delta = -ADAM_LR * (m_hat / (jnp.sqrt(v_hat) + ADAM_EPS) + ADAM_WD * w)
    return delta, m, v


def _row_tile(r):
    return max(t for t in range(8, min(r, 256) + 1, 8) if r % t == 0)


def _adamw(w, g, m, v, name):
    r, n = w.shape

    def body(w_ref, g_ref, m_ref, v_ref, d_ref, nm_ref, nv_ref):
        d_ref[...], nm_ref[...], nv_ref[...] = _adamw_math(w_ref[...], g_ref[...], m_ref[...], v_ref[...])

    return _row_call(body, name, r, _row_tile(r), [w, g, m, v], [], [((r, n), F32)] * 3)


def _adamw_sum(landed, w, m, v, name):
    r, n = w.shape

    def body(l_ref, w_ref, m_ref, v_ref, g_ref, d_ref, nm_ref, nv_ref):
        g = l_ref[0].astype(F32)
        for dev in range(1, N_DEV):
            g = g + l_ref[dev].astype(F32)
        g_ref[...] = g
        d_ref[...], nm_ref[...], nv_ref[...] = _adamw_math(w_ref[...], g, m_ref[...], v_ref[...])

    tm = max(t for t in range(16, min(r, 256) + 1, 16) if r % t == 0)
    return _row_call(body, name, r, tm, [landed, w, m, v], [], [((r, n), F32)] * 4)


def _adamw_small(first, rest, w, m, v, row_counts):
    n_rest = w.shape[0] - first.shape[1]

    def body(f_ref, r_ref, w_ref, m_ref, v_ref, loss_ref, *out_refs):
        gf, gr = f_ref[0], r_ref[0]
        for dev in range(1, N_DEV):
            gf, gr = gf + f_ref[dev], gr + r_ref[dev]
        loss_ref[...] = gr[n_rest:n_rest + 8]
        g = jnp.concatenate([gf, gr[0:n_rest]], axis=0)
        d, nm, nv = _adamw_math(w_ref[...], g, m_ref[...], v_ref[...])
        off = 0
        for p, rows in enumerate(row_counts):
            for k, val in enumerate((g, d, nm, nv)):
                out_refs[4 * p + k][...] = val[off:off + rows]
            off += rows

    outs = [_sds((8, LANES), F32)] + [_sds((rows, LANES), F32) for rows in row_counts for _ in range(4)]
    return pl.pallas_call(body, name="adamw_small", out_shape=outs, compiler_params=_params())(first, rest, w, m, v)


SMALL = ("norm_mix", "q_a_norm", "kv_a_norm", "q_norm", "k_norm", "ssm_a_re", "ssm_a_im", "ssm_log_dt", "ssm_b_re",
         "ssm_b_im", "ssm_c_re", "ssm_c_im", "ssm_d", "b_glu", "norm_mlp")
WEIGHT_ORDER = ("norm_mix", "w_in", "q_a_norm", "kv_a_norm", "w_q_b", "w_kv_b", "q_norm", "k_norm", "w_o_mla",
                "ssm_a_re", "ssm_a_im", "ssm_log_dt", "ssm_b_re", "ssm_b_im", "ssm_c_re", "ssm_c_im", "ssm_d", "w_glu",
                "b_glu", "w_o_ssm", "w_out", "norm_mlp", "w_up", "w_down")
IN_SHARD = D_IN // N_DEV


def _pack_small(vals, names=SMALL):
    parts = []
    for n in names:
        flat = vals[n].reshape(-1)
        size = -(-flat.shape[0] // (8 * LANES)) * 8 * LANES
        parts.append(jnp.pad(flat, (0, size - flat.shape[0])).reshape(-1, LANES))
    return jnp.concatenate(parts, axis=0)


def _small_rows(like):
    return [-(-like[n].size // (8 * LANES)) * 8 for n in SMALL]


def _step(x, pos_col, target, w, small):
    bf = {n: a.astype(BF16) for n, a in w.items()}
    gq, gk = _pad_gain(small["q_norm"]), _pad_gain(small["k_norm"])
    a_re = small["ssm_a_re"].reshape(1, N_STATE)
    a_im = small["ssm_a_im"].reshape(1, N_STATE)
    log_dt = jnp.repeat(small["ssm_log_dt"].reshape(SSM_GROUPS), SSM_STATE).reshape(1, N_STATE)
    bt_re, bt_im = _swap_minor(small["ssm_b_re"]), _swap_minor(small["ssm_b_im"])
    c2_re, c2_im = _swap_minor(small["ssm_c_re"]), _swap_minor(small["ssm_c_im"])
    d_row = small["ssm_d"].reshape(1, SSM_WIDTH)

    w_in_all = _all_gather(bf["w_in"], "gather_w_in")
    w_in_pad = _pad_in(jnp.transpose(w_in_all, (1, 0, 2)).reshape(D_MODEL, D_IN))
    cos_t, sin_t = _rope_tables(pos_col)
    lam, bblk, cblk = _ssm_prep(a_re, a_im, log_dt, bt_re, bt_im, c2_re, c2_im)
    wq_mine = jnp.pad(bf["w_q_b"], ((0, 0), (0, QK_PAD - QK_HEAD)))
    u, ql, kvl, gs, gm, xn_t, w_glu, w_o_ssm = _in_proj(
        x, small["norm_mix"], w_in_pad, xch=[(bf["w_glu"], False), (bf["w_o_ssm"], False)])
    w_glu = w_glu.reshape(SSM_WIDTH, SSM_WIDTH)
    y, y_ssm, st, wq, wkv, w_o_mla, w_out = _ssm_fwd(
        u, bblk, cblk, lam, d_row, w_glu, small["b_glu"], w_o_ssm,
        xch=[(wq_mine, False), (bf["w_kv_b"], False), (bf["w_o_mla"], False), (bf["w_out"], False)])
    w_o_mla, w_out = w_o_mla.reshape(D_MODEL, D_MODEL), w_out.reshape(D_MODEL, D_MODEL)
    wq = jnp.transpose(wq, (1, 0, 2)).reshape(Q_LORA, N_HEADS * QK_PAD)
    wkv = jnp.transpose(wkv, (1, 0, 2)).reshape(KV_LORA, N_HEADS * QK_PAD)
    q, k, v, kt, vt = _qkv_prep(ql, kvl, small["q_a_norm"], small["kv_a_norm"], wq, wkv, gq, gk, cos_t, sin_t)
    attn, lse, attn_t, w_up, w_down = _attn_fwd(q, k, vt, xch=[(bf["w_up"], False), (bf["w_down"], False)])
    h, y_mla, mixed_t = _merge(attn, gs, gm, y_ssm, x, w_o_mla, w_out)
    hn, dout, hn_t, loss = _mlp_fwd_loss(h, target, small["norm_mlp"], w_up, w_down)

    da, dh, dout_b, hid_t, d_norm_mlp = _mlp_bwd(dout, hn, h, small["norm_mlp"], w_up, w_down)
    p_w_down = _matmul_tn_shards(hid_t, dout_b, "dw_down", False, tm=1024, turned=True)
    p_w_up = _matmul_tn_shards(hn_t, da, "dw_up", True, turned=True)
    dgs, dgm, dy_ssm, dy_mla, dattn, delta = _merge_bwd(dh, gs, gm, y_ssm, y_mla, attn, w_out, w_o_mla)
    p_w_out = _matmul_tn_shards(mixed_t, dh, "dw_out", False, tm=1024, turned=True)
    p_w_o_mla = _matmul_tn_shards(attn_t, dy_mla, "dw_o_mla", False, turned=True)
    dq, dk, dv, l_w_up, l_w_down, l_w_out, l_w_o_mla = _attn_bwd(
        q, k, kt, v, lse, delta, dattn, xch=[(p_w_up, True), (p_w_down, True), (p_w_out, True), (p_w_o_mla, True)])
    dql, dkvl, d_q_a_norm, d_kv_a_norm, d_gq, d_gk, g_wq, g_wkv = _qkv_prep_bwd(
        ql, kvl, dq, dk, dv, small["q_a_norm"], small["kv_a_norm"], wq, wkv, gq, gk, cos_t, sin_t, xch=[])
    p_wq = jnp.transpose(g_wq.reshape(Q_LORA, N_HEADS, QK_PAD), (1, 0, 2)).astype(BF16)
    p_wkv = jnp.transpose(g_wkv.reshape(KV_LORA, N_HEADS, QK_PAD), (1, 0, 2)).astype(BF16)
    dy, d_b_glu, g_w_glu, g_w_o_ssm = _glu_bwd(dy_ssm, y, w_glu, small["b_glu"], w_o_ssm)
    p_w_o_ssm = jnp.transpose(g_w_o_ssm.reshape(SSM_WIDTH, N_DEV, OUT_SHARD), (1, 0, 2)).astype(BF16)
    p_w_glu = g_w_glu.reshape(N_DEV, SSM_WIDTH // N_DEV, SSM_WIDTH).astype(BF16)
    du, dlam, d_d, d_bblk, d_cblk_t, l_wq, l_wkv, l_w_glu, l_w_o_ssm = _ssm_bwd(
        u, dy, st, bblk, cblk, lam, d_row, xch=[(p_wq, True), (p_wkv, True), (p_w_glu, True), (p_w_o_ssm, True)])
    d_a_re, d_a_im, d_log_dt, d_bt_re, d_bt_im, d_c_re, d_c_im = _ssm_prep_bwd(
        a_re, a_im, log_dt, bt_re, bt_im, dlam, d_bblk, d_cblk_t)
    tr = lambda mat: jnp.transpose(mat.reshape(SSM_GROUPS, SSM_GROUP_CH, SSM_STATE), (0, 2, 1))
    g_small = {
        "q_a_norm": d_q_a_norm, "kv_a_norm": d_kv_a_norm, "q_norm": d_gq[:, :QK_HEAD], "k_norm": d_gk[:, :QK_HEAD],
        "ssm_a_re": d_a_re, "ssm_a_im": d_a_im, "ssm_log_dt": d_log_dt,
        "ssm_b_re": tr(d_bt_re), "ssm_b_im": tr(d_bt_im), "ssm_c_re": d_c_re, "ssm_c_im": d_c_im,
        "ssm_d": d_d, "b_glu": d_b_glu, "norm_mlp": d_norm_mlp,
    }
    rest = jnp.concatenate([_pack_small(g_small, SMALL[1:]), loss], axis=0)
    dx, d_norm_mix, g_w_in_pad, g_rest_all = _in_proj_bwd(
        (du, dql, dkvl, dgs, dgm), dh, x, xn_t, small["norm_mix"], w_in_pad, xch=[(rest, False)])
    parts = jnp.transpose(_unpad_in(g_w_in_pad).reshape(D_MODEL, N_DEV, IN_SHARD), (1, 0, 2))
    g_w_in_mine, g_first_all = _reduce_scatter(parts, _pack_small({SMALL[0]: d_norm_mix}, SMALL[:1]), "reduce_w_in")
    landed = {"w_q_b": l_wq[:, :, :QK_HEAD], "w_kv_b": l_wkv, "w_o_mla": l_w_o_mla, "w_glu": l_w_glu,
              "w_o_ssm": l_w_o_ssm, "w_out": l_w_out, "w_up": l_w_up, "w_down": l_w_down}
    return dx, landed, g_w_in_mine, g_first_all, g_rest_all


def kernel(x, positions, norm_mix, w_in, q_a_norm, kv_a_norm, w_q_b, w_kv_b, q_norm, k_norm, w_o_mla, ssm_a_re, ssm_a_im, ssm_log_dt, ssm_b_re, ssm_b_im, ssm_c_re, ssm_c_im, ssm_d, w_glu, b_glu, w_o_ssm, w_out, norm_mlp, w_up, w_down, loss_target, m_norm_mix, m_w_in, m_q_a_norm, m_kv_a_norm, m_w_q_b, m_w_kv_b, m_q_norm, m_k_norm, m_w_o_mla, m_ssm_a_re, m_ssm_a_im, m_ssm_log_dt, m_ssm_b_re, m_ssm_b_im, m_ssm_c_re, m_ssm_c_im, m_ssm_d, m_w_glu, m_b_glu, m_w_o_ssm, m_w_out, m_norm_mlp, m_w_up, m_w_down, v_norm_mix, v_w_in, v_q_a_norm, v_kv_a_norm, v_w_q_b, v_w_kv_b, v_q_norm, v_k_norm, v_w_o_mla, v_ssm_a_re, v_ssm_a_im, v_ssm_log_dt, v_ssm_b_re, v_ssm_b_im, v_ssm_c_re, v_ssm_c_im, v_ssm_d, v_w_glu, v_b_glu, v_w_o_ssm, v_w_out, v_norm_mlp, v_w_up, v_w_down):
    given = dict(locals())
    w = {n: given[n] for n in WEIGHT_ORDER}
    m = {n: given["m_" + n] for n in WEIGHT_ORDER}
    v = {n: given["v_" + n] for n in WEIGHT_ORDER}
    big = [n for n in WEIGHT_ORDER if n not in SMALL]
    small = {n: w[n] for n in SMALL}

    dx, landed, g_w_in, g_first_all, g_rest_all = _step(
        x[0], positions.reshape(-1, 1), loss_target[0], {n: w[n][0] for n in big}, small)

    grads, deltas, new_m, new_v = {}, {}, {}, {}
    for n in big:
        if n in ("w_in", "w_q_b"):
            wt, mt, vt = jnp.transpose(w[n][0]), jnp.transpose(m[n][0]), jnp.transpose(v[n][0])
            if n == "w_in":
                g = jnp.transpose(g_w_in)
                d, nm, nv = _adamw(wt, g, mt, vt, "adamw_" + n)
            else:
                g, d, nm, nv = _adamw_sum(jnp.transpose(landed[n], (0, 2, 1)), wt, mt, vt, "adamw_" + n)
            g, d, nm, nv = (jnp.transpose(a) for a in (g, d, nm, nv))
        else:
            g, d, nm, nv = _adamw_sum(landed[n], w[n][0], m[n][0], v[n][0], "adamw_" + n)
        grads[n], deltas[n], new_m[n], new_v[n] = g[None], d[None], nm[None], nv[None]

    outs = _adamw_small(g_first_all, g_rest_all, _pack_small(small), _pack_small({n: m[n] for n in SMALL}),
                        _pack_small({n: v[n] for n in SMALL}), _small_rows(small))
    for p, n in enumerate(SMALL):
        for k, dst in enumerate((grads, deltas, new_m, new_v)):
            dst[n] = outs[1 + 4 * p + k].reshape(-1)[:small[n].size].reshape(small[n].shape)

    return (outs[0][0, 0], dx[None], *[grads[n] for n in WEIGHT_ORDER], *[deltas[n] for n in WEIGHT_ORDER],
            *[new_m[n] for n in WEIGHT_ORDER], *[new_v[n] for n in WEIGHT_ORDER])
```
